```python
import math
import jax
import jax.numpy as jnp
from jax import lax
import numpy as np

D_MODEL = 1024
BATCH = 8
SEQ = 2048
DEPTH = 1

N_META = 16
D_MIX = 2 * D_MODEL
SSD_WIDTH = D_MIX // 2
SSD_HEAD_DIM = 64
SSD_HEADS = SSD_WIDTH // SSD_HEAD_DIM
SSD_GROUPS = 2
SSD_HPG = SSD_HEADS // SSD_GROUPS
SSD_STATE = 128
SSD_CONV = 4
SSD_CHUNK = 128
SSD_CONV_DIM = SSD_WIDTH + 2 * SSD_GROUPS * SSD_STATE
LRU_WIDTH = D_MIX - SSD_WIDTH
LRU_BLOCKS = 16
LRU_BLOCK_W = LRU_WIDTH // LRU_BLOCKS
LRU_CONV = 4
LRU_C = 8.0
D_FF = -(-(8 * D_MODEL) // (3 * 256)) * 256
IN_COLS = SSD_WIDTH + SSD_CONV_DIM + SSD_HEADS + 2 * LRU_WIDTH
IN_SPLITS = [SSD_WIDTH, SSD_WIDTH + SSD_CONV_DIM, SSD_WIDTH + SSD_CONV_DIM + SSD_HEADS, SSD_WIDTH + SSD_CONV_DIM + SSD_HEADS + LRU_WIDTH]
EPS = 1e-6

kernel_name = 'hymba_ssd_rglru_hybrid_block'


def rmsnorm(x, w):
    xf = x.astype(jnp.float32)
    y = xf * lax.rsqrt(jnp.mean(xf * xf, axis=-1, keepdims=True) + EPS)
    return (y * w.astype(jnp.float32)).astype(x.dtype)


def causal_dwconv(x, w, b):
    k, c = w.shape
    y = lax.conv_general_dilated(x, w[:, None, :].astype(x.dtype), window_strides=(1,), padding=[(k - 1, 0)], dimension_numbers=('NWC', 'WIO', 'NWC'), feature_group_count=c)
    return y + b.astype(x.dtype)


def _to_chunks(t, pad):
    t = jnp.pad(t, [(0, 0), (pad, 0)] + [(0, 0)] * (t.ndim - 2))
    return t.reshape((t.shape[0], -1, SSD_CHUNK) + t.shape[2:])


def ssd_mixer(z, xbc, dt_raw, conv_w, conv_b, dt_bias, a_log, d_skip, norm_w):
    bsz, seqlen, _ = z.shape
    f32 = jnp.float32
    xbc = jax.nn.silu(causal_dwconv(xbc, conv_w, conv_b))
    xs, b_in, c_in = jnp.split(xbc, [SSD_WIDTH, SSD_WIDTH + SSD_GROUPS * SSD_STATE], axis=-1)
    dt = jax.nn.softplus(dt_raw.astype(f32) + dt_bias.astype(f32))
    a = -jnp.exp(a_log.astype(f32)).reshape(SSD_GROUPS, SSD_HPG)
    pad = (-seqlen) % SSD_CHUNK
    x_c = _to_chunks(xs.astype(f32).reshape(bsz, seqlen, SSD_GROUPS, SSD_HPG, SSD_HEAD_DIM), pad)
    b_c = _to_chunks(b_in.astype(f32).reshape(bsz, seqlen, SSD_GROUPS, SSD_STATE), pad)
    c_c = _to_chunks(c_in.astype(f32).reshape(bsz, seqlen, SSD_GROUPS, SSD_STATE), pad)
    dt_c = _to_chunks(dt.reshape(bsz, seqlen, SSD_GROUPS, SSD_HPG), pad)
    cs = jnp.cumsum(dt_c * a, axis=2)
    xdt = x_c * dt_c[..., None]
    causal = jnp.tril(jnp.ones((SSD_CHUNK, SSD_CHUNK), dtype=bool))
    seg = cs[:, :, :, None] - cs[:, :, None, :]
    lmat = jnp.exp(jnp.where(causal[:, :, None, None], seg, -jnp.inf))
    cb = jnp.einsum('bclgn,bcsgn->bclsg', c_c, b_c)
    y_diag = jnp.einsum('bclsgj,bcsgjp->bclgjp', cb[..., None] * lmat, xdt)
    decay_states = jnp.exp(cs[:, :, -1:] - cs)
    states = jnp.einsum('bclgn,bclgjp->bcgjpn', b_c, xdt * decay_states[..., None])
    chunk_decay = jnp.exp(cs[:, :, -1])

    def step(h, inp):
        s, d = inp
        return h * d[..., None, None] + s, h

    h0 = jnp.zeros((bsz, SSD_GROUPS, SSD_HPG, SSD_HEAD_DIM, SSD_STATE), f32)
    _, prev = lax.scan(step, h0, (jnp.moveaxis(states, 1, 0), jnp.moveaxis(chunk_decay, 1, 0)))
    prev = jnp.moveaxis(prev, 0, 1)
    y_off = jnp.einsum('bclgn,bcgjpn->bclgjp', c_c, prev) * jnp.exp(cs)[..., None]
    y = (y_diag + y_off).reshape(bsz, -1, SSD_WIDTH)[:, pad:]
    y = y + (xs.astype(f32).reshape(bsz, seqlen, SSD_HEADS, SSD_HEAD_DIM) * d_skip.astype(f32)[:, None]).reshape(bsz, seqlen, SSD_WIDTH)
    y = y.astype(z.dtype)
    g = (y * jax.nn.silu(z)).reshape(bsz, seqlen, SSD_GROUPS, SSD_WIDTH // SSD_GROUPS)
    return rmsnorm(g, norm_w.reshape(SSD_GROUPS, -1)).reshape(bsz, seqlen, SSD_WIDTH)


def rglru_mixer(gate, xr, conv_w, conv_b, wa, ba, wx, bx, lam, norm_w):
    bsz, seqlen, _ = xr.shape
    f32 = jnp.float32
    xr = causal_dwconv(xr, conv_w, conv_b)
    xb = xr.reshape(bsz, seqlen, LRU_BLOCKS, LRU_BLOCK_W)
    r = jax.nn.sigmoid(jnp.einsum('btni,nij->btnj', xb, wa).reshape(bsz, seqlen, LRU_WIDTH) + ba)
    i = jax.nn.sigmoid(jnp.einsum('btni,nij->btnj', xb, wx).reshape(bsz, seqlen, LRU_WIDTH) + bx)
    log_a = -LRU_C * r.astype(f32) * jax.nn.softplus(-lam.astype(f32))
    a = jnp.exp(log_a)
    u = jnp.sqrt(-jnp.expm1(2.0 * log_a)) * (i * xr).astype(f32)

    def combine(left, right):
        a1, b1 = left
        a2, b2 = right
        return a1 * a2, a2 * b1 + b2

    _, h = lax.associative_scan(combine, (a, u), axis=1)
    y = jax.nn.gelu(gate) * h.astype(gate.dtype)
    return rmsnorm(y, norm_w)


def _fwd_setup_inputs(seed: int = 0) -> dict:
    key = jax.random.key(seed)
    ks = jax.random.split(key, 24)
    nrm = jax.random.normal
    dt0 = jnp.exp(jax.random.uniform(ks[6], (DEPTH, SSD_HEADS), minval=math.log(1e-3), maxval=math.log(1e-1)))
    a_base = jax.random.uniform(ks[15], (DEPTH, LRU_WIDTH), minval=0.9, maxval=0.999)
    s = a_base ** (1.0 / LRU_C)
    return {
        'x': nrm(ks[0], (BATCH, SEQ, D_MODEL), jnp.float32),
        'meta_tokens': nrm(ks[1], (N_META, D_MODEL), jnp.float32),
        'norm1_w': 1.0 + 0.02 * nrm(ks[2], (DEPTH, D_MODEL)),
        'w_in': nrm(ks[3], (DEPTH, D_MODEL, IN_COLS)) * D_MODEL ** -0.5,
        'ssd_conv_w': nrm(ks[4], (DEPTH, SSD_CONV, SSD_CONV_DIM)) * SSD_CONV ** -0.5,
        'ssd_conv_b': 0.02 * nrm(ks[5], (DEPTH, SSD_CONV_DIM)),
        'ssd_dt_bias': dt0 + jnp.log(-jnp.expm1(-dt0)),
        'ssd_a_log': jnp.log(jax.random.uniform(ks[7], (DEPTH, SSD_HEADS), minval=1.0, maxval=16.0)),
        'ssd_d': 1.0 + 0.1 * nrm(ks[8], (DEPTH, SSD_HEADS)),
        'ssd_norm_w': 1.0 + 0.02 * nrm(ks[9], (DEPTH, SSD_WIDTH)),
        'lru_conv_w': nrm(ks[10], (DEPTH, LRU_CONV, LRU_WIDTH)) * LRU_CONV ** -0.5,
        'lru_conv_b': 0.02 * nrm(ks[11], (DEPTH, LRU_WIDTH)),
        'lru_wa': nrm(ks[12], (DEPTH, LRU_BLOCKS, LRU_BLOCK_W, LRU_BLOCK_W)) * LRU_BLOCK_W ** -0.5,
        'lru_ba': 0.02 * nrm(ks[13], (DEPTH, LRU_WIDTH)),
        'lru_wx': nrm(ks[14], (DEPTH, LRU_BLOCKS, LRU_BLOCK_W, LRU_BLOCK_W)) * LRU_BLOCK_W ** -0.5,
        'lru_bx': 0.02 * nrm(ks[16], (DEPTH, LRU_WIDTH)),
        'lru_lambda': jnp.log(s) - jnp.log1p(-s),
        'lru_norm_w': 1.0 + 0.02 * nrm(ks[17], (DEPTH, LRU_WIDTH)),
        'w_out': nrm(ks[18], (DEPTH, D_MIX, D_MODEL)) * D_MIX ** -0.5,
        'norm2_w': 1.0 + 0.02 * nrm(ks[19], (DEPTH, D_MODEL)),
        'w_gate': nrm(ks[20], (DEPTH, D_MODEL, D_FF)) * D_MODEL ** -0.5,
        'w_up': nrm(ks[21], (DEPTH, D_MODEL, D_FF)) * D_MODEL ** -0.5,
        'w_down': nrm(ks[22], (DEPTH, D_FF, D_MODEL)) * D_FF ** -0.5,
        'final_norm_w': 1.0 + 0.02 * nrm(ks[23], (D_MODEL,)),
    }


def _fwd_reference(x, meta_tokens, norm1_w, w_in, ssd_conv_w, ssd_conv_b, ssd_dt_bias, ssd_a_log, ssd_d, ssd_norm_w, lru_conv_w, lru_conv_b, lru_wa, lru_ba, lru_wx, lru_bx, lru_lambda, lru_norm_w, w_out, norm2_w, w_gate, w_up, w_down, final_norm_w):
    bsz = x.shape[0]
    meta = jnp.broadcast_to(meta_tokens.astype(x.dtype)[None], (bsz, N_META, D_MODEL))
    h = jnp.concatenate([meta, x], axis=1)
    for li in range(DEPTH):
        u = rmsnorm(h, norm1_w[li])
        proj = u @ w_in[li]
        z, xbc, dt_raw, g_lru, x_lru = jnp.split(proj, IN_SPLITS, axis=-1)
        y_ssd = ssd_mixer(z, xbc, dt_raw, ssd_conv_w[li], ssd_conv_b[li], ssd_dt_bias[li], ssd_a_log[li], ssd_d[li], ssd_norm_w[li])
        y_lru = rglru_mixer(g_lru, x_lru, lru_conv_w[li], lru_conv_b[li], lru_wa[li], lru_ba[li], lru_wx[li], lru_bx[li], lru_lambda[li], lru_norm_w[li])
        h = h + jnp.concatenate([y_ssd, y_lru], axis=-1) @ w_out[li]
        u = rmsnorm(h, norm2_w[li])
        h = h + (jax.nn.silu(u @ w_gate[li]) * (u @ w_up[li])) @ w_down[li]
    h = rmsnorm(h, final_norm_w)
    return h[:, N_META:]


import jax as _jax
import jax.numpy as _jnp

TWIN_FORMAT = 'train_step'
FWD_PARAMS = ['x', 'meta_tokens', 'norm1_w', 'w_in', 'ssd_conv_w', 'ssd_conv_b', 'ssd_dt_bias', 'ssd_a_log', 'ssd_d', 'ssd_norm_w', 'lru_conv_w', 'lru_conv_b', 'lru_wa', 'lru_ba', 'lru_wx', 'lru_bx', 'lru_lambda', 'lru_norm_w', 'w_out', 'norm2_w', 'w_gate', 'w_up', 'w_down', 'final_norm_w']
TWIN_WEIGHTS = ['meta_tokens', 'norm1_w', 'w_in', 'ssd_conv_w', 'ssd_conv_b', 'ssd_dt_bias', 'ssd_a_log', 'ssd_d', 'ssd_norm_w', 'lru_conv_w', 'lru_conv_b', 'lru_wa', 'lru_ba', 'lru_wx', 'lru_bx', 'lru_lambda', 'lru_norm_w', 'w_out', 'norm2_w', 'w_gate', 'w_up', 'w_down', 'final_norm_w']
TWIN_DIFF_INPUT = 'x'
TWIN_INPUTS = ['x', 'meta_tokens', 'norm1_w', 'w_in', 'ssd_conv_w', 'ssd_conv_b', 'ssd_dt_bias', 'ssd_a_log', 'ssd_d', 'ssd_norm_w', 'lru_conv_w', 'lru_conv_b', 'lru_wa', 'lru_ba', 'lru_wx', 'lru_bx', 'lru_lambda', 'lru_norm_w', 'w_out', 'norm2_w', 'w_gate', 'w_up', 'w_down', 'final_norm_w', 'loss_target', 'm_meta_tokens', 'm_norm1_w', 'm_w_in', 'm_ssd_conv_w', 'm_ssd_conv_b', 'm_ssd_dt_bias', 'm_ssd_a_log', 'm_ssd_d', 'm_ssd_norm_w', 'm_lru_conv_w', 'm_lru_conv_b', 'm_lru_wa', 'm_lru_ba', 'm_lru_wx', 'm_lru_bx', 'm_lru_lambda', 'm_lru_norm_w', 'm_w_out', 'm_norm2_w', 'm_w_gate', 'm_w_up', 'm_w_down', 'm_final_norm_w', 'v_meta_tokens', 'v_norm1_w', 'v_w_in', 'v_ssd_conv_w', 'v_ssd_conv_b', 'v_ssd_dt_bias', 'v_ssd_a_log', 'v_ssd_d', 'v_ssd_norm_w', 'v_lru_conv_w', 'v_lru_conv_b', 'v_lru_wa', 'v_lru_ba', 'v_lru_wx', 'v_lru_bx', 'v_lru_lambda', 'v_lru_norm_w', 'v_w_out', 'v_norm2_w', 'v_w_gate', 'v_w_up', 'v_w_down', 'v_final_norm_w']
TWIN_OUTPUTS = ['loss', 'grad_x', 'grad_meta_tokens', 'grad_norm1_w', 'grad_w_in', 'grad_ssd_conv_w', 'grad_ssd_conv_b', 'grad_ssd_dt_bias', 'grad_ssd_a_log', 'grad_ssd_d', 'grad_ssd_norm_w', 'grad_lru_conv_w', 'grad_lru_conv_b', 'grad_lru_wa', 'grad_lru_ba', 'grad_lru_wx', 'grad_lru_bx', 'grad_lru_lambda', 'grad_lru_norm_w', 'grad_w_out', 'grad_norm2_w', 'grad_w_gate', 'grad_w_up', 'grad_w_down', 'grad_final_norm_w', 'delta_meta_tokens', 'delta_norm1_w', 'delta_w_in', 'delta_ssd_conv_w', 'delta_ssd_conv_b', 'delta_ssd_dt_bias', 'delta_ssd_a_log', 'delta_ssd_d', 'delta_ssd_norm_w', 'delta_lru_conv_w', 'delta_lru_conv_b', 'delta_lru_wa', 'delta_lru_ba', 'delta_lru_wx', 'delta_lru_bx', 'delta_lru_lambda', 'delta_lru_norm_w', 'delta_w_out', 'delta_norm2_w', 'delta_w_gate', 'delta_w_up', 'delta_w_down', 'delta_final_norm_w', 'new_m_meta_tokens', 'new_m_norm1_w', 'new_m_w_in', 'new_m_ssd_conv_w', 'new_m_ssd_conv_b', 'new_m_ssd_dt_bias', 'new_m_ssd_a_log', 'new_m_ssd_d', 'new_m_ssd_norm_w', 'new_m_lru_conv_w', 'new_m_lru_conv_b', 'new_m_lru_wa', 'new_m_lru_ba', 'new_m_lru_wx', 'new_m_lru_bx', 'new_m_lru_lambda', 'new_m_lru_norm_w', 'new_m_w_out', 'new_m_norm2_w', 'new_m_w_gate', 'new_m_w_up', 'new_m_w_down', 'new_m_final_norm_w', 'new_v_meta_tokens', 'new_v_norm1_w', 'new_v_w_in', 'new_v_ssd_conv_w', 'new_v_ssd_conv_b', 'new_v_ssd_dt_bias', 'new_v_ssd_a_log', 'new_v_ssd_d', 'new_v_ssd_norm_w', 'new_v_lru_conv_w', 'new_v_lru_conv_b', 'new_v_lru_wa', 'new_v_lru_ba', 'new_v_lru_wx', 'new_v_lru_bx', 'new_v_lru_lambda', 'new_v_lru_norm_w', 'new_v_w_out', 'new_v_norm2_w', 'new_v_w_gate', 'new_v_w_up', 'new_v_w_down', 'new_v_final_norm_w']
TWIN_LEAF_KINDS = {'loss': 'loss', 'grad_x': 'grad_x', 'grad_meta_tokens': 'grad_w', 'grad_norm1_w': 'grad_w', 'grad_w_in': 'grad_w', 'grad_ssd_conv_w': 'grad_w', 'grad_ssd_conv_b': 'grad_w', 'grad_ssd_dt_bias': 'grad_w', 'grad_ssd_a_log': 'grad_w', 'grad_ssd_d': 'grad_w', 'grad_ssd_norm_w': 'grad_w', 'grad_lru_conv_w': 'grad_w', 'grad_lru_conv_b': 'grad_w', 'grad_lru_wa': 'grad_w', 'grad_lru_ba': 'grad_w', 'grad_lru_wx': 'grad_w', 'grad_lru_bx': 'grad_w', 'grad_lru_lambda': 'grad_w', 'grad_lru_norm_w': 'grad_w', 'grad_w_out': 'grad_w', 'grad_norm2_w': 'grad_w', 'grad_w_gate': 'grad_w', 'grad_w_up': 'grad_w', 'grad_w_down': 'grad_w', 'grad_final_norm_w': 'grad_w', 'delta_meta_tokens': 'delta_w', 'delta_norm1_w': 'delta_w', 'delta_w_in': 'delta_w', 'delta_ssd_conv_w': 'delta_w', 'delta_ssd_conv_b': 'delta_w', 'delta_ssd_dt_bias': 'delta_w', 'delta_ssd_a_log': 'delta_w', 'delta_ssd_d': 'delta_w', 'delta_ssd_norm_w': 'delta_w', 'delta_lru_conv_w': 'delta_w', 'delta_lru_conv_b': 'delta_w', 'delta_lru_wa': 'delta_w', 'delta_lru_ba': 'delta_w', 'delta_lru_wx': 'delta_w', 'delta_lru_bx': 'delta_w', 'delta_lru_lambda': 'delta_w', 'delta_lru_norm_w': 'delta_w', 'delta_w_out': 'delta_w', 'delta_norm2_w': 'delta_w', 'delta_w_gate': 'delta_w', 'delta_w_up': 'delta_w', 'delta_w_down': 'delta_w', 'delta_final_norm_w': 'delta_w', 'new_m_meta_tokens': 'new_m', 'new_m_norm1_w': 'new_m', 'new_m_w_in': 'new_m', 'new_m_ssd_conv_w': 'new_m', 'new_m_ssd_conv_b': 'new_m', 'new_m_ssd_dt_bias': 'new_m', 'new_m_ssd_a_log': 'new_m', 'new_m_ssd_d': 'new_m', 'new_m_ssd_norm_w': 'new_m', 'new_m_lru_conv_w': 'new_m', 'new_m_lru_conv_b': 'new_m', 'new_m_lru_wa': 'new_m', 'new_m_lru_ba': 'new_m', 'new_m_lru_wx': 'new_m', 'new_m_lru_bx': 'new_m', 'new_m_lru_lambda': 'new_m', 'new_m_lru_norm_w': 'new_m', 'new_m_w_out': 'new_m', 'new_m_norm2_w': 'new_m', 'new_m_w_gate': 'new_m', 'new_m_w_up': 'new_m', 'new_m_w_down': 'new_m', 'new_m_final_norm_w': 'new_m', 'new_v_meta_tokens': 'new_v', 'new_v_norm1_w': 'new_v', 'new_v_w_in': 'new_v', 'new_v_ssd_conv_w': 'new_v', 'new_v_ssd_conv_b': 'new_v', 'new_v_ssd_dt_bias': 'new_v', 'new_v_ssd_a_log': 'new_v', 'new_v_ssd_d': 'new_v', 'new_v_ssd_norm_w': 'new_v', 'new_v_lru_conv_w': 'new_v', 'new_v_lru_conv_b': 'new_v', 'new_v_lru_wa': 'new_v', 'new_v_lru_ba': 'new_v', 'new_v_lru_wx': 'new_v', 'new_v_lru_bx': 'new_v', 'new_v_lru_lambda': 'new_v', 'new_v_lru_norm_w': 'new_v', 'new_v_w_out': 'new_v', 'new_v_norm2_w': 'new_v', 'new_v_w_gate': 'new_v', 'new_v_w_up': 'new_v', 'new_v_w_down': 'new_v', 'new_v_final_norm_w': 'new_v'}


def _forward(args):
    return _fwd_reference(*[args[k] for k in FWD_PARAMS])


def _output_shape():
    out = _jax.eval_shape(lambda: _forward(_fwd_setup_inputs(0)))
    return out.shape, out.dtype

N_MICROBATCH = 1
ADAM_LR = 0.001
ADAM_B1 = 0.9
ADAM_B2 = 0.999
ADAM_EPS = 1e-08
ADAM_WD = 0.01
ADAM_STEP = 10
PER_EXAMPLE_BATCH_AXIS = {'x': 0, 'loss_target': 0}
SHARED_INPUTS = []
_WEIGHT_DTYPES = {'meta_tokens': _jnp.float32, 'norm1_w': _jnp.float32, 'w_in': _jnp.float32, 'ssd_conv_w': _jnp.float32, 'ssd_conv_b': _jnp.float32, 'ssd_dt_bias': _jnp.float32, 'ssd_a_log': _jnp.float32, 'ssd_d': _jnp.float32, 'ssd_norm_w': _jnp.float32, 'lru_conv_w': _jnp.float32, 'lru_conv_b': _jnp.float32, 'lru_wa': _jnp.float32, 'lru_ba': _jnp.float32, 'lru_wx': _jnp.float32, 'lru_bx': _jnp.float32, 'lru_lambda': _jnp.float32, 'lru_norm_w': _jnp.float32, 'w_out': _jnp.float32, 'norm2_w': _jnp.float32, 'w_gate': _jnp.float32, 'w_up': _jnp.float32, 'w_down': _jnp.float32, 'final_norm_w': _jnp.float32}
MOMENT_SCALE = {'meta_tokens': 4.477679e-03, 'norm1_w': 1.387410e-01, 'w_in': 6.655093e-02, 'ssd_conv_w': 6.014777e-02, 'ssd_conv_b': 8.005138e-02, 'ssd_dt_bias': 9.860091e-02, 'ssd_a_log': 2.418193e-01, 'ssd_d': 3.436036e-01, 'ssd_norm_w': 6.983064e-02, 'lru_conv_w': 7.188320e-02, 'lru_conv_b': 6.425862e-01, 'lru_wa': 2.567597e-02, 'lru_ba': 1.940921e-02, 'lru_wx': 4.745409e-02, 'lru_bx': 2.439692e-02, 'lru_lambda': 3.590407e-02, 'lru_norm_w': 7.045282e-02, 'w_out': 9.571034e-02, 'norm2_w': 7.152622e-02, 'w_gate': 3.029397e-02, 'w_up': 2.938107e-02, 'w_down': 4.863511e-02, 'final_norm_w': 1.600849e+01}


def _to_microbatches(a, axis):
    t = _jnp.moveaxis(a, axis, 0)
    t = t.reshape((N_MICROBATCH, t.shape[0] // N_MICROBATCH) + t.shape[1:])
    return _jnp.moveaxis(t, 1, axis + 1)


def setup_inputs(seed: int = 0) -> dict:
    inp = _fwd_setup_inputs(seed)
    key = _jax.random.fold_in(_jax.random.key(seed), 7919)
    shape, _ = _output_shape()
    out = dict(inp)
    out["loss_target"] = _jax.random.normal(_jax.random.fold_in(key, 0), shape, _jnp.float32)
    for i, name in enumerate(TWIN_WEIGHTS):
        w = inp[name].astype(_jnp.float32)
        if MOMENT_SCALE is None:
            s = _jnp.sqrt(_jnp.mean(_jnp.square(w)) + 1e-30)
        else:
            s = MOMENT_SCALE[name]
        km, kv = _jax.random.split(_jax.random.fold_in(key, i + 1))
        out[name] = w
        out["m_" + name] = s * _jax.random.normal(km, w.shape, _jnp.float32)
        out["v_" + name] = (s * s) * _jax.random.uniform(kv, w.shape, _jnp.float32, 0.5, 1.5)
    if N_MICROBATCH > 1:
        for name, axis in PER_EXAMPLE_BATCH_AXIS.items():
            out[name] = _to_microbatches(out[name], axis)
    return {'x': out['x'], 'meta_tokens': out['meta_tokens'], 'norm1_w': out['norm1_w'], 'w_in': out['w_in'], 'ssd_conv_w': out['ssd_conv_w'], 'ssd_conv_b': out['ssd_conv_b'], 'ssd_dt_bias': out['ssd_dt_bias'], 'ssd_a_log': out['ssd_a_log'], 'ssd_d': out['ssd_d'], 'ssd_norm_w': out['ssd_norm_w'], 'lru_conv_w': out['lru_conv_w'], 'lru_conv_b': out['lru_conv_b'], 'lru_wa': out['lru_wa'], 'lru_ba': out['lru_ba'], 'lru_wx': out['lru_wx'], 'lru_bx': out['lru_bx'], 'lru_lambda': out['lru_lambda'], 'lru_norm_w': out['lru_norm_w'], 'w_out': out['w_out'], 'norm2_w': out['norm2_w'], 'w_gate': out['w_gate'], 'w_up': out['w_up'], 'w_down': out['w_down'], 'final_norm_w': out['final_norm_w'], 'loss_target': out['loss_target'], 'm_meta_tokens': out['m_meta_tokens'], 'm_norm1_w': out['m_norm1_w'], 'm_w_in': out['m_w_in'], 'm_ssd_conv_w': out['m_ssd_conv_w'], 'm_ssd_conv_b': out['m_ssd_conv_b'], 'm_ssd_dt_bias': out['m_ssd_dt_bias'], 'm_ssd_a_log': out['m_ssd_a_log'], 'm_ssd_d': out['m_ssd_d'], 'm_ssd_norm_w': out['m_ssd_norm_w'], 'm_lru_conv_w': out['m_lru_conv_w'], 'm_lru_conv_b': out['m_lru_conv_b'], 'm_lru_wa': out['m_lru_wa'], 'm_lru_ba': out['m_lru_ba'], 'm_lru_wx': out['m_lru_wx'], 'm_lru_bx': out['m_lru_bx'], 'm_lru_lambda': out['m_lru_lambda'], 'm_lru_norm_w': out['m_lru_norm_w'], 'm_w_out': out['m_w_out'], 'm_norm2_w': out['m_norm2_w'], 'm_w_gate': out['m_w_gate'], 'm_w_up': out['m_w_up'], 'm_w_down': out['m_w_down'], 'm_final_norm_w': out['m_final_norm_w'], 'v_meta_tokens': out['v_meta_tokens'], 'v_norm1_w': out['v_norm1_w'], 'v_w_in': out['v_w_in'], 'v_ssd_conv_w': out['v_ssd_conv_w'], 'v_ssd_conv_b': out['v_ssd_conv_b'], 'v_ssd_dt_bias': out['v_ssd_dt_bias'], 'v_ssd_a_log': out['v_ssd_a_log'], 'v_ssd_d': out['v_ssd_d'], 'v_ssd_norm_w': out['v_ssd_norm_w'], 'v_lru_conv_w': out['v_lru_conv_w'], 'v_lru_conv_b': out['v_lru_conv_b'], 'v_lru_wa': out['v_lru_wa'], 'v_lru_ba': out['v_lru_ba'], 'v_lru_wx': out['v_lru_wx'], 'v_lru_bx': out['v_lru_bx'], 'v_lru_lambda': out['v_lru_lambda'], 'v_lru_norm_w': out['v_lru_norm_w'], 'v_w_out': out['v_w_out'], 'v_norm2_w': out['v_norm2_w'], 'v_w_gate': out['v_w_gate'], 'v_w_up': out['v_w_up'], 'v_w_down': out['v_w_down'], 'v_final_norm_w': out['v_final_norm_w']}


def _loss(weights, diff, rest, loss_target):
    with _jax.named_scope("forward"):
        args = {**rest, TWIN_DIFF_INPUT: diff, **{k: w.astype(_WEIGHT_DTYPES[k]) for k, w in weights.items()}}
        y = _forward(args)
    with _jax.named_scope("loss_head"):
        err = _jnp.square(y.astype(_jnp.float32) - loss_target)
        return 0.5 * _jnp.sum(_jnp.mean(err, axis=-1)) if err.ndim else 0.5 * err


def _adamw(w, g, m, v):
    m = ADAM_B1 * m + (1.0 - ADAM_B1) * g
    v = ADAM_B2 * v + (1.0 - ADAM_B2) * _jnp.square(g)
    m_hat = m / (1.0 - ADAM_B1 ** ADAM_STEP)
    v_hat = v / (1.0 - ADAM_B2 ** ADAM_STEP)
    delta = -ADAM_LR * (m_hat / (_jnp.sqrt(v_hat) + ADAM_EPS) + ADAM_WD * w)
    return delta, m, v


def reference(x, meta_tokens, norm1_w, w_in, ssd_conv_w, ssd_conv_b, ssd_dt_bias, ssd_a_log, ssd_d, ssd_norm_w, lru_conv_w, lru_conv_b, lru_wa, lru_ba, lru_wx, lru_bx, lru_lambda, lru_norm_w, w_out, norm2_w, w_gate, w_up, w_down, final_norm_w, loss_target, m_meta_tokens, m_norm1_w, m_w_in, m_ssd_conv_w, m_ssd_conv_b, m_ssd_dt_bias, m_ssd_a_log, m_ssd_d, m_ssd_norm_w, m_lru_conv_w, m_lru_conv_b, m_lru_wa, m_lru_ba, m_lru_wx, m_lru_bx, m_lru_lambda, m_lru_norm_w, m_w_out, m_norm2_w, m_w_gate, m_w_up, m_w_down, m_final_norm_w, v_meta_tokens, v_norm1_w, v_w_in, v_ssd_conv_w, v_ssd_conv_b, v_ssd_dt_bias, v_ssd_a_log, v_ssd_d, v_ssd_norm_w, v_lru_conv_w, v_lru_conv_b, v_lru_wa, v_lru_ba, v_lru_wx, v_lru_bx, v_lru_lambda, v_lru_norm_w, v_w_out, v_norm2_w, v_w_gate, v_w_up, v_w_down, v_final_norm_w):
    given = dict(x=x, meta_tokens=meta_tokens, norm1_w=norm1_w, w_in=w_in, ssd_conv_w=ssd_conv_w, ssd_conv_b=ssd_conv_b, ssd_dt_bias=ssd_dt_bias, ssd_a_log=ssd_a_log, ssd_d=ssd_d, ssd_norm_w=ssd_norm_w, lru_conv_w=lru_conv_w, lru_conv_b=lru_conv_b, lru_wa=lru_wa, lru_ba=lru_ba, lru_wx=lru_wx, lru_bx=lru_bx, lru_lambda=lru_lambda, lru_norm_w=lru_norm_w, w_out=w_out, norm2_w=norm2_w, w_gate=w_gate, w_up=w_up, w_down=w_down, final_norm_w=final_norm_w, loss_target=loss_target, m_meta_tokens=m_meta_tokens, m_norm1_w=m_norm1_w, m_w_in=m_w_in, m_ssd_conv_w=m_ssd_conv_w, m_ssd_conv_b=m_ssd_conv_b, m_ssd_dt_bias=m_ssd_dt_bias, m_ssd_a_log=m_ssd_a_log, m_ssd_d=m_ssd_d, m_ssd_norm_w=m_ssd_norm_w, m_lru_conv_w=m_lru_conv_w, m_lru_conv_b=m_lru_conv_b, m_lru_wa=m_lru_wa, m_lru_ba=m_lru_ba, m_lru_wx=m_lru_wx, m_lru_bx=m_lru_bx, m_lru_lambda=m_lru_lambda, m_lru_norm_w=m_lru_norm_w, m_w_out=m_w_out, m_norm2_w=m_norm2_w, m_w_gate=m_w_gate, m_w_up=m_w_up, m_w_down=m_w_down, m_final_norm_w=m_final_norm_w, v_meta_tokens=v_meta_tokens, v_norm1_w=v_norm1_w, v_w_in=v_w_in, v_ssd_conv_w=v_ssd_conv_w, v_ssd_conv_b=v_ssd_conv_b, v_ssd_dt_bias=v_ssd_dt_bias, v_ssd_a_log=v_ssd_a_log, v_ssd_d=v_ssd_d, v_ssd_norm_w=v_ssd_norm_w, v_lru_conv_w=v_lru_conv_w, v_lru_conv_b=v_lru_conv_b, v_lru_wa=v_lru_wa, v_lru_ba=v_lru_ba, v_lru_wx=v_lru_wx, v_lru_bx=v_lru_bx, v_lru_lambda=v_lru_lambda, v_lru_norm_w=v_lru_norm_w, v_w_out=v_w_out, v_norm2_w=v_norm2_w, v_w_gate=v_w_gate, v_w_up=v_w_up, v_w_down=v_w_down, v_final_norm_w=v_final_norm_w)
    weights = {n: given[n] for n in TWIN_WEIGHTS}
    shared = {n: given[n] for n in SHARED_INPUTS}
    per_example = {n: given[n] for n in ['x']}
    grad_fn = _jax.value_and_grad(_loss, argnums=(0, 1))

    def one_microbatch(ex, loss_target):
        ex = dict(ex)
        diff = ex.pop(TWIN_DIFF_INPUT)
        return grad_fn(weights, diff, {**shared, **ex}, loss_target)

    if N_MICROBATCH == 1:
        loss, (grad_w, grad_x) = one_microbatch(per_example, given["loss_target"])
    else:
        def body(carry, xs):
            loss_sum, grad_sum = carry
            l_k, (gw_k, gx_k) = one_microbatch(xs[0], xs[1])
            with _jax.named_scope("update"):
                return (loss_sum + l_k, _jax.tree.map(_jnp.add, grad_sum, gw_k)), gx_k

        init = (_jnp.zeros((), _jnp.float32), _jax.tree.map(_jnp.zeros_like, weights))
        (loss, grad_w), grad_x = _jax.lax.scan(body, init, (per_example, given["loss_target"]))
    with _jax.named_scope("update"):
        delta_w, new_m, new_v = {}, {}, {}
        for n in TWIN_WEIGHTS:
            delta_w[n], new_m[n], new_v[n] = _adamw(weights[n], grad_w[n], given["m_" + n], given["v_" + n])
    return (loss, grad_x, *[grad_w[n] for n in TWIN_WEIGHTS], *[delta_w[n] for n in TWIN_WEIGHTS],
            *[new_m[n] for n in TWIN_WEIGHTS], *[new_v[n] for n in TWIN_WEIGHTS])
```

```python
import math

import jax
import jax.numpy as jnp
from jax import lax
from jax.experimental import pallas as pl
from jax.experimental.pallas import tpu as pltpu

F32 = jnp.float32
BF = jnp.bfloat16

D = 1024
SEQ = 2048
N_META = 16
Q = 128
NPAD = 112
T = NPAD + N_META + SEQ
NCH = T // Q
RC = 544
D_FF = 2816
SSD_W = 1024
LRU_W = 1024
XBC = 1536
IN_COLS = 4624
PZ, PG, PXL, PXBC, PDT = 0, 1024, 2048, 3072, 4608
NP_IN = 5120
EPS = 1e-6
LRU_C = 8.0
VMEM_LIMIT = 56 * 1024 * 1024

ADAM_LR, ADAM_B1, ADAM_B2, ADAM_EPS, ADAM_WD, ADAM_STEP = 0.001, 0.9, 0.999, 1e-08, 0.01, 10

NT_DIMS = (((1,), (1,)), ((), ()))
TN_DIMS = (((0,), (0,)), ((), ()))
MESH = pl.DeviceIdType.MESH


def _params(n_grid=1, limit=VMEM_LIMIT):
    return pltpu.CompilerParams(dimension_semantics=("arbitrary",) * n_grid, vmem_limit_bytes=limit)


def _spec(shape, imap, single=False):
    if single:
        return pl.BlockSpec(shape, imap, pipeline_mode=pl.Buffered(1))
    return pl.BlockSpec(shape, imap)


def _sigmoid(x):
    return 1.0 / (1.0 + jnp.exp(-x))


def _softplus(x):
    return jnp.maximum(x, 0.0) + jnp.log(1.0 + jnp.exp(-jnp.abs(x)))


def _rms_stats(h):
    return lax.rsqrt(jnp.mean(h * h, axis=-1, keepdims=True) + EPS)


def _rms(h, w):
    return (h * _rms_stats(h)) * w


def _rms_bwd(du, h, w):
    r = _rms_stats(h)
    n = h * r
    dn = du * w
    dh = r * (dn - n * jnp.mean(dn * n, axis=-1, keepdims=True))
    return dh, du * n


_G0 = math.sqrt(2.0 / math.pi)


def _gelu(x):
    return 0.5 * x * (1.0 + jnp.tanh(_G0 * (x + 0.044715 * (x * x * x))))


def _gelu_grad(x):
    t = jnp.tanh(_G0 * (x + 0.044715 * (x * x * x)))
    return 0.5 * (1.0 + t) + 0.5 * x * (1.0 - t * t) * (_G0 * (1.0 + 3.0 * 0.044715 * (x * x)))


def _rows(shape, r0=0):
    return lax.broadcasted_iota(jnp.int32, shape, 0) + r0


def _lanes(shape):
    return lax.broadcasted_iota(jnp.int32, shape, 1)


def _shift_down(x, s):
    if s == 0:
        return x
    return jnp.where(_rows(x.shape) >= s, pltpu.roll(x, s, axis=0), 0.0)


def _shift_up(x, s):
    if s == 0:
        return x
    n = x.shape[0]
    return jnp.where(_rows(x.shape) < n - s, pltpu.roll(x, n - s, axis=0), 0.0)


def _conv(x, w, b):
    y = b + w[3:4, :] * x
    for k in range(3):
        y = y + w[k:k + 1, :] * _shift_down(x, 3 - k)
    return y


def _conv_bwd(dy, x, w):
    dx = w[3:4, :] * dy
    dws = []
    for k in range(3):
        dx = dx + w[k:k + 1, :] * _shift_up(dy, 3 - k)
        dws.append(jnp.sum(dy * _shift_down(x, 3 - k), axis=0, keepdims=True))
    dws.append(jnp.sum(dy * x, axis=0, keepdims=True))
    return dx, jnp.concatenate(dws, axis=0), jnp.sum(dy, axis=0, keepdims=True)


def _row_chunks(fn):
    def step(m, carry):
        fn(pl.multiple_of(m * RC, RC))
        return carry

    lax.fori_loop(0, T // RC, step, 0)


def _q_chunks(fn):
    def step(m, carry):
        fn(pl.multiple_of(m * Q, Q))
        return carry

    lax.fori_loop(0, NCH, step, 0)


def norm_matmul(name, h, wn, w, tn, resid=None):
    k, n = w.shape

    def body(*refs):
        if resid is None:
            h_ref, wn_ref, w_ref, o_ref, u_ref = refs
        else:
            h_ref, wn_ref, w_ref, r_ref, o_ref, u_ref = refs

        @pl.when(pl.program_id(0) == 0)
        def _():
            def norm(r0):
                u_ref[pl.ds(r0, Q), :] = _rms(h_ref[pl.ds(r0, Q), :], wn_ref[...]).astype(BF)

            _q_chunks(norm)

        def mm(r0):
            acc = jnp.dot(u_ref[pl.ds(r0, RC), :], w_ref[...], preferred_element_type=F32)
            if resid is not None:
                acc = acc + r_ref[pl.ds(r0, RC), :]
            o_ref[pl.ds(r0, RC), :] = acc

        _row_chunks(mm)

    in_specs = [_spec((T, k), lambda j: (0, 0), single=True), _spec((1, k), lambda j: (0, 0)), _spec((k, tn), lambda j: (0, j))]
    args = [h, wn, w]
    if resid is not None:
        in_specs.append(_spec((T, tn), lambda j: (0, j)))
        args.append(resid)
    return pl.pallas_call(
        body, grid=(n // tn,), in_specs=in_specs,
        out_specs=[_spec((T, tn), lambda j: (0, j)), _spec((T, k), lambda j: (0, 0))],
        out_shape=[jax.ShapeDtypeStruct((T, n), F32), jax.ShapeDtypeStruct((T, k), BF)],
        compiler_params=_params(), name=name)(*args)


def matmul(name, a, w, tn, *, nt=False, resid=None, w_row0=0):
    k = a.shape[1]
    n = w.shape[0] if nt else w.shape[1]

    def body(*refs):
        if resid is None:
            a_ref, w_ref, o_ref = refs
        else:
            a_ref, w_ref, r_ref, o_ref = refs

        def mm(r0):
            if nt:
                acc = lax.dot_general(a_ref[pl.ds(r0, RC), :], w_ref[...], NT_DIMS, preferred_element_type=F32)
            else:
                acc = jnp.dot(a_ref[pl.ds(r0, RC), :], w_ref[...], preferred_element_type=F32)
            if resid is not None:
                acc = acc + r_ref[pl.ds(r0, RC), :]
            o_ref[pl.ds(r0, RC), :] = acc

        _row_chunks(mm)

    w_spec = _spec((tn, k), lambda j: (j, 0)) if nt else _spec((k, tn), lambda j: (w_row0, j))
    in_specs = [_spec((T, k), lambda j: (0, 0), single=True), w_spec]
    args = [a, w]
    if resid is not None:
        in_specs.append(_spec((T, tn), lambda j: (0, j)))
        args.append(resid)
    return pl.pallas_call(
        body, grid=(n // tn,), in_specs=in_specs, out_specs=_spec((T, tn), lambda j: (0, j)),
        out_shape=jax.ShapeDtypeStruct((T, n), F32), compiler_params=_params(), name=name)(*args)


def matmul_tn(name, a, b, tm, tn):
    m, n = a.shape[1], b.shape[1]

    def body(a_ref, b_ref, o_ref, acc_ref):
        acc_ref[...] = jnp.zeros_like(acc_ref)

        def mm(r0):
            acc_ref[...] += lax.dot_general(a_ref[pl.ds(r0, RC), :], b_ref[pl.ds(r0, RC), :], TN_DIMS, preferred_element_type=F32)

        _row_chunks(mm)
        o_ref[...] = acc_ref[...].astype(BF)

    return pl.pallas_call(
        body, grid=(m // tm, n // tn),
        in_specs=[_spec((T, tm), lambda i, j: (0, i)), _spec((T, tn), lambda i, j: (0, j))],
        out_specs=_spec((tm, tn), lambda i, j: (i, j)),
        out_shape=jax.ShapeDtypeStruct((m, n), BF),
        scratch_shapes=[pltpu.VMEM((tm, tn), F32)],
        compiler_params=_params(2), name=name)(a, b)


def conv_silu_fwd(proj, cw, cb):
    def body(x_ref, w_ref, b_ref, o_ref):
        pre = _conv(x_ref[...], w_ref[...], b_ref[...])
        o_ref[...] = pre * _sigmoid(pre)

    c0 = PXBC // 128
    return pl.pallas_call(
        body, grid=(XBC // 128,),
        in_specs=[_spec((T, 128), lambda c: (0, c0 + c)), _spec((4, 128), lambda c: (0, c)), _spec((1, 128), lambda c: (0, c))],
        out_specs=_spec((T, 128), lambda c: (0, c)),
        out_shape=jax.ShapeDtypeStruct((T, XBC), F32), compiler_params=_params(), name="conv_silu_fwd")(proj, cw, cb)


def conv_silu_bwd(dact, proj, cw, cb):
    def body(d_ref, x_ref, w_ref, b_ref, dx_ref, dw_ref, db_ref):
        x = x_ref[...]
        pre = _conv(x, w_ref[...], b_ref[...])
        sg = _sigmoid(pre)
        dpre = d_ref[...] * (sg * (1.0 + pre * (1.0 - sg)))
        dx, dw, db = _conv_bwd(dpre, x, w_ref[...])
        dx_ref[...] = dx.astype(BF)
        dw_ref[...] = dw
        db_ref[...] = db

    c0 = PXBC // 128
    return pl.pallas_call(
        body, grid=(XBC // 128,),
        in_specs=[_spec((T, 128), lambda c: (0, c)), _spec((T, 128), lambda c: (0, c0 + c)),
                  _spec((4, 128), lambda c: (0, c)), _spec((1, 128), lambda c: (0, c))],
        out_specs=[_spec((T, 128), lambda c: (0, c)), _spec((4, 128), lambda c: (0, c)), _spec((1, 128), lambda c: (0, c))],
        out_shape=[jax.ShapeDtypeStruct((T, XBC), BF), jax.ShapeDtypeStruct((4, XBC), F32), jax.ShapeDtypeStruct((1, XBC), F32)],
        compiler_params=_params(), name="conv_silu_bwd")(dact, proj, cw, cb)


def _ssd_chunk_common(row0, dt_ref, b_ref, c_ref, bias, a_neg):
    shape = (Q, Q)
    lane = _lanes(shape)
    sub = _rows(shape)
    live = (_rows(shape, row0) >= NPAD) & (lane < 8)
    dtr = dt_ref[:, :]
    dt = jnp.where(live, _softplus(dtr + bias), 0.0)
    d_a = dt * a_neg
    tri = (sub >= lane).astype(F32)
    cs = jnp.dot(tri, d_a, precision=lax.Precision.HIGHEST, preferred_element_type=F32)
    cs_t = cs.T
    bc = b_ref[:, :].astype(BF)
    cc = c_ref[:, :].astype(BF)
    cb = lax.dot_general(cc, bc, NT_DIMS, preferred_element_type=F32)
    cs_last = cs[Q - 1:Q, :]
    return dict(lane=lane, sub=sub, live=live, dtr=dtr, dt=dt, cs=cs, cs_t=cs_t, bc=bc, cc=cc, cb=cb,
                ecs=jnp.exp(cs), dsm=jnp.exp(cs_last - cs), gam=jnp.exp(cs_last), tri=tri)


def _pair(lane_even, mat, j):
    return jnp.where(lane_even, mat[:, j:j + 1], mat[:, j + 1:j + 2])


def _head_decay(cm, j):
    seg = cm["cs"][:, j:j + 1] - cm["cs_t"][j:j + 1, :]
    return jnp.exp(jnp.where(cm["sub"] >= cm["lane"], seg, -jnp.inf))


def ssd_fwd(xbc_act, proj, dt_bias2, a_log2, d2, norm_w):
    def body(x_ref, b_ref, c_ref, dt_ref, z_ref, bias_ref, alog_ref, d_ref, nw_ref, yn_ref, y_ref, hp_ref, h_scr):
        c = pl.program_id(1)

        @pl.when(c == 0)
        def _():
            h_scr[...] = jnp.zeros_like(h_scr)

        bias = bias_ref[0]
        a_neg = -jnp.exp(alog_ref[0])
        dsk = d_ref[0]
        cm = _ssd_chunk_common(c * Q, dt_ref, b_ref, c_ref, bias, a_neg)
        lane_even = cm["lane"] < 64
        sub_even = cm["sub"] < 64
        for p in range(4):
            je, jo = 2 * p, 2 * p + 1
            xp = x_ref[:, 128 * p:128 * p + 128]
            xdt = xp * _pair(lane_even, cm["dt"], je)
            xdt_b = xdt.astype(BF)
            m_e = (cm["cb"] * _head_decay(cm, je)).astype(BF)
            m_o = (cm["cb"] * _head_decay(cm, jo)).astype(BF)
            zero = jnp.zeros_like(xdt_b)
            yd = (jnp.dot(m_e, jnp.where(lane_even, xdt_b, zero), preferred_element_type=F32)
                  + jnp.dot(m_o, jnp.where(lane_even, zero, xdt_b), preferred_element_type=F32))
            hp = h_scr[p]
            hp_ref[0, 0, p] = hp
            yo = lax.dot_general(cm["cc"], hp.astype(BF), NT_DIMS, preferred_element_type=F32) * _pair(lane_even, cm["ecs"], je)
            dsk_p = jnp.where(lane_even[0:1, :], dsk[:, je:je + 1], dsk[:, jo:jo + 1])
            y_ref[:, 128 * p:128 * p + 128] = yd + yo + xp * dsk_p
            st = lax.dot_general((xdt * _pair(lane_even, cm["dsm"], je)).astype(BF), cm["bc"], TN_DIMS, preferred_element_type=F32)
            gam = jnp.where(sub_even[:, 0:1], cm["gam"][:, je:je + 1], cm["gam"][:, jo:jo + 1])
            h_scr[p] = hp * gam + st
        zc = z_ref[:, :]
        gated = y_ref[:, :] * (zc * _sigmoid(zc))
        yn_ref[:, :] = _rms(gated, nw_ref[...]).astype(BF)

    par = _spec((1, 1, 128), lambda g, c: (g, 0, 0))
    wide = _spec((Q, 512), lambda g, c: (c, g))
    return pl.pallas_call(
        body, grid=(2, NCH),
        in_specs=[wide, _spec((Q, 128), lambda g, c: (c, 8 + g)), _spec((Q, 128), lambda g, c: (c, 10 + g)),
                  _spec((Q, 128), lambda g, c: (c, PDT // 128 + g)), wide, par, par, par, _spec((1, 512), lambda g, c: (0, g))],
        out_specs=[wide, wide, _spec((1, 1, 4, 128, 128), lambda g, c: (g, c, 0, 0, 0))],
        out_shape=[jax.ShapeDtypeStruct((T, SSD_W), BF), jax.ShapeDtypeStruct((T, SSD_W), F32),
                   jax.ShapeDtypeStruct((2, NCH, 4, 128, 128), F32)],
        scratch_shapes=[pltpu.VMEM((4, 128, 128), F32)],
        compiler_params=_params(2), name="ssd_fwd")(xbc_act, xbc_act, xbc_act, proj, proj, dt_bias2, a_log2, d2, norm_w)


def ssd_bwd(dyn, xbc_act, proj, y_pre, h_prev, dt_bias2, a_log2, d2, norm_w):
    def body(dyn_ref, x_ref, b_ref, c_ref, dt_ref, z_ref, y_ref, hp_ref, bias_ref, alog_ref, d_ref, nw_ref,
             dz_ref, dx_ref, db_ref, dc_ref, ddt_ref, dpar_ref, dnw_ref, dh_scr, acc_scr):
        ci = pl.program_id(1)

        @pl.when(ci == 0)
        def _():
            dh_scr[...] = jnp.zeros_like(dh_scr)
            acc_scr[...] = jnp.zeros_like(acc_scr)
            dnw_ref[...] = jnp.zeros_like(dnw_ref)

        bias = bias_ref[0]
        a_neg = -jnp.exp(alog_ref[0])
        dsk = d_ref[0]
        cm = _ssd_chunk_common((NCH - 1 - ci) * Q, dt_ref, b_ref, c_ref, bias, a_neg)
        lane, sub = cm["lane"], cm["sub"]
        lane_even = lane < 64
        sub_even = sub < 64
        zc = z_ref[:, :]
        yc = y_ref[:, :]
        sg = _sigmoid(zc)
        sz = zc * sg
        dgated, dnw = _rms_bwd(dyn_ref[:, :], yc * sz, nw_ref[...])
        dnw_ref[...] += jnp.sum(dnw, axis=0, keepdims=True)
        dz_ref[:, :] = (dgated * yc * (sg * (1.0 + zc * (1.0 - sg)))).astype(BF)
        dy_all = dgated * sz
        dcb = jnp.zeros((Q, Q), F32)
        db_acc = jnp.zeros((Q, Q), F32)
        dc_acc = jnp.zeros((Q, Q), F32)
        dcs_col = jnp.zeros((Q, Q), F32)
        dcs_row = jnp.zeros((Q, Q), F32)
        ddt = jnp.zeros((Q, Q), F32)
        for p in range(4):
            je, jo = 2 * p, 2 * p + 1
            xp = x_ref[:, 128 * p:128 * p + 128]
            dy = dy_all[:, 128 * p:128 * p + 128]
            dt_p = _pair(lane_even, cm["dt"], je)
            xdt = xp * dt_p
            xdt_b = xdt.astype(BF)
            dy_b = dy.astype(BF)
            zero = jnp.zeros_like(dy_b)
            hp = hp_ref[0, 0, p]
            hp_b = hp.astype(BF)
            dh = dh_scr[p]
            dh_b = dh.astype(BF)
            acc_scr[p:p + 1, :] += jnp.sum(dy * xp, axis=0, keepdims=True)
            dsk_p = jnp.where(lane_even[0:1, :], dsk[:, je:je + 1], dsk[:, jo:jo + 1])
            dxp = dy * dsk_p
            e_p = _pair(lane_even, cm["ecs"], je)
            g_p = lax.dot_general(cm["cc"], hp_b, NT_DIMS, preferred_element_type=F32)
            dg_b = (dy * e_p).astype(BF)
            de = dy * g_p * e_p
            dc_acc = dc_acc + jnp.dot(dg_b, hp_b, preferred_element_type=F32)
            dh_in = lax.dot_general(dg_b, cm["cc"], TN_DIMS, preferred_element_type=F32)
            ds_p = _pair(lane_even, cm["dsm"], je)
            r_p = lax.dot_general(cm["bc"], dh_b, NT_DIMS, preferred_element_type=F32)
            dxdt = r_p * ds_p
            tt = r_p * xdt * ds_p
            db_acc = db_acc + jnp.dot((xdt * ds_p).astype(BF), dh_b, preferred_element_type=F32)
            dgam_m = dh * hp
            for j, even in ((je, True), (jo, False)):
                sel = lane_even if even else jnp.logical_not(lane_even)
                ssel = sub_even if even else jnp.logical_not(sub_even)
                l_j = _head_decay(cm, j)
                m_j = cm["cb"] * l_j
                dm = lax.dot_general(jnp.where(sel, dy_b, zero), xdt_b, NT_DIMS, preferred_element_type=F32)
                dxdt = dxdt + lax.dot_general(m_j.astype(BF), jnp.where(sel, dy_b, zero), TN_DIMS, preferred_element_type=F32)
                w_j = dm * m_j
                dcb = dcb + dm * l_j
                t_j = jnp.sum(jnp.where(sel, tt, 0.0), axis=1, keepdims=True)
                col = (jnp.sum(w_j, axis=1, keepdims=True) + jnp.sum(jnp.where(sel, de, 0.0), axis=1, keepdims=True) - t_j)
                gam_j = cm["gam"][:, j:j + 1]
                last = (jnp.sum(t_j, axis=0, keepdims=True)
                        + jnp.sum(jnp.sum(jnp.where(ssel, dgam_m, 0.0), axis=1, keepdims=True), axis=0, keepdims=True) * gam_j)
                col = col + jnp.where(sub[:, 0:1] == Q - 1, last, 0.0)
                dcs_col = dcs_col + jnp.where(lane == j, col, 0.0)
                dcs_row = dcs_row + jnp.where(sub == j, jnp.sum(w_j, axis=0, keepdims=True), 0.0)
            gam = jnp.where(sub_even[:, 0:1], cm["gam"][:, je:je + 1], cm["gam"][:, jo:jo + 1])
            dh_scr[p] = dh_in + dh * gam
            dx_ref[:, 128 * p:128 * p + 128] = dxp + dxdt * dt_p
            dd = dxdt * xp
            ddt = ddt + jnp.where(lane == je, jnp.sum(jnp.where(lane_even, dd, 0.0), axis=1, keepdims=True), 0.0)
            ddt = ddt + jnp.where(lane == jo, jnp.sum(jnp.where(lane_even, 0.0, dd), axis=1, keepdims=True), 0.0)
        dcb_b = dcb.astype(BF)
        dc_ref[:, :] = dc_acc + jnp.dot(dcb_b, cm["bc"], preferred_element_type=F32)
        db_ref[:, :] = db_acc + lax.dot_general(dcb_b, cm["cc"], TN_DIMS, preferred_element_type=F32)
        dcs = dcs_col - dcs_row.T
        dd_a = lax.dot_general(cm["tri"], dcs, TN_DIMS, precision=lax.Precision.HIGHEST, preferred_element_type=F32)
        ddt = ddt + dd_a * a_neg
        acc_scr[5:6, :] += jnp.sum(dd_a * cm["dt"], axis=0, keepdims=True)
        draw = jnp.where(cm["live"], ddt * _sigmoid(cm["dtr"] + bias), 0.0)
        acc_scr[4:5, :] += jnp.sum(draw, axis=0, keepdims=True)
        ddt_ref[:, :] = draw.astype(BF)

        @pl.when(ci == NCH - 1)
        def _():
            lane1 = _lanes((1, 128))
            dd = jnp.zeros((1, 128), F32)
            for p in range(4):
                row = acc_scr[p:p + 1, :]
                dd = dd + jnp.where(lane1 == 2 * p, jnp.sum(jnp.where(lane1 < 64, row, 0.0), axis=1, keepdims=True), 0.0)
                dd = dd + jnp.where(lane1 == 2 * p + 1, jnp.sum(jnp.where(lane1 < 64, 0.0, row), axis=1, keepdims=True), 0.0)
            dpar_ref[0] = jnp.concatenate([acc_scr[4:5, :], acc_scr[5:6, :] * a_neg, dd, jnp.zeros((5, 128), F32)], axis=0)

    par = _spec((1, 1, 128), lambda g, c: (g, 0, 0))
    wide = _spec((Q, 512), lambda g, c: (NCH - 1 - c, g))
    thin = _spec((Q, 128), lambda g, c: (NCH - 1 - c, g))
    return pl.pallas_call(
        body, grid=(2, NCH),
        in_specs=[wide, wide, _spec((Q, 128), lambda g, c: (NCH - 1 - c, 8 + g)), _spec((Q, 128), lambda g, c: (NCH - 1 - c, 10 + g)),
                  _spec((Q, 128), lambda g, c: (NCH - 1 - c, PDT // 128 + g)), wide, wide,
                  _spec((1, 1, 4, 128, 128), lambda g, c: (g, NCH - 1 - c, 0, 0, 0)), par, par, par, _spec((1, 512), lambda g, c: (0, g))],
        out_specs=[wide, wide, thin, thin, thin, _spec((1, 8, 128), lambda g, c: (g, 0, 0)), _spec((1, 512), lambda g, c: (0, g))],
        out_shape=[jax.ShapeDtypeStruct((T, SSD_W), BF), jax.ShapeDtypeStruct((T, SSD_W), F32), jax.ShapeDtypeStruct((T, 256), F32),
                   jax.ShapeDtypeStruct((T, 256), F32), jax.ShapeDtypeStruct((T, 256), BF), jax.ShapeDtypeStruct((2, 8, 128), F32),
                   jax.ShapeDtypeStruct((1, SSD_W), F32)],
        scratch_shapes=[pltpu.VMEM((4, 128, 128), F32), pltpu.VMEM((8, 128), F32)],
        compiler_params=_params(2), name="ssd_bwd")(dyn, xbc_act, xbc_act, xbc_act, proj, proj, y_pre, h_prev, dt_bias2, a_log2, d2, norm_w)


def _lru_gates(xl, cw, cb, wa, ba, wx, bx, lam):
    xr = _conv(xl, cw, cb)
    xr_b = xr.astype(BF)
    r = _sigmoid(jnp.dot(xr_b, wa, preferred_element_type=F32) + ba)
    i = _sigmoid(jnp.dot(xr_b, wx, preferred_element_type=F32) + bx)
    sp = _softplus(-lam)
    la = (-LRU_C) * r * sp
    a = jnp.exp(la)
    mult = jnp.sqrt(-jnp.tanh(la) * (a * a + 1.0))
    return xr, xr_b, r, i, sp, a, mult


def lru_gates_fwd(proj, cw, cb, wa2, ba, wx2, bx, lam):
    def body(x_ref, cw_ref, cb_ref, wa_ref, ba_ref, wx_ref, bx_ref, lam_ref, a_ref, u_ref):
        xr, _, _, i, _, a, mult = _lru_gates(x_ref[...], cw_ref[...], cb_ref[...], wa_ref[0], ba_ref[...], wx_ref[0], bx_ref[...], lam_ref[...])
        a_ref[...] = a
        u_ref[...] = jnp.where(_rows(a.shape) >= NPAD, mult * (i * xr), 0.0)

    c0 = PXL // 128
    vec = _spec((1, 128), lambda c: (0, c))
    mat = _spec((1, 128, 128), lambda c: (c, 0, 0))
    return pl.pallas_call(
        body, grid=(8,),
        in_specs=[_spec((T, 128), lambda c: (0, c0 + c)), _spec((4, 128), lambda c: (0, c)), vec, mat, vec, mat, vec, vec],
        out_specs=[_spec((T, 128), lambda c: (0, c)), _spec((T, 128), lambda c: (0, c))],
        out_shape=[jax.ShapeDtypeStruct((T, LRU_W), F32), jax.ShapeDtypeStruct((T, LRU_W), F32)],
        compiler_params=_params(), name="lru_gates_fwd")(proj, cw, cb, wa2, ba, wx2, bx, lam)


def lru_scan_fwd(a, u):
    def body(a_ref, u_ref, h_ref):
        def step(i, h):
            base = pl.multiple_of(i * 8, 8)
            for k in range(8):
                h = a_ref[pl.ds(base + k, 1), :] * h + u_ref[pl.ds(base + k, 1), :]
                h_ref[pl.ds(base + k, 1), :] = h
            return h

        lax.fori_loop(0, T // 8, step, jnp.zeros((1, LRU_W), F32))

    return pl.pallas_call(body, out_shape=jax.ShapeDtypeStruct((T, LRU_W), F32), compiler_params=_params(0), name="lru_scan_fwd")(a, u)


def lru_scan_bwd(a, dh_out):
    def body(a_ref, d_ref, o_ref):
        def step(i, carry):
            base = pl.multiple_of(T - 8 - i * 8, 8)
            for k in range(7, -1, -1):
                carry = d_ref[pl.ds(base + k, 1), :] + carry
                o_ref[pl.ds(base + k, 1), :] = carry
                carry = carry * a_ref[pl.ds(base + k, 1), :]
            return carry

        lax.fori_loop(0, T // 8, step, jnp.zeros((1, LRU_W), F32))

    return pl.pallas_call(body, out_shape=jax.ShapeDtypeStruct((T, LRU_W), F32), compiler_params=_params(0), name="lru_scan_bwd")(a, dh_out)


def lru_out_proj(proj, hseq, norm_w, w_out, resid):
    tn = 512

    def body(g_ref, h_ref, wn_ref, w_ref, r_ref, o_ref, u_ref):
        @pl.when(pl.program_id(0) == 0)
        def _():
            def norm(r0):
                y = _gelu(g_ref[pl.ds(r0, Q), :]) * h_ref[pl.ds(r0, Q), :]
                u_ref[pl.ds(r0, Q), :] = _rms(y, wn_ref[...]).astype(BF)

            _q_chunks(norm)

        def mm(r0):
            o_ref[pl.ds(r0, RC), :] = r_ref[pl.ds(r0, RC), :] + jnp.dot(u_ref[pl.ds(r0, RC), :], w_ref[...], preferred_element_type=F32)

        _row_chunks(mm)

    return pl.pallas_call(
        body, grid=(D // tn,),
        in_specs=[_spec((T, LRU_W), lambda j: (0, PG // LRU_W), single=True), _spec((T, LRU_W), lambda j: (0, 0), single=True),
                  _spec((1, LRU_W), lambda j: (0, 0)), _spec((LRU_W, tn), lambda j: (1, j)), _spec((T, tn), lambda j: (0, j))],
        out_specs=[_spec((T, tn), lambda j: (0, j)), _spec((T, LRU_W), lambda j: (0, 0))],
        out_shape=[jax.ShapeDtypeStruct((T, D), F32), jax.ShapeDtypeStruct((T, LRU_W), BF)],
        compiler_params=_params(), name="lru_out_proj")(proj, hseq, norm_w, w_out, resid)


def lru_norm_bwd(dcat, proj, hseq, norm_w):
    def body(d_ref, g_ref, h_ref, wn_ref, dh_ref, dg_ref, dw_ref):
        @pl.when(pl.program_id(0) == 0)
        def _():
            dw_ref[...] = jnp.zeros_like(dw_ref)

        g = g_ref[...]
        h = h_ref[...]
        ge = _gelu(g)
        dy, dw = _rms_bwd(d_ref[...], ge * h, wn_ref[...])
        dw_ref[...] += jnp.sum(dw, axis=0, keepdims=True)
        dh_ref[...] = dy * ge
        dg_ref[...] = (dy * h * _gelu_grad(g)).astype(BF)

    tile = lambda col: _spec((Q, LRU_W), lambda i: (i, col))
    return pl.pallas_call(
        body, grid=(NCH,),
        in_specs=[tile(1), tile(PG // LRU_W), tile(0), _spec((1, LRU_W), lambda i: (0, 0))],
        out_specs=[tile(0), tile(0), _spec((1, LRU_W), lambda i: (0, 0))],
        out_shape=[jax.ShapeDtypeStruct((T, LRU_W), F32), jax.ShapeDtypeStruct((T, LRU_W), BF), jax.ShapeDtypeStruct((1, LRU_W), F32)],
        compiler_params=_params(), name="lru_norm_bwd")(dcat, proj, hseq, norm_w)


def lru_gates_bwd(dhs, hseq, proj, cw, cb, wa2, ba, wx2, bx, lam):
    def body(dh_ref, h_ref, x_ref, cw_ref, cb_ref, wa_ref, ba_ref, wx_ref, bx_ref, lam_ref,
             dx_ref, dcw_ref, dcb_ref, dwa_ref, dba_ref, dwx_ref, dbx_ref, dlam_ref):
        xl = x_ref[...]
        lam = lam_ref[...]
        xr, xr_b, r, i, sp, a, mult = _lru_gates(xl, cw_ref[...], cb_ref[...], wa_ref[0], ba_ref[...], wx_ref[0], bx_ref[...], lam)
        dh = dh_ref[...]
        da = dh * _shift_down(h_ref[...], 1)
        du = jnp.where(_rows(dh.shape) >= NPAD, dh, 0.0)
        dmult = du * (i * xr)
        di = du * (mult * xr)
        dxr = du * (mult * i)
        dla = da * a - dmult * (a * a) / mult
        dr = dla * ((-LRU_C) * sp)
        dsp = jnp.sum(dla * ((-LRU_C) * r), axis=0, keepdims=True)
        dlam_ref[...] = -dsp * _sigmoid(-lam)
        dpr = dr * r * (1.0 - r)
        dpi = di * i * (1.0 - i)
        dba_ref[...] = jnp.sum(dpr, axis=0, keepdims=True)
        dbx_ref[...] = jnp.sum(dpi, axis=0, keepdims=True)
        dpr_b = dpr.astype(BF)
        dpi_b = dpi.astype(BF)
        dxr = (dxr + lax.dot_general(dpr_b, wa_ref[0], NT_DIMS, preferred_element_type=F32)
               + lax.dot_general(dpi_b, wx_ref[0], NT_DIMS, preferred_element_type=F32))
        dwa_ref[0] = lax.dot_general(xr_b, dpr_b, TN_DIMS, preferred_element_type=F32)
        dwx_ref[0] = lax.dot_general(xr_b, dpi_b, TN_DIMS, preferred_element_type=F32)
        dx, dcw, dcb = _conv_bwd(dxr, xl, cw_ref[...])
        dx_ref[...] = dx.astype(BF)
        dcw_ref[...] = dcw
        dcb_ref[...] = dcb

    c0 = PXL // 128
    vec = _spec((1, 128), lambda c: (0, c))
    mat = _spec((1, 128, 128), lambda c: (c, 0, 0))
    col = _spec((T, 128), lambda c: (0, c))
    vshape = jax.ShapeDtypeStruct((1, LRU_W), F32)
    mshape = jax.ShapeDtypeStruct((8, 128, 128), F32)
    return pl.pallas_call(
        body, grid=(8,),
        in_specs=[col, col, _spec((T, 128), lambda c: (0, c0 + c)), _spec((4, 128), lambda c: (0, c)), vec, mat, vec, mat, vec, vec],
        out_specs=[col, _spec((4, 128), lambda c: (0, c)), vec, mat, vec, mat, vec, vec],
        out_shape=[jax.ShapeDtypeStruct((T, LRU_W), BF), jax.ShapeDtypeStruct((4, LRU_W), F32), vshape, mshape, vshape, mshape, vshape, vshape],
        compiler_params=_params(), name="lru_gates_bwd")(dhs, hseq, proj, cw, cb, wa2, ba, wx2, bx, lam)


def gate_up(h1, wn, w_gate, w_up):
    tn = 256

    def body(h_ref, wn_ref, wg_ref, wu_ref, gt_ref, up_ref, act_ref, u_ref):
        @pl.when(pl.program_id(0) == 0)
        def _():
            def norm(r0):
                u_ref[pl.ds(r0, Q), :] = _rms(h_ref[pl.ds(r0, Q), :], wn_ref[...]).astype(BF)

            _q_chunks(norm)

        def mm(r0):
            u = u_ref[pl.ds(r0, RC), :]
            gt = jnp.dot(u, wg_ref[...], preferred_element_type=F32)
            up = jnp.dot(u, wu_ref[...], preferred_element_type=F32)
            gt_ref[pl.ds(r0, RC), :] = gt.astype(BF)
            up_ref[pl.ds(r0, RC), :] = up.astype(BF)
            act_ref[pl.ds(r0, RC), :] = (gt * _sigmoid(gt) * up).astype(BF)

        _row_chunks(mm)

    tile = _spec((T, tn), lambda j: (0, j))
    big = jax.ShapeDtypeStruct((T, D_FF), BF)
    return pl.pallas_call(
        body, grid=(D_FF // tn,),
        in_specs=[_spec((T, D), lambda j: (0, 0), single=True), _spec((1, D), lambda j: (0, 0)),
                  _spec((D, tn), lambda j: (0, j)), _spec((D, tn), lambda j: (0, j))],
        out_specs=[tile, tile, tile, _spec((T, D), lambda j: (0, 0))],
        out_shape=[big, big, big, jax.ShapeDtypeStruct((T, D), BF)],
        compiler_params=_params(), name="gate_up")(h1, wn, w_gate, w_up)


def loss_bwd(h2, target, wf):
    def body(h_ref, t_ref, w_ref, d_ref, db_ref, l_ref, dw_ref):
        i = pl.program_id(0)

        @pl.when(i == 0)
        def _():
            l_ref[...] = jnp.zeros_like(l_ref)
            dw_ref[...] = jnp.zeros_like(dw_ref)

        h = h_ref[...]
        err = jnp.where(i >= 1, _rms(h, w_ref[...]) - t_ref[...], 0.0)
        l_ref[...] += 0.5 * jnp.sum(jnp.sum(err * err, axis=1, keepdims=True) * (1.0 / D), axis=0, keepdims=True)
        dh, dw = _rms_bwd(err * (1.0 / D), h, w_ref[...])
        dw_ref[...] += jnp.sum(dw, axis=0, keepdims=True)
        d_ref[...] = dh
        db_ref[...] = dh.astype(BF)

    tile = _spec((Q, D), lambda i: (i, 0))
    return pl.pallas_call(
        body, grid=(NCH,),
        in_specs=[tile, _spec((Q, D), lambda i: (jnp.maximum(i - 1, 0), 0)), _spec((1, D), lambda i: (0, 0))],
        out_specs=[tile, tile, _spec((1, 128), lambda i: (0, 0)), _spec((1, D), lambda i: (0, 0))],
        out_shape=[jax.ShapeDtypeStruct((T, D), F32), jax.ShapeDtypeStruct((T, D), BF), jax.ShapeDtypeStruct((1, 128), F32),
                   jax.ShapeDtypeStruct((1, D), F32)],
        compiler_params=_params(), name="loss_bwd")(h2, target, wf)


def swiglu_bwd(dh2_b, w_down, gt, up):
    tn = 256

    def body(d_ref, w_ref, gt_ref, up_ref, dg_ref, du_ref):
        def mm(r0):
            dact = lax.dot_general(d_ref[pl.ds(r0, RC), :], w_ref[...], NT_DIMS, preferred_element_type=F32)
            gt_ = gt_ref[pl.ds(r0, RC), :].astype(F32)
            up_ = up_ref[pl.ds(r0, RC), :].astype(F32)
            sg = _sigmoid(gt_)
            dg_ref[pl.ds(r0, RC), :] = (dact * up_ * (sg * (1.0 + gt_ * (1.0 - sg)))).astype(BF)
            du_ref[pl.ds(r0, RC), :] = (dact * (gt_ * sg)).astype(BF)

        _row_chunks(mm)

    tile = _spec((T, tn), lambda j: (0, j))
    big = jax.ShapeDtypeStruct((T, D_FF), BF)
    return pl.pallas_call(
        body, grid=(D_FF // tn,),
        in_specs=[_spec((T, D), lambda j: (0, 0), single=True), _spec((tn, D), lambda j: (j, 0)), tile, tile],
        out_specs=[tile, tile], out_shape=[big, big], compiler_params=_params(), name="swiglu_bwd")(dh2_b, w_down, gt, up)


def gate_up_bwd(dgt, dup, w_gate, w_up):
    tn = 512

    def body(dg_ref, du_ref, wg_ref, wu_ref, o_ref):
        def mm(r0):
            o_ref[pl.ds(r0, RC), :] = (
                lax.dot_general(dg_ref[pl.ds(r0, RC), :], wg_ref[...], NT_DIMS, preferred_element_type=F32)
                + lax.dot_general(du_ref[pl.ds(r0, RC), :], wu_ref[...], NT_DIMS, preferred_element_type=F32))

        _row_chunks(mm)

    res = _spec((T, D_FF), lambda j: (0, 0), single=True)
    wt = _spec((tn, D_FF), lambda j: (j, 0))
    return pl.pallas_call(
        body, grid=(D // tn,), in_specs=[res, res, wt, wt], out_specs=_spec((T, tn), lambda j: (0, j)),
        out_shape=jax.ShapeDtypeStruct((T, D), F32), compiler_params=_params(), name="gate_up_bwd")(dgt, dup, w_gate, w_up)


def norm_bwd(name, du, h, wn, dres, want_bf16):
    def body(*refs):
        if want_bf16:
            du_ref, h_ref, w_ref, r_ref, d_ref, db_ref, dw_ref = refs
        else:
            du_ref, h_ref, w_ref, r_ref, d_ref, dw_ref = refs

        @pl.when(pl.program_id(0) == 0)
        def _():
            dw_ref[...] = jnp.zeros_like(dw_ref)

        dh, dw = _rms_bwd(du_ref[...], h_ref[...], w_ref[...])
        dw_ref[...] += jnp.sum(dw, axis=0, keepdims=True)
        dh = dh + r_ref[...]
        d_ref[...] = dh
        if want_bf16:
            db_ref[...] = dh.astype(BF)

    tile = _spec((Q, D), lambda i: (i, 0))
    vec = _spec((1, D), lambda i: (0, 0))
    out_specs = [tile] + ([tile] if want_bf16 else []) + [vec]
    out_shape = ([jax.ShapeDtypeStruct((T, D), F32)] + ([jax.ShapeDtypeStruct((T, D), BF)] if want_bf16 else [])
                 + [jax.ShapeDtypeStruct((1, D), F32)])
    return pl.pallas_call(body, grid=(NCH,), in_specs=[tile, tile, vec, tile], out_specs=out_specs, out_shape=out_shape,
                          compiler_params=_params(), name=name)(du, h, wn, dres)


def _adamw(w, g, m, v):
    m = ADAM_B1 * m + (1.0 - ADAM_B1) * g
    v = ADAM_B2 * v + (1.0 - ADAM_B2) * (g * g)
    m_hat = m / (1.0 - ADAM_B1 ** ADAM_STEP)
    v_hat = v / (1.0 - ADAM_B2 ** ADAM_STEP)
    delta = -ADAM_LR * (m_hat / (jnp.sqrt(v_hat) + ADAM_EPS) + ADAM_WD * w)
    return delta, m, v


def adamw_shard(name, recv, w, m, v, tr):
    r, c = w.shape

    def body(p_ref, w_ref, m_ref, v_ref, g_ref, d_ref, mo_ref, vo_ref):
        g = p_ref[0].astype(F32)
        for s in range(1, 8):
            g = g + p_ref[s].astype(F32)
        g_ref[...] = g
        d_ref[...], mo_ref[...], vo_ref[...] = _adamw(w_ref[...], g, m_ref[...], v_ref[...])

    tile = _spec((tr, c), lambda i: (i, 0))
    shape = jax.ShapeDtypeStruct((r, c), F32)
    return pl.pallas_call(
        body, grid=(r // tr,), in_specs=[_spec((8, tr, c), lambda i: (0, i, 0)), tile, tile, tile],
        out_specs=[tile] * 4, out_shape=[shape] * 4, compiler_params=_params(), name=name)(recv, w, m, v)


def sum_slabs(recv):
    def body(p_ref, o_ref):
        g = p_ref[0]
        for s in range(1, 8):
            g = g + p_ref[s]
        o_ref[...] = g

    return pl.pallas_call(body, out_shape=jax.ShapeDtypeStruct(recv.shape[1:], F32), compiler_params=_params(0), name="sum_slabs")(recv)


SIMPLE = [("norm1_w", 1024), ("ssd_conv_b", 1536), ("ssd_dt_bias", 16), ("ssd_a_log", 16), ("ssd_d", 16), ("ssd_norm_w", 1024),
          ("lru_conv_b", 1024), ("lru_ba", 1024), ("lru_bx", 1024), ("lru_lambda", 1024), ("lru_norm_w", 1024), ("norm2_w", 1024),
          ("final_norm_w", 1024)]
SPECIAL = ["lru_wa", "lru_wx", "meta_tokens", "ssd_conv_w", "lru_conv_w"]
SM_ROWS = 176
SM_WA, SM_WX, SM_META, SM_SCW, SM_LCW, SM_LOSS = 14, 78, 142, 158, 166, 170


def _simple_rows():
    rows, r = {}, 0
    for name, n in SIMPLE:
        rows[name] = r
        r += -(-n // 1024)
    return rows


def adamw_small(sm, special_g, ws, ms, vs):
    rows = _simple_rows()
    ns, nx = len(SIMPLE), len(SPECIAL)

    def body(*refs):
        sm_ref = refs[0]
        gx = refs[1:1 + nx]
        wr = refs[1 + nx:1 + nx + ns + nx]
        mr = refs[1 + nx + ns + nx:1 + nx + 2 * (ns + nx)]
        vr = refs[1 + nx + 2 * (ns + nx):1 + nx + 3 * (ns + nx)]
        outs = refs[1 + nx + 3 * (ns + nx):]
        o = 0
        for k, (name, n) in enumerate(SIMPLE):
            r0 = rows[name]
            for c0 in range(0, n, 1024):
                wd = min(1024, n - c0)
                g = sm_ref[r0 + c0 // 1024:r0 + c0 // 1024 + 1, 0:wd]
                sl = (slice(None), slice(c0, c0 + wd))
                d, m2, v2 = _adamw(wr[k][sl], g, mr[k][sl], vr[k][sl])
                outs[o][sl] = g
                outs[o + 1][sl] = d
                outs[o + 2][sl] = m2
                outs[o + 3][sl] = v2
            o += 4
        for k in range(nx):
            d, m2, v2 = _adamw(wr[ns + k][...], gx[k][...], mr[ns + k][...], vr[ns + k][...])
            outs[o][...] = d
            outs[o + 1][...] = m2
            outs[o + 2][...] = v2
            o += 3

    out_shape = []
    for k in range(ns):
        out_shape += [jax.ShapeDtypeStruct(ws[k].shape, F32)] * 4
    for k in range(nx):
        out_shape += [jax.ShapeDtypeStruct(ws[ns + k].shape, F32)] * 3
    return pl.pallas_call(body, out_shape=out_shape, compiler_params=_params(0), name="adamw_small")(sm, *special_g, *ws, *ms, *vs)


def _place():
    return lax.axis_index("x"), lax.axis_index("y"), lax.axis_index("c")


def _index(px, py, pc):
    return 4 * px + 2 * py + pc


def all_gather(name, shards):
    n = len(shards)
    hbm = pl.BlockSpec(memory_space=pl.ANY)

    def body(*refs):
        ins, outs = refs[:n], refs[n:2 * n]
        send_sems, recv_sems, local_sems = refs[2 * n:]
        x, y, c = _place()
        me, sibling = (x, y, c), (x, y, 1 - c)
        chips = [(1 - x, y), (x, 1 - y), (1 - x, 1 - y)]

        def copy(i, k, block, to, src=None):
            dst = outs[i].at[_index(*block)]
            return pltpu.make_async_remote_copy(src_ref=dst if src is None else src, dst_ref=dst, send_sem=send_sems.at[7 * i + k],
                                                recv_sem=recv_sems.at[7 * i + k], device_id=to, device_id_type=MESH)

        mine = [pltpu.make_async_copy(ins[i], outs[i].at[_index(*me)], local_sems.at[i]) for i in range(n)]
        for cp in mine:
            cp.start()
        first = []
        for i in range(n):
            first += [copy(i, 1 + j, me, (*chip, c), src=ins[i]) for j, chip in enumerate(chips)]
            first.append(copy(i, 0, me, sibling, src=ins[i]))
        for cp in first:
            cp.start()
        passed = []
        for i in range(n):
            for j, chip in enumerate(chips):
                copy(i, 1 + j, (*chip, c), me).wait_recv()
                cp = copy(i, 4 + j, (*chip, c), sibling)
                cp.start()
                passed.append(cp)
        for i in range(n):
            copy(i, 0, sibling, me).wait_recv()
            for j, chip in enumerate(chips):
                copy(i, 4 + j, (*chip, 1 - c), me).wait_recv()
        for cp in first + passed:
            cp.wait_send()
        for cp in mine:
            cp.wait()

    return pl.pallas_call(
        body, in_specs=[hbm] * n, out_specs=[hbm] * n,
        out_shape=[jax.ShapeDtypeStruct((8,) + s.shape, s.dtype) for s in shards],
        scratch_shapes=[pltpu.SemaphoreType.DMA((7 * n,)), pltpu.SemaphoreType.DMA((7 * n,)), pltpu.SemaphoreType.DMA((n,))],
        name=name)(*shards)


def exchange_slabs(name, parts):
    n = len(parts)
    hbm = pl.BlockSpec(memory_space=pl.ANY)

    def body(*refs):
        ins, outs = refs[:n], refs[n:2 * n]
        send_sems, recv_sems, local_sems = refs[2 * n:]
        x, y, c = _place()
        mine = _index(x, y, c)
        peers = []
        for k in range(1, 8):
            peers.append(((1 - x) if k & 4 else x, (1 - y) if k & 2 else y, (1 - c) if k & 1 else c))

        def copy(i, k):
            peer = peers[k]
            return pltpu.make_async_remote_copy(src_ref=ins[i].at[_index(*peer)], dst_ref=outs[i].at[mine], send_sem=send_sems.at[7 * i + k],
                                                recv_sem=recv_sems.at[7 * i + k], device_id=peer, device_id_type=MESH)

        def arrival(i, k):
            peer = peers[k]
            return pltpu.make_async_remote_copy(src_ref=ins[i].at[mine], dst_ref=outs[i].at[_index(*peer)], send_sem=send_sems.at[7 * i + k],
                                                recv_sem=recv_sems.at[7 * i + k], device_id=peer, device_id_type=MESH)

        own = [pltpu.make_async_copy(ins[i].at[mine], outs[i].at[mine], local_sems.at[i]) for i in range(n)]
        sent = [copy(i, k) for i in range(n) for k in range(7)]
        for cp in sent + own:
            cp.start()
        for i in range(n):
            for k in range(7):
                arrival(i, k).wait_recv()
        for cp in sent:
            cp.wait_send()
        for cp in own:
            cp.wait()

    return pl.pallas_call(
        body, in_specs=[hbm] * n, out_specs=[hbm] * n,
        out_shape=[jax.ShapeDtypeStruct(p.shape, p.dtype) for p in parts],
        scratch_shapes=[pltpu.SemaphoreType.DMA((7 * n,)), pltpu.SemaphoreType.DMA((7 * n,)), pltpu.SemaphoreType.DMA((n,))],
        name=name)(*parts)


WEIGHTS = ["meta_tokens", "norm1_w", "w_in", "ssd_conv_w", "ssd_conv_b", "ssd_dt_bias", "ssd_a_log", "ssd_d", "ssd_norm_w", "lru_conv_w",
           "lru_conv_b", "lru_wa", "lru_ba", "lru_wx", "lru_bx", "lru_lambda", "lru_norm_w", "w_out", "norm2_w", "w_gate", "w_up", "w_down",
           "final_norm_w"]
BIG = ["w_in", "w_out", "w_gate", "w_up", "w_down"]
BIG_ROW_TILE = {"w_in": 256, "w_out": 128, "w_gate": 256, "w_up": 256, "w_down": 176}


def _pair_blocks(w):
    w = w.reshape(8, 2, 64, 64)
    z = jnp.zeros((8, 64, 64), w.dtype)
    return jnp.concatenate([jnp.concatenate([w[:, 0], z], axis=2), jnp.concatenate([z, w[:, 1]], axis=2)], axis=1)


def _unpair_blocks(w2):
    return jnp.stack([w2[:, :64, :64], w2[:, 64:, 64:]], axis=1).reshape(16, 64, 64)


def _per_group(v):
    return jnp.pad(v.reshape(2, 1, 8), ((0, 0), (0, 0), (0, 120)))


def _pad_cols(v, n):
    return jnp.pad(v, ((0, 0), (0, n - v.shape[1])))


def local_step(x, target, meta, ssd_cw, lru_cw, w_in, w_out, w_gate, w_up, w_down, p):
    z120 = jnp.zeros((D, 120), BF)
    w_p = jnp.concatenate([w_in[:, 0:1024], w_in[:, 2576:3600], w_in[:, 3600:4624], w_in[:, 1024:2560],
                           w_in[:, 2560:2568], z120, w_in[:, 2568:2576], z120, jnp.zeros((D, 256), BF)], axis=1)
    bias2, alog2, d2 = _per_group(p["ssd_dt_bias"]), _per_group(p["ssd_a_log"]), _per_group(p["ssd_d"])
    wa2 = _pair_blocks(p["lru_wa"]).astype(BF)
    wx2 = _pair_blocks(p["lru_wx"]).astype(BF)
    lru = (lru_cw, p["lru_conv_b"], wa2, p["lru_ba"], wx2, p["lru_bx"], p["lru_lambda"])

    h0 = jnp.concatenate([jnp.zeros((NPAD, D), F32), meta, x], axis=0)
    proj, u1 = norm_matmul("in_proj", h0, p["norm1_w"], w_p, 512)
    xbc_act = conv_silu_fwd(proj, ssd_cw, p["ssd_conv_b"])
    yn_ssd, y_pre, h_prev = ssd_fwd(xbc_act, proj, bias2, alog2, d2, p["ssd_norm_w"])
    a, u = lru_gates_fwd(proj, *lru)
    hseq = lru_scan_fwd(a, u)
    h1a = matmul("out_proj_ssd", yn_ssd, w_out, 512, resid=h0)
    h1, yn_lru = lru_out_proj(proj, hseq, p["lru_norm_w"], w_out, h1a)
    gt, up, act, u2 = gate_up(h1, p["norm2_w"], w_gate, w_up)
    h2 = matmul("down_proj", act, w_down, 512, resid=h1)
    dh2, dh2_b, loss, d_fnw = loss_bwd(h2, target, p["final_norm_w"])

    dgt, dup = swiglu_bwd(dh2_b, w_down, gt, up)
    du2 = gate_up_bwd(dgt, dup, w_gate, w_up)
    dh1, dh1_b, d_n2 = norm_bwd("norm2_bwd", du2, h1, p["norm2_w"], dh2, True)
    g_down = matmul_tn("dw_down", act, dh2_b, 1408, 512)
    g_gate = matmul_tn("dw_gate", u2, dgt, 512, 1408)
    g_up = matmul_tn("dw_up", u2, dup, 512, 1408)
    dcat = matmul("out_proj_bwd", dh1_b, w_out, 512, nt=True)
    g_out = jnp.concatenate([matmul_tn("dw_out_ssd", yn_ssd, dh1_b, 512, 1024), matmul_tn("dw_out_lru", yn_lru, dh1_b, 512, 1024)], axis=0)

    dh_out, dg_b, d_lnw = lru_norm_bwd(dcat, proj, hseq, p["lru_norm_w"])
    dhs = lru_scan_bwd(a, dh_out)
    dxl_b, d_lcw, d_lcb, dwa2, d_ba, dwx2, d_bx, d_lam = lru_gates_bwd(dhs, hseq, proj, *lru)
    dz_b, dx, d_b, d_c, ddt_b, dpar, d_snw = ssd_bwd(dcat, xbc_act, proj, y_pre, h_prev, bias2, alog2, d2, p["ssd_norm_w"])
    dxbc_b, d_scw, d_scb = conv_silu_bwd(jnp.concatenate([dx, d_b, d_c], axis=1), proj, ssd_cw, p["ssd_conv_b"])
    dproj = jnp.concatenate([dz_b, dg_b, dxl_b, dxbc_b, ddt_b, jnp.zeros((T, 256), BF)], axis=1)
    du1 = matmul("in_proj_bwd", dproj, w_p, 512, nt=True)
    dh0, d_n1 = norm_bwd("norm1_bwd", du1, h0, p["norm1_w"], dh1, False)
    g_p = matmul_tn("dw_in", u1, dproj, 512, 1024)
    g_in = jnp.concatenate([g_p[:, 0:1024], g_p[:, PXBC:PXBC + XBC], g_p[:, PDT:PDT + 8], g_p[:, PDT + 128:PDT + 136],
                            g_p[:, PG:PG + 1024], g_p[:, PXL:PXL + 1024]], axis=1)
    big = {"w_in": g_in, "w_out": g_out, "w_gate": g_gate, "w_up": g_up, "w_down": g_down}
    small = {"norm1_w": d_n1, "ssd_conv_b": d_scb, "ssd_dt_bias": dpar[:, 0, :8].reshape(1, 16), "ssd_a_log": dpar[:, 1, :8].reshape(1, 16),
             "ssd_d": dpar[:, 2, :8].reshape(1, 16), "ssd_norm_w": d_snw, "lru_conv_b": d_lcb, "lru_ba": d_ba, "lru_bx": d_bx,
             "lru_lambda": d_lam, "lru_norm_w": d_lnw, "norm2_w": d_n2, "final_norm_w": d_fnw,
             "lru_wa": _unpair_blocks(dwa2), "lru_wx": _unpair_blocks(dwx2), "meta_tokens": dh0[NPAD:NPAD + N_META],
             "ssd_conv_w": d_scw, "lru_conv_w": d_lcw}
    return loss, dh0[NPAD + N_META:], big, small


def _pack_small(small, loss):
    rows = [_pad_cols(small[name], -(-n // 1024) * 1024).reshape(-1, 1024) for name, n in SIMPLE]
    rows += [small["lru_wa"].reshape(64, 1024), small["lru_wx"].reshape(64, 1024), small["meta_tokens"],
             _pad_cols(small["ssd_conv_w"], 2048).reshape(8, 1024), small["lru_conv_w"], _pad_cols(loss[:, 0:1], 1024)]
    sm = jnp.concatenate(rows, axis=0)
    return jnp.pad(sm, ((0, SM_ROWS - sm.shape[0]), (0, 0)))


def _slabs(g, name):
    if name in ("w_in", "w_gate", "w_up"):
        return g.reshape(g.shape[0], 8, g.shape[1] // 8).transpose(1, 0, 2)
    return g.reshape(8, g.shape[0] // 8, g.shape[1])


def _unslab(g, name):
    if name in ("w_in", "w_gate", "w_up"):
        return g.transpose(1, 0, 2).reshape(g.shape[1], 8 * g.shape[2])
    return g.reshape(8 * g.shape[1], g.shape[2])


def kernel(x, meta_tokens, norm1_w, w_in, ssd_conv_w, ssd_conv_b, ssd_dt_bias, ssd_a_log, ssd_d, ssd_norm_w, lru_conv_w, lru_conv_b, lru_wa, lru_ba, lru_wx, lru_bx, lru_lambda, lru_norm_w, w_out, norm2_w, w_gate, w_up, w_down, final_norm_w, loss_target, m_meta_tokens, m_norm1_w, m_w_in, m_ssd_conv_w, m_ssd_conv_b, m_ssd_dt_bias, m_ssd_a_log, m_ssd_d, m_ssd_norm_w, m_lru_conv_w, m_lru_conv_b, m_lru_wa, m_lru_ba, m_lru_wx, m_lru_bx, m_lru_lambda, m_lru_norm_w, m_w_out, m_norm2_w, m_w_gate, m_w_up, m_w_down, m_final_norm_w, v_meta_tokens, v_norm1_w, v_w_in, v_ssd_conv_w, v_ssd_conv_b, v_ssd_dt_bias, v_ssd_a_log, v_ssd_d, v_ssd_norm_w, v_lru_conv_w, v_lru_conv_b, v_lru_wa, v_lru_ba, v_lru_wx, v_lru_bx, v_lru_lambda, v_lru_norm_w, v_w_out, v_norm2_w, v_w_gate, v_w_up, v_w_down, v_final_norm_w):
    w = dict(meta_tokens=meta_tokens, norm1_w=norm1_w, w_in=w_in[0], ssd_conv_w=ssd_conv_w[0], ssd_conv_b=ssd_conv_b, ssd_dt_bias=ssd_dt_bias,
             ssd_a_log=ssd_a_log, ssd_d=ssd_d, ssd_norm_w=ssd_norm_w, lru_conv_w=lru_conv_w[0], lru_conv_b=lru_conv_b, lru_wa=lru_wa[0],
             lru_ba=lru_ba, lru_wx=lru_wx[0], lru_bx=lru_bx, lru_lambda=lru_lambda, lru_norm_w=lru_norm_w, w_out=w_out[0], norm2_w=norm2_w,
             w_gate=w_gate[0], w_up=w_up[0], w_down=w_down[0], final_norm_w=final_norm_w.reshape(1, D))
    m = dict(meta_tokens=m_meta_tokens, norm1_w=m_norm1_w, w_in=m_w_in[0], ssd_conv_w=m_ssd_conv_w[0], ssd_conv_b=m_ssd_conv_b,
             ssd_dt_bias=m_ssd_dt_bias, ssd_a_log=m_ssd_a_log, ssd_d=m_ssd_d, ssd_norm_w=m_ssd_norm_w, lru_conv_w=m_lru_conv_w[0],
             lru_conv_b=m_lru_conv_b, lru_wa=m_lru_wa[0], lru_ba=m_lru_ba, lru_wx=m_lru_wx[0], lru_bx=m_lru_bx, lru_lambda=m_lru_lambda,
             lru_norm_w=m_lru_norm_w, w_out=m_w_out[0], norm2_w=m_norm2_w, w_gate=m_w_gate[0], w_up=m_w_up[0], w_down=m_w_down[0],
             final_norm_w=m_final_norm_w.reshape(1, D))
    v = dict(meta_tokens=v_meta_tokens, norm1_w=v_norm1_w, w_in=v_w_in[0], ssd_conv_w=v_ssd_conv_w[0], ssd_conv_b=v_ssd_conv_b,
             ssd_dt_bias=v_ssd_dt_bias, ssd_a_log=v_ssd_a_log, ssd_d=v_ssd_d, ssd_norm_w=v_ssd_norm_w, lru_conv_w=v_lru_conv_w[0],
             lru_conv_b=v_lru_conv_b, lru_wa=v_lru_wa[0], lru_ba=v_lru_ba, lru_wx=v_lru_wx[0], lru_bx=v_lru_bx, lru_lambda=v_lru_lambda,
             lru_norm_w=v_lru_norm_w, w_out=v_w_out[0], norm2_w=v_norm2_w, w_gate=v_w_gate[0], w_up=v_w_up[0], w_down=v_w_down[0],
             final_norm_w=v_final_norm_w.reshape(1, D))
    shapes = dict(meta_tokens=meta_tokens.shape, norm1_w=norm1_w.shape, w_in=w_in.shape, ssd_conv_w=ssd_conv_w.shape,
                  ssd_conv_b=ssd_conv_b.shape, ssd_dt_bias=ssd_dt_bias.shape, ssd_a_log=ssd_a_log.shape, ssd_d=ssd_d.shape,
                  ssd_norm_w=ssd_norm_w.shape, lru_conv_w=lru_conv_w.shape, lru_conv_b=lru_conv_b.shape, lru_wa=lru_wa.shape,
                  lru_ba=lru_ba.shape, lru_wx=lru_wx.shape, lru_bx=lru_bx.shape, lru_lambda=lru_lambda.shape, lru_norm_w=lru_norm_w.shape,
                  w_out=w_out.shape, norm2_w=norm2_w.shape, w_gate=w_gate.shape, w_up=w_up.shape, w_down=w_down.shape,
                  final_norm_w=final_norm_w.shape)
    me = _index(*_place())

    small_shard = jnp.concatenate([w["meta_tokens"], _pad_cols(w["ssd_conv_w"], 256).reshape(8, 128), w["lru_conv_w"],
                                   jnp.zeros((4, 128), F32)], axis=0)
    gathered = all_gather("gather_weights", [w[n].astype(BF) for n in BIG] + [small_shard])
    full = {n: _unslab(g, n) for n, g in zip(BIG, gathered)}
    gs = gathered[5]
    meta_full = gs[:, 0:16].transpose(1, 0, 2).reshape(N_META, D)
    ssd_cw = gs[:, 16:24].reshape(8, 4, 256)[:, :, :192].transpose(1, 0, 2).reshape(4, XBC)
    lru_cw = gs[:, 24:28].transpose(1, 0, 2).reshape(4, LRU_W)

    loss, grad_x, big, small = local_step(x[0], loss_target[0], meta_full, ssd_cw, lru_cw, full["w_in"], full["w_out"], full["w_gate"],
                                          full["w_up"], full["w_down"], w)

    parts = [_slabs(big[n], n) for n in BIG] + [_pack_small(small, loss).reshape(8, SM_ROWS // 8, 1024)]
    recv = exchange_slabs("exchange_grads", parts)
    out = {}
    for n, r in zip(BIG, recv):
        out[n] = adamw_shard("adamw_" + n, r, w[n], m[n], v[n], BIG_ROW_TILE[n])
    sm = all_gather("gather_small_grads", [sum_slabs(recv[5])])[0].reshape(SM_ROWS, 1024)
    special_g = [sm[SM_WA:SM_WA + 64].reshape(16, 64, 64), sm[SM_WX:SM_WX + 64].reshape(16, 64, 64),
                 lax.dynamic_slice(sm[SM_META:SM_META + 16], (0, 128 * me), (16, 128)),
                 lax.dynamic_slice(sm[SM_SCW:SM_SCW + 8].reshape(4, 2048), (0, 192 * me), (4, 192)),
                 lax.dynamic_slice(sm[SM_LCW:SM_LCW + 4], (0, 128 * me), (4, 128))]
    names = [n for n, _ in SIMPLE] + SPECIAL
    res = adamw_small(sm, special_g, [w[n] for n in names], [m[n] for n in names], [v[n] for n in names])
    for k, (n, _) in enumerate(SIMPLE):
        out[n] = res[4 * k:4 * k + 4]
    for k, n in enumerate(SPECIAL):
        o = 4 * len(SIMPLE) + 3 * k
        out[n] = [special_g[k]] + list(res[o:o + 3])
    loss_total = sm[SM_LOSS, 0]
    flat = [loss_total, grad_x[None]]
    for k in range(4):
        flat += [out[n][k].reshape(shapes[n]) for n in WEIGHTS]
    return tuple(flat)
```

```python
import math

import jax
import jax.numpy as jnp
from jax import lax
from jax.experimental import pallas as pl
from jax.experimental.pallas import tpu as pltpu

F32 = jnp.float32
BF = jnp.bfloat16

D = 1024
SEQ = 2048
N_META = 16
Q = 128
NPAD = 112
T = NPAD + N_META + SEQ
NCH = T // Q
RC = 544
D_FF = 2816
SSD_W = 1024
LRU_W = 1024
XBC = 1536
IN_COLS = 4624
PZ, PG, PXL, PXBC, PDT = 0, 1024, 2048, 3072, 4608
NP_IN = 5120
EPS = 1e-6
LRU_C = 8.0
VMEM_LIMIT = 56 * 1024 * 1024

ADAM_LR, ADAM_B1, ADAM_B2, ADAM_EPS, ADAM_WD, ADAM_STEP = 0.001, 0.9, 0.999, 1e-08, 0.01, 10

NT_DIMS = (((1,), (1,)), ((), ()))
TN_DIMS = (((0,), (0,)), ((), ()))
MESH = pl.DeviceIdType.MESH


def _params(n_grid=1, limit=VMEM_LIMIT):
    return pltpu.CompilerParams(dimension_semantics=("arbitrary",) * n_grid, vmem_limit_bytes=limit)


def _spec(shape, imap, single=False):
    if single:
        return pl.BlockSpec(shape, imap, pipeline_mode=pl.Buffered(1))
    return pl.BlockSpec(shape, imap)


def _sigmoid(x):
    return 1.0 / (1.0 + jnp.exp(-x))


def _softplus(x):
    return jnp.maximum(x, 0.0) + jnp.log(1.0 + jnp.exp(-jnp.abs(x)))


def _rms_stats(h):
    return lax.rsqrt(jnp.mean(h * h, axis=-1, keepdims=True) + EPS)


def _rms(h, w):
    return (h * _rms_stats(h)) * w


def _rms_bwd(du, h, w):
    r = _rms_stats(h)
    n = h * r
    dn = du * w
    dh = r * (dn - n * jnp.mean(dn * n, axis=-1, keepdims=True))
    return dh, du * n


_G0 = math.sqrt(2.0 / math.pi)


def _gelu(x):
    return 0.5 * x * (1.0 + jnp.tanh(_G0 * (x + 0.044715 * (x * x * x))))


def _gelu_grad(x):
    t = jnp.tanh(_G0 * (x + 0.044715 * (x * x * x)))
    return 0.5 * (1.0 + t) + 0.5 * x * (1.0 - t * t) * (_G0 * (1.0 + 3.0 * 0.044715 * (x * x)))


def _rows(shape, r0=0):
    return lax.broadcasted_iota(jnp.int32, shape, 0) + r0


def _lanes(shape):
    return lax.broadcasted_iota(jnp.int32, shape, 1)


def _shift_down(x, s):
    if s == 0:
        return x
    return jnp.where(_rows(x.shape) >= s, pltpu.roll(x, s, axis=0), 0.0)


def _shift_up(x, s):
    if s == 0:
        return x
    n = x.shape[0]
    return jnp.where(_rows(x.shape) < n - s, pltpu.roll(x, n - s, axis=0), 0.0)


def _conv(x, w, b):
    y = b + w[3:4, :] * x
    for k in range(3):
        y = y + w[k:k + 1, :] * _shift_down(x, 3 - k)
    return y


def _conv_bwd(dy, x, w):
    dx = w[3:4, :] * dy
    dws = []
    for k in range(3):
        dx = dx + w[k:k + 1, :] * _shift_up(dy, 3 - k)
        dws.append(jnp.sum(dy * _shift_down(x, 3 - k), axis=0, keepdims=True))
    dws.append(jnp.sum(dy * x, axis=0, keepdims=True))
    return dx, jnp.concatenate(dws, axis=0), jnp.sum(dy, axis=0, keepdims=True)


def _row_chunks(fn):
    def step(m, carry):
        fn(pl.multiple_of(m * RC, RC))
        return carry

    lax.fori_loop(0, T // RC, step, 0)


def _q_chunks(fn):
    def step(m, carry):
        fn(pl.multiple_of(m * Q, Q))
        return carry

    lax.fori_loop(0, NCH, step, 0)


def norm_matmul(name, h, wn, w, tn, resid=None):
    k, n = w.shape

    def body(*refs):
        if resid is None:
            h_ref, wn_ref, w_ref, o_ref, u_ref = refs
        else:
            h_ref, wn_ref, w_ref, r_ref, o_ref, u_ref = refs

        @pl.when(pl.program_id(0) == 0)
        def _():
            def norm(r0):
                u_ref[pl.ds(r0, Q), :] = _rms(h_ref[pl.ds(r0, Q), :], wn_ref[...]).astype(BF)

            _q_chunks(norm)

        def mm(r0):
            acc = jnp.dot(u_ref[pl.ds(r0, RC), :], w_ref[...], preferred_element_type=F32)
            if resid is not None:
                acc = acc + r_ref[pl.ds(r0, RC), :]
            o_ref[pl.ds(r0, RC), :] = acc

        _row_chunks(mm)

    in_specs = [_spec((T, k), lambda j: (0, 0), single=True), _spec((1, k), lambda j: (0, 0)), _spec((k, tn), lambda j: (0, j))]
    args = [h, wn, w]
    if resid is not None:
        in_specs.append(_spec((T, tn), lambda j: (0, j)))
        args.append(resid)
    return pl.pallas_call(
        body, grid=(n // tn,), in_specs=in_specs,
        out_specs=[_spec((T, tn), lambda j: (0, j)), _spec((T, k), lambda j: (0, 0))],
        out_shape=[jax.ShapeDtypeStruct((T, n), F32), jax.ShapeDtypeStruct((T, k), BF)],
        compiler_params=_params(), name=name)(*args)


def matmul(name, a, w, tn, *, nt=False, resid=None, w_row0=0):
    k = a.shape[1]
    n = w.shape[0] if nt else w.shape[1]

    def body(*refs):
        if resid is None:
            a_ref, w_ref, o_ref = refs
        else:
            a_ref, w_ref, r_ref, o_ref = refs

        def mm(r0):
            if nt:
                acc = lax.dot_general(a_ref[pl.ds(r0, RC), :], w_ref[...], NT_DIMS, preferred_element_type=F32)
            else:
                acc = jnp.dot(a_ref[pl.ds(r0, RC), :], w_ref[...], preferred_element_type=F32)
            if resid is not None:
                acc = acc + r_ref[pl.ds(r0, RC), :]
            o_ref[pl.ds(r0, RC), :] = acc

        _row_chunks(mm)

    w_spec = _spec((tn, k), lambda j: (j, 0)) if nt else _spec((k, tn), lambda j: (w_row0, j))
    in_specs = [_spec((T, k), lambda j: (0, 0), single=True), w_spec]
    args = [a, w]
    if resid is not None:
        in_specs.append(_spec((T, tn), lambda j: (0, j)))
        args.append(resid)
    return pl.pallas_call(
        body, grid=(n // tn,), in_specs=in_specs, out_specs=_spec((T, tn), lambda j: (0, j)),
        out_shape=jax.ShapeDtypeStruct((T, n), F32), compiler_params=_params(), name=name)(*args)


def matmul_tn(name, a, b, tm, tn):
    m, n = a.shape[1], b.shape[1]

    def body(a_ref, b_ref, o_ref, acc_ref):
        acc_ref[...] = jnp.zeros_like(acc_ref)

        def mm(r0):
            acc_ref[...] += lax.dot_general(a_ref[pl.ds(r0, RC), :], b_ref[pl.ds(r0, RC), :], TN_DIMS, preferred_element_type=F32)

        _row_chunks(mm)
        o_ref[...] = acc_ref[...].astype(BF)

    return pl.pallas_call(
        body, grid=(m // tm, n // tn),
        in_specs=[_spec((T, tm), lambda i, j: (0, i)), _spec((T, tn), lambda i, j: (0, j))],
        out_specs=_spec((tm, tn), lambda i, j: (i, j)),
        out_shape=jax.ShapeDtypeStruct((m, n), BF),
        scratch_shapes=[pltpu.VMEM((tm, tn), F32)],
        compiler_params=_params(2), name=name)(a, b)


def conv_silu_fwd(proj, cw, cb):
    def body(x_ref, w_ref, b_ref, o_ref):
        pre = _conv(x_ref[...], w_ref[...], b_ref[...])
        o_ref[...] = pre * _sigmoid(pre)

    c0 = PXBC // 128
    return pl.pallas_call(
        body, grid=(XBC // 128,),
        in_specs=[_spec((T, 128), lambda c: (0, c0 + c)), _spec((4, 128), lambda c: (0, c)), _spec((1, 128), lambda c: (0, c))],
        out_specs=_spec((T, 128), lambda c: (0, c)),
        out_shape=jax.ShapeDtypeStruct((T, XBC), F32), compiler_params=_params(), name="conv_silu_fwd")(proj, cw, cb)


def conv_silu_bwd(dact, proj, cw, cb):
    def body(d_ref, x_ref, w_ref, b_ref, dx_ref, dw_ref, db_ref):
        x = x_ref[...]
        pre = _conv(x, w_ref[...], b_ref[...])
        sg = _sigmoid(pre)
        dpre = d_ref[...] * (sg * (1.0 + pre * (1.0 - sg)))
        dx, dw, db = _conv_bwd(dpre, x, w_ref[...])
        dx_ref[...] = dx.astype(BF)
        dw_ref[...] = dw
        db_ref[...] = db

    c0 = PXBC // 128
    return pl.pallas_call(
        body, grid=(XBC // 128,),
        in_specs=[_spec((T, 128), lambda c: (0, c)), _spec((T, 128), lambda c: (0, c0 + c)),
                  _spec((4, 128), lambda c: (0, c)), _spec((1, 128), lambda c: (0, c))],
        out_specs=[_spec((T, 128), lambda c: (0, c)), _spec((4, 128), lambda c: (0, c)), _spec((1, 128), lambda c: (0, c))],
        out_shape=[jax.ShapeDtypeStruct((T, XBC), BF), jax.ShapeDtypeStruct((4, XBC), F32), jax.ShapeDtypeStruct((1, XBC), F32)],
        compiler_params=_params(), name="conv_silu_bwd")(dact, proj, cw, cb)


def _ssd_chunk_common(row0, dt_ref, b_ref, c_ref, bias, a_neg):
    shape = (Q, Q)
    lane = _lanes(shape)
    sub = _rows(shape)
    live = (_rows(shape, row0) >= NPAD) & (lane < 8)
    dtr = dt_ref[:, :]
    dt = jnp.where(live, _softplus(dtr + bias), 0.0)
    d_a = dt * a_neg
    tri = (sub >= lane).astype(F32)
    cs = jnp.dot(tri, d_a, precision=lax.Precision.HIGHEST, preferred_element_type=F32)
    cs_t = cs.T
    bc = b_ref[:, :].astype(BF)
    cc = c_ref[:, :].astype(BF)
    cb = lax.dot_general(cc, bc, NT_DIMS, preferred_element_type=F32)
    cs_last = cs[Q - 1:Q, :]
    return dict(lane=lane, sub=sub, live=live, dtr=dtr, dt=dt, cs=cs, cs_t=cs_t, bc=bc, cc=cc, cb=cb,
                ecs=jnp.exp(cs), dsm=jnp.exp(cs_last - cs), gam=jnp.exp(cs_last), tri=tri)


def _pair(lane_even, mat, j):
    return jnp.where(lane_even, mat[:, j:j + 1], mat[:, j + 1:j + 2])


def _head_decay(cm, j):
    seg = cm["cs"][:, j:j + 1] - cm["cs_t"][j:j + 1, :]
    return jnp.exp(jnp.where(cm["sub"] >= cm["lane"], seg, -jnp.inf))


def ssd_fwd(xbc_act, proj, dt_bias2, a_log2, d2, norm_w):
    def body(x_ref, b_ref, c_ref, dt_ref, z_ref, bias_ref, alog_ref, d_ref, nw_ref, yn_ref, y_ref, hp_ref, h_scr):
        c = pl.program_id(1)

        @pl.when(c == 0)
        def _():
            h_scr[...] = jnp.zeros_like(h_scr)

        bias = bias_ref[0]
        a_neg = -jnp.exp(alog_ref[0])
        dsk = d_ref[0]
        cm = _ssd_chunk_common(c * Q, dt_ref, b_ref, c_ref, bias, a_neg)
        lane_even = cm["lane"] < 64
        sub_even = cm["sub"] < 64
        for p in range(4):
            je, jo = 2 * p, 2 * p + 1
            xp = x_ref[:, 128 * p:128 * p + 128]
            xdt = xp * _pair(lane_even, cm["dt"], je)
            xdt_b = xdt.astype(BF)
            m_e = (cm["cb"] * _head_decay(cm, je)).astype(BF)
            m_o = (cm["cb"] * _head_decay(cm, jo)).astype(BF)
            zero = jnp.zeros_like(xdt_b)
            yd = (jnp.dot(m_e, jnp.where(lane_even, xdt_b, zero), preferred_element_type=F32)
                  + jnp.dot(m_o, jnp.where(lane_even, zero, xdt_b), preferred_element_type=F32))
            hp = h_scr[p]
            hp_ref[0, 0, p] = hp
            yo = lax.dot_general(cm["cc"], hp.astype(BF), NT_DIMS, preferred_element_type=F32) * _pair(lane_even, cm["ecs"], je)
            dsk_p = jnp.where(lane_even[0:1, :], dsk[:, je:je + 1], dsk[:, jo:jo + 1])
            y_ref[:, 128 * p:128 * p + 128] = yd + yo + xp * dsk_p
            st = lax.dot_general((xdt * _pair(lane_even, cm["dsm"], je)).astype(BF), cm["bc"], TN_DIMS, preferred_element_type=F32)
            gam = jnp.where(sub_even[:, 0:1], cm["gam"][:, je:je + 1], cm["gam"][:, jo:jo + 1])
            h_scr[p] = hp * gam + st
        zc = z_ref[:, :]
        gated = y_ref[:, :] * (zc * _sigmoid(zc))
        yn_ref[:, :] = _rms(gated, nw_ref[...]).astype(BF)

    par = _spec((1, 1, 128), lambda g, c: (g, 0, 0))
    wide = _spec((Q, 512), lambda g, c: (c, g))
    return pl.pallas_call(
        body, grid=(2, NCH),
        in_specs=[wide, _spec((Q, 128), lambda g, c: (c, 8 + g)), _spec((Q, 128), lambda g, c: (c, 10 + g)),
                  _spec((Q, 128), lambda g, c: (c, PDT // 128 + g)), wide, par, par, par, _spec((1, 512), lambda g, c: (0, g))],
        out_specs=[wide, wide, _spec((1, 1, 4, 128, 128), lambda g, c: (g, c, 0, 0, 0))],
        out_shape=[jax.ShapeDtypeStruct((T, SSD_W), BF), jax.ShapeDtypeStruct((T, SSD_W), F32),
                   jax.ShapeDtypeStruct((2, NCH, 4, 128, 128), F32)],
        scratch_shapes=[pltpu.VMEM((4, 128, 128), F32)],
        compiler_params=_params(2), name="ssd_fwd")(xbc_act, xbc_act, xbc_act, proj, proj, dt_bias2, a_log2, d2, norm_w)


def ssd_bwd(dyn, xbc_act, proj, y_pre, h_prev, dt_bias2, a_log2, d2, norm_w):
    def body(dyn_ref, x_ref, b_ref, c_ref, dt_ref, z_ref, y_ref, hp_ref, bias_ref, alog_ref, d_ref, nw_ref,
             dz_ref, dx_ref, db_ref, dc_ref, ddt_ref, dpar_ref, dnw_ref, dh_scr, acc_scr):
        ci = pl.program_id(1)

        @pl.when(ci == 0)
        def _():
            dh_scr[...] = jnp.zeros_like(dh_scr)
            acc_scr[...] = jnp.zeros_like(acc_scr)
            dnw_ref[...] = jnp.zeros_like(dnw_ref)

        bias = bias_ref[0]
        a_neg = -jnp.exp(alog_ref[0])
        dsk = d_ref[0]
        cm = _ssd_chunk_common((NCH - 1 - ci) * Q, dt_ref, b_ref, c_ref, bias, a_neg)
        lane, sub = cm["lane"], cm["sub"]
        lane_even = lane < 64
        sub_even = sub < 64
        zc = z_ref[:, :]
        yc = y_ref[:, :]
        sg = _sigmoid(zc)
        sz = zc * sg
        dgated, dnw = _rms_bwd(dyn_ref[:, :], yc * sz, nw_ref[...])
        dnw_ref[...] += jnp.sum(dnw, axis=0, keepdims=True)
        dz_ref[:, :] = (dgated * yc * (sg * (1.0 + zc * (1.0 - sg)))).astype(BF)
        dy_all = dgated * sz
        dcb = jnp.zeros((Q, Q), F32)
        db_acc = jnp.zeros((Q, Q), F32)
        dc_acc = jnp.zeros((Q, Q), F32)
        dcs_col = jnp.zeros((Q, Q), F32)
        dcs_row = jnp.zeros((Q, Q), F32)
        ddt = jnp.zeros((Q, Q), F32)
        for p in range(4):
            je, jo = 2 * p, 2 * p + 1
            xp = x_ref[:, 128 * p:128 * p + 128]
            dy = dy_all[:, 128 * p:128 * p + 128]
            dt_p = _pair(lane_even, cm["dt"], je)
            xdt = xp * dt_p
            xdt_b = xdt.astype(BF)
            dy_b = dy.astype(BF)
            zero = jnp.zeros_like(dy_b)
            hp = hp_ref[0, 0, p]
            hp_b = hp.astype(BF)
            dh = dh_scr[p]
            dh_b = dh.astype(BF)
            acc_scr[p:p + 1, :] += jnp.sum(dy * xp, axis=0, keepdims=True)
            dsk_p = jnp.where(lane_even[0:1, :], dsk[:, je:je + 1], dsk[:, jo:jo + 1])
            dxp = dy * dsk_p
            e_p = _pair(lane_even, cm["ecs"], je)
            g_p = lax.dot_general(cm["cc"], hp_b, NT_DIMS, preferred_element_type=F32)
            dg_b = (dy * e_p).astype(BF)
            de = dy * g_p * e_p
            dc_acc = dc_acc + jnp.dot(dg_b, hp_b, preferred_element_type=F32)
            dh_in = lax.dot_general(dg_b, cm["cc"], TN_DIMS, preferred_element_type=F32)
            ds_p = _pair(lane_even, cm["dsm"], je)
            r_p = lax.dot_general(cm["bc"], dh_b, NT_DIMS, preferred_element_type=F32)
            dxdt = r_p * ds_p
            tt = r_p * xdt * ds_p
            db_acc = db_acc + jnp.dot((xdt * ds_p).astype(BF), dh_b, preferred_element_type=F32)
            dgam_m = dh * hp
            for j, even in ((je, True), (jo, False)):
                sel = lane_even if even else jnp.logical_not(lane_even)
                ssel = sub_even if even else jnp.logical_not(sub_even)
                l_j = _head_decay(cm, j)
                m_j = cm["cb"] * l_j
                dm = lax.dot_general(jnp.where(sel, dy_b, zero), xdt_b, NT_DIMS, preferred_element_type=F32)
                dxdt = dxdt + lax.dot_general(m_j.astype(BF), jnp.where(sel, dy_b, zero), TN_DIMS, preferred_element_type=F32)
                w_j = dm * m_j
                dcb = dcb + dm * l_j
                t_j = jnp.sum(jnp.where(sel, tt, 0.0), axis=1, keepdims=True)
                col = (jnp.sum(w_j, axis=1, keepdims=True) + jnp.sum(jnp.where(sel, de, 0.0), axis=1, keepdims=True) - t_j)
                gam_j = cm["gam"][:, j:j + 1]
                last = (jnp.sum(t_j, axis=0, keepdims=True)
                        + jnp.sum(jnp.sum(jnp.where(ssel, dgam_m, 0.0), axis=1, keepdims=True), axis=0, keepdims=True) * gam_j)
                col = col + jnp.where(sub[:, 0:1] == Q - 1, last, 0.0)
                dcs_col = dcs_col + jnp.where(lane == j, col, 0.0)
                dcs_row = dcs_row + jnp.where(sub == j, jnp.sum(w_j, axis=0, keepdims=True), 0.0)
            gam = jnp.where(sub_even[:, 0:1], cm["gam"][:, je:je + 1], cm["gam"][:, jo:jo + 1])
            dh_scr[p] = dh_in + dh * gam
            dx_ref[:, 128 * p:128 * p + 128] = dxp + dxdt * dt_p
            dd = dxdt * xp
            ddt = ddt + jnp.where(lane == je, jnp.sum(jnp.where(lane_even, dd, 0.0), axis=1, keepdims=True), 0.0)
            ddt = ddt + jnp.where(lane == jo, jnp.sum(jnp.where(lane_even, 0.0, dd), axis=1, keepdims=True), 0.0)
        dcb_b = dcb.astype(BF)
        dc_ref[:, :] = dc_acc + jnp.dot(dcb_b, cm["bc"], preferred_element_type=F32)
        db_ref[:, :] = db_acc + lax.dot_general(dcb_b, cm["cc"], TN_DIMS, preferred_element_type=F32)
        dcs = dcs_col - dcs_row.T
        dd_a = lax.dot_general(cm["tri"], dcs, TN_DIMS, precision=lax.Precision.HIGHEST, preferred_element_type=F32)
        ddt = ddt + dd_a * a_neg
        acc_scr[5:6, :] += jnp.sum(dd_a * cm["dt"], axis=0, keepdims=True)
        draw = jnp.where(cm["live"], ddt * _sigmoid(cm["dtr"] + bias), 0.0)
        acc_scr[4:5, :] += jnp.sum(draw, axis=0, keepdims=True)
        ddt_ref[:, :] = draw.astype(BF)

        @pl.when(ci == NCH - 1)
        def _():
            lane1 = _lanes((1, 128))
            dd = jnp.zeros((1, 128), F32)
            for p in range(4):
                row = acc_scr[p:p + 1, :]
                dd = dd + jnp.where(lane1 == 2 * p, jnp.sum(jnp.where(lane1 < 64, row, 0.0), axis=1, keepdims=True), 0.0)
                dd = dd + jnp.where(lane1 == 2 * p + 1, jnp.sum(jnp.where(lane1 < 64, 0.0, row), axis=1, keepdims=True), 0.0)
            dpar_ref[0] = jnp.concatenate([acc_scr[4:5, :], acc_scr[5:6, :] * a_neg, dd, jnp.zeros((5, 128), F32)], axis=0)

    par = _spec((1, 1, 128), lambda g, c: (g, 0, 0))
    wide = _spec((Q, 512), lambda g, c: (NCH - 1 - c, g))
    thin = _spec((Q, 128), lambda g, c: (NCH - 1 - c, g))
    return pl.pallas_call(
        body, grid=(2, NCH),
        in_specs=[wide, wide, _spec((Q, 128), lambda g, c: (NCH - 1 - c, 8 + g)), _spec((Q, 128), lambda g, c: (NCH - 1 - c, 10 + g)),
                  _spec((Q, 128), lambda g, c: (NCH - 1 - c, PDT // 128 + g)), wide, wide,
                  _spec((1, 1, 4, 128, 128), lambda g, c: (g, NCH - 1 - c, 0, 0, 0)), par, par, par, _spec((1, 512), lambda g, c: (0, g))],
        out_specs=[wide, wide, thin, thin, thin, _spec((1, 8, 128), lambda g, c: (g, 0, 0)), _spec((1, 512), lambda g, c: (0, g))],
        out_shape=[jax.ShapeDtypeStruct((T, SSD_W), BF), jax.ShapeDtypeStruct((T, SSD_W), F32), jax.ShapeDtypeStruct((T, 256), F32),
                   jax.ShapeDtypeStruct((T, 256), F32), jax.ShapeDtypeStruct((T, 256), BF), jax.ShapeDtypeStruct((2, 8, 128), F32),
                   jax.ShapeDtypeStruct((1, SSD_W), F32)],
        scratch_shapes=[pltpu.VMEM((4, 128, 128), F32), pltpu.VMEM((8, 128), F32)],
        compiler_params=_params(2), name="ssd_bwd")(dyn, xbc_act, xbc_act, xbc_act, proj, proj, y_pre, h_prev, dt_bias2, a_log2, d2, norm_w)


def _lru_gates(xl, cw, cb, wa, ba, wx, bx, lam):
    xr = _conv(xl, cw, cb)
    xr_b = xr.astype(BF)
    r = _sigmoid(jnp.dot(xr_b, wa, preferred_element_type=F32) + ba)
    i = _sigmoid(jnp.dot(xr_b, wx, preferred_element_type=F32) + bx)
    sp = _softplus(-lam)
    la = (-LRU_C) * r * sp
    a = jnp.exp(la)
    mult = jnp.sqrt(-jnp.tanh(la) * (a * a + 1.0))
    return xr, xr_b, r, i, sp, a, mult


def lru_gates_fwd(proj, cw, cb, wa2, ba, wx2, bx, lam):
    def body(x_ref, cw_ref, cb_ref, wa_ref, ba_ref, wx_ref, bx_ref, lam_ref, a_ref, u_ref):
        xr, _, _, i, _, a, mult = _lru_gates(x_ref[...], cw_ref[...], cb_ref[...], wa_ref[0], ba_ref[...], wx_ref[0], bx_ref[...], lam_ref[...])
        a_ref[...] = a
        u_ref[...] = jnp.where(_rows(a.shape) >= NPAD, mult * (i * xr), 0.0)

    c0 = PXL // 128
    vec = _spec((1, 128), lambda c: (0, c))
    mat = _spec((1, 128, 128), lambda c: (c, 0, 0))
    return pl.pallas_call(
        body, grid=(8,),
        in_specs=[_spec((T, 128), lambda c: (0, c0 + c)), _spec((4, 128), lambda c: (0, c)), vec, mat, vec, mat, vec, vec],
        out_specs=[_spec((T, 128), lambda c: (0, c)), _spec((T, 128), lambda c: (0, c))],
        out_shape=[jax.ShapeDtypeStruct((T, LRU_W), F32), jax.ShapeDtypeStruct((T, LRU_W), F32)],
        compiler_params=_params(), name="lru_gates_fwd")(proj, cw, cb, wa2, ba, wx2, bx, lam)


def lru_scan_fwd(a, u):
    def body(a_ref, u_ref, h_ref):
        def step(i, h):
            base = pl.multiple_of(i * 8, 8)
            for k in range(8):
                h = a_ref[pl.ds(base + k, 1), :] * h + u_ref[pl.ds(base + k, 1), :]
                h_ref[pl.ds(base + k, 1), :] = h
            return h

        lax.fori_loop(0, T // 8, step, jnp.zeros((1, LRU_W), F32))

    return pl.pallas_call(body, out_shape=jax.ShapeDtypeStruct((T, LRU_W), F32), compiler_params=_params(0), name="lru_scan_fwd")(a, u)


def lru_scan_bwd(a, dh_out):
    def body(a_ref, d_ref, o_ref):
        def step(i, carry):
            base = pl.multiple_of(T - 8 - i * 8, 8)
            for k in range(7, -1, -1):
                carry = d_ref[pl.ds(base + k, 1), :] + carry
                o_ref[pl.ds(base + k, 1), :] = carry
                carry = carry * a_ref[pl.ds(base + k, 1), :]
            return carry

        lax.fori_loop(0, T // 8, step, jnp.zeros((1, LRU_W), F32))

    return pl.pallas_call(body, out_shape=jax.ShapeDtypeStruct((T, LRU_W), F32), compiler_params=_params(0), name="lru_scan_bwd")(a, dh_out)


def lru_out_proj(proj, hseq, norm_w, w_out, resid):
    tn = 512

    def body(g_ref, h_ref, wn_ref, w_ref, r_ref, o_ref, u_ref):
        @pl.when(pl.program_id(0) == 0)
        def _():
            def norm(r0):
                y = _gelu(g_ref[pl.ds(r0, Q), :]) * h_ref[pl.ds(r0, Q), :]
                u_ref[pl.ds(r0, Q), :] = _rms(y, wn_ref[...]).astype(BF)

            _q_chunks(norm)

        def mm(r0):
            o_ref[pl.ds(r0, RC), :] = r_ref[pl.ds(r0, RC), :] + jnp.dot(u_ref[pl.ds(r0, RC), :], w_ref[...], preferred_element_type=F32)

        _row_chunks(mm)

    return pl.pallas_call(
        body, grid=(D // tn,),
        in_specs=[_spec((T, LRU_W), lambda j: (0, PG // LRU_W), single=True), _spec((T, LRU_W), lambda j: (0, 0), single=True),
                  _spec((1, LRU_W), lambda j: (0, 0)), _spec((LRU_W, tn), lambda j: (1, j)), _spec((T, tn), lambda j: (0, j))],
        out_specs=[_spec((T, tn), lambda j: (0, j)), _spec((T, LRU_W), lambda j: (0, 0))],
        out_shape=[jax.ShapeDtypeStruct((T, D), F32), jax.ShapeDtypeStruct((T, LRU_W), BF)],
        compiler_params=_params(), name="lru_out_proj")(proj, hseq, norm_w, w_out, resid)


def lru_norm_bwd(dcat, proj, hseq, norm_w):
    def body(d_ref, g_ref, h_ref, wn_ref, dh_ref, dg_ref, dw_ref):
        @pl.when(pl.program_id(0) == 0)
        def _():
            dw_ref[...] = jnp.zeros_like(dw_ref)

        g = g_ref[...]
        h = h_ref[...]
        ge = _gelu(g)
        dy, dw = _rms_bwd(d_ref[...], ge * h, wn_ref[...])
        dw_ref[...] += jnp.sum(dw, axis=0, keepdims=True)
        dh_ref[...] = dy * ge
        dg_ref[...] = (dy * h * _gelu_grad(g)).astype(BF)

    tile = lambda col: _spec((Q, LRU_W), lambda i: (i, col))
    return pl.pallas_call(
        body, grid=(NCH,),
        in_specs=[tile(1), tile(PG // LRU_W), tile(0), _spec((1, LRU_W), lambda i: (0, 0))],
        out_specs=[tile(0), tile(0), _spec((1, LRU_W), lambda i: (0, 0))],
        out_shape=[jax.ShapeDtypeStruct((T, LRU_W), F32), jax.ShapeDtypeStruct((T, LRU_W), BF), jax.ShapeDtypeStruct((1, LRU_W), F32)],
        compiler_params=_params(), name="lru_norm_bwd")(dcat, proj, hseq, norm_w)


def lru_gates_bwd(dhs, hseq, proj, cw, cb, wa2, ba, wx2, bx, lam):
    def body(dh_ref, h_ref, x_ref, cw_ref, cb_ref, wa_ref, ba_ref, wx_ref, bx_ref, lam_ref,
             dx_ref, dcw_ref, dcb_ref, dwa_ref, dba_ref, dwx_ref, dbx_ref, dlam_ref):
        xl = x_ref[...]
        lam = lam_ref[...]
        xr, xr_b, r, i, sp, a, mult = _lru_gates(xl, cw_ref[...], cb_ref[...], wa_ref[0], ba_ref[...], wx_ref[0], bx_ref[...], lam)
        dh = dh_ref[...]
        da = dh * _shift_down(h_ref[...], 1)
        du = jnp.where(_rows(dh.shape) >= NPAD, dh, 0.0)
        dmult = du * (i * xr)
        di = du * (mult * xr)
        dxr = du * (mult * i)
        dla = da * a - dmult * (a * a) / mult
        dr = dla * ((-LRU_C) * sp)
        dsp = jnp.sum(dla * ((-LRU_C) * r), axis=0, keepdims=True)
        dlam_ref[...] = -dsp * _sigmoid(-lam)
        dpr = dr * r * (1.0 - r)
        dpi = di * i * (1.0 - i)
        dba_ref[...] = jnp.sum(dpr, axis=0, keepdims=True)
        dbx_ref[...] = jnp.sum(dpi, axis=0, keepdims=True)
        dpr_b = dpr.astype(BF)
        dpi_b = dpi.astype(BF)
        dxr = (dxr + lax.dot_general(dpr_b, wa_ref[0], NT_DIMS, preferred_element_type=F32)
               + lax.dot_general(dpi_b, wx_ref[0], NT_DIMS, preferred_element_type=F32))
        dwa_ref[0] = lax.dot_general(xr_b, dpr_b, TN_DIMS, preferred_element_type=F32)
        dwx_ref[0] = lax.dot_general(xr_b, dpi_b, TN_DIMS, preferred_element_type=F32)
        dx, dcw, dcb = _conv_bwd(dxr, xl, cw_ref[...])
        dx_ref[...] = dx.astype(BF)
        dcw_ref[...] = dcw
        dcb_ref[...] = dcb

    c0 = PXL // 128
    vec = _spec((1, 128), lambda c: (0, c))
    mat = _spec((1, 128, 128), lambda c: (c, 0, 0))
    col = _spec((T, 128), lambda c: (0, c))
    vshape = jax.ShapeDtypeStruct((1, LRU_W), F32)
    mshape = jax.ShapeDtypeStruct((8, 128, 128), F32)
    return pl.pallas_call(
        body, grid=(8,),
        in_specs=[col, col, _spec((T, 128), lambda c: (0, c0 + c)), _spec((4, 128), lambda c: (0, c)), vec, mat, vec, mat, vec, vec],
        out_specs=[col, _spec((4, 128), lambda c: (0, c)), vec, mat, vec, mat, vec, vec],
        out_shape=[jax.ShapeDtypeStruct((T, LRU_W), BF), jax.ShapeDtypeStruct((4, LRU_W), F32), vshape, mshape, vshape, mshape, vshape, vshape],
        compiler_params=_params(), name="lru_gates_bwd")(dhs, hseq, proj, cw, cb, wa2, ba, wx2, bx, lam)


def gate_up(h1, wn, w_gate, w_up):
    tn = 256

    def body(h_ref, wn_ref, wg_ref, wu_ref, gt_ref, up_ref, act_ref, u_ref):
        @pl.when(pl.program_id(0) == 0)
        def _():
            def norm(r0):
                u_ref[pl.ds(r0, Q), :] = _rms(h_ref[pl.ds(r0, Q), :], wn_ref[...]).astype(BF)

            _q_chunks(norm)

        def mm(r0):
            u = u_ref[pl.ds(r0, RC), :]
            gt = jnp.dot(u, wg_ref[...], preferred_element_type=F32)
            up = jnp.dot(u, wu_ref[...], preferred_element_type=F32)
            gt_ref[pl.ds(r0, RC), :] = gt.astype(BF)
            up_ref[pl.ds(r0, RC), :] = up.astype(BF)
            act_ref[pl.ds(r0, RC), :] = (gt * _sigmoid(gt) * up).astype(BF)

        _row_chunks(mm)

    tile = _spec((T, tn), lambda j: (0, j))
    big = jax.ShapeDtypeStruct((T, D_FF), BF)
    return pl.pallas_call(
        body, grid=(D_FF // tn,),
        in_specs=[_spec((T, D), lambda j: (0, 0), single=True), _spec((1, D), lambda j: (0, 0)),
                  _spec((D, tn), lambda j: (0, j)), _spec((D, tn), lambda j: (0, j))],
        out_specs=[tile, tile, tile, _spec((T, D), lambda j: (0, 0))],
        out_shape=[big, big, big, jax.ShapeDtypeStruct((T, D), BF)],
        compiler_params=_params(), name="gate_up")(h1, wn, w_gate, w_up)


def loss_bwd(h2, target, wf):
    def body(h_ref, t_ref, w_ref, d_ref, db_ref, l_ref, dw_ref):
        i = pl.program_id(0)

        @pl.when(i == 0)
        def _():
            l_ref[...] = jnp.zeros_like(l_ref)
            dw_ref[...] = jnp.zeros_like(dw_ref)

        h = h_ref[...]
        err = jnp.where(i >= 1, _rms(h, w_ref[...]) - t_ref[...], 0.0)
        l_ref[...] += 0.5 * jnp.sum(jnp.sum(err * err, axis=1, keepdims=True) * (1.0 / D), axis=0, keepdims=True)
        dh, dw = _rms_bwd(err * (1.0 / D), h, w_ref[...])
        dw_ref[...] += jnp.sum(dw, axis=0, keepdims=True)
        d_ref[...] = dh
        db_ref[...] = dh.astype(BF)

    tile = _spec((Q, D), lambda i: (i, 0))
    return pl.pallas_call(
        body, grid=(NCH,),
        in_specs=[tile, _spec((Q, D), lambda i: (jnp.maximum(i - 1, 0), 0)), _spec((1, D), lambda i: (0, 0))],
        out_specs=[tile, tile, _spec((1, 128), lambda i: (0, 0)), _spec((1, D), lambda i: (0, 0))],
        out_shape=[jax.ShapeDtypeStruct((T, D), F32), jax.ShapeDtypeStruct((T, D), BF), jax.ShapeDtypeStruct((1, 128), F32),
                   jax.ShapeDtypeStruct((1, D), F32)],
        compiler_params=_params(), name="loss_bwd")(h2, target, wf)


def swiglu_bwd(dh2_b, w_down, gt, up):
    tn = 256

    def body(d_ref, w_ref, gt_ref, up_ref, dg_ref, du_ref):
        def mm(r0):
            dact = lax.dot_general(d_ref[pl.ds(r0, RC), :], w_ref[...], NT_DIMS, preferred_element_type=F32)
            gt_ = gt_ref[pl.ds(r0, RC), :].astype(F32)
            up_ = up_ref[pl.ds(r0, RC), :].astype(F32)
            sg = _sigmoid(gt_)
            dg_ref[pl.ds(r0, RC), :] = (dact * up_ * (sg * (1.0 + gt_ * (1.0 - sg)))).astype(BF)
            du_ref[pl.ds(r0, RC), :] = (dact * (gt_ * sg)).astype(BF)

        _row_chunks(mm)

    tile = _spec((T, tn), lambda j: (0, j))
    big = jax.ShapeDtypeStruct((T, D_FF), BF)
    return pl.pallas_call(
        body, grid=(D_FF // tn,),
        in_specs=[_spec((T, D), lambda j: (0, 0), single=True), _spec((tn, D), lambda j: (j, 0)), tile, tile],
        out_specs=[tile, tile], out_shape=[big, big], compiler_params=_params(), name="swiglu_bwd")(dh2_b, w_down, gt, up)


def gate_up_bwd(dgt, dup, w_gate, w_up):
    tn = 512

    def body(dg_ref, du_ref, wg_ref, wu_ref, o_ref):
        def mm(r0):
            o_ref[pl.ds(r0, RC), :] = (
                lax.dot_general(dg_ref[pl.ds(r0, RC), :], wg_ref[...], NT_DIMS, preferred_element_type=F32)
                + lax.dot_general(du_ref[pl.ds(r0, RC), :], wu_ref[...], NT_DIMS, preferred_element_type=F32))

        _row_chunks(mm)

    res = _spec((T, D_FF), lambda j: (0, 0), single=True)
    wt = _spec((tn, D_FF), lambda j: (j, 0))
    return pl.pallas_call(
        body, grid=(D // tn,), in_specs=[res, res, wt, wt], out_specs=_spec((T, tn), lambda j: (0, j)),
        out_shape=jax.ShapeDtypeStruct((T, D), F32), compiler_params=_params(), name="gate_up_bwd")(dgt, dup, w_gate, w_up)


def norm_bwd(name, du, h, wn, dres, want_bf16):
    def body(*refs):
        if want_bf16:
            du_ref, h_ref, w_ref, r_ref, d_ref, db_ref, dw_ref = refs
        else:
            du_ref, h_ref, w_ref, r_ref, d_ref, dw_ref = refs

        @pl.when(pl.program_id(0) == 0)
        def _():
            dw_ref[...] = jnp.zeros_like(dw_ref)

        dh, dw = _rms_bwd(du_ref[...], h_ref[...], w_ref[...])
        dw_ref[...] += jnp.sum(dw, axis=0, keepdims=True)
        dh = dh + r_ref[...]
        d_ref[...] = dh
        if want_bf16:
            db_ref[...] = dh.astype(BF)

    tile = _spec((Q, D), lambda i: (i, 0))
    vec = _spec((1, D), lambda i: (0, 0))
    out_specs = [tile] + ([tile] if want_bf16 else []) + [vec]
    out_shape = ([jax.ShapeDtypeStruct((T, D), F32)] + ([jax.ShapeDtypeStruct((T, D), BF)] if want_bf16 else [])
                 + [jax.ShapeDtypeStruct((1, D), F32)])
    return pl.pallas_call(body, grid=(NCH,), in_specs=[tile, tile, vec, tile], out_specs=out_specs, out_shape=out_shape,
                          compiler_params=_params(), name=name)(du, h, wn, dres)


def _adamw(w, g, m, v):
    m = ADAM_B1 * m + (1.0 - ADAM_B1) * g
    v = ADAM_B2 * v + (1.0 - ADAM_B2) * (g * g)
    m_hat = m / (1.0 - ADAM_B1 ** ADAM_STEP)
    v_hat = v / (1.0 - ADAM_B2 ** ADAM_STEP)
    delta = -ADAM_LR * (m_hat / (jnp.sqrt(v_hat) + ADAM_EPS) + ADAM_WD * w)
    return delta, m, v


def adamw_shard(name, recv, w, m, v, tr):
    r, c = w.shape

    def body(p_ref, w_ref, m_ref, v_ref, g_ref, d_ref, mo_ref, vo_ref):
        g = p_ref[0].astype(F32)
        for s in range(1, 8):
            g = g + p_ref[s].astype(F32)
        g_ref[...] = g
        d_ref[...], mo_ref[...], vo_ref[...] = _adamw(w_ref[...], g, m_ref[...], v_ref[...])

    tile = _spec((tr, c), lambda i: (i, 0))
    shape = jax.ShapeDtypeStruct((r, c), F32)
    return pl.pallas_call(
        body, grid=(r // tr,), in_specs=[_spec((8, tr, c), lambda i: (0, i, 0)), tile, tile, tile],
        out_specs=[tile] * 4, out_shape=[shape] * 4, compiler_params=_params(), name=name)(recv, w, m, v)


def sum_slabs(recv):
    def body(p_ref, o_ref):
        g = p_ref[0]
        for s in range(1, 8):
            g = g + p_ref[s]
        o_ref[...] = g

    return pl.pallas_call(body, out_shape=jax.ShapeDtypeStruct(recv.shape[1:], F32), compiler_params=_params(0), name="sum_slabs")(recv)


SIMPLE = [("norm1_w", 1024), ("ssd_conv_b", 1536), ("ssd_dt_bias", 16), ("ssd_a_log", 16), ("ssd_d", 16), ("ssd_norm_w", 1024),
          ("lru_conv_b", 1024), ("lru_ba", 1024), ("lru_bx", 1024), ("lru_lambda", 1024), ("lru_norm_w", 1024), ("norm2_w", 1024),
          ("final_norm_w", 1024)]
SPECIAL = ["lru_wa", "lru_wx", "meta_tokens", "ssd_conv_w", "lru_conv_w"]
SM_ROWS = 176
SM_WA, SM_WX, SM_META, SM_SCW, SM_LCW, SM_LOSS = 14, 78, 142, 158, 166, 170


def _simple_rows():
    rows, r = {}, 0
    for name, n in SIMPLE:
        rows[name] = r
        r += -(-n // 1024)
    return rows


def adamw_small(sm, special_g, ws, ms, vs):
    rows = _simple_rows()
    ns, nx = len(SIMPLE), len(SPECIAL)

    def body(*refs):
        sm_ref = refs[0]
        gx = refs[1:1 + nx]
        wr = refs[1 + nx:1 + nx + ns + nx]
        mr = refs[1 + nx + ns + nx:1 + nx + 2 * (ns + nx)]
        vr = refs[1 + nx + 2 * (ns + nx):1 + nx + 3 * (ns + nx)]
        outs = refs[1 + nx + 3 * (ns + nx):]
        o = 0
        for k, (name, n) in enumerate(SIMPLE):
            r0 = rows[name]
            for c0 in range(0, n, 1024):
                wd = min(1024, n - c0)
                g = sm_ref[r0 + c0 // 1024:r0 + c0 // 1024 + 1, 0:wd]
                sl = (slice(None), slice(c0, c0 + wd))
                d, m2, v2 = _adamw(wr[k][sl], g, mr[k][sl], vr[k][sl])
                outs[o][sl] = g
                outs[o + 1][sl] = d
                outs[o + 2][sl] = m2
                outs[o + 3][sl] = v2
            o += 4
        for k in range(nx):
            d, m2, v2 = _adamw(wr[ns + k][...], gx[k][...], mr[ns + k][...], vr[ns + k][...])
            outs[o][...] = d
            outs[o + 1][...] = m2
            outs[o + 2][...] = v2
            o += 3

    out_shape = []
    for k in range(ns):
        out_shape += [jax.ShapeDtypeStruct(ws[k].shape, F32)] * 4
    for k in range(nx):
        out_shape += [jax.ShapeDtypeStruct(ws[ns + k].shape, F32)] * 3
    return pl.pallas_call(body, out_shape=out_shape, compiler_params=_params(0), name="adamw_small")(sm, *special_g, *ws, *ms, *vs)


def _place():
    return lax.axis_index("x"), lax.axis_index("y"), lax.axis_index("c")


def _index(px, py, pc):
    return 4 * px + 2 * py + pc


def all_gather(name, shards):
    n = len(shards)
    hbm = pl.BlockSpec(memory_space=pl.ANY)

    def body(*refs):
        ins, outs = refs[:n], refs[n:2 * n]
        send_sems, recv_sems, local_sems = refs[2 * n:]
        x, y, c = _place()
        me, sibling = (x, y, c), (x, y, 1 - c)
        chips = [(1 - x, y), (x, 1 - y), (1 - x, 1 - y)]

        def copy(i, k, block, to, src=None):
            dst = outs[i].at[_index(*block)]
            return pltpu.make_async_remote_copy(src_ref=dst if src is None else src, dst_ref=dst, send_sem=send_sems.at[7 * i + k],
                                                recv_sem=recv_sems.at[7 * i + k], device_id=to, device_id_type=MESH)

        mine = [pltpu.make_async_copy(ins[i], outs[i].at[_index(*me)], local_sems.at[i]) for i in range(n)]
        for cp in mine:
            cp.start()
        first = []
        for i in range(n):
            first += [copy(i, 1 + j, me, (*chip, c), src=ins[i]) for j, chip in enumerate(chips)]
            first.append(copy(i, 0, me, sibling, src=ins[i]))
        for cp in first:
            cp.start()
        passed = []
        for i in range(n):
            for j, chip in enumerate(chips):
                copy(i, 1 + j, (*chip, c), me).wait_recv()
                cp = copy(i, 4 + j, (*chip, c), sibling)
                cp.start()
                passed.append(cp)
        for i in range(n):
            copy(i, 0, sibling, me).wait_recv()
            for j, chip in enumerate(chips):
                copy(i, 4 + j, (*chip, 1 - c), me).wait_recv()
        for cp in first + passed:
            cp.wait_send()
        for cp in mine:
            cp.wait()

    return pl.pallas_call(
        body, in_specs=[hbm] * n, out_specs=[hbm] * n,
        out_shape=[jax.ShapeDtypeStruct((8,) + s.shape, s.dtype) for s in shards],
        scratch_shapes=[pltpu.SemaphoreType.DMA((7 * n,)), pltpu.SemaphoreType.DMA((7 * n,)), pltpu.SemaphoreType.DMA((n,))],
        name=name)(*shards)


HBM_SPEC = pl.BlockSpec(memory_space=pltpu.HBM)
SEM_SPEC = pl.BlockSpec(memory_space=pltpu.SEMAPHORE)
EFFECT = pltpu.SideEffectType.DATAFLOW_SIDE_EFFECTING


def _peers(x, y, c):
    return [((1 - x) if k & 4 else x, (1 - y) if k & 2 else y, (1 - c) if k & 1 else c) for k in range(1, 8)]


def _peer_copy(src, land, send_sems, recv_sems, k, peer, mine, slab_src):
    return pltpu.make_async_remote_copy(src_ref=src.at[_index(*peer)] if slab_src else src, dst_ref=land.at[mine], send_sem=send_sems.at[k],
                                        recv_sem=recv_sems.at[k], device_id=peer, device_id_type=MESH)


def copies_start(name, srcs, slab_src, after):
    n = len(srcs)
    zones = [jax.ShapeDtypeStruct(s.shape if slab_src else (8,) + s.shape, s.dtype) for s in srcs]

    def body(*refs):
        ins, lands = refs[:n], refs[n:2 * n]
        sends, recvs = refs[2 * n + 1:3 * n + 1], refs[3 * n + 1:4 * n + 1]
        token = refs[-1]
        x, y, c = _place()
        mine = _index(x, y, c)
        for i in range(n):
            for k, peer in enumerate(_peers(x, y, c)):
                _peer_copy(ins[i], lands[i], sends[i], recvs[i], k, peer, mine, slab_src).start()
        token[...] = jnp.zeros_like(token)

    sem = pltpu.SemaphoreType.DMA((7,))
    res = pl.pallas_call(
        body, name=name,
        out_shape=([sem] * (2 * n) + [pltpu.HBM(s.shape, s.dtype) for s in srcs] + [pltpu.HBM(z.shape, z.dtype) for z in zones]
                   + [jax.ShapeDtypeStruct((8, 128), F32)]),
        in_specs=[HBM_SPEC] * (2 * n) + [pl.BlockSpec(memory_space=pl.ANY)],
        out_specs=[SEM_SPEC] * (2 * n) + [HBM_SPEC] * (2 * n) + [pl.BlockSpec(memory_space=pltpu.VMEM)],
        input_output_aliases={i: 2 * n + i for i in range(2 * n)},
        compiler_params=pltpu.CompilerParams(has_side_effects=EFFECT),
    )(*[pltpu.with_memory_space_constraint(s, pltpu.HBM) for s in srcs],
      *[pltpu.with_memory_space_constraint(lax.empty(z.shape, z.dtype), pltpu.HBM) for z in zones], after)
    return [(res[i], res[n + i], res[2 * n + i], res[3 * n + i]) for i in range(n)], res[-1][0:1, 0:1]


def copies_wait(name, started, slab_src, after):
    n = len(started)

    def body(*refs):
        ins, lands = refs[:n], refs[n:2 * n]
        sends, recvs = refs[2 * n:3 * n], refs[3 * n:4 * n]
        local_sems = refs[-1]
        x, y, c = _place()
        mine = _index(x, y, c)
        own = [pltpu.make_async_copy(ins[i].at[mine] if slab_src else ins[i], lands[i].at[mine], local_sems.at[i]) for i in range(n)]
        for cp in own:
            cp.start()
        for i in range(n):
            for k, peer in enumerate(_peers(x, y, c)):
                arrival = pltpu.make_async_remote_copy(src_ref=ins[i].at[mine] if slab_src else ins[i], dst_ref=lands[i].at[_index(*peer)],
                                                       send_sem=sends[i].at[k], recv_sem=recvs[i].at[k], device_id=peer, device_id_type=MESH)
                arrival.wait_send()
                arrival.wait_recv()
        for cp in own:
            cp.wait()

    srcs = [s[2] for s in started]
    lands = [s[3] for s in started]
    res = pl.pallas_call(
        body, name=name,
        out_shape=[pltpu.HBM(s.shape, s.dtype) for s in srcs] + [pltpu.HBM(z.shape, z.dtype) for z in lands],
        in_specs=[HBM_SPEC] * (2 * n) + [SEM_SPEC] * (2 * n) + [pl.BlockSpec(memory_space=pl.ANY)],
        out_specs=[HBM_SPEC] * (2 * n),
        input_output_aliases={i: i for i in range(2 * n)},
        scratch_shapes=[pltpu.SemaphoreType.DMA((n,))],
        compiler_params=pltpu.CompilerParams(has_side_effects=EFFECT),
    )(*srcs, *lands, *[s[0] for s in started], *[s[1] for s in started], after)
    return list(res[n:])


WEIGHTS = ["meta_tokens", "norm1_w", "w_in", "ssd_conv_w", "ssd_conv_b", "ssd_dt_bias", "ssd_a_log", "ssd_d", "ssd_norm_w", "lru_conv_w",
           "lru_conv_b", "lru_wa", "lru_ba", "lru_wx", "lru_bx", "lru_lambda", "lru_norm_w", "w_out", "norm2_w", "w_gate", "w_up", "w_down",
           "final_norm_w"]
BIG = ["w_in", "w_out", "w_gate", "w_up", "w_down"]
BIG_ROW_TILE = {"w_in": 256, "w_out": 128, "w_gate": 256, "w_up": 256, "w_down": 176}


def _pair_blocks(w):
    w = w.reshape(8, 2, 64, 64)
    z = jnp.zeros((8, 64, 64), w.dtype)
    return jnp.concatenate([jnp.concatenate([w[:, 0], z], axis=2), jnp.concatenate([z, w[:, 1]], axis=2)], axis=1)


def _unpair_blocks(w2):
    return jnp.stack([w2[:, :64, :64], w2[:, 64:, 64:]], axis=1).reshape(16, 64, 64)


def _per_group(v):
    return jnp.pad(v.reshape(2, 1, 8), ((0, 0), (0, 0), (0, 120)))


def _pad_cols(v, n):
    return jnp.pad(v, ((0, 0), (0, n - v.shape[1])))


def local_step(x, target, meta, ssd_cw, lru_cw, w_in, fetch, send, p):
    z120 = jnp.zeros((D, 120), BF)
    w_p = jnp.concatenate([w_in[:, 0:1024], w_in[:, 2576:3600], w_in[:, 3600:4624], w_in[:, 1024:2560],
                           w_in[:, 2560:2568], z120, w_in[:, 2568:2576], z120, jnp.zeros((D, 256), BF)], axis=1)
    bias2, alog2, d2 = _per_group(p["ssd_dt_bias"]), _per_group(p["ssd_a_log"]), _per_group(p["ssd_d"])
    wa2 = _pair_blocks(p["lru_wa"]).astype(BF)
    wx2 = _pair_blocks(p["lru_wx"]).astype(BF)
    lru = (lru_cw, p["lru_conv_b"], wa2, p["lru_ba"], wx2, p["lru_bx"], p["lru_lambda"])

    h0 = jnp.concatenate([jnp.zeros((NPAD, D), F32), meta, x], axis=0)
    proj, u1 = norm_matmul("in_proj", h0, p["norm1_w"], w_p, 512)
    xbc_act = conv_silu_fwd(proj, ssd_cw, p["ssd_conv_b"])
    yn_ssd, y_pre, h_prev = ssd_fwd(xbc_act, proj, bias2, alog2, d2, p["ssd_norm_w"])
    a, u = lru_gates_fwd(proj, *lru)
    hseq = lru_scan_fwd(a, u)
    (w_out,) = fetch(["w_out"], hseq)
    h1a = matmul("out_proj_ssd", yn_ssd, w_out, 512, resid=h0)
    h1, yn_lru = lru_out_proj(proj, hseq, p["lru_norm_w"], w_out, h1a)
    w_gate, w_up = fetch(["w_gate", "w_up"], h1)
    gt, up, act, u2 = gate_up(h1, p["norm2_w"], w_gate, w_up)
    (w_down,) = fetch(["w_down"], act)
    h2 = matmul("down_proj", act, w_down, 512, resid=h1)
    dh2, dh2_b, loss, d_fnw = loss_bwd(h2, target, p["final_norm_w"])

    dgt, dup = swiglu_bwd(dh2_b, w_down, gt, up)
    g_down = matmul_tn("dw_down", act, dh2_b, 1408, 512)
    g_gate = matmul_tn("dw_gate", u2, dgt, 512, 1408)
    g_up = matmul_tn("dw_up", u2, dup, 512, 1408)
    sent = send({"w_down": g_down, "w_gate": g_gate, "w_up": g_up})
    du2 = gate_up_bwd(dgt, dup, w_gate, w_up)
    dh1, dh1_b, d_n2 = norm_bwd("norm2_bwd", du2, h1, p["norm2_w"] + sent, dh2, True)
    g_out = jnp.concatenate([matmul_tn("dw_out_ssd", yn_ssd, dh1_b, 512, 1024), matmul_tn("dw_out_lru", yn_lru, dh1_b, 512, 1024)], axis=0)
    sent = send({"w_out": g_out})
    dcat = matmul("out_proj_bwd", dh1_b, w_out, 512, nt=True)

    dh_out, dg_b, d_lnw = lru_norm_bwd(dcat, proj, hseq, p["lru_norm_w"] + sent)
    dhs = lru_scan_bwd(a, dh_out)
    dxl_b, d_lcw, d_lcb, dwa2, d_ba, dwx2, d_bx, d_lam = lru_gates_bwd(dhs, hseq, proj, *lru)
    dz_b, dx, d_b, d_c, ddt_b, dpar, d_snw = ssd_bwd(dcat, xbc_act, proj, y_pre, h_prev, bias2, alog2, d2, p["ssd_norm_w"])
    dxbc_b, d_scw, d_scb = conv_silu_bwd(jnp.concatenate([dx, d_b, d_c], axis=1), proj, ssd_cw, p["ssd_conv_b"])
    dproj = jnp.concatenate([dz_b, dg_b, dxl_b, dxbc_b, ddt_b, jnp.zeros((T, 256), BF)], axis=1)
    du1 = matmul("in_proj_bwd", dproj, w_p, 512, nt=True)
    dh0, d_n1 = norm_bwd("norm1_bwd", du1, h0, p["norm1_w"], dh1, False)
    g_p = matmul_tn("dw_in", u1, dproj, 512, 1024)
    g_in = jnp.concatenate([g_p[:, 0:1024], g_p[:, PXBC:PXBC + XBC], g_p[:, PDT:PDT + 8], g_p[:, PDT + 128:PDT + 136],
                            g_p[:, PG:PG + 1024], g_p[:, PXL:PXL + 1024]], axis=1)
    small = {"norm1_w": d_n1, "ssd_conv_b": d_scb, "ssd_dt_bias": dpar[:, 0, :8].reshape(1, 16), "ssd_a_log": dpar[:, 1, :8].reshape(1, 16),
             "ssd_d": dpar[:, 2, :8].reshape(1, 16), "ssd_norm_w": d_snw, "lru_conv_b": d_lcb, "lru_ba": d_ba, "lru_bx": d_bx,
             "lru_lambda": d_lam, "lru_norm_w": d_lnw, "norm2_w": d_n2, "final_norm_w": d_fnw,
             "lru_wa": _unpair_blocks(dwa2), "lru_wx": _unpair_blocks(dwx2), "meta_tokens": dh0[NPAD:NPAD + N_META],
             "ssd_conv_w": d_scw, "lru_conv_w": d_lcw}
    return loss, dh0[NPAD + N_META:], g_in, small


def _pack_small(small, loss):
    rows = [_pad_cols(small[name], -(-n // 1024) * 1024).reshape(-1, 1024) for name, n in SIMPLE]
    rows += [small["lru_wa"].reshape(64, 1024), small["lru_wx"].reshape(64, 1024), small["meta_tokens"],
             _pad_cols(small["ssd_conv_w"], 2048).reshape(8, 1024), small["lru_conv_w"], _pad_cols(loss[:, 0:1], 1024)]
    sm = jnp.concatenate(rows, axis=0)
    return jnp.pad(sm, ((0, SM_ROWS - sm.shape[0]), (0, 0)))


def _slabs(g, name):
    if name in ("w_in", "w_gate", "w_up"):
        return g.reshape(g.shape[0], 8, g.shape[1] // 8).transpose(1, 0, 2)
    return g.reshape(8, g.shape[0] // 8, g.shape[1])


def _unslab(g, name):
    if name in ("w_in", "w_gate", "w_up"):
        return g.transpose(1, 0, 2).reshape(g.shape[1], 8 * g.shape[2])
    return g.reshape(8 * g.shape[1], g.shape[2])


def kernel(x, meta_tokens, norm1_w, w_in, ssd_conv_w, ssd_conv_b, ssd_dt_bias, ssd_a_log, ssd_d, ssd_norm_w, lru_conv_w, lru_conv_b, lru_wa, lru_ba, lru_wx, lru_bx, lru_lambda, lru_norm_w, w_out, norm2_w, w_gate, w_up, w_down, final_norm_w, loss_target, m_meta_tokens, m_norm1_w, m_w_in, m_ssd_conv_w, m_ssd_conv_b, m_ssd_dt_bias, m_ssd_a_log, m_ssd_d, m_ssd_norm_w, m_lru_conv_w, m_lru_conv_b, m_lru_wa, m_lru_ba, m_lru_wx, m_lru_bx, m_lru_lambda, m_lru_norm_w, m_w_out, m_norm2_w, m_w_gate, m_w_up, m_w_down, m_final_norm_w, v_meta_tokens, v_norm1_w, v_w_in, v_ssd_conv_w, v_ssd_conv_b, v_ssd_dt_bias, v_ssd_a_log, v_ssd_d, v_ssd_norm_w, v_lru_conv_w, v_lru_conv_b, v_lru_wa, v_lru_ba, v_lru_wx, v_lru_bx, v_lru_lambda, v_lru_norm_w, v_w_out, v_norm2_w, v_w_gate, v_w_up, v_w_down, v_final_norm_w):
    w = dict(meta_tokens=meta_tokens, norm1_w=norm1_w, w_in=w_in[0], ssd_conv_w=ssd_conv_w[0], ssd_conv_b=ssd_conv_b, ssd_dt_bias=ssd_dt_bias,
             ssd_a_log=ssd_a_log, ssd_d=ssd_d, ssd_norm_w=ssd_norm_w, lru_conv_w=lru_conv_w[0], lru_conv_b=lru_conv_b, lru_wa=lru_wa[0],
             lru_ba=lru_ba, lru_wx=lru_wx[0], lru_bx=lru_bx, lru_lambda=lru_lambda, lru_norm_w=lru_norm_w, w_out=w_out[0], norm2_w=norm2_w,
             w_gate=w_gate[0], w_up=w_up[0], w_down=w_down[0], final_norm_w=final_norm_w.reshape(1, D))
    m = dict(meta_tokens=m_meta_tokens, norm1_w=m_norm1_w, w_in=m_w_in[0], ssd_conv_w=m_ssd_conv_w[0], ssd_conv_b=m_ssd_conv_b,
             ssd_dt_bias=m_ssd_dt_bias, ssd_a_log=m_ssd_a_log, ssd_d=m_ssd_d, ssd_norm_w=m_ssd_norm_w, lru_conv_w=m_lru_conv_w[0],
             lru_conv_b=m_lru_conv_b, lru_wa=m_lru_wa[0], lru_ba=m_lru_ba, lru_wx=m_lru_wx[0], lru_bx=m_lru_bx, lru_lambda=m_lru_lambda,
             lru_norm_w=m_lru_norm_w, w_out=m_w_out[0], norm2_w=m_norm2_w, w_gate=m_w_gate[0], w_up=m_w_up[0], w_down=m_w_down[0],
             final_norm_w=m_final_norm_w.reshape(1, D))
    v = dict(meta_tokens=v_meta_tokens, norm1_w=v_norm1_w, w_in=v_w_in[0], ssd_conv_w=v_ssd_conv_w[0], ssd_conv_b=v_ssd_conv_b,
             ssd_dt_bias=v_ssd_dt_bias, ssd_a_log=v_ssd_a_log, ssd_d=v_ssd_d, ssd_norm_w=v_ssd_norm_w, lru_conv_w=v_lru_conv_w[0],
             lru_conv_b=v_lru_conv_b, lru_wa=v_lru_wa[0], lru_ba=v_lru_ba, lru_wx=v_lru_wx[0], lru_bx=v_lru_bx, lru_lambda=v_lru_lambda,
             lru_norm_w=v_lru_norm_w, w_out=v_w_out[0], norm2_w=v_norm2_w, w_gate=v_w_gate[0], w_up=v_w_up[0], w_down=v_w_down[0],
             final_norm_w=v_final_norm_w.reshape(1, D))
    shapes = dict(meta_tokens=meta_tokens.shape, norm1_w=norm1_w.shape, w_in=w_in.shape, ssd_conv_w=ssd_conv_w.shape,
                  ssd_conv_b=ssd_conv_b.shape, ssd_dt_bias=ssd_dt_bias.shape, ssd_a_log=ssd_a_log.shape, ssd_d=ssd_d.shape,
                  ssd_norm_w=ssd_norm_w.shape, lru_conv_w=lru_conv_w.shape, lru_conv_b=lru_conv_b.shape, lru_wa=lru_wa.shape,
                  lru_ba=lru_ba.shape, lru_wx=lru_wx.shape, lru_bx=lru_bx.shape, lru_lambda=lru_lambda.shape, lru_norm_w=lru_norm_w.shape,
                  w_out=w_out.shape, norm2_w=norm2_w.shape, w_gate=w_gate.shape, w_up=w_up.shape, w_down=w_down.shape,
                  final_norm_w=final_norm_w.shape)
    me = _index(*_place())

    small_shard = jnp.concatenate([w["meta_tokens"], _pad_cols(w["ssd_conv_w"], 256).reshape(8, 128), w["lru_conv_w"],
                                   jnp.zeros((4, 128), F32)], axis=0)
    g_in, gs = all_gather("gather_w_in", [w["w_in"].astype(BF), small_shard])
    later = ["w_out", "w_gate", "w_up", "w_down"]
    started, behind = copies_start("gather_rest_start", [w[n].astype(BF) for n in later], False, gs)
    started = dict(zip(later, started))
    meta_full = gs[:, 0:16].transpose(1, 0, 2).reshape(N_META, D)
    ssd_cw = gs[:, 16:24].reshape(8, 4, 256)[:, :, :192].transpose(1, 0, 2).reshape(4, XBC)
    lru_cw = gs[:, 24:28].transpose(1, 0, 2).reshape(4, LRU_W)

    def fetch(names, after):
        got = copies_wait("gather_" + names[0] + "_wait", [started[n] for n in names], False, after)
        return [_unslab(g, n) for n, g in zip(names, got)]

    in_flight = {}

    def send(grads, extra=()):
        names = list(grads)
        st, token = copies_start("grads_" + names[0] + "_start", [_slabs(grads[n], n) for n in names] + list(extra), True, grads[names[0]])
        in_flight.update(zip(names + ["small"] * len(extra), st))
        return token

    loss, grad_x, grad_in, small = local_step(x[0], loss_target[0], meta_full, ssd_cw, lru_cw, _unslab(g_in, "w_in"), fetch, send,
                                              {**w, "norm1_w": w["norm1_w"] + behind})
    send({"w_in": grad_in}, [_pack_small(small, loss).reshape(8, SM_ROWS // 8, 1024)])

    out = {}
    early = ["w_down", "w_gate", "w_up", "w_out"]
    recv = dict(zip(early, copies_wait("grads_early_wait", [in_flight[n] for n in early], True, grad_x)))
    for n in early:
        out[n] = adamw_shard("adamw_" + n, recv[n], w[n], m[n], v[n], BIG_ROW_TILE[n])
    recv_in, recv_small = copies_wait("grads_late_wait", [in_flight["w_in"], in_flight["small"]], True, out["w_out"][0])
    out["w_in"] = adamw_shard("adamw_w_in", recv_in, w["w_in"], m["w_in"], v["w_in"], BIG_ROW_TILE["w_in"])
    sm = all_gather("gather_small_grads", [sum_slabs(recv_small)])[0].reshape(SM_ROWS, 1024)
    special_g = [sm[SM_WA:SM_WA + 64].reshape(16, 64, 64), sm[SM_WX:SM_WX + 64].reshape(16, 64, 64),
                 lax.dynamic_slice(sm[SM_META:SM_META + 16], (0, 128 * me), (16, 128)),
                 lax.dynamic_slice(sm[SM_SCW:SM_SCW + 8].reshape(4, 2048), (0, 192 * me), (4, 192)),
                 lax.dynamic_slice(sm[SM_LCW:SM_LCW + 4], (0, 128 * me), (4, 128))]
    names = [n for n, _ in SIMPLE] + SPECIAL
    res = adamw_small(sm, special_g, [w[n] for n in names], [m[n] for n in names], [v[n] for n in names])
    for k, (n, _) in enumerate(SIMPLE):
        out[n] = res[4 * k:4 * k + 4]
    for k, n in enumerate(SPECIAL):
        o = 4 * len(SIMPLE) + 3 * k
        out[n] = [special_g[k]] + list(res[o:o + 3])
    loss_total = sm[SM_LOSS, 0]
    flat = [loss_total, grad_x[None]]
    for k in range(4):
        flat += [out[n][k].reshape(shapes[n]) for n in WEIGHTS]
    return tuple(flat)
```

```python
import math

import jax
import jax.numpy as jnp
from jax import lax
from jax.experimental import pallas as pl
from jax.experimental.pallas import tpu as pltpu

F32 = jnp.float32
BF = jnp.bfloat16

D = 1024
SEQ = 2048
N_META = 16
Q = 128
NPAD = 112
T = NPAD + N_META + SEQ
NCH = T // Q
RC = 544
D_FF = 2816
SSD_W = 1024
LRU_W = 1024
XBC = 1536
IN_COLS = 4624
PZ, PG, PXL, PXBC, PDT = 0, 1024, 2048, 3072, 4608
NP_IN = 5120
EPS = 1e-6
LRU_C = 8.0
VMEM_LIMIT = 56 * 1024 * 1024

ADAM_LR, ADAM_B1, ADAM_B2, ADAM_EPS, ADAM_WD, ADAM_STEP = 0.001, 0.9, 0.999, 1e-08, 0.01, 10

NT_DIMS = (((1,), (1,)), ((), ()))
TN_DIMS = (((0,), (0,)), ((), ()))
MESH = pl.DeviceIdType.MESH


def _params(n_grid=1, limit=VMEM_LIMIT):
    return pltpu.CompilerParams(dimension_semantics=("arbitrary",) * n_grid, vmem_limit_bytes=limit)


def _spec(shape, imap, single=False):
    if single:
        return pl.BlockSpec(shape, imap, pipeline_mode=pl.Buffered(1))
    return pl.BlockSpec(shape, imap)


def _sigmoid(x):
    return 1.0 / (1.0 + jnp.exp(-x))


def _softplus(x):
    return jnp.maximum(x, 0.0) + jnp.log(1.0 + jnp.exp(-jnp.abs(x)))


def _rms_stats(h):
    return lax.rsqrt(jnp.mean(h * h, axis=-1, keepdims=True) + EPS)


def _rms(h, w):
    return (h * _rms_stats(h)) * w


def _rms_bwd(du, h, w):
    r = _rms_stats(h)
    n = h * r
    dn = du * w
    dh = r * (dn - n * jnp.mean(dn * n, axis=-1, keepdims=True))
    return dh, du * n


_G0 = math.sqrt(2.0 / math.pi)


def _gelu(x):
    return 0.5 * x * (1.0 + jnp.tanh(_G0 * (x + 0.044715 * (x * x * x))))


def _gelu_grad(x):
    t = jnp.tanh(_G0 * (x + 0.044715 * (x * x * x)))
    return 0.5 * (1.0 + t) + 0.5 * x * (1.0 - t * t) * (_G0 * (1.0 + 3.0 * 0.044715 * (x * x)))


def _rows(shape, r0=0):
    return lax.broadcasted_iota(jnp.int32, shape, 0) + r0


def _lanes(shape):
    return lax.broadcasted_iota(jnp.int32, shape, 1)


def _shift_down(x, s):
    if s == 0:
        return x
    return jnp.where(_rows(x.shape) >= s, pltpu.roll(x, s, axis=0), 0.0)


def _shift_up(x, s):
    if s == 0:
        return x
    n = x.shape[0]
    return jnp.where(_rows(x.shape) < n - s, pltpu.roll(x, n - s, axis=0), 0.0)


def _conv(x, w, b):
    y = b + w[3:4, :] * x
    for k in range(3):
        y = y + w[k:k + 1, :] * _shift_down(x, 3 - k)
    return y


def _conv_bwd(dy, x, w):
    dx = w[3:4, :] * dy
    dws = []
    for k in range(3):
        dx = dx + w[k:k + 1, :] * _shift_up(dy, 3 - k)
        dws.append(jnp.sum(dy * _shift_down(x, 3 - k), axis=0, keepdims=True))
    dws.append(jnp.sum(dy * x, axis=0, keepdims=True))
    return dx, jnp.concatenate(dws, axis=0), jnp.sum(dy, axis=0, keepdims=True)


HALF = RC // 2


def _col_tiles(n, tn, fn):
    def step(j, carry):
        fn(pl.multiple_of(j * tn, tn))
        return carry

    lax.fori_loop(0, n // tn, step, 0)


def _rows_spec(cols, block_col=0):
    return _spec((RC, cols), lambda i: (i, block_col))


def _whole(shape):
    return _spec(shape, lambda i: tuple(0 for _ in shape), single=True)


def _vec(cols):
    return _spec((1, cols), lambda i: (0, 0))


def _zero_at_first(*refs):
    @pl.when(pl.program_id(0) == 0)
    def _():
        for r in refs:
            r[...] = jnp.zeros_like(r)


def in_proj(h0, wn, w_p):
    def body(h_ref, wn_ref, w_ref, o_ref, u_ref):
        for r in (0, HALF):
            u_ref[r:r + HALF, :] = _rms(h_ref[r:r + HALF, :], wn_ref[...]).astype(BF)

        def tile(c0):
            o_ref[:, pl.ds(c0, 512)] = jnp.dot(u_ref[...], w_ref[:, pl.ds(c0, 512)], preferred_element_type=F32)

        _col_tiles(NP_IN, 512, tile)

    return pl.pallas_call(
        body, grid=(T // RC,), in_specs=[_rows_spec(D), _vec(D), _whole((D, NP_IN))],
        out_specs=[_rows_spec(NP_IN), _rows_spec(D)],
        out_shape=[jax.ShapeDtypeStruct((T, NP_IN), F32), jax.ShapeDtypeStruct((T, D), BF)],
        compiler_params=_params(), name="in_proj")(h0, wn, w_p)


def out_proj(yn_ssd, proj, hseq, lru_nw, w_out, h0):
    def body(y_ref, g_ref, h_ref, wn_ref, w_ref, r_ref, o_ref, cat_ref):
        cat_ref[:, 0:SSD_W] = y_ref[...]
        for r in (0, HALF):
            y = _gelu(g_ref[r:r + HALF, :]) * h_ref[r:r + HALF, :]
            cat_ref[r:r + HALF, SSD_W:] = _rms(y, wn_ref[...]).astype(BF)

        def tile(c0):
            o_ref[:, pl.ds(c0, 512)] = r_ref[:, pl.ds(c0, 512)] + jnp.dot(cat_ref[...], w_ref[:, pl.ds(c0, 512)], preferred_element_type=F32)

        _col_tiles(D, 512, tile)

    return pl.pallas_call(
        body, grid=(T // RC,),
        in_specs=[_rows_spec(SSD_W), _rows_spec(LRU_W, PG // LRU_W), _rows_spec(LRU_W), _vec(LRU_W), _whole((SSD_W + LRU_W, D)), _rows_spec(D)],
        out_specs=[_rows_spec(D), _rows_spec(SSD_W + LRU_W)],
        out_shape=[jax.ShapeDtypeStruct((T, D), F32), jax.ShapeDtypeStruct((T, SSD_W + LRU_W), BF)],
        compiler_params=_params(), name="out_proj")(yn_ssd, proj, hseq, lru_nw, w_out, h0)


def out_proj_bwd(dh1_b, w_out, proj, hseq, lru_nw):
    def body(d_ref, w_ref, g_ref, h_ref, wn_ref, dy_ref, dh_ref, dg_ref, dw_ref, dl_scr):
        _zero_at_first(dw_ref)

        def tile(c0):
            dy_ref[:, pl.ds(c0, 512)] = lax.dot_general(d_ref[...], w_ref[pl.ds(c0, 512), :], NT_DIMS, preferred_element_type=F32)
            dl_scr[:, pl.ds(c0, 512)] = lax.dot_general(d_ref[...], w_ref[pl.ds(SSD_W + c0, 512), :], NT_DIMS, preferred_element_type=F32)

        _col_tiles(SSD_W, 512, tile)
        for r in (0, HALF):
            g = g_ref[r:r + HALF, :]
            h = h_ref[r:r + HALF, :]
            ge = _gelu(g)
            dy, dw = _rms_bwd(dl_scr[r:r + HALF, :], ge * h, wn_ref[...])
            dw_ref[...] += jnp.sum(dw, axis=0, keepdims=True)
            dh_ref[r:r + HALF, :] = dy * ge
            dg_ref[r:r + HALF, :] = (dy * h * _gelu_grad(g)).astype(BF)

    return pl.pallas_call(
        body, grid=(T // RC,),
        in_specs=[_rows_spec(D), _whole((SSD_W + LRU_W, D)), _rows_spec(LRU_W, PG // LRU_W), _rows_spec(LRU_W), _vec(LRU_W)],
        out_specs=[_rows_spec(SSD_W), _rows_spec(LRU_W), _rows_spec(LRU_W), _vec(LRU_W)],
        out_shape=[jax.ShapeDtypeStruct((T, SSD_W), F32), jax.ShapeDtypeStruct((T, LRU_W), F32), jax.ShapeDtypeStruct((T, LRU_W), BF),
                   jax.ShapeDtypeStruct((1, LRU_W), F32)],
        scratch_shapes=[pltpu.VMEM((RC, LRU_W), F32)],
        compiler_params=_params(), name="out_proj_bwd")(dh1_b, w_out, proj, hseq, lru_nw)


def in_proj_bwd(dproj, w_p, h0, wn, dh1):
    def body(d_ref, w_ref, h_ref, wn_ref, r_ref, o_ref, dw_ref, du_scr):
        _zero_at_first(dw_ref)

        def tile(c0):
            du_scr[:, pl.ds(c0, 512)] = lax.dot_general(d_ref[...], w_ref[pl.ds(c0, 512), :], NT_DIMS, preferred_element_type=F32)

        _col_tiles(D, 512, tile)
        for r in (0, HALF):
            dh, dw = _rms_bwd(du_scr[r:r + HALF, :], h_ref[r:r + HALF, :], wn_ref[...])
            dw_ref[...] += jnp.sum(dw, axis=0, keepdims=True)
            o_ref[r:r + HALF, :] = dh + r_ref[r:r + HALF, :]

    return pl.pallas_call(
        body, grid=(T // RC,), in_specs=[_rows_spec(NP_IN), _whole((D, NP_IN)), _rows_spec(D), _vec(D), _rows_spec(D)],
        out_specs=[_rows_spec(D), _vec(D)],
        out_shape=[jax.ShapeDtypeStruct((T, D), F32), jax.ShapeDtypeStruct((1, D), F32)],
        scratch_shapes=[pltpu.VMEM((RC, D), F32)],
        compiler_params=_params(), name="in_proj_bwd")(dproj, w_p, h0, wn, dh1)


def matmul_tn(name, a, b, tm, tn):
    m, n = a.shape[1], b.shape[1]

    def body(a_ref, b_ref, o_ref, acc_ref):
        acc_ref[...] = jnp.zeros_like(acc_ref)

        def mm(r0):
            acc_ref[...] += lax.dot_general(a_ref[pl.ds(r0, RC), :], b_ref[pl.ds(r0, RC), :], TN_DIMS, preferred_element_type=F32)

        _col_tiles(T, RC, mm)
        o_ref[...] = acc_ref[...].astype(BF)

    return pl.pallas_call(
        body, grid=(m // tm, n // tn),
        in_specs=[_spec((T, tm), lambda i, j: (0, i)), _spec((T, tn), lambda i, j: (0, j))],
        out_specs=_spec((tm, tn), lambda i, j: (i, j)),
        out_shape=jax.ShapeDtypeStruct((m, n), BF),
        scratch_shapes=[pltpu.VMEM((tm, tn), F32)],
        compiler_params=_params(2), name=name)(a, b)


def conv_silu_fwd(proj, cw, cb):
    def body(x_ref, w_ref, b_ref, o_ref):
        pre = _conv(x_ref[...], w_ref[...], b_ref[...])
        o_ref[...] = pre * _sigmoid(pre)

    c0 = PXBC // 128
    return pl.pallas_call(
        body, grid=(XBC // 128,),
        in_specs=[_spec((T, 128), lambda c: (0, c0 + c)), _spec((4, 128), lambda c: (0, c)), _spec((1, 128), lambda c: (0, c))],
        out_specs=_spec((T, 128), lambda c: (0, c)),
        out_shape=jax.ShapeDtypeStruct((T, XBC), F32), compiler_params=_params(), name="conv_silu_fwd")(proj, cw, cb)


def conv_silu_bwd(dact, proj, cw, cb):
    def body(d_ref, x_ref, w_ref, b_ref, dx_ref, dw_ref, db_ref):
        x = x_ref[...]
        pre = _conv(x, w_ref[...], b_ref[...])
        sg = _sigmoid(pre)
        dpre = d_ref[...] * (sg * (1.0 + pre * (1.0 - sg)))
        dx, dw, db = _conv_bwd(dpre, x, w_ref[...])
        dx_ref[...] = dx.astype(BF)
        dw_ref[...] = dw
        db_ref[...] = db

    c0 = PXBC // 128
    return pl.pallas_call(
        body, grid=(XBC // 128,),
        in_specs=[_spec((T, 128), lambda c: (0, c)), _spec((T, 128), lambda c: (0, c0 + c)),
                  _spec((4, 128), lambda c: (0, c)), _spec((1, 128), lambda c: (0, c))],
        out_specs=[_spec((T, 128), lambda c: (0, c)), _spec((4, 128), lambda c: (0, c)), _spec((1, 128), lambda c: (0, c))],
        out_shape=[jax.ShapeDtypeStruct((T, XBC), BF), jax.ShapeDtypeStruct((4, XBC), F32), jax.ShapeDtypeStruct((1, XBC), F32)],
        compiler_params=_params(), name="conv_silu_bwd")(dact, proj, cw, cb)


def _ssd_chunk_common(row0, dt_ref, b_ref, c_ref, bias, a_neg):
    shape = (Q, Q)
    lane = _lanes(shape)
    sub = _rows(shape)
    live = (_rows(shape, row0) >= NPAD) & (lane < 8)
    dtr = dt_ref[:, :]
    dt = jnp.where(live, _softplus(dtr + bias), 0.0)
    d_a = dt * a_neg
    tri = (sub >= lane).astype(F32)
    cs = jnp.dot(tri, d_a, precision=lax.Precision.HIGHEST, preferred_element_type=F32)
    cs_t = cs.T
    bc = b_ref[:, :].astype(BF)
    cc = c_ref[:, :].astype(BF)
    cb = lax.dot_general(cc, bc, NT_DIMS, preferred_element_type=F32)
    cs_last = cs[Q - 1:Q, :]
    return dict(lane=lane, sub=sub, live=live, dtr=dtr, dt=dt, cs=cs, cs_t=cs_t, bc=bc, cc=cc, cb=cb,
                ecs=jnp.exp(cs), dsm=jnp.exp(cs_last - cs), gam=jnp.exp(cs_last), tri=tri)


def _pair(lane_even, mat, j):
    return jnp.where(lane_even, mat[:, j:j + 1], mat[:, j + 1:j + 2])


def _head_decay(cm, j):
    seg = cm["cs"][:, j:j + 1] - cm["cs_t"][j:j + 1, :]
    return jnp.exp(jnp.where(cm["sub"] >= cm["lane"], seg, -jnp.inf))


def ssd_fwd(xbc_act, proj, dt_bias2, a_log2, d2, norm_w):
    def body(x_ref, b_ref, c_ref, dt_ref, z_ref, bias_ref, alog_ref, d_ref, nw_ref, yn_ref, y_ref, hp_ref, h_scr):
        c = pl.program_id(1)

        @pl.when(c == 0)
        def _():
            h_scr[...] = jnp.zeros_like(h_scr)

        bias = bias_ref[0]
        a_neg = -jnp.exp(alog_ref[0])
        dsk = d_ref[0]
        cm = _ssd_chunk_common(c * Q, dt_ref, b_ref, c_ref, bias, a_neg)
        lane_even = cm["lane"] < 64
        sub_even = cm["sub"] < 64
        for p in range(4):
            je, jo = 2 * p, 2 * p + 1
            xp = x_ref[:, 128 * p:128 * p + 128]
            xdt = xp * _pair(lane_even, cm["dt"], je)
            xdt_b = xdt.astype(BF)
            m_e = (cm["cb"] * _head_decay(cm, je)).astype(BF)
            m_o = (cm["cb"] * _head_decay(cm, jo)).astype(BF)
            zero = jnp.zeros_like(xdt_b)
            yd = (jnp.dot(m_e, jnp.where(lane_even, xdt_b, zero), preferred_element_type=F32)
                  + jnp.dot(m_o, jnp.where(lane_even, zero, xdt_b), preferred_element_type=F32))
            hp = h_scr[p]
            hp_ref[0, 0, p] = hp
            yo = lax.dot_general(cm["cc"], hp.astype(BF), NT_DIMS, preferred_element_type=F32) * _pair(lane_even, cm["ecs"], je)
            dsk_p = jnp.where(lane_even[0:1, :], dsk[:, je:je + 1], dsk[:, jo:jo + 1])
            y_ref[:, 128 * p:128 * p + 128] = yd + yo + xp * dsk_p
            st = lax.dot_general((xdt * _pair(lane_even, cm["dsm"], je)).astype(BF), cm["bc"], TN_DIMS, preferred_element_type=F32)
            gam = jnp.where(sub_even[:, 0:1], cm["gam"][:, je:je + 1], cm["gam"][:, jo:jo + 1])
            h_scr[p] = hp * gam + st
        zc = z_ref[:, :]
        gated = y_ref[:, :] * (zc * _sigmoid(zc))
        yn_ref[:, :] = _rms(gated, nw_ref[...]).astype(BF)

    par = _spec((1, 1, 128), lambda g, c: (g, 0, 0))
    wide = _spec((Q, 512), lambda g, c: (c, g))
    return pl.pallas_call(
        body, grid=(2, NCH),
        in_specs=[wide, _spec((Q, 128), lambda g, c: (c, 8 + g)), _spec((Q, 128), lambda g, c: (c, 10 + g)),
                  _spec((Q, 128), lambda g, c: (c, PDT // 128 + g)), wide, par, par, par, _spec((1, 512), lambda g, c: (0, g))],
        out_specs=[wide, wide, _spec((1, 1, 4, 128, 128), lambda g, c: (g, c, 0, 0, 0))],
        out_shape=[jax.ShapeDtypeStruct((T, SSD_W), BF), jax.ShapeDtypeStruct((T, SSD_W), F32),
                   jax.ShapeDtypeStruct((2, NCH, 4, 128, 128), F32)],
        scratch_shapes=[pltpu.VMEM((4, 128, 128), F32)],
        compiler_params=_params(2), name="ssd_fwd")(xbc_act, xbc_act, xbc_act, proj, proj, dt_bias2, a_log2, d2, norm_w)


def ssd_bwd(dyn, xbc_act, proj, y_pre, h_prev, dt_bias2, a_log2, d2, norm_w):
    def body(dyn_ref, x_ref, b_ref, c_ref, dt_ref, z_ref, y_ref, hp_ref, bias_ref, alog_ref, d_ref, nw_ref,
             dz_ref, dx_ref, db_ref, dc_ref, ddt_ref, dpar_ref, dnw_ref, dh_scr, acc_scr):
        ci = pl.program_id(1)

        @pl.when(ci == 0)
        def _():
            dh_scr[...] = jnp.zeros_like(dh_scr)
            acc_scr[...] = jnp.zeros_like(acc_scr)
            dnw_ref[...] = jnp.zeros_like(dnw_ref)

        bias = bias_ref[0]
        a_neg = -jnp.exp(alog_ref[0])
        dsk = d_ref[0]
        cm = _ssd_chunk_common((NCH - 1 - ci) * Q, dt_ref, b_ref, c_ref, bias, a_neg)
        lane, sub = cm["lane"], cm["sub"]
        lane_even = lane < 64
        sub_even = sub < 64
        zc = z_ref[:, :]
        yc = y_ref[:, :]
        sg = _sigmoid(zc)
        sz = zc * sg
        dgated, dnw = _rms_bwd(dyn_ref[:, :], yc * sz, nw_ref[...])
        dnw_ref[...] += jnp.sum(dnw, axis=0, keepdims=True)
        dz_ref[:, :] = (dgated * yc * (sg * (1.0 + zc * (1.0 - sg)))).astype(BF)
        dy_all = dgated * sz
        dcb = jnp.zeros((Q, Q), F32)
        db_acc = jnp.zeros((Q, Q), F32)
        dc_acc = jnp.zeros((Q, Q), F32)
        dcs_col = jnp.zeros((Q, Q), F32)
        dcs_row = jnp.zeros((Q, Q), F32)
        ddt = jnp.zeros((Q, Q), F32)
        for p in range(4):
            je, jo = 2 * p, 2 * p + 1
            xp = x_ref[:, 128 * p:128 * p + 128]
            dy = dy_all[:, 128 * p:128 * p + 128]
            dt_p = _pair(lane_even, cm["dt"], je)
            xdt = xp * dt_p
            xdt_b = xdt.astype(BF)
            dy_b = dy.astype(BF)
            zero = jnp.zeros_like(dy_b)
            hp = hp_ref[0, 0, p]
            hp_b = hp.astype(BF)
            dh = dh_scr[p]
            dh_b = dh.astype(BF)
            acc_scr[p:p + 1, :] += jnp.sum(dy * xp, axis=0, keepdims=True)
            dsk_p = jnp.where(lane_even[0:1, :], dsk[:, je:je + 1], dsk[:, jo:jo + 1])
            dxp = dy * dsk_p
            e_p = _pair(lane_even, cm["ecs"], je)
            g_p = lax.dot_general(cm["cc"], hp_b, NT_DIMS, preferred_element_type=F32)
            dg_b = (dy * e_p).astype(BF)
            de = dy * g_p * e_p
            dc_acc = dc_acc + jnp.dot(dg_b, hp_b, preferred_element_type=F32)
            dh_in = lax.dot_general(dg_b, cm["cc"], TN_DIMS, preferred_element_type=F32)
            ds_p = _pair(lane_even, cm["dsm"], je)
            r_p = lax.dot_general(cm["bc"], dh_b, NT_DIMS, preferred_element_type=F32)
            dxdt = r_p * ds_p
            tt = r_p * xdt * ds_p
            db_acc = db_acc + jnp.dot((xdt * ds_p).astype(BF), dh_b, preferred_element_type=F32)
            dgam_m = dh * hp
            for j, even in ((je, True), (jo, False)):
                sel = lane_even if even else jnp.logical_not(lane_even)
                ssel = sub_even if even else jnp.logical_not(sub_even)
                l_j = _head_decay(cm, j)
                m_j = cm["cb"] * l_j
                dm = lax.dot_general(jnp.where(sel, dy_b, zero), xdt_b, NT_DIMS, preferred_element_type=F32)
                dxdt = dxdt + lax.dot_general(m_j.astype(BF), jnp.where(sel, dy_b, zero), TN_DIMS, preferred_element_type=F32)
                w_j = dm * m_j
                dcb = dcb + dm * l_j
                t_j = jnp.sum(jnp.where(sel, tt, 0.0), axis=1, keepdims=True)
                col = (jnp.sum(w_j, axis=1, keepdims=True) + jnp.sum(jnp.where(sel, de, 0.0), axis=1, keepdims=True) - t_j)
                gam_j = cm["gam"][:, j:j + 1]
                last = (jnp.sum(t_j, axis=0, keepdims=True)
                        + jnp.sum(jnp.sum(jnp.where(ssel, dgam_m, 0.0), axis=1, keepdims=True), axis=0, keepdims=True) * gam_j)
                col = col + jnp.where(sub[:, 0:1] == Q - 1, last, 0.0)
                dcs_col = dcs_col + jnp.where(lane == j, col, 0.0)
                dcs_row = dcs_row + jnp.where(sub == j, jnp.sum(w_j, axis=0, keepdims=True), 0.0)
            gam = jnp.where(sub_even[:, 0:1], cm["gam"][:, je:je + 1], cm["gam"][:, jo:jo + 1])
            dh_scr[p] = dh_in + dh * gam
            dx_ref[:, 128 * p:128 * p + 128] = dxp + dxdt * dt_p
            dd = dxdt * xp
            ddt = ddt + jnp.where(lane == je, jnp.sum(jnp.where(lane_even, dd, 0.0), axis=1, keepdims=True), 0.0)
            ddt = ddt + jnp.where(lane == jo, jnp.sum(jnp.where(lane_even, 0.0, dd), axis=1, keepdims=True), 0.0)
        dcb_b = dcb.astype(BF)
        dc_ref[:, :] = dc_acc + jnp.dot(dcb_b, cm["bc"], preferred_element_type=F32)
        db_ref[:, :] = db_acc + lax.dot_general(dcb_b, cm["cc"], TN_DIMS, preferred_element_type=F32)
        dcs = dcs_col - dcs_row.T
        dd_a = lax.dot_general(cm["tri"], dcs, TN_DIMS, precision=lax.Precision.HIGHEST, preferred_element_type=F32)
        ddt = ddt + dd_a * a_neg
        acc_scr[5:6, :] += jnp.sum(dd_a * cm["dt"], axis=0, keepdims=True)
        draw = jnp.where(cm["live"], ddt * _sigmoid(cm["dtr"] + bias), 0.0)
        acc_scr[4:5, :] += jnp.sum(draw, axis=0, keepdims=True)
        ddt_ref[:, :] = draw.astype(BF)

        @pl.when(ci == NCH - 1)
        def _():
            lane1 = _lanes((1, 128))
            dd = jnp.zeros((1, 128), F32)
            for p in range(4):
                row = acc_scr[p:p + 1, :]
                dd = dd + jnp.where(lane1 == 2 * p, jnp.sum(jnp.where(lane1 < 64, row, 0.0), axis=1, keepdims=True), 0.0)
                dd = dd + jnp.where(lane1 == 2 * p + 1, jnp.sum(jnp.where(lane1 < 64, 0.0, row), axis=1, keepdims=True), 0.0)
            dpar_ref[0] = jnp.concatenate([acc_scr[4:5, :], acc_scr[5:6, :] * a_neg, dd, jnp.zeros((5, 128), F32)], axis=0)

    par = _spec((1, 1, 128), lambda g, c: (g, 0, 0))
    wide = _spec((Q, 512), lambda g, c: (NCH - 1 - c, g))
    thin = _spec((Q, 128), lambda g, c: (NCH - 1 - c, g))
    return pl.pallas_call(
        body, grid=(2, NCH),
        in_specs=[wide, wide, _spec((Q, 128), lambda g, c: (NCH - 1 - c, 8 + g)), _spec((Q, 128), lambda g, c: (NCH - 1 - c, 10 + g)),
                  _spec((Q, 128), lambda g, c: (NCH - 1 - c, PDT // 128 + g)), wide, wide,
                  _spec((1, 1, 4, 128, 128), lambda g, c: (g, NCH - 1 - c, 0, 0, 0)), par, par, par, _spec((1, 512), lambda g, c: (0, g))],
        out_specs=[wide, wide, thin, thin, thin, _spec((1, 8, 128), lambda g, c: (g, 0, 0)), _spec((1, 512), lambda g, c: (0, g))],
        out_shape=[jax.ShapeDtypeStruct((T, SSD_W), BF), jax.ShapeDtypeStruct((T, SSD_W), F32), jax.ShapeDtypeStruct((T, 256), F32),
                   jax.ShapeDtypeStruct((T, 256), F32), jax.ShapeDtypeStruct((T, 256), BF), jax.ShapeDtypeStruct((2, 8, 128), F32),
                   jax.ShapeDtypeStruct((1, SSD_W), F32)],
        scratch_shapes=[pltpu.VMEM((4, 128, 128), F32), pltpu.VMEM((8, 128), F32)],
        compiler_params=_params(2), name="ssd_bwd")(dyn, xbc_act, xbc_act, xbc_act, proj, proj, y_pre, h_prev, dt_bias2, a_log2, d2, norm_w)


def _lru_gates(xl, cw, cb, wa, ba, wx, bx, lam):
    xr = _conv(xl, cw, cb)
    xr_b = xr.astype(BF)
    r = _sigmoid(jnp.dot(xr_b, wa, preferred_element_type=F32) + ba)
    i = _sigmoid(jnp.dot(xr_b, wx, preferred_element_type=F32) + bx)
    sp = _softplus(-lam)
    la = (-LRU_C) * r * sp
    a = jnp.exp(la)
    mult = jnp.sqrt(-jnp.tanh(la) * (a * a + 1.0))
    return xr, xr_b, r, i, sp, a, mult


def lru_gates_fwd(proj, cw, cb, wa2, ba, wx2, bx, lam):
    def body(x_ref, cw_ref, cb_ref, wa_ref, ba_ref, wx_ref, bx_ref, lam_ref, a_ref, u_ref):
        xr, _, _, i, _, a, mult = _lru_gates(x_ref[...], cw_ref[...], cb_ref[...], wa_ref[0], ba_ref[...], wx_ref[0], bx_ref[...], lam_ref[...])
        a_ref[...] = a
        u_ref[...] = jnp.where(_rows(a.shape) >= NPAD, mult * (i * xr), 0.0)

    c0 = PXL // 128
    vec = _spec((1, 128), lambda c: (0, c))
    mat = _spec((1, 128, 128), lambda c: (c, 0, 0))
    return pl.pallas_call(
        body, grid=(8,),
        in_specs=[_spec((T, 128), lambda c: (0, c0 + c)), _spec((4, 128), lambda c: (0, c)), vec, mat, vec, mat, vec, vec],
        out_specs=[_spec((T, 128), lambda c: (0, c)), _spec((T, 128), lambda c: (0, c))],
        out_shape=[jax.ShapeDtypeStruct((T, LRU_W), F32), jax.ShapeDtypeStruct((T, LRU_W), F32)],
        compiler_params=_params(), name="lru_gates_fwd")(proj, cw, cb, wa2, ba, wx2, bx, lam)


def lru_scan_fwd(a, u):
    def body(a_ref, u_ref, h_ref):
        def step(i, h):
            base = pl.multiple_of(i * 8, 8)
            for k in range(8):
                h = a_ref[pl.ds(base + k, 1), :] * h + u_ref[pl.ds(base + k, 1), :]
                h_ref[pl.ds(base + k, 1), :] = h
            return h

        lax.fori_loop(0, T // 8, step, jnp.zeros((1, LRU_W), F32))

    return pl.pallas_call(body, out_shape=jax.ShapeDtypeStruct((T, LRU_W), F32), compiler_params=_params(0), name="lru_scan_fwd")(a, u)


def lru_scan_bwd(a, dh_out):
    def body(a_ref, d_ref, o_ref):
        def step(i, carry):
            base = pl.multiple_of(T - 8 - i * 8, 8)
            for k in range(7, -1, -1):
                carry = d_ref[pl.ds(base + k, 1), :] + carry
                o_ref[pl.ds(base + k, 1), :] = carry
                carry = carry * a_ref[pl.ds(base + k, 1), :]
            return carry

        lax.fori_loop(0, T // 8, step, jnp.zeros((1, LRU_W), F32))

    return pl.pallas_call(body, out_shape=jax.ShapeDtypeStruct((T, LRU_W), F32), compiler_params=_params(0), name="lru_scan_bwd")(a, dh_out)


def lru_gates_bwd(dhs, hseq, proj, cw, cb, wa2, ba, wx2, bx, lam):
    def body(dh_ref, h_ref, x_ref, cw_ref, cb_ref, wa_ref, ba_ref, wx_ref, bx_ref, lam_ref,
             dx_ref, dcw_ref, dcb_ref, dwa_ref, dba_ref, dwx_ref, dbx_ref, dlam_ref):
        xl = x_ref[...]
        lam = lam_ref[...]
        xr, xr_b, r, i, sp, a, mult = _lru_gates(xl, cw_ref[...], cb_ref[...], wa_ref[0], ba_ref[...], wx_ref[0], bx_ref[...], lam)
        dh = dh_ref[...]
        da = dh * _shift_down(h_ref[...], 1)
        du = jnp.where(_rows(dh.shape) >= NPAD, dh, 0.0)
        dmult = du * (i * xr)
        di = du * (mult * xr)
        dxr = du * (mult * i)
        dla = da * a - dmult * (a * a) / mult
        dr = dla * ((-LRU_C) * sp)
        dsp = jnp.sum(dla * ((-LRU_C) * r), axis=0, keepdims=True)
        dlam_ref[...] = -dsp * _sigmoid(-lam)
        dpr = dr * r * (1.0 - r)
        dpi = di * i * (1.0 - i)
        dba_ref[...] = jnp.sum(dpr, axis=0, keepdims=True)
        dbx_ref[...] = jnp.sum(dpi, axis=0, keepdims=True)
        dpr_b = dpr.astype(BF)
        dpi_b = dpi.astype(BF)
        dxr = (dxr + lax.dot_general(dpr_b, wa_ref[0], NT_DIMS, preferred_element_type=F32)
               + lax.dot_general(dpi_b, wx_ref[0], NT_DIMS, preferred_element_type=F32))
        dwa_ref[0] = lax.dot_general(xr_b, dpr_b, TN_DIMS, preferred_element_type=F32)
        dwx_ref[0] = lax.dot_general(xr_b, dpi_b, TN_DIMS, preferred_element_type=F32)
        dx, dcw, dcb = _conv_bwd(dxr, xl, cw_ref[...])
        dx_ref[...] = dx.astype(BF)
        dcw_ref[...] = dcw
        dcb_ref[...] = dcb

    c0 = PXL // 128
    vec = _spec((1, 128), lambda c: (0, c))
    mat = _spec((1, 128, 128), lambda c: (c, 0, 0))
    col = _spec((T, 128), lambda c: (0, c))
    vshape = jax.ShapeDtypeStruct((1, LRU_W), F32)
    mshape = jax.ShapeDtypeStruct((8, 128, 128), F32)
    return pl.pallas_call(
        body, grid=(8,),
        in_specs=[col, col, _spec((T, 128), lambda c: (0, c0 + c)), _spec((4, 128), lambda c: (0, c)), vec, mat, vec, mat, vec, vec],
        out_specs=[col, _spec((4, 128), lambda c: (0, c)), vec, mat, vec, mat, vec, vec],
        out_shape=[jax.ShapeDtypeStruct((T, LRU_W), BF), jax.ShapeDtypeStruct((4, LRU_W), F32), vshape, mshape, vshape, mshape, vshape, vshape],
        compiler_params=_params(), name="lru_gates_bwd")(dhs, hseq, proj, cw, cb, wa2, ba, wx2, bx, lam)


def gate_up(h1, wn, w_gate, w_up):
    def body(h_ref, wn_ref, wg_ref, wu_ref, gt_ref, up_ref, act_ref, u_ref):
        for r in (0, HALF):
            u_ref[r:r + HALF, :] = _rms(h_ref[r:r + HALF, :], wn_ref[...]).astype(BF)

        def tile(c0):
            cols = pl.ds(c0, 256)
            gt = jnp.dot(u_ref[...], wg_ref[:, cols], preferred_element_type=F32)
            up = jnp.dot(u_ref[...], wu_ref[:, cols], preferred_element_type=F32)
            gt_ref[:, cols] = gt.astype(BF)
            up_ref[:, cols] = up.astype(BF)
            act_ref[:, cols] = (gt * _sigmoid(gt) * up).astype(BF)

        _col_tiles(D_FF, 256, tile)

    big = jax.ShapeDtypeStruct((T, D_FF), BF)
    return pl.pallas_call(
        body, grid=(T // RC,), in_specs=[_rows_spec(D), _vec(D), _whole((D, D_FF)), _whole((D, D_FF))],
        out_specs=[_rows_spec(D_FF), _rows_spec(D_FF), _rows_spec(D_FF), _rows_spec(D)],
        out_shape=[big, big, big, jax.ShapeDtypeStruct((T, D), BF)],
        compiler_params=_params(), name="gate_up")(h1, wn, w_gate, w_up)


def down_loss(act, w_down, h1, target, wf):
    def body(a_ref, w_ref, r_ref, t_ref, wf_ref, d_ref, db_ref, l_ref, dw_ref, h_scr):
        _zero_at_first(l_ref, dw_ref)

        def tile(c0):
            cols = pl.ds(c0, 512)
            h_scr[:, cols] = r_ref[:, cols] + jnp.dot(a_ref[...], w_ref[:, cols], preferred_element_type=F32)

        _col_tiles(D, 512, tile)
        for r in (0, HALF):
            h = h_scr[r:r + HALF, :]
            live = _rows((HALF, D), pl.program_id(0) * RC + r) >= NPAD + N_META
            err = jnp.where(live, _rms(h, wf_ref[...]) - t_ref[r:r + HALF, :], 0.0)
            l_ref[...] += 0.5 * jnp.sum(jnp.sum(err * err, axis=1, keepdims=True) * (1.0 / D), axis=0, keepdims=True)
            dh, dw = _rms_bwd(err * (1.0 / D), h, wf_ref[...])
            dw_ref[...] += jnp.sum(dw, axis=0, keepdims=True)
            d_ref[r:r + HALF, :] = dh
            db_ref[r:r + HALF, :] = dh.astype(BF)

    return pl.pallas_call(
        body, grid=(T // RC,), in_specs=[_rows_spec(D_FF), _whole((D_FF, D)), _rows_spec(D), _rows_spec(D), _vec(D)],
        out_specs=[_rows_spec(D), _rows_spec(D), _spec((1, 128), lambda i: (0, 0)), _vec(D)],
        out_shape=[jax.ShapeDtypeStruct((T, D), F32), jax.ShapeDtypeStruct((T, D), BF), jax.ShapeDtypeStruct((1, 128), F32),
                   jax.ShapeDtypeStruct((1, D), F32)],
        scratch_shapes=[pltpu.VMEM((RC, D), F32)],
        compiler_params=_params(), name="down_loss")(act, w_down, h1, target, wf)


def swiglu_bwd(dh2_b, w_down, gt, up):
    def body(d_ref, w_ref, gt_ref, up_ref, dg_ref, du_ref):
        def tile(c0):
            cols = pl.ds(c0, 256)
            dact = lax.dot_general(d_ref[...], w_ref[cols, :], NT_DIMS, preferred_element_type=F32)
            gt_ = gt_ref[:, cols].astype(F32)
            up_ = up_ref[:, cols].astype(F32)
            sg = _sigmoid(gt_)
            dg_ref[:, cols] = (dact * up_ * (sg * (1.0 + gt_ * (1.0 - sg)))).astype(BF)
            du_ref[:, cols] = (dact * (gt_ * sg)).astype(BF)

        _col_tiles(D_FF, 256, tile)

    big = jax.ShapeDtypeStruct((T, D_FF), BF)
    return pl.pallas_call(
        body, grid=(T // RC,), in_specs=[_rows_spec(D), _whole((D_FF, D)), _rows_spec(D_FF), _rows_spec(D_FF)],
        out_specs=[_rows_spec(D_FF), _rows_spec(D_FF)], out_shape=[big, big], compiler_params=_params(), name="swiglu_bwd")(dh2_b, w_down, gt, up)


def gate_up_bwd(dgt, dup, w_gate, w_up, h1, wn, dh2):
    def body(dg_ref, du_ref, wg_ref, wu_ref, h_ref, wn_ref, r_ref, d_ref, db_ref, dw_ref, du_scr):
        _zero_at_first(dw_ref)

        def tile(c0):
            rows = pl.ds(c0, 512)
            du_scr[:, rows] = (lax.dot_general(dg_ref[...], wg_ref[rows, :], NT_DIMS, preferred_element_type=F32)
                               + lax.dot_general(du_ref[...], wu_ref[rows, :], NT_DIMS, preferred_element_type=F32))

        _col_tiles(D, 512, tile)
        for r in (0, HALF):
            dh, dw = _rms_bwd(du_scr[r:r + HALF, :], h_ref[r:r + HALF, :], wn_ref[...])
            dw_ref[...] += jnp.sum(dw, axis=0, keepdims=True)
            dh = dh + r_ref[r:r + HALF, :]
            d_ref[r:r + HALF, :] = dh
            db_ref[r:r + HALF, :] = dh.astype(BF)

    return pl.pallas_call(
        body, grid=(T // RC,),
        in_specs=[_rows_spec(D_FF), _rows_spec(D_FF), _whole((D, D_FF)), _whole((D, D_FF)), _rows_spec(D), _vec(D), _rows_spec(D)],
        out_specs=[_rows_spec(D), _rows_spec(D), _vec(D)],
        out_shape=[jax.ShapeDtypeStruct((T, D), F32), jax.ShapeDtypeStruct((T, D), BF), jax.ShapeDtypeStruct((1, D), F32)],
        scratch_shapes=[pltpu.VMEM((RC, D), F32)],
        compiler_params=_params(), name="gate_up_bwd")(dgt, dup, w_gate, w_up, h1, wn, dh2)


def _adamw(w, g, m, v):
    m = ADAM_B1 * m + (1.0 - ADAM_B1) * g
    v = ADAM_B2 * v + (1.0 - ADAM_B2) * (g * g)
    m_hat = m / (1.0 - ADAM_B1 ** ADAM_STEP)
    v_hat = v / (1.0 - ADAM_B2 ** ADAM_STEP)
    delta = -ADAM_LR * (m_hat / (jnp.sqrt(v_hat) + ADAM_EPS) + ADAM_WD * w)
    return delta, m, v


def adamw_shard(name, recv, w, m, v, tr):
    r, c = w.shape

    def body(p_ref, w_ref, m_ref, v_ref, g_ref, d_ref, mo_ref, vo_ref):
        g = p_ref[0].astype(F32)
        for s in range(1, 8):
            g = g + p_ref[s].astype(F32)
        g_ref[...] = g
        d_ref[...], mo_ref[...], vo_ref[...] = _adamw(w_ref[...], g, m_ref[...], v_ref[...])

    tile = _spec((tr, c), lambda i: (i, 0))
    shape = jax.ShapeDtypeStruct((r, c), F32)
    return pl.pallas_call(
        body, grid=(r // tr,), in_specs=[_spec((8, tr, c), lambda i: (0, i, 0)), tile, tile, tile],
        out_specs=[tile] * 4, out_shape=[shape] * 4, compiler_params=_params(), name=name)(recv, w, m, v)


def sum_slabs(recv):
    def body(p_ref, o_ref):
        g = p_ref[0]
        for s in range(1, 8):
            g = g + p_ref[s]
        o_ref[...] = g

    return pl.pallas_call(body, out_shape=jax.ShapeDtypeStruct(recv.shape[1:], F32), compiler_params=_params(0), name="sum_slabs")(recv)


SIMPLE = [("norm1_w", 1024), ("ssd_conv_b", 1536), ("ssd_dt_bias", 16), ("ssd_a_log", 16), ("ssd_d", 16), ("ssd_norm_w", 1024),
          ("lru_conv_b", 1024), ("lru_ba", 1024), ("lru_bx", 1024), ("lru_lambda", 1024), ("lru_norm_w", 1024), ("norm2_w", 1024),
          ("final_norm_w", 1024)]
SPECIAL = ["lru_wa", "lru_wx", "meta_tokens", "ssd_conv_w", "lru_conv_w"]
SM_ROWS = 176
SM_WA, SM_WX, SM_META, SM_SCW, SM_LCW, SM_LOSS = 14, 78, 142, 158, 166, 170


def _simple_rows():
    rows, r = {}, 0
    for name, n in SIMPLE:
        rows[name] = r
        r += -(-n // 1024)
    return rows


def adamw_small(sm, special_g, ws, ms, vs):
    rows = _simple_rows()
    ns, nx = len(SIMPLE), len(SPECIAL)

    def body(*refs):
        sm_ref = refs[0]
        gx = refs[1:1 + nx]
        wr = refs[1 + nx:1 + nx + ns + nx]
        mr = refs[1 + nx + ns + nx:1 + nx + 2 * (ns + nx)]
        vr = refs[1 + nx + 2 * (ns + nx):1 + nx + 3 * (ns + nx)]
        outs = refs[1 + nx + 3 * (ns + nx):]
        o = 0
        for k, (name, n) in enumerate(SIMPLE):
            r0 = rows[name]
            for c0 in range(0, n, 1024):
                wd = min(1024, n - c0)
                g = sm_ref[r0 + c0 // 1024:r0 + c0 // 1024 + 1, 0:wd]
                sl = (slice(None), slice(c0, c0 + wd))
                d, m2, v2 = _adamw(wr[k][sl], g, mr[k][sl], vr[k][sl])
                outs[o][sl] = g
                outs[o + 1][sl] = d
                outs[o + 2][sl] = m2
                outs[o + 3][sl] = v2
            o += 4
        for k in range(nx):
            d, m2, v2 = _adamw(wr[ns + k][...], gx[k][...], mr[ns + k][...], vr[ns + k][...])
            outs[o][...] = d
            outs[o + 1][...] = m2
            outs[o + 2][...] = v2
            o += 3

    out_shape = []
    for k in range(ns):
        out_shape += [jax.ShapeDtypeStruct(ws[k].shape, F32)] * 4
    for k in range(nx):
        out_shape += [jax.ShapeDtypeStruct(ws[ns + k].shape, F32)] * 3
    return pl.pallas_call(body, out_shape=out_shape, compiler_params=_params(0), name="adamw_small")(sm, *special_g, *ws, *ms, *vs)


def _place():
    return lax.axis_index("x"), lax.axis_index("y"), lax.axis_index("c")


def _index(px, py, pc):
    return 4 * px + 2 * py + pc


def all_gather(name, shards):
    n = len(shards)
    hbm = pl.BlockSpec(memory_space=pl.ANY)

    def body(*refs):
        ins, outs = refs[:n], refs[n:2 * n]
        send_sems, recv_sems, local_sems = refs[2 * n:]
        x, y, c = _place()
        me, sibling = (x, y, c), (x, y, 1 - c)
        chips = [(1 - x, y), (x, 1 - y), (1 - x, 1 - y)]

        def copy(i, k, block, to, src=None):
            dst = outs[i].at[_index(*block)]
            return pltpu.make_async_remote_copy(src_ref=dst if src is None else src, dst_ref=dst, send_sem=send_sems.at[7 * i + k],
                                                recv_sem=recv_sems.at[7 * i + k], device_id=to, device_id_type=MESH)

        mine = [pltpu.make_async_copy(ins[i], outs[i].at[_index(*me)], local_sems.at[i]) for i in range(n)]
        for cp in mine:
            cp.start()
        first = []
        for i in range(n):
            first += [copy(i, 1 + j, me, (*chip, c), src=ins[i]) for j, chip in enumerate(chips)]
            first.append(copy(i, 0, me, sibling, src=ins[i]))
        for cp in first:
            cp.start()
        passed = []
        for i in range(n):
            for j, chip in enumerate(chips):
                copy(i, 1 + j, (*chip, c), me).wait_recv()
                cp = copy(i, 4 + j, (*chip, c), sibling)
                cp.start()
                passed.append(cp)
        for i in range(n):
            copy(i, 0, sibling, me).wait_recv()
            for j, chip in enumerate(chips):
                copy(i, 4 + j, (*chip, 1 - c), me).wait_recv()
        for cp in first + passed:
            cp.wait_send()
        for cp in mine:
            cp.wait()

    return pl.pallas_call(
        body, in_specs=[hbm] * n, out_specs=[hbm] * n,
        out_shape=[jax.ShapeDtypeStruct((8,) + s.shape, s.dtype) for s in shards],
        scratch_shapes=[pltpu.SemaphoreType.DMA((7 * n,)), pltpu.SemaphoreType.DMA((7 * n,)), pltpu.SemaphoreType.DMA((n,))],
        name=name)(*shards)


HBM_SPEC = pl.BlockSpec(memory_space=pltpu.HBM)
SEM_SPEC = pl.BlockSpec(memory_space=pltpu.SEMAPHORE)
EFFECT = pltpu.SideEffectType.DATAFLOW_SIDE_EFFECTING


def _peers(x, y, c):
    return [((1 - x) if k & 4 else x, (1 - y) if k & 2 else y, (1 - c) if k & 1 else c) for k in range(1, 8)]


def _peer_copy(src, land, send_sems, recv_sems, k, peer, mine, slab_src):
    return pltpu.make_async_remote_copy(src_ref=src.at[_index(*peer)] if slab_src else src, dst_ref=land.at[mine], send_sem=send_sems.at[k],
                                        recv_sem=recv_sems.at[k], device_id=peer, device_id_type=MESH)


def copies_start(name, srcs, slab_src, after):
    n = len(srcs)
    zones = [jax.ShapeDtypeStruct(s.shape if slab_src else (8,) + s.shape, s.dtype) for s in srcs]

    def body(*refs):
        ins, lands = refs[:n], refs[n:2 * n]
        sends, recvs = refs[2 * n + 1:3 * n + 1], refs[3 * n + 1:4 * n + 1]
        token = refs[-1]
        x, y, c = _place()
        mine = _index(x, y, c)
        for i in range(n):
            for k, peer in enumerate(_peers(x, y, c)):
                _peer_copy(ins[i], lands[i], sends[i], recvs[i], k, peer, mine, slab_src).start()
        token[...] = jnp.zeros_like(token)

    sem = pltpu.SemaphoreType.DMA((7,))
    res = pl.pallas_call(
        body, name=name,
        out_shape=([sem] * (2 * n) + [pltpu.HBM(s.shape, s.dtype) for s in srcs] + [pltpu.HBM(z.shape, z.dtype) for z in zones]
                   + [jax.ShapeDtypeStruct((8, 128), F32)]),
        in_specs=[HBM_SPEC] * (2 * n) + [pl.BlockSpec(memory_space=pl.ANY)],
        out_specs=[SEM_SPEC] * (2 * n) + [HBM_SPEC] * (2 * n) + [pl.BlockSpec(memory_space=pltpu.VMEM)],
        input_output_aliases={i: 2 * n + i for i in range(2 * n)},
        compiler_params=pltpu.CompilerParams(has_side_effects=EFFECT),
    )(*[pltpu.with_memory_space_constraint(s, pltpu.HBM) for s in srcs],
      *[pltpu.with_memory_space_constraint(lax.empty(z.shape, z.dtype), pltpu.HBM) for z in zones], after)
    return [(res[i], res[n + i], res[2 * n + i], res[3 * n + i]) for i in range(n)], res[-1][0:1, 0:1]


def copies_wait(name, started, slab_src, after):
    n = len(started)

    def body(*refs):
        ins, lands = refs[:n], refs[n:2 * n]
        sends, recvs = refs[2 * n:3 * n], refs[3 * n:4 * n]
        local_sems = refs[-1]
        x, y, c = _place()
        mine = _index(x, y, c)
        own = [pltpu.make_async_copy(ins[i].at[mine] if slab_src else ins[i], lands[i].at[mine], local_sems.at[i]) for i in range(n)]
        for cp in own:
            cp.start()
        for i in range(n):
            for k, peer in enumerate(_peers(x, y, c)):
                arrival = pltpu.make_async_remote_copy(src_ref=ins[i].at[mine] if slab_src else ins[i], dst_ref=lands[i].at[_index(*peer)],
                                                       send_sem=sends[i].at[k], recv_sem=recvs[i].at[k], device_id=peer, device_id_type=MESH)
                arrival.wait_send()
                arrival.wait_recv()
        for cp in own:
            cp.wait()

    srcs = [s[2] for s in started]
    lands = [s[3] for s in started]
    res = pl.pallas_call(
        body, name=name,
        out_shape=[pltpu.HBM(s.shape, s.dtype) for s in srcs] + [pltpu.HBM(z.shape, z.dtype) for z in lands],
        in_specs=[HBM_SPEC] * (2 * n) + [SEM_SPEC] * (2 * n) + [pl.BlockSpec(memory_space=pl.ANY)],
        out_specs=[HBM_SPEC] * (2 * n),
        input_output_aliases={i: i for i in range(2 * n)},
        scratch_shapes=[pltpu.SemaphoreType.DMA((n,))],
        compiler_params=pltpu.CompilerParams(has_side_effects=EFFECT),
    )(*srcs, *lands, *[s[0] for s in started], *[s[1] for s in started], after)
    return list(res[n:])


WEIGHTS = ["meta_tokens", "norm1_w", "w_in", "ssd_conv_w", "ssd_conv_b", "ssd_dt_bias", "ssd_a_log", "ssd_d", "ssd_norm_w", "lru_conv_w",
           "lru_conv_b", "lru_wa", "lru_ba", "lru_wx", "lru_bx", "lru_lambda", "lru_norm_w", "w_out", "norm2_w", "w_gate", "w_up", "w_down",
           "final_norm_w"]
BIG = ["w_in", "w_out", "w_gate", "w_up", "w_down"]
BIG_ROW_TILE = {"w_in": 256, "w_out": 128, "w_gate": 256, "w_up": 256, "w_down": 176}


def _pair_blocks(w):
    w = w.reshape(8, 2, 64, 64)
    z = jnp.zeros((8, 64, 64), w.dtype)
    return jnp.concatenate([jnp.concatenate([w[:, 0], z], axis=2), jnp.concatenate([z, w[:, 1]], axis=2)], axis=1)


def _unpair_blocks(w2):
    return jnp.stack([w2[:, :64, :64], w2[:, 64:, 64:]], axis=1).reshape(16, 64, 64)


def _per_group(v):
    return jnp.pad(v.reshape(2, 1, 8), ((0, 0), (0, 0), (0, 120)))


def _pad_cols(v, n):
    return jnp.pad(v, ((0, 0), (0, n - v.shape[1])))


def local_step(x, target, meta, ssd_cw, lru_cw, w_in, fetch, send, p):
    z120 = jnp.zeros((D, 120), BF)
    w_p = jnp.concatenate([w_in[:, 0:1024], w_in[:, 2576:3600], w_in[:, 3600:4624], w_in[:, 1024:2560],
                           w_in[:, 2560:2568], z120, w_in[:, 2568:2576], z120, jnp.zeros((D, 256), BF)], axis=1)
    bias2, alog2, d2 = _per_group(p["ssd_dt_bias"]), _per_group(p["ssd_a_log"]), _per_group(p["ssd_d"])
    wa2 = _pair_blocks(p["lru_wa"]).astype(BF)
    wx2 = _pair_blocks(p["lru_wx"]).astype(BF)
    lru = (lru_cw, p["lru_conv_b"], wa2, p["lru_ba"], wx2, p["lru_bx"], p["lru_lambda"])

    h0 = jnp.concatenate([jnp.zeros((NPAD, D), F32), meta, x], axis=0)
    target = jnp.concatenate([jnp.zeros((NPAD + N_META, D), F32), target], axis=0)
    proj, u1 = in_proj(h0, p["norm1_w"], w_p)
    xbc_act = conv_silu_fwd(proj, ssd_cw, p["ssd_conv_b"])
    yn_ssd, y_pre, h_prev = ssd_fwd(xbc_act, proj, bias2, alog2, d2, p["ssd_norm_w"])
    a, u = lru_gates_fwd(proj, *lru)
    hseq = lru_scan_fwd(a, u)
    (w_out,) = fetch(["w_out"], hseq)
    h1, cat = out_proj(yn_ssd, proj, hseq, p["lru_norm_w"], w_out, h0)
    w_gate, w_up = fetch(["w_gate", "w_up"], h1)
    gt, up, act, u2 = gate_up(h1, p["norm2_w"], w_gate, w_up)
    (w_down,) = fetch(["w_down"], act)
    dh2, dh2_b, loss, d_fnw = down_loss(act, w_down, h1, target, p["final_norm_w"])

    dgt, dup = swiglu_bwd(dh2_b, w_down, gt, up)
    g_down = matmul_tn("dw_down", act, dh2_b, 1408, 512)
    g_gate = matmul_tn("dw_gate", u2, dgt, 512, 1408)
    g_up = matmul_tn("dw_up", u2, dup, 512, 1408)
    sent = send({"w_down": g_down, "w_gate": g_gate, "w_up": g_up})
    dh1, dh1_b, d_n2 = gate_up_bwd(dgt, dup, w_gate, w_up, h1, p["norm2_w"] + sent, dh2)
    sent = send({"w_out": matmul_tn("dw_out", cat, dh1_b, 512, 1024)})
    dyn, dh_out, dg_b, d_lnw = out_proj_bwd(dh1_b, w_out, proj, hseq, p["lru_norm_w"] + sent)

    dhs = lru_scan_bwd(a, dh_out)
    dxl_b, d_lcw, d_lcb, dwa2, d_ba, dwx2, d_bx, d_lam = lru_gates_bwd(dhs, hseq, proj, *lru)
    dz_b, dx, d_b, d_c, ddt_b, dpar, d_snw = ssd_bwd(dyn, xbc_act, proj, y_pre, h_prev, bias2, alog2, d2, p["ssd_norm_w"])
    dxbc_b, d_scw, d_scb = conv_silu_bwd(jnp.concatenate([dx, d_b, d_c], axis=1), proj, ssd_cw, p["ssd_conv_b"])
    dproj = jnp.concatenate([dz_b, dg_b, dxl_b, dxbc_b, ddt_b, jnp.zeros((T, 256), BF)], axis=1)
    dh0, d_n1 = in_proj_bwd(dproj, w_p, h0, p["norm1_w"], dh1)
    g_p = matmul_tn("dw_in", u1, dproj, 512, 1024)
    g_in = jnp.concatenate([g_p[:, 0:1024], g_p[:, PXBC:PXBC + XBC], g_p[:, PDT:PDT + 8], g_p[:, PDT + 128:PDT + 136],
                            g_p[:, PG:PG + 1024], g_p[:, PXL:PXL + 1024]], axis=1)
    small = {"norm1_w": d_n1, "ssd_conv_b": d_scb, "ssd_dt_bias": dpar[:, 0, :8].reshape(1, 16), "ssd_a_log": dpar[:, 1, :8].reshape(1, 16),
             "ssd_d": dpar[:, 2, :8].reshape(1, 16), "ssd_norm_w": d_snw, "lru_conv_b": d_lcb, "lru_ba": d_ba, "lru_bx": d_bx,
             "lru_lambda": d_lam, "lru_norm_w": d_lnw, "norm2_w": d_n2, "final_norm_w": d_fnw,
             "lru_wa": _unpair_blocks(dwa2), "lru_wx": _unpair_blocks(dwx2), "meta_tokens": dh0[NPAD:NPAD + N_META],
             "ssd_conv_w": d_scw, "lru_conv_w": d_lcw}
    return loss, dh0[NPAD + N_META:], g_in, small


def _pack_small(small, loss):
    rows = [_pad_cols(small[name], -(-n // 1024) * 1024).reshape(-1, 1024) for name, n in SIMPLE]
    rows += [small["lru_wa"].reshape(64, 1024), small["lru_wx"].reshape(64, 1024), small["meta_tokens"],
             _pad_cols(small["ssd_conv_w"], 2048).reshape(8, 1024), small["lru_conv_w"], _pad_cols(loss[:, 0:1], 1024)]
    sm = jnp.concatenate(rows, axis=0)
    return jnp.pad(sm, ((0, SM_ROWS - sm.shape[0]), (0, 0)))


def _slabs(g, name):
    if name in ("w_in", "w_gate", "w_up"):
        return g.reshape(g.shape[0], 8, g.shape[1] // 8).transpose(1, 0, 2)
    return g.reshape(8, g.shape[0] // 8, g.shape[1])


def _unslab(g, name):
    if name in ("w_in", "w_gate", "w_up"):
        return g.transpose(1, 0, 2).reshape(g.shape[1], 8 * g.shape[2])
    return g.reshape(8 * g.shape[1], g.shape[2])


def kernel(x, meta_tokens, norm1_w, w_in, ssd_conv_w, ssd_conv_b, ssd_dt_bias, ssd_a_log, ssd_d, ssd_norm_w, lru_conv_w, lru_conv_b, lru_wa, lru_ba, lru_wx, lru_bx, lru_lambda, lru_norm_w, w_out, norm2_w, w_gate, w_up, w_down, final_norm_w, loss_target, m_meta_tokens, m_norm1_w, m_w_in, m_ssd_conv_w, m_ssd_conv_b, m_ssd_dt_bias, m_ssd_a_log, m_ssd_d, m_ssd_norm_w, m_lru_conv_w, m_lru_conv_b, m_lru_wa, m_lru_ba, m_lru_wx, m_lru_bx, m_lru_lambda, m_lru_norm_w, m_w_out, m_norm2_w, m_w_gate, m_w_up, m_w_down, m_final_norm_w, v_meta_tokens, v_norm1_w, v_w_in, v_ssd_conv_w, v_ssd_conv_b, v_ssd_dt_bias, v_ssd_a_log, v_ssd_d, v_ssd_norm_w, v_lru_conv_w, v_lru_conv_b, v_lru_wa, v_lru_ba, v_lru_wx, v_lru_bx, v_lru_lambda, v_lru_norm_w, v_w_out, v_norm2_w, v_w_gate, v_w_up, v_w_down, v_final_norm_w):
    w = dict(meta_tokens=meta_tokens, norm1_w=norm1_w, w_in=w_in[0], ssd_conv_w=ssd_conv_w[0], ssd_conv_b=ssd_conv_b, ssd_dt_bias=ssd_dt_bias,
             ssd_a_log=ssd_a_log, ssd_d=ssd_d, ssd_norm_w=ssd_norm_w, lru_conv_w=lru_conv_w[0], lru_conv_b=lru_conv_b, lru_wa=lru_wa[0],
             lru_ba=lru_ba, lru_wx=lru_wx[0], lru_bx=lru_bx, lru_lambda=lru_lambda, lru_norm_w=lru_norm_w, w_out=w_out[0], norm2_w=norm2_w,
             w_gate=w_gate[0], w_up=w_up[0], w_down=w_down[0], final_norm_w=final_norm_w.reshape(1, D))
    m = dict(meta_tokens=m_meta_tokens, norm1_w=m_norm1_w, w_in=m_w_in[0], ssd_conv_w=m_ssd_conv_w[0], ssd_conv_b=m_ssd_conv_b,
             ssd_dt_bias=m_ssd_dt_bias, ssd_a_log=m_ssd_a_log, ssd_d=m_ssd_d, ssd_norm_w=m_ssd_norm_w, lru_conv_w=m_lru_conv_w[0],
             lru_conv_b=m_lru_conv_b, lru_wa=m_lru_wa[0], lru_ba=m_lru_ba, lru_wx=m_lru_wx[0], lru_bx=m_lru_bx, lru_lambda=m_lru_lambda,
             lru_norm_w=m_lru_norm_w, w_out=m_w_out[0], norm2_w=m_norm2_w, w_gate=m_w_gate[0], w_up=m_w_up[0], w_down=m_w_down[0],
             final_norm_w=m_final_norm_w.reshape(1, D))
    v = dict(meta_tokens=v_meta_tokens, norm1_w=v_norm1_w, w_in=v_w_in[0], ssd_conv_w=v_ssd_conv_w[0], ssd_conv_b=v_ssd_conv_b,
             ssd_dt_bias=v_ssd_dt_bias, ssd_a_log=v_ssd_a_log, ssd_d=v_ssd_d, ssd_norm_w=v_ssd_norm_w, lru_conv_w=v_lru_conv_w[0],
             lru_conv_b=v_lru_conv_b, lru_wa=v_lru_wa[0], lru_ba=v_lru_ba, lru_wx=v_lru_wx[0], lru_bx=v_lru_bx, lru_lambda=v_lru_lambda,
             lru_norm_w=v_lru_norm_w, w_out=v_w_out[0], norm2_w=v_norm2_w, w_gate=v_w_gate[0], w_up=v_w_up[0], w_down=v_w_down[0],
             final_norm_w=v_final_norm_w.reshape(1, D))
    shapes = dict(meta_tokens=meta_tokens.shape, norm1_w=norm1_w.shape, w_in=w_in.shape, ssd_conv_w=ssd_conv_w.shape,
                  ssd_conv_b=ssd_conv_b.shape, ssd_dt_bias=ssd_dt_bias.shape, ssd_a_log=ssd_a_log.shape, ssd_d=ssd_d.shape,
                  ssd_norm_w=ssd_norm_w.shape, lru_conv_w=lru_conv_w.shape, lru_conv_b=lru_conv_b.shape, lru_wa=lru_wa.shape,
                  lru_ba=lru_ba.shape, lru_wx=lru_wx.shape, lru_bx=lru_bx.shape, lru_lambda=lru_lambda.shape, lru_norm_w=lru_norm_w.shape,
                  w_out=w_out.shape, norm2_w=norm2_w.shape, w_gate=w_gate.shape, w_up=w_up.shape, w_down=w_down.shape,
                  final_norm_w=final_norm_w.shape)
    me = _index(*_place())

    small_shard = jnp.concatenate([w["meta_tokens"], _pad_cols(w["ssd_conv_w"], 256).reshape(8, 128), w["lru_conv_w"],
                                   jnp.zeros((4, 128), F32)], axis=0)
    g_in, gs = all_gather("gather_w_in", [w["w_in"].astype(BF), small_shard])
    later = ["w_out", "w_gate", "w_up", "w_down"]
    started, behind = copies_start("gather_rest_start", [w[n].astype(BF) for n in later], False, gs)
    started = dict(zip(later, started))
    meta_full = gs[:, 0:16].transpose(1, 0, 2).reshape(N_META, D)
    ssd_cw = gs[:, 16:24].reshape(8, 4, 256)[:, :, :192].transpose(1, 0, 2).reshape(4, XBC)
    lru_cw = gs[:, 24:28].transpose(1, 0, 2).reshape(4, LRU_W)

    def fetch(names, after):
        got = copies_wait("gather_" + names[0] + "_wait", [started[n] for n in names], False, after)
        return [_unslab(g, n) for n, g in zip(names, got)]

    in_flight = {}

    def send(grads, extra=()):
        names = list(grads)
        st, token = copies_start("grads_" + names[0] + "_start", [_slabs(grads[n], n) for n in names] + list(extra), True, grads[names[0]])
        in_flight.update(zip(names + ["small"] * len(extra), st))
        return token

    loss, grad_x, grad_in, small = local_step(x[0], loss_target[0], meta_full, ssd_cw, lru_cw, _unslab(g_in, "w_in"), fetch, send,
                                              {**w, "norm1_w": w["norm1_w"] + behind})
    send({"w_in": grad_in}, [_pack_small(small, loss).reshape(8, SM_ROWS // 8, 1024)])

    out = {}
    early = ["w_down", "w_gate", "w_up", "w_out"]
    recv = dict(zip(early, copies_wait("grads_early_wait", [in_flight[n] for n in early], True, grad_x)))
    for n in early:
        out[n] = adamw_shard("adamw_" + n, recv[n], w[n], m[n], v[n], BIG_ROW_TILE[n])
    recv_in, recv_small = copies_wait("grads_late_wait", [in_flight["w_in"], in_flight["small"]], True, out["w_out"][0])
    out["w_in"] = adamw_shard("adamw_w_in", recv_in, w["w_in"], m["w_in"], v["w_in"], BIG_ROW_TILE["w_in"])
    sm = all_gather("gather_small_grads", [sum_slabs(recv_small)])[0].reshape(SM_ROWS, 1024)
    special_g = [sm[SM_WA:SM_WA + 64].reshape(16, 64, 64), sm[SM_WX:SM_WX + 64].reshape(16, 64, 64),
                 lax.dynamic_slice(sm[SM_META:SM_META + 16], (0, 128 * me), (16, 128)),
                 lax.dynamic_slice(sm[SM_SCW:SM_SCW + 8].reshape(4, 2048), (0, 192 * me), (4, 192)),
                 lax.dynamic_slice(sm[SM_LCW:SM_LCW + 4], (0, 128 * me), (4, 128))]
    names = [n for n, _ in SIMPLE] + SPECIAL
    res = adamw_small(sm, special_g, [w[n] for n in names], [m[n] for n in names], [v[n] for n in names])
    for k, (n, _) in enumerate(SIMPLE):
        out[n] = res[4 * k:4 * k + 4]
    for k, n in enumerate(SPECIAL):
        o = 4 * len(SIMPLE) + 3 * k
        out[n] = [special_g[k]] + list(res[o:o + 3])
    loss_total = sm[SM_LOSS, 0]
    flat = [loss_total, grad_x[None]]
    for k in range(4):
        flat += [out[n][k].reshape(shapes[n]) for n in WEIGHTS]
    return tuple(flat)
```

```python
import math

import jax
import jax.numpy as jnp
from jax import lax
from jax.experimental import pallas as pl
from jax.experimental.pallas import tpu as pltpu

F32 = jnp.float32
BF = jnp.bfloat16

D = 1024
SEQ = 2048
N_META = 16
Q = 128
NPAD = 112
T = NPAD + N_META + SEQ
NCH = T // Q
RC = 544
D_FF = 2816
SSD_W = 1024
LRU_W = 1024
XBC = 1536
IN_COLS = 4624
PZ, PG, PXL, PXBC, PDT = 0, 1024, 2048, 3072, 4608
NP_IN = 5120
EPS = 1e-6
LRU_C = 8.0
VMEM_LIMIT = 56 * 1024 * 1024

ADAM_LR, ADAM_B1, ADAM_B2, ADAM_EPS, ADAM_WD, ADAM_STEP = 0.001, 0.9, 0.999, 1e-08, 0.01, 10

NT_DIMS = (((1,), (1,)), ((), ()))
TN_DIMS = (((0,), (0,)), ((), ()))
MESH = pl.DeviceIdType.MESH


def _params(n_grid=1, limit=VMEM_LIMIT):
    return pltpu.CompilerParams(dimension_semantics=("arbitrary",) * n_grid, vmem_limit_bytes=limit)


def _spec(shape, imap, single=False):
    if single:
        return pl.BlockSpec(shape, imap, pipeline_mode=pl.Buffered(1))
    return pl.BlockSpec(shape, imap)


def _sigmoid(x):
    return 1.0 / (1.0 + jnp.exp(-x))


def _softplus(x):
    return jnp.maximum(x, 0.0) + jnp.log(1.0 + jnp.exp(-jnp.abs(x)))


def _rms_stats(h):
    return lax.rsqrt(jnp.mean(h * h, axis=-1, keepdims=True) + EPS)


def _rms(h, w):
    return (h * _rms_stats(h)) * w


def _rms_bwd(du, h, w):
    r = _rms_stats(h)
    n = h * r
    dn = du * w
    dh = r * (dn - n * jnp.mean(dn * n, axis=-1, keepdims=True))
    return dh, du * n


_G0 = math.sqrt(2.0 / math.pi)


def _gelu(x):
    return 0.5 * x * (1.0 + jnp.tanh(_G0 * (x + 0.044715 * (x * x * x))))


def _gelu_grad(x):
    t = jnp.tanh(_G0 * (x + 0.044715 * (x * x * x)))
    return 0.5 * (1.0 + t) + 0.5 * x * (1.0 - t * t) * (_G0 * (1.0 + 3.0 * 0.044715 * (x * x)))


def _rows(shape, r0=0):
    return lax.broadcasted_iota(jnp.int32, shape, 0) + r0


def _lanes(shape):
    return lax.broadcasted_iota(jnp.int32, shape, 1)


def _shift_down(x, s):
    if s == 0:
        return x
    return jnp.where(_rows(x.shape) >= s, pltpu.roll(x, s, axis=0), 0.0)


def _shift_up(x, s):
    if s == 0:
        return x
    n = x.shape[0]
    return jnp.where(_rows(x.shape) < n - s, pltpu.roll(x, n - s, axis=0), 0.0)


def _conv(x, w, b):
    y = b + w[3:4, :] * x
    for k in range(3):
        y = y + w[k:k + 1, :] * _shift_down(x, 3 - k)
    return y


def _conv_bwd(dy, x, w):
    dx = w[3:4, :] * dy
    dws = []
    for k in range(3):
        dx = dx + w[k:k + 1, :] * _shift_up(dy, 3 - k)
        dws.append(jnp.sum(dy * _shift_down(x, 3 - k), axis=0, keepdims=True))
    dws.append(jnp.sum(dy * x, axis=0, keepdims=True))
    return dx, jnp.concatenate(dws, axis=0), jnp.sum(dy, axis=0, keepdims=True)


HALF = RC // 2


def _col_tiles(n, tn, fn):
    def step(j, carry):
        fn(pl.multiple_of(j * tn, tn))
        return carry

    lax.fori_loop(0, n // tn, step, 0)


def _rows_spec(cols, block_col=0):
    return _spec((RC, cols), lambda i: (i, block_col))


def _whole(shape):
    return _spec(shape, lambda i: tuple(0 for _ in shape), single=True)


def _vec(cols):
    return _spec((1, cols), lambda i: (0, 0))


def _zero_at_first(*refs):
    @pl.when(pl.program_id(0) == 0)
    def _():
        for r in refs:
            r[...] = jnp.zeros_like(r)


def in_proj(h0, wn, w_p):
    def body(h_ref, wn_ref, w_ref, o_ref, u_ref):
        for r in (0, HALF):
            u_ref[r:r + HALF, :] = _rms(h_ref[r:r + HALF, :], wn_ref[...]).astype(BF)

        def tile(c0):
            o_ref[:, pl.ds(c0, 512)] = jnp.dot(u_ref[...], w_ref[:, pl.ds(c0, 512)], preferred_element_type=F32)

        _col_tiles(NP_IN, 512, tile)

    return pl.pallas_call(
        body, grid=(T // RC,), in_specs=[_rows_spec(D), _vec(D), _whole((D, NP_IN))],
        out_specs=[_rows_spec(NP_IN), _rows_spec(D)],
        out_shape=[jax.ShapeDtypeStruct((T, NP_IN), F32), jax.ShapeDtypeStruct((T, D), BF)],
        compiler_params=_params(), name="in_proj")(h0, wn, w_p)


def out_proj(yn_ssd, proj, hseq, lru_nw, w_out, h0):
    def body(y_ref, g_ref, h_ref, wn_ref, w_ref, r_ref, o_ref, cat_ref):
        cat_ref[:, 0:SSD_W] = y_ref[...]
        for r in (0, HALF):
            y = _gelu(g_ref[r:r + HALF, :]) * h_ref[r:r + HALF, :]
            cat_ref[r:r + HALF, SSD_W:] = _rms(y, wn_ref[...]).astype(BF)

        def tile(c0):
            o_ref[:, pl.ds(c0, 512)] = r_ref[:, pl.ds(c0, 512)] + jnp.dot(cat_ref[...], w_ref[:, pl.ds(c0, 512)], preferred_element_type=F32)

        _col_tiles(D, 512, tile)

    return pl.pallas_call(
        body, grid=(T // RC,),
        in_specs=[_rows_spec(SSD_W), _rows_spec(LRU_W, PG // LRU_W), _rows_spec(LRU_W), _vec(LRU_W), _whole((SSD_W + LRU_W, D)), _rows_spec(D)],
        out_specs=[_rows_spec(D), _rows_spec(SSD_W + LRU_W)],
        out_shape=[jax.ShapeDtypeStruct((T, D), F32), jax.ShapeDtypeStruct((T, SSD_W + LRU_W), BF)],
        compiler_params=_params(), name="out_proj")(yn_ssd, proj, hseq, lru_nw, w_out, h0)


def out_proj_bwd(dh1_b, w_out, proj, hseq, lru_nw):
    def body(d_ref, w_ref, g_ref, h_ref, wn_ref, dy_ref, dh_ref, dg_ref, dw_ref, dl_scr):
        _zero_at_first(dw_ref)

        def tile(c0):
            dy_ref[:, pl.ds(c0, 512)] = lax.dot_general(d_ref[...], w_ref[pl.ds(c0, 512), :], NT_DIMS, preferred_element_type=F32)
            dl_scr[:, pl.ds(c0, 512)] = lax.dot_general(d_ref[...], w_ref[pl.ds(SSD_W + c0, 512), :], NT_DIMS, preferred_element_type=F32)

        _col_tiles(SSD_W, 512, tile)
        for r in (0, HALF):
            g = g_ref[r:r + HALF, :]
            h = h_ref[r:r + HALF, :]
            ge = _gelu(g)
            dy, dw = _rms_bwd(dl_scr[r:r + HALF, :], ge * h, wn_ref[...])
            dw_ref[...] += jnp.sum(dw, axis=0, keepdims=True)
            dh_ref[r:r + HALF, :] = dy * ge
            dg_ref[r:r + HALF, :] = (dy * h * _gelu_grad(g)).astype(BF)

    return pl.pallas_call(
        body, grid=(T // RC,),
        in_specs=[_rows_spec(D), _whole((SSD_W + LRU_W, D)), _rows_spec(LRU_W, PG // LRU_W), _rows_spec(LRU_W), _vec(LRU_W)],
        out_specs=[_rows_spec(SSD_W), _rows_spec(LRU_W), _rows_spec(LRU_W), _vec(LRU_W)],
        out_shape=[jax.ShapeDtypeStruct((T, SSD_W), F32), jax.ShapeDtypeStruct((T, LRU_W), F32), jax.ShapeDtypeStruct((T, LRU_W), BF),
                   jax.ShapeDtypeStruct((1, LRU_W), F32)],
        scratch_shapes=[pltpu.VMEM((RC, LRU_W), F32)],
        compiler_params=_params(), name="out_proj_bwd")(dh1_b, w_out, proj, hseq, lru_nw)


def in_proj_bwd(dproj, w_p, h0, wn, dh1):
    def body(d_ref, w_ref, h_ref, wn_ref, r_ref, o_ref, dw_ref, du_scr):
        _zero_at_first(dw_ref)

        def tile(c0):
            du_scr[:, pl.ds(c0, 512)] = lax.dot_general(d_ref[...], w_ref[pl.ds(c0, 512), :], NT_DIMS, preferred_element_type=F32)

        _col_tiles(D, 512, tile)
        for r in (0, HALF):
            dh, dw = _rms_bwd(du_scr[r:r + HALF, :], h_ref[r:r + HALF, :], wn_ref[...])
            dw_ref[...] += jnp.sum(dw, axis=0, keepdims=True)
            o_ref[r:r + HALF, :] = dh + r_ref[r:r + HALF, :]

    return pl.pallas_call(
        body, grid=(T // RC,), in_specs=[_rows_spec(NP_IN), _whole((D, NP_IN)), _rows_spec(D), _vec(D), _rows_spec(D)],
        out_specs=[_rows_spec(D), _vec(D)],
        out_shape=[jax.ShapeDtypeStruct((T, D), F32), jax.ShapeDtypeStruct((1, D), F32)],
        scratch_shapes=[pltpu.VMEM((RC, D), F32)],
        compiler_params=_params(), name="in_proj_bwd")(dproj, w_p, h0, wn, dh1)


def matmul_tn(name, a, b, tm, tn):
    m, n = a.shape[1], b.shape[1]

    def body(a_ref, b_ref, o_ref, acc_ref):
        acc_ref[...] = jnp.zeros_like(acc_ref)

        def mm(r0):
            acc_ref[...] += lax.dot_general(a_ref[pl.ds(r0, RC), :], b_ref[pl.ds(r0, RC), :], TN_DIMS, preferred_element_type=F32)

        _col_tiles(T, RC, mm)
        o_ref[...] = acc_ref[...].astype(BF)

    return pl.pallas_call(
        body, grid=(m // tm, n // tn),
        in_specs=[_spec((T, tm), lambda i, j: (0, i)), _spec((T, tn), lambda i, j: (0, j))],
        out_specs=_spec((tm, tn), lambda i, j: (i, j)),
        out_shape=jax.ShapeDtypeStruct((m, n), BF),
        scratch_shapes=[pltpu.VMEM((tm, tn), F32)],
        compiler_params=_params(2), name=name)(a, b)


def conv_silu_fwd(proj, cw, cb):
    def body(x_ref, w_ref, b_ref, o_ref):
        pre = _conv(x_ref[...], w_ref[...], b_ref[...])
        o_ref[...] = pre * _sigmoid(pre)

    c0 = PXBC // 128
    return pl.pallas_call(
        body, grid=(XBC // 128,),
        in_specs=[_spec((T, 128), lambda c: (0, c0 + c)), _spec((4, 128), lambda c: (0, c)), _spec((1, 128), lambda c: (0, c))],
        out_specs=_spec((T, 128), lambda c: (0, c)),
        out_shape=jax.ShapeDtypeStruct((T, XBC), F32), compiler_params=_params(), name="conv_silu_fwd")(proj, cw, cb)


def conv_silu_bwd(dact, proj, cw, cb):
    def body(d_ref, x_ref, w_ref, b_ref, dx_ref, dw_ref, db_ref):
        x = x_ref[...]
        pre = _conv(x, w_ref[...], b_ref[...])
        sg = _sigmoid(pre)
        dpre = d_ref[...] * (sg * (1.0 + pre * (1.0 - sg)))
        dx, dw, db = _conv_bwd(dpre, x, w_ref[...])
        dx_ref[...] = dx.astype(BF)
        dw_ref[...] = dw
        db_ref[...] = db

    c0 = PXBC // 128
    return pl.pallas_call(
        body, grid=(XBC // 128,),
        in_specs=[_spec((T, 128), lambda c: (0, c)), _spec((T, 128), lambda c: (0, c0 + c)),
                  _spec((4, 128), lambda c: (0, c)), _spec((1, 128), lambda c: (0, c))],
        out_specs=[_spec((T, 128), lambda c: (0, c)), _spec((4, 128), lambda c: (0, c)), _spec((1, 128), lambda c: (0, c))],
        out_shape=[jax.ShapeDtypeStruct((T, XBC), BF), jax.ShapeDtypeStruct((4, XBC), F32), jax.ShapeDtypeStruct((1, XBC), F32)],
        compiler_params=_params(), name="conv_silu_bwd")(dact, proj, cw, cb)


def _ssd_chunk_common(row0, dt_ref, b_ref, c_ref, bias, a_neg):
    shape = (Q, Q)
    lane = _lanes(shape)
    sub = _rows(shape)
    live = (_rows(shape, row0) >= NPAD) & (lane < 8)
    dtr = dt_ref[:, :]
    dt = jnp.where(live, _softplus(dtr + bias), 0.0)
    d_a = dt * a_neg
    tri = (sub >= lane).astype(F32)
    cs = jnp.dot(tri, d_a, precision=lax.Precision.HIGHEST, preferred_element_type=F32)
    cs_t = cs.T
    bc = b_ref[:, :].astype(BF)
    cc = c_ref[:, :].astype(BF)
    cb = lax.dot_general(cc, bc, NT_DIMS, preferred_element_type=F32)
    cs_last = cs[Q - 1:Q, :]
    return dict(lane=lane, sub=sub, live=live, dtr=dtr, dt=dt, cs=cs, cs_t=cs_t, bc=bc, cc=cc, cb=cb,
                ecs=jnp.exp(cs), dsm=jnp.exp(cs_last - cs), gam=jnp.exp(cs_last), tri=tri)


def _pair(lane_even, mat, j):
    return jnp.where(lane_even, mat[:, j:j + 1], mat[:, j + 1:j + 2])


def _head_decay(cm, j):
    seg = cm["cs"][:, j:j + 1] - cm["cs_t"][j:j + 1, :]
    return jnp.exp(jnp.where(cm["sub"] >= cm["lane"], seg, -jnp.inf))


def ssd_fwd(xbc_act, proj, dt_bias2, a_log2, d2, norm_w):
    def body(x_ref, b_ref, c_ref, dt_ref, z_ref, bias_ref, alog_ref, d_ref, nw_ref, yn_ref, y_ref, hp_ref, h_scr):
        c = pl.program_id(1)

        @pl.when(c == 0)
        def _():
            h_scr[...] = jnp.zeros_like(h_scr)

        bias = bias_ref[0]
        a_neg = -jnp.exp(alog_ref[0])
        dsk = d_ref[0]
        cm = _ssd_chunk_common(c * Q, dt_ref, b_ref, c_ref, bias, a_neg)
        lane_even = cm["lane"] < 64
        sub_even = cm["sub"] < 64
        for p in range(4):
            je, jo = 2 * p, 2 * p + 1
            xp = x_ref[:, 128 * p:128 * p + 128]
            xdt = xp * _pair(lane_even, cm["dt"], je)
            xdt_b = xdt.astype(BF)
            m_e = (cm["cb"] * _head_decay(cm, je)).astype(BF)
            m_o = (cm["cb"] * _head_decay(cm, jo)).astype(BF)
            zero = jnp.zeros_like(xdt_b)
            yd = (jnp.dot(m_e, jnp.where(lane_even, xdt_b, zero), preferred_element_type=F32)
                  + jnp.dot(m_o, jnp.where(lane_even, zero, xdt_b), preferred_element_type=F32))
            hp = h_scr[p]
            hp_ref[0, 0, p] = hp
            yo = lax.dot_general(cm["cc"], hp.astype(BF), NT_DIMS, preferred_element_type=F32) * _pair(lane_even, cm["ecs"], je)
            dsk_p = jnp.where(lane_even[0:1, :], dsk[:, je:je + 1], dsk[:, jo:jo + 1])
            y_ref[:, 128 * p:128 * p + 128] = yd + yo + xp * dsk_p
            st = lax.dot_general((xdt * _pair(lane_even, cm["dsm"], je)).astype(BF), cm["bc"], TN_DIMS, preferred_element_type=F32)
            gam = jnp.where(sub_even[:, 0:1], cm["gam"][:, je:je + 1], cm["gam"][:, jo:jo + 1])
            h_scr[p] = hp * gam + st
        zc = z_ref[:, :]
        gated = y_ref[:, :] * (zc * _sigmoid(zc))
        yn_ref[:, :] = _rms(gated, nw_ref[...]).astype(BF)

    par = _spec((1, 1, 128), lambda g, c: (g, 0, 0))
    wide = _spec((Q, 512), lambda g, c: (c, g))
    return pl.pallas_call(
        body, grid=(2, NCH),
        in_specs=[wide, _spec((Q, 128), lambda g, c: (c, 8 + g)), _spec((Q, 128), lambda g, c: (c, 10 + g)),
                  _spec((Q, 128), lambda g, c: (c, PDT // 128 + g)), wide, par, par, par, _spec((1, 512), lambda g, c: (0, g))],
        out_specs=[wide, wide, _spec((1, 1, 4, 128, 128), lambda g, c: (g, c, 0, 0, 0))],
        out_shape=[jax.ShapeDtypeStruct((T, SSD_W), BF), jax.ShapeDtypeStruct((T, SSD_W), F32),
                   jax.ShapeDtypeStruct((2, NCH, 4, 128, 128), F32)],
        scratch_shapes=[pltpu.VMEM((4, 128, 128), F32)],
        compiler_params=_params(2), name="ssd_fwd")(xbc_act, xbc_act, xbc_act, proj, proj, dt_bias2, a_log2, d2, norm_w)


def ssd_bwd(dyn, xbc_act, proj, y_pre, h_prev, dt_bias2, a_log2, d2, norm_w):
    def body(dyn_ref, x_ref, b_ref, c_ref, dt_ref, z_ref, y_ref, hp_ref, bias_ref, alog_ref, d_ref, nw_ref,
             dz_ref, dx_ref, db_ref, dc_ref, ddt_ref, dpar_ref, dnw_ref, dh_scr, acc_scr):
        ci = pl.program_id(1)

        @pl.when(ci == 0)
        def _():
            dh_scr[...] = jnp.zeros_like(dh_scr)
            acc_scr[...] = jnp.zeros_like(acc_scr)
            dnw_ref[...] = jnp.zeros_like(dnw_ref)

        bias = bias_ref[0]
        a_neg = -jnp.exp(alog_ref[0])
        dsk = d_ref[0]
        cm = _ssd_chunk_common((NCH - 1 - ci) * Q, dt_ref, b_ref, c_ref, bias, a_neg)
        lane, sub = cm["lane"], cm["sub"]
        lane_even = lane < 64
        sub_even = sub < 64
        zc = z_ref[:, :]
        yc = y_ref[:, :]
        sg = _sigmoid(zc)
        sz = zc * sg
        dgated, dnw = _rms_bwd(dyn_ref[:, :], yc * sz, nw_ref[...])
        dnw_ref[...] += jnp.sum(dnw, axis=0, keepdims=True)
        dz_ref[:, :] = (dgated * yc * (sg * (1.0 + zc * (1.0 - sg)))).astype(BF)
        dy_all = dgated * sz
        dcb = jnp.zeros((Q, Q), F32)
        db_acc = jnp.zeros((Q, Q), F32)
        dc_acc = jnp.zeros((Q, Q), F32)
        dcs_col = jnp.zeros((Q, Q), F32)
        dcs_row = jnp.zeros((Q, Q), F32)
        ddt = jnp.zeros((Q, Q), F32)
        for p in range(4):
            je, jo = 2 * p, 2 * p + 1
            xp = x_ref[:, 128 * p:128 * p + 128]
            dy = dy_all[:, 128 * p:128 * p + 128]
            dt_p = _pair(lane_even, cm["dt"], je)
            xdt = xp * dt_p
            xdt_b = xdt.astype(BF)
            dy_b = dy.astype(BF)
            zero = jnp.zeros_like(dy_b)
            hp = hp_ref[0, 0, p]
            hp_b = hp.astype(BF)
            dh = dh_scr[p]
            dh_b = dh.astype(BF)
            acc_scr[p:p + 1, :] += jnp.sum(dy * xp, axis=0, keepdims=True)
            dsk_p = jnp.where(lane_even[0:1, :], dsk[:, je:je + 1], dsk[:, jo:jo + 1])
            dxp = dy * dsk_p
            e_p = _pair(lane_even, cm["ecs"], je)
            g_p = lax.dot_general(cm["cc"], hp_b, NT_DIMS, preferred_element_type=F32)
            dg_b = (dy * e_p).astype(BF)
            de = dy * g_p * e_p
            dc_acc = dc_acc + jnp.dot(dg_b, hp_b, preferred_element_type=F32)
            dh_in = lax.dot_general(dg_b, cm["cc"], TN_DIMS, preferred_element_type=F32)
            ds_p = _pair(lane_even, cm["dsm"], je)
            r_p = lax.dot_general(cm["bc"], dh_b, NT_DIMS, preferred_element_type=F32)
            dxdt = r_p * ds_p
            tt = r_p * xdt * ds_p
            db_acc = db_acc + jnp.dot((xdt * ds_p).astype(BF), dh_b, preferred_element_type=F32)
            dgam_m = dh * hp
            for j, even in ((je, True), (jo, False)):
                sel = lane_even if even else jnp.logical_not(lane_even)
                ssel = sub_even if even else jnp.logical_not(sub_even)
                l_j = _head_decay(cm, j)
                m_j = cm["cb"] * l_j
                dm = lax.dot_general(jnp.where(sel, dy_b, zero), xdt_b, NT_DIMS, preferred_element_type=F32)
                dxdt = dxdt + lax.dot_general(m_j.astype(BF), jnp.where(sel, dy_b, zero), TN_DIMS, preferred_element_type=F32)
                w_j = dm * m_j
                dcb = dcb + dm * l_j
                t_j = jnp.sum(jnp.where(sel, tt, 0.0), axis=1, keepdims=True)
                col = (jnp.sum(w_j, axis=1, keepdims=True) + jnp.sum(jnp.where(sel, de, 0.0), axis=1, keepdims=True) - t_j)
                gam_j = cm["gam"][:, j:j + 1]
                last = (jnp.sum(t_j, axis=0, keepdims=True)
                        + jnp.sum(jnp.sum(jnp.where(ssel, dgam_m, 0.0), axis=1, keepdims=True), axis=0, keepdims=True) * gam_j)
                col = col + jnp.where(sub[:, 0:1] == Q - 1, last, 0.0)
                dcs_col = dcs_col + jnp.where(lane == j, col, 0.0)
                dcs_row = dcs_row + jnp.where(sub == j, jnp.sum(w_j, axis=0, keepdims=True), 0.0)
            gam = jnp.where(sub_even[:, 0:1], cm["gam"][:, je:je + 1], cm["gam"][:, jo:jo + 1])
            dh_scr[p] = dh_in + dh * gam
            dx_ref[:, 128 * p:128 * p + 128] = dxp + dxdt * dt_p
            dd = dxdt * xp
            ddt = ddt + jnp.where(lane == je, jnp.sum(jnp.where(lane_even, dd, 0.0), axis=1, keepdims=True), 0.0)
            ddt = ddt + jnp.where(lane == jo, jnp.sum(jnp.where(lane_even, 0.0, dd), axis=1, keepdims=True), 0.0)
        dcb_b = dcb.astype(BF)
        dc_ref[:, :] = dc_acc + jnp.dot(dcb_b, cm["bc"], preferred_element_type=F32)
        db_ref[:, :] = db_acc + lax.dot_general(dcb_b, cm["cc"], TN_DIMS, preferred_element_type=F32)
        dcs = dcs_col - dcs_row.T
        dd_a = lax.dot_general(cm["tri"], dcs, TN_DIMS, precision=lax.Precision.HIGHEST, preferred_element_type=F32)
        ddt = ddt + dd_a * a_neg
        acc_scr[5:6, :] += jnp.sum(dd_a * cm["dt"], axis=0, keepdims=True)
        draw = jnp.where(cm["live"], ddt * _sigmoid(cm["dtr"] + bias), 0.0)
        acc_scr[4:5, :] += jnp.sum(draw, axis=0, keepdims=True)
        ddt_ref[:, :] = draw.astype(BF)

        @pl.when(ci == NCH - 1)
        def _():
            lane1 = _lanes((1, 128))
            dd = jnp.zeros((1, 128), F32)
            for p in range(4):
                row = acc_scr[p:p + 1, :]
                dd = dd + jnp.where(lane1 == 2 * p, jnp.sum(jnp.where(lane1 < 64, row, 0.0), axis=1, keepdims=True), 0.0)
                dd = dd + jnp.where(lane1 == 2 * p + 1, jnp.sum(jnp.where(lane1 < 64, 0.0, row), axis=1, keepdims=True), 0.0)
            dpar_ref[0] = jnp.concatenate([acc_scr[4:5, :], acc_scr[5:6, :] * a_neg, dd, jnp.zeros((5, 128), F32)], axis=0)

    par = _spec((1, 1, 128), lambda g, c: (g, 0, 0))
    wide = _spec((Q, 512), lambda g, c: (NCH - 1 - c, g))
    thin = _spec((Q, 128), lambda g, c: (NCH - 1 - c, g))
    return pl.pallas_call(
        body, grid=(2, NCH),
        in_specs=[wide, wide, _spec((Q, 128), lambda g, c: (NCH - 1 - c, 8 + g)), _spec((Q, 128), lambda g, c: (NCH - 1 - c, 10 + g)),
                  _spec((Q, 128), lambda g, c: (NCH - 1 - c, PDT // 128 + g)), wide, wide,
                  _spec((1, 1, 4, 128, 128), lambda g, c: (g, NCH - 1 - c, 0, 0, 0)), par, par, par, _spec((1, 512), lambda g, c: (0, g))],
        out_specs=[wide, wide, thin, thin, thin, _spec((1, 8, 128), lambda g, c: (g, 0, 0)), _spec((1, 512), lambda g, c: (0, g))],
        out_shape=[jax.ShapeDtypeStruct((T, SSD_W), BF), jax.ShapeDtypeStruct((T, SSD_W), F32), jax.ShapeDtypeStruct((T, 256), F32),
                   jax.ShapeDtypeStruct((T, 256), F32), jax.ShapeDtypeStruct((T, 256), BF), jax.ShapeDtypeStruct((2, 8, 128), F32),
                   jax.ShapeDtypeStruct((1, SSD_W), F32)],
        scratch_shapes=[pltpu.VMEM((4, 128, 128), F32), pltpu.VMEM((8, 128), F32)],
        compiler_params=_params(2), name="ssd_bwd")(dyn, xbc_act, xbc_act, xbc_act, proj, proj, y_pre, h_prev, dt_bias2, a_log2, d2, norm_w)


def _lru_gates(xl, cw, cb, wa, ba, wx, bx, lam):
    xr = _conv(xl, cw, cb)
    xr_b = xr.astype(BF)
    r = _sigmoid(jnp.dot(xr_b, wa, preferred_element_type=F32) + ba)
    i = _sigmoid(jnp.dot(xr_b, wx, preferred_element_type=F32) + bx)
    sp = _softplus(-lam)
    la = (-LRU_C) * r * sp
    a = jnp.exp(la)
    mult = jnp.sqrt(-jnp.tanh(la) * (a * a + 1.0))
    return xr, xr_b, r, i, sp, a, mult


def lru_gates_fwd(proj, cw, cb, wa2, ba, wx2, bx, lam):
    def body(x_ref, cw_ref, cb_ref, wa_ref, ba_ref, wx_ref, bx_ref, lam_ref, a_ref, u_ref):
        xr, _, _, i, _, a, mult = _lru_gates(x_ref[...], cw_ref[...], cb_ref[...], wa_ref[0], ba_ref[...], wx_ref[0], bx_ref[...], lam_ref[...])
        a_ref[...] = a
        u_ref[...] = jnp.where(_rows(a.shape) >= NPAD, mult * (i * xr), 0.0)

    c0 = PXL // 128
    vec = _spec((1, 128), lambda c: (0, c))
    mat = _spec((1, 128, 128), lambda c: (c, 0, 0))
    return pl.pallas_call(
        body, grid=(8,),
        in_specs=[_spec((T, 128), lambda c: (0, c0 + c)), _spec((4, 128), lambda c: (0, c)), vec, mat, vec, mat, vec, vec],
        out_specs=[_spec((T, 128), lambda c: (0, c)), _spec((T, 128), lambda c: (0, c))],
        out_shape=[jax.ShapeDtypeStruct((T, LRU_W), F32), jax.ShapeDtypeStruct((T, LRU_W), F32)],
        compiler_params=_params(), name="lru_gates_fwd")(proj, cw, cb, wa2, ba, wx2, bx, lam)


def lru_scan_fwd(a, u):
    def body(a_ref, u_ref, h_ref):
        def step(i, h):
            base = pl.multiple_of(i * 8, 8)
            for k in range(8):
                h = a_ref[pl.ds(base + k, 1), :] * h + u_ref[pl.ds(base + k, 1), :]
                h_ref[pl.ds(base + k, 1), :] = h
            return h

        lax.fori_loop(0, T // 8, step, jnp.zeros((1, LRU_W), F32))

    return pl.pallas_call(body, out_shape=jax.ShapeDtypeStruct((T, LRU_W), F32), compiler_params=_params(0), name="lru_scan_fwd")(a, u)


def lru_scan_bwd(a, dh_out):
    def body(a_ref, d_ref, o_ref):
        def step(i, carry):
            base = pl.multiple_of(T - 8 - i * 8, 8)
            for k in range(7, -1, -1):
                carry = d_ref[pl.ds(base + k, 1), :] + carry
                o_ref[pl.ds(base + k, 1), :] = carry
                carry = carry * a_ref[pl.ds(base + k, 1), :]
            return carry

        lax.fori_loop(0, T // 8, step, jnp.zeros((1, LRU_W), F32))

    return pl.pallas_call(body, out_shape=jax.ShapeDtypeStruct((T, LRU_W), F32), compiler_params=_params(0), name="lru_scan_bwd")(a, dh_out)


def lru_gates_bwd(dhs, hseq, proj, cw, cb, wa2, ba, wx2, bx, lam):
    def body(dh_ref, h_ref, x_ref, cw_ref, cb_ref, wa_ref, ba_ref, wx_ref, bx_ref, lam_ref,
             dx_ref, dcw_ref, dcb_ref, dwa_ref, dba_ref, dwx_ref, dbx_ref, dlam_ref):
        xl = x_ref[...]
        lam = lam_ref[...]
        xr, xr_b, r, i, sp, a, mult = _lru_gates(xl, cw_ref[...], cb_ref[...], wa_ref[0], ba_ref[...], wx_ref[0], bx_ref[...], lam)
        dh = dh_ref[...]
        da = dh * _shift_down(h_ref[...], 1)
        du = jnp.where(_rows(dh.shape) >= NPAD, dh, 0.0)
        dmult = du * (i * xr)
        di = du * (mult * xr)
        dxr = du * (mult * i)
        dla = da * a - dmult * (a * a) / mult
        dr = dla * ((-LRU_C) * sp)
        dsp = jnp.sum(dla * ((-LRU_C) * r), axis=0, keepdims=True)
        dlam_ref[...] = -dsp * _sigmoid(-lam)
        dpr = dr * r * (1.0 - r)
        dpi = di * i * (1.0 - i)
        dba_ref[...] = jnp.sum(dpr, axis=0, keepdims=True)
        dbx_ref[...] = jnp.sum(dpi, axis=0, keepdims=True)
        dpr_b = dpr.astype(BF)
        dpi_b = dpi.astype(BF)
        dxr = (dxr + lax.dot_general(dpr_b, wa_ref[0], NT_DIMS, preferred_element_type=F32)
               + lax.dot_general(dpi_b, wx_ref[0], NT_DIMS, preferred_element_type=F32))
        dwa_ref[0] = lax.dot_general(xr_b, dpr_b, TN_DIMS, preferred_element_type=F32)
        dwx_ref[0] = lax.dot_general(xr_b, dpi_b, TN_DIMS, preferred_element_type=F32)
        dx, dcw, dcb = _conv_bwd(dxr, xl, cw_ref[...])
        dx_ref[...] = dx.astype(BF)
        dcw_ref[...] = dcw
        dcb_ref[...] = dcb

    c0 = PXL // 128
    vec = _spec((1, 128), lambda c: (0, c))
    mat = _spec((1, 128, 128), lambda c: (c, 0, 0))
    col = _spec((T, 128), lambda c: (0, c))
    vshape = jax.ShapeDtypeStruct((1, LRU_W), F32)
    mshape = jax.ShapeDtypeStruct((8, 128, 128), F32)
    return pl.pallas_call(
        body, grid=(8,),
        in_specs=[col, col, _spec((T, 128), lambda c: (0, c0 + c)), _spec((4, 128), lambda c: (0, c)), vec, mat, vec, mat, vec, vec],
        out_specs=[col, _spec((4, 128), lambda c: (0, c)), vec, mat, vec, mat, vec, vec],
        out_shape=[jax.ShapeDtypeStruct((T, LRU_W), BF), jax.ShapeDtypeStruct((4, LRU_W), F32), vshape, mshape, vshape, mshape, vshape, vshape],
        compiler_params=_params(), name="lru_gates_bwd")(dhs, hseq, proj, cw, cb, wa2, ba, wx2, bx, lam)


def gate_up(h1, wn, w_gate, w_up):
    def body(h_ref, wn_ref, wg_ref, wu_ref, gt_ref, up_ref, act_ref, u_ref):
        for r in (0, HALF):
            u_ref[r:r + HALF, :] = _rms(h_ref[r:r + HALF, :], wn_ref[...]).astype(BF)

        def tile(c0):
            cols = pl.ds(c0, 256)
            gt = jnp.dot(u_ref[...], wg_ref[:, cols], preferred_element_type=F32)
            up = jnp.dot(u_ref[...], wu_ref[:, cols], preferred_element_type=F32)
            gt_ref[:, cols] = gt.astype(BF)
            up_ref[:, cols] = up.astype(BF)
            act_ref[:, cols] = (gt * _sigmoid(gt) * up).astype(BF)

        _col_tiles(D_FF, 256, tile)

    big = jax.ShapeDtypeStruct((T, D_FF), BF)
    return pl.pallas_call(
        body, grid=(T // RC,), in_specs=[_rows_spec(D), _vec(D), _whole((D, D_FF)), _whole((D, D_FF))],
        out_specs=[_rows_spec(D_FF), _rows_spec(D_FF), _rows_spec(D_FF), _rows_spec(D)],
        out_shape=[big, big, big, jax.ShapeDtypeStruct((T, D), BF)],
        compiler_params=_params(), name="gate_up")(h1, wn, w_gate, w_up)


def down_loss(act, w_down, h1, target, wf):
    def body(a_ref, w_ref, r_ref, t_ref, wf_ref, d_ref, db_ref, l_ref, dw_ref, h_scr):
        _zero_at_first(l_ref, dw_ref)

        def tile(c0):
            cols = pl.ds(c0, 512)
            h_scr[:, cols] = r_ref[:, cols] + jnp.dot(a_ref[...], w_ref[:, cols], preferred_element_type=F32)

        _col_tiles(D, 512, tile)
        for r in (0, HALF):
            h = h_scr[r:r + HALF, :]
            live = _rows((HALF, D), pl.program_id(0) * RC + r) >= NPAD + N_META
            err = jnp.where(live, _rms(h, wf_ref[...]) - t_ref[r:r + HALF, :], 0.0)
            l_ref[...] += 0.5 * jnp.sum(jnp.sum(err * err, axis=1, keepdims=True) * (1.0 / D), axis=0, keepdims=True)
            dh, dw = _rms_bwd(err * (1.0 / D), h, wf_ref[...])
            dw_ref[...] += jnp.sum(dw, axis=0, keepdims=True)
            d_ref[r:r + HALF, :] = dh
            db_ref[r:r + HALF, :] = dh.astype(BF)

    return pl.pallas_call(
        body, grid=(T // RC,), in_specs=[_rows_spec(D_FF), _whole((D_FF, D)), _rows_spec(D), _rows_spec(D), _vec(D)],
        out_specs=[_rows_spec(D), _rows_spec(D), _spec((1, 128), lambda i: (0, 0)), _vec(D)],
        out_shape=[jax.ShapeDtypeStruct((T, D), F32), jax.ShapeDtypeStruct((T, D), BF), jax.ShapeDtypeStruct((1, 128), F32),
                   jax.ShapeDtypeStruct((1, D), F32)],
        scratch_shapes=[pltpu.VMEM((RC, D), F32)],
        compiler_params=_params(), name="down_loss")(act, w_down, h1, target, wf)


def swiglu_bwd(dh2_b, w_down, gt, up):
    def body(d_ref, w_ref, gt_ref, up_ref, dg_ref, du_ref):
        def tile(c0):
            cols = pl.ds(c0, 256)
            dact = lax.dot_general(d_ref[...], w_ref[cols, :], NT_DIMS, preferred_element_type=F32)
            gt_ = gt_ref[:, cols].astype(F32)
            up_ = up_ref[:, cols].astype(F32)
            sg = _sigmoid(gt_)
            dg_ref[:, cols] = (dact * up_ * (sg * (1.0 + gt_ * (1.0 - sg)))).astype(BF)
            du_ref[:, cols] = (dact * (gt_ * sg)).astype(BF)

        _col_tiles(D_FF, 256, tile)

    big = jax.ShapeDtypeStruct((T, D_FF), BF)
    return pl.pallas_call(
        body, grid=(T // RC,), in_specs=[_rows_spec(D), _whole((D_FF, D)), _rows_spec(D_FF), _rows_spec(D_FF)],
        out_specs=[_rows_spec(D_FF), _rows_spec(D_FF)], out_shape=[big, big], compiler_params=_params(), name="swiglu_bwd")(dh2_b, w_down, gt, up)


def gate_up_bwd(dgt, dup, w_gate, w_up, h1, wn, dh2):
    def body(dg_ref, du_ref, wg_ref, wu_ref, h_ref, wn_ref, r_ref, d_ref, db_ref, dw_ref, du_scr):
        _zero_at_first(dw_ref)

        def tile(c0):
            rows = pl.ds(c0, 512)
            du_scr[:, rows] = (lax.dot_general(dg_ref[...], wg_ref[rows, :], NT_DIMS, preferred_element_type=F32)
                               + lax.dot_general(du_ref[...], wu_ref[rows, :], NT_DIMS, preferred_element_type=F32))

        _col_tiles(D, 512, tile)
        for r in (0, HALF):
            dh, dw = _rms_bwd(du_scr[r:r + HALF, :], h_ref[r:r + HALF, :], wn_ref[...])
            dw_ref[...] += jnp.sum(dw, axis=0, keepdims=True)
            dh = dh + r_ref[r:r + HALF, :]
            d_ref[r:r + HALF, :] = dh
            db_ref[r:r + HALF, :] = dh.astype(BF)

    return pl.pallas_call(
        body, grid=(T // RC,),
        in_specs=[_rows_spec(D_FF), _rows_spec(D_FF), _whole((D, D_FF)), _whole((D, D_FF)), _rows_spec(D), _vec(D), _rows_spec(D)],
        out_specs=[_rows_spec(D), _rows_spec(D), _vec(D)],
        out_shape=[jax.ShapeDtypeStruct((T, D), F32), jax.ShapeDtypeStruct((T, D), BF), jax.ShapeDtypeStruct((1, D), F32)],
        scratch_shapes=[pltpu.VMEM((RC, D), F32)],
        compiler_params=_params(), name="gate_up_bwd")(dgt, dup, w_gate, w_up, h1, wn, dh2)


def _adamw(w, g, m, v):
    m = ADAM_B1 * m + (1.0 - ADAM_B1) * g
    v = ADAM_B2 * v + (1.0 - ADAM_B2) * (g * g)
    m_hat = m / (1.0 - ADAM_B1 ** ADAM_STEP)
    v_hat = v / (1.0 - ADAM_B2 ** ADAM_STEP)
    delta = -ADAM_LR * (m_hat / (jnp.sqrt(v_hat) + ADAM_EPS) + ADAM_WD * w)
    return delta, m, v


def adamw_shard(name, recv, w, m, v, tr):
    r, c = w.shape

    def body(p_ref, w_ref, m_ref, v_ref, g_ref, d_ref, mo_ref, vo_ref):
        g = p_ref[0].astype(F32)
        for s in range(1, 8):
            g = g + p_ref[s].astype(F32)
        g_ref[...] = g
        d_ref[...], mo_ref[...], vo_ref[...] = _adamw(w_ref[...], g, m_ref[...], v_ref[...])

    tile = _spec((tr, c), lambda i: (i, 0))
    shape = jax.ShapeDtypeStruct((r, c), F32)
    return pl.pallas_call(
        body, grid=(r // tr,), in_specs=[_spec((8, tr, c), lambda i: (0, i, 0)), tile, tile, tile],
        out_specs=[tile] * 4, out_shape=[shape] * 4, compiler_params=_params(), name=name)(recv, w, m, v)


def sum_slabs(recv):
    def body(p_ref, o_ref):
        g = p_ref[0]
        for s in range(1, 8):
            g = g + p_ref[s]
        o_ref[...] = g

    return pl.pallas_call(body, out_shape=jax.ShapeDtypeStruct(recv.shape[1:], F32), compiler_params=_params(0), name="sum_slabs")(recv)


SIMPLE = [("norm1_w", 1024), ("ssd_conv_b", 1536), ("ssd_dt_bias", 16), ("ssd_a_log", 16), ("ssd_d", 16), ("ssd_norm_w", 1024),
          ("lru_conv_b", 1024), ("lru_ba", 1024), ("lru_bx", 1024), ("lru_lambda", 1024), ("lru_norm_w", 1024), ("norm2_w", 1024),
          ("final_norm_w", 1024)]
SPECIAL = ["lru_wa", "lru_wx", "meta_tokens", "ssd_conv_w", "lru_conv_w"]
SM_ROWS = 176
SM_WA, SM_WX, SM_META, SM_SCW, SM_LCW, SM_LOSS = 14, 78, 142, 158, 166, 170


def _simple_rows():
    rows, r = {}, 0
    for name, n in SIMPLE:
        rows[name] = r
        r += -(-n // 1024)
    return rows


def adamw_small(sm, special_g, ws, ms, vs):
    rows = _simple_rows()
    ns, nx = len(SIMPLE), len(SPECIAL)

    def body(*refs):
        sm_ref = refs[0]
        gx = refs[1:1 + nx]
        wr = refs[1 + nx:1 + nx + ns + nx]
        mr = refs[1 + nx + ns + nx:1 + nx + 2 * (ns + nx)]
        vr = refs[1 + nx + 2 * (ns + nx):1 + nx + 3 * (ns + nx)]
        outs = refs[1 + nx + 3 * (ns + nx):]
        o = 0
        for k, (name, n) in enumerate(SIMPLE):
            r0 = rows[name]
            for c0 in range(0, n, 1024):
                wd = min(1024, n - c0)
                g = sm_ref[r0 + c0 // 1024:r0 + c0 // 1024 + 1, 0:wd]
                sl = (slice(None), slice(c0, c0 + wd))
                d, m2, v2 = _adamw(wr[k][sl], g, mr[k][sl], vr[k][sl])
                outs[o][sl] = g
                outs[o + 1][sl] = d
                outs[o + 2][sl] = m2
                outs[o + 3][sl] = v2
            o += 4
        for k in range(nx):
            d, m2, v2 = _adamw(wr[ns + k][...], gx[k][...], mr[ns + k][...], vr[ns + k][...])
            outs[o][...] = d
            outs[o + 1][...] = m2
            outs[o + 2][...] = v2
            o += 3

    out_shape = []
    for k in range(ns):
        out_shape += [jax.ShapeDtypeStruct(ws[k].shape, F32)] * 4
    for k in range(nx):
        out_shape += [jax.ShapeDtypeStruct(ws[ns + k].shape, F32)] * 3
    return pl.pallas_call(body, out_shape=out_shape, compiler_params=_params(0), name="adamw_small")(sm, *special_g, *ws, *ms, *vs)


def _place():
    return lax.axis_index("x"), lax.axis_index("y"), lax.axis_index("c")


def _index(px, py, pc):
    return 4 * px + 2 * py + pc


def all_gather(name, shards):
    n = len(shards)
    hbm = pl.BlockSpec(memory_space=pl.ANY)

    def body(*refs):
        ins, outs = refs[:n], refs[n:2 * n]
        send_sems, recv_sems, local_sems = refs[2 * n:]
        x, y, c = _place()
        me, sibling = (x, y, c), (x, y, 1 - c)
        chips = [(1 - x, y), (x, 1 - y), (1 - x, 1 - y)]

        def copy(i, k, block, to, src=None):
            dst = outs[i].at[_index(*block)]
            return pltpu.make_async_remote_copy(src_ref=dst if src is None else src, dst_ref=dst, send_sem=send_sems.at[7 * i + k],
                                                recv_sem=recv_sems.at[7 * i + k], device_id=to, device_id_type=MESH)

        mine = [pltpu.make_async_copy(ins[i], outs[i].at[_index(*me)], local_sems.at[i]) for i in range(n)]
        for cp in mine:
            cp.start()
        first = []
        for i in range(n):
            first += [copy(i, 1 + j, me, (*chip, c), src=ins[i]) for j, chip in enumerate(chips)]
            first.append(copy(i, 0, me, sibling, src=ins[i]))
        for cp in first:
            cp.start()
        passed = []
        for i in range(n):
            for j, chip in enumerate(chips):
                copy(i, 1 + j, (*chip, c), me).wait_recv()
                cp = copy(i, 4 + j, (*chip, c), sibling)
                cp.start()
                passed.append(cp)
        for i in range(n):
            copy(i, 0, sibling, me).wait_recv()
            for j, chip in enumerate(chips):
                copy(i, 4 + j, (*chip, 1 - c), me).wait_recv()
        for cp in first + passed:
            cp.wait_send()
        for cp in mine:
            cp.wait()

    return pl.pallas_call(
        body, in_specs=[hbm] * n, out_specs=[hbm] * n,
        out_shape=[jax.ShapeDtypeStruct((8,) + s.shape, s.dtype) for s in shards],
        scratch_shapes=[pltpu.SemaphoreType.DMA((7 * n,)), pltpu.SemaphoreType.DMA((7 * n,)), pltpu.SemaphoreType.DMA((n,))],
        name=name)(*shards)


HBM_SPEC = pl.BlockSpec(memory_space=pltpu.HBM)
SEM_SPEC = pl.BlockSpec(memory_space=pltpu.SEMAPHORE)
EFFECT = pltpu.SideEffectType.DATAFLOW_SIDE_EFFECTING


def _peers(x, y, c):
    return [((1 - x) if k & 4 else x, (1 - y) if k & 2 else y, (1 - c) if k & 1 else c) for k in range(1, 8)]


def _pieces(rows):
    for n in (4, 2):
        if rows % (16 * n) == 0:
            return [(r * (rows // n), rows // n) for r in range(n)]
    return [(0, rows)]


def _peer_copies(src, land, send_sems, recv_sems, k, peer, mine, slab_src):
    block = src.at[_index(*peer)] if slab_src else src
    return [pltpu.make_async_remote_copy(src_ref=block.at[pl.ds(r0, nr)], dst_ref=land.at[mine, pl.ds(r0, nr)], send_sem=send_sems.at[k],
                                         recv_sem=recv_sems.at[k], device_id=peer, device_id_type=MESH)
            for r0, nr in _pieces(block.shape[0])]


def copies_start(name, srcs, slab_src, after):
    n = len(srcs)
    zones = [jax.ShapeDtypeStruct(s.shape if slab_src else (8,) + s.shape, s.dtype) for s in srcs]

    def body(*refs):
        ins, lands = refs[:n], refs[n:2 * n]
        sends, recvs = refs[2 * n + 1:3 * n + 1], refs[3 * n + 1:4 * n + 1]
        token = refs[-1]
        x, y, c = _place()
        mine = _index(x, y, c)
        for i in range(n):
            per_peer = [_peer_copies(ins[i], lands[i], sends[i], recvs[i], k, peer, mine, slab_src) for k, peer in enumerate(_peers(x, y, c))]
            for piece in zip(*per_peer):
                for cp in piece:
                    cp.start()
        token[...] = jnp.zeros_like(token)

    sem = pltpu.SemaphoreType.DMA((7,))
    res = pl.pallas_call(
        body, name=name,
        out_shape=([sem] * (2 * n) + [pltpu.HBM(s.shape, s.dtype) for s in srcs] + [pltpu.HBM(z.shape, z.dtype) for z in zones]
                   + [jax.ShapeDtypeStruct((8, 128), F32)]),
        in_specs=[HBM_SPEC] * (2 * n) + [pl.BlockSpec(memory_space=pl.ANY)],
        out_specs=[SEM_SPEC] * (2 * n) + [HBM_SPEC] * (2 * n) + [pl.BlockSpec(memory_space=pltpu.VMEM)],
        input_output_aliases={i: 2 * n + i for i in range(2 * n)},
        compiler_params=pltpu.CompilerParams(has_side_effects=EFFECT),
    )(*[pltpu.with_memory_space_constraint(s, pltpu.HBM) for s in srcs],
      *[pltpu.with_memory_space_constraint(lax.empty(z.shape, z.dtype), pltpu.HBM) for z in zones], after)
    return [(res[i], res[n + i], res[2 * n + i], res[3 * n + i]) for i in range(n)], res[-1][0:1, 0:1]


def copies_wait(name, started, slab_src, after):
    n = len(started)

    def body(*refs):
        ins, lands = refs[:n], refs[n:2 * n]
        sends, recvs = refs[2 * n:3 * n], refs[3 * n:4 * n]
        local_sems = refs[-1]
        x, y, c = _place()
        mine = _index(x, y, c)
        own = [pltpu.make_async_copy(ins[i].at[mine] if slab_src else ins[i], lands[i].at[mine], local_sems.at[i]) for i in range(n)]
        for cp in own:
            cp.start()
        for i in range(n):
            for k, peer in enumerate(_peers(x, y, c)):
                arrival = pltpu.make_async_remote_copy(src_ref=ins[i].at[mine] if slab_src else ins[i], dst_ref=lands[i].at[_index(*peer)],
                                                       send_sem=sends[i].at[k], recv_sem=recvs[i].at[k], device_id=peer, device_id_type=MESH)
                arrival.wait_send()
                arrival.wait_recv()
        for cp in own:
            cp.wait()

    srcs = [s[2] for s in started]
    lands = [s[3] for s in started]
    res = pl.pallas_call(
        body, name=name,
        out_shape=[pltpu.HBM(s.shape, s.dtype) for s in srcs] + [pltpu.HBM(z.shape, z.dtype) for z in lands],
        in_specs=[HBM_SPEC] * (2 * n) + [SEM_SPEC] * (2 * n) + [pl.BlockSpec(memory_space=pl.ANY)],
        out_specs=[HBM_SPEC] * (2 * n),
        input_output_aliases={i: i for i in range(2 * n)},
        scratch_shapes=[pltpu.SemaphoreType.DMA((n,))],
        compiler_params=pltpu.CompilerParams(has_side_effects=EFFECT),
    )(*srcs, *lands, *[s[0] for s in started], *[s[1] for s in started], after)
    return list(res[n:])


WEIGHTS = ["meta_tokens", "norm1_w", "w_in", "ssd_conv_w", "ssd_conv_b", "ssd_dt_bias", "ssd_a_log", "ssd_d", "ssd_norm_w", "lru_conv_w",
           "lru_conv_b", "lru_wa", "lru_ba", "lru_wx", "lru_bx", "lru_lambda", "lru_norm_w", "w_out", "norm2_w", "w_gate", "w_up", "w_down",
           "final_norm_w"]
BIG = ["w_in", "w_out", "w_gate", "w_up", "w_down"]
BIG_ROW_TILE = {"w_in": 256, "w_out": 128, "w_gate": 256, "w_up": 256, "w_down": 176}


def _pair_blocks(w):
    w = w.reshape(8, 2, 64, 64)
    z = jnp.zeros((8, 64, 64), w.dtype)
    return jnp.concatenate([jnp.concatenate([w[:, 0], z], axis=2), jnp.concatenate([z, w[:, 1]], axis=2)], axis=1)


def _unpair_blocks(w2):
    return jnp.stack([w2[:, :64, :64], w2[:, 64:, 64:]], axis=1).reshape(16, 64, 64)


def _per_group(v):
    return jnp.pad(v.reshape(2, 1, 8), ((0, 0), (0, 0), (0, 120)))


def _pad_cols(v, n):
    return jnp.pad(v, ((0, 0), (0, n - v.shape[1])))


def local_step(x, target, meta, ssd_cw, lru_cw, w_in, fetch, send, p):
    z120 = jnp.zeros((D, 120), BF)
    w_p = jnp.concatenate([w_in[:, 0:1024], w_in[:, 2576:3600], w_in[:, 3600:4624], w_in[:, 1024:2560],
                           w_in[:, 2560:2568], z120, w_in[:, 2568:2576], z120, jnp.zeros((D, 256), BF)], axis=1)
    bias2, alog2, d2 = _per_group(p["ssd_dt_bias"]), _per_group(p["ssd_a_log"]), _per_group(p["ssd_d"])
    wa2 = _pair_blocks(p["lru_wa"]).astype(BF)
    wx2 = _pair_blocks(p["lru_wx"]).astype(BF)
    lru = (lru_cw, p["lru_conv_b"], wa2, p["lru_ba"], wx2, p["lru_bx"], p["lru_lambda"])

    h0 = jnp.concatenate([jnp.zeros((NPAD, D), F32), meta, x], axis=0)
    target = jnp.concatenate([jnp.zeros((NPAD + N_META, D), F32), target], axis=0)
    proj, u1 = in_proj(h0, p["norm1_w"], w_p)
    xbc_act = conv_silu_fwd(proj, ssd_cw, p["ssd_conv_b"])
    yn_ssd, y_pre, h_prev = ssd_fwd(xbc_act, proj, bias2, alog2, d2, p["ssd_norm_w"])
    a, u = lru_gates_fwd(proj, *lru)
    hseq = lru_scan_fwd(a, u)
    (w_out,) = fetch(["w_out"], hseq)
    h1, cat = out_proj(yn_ssd, proj, hseq, p["lru_norm_w"], w_out, h0)
    w_gate, w_up = fetch(["w_gate", "w_up"], h1)
    gt, up, act, u2 = gate_up(h1, p["norm2_w"], w_gate, w_up)
    (w_down,) = fetch(["w_down"], act)
    dh2, dh2_b, loss, d_fnw = down_loss(act, w_down, h1, target, p["final_norm_w"])

    dgt, dup = swiglu_bwd(dh2_b, w_down, gt, up)
    g_down = matmul_tn("dw_down", act, dh2_b, 1408, 512)
    g_gate = matmul_tn("dw_gate", u2, dgt, 512, 1408)
    g_up = matmul_tn("dw_up", u2, dup, 512, 1408)
    sent = send({"w_down": g_down, "w_gate": g_gate, "w_up": g_up})
    dh1, dh1_b, d_n2 = gate_up_bwd(dgt, dup, w_gate, w_up, h1, p["norm2_w"] + sent, dh2)
    sent = send({"w_out": matmul_tn("dw_out", cat, dh1_b, 512, 1024)})
    dyn, dh_out, dg_b, d_lnw = out_proj_bwd(dh1_b, w_out, proj, hseq, p["lru_norm_w"] + sent)

    dhs = lru_scan_bwd(a, dh_out)
    dxl_b, d_lcw, d_lcb, dwa2, d_ba, dwx2, d_bx, d_lam = lru_gates_bwd(dhs, hseq, proj, *lru)
    dz_b, dx, d_b, d_c, ddt_b, dpar, d_snw = ssd_bwd(dyn, xbc_act, proj, y_pre, h_prev, bias2, alog2, d2, p["ssd_norm_w"])
    dxbc_b, d_scw, d_scb = conv_silu_bwd(jnp.concatenate([dx, d_b, d_c], axis=1), proj, ssd_cw, p["ssd_conv_b"])
    dproj = jnp.concatenate([dz_b, dg_b, dxl_b, dxbc_b, ddt_b, jnp.zeros((T, 256), BF)], axis=1)
    g_p = matmul_tn("dw_in", u1, dproj, 512, 1024)
    g_in = jnp.concatenate([g_p[:, 0:1024], g_p[:, PXBC:PXBC + XBC], g_p[:, PDT:PDT + 8], g_p[:, PDT + 128:PDT + 136],
                            g_p[:, PG:PG + 1024], g_p[:, PXL:PXL + 1024]], axis=1)
    sent = send({"w_in": g_in})
    dh0, d_n1 = in_proj_bwd(dproj, w_p, h0, p["norm1_w"] + sent, dh1)
    small = {"norm1_w": d_n1, "ssd_conv_b": d_scb, "ssd_dt_bias": dpar[:, 0, :8].reshape(1, 16), "ssd_a_log": dpar[:, 1, :8].reshape(1, 16),
             "ssd_d": dpar[:, 2, :8].reshape(1, 16), "ssd_norm_w": d_snw, "lru_conv_b": d_lcb, "lru_ba": d_ba, "lru_bx": d_bx,
             "lru_lambda": d_lam, "lru_norm_w": d_lnw, "norm2_w": d_n2, "final_norm_w": d_fnw,
             "lru_wa": _unpair_blocks(dwa2), "lru_wx": _unpair_blocks(dwx2), "meta_tokens": dh0[NPAD:NPAD + N_META],
             "ssd_conv_w": d_scw, "lru_conv_w": d_lcw}
    return loss, dh0[NPAD + N_META:], small


def _pack_small(small, loss):
    rows = [_pad_cols(small[name], -(-n // 1024) * 1024).reshape(-1, 1024) for name, n in SIMPLE]
    rows += [small["lru_wa"].reshape(64, 1024), small["lru_wx"].reshape(64, 1024), small["meta_tokens"],
             _pad_cols(small["ssd_conv_w"], 2048).reshape(8, 1024), small["lru_conv_w"], _pad_cols(loss[:, 0:1], 1024)]
    sm = jnp.concatenate(rows, axis=0)
    return jnp.pad(sm, ((0, SM_ROWS - sm.shape[0]), (0, 0)))


def _slabs(g, name):
    if name in ("w_in", "w_gate", "w_up"):
        return g.reshape(g.shape[0], 8, g.shape[1] // 8).transpose(1, 0, 2)
    return g.reshape(8, g.shape[0] // 8, g.shape[1])


def _unslab(g, name):
    if name in ("w_in", "w_gate", "w_up"):
        return g.transpose(1, 0, 2).reshape(g.shape[1], 8 * g.shape[2])
    return g.reshape(8 * g.shape[1], g.shape[2])


def kernel(x, meta_tokens, norm1_w, w_in, ssd_conv_w, ssd_conv_b, ssd_dt_bias, ssd_a_log, ssd_d, ssd_norm_w, lru_conv_w, lru_conv_b, lru_wa, lru_ba, lru_wx, lru_bx, lru_lambda, lru_norm_w, w_out, norm2_w, w_gate, w_up, w_down, final_norm_w, loss_target, m_meta_tokens, m_norm1_w, m_w_in, m_ssd_conv_w, m_ssd_conv_b, m_ssd_dt_bias, m_ssd_a_log, m_ssd_d, m_ssd_norm_w, m_lru_conv_w, m_lru_conv_b, m_lru_wa, m_lru_ba, m_lru_wx, m_lru_bx, m_lru_lambda, m_lru_norm_w, m_w_out, m_norm2_w, m_w_gate, m_w_up, m_w_down, m_final_norm_w, v_meta_tokens, v_norm1_w, v_w_in, v_ssd_conv_w, v_ssd_conv_b, v_ssd_dt_bias, v_ssd_a_log, v_ssd_d, v_ssd_norm_w, v_lru_conv_w, v_lru_conv_b, v_lru_wa, v_lru_ba, v_lru_wx, v_lru_bx, v_lru_lambda, v_lru_norm_w, v_w_out, v_norm2_w, v_w_gate, v_w_up, v_w_down, v_final_norm_w):
    w = dict(meta_tokens=meta_tokens, norm1_w=norm1_w, w_in=w_in[0], ssd_conv_w=ssd_conv_w[0], ssd_conv_b=ssd_conv_b, ssd_dt_bias=ssd_dt_bias,
             ssd_a_log=ssd_a_log, ssd_d=ssd_d, ssd_norm_w=ssd_norm_w, lru_conv_w=lru_conv_w[0], lru_conv_b=lru_conv_b, lru_wa=lru_wa[0],
             lru_ba=lru_ba, lru_wx=lru_wx[0], lru_bx=lru_bx, lru_lambda=lru_lambda, lru_norm_w=lru_norm_w, w_out=w_out[0], norm2_w=norm2_w,
             w_gate=w_gate[0], w_up=w_up[0], w_down=w_down[0], final_norm_w=final_norm_w.reshape(1, D))
    m = dict(meta_tokens=m_meta_tokens, norm1_w=m_norm1_w, w_in=m_w_in[0], ssd_conv_w=m_ssd_conv_w[0], ssd_conv_b=m_ssd_conv_b,
             ssd_dt_bias=m_ssd_dt_bias, ssd_a_log=m_ssd_a_log, ssd_d=m_ssd_d, ssd_norm_w=m_ssd_norm_w, lru_conv_w=m_lru_conv_w[0],
             lru_conv_b=m_lru_conv_b, lru_wa=m_lru_wa[0], lru_ba=m_lru_ba, lru_wx=m_lru_wx[0], lru_bx=m_lru_bx, lru_lambda=m_lru_lambda,
             lru_norm_w=m_lru_norm_w, w_out=m_w_out[0], norm2_w=m_norm2_w, w_gate=m_w_gate[0], w_up=m_w_up[0], w_down=m_w_down[0],
             final_norm_w=m_final_norm_w.reshape(1, D))
    v = dict(meta_tokens=v_meta_tokens, norm1_w=v_norm1_w, w_in=v_w_in[0], ssd_conv_w=v_ssd_conv_w[0], ssd_conv_b=v_ssd_conv_b,
             ssd_dt_bias=v_ssd_dt_bias, ssd_a_log=v_ssd_a_log, ssd_d=v_ssd_d, ssd_norm_w=v_ssd_norm_w, lru_conv_w=v_lru_conv_w[0],
             lru_conv_b=v_lru_conv_b, lru_wa=v_lru_wa[0], lru_ba=v_lru_ba, lru_wx=v_lru_wx[0], lru_bx=v_lru_bx, lru_lambda=v_lru_lambda,
             lru_norm_w=v_lru_norm_w, w_out=v_w_out[0], norm2_w=v_norm2_w, w_gate=v_w_gate[0], w_up=v_w_up[0], w_down=v_w_down[0],
             final_norm_w=v_final_norm_w.reshape(1, D))
    shapes = dict(meta_tokens=meta_tokens.shape, norm1_w=norm1_w.shape, w_in=w_in.shape, ssd_conv_w=ssd_conv_w.shape,
                  ssd_conv_b=ssd_conv_b.shape, ssd_dt_bias=ssd_dt_bias.shape, ssd_a_log=ssd_a_log.shape, ssd_d=ssd_d.shape,
                  ssd_norm_w=ssd_norm_w.shape, lru_conv_w=lru_conv_w.shape, lru_conv_b=lru_conv_b.shape, lru_wa=lru_wa.shape,
                  lru_ba=lru_ba.shape, lru_wx=lru_wx.shape, lru_bx=lru_bx.shape, lru_lambda=lru_lambda.shape, lru_norm_w=lru_norm_w.shape,
                  w_out=w_out.shape, norm2_w=norm2_w.shape, w_gate=w_gate.shape, w_up=w_up.shape, w_down=w_down.shape,
                  final_norm_w=final_norm_w.shape)
    me = _index(*_place())

    small_shard = jnp.concatenate([w["meta_tokens"], _pad_cols(w["ssd_conv_w"], 256).reshape(8, 128), w["lru_conv_w"],
                                   jnp.zeros((4, 128), F32)], axis=0)
    g_in, gs = all_gather("gather_w_in", [w["w_in"].astype(BF), small_shard])
    later = ["w_out", "w_gate", "w_up", "w_down"]
    started, behind = copies_start("gather_rest_start", [w[n].astype(BF) for n in later], False, gs)
    started = dict(zip(later, started))
    meta_full = gs[:, 0:16].transpose(1, 0, 2).reshape(N_META, D)
    ssd_cw = gs[:, 16:24].reshape(8, 4, 256)[:, :, :192].transpose(1, 0, 2).reshape(4, XBC)
    lru_cw = gs[:, 24:28].transpose(1, 0, 2).reshape(4, LRU_W)

    def fetch(names, after):
        got = copies_wait("gather_" + names[0] + "_wait", [started[n] for n in names], False, after)
        return [_unslab(g, n) for n, g in zip(names, got)]

    in_flight = {}

    def send(grads):
        names = list(grads)
        st, token = copies_start("grads_" + names[0] + "_start", [grads[n] if n == "small" else _slabs(grads[n], n) for n in names], True,
                                 grads[names[0]])
        in_flight.update(zip(names, st))
        return token

    loss, grad_x, small = local_step(x[0], loss_target[0], meta_full, ssd_cw, lru_cw, _unslab(g_in, "w_in"), fetch, send,
                                     {**w, "norm1_w": w["norm1_w"] + behind})
    send({"small": _pack_small(small, loss).reshape(8, SM_ROWS // 8, 1024)})

    out = {}
    early = ["w_down", "w_gate", "w_up", "w_out"]
    recv = dict(zip(early, copies_wait("grads_early_wait", [in_flight[n] for n in early], True, in_flight["small"][2])))
    for n in early:
        out[n] = adamw_shard("adamw_" + n, recv[n], w[n], m[n], v[n], BIG_ROW_TILE[n])
    recv_in, recv_small = copies_wait("grads_late_wait", [in_flight["w_in"], in_flight["small"]], True, out["w_out"][0])
    out["w_in"] = adamw_shard("adamw_w_in", recv_in, w["w_in"], m["w_in"], v["w_in"], BIG_ROW_TILE["w_in"])
    sm = all_gather("gather_small_grads", [sum_slabs(recv_small)])[0].reshape(SM_ROWS, 1024)
    special_g = [sm[SM_WA:SM_WA + 64].reshape(16, 64, 64), sm[SM_WX:SM_WX + 64].reshape(16, 64, 64),
                 lax.dynamic_slice(sm[SM_META:SM_META + 16], (0, 128 * me), (16, 128)),
                 lax.dynamic_slice(sm[SM_SCW:SM_SCW + 8].reshape(4, 2048), (0, 192 * me), (4, 192)),
                 lax.dynamic_slice(sm[SM_LCW:SM_LCW + 4], (0, 128 * me), (4, 128))]
    names = [n for n, _ in SIMPLE] + SPECIAL
    res = adamw_small(sm, special_g, [w[n] for n in names], [m[n] for n in names], [v[n] for n in names])
    for k, (n, _) in enumerate(SIMPLE):
        out[n] = res[4 * k:4 * k + 4]
    for k, n in enumerate(SPECIAL):
        o = 4 * len(SIMPLE) + 3 * k
        out[n] = [special_g[k]] + list(res[o:o + 3])
    loss_total = sm[SM_LOSS, 0]
    flat = [loss_total, grad_x[None]]
    for k in range(4):
        flat += [out[n][k].reshape(shapes[n]) for n in WEIGHTS]
    return tuple(flat)
```

```python
import math

import jax
import jax.numpy as jnp
from jax import lax
from jax.experimental import pallas as pl
from jax.experimental.pallas import tpu as pltpu

F32 = jnp.float32
BF = jnp.bfloat16

D = 1024
SEQ = 2048
N_META = 16
Q = 128
NPAD = 112
T = NPAD + N_META + SEQ
NCH = T // Q
RC = 544
D_FF = 2816
SSD_W = 1024
LRU_W = 1024
XBC = 1536
IN_COLS = 4624
PZ, PG, PXL, PXBC, PDT = 0, 1024, 2048, 3072, 4608
NP_IN = 5120
EPS = 1e-6
LRU_C = 8.0
VMEM_LIMIT = 56 * 1024 * 1024

ADAM_LR, ADAM_B1, ADAM_B2, ADAM_EPS, ADAM_WD, ADAM_STEP = 0.001, 0.9, 0.999, 1e-08, 0.01, 10

NT_DIMS = (((1,), (1,)), ((), ()))
TN_DIMS = (((0,), (0,)), ((), ()))
MESH = pl.DeviceIdType.MESH


def _params(n_grid=1, limit=VMEM_LIMIT):
    return pltpu.CompilerParams(dimension_semantics=("arbitrary",) * n_grid, vmem_limit_bytes=limit)


def _spec(shape, imap, single=False):
    if single:
        return pl.BlockSpec(shape, imap, pipeline_mode=pl.Buffered(1))
    return pl.BlockSpec(shape, imap)


def _sigmoid(x):
    return 1.0 / (1.0 + jnp.exp(-x))


def _softplus(x):
    return jnp.maximum(x, 0.0) + jnp.log(1.0 + jnp.exp(-jnp.abs(x)))


def _rms_stats(h):
    return lax.rsqrt(jnp.mean(h * h, axis=-1, keepdims=True) + EPS)


def _rms(h, w):
    return (h * _rms_stats(h)) * w


def _rms_bwd(du, h, w):
    r = _rms_stats(h)
    n = h * r
    dn = du * w
    dh = r * (dn - n * jnp.mean(dn * n, axis=-1, keepdims=True))
    return dh, du * n


_G0 = math.sqrt(2.0 / math.pi)


def _gelu(x):
    return 0.5 * x * (1.0 + jnp.tanh(_G0 * (x + 0.044715 * (x * x * x))))


def _gelu_grad(x):
    t = jnp.tanh(_G0 * (x + 0.044715 * (x * x * x)))
    return 0.5 * (1.0 + t) + 0.5 * x * (1.0 - t * t) * (_G0 * (1.0 + 3.0 * 0.044715 * (x * x)))


def _rows(shape, r0=0):
    return lax.broadcasted_iota(jnp.int32, shape, 0) + r0


def _lanes(shape):
    return lax.broadcasted_iota(jnp.int32, shape, 1)


def _shift_down(x, s):
    if s == 0:
        return x
    return jnp.where(_rows(x.shape) >= s, pltpu.roll(x, s, axis=0), 0.0)


def _shift_up(x, s):
    if s == 0:
        return x
    n = x.shape[0]
    return jnp.where(_rows(x.shape) < n - s, pltpu.roll(x, n - s, axis=0), 0.0)


def _conv(x, w, b):
    y = b + w[3:4, :] * x
    for k in range(3):
        y = y + w[k:k + 1, :] * _shift_down(x, 3 - k)
    return y


def _conv_bwd(dy, x, w):
    dx = w[3:4, :] * dy
    dws = []
    for k in range(3):
        dx = dx + w[k:k + 1, :] * _shift_up(dy, 3 - k)
        dws.append(jnp.sum(dy * _shift_down(x, 3 - k), axis=0, keepdims=True))
    dws.append(jnp.sum(dy * x, axis=0, keepdims=True))
    return dx, jnp.concatenate(dws, axis=0), jnp.sum(dy, axis=0, keepdims=True)


HALF = RC // 2


def _col_tiles(n, tn, fn):
    def step(j, carry):
        fn(pl.multiple_of(j * tn, tn))
        return carry

    lax.fori_loop(0, n // tn, step, 0)


def _rows_spec(cols, block_col=0):
    return _spec((RC, cols), lambda i: (i, block_col))


def _whole(shape):
    return _spec(shape, lambda i: tuple(0 for _ in shape), single=True)


def _vec(cols):
    return _spec((1, cols), lambda i: (0, 0))


def _zero_at_first(*refs):
    @pl.when(pl.program_id(0) == 0)
    def _():
        for r in refs:
            r[...] = jnp.zeros_like(r)


def in_proj(h0, wn, w_p):
    def body(h_ref, wn_ref, w_ref, o_ref, u_ref):
        for r in (0, HALF):
            u_ref[r:r + HALF, :] = _rms(h_ref[r:r + HALF, :], wn_ref[...]).astype(BF)

        def tile(c0):
            o_ref[:, pl.ds(c0, 512)] = jnp.dot(u_ref[...], w_ref[:, pl.ds(c0, 512)], preferred_element_type=F32)

        _col_tiles(NP_IN, 512, tile)

    return pl.pallas_call(
        body, grid=(T // RC,), in_specs=[_rows_spec(D), _vec(D), _whole((D, NP_IN))],
        out_specs=[_rows_spec(NP_IN), _rows_spec(D)],
        out_shape=[jax.ShapeDtypeStruct((T, NP_IN), F32), jax.ShapeDtypeStruct((T, D), BF)],
        compiler_params=_params(), name="in_proj")(h0, wn, w_p)


def out_proj(yn_ssd, proj, hseq, lru_nw, w_out, h0):
    def body(y_ref, g_ref, h_ref, wn_ref, w_ref, r_ref, o_ref, cat_ref):
        cat_ref[:, 0:SSD_W] = y_ref[...]
        for r in (0, HALF):
            y = _gelu(g_ref[r:r + HALF, :]) * h_ref[r:r + HALF, :]
            cat_ref[r:r + HALF, SSD_W:] = _rms(y, wn_ref[...]).astype(BF)

        def tile(c0):
            o_ref[:, pl.ds(c0, 512)] = r_ref[:, pl.ds(c0, 512)] + jnp.dot(cat_ref[...], w_ref[:, pl.ds(c0, 512)], preferred_element_type=F32)

        _col_tiles(D, 512, tile)

    return pl.pallas_call(
        body, grid=(T // RC,),
        in_specs=[_rows_spec(SSD_W), _rows_spec(LRU_W, PG // LRU_W), _rows_spec(LRU_W), _vec(LRU_W), _whole((SSD_W + LRU_W, D)), _rows_spec(D)],
        out_specs=[_rows_spec(D), _rows_spec(SSD_W + LRU_W)],
        out_shape=[jax.ShapeDtypeStruct((T, D), F32), jax.ShapeDtypeStruct((T, SSD_W + LRU_W), BF)],
        compiler_params=_params(), name="out_proj")(yn_ssd, proj, hseq, lru_nw, w_out, h0)


def out_proj_bwd(dh1_b, w_out, proj, hseq, lru_nw):
    def body(d_ref, w_ref, g_ref, h_ref, wn_ref, dy_ref, dh_ref, dg_ref, dw_ref, dl_scr):
        _zero_at_first(dw_ref)

        def tile(c0):
            dy_ref[:, pl.ds(c0, 512)] = lax.dot_general(d_ref[...], w_ref[pl.ds(c0, 512), :], NT_DIMS, preferred_element_type=F32)
            dl_scr[:, pl.ds(c0, 512)] = lax.dot_general(d_ref[...], w_ref[pl.ds(SSD_W + c0, 512), :], NT_DIMS, preferred_element_type=F32)

        _col_tiles(SSD_W, 512, tile)
        for r in (0, HALF):
            g = g_ref[r:r + HALF, :]
            h = h_ref[r:r + HALF, :]
            ge = _gelu(g)
            dy, dw = _rms_bwd(dl_scr[r:r + HALF, :], ge * h, wn_ref[...])
            dw_ref[...] += jnp.sum(dw, axis=0, keepdims=True)
            dh_ref[r:r + HALF, :] = dy * ge
            dg_ref[r:r + HALF, :] = (dy * h * _gelu_grad(g)).astype(BF)

    return pl.pallas_call(
        body, grid=(T // RC,),
        in_specs=[_rows_spec(D), _whole((SSD_W + LRU_W, D)), _rows_spec(LRU_W, PG // LRU_W), _rows_spec(LRU_W), _vec(LRU_W)],
        out_specs=[_rows_spec(SSD_W), _rows_spec(LRU_W), _rows_spec(LRU_W), _vec(LRU_W)],
        out_shape=[jax.ShapeDtypeStruct((T, SSD_W), F32), jax.ShapeDtypeStruct((T, LRU_W), F32), jax.ShapeDtypeStruct((T, LRU_W), BF),
                   jax.ShapeDtypeStruct((1, LRU_W), F32)],
        scratch_shapes=[pltpu.VMEM((RC, LRU_W), F32)],
        compiler_params=_params(), name="out_proj_bwd")(dh1_b, w_out, proj, hseq, lru_nw)


def in_proj_bwd(dproj, w_p, h0, wn, dh1):
    def body(d_ref, w_ref, h_ref, wn_ref, r_ref, o_ref, dw_ref, du_scr):
        _zero_at_first(dw_ref)

        def tile(c0):
            du_scr[:, pl.ds(c0, 512)] = lax.dot_general(d_ref[...], w_ref[pl.ds(c0, 512), :], NT_DIMS, preferred_element_type=F32)

        _col_tiles(D, 512, tile)
        for r in (0, HALF):
            dh, dw = _rms_bwd(du_scr[r:r + HALF, :], h_ref[r:r + HALF, :], wn_ref[...])
            dw_ref[...] += jnp.sum(dw, axis=0, keepdims=True)
            o_ref[r:r + HALF, :] = dh + r_ref[r:r + HALF, :]

    return pl.pallas_call(
        body, grid=(T // RC,), in_specs=[_rows_spec(NP_IN), _whole((D, NP_IN)), _rows_spec(D), _vec(D), _rows_spec(D)],
        out_specs=[_rows_spec(D), _vec(D)],
        out_shape=[jax.ShapeDtypeStruct((T, D), F32), jax.ShapeDtypeStruct((1, D), F32)],
        scratch_shapes=[pltpu.VMEM((RC, D), F32)],
        compiler_params=_params(), name="in_proj_bwd")(dproj, w_p, h0, wn, dh1)


def matmul_tn(name, a, b, tm, tn):
    m, n = a.shape[1], b.shape[1]

    def body(a_ref, b_ref, o_ref, acc_ref):
        acc_ref[...] = jnp.zeros_like(acc_ref)

        def mm(r0):
            acc_ref[...] += lax.dot_general(a_ref[pl.ds(r0, RC), :], b_ref[pl.ds(r0, RC), :], TN_DIMS, preferred_element_type=F32)

        _col_tiles(T, RC, mm)
        o_ref[...] = acc_ref[...].astype(BF)

    return pl.pallas_call(
        body, grid=(m // tm, n // tn),
        in_specs=[_spec((T, tm), lambda i, j: (0, i)), _spec((T, tn), lambda i, j: (0, j))],
        out_specs=_spec((tm, tn), lambda i, j: (i, j)),
        out_shape=jax.ShapeDtypeStruct((m, n), BF),
        scratch_shapes=[pltpu.VMEM((tm, tn), F32)],
        compiler_params=_params(2), name=name)(a, b)


def conv_silu_fwd(proj, cw, cb):
    def body(x_ref, w_ref, b_ref, o_ref):
        pre = _conv(x_ref[...], w_ref[...], b_ref[...])
        o_ref[...] = pre * _sigmoid(pre)

    c0 = PXBC // 128
    return pl.pallas_call(
        body, grid=(XBC // 128,),
        in_specs=[_spec((T, 128), lambda c: (0, c0 + c)), _spec((4, 128), lambda c: (0, c)), _spec((1, 128), lambda c: (0, c))],
        out_specs=_spec((T, 128), lambda c: (0, c)),
        out_shape=jax.ShapeDtypeStruct((T, XBC), F32), compiler_params=_params(), name="conv_silu_fwd")(proj, cw, cb)


def conv_silu_bwd(dact, proj, cw, cb):
    def body(d_ref, x_ref, w_ref, b_ref, dx_ref, dw_ref, db_ref):
        x = x_ref[...]
        pre = _conv(x, w_ref[...], b_ref[...])
        sg = _sigmoid(pre)
        dpre = d_ref[...] * (sg * (1.0 + pre * (1.0 - sg)))
        dx, dw, db = _conv_bwd(dpre, x, w_ref[...])
        dx_ref[...] = dx.astype(BF)
        dw_ref[...] = dw
        db_ref[...] = db

    c0 = PXBC // 128
    return pl.pallas_call(
        body, grid=(XBC // 128,),
        in_specs=[_spec((T, 128), lambda c: (0, c)), _spec((T, 128), lambda c: (0, c0 + c)),
                  _spec((4, 128), lambda c: (0, c)), _spec((1, 128), lambda c: (0, c))],
        out_specs=[_spec((T, 128), lambda c: (0, c)), _spec((4, 128), lambda c: (0, c)), _spec((1, 128), lambda c: (0, c))],
        out_shape=[jax.ShapeDtypeStruct((T, XBC), BF), jax.ShapeDtypeStruct((4, XBC), F32), jax.ShapeDtypeStruct((1, XBC), F32)],
        compiler_params=_params(), name="conv_silu_bwd")(dact, proj, cw, cb)


def _ssd_chunk_common(row0, dt_ref, b_ref, c_ref, bias, a_neg):
    shape = (Q, Q)
    lane = _lanes(shape)
    sub = _rows(shape)
    live = (_rows(shape, row0) >= NPAD) & (lane < 8)
    dtr = dt_ref[:, :]
    dt = jnp.where(live, _softplus(dtr + bias), 0.0)
    d_a = dt * a_neg
    tri = (sub >= lane).astype(F32)
    cs = jnp.dot(tri, d_a, precision=lax.Precision.HIGHEST, preferred_element_type=F32)
    cs_t = cs.T
    bc = b_ref[:, :].astype(BF)
    cc = c_ref[:, :].astype(BF)
    cb = lax.dot_general(cc, bc, NT_DIMS, preferred_element_type=F32)
    cs_last = cs[Q - 1:Q, :]
    return dict(lane=lane, sub=sub, live=live, dtr=dtr, dt=dt, cs=cs, cs_t=cs_t, bc=bc, cc=cc, cb=cb,
                ecs=jnp.exp(cs), dsm=jnp.exp(cs_last - cs), gam=jnp.exp(cs_last), tri=tri)


def _pair(lane_even, mat, j):
    return jnp.where(lane_even, mat[:, j:j + 1], mat[:, j + 1:j + 2])


def _head_decay(cm, j):
    seg = cm["cs"][:, j:j + 1] - cm["cs_t"][j:j + 1, :]
    return jnp.exp(jnp.where(cm["sub"] >= cm["lane"], seg, -jnp.inf))


def ssd_fwd(xbc_act, proj, dt_bias2, a_log2, d2, norm_w):
    def body(x_ref, b_ref, c_ref, dt_ref, z_ref, bias_ref, alog_ref, d_ref, nw_ref, yn_ref, y_ref, hp_ref, h_scr):
        c = pl.program_id(1)

        @pl.when(c == 0)
        def _():
            h_scr[...] = jnp.zeros_like(h_scr)

        bias = bias_ref[0]
        a_neg = -jnp.exp(alog_ref[0])
        dsk = d_ref[0]
        cm = _ssd_chunk_common(c * Q, dt_ref, b_ref, c_ref, bias, a_neg)
        lane_even = cm["lane"] < 64
        sub_even = cm["sub"] < 64
        for p in range(4):
            je, jo = 2 * p, 2 * p + 1
            xp = x_ref[:, 128 * p:128 * p + 128]
            xdt = xp * _pair(lane_even, cm["dt"], je)
            xdt_b = xdt.astype(BF)
            m_e = (cm["cb"] * _head_decay(cm, je)).astype(BF)
            m_o = (cm["cb"] * _head_decay(cm, jo)).astype(BF)
            zero = jnp.zeros_like(xdt_b)
            yd = (jnp.dot(m_e, jnp.where(lane_even, xdt_b, zero), preferred_element_type=F32)
                  + jnp.dot(m_o, jnp.where(lane_even, zero, xdt_b), preferred_element_type=F32))
            hp = h_scr[p]
            hp_ref[0, 0, p] = hp
            yo = lax.dot_general(cm["cc"], hp.astype(BF), NT_DIMS, preferred_element_type=F32) * _pair(lane_even, cm["ecs"], je)
            dsk_p = jnp.where(lane_even[0:1, :], dsk[:, je:je + 1], dsk[:, jo:jo + 1])
            y_ref[:, 128 * p:128 * p + 128] = yd + yo + xp * dsk_p
            st = lax.dot_general((xdt * _pair(lane_even, cm["dsm"], je)).astype(BF), cm["bc"], TN_DIMS, preferred_element_type=F32)
            gam = jnp.where(sub_even[:, 0:1], cm["gam"][:, je:je + 1], cm["gam"][:, jo:jo + 1])
            h_scr[p] = hp * gam + st
        zc = z_ref[:, :]
        gated = y_ref[:, :] * (zc * _sigmoid(zc))
        yn_ref[:, :] = _rms(gated, nw_ref[...]).astype(BF)

    par = _spec((1, 1, 128), lambda g, c: (g, 0, 0))
    wide = _spec((Q, 512), lambda g, c: (c, g))
    return pl.pallas_call(
        body, grid=(2, NCH),
        in_specs=[wide, _spec((Q, 128), lambda g, c: (c, 8 + g)), _spec((Q, 128), lambda g, c: (c, 10 + g)),
                  _spec((Q, 128), lambda g, c: (c, PDT // 128 + g)), wide, par, par, par, _spec((1, 512), lambda g, c: (0, g))],
        out_specs=[wide, wide, _spec((1, 1, 4, 128, 128), lambda g, c: (g, c, 0, 0, 0))],
        out_shape=[jax.ShapeDtypeStruct((T, SSD_W), BF), jax.ShapeDtypeStruct((T, SSD_W), F32),
                   jax.ShapeDtypeStruct((2, NCH, 4, 128, 128), F32)],
        scratch_shapes=[pltpu.VMEM((4, 128, 128), F32)],
        compiler_params=_params(2), name="ssd_fwd")(xbc_act, xbc_act, xbc_act, proj, proj, dt_bias2, a_log2, d2, norm_w)


def ssd_bwd(dyn, xbc_act, proj, y_pre, h_prev, dt_bias2, a_log2, d2, norm_w):
    def body(dyn_ref, x_ref, b_ref, c_ref, dt_ref, z_ref, y_ref, hp_ref, bias_ref, alog_ref, d_ref, nw_ref,
             dz_ref, dx_ref, db_ref, dc_ref, ddt_ref, dpar_ref, dnw_ref, dh_scr, acc_scr):
        ci = pl.program_id(1)

        @pl.when(ci == 0)
        def _():
            dh_scr[...] = jnp.zeros_like(dh_scr)
            acc_scr[...] = jnp.zeros_like(acc_scr)
            dnw_ref[...] = jnp.zeros_like(dnw_ref)

        bias = bias_ref[0]
        a_neg = -jnp.exp(alog_ref[0])
        dsk = d_ref[0]
        cm = _ssd_chunk_common((NCH - 1 - ci) * Q, dt_ref, b_ref, c_ref, bias, a_neg)
        lane, sub = cm["lane"], cm["sub"]
        lane_even = lane < 64
        sub_even = sub < 64
        zc = z_ref[:, :]
        yc = y_ref[:, :]
        sg = _sigmoid(zc)
        sz = zc * sg
        dgated, dnw = _rms_bwd(dyn_ref[:, :], yc * sz, nw_ref[...])
        dnw_ref[...] += jnp.sum(dnw, axis=0, keepdims=True)
        dz_ref[:, :] = (dgated * yc * (sg * (1.0 + zc * (1.0 - sg)))).astype(BF)
        dy_all = dgated * sz
        dcb = jnp.zeros((Q, Q), F32)
        db_acc = jnp.zeros((Q, Q), F32)
        dc_acc = jnp.zeros((Q, Q), F32)
        dcs_col = jnp.zeros((Q, Q), F32)
        dcs_row = jnp.zeros((Q, Q), F32)
        ddt = jnp.zeros((Q, Q), F32)
        for p in range(4):
            je, jo = 2 * p, 2 * p + 1
            xp = x_ref[:, 128 * p:128 * p + 128]
            dy = dy_all[:, 128 * p:128 * p + 128]
            dt_p = _pair(lane_even, cm["dt"], je)
            xdt = xp * dt_p
            xdt_b = xdt.astype(BF)
            dy_b = dy.astype(BF)
            zero = jnp.zeros_like(dy_b)
            hp = hp_ref[0, 0, p]
            hp_b = hp.astype(BF)
            dh = dh_scr[p]
            dh_b = dh.astype(BF)
            acc_scr[p:p + 1, :] += jnp.sum(dy * xp, axis=0, keepdims=True)
            dsk_p = jnp.where(lane_even[0:1, :], dsk[:, je:je + 1], dsk[:, jo:jo + 1])
            dxp = dy * dsk_p
            e_p = _pair(lane_even, cm["ecs"], je)
            g_p = lax.dot_general(cm["cc"], hp_b, NT_DIMS, preferred_element_type=F32)
            dg_b = (dy * e_p).astype(BF)
            de = dy * g_p * e_p
            dc_acc = dc_acc + jnp.dot(dg_b, hp_b, preferred_element_type=F32)
            dh_in = lax.dot_general(dg_b, cm["cc"], TN_DIMS, preferred_element_type=F32)
            ds_p = _pair(lane_even, cm["dsm"], je)
            r_p = lax.dot_general(cm["bc"], dh_b, NT_DIMS, preferred_element_type=F32)
            dxdt = r_p * ds_p
            tt = r_p * xdt * ds_p
            db_acc = db_acc + jnp.dot((xdt * ds_p).astype(BF), dh_b, preferred_element_type=F32)
            dgam_m = dh * hp
            for j, even in ((je, True), (jo, False)):
                sel = lane_even if even else jnp.logical_not(lane_even)
                ssel = sub_even if even else jnp.logical_not(sub_even)
                l_j = _head_decay(cm, j)
                m_j = cm["cb"] * l_j
                dm = lax.dot_general(jnp.where(sel, dy_b, zero), xdt_b, NT_DIMS, preferred_element_type=F32)
                dxdt = dxdt + lax.dot_general(m_j.astype(BF), jnp.where(sel, dy_b, zero), TN_DIMS, preferred_element_type=F32)
                w_j = dm * m_j
                dcb = dcb + dm * l_j
                t_j = jnp.sum(jnp.where(sel, tt, 0.0), axis=1, keepdims=True)
                col = (jnp.sum(w_j, axis=1, keepdims=True) + jnp.sum(jnp.where(sel, de, 0.0), axis=1, keepdims=True) - t_j)
                gam_j = cm["gam"][:, j:j + 1]
                last = (jnp.sum(t_j, axis=0, keepdims=True)
                        + jnp.sum(jnp.sum(jnp.where(ssel, dgam_m, 0.0), axis=1, keepdims=True), axis=0, keepdims=True) * gam_j)
                col = col + jnp.where(sub[:, 0:1] == Q - 1, last, 0.0)
                dcs_col = dcs_col + jnp.where(lane == j, col, 0.0)
                dcs_row = dcs_row + jnp.where(sub == j, jnp.sum(w_j, axis=0, keepdims=True), 0.0)
            gam = jnp.where(sub_even[:, 0:1], cm["gam"][:, je:je + 1], cm["gam"][:, jo:jo + 1])
            dh_scr[p] = dh_in + dh * gam
            dx_ref[:, 128 * p:128 * p + 128] = dxp + dxdt * dt_p
            dd = dxdt * xp
            ddt = ddt + jnp.where(lane == je, jnp.sum(jnp.where(lane_even, dd, 0.0), axis=1, keepdims=True), 0.0)
            ddt = ddt + jnp.where(lane == jo, jnp.sum(jnp.where(lane_even, 0.0, dd), axis=1, keepdims=True), 0.0)
        dcb_b = dcb.astype(BF)
        dc_ref[:, :] = dc_acc + jnp.dot(dcb_b, cm["bc"], preferred_element_type=F32)
        db_ref[:, :] = db_acc + lax.dot_general(dcb_b, cm["cc"], TN_DIMS, preferred_element_type=F32)
        dcs = dcs_col - dcs_row.T
        dd_a = lax.dot_general(cm["tri"], dcs, TN_DIMS, precision=lax.Precision.HIGHEST, preferred_element_type=F32)
        ddt = ddt + dd_a * a_neg
        acc_scr[5:6, :] += jnp.sum(dd_a * cm["dt"], axis=0, keepdims=True)
        draw = jnp.where(cm["live"], ddt * _sigmoid(cm["dtr"] + bias), 0.0)
        acc_scr[4:5, :] += jnp.sum(draw, axis=0, keepdims=True)
        ddt_ref[:, :] = draw.astype(BF)

        @pl.when(ci == NCH - 1)
        def _():
            lane1 = _lanes((1, 128))
            dd = jnp.zeros((1, 128), F32)
            for p in range(4):
                row = acc_scr[p:p + 1, :]
                dd = dd + jnp.where(lane1 == 2 * p, jnp.sum(jnp.where(lane1 < 64, row, 0.0), axis=1, keepdims=True), 0.0)
                dd = dd + jnp.where(lane1 == 2 * p + 1, jnp.sum(jnp.where(lane1 < 64, 0.0, row), axis=1, keepdims=True), 0.0)
            dpar_ref[0] = jnp.concatenate([acc_scr[4:5, :], acc_scr[5:6, :] * a_neg, dd, jnp.zeros((5, 128), F32)], axis=0)

    par = _spec((1, 1, 128), lambda g, c: (g, 0, 0))
    wide = _spec((Q, 512), lambda g, c: (NCH - 1 - c, g))
    thin = _spec((Q, 128), lambda g, c: (NCH - 1 - c, g))
    return pl.pallas_call(
        body, grid=(2, NCH),
        in_specs=[wide, wide, _spec((Q, 128), lambda g, c: (NCH - 1 - c, 8 + g)), _spec((Q, 128), lambda g, c: (NCH - 1 - c, 10 + g)),
                  _spec((Q, 128), lambda g, c: (NCH - 1 - c, PDT // 128 + g)), wide, wide,
                  _spec((1, 1, 4, 128, 128), lambda g, c: (g, NCH - 1 - c, 0, 0, 0)), par, par, par, _spec((1, 512), lambda g, c: (0, g))],
        out_specs=[wide, wide, thin, thin, thin, _spec((1, 8, 128), lambda g, c: (g, 0, 0)), _spec((1, 512), lambda g, c: (0, g))],
        out_shape=[jax.ShapeDtypeStruct((T, SSD_W), BF), jax.ShapeDtypeStruct((T, SSD_W), F32), jax.ShapeDtypeStruct((T, 256), F32),
                   jax.ShapeDtypeStruct((T, 256), F32), jax.ShapeDtypeStruct((T, 256), BF), jax.ShapeDtypeStruct((2, 8, 128), F32),
                   jax.ShapeDtypeStruct((1, SSD_W), F32)],
        scratch_shapes=[pltpu.VMEM((4, 128, 128), F32), pltpu.VMEM((8, 128), F32)],
        compiler_params=_params(2), name="ssd_bwd")(dyn, xbc_act, xbc_act, xbc_act, proj, proj, y_pre, h_prev, dt_bias2, a_log2, d2, norm_w)


def _lru_gates(xl, cw, cb, wa, ba, wx, bx, lam):
    xr = _conv(xl, cw, cb)
    xr_b = xr.astype(BF)
    r = _sigmoid(jnp.dot(xr_b, wa, preferred_element_type=F32) + ba)
    i = _sigmoid(jnp.dot(xr_b, wx, preferred_element_type=F32) + bx)
    sp = _softplus(-lam)
    la = (-LRU_C) * r * sp
    a = jnp.exp(la)
    mult = jnp.sqrt(-jnp.tanh(la) * (a * a + 1.0))
    return xr, xr_b, r, i, sp, a, mult


def lru_gates_fwd(proj, cw, cb, wa2, ba, wx2, bx, lam):
    def body(x_ref, cw_ref, cb_ref, wa_ref, ba_ref, wx_ref, bx_ref, lam_ref, a_ref, u_ref):
        xr, _, _, i, _, a, mult = _lru_gates(x_ref[...], cw_ref[...], cb_ref[...], wa_ref[0], ba_ref[...], wx_ref[0], bx_ref[...], lam_ref[...])
        a_ref[...] = a
        u_ref[...] = jnp.where(_rows(a.shape) >= NPAD, mult * (i * xr), 0.0)

    c0 = PXL // 128
    vec = _spec((1, 128), lambda c: (0, c))
    mat = _spec((1, 128, 128), lambda c: (c, 0, 0))
    return pl.pallas_call(
        body, grid=(8,),
        in_specs=[_spec((T, 128), lambda c: (0, c0 + c)), _spec((4, 128), lambda c: (0, c)), vec, mat, vec, mat, vec, vec],
        out_specs=[_spec((T, 128), lambda c: (0, c)), _spec((T, 128), lambda c: (0, c))],
        out_shape=[jax.ShapeDtypeStruct((T, LRU_W), F32), jax.ShapeDtypeStruct((T, LRU_W), F32)],
        compiler_params=_params(), name="lru_gates_fwd")(proj, cw, cb, wa2, ba, wx2, bx, lam)


def lru_scan_fwd(a, u):
    def body(a_ref, u_ref, h_ref):
        def step(i, h):
            base = pl.multiple_of(i * 8, 8)
            for k in range(8):
                h = a_ref[pl.ds(base + k, 1), :] * h + u_ref[pl.ds(base + k, 1), :]
                h_ref[pl.ds(base + k, 1), :] = h
            return h

        lax.fori_loop(0, T // 8, step, jnp.zeros((1, LRU_W), F32))

    return pl.pallas_call(body, out_shape=jax.ShapeDtypeStruct((T, LRU_W), F32), compiler_params=_params(0), name="lru_scan_fwd")(a, u)


def lru_scan_bwd(a, dh_out):
    def body(a_ref, d_ref, o_ref):
        def step(i, carry):
            base = pl.multiple_of(T - 8 - i * 8, 8)
            for k in range(7, -1, -1):
                carry = d_ref[pl.ds(base + k, 1), :] + carry
                o_ref[pl.ds(base + k, 1), :] = carry
                carry = carry * a_ref[pl.ds(base + k, 1), :]
            return carry

        lax.fori_loop(0, T // 8, step, jnp.zeros((1, LRU_W), F32))

    return pl.pallas_call(body, out_shape=jax.ShapeDtypeStruct((T, LRU_W), F32), compiler_params=_params(0), name="lru_scan_bwd")(a, dh_out)


def lru_gates_bwd(dhs, hseq, proj, cw, cb, wa2, ba, wx2, bx, lam):
    def body(dh_ref, h_ref, x_ref, cw_ref, cb_ref, wa_ref, ba_ref, wx_ref, bx_ref, lam_ref,
             dx_ref, dcw_ref, dcb_ref, dwa_ref, dba_ref, dwx_ref, dbx_ref, dlam_ref):
        xl = x_ref[...]
        lam = lam_ref[...]
        xr, xr_b, r, i, sp, a, mult = _lru_gates(xl, cw_ref[...], cb_ref[...], wa_ref[0], ba_ref[...], wx_ref[0], bx_ref[...], lam)
        dh = dh_ref[...]
        da = dh * _shift_down(h_ref[...], 1)
        du = jnp.where(_rows(dh.shape) >= NPAD, dh, 0.0)
        dmult = du * (i * xr)
        di = du * (mult * xr)
        dxr = du * (mult * i)
        dla = da * a - dmult * (a * a) / mult
        dr = dla * ((-LRU_C) * sp)
        dsp = jnp.sum(dla * ((-LRU_C) * r), axis=0, keepdims=True)
        dlam_ref[...] = -dsp * _sigmoid(-lam)
        dpr = dr * r * (1.0 - r)
        dpi = di * i * (1.0 - i)
        dba_ref[...] = jnp.sum(dpr, axis=0, keepdims=True)
        dbx_ref[...] = jnp.sum(dpi, axis=0, keepdims=True)
        dpr_b = dpr.astype(BF)
        dpi_b = dpi.astype(BF)
        dxr = (dxr + lax.dot_general(dpr_b, wa_ref[0], NT_DIMS, preferred_element_type=F32)
               + lax.dot_general(dpi_b, wx_ref[0], NT_DIMS, preferred_element_type=F32))
        dwa_ref[0] = lax.dot_general(xr_b, dpr_b, TN_DIMS, preferred_element_type=F32)
        dwx_ref[0] = lax.dot_general(xr_b, dpi_b, TN_DIMS, preferred_element_type=F32)
        dx, dcw, dcb = _conv_bwd(dxr, xl, cw_ref[...])
        dx_ref[...] = dx.astype(BF)
        dcw_ref[...] = dcw
        dcb_ref[...] = dcb

    c0 = PXL // 128
    vec = _spec((1, 128), lambda c: (0, c))
    mat = _spec((1, 128, 128), lambda c: (c, 0, 0))
    col = _spec((T, 128), lambda c: (0, c))
    vshape = jax.ShapeDtypeStruct((1, LRU_W), F32)
    mshape = jax.ShapeDtypeStruct((8, 128, 128), F32)
    return pl.pallas_call(
        body, grid=(8,),
        in_specs=[col, col, _spec((T, 128), lambda c: (0, c0 + c)), _spec((4, 128), lambda c: (0, c)), vec, mat, vec, mat, vec, vec],
        out_specs=[col, _spec((4, 128), lambda c: (0, c)), vec, mat, vec, mat, vec, vec],
        out_shape=[jax.ShapeDtypeStruct((T, LRU_W), BF), jax.ShapeDtypeStruct((4, LRU_W), F32), vshape, mshape, vshape, mshape, vshape, vshape],
        compiler_params=_params(), name="lru_gates_bwd")(dhs, hseq, proj, cw, cb, wa2, ba, wx2, bx, lam)


def gate_up(h1, wn, w_gate, w_up):
    def body(h_ref, wn_ref, wg_ref, wu_ref, gt_ref, up_ref, act_ref, u_ref):
        for r in (0, HALF):
            u_ref[r:r + HALF, :] = _rms(h_ref[r:r + HALF, :], wn_ref[...]).astype(BF)

        def tile(c0):
            cols = pl.ds(c0, 256)
            gt = jnp.dot(u_ref[...], wg_ref[:, cols], preferred_element_type=F32)
            up = jnp.dot(u_ref[...], wu_ref[:, cols], preferred_element_type=F32)
            gt_ref[:, cols] = gt.astype(BF)
            up_ref[:, cols] = up.astype(BF)
            act_ref[:, cols] = (gt * _sigmoid(gt) * up).astype(BF)

        _col_tiles(D_FF, 256, tile)

    big = jax.ShapeDtypeStruct((T, D_FF), BF)
    return pl.pallas_call(
        body, grid=(T // RC,), in_specs=[_rows_spec(D), _vec(D), _whole((D, D_FF)), _whole((D, D_FF))],
        out_specs=[_rows_spec(D_FF), _rows_spec(D_FF), _rows_spec(D_FF), _rows_spec(D)],
        out_shape=[big, big, big, jax.ShapeDtypeStruct((T, D), BF)],
        compiler_params=_params(), name="gate_up")(h1, wn, w_gate, w_up)


def down_loss(act, w_down, h1, target, wf):
    def body(a_ref, w_ref, r_ref, t_ref, wf_ref, d_ref, db_ref, l_ref, dw_ref, h_scr):
        _zero_at_first(l_ref, dw_ref)

        def tile(c0):
            cols = pl.ds(c0, 512)
            h_scr[:, cols] = r_ref[:, cols] + jnp.dot(a_ref[...], w_ref[:, cols], preferred_element_type=F32)

        _col_tiles(D, 512, tile)
        for r in (0, HALF):
            h = h_scr[r:r + HALF, :]
            live = _rows((HALF, D), pl.program_id(0) * RC + r) >= NPAD + N_META
            err = jnp.where(live, _rms(h, wf_ref[...]) - t_ref[r:r + HALF, :], 0.0)
            l_ref[...] += 0.5 * jnp.sum(jnp.sum(err * err, axis=1, keepdims=True) * (1.0 / D), axis=0, keepdims=True)
            dh, dw = _rms_bwd(err * (1.0 / D), h, wf_ref[...])
            dw_ref[...] += jnp.sum(dw, axis=0, keepdims=True)
            d_ref[r:r + HALF, :] = dh
            db_ref[r:r + HALF, :] = dh.astype(BF)

    return pl.pallas_call(
        body, grid=(T // RC,), in_specs=[_rows_spec(D_FF), _whole((D_FF, D)), _rows_spec(D), _rows_spec(D), _vec(D)],
        out_specs=[_rows_spec(D), _rows_spec(D), _spec((1, 128), lambda i: (0, 0)), _vec(D)],
        out_shape=[jax.ShapeDtypeStruct((T, D), F32), jax.ShapeDtypeStruct((T, D), BF), jax.ShapeDtypeStruct((1, 128), F32),
                   jax.ShapeDtypeStruct((1, D), F32)],
        scratch_shapes=[pltpu.VMEM((RC, D), F32)],
        compiler_params=_params(), name="down_loss")(act, w_down, h1, target, wf)


def swiglu_bwd(dh2_b, w_down, gt, up):
    def body(d_ref, w_ref, gt_ref, up_ref, dg_ref, du_ref):
        def tile(c0):
            cols = pl.ds(c0, 256)
            dact = lax.dot_general(d_ref[...], w_ref[cols, :], NT_DIMS, preferred_element_type=F32)
            gt_ = gt_ref[:, cols].astype(F32)
            up_ = up_ref[:, cols].astype(F32)
            sg = _sigmoid(gt_)
            dg_ref[:, cols] = (dact * up_ * (sg * (1.0 + gt_ * (1.0 - sg)))).astype(BF)
            du_ref[:, cols] = (dact * (gt_ * sg)).astype(BF)

        _col_tiles(D_FF, 256, tile)

    big = jax.ShapeDtypeStruct((T, D_FF), BF)
    return pl.pallas_call(
        body, grid=(T // RC,), in_specs=[_rows_spec(D), _whole((D_FF, D)), _rows_spec(D_FF), _rows_spec(D_FF)],
        out_specs=[_rows_spec(D_FF), _rows_spec(D_FF)], out_shape=[big, big], compiler_params=_params(), name="swiglu_bwd")(dh2_b, w_down, gt, up)


def gate_up_bwd(dgt, dup, w_gate, w_up, h1, wn, dh2):
    def body(dg_ref, du_ref, wg_ref, wu_ref, h_ref, wn_ref, r_ref, d_ref, db_ref, dw_ref, du_scr):
        _zero_at_first(dw_ref)

        def tile(c0):
            rows = pl.ds(c0, 512)
            du_scr[:, rows] = (lax.dot_general(dg_ref[...], wg_ref[rows, :], NT_DIMS, preferred_element_type=F32)
                               + lax.dot_general(du_ref[...], wu_ref[rows, :], NT_DIMS, preferred_element_type=F32))

        _col_tiles(D, 512, tile)
        for r in (0, HALF):
            dh, dw = _rms_bwd(du_scr[r:r + HALF, :], h_ref[r:r + HALF, :], wn_ref[...])
            dw_ref[...] += jnp.sum(dw, axis=0, keepdims=True)
            dh = dh + r_ref[r:r + HALF, :]
            d_ref[r:r + HALF, :] = dh
            db_ref[r:r + HALF, :] = dh.astype(BF)

    return pl.pallas_call(
        body, grid=(T // RC,),
        in_specs=[_rows_spec(D_FF), _rows_spec(D_FF), _whole((D, D_FF)), _whole((D, D_FF)), _rows_spec(D), _vec(D), _rows_spec(D)],
        out_specs=[_rows_spec(D), _rows_spec(D), _vec(D)],
        out_shape=[jax.ShapeDtypeStruct((T, D), F32), jax.ShapeDtypeStruct((T, D), BF), jax.ShapeDtypeStruct((1, D), F32)],
        scratch_shapes=[pltpu.VMEM((RC, D), F32)],
        compiler_params=_params(), name="gate_up_bwd")(dgt, dup, w_gate, w_up, h1, wn, dh2)


def _adamw(w, g, m, v):
    m = ADAM_B1 * m + (1.0 - ADAM_B1) * g
    v = ADAM_B2 * v + (1.0 - ADAM_B2) * (g * g)
    m_hat = m / (1.0 - ADAM_B1 ** ADAM_STEP)
    v_hat = v / (1.0 - ADAM_B2 ** ADAM_STEP)
    delta = -ADAM_LR * (m_hat / (jnp.sqrt(v_hat) + ADAM_EPS) + ADAM_WD * w)
    return delta, m, v


def adamw_shard(name, recv, w, m, v, tr):
    r, c = w.shape

    def body(p_ref, w_ref, m_ref, v_ref, g_ref, d_ref, mo_ref, vo_ref):
        g = p_ref[0].astype(F32)
        for s in range(1, 8):
            g = g + p_ref[s].astype(F32)
        g_ref[...] = g
        d_ref[...], mo_ref[...], vo_ref[...] = _adamw(w_ref[...], g, m_ref[...], v_ref[...])

    tile = _spec((tr, c), lambda i: (i, 0))
    shape = jax.ShapeDtypeStruct((r, c), F32)
    return pl.pallas_call(
        body, grid=(r // tr,), in_specs=[_spec((8, tr, c), lambda i: (0, i, 0)), tile, tile, tile],
        out_specs=[tile] * 4, out_shape=[shape] * 4, compiler_params=_params(), name=name)(recv, w, m, v)


def sum_slabs(recv):
    def body(p_ref, o_ref):
        g = p_ref[0]
        for s in range(1, 8):
            g = g + p_ref[s]
        o_ref[...] = g

    return pl.pallas_call(body, out_shape=jax.ShapeDtypeStruct(recv.shape[1:], F32), compiler_params=_params(0), name="sum_slabs")(recv)


SIMPLE = [("norm1_w", 1024), ("ssd_conv_b", 1536), ("ssd_dt_bias", 16), ("ssd_a_log", 16), ("ssd_d", 16), ("ssd_norm_w", 1024),
          ("lru_conv_b", 1024), ("lru_ba", 1024), ("lru_bx", 1024), ("lru_lambda", 1024), ("lru_norm_w", 1024), ("norm2_w", 1024),
          ("final_norm_w", 1024)]
SPECIAL = ["lru_wa", "lru_wx", "meta_tokens", "ssd_conv_w", "lru_conv_w"]
SM_ROWS = 176
SM_WA, SM_WX, SM_META, SM_SCW, SM_LCW, SM_LOSS = 14, 78, 142, 158, 166, 170


def _simple_rows():
    rows, r = {}, 0
    for name, n in SIMPLE:
        rows[name] = r
        r += -(-n // 1024)
    return rows


def adamw_small(sm, special_g, ws, ms, vs):
    rows = _simple_rows()
    ns, nx = len(SIMPLE), len(SPECIAL)

    def body(*refs):
        sm_ref = refs[0]
        gx = refs[1:1 + nx]
        wr = refs[1 + nx:1 + nx + ns + nx]
        mr = refs[1 + nx + ns + nx:1 + nx + 2 * (ns + nx)]
        vr = refs[1 + nx + 2 * (ns + nx):1 + nx + 3 * (ns + nx)]
        outs = refs[1 + nx + 3 * (ns + nx):]
        o = 0
        for k, (name, n) in enumerate(SIMPLE):
            r0 = rows[name]
            for c0 in range(0, n, 1024):
                wd = min(1024, n - c0)
                g = sm_ref[r0 + c0 // 1024:r0 + c0 // 1024 + 1, 0:wd]
                sl = (slice(None), slice(c0, c0 + wd))
                d, m2, v2 = _adamw(wr[k][sl], g, mr[k][sl], vr[k][sl])
                outs[o][sl] = g
                outs[o + 1][sl] = d
                outs[o + 2][sl] = m2
                outs[o + 3][sl] = v2
            o += 4
        for k in range(nx):
            d, m2, v2 = _adamw(wr[ns + k][...], gx[k][...], mr[ns + k][...], vr[ns + k][...])
            outs[o][...] = d
            outs[o + 1][...] = m2
            outs[o + 2][...] = v2
            o += 3

    out_shape = []
    for k in range(ns):
        out_shape += [jax.ShapeDtypeStruct(ws[k].shape, F32)] * 4
    for k in range(nx):
        out_shape += [jax.ShapeDtypeStruct(ws[ns + k].shape, F32)] * 3
    return pl.pallas_call(body, out_shape=out_shape, compiler_params=_params(0), name="adamw_small")(sm, *special_g, *ws, *ms, *vs)


def _place():
    return lax.axis_index("x"), lax.axis_index("y"), lax.axis_index("c")


def _index(px, py, pc):
    return 4 * px + 2 * py + pc


def all_gather(name, shards):
    n = len(shards)
    hbm = pl.BlockSpec(memory_space=pl.ANY)

    def body(*refs):
        ins, outs = refs[:n], refs[n:2 * n]
        send_sems, recv_sems, local_sems = refs[2 * n:]
        x, y, c = _place()
        me, sibling = (x, y, c), (x, y, 1 - c)
        chips = [(1 - x, y), (x, 1 - y), (1 - x, 1 - y)]

        def copy(i, k, block, to, src=None):
            dst = outs[i].at[_index(*block)]
            return pltpu.make_async_remote_copy(src_ref=dst if src is None else src, dst_ref=dst, send_sem=send_sems.at[7 * i + k],
                                                recv_sem=recv_sems.at[7 * i + k], device_id=to, device_id_type=MESH)

        mine = [pltpu.make_async_copy(ins[i], outs[i].at[_index(*me)], local_sems.at[i]) for i in range(n)]
        for cp in mine:
            cp.start()
        first = []
        for i in range(n):
            first += [copy(i, 1 + j, me, (*chip, c), src=ins[i]) for j, chip in enumerate(chips)]
            first.append(copy(i, 0, me, sibling, src=ins[i]))
        for cp in first:
            cp.start()
        passed = []
        for i in range(n):
            for j, chip in enumerate(chips):
                copy(i, 1 + j, (*chip, c), me).wait_recv()
                cp = copy(i, 4 + j, (*chip, c), sibling)
                cp.start()
                passed.append(cp)
        for i in range(n):
            copy(i, 0, sibling, me).wait_recv()
            for j, chip in enumerate(chips):
                copy(i, 4 + j, (*chip, 1 - c), me).wait_recv()
        for cp in first + passed:
            cp.wait_send()
        for cp in mine:
            cp.wait()

    return pl.pallas_call(
        body, in_specs=[hbm] * n, out_specs=[hbm] * n,
        out_shape=[jax.ShapeDtypeStruct((8,) + s.shape, s.dtype) for s in shards],
        scratch_shapes=[pltpu.SemaphoreType.DMA((7 * n,)), pltpu.SemaphoreType.DMA((7 * n,)), pltpu.SemaphoreType.DMA((n,))],
        name=name)(*shards)


HBM_SPEC = pl.BlockSpec(memory_space=pltpu.HBM)
SEM_SPEC = pl.BlockSpec(memory_space=pltpu.SEMAPHORE)
EFFECT = pltpu.SideEffectType.DATAFLOW_SIDE_EFFECTING


def _peers(x, y, c):
    return [((1 - x) if k & 4 else x, (1 - y) if k & 2 else y, (1 - c) if k & 1 else c) for k in range(1, 8)]


def _pieces(rows):
    for n in (4, 2):
        if rows % (16 * n) == 0:
            return [(r * (rows // n), rows // n) for r in range(n)]
    return [(0, rows)]


def _peer_copies(src, land, send_sems, recv_sems, k, peer, mine, slab_src):
    block = src.at[_index(*peer)] if slab_src else src
    return [pltpu.make_async_remote_copy(src_ref=block.at[pl.ds(r0, nr)], dst_ref=land.at[mine, pl.ds(r0, nr)], send_sem=send_sems.at[k],
                                         recv_sem=recv_sems.at[k], device_id=peer, device_id_type=MESH)
            for r0, nr in _pieces(block.shape[0])]


def copies_start(name, srcs, slab_src, after):
    n = len(srcs)
    zones = [jax.ShapeDtypeStruct(s.shape if slab_src else (8,) + s.shape, s.dtype) for s in srcs]

    def body(*refs):
        ins, lands = refs[:n], refs[n:2 * n]
        sends, recvs = refs[2 * n + 1:3 * n + 1], refs[3 * n + 1:4 * n + 1]
        token = refs[-1]
        x, y, c = _place()
        mine = _index(x, y, c)
        for i in range(n):
            per_peer = [_peer_copies(ins[i], lands[i], sends[i], recvs[i], k, peer, mine, slab_src) for k, peer in enumerate(_peers(x, y, c))]
            for piece in zip(*per_peer):
                for cp in piece:
                    cp.start()
        token[...] = jnp.zeros_like(token)

    sem = pltpu.SemaphoreType.DMA((7,))
    res = pl.pallas_call(
        body, name=name,
        out_shape=([sem] * (2 * n) + [pltpu.HBM(s.shape, s.dtype) for s in srcs] + [pltpu.HBM(z.shape, z.dtype) for z in zones]
                   + [jax.ShapeDtypeStruct((8, 128), F32)]),
        in_specs=[HBM_SPEC] * (2 * n) + [pl.BlockSpec(memory_space=pl.ANY)],
        out_specs=[SEM_SPEC] * (2 * n) + [HBM_SPEC] * (2 * n) + [pl.BlockSpec(memory_space=pltpu.VMEM)],
        input_output_aliases={i: 2 * n + i for i in range(2 * n)},
        compiler_params=pltpu.CompilerParams(has_side_effects=EFFECT),
    )(*[pltpu.with_memory_space_constraint(s, pltpu.HBM) for s in srcs],
      *[pltpu.with_memory_space_constraint(lax.empty(z.shape, z.dtype), pltpu.HBM) for z in zones], after)
    return [(res[i], res[n + i], res[2 * n + i], res[3 * n + i]) for i in range(n)], res[-1][0:1, 0:1]


def copies_wait(name, started, slab_src, after):
    n = len(started)

    def body(*refs):
        ins, lands = refs[:n], refs[n:2 * n]
        sends, recvs = refs[2 * n:3 * n], refs[3 * n:4 * n]
        x, y, c = _place()
        mine = _index(x, y, c)
        for i in range(n):
            for k, peer in enumerate(_peers(x, y, c)):
                arrival = pltpu.make_async_remote_copy(src_ref=ins[i].at[mine] if slab_src else ins[i], dst_ref=lands[i].at[_index(*peer)],
                                                       send_sem=sends[i].at[k], recv_sem=recvs[i].at[k], device_id=peer, device_id_type=MESH)
                arrival.wait_send()
                arrival.wait_recv()

    srcs = [s[2] for s in started]
    lands = [s[3] for s in started]
    res = pl.pallas_call(
        body, name=name,
        out_shape=[pltpu.HBM(s.shape, s.dtype) for s in srcs] + [pltpu.HBM(z.shape, z.dtype) for z in lands],
        in_specs=[HBM_SPEC] * (2 * n) + [SEM_SPEC] * (2 * n) + [pl.BlockSpec(memory_space=pl.ANY)],
        out_specs=[HBM_SPEC] * (2 * n),
        input_output_aliases={i: i for i in range(2 * n)},
        compiler_params=pltpu.CompilerParams(has_side_effects=EFFECT),
    )(*srcs, *lands, *[s[0] for s in started], *[s[1] for s in started], after)
    me = _index(*_place())
    own = [lax.dynamic_index_in_dim(s, me, 0, keepdims=True) if slab_src else s[None] for s in res[:n]]
    return [lax.dynamic_update_slice_in_dim(z, o, me, 0) for z, o in zip(res[n:], own)]


WEIGHTS = ["meta_tokens", "norm1_w", "w_in", "ssd_conv_w", "ssd_conv_b", "ssd_dt_bias", "ssd_a_log", "ssd_d", "ssd_norm_w", "lru_conv_w",
           "lru_conv_b", "lru_wa", "lru_ba", "lru_wx", "lru_bx", "lru_lambda", "lru_norm_w", "w_out", "norm2_w", "w_gate", "w_up", "w_down",
           "final_norm_w"]
BIG = ["w_in", "w_out", "w_gate", "w_up", "w_down"]
BIG_ROW_TILE = {"w_in": 256, "w_out": 128, "w_gate": 256, "w_up": 256, "w_down": 176}


def _pair_blocks(w):
    w = w.reshape(8, 2, 64, 64)
    z = jnp.zeros((8, 64, 64), w.dtype)
    return jnp.concatenate([jnp.concatenate([w[:, 0], z], axis=2), jnp.concatenate([z, w[:, 1]], axis=2)], axis=1)


def _unpair_blocks(w2):
    return jnp.stack([w2[:, :64, :64], w2[:, 64:, 64:]], axis=1).reshape(16, 64, 64)


def _per_group(v):
    return jnp.pad(v.reshape(2, 1, 8), ((0, 0), (0, 0), (0, 120)))


def _pad_cols(v, n):
    return jnp.pad(v, ((0, 0), (0, n - v.shape[1])))


def local_step(x, target, meta, ssd_cw, lru_cw, w_in, fetch, send, p):
    z120 = jnp.zeros((D, 120), BF)
    w_p = jnp.concatenate([w_in[:, 0:1024], w_in[:, 2576:3600], w_in[:, 3600:4624], w_in[:, 1024:2560],
                           w_in[:, 2560:2568], z120, w_in[:, 2568:2576], z120, jnp.zeros((D, 256), BF)], axis=1)
    bias2, alog2, d2 = _per_group(p["ssd_dt_bias"]), _per_group(p["ssd_a_log"]), _per_group(p["ssd_d"])
    wa2 = _pair_blocks(p["lru_wa"]).astype(BF)
    wx2 = _pair_blocks(p["lru_wx"]).astype(BF)
    lru = (lru_cw, p["lru_conv_b"], wa2, p["lru_ba"], wx2, p["lru_bx"], p["lru_lambda"])

    h0 = jnp.concatenate([jnp.zeros((NPAD, D), F32), meta, x], axis=0)
    target = jnp.concatenate([jnp.zeros((NPAD + N_META, D), F32), target], axis=0)
    proj, u1 = in_proj(h0, p["norm1_w"], w_p)
    xbc_act = conv_silu_fwd(proj, ssd_cw, p["ssd_conv_b"])
    yn_ssd, y_pre, h_prev = ssd_fwd(xbc_act, proj, bias2, alog2, d2, p["ssd_norm_w"])
    a, u = lru_gates_fwd(proj, *lru)
    hseq = lru_scan_fwd(a, u)
    (w_out,) = fetch(["w_out"], hseq)
    h1, cat = out_proj(yn_ssd, proj, hseq, p["lru_norm_w"], w_out, h0)
    w_gate, w_up = fetch(["w_gate", "w_up"], h1)
    gt, up, act, u2 = gate_up(h1, p["norm2_w"], w_gate, w_up)
    (w_down,) = fetch(["w_down"], act)
    dh2, dh2_b, loss, d_fnw = down_loss(act, w_down, h1, target, p["final_norm_w"])

    dgt, dup = swiglu_bwd(dh2_b, w_down, gt, up)
    g_down = matmul_tn("dw_down", act, dh2_b, 1408, 512)
    g_gate = matmul_tn("dw_gate", u2, dgt, 512, 1408)
    g_up = matmul_tn("dw_up", u2, dup, 512, 1408)
    sent = send({"w_down": g_down, "w_gate": g_gate, "w_up": g_up})
    dh1, dh1_b, d_n2 = gate_up_bwd(dgt, dup, w_gate, w_up, h1, p["norm2_w"] + sent, dh2)
    sent = send({"w_out": matmul_tn("dw_out", cat, dh1_b, 512, 1024)})
    dyn, dh_out, dg_b, d_lnw = out_proj_bwd(dh1_b, w_out, proj, hseq, p["lru_norm_w"] + sent)

    dhs = lru_scan_bwd(a, dh_out)
    dxl_b, d_lcw, d_lcb, dwa2, d_ba, dwx2, d_bx, d_lam = lru_gates_bwd(dhs, hseq, proj, *lru)
    dz_b, dx, d_b, d_c, ddt_b, dpar, d_snw = ssd_bwd(dyn, xbc_act, proj, y_pre, h_prev, bias2, alog2, d2, p["ssd_norm_w"])
    dxbc_b, d_scw, d_scb = conv_silu_bwd(jnp.concatenate([dx, d_b, d_c], axis=1), proj, ssd_cw, p["ssd_conv_b"])
    dproj = jnp.concatenate([dz_b, dg_b, dxl_b, dxbc_b, ddt_b, jnp.zeros((T, 256), BF)], axis=1)
    g_p = matmul_tn("dw_in", u1, dproj, 512, 1024)
    g_in = jnp.concatenate([g_p[:, 0:1024], g_p[:, PXBC:PXBC + XBC], g_p[:, PDT:PDT + 8], g_p[:, PDT + 128:PDT + 136],
                            g_p[:, PG:PG + 1024], g_p[:, PXL:PXL + 1024]], axis=1)
    sent = send({"w_in": g_in})
    dh0, d_n1 = in_proj_bwd(dproj, w_p, h0, p["norm1_w"] + sent, dh1)
    small = {"norm1_w": d_n1, "ssd_conv_b": d_scb, "ssd_dt_bias": dpar[:, 0, :8].reshape(1, 16), "ssd_a_log": dpar[:, 1, :8].reshape(1, 16),
             "ssd_d": dpar[:, 2, :8].reshape(1, 16), "ssd_norm_w": d_snw, "lru_conv_b": d_lcb, "lru_ba": d_ba, "lru_bx": d_bx,
             "lru_lambda": d_lam, "lru_norm_w": d_lnw, "norm2_w": d_n2, "final_norm_w": d_fnw,
             "lru_wa": _unpair_blocks(dwa2), "lru_wx": _unpair_blocks(dwx2), "meta_tokens": dh0[NPAD:NPAD + N_META],
             "ssd_conv_w": d_scw, "lru_conv_w": d_lcw}
    return loss, dh0[NPAD + N_META:], small


def _pack_small(small, loss):
    rows = [_pad_cols(small[name], -(-n // 1024) * 1024).reshape(-1, 1024) for name, n in SIMPLE]
    rows += [small["lru_wa"].reshape(64, 1024), small["lru_wx"].reshape(64, 1024), small["meta_tokens"],
             _pad_cols(small["ssd_conv_w"], 2048).reshape(8, 1024), small["lru_conv_w"], _pad_cols(loss[:, 0:1], 1024)]
    sm = jnp.concatenate(rows, axis=0)
    return jnp.pad(sm, ((0, SM_ROWS - sm.shape[0]), (0, 0)))


def _slabs(g, name):
    if name in ("w_in", "w_gate", "w_up"):
        return g.reshape(g.shape[0], 8, g.shape[1] // 8).transpose(1, 0, 2)
    return g.reshape(8, g.shape[0] // 8, g.shape[1])


def _unslab(g, name):
    if name in ("w_in", "w_gate", "w_up"):
        return g.transpose(1, 0, 2).reshape(g.shape[1], 8 * g.shape[2])
    return g.reshape(8 * g.shape[1], g.shape[2])


def kernel(x, meta_tokens, norm1_w, w_in, ssd_conv_w, ssd_conv_b, ssd_dt_bias, ssd_a_log, ssd_d, ssd_norm_w, lru_conv_w, lru_conv_b, lru_wa, lru_ba, lru_wx, lru_bx, lru_lambda, lru_norm_w, w_out, norm2_w, w_gate, w_up, w_down, final_norm_w, loss_target, m_meta_tokens, m_norm1_w, m_w_in, m_ssd_conv_w, m_ssd_conv_b, m_ssd_dt_bias, m_ssd_a_log, m_ssd_d, m_ssd_norm_w, m_lru_conv_w, m_lru_conv_b, m_lru_wa, m_lru_ba, m_lru_wx, m_lru_bx, m_lru_lambda, m_lru_norm_w, m_w_out, m_norm2_w, m_w_gate, m_w_up, m_w_down, m_final_norm_w, v_meta_tokens, v_norm1_w, v_w_in, v_ssd_conv_w, v_ssd_conv_b, v_ssd_dt_bias, v_ssd_a_log, v_ssd_d, v_ssd_norm_w, v_lru_conv_w, v_lru_conv_b, v_lru_wa, v_lru_ba, v_lru_wx, v_lru_bx, v_lru_lambda, v_lru_norm_w, v_w_out, v_norm2_w, v_w_gate, v_w_up, v_w_down, v_final_norm_w):
    w = dict(meta_tokens=meta_tokens, norm1_w=norm1_w, w_in=w_in[0], ssd_conv_w=ssd_conv_w[0], ssd_conv_b=ssd_conv_b, ssd_dt_bias=ssd_dt_bias,
             ssd_a_log=ssd_a_log, ssd_d=ssd_d, ssd_norm_w=ssd_norm_w, lru_conv_w=lru_conv_w[0], lru_conv_b=lru_conv_b, lru_wa=lru_wa[0],
             lru_ba=lru_ba, lru_wx=lru_wx[0], lru_bx=lru_bx, lru_lambda=lru_lambda, lru_norm_w=lru_norm_w, w_out=w_out[0], norm2_w=norm2_w,
             w_gate=w_gate[0], w_up=w_up[0], w_down=w_down[0], final_norm_w=final_norm_w.reshape(1, D))
    m = dict(meta_tokens=m_meta_tokens, norm1_w=m_norm1_w, w_in=m_w_in[0], ssd_conv_w=m_ssd_conv_w[0], ssd_conv_b=m_ssd_conv_b,
             ssd_dt_bias=m_ssd_dt_bias, ssd_a_log=m_ssd_a_log, ssd_d=m_ssd_d, ssd_norm_w=m_ssd_norm_w, lru_conv_w=m_lru_conv_w[0],
             lru_conv_b=m_lru_conv_b, lru_wa=m_lru_wa[0], lru_ba=m_lru_ba, lru_wx=m_lru_wx[0], lru_bx=m_lru_bx, lru_lambda=m_lru_lambda,
             lru_norm_w=m_lru_norm_w, w_out=m_w_out[0], norm2_w=m_norm2_w, w_gate=m_w_gate[0], w_up=m_w_up[0], w_down=m_w_down[0],
             final_norm_w=m_final_norm_w.reshape(1, D))
    v = dict(meta_tokens=v_meta_tokens, norm1_w=v_norm1_w, w_in=v_w_in[0], ssd_conv_w=v_ssd_conv_w[0], ssd_conv_b=v_ssd_conv_b,
             ssd_dt_bias=v_ssd_dt_bias, ssd_a_log=v_ssd_a_log, ssd_d=v_ssd_d, ssd_norm_w=v_ssd_norm_w, lru_conv_w=v_lru_conv_w[0],
             lru_conv_b=v_lru_conv_b, lru_wa=v_lru_wa[0], lru_ba=v_lru_ba, lru_wx=v_lru_wx[0], lru_bx=v_lru_bx, lru_lambda=v_lru_lambda,
             lru_norm_w=v_lru_norm_w, w_out=v_w_out[0], norm2_w=v_norm2_w, w_gate=v_w_gate[0], w_up=v_w_up[0], w_down=v_w_down[0],
             final_norm_w=v_final_norm_w.reshape(1, D))
    shapes = dict(meta_tokens=meta_tokens.shape, norm1_w=norm1_w.shape, w_in=w_in.shape, ssd_conv_w=ssd_conv_w.shape,
                  ssd_conv_b=ssd_conv_b.shape, ssd_dt_bias=ssd_dt_bias.shape, ssd_a_log=ssd_a_log.shape, ssd_d=ssd_d.shape,
                  ssd_norm_w=ssd_norm_w.shape, lru_conv_w=lru_conv_w.shape, lru_conv_b=lru_conv_b.shape, lru_wa=lru_wa.shape,
                  lru_ba=lru_ba.shape, lru_wx=lru_wx.shape, lru_bx=lru_bx.shape, lru_lambda=lru_lambda.shape, lru_norm_w=lru_norm_w.shape,
                  w_out=w_out.shape, norm2_w=norm2_w.shape, w_gate=w_gate.shape, w_up=w_up.shape, w_down=w_down.shape,
                  final_norm_w=final_norm_w.shape)
    me = _index(*_place())

    small_shard = jnp.concatenate([w["meta_tokens"], _pad_cols(w["ssd_conv_w"], 256).reshape(8, 128), w["lru_conv_w"],
                                   jnp.zeros((4, 128), F32)], axis=0)
    g_in, gs = all_gather("gather_w_in", [w["w_in"].astype(BF), small_shard])
    later = ["w_out", "w_gate", "w_up", "w_down"]
    started, behind = copies_start("gather_rest_start", [w[n].astype(BF) for n in later], False, gs)
    started = dict(zip(later, started))
    meta_full = gs[:, 0:16].transpose(1, 0, 2).reshape(N_META, D)
    ssd_cw = gs[:, 16:24].reshape(8, 4, 256)[:, :, :192].transpose(1, 0, 2).reshape(4, XBC)
    lru_cw = gs[:, 24:28].transpose(1, 0, 2).reshape(4, LRU_W)

    def fetch(names, after):
        got = copies_wait("gather_" + names[0] + "_wait", [started[n] for n in names], False, after)
        return [_unslab(g, n) for n, g in zip(names, got)]

    in_flight = {}

    def send(grads):
        names = list(grads)
        st, token = copies_start("grads_" + names[0] + "_start", [grads[n] if n == "small" else _slabs(grads[n], n) for n in names], True,
                                 grads[names[0]])
        in_flight.update(zip(names, st))
        return token

    loss, grad_x, small = local_step(x[0], loss_target[0], meta_full, ssd_cw, lru_cw, _unslab(g_in, "w_in"), fetch, send,
                                     {**w, "norm1_w": w["norm1_w"] + behind})
    send({"small": _pack_small(small, loss).reshape(8, SM_ROWS // 8, 1024)})

    out = {}
    early = ["w_down", "w_gate", "w_up", "w_out"]
    recv = dict(zip(early, copies_wait("grads_early_wait", [in_flight[n] for n in early], True, in_flight["small"][2])))
    for n in early:
        out[n] = adamw_shard("adamw_" + n, recv[n], w[n], m[n], v[n], BIG_ROW_TILE[n])
    recv_in, recv_small = copies_wait("grads_late_wait", [in_flight["w_in"], in_flight["small"]], True, out["w_out"][0])
    out["w_in"] = adamw_shard("adamw_w_in", recv_in, w["w_in"], m["w_in"], v["w_in"], BIG_ROW_TILE["w_in"])
    sm = all_gather("gather_small_grads", [sum_slabs(recv_small)])[0].reshape(SM_ROWS, 1024)
    special_g = [sm[SM_WA:SM_WA + 64].reshape(16, 64, 64), sm[SM_WX:SM_WX + 64].reshape(16, 64, 64),
                 lax.dynamic_slice(sm[SM_META:SM_META + 16], (0, 128 * me), (16, 128)),
                 lax.dynamic_slice(sm[SM_SCW:SM_SCW + 8].reshape(4, 2048), (0, 192 * me), (4, 192)),
                 lax.dynamic_slice(sm[SM_LCW:SM_LCW + 4], (0, 128 * me), (4, 128))]
    names = [n for n, _ in SIMPLE] + SPECIAL
    res = adamw_small(sm, special_g, [w[n] for n in names], [m[n] for n in names], [v[n] for n in names])
    for k, (n, _) in enumerate(SIMPLE):
        out[n] = res[4 * k:4 * k + 4]
    for k, n in enumerate(SPECIAL):
        o = 4 * len(SIMPLE) + 3 * k
        out[n] = [special_g[k]] + list(res[o:o + 3])
    loss_total = sm[SM_LOSS, 0]
    flat = [loss_total, grad_x[None]]
    for k in range(4):
        flat += [out[n][k].reshape(shapes[n]) for n in WEIGHTS]
    return tuple(flat)
```

```python
import math

import jax
import jax.numpy as jnp
from jax import lax
from jax.experimental import pallas as pl
from jax.experimental.pallas import tpu as pltpu

F32 = jnp.float32
BF = jnp.bfloat16

D = 1024
SEQ = 2048
N_META = 16
Q = 128
NPAD = 112
T = NPAD + N_META + SEQ
NCH = T // Q
RC = 544
D_FF = 2816
SSD_W = 1024
LRU_W = 1024
XBC = 1536
IN_COLS = 4624
PZ, PG, PXL, PXBC, PDT = 0, 1024, 2048, 3072, 4608
NP_IN = 5120
EPS = 1e-6
LRU_C = 8.0
VMEM_LIMIT = 56 * 1024 * 1024

ADAM_LR, ADAM_B1, ADAM_B2, ADAM_EPS, ADAM_WD, ADAM_STEP = 0.001, 0.9, 0.999, 1e-08, 0.01, 10

NT_DIMS = (((1,), (1,)), ((), ()))
TN_DIMS = (((0,), (0,)), ((), ()))
MESH = pl.DeviceIdType.MESH


def _params(n_grid=1, limit=VMEM_LIMIT):
    return pltpu.CompilerParams(dimension_semantics=("arbitrary",) * n_grid, vmem_limit_bytes=limit)


def _spec(shape, imap, single=False):
    if single:
        return pl.BlockSpec(shape, imap, pipeline_mode=pl.Buffered(1))
    return pl.BlockSpec(shape, imap)


def _sigmoid(x):
    return 1.0 / (1.0 + jnp.exp(-x))


def _softplus(x):
    return jnp.maximum(x, 0.0) + jnp.log(1.0 + jnp.exp(-jnp.abs(x)))


def _rms_stats(h):
    return lax.rsqrt(jnp.mean(h * h, axis=-1, keepdims=True) + EPS)


def _rms(h, w):
    return (h * _rms_stats(h)) * w


def _rms_bwd(du, h, w):
    r = _rms_stats(h)
    n = h * r
    dn = du * w
    dh = r * (dn - n * jnp.mean(dn * n, axis=-1, keepdims=True))
    return dh, du * n


_G0 = math.sqrt(2.0 / math.pi)


def _gelu(x):
    return 0.5 * x * (1.0 + jnp.tanh(_G0 * (x + 0.044715 * (x * x * x))))


def _gelu_grad(x):
    t = jnp.tanh(_G0 * (x + 0.044715 * (x * x * x)))
    return 0.5 * (1.0 + t) + 0.5 * x * (1.0 - t * t) * (_G0 * (1.0 + 3.0 * 0.044715 * (x * x)))


def _rows(shape, r0=0):
    return lax.broadcasted_iota(jnp.int32, shape, 0) + r0


def _lanes(shape):
    return lax.broadcasted_iota(jnp.int32, shape, 1)


HALO = 8


def _fill_padded(pad_ref, x_ref):
    pad_ref[0:HALO, :] = jnp.zeros((HALO, pad_ref.shape[1]), F32)
    pad_ref[T + HALO:T + 2 * HALO, :] = jnp.zeros((HALO, pad_ref.shape[1]), F32)

    def step(c, carry):
        r0 = pl.multiple_of(c * Q, Q)
        pad_ref[pl.ds(r0 + HALO, Q), :] = x_ref[pl.ds(r0, Q), :]
        return carry

    lax.fori_loop(0, NCH, step, 0)


def _back(pad_ref, r0):
    win = pad_ref[pl.ds(r0, Q + HALO), :]
    return lambda s: win[HALO:, :] if s == 0 else pltpu.roll(win, s, axis=0)[HALO:, :]


def _ahead(pad_ref, r0):
    win = pad_ref[pl.ds(r0 + HALO, Q + HALO), :]
    return lambda s: win[:Q, :] if s == 0 else pltpu.roll(win, Q + HALO - s, axis=0)[:Q, :]


def _conv(back, w, b):
    y = b + w[3:4, :] * back(0)
    for k in range(3):
        y = y + w[k:k + 1, :] * back(3 - k)
    return y


def _conv_bwd_x(ahead, w):
    dx = w[3:4, :] * ahead(0)
    for k in range(3):
        dx = dx + w[k:k + 1, :] * ahead(3 - k)
    return dx


def _conv_bwd_w(dy, back):
    dws = [jnp.sum(dy * back(3 - k), axis=0, keepdims=True) for k in range(4)]
    return jnp.concatenate(dws, axis=0), jnp.sum(dy, axis=0, keepdims=True)


def _chunks(fn):
    def step(c, carry):
        fn(pl.multiple_of(c * Q, Q))
        return carry

    lax.fori_loop(0, NCH, step, 0)


HALF = RC // 2


def _col_tiles(n, tn, fn):
    def step(j, carry):
        fn(pl.multiple_of(j * tn, tn))
        return carry

    lax.fori_loop(0, n // tn, step, 0)


def _rows_spec(cols, block_col=0):
    return _spec((RC, cols), lambda i: (i, block_col))


def _whole(shape):
    return _spec(shape, lambda i: tuple(0 for _ in shape), single=True)


def _vec(cols):
    return _spec((1, cols), lambda i: (0, 0))


def _zero_at_first(*refs):
    @pl.when(pl.program_id(0) == 0)
    def _():
        for r in refs:
            r[...] = jnp.zeros_like(r)


def in_proj(h0, wn, w_p):
    def body(h_ref, wn_ref, w_ref, o_ref, u_ref):
        for r in (0, HALF):
            u_ref[r:r + HALF, :] = _rms(h_ref[r:r + HALF, :], wn_ref[...]).astype(BF)

        def tile(c0):
            o_ref[:, pl.ds(c0, 512)] = jnp.dot(u_ref[...], w_ref[:, pl.ds(c0, 512)], preferred_element_type=F32)

        _col_tiles(NP_IN, 512, tile)

    return pl.pallas_call(
        body, grid=(T // RC,), in_specs=[_rows_spec(D), _vec(D), _whole((D, NP_IN))],
        out_specs=[_rows_spec(NP_IN), _rows_spec(D)],
        out_shape=[jax.ShapeDtypeStruct((T, NP_IN), F32), jax.ShapeDtypeStruct((T, D), BF)],
        compiler_params=_params(), name="in_proj")(h0, wn, w_p)


def out_proj(yn_ssd, proj, hseq, lru_nw, w_out, h0):
    def body(y_ref, g_ref, h_ref, wn_ref, w_ref, r_ref, o_ref, cat_ref):
        cat_ref[:, 0:SSD_W] = y_ref[...]
        for r in (0, HALF):
            y = _gelu(g_ref[r:r + HALF, :]) * h_ref[r:r + HALF, :]
            cat_ref[r:r + HALF, SSD_W:] = _rms(y, wn_ref[...]).astype(BF)

        def tile(c0):
            o_ref[:, pl.ds(c0, 512)] = r_ref[:, pl.ds(c0, 512)] + jnp.dot(cat_ref[...], w_ref[:, pl.ds(c0, 512)], preferred_element_type=F32)

        _col_tiles(D, 512, tile)

    return pl.pallas_call(
        body, grid=(T // RC,),
        in_specs=[_rows_spec(SSD_W), _rows_spec(LRU_W, PG // LRU_W), _rows_spec(LRU_W), _vec(LRU_W), _whole((SSD_W + LRU_W, D)), _rows_spec(D)],
        out_specs=[_rows_spec(D), _rows_spec(SSD_W + LRU_W)],
        out_shape=[jax.ShapeDtypeStruct((T, D), F32), jax.ShapeDtypeStruct((T, SSD_W + LRU_W), BF)],
        compiler_params=_params(), name="out_proj")(yn_ssd, proj, hseq, lru_nw, w_out, h0)


def out_proj_bwd(dh1_b, w_out, proj, hseq, lru_nw):
    def body(d_ref, w_ref, g_ref, h_ref, wn_ref, dy_ref, dh_ref, dg_ref, dw_ref, dl_scr):
        _zero_at_first(dw_ref)

        def tile(c0):
            dy_ref[:, pl.ds(c0, 512)] = lax.dot_general(d_ref[...], w_ref[pl.ds(c0, 512), :], NT_DIMS, preferred_element_type=F32)
            dl_scr[:, pl.ds(c0, 512)] = lax.dot_general(d_ref[...], w_ref[pl.ds(SSD_W + c0, 512), :], NT_DIMS, preferred_element_type=F32)

        _col_tiles(SSD_W, 512, tile)
        for r in (0, HALF):
            g = g_ref[r:r + HALF, :]
            h = h_ref[r:r + HALF, :]
            ge = _gelu(g)
            dy, dw = _rms_bwd(dl_scr[r:r + HALF, :], ge * h, wn_ref[...])
            dw_ref[...] += jnp.sum(dw, axis=0, keepdims=True)
            dh_ref[r:r + HALF, :] = dy * ge
            dg_ref[r:r + HALF, :] = (dy * h * _gelu_grad(g)).astype(BF)

    return pl.pallas_call(
        body, grid=(T // RC,),
        in_specs=[_rows_spec(D), _whole((SSD_W + LRU_W, D)), _rows_spec(LRU_W, PG // LRU_W), _rows_spec(LRU_W), _vec(LRU_W)],
        out_specs=[_rows_spec(SSD_W), _rows_spec(LRU_W), _rows_spec(LRU_W), _vec(LRU_W)],
        out_shape=[jax.ShapeDtypeStruct((T, SSD_W), F32), jax.ShapeDtypeStruct((T, LRU_W), F32), jax.ShapeDtypeStruct((T, LRU_W), BF),
                   jax.ShapeDtypeStruct((1, LRU_W), F32)],
        scratch_shapes=[pltpu.VMEM((RC, LRU_W), F32)],
        compiler_params=_params(), name="out_proj_bwd")(dh1_b, w_out, proj, hseq, lru_nw)


def in_proj_bwd(dproj, w_p, h0, wn, dh1):
    def body(d_ref, w_ref, h_ref, wn_ref, r_ref, o_ref, dw_ref, du_scr):
        _zero_at_first(dw_ref)

        def tile(c0):
            du_scr[:, pl.ds(c0, 512)] = lax.dot_general(d_ref[...], w_ref[pl.ds(c0, 512), :], NT_DIMS, preferred_element_type=F32)

        _col_tiles(D, 512, tile)
        for r in (0, HALF):
            dh, dw = _rms_bwd(du_scr[r:r + HALF, :], h_ref[r:r + HALF, :], wn_ref[...])
            dw_ref[...] += jnp.sum(dw, axis=0, keepdims=True)
            o_ref[r:r + HALF, :] = dh + r_ref[r:r + HALF, :]

    return pl.pallas_call(
        body, grid=(T // RC,), in_specs=[_rows_spec(NP_IN), _whole((D, NP_IN)), _rows_spec(D), _vec(D), _rows_spec(D)],
        out_specs=[_rows_spec(D), _vec(D)],
        out_shape=[jax.ShapeDtypeStruct((T, D), F32), jax.ShapeDtypeStruct((1, D), F32)],
        scratch_shapes=[pltpu.VMEM((RC, D), F32)],
        compiler_params=_params(), name="in_proj_bwd")(dproj, w_p, h0, wn, dh1)


def matmul_tn(name, a, b, tm, tn):
    m, n = a.shape[1], b.shape[1]

    def body(a_ref, b_ref, o_ref, acc_ref):
        acc_ref[...] = jnp.zeros_like(acc_ref)

        def mm(r0):
            acc_ref[...] += lax.dot_general(a_ref[pl.ds(r0, RC), :], b_ref[pl.ds(r0, RC), :], TN_DIMS, preferred_element_type=F32)

        _col_tiles(T, RC, mm)
        o_ref[...] = acc_ref[...].astype(BF)

    return pl.pallas_call(
        body, grid=(m // tm, n // tn),
        in_specs=[_spec((T, tm), lambda i, j: (0, i)), _spec((T, tn), lambda i, j: (0, j))],
        out_specs=_spec((tm, tn), lambda i, j: (i, j)),
        out_shape=jax.ShapeDtypeStruct((m, n), BF),
        scratch_shapes=[pltpu.VMEM((tm, tn), F32)],
        compiler_params=_params(2), name=name)(a, b)


def conv_silu_fwd(proj, cw, cb):
    def body(x_ref, w_ref, b_ref, o_ref, xpad):
        _fill_padded(xpad, x_ref)

        def chunk(r0):
            pre = _conv(_back(xpad, r0), w_ref[...], b_ref[...])
            o_ref[pl.ds(r0, Q), :] = pre * _sigmoid(pre)

        _chunks(chunk)

    c0 = PXBC // 128
    return pl.pallas_call(
        body, grid=(XBC // 128,),
        in_specs=[_spec((T, 128), lambda c: (0, c0 + c)), _spec((4, 128), lambda c: (0, c)), _spec((1, 128), lambda c: (0, c))],
        out_specs=_spec((T, 128), lambda c: (0, c)),
        out_shape=jax.ShapeDtypeStruct((T, XBC), F32), scratch_shapes=[pltpu.VMEM((T + 2 * HALO, 128), F32)],
        compiler_params=_params(), name="conv_silu_fwd")(proj, cw, cb)


def conv_silu_bwd(dx, d_b, d_c, proj, cw, cb):
    def body(dx_ref, db_ref, dc_ref, x_ref, w_ref, b_ref, o_ref, dw_ref, dbias_ref, xpad, dpad):
        tile = pl.program_id(0)
        _fill_padded(xpad, x_ref)
        dpad[0:HALO, :] = jnp.zeros((HALO, 128), F32)
        dpad[T + HALO:T + 2 * HALO, :] = jnp.zeros((HALO, 128), F32)
        dw_ref[...] = jnp.zeros_like(dw_ref)
        dbias_ref[...] = jnp.zeros_like(dbias_ref)

        def first(r0):
            back = _back(xpad, r0)
            pre = _conv(back, w_ref[...], b_ref[...])
            sg = _sigmoid(pre)
            rows = pl.ds(r0, Q)
            d = jnp.where(tile < 8, dx_ref[rows, :], jnp.where(tile < 10, db_ref[rows, :], dc_ref[rows, :]))
            dpre = d * (sg * (1.0 + pre * (1.0 - sg)))
            dpad[pl.ds(r0 + HALO, Q), :] = dpre
            dw, dbias = _conv_bwd_w(dpre, back)
            dw_ref[...] += dw
            dbias_ref[...] += dbias

        _chunks(first)

        def second(r0):
            o_ref[pl.ds(r0, Q), :] = _conv_bwd_x(_ahead(dpad, r0), w_ref[...]).astype(BF)

        _chunks(second)

    c0 = PXBC // 128
    pad = pltpu.VMEM((T + 2 * HALO, 128), F32)
    return pl.pallas_call(
        body, grid=(XBC // 128,),
        in_specs=[_spec((T, 128), lambda c: (0, jnp.minimum(c, 7))), _spec((T, 128), lambda c: (0, jnp.clip(c - 8, 0, 1))),
                  _spec((T, 128), lambda c: (0, jnp.clip(c - 10, 0, 1))), _spec((T, 128), lambda c: (0, c0 + c)),
                  _spec((4, 128), lambda c: (0, c)), _spec((1, 128), lambda c: (0, c))],
        out_specs=[_spec((T, 128), lambda c: (0, c)), _spec((4, 128), lambda c: (0, c)), _spec((1, 128), lambda c: (0, c))],
        out_shape=[jax.ShapeDtypeStruct((T, XBC), BF), jax.ShapeDtypeStruct((4, XBC), F32), jax.ShapeDtypeStruct((1, XBC), F32)],
        scratch_shapes=[pad, pad], compiler_params=_params(), name="conv_silu_bwd")(dx, d_b, d_c, proj, cw, cb)


def _ssd_chunk_common(row0, dt_ref, b_ref, c_ref, bias, a_neg):
    shape = (Q, Q)
    lane = _lanes(shape)
    sub = _rows(shape)
    live = (_rows(shape, row0) >= NPAD) & (lane < 8)
    dtr = dt_ref[:, :]
    dt = jnp.where(live, _softplus(dtr + bias), 0.0)
    d_a = dt * a_neg
    tri = (sub >= lane).astype(F32)
    cs = jnp.dot(tri, d_a, precision=lax.Precision.HIGHEST, preferred_element_type=F32)
    cs_t = cs.T
    bc = b_ref[:, :].astype(BF)
    cc = c_ref[:, :].astype(BF)
    cb = lax.dot_general(cc, bc, NT_DIMS, preferred_element_type=F32)
    cs_last = cs[Q - 1:Q, :]
    return dict(lane=lane, sub=sub, live=live, dtr=dtr, dt=dt, cs=cs, cs_t=cs_t, bc=bc, cc=cc, cb=cb,
                ecs=jnp.exp(cs), dsm=jnp.exp(cs_last - cs), gam=jnp.exp(cs_last), tri=tri)


def _pair(lane_even, mat, j):
    return jnp.where(lane_even, mat[:, j:j + 1], mat[:, j + 1:j + 2])


def _head_decay(cm, j):
    seg = cm["cs"][:, j:j + 1] - cm["cs_t"][j:j + 1, :]
    return jnp.exp(jnp.where(cm["sub"] >= cm["lane"], seg, -jnp.inf))


def ssd_fwd(xbc_act, proj, dt_bias2, a_log2, d2, norm_w):
    def body(x_ref, b_ref, c_ref, dt_ref, z_ref, bias_ref, alog_ref, d_ref, nw_ref, yn_ref, y_ref, hp_ref, h_scr):
        c = pl.program_id(1)

        @pl.when(c == 0)
        def _():
            h_scr[...] = jnp.zeros_like(h_scr)

        bias = bias_ref[0]
        a_neg = -jnp.exp(alog_ref[0])
        dsk = d_ref[0]
        cm = _ssd_chunk_common(c * Q, dt_ref, b_ref, c_ref, bias, a_neg)
        lane_even = cm["lane"] < 64
        sub_even = cm["sub"] < 64
        for p in range(4):
            je, jo = 2 * p, 2 * p + 1
            xp = x_ref[:, 128 * p:128 * p + 128]
            xdt = xp * _pair(lane_even, cm["dt"], je)
            xdt_b = xdt.astype(BF)
            m_e = (cm["cb"] * _head_decay(cm, je)).astype(BF)
            m_o = (cm["cb"] * _head_decay(cm, jo)).astype(BF)
            zero = jnp.zeros_like(xdt_b)
            yd = (jnp.dot(m_e, jnp.where(lane_even, xdt_b, zero), preferred_element_type=F32)
                  + jnp.dot(m_o, jnp.where(lane_even, zero, xdt_b), preferred_element_type=F32))
            hp = h_scr[p]
            hp_ref[0, 0, p] = hp
            yo = lax.dot_general(cm["cc"], hp.astype(BF), NT_DIMS, preferred_element_type=F32) * _pair(lane_even, cm["ecs"], je)
            dsk_p = jnp.where(lane_even[0:1, :], dsk[:, je:je + 1], dsk[:, jo:jo + 1])
            y_ref[:, 128 * p:128 * p + 128] = yd + yo + xp * dsk_p
            st = lax.dot_general((xdt * _pair(lane_even, cm["dsm"], je)).astype(BF), cm["bc"], TN_DIMS, preferred_element_type=F32)
            gam = jnp.where(sub_even[:, 0:1], cm["gam"][:, je:je + 1], cm["gam"][:, jo:jo + 1])
            h_scr[p] = hp * gam + st
        zc = z_ref[:, :]
        gated = y_ref[:, :] * (zc * _sigmoid(zc))
        yn_ref[:, :] = _rms(gated, nw_ref[...]).astype(BF)

    par = _spec((1, 1, 128), lambda g, c: (g, 0, 0))
    wide = _spec((Q, 512), lambda g, c: (c, g))
    return pl.pallas_call(
        body, grid=(2, NCH),
        in_specs=[wide, _spec((Q, 128), lambda g, c: (c, 8 + g)), _spec((Q, 128), lambda g, c: (c, 10 + g)),
                  _spec((Q, 128), lambda g, c: (c, PDT // 128 + g)), wide, par, par, par, _spec((1, 512), lambda g, c: (0, g))],
        out_specs=[wide, wide, _spec((1, 1, 4, 128, 128), lambda g, c: (g, c, 0, 0, 0))],
        out_shape=[jax.ShapeDtypeStruct((T, SSD_W), BF), jax.ShapeDtypeStruct((T, SSD_W), F32),
                   jax.ShapeDtypeStruct((2, NCH, 4, 128, 128), F32)],
        scratch_shapes=[pltpu.VMEM((4, 128, 128), F32)],
        compiler_params=_params(2), name="ssd_fwd")(xbc_act, xbc_act, xbc_act, proj, proj, dt_bias2, a_log2, d2, norm_w)


def ssd_bwd(dyn, xbc_act, proj, y_pre, h_prev, dt_bias2, a_log2, d2, norm_w):
    def body(dyn_ref, x_ref, b_ref, c_ref, dt_ref, z_ref, y_ref, hp_ref, bias_ref, alog_ref, d_ref, nw_ref,
             dz_ref, dx_ref, db_ref, dc_ref, ddt_ref, dpar_ref, dnw_ref, dh_scr, acc_scr):
        ci = pl.program_id(1)

        @pl.when(ci == 0)
        def _():
            dh_scr[...] = jnp.zeros_like(dh_scr)
            acc_scr[...] = jnp.zeros_like(acc_scr)
            dnw_ref[...] = jnp.zeros_like(dnw_ref)

        bias = bias_ref[0]
        a_neg = -jnp.exp(alog_ref[0])
        dsk = d_ref[0]
        cm = _ssd_chunk_common((NCH - 1 - ci) * Q, dt_ref, b_ref, c_ref, bias, a_neg)
        lane, sub = cm["lane"], cm["sub"]
        lane_even = lane < 64
        sub_even = sub < 64
        zc = z_ref[:, :]
        yc = y_ref[:, :]
        sg = _sigmoid(zc)
        sz = zc * sg
        dgated, dnw = _rms_bwd(dyn_ref[:, :], yc * sz, nw_ref[...])
        dnw_ref[...] += jnp.sum(dnw, axis=0, keepdims=True)
        dz_ref[:, :] = (dgated * yc * (sg * (1.0 + zc * (1.0 - sg)))).astype(BF)
        dy_all = dgated * sz
        dcb = jnp.zeros((Q, Q), F32)
        db_acc = jnp.zeros((Q, Q), F32)
        dc_acc = jnp.zeros((Q, Q), F32)
        dcs_col = jnp.zeros((Q, Q), F32)
        dcs_row = jnp.zeros((Q, Q), F32)
        ddt = jnp.zeros((Q, Q), F32)
        for p in range(4):
            je, jo = 2 * p, 2 * p + 1
            xp = x_ref[:, 128 * p:128 * p + 128]
            dy = dy_all[:, 128 * p:128 * p + 128]
            dt_p = _pair(lane_even, cm["dt"], je)
            xdt = xp * dt_p
            xdt_b = xdt.astype(BF)
            dy_b = dy.astype(BF)
            zero = jnp.zeros_like(dy_b)
            hp = hp_ref[0, 0, p]
            hp_b = hp.astype(BF)
            dh = dh_scr[p]
            dh_b = dh.astype(BF)
            acc_scr[p:p + 1, :] += jnp.sum(dy * xp, axis=0, keepdims=True)
            dsk_p = jnp.where(lane_even[0:1, :], dsk[:, je:je + 1], dsk[:, jo:jo + 1])
            dxp = dy * dsk_p
            e_p = _pair(lane_even, cm["ecs"], je)
            g_p = lax.dot_general(cm["cc"], hp_b, NT_DIMS, preferred_element_type=F32)
            dg_b = (dy * e_p).astype(BF)
            de = dy * g_p * e_p
            dc_acc = dc_acc + jnp.dot(dg_b, hp_b, preferred_element_type=F32)
            dh_in = lax.dot_general(dg_b, cm["cc"], TN_DIMS, preferred_element_type=F32)
            ds_p = _pair(lane_even, cm["dsm"], je)
            r_p = lax.dot_general(cm["bc"], dh_b, NT_DIMS, preferred_element_type=F32)
            dxdt = r_p * ds_p
            tt = r_p * xdt * ds_p
            db_acc = db_acc + jnp.dot((xdt * ds_p).astype(BF), dh_b, preferred_element_type=F32)
            dgam_m = dh * hp
            for j, even in ((je, True), (jo, False)):
                sel = lane_even if even else jnp.logical_not(lane_even)
                ssel = sub_even if even else jnp.logical_not(sub_even)
                l_j = _head_decay(cm, j)
                m_j = cm["cb"] * l_j
                dm = lax.dot_general(jnp.where(sel, dy_b, zero), xdt_b, NT_DIMS, preferred_element_type=F32)
                dxdt = dxdt + lax.dot_general(m_j.astype(BF), jnp.where(sel, dy_b, zero), TN_DIMS, preferred_element_type=F32)
                w_j = dm * m_j
                dcb = dcb + dm * l_j
                t_j = jnp.sum(jnp.where(sel, tt, 0.0), axis=1, keepdims=True)
                col = (jnp.sum(w_j, axis=1, keepdims=True) + jnp.sum(jnp.where(sel, de, 0.0), axis=1, keepdims=True) - t_j)
                gam_j = cm["gam"][:, j:j + 1]
                last = (jnp.sum(t_j, axis=0, keepdims=True)
                        + jnp.sum(jnp.sum(jnp.where(ssel, dgam_m, 0.0), axis=1, keepdims=True), axis=0, keepdims=True) * gam_j)
                col = col + jnp.where(sub[:, 0:1] == Q - 1, last, 0.0)
                dcs_col = dcs_col + jnp.where(lane == j, col, 0.0)
                dcs_row = dcs_row + jnp.where(sub == j, jnp.sum(w_j, axis=0, keepdims=True), 0.0)
            gam = jnp.where(sub_even[:, 0:1], cm["gam"][:, je:je + 1], cm["gam"][:, jo:jo + 1])
            dh_scr[p] = dh_in + dh * gam
            dx_ref[:, 128 * p:128 * p + 128] = dxp + dxdt * dt_p
            dd = dxdt * xp
            ddt = ddt + jnp.where(lane == je, jnp.sum(jnp.where(lane_even, dd, 0.0), axis=1, keepdims=True), 0.0)
            ddt = ddt + jnp.where(lane == jo, jnp.sum(jnp.where(lane_even, 0.0, dd), axis=1, keepdims=True), 0.0)
        dcb_b = dcb.astype(BF)
        dc_ref[:, :] = dc_acc + jnp.dot(dcb_b, cm["bc"], preferred_element_type=F32)
        db_ref[:, :] = db_acc + lax.dot_general(dcb_b, cm["cc"], TN_DIMS, preferred_element_type=F32)
        dcs = dcs_col - dcs_row.T
        dd_a = lax.dot_general(cm["tri"], dcs, TN_DIMS, precision=lax.Precision.HIGHEST, preferred_element_type=F32)
        ddt = ddt + dd_a * a_neg
        acc_scr[5:6, :] += jnp.sum(dd_a * cm["dt"], axis=0, keepdims=True)
        draw = jnp.where(cm["live"], ddt * _sigmoid(cm["dtr"] + bias), 0.0)
        acc_scr[4:5, :] += jnp.sum(draw, axis=0, keepdims=True)
        ddt_ref[:, :] = draw.astype(BF)

        @pl.when(ci == NCH - 1)
        def _():
            lane1 = _lanes((1, 128))
            dd = jnp.zeros((1, 128), F32)
            for p in range(4):
                row = acc_scr[p:p + 1, :]
                dd = dd + jnp.where(lane1 == 2 * p, jnp.sum(jnp.where(lane1 < 64, row, 0.0), axis=1, keepdims=True), 0.0)
                dd = dd + jnp.where(lane1 == 2 * p + 1, jnp.sum(jnp.where(lane1 < 64, 0.0, row), axis=1, keepdims=True), 0.0)
            dpar_ref[0] = jnp.concatenate([acc_scr[4:5, :], acc_scr[5:6, :] * a_neg, dd, jnp.zeros((5, 128), F32)], axis=0)

    par = _spec((1, 1, 128), lambda g, c: (g, 0, 0))
    wide = _spec((Q, 512), lambda g, c: (NCH - 1 - c, g))
    thin = _spec((Q, 128), lambda g, c: (NCH - 1 - c, g))
    return pl.pallas_call(
        body, grid=(2, NCH),
        in_specs=[wide, wide, _spec((Q, 128), lambda g, c: (NCH - 1 - c, 8 + g)), _spec((Q, 128), lambda g, c: (NCH - 1 - c, 10 + g)),
                  _spec((Q, 128), lambda g, c: (NCH - 1 - c, PDT // 128 + g)), wide, wide,
                  _spec((1, 1, 4, 128, 128), lambda g, c: (g, NCH - 1 - c, 0, 0, 0)), par, par, par, _spec((1, 512), lambda g, c: (0, g))],
        out_specs=[wide, wide, thin, thin, thin, _spec((1, 8, 128), lambda g, c: (g, 0, 0)), _spec((1, 512), lambda g, c: (0, g))],
        out_shape=[jax.ShapeDtypeStruct((T, SSD_W), BF), jax.ShapeDtypeStruct((T, SSD_W), F32), jax.ShapeDtypeStruct((T, 256), F32),
                   jax.ShapeDtypeStruct((T, 256), F32), jax.ShapeDtypeStruct((T, 256), BF), jax.ShapeDtypeStruct((2, 8, 128), F32),
                   jax.ShapeDtypeStruct((1, SSD_W), F32)],
        scratch_shapes=[pltpu.VMEM((4, 128, 128), F32), pltpu.VMEM((8, 128), F32)],
        compiler_params=_params(2), name="ssd_bwd")(dyn, xbc_act, xbc_act, xbc_act, proj, proj, y_pre, h_prev, dt_bias2, a_log2, d2, norm_w)


def _lru_gates(back, cw, cb, wa, ba, wx, bx, lam):
    xr = _conv(back, cw, cb)
    xr_b = xr.astype(BF)
    r = _sigmoid(jnp.dot(xr_b, wa, preferred_element_type=F32) + ba)
    i = _sigmoid(jnp.dot(xr_b, wx, preferred_element_type=F32) + bx)
    sp = _softplus(-lam)
    la = (-LRU_C) * r * sp
    a = jnp.exp(la)
    mult = jnp.sqrt(-jnp.tanh(la) * (a * a + 1.0))
    return xr, xr_b, r, i, sp, a, mult


def lru_gates_fwd(proj, cw, cb, wa2, ba, wx2, bx, lam):
    def body(x_ref, cw_ref, cb_ref, wa_ref, ba_ref, wx_ref, bx_ref, lam_ref, a_ref, u_ref, xpad):
        _fill_padded(xpad, x_ref)

        def chunk(r0):
            xr, _, _, i, _, a, mult = _lru_gates(_back(xpad, r0), cw_ref[...], cb_ref[...], wa_ref[0], ba_ref[...], wx_ref[0], bx_ref[...],
                                                 lam_ref[...])
            a_ref[pl.ds(r0, Q), :] = a
            u_ref[pl.ds(r0, Q), :] = jnp.where(_rows(a.shape, r0) >= NPAD, mult * (i * xr), 0.0)

        _chunks(chunk)

    c0 = PXL // 128
    vec = _spec((1, 128), lambda c: (0, c))
    mat = _spec((1, 128, 128), lambda c: (c, 0, 0))
    return pl.pallas_call(
        body, grid=(8,),
        in_specs=[_spec((T, 128), lambda c: (0, c0 + c)), _spec((4, 128), lambda c: (0, c)), vec, mat, vec, mat, vec, vec],
        out_specs=[_spec((T, 128), lambda c: (0, c)), _spec((T, 128), lambda c: (0, c))],
        out_shape=[jax.ShapeDtypeStruct((T, LRU_W), F32), jax.ShapeDtypeStruct((T, LRU_W), F32)],
        scratch_shapes=[pltpu.VMEM((T + 2 * HALO, 128), F32)],
        compiler_params=_params(), name="lru_gates_fwd")(proj, cw, cb, wa2, ba, wx2, bx, lam)


def lru_scan_fwd(a, u):
    def body(a_ref, u_ref, h_ref):
        def step(i, h):
            base = pl.multiple_of(i * 8, 8)
            for k in range(8):
                h = a_ref[pl.ds(base + k, 1), :] * h + u_ref[pl.ds(base + k, 1), :]
                h_ref[pl.ds(base + k, 1), :] = h
            return h

        lax.fori_loop(0, T // 8, step, jnp.zeros((1, LRU_W), F32))

    return pl.pallas_call(body, out_shape=jax.ShapeDtypeStruct((T, LRU_W), F32), compiler_params=_params(0), name="lru_scan_fwd")(a, u)


def lru_scan_bwd(a, dh_out):
    def body(a_ref, d_ref, o_ref):
        def step(i, carry):
            base = pl.multiple_of(T - 8 - i * 8, 8)
            for k in range(7, -1, -1):
                carry = d_ref[pl.ds(base + k, 1), :] + carry
                o_ref[pl.ds(base + k, 1), :] = carry
                carry = carry * a_ref[pl.ds(base + k, 1), :]
            return carry

        lax.fori_loop(0, T // 8, step, jnp.zeros((1, LRU_W), F32))

    return pl.pallas_call(body, out_shape=jax.ShapeDtypeStruct((T, LRU_W), F32), compiler_params=_params(0), name="lru_scan_bwd")(a, dh_out)


def lru_gates_bwd(dhs, hseq, proj, cw, cb, wa2, ba, wx2, bx, lam):
    def body(dh_ref, h_ref, x_ref, cw_ref, cb_ref, wa_ref, ba_ref, wx_ref, bx_ref, lam_ref,
             dx_ref, dcw_ref, dcb_ref, dwa_ref, dba_ref, dwx_ref, dbx_ref, dlam_ref, xpad, hpad, dpad):
        _fill_padded(xpad, x_ref)
        _fill_padded(hpad, h_ref)
        dpad[0:HALO, :] = jnp.zeros((HALO, 128), F32)
        dpad[T + HALO:T + 2 * HALO, :] = jnp.zeros((HALO, 128), F32)
        for ref in (dcw_ref, dcb_ref, dwa_ref, dba_ref, dwx_ref, dbx_ref, dlam_ref):
            ref[...] = jnp.zeros_like(ref)
        lam = lam_ref[...]

        def first(r0):
            back = _back(xpad, r0)
            xr, xr_b, r, i, sp, a, mult = _lru_gates(back, cw_ref[...], cb_ref[...], wa_ref[0], ba_ref[...], wx_ref[0], bx_ref[...], lam)
            dh = dh_ref[pl.ds(r0, Q), :]
            da = dh * _back(hpad, r0)(1)
            du = jnp.where(_rows(dh.shape, r0) >= NPAD, dh, 0.0)
            dmult = du * (i * xr)
            di = du * (mult * xr)
            dxr = du * (mult * i)
            dla = da * a - dmult * (a * a) / mult
            dr = dla * ((-LRU_C) * sp)
            dlam_ref[...] += jnp.sum(dla * ((-LRU_C) * r), axis=0, keepdims=True)
            dpr = dr * r * (1.0 - r)
            dpi = di * i * (1.0 - i)
            dba_ref[...] += jnp.sum(dpr, axis=0, keepdims=True)
            dbx_ref[...] += jnp.sum(dpi, axis=0, keepdims=True)
            dpr_b = dpr.astype(BF)
            dpi_b = dpi.astype(BF)
            dxr = (dxr + lax.dot_general(dpr_b, wa_ref[0], NT_DIMS, preferred_element_type=F32)
                   + lax.dot_general(dpi_b, wx_ref[0], NT_DIMS, preferred_element_type=F32))
            dwa_ref[0] += lax.dot_general(xr_b, dpr_b, TN_DIMS, preferred_element_type=F32)
            dwx_ref[0] += lax.dot_general(xr_b, dpi_b, TN_DIMS, preferred_element_type=F32)
            dpad[pl.ds(r0 + HALO, Q), :] = dxr
            dcw, dcb = _conv_bwd_w(dxr, back)
            dcw_ref[...] += dcw
            dcb_ref[...] += dcb

        _chunks(first)
        dlam_ref[...] = -dlam_ref[...] * _sigmoid(-lam)

        def second(r0):
            dx_ref[pl.ds(r0, Q), :] = _conv_bwd_x(_ahead(dpad, r0), cw_ref[...]).astype(BF)

        _chunks(second)

    c0 = PXL // 128
    vec = _spec((1, 128), lambda c: (0, c))
    mat = _spec((1, 128, 128), lambda c: (c, 0, 0))
    col = _spec((T, 128), lambda c: (0, c))
    vshape = jax.ShapeDtypeStruct((1, LRU_W), F32)
    mshape = jax.ShapeDtypeStruct((8, 128, 128), F32)
    pad = pltpu.VMEM((T + 2 * HALO, 128), F32)
    return pl.pallas_call(
        body, grid=(8,),
        in_specs=[col, col, _spec((T, 128), lambda c: (0, c0 + c)), _spec((4, 128), lambda c: (0, c)), vec, mat, vec, mat, vec, vec],
        out_specs=[col, _spec((4, 128), lambda c: (0, c)), vec, mat, vec, mat, vec, vec],
        out_shape=[jax.ShapeDtypeStruct((T, LRU_W), BF), jax.ShapeDtypeStruct((4, LRU_W), F32), vshape, mshape, vshape, mshape, vshape, vshape],
        scratch_shapes=[pad, pad, pad], compiler_params=_params(), name="lru_gates_bwd")(dhs, hseq, proj, cw, cb, wa2, ba, wx2, bx, lam)


def gate_up(h1, wn, w_gate, w_up):
    def body(h_ref, wn_ref, wg_ref, wu_ref, gt_ref, up_ref, act_ref, u_ref):
        for r in (0, HALF):
            u_ref[r:r + HALF, :] = _rms(h_ref[r:r + HALF, :], wn_ref[...]).astype(BF)

        def tile(c0):
            cols = pl.ds(c0, 256)
            gt = jnp.dot(u_ref[...], wg_ref[:, cols], preferred_element_type=F32)
            up = jnp.dot(u_ref[...], wu_ref[:, cols], preferred_element_type=F32)
            gt_ref[:, cols] = gt.astype(BF)
            up_ref[:, cols] = up.astype(BF)
            act_ref[:, cols] = (gt * _sigmoid(gt) * up).astype(BF)

        _col_tiles(D_FF, 256, tile)

    big = jax.ShapeDtypeStruct((T, D_FF), BF)
    return pl.pallas_call(
        body, grid=(T // RC,), in_specs=[_rows_spec(D), _vec(D), _whole((D, D_FF)), _whole((D, D_FF))],
        out_specs=[_rows_spec(D_FF), _rows_spec(D_FF), _rows_spec(D_FF), _rows_spec(D)],
        out_shape=[big, big, big, jax.ShapeDtypeStruct((T, D), BF)],
        compiler_params=_params(), name="gate_up")(h1, wn, w_gate, w_up)


def down_loss(act, w_down, h1, target, wf):
    first = NPAD + N_META

    def body(a_ref, w_ref, r_ref, t_hbm, wf_ref, d_ref, db_ref, l_ref, dw_ref, h_scr, t_ref, t_sem):
        i = pl.program_id(0)
        _zero_at_first(l_ref, dw_ref)
        head = pltpu.make_async_copy(t_hbm.at[pl.ds(0, RC - first)], t_ref.at[pl.ds(first, RC - first)], t_sem)
        rest = pltpu.make_async_copy(t_hbm.at[pl.ds(pl.multiple_of(jnp.maximum(i * RC - first, 0), 32), RC)], t_ref, t_sem)

        @pl.when(i == 0)
        def _():
            t_ref[0:first, :] = jnp.zeros((first, D), F32)
            head.start()

        @pl.when(i > 0)
        def _():
            rest.start()

        def tile(c0):
            cols = pl.ds(c0, 512)
            h_scr[:, cols] = r_ref[:, cols] + jnp.dot(a_ref[...], w_ref[:, cols], preferred_element_type=F32)

        _col_tiles(D, 512, tile)

        @pl.when(i == 0)
        def _():
            head.wait()

        @pl.when(i > 0)
        def _():
            rest.wait()

        for r in (0, HALF):
            h = h_scr[r:r + HALF, :]
            live = _rows((HALF, D), i * RC + r) >= first
            err = jnp.where(live, _rms(h, wf_ref[...]) - t_ref[r:r + HALF, :], 0.0)
            l_ref[...] += 0.5 * jnp.sum(jnp.sum(err * err, axis=1, keepdims=True) * (1.0 / D), axis=0, keepdims=True)
            dh, dw = _rms_bwd(err * (1.0 / D), h, wf_ref[...])
            dw_ref[...] += jnp.sum(dw, axis=0, keepdims=True)
            d_ref[r:r + HALF, :] = dh
            db_ref[r:r + HALF, :] = dh.astype(BF)

    return pl.pallas_call(
        body, grid=(T // RC,),
        in_specs=[_rows_spec(D_FF), _whole((D_FF, D)), _rows_spec(D), pl.BlockSpec(memory_space=pl.ANY), _vec(D)],
        out_specs=[_rows_spec(D), _rows_spec(D), _spec((1, 128), lambda i: (0, 0)), _vec(D)],
        out_shape=[jax.ShapeDtypeStruct((T, D), F32), jax.ShapeDtypeStruct((T, D), BF), jax.ShapeDtypeStruct((1, 128), F32),
                   jax.ShapeDtypeStruct((1, D), F32)],
        scratch_shapes=[pltpu.VMEM((RC, D), F32), pltpu.VMEM((RC, D), F32), pltpu.SemaphoreType.DMA],
        compiler_params=_params(), name="down_loss")(act, w_down, h1, target, wf)


def swiglu_bwd(dh2_b, w_down, gt, up):
    def body(d_ref, w_ref, gt_ref, up_ref, dg_ref, du_ref):
        def tile(c0):
            cols = pl.ds(c0, 256)
            dact = lax.dot_general(d_ref[...], w_ref[cols, :], NT_DIMS, preferred_element_type=F32)
            gt_ = gt_ref[:, cols].astype(F32)
            up_ = up_ref[:, cols].astype(F32)
            sg = _sigmoid(gt_)
            dg_ref[:, cols] = (dact * up_ * (sg * (1.0 + gt_ * (1.0 - sg)))).astype(BF)
            du_ref[:, cols] = (dact * (gt_ * sg)).astype(BF)

        _col_tiles(D_FF, 256, tile)

    big = jax.ShapeDtypeStruct((T, D_FF), BF)
    return pl.pallas_call(
        body, grid=(T // RC,), in_specs=[_rows_spec(D), _whole((D_FF, D)), _rows_spec(D_FF), _rows_spec(D_FF)],
        out_specs=[_rows_spec(D_FF), _rows_spec(D_FF)], out_shape=[big, big], compiler_params=_params(), name="swiglu_bwd")(dh2_b, w_down, gt, up)


def gate_up_bwd(dgt, dup, w_gate, w_up, h1, wn, dh2):
    def body(dg_ref, du_ref, wg_ref, wu_ref, h_ref, wn_ref, r_ref, d_ref, db_ref, dw_ref, du_scr):
        _zero_at_first(dw_ref)

        def tile(c0):
            rows = pl.ds(c0, 512)
            du_scr[:, rows] = (lax.dot_general(dg_ref[...], wg_ref[rows, :], NT_DIMS, preferred_element_type=F32)
                               + lax.dot_general(du_ref[...], wu_ref[rows, :], NT_DIMS, preferred_element_type=F32))

        _col_tiles(D, 512, tile)
        for r in (0, HALF):
            dh, dw = _rms_bwd(du_scr[r:r + HALF, :], h_ref[r:r + HALF, :], wn_ref[...])
            dw_ref[...] += jnp.sum(dw, axis=0, keepdims=True)
            dh = dh + r_ref[r:r + HALF, :]
            d_ref[r:r + HALF, :] = dh
            db_ref[r:r + HALF, :] = dh.astype(BF)

    return pl.pallas_call(
        body, grid=(T // RC,),
        in_specs=[_rows_spec(D_FF), _rows_spec(D_FF), _whole((D, D_FF)), _whole((D, D_FF)), _rows_spec(D), _vec(D), _rows_spec(D)],
        out_specs=[_rows_spec(D), _rows_spec(D), _vec(D)],
        out_shape=[jax.ShapeDtypeStruct((T, D), F32), jax.ShapeDtypeStruct((T, D), BF), jax.ShapeDtypeStruct((1, D), F32)],
        scratch_shapes=[pltpu.VMEM((RC, D), F32)],
        compiler_params=_params(), name="gate_up_bwd")(dgt, dup, w_gate, w_up, h1, wn, dh2)


def _adamw(w, g, m, v):
    m = ADAM_B1 * m + (1.0 - ADAM_B1) * g
    v = ADAM_B2 * v + (1.0 - ADAM_B2) * (g * g)
    m_hat = m / (1.0 - ADAM_B1 ** ADAM_STEP)
    v_hat = v / (1.0 - ADAM_B2 ** ADAM_STEP)
    delta = -ADAM_LR * (m_hat / (jnp.sqrt(v_hat) + ADAM_EPS) + ADAM_WD * w)
    return delta, m, v


def adamw_shard(name, recv, w, m, v, tr):
    r, c = w.shape

    def body(p_ref, w_ref, m_ref, v_ref, g_ref, d_ref, mo_ref, vo_ref):
        g = p_ref[0].astype(F32)
        for s in range(1, 8):
            g = g + p_ref[s].astype(F32)
        g_ref[...] = g
        d_ref[...], mo_ref[...], vo_ref[...] = _adamw(w_ref[...], g, m_ref[...], v_ref[...])

    tile = _spec((tr, c), lambda i: (i, 0))
    shape = jax.ShapeDtypeStruct((r, c), F32)
    return pl.pallas_call(
        body, grid=(r // tr,), in_specs=[_spec((8, tr, c), lambda i: (0, i, 0)), tile, tile, tile],
        out_specs=[tile] * 4, out_shape=[shape] * 4, compiler_params=_params(), name=name)(recv, w, m, v)


def sum_slabs(recv):
    def body(p_ref, o_ref):
        g = p_ref[0]
        for s in range(1, 8):
            g = g + p_ref[s]
        o_ref[...] = g

    return pl.pallas_call(body, out_shape=jax.ShapeDtypeStruct(recv.shape[1:], F32), compiler_params=_params(0), name="sum_slabs")(recv)


SIMPLE = [("norm1_w", 1024), ("ssd_conv_b", 1536), ("ssd_dt_bias", 16), ("ssd_a_log", 16), ("ssd_d", 16), ("ssd_norm_w", 1024),
          ("lru_conv_b", 1024), ("lru_ba", 1024), ("lru_bx", 1024), ("lru_lambda", 1024), ("lru_norm_w", 1024), ("norm2_w", 1024),
          ("final_norm_w", 1024)]
SPECIAL = ["lru_wa", "lru_wx", "meta_tokens", "ssd_conv_w", "lru_conv_w"]
SM_ROWS = 176
SM_WA, SM_WX, SM_META, SM_SCW, SM_LCW, SM_LOSS = 14, 78, 142, 158, 166, 170


def _simple_rows():
    rows, r = {}, 0
    for name, n in SIMPLE:
        rows[name] = r
        r += -(-n // 1024)
    return rows


def adamw_small(sm, special_g, ws, ms, vs):
    rows = _simple_rows()
    ns, nx = len(SIMPLE), len(SPECIAL)

    def body(*refs):
        sm_ref = refs[0]
        gx = refs[1:1 + nx]
        wr = refs[1 + nx:1 + nx + ns + nx]
        mr = refs[1 + nx + ns + nx:1 + nx + 2 * (ns + nx)]
        vr = refs[1 + nx + 2 * (ns + nx):1 + nx + 3 * (ns + nx)]
        outs = refs[1 + nx + 3 * (ns + nx):]
        o = 0
        for k, (name, n) in enumerate(SIMPLE):
            r0 = rows[name]
            for c0 in range(0, n, 1024):
                wd = min(1024, n - c0)
                g = sm_ref[r0 + c0 // 1024:r0 + c0 // 1024 + 1, 0:wd]
                sl = (slice(None), slice(c0, c0 + wd))
                d, m2, v2 = _adamw(wr[k][sl], g, mr[k][sl], vr[k][sl])
                outs[o][sl] = g
                outs[o + 1][sl] = d
                outs[o + 2][sl] = m2
                outs[o + 3][sl] = v2
            o += 4
        for k in range(nx):
            d, m2, v2 = _adamw(wr[ns + k][...], gx[k][...], mr[ns + k][...], vr[ns + k][...])
            outs[o][...] = d
            outs[o + 1][...] = m2
            outs[o + 2][...] = v2
            o += 3

    out_shape = []
    for k in range(ns):
        out_shape += [jax.ShapeDtypeStruct(ws[k].shape, F32)] * 4
    for k in range(nx):
        out_shape += [jax.ShapeDtypeStruct(ws[ns + k].shape, F32)] * 3
    return pl.pallas_call(body, out_shape=out_shape, compiler_params=_params(0), name="adamw_small")(sm, *special_g, *ws, *ms, *vs)


def _place():
    return lax.axis_index("x"), lax.axis_index("y"), lax.axis_index("c")


def _index(px, py, pc):
    return 4 * px + 2 * py + pc


def all_gather(name, shards):
    n = len(shards)
    hbm = pl.BlockSpec(memory_space=pl.ANY)

    def body(*refs):
        ins, outs = refs[:n], refs[n:2 * n]
        send_sems, recv_sems, local_sems = refs[2 * n:]
        x, y, c = _place()
        me, sibling = (x, y, c), (x, y, 1 - c)
        chips = [(1 - x, y), (x, 1 - y), (1 - x, 1 - y)]

        def copy(i, k, block, to, src=None):
            dst = outs[i].at[_index(*block)]
            return pltpu.make_async_remote_copy(src_ref=dst if src is None else src, dst_ref=dst, send_sem=send_sems.at[7 * i + k],
                                                recv_sem=recv_sems.at[7 * i + k], device_id=to, device_id_type=MESH)

        mine = [pltpu.make_async_copy(ins[i], outs[i].at[_index(*me)], local_sems.at[i]) for i in range(n)]
        for cp in mine:
            cp.start()
        first = []
        for i in range(n):
            first += [copy(i, 1 + j, me, (*chip, c), src=ins[i]) for j, chip in enumerate(chips)]
            first.append(copy(i, 0, me, sibling, src=ins[i]))
        for cp in first:
            cp.start()
        passed = []
        for i in range(n):
            for j, chip in enumerate(chips):
                copy(i, 1 + j, (*chip, c), me).wait_recv()
                cp = copy(i, 4 + j, (*chip, c), sibling)
                cp.start()
                passed.append(cp)
        for i in range(n):
            copy(i, 0, sibling, me).wait_recv()
            for j, chip in enumerate(chips):
                copy(i, 4 + j, (*chip, 1 - c), me).wait_recv()
        for cp in first + passed:
            cp.wait_send()
        for cp in mine:
            cp.wait()

    return pl.pallas_call(
        body, in_specs=[hbm] * n, out_specs=[hbm] * n,
        out_shape=[jax.ShapeDtypeStruct((8,) + s.shape, s.dtype) for s in shards],
        scratch_shapes=[pltpu.SemaphoreType.DMA((7 * n,)), pltpu.SemaphoreType.DMA((7 * n,)), pltpu.SemaphoreType.DMA((n,))],
        name=name)(*shards)


HBM_SPEC = pl.BlockSpec(memory_space=pltpu.HBM)
SEM_SPEC = pl.BlockSpec(memory_space=pltpu.SEMAPHORE)
EFFECT = pltpu.SideEffectType.DATAFLOW_SIDE_EFFECTING


def _peers(x, y, c):
    return [((1 - x) if k & 4 else x, (1 - y) if k & 2 else y, (1 - c) if k & 1 else c) for k in range(1, 8)]


def _pieces(rows):
    for n in (4, 2):
        if rows % (16 * n) == 0:
            return [(r * (rows // n), rows // n) for r in range(n)]
    return [(0, rows)]


def _peer_copies(src, land, send_sems, recv_sems, k, peer, mine, slab_src):
    block = src.at[_index(*peer)] if slab_src else src
    return [pltpu.make_async_remote_copy(src_ref=block.at[pl.ds(r0, nr)], dst_ref=land.at[mine, pl.ds(r0, nr)], send_sem=send_sems.at[k],
                                         recv_sem=recv_sems.at[k], device_id=peer, device_id_type=MESH)
            for r0, nr in _pieces(block.shape[0])]


def copies_start(name, srcs, slab_src, after):
    n = len(srcs)
    zones = [jax.ShapeDtypeStruct(s.shape if slab_src else (8,) + s.shape, s.dtype) for s in srcs]

    def body(*refs):
        ins, lands = refs[:n], refs[n:2 * n]
        sends, recvs = refs[2 * n + 1:3 * n + 1], refs[3 * n + 1:4 * n + 1]
        token = refs[-1]
        x, y, c = _place()
        mine = _index(x, y, c)
        for i in range(n):
            per_peer = [_peer_copies(ins[i], lands[i], sends[i], recvs[i], k, peer, mine, slab_src) for k, peer in enumerate(_peers(x, y, c))]
            for piece in zip(*per_peer):
                for cp in piece:
                    cp.start()
        token[...] = jnp.zeros_like(token)

    sem = pltpu.SemaphoreType.DMA((7,))
    res = pl.pallas_call(
        body, name=name,
        out_shape=([sem] * (2 * n) + [pltpu.HBM(s.shape, s.dtype) for s in srcs] + [pltpu.HBM(z.shape, z.dtype) for z in zones]
                   + [jax.ShapeDtypeStruct((8, 128), F32)]),
        in_specs=[HBM_SPEC] * (2 * n) + [pl.BlockSpec(memory_space=pl.ANY)],
        out_specs=[SEM_SPEC] * (2 * n) + [HBM_SPEC] * (2 * n) + [pl.BlockSpec(memory_space=pltpu.VMEM)],
        input_output_aliases={i: 2 * n + i for i in range(2 * n)},
        compiler_params=pltpu.CompilerParams(has_side_effects=EFFECT),
    )(*[pltpu.with_memory_space_constraint(s, pltpu.HBM) for s in srcs],
      *[pltpu.with_memory_space_constraint(lax.empty(z.shape, z.dtype), pltpu.HBM) for z in zones], after)
    return [(res[i], res[n + i], res[2 * n + i], res[3 * n + i]) for i in range(n)], res[-1][0:1, 0:1]


def copies_wait(name, started, slab_src, after):
    n = len(started)

    def body(*refs):
        ins, lands = refs[:n], refs[n:2 * n]
        sends, recvs = refs[2 * n:3 * n], refs[3 * n:4 * n]
        x, y, c = _place()
        mine = _index(x, y, c)
        for i in range(n):
            for k, peer in enumerate(_peers(x, y, c)):
                arrival = pltpu.make_async_remote_copy(src_ref=ins[i].at[mine] if slab_src else ins[i], dst_ref=lands[i].at[_index(*peer)],
                                                       send_sem=sends[i].at[k], recv_sem=recvs[i].at[k], device_id=peer, device_id_type=MESH)
                arrival.wait_send()
                arrival.wait_recv()

    srcs = [s[2] for s in started]
    lands = [s[3] for s in started]
    afters = list(after) if isinstance(after, (list, tuple)) else [after]
    res = pl.pallas_call(
        body, name=name,
        out_shape=[pltpu.HBM(s.shape, s.dtype) for s in srcs] + [pltpu.HBM(z.shape, z.dtype) for z in lands],
        in_specs=[HBM_SPEC] * (2 * n) + [SEM_SPEC] * (2 * n) + [pl.BlockSpec(memory_space=pl.ANY)] * len(afters),
        out_specs=[HBM_SPEC] * (2 * n),
        input_output_aliases={i: i for i in range(2 * n)},
        compiler_params=pltpu.CompilerParams(has_side_effects=EFFECT),
    )(*srcs, *lands, *[s[0] for s in started], *[s[1] for s in started], *afters)
    me = _index(*_place())
    own = [lax.dynamic_index_in_dim(s, me, 0, keepdims=True) if slab_src else s[None] for s in res[:n]]
    return [lax.dynamic_update_slice_in_dim(z, o, me, 0) for z, o in zip(res[n:], own)]


WEIGHTS = ["meta_tokens", "norm1_w", "w_in", "ssd_conv_w", "ssd_conv_b", "ssd_dt_bias", "ssd_a_log", "ssd_d", "ssd_norm_w", "lru_conv_w",
           "lru_conv_b", "lru_wa", "lru_ba", "lru_wx", "lru_bx", "lru_lambda", "lru_norm_w", "w_out", "norm2_w", "w_gate", "w_up", "w_down",
           "final_norm_w"]
BIG = ["w_in", "w_out", "w_gate", "w_up", "w_down"]
BIG_ROW_TILE = {"w_in": 256, "w_out": 128, "w_gate": 256, "w_up": 256, "w_down": 176}


def _pair_blocks(w):
    w = w.reshape(8, 2, 64, 64)
    z = jnp.zeros((8, 64, 64), w.dtype)
    return jnp.concatenate([jnp.concatenate([w[:, 0], z], axis=2), jnp.concatenate([z, w[:, 1]], axis=2)], axis=1)


def _unpair_blocks(w2):
    return jnp.stack([w2[:, :64, :64], w2[:, 64:, 64:]], axis=1).reshape(16, 64, 64)


def _per_group(v):
    return jnp.pad(v.reshape(2, 1, 8), ((0, 0), (0, 0), (0, 120)))


def _pad_cols(v, n):
    return jnp.pad(v, ((0, 0), (0, n - v.shape[1])))


def local_step(x, target, meta, ssd_cw, lru_cw, w_in, fetch, send, p):
    z120 = jnp.zeros((D, 120), BF)
    w_p = jnp.concatenate([w_in[:, 0:1024], w_in[:, 2576:3600], w_in[:, 3600:4624], w_in[:, 1024:2560],
                           w_in[:, 2560:2568], z120, w_in[:, 2568:2576], z120, jnp.zeros((D, 256), BF)], axis=1)
    bias2, alog2, d2 = _per_group(p["ssd_dt_bias"]), _per_group(p["ssd_a_log"]), _per_group(p["ssd_d"])
    wa2 = _pair_blocks(p["lru_wa"]).astype(BF)
    wx2 = _pair_blocks(p["lru_wx"]).astype(BF)
    lru = (lru_cw, p["lru_conv_b"], wa2, p["lru_ba"], wx2, p["lru_bx"], p["lru_lambda"])

    h0 = jnp.concatenate([jnp.zeros((NPAD, D), F32), meta, x], axis=0)
    proj, u1 = in_proj(h0, p["norm1_w"], w_p)
    xbc_act = conv_silu_fwd(proj, ssd_cw, p["ssd_conv_b"])
    yn_ssd, y_pre, h_prev = ssd_fwd(xbc_act, proj, bias2, alog2, d2, p["ssd_norm_w"])
    a, u = lru_gates_fwd(proj, *lru)
    hseq = lru_scan_fwd(a, u)
    (w_out,) = fetch(["w_out"], hseq)
    h1, cat = out_proj(yn_ssd, proj, hseq, p["lru_norm_w"], w_out, h0)
    w_gate, w_up = fetch(["w_gate", "w_up"], h1)
    gt, up, act, u2 = gate_up(h1, p["norm2_w"], w_gate, w_up)
    (w_down,) = fetch(["w_down"], act)
    dh2, dh2_b, loss, d_fnw = down_loss(act, w_down, h1, target, p["final_norm_w"])

    dgt, dup = swiglu_bwd(dh2_b, w_down, gt, up)
    g_down = matmul_tn("dw_down", act, dh2_b, 1408, 512)
    g_gate = matmul_tn("dw_gate", u2, dgt, 512, 1408)
    g_up = matmul_tn("dw_up", u2, dup, 512, 1408)
    sent = send({"w_down": g_down, "w_gate": g_gate, "w_up": g_up})
    dh1, dh1_b, d_n2 = gate_up_bwd(dgt, dup, w_gate, w_up, h1, p["norm2_w"] + sent, dh2)
    sent = send({"w_out": matmul_tn("dw_out", cat, dh1_b, 512, 1024)})
    dyn, dh_out, dg_b, d_lnw = out_proj_bwd(dh1_b, w_out, proj, hseq, p["lru_norm_w"] + sent)

    dhs = lru_scan_bwd(a, dh_out)
    dxl_b, d_lcw, d_lcb, dwa2, d_ba, dwx2, d_bx, d_lam = lru_gates_bwd(dhs, hseq, proj, *lru)
    dz_b, dx, d_b, d_c, ddt_b, dpar, d_snw = ssd_bwd(dyn, xbc_act, proj, y_pre, h_prev, bias2, alog2, d2, p["ssd_norm_w"])
    dxbc_b, d_scw, d_scb = conv_silu_bwd(dx, d_b, d_c, proj, ssd_cw, p["ssd_conv_b"])
    dproj = jnp.concatenate([dz_b, dg_b, dxl_b, dxbc_b, ddt_b, jnp.zeros((T, 256), BF)], axis=1)
    g_p = matmul_tn("dw_in", u1, dproj, 512, 1024)
    g_in = jnp.concatenate([g_p[:, 0:1024], g_p[:, PXBC:PXBC + XBC], g_p[:, PDT:PDT + 8], g_p[:, PDT + 128:PDT + 136],
                            g_p[:, PG:PG + 1024], g_p[:, PXL:PXL + 1024]], axis=1)
    sent = send({"w_in": g_in})
    dh0, d_n1 = in_proj_bwd(dproj, w_p, h0, p["norm1_w"] + sent, dh1)
    small = {"norm1_w": d_n1, "ssd_conv_b": d_scb, "ssd_dt_bias": dpar[:, 0, :8].reshape(1, 16), "ssd_a_log": dpar[:, 1, :8].reshape(1, 16),
             "ssd_d": dpar[:, 2, :8].reshape(1, 16), "ssd_norm_w": d_snw, "lru_conv_b": d_lcb, "lru_ba": d_ba, "lru_bx": d_bx,
             "lru_lambda": d_lam, "lru_norm_w": d_lnw, "norm2_w": d_n2, "final_norm_w": d_fnw,
             "lru_wa": _unpair_blocks(dwa2), "lru_wx": _unpair_blocks(dwx2), "meta_tokens": dh0[NPAD:NPAD + N_META],
             "ssd_conv_w": d_scw, "lru_conv_w": d_lcw}
    return loss, dh0[NPAD + N_META:], small


def _pack_small(small, loss):
    rows = [_pad_cols(small[name], -(-n // 1024) * 1024).reshape(-1, 1024) for name, n in SIMPLE]
    rows += [small["lru_wa"].reshape(64, 1024), small["lru_wx"].reshape(64, 1024), small["meta_tokens"],
             _pad_cols(small["ssd_conv_w"], 2048).reshape(8, 1024), small["lru_conv_w"], _pad_cols(loss[:, 0:1], 1024)]
    sm = jnp.concatenate(rows, axis=0)
    return jnp.pad(sm, ((0, SM_ROWS - sm.shape[0]), (0, 0)))


def _slabs(g, name):
    if name in ("w_in", "w_gate", "w_up"):
        return g.reshape(g.shape[0], 8, g.shape[1] // 8).transpose(1, 0, 2)
    return g.reshape(8, g.shape[0] // 8, g.shape[1])


def _unslab(g, name):
    if name in ("w_in", "w_gate", "w_up"):
        return g.transpose(1, 0, 2).reshape(g.shape[1], 8 * g.shape[2])
    return g.reshape(8 * g.shape[1], g.shape[2])


def kernel(x, meta_tokens, norm1_w, w_in, ssd_conv_w, ssd_conv_b, ssd_dt_bias, ssd_a_log, ssd_d, ssd_norm_w, lru_conv_w, lru_conv_b, lru_wa, lru_ba, lru_wx, lru_bx, lru_lambda, lru_norm_w, w_out, norm2_w, w_gate, w_up, w_down, final_norm_w, loss_target, m_meta_tokens, m_norm1_w, m_w_in, m_ssd_conv_w, m_ssd_conv_b, m_ssd_dt_bias, m_ssd_a_log, m_ssd_d, m_ssd_norm_w, m_lru_conv_w, m_lru_conv_b, m_lru_wa, m_lru_ba, m_lru_wx, m_lru_bx, m_lru_lambda, m_lru_norm_w, m_w_out, m_norm2_w, m_w_gate, m_w_up, m_w_down, m_final_norm_w, v_meta_tokens, v_norm1_w, v_w_in, v_ssd_conv_w, v_ssd_conv_b, v_ssd_dt_bias, v_ssd_a_log, v_ssd_d, v_ssd_norm_w, v_lru_conv_w, v_lru_conv_b, v_lru_wa, v_lru_ba, v_lru_wx, v_lru_bx, v_lru_lambda, v_lru_norm_w, v_w_out, v_norm2_w, v_w_gate, v_w_up, v_w_down, v_final_norm_w):
    w = dict(meta_tokens=meta_tokens, norm1_w=norm1_w, w_in=w_in[0], ssd_conv_w=ssd_conv_w[0], ssd_conv_b=ssd_conv_b, ssd_dt_bias=ssd_dt_bias,
             ssd_a_log=ssd_a_log, ssd_d=ssd_d, ssd_norm_w=ssd_norm_w, lru_conv_w=lru_conv_w[0], lru_conv_b=lru_conv_b, lru_wa=lru_wa[0],
             lru_ba=lru_ba, lru_wx=lru_wx[0], lru_bx=lru_bx, lru_lambda=lru_lambda, lru_norm_w=lru_norm_w, w_out=w_out[0], norm2_w=norm2_w,
             w_gate=w_gate[0], w_up=w_up[0], w_down=w_down[0], final_norm_w=final_norm_w.reshape(1, D))
    m = dict(meta_tokens=m_meta_tokens, norm1_w=m_norm1_w, w_in=m_w_in[0], ssd_conv_w=m_ssd_conv_w[0], ssd_conv_b=m_ssd_conv_b,
             ssd_dt_bias=m_ssd_dt_bias, ssd_a_log=m_ssd_a_log, ssd_d=m_ssd_d, ssd_norm_w=m_ssd_norm_w, lru_conv_w=m_lru_conv_w[0],
             lru_conv_b=m_lru_conv_b, lru_wa=m_lru_wa[0], lru_ba=m_lru_ba, lru_wx=m_lru_wx[0], lru_bx=m_lru_bx, lru_lambda=m_lru_lambda,
             lru_norm_w=m_lru_norm_w, w_out=m_w_out[0], norm2_w=m_norm2_w, w_gate=m_w_gate[0], w_up=m_w_up[0], w_down=m_w_down[0],
             final_norm_w=m_final_norm_w.reshape(1, D))
    v = dict(meta_tokens=v_meta_tokens, norm1_w=v_norm1_w, w_in=v_w_in[0], ssd_conv_w=v_ssd_conv_w[0], ssd_conv_b=v_ssd_conv_b,
             ssd_dt_bias=v_ssd_dt_bias, ssd_a_log=v_ssd_a_log, ssd_d=v_ssd_d, ssd_norm_w=v_ssd_norm_w, lru_conv_w=v_lru_conv_w[0],
             lru_conv_b=v_lru_conv_b, lru_wa=v_lru_wa[0], lru_ba=v_lru_ba, lru_wx=v_lru_wx[0], lru_bx=v_lru_bx, lru_lambda=v_lru_lambda,
             lru_norm_w=v_lru_norm_w, w_out=v_w_out[0], norm2_w=v_norm2_w, w_gate=v_w_gate[0], w_up=v_w_up[0], w_down=v_w_down[0],
             final_norm_w=v_final_norm_w.reshape(1, D))
    shapes = dict(meta_tokens=meta_tokens.shape, norm1_w=norm1_w.shape, w_in=w_in.shape, ssd_conv_w=ssd_conv_w.shape,
                  ssd_conv_b=ssd_conv_b.shape, ssd_dt_bias=ssd_dt_bias.shape, ssd_a_log=ssd_a_log.shape, ssd_d=ssd_d.shape,
                  ssd_norm_w=ssd_norm_w.shape, lru_conv_w=lru_conv_w.shape, lru_conv_b=lru_conv_b.shape, lru_wa=lru_wa.shape,
                  lru_ba=lru_ba.shape, lru_wx=lru_wx.shape, lru_bx=lru_bx.shape, lru_lambda=lru_lambda.shape, lru_norm_w=lru_norm_w.shape,
                  w_out=w_out.shape, norm2_w=norm2_w.shape, w_gate=w_gate.shape, w_up=w_up.shape, w_down=w_down.shape,
                  final_norm_w=final_norm_w.shape)
    me = _index(*_place())

    small_shard = jnp.concatenate([w["meta_tokens"], _pad_cols(w["ssd_conv_w"], 256).reshape(8, 128), w["lru_conv_w"],
                                   jnp.zeros((4, 128), F32)], axis=0)
    g_in, gs = all_gather("gather_w_in", [w["w_in"].astype(BF), small_shard])
    later = ["w_out", "w_gate", "w_up", "w_down"]
    started, behind = copies_start("gather_rest_start", [w[n].astype(BF) for n in later], False, gs)
    started = dict(zip(later, started))
    meta_full = gs[:, 0:16].transpose(1, 0, 2).reshape(N_META, D)
    ssd_cw = gs[:, 16:24].reshape(8, 4, 256)[:, :, :192].transpose(1, 0, 2).reshape(4, XBC)
    lru_cw = gs[:, 24:28].transpose(1, 0, 2).reshape(4, LRU_W)

    def fetch(names, after):
        got = copies_wait("gather_" + names[0] + "_wait", [started[n] for n in names], False, after)
        return [_unslab(g, n) for n, g in zip(names, got)]

    in_flight = {}

    def send(grads):
        names = list(grads)
        st, token = copies_start("grads_" + names[0] + "_start", [grads[n] if n == "small" else _slabs(grads[n], n) for n in names], True,
                                 grads[names[0]])
        in_flight.update(zip(names, st))
        return token

    loss, grad_x, small = local_step(x[0], loss_target[0], meta_full, ssd_cw, lru_cw, _unslab(g_in, "w_in"), fetch, send,
                                     {**w, "norm1_w": w["norm1_w"] + behind})
    send({"small": _pack_small(small, loss).reshape(8, SM_ROWS // 8, 1024)})

    out = {}
    early = ["w_down", "w_gate", "w_up", "w_out"]
    recv = dict(zip(early, copies_wait("grads_early_wait", [in_flight[n] for n in early], True, in_flight["small"][2])))
    for n in early:
        out[n] = adamw_shard("adamw_" + n, recv[n], w[n], m[n], v[n], BIG_ROW_TILE[n])
    recv_in, recv_small = copies_wait("grads_late_wait", [in_flight["w_in"], in_flight["small"]], True, [out[n][0] for n in early])
    out["w_in"] = adamw_shard("adamw_w_in", recv_in, w["w_in"], m["w_in"], v["w_in"], BIG_ROW_TILE["w_in"])
    sm = all_gather("gather_small_grads", [sum_slabs(recv_small)])[0].reshape(SM_ROWS, 1024)
    special_g = [sm[SM_WA:SM_WA + 64].reshape(16, 64, 64), sm[SM_WX:SM_WX + 64].reshape(16, 64, 64),
                 lax.dynamic_slice(sm[SM_META:SM_META + 16], (0, 128 * me), (16, 128)),
                 lax.dynamic_slice(sm[SM_SCW:SM_SCW + 8].reshape(4, 2048), (0, 192 * me), (4, 192)),
                 lax.dynamic_slice(sm[SM_LCW:SM_LCW + 4], (0, 128 * me), (4, 128))]
    names = [n for n, _ in SIMPLE] + SPECIAL
    res = adamw_small(sm, special_g, [w[n] for n in names], [m[n] for n in names], [v[n] for n in names])
    for k, (n, _) in enumerate(SIMPLE):
        out[n] = res[4 * k:4 * k + 4]
    for k, n in enumerate(SPECIAL):
        o = 4 * len(SIMPLE) + 3 * k
        out[n] = [special_g[k]] + list(res[o:o + 3])
    loss_total = sm[SM_LOSS, 0]
    flat = [loss_total, grad_x[None]]
    for k in range(4):
        flat += [out[n][k].reshape(shapes[n]) for n in WEIGHTS]
    return tuple(flat)
```

```python
import math

import jax
import jax.numpy as jnp
from jax import lax
from jax.experimental import pallas as pl
from jax.experimental.pallas import tpu as pltpu

F32 = jnp.float32
BF = jnp.bfloat16

D = 1024
SEQ = 2048
N_META = 16
Q = 128
NPAD = 112
T = NPAD + N_META + SEQ
NCH = T // Q
RC = 544
D_FF = 2816
SSD_W = 1024
LRU_W = 1024
XBC = 1536
IN_COLS = 4624
PZ, PG, PXL, PXBC, PDT = 0, 1024, 2048, 3072, 4608
NP_IN = 4864
EPS = 1e-6
LRU_C = 8.0
VMEM_LIMIT = 56 * 1024 * 1024

ADAM_LR, ADAM_B1, ADAM_B2, ADAM_EPS, ADAM_WD, ADAM_STEP = 0.001, 0.9, 0.999, 1e-08, 0.01, 10

NT_DIMS = (((1,), (1,)), ((), ()))
TN_DIMS = (((0,), (0,)), ((), ()))
MESH = pl.DeviceIdType.MESH


def _params(n_grid=1, limit=VMEM_LIMIT):
    return pltpu.CompilerParams(dimension_semantics=("arbitrary",) * n_grid, vmem_limit_bytes=limit)


def _spec(shape, imap, single=False):
    if single:
        return pl.BlockSpec(shape, imap, pipeline_mode=pl.Buffered(1))
    return pl.BlockSpec(shape, imap)


def _sigmoid(x):
    return 1.0 / (1.0 + jnp.exp(-x))


def _softplus(x):
    return jnp.maximum(x, 0.0) + jnp.log(1.0 + jnp.exp(-jnp.abs(x)))


def _rms_stats(h):
    return lax.rsqrt(jnp.mean(h * h, axis=-1, keepdims=True) + EPS)


def _rms(h, w):
    return (h * _rms_stats(h)) * w


def _rms_bwd(du, h, w):
    r = _rms_stats(h)
    n = h * r
    dn = du * w
    dh = r * (dn - n * jnp.mean(dn * n, axis=-1, keepdims=True))
    return dh, du * n


_G0 = math.sqrt(2.0 / math.pi)


def _gelu(x):
    return 0.5 * x * (1.0 + jnp.tanh(_G0 * (x + 0.044715 * (x * x * x))))


def _gelu_grad(x):
    t = jnp.tanh(_G0 * (x + 0.044715 * (x * x * x)))
    return 0.5 * (1.0 + t) + 0.5 * x * (1.0 - t * t) * (_G0 * (1.0 + 3.0 * 0.044715 * (x * x)))


def _rows(shape, r0=0):
    return lax.broadcasted_iota(jnp.int32, shape, 0) + r0


def _lanes(shape):
    return lax.broadcasted_iota(jnp.int32, shape, 1)


HALO = 8


def _fill_padded(pad_ref, x_ref):
    pad_ref[0:HALO, :] = jnp.zeros((HALO, pad_ref.shape[1]), F32)
    pad_ref[T + HALO:T + 2 * HALO, :] = jnp.zeros((HALO, pad_ref.shape[1]), F32)

    def step(c, carry):
        r0 = pl.multiple_of(c * Q, Q)
        pad_ref[pl.ds(r0 + HALO, Q), :] = x_ref[pl.ds(r0, Q), :]
        return carry

    lax.fori_loop(0, NCH, step, 0)


def _back(pad_ref, r0):
    win = pad_ref[pl.ds(r0, Q + HALO), :]
    return lambda s: win[HALO:, :] if s == 0 else pltpu.roll(win, s, axis=0)[HALO:, :]


def _ahead(pad_ref, r0):
    win = pad_ref[pl.ds(r0 + HALO, Q + HALO), :]
    return lambda s: win[:Q, :] if s == 0 else pltpu.roll(win, Q + HALO - s, axis=0)[:Q, :]


def _conv(back, w, b):
    y = b + w[3:4, :] * back(0)
    for k in range(3):
        y = y + w[k:k + 1, :] * back(3 - k)
    return y


def _conv_bwd_x(ahead, w):
    dx = w[3:4, :] * ahead(0)
    for k in range(3):
        dx = dx + w[k:k + 1, :] * ahead(3 - k)
    return dx


def _conv_bwd_w(dy, back):
    dws = [jnp.sum(dy * back(3 - k), axis=0, keepdims=True) for k in range(4)]
    return jnp.concatenate(dws, axis=0), jnp.sum(dy, axis=0, keepdims=True)


def _chunks(fn, unrolled=False):
    if unrolled:
        for c in range(NCH):
            fn(c * Q)
        return

    def step(c, carry):
        fn(pl.multiple_of(c * Q, Q))
        return carry

    lax.fori_loop(0, NCH, step, 0)


HALF = RC // 2


def _col_tiles(n, tn, fn):
    def step(j, carry):
        fn(pl.multiple_of(j * tn, tn))
        return carry

    lax.fori_loop(0, n // tn, step, 0)


def _rows_spec(cols, block_col=0):
    return _spec((RC, cols), lambda i: (i, block_col))


def _whole(shape):
    return _spec(shape, lambda i: tuple(0 for _ in shape), single=True)


def _vec(cols):
    return _spec((1, cols), lambda i: (0, 0))


def _zero_at_first(*refs):
    @pl.when(pl.program_id(0) == 0)
    def _():
        for r in refs:
            r[...] = jnp.zeros_like(r)


IN_RUNS = ((PZ, 0, 1024), (PG, 2576, 2048), (PXBC, 1024, XBC))


def _in_tiles(fn):
    for pcol, wrow, width in IN_RUNS:
        def step(j, carry, pcol=pcol, wrow=wrow):
            fn(pl.multiple_of(pcol + j * 512, 512), pl.multiple_of(wrow + j * 512, 16))
            return carry

        lax.fori_loop(0, width // 512, step, 0)


def in_proj(h0, wn, w_t, w_dt):
    def body(h_ref, wn_ref, w_ref, wdt_ref, o_ref, u_ref):
        for r in (0, HALF):
            u_ref[r:r + HALF, :] = _rms(h_ref[r:r + HALF, :], wn_ref[...]).astype(BF)

        def tile(pcol, wrow):
            o_ref[:, pl.ds(pcol, 512)] = lax.dot_general(u_ref[...], w_ref[pl.ds(wrow, 512), :], NT_DIMS, preferred_element_type=F32)

        _in_tiles(tile)
        o_ref[:, PDT:PDT + 256] = lax.dot_general(u_ref[...], wdt_ref[...], NT_DIMS, preferred_element_type=F32)

    return pl.pallas_call(
        body, grid=(T // RC,), in_specs=[_rows_spec(D), _vec(D), _whole((IN_COLS, D)), _whole((256, D))],
        out_specs=[_rows_spec(NP_IN), _rows_spec(D)],
        out_shape=[jax.ShapeDtypeStruct((T, NP_IN), F32), jax.ShapeDtypeStruct((T, D), BF)],
        compiler_params=_params(), name="in_proj")(h0, wn, w_t, w_dt)


def out_proj(yn_ssd, proj, hseq, lru_nw, w_out, h0):
    def body(y_ref, g_ref, h_ref, wn_ref, w_ref, r_ref, o_ref, cat_ref):
        cat_ref[:, 0:SSD_W] = y_ref[...]
        for r in (0, HALF):
            y = _gelu(g_ref[r:r + HALF, :]) * h_ref[r:r + HALF, :]
            cat_ref[r:r + HALF, SSD_W:] = _rms(y, wn_ref[...]).astype(BF)

        def tile(c0):
            o_ref[:, pl.ds(c0, 512)] = r_ref[:, pl.ds(c0, 512)] + jnp.dot(cat_ref[...], w_ref[:, pl.ds(c0, 512)], preferred_element_type=F32)

        _col_tiles(D, 512, tile)

    return pl.pallas_call(
        body, grid=(T // RC,),
        in_specs=[_rows_spec(SSD_W), _rows_spec(LRU_W, PG // LRU_W), _rows_spec(LRU_W), _vec(LRU_W), _whole((SSD_W + LRU_W, D)), _rows_spec(D)],
        out_specs=[_rows_spec(D), _rows_spec(SSD_W + LRU_W)],
        out_shape=[jax.ShapeDtypeStruct((T, D), F32), jax.ShapeDtypeStruct((T, SSD_W + LRU_W), BF)],
        compiler_params=_params(), name="out_proj")(yn_ssd, proj, hseq, lru_nw, w_out, h0)


def out_proj_bwd(dh1_b, w_out, proj, hseq, lru_nw):
    def body(d_ref, w_ref, g_ref, h_ref, wn_ref, dy_ref, dh_ref, dg_ref, dw_ref, dl_scr):
        _zero_at_first(dw_ref)

        def tile(c0):
            dy_ref[:, pl.ds(c0, 512)] = lax.dot_general(d_ref[...], w_ref[pl.ds(c0, 512), :], NT_DIMS, preferred_element_type=F32)
            dl_scr[:, pl.ds(c0, 512)] = lax.dot_general(d_ref[...], w_ref[pl.ds(SSD_W + c0, 512), :], NT_DIMS, preferred_element_type=F32)

        _col_tiles(SSD_W, 512, tile)
        for r in (0, HALF):
            g = g_ref[r:r + HALF, :]
            h = h_ref[r:r + HALF, :]
            ge = _gelu(g)
            dy, dw = _rms_bwd(dl_scr[r:r + HALF, :], ge * h, wn_ref[...])
            dw_ref[...] += jnp.sum(dw, axis=0, keepdims=True)
            dh_ref[r:r + HALF, :] = dy * ge
            dg_ref[r:r + HALF, :] = (dy * h * _gelu_grad(g)).astype(BF)

    return pl.pallas_call(
        body, grid=(T // RC,),
        in_specs=[_rows_spec(D), _whole((SSD_W + LRU_W, D)), _rows_spec(LRU_W, PG // LRU_W), _rows_spec(LRU_W), _vec(LRU_W)],
        out_specs=[_rows_spec(SSD_W), _rows_spec(LRU_W), _rows_spec(LRU_W), _vec(LRU_W)],
        out_shape=[jax.ShapeDtypeStruct((T, SSD_W), F32), jax.ShapeDtypeStruct((T, LRU_W), F32), jax.ShapeDtypeStruct((T, LRU_W), BF),
                   jax.ShapeDtypeStruct((1, LRU_W), F32)],
        scratch_shapes=[pltpu.VMEM((RC, LRU_W), F32)],
        compiler_params=_params(), name="out_proj_bwd")(dh1_b, w_out, proj, hseq, lru_nw)


def in_proj_bwd(dz, dg, dxl, dxbc, ddt, w_t, w_dt, h0, wn, dh1):
    def body(dz_ref, dg_ref, dxl_ref, dxbc_ref, ddt_ref, w_ref, wdt_ref, h_ref, wn_ref, r_ref, o_ref, dw_ref, du_scr):
        _zero_at_first(dw_ref)
        du_scr[...] = jnp.dot(ddt_ref[...], wdt_ref[...], preferred_element_type=F32)
        for d_ref, wrow, width in ((dz_ref, 0, 1024), (dxbc_ref, 1024, XBC), (dg_ref, 2576, 1024), (dxl_ref, 3600, 1024)):
            def step(j, carry, d_ref=d_ref, wrow=wrow):
                c0 = pl.multiple_of(j * 512, 512)
                du_scr[...] += jnp.dot(d_ref[:, pl.ds(c0, 512)], w_ref[pl.ds(pl.multiple_of(wrow + c0, 16), 512), :], preferred_element_type=F32)
                return carry

            lax.fori_loop(0, width // 512, step, 0)
        for r in (0, HALF):
            dh, dw = _rms_bwd(du_scr[r:r + HALF, :], h_ref[r:r + HALF, :], wn_ref[...])
            dw_ref[...] += jnp.sum(dw, axis=0, keepdims=True)
            o_ref[r:r + HALF, :] = dh + r_ref[r:r + HALF, :]

    return pl.pallas_call(
        body, grid=(T // RC,),
        in_specs=[_rows_spec(SSD_W), _rows_spec(LRU_W), _rows_spec(LRU_W), _rows_spec(XBC), _rows_spec(256), _whole((IN_COLS, D)),
                  _whole((256, D)), _rows_spec(D), _vec(D), _rows_spec(D)],
        out_specs=[_rows_spec(D), _vec(D)],
        out_shape=[jax.ShapeDtypeStruct((T, D), F32), jax.ShapeDtypeStruct((1, D), F32)],
        scratch_shapes=[pltpu.VMEM((RC, D), F32)],
        compiler_params=_params(), name="in_proj_bwd")(dz, dg, dxl, dxbc, ddt, w_t, w_dt, h0, wn, dh1)


def matmul_tn(name, a, b, tm, tn):
    m, n = a.shape[1], b.shape[1]

    def body(a_ref, b_ref, o_ref, acc_ref):
        acc_ref[...] = jnp.zeros_like(acc_ref)

        def mm(r0):
            acc_ref[...] += lax.dot_general(a_ref[pl.ds(r0, RC), :], b_ref[pl.ds(r0, RC), :], TN_DIMS, preferred_element_type=F32)

        _col_tiles(T, RC, mm)
        o_ref[...] = acc_ref[...].astype(BF)

    return pl.pallas_call(
        body, grid=(m // tm, n // tn),
        in_specs=[_spec((T, tm), lambda i, j: (0, i)), _spec((T, tn), lambda i, j: (0, j))],
        out_specs=_spec((tm, tn), lambda i, j: (i, j)),
        out_shape=jax.ShapeDtypeStruct((m, n), BF),
        scratch_shapes=[pltpu.VMEM((tm, tn), F32)],
        compiler_params=_params(2), name=name)(a, b)


def conv_silu_fwd(proj, cw, cb):
    def body(x_ref, w_ref, b_ref, o_ref, xpad):
        _fill_padded(xpad, x_ref)

        def chunk(r0):
            pre = _conv(_back(xpad, r0), w_ref[...], b_ref[...])
            o_ref[pl.ds(r0, Q), :] = pre * _sigmoid(pre)

        _chunks(chunk)

    c0 = PXBC // 128
    return pl.pallas_call(
        body, grid=(XBC // 128,),
        in_specs=[_spec((T, 128), lambda c: (0, c0 + c)), _spec((4, 128), lambda c: (0, c)), _spec((1, 128), lambda c: (0, c))],
        out_specs=_spec((T, 128), lambda c: (0, c)),
        out_shape=jax.ShapeDtypeStruct((T, XBC), F32), scratch_shapes=[pltpu.VMEM((T + 2 * HALO, 128), F32)],
        compiler_params=_params(), name="conv_silu_fwd")(proj, cw, cb)


def conv_silu_bwd(dx, d_b, d_c, proj, cw, cb):
    def body(dx_ref, db_ref, dc_ref, x_ref, w_ref, b_ref, o_ref, dw_ref, dbias_ref, xpad, dpad):
        tile = pl.program_id(0)
        _fill_padded(xpad, x_ref)
        dpad[0:HALO, :] = jnp.zeros((HALO, 128), F32)
        dpad[T + HALO:T + 2 * HALO, :] = jnp.zeros((HALO, 128), F32)
        dw_ref[...] = jnp.zeros_like(dw_ref)
        dbias_ref[...] = jnp.zeros_like(dbias_ref)

        def first(r0):
            back = _back(xpad, r0)
            pre = _conv(back, w_ref[...], b_ref[...])
            sg = _sigmoid(pre)
            rows = pl.ds(r0, Q)
            d = jnp.where(tile < 8, dx_ref[rows, :], jnp.where(tile < 10, db_ref[rows, :], dc_ref[rows, :]))
            dpre = d * (sg * (1.0 + pre * (1.0 - sg)))
            dpad[pl.ds(r0 + HALO, Q), :] = dpre
            dw, dbias = _conv_bwd_w(dpre, back)
            dw_ref[...] += dw
            dbias_ref[...] += dbias

        _chunks(first)

        def second(r0):
            o_ref[pl.ds(r0, Q), :] = _conv_bwd_x(_ahead(dpad, r0), w_ref[...]).astype(BF)

        _chunks(second)

    c0 = PXBC // 128
    pad = pltpu.VMEM((T + 2 * HALO, 128), F32)
    return pl.pallas_call(
        body, grid=(XBC // 128,),
        in_specs=[_spec((T, 128), lambda c: (0, jnp.minimum(c, 7))), _spec((T, 128), lambda c: (0, jnp.clip(c - 8, 0, 1))),
                  _spec((T, 128), lambda c: (0, jnp.clip(c - 10, 0, 1))), _spec((T, 128), lambda c: (0, c0 + c)),
                  _spec((4, 128), lambda c: (0, c)), _spec((1, 128), lambda c: (0, c))],
        out_specs=[_spec((T, 128), lambda c: (0, c)), _spec((4, 128), lambda c: (0, c)), _spec((1, 128), lambda c: (0, c))],
        out_shape=[jax.ShapeDtypeStruct((T, XBC), BF), jax.ShapeDtypeStruct((4, XBC), F32), jax.ShapeDtypeStruct((1, XBC), F32)],
        scratch_shapes=[pad, pad], compiler_params=_params(), name="conv_silu_bwd")(dx, d_b, d_c, proj, cw, cb)


def _ssd_chunk_common(row0, dt_ref, b_ref, c_ref, bias, a_neg):
    shape = (Q, Q)
    lane = _lanes(shape)
    sub = _rows(shape)
    live = (_rows(shape, row0) >= NPAD) & (lane < 8)
    dtr = dt_ref[:, :]
    dt = jnp.where(live, _softplus(dtr + bias), 0.0)
    d_a = dt * a_neg
    tri = (sub >= lane).astype(F32)
    cs = jnp.dot(tri, d_a, precision=lax.Precision.HIGHEST, preferred_element_type=F32)
    cs_t = cs.T
    bc = b_ref[:, :].astype(BF)
    cc = c_ref[:, :].astype(BF)
    cb = lax.dot_general(cc, bc, NT_DIMS, preferred_element_type=F32)
    cs_last = cs[Q - 1:Q, :]
    return dict(lane=lane, sub=sub, live=live, dtr=dtr, dt=dt, cs=cs, cs_t=cs_t, bc=bc, cc=cc, cb=cb,
                ecs=jnp.exp(cs), dsm=jnp.exp(cs_last - cs), gam=jnp.exp(cs_last), tri=tri)


def _pair(lane_even, mat, j):
    return jnp.where(lane_even, mat[:, j:j + 1], mat[:, j + 1:j + 2])


def _head_decay(cm, j):
    seg = cm["cs"][:, j:j + 1] - cm["cs_t"][j:j + 1, :]
    return jnp.exp(jnp.where(cm["sub"] >= cm["lane"], seg, -jnp.inf))


def ssd_fwd(xbc_act, proj, dt_bias2, a_log2, d2, norm_w):
    def body(x_ref, b_ref, c_ref, dt_ref, z_ref, bias_ref, alog_ref, d_ref, nw_ref, yn_ref, y_ref, hp_ref, h_scr):
        c = pl.program_id(1)

        @pl.when(c == 0)
        def _():
            h_scr[...] = jnp.zeros_like(h_scr)

        bias = bias_ref[0]
        a_neg = -jnp.exp(alog_ref[0])
        dsk = d_ref[0]
        cm = _ssd_chunk_common(c * Q, dt_ref, b_ref, c_ref, bias, a_neg)
        lane_even = cm["lane"] < 64
        sub_even = cm["sub"] < 64
        for p in range(4):
            je, jo = 2 * p, 2 * p + 1
            xp = x_ref[:, 128 * p:128 * p + 128]
            xdt = xp * _pair(lane_even, cm["dt"], je)
            xdt_b = xdt.astype(BF)
            m_e = (cm["cb"] * _head_decay(cm, je)).astype(BF)
            m_o = (cm["cb"] * _head_decay(cm, jo)).astype(BF)
            zero = jnp.zeros_like(xdt_b)
            yd = (jnp.dot(m_e, jnp.where(lane_even, xdt_b, zero), preferred_element_type=F32)
                  + jnp.dot(m_o, jnp.where(lane_even, zero, xdt_b), preferred_element_type=F32))
            hp = h_scr[p]
            hp_ref[0, 0, p] = hp
            yo = lax.dot_general(cm["cc"], hp.astype(BF), NT_DIMS, preferred_element_type=F32) * _pair(lane_even, cm["ecs"], je)
            dsk_p = jnp.where(lane_even[0:1, :], dsk[:, je:je + 1], dsk[:, jo:jo + 1])
            y_ref[:, 128 * p:128 * p + 128] = yd + yo + xp * dsk_p
            st = lax.dot_general((xdt * _pair(lane_even, cm["dsm"], je)).astype(BF), cm["bc"], TN_DIMS, preferred_element_type=F32)
            gam = jnp.where(sub_even[:, 0:1], cm["gam"][:, je:je + 1], cm["gam"][:, jo:jo + 1])
            h_scr[p] = hp * gam + st
        zc = z_ref[:, :]
        gated = y_ref[:, :] * (zc * _sigmoid(zc))
        yn_ref[:, :] = _rms(gated, nw_ref[...]).astype(BF)

    par = _spec((1, 1, 128), lambda g, c: (g, 0, 0))
    wide = _spec((Q, 512), lambda g, c: (c, g))
    return pl.pallas_call(
        body, grid=(2, NCH),
        in_specs=[wide, _spec((Q, 128), lambda g, c: (c, 8 + g)), _spec((Q, 128), lambda g, c: (c, 10 + g)),
                  _spec((Q, 128), lambda g, c: (c, PDT // 128 + g)), wide, par, par, par, _spec((1, 512), lambda g, c: (0, g))],
        out_specs=[wide, wide, _spec((1, 1, 4, 128, 128), lambda g, c: (g, c, 0, 0, 0))],
        out_shape=[jax.ShapeDtypeStruct((T, SSD_W), BF), jax.ShapeDtypeStruct((T, SSD_W), F32),
                   jax.ShapeDtypeStruct((2, NCH, 4, 128, 128), F32)],
        scratch_shapes=[pltpu.VMEM((4, 128, 128), F32)],
        compiler_params=_params(2), name="ssd_fwd")(xbc_act, xbc_act, xbc_act, proj, proj, dt_bias2, a_log2, d2, norm_w)


def ssd_bwd(dyn, xbc_act, proj, y_pre, h_prev, dt_bias2, a_log2, d2, norm_w):
    def body(dyn_ref, x_ref, b_ref, c_ref, dt_ref, z_ref, y_ref, hp_ref, bias_ref, alog_ref, d_ref, nw_ref,
             dz_ref, dx_ref, db_ref, dc_ref, ddt_ref, dpar_ref, dnw_ref, dh_scr, acc_scr):
        ci = pl.program_id(1)

        @pl.when(ci == 0)
        def _():
            dh_scr[...] = jnp.zeros_like(dh_scr)
            acc_scr[...] = jnp.zeros_like(acc_scr)
            dnw_ref[...] = jnp.zeros_like(dnw_ref)

        bias = bias_ref[0]
        a_neg = -jnp.exp(alog_ref[0])
        dsk = d_ref[0]
        cm = _ssd_chunk_common((NCH - 1 - ci) * Q, dt_ref, b_ref, c_ref, bias, a_neg)
        lane, sub = cm["lane"], cm["sub"]
        lane_even = lane < 64
        sub_even = sub < 64
        zc = z_ref[:, :]
        yc = y_ref[:, :]
        sg = _sigmoid(zc)
        sz = zc * sg
        dgated, dnw = _rms_bwd(dyn_ref[:, :], yc * sz, nw_ref[...])
        dnw_ref[...] += jnp.sum(dnw, axis=0, keepdims=True)
        dz_ref[:, :] = (dgated * yc * (sg * (1.0 + zc * (1.0 - sg)))).astype(BF)
        dy_all = dgated * sz
        dcb = jnp.zeros((Q, Q), F32)
        db_acc = jnp.zeros((Q, Q), F32)
        dc_acc = jnp.zeros((Q, Q), F32)
        dcs_col = jnp.zeros((Q, Q), F32)
        dcs_row = jnp.zeros((Q, Q), F32)
        ddt = jnp.zeros((Q, Q), F32)
        for p in range(4):
            je, jo = 2 * p, 2 * p + 1
            xp = x_ref[:, 128 * p:128 * p + 128]
            dy = dy_all[:, 128 * p:128 * p + 128]
            dt_p = _pair(lane_even, cm["dt"], je)
            xdt = xp * dt_p
            xdt_b = xdt.astype(BF)
            dy_b = dy.astype(BF)
            zero = jnp.zeros_like(dy_b)
            hp = hp_ref[0, 0, p]
            hp_b = hp.astype(BF)
            dh = dh_scr[p]
            dh_b = dh.astype(BF)
            acc_scr[p:p + 1, :] += jnp.sum(dy * xp, axis=0, keepdims=True)
            dsk_p = jnp.where(lane_even[0:1, :], dsk[:, je:je + 1], dsk[:, jo:jo + 1])
            dxp = dy * dsk_p
            e_p = _pair(lane_even, cm["ecs"], je)
            g_p = lax.dot_general(cm["cc"], hp_b, NT_DIMS, preferred_element_type=F32)
            dg_b = (dy * e_p).astype(BF)
            de = dy * g_p * e_p
            dc_acc = dc_acc + jnp.dot(dg_b, hp_b, preferred_element_type=F32)
            dh_in = lax.dot_general(dg_b, cm["cc"], TN_DIMS, preferred_element_type=F32)
            ds_p = _pair(lane_even, cm["dsm"], je)
            r_p = lax.dot_general(cm["bc"], dh_b, NT_DIMS, preferred_element_type=F32)
            dxdt = r_p * ds_p
            tt = r_p * xdt * ds_p
            db_acc = db_acc + jnp.dot((xdt * ds_p).astype(BF), dh_b, preferred_element_type=F32)
            dgam_m = dh * hp
            for j, even in ((je, True), (jo, False)):
                sel = lane_even if even else jnp.logical_not(lane_even)
                ssel = sub_even if even else jnp.logical_not(sub_even)
                l_j = _head_decay(cm, j)
                m_j = cm["cb"] * l_j
                dm = lax.dot_general(jnp.where(sel, dy_b, zero), xdt_b, NT_DIMS, preferred_element_type=F32)
                dxdt = dxdt + lax.dot_general(m_j.astype(BF), jnp.where(sel, dy_b, zero), TN_DIMS, preferred_element_type=F32)
                w_j = dm * m_j
                dcb = dcb + dm * l_j
                t_j = jnp.sum(jnp.where(sel, tt, 0.0), axis=1, keepdims=True)
                col = (jnp.sum(w_j, axis=1, keepdims=True) + jnp.sum(jnp.where(sel, de, 0.0), axis=1, keepdims=True) - t_j)
                gam_j = cm["gam"][:, j:j + 1]
                last = (jnp.sum(t_j, axis=0, keepdims=True)
                        + jnp.sum(jnp.sum(jnp.where(ssel, dgam_m, 0.0), axis=1, keepdims=True), axis=0, keepdims=True) * gam_j)
                col = col + jnp.where(sub[:, 0:1] == Q - 1, last, 0.0)
                dcs_col = dcs_col + jnp.where(lane == j, col, 0.0)
                dcs_row = dcs_row + jnp.where(sub == j, jnp.sum(w_j, axis=0, keepdims=True), 0.0)
            gam = jnp.where(sub_even[:, 0:1], cm["gam"][:, je:je + 1], cm["gam"][:, jo:jo + 1])
            dh_scr[p] = dh_in + dh * gam
            dx_ref[:, 128 * p:128 * p + 128] = dxp + dxdt * dt_p
            dd = dxdt * xp
            ddt = ddt + jnp.where(lane == je, jnp.sum(jnp.where(lane_even, dd, 0.0), axis=1, keepdims=True), 0.0)
            ddt = ddt + jnp.where(lane == jo, jnp.sum(jnp.where(lane_even, 0.0, dd), axis=1, keepdims=True), 0.0)
        dcb_b = dcb.astype(BF)
        dc_ref[:, :] = dc_acc + jnp.dot(dcb_b, cm["bc"], preferred_element_type=F32)
        db_ref[:, :] = db_acc + lax.dot_general(dcb_b, cm["cc"], TN_DIMS, preferred_element_type=F32)
        dcs = dcs_col - dcs_row.T
        dd_a = lax.dot_general(cm["tri"], dcs, TN_DIMS, precision=lax.Precision.HIGHEST, preferred_element_type=F32)
        ddt = ddt + dd_a * a_neg
        acc_scr[5:6, :] += jnp.sum(dd_a * cm["dt"], axis=0, keepdims=True)
        draw = jnp.where(cm["live"], ddt * _sigmoid(cm["dtr"] + bias), 0.0)
        acc_scr[4:5, :] += jnp.sum(draw, axis=0, keepdims=True)
        ddt_ref[:, :] = draw.astype(BF)

        @pl.when(ci == NCH - 1)
        def _():
            lane1 = _lanes((1, 128))
            dd = jnp.zeros((1, 128), F32)
            for p in range(4):
                row = acc_scr[p:p + 1, :]
                dd = dd + jnp.where(lane1 == 2 * p, jnp.sum(jnp.where(lane1 < 64, row, 0.0), axis=1, keepdims=True), 0.0)
                dd = dd + jnp.where(lane1 == 2 * p + 1, jnp.sum(jnp.where(lane1 < 64, 0.0, row), axis=1, keepdims=True), 0.0)
            dpar_ref[0] = jnp.concatenate([acc_scr[4:5, :], acc_scr[5:6, :] * a_neg, dd, jnp.zeros((5, 128), F32)], axis=0)

    par = _spec((1, 1, 128), lambda g, c: (g, 0, 0))
    wide = _spec((Q, 512), lambda g, c: (NCH - 1 - c, g))
    thin = _spec((Q, 128), lambda g, c: (NCH - 1 - c, g))
    return pl.pallas_call(
        body, grid=(2, NCH),
        in_specs=[wide, wide, _spec((Q, 128), lambda g, c: (NCH - 1 - c, 8 + g)), _spec((Q, 128), lambda g, c: (NCH - 1 - c, 10 + g)),
                  _spec((Q, 128), lambda g, c: (NCH - 1 - c, PDT // 128 + g)), wide, wide,
                  _spec((1, 1, 4, 128, 128), lambda g, c: (g, NCH - 1 - c, 0, 0, 0)), par, par, par, _spec((1, 512), lambda g, c: (0, g))],
        out_specs=[wide, wide, thin, thin, thin, _spec((1, 8, 128), lambda g, c: (g, 0, 0)), _spec((1, 512), lambda g, c: (0, g))],
        out_shape=[jax.ShapeDtypeStruct((T, SSD_W), BF), jax.ShapeDtypeStruct((T, SSD_W), F32), jax.ShapeDtypeStruct((T, 256), F32),
                   jax.ShapeDtypeStruct((T, 256), F32), jax.ShapeDtypeStruct((T, 256), BF), jax.ShapeDtypeStruct((2, 8, 128), F32),
                   jax.ShapeDtypeStruct((1, SSD_W), F32)],
        scratch_shapes=[pltpu.VMEM((4, 128, 128), F32), pltpu.VMEM((8, 128), F32)],
        compiler_params=_params(2), name="ssd_bwd")(dyn, xbc_act, xbc_act, xbc_act, proj, proj, y_pre, h_prev, dt_bias2, a_log2, d2, norm_w)


def _lru_gates(back, cw, cb, wa, ba, wx, bx, lam):
    xr = _conv(back, cw, cb)
    xr_b = xr.astype(BF)
    r = _sigmoid(jnp.dot(xr_b, wa, preferred_element_type=F32) + ba)
    i = _sigmoid(jnp.dot(xr_b, wx, preferred_element_type=F32) + bx)
    sp = _softplus(-lam)
    la = (-LRU_C) * r * sp
    a = jnp.exp(la)
    mult = jnp.sqrt(-jnp.tanh(la) * (a * a + 1.0))
    return xr, xr_b, r, i, sp, a, mult


def lru_gates_fwd(proj, cw, cb, wa2, ba, wx2, bx, lam):
    def body(x_ref, cw_ref, cb_ref, wa_ref, ba_ref, wx_ref, bx_ref, lam_ref, a_ref, u_ref, xpad):
        _fill_padded(xpad, x_ref)

        def chunk(r0):
            xr, _, _, i, _, a, mult = _lru_gates(_back(xpad, r0), cw_ref[...], cb_ref[...], wa_ref[0], ba_ref[...], wx_ref[0], bx_ref[...],
                                                 lam_ref[...])
            a_ref[pl.ds(r0, Q), :] = a
            u_ref[pl.ds(r0, Q), :] = jnp.where(_rows(a.shape, r0) >= NPAD, mult * (i * xr), 0.0)

        _chunks(chunk, unrolled=True)

    c0 = PXL // 128
    vec = _spec((1, 128), lambda c: (0, c))
    mat = _spec((1, 128, 128), lambda c: (c, 0, 0))
    return pl.pallas_call(
        body, grid=(8,),
        in_specs=[_spec((T, 128), lambda c: (0, c0 + c)), _spec((4, 128), lambda c: (0, c)), vec, mat, vec, mat, vec, vec],
        out_specs=[_spec((T, 128), lambda c: (0, c)), _spec((T, 128), lambda c: (0, c))],
        out_shape=[jax.ShapeDtypeStruct((T, LRU_W), F32), jax.ShapeDtypeStruct((T, LRU_W), F32)],
        scratch_shapes=[pltpu.VMEM((T + 2 * HALO, 128), F32)],
        compiler_params=_params(), name="lru_gates_fwd")(proj, cw, cb, wa2, ba, wx2, bx, lam)


def lru_scan_fwd(a, u):
    def body(a_ref, u_ref, h_ref):
        def step(i, h):
            base = pl.multiple_of(i * 8, 8)
            for k in range(8):
                h = a_ref[pl.ds(base + k, 1), :] * h + u_ref[pl.ds(base + k, 1), :]
                h_ref[pl.ds(base + k, 1), :] = h
            return h

        lax.fori_loop(0, T // 8, step, jnp.zeros((1, LRU_W), F32))

    return pl.pallas_call(body, out_shape=jax.ShapeDtypeStruct((T, LRU_W), F32), compiler_params=_params(0), name="lru_scan_fwd")(a, u)


def lru_scan_bwd(a, dh_out):
    def body(a_ref, d_ref, o_ref):
        def step(i, carry):
            base = pl.multiple_of(T - 8 - i * 8, 8)
            for k in range(7, -1, -1):
                carry = d_ref[pl.ds(base + k, 1), :] + carry
                o_ref[pl.ds(base + k, 1), :] = carry
                carry = carry * a_ref[pl.ds(base + k, 1), :]
            return carry

        lax.fori_loop(0, T // 8, step, jnp.zeros((1, LRU_W), F32))

    return pl.pallas_call(body, out_shape=jax.ShapeDtypeStruct((T, LRU_W), F32), compiler_params=_params(0), name="lru_scan_bwd")(a, dh_out)


def lru_gates_bwd(dhs, hseq, proj, cw, cb, wa2, ba, wx2, bx, lam):
    def body(dh_ref, h_ref, x_ref, cw_ref, cb_ref, wa_ref, ba_ref, wx_ref, bx_ref, lam_ref,
             dx_ref, dcw_ref, dcb_ref, dwa_ref, dba_ref, dwx_ref, dbx_ref, dlam_ref, xpad, hpad, dpad):
        _fill_padded(xpad, x_ref)
        _fill_padded(hpad, h_ref)
        dpad[0:HALO, :] = jnp.zeros((HALO, 128), F32)
        dpad[T + HALO:T + 2 * HALO, :] = jnp.zeros((HALO, 128), F32)
        for ref in (dcw_ref, dcb_ref, dwa_ref, dba_ref, dwx_ref, dbx_ref, dlam_ref):
            ref[...] = jnp.zeros_like(ref)
        lam = lam_ref[...]

        def first(r0):
            back = _back(xpad, r0)
            xr, xr_b, r, i, sp, a, mult = _lru_gates(back, cw_ref[...], cb_ref[...], wa_ref[0], ba_ref[...], wx_ref[0], bx_ref[...], lam)
            dh = dh_ref[pl.ds(r0, Q), :]
            da = dh * _back(hpad, r0)(1)
            du = jnp.where(_rows(dh.shape, r0) >= NPAD, dh, 0.0)
            dmult = du * (i * xr)
            di = du * (mult * xr)
            dxr = du * (mult * i)
            dla = da * a - dmult * (a * a) / mult
            dr = dla * ((-LRU_C) * sp)
            dlam_ref[...] += jnp.sum(dla * ((-LRU_C) * r), axis=0, keepdims=True)
            dpr = dr * r * (1.0 - r)
            dpi = di * i * (1.0 - i)
            dba_ref[...] += jnp.sum(dpr, axis=0, keepdims=True)
            dbx_ref[...] += jnp.sum(dpi, axis=0, keepdims=True)
            dpr_b = dpr.astype(BF)
            dpi_b = dpi.astype(BF)
            dxr = (dxr + lax.dot_general(dpr_b, wa_ref[0], NT_DIMS, preferred_element_type=F32)
                   + lax.dot_general(dpi_b, wx_ref[0], NT_DIMS, preferred_element_type=F32))
            dwa_ref[0] += lax.dot_general(xr_b, dpr_b, TN_DIMS, preferred_element_type=F32)
            dwx_ref[0] += lax.dot_general(xr_b, dpi_b, TN_DIMS, preferred_element_type=F32)
            dpad[pl.ds(r0 + HALO, Q), :] = dxr
            dcw, dcb = _conv_bwd_w(dxr, back)
            dcw_ref[...] += dcw
            dcb_ref[...] += dcb

        _chunks(first, unrolled=True)
        dlam_ref[...] = -dlam_ref[...] * _sigmoid(-lam)

        def second(r0):
            dx_ref[pl.ds(r0, Q), :] = _conv_bwd_x(_ahead(dpad, r0), cw_ref[...]).astype(BF)

        _chunks(second)

    c0 = PXL // 128
    vec = _spec((1, 128), lambda c: (0, c))
    mat = _spec((1, 128, 128), lambda c: (c, 0, 0))
    col = _spec((T, 128), lambda c: (0, c))
    vshape = jax.ShapeDtypeStruct((1, LRU_W), F32)
    mshape = jax.ShapeDtypeStruct((8, 128, 128), F32)
    pad = pltpu.VMEM((T + 2 * HALO, 128), F32)
    return pl.pallas_call(
        body, grid=(8,),
        in_specs=[col, col, _spec((T, 128), lambda c: (0, c0 + c)), _spec((4, 128), lambda c: (0, c)), vec, mat, vec, mat, vec, vec],
        out_specs=[col, _spec((4, 128), lambda c: (0, c)), vec, mat, vec, mat, vec, vec],
        out_shape=[jax.ShapeDtypeStruct((T, LRU_W), BF), jax.ShapeDtypeStruct((4, LRU_W), F32), vshape, mshape, vshape, mshape, vshape, vshape],
        scratch_shapes=[pad, pad, pad], compiler_params=_params(), name="lru_gates_bwd")(dhs, hseq, proj, cw, cb, wa2, ba, wx2, bx, lam)


def gate_up(h1, wn, w_gate, w_up):
    def body(h_ref, wn_ref, wg_ref, wu_ref, gt_ref, up_ref, act_ref, u_ref):
        for r in (0, HALF):
            u_ref[r:r + HALF, :] = _rms(h_ref[r:r + HALF, :], wn_ref[...]).astype(BF)

        def tile(c0):
            cols = pl.ds(c0, 256)
            gt = lax.dot_general(u_ref[...], wg_ref[cols, :], NT_DIMS, preferred_element_type=F32)
            up = lax.dot_general(u_ref[...], wu_ref[cols, :], NT_DIMS, preferred_element_type=F32)
            gt_ref[:, cols] = gt.astype(BF)
            up_ref[:, cols] = up.astype(BF)
            act_ref[:, cols] = (gt * _sigmoid(gt) * up).astype(BF)

        _col_tiles(D_FF, 256, tile)

    big = jax.ShapeDtypeStruct((T, D_FF), BF)
    return pl.pallas_call(
        body, grid=(T // RC,), in_specs=[_rows_spec(D), _vec(D), _whole((D_FF, D)), _whole((D_FF, D))],
        out_specs=[_rows_spec(D_FF), _rows_spec(D_FF), _rows_spec(D_FF), _rows_spec(D)],
        out_shape=[big, big, big, jax.ShapeDtypeStruct((T, D), BF)],
        compiler_params=_params(), name="gate_up")(h1, wn, w_gate, w_up)


def down_loss(act, w_down, h1, target, wf):
    first = NPAD + N_META

    def body(a_ref, w_ref, r_ref, t_hbm, wf_ref, d_ref, db_ref, l_ref, dw_ref, h_scr, t_ref, t_sem):
        i = pl.program_id(0)
        _zero_at_first(l_ref, dw_ref)
        head = pltpu.make_async_copy(t_hbm.at[pl.ds(0, RC - first)], t_ref.at[pl.ds(first, RC - first)], t_sem)
        rest = pltpu.make_async_copy(t_hbm.at[pl.ds(pl.multiple_of(jnp.maximum(i * RC - first, 0), 32), RC)], t_ref, t_sem)

        @pl.when(i == 0)
        def _():
            t_ref[0:first, :] = jnp.zeros((first, D), F32)
            head.start()

        @pl.when(i > 0)
        def _():
            rest.start()

        def tile(c0):
            cols = pl.ds(c0, 512)
            h_scr[:, cols] = r_ref[:, cols] + jnp.dot(a_ref[...], w_ref[:, cols], preferred_element_type=F32)

        _col_tiles(D, 512, tile)

        @pl.when(i == 0)
        def _():
            head.wait()

        @pl.when(i > 0)
        def _():
            rest.wait()

        for r in (0, HALF):
            h = h_scr[r:r + HALF, :]
            live = _rows((HALF, D), i * RC + r) >= first
            err = jnp.where(live, _rms(h, wf_ref[...]) - t_ref[r:r + HALF, :], 0.0)
            l_ref[...] += 0.5 * jnp.sum(jnp.sum(err * err, axis=1, keepdims=True) * (1.0 / D), axis=0, keepdims=True)
            dh, dw = _rms_bwd(err * (1.0 / D), h, wf_ref[...])
            dw_ref[...] += jnp.sum(dw, axis=0, keepdims=True)
            d_ref[r:r + HALF, :] = dh
            db_ref[r:r + HALF, :] = dh.astype(BF)

    return pl.pallas_call(
        body, grid=(T // RC,),
        in_specs=[_rows_spec(D_FF), _whole((D_FF, D)), _rows_spec(D), pl.BlockSpec(memory_space=pl.ANY), _vec(D)],
        out_specs=[_rows_spec(D), _rows_spec(D), _spec((1, 128), lambda i: (0, 0)), _vec(D)],
        out_shape=[jax.ShapeDtypeStruct((T, D), F32), jax.ShapeDtypeStruct((T, D), BF), jax.ShapeDtypeStruct((1, 128), F32),
                   jax.ShapeDtypeStruct((1, D), F32)],
        scratch_shapes=[pltpu.VMEM((RC, D), F32), pltpu.VMEM((RC, D), F32), pltpu.SemaphoreType.DMA],
        compiler_params=_params(), name="down_loss")(act, w_down, h1, target, wf)


def swiglu_bwd(dh2_b, w_down, gt, up):
    def body(d_ref, w_ref, gt_ref, up_ref, dg_ref, du_ref):
        def tile(c0):
            cols = pl.ds(c0, 256)
            dact = lax.dot_general(d_ref[...], w_ref[cols, :], NT_DIMS, preferred_element_type=F32)
            gt_ = gt_ref[:, cols].astype(F32)
            up_ = up_ref[:, cols].astype(F32)
            sg = _sigmoid(gt_)
            dg_ref[:, cols] = (dact * up_ * (sg * (1.0 + gt_ * (1.0 - sg)))).astype(BF)
            du_ref[:, cols] = (dact * (gt_ * sg)).astype(BF)

        _col_tiles(D_FF, 256, tile)

    big = jax.ShapeDtypeStruct((T, D_FF), BF)
    return pl.pallas_call(
        body, grid=(T // RC,), in_specs=[_rows_spec(D), _whole((D_FF, D)), _rows_spec(D_FF), _rows_spec(D_FF)],
        out_specs=[_rows_spec(D_FF), _rows_spec(D_FF)], out_shape=[big, big], compiler_params=_params(), name="swiglu_bwd")(dh2_b, w_down, gt, up)


def gate_up_bwd(dgt, dup, w_gate, w_up, h1, wn, dh2):
    def body(dg_ref, du_ref, wg_ref, wu_ref, h_ref, wn_ref, r_ref, d_ref, db_ref, dw_ref, du_scr):
        _zero_at_first(dw_ref)

        du_scr[...] = jnp.zeros_like(du_scr)

        def tile(c0):
            k = pl.ds(c0, 256)
            du_scr[...] += (jnp.dot(dg_ref[:, k], wg_ref[k, :], preferred_element_type=F32)
                            + jnp.dot(du_ref[:, k], wu_ref[k, :], preferred_element_type=F32))

        _col_tiles(D_FF, 256, tile)
        for r in (0, HALF):
            dh, dw = _rms_bwd(du_scr[r:r + HALF, :], h_ref[r:r + HALF, :], wn_ref[...])
            dw_ref[...] += jnp.sum(dw, axis=0, keepdims=True)
            dh = dh + r_ref[r:r + HALF, :]
            d_ref[r:r + HALF, :] = dh
            db_ref[r:r + HALF, :] = dh.astype(BF)

    return pl.pallas_call(
        body, grid=(T // RC,),
        in_specs=[_rows_spec(D_FF), _rows_spec(D_FF), _whole((D_FF, D)), _whole((D_FF, D)), _rows_spec(D), _vec(D), _rows_spec(D)],
        out_specs=[_rows_spec(D), _rows_spec(D), _vec(D)],
        out_shape=[jax.ShapeDtypeStruct((T, D), F32), jax.ShapeDtypeStruct((T, D), BF), jax.ShapeDtypeStruct((1, D), F32)],
        scratch_shapes=[pltpu.VMEM((RC, D), F32)],
        compiler_params=_params(), name="gate_up_bwd")(dgt, dup, w_gate, w_up, h1, wn, dh2)


def _adamw(w, g, m, v):
    m = ADAM_B1 * m + (1.0 - ADAM_B1) * g
    v = ADAM_B2 * v + (1.0 - ADAM_B2) * (g * g)
    m_hat = m / (1.0 - ADAM_B1 ** ADAM_STEP)
    v_hat = v / (1.0 - ADAM_B2 ** ADAM_STEP)
    delta = -ADAM_LR * (m_hat / (jnp.sqrt(v_hat) + ADAM_EPS) + ADAM_WD * w)
    return delta, m, v


def adamw_shard(name, recv, w, m, v, tr, tc):
    r, c = w.shape

    def body(p_ref, w_ref, m_ref, v_ref, g_ref, d_ref, mo_ref, vo_ref):
        g = p_ref[0].astype(F32)
        for s in range(1, 8):
            g = g + p_ref[s].astype(F32)
        g_ref[...] = g
        d_ref[...], mo_ref[...], vo_ref[...] = _adamw(w_ref[...], g, m_ref[...], v_ref[...])

    tile = _spec((tr, tc), lambda i, j: (i, j))
    shape = jax.ShapeDtypeStruct((r, c), F32)
    return pl.pallas_call(
        body, grid=(r // tr, c // tc), in_specs=[_spec((8, tr, tc), lambda i, j: (0, i, j)), tile, tile, tile],
        out_specs=[tile] * 4, out_shape=[shape] * 4, compiler_params=_params(2), name=name)(recv, w, m, v)


def sum_slabs(recv):
    def body(p_ref, o_ref):
        g = p_ref[0]
        for s in range(1, 8):
            g = g + p_ref[s]
        o_ref[...] = g

    return pl.pallas_call(body, out_shape=jax.ShapeDtypeStruct(recv.shape[1:], F32), compiler_params=_params(0), name="sum_slabs")(recv)


SIMPLE = [("norm1_w", 1024), ("ssd_conv_b", 1536), ("ssd_dt_bias", 16), ("ssd_a_log", 16), ("ssd_d", 16), ("ssd_norm_w", 1024),
          ("lru_conv_b", 1024), ("lru_ba", 1024), ("lru_bx", 1024), ("lru_lambda", 1024), ("lru_norm_w", 1024), ("norm2_w", 1024),
          ("final_norm_w", 1024)]
SPECIAL = ["lru_wa", "lru_wx", "meta_tokens", "ssd_conv_w", "lru_conv_w"]
SM_ROWS = 176
SM_WA, SM_WX, SM_META, SM_SCW, SM_LCW, SM_LOSS = 14, 78, 142, 158, 166, 170


def _simple_rows():
    rows, r = {}, 0
    for name, n in SIMPLE:
        rows[name] = r
        r += -(-n // 1024)
    return rows


def adamw_small(sm, special_g, ws, ms, vs):
    rows = _simple_rows()
    ns, nx = len(SIMPLE), len(SPECIAL)

    def body(*refs):
        sm_ref = refs[0]
        gx = refs[1:1 + nx]
        wr = refs[1 + nx:1 + nx + ns + nx]
        mr = refs[1 + nx + ns + nx:1 + nx + 2 * (ns + nx)]
        vr = refs[1 + nx + 2 * (ns + nx):1 + nx + 3 * (ns + nx)]
        outs = refs[1 + nx + 3 * (ns + nx):]
        o = 0
        for k, (name, n) in enumerate(SIMPLE):
            r0 = rows[name]
            for c0 in range(0, n, 1024):
                wd = min(1024, n - c0)
                g = sm_ref[r0 + c0 // 1024:r0 + c0 // 1024 + 1, 0:wd]
                sl = (slice(None), slice(c0, c0 + wd))
                d, m2, v2 = _adamw(wr[k][sl], g, mr[k][sl], vr[k][sl])
                outs[o][sl] = g
                outs[o + 1][sl] = d
                outs[o + 2][sl] = m2
                outs[o + 3][sl] = v2
            o += 4
        for k in range(nx):
            d, m2, v2 = _adamw(wr[ns + k][...], gx[k][...], mr[ns + k][...], vr[ns + k][...])
            outs[o][...] = d
            outs[o + 1][...] = m2
            outs[o + 2][...] = v2
            o += 3

    out_shape = []
    for k in range(ns):
        out_shape += [jax.ShapeDtypeStruct(ws[k].shape, F32)] * 4
    for k in range(nx):
        out_shape += [jax.ShapeDtypeStruct(ws[ns + k].shape, F32)] * 3
    return pl.pallas_call(body, out_shape=out_shape, compiler_params=_params(0), name="adamw_small")(sm, *special_g, *ws, *ms, *vs)


def _place():
    return lax.axis_index("x"), lax.axis_index("y"), lax.axis_index("c")


def _index(px, py, pc):
    return 4 * px + 2 * py + pc


def all_gather(name, shards):
    n = len(shards)
    hbm = pl.BlockSpec(memory_space=pl.ANY)

    def body(*refs):
        ins, outs = refs[:n], refs[n:2 * n]
        send_sems, recv_sems, local_sems = refs[2 * n:]
        x, y, c = _place()
        me, sibling = (x, y, c), (x, y, 1 - c)
        chips = [(1 - x, y), (x, 1 - y), (1 - x, 1 - y)]

        def copy(i, k, block, to, src=None):
            dst = outs[i].at[_index(*block)]
            return pltpu.make_async_remote_copy(src_ref=dst if src is None else src, dst_ref=dst, send_sem=send_sems.at[7 * i + k],
                                                recv_sem=recv_sems.at[7 * i + k], device_id=to, device_id_type=MESH)

        mine = [pltpu.make_async_copy(ins[i], outs[i].at[_index(*me)], local_sems.at[i]) for i in range(n)]
        for cp in mine:
            cp.start()
        first = []
        for i in range(n):
            first += [copy(i, 1 + j, me, (*chip, c), src=ins[i]) for j, chip in enumerate(chips)]
            first.append(copy(i, 0, me, sibling, src=ins[i]))
        for cp in first:
            cp.start()
        passed = []
        for i in range(n):
            for j, chip in enumerate(chips):
                copy(i, 1 + j, (*chip, c), me).wait_recv()
                cp = copy(i, 4 + j, (*chip, c), sibling)
                cp.start()
                passed.append(cp)
        for i in range(n):
            copy(i, 0, sibling, me).wait_recv()
            for j, chip in enumerate(chips):
                copy(i, 4 + j, (*chip, 1 - c), me).wait_recv()
        for cp in first + passed:
            cp.wait_send()
        for cp in mine:
            cp.wait()

    return pl.pallas_call(
        body, in_specs=[hbm] * n, out_specs=[hbm] * n,
        out_shape=[jax.ShapeDtypeStruct((8,) + s.shape, s.dtype) for s in shards],
        scratch_shapes=[pltpu.SemaphoreType.DMA((7 * n,)), pltpu.SemaphoreType.DMA((7 * n,)), pltpu.SemaphoreType.DMA((n,))],
        name=name)(*shards)


HBM_SPEC = pl.BlockSpec(memory_space=pltpu.HBM)
SEM_SPEC = pl.BlockSpec(memory_space=pltpu.SEMAPHORE)
EFFECT = pltpu.SideEffectType.DATAFLOW_SIDE_EFFECTING


def _peers(x, y, c):
    return [((1 - x) if k & 4 else x, (1 - y) if k & 2 else y, (1 - c) if k & 1 else c) for k in range(1, 8)]


def _pieces(rows):
    for n in (4, 2):
        if rows % (16 * n) == 0:
            return [(r * (rows // n), rows // n) for r in range(n)]
    return [(0, rows)]


def _peer_copies(src, land, send_sems, recv_sems, k, peer, mine, slab_src):
    block = src.at[_index(*peer)] if slab_src else src
    return [pltpu.make_async_remote_copy(src_ref=block.at[pl.ds(r0, nr)], dst_ref=land.at[mine, pl.ds(r0, nr)], send_sem=send_sems.at[k],
                                         recv_sem=recv_sems.at[k], device_id=peer, device_id_type=MESH)
            for r0, nr in _pieces(block.shape[0])]


def copies_start(name, srcs, slab_src, after):
    n = len(srcs)
    zones = [jax.ShapeDtypeStruct(s.shape if slab_src else (8,) + s.shape, s.dtype) for s in srcs]

    def body(*refs):
        ins, lands = refs[:n], refs[n:2 * n]
        sends, recvs = refs[2 * n + 1:3 * n + 1], refs[3 * n + 1:4 * n + 1]
        token = refs[-1]
        x, y, c = _place()
        mine = _index(x, y, c)
        for i in range(n):
            per_peer = [_peer_copies(ins[i], lands[i], sends[i], recvs[i], k, peer, mine, slab_src) for k, peer in enumerate(_peers(x, y, c))]
            for piece in zip(*per_peer):
                for cp in piece:
                    cp.start()
        token[...] = jnp.zeros_like(token)

    sem = pltpu.SemaphoreType.DMA((7,))
    res = pl.pallas_call(
        body, name=name,
        out_shape=([sem] * (2 * n) + [pltpu.HBM(s.shape, s.dtype) for s in srcs] + [pltpu.HBM(z.shape, z.dtype) for z in zones]
                   + [jax.ShapeDtypeStruct((8, 128), F32)]),
        in_specs=[HBM_SPEC] * (2 * n) + [pl.BlockSpec(memory_space=pl.ANY)],
        out_specs=[SEM_SPEC] * (2 * n) + [HBM_SPEC] * (2 * n) + [pl.BlockSpec(memory_space=pltpu.VMEM)],
        input_output_aliases={i: 2 * n + i for i in range(2 * n)},
        compiler_params=pltpu.CompilerParams(has_side_effects=EFFECT),
    )(*[pltpu.with_memory_space_constraint(s, pltpu.HBM) for s in srcs],
      *[pltpu.with_memory_space_constraint(lax.empty(z.shape, z.dtype), pltpu.HBM) for z in zones], after)
    return [(res[i], res[n + i], res[2 * n + i], res[3 * n + i]) for i in range(n)], res[-1][0:1, 0:1]


def copies_wait(name, started, slab_src, after):
    n = len(started)

    def body(*refs):
        ins, lands = refs[:n], refs[n:2 * n]
        sends, recvs = refs[2 * n:3 * n], refs[3 * n:4 * n]
        x, y, c = _place()
        mine = _index(x, y, c)
        for i in range(n):
            for k, peer in enumerate(_peers(x, y, c)):
                arrival = pltpu.make_async_remote_copy(src_ref=ins[i].at[mine] if slab_src else ins[i], dst_ref=lands[i].at[_index(*peer)],
                                                       send_sem=sends[i].at[k], recv_sem=recvs[i].at[k], device_id=peer, device_id_type=MESH)
                arrival.wait_send()
                arrival.wait_recv()

    srcs = [s[2] for s in started]
    lands = [s[3] for s in started]
    afters = list(after) if isinstance(after, (list, tuple)) else [after]
    res = pl.pallas_call(
        body, name=name,
        out_shape=[pltpu.HBM(s.shape, s.dtype) for s in srcs] + [pltpu.HBM(z.shape, z.dtype) for z in lands],
        in_specs=[HBM_SPEC] * (2 * n) + [SEM_SPEC] * (2 * n) + [pl.BlockSpec(memory_space=pl.ANY)] * len(afters),
        out_specs=[HBM_SPEC] * (2 * n),
        input_output_aliases={i: i for i in range(2 * n)},
        compiler_params=pltpu.CompilerParams(has_side_effects=EFFECT),
    )(*srcs, *lands, *[s[0] for s in started], *[s[1] for s in started], *afters)
    me = _index(*_place())
    own = [lax.dynamic_index_in_dim(s, me, 0, keepdims=True) if slab_src else s[None] for s in res[:n]]
    return [lax.dynamic_update_slice_in_dim(z, o, me, 0) for z, o in zip(res[n:], own)]


WEIGHTS = ["meta_tokens", "norm1_w", "w_in", "ssd_conv_w", "ssd_conv_b", "ssd_dt_bias", "ssd_a_log", "ssd_d", "ssd_norm_w", "lru_conv_w",
           "lru_conv_b", "lru_wa", "lru_ba", "lru_wx", "lru_bx", "lru_lambda", "lru_norm_w", "w_out", "norm2_w", "w_gate", "w_up", "w_down",
           "final_norm_w"]
BIG = ["w_in", "w_out", "w_gate", "w_up", "w_down"]
COLUMN_SHARDED = ["w_in", "w_gate", "w_up"]
BIG_TILE = {"w_in": (578, 256), "w_out": (128, 1024), "w_gate": (176, 1024), "w_up": (176, 1024), "w_down": (176, 1024)}


def _pair_blocks(w):
    w = w.reshape(8, 2, 64, 64)
    z = jnp.zeros((8, 64, 64), w.dtype)
    return jnp.concatenate([jnp.concatenate([w[:, 0], z], axis=2), jnp.concatenate([z, w[:, 1]], axis=2)], axis=1)


def _unpair_blocks(w2):
    return jnp.stack([w2[:, :64, :64], w2[:, 64:, 64:]], axis=1).reshape(16, 64, 64)


def _per_group(v):
    return jnp.pad(v.reshape(2, 1, 8), ((0, 0), (0, 0), (0, 120)))


def _pad_cols(v, n):
    return jnp.pad(v, ((0, 0), (0, n - v.shape[1])))


def local_step(x, target, meta, ssd_cw, lru_cw, w_in, fetch, send, p):
    z120 = jnp.zeros((120, D), BF)
    w_dt = jnp.concatenate([w_in[2560:2568], z120, w_in[2568:2576], z120], axis=0)
    bias2, alog2, d2 = _per_group(p["ssd_dt_bias"]), _per_group(p["ssd_a_log"]), _per_group(p["ssd_d"])
    wa2 = _pair_blocks(p["lru_wa"]).astype(BF)
    wx2 = _pair_blocks(p["lru_wx"]).astype(BF)
    lru = (lru_cw, p["lru_conv_b"], wa2, p["lru_ba"], wx2, p["lru_bx"], p["lru_lambda"])

    h0 = jnp.concatenate([jnp.zeros((NPAD, D), F32), meta, x], axis=0)
    proj, u1 = in_proj(h0, p["norm1_w"], w_in, w_dt)
    xbc_act = conv_silu_fwd(proj, ssd_cw, p["ssd_conv_b"])
    yn_ssd, y_pre, h_prev = ssd_fwd(xbc_act, proj, bias2, alog2, d2, p["ssd_norm_w"])
    a, u = lru_gates_fwd(proj, *lru)
    hseq = lru_scan_fwd(a, u)
    (w_out,) = fetch(["w_out"], hseq)
    h1, cat = out_proj(yn_ssd, proj, hseq, p["lru_norm_w"], w_out, h0)
    w_gate, w_up = fetch(["w_gate", "w_up"], h1)
    gt, up, act, u2 = gate_up(h1, p["norm2_w"], w_gate, w_up)
    (w_down,) = fetch(["w_down"], act)
    dh2, dh2_b, loss, d_fnw = down_loss(act, w_down, h1, target, p["final_norm_w"])

    dgt, dup = swiglu_bwd(dh2_b, w_down, gt, up)
    g_down = matmul_tn("dw_down", act, dh2_b, 1408, 512)
    g_gate = matmul_tn("dw_gate", dgt, u2, 1408, 512)
    g_up = matmul_tn("dw_up", dup, u2, 1408, 512)
    sent = send({"w_down": g_down, "w_gate": g_gate, "w_up": g_up})
    dh1, dh1_b, d_n2 = gate_up_bwd(dgt, dup, w_gate, w_up, h1, p["norm2_w"] + sent, dh2)
    sent = send({"w_out": matmul_tn("dw_out", cat, dh1_b, 512, 1024)})
    dyn, dh_out, dg_b, d_lnw = out_proj_bwd(dh1_b, w_out, proj, hseq, p["lru_norm_w"] + sent)

    dhs = lru_scan_bwd(a, dh_out)
    dxl_b, d_lcw, d_lcb, dwa2, d_ba, dwx2, d_bx, d_lam = lru_gates_bwd(dhs, hseq, proj, *lru)
    dz_b, dx, d_b, d_c, ddt_b, dpar, d_snw = ssd_bwd(dyn, xbc_act, proj, y_pre, h_prev, bias2, alog2, d2, p["ssd_norm_w"])
    dxbc_b, d_scw, d_scb = conv_silu_bwd(dx, d_b, d_c, proj, ssd_cw, p["ssd_conv_b"])
    g_dt = matmul_tn("dw_in_dt", ddt_b, u1, 256, 512)
    g_in = jnp.concatenate([matmul_tn("dw_in_z", dz_b, u1, 512, 1024), matmul_tn("dw_in_xbc", dxbc_b, u1, 512, 1024), g_dt[0:8], g_dt[128:136],
                            matmul_tn("dw_in_g", dg_b, u1, 512, 1024), matmul_tn("dw_in_xl", dxl_b, u1, 512, 1024)], axis=0)
    sent = send({"w_in": g_in})
    dh0, d_n1 = in_proj_bwd(dz_b, dg_b, dxl_b, dxbc_b, ddt_b, w_in, w_dt, h0, p["norm1_w"] + sent, dh1)
    small = {"norm1_w": d_n1, "ssd_conv_b": d_scb, "ssd_dt_bias": dpar[:, 0, :8].reshape(1, 16), "ssd_a_log": dpar[:, 1, :8].reshape(1, 16),
             "ssd_d": dpar[:, 2, :8].reshape(1, 16), "ssd_norm_w": d_snw, "lru_conv_b": d_lcb, "lru_ba": d_ba, "lru_bx": d_bx,
             "lru_lambda": d_lam, "lru_norm_w": d_lnw, "norm2_w": d_n2, "final_norm_w": d_fnw,
             "lru_wa": _unpair_blocks(dwa2), "lru_wx": _unpair_blocks(dwx2), "meta_tokens": dh0[NPAD:NPAD + N_META],
             "ssd_conv_w": d_scw, "lru_conv_w": d_lcw}
    return loss, dh0[NPAD + N_META:], small


def _pack_small(small, loss):
    rows = [_pad_cols(small[name], -(-n // 1024) * 1024).reshape(-1, 1024) for name, n in SIMPLE]
    rows += [small["lru_wa"].reshape(64, 1024), small["lru_wx"].reshape(64, 1024), small["meta_tokens"],
             _pad_cols(small["ssd_conv_w"], 2048).reshape(8, 1024), small["lru_conv_w"], _pad_cols(loss[:, 0:1], 1024)]
    sm = jnp.concatenate(rows, axis=0)
    return jnp.pad(sm, ((0, SM_ROWS - sm.shape[0]), (0, 0)))


def _slabs(g):
    return g.reshape(8, g.shape[0] // 8, g.shape[1])


def _unslab(g):
    return g.reshape(8 * g.shape[1], g.shape[2])


def kernel(x, meta_tokens, norm1_w, w_in, ssd_conv_w, ssd_conv_b, ssd_dt_bias, ssd_a_log, ssd_d, ssd_norm_w, lru_conv_w, lru_conv_b, lru_wa, lru_ba, lru_wx, lru_bx, lru_lambda, lru_norm_w, w_out, norm2_w, w_gate, w_up, w_down, final_norm_w, loss_target, m_meta_tokens, m_norm1_w, m_w_in, m_ssd_conv_w, m_ssd_conv_b, m_ssd_dt_bias, m_ssd_a_log, m_ssd_d, m_ssd_norm_w, m_lru_conv_w, m_lru_conv_b, m_lru_wa, m_lru_ba, m_lru_wx, m_lru_bx, m_lru_lambda, m_lru_norm_w, m_w_out, m_norm2_w, m_w_gate, m_w_up, m_w_down, m_final_norm_w, v_meta_tokens, v_norm1_w, v_w_in, v_ssd_conv_w, v_ssd_conv_b, v_ssd_dt_bias, v_ssd_a_log, v_ssd_d, v_ssd_norm_w, v_lru_conv_w, v_lru_conv_b, v_lru_wa, v_lru_ba, v_lru_wx, v_lru_bx, v_lru_lambda, v_lru_norm_w, v_w_out, v_norm2_w, v_w_gate, v_w_up, v_w_down, v_final_norm_w):
    w = dict(meta_tokens=meta_tokens, norm1_w=norm1_w, w_in=w_in[0], ssd_conv_w=ssd_conv_w[0], ssd_conv_b=ssd_conv_b, ssd_dt_bias=ssd_dt_bias,
             ssd_a_log=ssd_a_log, ssd_d=ssd_d, ssd_norm_w=ssd_norm_w, lru_conv_w=lru_conv_w[0], lru_conv_b=lru_conv_b, lru_wa=lru_wa[0],
             lru_ba=lru_ba, lru_wx=lru_wx[0], lru_bx=lru_bx, lru_lambda=lru_lambda, lru_norm_w=lru_norm_w, w_out=w_out[0], norm2_w=norm2_w,
             w_gate=w_gate[0], w_up=w_up[0], w_down=w_down[0], final_norm_w=final_norm_w.reshape(1, D))
    m = dict(meta_tokens=m_meta_tokens, norm1_w=m_norm1_w, w_in=m_w_in[0], ssd_conv_w=m_ssd_conv_w[0], ssd_conv_b=m_ssd_conv_b,
             ssd_dt_bias=m_ssd_dt_bias, ssd_a_log=m_ssd_a_log, ssd_d=m_ssd_d, ssd_norm_w=m_ssd_norm_w, lru_conv_w=m_lru_conv_w[0],
             lru_conv_b=m_lru_conv_b, lru_wa=m_lru_wa[0], lru_ba=m_lru_ba, lru_wx=m_lru_wx[0], lru_bx=m_lru_bx, lru_lambda=m_lru_lambda,
             lru_norm_w=m_lru_norm_w, w_out=m_w_out[0], norm2_w=m_norm2_w, w_gate=m_w_gate[0], w_up=m_w_up[0], w_down=m_w_down[0],
             final_norm_w=m_final_norm_w.reshape(1, D))
    v = dict(meta_tokens=v_meta_tokens, norm1_w=v_norm1_w, w_in=v_w_in[0], ssd_conv_w=v_ssd_conv_w[0], ssd_conv_b=v_ssd_conv_b,
             ssd_dt_bias=v_ssd_dt_bias, ssd_a_log=v_ssd_a_log, ssd_d=v_ssd_d, ssd_norm_w=v_ssd_norm_w, lru_conv_w=v_lru_conv_w[0],
             lru_conv_b=v_lru_conv_b, lru_wa=v_lru_wa[0], lru_ba=v_lru_ba, lru_wx=v_lru_wx[0], lru_bx=v_lru_bx, lru_lambda=v_lru_lambda,
             lru_norm_w=v_lru_norm_w, w_out=v_w_out[0], norm2_w=v_norm2_w, w_gate=v_w_gate[0], w_up=v_w_up[0], w_down=v_w_down[0],
             final_norm_w=v_final_norm_w.reshape(1, D))
    shapes = dict(meta_tokens=meta_tokens.shape, norm1_w=norm1_w.shape, w_in=w_in.shape, ssd_conv_w=ssd_conv_w.shape,
                  ssd_conv_b=ssd_conv_b.shape, ssd_dt_bias=ssd_dt_bias.shape, ssd_a_log=ssd_a_log.shape, ssd_d=ssd_d.shape,
                  ssd_norm_w=ssd_norm_w.shape, lru_conv_w=lru_conv_w.shape, lru_conv_b=lru_conv_b.shape, lru_wa=lru_wa.shape,
                  lru_ba=lru_ba.shape, lru_wx=lru_wx.shape, lru_bx=lru_bx.shape, lru_lambda=lru_lambda.shape, lru_norm_w=lru_norm_w.shape,
                  w_out=w_out.shape, norm2_w=norm2_w.shape, w_gate=w_gate.shape, w_up=w_up.shape, w_down=w_down.shape,
                  final_norm_w=final_norm_w.shape)
    me = _index(*_place())
    for n in COLUMN_SHARDED:
        w[n], m[n], v[n] = w[n].T, m[n].T, v[n].T

    small_shard = jnp.concatenate([w["meta_tokens"], _pad_cols(w["ssd_conv_w"], 256).reshape(8, 128), w["lru_conv_w"],
                                   jnp.zeros((4, 128), F32)], axis=0)
    g_in, gs = all_gather("gather_w_in", [w["w_in"].astype(BF), small_shard])
    later = ["w_out", "w_gate", "w_up", "w_down"]
    started, behind = copies_start("gather_rest_start", [w[n].astype(BF) for n in later], False, gs)
    started = dict(zip(later, started))
    meta_full = gs[:, 0:16].transpose(1, 0, 2).reshape(N_META, D)
    ssd_cw = gs[:, 16:24].reshape(8, 4, 256)[:, :, :192].transpose(1, 0, 2).reshape(4, XBC)
    lru_cw = gs[:, 24:28].transpose(1, 0, 2).reshape(4, LRU_W)

    def fetch(names, after):
        got = copies_wait("gather_" + names[0] + "_wait", [started[n] for n in names], False, after)
        return [_unslab(g) for g in got]

    in_flight = {}

    def send(grads):
        names = list(grads)
        st, token = copies_start("grads_" + names[0] + "_start", [grads[n] if n == "small" else _slabs(grads[n]) for n in names], True,
                                 grads[names[0]])
        in_flight.update(zip(names, st))
        return token

    loss, grad_x, small = local_step(x[0], loss_target[0], meta_full, ssd_cw, lru_cw, _unslab(g_in), fetch, send,
                                     {**w, "norm1_w": w["norm1_w"] + behind})
    send({"small": _pack_small(small, loss).reshape(8, SM_ROWS // 8, 1024)})

    out = {}
    early = ["w_down", "w_gate", "w_up", "w_out"]
    recv = dict(zip(early, copies_wait("grads_early_wait", [in_flight[n] for n in early], True, in_flight["small"][2])))
    for n in early:
        out[n] = adamw_shard("adamw_" + n, recv[n], w[n], m[n], v[n], *BIG_TILE[n])
    recv_in, recv_small = copies_wait("grads_late_wait", [in_flight["w_in"], in_flight["small"]], True, [out[n][0] for n in early])
    out["w_in"] = adamw_shard("adamw_w_in", recv_in, w["w_in"], m["w_in"], v["w_in"], *BIG_TILE["w_in"])
    for n in COLUMN_SHARDED:
        out[n] = [o.T for o in out[n]]
    sm = all_gather("gather_small_grads", [sum_slabs(recv_small)])[0].reshape(SM_ROWS, 1024)
    special_g = [sm[SM_WA:SM_WA + 64].reshape(16, 64, 64), sm[SM_WX:SM_WX + 64].reshape(16, 64, 64),
                 lax.dynamic_slice(sm[SM_META:SM_META + 16], (0, 128 * me), (16, 128)),
                 lax.dynamic_slice(sm[SM_SCW:SM_SCW + 8].reshape(4, 2048), (0, 192 * me), (4, 192)),
                 lax.dynamic_slice(sm[SM_LCW:SM_LCW + 4], (0, 128 * me), (4, 128))]
    names = [n for n, _ in SIMPLE] + SPECIAL
    res = adamw_small(sm, special_g, [w[n] for n in names], [m[n] for n in names], [v[n] for n in names])
    for k, (n, _) in enumerate(SIMPLE):
        out[n] = res[4 * k:4 * k + 4]
    for k, n in enumerate(SPECIAL):
        o = 4 * len(SIMPLE) + 3 * k
        out[n] = [special_g[k]] + list(res[o:o + 3])
    loss_total = sm[SM_LOSS, 0]
    flat = [loss_total, grad_x[None]]
    for k in range(4):
        flat += [out[n][k].reshape(shapes[n]) for n in WEIGHTS]
    return tuple(flat)
```

```python
import math

import jax
import jax.numpy as jnp
from jax import lax
from jax.experimental import pallas as pl
from jax.experimental.pallas import tpu as pltpu

F32 = jnp.float32
BF = jnp.bfloat16

D = 1024
SEQ = 2048
N_META = 16
Q = 128
NPAD = 112
T = NPAD + N_META + SEQ
NCH = T // Q
RC = 544
D_FF = 2816
SSD_W = 1024
LRU_W = 1024
XBC = 1536
IN_COLS = 4624
PZ, PG, PXL, PXBC, PDT = 0, 1024, 2048, 3072, 4608
NP_IN = 4864
EPS = 1e-6
LRU_C = 8.0
VMEM_LIMIT = 56 * 1024 * 1024

ADAM_LR, ADAM_B1, ADAM_B2, ADAM_EPS, ADAM_WD, ADAM_STEP = 0.001, 0.9, 0.999, 1e-08, 0.01, 10

NT_DIMS = (((1,), (1,)), ((), ()))
TN_DIMS = (((0,), (0,)), ((), ()))
MESH = pl.DeviceIdType.MESH


def _params(n_grid=1, limit=VMEM_LIMIT):
    return pltpu.CompilerParams(dimension_semantics=("arbitrary",) * n_grid, vmem_limit_bytes=limit)


def _spec(shape, imap, single=False):
    if single:
        return pl.BlockSpec(shape, imap, pipeline_mode=pl.Buffered(1))
    return pl.BlockSpec(shape, imap)


def _sigmoid(x):
    return 1.0 / (1.0 + jnp.exp(-x))


def _softplus(x):
    return jnp.maximum(x, 0.0) + jnp.log(1.0 + jnp.exp(-jnp.abs(x)))


def _rms_stats(h):
    return lax.rsqrt(jnp.mean(h * h, axis=-1, keepdims=True) + EPS)


def _rms(h, w):
    return (h * _rms_stats(h)) * w


def _rms_bwd(du, h, w):
    r = _rms_stats(h)
    n = h * r
    dn = du * w
    dh = r * (dn - n * jnp.mean(dn * n, axis=-1, keepdims=True))
    return dh, du * n


_G0 = math.sqrt(2.0 / math.pi)


def _gelu(x):
    return 0.5 * x * (1.0 + jnp.tanh(_G0 * (x + 0.044715 * (x * x * x))))


def _gelu_grad(x):
    t = jnp.tanh(_G0 * (x + 0.044715 * (x * x * x)))
    return 0.5 * (1.0 + t) + 0.5 * x * (1.0 - t * t) * (_G0 * (1.0 + 3.0 * 0.044715 * (x * x)))


def _rows(shape, r0=0):
    return lax.broadcasted_iota(jnp.int32, shape, 0) + r0


def _lanes(shape):
    return lax.broadcasted_iota(jnp.int32, shape, 1)


HALO = 8


def _fill_padded(pad_ref, x_ref):
    pad_ref[0:HALO, :] = jnp.zeros((HALO, pad_ref.shape[1]), F32)
    pad_ref[T + HALO:T + 2 * HALO, :] = jnp.zeros((HALO, pad_ref.shape[1]), F32)

    def step(c, carry):
        r0 = pl.multiple_of(c * Q, Q)
        pad_ref[pl.ds(r0 + HALO, Q), :] = x_ref[pl.ds(r0, Q), :]
        return carry

    lax.fori_loop(0, NCH, step, 0)


def _back(pad_ref, r0):
    win = pad_ref[pl.ds(r0, Q + HALO), :]
    return lambda s: win[HALO:, :] if s == 0 else pltpu.roll(win, s, axis=0)[HALO:, :]


def _ahead(pad_ref, r0):
    win = pad_ref[pl.ds(r0 + HALO, Q + HALO), :]
    return lambda s: win[:Q, :] if s == 0 else pltpu.roll(win, Q + HALO - s, axis=0)[:Q, :]


def _conv(back, w, b):
    y = b + w[3:4, :] * back(0)
    for k in range(3):
        y = y + w[k:k + 1, :] * back(3 - k)
    return y


def _conv_bwd_x(ahead, w):
    dx = w[3:4, :] * ahead(0)
    for k in range(3):
        dx = dx + w[k:k + 1, :] * ahead(3 - k)
    return dx


def _conv_bwd_w(dy, back):
    dws = [jnp.sum(dy * back(3 - k), axis=0, keepdims=True) for k in range(4)]
    return jnp.concatenate(dws, axis=0), jnp.sum(dy, axis=0, keepdims=True)


def _chunks(fn, unrolled=False):
    if unrolled:
        for c in range(NCH):
            fn(c * Q)
        return

    def step(c, carry):
        fn(pl.multiple_of(c * Q, Q))
        return carry

    lax.fori_loop(0, NCH, step, 0)


HALF = RC // 2


def _col_tiles(n, tn, fn):
    def step(j, carry):
        fn(pl.multiple_of(j * tn, tn))
        return carry

    lax.fori_loop(0, n // tn, step, 0)


def _rows_spec(cols, block_col=0):
    return _spec((RC, cols), lambda i: (i, block_col))


def _whole(shape):
    return _spec(shape, lambda i: tuple(0 for _ in shape), single=True)


def _vec(cols):
    return _spec((1, cols), lambda i: (0, 0))


def _zero_at_first(*refs):
    @pl.when(pl.program_id(0) == 0)
    def _():
        for r in refs:
            r[...] = jnp.zeros_like(r)


IN_RUNS = ((PZ, 0, 1024), (PG, 2576, 2048), (PXBC, 1024, XBC))


def _in_tiles(fn):
    for pcol, wrow, width in IN_RUNS:
        def step(j, carry, pcol=pcol, wrow=wrow):
            fn(pl.multiple_of(pcol + j * 512, 512), pl.multiple_of(wrow + j * 512, 16))
            return carry

        lax.fori_loop(0, width // 512, step, 0)


def in_proj(h0, wn, w_t, w_dt):
    def body(h_ref, wn_ref, w_ref, wdt_ref, o_ref, u_ref):
        for r in (0, HALF):
            u_ref[r:r + HALF, :] = _rms(h_ref[r:r + HALF, :], wn_ref[...]).astype(BF)

        def tile(pcol, wrow):
            o_ref[:, pl.ds(pcol, 512)] = lax.dot_general(u_ref[...], w_ref[pl.ds(wrow, 512), :], NT_DIMS, preferred_element_type=F32)

        _in_tiles(tile)
        o_ref[:, PDT:PDT + 256] = lax.dot_general(u_ref[...], wdt_ref[...], NT_DIMS, preferred_element_type=F32)

    return pl.pallas_call(
        body, grid=(T // RC,), in_specs=[_rows_spec(D), _vec(D), _whole((IN_COLS, D)), _whole((256, D))],
        out_specs=[_rows_spec(NP_IN), _rows_spec(D)],
        out_shape=[jax.ShapeDtypeStruct((T, NP_IN), F32), jax.ShapeDtypeStruct((T, D), BF)],
        compiler_params=_params(), name="in_proj")(h0, wn, w_t, w_dt)


def out_proj(yn_ssd, proj, hseq, lru_nw, w_out, h0):
    def body(y_ref, g_ref, h_ref, wn_ref, w_ref, r_ref, o_ref, cat_ref):
        cat_ref[:, 0:SSD_W] = y_ref[...]
        for r in (0, HALF):
            y = _gelu(g_ref[r:r + HALF, :]) * h_ref[r:r + HALF, :]
            cat_ref[r:r + HALF, SSD_W:] = _rms(y, wn_ref[...]).astype(BF)

        def tile(c0):
            o_ref[:, pl.ds(c0, 512)] = r_ref[:, pl.ds(c0, 512)] + jnp.dot(cat_ref[...], w_ref[:, pl.ds(c0, 512)], preferred_element_type=F32)

        _col_tiles(D, 512, tile)

    return pl.pallas_call(
        body, grid=(T // RC,),
        in_specs=[_rows_spec(SSD_W), _rows_spec(LRU_W, PG // LRU_W), _rows_spec(LRU_W), _vec(LRU_W), _whole((SSD_W + LRU_W, D)), _rows_spec(D)],
        out_specs=[_rows_spec(D), _rows_spec(SSD_W + LRU_W)],
        out_shape=[jax.ShapeDtypeStruct((T, D), F32), jax.ShapeDtypeStruct((T, SSD_W + LRU_W), BF)],
        compiler_params=_params(), name="out_proj")(yn_ssd, proj, hseq, lru_nw, w_out, h0)


def out_proj_bwd(dh1_b, w_out, proj, hseq, lru_nw):
    def body(d_ref, w_ref, g_ref, h_ref, wn_ref, dy_ref, dh_ref, dg_ref, dw_ref, dl_scr):
        _zero_at_first(dw_ref)

        def tile(c0):
            dy_ref[:, pl.ds(c0, 512)] = lax.dot_general(d_ref[...], w_ref[pl.ds(c0, 512), :], NT_DIMS, preferred_element_type=F32)
            dl_scr[:, pl.ds(c0, 512)] = lax.dot_general(d_ref[...], w_ref[pl.ds(SSD_W + c0, 512), :], NT_DIMS, preferred_element_type=F32)

        _col_tiles(SSD_W, 512, tile)
        for r in (0, HALF):
            g = g_ref[r:r + HALF, :]
            h = h_ref[r:r + HALF, :]
            ge = _gelu(g)
            dy, dw = _rms_bwd(dl_scr[r:r + HALF, :], ge * h, wn_ref[...])
            dw_ref[...] += jnp.sum(dw, axis=0, keepdims=True)
            dh_ref[r:r + HALF, :] = dy * ge
            dg_ref[r:r + HALF, :] = (dy * h * _gelu_grad(g)).astype(BF)

    return pl.pallas_call(
        body, grid=(T // RC,),
        in_specs=[_rows_spec(D), _whole((SSD_W + LRU_W, D)), _rows_spec(LRU_W, PG // LRU_W), _rows_spec(LRU_W), _vec(LRU_W)],
        out_specs=[_rows_spec(SSD_W), _rows_spec(LRU_W), _rows_spec(LRU_W), _vec(LRU_W)],
        out_shape=[jax.ShapeDtypeStruct((T, SSD_W), F32), jax.ShapeDtypeStruct((T, LRU_W), F32), jax.ShapeDtypeStruct((T, LRU_W), BF),
                   jax.ShapeDtypeStruct((1, LRU_W), F32)],
        scratch_shapes=[pltpu.VMEM((RC, LRU_W), F32)],
        compiler_params=_params(), name="out_proj_bwd")(dh1_b, w_out, proj, hseq, lru_nw)


def in_proj_bwd(dz, dg, dxl, dxbc, ddt, w_t, w_dt, h0, wn, dh1):
    def body(dz_ref, dg_ref, dxl_ref, dxbc_ref, ddt_ref, w_ref, wdt_ref, h_ref, wn_ref, r_ref, o_ref, dw_ref, du_scr):
        _zero_at_first(dw_ref)
        du_scr[...] = jnp.dot(ddt_ref[...], wdt_ref[...], preferred_element_type=F32)
        for d_ref, wrow, width in ((dz_ref, 0, 1024), (dxbc_ref, 1024, XBC), (dg_ref, 2576, 1024), (dxl_ref, 3600, 1024)):
            def step(j, carry, d_ref=d_ref, wrow=wrow):
                c0 = pl.multiple_of(j * 512, 512)
                du_scr[...] += jnp.dot(d_ref[:, pl.ds(c0, 512)], w_ref[pl.ds(pl.multiple_of(wrow + c0, 16), 512), :], preferred_element_type=F32)
                return carry

            lax.fori_loop(0, width // 512, step, 0)
        for r in (0, HALF):
            dh, dw = _rms_bwd(du_scr[r:r + HALF, :], h_ref[r:r + HALF, :], wn_ref[...])
            dw_ref[...] += jnp.sum(dw, axis=0, keepdims=True)
            o_ref[r:r + HALF, :] = dh + r_ref[r:r + HALF, :]

    return pl.pallas_call(
        body, grid=(T // RC,),
        in_specs=[_rows_spec(SSD_W), _rows_spec(LRU_W), _rows_spec(LRU_W), _rows_spec(XBC), _rows_spec(256), _whole((IN_COLS, D)),
                  _whole((256, D)), _rows_spec(D), _vec(D), _rows_spec(D)],
        out_specs=[_rows_spec(D), _vec(D)],
        out_shape=[jax.ShapeDtypeStruct((T, D), F32), jax.ShapeDtypeStruct((1, D), F32)],
        scratch_shapes=[pltpu.VMEM((RC, D), F32)],
        compiler_params=_params(), name="in_proj_bwd")(dz, dg, dxl, dxbc, ddt, w_t, w_dt, h0, wn, dh1)


def matmul_tn(name, a, b, tm, tn):
    m, n = a.shape[1], b.shape[1]

    def body(a_ref, b_ref, o_ref, acc_ref):
        acc_ref[...] = jnp.zeros_like(acc_ref)

        def mm(r0):
            acc_ref[...] += lax.dot_general(a_ref[pl.ds(r0, RC), :], b_ref[pl.ds(r0, RC), :], TN_DIMS, preferred_element_type=F32)

        _col_tiles(T, RC, mm)
        o_ref[...] = acc_ref[...].astype(BF)

    return pl.pallas_call(
        body, grid=(m // tm, n // tn),
        in_specs=[_spec((T, tm), lambda i, j: (0, i)), _spec((T, tn), lambda i, j: (0, j))],
        out_specs=_spec((tm, tn), lambda i, j: (i, j)),
        out_shape=jax.ShapeDtypeStruct((m, n), BF),
        scratch_shapes=[pltpu.VMEM((tm, tn), F32)],
        compiler_params=_params(2), name=name)(a, b)


def conv_silu_fwd(proj, cw, cb):
    def body(x_ref, w_ref, b_ref, o_ref, xpad):
        _fill_padded(xpad, x_ref)

        def chunk(r0):
            pre = _conv(_back(xpad, r0), w_ref[...], b_ref[...])
            o_ref[pl.ds(r0, Q), :] = pre * _sigmoid(pre)

        _chunks(chunk)

    c0 = PXBC // 128
    return pl.pallas_call(
        body, grid=(XBC // 128,),
        in_specs=[_spec((T, 128), lambda c: (0, c0 + c)), _spec((4, 128), lambda c: (0, c)), _spec((1, 128), lambda c: (0, c))],
        out_specs=_spec((T, 128), lambda c: (0, c)),
        out_shape=jax.ShapeDtypeStruct((T, XBC), F32), scratch_shapes=[pltpu.VMEM((T + 2 * HALO, 128), F32)],
        compiler_params=_params(), name="conv_silu_fwd")(proj, cw, cb)


def conv_silu_bwd(dx, d_b, d_c, proj, cw, cb):
    def body(dx_ref, db_ref, dc_ref, x_ref, w_ref, b_ref, o_ref, dw_ref, dbias_ref, xpad, dpad):
        tile = pl.program_id(0)
        _fill_padded(xpad, x_ref)
        dpad[0:HALO, :] = jnp.zeros((HALO, 128), F32)
        dpad[T + HALO:T + 2 * HALO, :] = jnp.zeros((HALO, 128), F32)
        dw_ref[...] = jnp.zeros_like(dw_ref)
        dbias_ref[...] = jnp.zeros_like(dbias_ref)

        def first(r0):
            back = _back(xpad, r0)
            pre = _conv(back, w_ref[...], b_ref[...])
            sg = _sigmoid(pre)
            rows = pl.ds(r0, Q)
            d = jnp.where(tile < 8, dx_ref[rows, :], jnp.where(tile < 10, db_ref[rows, :], dc_ref[rows, :]))
            dpre = d * (sg * (1.0 + pre * (1.0 - sg)))
            dpad[pl.ds(r0 + HALO, Q), :] = dpre
            dw, dbias = _conv_bwd_w(dpre, back)
            dw_ref[...] += dw
            dbias_ref[...] += dbias

        _chunks(first)

        def second(r0):
            o_ref[pl.ds(r0, Q), :] = _conv_bwd_x(_ahead(dpad, r0), w_ref[...]).astype(BF)

        _chunks(second)

    c0 = PXBC // 128
    pad = pltpu.VMEM((T + 2 * HALO, 128), F32)
    return pl.pallas_call(
        body, grid=(XBC // 128,),
        in_specs=[_spec((T, 128), lambda c: (0, jnp.minimum(c, 7))), _spec((T, 128), lambda c: (0, jnp.clip(c - 8, 0, 1))),
                  _spec((T, 128), lambda c: (0, jnp.clip(c - 10, 0, 1))), _spec((T, 128), lambda c: (0, c0 + c)),
                  _spec((4, 128), lambda c: (0, c)), _spec((1, 128), lambda c: (0, c))],
        out_specs=[_spec((T, 128), lambda c: (0, c)), _spec((4, 128), lambda c: (0, c)), _spec((1, 128), lambda c: (0, c))],
        out_shape=[jax.ShapeDtypeStruct((T, XBC), BF), jax.ShapeDtypeStruct((4, XBC), F32), jax.ShapeDtypeStruct((1, XBC), F32)],
        scratch_shapes=[pad, pad], compiler_params=_params(), name="conv_silu_bwd")(dx, d_b, d_c, proj, cw, cb)


def _ssd_chunk_common(row0, dt_ref, b_ref, c_ref, bias, a_neg):
    shape = (Q, Q)
    lane = _lanes(shape)
    sub = _rows(shape)
    live = (_rows(shape, row0) >= NPAD) & (lane < 8)
    dtr = dt_ref[:, :]
    dt = jnp.where(live, _softplus(dtr + bias), 0.0)
    d_a = dt * a_neg
    tri = (sub >= lane).astype(F32)
    cs = jnp.dot(tri, d_a, precision=lax.Precision.HIGHEST, preferred_element_type=F32)
    cs_t = cs.T
    b_f = b_ref[:, :]
    bc = b_f.astype(BF)
    cc = c_ref[:, :].astype(BF)
    cb = lax.dot_general(cc, bc, NT_DIMS, preferred_element_type=F32)
    cs_last = cs[Q - 1:Q, :]
    return dict(lane=lane, sub=sub, live=live, dtr=dtr, dt=dt, cs=cs, cs_t=cs_t, bc=bc, cc=cc, cb=cb, bc_t=b_f.T.astype(BF),
                ecs=jnp.exp(cs), dsm=jnp.exp(cs_last - cs), gam=jnp.exp(cs_last))


def _pair(lane_even, mat, j):
    return jnp.where(lane_even, mat[:, j:j + 1], mat[:, j + 1:j + 2])


def _pair_row(lane_even, mat, j):
    return jnp.where(lane_even[0:1, :], mat[:, j:j + 1], mat[:, j + 1:j + 2])


def _head_decay(cm, j):
    seg = cm["cs"][:, j:j + 1] - cm["cs_t"][j:j + 1, :]
    return jnp.exp(jnp.where(cm["sub"] >= cm["lane"], seg, -jnp.inf))


def _head_decay_t(cm, j):
    seg = cm["cs_t"][j:j + 1, :] - cm["cs"][:, j:j + 1]
    return jnp.exp(jnp.where(cm["lane"] >= cm["sub"], seg, -jnp.inf))


def ssd_fwd(xbc_act, proj, dt_bias2, a_log2, d2, norm_w):
    def body(x_ref, b_ref, c_ref, dt_ref, z_ref, bias_ref, alog_ref, d_ref, nw_ref, yn_ref, y_ref, hp_ref, h_scr):
        c = pl.program_id(1)

        @pl.when(c == 0)
        def _():
            h_scr[...] = jnp.zeros_like(h_scr)

        bias = bias_ref[0]
        a_neg = -jnp.exp(alog_ref[0])
        dsk = d_ref[0]
        cm = _ssd_chunk_common(c * Q, dt_ref, b_ref, c_ref, bias, a_neg)
        lane_even = cm["lane"] < 64
        for p in range(4):
            je, jo = 2 * p, 2 * p + 1
            xp = x_ref[:, 128 * p:128 * p + 128]
            xdt = xp * _pair(lane_even, cm["dt"], je)
            xdt_b = xdt.astype(BF)
            m_e = (cm["cb"] * _head_decay(cm, je)).astype(BF)
            m_o = (cm["cb"] * _head_decay(cm, jo)).astype(BF)
            zero = jnp.zeros_like(xdt_b)
            yd = (jnp.dot(m_e, jnp.where(lane_even, xdt_b, zero), preferred_element_type=F32)
                  + jnp.dot(m_o, jnp.where(lane_even, zero, xdt_b), preferred_element_type=F32))
            hp = h_scr[p]
            hp_ref[0, 0, p] = hp
            yo = jnp.dot(cm["cc"], hp.astype(BF), preferred_element_type=F32) * _pair(lane_even, cm["ecs"], je)
            y_ref[:, 128 * p:128 * p + 128] = yd + yo + xp * _pair_row(lane_even, dsk, je)
            st = jnp.dot(cm["bc_t"], (xdt * _pair(lane_even, cm["dsm"], je)).astype(BF), preferred_element_type=F32)
            h_scr[p] = hp * _pair_row(lane_even, cm["gam"], je) + st
        zc = z_ref[:, :]
        gated = y_ref[:, :] * (zc * _sigmoid(zc))
        yn_ref[:, :] = _rms(gated, nw_ref[...]).astype(BF)

    par = _spec((1, 1, 128), lambda g, c: (g, 0, 0))
    wide = _spec((Q, 512), lambda g, c: (c, g))
    return pl.pallas_call(
        body, grid=(2, NCH),
        in_specs=[wide, _spec((Q, 128), lambda g, c: (c, 8 + g)), _spec((Q, 128), lambda g, c: (c, 10 + g)),
                  _spec((Q, 128), lambda g, c: (c, PDT // 128 + g)), wide, par, par, par, _spec((1, 512), lambda g, c: (0, g))],
        out_specs=[wide, wide, _spec((1, 1, 4, 128, 128), lambda g, c: (g, c, 0, 0, 0))],
        out_shape=[jax.ShapeDtypeStruct((T, SSD_W), BF), jax.ShapeDtypeStruct((T, SSD_W), F32),
                   jax.ShapeDtypeStruct((2, NCH, 4, 128, 128), F32)],
        scratch_shapes=[pltpu.VMEM((4, 128, 128), F32)],
        compiler_params=_params(2), name="ssd_fwd")(xbc_act, xbc_act, xbc_act, proj, proj, dt_bias2, a_log2, d2, norm_w)


def ssd_bwd(dyn, xbc_act, proj, y_pre, h_prev, dt_bias2, a_log2, d2, norm_w):
    def body(dyn_ref, x_ref, b_ref, c_ref, dt_ref, z_ref, y_ref, hp_ref, bias_ref, alog_ref, d_ref, nw_ref,
             dz_ref, dx_ref, db_ref, dc_ref, ddt_ref, dpar_ref, dnw_ref, dh_scr, acc_scr):
        ci = pl.program_id(1)

        @pl.when(ci == 0)
        def _():
            dh_scr[...] = jnp.zeros_like(dh_scr)
            acc_scr[...] = jnp.zeros_like(acc_scr)
            dnw_ref[...] = jnp.zeros_like(dnw_ref)

        bias = bias_ref[0]
        a_neg = -jnp.exp(alog_ref[0])
        dsk = d_ref[0]
        cm = _ssd_chunk_common((NCH - 1 - ci) * Q, dt_ref, b_ref, c_ref, bias, a_neg)
        lane, sub = cm["lane"], cm["sub"]
        lane_even = lane < 64
        cc_t = c_ref[:, :].T.astype(BF)
        cb_t = lax.dot_general(cm["bc"], cm["cc"], NT_DIMS, preferred_element_type=F32)
        zc = z_ref[:, :]
        yc = y_ref[:, :]
        sg = _sigmoid(zc)
        sz = zc * sg
        dgated, dnw = _rms_bwd(dyn_ref[:, :], yc * sz, nw_ref[...])
        dnw_ref[...] += jnp.sum(dnw, axis=0, keepdims=True)
        dz_ref[:, :] = (dgated * yc * (sg * (1.0 + zc * (1.0 - sg)))).astype(BF)
        dy_all = dgated * sz
        dcb = jnp.zeros((Q, Q), F32)
        dcb_t = jnp.zeros((Q, Q), F32)
        db_acc = jnp.zeros((Q, Q), F32)
        dc_acc = jnp.zeros((Q, Q), F32)
        dcs = jnp.zeros((Q, Q), F32)
        ddt = jnp.zeros((Q, Q), F32)
        for p in range(4):
            je, jo = 2 * p, 2 * p + 1
            xp = x_ref[:, 128 * p:128 * p + 128]
            dy = dy_all[:, 128 * p:128 * p + 128]
            dt_p = _pair(lane_even, cm["dt"], je)
            xdt = xp * dt_p
            xdt_b = xdt.astype(BF)
            dy_b = dy.astype(BF)
            zero = jnp.zeros_like(dy_b)
            hp = hp_ref[0, 0, p]
            hp_b = hp.astype(BF)
            dh = dh_scr[p]
            dh_b = dh.astype(BF)
            acc_scr[p:p + 1, :] += jnp.sum(dy * xp, axis=0, keepdims=True)
            dxp = dy * _pair_row(lane_even, dsk, je)
            e_p = _pair(lane_even, cm["ecs"], je)
            g_p = jnp.dot(cm["cc"], hp_b, preferred_element_type=F32)
            dg_b = (dy * e_p).astype(BF)
            de = dy * g_p * e_p
            dc_acc = dc_acc + lax.dot_general(dg_b, hp_b, NT_DIMS, preferred_element_type=F32)
            dh_in = jnp.dot(cc_t, dg_b, preferred_element_type=F32)
            ds_p = _pair(lane_even, cm["dsm"], je)
            r_p = jnp.dot(cm["bc"], dh_b, preferred_element_type=F32)
            dxdt = r_p * ds_p
            tt = r_p * xdt * ds_p
            db_acc = db_acc + lax.dot_general((xdt * ds_p).astype(BF), dh_b, NT_DIMS, preferred_element_type=F32)
            dgam_m = jnp.sum(dh * hp, axis=0, keepdims=True)
            for j, even in ((je, True), (jo, False)):
                sel = lane_even if even else jnp.logical_not(lane_even)
                dy_j = jnp.where(sel, dy_b, zero)
                l_j = _head_decay(cm, j)
                l_jt = _head_decay_t(cm, j)
                m_j = cm["cb"] * l_j
                m_jt = cb_t * l_jt
                dm = lax.dot_general(dy_j, xdt_b, NT_DIMS, preferred_element_type=F32)
                dm_t = lax.dot_general(xdt_b, dy_j, NT_DIMS, preferred_element_type=F32)
                dxdt = dxdt + jnp.dot(m_jt.astype(BF), dy_j, preferred_element_type=F32)
                dcb = dcb + dm * l_j
                dcb_t = dcb_t + dm_t * l_jt
                t_j = jnp.where(sel, tt, 0.0)
                col = jnp.sum(dm * m_j - dm_t * m_jt + (jnp.where(sel, de, 0.0) - t_j), axis=1, keepdims=True)
                gam_j = cm["gam"][:, j:j + 1]
                last = (jnp.sum(jnp.sum(t_j, axis=0, keepdims=True), axis=1, keepdims=True)
                        + jnp.sum(jnp.where(sel[0:1, :], dgam_m, 0.0), axis=1, keepdims=True) * gam_j)
                col = col + jnp.where(sub[:, 0:1] == Q - 1, last, 0.0)
                dcs = dcs + jnp.where(lane == j, col, 0.0)
            dh_scr[p] = dh_in + dh * _pair_row(lane_even, cm["gam"], je)
            dx_ref[:, 128 * p:128 * p + 128] = dxp + dxdt * dt_p
            dd = dxdt * xp
            ddt = ddt + jnp.where(lane == je, jnp.sum(jnp.where(lane_even, dd, 0.0), axis=1, keepdims=True), 0.0)
            ddt = ddt + jnp.where(lane == jo, jnp.sum(jnp.where(lane_even, 0.0, dd), axis=1, keepdims=True), 0.0)
        dc_ref[:, :] = dc_acc + jnp.dot(dcb.astype(BF), cm["bc"], preferred_element_type=F32)
        db_ref[:, :] = db_acc + jnp.dot(dcb_t.astype(BF), cm["cc"], preferred_element_type=F32)
        tri_t = (sub <= lane).astype(F32)
        dd_a = jnp.dot(tri_t, dcs, precision=lax.Precision.HIGHEST, preferred_element_type=F32)
        ddt = ddt + dd_a * a_neg
        acc_scr[5:6, :] += jnp.sum(dd_a * cm["dt"], axis=0, keepdims=True)
        draw = jnp.where(cm["live"], ddt * _sigmoid(cm["dtr"] + bias), 0.0)
        acc_scr[4:5, :] += jnp.sum(draw, axis=0, keepdims=True)
        ddt_ref[:, :] = draw.astype(BF)

        @pl.when(ci == NCH - 1)
        def _():
            lane1 = _lanes((1, 128))
            dd = jnp.zeros((1, 128), F32)
            for p in range(4):
                row = acc_scr[p:p + 1, :]
                dd = dd + jnp.where(lane1 == 2 * p, jnp.sum(jnp.where(lane1 < 64, row, 0.0), axis=1, keepdims=True), 0.0)
                dd = dd + jnp.where(lane1 == 2 * p + 1, jnp.sum(jnp.where(lane1 < 64, 0.0, row), axis=1, keepdims=True), 0.0)
            dpar_ref[0] = jnp.concatenate([acc_scr[4:5, :], acc_scr[5:6, :] * a_neg, dd, jnp.zeros((5, 128), F32)], axis=0)

    par = _spec((1, 1, 128), lambda g, c: (g, 0, 0))
    wide = _spec((Q, 512), lambda g, c: (NCH - 1 - c, g))
    thin = _spec((Q, 128), lambda g, c: (NCH - 1 - c, g))
    return pl.pallas_call(
        body, grid=(2, NCH),
        in_specs=[wide, wide, _spec((Q, 128), lambda g, c: (NCH - 1 - c, 8 + g)), _spec((Q, 128), lambda g, c: (NCH - 1 - c, 10 + g)),
                  _spec((Q, 128), lambda g, c: (NCH - 1 - c, PDT // 128 + g)), wide, wide,
                  _spec((1, 1, 4, 128, 128), lambda g, c: (g, NCH - 1 - c, 0, 0, 0)), par, par, par, _spec((1, 512), lambda g, c: (0, g))],
        out_specs=[wide, wide, thin, thin, thin, _spec((1, 8, 128), lambda g, c: (g, 0, 0)), _spec((1, 512), lambda g, c: (0, g))],
        out_shape=[jax.ShapeDtypeStruct((T, SSD_W), BF), jax.ShapeDtypeStruct((T, SSD_W), F32), jax.ShapeDtypeStruct((T, 256), F32),
                   jax.ShapeDtypeStruct((T, 256), F32), jax.ShapeDtypeStruct((T, 256), BF), jax.ShapeDtypeStruct((2, 8, 128), F32),
                   jax.ShapeDtypeStruct((1, SSD_W), F32)],
        scratch_shapes=[pltpu.VMEM((4, 128, 128), F32), pltpu.VMEM((8, 128), F32)],
        compiler_params=_params(2), name="ssd_bwd")(dyn, xbc_act, xbc_act, xbc_act, proj, proj, y_pre, h_prev, dt_bias2, a_log2, d2, norm_w)


def _lru_gates(back, cw, cb, wa, ba, wx, bx, lam):
    xr = _conv(back, cw, cb)
    xr_b = xr.astype(BF)
    r = _sigmoid(jnp.dot(xr_b, wa, preferred_element_type=F32) + ba)
    i = _sigmoid(jnp.dot(xr_b, wx, preferred_element_type=F32) + bx)
    sp = _softplus(-lam)
    la = (-LRU_C) * r * sp
    a = jnp.exp(la)
    mult = jnp.sqrt(-jnp.tanh(la) * (a * a + 1.0))
    return xr, xr_b, r, i, sp, a, mult


def lru_gates_fwd(proj, cw, cb, wa2, ba, wx2, bx, lam):
    def body(x_ref, cw_ref, cb_ref, wa_ref, ba_ref, wx_ref, bx_ref, lam_ref, a_ref, u_ref, xpad):
        _fill_padded(xpad, x_ref)

        def chunk(r0):
            xr, _, _, i, _, a, mult = _lru_gates(_back(xpad, r0), cw_ref[...], cb_ref[...], wa_ref[0], ba_ref[...], wx_ref[0], bx_ref[...],
                                                 lam_ref[...])
            a_ref[pl.ds(r0, Q), :] = a
            u_ref[pl.ds(r0, Q), :] = jnp.where(_rows(a.shape, r0) >= NPAD, mult * (i * xr), 0.0)

        _chunks(chunk, unrolled=True)

    c0 = PXL // 128
    vec = _spec((1, 128), lambda c: (0, c))
    mat = _spec((1, 128, 128), lambda c: (c, 0, 0))
    return pl.pallas_call(
        body, grid=(8,),
        in_specs=[_spec((T, 128), lambda c: (0, c0 + c)), _spec((4, 128), lambda c: (0, c)), vec, mat, vec, mat, vec, vec],
        out_specs=[_spec((T, 128), lambda c: (0, c)), _spec((T, 128), lambda c: (0, c))],
        out_shape=[jax.ShapeDtypeStruct((T, LRU_W), F32), jax.ShapeDtypeStruct((T, LRU_W), F32)],
        scratch_shapes=[pltpu.VMEM((T + 2 * HALO, 128), F32)],
        compiler_params=_params(), name="lru_gates_fwd")(proj, cw, cb, wa2, ba, wx2, bx, lam)


def lru_scan_fwd(a, u):
    def body(a_ref, u_ref, h_ref):
        def step(i, h):
            base = pl.multiple_of(i * 8, 8)
            for k in range(8):
                h = a_ref[pl.ds(base + k, 1), :] * h + u_ref[pl.ds(base + k, 1), :]
                h_ref[pl.ds(base + k, 1), :] = h
            return h

        lax.fori_loop(0, T // 8, step, jnp.zeros((1, LRU_W), F32))

    return pl.pallas_call(body, out_shape=jax.ShapeDtypeStruct((T, LRU_W), F32), compiler_params=_params(0), name="lru_scan_fwd")(a, u)


def lru_scan_bwd(a, dh_out):
    def body(a_ref, d_ref, o_ref):
        def step(i, carry):
            base = pl.multiple_of(T - 8 - i * 8, 8)
            for k in range(7, -1, -1):
                carry = d_ref[pl.ds(base + k, 1), :] + carry
                o_ref[pl.ds(base + k, 1), :] = carry
                carry = carry * a_ref[pl.ds(base + k, 1), :]
            return carry

        lax.fori_loop(0, T // 8, step, jnp.zeros((1, LRU_W), F32))

    return pl.pallas_call(body, out_shape=jax.ShapeDtypeStruct((T, LRU_W), F32), compiler_params=_params(0), name="lru_scan_bwd")(a, dh_out)


def lru_gates_bwd(dhs, hseq, proj, cw, cb, wa2, ba, wx2, bx, lam):
    def body(dh_ref, h_ref, x_ref, cw_ref, cb_ref, wa_ref, ba_ref, wx_ref, bx_ref, lam_ref,
             dx_ref, dcw_ref, dcb_ref, dwa_ref, dba_ref, dwx_ref, dbx_ref, dlam_ref, xpad, hpad, dpad):
        _fill_padded(xpad, x_ref)
        _fill_padded(hpad, h_ref)
        dpad[0:HALO, :] = jnp.zeros((HALO, 128), F32)
        dpad[T + HALO:T + 2 * HALO, :] = jnp.zeros((HALO, 128), F32)
        for ref in (dcw_ref, dcb_ref, dwa_ref, dba_ref, dwx_ref, dbx_ref, dlam_ref):
            ref[...] = jnp.zeros_like(ref)
        lam = lam_ref[...]

        def first(r0):
            back = _back(xpad, r0)
            xr, xr_b, r, i, sp, a, mult = _lru_gates(back, cw_ref[...], cb_ref[...], wa_ref[0], ba_ref[...], wx_ref[0], bx_ref[...], lam)
            dh = dh_ref[pl.ds(r0, Q), :]
            da = dh * _back(hpad, r0)(1)
            du = jnp.where(_rows(dh.shape, r0) >= NPAD, dh, 0.0)
            dmult = du * (i * xr)
            di = du * (mult * xr)
            dxr = du * (mult * i)
            dla = da * a - dmult * (a * a) / mult
            dr = dla * ((-LRU_C) * sp)
            dlam_ref[...] += jnp.sum(dla * ((-LRU_C) * r), axis=0, keepdims=True)
            dpr = dr * r * (1.0 - r)
            dpi = di * i * (1.0 - i)
            dba_ref[...] += jnp.sum(dpr, axis=0, keepdims=True)
            dbx_ref[...] += jnp.sum(dpi, axis=0, keepdims=True)
            dpr_b = dpr.astype(BF)
            dpi_b = dpi.astype(BF)
            dxr = (dxr + lax.dot_general(dpr_b, wa_ref[0], NT_DIMS, preferred_element_type=F32)
                   + lax.dot_general(dpi_b, wx_ref[0], NT_DIMS, preferred_element_type=F32))
            dwa_ref[0] += lax.dot_general(xr_b, dpr_b, TN_DIMS, preferred_element_type=F32)
            dwx_ref[0] += lax.dot_general(xr_b, dpi_b, TN_DIMS, preferred_element_type=F32)
            dpad[pl.ds(r0 + HALO, Q), :] = dxr
            dcw, dcb = _conv_bwd_w(dxr, back)
            dcw_ref[...] += dcw
            dcb_ref[...] += dcb

        _chunks(first, unrolled=True)
        dlam_ref[...] = -dlam_ref[...] * _sigmoid(-lam)

        def second(r0):
            dx_ref[pl.ds(r0, Q), :] = _conv_bwd_x(_ahead(dpad, r0), cw_ref[...]).astype(BF)

        _chunks(second)

    c0 = PXL // 128
    vec = _spec((1, 128), lambda c: (0, c))
    mat = _spec((1, 128, 128), lambda c: (c, 0, 0))
    col = _spec((T, 128), lambda c: (0, c))
    vshape = jax.ShapeDtypeStruct((1, LRU_W), F32)
    mshape = jax.ShapeDtypeStruct((8, 128, 128), F32)
    pad = pltpu.VMEM((T + 2 * HALO, 128), F32)
    return pl.pallas_call(
        body, grid=(8,),
        in_specs=[col, col, _spec((T, 128), lambda c: (0, c0 + c)), _spec((4, 128), lambda c: (0, c)), vec, mat, vec, mat, vec, vec],
        out_specs=[col, _spec((4, 128), lambda c: (0, c)), vec, mat, vec, mat, vec, vec],
        out_shape=[jax.ShapeDtypeStruct((T, LRU_W), BF), jax.ShapeDtypeStruct((4, LRU_W), F32), vshape, mshape, vshape, mshape, vshape, vshape],
        scratch_shapes=[pad, pad, pad], compiler_params=_params(), name="lru_gates_bwd")(dhs, hseq, proj, cw, cb, wa2, ba, wx2, bx, lam)


def gate_up(h1, wn, w_gate, w_up):
    def body(h_ref, wn_ref, wg_ref, wu_ref, gt_ref, up_ref, act_ref, u_ref):
        for r in (0, HALF):
            u_ref[r:r + HALF, :] = _rms(h_ref[r:r + HALF, :], wn_ref[...]).astype(BF)

        def tile(c0):
            cols = pl.ds(c0, 256)
            gt = lax.dot_general(u_ref[...], wg_ref[cols, :], NT_DIMS, preferred_element_type=F32)
            up = lax.dot_general(u_ref[...], wu_ref[cols, :], NT_DIMS, preferred_element_type=F32)
            gt_ref[:, cols] = gt.astype(BF)
            up_ref[:, cols] = up.astype(BF)
            act_ref[:, cols] = (gt * _sigmoid(gt) * up).astype(BF)

        _col_tiles(D_FF, 256, tile)

    big = jax.ShapeDtypeStruct((T, D_FF), BF)
    return pl.pallas_call(
        body, grid=(T // RC,), in_specs=[_rows_spec(D), _vec(D), _whole((D_FF, D)), _whole((D_FF, D))],
        out_specs=[_rows_spec(D_FF), _rows_spec(D_FF), _rows_spec(D_FF), _rows_spec(D)],
        out_shape=[big, big, big, jax.ShapeDtypeStruct((T, D), BF)],
        compiler_params=_params(), name="gate_up")(h1, wn, w_gate, w_up)


def down_loss(act, w_down, h1, target, wf):
    first = NPAD + N_META

    def body(a_ref, w_ref, r_ref, t_hbm, wf_ref, d_ref, db_ref, l_ref, dw_ref, h_scr, t_ref, t_sem):
        i = pl.program_id(0)
        _zero_at_first(l_ref, dw_ref)
        head = pltpu.make_async_copy(t_hbm.at[pl.ds(0, RC - first)], t_ref.at[pl.ds(first, RC - first)], t_sem)
        rest = pltpu.make_async_copy(t_hbm.at[pl.ds(pl.multiple_of(jnp.maximum(i * RC - first, 0), 32), RC)], t_ref, t_sem)

        @pl.when(i == 0)
        def _():
            t_ref[0:first, :] = jnp.zeros((first, D), F32)
            head.start()

        @pl.when(i > 0)
        def _():
            rest.start()

        def tile(c0):
            cols = pl.ds(c0, 512)
            h_scr[:, cols] = r_ref[:, cols] + jnp.dot(a_ref[...], w_ref[:, cols], preferred_element_type=F32)

        _col_tiles(D, 512, tile)

        @pl.when(i == 0)
        def _():
            head.wait()

        @pl.when(i > 0)
        def _():
            rest.wait()

        for r in (0, HALF):
            h = h_scr[r:r + HALF, :]
            live = _rows((HALF, D), i * RC + r) >= first
            err = jnp.where(live, _rms(h, wf_ref[...]) - t_ref[r:r + HALF, :], 0.0)
            l_ref[...] += 0.5 * jnp.sum(jnp.sum(err * err, axis=1, keepdims=True) * (1.0 / D), axis=0, keepdims=True)
            dh, dw = _rms_bwd(err * (1.0 / D), h, wf_ref[...])
            dw_ref[...] += jnp.sum(dw, axis=0, keepdims=True)
            d_ref[r:r + HALF, :] = dh
            db_ref[r:r + HALF, :] = dh.astype(BF)

    return pl.pallas_call(
        body, grid=(T // RC,),
        in_specs=[_rows_spec(D_FF), _whole((D_FF, D)), _rows_spec(D), pl.BlockSpec(memory_space=pl.ANY), _vec(D)],
        out_specs=[_rows_spec(D), _rows_spec(D), _spec((1, 128), lambda i: (0, 0)), _vec(D)],
        out_shape=[jax.ShapeDtypeStruct((T, D), F32), jax.ShapeDtypeStruct((T, D), BF), jax.ShapeDtypeStruct((1, 128), F32),
                   jax.ShapeDtypeStruct((1, D), F32)],
        scratch_shapes=[pltpu.VMEM((RC, D), F32), pltpu.VMEM((RC, D), F32), pltpu.SemaphoreType.DMA],
        compiler_params=_params(), name="down_loss")(act, w_down, h1, target, wf)


def swiglu_bwd(dh2_b, w_down, gt, up):
    def body(d_ref, w_ref, gt_ref, up_ref, dg_ref, du_ref):
        def tile(c0):
            cols = pl.ds(c0, 256)
            dact = lax.dot_general(d_ref[...], w_ref[cols, :], NT_DIMS, preferred_element_type=F32)
            gt_ = gt_ref[:, cols].astype(F32)
            up_ = up_ref[:, cols].astype(F32)
            sg = _sigmoid(gt_)
            dg_ref[:, cols] = (dact * up_ * (sg * (1.0 + gt_ * (1.0 - sg)))).astype(BF)
            du_ref[:, cols] = (dact * (gt_ * sg)).astype(BF)

        _col_tiles(D_FF, 256, tile)

    big = jax.ShapeDtypeStruct((T, D_FF), BF)
    return pl.pallas_call(
        body, grid=(T // RC,), in_specs=[_rows_spec(D), _whole((D_FF, D)), _rows_spec(D_FF), _rows_spec(D_FF)],
        out_specs=[_rows_spec(D_FF), _rows_spec(D_FF)], out_shape=[big, big], compiler_params=_params(), name="swiglu_bwd")(dh2_b, w_down, gt, up)


def gate_up_bwd(dgt, dup, w_gate, w_up, h1, wn, dh2):
    def body(dg_ref, du_ref, wg_ref, wu_ref, h_ref, wn_ref, r_ref, d_ref, db_ref, dw_ref, du_scr):
        _zero_at_first(dw_ref)

        du_scr[...] = jnp.zeros_like(du_scr)

        def tile(c0):
            k = pl.ds(c0, 256)
            du_scr[...] += (jnp.dot(dg_ref[:, k], wg_ref[k, :], preferred_element_type=F32)
                            + jnp.dot(du_ref[:, k], wu_ref[k, :], preferred_element_type=F32))

        _col_tiles(D_FF, 256, tile)
        for r in (0, HALF):
            dh, dw = _rms_bwd(du_scr[r:r + HALF, :], h_ref[r:r + HALF, :], wn_ref[...])
            dw_ref[...] += jnp.sum(dw, axis=0, keepdims=True)
            dh = dh + r_ref[r:r + HALF, :]
            d_ref[r:r + HALF, :] = dh
            db_ref[r:r + HALF, :] = dh.astype(BF)

    return pl.pallas_call(
        body, grid=(T // RC,),
        in_specs=[_rows_spec(D_FF), _rows_spec(D_FF), _whole((D_FF, D)), _whole((D_FF, D)), _rows_spec(D), _vec(D), _rows_spec(D)],
        out_specs=[_rows_spec(D), _rows_spec(D), _vec(D)],
        out_shape=[jax.ShapeDtypeStruct((T, D), F32), jax.ShapeDtypeStruct((T, D), BF), jax.ShapeDtypeStruct((1, D), F32)],
        scratch_shapes=[pltpu.VMEM((RC, D), F32)],
        compiler_params=_params(), name="gate_up_bwd")(dgt, dup, w_gate, w_up, h1, wn, dh2)


def _adamw(w, g, m, v):
    m = ADAM_B1 * m + (1.0 - ADAM_B1) * g
    v = ADAM_B2 * v + (1.0 - ADAM_B2) * (g * g)
    m_hat = m / (1.0 - ADAM_B1 ** ADAM_STEP)
    v_hat = v / (1.0 - ADAM_B2 ** ADAM_STEP)
    delta = -ADAM_LR * (m_hat / (jnp.sqrt(v_hat) + ADAM_EPS) + ADAM_WD * w)
    return delta, m, v


def adamw_shard(name, recv, w, m, v, tr, tc):
    r, c = w.shape

    def body(p_ref, w_ref, m_ref, v_ref, g_ref, d_ref, mo_ref, vo_ref):
        g = p_ref[0].astype(F32)
        for s in range(1, 8):
            g = g + p_ref[s].astype(F32)
        g_ref[...] = g
        d_ref[...], mo_ref[...], vo_ref[...] = _adamw(w_ref[...], g, m_ref[...], v_ref[...])

    tile = _spec((tr, tc), lambda i, j: (i, j))
    shape = jax.ShapeDtypeStruct((r, c), F32)
    return pl.pallas_call(
        body, grid=(r // tr, c // tc), in_specs=[_spec((8, tr, tc), lambda i, j: (0, i, j)), tile, tile, tile],
        out_specs=[tile] * 4, out_shape=[shape] * 4, compiler_params=_params(2), name=name)(recv, w, m, v)


def sum_slabs(recv):
    def body(p_ref, o_ref):
        g = p_ref[0]
        for s in range(1, 8):
            g = g + p_ref[s]
        o_ref[...] = g

    return pl.pallas_call(body, out_shape=jax.ShapeDtypeStruct(recv.shape[1:], F32), compiler_params=_params(0), name="sum_slabs")(recv)


SIMPLE = [("norm1_w", 1024), ("ssd_conv_b", 1536), ("ssd_dt_bias", 16), ("ssd_a_log", 16), ("ssd_d", 16), ("ssd_norm_w", 1024),
          ("lru_conv_b", 1024), ("lru_ba", 1024), ("lru_bx", 1024), ("lru_lambda", 1024), ("lru_norm_w", 1024), ("norm2_w", 1024),
          ("final_norm_w", 1024)]
SPECIAL = ["lru_wa", "lru_wx", "meta_tokens", "ssd_conv_w", "lru_conv_w"]
SM_ROWS = 176
SM_WA, SM_WX, SM_META, SM_SCW, SM_LCW, SM_LOSS = 14, 78, 142, 158, 166, 170


def _simple_rows():
    rows, r = {}, 0
    for name, n in SIMPLE:
        rows[name] = r
        r += -(-n // 1024)
    return rows


def adamw_small(sm, special_g, ws, ms, vs):
    rows = _simple_rows()
    ns, nx = len(SIMPLE), len(SPECIAL)

    def body(*refs):
        sm_ref = refs[0]
        gx = refs[1:1 + nx]
        wr = refs[1 + nx:1 + nx + ns + nx]
        mr = refs[1 + nx + ns + nx:1 + nx + 2 * (ns + nx)]
        vr = refs[1 + nx + 2 * (ns + nx):1 + nx + 3 * (ns + nx)]
        outs = refs[1 + nx + 3 * (ns + nx):]
        o = 0
        for k, (name, n) in enumerate(SIMPLE):
            r0 = rows[name]
            for c0 in range(0, n, 1024):
                wd = min(1024, n - c0)
                g = sm_ref[r0 + c0 // 1024:r0 + c0 // 1024 + 1, 0:wd]
                sl = (slice(None), slice(c0, c0 + wd))
                d, m2, v2 = _adamw(wr[k][sl], g, mr[k][sl], vr[k][sl])
                outs[o][sl] = g
                outs[o + 1][sl] = d
                outs[o + 2][sl] = m2
                outs[o + 3][sl] = v2
            o += 4
        for k in range(nx):
            d, m2, v2 = _adamw(wr[ns + k][...], gx[k][...], mr[ns + k][...], vr[ns + k][...])
            outs[o][...] = d
            outs[o + 1][...] = m2
            outs[o + 2][...] = v2
            o += 3

    out_shape = []
    for k in range(ns):
        out_shape += [jax.ShapeDtypeStruct(ws[k].shape, F32)] * 4
    for k in range(nx):
        out_shape += [jax.ShapeDtypeStruct(ws[ns + k].shape, F32)] * 3
    return pl.pallas_call(body, out_shape=out_shape, compiler_params=_params(0), name="adamw_small")(sm, *special_g, *ws, *ms, *vs)


def _place():
    return lax.axis_index("x"), lax.axis_index("y"), lax.axis_index("c")


def _index(px, py, pc):
    return 4 * px + 2 * py + pc


def all_gather(name, shards):
    n = len(shards)
    hbm = pl.BlockSpec(memory_space=pl.ANY)

    def body(*refs):
        ins, outs = refs[:n], refs[n:2 * n]
        send_sems, recv_sems, local_sems = refs[2 * n:]
        x, y, c = _place()
        me, sibling = (x, y, c), (x, y, 1 - c)
        chips = [(1 - x, y), (x, 1 - y), (1 - x, 1 - y)]

        def copy(i, k, block, to, src=None):
            dst = outs[i].at[_index(*block)]
            return pltpu.make_async_remote_copy(src_ref=dst if src is None else src, dst_ref=dst, send_sem=send_sems.at[7 * i + k],
                                                recv_sem=recv_sems.at[7 * i + k], device_id=to, device_id_type=MESH)

        mine = [pltpu.make_async_copy(ins[i], outs[i].at[_index(*me)], local_sems.at[i]) for i in range(n)]
        for cp in mine:
            cp.start()
        first = []
        for i in range(n):
            first += [copy(i, 1 + j, me, (*chip, c), src=ins[i]) for j, chip in enumerate(chips)]
            first.append(copy(i, 0, me, sibling, src=ins[i]))
        for cp in first:
            cp.start()
        passed = []
        for i in range(n):
            for j, chip in enumerate(chips):
                copy(i, 1 + j, (*chip, c), me).wait_recv()
                cp = copy(i, 4 + j, (*chip, c), sibling)
                cp.start()
                passed.append(cp)
        for i in range(n):
            copy(i, 0, sibling, me).wait_recv()
            for j, chip in enumerate(chips):
                copy(i, 4 + j, (*chip, 1 - c), me).wait_recv()
        for cp in first + passed:
            cp.wait_send()
        for cp in mine:
            cp.wait()

    return pl.pallas_call(
        body, in_specs=[hbm] * n, out_specs=[hbm] * n,
        out_shape=[jax.ShapeDtypeStruct((8,) + s.shape, s.dtype) for s in shards],
        scratch_shapes=[pltpu.SemaphoreType.DMA((7 * n,)), pltpu.SemaphoreType.DMA((7 * n,)), pltpu.SemaphoreType.DMA((n,))],
        name=name)(*shards)


HBM_SPEC = pl.BlockSpec(memory_space=pltpu.HBM)
SEM_SPEC = pl.BlockSpec(memory_space=pltpu.SEMAPHORE)
EFFECT = pltpu.SideEffectType.DATAFLOW_SIDE_EFFECTING


def _peers(x, y, c):
    return [((1 - x) if k & 4 else x, (1 - y) if k & 2 else y, (1 - c) if k & 1 else c) for k in range(1, 8)]


def _pieces(rows):
    for n in (4, 2):
        if rows % (16 * n) == 0:
            return [(r * (rows // n), rows // n) for r in range(n)]
    return [(0, rows)]


def _peer_copies(src, land, send_sems, recv_sems, k, peer, mine, slab_src):
    block = src.at[_index(*peer)] if slab_src else src
    return [pltpu.make_async_remote_copy(src_ref=block.at[pl.ds(r0, nr)], dst_ref=land.at[mine, pl.ds(r0, nr)], send_sem=send_sems.at[k],
                                         recv_sem=recv_sems.at[k], device_id=peer, device_id_type=MESH)
            for r0, nr in _pieces(block.shape[0])]


def copies_start(name, srcs, slab_src, after):
    n = len(srcs)
    zones = [jax.ShapeDtypeStruct(s.shape if slab_src else (8,) + s.shape, s.dtype) for s in srcs]

    def body(*refs):
        ins, lands = refs[:n], refs[n:2 * n]
        sends, recvs = refs[2 * n + 1:3 * n + 1], refs[3 * n + 1:4 * n + 1]
        token = refs[-1]
        x, y, c = _place()
        mine = _index(x, y, c)
        for i in range(n):
            per_peer = [_peer_copies(ins[i], lands[i], sends[i], recvs[i], k, peer, mine, slab_src) for k, peer in enumerate(_peers(x, y, c))]
            for piece in zip(*per_peer):
                for cp in piece:
                    cp.start()
        token[...] = jnp.zeros_like(token)

    sem = pltpu.SemaphoreType.DMA((7,))
    res = pl.pallas_call(
        body, name=name,
        out_shape=([sem] * (2 * n) + [pltpu.HBM(s.shape, s.dtype) for s in srcs] + [pltpu.HBM(z.shape, z.dtype) for z in zones]
                   + [jax.ShapeDtypeStruct((8, 128), F32)]),
        in_specs=[HBM_SPEC] * (2 * n) + [pl.BlockSpec(memory_space=pl.ANY)],
        out_specs=[SEM_SPEC] * (2 * n) + [HBM_SPEC] * (2 * n) + [pl.BlockSpec(memory_space=pltpu.VMEM)],
        input_output_aliases={i: 2 * n + i for i in range(2 * n)},
        compiler_params=pltpu.CompilerParams(has_side_effects=EFFECT),
    )(*[pltpu.with_memory_space_constraint(s, pltpu.HBM) for s in srcs],
      *[pltpu.with_memory_space_constraint(lax.empty(z.shape, z.dtype), pltpu.HBM) for z in zones], after)
    return [(res[i], res[n + i], res[2 * n + i], res[3 * n + i]) for i in range(n)], res[-1][0:1, 0:1]


def copies_wait(name, started, slab_src, after):
    n = len(started)

    def body(*refs):
        ins, lands = refs[:n], refs[n:2 * n]
        sends, recvs = refs[2 * n:3 * n], refs[3 * n:4 * n]
        x, y, c = _place()
        mine = _index(x, y, c)
        for i in range(n):
            for k, peer in enumerate(_peers(x, y, c)):
                arrival = pltpu.make_async_remote_copy(src_ref=ins[i].at[mine] if slab_src else ins[i], dst_ref=lands[i].at[_index(*peer)],
                                                       send_sem=sends[i].at[k], recv_sem=recvs[i].at[k], device_id=peer, device_id_type=MESH)
                arrival.wait_send()
                arrival.wait_recv()

    srcs = [s[2] for s in started]
    lands = [s[3] for s in started]
    afters = list(after) if isinstance(after, (list, tuple)) else [after]
    res = pl.pallas_call(
        body, name=name,
        out_shape=[pltpu.HBM(s.shape, s.dtype) for s in srcs] + [pltpu.HBM(z.shape, z.dtype) for z in lands],
        in_specs=[HBM_SPEC] * (2 * n) + [SEM_SPEC] * (2 * n) + [pl.BlockSpec(memory_space=pl.ANY)] * len(afters),
        out_specs=[HBM_SPEC] * (2 * n),
        input_output_aliases={i: i for i in range(2 * n)},
        compiler_params=pltpu.CompilerParams(has_side_effects=EFFECT),
    )(*srcs, *lands, *[s[0] for s in started], *[s[1] for s in started], *afters)
    me = _index(*_place())
    own = [lax.dynamic_index_in_dim(s, me, 0, keepdims=True) if slab_src else s[None] for s in res[:n]]
    return [lax.dynamic_update_slice_in_dim(z, o, me, 0) for z, o in zip(res[n:], own)]


WEIGHTS = ["meta_tokens", "norm1_w", "w_in", "ssd_conv_w", "ssd_conv_b", "ssd_dt_bias", "ssd_a_log", "ssd_d", "ssd_norm_w", "lru_conv_w",
           "lru_conv_b", "lru_wa", "lru_ba", "lru_wx", "lru_bx", "lru_lambda", "lru_norm_w", "w_out", "norm2_w", "w_gate", "w_up", "w_down",
           "final_norm_w"]
BIG = ["w_in", "w_out", "w_gate", "w_up", "w_down"]
COLUMN_SHARDED = ["w_in", "w_gate", "w_up"]
BIG_TILE = {"w_in": (578, 256), "w_out": (128, 1024), "w_gate": (176, 1024), "w_up": (176, 1024), "w_down": (176, 1024)}


def _pair_blocks(w):
    w = w.reshape(8, 2, 64, 64)
    z = jnp.zeros((8, 64, 64), w.dtype)
    return jnp.concatenate([jnp.concatenate([w[:, 0], z], axis=2), jnp.concatenate([z, w[:, 1]], axis=2)], axis=1)


def _unpair_blocks(w2):
    return jnp.stack([w2[:, :64, :64], w2[:, 64:, 64:]], axis=1).reshape(16, 64, 64)


def _per_group(v):
    return jnp.pad(v.reshape(2, 1, 8), ((0, 0), (0, 0), (0, 120)))


def _pad_cols(v, n):
    return jnp.pad(v, ((0, 0), (0, n - v.shape[1])))


def local_step(x, target, meta, ssd_cw, lru_cw, w_in, fetch, send, p):
    z120 = jnp.zeros((120, D), BF)
    w_dt = jnp.concatenate([w_in[2560:2568], z120, w_in[2568:2576], z120], axis=0)
    bias2, alog2, d2 = _per_group(p["ssd_dt_bias"]), _per_group(p["ssd_a_log"]), _per_group(p["ssd_d"])
    wa2 = _pair_blocks(p["lru_wa"]).astype(BF)
    wx2 = _pair_blocks(p["lru_wx"]).astype(BF)
    lru = (lru_cw, p["lru_conv_b"], wa2, p["lru_ba"], wx2, p["lru_bx"], p["lru_lambda"])

    h0 = jnp.concatenate([jnp.zeros((NPAD, D), F32), meta, x], axis=0)
    proj, u1 = in_proj(h0, p["norm1_w"], w_in, w_dt)
    xbc_act = conv_silu_fwd(proj, ssd_cw, p["ssd_conv_b"])
    yn_ssd, y_pre, h_prev = ssd_fwd(xbc_act, proj, bias2, alog2, d2, p["ssd_norm_w"])
    a, u = lru_gates_fwd(proj, *lru)
    hseq = lru_scan_fwd(a, u)
    (w_out,) = fetch(["w_out"], hseq)
    h1, cat = out_proj(yn_ssd, proj, hseq, p["lru_norm_w"], w_out, h0)
    w_gate, w_up = fetch(["w_gate", "w_up"], h1)
    gt, up, act, u2 = gate_up(h1, p["norm2_w"], w_gate, w_up)
    (w_down,) = fetch(["w_down"], act)
    dh2, dh2_b, loss, d_fnw = down_loss(act, w_down, h1, target, p["final_norm_w"])

    dgt, dup = swiglu_bwd(dh2_b, w_down, gt, up)
    g_down = matmul_tn("dw_down", act, dh2_b, 1408, 512)
    g_gate = matmul_tn("dw_gate", dgt, u2, 1408, 512)
    g_up = matmul_tn("dw_up", dup, u2, 1408, 512)
    sent = send({"w_down": g_down, "w_gate": g_gate, "w_up": g_up})
    dh1, dh1_b, d_n2 = gate_up_bwd(dgt, dup, w_gate, w_up, h1, p["norm2_w"] + sent, dh2)
    sent = send({"w_out": matmul_tn("dw_out", cat, dh1_b, 512, 1024)})
    dyn, dh_out, dg_b, d_lnw = out_proj_bwd(dh1_b, w_out, proj, hseq, p["lru_norm_w"] + sent)

    dhs = lru_scan_bwd(a, dh_out)
    dxl_b, d_lcw, d_lcb, dwa2, d_ba, dwx2, d_bx, d_lam = lru_gates_bwd(dhs, hseq, proj, *lru)
    dz_b, dx, d_b, d_c, ddt_b, dpar, d_snw = ssd_bwd(dyn, xbc_act, proj, y_pre, h_prev, bias2, alog2, d2, p["ssd_norm_w"])
    dxbc_b, d_scw, d_scb = conv_silu_bwd(dx, d_b, d_c, proj, ssd_cw, p["ssd_conv_b"])
    g_dt = matmul_tn("dw_in_dt", ddt_b, u1, 256, 512)
    g_in = jnp.concatenate([matmul_tn("dw_in_z", dz_b, u1, 512, 1024), matmul_tn("dw_in_xbc", dxbc_b, u1, 512, 1024), g_dt[0:8], g_dt[128:136],
                            matmul_tn("dw_in_g", dg_b, u1, 512, 1024), matmul_tn("dw_in_xl", dxl_b, u1, 512, 1024)], axis=0)
    sent = send({"w_in": g_in})
    dh0, d_n1 = in_proj_bwd(dz_b, dg_b, dxl_b, dxbc_b, ddt_b, w_in, w_dt, h0, p["norm1_w"] + sent, dh1)
    small = {"norm1_w": d_n1, "ssd_conv_b": d_scb, "ssd_dt_bias": dpar[:, 0, :8].reshape(1, 16), "ssd_a_log": dpar[:, 1, :8].reshape(1, 16),
             "ssd_d": dpar[:, 2, :8].reshape(1, 16), "ssd_norm_w": d_snw, "lru_conv_b": d_lcb, "lru_ba": d_ba, "lru_bx": d_bx,
             "lru_lambda": d_lam, "lru_norm_w": d_lnw, "norm2_w": d_n2, "final_norm_w": d_fnw,
             "lru_wa": _unpair_blocks(dwa2), "lru_wx": _unpair_blocks(dwx2), "meta_tokens": dh0[NPAD:NPAD + N_META],
             "ssd_conv_w": d_scw, "lru_conv_w": d_lcw}
    return loss, dh0[NPAD + N_META:], small


def _pack_small(small, loss):
    rows = [_pad_cols(small[name], -(-n // 1024) * 1024).reshape(-1, 1024) for name, n in SIMPLE]
    rows += [small["lru_wa"].reshape(64, 1024), small["lru_wx"].reshape(64, 1024), small["meta_tokens"],
             _pad_cols(small["ssd_conv_w"], 2048).reshape(8, 1024), small["lru_conv_w"], _pad_cols(loss[:, 0:1], 1024)]
    sm = jnp.concatenate(rows, axis=0)
    return jnp.pad(sm, ((0, SM_ROWS - sm.shape[0]), (0, 0)))


def _slabs(g):
    return g.reshape(8, g.shape[0] // 8, g.shape[1])


def _unslab(g):
    return g.reshape(8 * g.shape[1], g.shape[2])


def kernel(x, meta_tokens, norm1_w, w_in, ssd_conv_w, ssd_conv_b, ssd_dt_bias, ssd_a_log, ssd_d, ssd_norm_w, lru_conv_w, lru_conv_b, lru_wa, lru_ba, lru_wx, lru_bx, lru_lambda, lru_norm_w, w_out, norm2_w, w_gate, w_up, w_down, final_norm_w, loss_target, m_meta_tokens, m_norm1_w, m_w_in, m_ssd_conv_w, m_ssd_conv_b, m_ssd_dt_bias, m_ssd_a_log, m_ssd_d, m_ssd_norm_w, m_lru_conv_w, m_lru_conv_b, m_lru_wa, m_lru_ba, m_lru_wx, m_lru_bx, m_lru_lambda, m_lru_norm_w, m_w_out, m_norm2_w, m_w_gate, m_w_up, m_w_down, m_final_norm_w, v_meta_tokens, v_norm1_w, v_w_in, v_ssd_conv_w, v_ssd_conv_b, v_ssd_dt_bias, v_ssd_a_log, v_ssd_d, v_ssd_norm_w, v_lru_conv_w, v_lru_conv_b, v_lru_wa, v_lru_ba, v_lru_wx, v_lru_bx, v_lru_lambda, v_lru_norm_w, v_w_out, v_norm2_w, v_w_gate, v_w_up, v_w_down, v_final_norm_w):
    w = dict(meta_tokens=meta_tokens, norm1_w=norm1_w, w_in=w_in[0], ssd_conv_w=ssd_conv_w[0], ssd_conv_b=ssd_conv_b, ssd_dt_bias=ssd_dt_bias,
             ssd_a_log=ssd_a_log, ssd_d=ssd_d, ssd_norm_w=ssd_norm_w, lru_conv_w=lru_conv_w[0], lru_conv_b=lru_conv_b, lru_wa=lru_wa[0],
             lru_ba=lru_ba, lru_wx=lru_wx[0], lru_bx=lru_bx, lru_lambda=lru_lambda, lru_norm_w=lru_norm_w, w_out=w_out[0], norm2_w=norm2_w,
             w_gate=w_gate[0], w_up=w_up[0], w_down=w_down[0], final_norm_w=final_norm_w.reshape(1, D))
    m = dict(meta_tokens=m_meta_tokens, norm1_w=m_norm1_w, w_in=m_w_in[0], ssd_conv_w=m_ssd_conv_w[0], ssd_conv_b=m_ssd_conv_b,
             ssd_dt_bias=m_ssd_dt_bias, ssd_a_log=m_ssd_a_log, ssd_d=m_ssd_d, ssd_norm_w=m_ssd_norm_w, lru_conv_w=m_lru_conv_w[0],
             lru_conv_b=m_lru_conv_b, lru_wa=m_lru_wa[0], lru_ba=m_lru_ba, lru_wx=m_lru_wx[0], lru_bx=m_lru_bx, lru_lambda=m_lru_lambda,
             lru_norm_w=m_lru_norm_w, w_out=m_w_out[0], norm2_w=m_norm2_w, w_gate=m_w_gate[0], w_up=m_w_up[0], w_down=m_w_down[0],
             final_norm_w=m_final_norm_w.reshape(1, D))
    v = dict(meta_tokens=v_meta_tokens, norm1_w=v_norm1_w, w_in=v_w_in[0], ssd_conv_w=v_ssd_conv_w[0], ssd_conv_b=v_ssd_conv_b,
             ssd_dt_bias=v_ssd_dt_bias, ssd_a_log=v_ssd_a_log, ssd_d=v_ssd_d, ssd_norm_w=v_ssd_norm_w, lru_conv_w=v_lru_conv_w[0],
             lru_conv_b=v_lru_conv_b, lru_wa=v_lru_wa[0], lru_ba=v_lru_ba, lru_wx=v_lru_wx[0], lru_bx=v_lru_bx, lru_lambda=v_lru_lambda,
             lru_norm_w=v_lru_norm_w, w_out=v_w_out[0], norm2_w=v_norm2_w, w_gate=v_w_gate[0], w_up=v_w_up[0], w_down=v_w_down[0],
             final_norm_w=v_final_norm_w.reshape(1, D))
    shapes = dict(meta_tokens=meta_tokens.shape, norm1_w=norm1_w.shape, w_in=w_in.shape, ssd_conv_w=ssd_conv_w.shape,
                  ssd_conv_b=ssd_conv_b.shape, ssd_dt_bias=ssd_dt_bias.shape, ssd_a_log=ssd_a_log.shape, ssd_d=ssd_d.shape,
                  ssd_norm_w=ssd_norm_w.shape, lru_conv_w=lru_conv_w.shape, lru_conv_b=lru_conv_b.shape, lru_wa=lru_wa.shape,
                  lru_ba=lru_ba.shape, lru_wx=lru_wx.shape, lru_bx=lru_bx.shape, lru_lambda=lru_lambda.shape, lru_norm_w=lru_norm_w.shape,
                  w_out=w_out.shape, norm2_w=norm2_w.shape, w_gate=w_gate.shape, w_up=w_up.shape, w_down=w_down.shape,
                  final_norm_w=final_norm_w.shape)
    me = _index(*_place())
    for n in COLUMN_SHARDED:
        w[n], m[n], v[n] = w[n].T, m[n].T, v[n].T

    small_shard = jnp.concatenate([w["meta_tokens"], _pad_cols(w["ssd_conv_w"], 256).reshape(8, 128), w["lru_conv_w"],
                                   jnp.zeros((4, 128), F32)], axis=0)
    g_in, gs = all_gather("gather_w_in", [w["w_in"].astype(BF), small_shard])
    later = ["w_out", "w_gate", "w_up", "w_down"]
    started, behind = copies_start("gather_rest_start", [w[n].astype(BF) for n in later], False, gs)
    started = dict(zip(later, started))
    meta_full = gs[:, 0:16].transpose(1, 0, 2).reshape(N_META, D)
    ssd_cw = gs[:, 16:24].reshape(8, 4, 256)[:, :, :192].transpose(1, 0, 2).reshape(4, XBC)
    lru_cw = gs[:, 24:28].transpose(1, 0, 2).reshape(4, LRU_W)

    def fetch(names, after):
        got = copies_wait("gather_" + names[0] + "_wait", [started[n] for n in names], False, after)
        return [_unslab(g) for g in got]

    in_flight = {}

    def send(grads):
        names = list(grads)
        st, token = copies_start("grads_" + names[0] + "_start", [grads[n] if n == "small" else _slabs(grads[n]) for n in names], True,
                                 grads[names[0]])
        in_flight.update(zip(names, st))
        return token

    loss, grad_x, small = local_step(x[0], loss_target[0], meta_full, ssd_cw, lru_cw, _unslab(g_in), fetch, send,
                                     {**w, "norm1_w": w["norm1_w"] + behind})
    send({"small": _pack_small(small, loss).reshape(8, SM_ROWS // 8, 1024)})

    out = {}
    early = ["w_down", "w_gate", "w_up", "w_out"]
    recv = dict(zip(early, copies_wait("grads_early_wait", [in_flight[n] for n in early], True, in_flight["small"][2])))
    for n in early:
        out[n] = adamw_shard("adamw_" + n, recv[n], w[n], m[n], v[n], *BIG_TILE[n])
    recv_in, recv_small = copies_wait("grads_late_wait", [in_flight["w_in"], in_flight["small"]], True, [out[n][0] for n in early])
    out["w_in"] = adamw_shard("adamw_w_in", recv_in, w["w_in"], m["w_in"], v["w_in"], *BIG_TILE["w_in"])
    for n in COLUMN_SHARDED:
        out[n] = [o.T for o in out[n]]
    sm = all_gather("gather_small_grads", [sum_slabs(recv_small)])[0].reshape(SM_ROWS, 1024)
    special_g = [sm[SM_WA:SM_WA + 64].reshape(16, 64, 64), sm[SM_WX:SM_WX + 64].reshape(16, 64, 64),
                 lax.dynamic_slice(sm[SM_META:SM_META + 16], (0, 128 * me), (16, 128)),
                 lax.dynamic_slice(sm[SM_SCW:SM_SCW + 8].reshape(4, 2048), (0, 192 * me), (4, 192)),
                 lax.dynamic_slice(sm[SM_LCW:SM_LCW + 4], (0, 128 * me), (4, 128))]
    names = [n for n, _ in SIMPLE] + SPECIAL
    res = adamw_small(sm, special_g, [w[n] for n in names], [m[n] for n in names], [v[n] for n in names])
    for k, (n, _) in enumerate(SIMPLE):
        out[n] = res[4 * k:4 * k + 4]
    for k, n in enumerate(SPECIAL):
        o = 4 * len(SIMPLE) + 3 * k
        out[n] = [special_g[k]] + list(res[o:o + 3])
    loss_total = sm[SM_LOSS, 0]
    flat = [loss_total, grad_x[None]]
    for k in range(4):
        flat += [out[n][k].reshape(shapes[n]) for n in WEIGHTS]
    return tuple(flat)
```

```python
import math

import jax
import jax.numpy as jnp
from jax import lax
from jax.experimental import pallas as pl
from jax.experimental.pallas import tpu as pltpu

F32 = jnp.float32
BF = jnp.bfloat16

D = 1024
SEQ = 2048
N_META = 16
Q = 128
NPAD = 112
T = NPAD + N_META + SEQ
NCH = T // Q
RC = 544
D_FF = 2816
SSD_W = 1024
LRU_W = 1024
XBC = 1536
IN_COLS = 4624
PZ, PG, PXL, PXBC, PDT = 0, 1024, 2048, 3072, 4608
NP_IN = 4864
EPS = 1e-6
LRU_C = 8.0
VMEM_LIMIT = 56 * 1024 * 1024

ADAM_LR, ADAM_B1, ADAM_B2, ADAM_EPS, ADAM_WD, ADAM_STEP = 0.001, 0.9, 0.999, 1e-08, 0.01, 10

NT_DIMS = (((1,), (1,)), ((), ()))
TN_DIMS = (((0,), (0,)), ((), ()))
MESH = pl.DeviceIdType.MESH


def _params(n_grid=1, limit=VMEM_LIMIT):
    return pltpu.CompilerParams(dimension_semantics=("arbitrary",) * n_grid, vmem_limit_bytes=limit)


def _spec(shape, imap, single=False):
    if single:
        return pl.BlockSpec(shape, imap, pipeline_mode=pl.Buffered(1))
    return pl.BlockSpec(shape, imap)


def _sigmoid(x):
    return 1.0 / (1.0 + jnp.exp(-x))


def _softplus(x):
    return jnp.maximum(x, 0.0) + jnp.log(1.0 + jnp.exp(-jnp.abs(x)))


def _rms_stats(h):
    return lax.rsqrt(jnp.mean(h * h, axis=-1, keepdims=True) + EPS)


def _rms(h, w):
    return (h * _rms_stats(h)) * w


def _rms_bwd(du, h, w):
    r = _rms_stats(h)
    n = h * r
    dn = du * w
    dh = r * (dn - n * jnp.mean(dn * n, axis=-1, keepdims=True))
    return dh, du * n


_G0 = math.sqrt(2.0 / math.pi)


def _gelu(x):
    return 0.5 * x * (1.0 + jnp.tanh(_G0 * (x + 0.044715 * (x * x * x))))


def _gelu_grad(x):
    t = jnp.tanh(_G0 * (x + 0.044715 * (x * x * x)))
    return 0.5 * (1.0 + t) + 0.5 * x * (1.0 - t * t) * (_G0 * (1.0 + 3.0 * 0.044715 * (x * x)))


def _rows(shape, r0=0):
    return lax.broadcasted_iota(jnp.int32, shape, 0) + r0


def _lanes(shape):
    return lax.broadcasted_iota(jnp.int32, shape, 1)


HALO = 8


def _fill_padded(pad_ref, x_ref):
    pad_ref[0:HALO, :] = jnp.zeros((HALO, pad_ref.shape[1]), F32)
    pad_ref[T + HALO:T + 2 * HALO, :] = jnp.zeros((HALO, pad_ref.shape[1]), F32)

    def step(c, carry):
        r0 = pl.multiple_of(c * Q, Q)
        pad_ref[pl.ds(r0 + HALO, Q), :] = x_ref[pl.ds(r0, Q), :]
        return carry

    lax.fori_loop(0, NCH, step, 0)


def _back(pad_ref, r0):
    win = pad_ref[pl.ds(r0, Q + HALO), :]
    return lambda s: win[HALO:, :] if s == 0 else pltpu.roll(win, s, axis=0)[HALO:, :]


def _ahead(pad_ref, r0):
    win = pad_ref[pl.ds(r0 + HALO, Q + HALO), :]
    return lambda s: win[:Q, :] if s == 0 else pltpu.roll(win, Q + HALO - s, axis=0)[:Q, :]


def _conv(back, w, b):
    y = b + w[3:4, :] * back(0)
    for k in range(3):
        y = y + w[k:k + 1, :] * back(3 - k)
    return y


def _conv_bwd_x(ahead, w):
    dx = w[3:4, :] * ahead(0)
    for k in range(3):
        dx = dx + w[k:k + 1, :] * ahead(3 - k)
    return dx


def _conv_bwd_w(dy, back):
    dws = [jnp.sum(dy * back(3 - k), axis=0, keepdims=True) for k in range(4)]
    return jnp.concatenate(dws, axis=0), jnp.sum(dy, axis=0, keepdims=True)


def _chunks(fn, unrolled=False):
    if unrolled:
        for c in range(NCH):
            fn(c * Q)
        return

    def step(c, carry):
        fn(pl.multiple_of(c * Q, Q))
        return carry

    lax.fori_loop(0, NCH, step, 0)


HALF = RC // 2


def _col_tiles(n, tn, fn):
    def step(j, carry):
        fn(pl.multiple_of(j * tn, tn))
        return carry

    lax.fori_loop(0, n // tn, step, 0)


def _rows_spec(cols, block_col=0):
    return _spec((RC, cols), lambda i: (i, block_col))


def _whole(shape):
    return _spec(shape, lambda i: tuple(0 for _ in shape), single=True)


def _vec(cols):
    return _spec((1, cols), lambda i: (0, 0))


def _zero_at_first(*refs):
    @pl.when(pl.program_id(0) == 0)
    def _():
        for r in refs:
            r[...] = jnp.zeros_like(r)


IN_RUNS = ((PZ, 0, 1024), (PG, 2576, 2048), (PXBC, 1024, XBC))


def _in_tiles(fn):
    for pcol, wrow, width in IN_RUNS:
        def step(j, carry, pcol=pcol, wrow=wrow):
            fn(pl.multiple_of(pcol + j * 512, 512), pl.multiple_of(wrow + j * 512, 16))
            return carry

        lax.fori_loop(0, width // 512, step, 0)


def in_proj(h0, wn, w_t, w_dt):
    def body(h_ref, wn_ref, w_ref, wdt_ref, o_ref, u_ref):
        for r in (0, HALF):
            u_ref[r:r + HALF, :] = _rms(h_ref[r:r + HALF, :], wn_ref[...]).astype(BF)

        def tile(pcol, wrow):
            o_ref[:, pl.ds(pcol, 512)] = lax.dot_general(u_ref[...], w_ref[pl.ds(wrow, 512), :], NT_DIMS, preferred_element_type=F32)

        _in_tiles(tile)
        o_ref[:, PDT:PDT + 256] = lax.dot_general(u_ref[...], wdt_ref[...], NT_DIMS, preferred_element_type=F32)

    return pl.pallas_call(
        body, grid=(T // RC,), in_specs=[_rows_spec(D), _vec(D), _whole((IN_COLS, D)), _whole((256, D))],
        out_specs=[_rows_spec(NP_IN), _rows_spec(D)],
        out_shape=[jax.ShapeDtypeStruct((T, NP_IN), F32), jax.ShapeDtypeStruct((T, D), BF)],
        compiler_params=_params(), name="in_proj")(h0, wn, w_t, w_dt)


def out_proj(yn_ssd, proj, hseq, lru_nw, w_out, h0):
    def body(y_ref, g_ref, h_ref, wn_ref, w_ref, r_ref, o_ref, cat_ref):
        cat_ref[:, 0:SSD_W] = y_ref[...]
        for r in (0, HALF):
            y = _gelu(g_ref[r:r + HALF, :]) * h_ref[r:r + HALF, :]
            cat_ref[r:r + HALF, SSD_W:] = _rms(y, wn_ref[...]).astype(BF)

        def tile(c0):
            o_ref[:, pl.ds(c0, 512)] = r_ref[:, pl.ds(c0, 512)] + jnp.dot(cat_ref[...], w_ref[:, pl.ds(c0, 512)], preferred_element_type=F32)

        _col_tiles(D, 512, tile)

    return pl.pallas_call(
        body, grid=(T // RC,),
        in_specs=[_rows_spec(SSD_W), _rows_spec(LRU_W, PG // LRU_W), _rows_spec(LRU_W), _vec(LRU_W), _whole((SSD_W + LRU_W, D)), _rows_spec(D)],
        out_specs=[_rows_spec(D), _rows_spec(SSD_W + LRU_W)],
        out_shape=[jax.ShapeDtypeStruct((T, D), F32), jax.ShapeDtypeStruct((T, SSD_W + LRU_W), BF)],
        compiler_params=_params(), name="out_proj")(yn_ssd, proj, hseq, lru_nw, w_out, h0)


def out_proj_bwd(dh1_b, w_out, proj, hseq, lru_nw):
    def body(d_ref, w_ref, g_ref, h_ref, wn_ref, dy_ref, dh_ref, dg_ref, dw_ref, dl_scr):
        _zero_at_first(dw_ref)

        def tile(c0):
            dy_ref[:, pl.ds(c0, 512)] = lax.dot_general(d_ref[...], w_ref[pl.ds(c0, 512), :], NT_DIMS, preferred_element_type=F32)
            dl_scr[:, pl.ds(c0, 512)] = lax.dot_general(d_ref[...], w_ref[pl.ds(SSD_W + c0, 512), :], NT_DIMS, preferred_element_type=F32)

        _col_tiles(SSD_W, 512, tile)
        for r in (0, HALF):
            g = g_ref[r:r + HALF, :]
            h = h_ref[r:r + HALF, :]
            ge = _gelu(g)
            dy, dw = _rms_bwd(dl_scr[r:r + HALF, :], ge * h, wn_ref[...])
            dw_ref[...] += jnp.sum(dw, axis=0, keepdims=True)
            dh_ref[r:r + HALF, :] = dy * ge
            dg_ref[r:r + HALF, :] = (dy * h * _gelu_grad(g)).astype(BF)

    return pl.pallas_call(
        body, grid=(T // RC,),
        in_specs=[_rows_spec(D), _whole((SSD_W + LRU_W, D)), _rows_spec(LRU_W, PG // LRU_W), _rows_spec(LRU_W), _vec(LRU_W)],
        out_specs=[_rows_spec(SSD_W), _rows_spec(LRU_W), _rows_spec(LRU_W), _vec(LRU_W)],
        out_shape=[jax.ShapeDtypeStruct((T, SSD_W), F32), jax.ShapeDtypeStruct((T, LRU_W), F32), jax.ShapeDtypeStruct((T, LRU_W), BF),
                   jax.ShapeDtypeStruct((1, LRU_W), F32)],
        scratch_shapes=[pltpu.VMEM((RC, LRU_W), F32)],
        compiler_params=_params(), name="out_proj_bwd")(dh1_b, w_out, proj, hseq, lru_nw)


def in_proj_bwd(dz, dg, dxl, dxbc, ddt, w_t, w_dt, h0, wn, dh1):
    def body(dz_ref, dg_ref, dxl_ref, dxbc_ref, ddt_ref, w_ref, wdt_ref, h_ref, wn_ref, r_ref, o_ref, dw_ref, du_scr):
        _zero_at_first(dw_ref)
        du_scr[...] = jnp.dot(ddt_ref[...], wdt_ref[...], preferred_element_type=F32)
        for d_ref, wrow, width in ((dz_ref, 0, 1024), (dxbc_ref, 1024, XBC), (dg_ref, 2576, 1024), (dxl_ref, 3600, 1024)):
            def step(j, carry, d_ref=d_ref, wrow=wrow):
                c0 = pl.multiple_of(j * 512, 512)
                du_scr[...] += jnp.dot(d_ref[:, pl.ds(c0, 512)], w_ref[pl.ds(pl.multiple_of(wrow + c0, 16), 512), :], preferred_element_type=F32)
                return carry

            lax.fori_loop(0, width // 512, step, 0)
        for r in (0, HALF):
            dh, dw = _rms_bwd(du_scr[r:r + HALF, :], h_ref[r:r + HALF, :], wn_ref[...])
            dw_ref[...] += jnp.sum(dw, axis=0, keepdims=True)
            o_ref[r:r + HALF, :] = dh + r_ref[r:r + HALF, :]

    return pl.pallas_call(
        body, grid=(T // RC,),
        in_specs=[_rows_spec(SSD_W), _rows_spec(LRU_W), _rows_spec(LRU_W), _rows_spec(XBC), _rows_spec(256), _whole((IN_COLS, D)),
                  _whole((256, D)), _rows_spec(D), _vec(D), _rows_spec(D)],
        out_specs=[_rows_spec(D), _vec(D)],
        out_shape=[jax.ShapeDtypeStruct((T, D), F32), jax.ShapeDtypeStruct((1, D), F32)],
        scratch_shapes=[pltpu.VMEM((RC, D), F32)],
        compiler_params=_params(), name="in_proj_bwd")(dz, dg, dxl, dxbc, ddt, w_t, w_dt, h0, wn, dh1)


def matmul_tn(name, a, b, tm, tn):
    m, n = a.shape[1], b.shape[1]

    def body(a_ref, b_ref, o_ref, acc_ref):
        acc_ref[...] = jnp.zeros_like(acc_ref)

        def mm(r0):
            acc_ref[...] += lax.dot_general(a_ref[pl.ds(r0, RC), :], b_ref[pl.ds(r0, RC), :], TN_DIMS, preferred_element_type=F32)

        _col_tiles(T, RC, mm)
        o_ref[...] = acc_ref[...].astype(BF)

    return pl.pallas_call(
        body, grid=(m // tm, n // tn),
        in_specs=[_spec((T, tm), lambda i, j: (0, i)), _spec((T, tn), lambda i, j: (0, j))],
        out_specs=_spec((tm, tn), lambda i, j: (i, j)),
        out_shape=jax.ShapeDtypeStruct((m, n), BF),
        scratch_shapes=[pltpu.VMEM((tm, tn), F32)],
        compiler_params=_params(2), name=name)(a, b)


def conv_silu_fwd(proj, cw, cb):
    def body(x_ref, w_ref, b_ref, o_ref, xpad):
        _fill_padded(xpad, x_ref)

        def chunk(r0):
            pre = _conv(_back(xpad, r0), w_ref[...], b_ref[...])
            o_ref[pl.ds(r0, Q), :] = pre * _sigmoid(pre)

        _chunks(chunk)

    c0 = PXBC // 128
    return pl.pallas_call(
        body, grid=(XBC // 128,),
        in_specs=[_spec((T, 128), lambda c: (0, c0 + c)), _spec((4, 128), lambda c: (0, c)), _spec((1, 128), lambda c: (0, c))],
        out_specs=_spec((T, 128), lambda c: (0, c)),
        out_shape=jax.ShapeDtypeStruct((T, XBC), F32), scratch_shapes=[pltpu.VMEM((T + 2 * HALO, 128), F32)],
        compiler_params=_params(), name="conv_silu_fwd")(proj, cw, cb)


def conv_silu_bwd(dx, d_b, d_c, proj, cw, cb):
    def body(dx_ref, db_ref, dc_ref, x_ref, w_ref, b_ref, o_ref, dw_ref, dbias_ref, xpad, dpad):
        tile = pl.program_id(0)
        _fill_padded(xpad, x_ref)
        dpad[0:HALO, :] = jnp.zeros((HALO, 128), F32)
        dpad[T + HALO:T + 2 * HALO, :] = jnp.zeros((HALO, 128), F32)
        dw_ref[...] = jnp.zeros_like(dw_ref)
        dbias_ref[...] = jnp.zeros_like(dbias_ref)

        def first(r0):
            back = _back(xpad, r0)
            pre = _conv(back, w_ref[...], b_ref[...])
            sg = _sigmoid(pre)
            rows = pl.ds(r0, Q)
            d = jnp.where(tile < 8, dx_ref[rows, :], jnp.where(tile < 10, db_ref[rows, :], dc_ref[rows, :]))
            dpre = d * (sg * (1.0 + pre * (1.0 - sg)))
            dpad[pl.ds(r0 + HALO, Q), :] = dpre
            dw, dbias = _conv_bwd_w(dpre, back)
            dw_ref[...] += dw
            dbias_ref[...] += dbias

        _chunks(first)

        def second(r0):
            o_ref[pl.ds(r0, Q), :] = _conv_bwd_x(_ahead(dpad, r0), w_ref[...]).astype(BF)

        _chunks(second)

    c0 = PXBC // 128
    pad = pltpu.VMEM((T + 2 * HALO, 128), F32)
    return pl.pallas_call(
        body, grid=(XBC // 128,),
        in_specs=[_spec((T, 128), lambda c: (0, jnp.minimum(c, 7))), _spec((T, 128), lambda c: (0, jnp.clip(c - 8, 0, 1))),
                  _spec((T, 128), lambda c: (0, jnp.clip(c - 10, 0, 1))), _spec((T, 128), lambda c: (0, c0 + c)),
                  _spec((4, 128), lambda c: (0, c)), _spec((1, 128), lambda c: (0, c))],
        out_specs=[_spec((T, 128), lambda c: (0, c)), _spec((4, 128), lambda c: (0, c)), _spec((1, 128), lambda c: (0, c))],
        out_shape=[jax.ShapeDtypeStruct((T, XBC), BF), jax.ShapeDtypeStruct((4, XBC), F32), jax.ShapeDtypeStruct((1, XBC), F32)],
        scratch_shapes=[pad, pad], compiler_params=_params(), name="conv_silu_bwd")(dx, d_b, d_c, proj, cw, cb)


def _ssd_chunk_common(row0, dt_ref, b_ref, c_ref, bias, a_neg):
    shape = (Q, Q)
    lane = _lanes(shape)
    sub = _rows(shape)
    live = (_rows(shape, row0) >= NPAD) & (lane < 8)
    dtr = dt_ref[:, :]
    dt = jnp.where(live, _softplus(dtr + bias), 0.0)
    d_a = dt * a_neg
    tri = (sub >= lane).astype(F32)
    cs = jnp.dot(tri, d_a, precision=lax.Precision.HIGHEST, preferred_element_type=F32)
    cs_t = cs.T
    b_f = b_ref[:, :]
    bc = b_f.astype(BF)
    cc = c_ref[:, :].astype(BF)
    cb = lax.dot_general(cc, bc, NT_DIMS, preferred_element_type=F32)
    cs_last = cs[Q - 1:Q, :]
    return dict(lane=lane, sub=sub, live=live, dtr=dtr, dt=dt, cs=cs, cs_t=cs_t, bc=bc, cc=cc, cb=cb, bc_t=b_f.T.astype(BF),
                ecs=jnp.exp(cs), dsm=jnp.exp(cs_last - cs), gam=jnp.exp(cs_last))


def _pair(lane_even, mat, j):
    return jnp.where(lane_even, mat[:, j:j + 1], mat[:, j + 1:j + 2])


def _pair_row(lane_even, mat, j):
    return jnp.where(lane_even[0:1, :], mat[:, j:j + 1], mat[:, j + 1:j + 2])


def _head_decay(cm, j):
    seg = cm["cs"][:, j:j + 1] - cm["cs_t"][j:j + 1, :]
    return jnp.exp(jnp.where(cm["sub"] >= cm["lane"], seg, -jnp.inf))


def _head_decay_t(cm, j):
    seg = cm["cs_t"][j:j + 1, :] - cm["cs"][:, j:j + 1]
    return jnp.exp(jnp.where(cm["lane"] >= cm["sub"], seg, -jnp.inf))


def ssd_fwd(xbc_act, proj, dt_bias2, a_log2, d2, norm_w):
    def body(x_ref, b_ref, c_ref, dt_ref, z_ref, bias_ref, alog_ref, d_ref, nw_ref, yn_ref, y_ref, hp_ref, h_scr):
        c = pl.program_id(1)

        @pl.when(c == 0)
        def _():
            h_scr[...] = jnp.zeros_like(h_scr)

        bias = bias_ref[0]
        a_neg = -jnp.exp(alog_ref[0])
        dsk = d_ref[0]
        cm = _ssd_chunk_common(c * Q, dt_ref, b_ref, c_ref, bias, a_neg)
        lane_even = cm["lane"] < 64
        for p in range(4):
            je, jo = 2 * p, 2 * p + 1
            xp = x_ref[:, 128 * p:128 * p + 128]
            xdt = xp * _pair(lane_even, cm["dt"], je)
            xdt_b = xdt.astype(BF)
            m_e = (cm["cb"] * _head_decay(cm, je)).astype(BF)
            m_o = (cm["cb"] * _head_decay(cm, jo)).astype(BF)
            zero = jnp.zeros_like(xdt_b)
            yd = (jnp.dot(m_e, jnp.where(lane_even, xdt_b, zero), preferred_element_type=F32)
                  + jnp.dot(m_o, jnp.where(lane_even, zero, xdt_b), preferred_element_type=F32))
            hp = h_scr[p]
            hp_ref[0, 0, p] = hp
            yo = jnp.dot(cm["cc"], hp.astype(BF), preferred_element_type=F32) * _pair(lane_even, cm["ecs"], je)
            y_ref[:, 128 * p:128 * p + 128] = yd + yo + xp * _pair_row(lane_even, dsk, je)
            st = jnp.dot(cm["bc_t"], (xdt * _pair(lane_even, cm["dsm"], je)).astype(BF), preferred_element_type=F32)
            h_scr[p] = hp * _pair_row(lane_even, cm["gam"], je) + st
        zc = z_ref[:, :]
        gated = y_ref[:, :] * (zc * _sigmoid(zc))
        yn_ref[:, :] = _rms(gated, nw_ref[...]).astype(BF)

    par = _spec((1, 1, 128), lambda g, c: (g, 0, 0))
    wide = _spec((Q, 512), lambda g, c: (c, g))
    return pl.pallas_call(
        body, grid=(2, NCH),
        in_specs=[wide, _spec((Q, 128), lambda g, c: (c, 8 + g)), _spec((Q, 128), lambda g, c: (c, 10 + g)),
                  _spec((Q, 128), lambda g, c: (c, PDT // 128 + g)), wide, par, par, par, _spec((1, 512), lambda g, c: (0, g))],
        out_specs=[wide, wide, _spec((1, 1, 4, 128, 128), lambda g, c: (g, c, 0, 0, 0))],
        out_shape=[jax.ShapeDtypeStruct((T, SSD_W), BF), jax.ShapeDtypeStruct((T, SSD_W), F32),
                   jax.ShapeDtypeStruct((2, NCH, 4, 128, 128), F32)],
        scratch_shapes=[pltpu.VMEM((4, 128, 128), F32)],
        compiler_params=_params(2), name="ssd_fwd")(xbc_act, xbc_act, xbc_act, proj, proj, dt_bias2, a_log2, d2, norm_w)


def ssd_bwd(dyn, xbc_act, proj, y_pre, h_prev, dt_bias2, a_log2, d2, norm_w):
    def body(dyn_ref, x_ref, b_ref, c_ref, dt_ref, z_ref, y_ref, hp_ref, bias_ref, alog_ref, d_ref, nw_ref,
             dz_ref, dx_ref, db_ref, dc_ref, ddt_ref, dpar_ref, dnw_ref, dh_scr, acc_scr):
        ci = pl.program_id(1)

        @pl.when(ci == 0)
        def _():
            dh_scr[...] = jnp.zeros_like(dh_scr)
            acc_scr[...] = jnp.zeros_like(acc_scr)
            dnw_ref[...] = jnp.zeros_like(dnw_ref)

        bias = bias_ref[0]
        a_neg = -jnp.exp(alog_ref[0])
        dsk = d_ref[0]
        cm = _ssd_chunk_common((NCH - 1 - ci) * Q, dt_ref, b_ref, c_ref, bias, a_neg)
        lane, sub = cm["lane"], cm["sub"]
        lane_even = lane < 64
        cc_t = c_ref[:, :].T.astype(BF)
        cb_t = lax.dot_general(cm["bc"], cm["cc"], NT_DIMS, preferred_element_type=F32)
        zc = z_ref[:, :]
        yc = y_ref[:, :]
        sg = _sigmoid(zc)
        sz = zc * sg
        dgated, dnw = _rms_bwd(dyn_ref[:, :], yc * sz, nw_ref[...])
        dnw_ref[...] += jnp.sum(dnw, axis=0, keepdims=True)
        dz_ref[:, :] = (dgated * yc * (sg * (1.0 + zc * (1.0 - sg)))).astype(BF)
        dy_all = dgated * sz
        dcb = jnp.zeros((Q, Q), F32)
        dcb_t = jnp.zeros((Q, Q), F32)
        db_acc = jnp.zeros((Q, Q), F32)
        dc_acc = jnp.zeros((Q, Q), F32)
        dcs = jnp.zeros((Q, Q), F32)
        ddt = jnp.zeros((Q, Q), F32)
        for p in range(4):
            je, jo = 2 * p, 2 * p + 1
            xp = x_ref[:, 128 * p:128 * p + 128]
            dy = dy_all[:, 128 * p:128 * p + 128]
            dt_p = _pair(lane_even, cm["dt"], je)
            xdt = xp * dt_p
            xdt_b = xdt.astype(BF)
            dy_b = dy.astype(BF)
            zero = jnp.zeros_like(dy_b)
            hp = hp_ref[0, 0, p]
            hp_b = hp.astype(BF)
            dh = dh_scr[p]
            dh_b = dh.astype(BF)
            acc_scr[p:p + 1, :] += jnp.sum(dy * xp, axis=0, keepdims=True)
            dxp = dy * _pair_row(lane_even, dsk, je)
            e_p = _pair(lane_even, cm["ecs"], je)
            g_p = jnp.dot(cm["cc"], hp_b, preferred_element_type=F32)
            dg_b = (dy * e_p).astype(BF)
            de = dy * g_p * e_p
            dc_acc = dc_acc + lax.dot_general(dg_b, hp_b, NT_DIMS, preferred_element_type=F32)
            dh_in = jnp.dot(cc_t, dg_b, preferred_element_type=F32)
            ds_p = _pair(lane_even, cm["dsm"], je)
            r_p = jnp.dot(cm["bc"], dh_b, preferred_element_type=F32)
            dxdt = r_p * ds_p
            tt = r_p * xdt * ds_p
            db_acc = db_acc + lax.dot_general((xdt * ds_p).astype(BF), dh_b, NT_DIMS, preferred_element_type=F32)
            dgam_m = jnp.sum(dh * hp, axis=0, keepdims=True)
            for j, even in ((je, True), (jo, False)):
                sel = lane_even if even else jnp.logical_not(lane_even)
                dy_j = jnp.where(sel, dy_b, zero)
                l_j = _head_decay(cm, j)
                l_jt = _head_decay_t(cm, j)
                m_j = cm["cb"] * l_j
                m_jt = cb_t * l_jt
                dm = lax.dot_general(dy_j, xdt_b, NT_DIMS, preferred_element_type=F32)
                dm_t = lax.dot_general(xdt_b, dy_j, NT_DIMS, preferred_element_type=F32)
                dxdt = dxdt + jnp.dot(m_jt.astype(BF), dy_j, preferred_element_type=F32)
                dcb = dcb + dm * l_j
                dcb_t = dcb_t + dm_t * l_jt
                t_j = jnp.where(sel, tt, 0.0)
                col = jnp.sum(dm * m_j - dm_t * m_jt + (jnp.where(sel, de, 0.0) - t_j), axis=1, keepdims=True)
                gam_j = cm["gam"][:, j:j + 1]
                last = (jnp.sum(jnp.sum(t_j, axis=0, keepdims=True), axis=1, keepdims=True)
                        + jnp.sum(jnp.where(sel[0:1, :], dgam_m, 0.0), axis=1, keepdims=True) * gam_j)
                col = col + jnp.where(sub[:, 0:1] == Q - 1, last, 0.0)
                dcs = dcs + jnp.where(lane == j, col, 0.0)
            dh_scr[p] = dh_in + dh * _pair_row(lane_even, cm["gam"], je)
            dx_ref[:, 128 * p:128 * p + 128] = dxp + dxdt * dt_p
            dd = dxdt * xp
            ddt = ddt + jnp.where(lane == je, jnp.sum(jnp.where(lane_even, dd, 0.0), axis=1, keepdims=True), 0.0)
            ddt = ddt + jnp.where(lane == jo, jnp.sum(jnp.where(lane_even, 0.0, dd), axis=1, keepdims=True), 0.0)
        dc_ref[:, :] = dc_acc + jnp.dot(dcb.astype(BF), cm["bc"], preferred_element_type=F32)
        db_ref[:, :] = db_acc + jnp.dot(dcb_t.astype(BF), cm["cc"], preferred_element_type=F32)
        tri_t = (sub <= lane).astype(F32)
        dd_a = jnp.dot(tri_t, dcs, precision=lax.Precision.HIGHEST, preferred_element_type=F32)
        ddt = ddt + dd_a * a_neg
        acc_scr[5:6, :] += jnp.sum(dd_a * cm["dt"], axis=0, keepdims=True)
        draw = jnp.where(cm["live"], ddt * _sigmoid(cm["dtr"] + bias), 0.0)
        acc_scr[4:5, :] += jnp.sum(draw, axis=0, keepdims=True)
        ddt_ref[:, :] = draw.astype(BF)

        @pl.when(ci == NCH - 1)
        def _():
            lane1 = _lanes((1, 128))
            dd = jnp.zeros((1, 128), F32)
            for p in range(4):
                row = acc_scr[p:p + 1, :]
                dd = dd + jnp.where(lane1 == 2 * p, jnp.sum(jnp.where(lane1 < 64, row, 0.0), axis=1, keepdims=True), 0.0)
                dd = dd + jnp.where(lane1 == 2 * p + 1, jnp.sum(jnp.where(lane1 < 64, 0.0, row), axis=1, keepdims=True), 0.0)
            dpar_ref[0] = jnp.concatenate([acc_scr[4:5, :], acc_scr[5:6, :] * a_neg, dd, jnp.zeros((5, 128), F32)], axis=0)

    par = _spec((1, 1, 128), lambda g, c: (g, 0, 0))
    wide = _spec((Q, 512), lambda g, c: (NCH - 1 - c, g))
    thin = _spec((Q, 128), lambda g, c: (NCH - 1 - c, g))
    return pl.pallas_call(
        body, grid=(2, NCH),
        in_specs=[wide, wide, _spec((Q, 128), lambda g, c: (NCH - 1 - c, 8 + g)), _spec((Q, 128), lambda g, c: (NCH - 1 - c, 10 + g)),
                  _spec((Q, 128), lambda g, c: (NCH - 1 - c, PDT // 128 + g)), wide, wide,
                  _spec((1, 1, 4, 128, 128), lambda g, c: (g, NCH - 1 - c, 0, 0, 0)), par, par, par, _spec((1, 512), lambda g, c: (0, g))],
        out_specs=[wide, wide, thin, thin, thin, _spec((1, 8, 128), lambda g, c: (g, 0, 0)), _spec((1, 512), lambda g, c: (0, g))],
        out_shape=[jax.ShapeDtypeStruct((T, SSD_W), BF), jax.ShapeDtypeStruct((T, SSD_W), F32), jax.ShapeDtypeStruct((T, 256), F32),
                   jax.ShapeDtypeStruct((T, 256), F32), jax.ShapeDtypeStruct((T, 256), BF), jax.ShapeDtypeStruct((2, 8, 128), F32),
                   jax.ShapeDtypeStruct((1, SSD_W), F32)],
        scratch_shapes=[pltpu.VMEM((4, 128, 128), F32), pltpu.VMEM((8, 128), F32)],
        compiler_params=_params(2), name="ssd_bwd")(dyn, xbc_act, xbc_act, xbc_act, proj, proj, y_pre, h_prev, dt_bias2, a_log2, d2, norm_w)


def _lru_gates(back, cw, cb, wa, ba, wx, bx, lam):
    xr = _conv(back, cw, cb)
    xr_b = xr.astype(BF)
    r = _sigmoid(jnp.dot(xr_b, wa, preferred_element_type=F32) + ba)
    i = _sigmoid(jnp.dot(xr_b, wx, preferred_element_type=F32) + bx)
    sp = _softplus(-lam)
    la = (-LRU_C) * r * sp
    a = jnp.exp(la)
    mult = jnp.sqrt(-jnp.tanh(la) * (a * a + 1.0))
    return xr, xr_b, r, i, sp, a, mult


def lru_gates_fwd(proj, cw, cb, wa2, ba, wx2, bx, lam):
    def body(x_ref, cw_ref, cb_ref, wa_ref, ba_ref, wx_ref, bx_ref, lam_ref, a_ref, u_ref, xpad):
        _fill_padded(xpad, x_ref)

        def chunk(r0):
            xr, _, _, i, _, a, mult = _lru_gates(_back(xpad, r0), cw_ref[...], cb_ref[...], wa_ref[0], ba_ref[...], wx_ref[0], bx_ref[...],
                                                 lam_ref[...])
            a_ref[pl.ds(r0, Q), :] = a
            u_ref[pl.ds(r0, Q), :] = jnp.where(_rows(a.shape, r0) >= NPAD, mult * (i * xr), 0.0)

        _chunks(chunk, unrolled=True)

    c0 = PXL // 128
    vec = _spec((1, 128), lambda c: (0, c))
    mat = _spec((1, 128, 128), lambda c: (c, 0, 0))
    return pl.pallas_call(
        body, grid=(8,),
        in_specs=[_spec((T, 128), lambda c: (0, c0 + c)), _spec((4, 128), lambda c: (0, c)), vec, mat, vec, mat, vec, vec],
        out_specs=[_spec((T, 128), lambda c: (0, c)), _spec((T, 128), lambda c: (0, c))],
        out_shape=[jax.ShapeDtypeStruct((T, LRU_W), F32), jax.ShapeDtypeStruct((T, LRU_W), F32)],
        scratch_shapes=[pltpu.VMEM((T + 2 * HALO, 128), F32)],
        compiler_params=_params(), name="lru_gates_fwd")(proj, cw, cb, wa2, ba, wx2, bx, lam)


def lru_scan_fwd(a, u):
    def body(a_ref, u_ref, h_ref):
        def step(i, h):
            base = pl.multiple_of(i * 8, 8)
            for k in range(8):
                h = a_ref[pl.ds(base + k, 1), :] * h + u_ref[pl.ds(base + k, 1), :]
                h_ref[pl.ds(base + k, 1), :] = h
            return h

        lax.fori_loop(0, T // 8, step, jnp.zeros((1, LRU_W), F32))

    return pl.pallas_call(body, out_shape=jax.ShapeDtypeStruct((T, LRU_W), F32), compiler_params=_params(0), name="lru_scan_fwd")(a, u)


def lru_scan_bwd(a, dh_out):
    def body(a_ref, d_ref, o_ref):
        def step(i, carry):
            base = pl.multiple_of(T - 8 - i * 8, 8)
            for k in range(7, -1, -1):
                carry = d_ref[pl.ds(base + k, 1), :] + carry
                o_ref[pl.ds(base + k, 1), :] = carry
                carry = carry * a_ref[pl.ds(base + k, 1), :]
            return carry

        lax.fori_loop(0, T // 8, step, jnp.zeros((1, LRU_W), F32))

    return pl.pallas_call(body, out_shape=jax.ShapeDtypeStruct((T, LRU_W), F32), compiler_params=_params(0), name="lru_scan_bwd")(a, dh_out)


def lru_gates_bwd(dhs, hseq, proj, cw, cb, wa2, ba, wx2, bx, lam):
    def body(dh_ref, h_ref, x_ref, cw_ref, cb_ref, wa_ref, ba_ref, wx_ref, bx_ref, lam_ref,
             dx_ref, dcw_ref, dcb_ref, dwa_ref, dba_ref, dwx_ref, dbx_ref, dlam_ref, xpad, hpad, dpad):
        _fill_padded(xpad, x_ref)
        _fill_padded(hpad, h_ref)
        dpad[0:HALO, :] = jnp.zeros((HALO, 128), F32)
        dpad[T + HALO:T + 2 * HALO, :] = jnp.zeros((HALO, 128), F32)
        for ref in (dcw_ref, dcb_ref, dwa_ref, dba_ref, dwx_ref, dbx_ref, dlam_ref):
            ref[...] = jnp.zeros_like(ref)
        lam = lam_ref[...]

        def first(r0):
            back = _back(xpad, r0)
            xr, xr_b, r, i, sp, a, mult = _lru_gates(back, cw_ref[...], cb_ref[...], wa_ref[0], ba_ref[...], wx_ref[0], bx_ref[...], lam)
            dh = dh_ref[pl.ds(r0, Q), :]
            da = dh * _back(hpad, r0)(1)
            du = jnp.where(_rows(dh.shape, r0) >= NPAD, dh, 0.0)
            dmult = du * (i * xr)
            di = du * (mult * xr)
            dxr = du * (mult * i)
            dla = da * a - dmult * (a * a) / mult
            dr = dla * ((-LRU_C) * sp)
            dlam_ref[...] += jnp.sum(dla * ((-LRU_C) * r), axis=0, keepdims=True)
            dpr = dr * r * (1.0 - r)
            dpi = di * i * (1.0 - i)
            dba_ref[...] += jnp.sum(dpr, axis=0, keepdims=True)
            dbx_ref[...] += jnp.sum(dpi, axis=0, keepdims=True)
            dpr_b = dpr.astype(BF)
            dpi_b = dpi.astype(BF)
            dxr = (dxr + lax.dot_general(dpr_b, wa_ref[0], NT_DIMS, preferred_element_type=F32)
                   + lax.dot_general(dpi_b, wx_ref[0], NT_DIMS, preferred_element_type=F32))
            dwa_ref[0] += lax.dot_general(xr_b, dpr_b, TN_DIMS, preferred_element_type=F32)
            dwx_ref[0] += lax.dot_general(xr_b, dpi_b, TN_DIMS, preferred_element_type=F32)
            dpad[pl.ds(r0 + HALO, Q), :] = dxr
            dcw, dcb = _conv_bwd_w(dxr, back)
            dcw_ref[...] += dcw
            dcb_ref[...] += dcb

        _chunks(first, unrolled=True)
        dlam_ref[...] = -dlam_ref[...] * _sigmoid(-lam)

        def second(r0):
            dx_ref[pl.ds(r0, Q), :] = _conv_bwd_x(_ahead(dpad, r0), cw_ref[...]).astype(BF)

        _chunks(second)

    c0 = PXL // 128
    vec = _spec((1, 128), lambda c: (0, c))
    mat = _spec((1, 128, 128), lambda c: (c, 0, 0))
    col = _spec((T, 128), lambda c: (0, c))
    vshape = jax.ShapeDtypeStruct((1, LRU_W), F32)
    mshape = jax.ShapeDtypeStruct((8, 128, 128), F32)
    pad = pltpu.VMEM((T + 2 * HALO, 128), F32)
    return pl.pallas_call(
        body, grid=(8,),
        in_specs=[col, col, _spec((T, 128), lambda c: (0, c0 + c)), _spec((4, 128), lambda c: (0, c)), vec, mat, vec, mat, vec, vec],
        out_specs=[col, _spec((4, 128), lambda c: (0, c)), vec, mat, vec, mat, vec, vec],
        out_shape=[jax.ShapeDtypeStruct((T, LRU_W), BF), jax.ShapeDtypeStruct((4, LRU_W), F32), vshape, mshape, vshape, mshape, vshape, vshape],
        scratch_shapes=[pad, pad, pad], compiler_params=_params(), name="lru_gates_bwd")(dhs, hseq, proj, cw, cb, wa2, ba, wx2, bx, lam)


def gate_up(h1, wn, w_gate, w_up):
    def body(h_ref, wn_ref, wg_ref, wu_ref, gt_ref, up_ref, act_ref, u_ref):
        for r in (0, HALF):
            u_ref[r:r + HALF, :] = _rms(h_ref[r:r + HALF, :], wn_ref[...]).astype(BF)

        def tile(c0):
            cols = pl.ds(c0, 256)
            gt = lax.dot_general(u_ref[...], wg_ref[cols, :], NT_DIMS, preferred_element_type=F32)
            up = lax.dot_general(u_ref[...], wu_ref[cols, :], NT_DIMS, preferred_element_type=F32)
            gt_ref[:, cols] = gt.astype(BF)
            up_ref[:, cols] = up.astype(BF)
            act_ref[:, cols] = (gt * _sigmoid(gt) * up).astype(BF)

        _col_tiles(D_FF, 256, tile)

    big = jax.ShapeDtypeStruct((T, D_FF), BF)
    return pl.pallas_call(
        body, grid=(T // RC,), in_specs=[_rows_spec(D), _vec(D), _whole((D_FF, D)), _whole((D_FF, D))],
        out_specs=[_rows_spec(D_FF), _rows_spec(D_FF), _rows_spec(D_FF), _rows_spec(D)],
        out_shape=[big, big, big, jax.ShapeDtypeStruct((T, D), BF)],
        compiler_params=_params(), name="gate_up")(h1, wn, w_gate, w_up)


def down_loss(act, w_down, h1, target, wf):
    first = NPAD + N_META

    def body(a_ref, w_ref, r_ref, t_hbm, wf_ref, d_ref, db_ref, l_ref, dw_ref, h_scr, t_ref, t_sem):
        i = pl.program_id(0)
        _zero_at_first(l_ref, dw_ref)
        head = pltpu.make_async_copy(t_hbm.at[pl.ds(0, RC - first)], t_ref.at[pl.ds(first, RC - first)], t_sem)
        rest = pltpu.make_async_copy(t_hbm.at[pl.ds(pl.multiple_of(jnp.maximum(i * RC - first, 0), 32), RC)], t_ref, t_sem)

        @pl.when(i == 0)
        def _():
            t_ref[0:first, :] = jnp.zeros((first, D), F32)
            head.start()

        @pl.when(i > 0)
        def _():
            rest.start()

        def tile(c0):
            cols = pl.ds(c0, 512)
            h_scr[:, cols] = r_ref[:, cols] + jnp.dot(a_ref[...], w_ref[:, cols], preferred_element_type=F32)

        _col_tiles(D, 512, tile)

        @pl.when(i == 0)
        def _():
            head.wait()

        @pl.when(i > 0)
        def _():
            rest.wait()

        for r in (0, HALF):
            h = h_scr[r:r + HALF, :]
            live = _rows((HALF, D), i * RC + r) >= first
            err = jnp.where(live, _rms(h, wf_ref[...]) - t_ref[r:r + HALF, :], 0.0)
            l_ref[...] += 0.5 * jnp.sum(jnp.sum(err * err, axis=1, keepdims=True) * (1.0 / D), axis=0, keepdims=True)
            dh, dw = _rms_bwd(err * (1.0 / D), h, wf_ref[...])
            dw_ref[...] += jnp.sum(dw, axis=0, keepdims=True)
            d_ref[r:r + HALF, :] = dh
            db_ref[r:r + HALF, :] = dh.astype(BF)

    return pl.pallas_call(
        body, grid=(T // RC,),
        in_specs=[_rows_spec(D_FF), _whole((D_FF, D)), _rows_spec(D), pl.BlockSpec(memory_space=pl.ANY), _vec(D)],
        out_specs=[_rows_spec(D), _rows_spec(D), _spec((1, 128), lambda i: (0, 0)), _vec(D)],
        out_shape=[jax.ShapeDtypeStruct((T, D), F32), jax.ShapeDtypeStruct((T, D), BF), jax.ShapeDtypeStruct((1, 128), F32),
                   jax.ShapeDtypeStruct((1, D), F32)],
        scratch_shapes=[pltpu.VMEM((RC, D), F32), pltpu.VMEM((RC, D), F32), pltpu.SemaphoreType.DMA],
        compiler_params=_params(), name="down_loss")(act, w_down, h1, target, wf)


def swiglu_bwd(dh2_b, w_down, gt, up):
    def body(d_ref, w_ref, gt_ref, up_ref, dg_ref, du_ref):
        def tile(c0):
            cols = pl.ds(c0, 256)
            dact = lax.dot_general(d_ref[...], w_ref[cols, :], NT_DIMS, preferred_element_type=F32)
            gt_ = gt_ref[:, cols].astype(F32)
            up_ = up_ref[:, cols].astype(F32)
            sg = _sigmoid(gt_)
            dg_ref[:, cols] = (dact * up_ * (sg * (1.0 + gt_ * (1.0 - sg)))).astype(BF)
            du_ref[:, cols] = (dact * (gt_ * sg)).astype(BF)

        _col_tiles(D_FF, 256, tile)

    big = jax.ShapeDtypeStruct((T, D_FF), BF)
    return pl.pallas_call(
        body, grid=(T // RC,), in_specs=[_rows_spec(D), _whole((D_FF, D)), _rows_spec(D_FF), _rows_spec(D_FF)],
        out_specs=[_rows_spec(D_FF), _rows_spec(D_FF)], out_shape=[big, big], compiler_params=_params(), name="swiglu_bwd")(dh2_b, w_down, gt, up)


def gate_up_bwd(dgt, dup, w_gate, w_up, h1, wn, dh2):
    def body(dg_ref, du_ref, wg_ref, wu_ref, h_ref, wn_ref, r_ref, d_ref, db_ref, dw_ref, du_scr):
        _zero_at_first(dw_ref)

        du_scr[...] = jnp.zeros_like(du_scr)

        def tile(c0):
            k = pl.ds(c0, 256)
            du_scr[...] += (jnp.dot(dg_ref[:, k], wg_ref[k, :], preferred_element_type=F32)
                            + jnp.dot(du_ref[:, k], wu_ref[k, :], preferred_element_type=F32))

        _col_tiles(D_FF, 256, tile)
        for r in (0, HALF):
            dh, dw = _rms_bwd(du_scr[r:r + HALF, :], h_ref[r:r + HALF, :], wn_ref[...])
            dw_ref[...] += jnp.sum(dw, axis=0, keepdims=True)
            dh = dh + r_ref[r:r + HALF, :]
            d_ref[r:r + HALF, :] = dh
            db_ref[r:r + HALF, :] = dh.astype(BF)

    return pl.pallas_call(
        body, grid=(T // RC,),
        in_specs=[_rows_spec(D_FF), _rows_spec(D_FF), _whole((D_FF, D)), _whole((D_FF, D)), _rows_spec(D), _vec(D), _rows_spec(D)],
        out_specs=[_rows_spec(D), _rows_spec(D), _vec(D)],
        out_shape=[jax.ShapeDtypeStruct((T, D), F32), jax.ShapeDtypeStruct((T, D), BF), jax.ShapeDtypeStruct((1, D), F32)],
        scratch_shapes=[pltpu.VMEM((RC, D), F32)],
        compiler_params=_params(), name="gate_up_bwd")(dgt, dup, w_gate, w_up, h1, wn, dh2)


def _adamw(w, g, m, v):
    m = ADAM_B1 * m + (1.0 - ADAM_B1) * g
    v = ADAM_B2 * v + (1.0 - ADAM_B2) * (g * g)
    m_hat = m / (1.0 - ADAM_B1 ** ADAM_STEP)
    v_hat = v / (1.0 - ADAM_B2 ** ADAM_STEP)
    delta = -ADAM_LR * (m_hat / (jnp.sqrt(v_hat) + ADAM_EPS) + ADAM_WD * w)
    return delta, m, v


def adamw_shard(name, recv, w, m, v, tr, tc):
    r, c = w.shape

    def body(p_ref, w_ref, m_ref, v_ref, g_ref, d_ref, mo_ref, vo_ref):
        g = p_ref[0].astype(F32)
        for s in range(1, 8):
            g = g + p_ref[s].astype(F32)
        g_ref[...] = g
        d_ref[...], mo_ref[...], vo_ref[...] = _adamw(w_ref[...], g, m_ref[...], v_ref[...])

    tile = _spec((tr, tc), lambda i, j: (i, j))
    shape = jax.ShapeDtypeStruct((r, c), F32)
    return pl.pallas_call(
        body, grid=(r // tr, c // tc), in_specs=[_spec((8, tr, tc), lambda i, j: (0, i, j)), tile, tile, tile],
        out_specs=[tile] * 4, out_shape=[shape] * 4, compiler_params=_params(2), name=name)(recv, w, m, v)


def sum_slabs(recv):
    def body(p_ref, o_ref):
        g = p_ref[0]
        for s in range(1, 8):
            g = g + p_ref[s]
        o_ref[...] = g

    return pl.pallas_call(body, out_shape=jax.ShapeDtypeStruct(recv.shape[1:], F32), compiler_params=_params(0), name="sum_slabs")(recv)


SIMPLE = [("norm1_w", 1024), ("ssd_conv_b", 1536), ("ssd_dt_bias", 16), ("ssd_a_log", 16), ("ssd_d", 16), ("ssd_norm_w", 1024),
          ("lru_conv_b", 1024), ("lru_ba", 1024), ("lru_bx", 1024), ("lru_lambda", 1024), ("lru_norm_w", 1024), ("norm2_w", 1024),
          ("final_norm_w", 1024)]
SPECIAL = ["lru_wa", "lru_wx", "meta_tokens", "ssd_conv_w", "lru_conv_w"]
SM_ROWS = 176
SM_WA, SM_WX, SM_META, SM_SCW, SM_LCW, SM_LOSS = 14, 78, 142, 158, 166, 170


def _simple_rows():
    rows, r = {}, 0
    for name, n in SIMPLE:
        rows[name] = r
        r += -(-n // 1024)
    return rows


def adamw_small(sm, special_g, ws, ms, vs):
    rows = _simple_rows()
    ns, nx = len(SIMPLE), len(SPECIAL)

    def body(*refs):
        sm_ref = refs[0]
        gx = refs[1:1 + nx]
        wr = refs[1 + nx:1 + nx + ns + nx]
        mr = refs[1 + nx + ns + nx:1 + nx + 2 * (ns + nx)]
        vr = refs[1 + nx + 2 * (ns + nx):1 + nx + 3 * (ns + nx)]
        outs = refs[1 + nx + 3 * (ns + nx):]
        o = 0
        for k, (name, n) in enumerate(SIMPLE):
            r0 = rows[name]
            for c0 in range(0, n, 1024):
                wd = min(1024, n - c0)
                g = sm_ref[r0 + c0 // 1024:r0 + c0 // 1024 + 1, 0:wd]
                sl = (slice(None), slice(c0, c0 + wd))
                d, m2, v2 = _adamw(wr[k][sl], g, mr[k][sl], vr[k][sl])
                outs[o][sl] = g
                outs[o + 1][sl] = d
                outs[o + 2][sl] = m2
                outs[o + 3][sl] = v2
            o += 4
        for k in range(nx):
            d, m2, v2 = _adamw(wr[ns + k][...], gx[k][...], mr[ns + k][...], vr[ns + k][...])
            outs[o][...] = d
            outs[o + 1][...] = m2
            outs[o + 2][...] = v2
            o += 3

    out_shape = []
    for k in range(ns):
        out_shape += [jax.ShapeDtypeStruct(ws[k].shape, F32)] * 4
    for k in range(nx):
        out_shape += [jax.ShapeDtypeStruct(ws[ns + k].shape, F32)] * 3
    return pl.pallas_call(body, out_shape=out_shape, compiler_params=_params(0), name="adamw_small")(sm, *special_g, *ws, *ms, *vs)


def _place():
    return lax.axis_index("x"), lax.axis_index("y"), lax.axis_index("c")


def _index(px, py, pc):
    return 4 * px + 2 * py + pc


def all_gather(name, shards):
    n = len(shards)
    hbm = pl.BlockSpec(memory_space=pl.ANY)

    def body(*refs):
        ins, outs = refs[:n], refs[n:2 * n]
        send_sems, recv_sems, local_sems = refs[2 * n:]
        x, y, c = _place()
        me, sibling = (x, y, c), (x, y, 1 - c)
        chips = [(1 - x, y), (x, 1 - y), (1 - x, 1 - y)]

        def copy(i, k, block, to, src=None):
            dst = outs[i].at[_index(*block)]
            return pltpu.make_async_remote_copy(src_ref=dst if src is None else src, dst_ref=dst, send_sem=send_sems.at[7 * i + k],
                                                recv_sem=recv_sems.at[7 * i + k], device_id=to, device_id_type=MESH)

        mine = [pltpu.make_async_copy(ins[i], outs[i].at[_index(*me)], local_sems.at[i]) for i in range(n)]
        for cp in mine:
            cp.start()
        first = []
        for i in range(n):
            first += [copy(i, 1 + j, me, (*chip, c), src=ins[i]) for j, chip in enumerate(chips)]
            first.append(copy(i, 0, me, sibling, src=ins[i]))
        for cp in first:
            cp.start()
        passed = []
        for i in range(n):
            for j, chip in enumerate(chips):
                copy(i, 1 + j, (*chip, c), me).wait_recv()
                cp = copy(i, 4 + j, (*chip, c), sibling)
                cp.start()
                passed.append(cp)
        for i in range(n):
            copy(i, 0, sibling, me).wait_recv()
            for j, chip in enumerate(chips):
                copy(i, 4 + j, (*chip, 1 - c), me).wait_recv()
        for cp in first + passed:
            cp.wait_send()
        for cp in mine:
            cp.wait()

    return pl.pallas_call(
        body, in_specs=[hbm] * n, out_specs=[hbm] * n,
        out_shape=[jax.ShapeDtypeStruct((8,) + s.shape, s.dtype) for s in shards],
        scratch_shapes=[pltpu.SemaphoreType.DMA((7 * n,)), pltpu.SemaphoreType.DMA((7 * n,)), pltpu.SemaphoreType.DMA((n,))],
        name=name)(*shards)


HBM_SPEC = pl.BlockSpec(memory_space=pltpu.HBM)
SEM_SPEC = pl.BlockSpec(memory_space=pltpu.SEMAPHORE)
EFFECT = pltpu.SideEffectType.DATAFLOW_SIDE_EFFECTING


def _peers(x, y, c):
    return [((1 - x) if k & 4 else x, (1 - y) if k & 2 else y, (1 - c) if k & 1 else c) for k in range(1, 8)]


def _pieces(rows):
    for n in (4, 2):
        if rows % (16 * n) == 0:
            return [(r * (rows // n), rows // n) for r in range(n)]
    return [(0, rows)]


def _peer_copies(src, land, send_sems, recv_sems, k, peer, mine, slab_src):
    block = src.at[_index(*peer)] if slab_src else src
    return [pltpu.make_async_remote_copy(src_ref=block.at[pl.ds(r0, nr)], dst_ref=land.at[mine, pl.ds(r0, nr)], send_sem=send_sems.at[k],
                                         recv_sem=recv_sems.at[k], device_id=peer, device_id_type=MESH)
            for r0, nr in _pieces(block.shape[0])]


def copies_start(name, srcs, slab_src, after):
    n = len(srcs)
    zones = [jax.ShapeDtypeStruct(s.shape if slab_src else (8,) + s.shape, s.dtype) for s in srcs]

    def body(*refs):
        ins, lands = refs[:n], refs[n:2 * n]
        sends, recvs = refs[2 * n + 1:3 * n + 1], refs[3 * n + 1:4 * n + 1]
        token = refs[-1]
        x, y, c = _place()
        mine = _index(x, y, c)
        for i in range(n):
            per_peer = [_peer_copies(ins[i], lands[i], sends[i], recvs[i], k, peer, mine, slab_src) for k, peer in enumerate(_peers(x, y, c))]
            for piece in zip(*per_peer):
                for cp in piece:
                    cp.start()
        token[...] = jnp.zeros_like(token)

    sem = pltpu.SemaphoreType.DMA((7,))
    res = pl.pallas_call(
        body, name=name,
        out_shape=([sem] * (2 * n) + [pltpu.HBM(s.shape, s.dtype) for s in srcs] + [pltpu.HBM(z.shape, z.dtype) for z in zones]
                   + [jax.ShapeDtypeStruct((8, 128), F32)]),
        in_specs=[HBM_SPEC] * (2 * n) + [pl.BlockSpec(memory_space=pl.ANY)],
        out_specs=[SEM_SPEC] * (2 * n) + [HBM_SPEC] * (2 * n) + [pl.BlockSpec(memory_space=pltpu.VMEM)],
        input_output_aliases={i: 2 * n + i for i in range(2 * n)},
        compiler_params=pltpu.CompilerParams(has_side_effects=EFFECT),
    )(*[pltpu.with_memory_space_constraint(s, pltpu.HBM) for s in srcs],
      *[pltpu.with_memory_space_constraint(lax.empty(z.shape, z.dtype), pltpu.HBM) for z in zones], after)
    return [(res[i], res[n + i], res[2 * n + i], res[3 * n + i]) for i in range(n)], res[-1][0:1, 0:1]


def copies_wait(name, started, slab_src, after):
    n = len(started)

    def body(*refs):
        ins, lands = refs[:n], refs[n:2 * n]
        sends, recvs = refs[2 * n:3 * n], refs[3 * n:4 * n]
        x, y, c = _place()
        mine = _index(x, y, c)
        for i in range(n):
            for k, peer in enumerate(_peers(x, y, c)):
                arrival = pltpu.make_async_remote_copy(src_ref=ins[i].at[mine] if slab_src else ins[i], dst_ref=lands[i].at[_index(*peer)],
                                                       send_sem=sends[i].at[k], recv_sem=recvs[i].at[k], device_id=peer, device_id_type=MESH)
                arrival.wait_send()
                arrival.wait_recv()

    srcs = [s[2] for s in started]
    lands = [s[3] for s in started]
    afters = list(after) if isinstance(after, (list, tuple)) else [after]
    res = pl.pallas_call(
        body, name=name,
        out_shape=[pltpu.HBM(s.shape, s.dtype) for s in srcs] + [pltpu.HBM(z.shape, z.dtype) for z in lands],
        in_specs=[HBM_SPEC] * (2 * n) + [SEM_SPEC] * (2 * n) + [pl.BlockSpec(memory_space=pl.ANY)] * len(afters),
        out_specs=[HBM_SPEC] * (2 * n),
        input_output_aliases={i: i for i in range(2 * n)},
        compiler_params=pltpu.CompilerParams(has_side_effects=EFFECT),
    )(*srcs, *lands, *[s[0] for s in started], *[s[1] for s in started], *afters)
    me = _index(*_place())
    own = [lax.dynamic_index_in_dim(s, me, 0, keepdims=True) if slab_src else s[None] for s in res[:n]]
    return [lax.dynamic_update_slice_in_dim(z, o, me, 0) for z, o in zip(res[n:], own)]


WEIGHTS = ["meta_tokens", "norm1_w", "w_in", "ssd_conv_w", "ssd_conv_b", "ssd_dt_bias", "ssd_a_log", "ssd_d", "ssd_norm_w", "lru_conv_w",
           "lru_conv_b", "lru_wa", "lru_ba", "lru_wx", "lru_bx", "lru_lambda", "lru_norm_w", "w_out", "norm2_w", "w_gate", "w_up", "w_down",
           "final_norm_w"]
BIG = ["w_in", "w_out", "w_gate", "w_up", "w_down"]
COLUMN_SHARDED = ["w_in", "w_gate", "w_up"]
BIG_TILE = {"w_in": (578, 256), "w_out": (128, 1024), "w_gate": (176, 1024), "w_up": (176, 1024), "w_down": (176, 1024)}


def _pair_blocks(w):
    w = w.reshape(8, 2, 64, 64)
    z = jnp.zeros((8, 64, 64), w.dtype)
    return jnp.concatenate([jnp.concatenate([w[:, 0], z], axis=2), jnp.concatenate([z, w[:, 1]], axis=2)], axis=1)


def _unpair_blocks(w2):
    return jnp.stack([w2[:, :64, :64], w2[:, 64:, 64:]], axis=1).reshape(16, 64, 64)


def _per_group(v):
    return jnp.pad(v.reshape(2, 1, 8), ((0, 0), (0, 0), (0, 120)))


def _pad_cols(v, n):
    return jnp.pad(v, ((0, 0), (0, n - v.shape[1])))


def local_step(x, target, meta, ssd_cw, lru_cw, w_in, fetch, send, p):
    z120 = jnp.zeros((120, D), BF)
    w_dt = jnp.concatenate([w_in[2560:2568], z120, w_in[2568:2576], z120], axis=0)
    bias2, alog2, d2 = _per_group(p["ssd_dt_bias"]), _per_group(p["ssd_a_log"]), _per_group(p["ssd_d"])
    wa2 = _pair_blocks(p["lru_wa"]).astype(BF)
    wx2 = _pair_blocks(p["lru_wx"]).astype(BF)
    lru = (lru_cw, p["lru_conv_b"], wa2, p["lru_ba"], wx2, p["lru_bx"], p["lru_lambda"])

    h0 = jnp.concatenate([jnp.zeros((NPAD, D), F32), meta, x], axis=0)
    proj, u1 = in_proj(h0, p["norm1_w"], w_in, w_dt)
    xbc_act = conv_silu_fwd(proj, ssd_cw, p["ssd_conv_b"])
    yn_ssd, y_pre, h_prev = ssd_fwd(xbc_act, proj, bias2, alog2, d2, p["ssd_norm_w"])
    a, u = lru_gates_fwd(proj, *lru)
    hseq = lru_scan_fwd(a, u)
    (w_out,) = fetch(["w_out"], hseq)
    h1, cat = out_proj(yn_ssd, proj, hseq, p["lru_norm_w"], w_out, h0)
    w_gate, w_up = fetch(["w_gate", "w_up"], h1)
    gt, up, act, u2 = gate_up(h1, p["norm2_w"], w_gate, w_up)
    (w_down,) = fetch(["w_down"], act)
    dh2, dh2_b, loss, d_fnw = down_loss(act, w_down, h1, target, p["final_norm_w"])

    dgt, dup = swiglu_bwd(dh2_b, w_down, gt, up)
    g_down = matmul_tn("dw_down", act, dh2_b, 1408, 512)
    g_gate = matmul_tn("dw_gate", dgt, u2, 1408, 512)
    g_up = matmul_tn("dw_up", dup, u2, 1408, 512)
    sent = send({"w_down": g_down, "w_gate": g_gate, "w_up": g_up})
    dh1, dh1_b, d_n2 = gate_up_bwd(dgt, dup, w_gate, w_up, h1, p["norm2_w"] + sent, dh2)
    sent = send({"w_out": matmul_tn("dw_out", cat, dh1_b, 512, 1024)})
    dyn, dh_out, dg_b, d_lnw = out_proj_bwd(dh1_b, w_out, proj, hseq, p["lru_norm_w"] + sent)

    dhs = lru_scan_bwd(a, dh_out)
    dxl_b, d_lcw, d_lcb, dwa2, d_ba, dwx2, d_bx, d_lam = lru_gates_bwd(dhs, hseq, proj, *lru)
    dz_b, dx, d_b, d_c, ddt_b, dpar, d_snw = ssd_bwd(dyn, xbc_act, proj, y_pre, h_prev, bias2, alog2, d2, p["ssd_norm_w"])
    dxbc_b, d_scw, d_scb = conv_silu_bwd(dx, d_b, d_c, proj, ssd_cw, p["ssd_conv_b"])
    g_dt = matmul_tn("dw_in_dt", ddt_b, u1, 256, 512)
    g_in = jnp.concatenate([matmul_tn("dw_in_z", dz_b, u1, 512, 1024), matmul_tn("dw_in_xbc", dxbc_b, u1, 512, 1024), g_dt[0:8], g_dt[128:136],
                            matmul_tn("dw_in_g", dg_b, u1, 512, 1024), matmul_tn("dw_in_xl", dxl_b, u1, 512, 1024)], axis=0)
    sent = send({"w_in": g_in})
    dh0, d_n1 = in_proj_bwd(dz_b, dg_b, dxl_b, dxbc_b, ddt_b, w_in, w_dt, h0, p["norm1_w"] + sent, dh1)
    small = {"norm1_w": d_n1, "ssd_conv_b": d_scb, "ssd_dt_bias": dpar[:, 0, :8].reshape(1, 16), "ssd_a_log": dpar[:, 1, :8].reshape(1, 16),
             "ssd_d": dpar[:, 2, :8].reshape(1, 16), "ssd_norm_w": d_snw, "lru_conv_b": d_lcb, "lru_ba": d_ba, "lru_bx": d_bx,
             "lru_lambda": d_lam, "lru_norm_w": d_lnw, "norm2_w": d_n2, "final_norm_w": d_fnw,
             "lru_wa": _unpair_blocks(dwa2), "lru_wx": _unpair_blocks(dwx2), "meta_tokens": dh0[NPAD:NPAD + N_META],
             "ssd_conv_w": d_scw, "lru_conv_w": d_lcw}
    return loss, dh0[NPAD + N_META:], small


def _pack_small(small, loss):
    rows = [_pad_cols(small[name], -(-n // 1024) * 1024).reshape(-1, 1024) for name, n in SIMPLE]
    rows += [small["lru_wa"].reshape(64, 1024), small["lru_wx"].reshape(64, 1024), small["meta_tokens"],
             _pad_cols(small["ssd_conv_w"], 2048).reshape(8, 1024), small["lru_conv_w"], _pad_cols(loss[:, 0:1], 1024)]
    sm = jnp.concatenate(rows, axis=0)
    return jnp.pad(sm, ((0, SM_ROWS - sm.shape[0]), (0, 0)))


def _slabs(g):
    return g.reshape(8, g.shape[0] // 8, g.shape[1])


def _unslab(g):
    return g.reshape(8 * g.shape[1], g.shape[2])


def kernel(x, meta_tokens, norm1_w, w_in, ssd_conv_w, ssd_conv_b, ssd_dt_bias, ssd_a_log, ssd_d, ssd_norm_w, lru_conv_w, lru_conv_b, lru_wa, lru_ba, lru_wx, lru_bx, lru_lambda, lru_norm_w, w_out, norm2_w, w_gate, w_up, w_down, final_norm_w, loss_target, m_meta_tokens, m_norm1_w, m_w_in, m_ssd_conv_w, m_ssd_conv_b, m_ssd_dt_bias, m_ssd_a_log, m_ssd_d, m_ssd_norm_w, m_lru_conv_w, m_lru_conv_b, m_lru_wa, m_lru_ba, m_lru_wx, m_lru_bx, m_lru_lambda, m_lru_norm_w, m_w_out, m_norm2_w, m_w_gate, m_w_up, m_w_down, m_final_norm_w, v_meta_tokens, v_norm1_w, v_w_in, v_ssd_conv_w, v_ssd_conv_b, v_ssd_dt_bias, v_ssd_a_log, v_ssd_d, v_ssd_norm_w, v_lru_conv_w, v_lru_conv_b, v_lru_wa, v_lru_ba, v_lru_wx, v_lru_bx, v_lru_lambda, v_lru_norm_w, v_w_out, v_norm2_w, v_w_gate, v_w_up, v_w_down, v_final_norm_w):
    w = dict(meta_tokens=meta_tokens, norm1_w=norm1_w, w_in=w_in[0], ssd_conv_w=ssd_conv_w[0], ssd_conv_b=ssd_conv_b, ssd_dt_bias=ssd_dt_bias,
             ssd_a_log=ssd_a_log, ssd_d=ssd_d, ssd_norm_w=ssd_norm_w, lru_conv_w=lru_conv_w[0], lru_conv_b=lru_conv_b, lru_wa=lru_wa[0],
             lru_ba=lru_ba, lru_wx=lru_wx[0], lru_bx=lru_bx, lru_lambda=lru_lambda, lru_norm_w=lru_norm_w, w_out=w_out[0], norm2_w=norm2_w,
             w_gate=w_gate[0], w_up=w_up[0], w_down=w_down[0], final_norm_w=final_norm_w.reshape(1, D))
    m = dict(meta_tokens=m_meta_tokens, norm1_w=m_norm1_w, w_in=m_w_in[0], ssd_conv_w=m_ssd_conv_w[0], ssd_conv_b=m_ssd_conv_b,
             ssd_dt_bias=m_ssd_dt_bias, ssd_a_log=m_ssd_a_log, ssd_d=m_ssd_d, ssd_norm_w=m_ssd_norm_w, lru_conv_w=m_lru_conv_w[0],
             lru_conv_b=m_lru_conv_b, lru_wa=m_lru_wa[0], lru_ba=m_lru_ba, lru_wx=m_lru_wx[0], lru_bx=m_lru_bx, lru_lambda=m_lru_lambda,
             lru_norm_w=m_lru_norm_w, w_out=m_w_out[0], norm2_w=m_norm2_w, w_gate=m_w_gate[0], w_up=m_w_up[0], w_down=m_w_down[0],
             final_norm_w=m_final_norm_w.reshape(1, D))
    v = dict(meta_tokens=v_meta_tokens, norm1_w=v_norm1_w, w_in=v_w_in[0], ssd_conv_w=v_ssd_conv_w[0], ssd_conv_b=v_ssd_conv_b,
             ssd_dt_bias=v_ssd_dt_bias, ssd_a_log=v_ssd_a_log, ssd_d=v_ssd_d, ssd_norm_w=v_ssd_norm_w, lru_conv_w=v_lru_conv_w[0],
             lru_conv_b=v_lru_conv_b, lru_wa=v_lru_wa[0], lru_ba=v_lru_ba, lru_wx=v_lru_wx[0], lru_bx=v_lru_bx, lru_lambda=v_lru_lambda,
             lru_norm_w=v_lru_norm_w, w_out=v_w_out[0], norm2_w=v_norm2_w, w_gate=v_w_gate[0], w_up=v_w_up[0], w_down=v_w_down[0],
             final_norm_w=v_final_norm_w.reshape(1, D))
    shapes = dict(meta_tokens=meta_tokens.shape, norm1_w=norm1_w.shape, w_in=w_in.shape, ssd_conv_w=ssd_conv_w.shape,
                  ssd_conv_b=ssd_conv_b.shape, ssd_dt_bias=ssd_dt_bias.shape, ssd_a_log=ssd_a_log.shape, ssd_d=ssd_d.shape,
                  ssd_norm_w=ssd_norm_w.shape, lru_conv_w=lru_conv_w.shape, lru_conv_b=lru_conv_b.shape, lru_wa=lru_wa.shape,
                  lru_ba=lru_ba.shape, lru_wx=lru_wx.shape, lru_bx=lru_bx.shape, lru_lambda=lru_lambda.shape, lru_norm_w=lru_norm_w.shape,
                  w_out=w_out.shape, norm2_w=norm2_w.shape, w_gate=w_gate.shape, w_up=w_up.shape, w_down=w_down.shape,
                  final_norm_w=final_norm_w.shape)
    me = _index(*_place())
    for n in COLUMN_SHARDED:
        w[n], m[n], v[n] = w[n].T, m[n].T, v[n].T

    small_shard = jnp.concatenate([w["meta_tokens"], _pad_cols(w["ssd_conv_w"], 256).reshape(8, 128), w["lru_conv_w"],
                                   jnp.zeros((4, 128), F32)], axis=0)
    g_in, gs = all_gather("gather_w_in", [w["w_in"].astype(BF), small_shard])
    later = ["w_out", "w_gate", "w_up", "w_down"]
    started, behind = copies_start("gather_rest_start", [w[n].astype(BF) for n in later], False, gs)
    started = dict(zip(later, started))
    meta_full = gs[:, 0:16].transpose(1, 0, 2).reshape(N_META, D)
    ssd_cw = gs[:, 16:24].reshape(8, 4, 256)[:, :, :192].transpose(1, 0, 2).reshape(4, XBC)
    lru_cw = gs[:, 24:28].transpose(1, 0, 2).reshape(4, LRU_W)

    def fetch(names, after):
        got = copies_wait("gather_" + names[0] + "_wait", [started[n] for n in names], False, after)
        return [_unslab(g) for g in got]

    in_flight = {}

    def send(grads):
        names = list(grads)
        st, token = copies_start("grads_" + names[0] + "_start", [grads[n] if n == "small" else _slabs(grads[n]) for n in names], True,
                                 grads[names[0]])
        in_flight.update(zip(names, st))
        return token

    loss, grad_x, small = local_step(x[0], loss_target[0], meta_full, ssd_cw, lru_cw, _unslab(g_in), fetch, send,
                                     {**w, "norm1_w": w["norm1_w"] + behind})
    send({"small": _pack_small(small, loss).reshape(8, SM_ROWS // 8, 1024)})

    out = {}
    early = ["w_down", "w_gate", "w_up", "w_out"]
    recv = dict(zip(early + ["small"], copies_wait("grads_early_wait", [in_flight[n] for n in early + ["small"]], True, in_flight["small"][2])))
    sm = all_gather("gather_small_grads", [sum_slabs(recv["small"])])[0].reshape(SM_ROWS, 1024)
    for n in early:
        out[n] = adamw_shard("adamw_" + n, recv[n], w[n], m[n], v[n], *BIG_TILE[n])
    special_g = [sm[SM_WA:SM_WA + 64].reshape(16, 64, 64), sm[SM_WX:SM_WX + 64].reshape(16, 64, 64),
                 lax.dynamic_slice(sm[SM_META:SM_META + 16], (0, 128 * me), (16, 128)),
                 lax.dynamic_slice(sm[SM_SCW:SM_SCW + 8].reshape(4, 2048), (0, 192 * me), (4, 192)),
                 lax.dynamic_slice(sm[SM_LCW:SM_LCW + 4], (0, 128 * me), (4, 128))]
    names = [n for n, _ in SIMPLE] + SPECIAL
    res = adamw_small(sm, special_g, [w[n] for n in names], [m[n] for n in names], [v[n] for n in names])
    for k, (n, _) in enumerate(SIMPLE):
        out[n] = res[4 * k:4 * k + 4]
    for k, n in enumerate(SPECIAL):
        o = 4 * len(SIMPLE) + 3 * k
        out[n] = [special_g[k]] + list(res[o:o + 3])
    (recv_in,) = copies_wait("grads_late_wait", [in_flight["w_in"]], True, [out[n][0] for n in early] + [res[0]])
    out["w_in"] = adamw_shard("adamw_w_in", recv_in, w["w_in"], m["w_in"], v["w_in"], *BIG_TILE["w_in"])
    for n in COLUMN_SHARDED:
        out[n] = [o.T for o in out[n]]
    loss_total = sm[SM_LOSS, 0]
    flat = [loss_total, grad_x[None]]
    for k in range(4):
        flat += [out[n][k].reshape(shapes[n]) for n in WEIGHTS]
    return tuple(flat)
```

```python
import math

import jax
import jax.numpy as jnp
from jax import lax
from jax.experimental import pallas as pl
from jax.experimental.pallas import tpu as pltpu

F32 = jnp.float32
BF = jnp.bfloat16

D = 1024
SEQ = 2048
N_META = 16
Q = 128
NPAD = 112
T = NPAD + N_META + SEQ
NCH = T // Q
RC = 544
D_FF = 2816
SSD_W = 1024
LRU_W = 1024
XBC = 1536
IN_COLS = 4624
PZ, PG, PXL, PXBC, PDT = 0, 1024, 2048, 3072, 4608
NP_IN = 4864
EPS = 1e-6
LRU_C = 8.0
VMEM_LIMIT = 56 * 1024 * 1024

ADAM_LR, ADAM_B1, ADAM_B2, ADAM_EPS, ADAM_WD, ADAM_STEP = 0.001, 0.9, 0.999, 1e-08, 0.01, 10

NT_DIMS = (((1,), (1,)), ((), ()))
TN_DIMS = (((0,), (0,)), ((), ()))
MESH = pl.DeviceIdType.MESH


def _params(n_grid=1, limit=VMEM_LIMIT):
    return pltpu.CompilerParams(dimension_semantics=("arbitrary",) * n_grid, vmem_limit_bytes=limit)


def _spec(shape, imap, single=False):
    if single:
        return pl.BlockSpec(shape, imap, pipeline_mode=pl.Buffered(1))
    return pl.BlockSpec(shape, imap)


def _sigmoid(x):
    return 0.5 * jnp.tanh(0.5 * x) + 0.5


def _sigmoid_gate(x):
    return 1.0 / (1.0 + jnp.exp(-x))


def _softplus(x):
    return jnp.maximum(x, 0.0) + jnp.log(1.0 + jnp.exp(-jnp.abs(x)))


def _rms_stats(h):
    return lax.rsqrt(jnp.mean(h * h, axis=-1, keepdims=True) + EPS)


def _rms(h, w):
    return (h * _rms_stats(h)) * w


def _rms_bwd(du, h, w):
    r = _rms_stats(h)
    n = h * r
    dn = du * w
    dh = r * (dn - n * jnp.mean(dn * n, axis=-1, keepdims=True))
    return dh, du * n


_G0 = math.sqrt(2.0 / math.pi)


def _gelu(x):
    return 0.5 * x * (1.0 + jnp.tanh(_G0 * (x + 0.044715 * (x * x * x))))


def _gelu_grad(x):
    t = jnp.tanh(_G0 * (x + 0.044715 * (x * x * x)))
    return 0.5 * (1.0 + t) + 0.5 * x * (1.0 - t * t) * (_G0 * (1.0 + 3.0 * 0.044715 * (x * x)))


def _rows(shape, r0=0):
    return lax.broadcasted_iota(jnp.int32, shape, 0) + r0


def _lanes(shape):
    return lax.broadcasted_iota(jnp.int32, shape, 1)


HALO = 8


def _fill_padded(pad_ref, x_ref):
    pad_ref[0:HALO, :] = jnp.zeros((HALO, pad_ref.shape[1]), F32)
    pad_ref[T + HALO:T + 2 * HALO, :] = jnp.zeros((HALO, pad_ref.shape[1]), F32)

    def step(c, carry):
        r0 = pl.multiple_of(c * Q, Q)
        pad_ref[pl.ds(r0 + HALO, Q), :] = x_ref[pl.ds(r0, Q), :]
        return carry

    lax.fori_loop(0, NCH, step, 0)


def _back(pad_ref, r0):
    win = pad_ref[pl.ds(r0, Q + HALO), :]
    return lambda s: win[HALO:, :] if s == 0 else pltpu.roll(win, s, axis=0)[HALO:, :]


def _ahead(pad_ref, r0):
    win = pad_ref[pl.ds(r0 + HALO, Q + HALO), :]
    return lambda s: win[:Q, :] if s == 0 else pltpu.roll(win, Q + HALO - s, axis=0)[:Q, :]


def _conv(back, w, b):
    y = b + w[3:4, :] * back(0)
    for k in range(3):
        y = y + w[k:k + 1, :] * back(3 - k)
    return y


def _conv_bwd_x(ahead, w):
    dx = w[3:4, :] * ahead(0)
    for k in range(3):
        dx = dx + w[k:k + 1, :] * ahead(3 - k)
    return dx


def _conv_bwd_w(dy, back):
    dws = [jnp.sum(dy * back(3 - k), axis=0, keepdims=True) for k in range(4)]
    return jnp.concatenate(dws, axis=0), jnp.sum(dy, axis=0, keepdims=True)


def _chunks(fn, unrolled=False):
    if unrolled:
        for c in range(NCH):
            fn(c * Q)
        return

    def step(c, carry):
        fn(pl.multiple_of(c * Q, Q))
        return carry

    lax.fori_loop(0, NCH, step, 0)


HALF = RC // 2


def _col_tiles(n, tn, fn):
    def step(j, carry):
        fn(pl.multiple_of(j * tn, tn))
        return carry

    lax.fori_loop(0, n // tn, step, 0)


def _rows_spec(cols, block_col=0):
    return _spec((RC, cols), lambda i: (i, block_col))


def _whole(shape):
    return _spec(shape, lambda i: tuple(0 for _ in shape), single=True)


def _vec(cols):
    return _spec((1, cols), lambda i: (0, 0))


def _zero_at_first(*refs):
    @pl.when(pl.program_id(0) == 0)
    def _():
        for r in refs:
            r[...] = jnp.zeros_like(r)


IN_RUNS = ((PZ, 0, 1024), (PG, 2576, 2048), (PXBC, 1024, XBC))


def _in_tiles(fn):
    for pcol, wrow, width in IN_RUNS:
        def step(j, carry, pcol=pcol, wrow=wrow):
            fn(pl.multiple_of(pcol + j * 512, 512), pl.multiple_of(wrow + j * 512, 16))
            return carry

        lax.fori_loop(0, width // 512, step, 0)


def in_proj(h0, wn, w_t, w_dt):
    def body(h_ref, wn_ref, w_ref, wdt_ref, o_ref, u_ref):
        for r in (0, HALF):
            u_ref[r:r + HALF, :] = _rms(h_ref[r:r + HALF, :], wn_ref[...]).astype(BF)

        def tile(pcol, wrow):
            o_ref[:, pl.ds(pcol, 512)] = lax.dot_general(u_ref[...], w_ref[pl.ds(wrow, 512), :], NT_DIMS, preferred_element_type=F32)

        _in_tiles(tile)
        o_ref[:, PDT:PDT + 256] = lax.dot_general(u_ref[...], wdt_ref[...], NT_DIMS, preferred_element_type=F32)

    return pl.pallas_call(
        body, grid=(T // RC,), in_specs=[_rows_spec(D), _vec(D), _whole((IN_COLS, D)), _whole((256, D))],
        out_specs=[_rows_spec(NP_IN), _rows_spec(D)],
        out_shape=[jax.ShapeDtypeStruct((T, NP_IN), F32), jax.ShapeDtypeStruct((T, D), BF)],
        compiler_params=_params(), name="in_proj")(h0, wn, w_t, w_dt)


def out_proj(yn_ssd, proj, hseq, lru_nw, w_out, h0):
    def body(y_ref, g_ref, h_ref, wn_ref, w_ref, r_ref, o_ref, cat_ref):
        cat_ref[:, 0:SSD_W] = y_ref[...]
        for r in (0, HALF):
            y = _gelu(g_ref[r:r + HALF, :]) * h_ref[r:r + HALF, :]
            cat_ref[r:r + HALF, SSD_W:] = _rms(y, wn_ref[...]).astype(BF)

        def tile(c0):
            o_ref[:, pl.ds(c0, 512)] = r_ref[:, pl.ds(c0, 512)] + jnp.dot(cat_ref[...], w_ref[:, pl.ds(c0, 512)], preferred_element_type=F32)

        _col_tiles(D, 512, tile)

    return pl.pallas_call(
        body, grid=(T // RC,),
        in_specs=[_rows_spec(SSD_W), _rows_spec(LRU_W, PG // LRU_W), _rows_spec(LRU_W), _vec(LRU_W), _whole((SSD_W + LRU_W, D)), _rows_spec(D)],
        out_specs=[_rows_spec(D), _rows_spec(SSD_W + LRU_W)],
        out_shape=[jax.ShapeDtypeStruct((T, D), F32), jax.ShapeDtypeStruct((T, SSD_W + LRU_W), BF)],
        compiler_params=_params(), name="out_proj")(yn_ssd, proj, hseq, lru_nw, w_out, h0)


def out_proj_bwd(dh1_b, w_out, proj, hseq, lru_nw):
    def body(d_ref, w_ref, g_ref, h_ref, wn_ref, dy_ref, dh_ref, dg_ref, dw_ref, dl_scr):
        _zero_at_first(dw_ref)

        def tile(c0):
            dy_ref[:, pl.ds(c0, 512)] = lax.dot_general(d_ref[...], w_ref[pl.ds(c0, 512), :], NT_DIMS, preferred_element_type=F32)
            dl_scr[:, pl.ds(c0, 512)] = lax.dot_general(d_ref[...], w_ref[pl.ds(SSD_W + c0, 512), :], NT_DIMS, preferred_element_type=F32)

        _col_tiles(SSD_W, 512, tile)
        for r in (0, HALF):
            g = g_ref[r:r + HALF, :]
            h = h_ref[r:r + HALF, :]
            ge = _gelu(g)
            dy, dw = _rms_bwd(dl_scr[r:r + HALF, :], ge * h, wn_ref[...])
            dw_ref[...] += jnp.sum(dw, axis=0, keepdims=True)
            dh_ref[r:r + HALF, :] = dy * ge
            dg_ref[r:r + HALF, :] = (dy * h * _gelu_grad(g)).astype(BF)

    return pl.pallas_call(
        body, grid=(T // RC,),
        in_specs=[_rows_spec(D), _whole((SSD_W + LRU_W, D)), _rows_spec(LRU_W, PG // LRU_W), _rows_spec(LRU_W), _vec(LRU_W)],
        out_specs=[_rows_spec(SSD_W), _rows_spec(LRU_W), _rows_spec(LRU_W), _vec(LRU_W)],
        out_shape=[jax.ShapeDtypeStruct((T, SSD_W), F32), jax.ShapeDtypeStruct((T, LRU_W), F32), jax.ShapeDtypeStruct((T, LRU_W), BF),
                   jax.ShapeDtypeStruct((1, LRU_W), F32)],
        scratch_shapes=[pltpu.VMEM((RC, LRU_W), F32)],
        compiler_params=_params(), name="out_proj_bwd")(dh1_b, w_out, proj, hseq, lru_nw)


def in_proj_bwd(dz, dg, dxl, dxbc, ddt, w_t, w_dt, h0, wn, dh1):
    def body(dz_ref, dg_ref, dxl_ref, dxbc_ref, ddt_ref, w_ref, wdt_ref, h_ref, wn_ref, r_ref, o_ref, dw_ref, du_scr):
        _zero_at_first(dw_ref)
        du_scr[...] = jnp.dot(ddt_ref[...], wdt_ref[...], preferred_element_type=F32)
        for d_ref, wrow, width in ((dz_ref, 0, 1024), (dxbc_ref, 1024, XBC), (dg_ref, 2576, 1024), (dxl_ref, 3600, 1024)):
            def step(j, carry, d_ref=d_ref, wrow=wrow):
                c0 = pl.multiple_of(j * 512, 512)
                du_scr[...] += jnp.dot(d_ref[:, pl.ds(c0, 512)], w_ref[pl.ds(pl.multiple_of(wrow + c0, 16), 512), :], preferred_element_type=F32)
                return carry

            lax.fori_loop(0, width // 512, step, 0)
        for r in (0, HALF):
            dh, dw = _rms_bwd(du_scr[r:r + HALF, :], h_ref[r:r + HALF, :], wn_ref[...])
            dw_ref[...] += jnp.sum(dw, axis=0, keepdims=True)
            o_ref[r:r + HALF, :] = dh + r_ref[r:r + HALF, :]

    return pl.pallas_call(
        body, grid=(T // RC,),
        in_specs=[_rows_spec(SSD_W), _rows_spec(LRU_W), _rows_spec(LRU_W), _rows_spec(XBC), _rows_spec(256), _whole((IN_COLS, D)),
                  _whole((256, D)), _rows_spec(D), _vec(D), _rows_spec(D)],
        out_specs=[_rows_spec(D), _vec(D)],
        out_shape=[jax.ShapeDtypeStruct((T, D), F32), jax.ShapeDtypeStruct((1, D), F32)],
        scratch_shapes=[pltpu.VMEM((RC, D), F32)],
        compiler_params=_params(), name="in_proj_bwd")(dz, dg, dxl, dxbc, ddt, w_t, w_dt, h0, wn, dh1)


def matmul_tn(name, a, b, tm, tn):
    m, n = a.shape[1], b.shape[1]

    def body(a_ref, b_ref, o_ref, acc_ref):
        acc_ref[...] = jnp.zeros_like(acc_ref)

        def mm(r0):
            acc_ref[...] += lax.dot_general(a_ref[pl.ds(r0, RC), :], b_ref[pl.ds(r0, RC), :], TN_DIMS, preferred_element_type=F32)

        _col_tiles(T, RC, mm)
        o_ref[...] = acc_ref[...].astype(BF)

    return pl.pallas_call(
        body, grid=(m // tm, n // tn),
        in_specs=[_spec((T, tm), lambda i, j: (0, i)), _spec((T, tn), lambda i, j: (0, j))],
        out_specs=_spec((tm, tn), lambda i, j: (i, j)),
        out_shape=jax.ShapeDtypeStruct((m, n), BF),
        scratch_shapes=[pltpu.VMEM((tm, tn), F32)],
        compiler_params=_params(2), name=name)(a, b)


def conv_silu_fwd(proj, cw, cb):
    def body(x_ref, w_ref, b_ref, o_ref, xpad):
        _fill_padded(xpad, x_ref)

        def chunk(r0):
            pre = _conv(_back(xpad, r0), w_ref[...], b_ref[...])
            o_ref[pl.ds(r0, Q), :] = pre * _sigmoid(pre)

        _chunks(chunk)

    c0 = PXBC // 128
    return pl.pallas_call(
        body, grid=(XBC // 128,),
        in_specs=[_spec((T, 128), lambda c: (0, c0 + c)), _spec((4, 128), lambda c: (0, c)), _spec((1, 128), lambda c: (0, c))],
        out_specs=_spec((T, 128), lambda c: (0, c)),
        out_shape=jax.ShapeDtypeStruct((T, XBC), F32), scratch_shapes=[pltpu.VMEM((T + 2 * HALO, 128), F32)],
        compiler_params=_params(), name="conv_silu_fwd")(proj, cw, cb)


def conv_silu_bwd(dx, d_b, d_c, proj, cw, cb):
    def body(dx_ref, db_ref, dc_ref, x_ref, w_ref, b_ref, o_ref, dw_ref, dbias_ref, xpad, dpad):
        tile = pl.program_id(0)
        _fill_padded(xpad, x_ref)
        dpad[0:HALO, :] = jnp.zeros((HALO, 128), F32)
        dpad[T + HALO:T + 2 * HALO, :] = jnp.zeros((HALO, 128), F32)
        dw_ref[...] = jnp.zeros_like(dw_ref)
        dbias_ref[...] = jnp.zeros_like(dbias_ref)

        def first(r0):
            back = _back(xpad, r0)
            pre = _conv(back, w_ref[...], b_ref[...])
            sg = _sigmoid(pre)
            rows = pl.ds(r0, Q)
            d = jnp.where(tile < 8, dx_ref[rows, :], jnp.where(tile < 10, db_ref[rows, :], dc_ref[rows, :]))
            dpre = d * (sg * (1.0 + pre * (1.0 - sg)))
            dpad[pl.ds(r0 + HALO, Q), :] = dpre
            dw, dbias = _conv_bwd_w(dpre, back)
            dw_ref[...] += dw
            dbias_ref[...] += dbias

        _chunks(first)

        def second(r0):
            o_ref[pl.ds(r0, Q), :] = _conv_bwd_x(_ahead(dpad, r0), w_ref[...]).astype(BF)

        _chunks(second)

    c0 = PXBC // 128
    pad = pltpu.VMEM((T + 2 * HALO, 128), F32)
    return pl.pallas_call(
        body, grid=(XBC // 128,),
        in_specs=[_spec((T, 128), lambda c: (0, jnp.minimum(c, 7))), _spec((T, 128), lambda c: (0, jnp.clip(c - 8, 0, 1))),
                  _spec((T, 128), lambda c: (0, jnp.clip(c - 10, 0, 1))), _spec((T, 128), lambda c: (0, c0 + c)),
                  _spec((4, 128), lambda c: (0, c)), _spec((1, 128), lambda c: (0, c))],
        out_specs=[_spec((T, 128), lambda c: (0, c)), _spec((4, 128), lambda c: (0, c)), _spec((1, 128), lambda c: (0, c))],
        out_shape=[jax.ShapeDtypeStruct((T, XBC), BF), jax.ShapeDtypeStruct((4, XBC), F32), jax.ShapeDtypeStruct((1, XBC), F32)],
        scratch_shapes=[pad, pad], compiler_params=_params(), name="conv_silu_bwd")(dx, d_b, d_c, proj, cw, cb)


def _ssd_chunk_common(row0, dt_ref, b_ref, c_ref, bias, a_neg):
    shape = (Q, Q)
    lane = _lanes(shape)
    sub = _rows(shape)
    live = (_rows(shape, row0) >= NPAD) & (lane < 8)
    dtr = dt_ref[:, :]
    dt = jnp.where(live, _softplus(dtr + bias), 0.0)
    d_a = dt * a_neg
    tri = (sub >= lane).astype(F32)
    cs = jnp.dot(tri, d_a, precision=lax.Precision.HIGHEST, preferred_element_type=F32)
    cs_t = cs.T
    b_f = b_ref[:, :]
    bc = b_f.astype(BF)
    cc = c_ref[:, :].astype(BF)
    cb = lax.dot_general(cc, bc, NT_DIMS, preferred_element_type=F32)
    cs_last = cs[Q - 1:Q, :]
    return dict(lane=lane, sub=sub, live=live, dtr=dtr, dt=dt, cs=cs, cs_t=cs_t, bc=bc, cc=cc, cb=cb, bc_t=b_f.T.astype(BF),
                ecs=jnp.exp(cs), dsm=jnp.exp(cs_last - cs), gam=jnp.exp(cs_last))


def _pair(lane_even, mat, j):
    return jnp.where(lane_even, mat[:, j:j + 1], mat[:, j + 1:j + 2])


def _pair_row(lane_even, mat, j):
    return jnp.where(lane_even[0:1, :], mat[:, j:j + 1], mat[:, j + 1:j + 2])


def _head_decay(cm, j):
    seg = cm["cs"][:, j:j + 1] - cm["cs_t"][j:j + 1, :]
    return jnp.exp(jnp.where(cm["sub"] >= cm["lane"], seg, -jnp.inf))


def _head_decay_t(cm, j):
    seg = cm["cs_t"][j:j + 1, :] - cm["cs"][:, j:j + 1]
    return jnp.exp(jnp.where(cm["lane"] >= cm["sub"], seg, -jnp.inf))


def ssd_fwd(xbc_act, proj, dt_bias2, a_log2, d2, norm_w):
    def body(x_ref, b_ref, c_ref, dt_ref, z_ref, bias_ref, alog_ref, d_ref, nw_ref, yn_ref, y_ref, hp_ref, h_scr):
        c = pl.program_id(1)

        @pl.when(c == 0)
        def _():
            h_scr[...] = jnp.zeros_like(h_scr)

        bias = bias_ref[0]
        a_neg = -jnp.exp(alog_ref[0])
        dsk = d_ref[0]
        cm = _ssd_chunk_common(c * Q, dt_ref, b_ref, c_ref, bias, a_neg)
        lane_even = cm["lane"] < 64
        for p in range(4):
            je, jo = 2 * p, 2 * p + 1
            xp = x_ref[:, 128 * p:128 * p + 128]
            xdt = xp * _pair(lane_even, cm["dt"], je)
            xdt_b = xdt.astype(BF)
            m_e = (cm["cb"] * _head_decay(cm, je)).astype(BF)
            m_o = (cm["cb"] * _head_decay(cm, jo)).astype(BF)
            zero = jnp.zeros_like(xdt_b)
            yd = (jnp.dot(m_e, jnp.where(lane_even, xdt_b, zero), preferred_element_type=F32)
                  + jnp.dot(m_o, jnp.where(lane_even, zero, xdt_b), preferred_element_type=F32))
            hp = h_scr[p]
            hp_ref[0, 0, p] = hp
            yo = jnp.dot(cm["cc"], hp.astype(BF), preferred_element_type=F32) * _pair(lane_even, cm["ecs"], je)
            y_ref[:, 128 * p:128 * p + 128] = yd + yo + xp * _pair_row(lane_even, dsk, je)
            st = jnp.dot(cm["bc_t"], (xdt * _pair(lane_even, cm["dsm"], je)).astype(BF), preferred_element_type=F32)
            h_scr[p] = hp * _pair_row(lane_even, cm["gam"], je) + st
        zc = z_ref[:, :]
        gated = y_ref[:, :] * (zc * _sigmoid(zc))
        yn_ref[:, :] = _rms(gated, nw_ref[...]).astype(BF)

    par = _spec((1, 1, 128), lambda g, c: (g, 0, 0))
    wide = _spec((Q, 512), lambda g, c: (c, g))
    return pl.pallas_call(
        body, grid=(2, NCH),
        in_specs=[wide, _spec((Q, 128), lambda g, c: (c, 8 + g)), _spec((Q, 128), lambda g, c: (c, 10 + g)),
                  _spec((Q, 128), lambda g, c: (c, PDT // 128 + g)), wide, par, par, par, _spec((1, 512), lambda g, c: (0, g))],
        out_specs=[wide, wide, _spec((1, 1, 4, 128, 128), lambda g, c: (g, c, 0, 0, 0))],
        out_shape=[jax.ShapeDtypeStruct((T, SSD_W), BF), jax.ShapeDtypeStruct((T, SSD_W), F32),
                   jax.ShapeDtypeStruct((2, NCH, 4, 128, 128), F32)],
        scratch_shapes=[pltpu.VMEM((4, 128, 128), F32)],
        compiler_params=_params(2), name="ssd_fwd")(xbc_act, xbc_act, xbc_act, proj, proj, dt_bias2, a_log2, d2, norm_w)


def ssd_bwd(dyn, xbc_act, proj, y_pre, h_prev, dt_bias2, a_log2, d2, norm_w):
    def body(dyn_ref, x_ref, b_ref, c_ref, dt_ref, z_ref, y_ref, hp_ref, bias_ref, alog_ref, d_ref, nw_ref,
             dz_ref, dx_ref, db_ref, dc_ref, ddt_ref, dpar_ref, dnw_ref, dh_scr, acc_scr):
        ci = pl.program_id(1)

        @pl.when(ci == 0)
        def _():
            dh_scr[...] = jnp.zeros_like(dh_scr)
            acc_scr[...] = jnp.zeros_like(acc_scr)
            dnw_ref[...] = jnp.zeros_like(dnw_ref)

        bias = bias_ref[0]
        a_neg = -jnp.exp(alog_ref[0])
        dsk = d_ref[0]
        cm = _ssd_chunk_common((NCH - 1 - ci) * Q, dt_ref, b_ref, c_ref, bias, a_neg)
        lane, sub = cm["lane"], cm["sub"]
        lane_even = lane < 64
        cc_t = c_ref[:, :].T.astype(BF)
        cb_t = lax.dot_general(cm["bc"], cm["cc"], NT_DIMS, preferred_element_type=F32)
        zc = z_ref[:, :]
        yc = y_ref[:, :]
        sg = _sigmoid(zc)
        sz = zc * sg
        dgated, dnw = _rms_bwd(dyn_ref[:, :], yc * sz, nw_ref[...])
        dnw_ref[...] += jnp.sum(dnw, axis=0, keepdims=True)
        dz_ref[:, :] = (dgated * yc * (sg * (1.0 + zc * (1.0 - sg)))).astype(BF)
        dy_all = dgated * sz
        dcb = jnp.zeros((Q, Q), F32)
        dcb_t = jnp.zeros((Q, Q), F32)
        db_acc = jnp.zeros((Q, Q), F32)
        dc_acc = jnp.zeros((Q, Q), F32)
        dcs = jnp.zeros((Q, Q), F32)
        ddt = jnp.zeros((Q, Q), F32)
        for p in range(4):
            je, jo = 2 * p, 2 * p + 1
            xp = x_ref[:, 128 * p:128 * p + 128]
            dy = dy_all[:, 128 * p:128 * p + 128]
            dt_p = _pair(lane_even, cm["dt"], je)
            xdt = xp * dt_p
            xdt_b = xdt.astype(BF)
            dy_b = dy.astype(BF)
            zero = jnp.zeros_like(dy_b)
            hp = hp_ref[0, 0, p]
            hp_b = hp.astype(BF)
            dh = dh_scr[p]
            dh_b = dh.astype(BF)
            acc_scr[p:p + 1, :] += jnp.sum(dy * xp, axis=0, keepdims=True)
            dxp = dy * _pair_row(lane_even, dsk, je)
            e_p = _pair(lane_even, cm["ecs"], je)
            g_p = jnp.dot(cm["cc"], hp_b, preferred_element_type=F32)
            dg_b = (dy * e_p).astype(BF)
            de = dy * g_p * e_p
            dc_acc = dc_acc + lax.dot_general(dg_b, hp_b, NT_DIMS, preferred_element_type=F32)
            dh_in = jnp.dot(cc_t, dg_b, preferred_element_type=F32)
            ds_p = _pair(lane_even, cm["dsm"], je)
            r_p = jnp.dot(cm["bc"], dh_b, preferred_element_type=F32)
            dxdt = r_p * ds_p
            tt = r_p * xdt * ds_p
            db_acc = db_acc + lax.dot_general((xdt * ds_p).astype(BF), dh_b, NT_DIMS, preferred_element_type=F32)
            dgam_m = jnp.sum(dh * hp, axis=0, keepdims=True)
            for j, even in ((je, True), (jo, False)):
                sel = lane_even if even else jnp.logical_not(lane_even)
                dy_j = jnp.where(sel, dy_b, zero)
                l_j = _head_decay(cm, j)
                l_jt = _head_decay_t(cm, j)
                m_j = cm["cb"] * l_j
                m_jt = cb_t * l_jt
                dm = lax.dot_general(dy_j, xdt_b, NT_DIMS, preferred_element_type=F32)
                dm_t = lax.dot_general(xdt_b, dy_j, NT_DIMS, preferred_element_type=F32)
                dxdt = dxdt + jnp.dot(m_jt.astype(BF), dy_j, preferred_element_type=F32)
                dcb = dcb + dm * l_j
                dcb_t = dcb_t + dm_t * l_jt
                t_j = jnp.where(sel, tt, 0.0)
                col = jnp.sum(dm * m_j - dm_t * m_jt + (jnp.where(sel, de, 0.0) - t_j), axis=1, keepdims=True)
                gam_j = cm["gam"][:, j:j + 1]
                last = (jnp.sum(jnp.sum(t_j, axis=0, keepdims=True), axis=1, keepdims=True)
                        + jnp.sum(jnp.where(sel[0:1, :], dgam_m, 0.0), axis=1, keepdims=True) * gam_j)
                col = col + jnp.where(sub[:, 0:1] == Q - 1, last, 0.0)
                dcs = dcs + jnp.where(lane == j, col, 0.0)
            dh_scr[p] = dh_in + dh * _pair_row(lane_even, cm["gam"], je)
            dx_ref[:, 128 * p:128 * p + 128] = dxp + dxdt * dt_p
            dd = dxdt * xp
            ddt = ddt + jnp.where(lane == je, jnp.sum(jnp.where(lane_even, dd, 0.0), axis=1, keepdims=True), 0.0)
            ddt = ddt + jnp.where(lane == jo, jnp.sum(jnp.where(lane_even, 0.0, dd), axis=1, keepdims=True), 0.0)
        dc_ref[:, :] = dc_acc + jnp.dot(dcb.astype(BF), cm["bc"], preferred_element_type=F32)
        db_ref[:, :] = db_acc + jnp.dot(dcb_t.astype(BF), cm["cc"], preferred_element_type=F32)
        tri_t = (sub <= lane).astype(F32)
        dd_a = jnp.dot(tri_t, dcs, precision=lax.Precision.HIGHEST, preferred_element_type=F32)
        ddt = ddt + dd_a * a_neg
        acc_scr[5:6, :] += jnp.sum(dd_a * cm["dt"], axis=0, keepdims=True)
        draw = jnp.where(cm["live"], ddt * _sigmoid_gate(cm["dtr"] + bias), 0.0)
        acc_scr[4:5, :] += jnp.sum(draw, axis=0, keepdims=True)
        ddt_ref[:, :] = draw.astype(BF)

        @pl.when(ci == NCH - 1)
        def _():
            lane1 = _lanes((1, 128))
            dd = jnp.zeros((1, 128), F32)
            for p in range(4):
                row = acc_scr[p:p + 1, :]
                dd = dd + jnp.where(lane1 == 2 * p, jnp.sum(jnp.where(lane1 < 64, row, 0.0), axis=1, keepdims=True), 0.0)
                dd = dd + jnp.where(lane1 == 2 * p + 1, jnp.sum(jnp.where(lane1 < 64, 0.0, row), axis=1, keepdims=True), 0.0)
            dpar_ref[0] = jnp.concatenate([acc_scr[4:5, :], acc_scr[5:6, :] * a_neg, dd, jnp.zeros((5, 128), F32)], axis=0)

    par = _spec((1, 1, 128), lambda g, c: (g, 0, 0))
    wide = _spec((Q, 512), lambda g, c: (NCH - 1 - c, g))
    thin = _spec((Q, 128), lambda g, c: (NCH - 1 - c, g))
    return pl.pallas_call(
        body, grid=(2, NCH),
        in_specs=[wide, wide, _spec((Q, 128), lambda g, c: (NCH - 1 - c, 8 + g)), _spec((Q, 128), lambda g, c: (NCH - 1 - c, 10 + g)),
                  _spec((Q, 128), lambda g, c: (NCH - 1 - c, PDT // 128 + g)), wide, wide,
                  _spec((1, 1, 4, 128, 128), lambda g, c: (g, NCH - 1 - c, 0, 0, 0)), par, par, par, _spec((1, 512), lambda g, c: (0, g))],
        out_specs=[wide, wide, thin, thin, thin, _spec((1, 8, 128), lambda g, c: (g, 0, 0)), _spec((1, 512), lambda g, c: (0, g))],
        out_shape=[jax.ShapeDtypeStruct((T, SSD_W), BF), jax.ShapeDtypeStruct((T, SSD_W), F32), jax.ShapeDtypeStruct((T, 256), F32),
                   jax.ShapeDtypeStruct((T, 256), F32), jax.ShapeDtypeStruct((T, 256), BF), jax.ShapeDtypeStruct((2, 8, 128), F32),
                   jax.ShapeDtypeStruct((1, SSD_W), F32)],
        scratch_shapes=[pltpu.VMEM((4, 128, 128), F32), pltpu.VMEM((8, 128), F32)],
        compiler_params=_params(2), name="ssd_bwd")(dyn, xbc_act, xbc_act, xbc_act, proj, proj, y_pre, h_prev, dt_bias2, a_log2, d2, norm_w)


def _lru_gates(back, cw, cb, wa, ba, wx, bx, lam):
    xr = _conv(back, cw, cb)
    xr_b = xr.astype(BF)
    r = _sigmoid_gate(jnp.dot(xr_b, wa, preferred_element_type=F32) + ba)
    i = _sigmoid_gate(jnp.dot(xr_b, wx, preferred_element_type=F32) + bx)
    sp = _softplus(-lam)
    la = (-LRU_C) * r * sp
    a = jnp.exp(la)
    mult2 = -jnp.tanh(la) * (a * a + 1.0)
    return xr, xr_b, r, i, sp, a, jnp.sqrt(mult2), mult2


def lru_gates_fwd(proj, cw, cb, wa2, ba, wx2, bx, lam):
    def body(x_ref, cw_ref, cb_ref, wa_ref, ba_ref, wx_ref, bx_ref, lam_ref, a_ref, u_ref, xpad):
        _fill_padded(xpad, x_ref)

        def chunk(r0):
            xr, _, _, i, _, a, mult, _ = _lru_gates(_back(xpad, r0), cw_ref[...], cb_ref[...], wa_ref[0], ba_ref[...], wx_ref[0], bx_ref[...],
                                                 lam_ref[...])
            a_ref[pl.ds(r0, Q), :] = a
            u_ref[pl.ds(r0, Q), :] = jnp.where(_rows(a.shape, r0) >= NPAD, mult * (i * xr), 0.0)

        _chunks(chunk, unrolled=True)

    c0 = PXL // 128
    vec = _spec((1, 128), lambda c: (0, c))
    mat = _spec((1, 128, 128), lambda c: (c, 0, 0))
    return pl.pallas_call(
        body, grid=(8,),
        in_specs=[_spec((T, 128), lambda c: (0, c0 + c)), _spec((4, 128), lambda c: (0, c)), vec, mat, vec, mat, vec, vec],
        out_specs=[_spec((T, 128), lambda c: (0, c)), _spec((T, 128), lambda c: (0, c))],
        out_shape=[jax.ShapeDtypeStruct((T, LRU_W), F32), jax.ShapeDtypeStruct((T, LRU_W), F32)],
        scratch_shapes=[pltpu.VMEM((T + 2 * HALO, 128), F32)],
        compiler_params=_params(), name="lru_gates_fwd")(proj, cw, cb, wa2, ba, wx2, bx, lam)


def lru_scan_fwd(a, u):
    def body(a_ref, u_ref, h_ref):
        def step(i, h):
            base = pl.multiple_of(i * 8, 8)
            for k in range(8):
                h = a_ref[pl.ds(base + k, 1), :] * h + u_ref[pl.ds(base + k, 1), :]
                h_ref[pl.ds(base + k, 1), :] = h
            return h

        lax.fori_loop(0, T // 8, step, jnp.zeros((1, LRU_W), F32))

    return pl.pallas_call(body, out_shape=jax.ShapeDtypeStruct((T, LRU_W), F32), compiler_params=_params(0), name="lru_scan_fwd")(a, u)


def lru_scan_bwd(a, dh_out):
    def body(a_ref, d_ref, o_ref):
        def step(i, carry):
            base = pl.multiple_of(T - 8 - i * 8, 8)
            for k in range(7, -1, -1):
                carry = d_ref[pl.ds(base + k, 1), :] + carry
                o_ref[pl.ds(base + k, 1), :] = carry
                carry = carry * a_ref[pl.ds(base + k, 1), :]
            return carry

        lax.fori_loop(0, T // 8, step, jnp.zeros((1, LRU_W), F32))

    return pl.pallas_call(body, out_shape=jax.ShapeDtypeStruct((T, LRU_W), F32), compiler_params=_params(0), name="lru_scan_bwd")(a, dh_out)


def lru_gates_bwd(dhs, hseq, proj, cw, cb, wa2, ba, wx2, bx, lam):
    def body(dh_ref, h_ref, x_ref, cw_ref, cb_ref, wa_ref, ba_ref, wx_ref, bx_ref, lam_ref,
             dx_ref, dcw_ref, dcb_ref, dwa_ref, dba_ref, dwx_ref, dbx_ref, dlam_ref, xpad, hpad, dpad):
        _fill_padded(xpad, x_ref)
        _fill_padded(hpad, h_ref)
        dpad[0:HALO, :] = jnp.zeros((HALO, 128), F32)
        dpad[T + HALO:T + 2 * HALO, :] = jnp.zeros((HALO, 128), F32)
        for ref in (dcw_ref, dcb_ref, dwa_ref, dba_ref, dwx_ref, dbx_ref, dlam_ref):
            ref[...] = jnp.zeros_like(ref)
        lam = lam_ref[...]

        def first(r0):
            back = _back(xpad, r0)
            xr, xr_b, r, i, sp, a, mult, mult2 = _lru_gates(back, cw_ref[...], cb_ref[...], wa_ref[0], ba_ref[...], wx_ref[0], bx_ref[...], lam)
            dh = dh_ref[pl.ds(r0, Q), :]
            da = dh * _back(hpad, r0)(1)
            du = jnp.where(_rows(dh.shape, r0) >= NPAD, dh, 0.0)
            dmult = du * (i * xr)
            di = du * (mult * xr)
            dxr = du * (mult * i)
            dla = da * a - dmult * (a * a) * lax.rsqrt(mult2)
            dr = dla * ((-LRU_C) * sp)
            dlam_ref[...] += jnp.sum(dla * ((-LRU_C) * r), axis=0, keepdims=True)
            dpr = dr * r * (1.0 - r)
            dpi = di * i * (1.0 - i)
            dba_ref[...] += jnp.sum(dpr, axis=0, keepdims=True)
            dbx_ref[...] += jnp.sum(dpi, axis=0, keepdims=True)
            dpr_b = dpr.astype(BF)
            dpi_b = dpi.astype(BF)
            dxr = (dxr + lax.dot_general(dpr_b, wa_ref[0], NT_DIMS, preferred_element_type=F32)
                   + lax.dot_general(dpi_b, wx_ref[0], NT_DIMS, preferred_element_type=F32))
            dwa_ref[0] += lax.dot_general(xr_b, dpr_b, TN_DIMS, preferred_element_type=F32)
            dwx_ref[0] += lax.dot_general(xr_b, dpi_b, TN_DIMS, preferred_element_type=F32)
            dpad[pl.ds(r0 + HALO, Q), :] = dxr
            dcw, dcb = _conv_bwd_w(dxr, back)
            dcw_ref[...] += dcw
            dcb_ref[...] += dcb

        _chunks(first, unrolled=True)
        dlam_ref[...] = -dlam_ref[...] * _sigmoid_gate(-lam)

        def second(r0):
            dx_ref[pl.ds(r0, Q), :] = _conv_bwd_x(_ahead(dpad, r0), cw_ref[...]).astype(BF)

        _chunks(second)

    c0 = PXL // 128
    vec = _spec((1, 128), lambda c: (0, c))
    mat = _spec((1, 128, 128), lambda c: (c, 0, 0))
    col = _spec((T, 128), lambda c: (0, c))
    vshape = jax.ShapeDtypeStruct((1, LRU_W), F32)
    mshape = jax.ShapeDtypeStruct((8, 128, 128), F32)
    pad = pltpu.VMEM((T + 2 * HALO, 128), F32)
    return pl.pallas_call(
        body, grid=(8,),
        in_specs=[col, col, _spec((T, 128), lambda c: (0, c0 + c)), _spec((4, 128), lambda c: (0, c)), vec, mat, vec, mat, vec, vec],
        out_specs=[col, _spec((4, 128), lambda c: (0, c)), vec, mat, vec, mat, vec, vec],
        out_shape=[jax.ShapeDtypeStruct((T, LRU_W), BF), jax.ShapeDtypeStruct((4, LRU_W), F32), vshape, mshape, vshape, mshape, vshape, vshape],
        scratch_shapes=[pad, pad, pad], compiler_params=_params(), name="lru_gates_bwd")(dhs, hseq, proj, cw, cb, wa2, ba, wx2, bx, lam)


def gate_up(h1, wn, w_gate, w_up):
    def body(h_ref, wn_ref, wg_ref, wu_ref, gt_ref, up_ref, act_ref, u_ref):
        for r in (0, HALF):
            u_ref[r:r + HALF, :] = _rms(h_ref[r:r + HALF, :], wn_ref[...]).astype(BF)

        def tile(c0):
            cols = pl.ds(c0, 256)
            gt = lax.dot_general(u_ref[...], wg_ref[cols, :], NT_DIMS, preferred_element_type=F32)
            up = lax.dot_general(u_ref[...], wu_ref[cols, :], NT_DIMS, preferred_element_type=F32)
            gt_ref[:, cols] = gt.astype(BF)
            up_ref[:, cols] = up.astype(BF)
            act_ref[:, cols] = (gt * _sigmoid(gt) * up).astype(BF)

        _col_tiles(D_FF, 256, tile)

    big = jax.ShapeDtypeStruct((T, D_FF), BF)
    return pl.pallas_call(
        body, grid=(T // RC,), in_specs=[_rows_spec(D), _vec(D), _whole((D_FF, D)), _whole((D_FF, D))],
        out_specs=[_rows_spec(D_FF), _rows_spec(D_FF), _rows_spec(D_FF), _rows_spec(D)],
        out_shape=[big, big, big, jax.ShapeDtypeStruct((T, D), BF)],
        compiler_params=_params(), name="gate_up")(h1, wn, w_gate, w_up)


def down_loss(act, w_down, h1, target, wf):
    first = NPAD + N_META

    def body(a_ref, w_ref, r_ref, t_hbm, wf_ref, d_ref, db_ref, l_ref, dw_ref, h_scr, t_ref, t_sem):
        i = pl.program_id(0)
        _zero_at_first(l_ref, dw_ref)
        head = pltpu.make_async_copy(t_hbm.at[pl.ds(0, RC - first)], t_ref.at[pl.ds(first, RC - first)], t_sem)
        rest = pltpu.make_async_copy(t_hbm.at[pl.ds(pl.multiple_of(jnp.maximum(i * RC - first, 0), 32), RC)], t_ref, t_sem)

        @pl.when(i == 0)
        def _():
            t_ref[0:first, :] = jnp.zeros((first, D), F32)
            head.start()

        @pl.when(i > 0)
        def _():
            rest.start()

        def tile(c0):
            cols = pl.ds(c0, 512)
            h_scr[:, cols] = r_ref[:, cols] + jnp.dot(a_ref[...], w_ref[:, cols], preferred_element_type=F32)

        _col_tiles(D, 512, tile)

        @pl.when(i == 0)
        def _():
            head.wait()

        @pl.when(i > 0)
        def _():
            rest.wait()

        for r in (0, HALF):
            h = h_scr[r:r + HALF, :]
            live = _rows((HALF, D), i * RC + r) >= first
            err = jnp.where(live, _rms(h, wf_ref[...]) - t_ref[r:r + HALF, :], 0.0)
            l_ref[...] += 0.5 * jnp.sum(jnp.sum(err * err, axis=1, keepdims=True) * (1.0 / D), axis=0, keepdims=True)
            dh, dw = _rms_bwd(err * (1.0 / D), h, wf_ref[...])
            dw_ref[...] += jnp.sum(dw, axis=0, keepdims=True)
            d_ref[r:r + HALF, :] = dh
            db_ref[r:r + HALF, :] = dh.astype(BF)

    return pl.pallas_call(
        body, grid=(T // RC,),
        in_specs=[_rows_spec(D_FF), _whole((D_FF, D)), _rows_spec(D), pl.BlockSpec(memory_space=pl.ANY), _vec(D)],
        out_specs=[_rows_spec(D), _rows_spec(D), _spec((1, 128), lambda i: (0, 0)), _vec(D)],
        out_shape=[jax.ShapeDtypeStruct((T, D), F32), jax.ShapeDtypeStruct((T, D), BF), jax.ShapeDtypeStruct((1, 128), F32),
                   jax.ShapeDtypeStruct((1, D), F32)],
        scratch_shapes=[pltpu.VMEM((RC, D), F32), pltpu.VMEM((RC, D), F32), pltpu.SemaphoreType.DMA],
        compiler_params=_params(), name="down_loss")(act, w_down, h1, target, wf)


def swiglu_bwd(dh2_b, w_down, gt, up):
    def body(d_ref, w_ref, gt_ref, up_ref, dg_ref, du_ref):
        def tile(c0):
            cols = pl.ds(c0, 256)
            dact = lax.dot_general(d_ref[...], w_ref[cols, :], NT_DIMS, preferred_element_type=F32)
            gt_ = gt_ref[:, cols].astype(F32)
            up_ = up_ref[:, cols].astype(F32)
            sg = _sigmoid(gt_)
            dg_ref[:, cols] = (dact * up_ * (sg * (1.0 + gt_ * (1.0 - sg)))).astype(BF)
            du_ref[:, cols] = (dact * (gt_ * sg)).astype(BF)

        _col_tiles(D_FF, 256, tile)

    big = jax.ShapeDtypeStruct((T, D_FF), BF)
    return pl.pallas_call(
        body, grid=(T // RC,), in_specs=[_rows_spec(D), _whole((D_FF, D)), _rows_spec(D_FF), _rows_spec(D_FF)],
        out_specs=[_rows_spec(D_FF), _rows_spec(D_FF)], out_shape=[big, big], compiler_params=_params(), name="swiglu_bwd")(dh2_b, w_down, gt, up)


def gate_up_bwd(dgt, dup, w_gate, w_up, h1, wn, dh2):
    def body(dg_ref, du_ref, wg_ref, wu_ref, h_ref, wn_ref, r_ref, d_ref, db_ref, dw_ref, du_scr):
        _zero_at_first(dw_ref)

        du_scr[...] = jnp.zeros_like(du_scr)

        def tile(c0):
            k = pl.ds(c0, 256)
            du_scr[...] += (jnp.dot(dg_ref[:, k], wg_ref[k, :], preferred_element_type=F32)
                            + jnp.dot(du_ref[:, k], wu_ref[k, :], preferred_element_type=F32))

        _col_tiles(D_FF, 256, tile)
        for r in (0, HALF):
            dh, dw = _rms_bwd(du_scr[r:r + HALF, :], h_ref[r:r + HALF, :], wn_ref[...])
            dw_ref[...] += jnp.sum(dw, axis=0, keepdims=True)
            dh = dh + r_ref[r:r + HALF, :]
            d_ref[r:r + HALF, :] = dh
            db_ref[r:r + HALF, :] = dh.astype(BF)

    return pl.pallas_call(
        body, grid=(T // RC,),
        in_specs=[_rows_spec(D_FF), _rows_spec(D_FF), _whole((D_FF, D)), _whole((D_FF, D)), _rows_spec(D), _vec(D), _rows_spec(D)],
        out_specs=[_rows_spec(D), _rows_spec(D), _vec(D)],
        out_shape=[jax.ShapeDtypeStruct((T, D), F32), jax.ShapeDtypeStruct((T, D), BF), jax.ShapeDtypeStruct((1, D), F32)],
        scratch_shapes=[pltpu.VMEM((RC, D), F32)],
        compiler_params=_params(), name="gate_up_bwd")(dgt, dup, w_gate, w_up, h1, wn, dh2)


def _adamw(w, g, m, v):
    m = ADAM_B1 * m + (1.0 - ADAM_B1) * g
    v = ADAM_B2 * v + (1.0 - ADAM_B2) * (g * g)
    m_hat = m / (1.0 - ADAM_B1 ** ADAM_STEP)
    v_hat = v / (1.0 - ADAM_B2 ** ADAM_STEP)
    delta = -ADAM_LR * (m_hat / (jnp.sqrt(v_hat) + ADAM_EPS) + ADAM_WD * w)
    return delta, m, v


def adamw_shard(name, recv, w, m, v, tr, tc):
    r, c = w.shape

    def body(p_ref, w_ref, m_ref, v_ref, g_ref, d_ref, mo_ref, vo_ref):
        g = p_ref[0].astype(F32)
        for s in range(1, 8):
            g = g + p_ref[s].astype(F32)
        g_ref[...] = g
        d_ref[...], mo_ref[...], vo_ref[...] = _adamw(w_ref[...], g, m_ref[...], v_ref[...])

    tile = _spec((tr, tc), lambda i, j: (i, j))
    shape = jax.ShapeDtypeStruct((r, c), F32)
    return pl.pallas_call(
        body, grid=(r // tr, c // tc), in_specs=[_spec((8, tr, tc), lambda i, j: (0, i, j)), tile, tile, tile],
        out_specs=[tile] * 4, out_shape=[shape] * 4, compiler_params=_params(2), name=name)(recv, w, m, v)


def sum_slabs(recv):
    def body(p_ref, o_ref):
        g = p_ref[0]
        for s in range(1, 8):
            g = g + p_ref[s]
        o_ref[...] = g

    return pl.pallas_call(body, out_shape=jax.ShapeDtypeStruct(recv.shape[1:], F32), compiler_params=_params(0), name="sum_slabs")(recv)


SIMPLE = [("norm1_w", 1024), ("ssd_conv_b", 1536), ("ssd_dt_bias", 16), ("ssd_a_log", 16), ("ssd_d", 16), ("ssd_norm_w", 1024),
          ("lru_conv_b", 1024), ("lru_ba", 1024), ("lru_bx", 1024), ("lru_lambda", 1024), ("lru_norm_w", 1024), ("norm2_w", 1024),
          ("final_norm_w", 1024)]
SPECIAL = ["lru_wa", "lru_wx", "meta_tokens", "ssd_conv_w", "lru_conv_w"]
SM_ROWS = 176
SM_WA, SM_WX, SM_META, SM_SCW, SM_LCW, SM_LOSS = 14, 78, 142, 158, 166, 170


def _simple_rows():
    rows, r = {}, 0
    for name, n in SIMPLE:
        rows[name] = r
        r += -(-n // 1024)
    return rows


def adamw_small(sm, special_g, ws, ms, vs):
    rows = _simple_rows()
    ns, nx = len(SIMPLE), len(SPECIAL)

    def body(*refs):
        sm_ref = refs[0]
        gx = refs[1:1 + nx]
        wr = refs[1 + nx:1 + nx + ns + nx]
        mr = refs[1 + nx + ns + nx:1 + nx + 2 * (ns + nx)]
        vr = refs[1 + nx + 2 * (ns + nx):1 + nx + 3 * (ns + nx)]
        outs = refs[1 + nx + 3 * (ns + nx):]
        o = 0
        for k, (name, n) in enumerate(SIMPLE):
            r0 = rows[name]
            for c0 in range(0, n, 1024):
                wd = min(1024, n - c0)
                g = sm_ref[r0 + c0 // 1024:r0 + c0 // 1024 + 1, 0:wd]
                sl = (slice(None), slice(c0, c0 + wd))
                d, m2, v2 = _adamw(wr[k][sl], g, mr[k][sl], vr[k][sl])
                outs[o][sl] = g
                outs[o + 1][sl] = d
                outs[o + 2][sl] = m2
                outs[o + 3][sl] = v2
            o += 4
        for k in range(nx):
            d, m2, v2 = _adamw(wr[ns + k][...], gx[k][...], mr[ns + k][...], vr[ns + k][...])
            outs[o][...] = d
            outs[o + 1][...] = m2
            outs[o + 2][...] = v2
            o += 3

    out_shape = []
    for k in range(ns):
        out_shape += [jax.ShapeDtypeStruct(ws[k].shape, F32)] * 4
    for k in range(nx):
        out_shape += [jax.ShapeDtypeStruct(ws[ns + k].shape, F32)] * 3
    return pl.pallas_call(body, out_shape=out_shape, compiler_params=_params(0), name="adamw_small")(sm, *special_g, *ws, *ms, *vs)


def _place():
    return lax.axis_index("x"), lax.axis_index("y"), lax.axis_index("c")


def _index(px, py, pc):
    return 4 * px + 2 * py + pc


def all_gather(name, shards):
    n = len(shards)
    hbm = pl.BlockSpec(memory_space=pl.ANY)

    def body(*refs):
        ins, outs = refs[:n], refs[n:2 * n]
        send_sems, recv_sems, local_sems = refs[2 * n:]
        x, y, c = _place()
        me, sibling = (x, y, c), (x, y, 1 - c)
        chips = [(1 - x, y), (x, 1 - y), (1 - x, 1 - y)]

        def copy(i, k, block, to, src=None):
            dst = outs[i].at[_index(*block)]
            return pltpu.make_async_remote_copy(src_ref=dst if src is None else src, dst_ref=dst, send_sem=send_sems.at[7 * i + k],
                                                recv_sem=recv_sems.at[7 * i + k], device_id=to, device_id_type=MESH)

        mine = [pltpu.make_async_copy(ins[i], outs[i].at[_index(*me)], local_sems.at[i]) for i in range(n)]
        for cp in mine:
            cp.start()
        first = []
        for i in range(n):
            first += [copy(i, 1 + j, me, (*chip, c), src=ins[i]) for j, chip in enumerate(chips)]
            first.append(copy(i, 0, me, sibling, src=ins[i]))
        for cp in first:
            cp.start()
        passed = []
        for i in range(n):
            for j, chip in enumerate(chips):
                copy(i, 1 + j, (*chip, c), me).wait_recv()
                cp = copy(i, 4 + j, (*chip, c), sibling)
                cp.start()
                passed.append(cp)
        for i in range(n):
            copy(i, 0, sibling, me).wait_recv()
            for j, chip in enumerate(chips):
                copy(i, 4 + j, (*chip, 1 - c), me).wait_recv()
        for cp in first + passed:
            cp.wait_send()
        for cp in mine:
            cp.wait()

    return pl.pallas_call(
        body, in_specs=[hbm] * n, out_specs=[hbm] * n,
        out_shape=[jax.ShapeDtypeStruct((8,) + s.shape, s.dtype) for s in shards],
        scratch_shapes=[pltpu.SemaphoreType.DMA((7 * n,)), pltpu.SemaphoreType.DMA((7 * n,)), pltpu.SemaphoreType.DMA((n,))],
        name=name)(*shards)


HBM_SPEC = pl.BlockSpec(memory_space=pltpu.HBM)
SEM_SPEC = pl.BlockSpec(memory_space=pltpu.SEMAPHORE)
EFFECT = pltpu.SideEffectType.DATAFLOW_SIDE_EFFECTING


def _peers(x, y, c):
    return [((1 - x) if k & 4 else x, (1 - y) if k & 2 else y, (1 - c) if k & 1 else c) for k in range(1, 8)]


def _pieces(rows):
    for n in (4, 2):
        if rows % (16 * n) == 0:
            return [(r * (rows // n), rows // n) for r in range(n)]
    return [(0, rows)]


def _peer_copies(src, land, send_sems, recv_sems, k, peer, mine, slab_src):
    block = src.at[_index(*peer)] if slab_src else src
    return [pltpu.make_async_remote_copy(src_ref=block.at[pl.ds(r0, nr)], dst_ref=land.at[mine, pl.ds(r0, nr)], send_sem=send_sems.at[k],
                                         recv_sem=recv_sems.at[k], device_id=peer, device_id_type=MESH)
            for r0, nr in _pieces(block.shape[0])]


def copies_start(name, srcs, slab_src, after):
    n = len(srcs)
    zones = [jax.ShapeDtypeStruct(s.shape if slab_src else (8,) + s.shape, s.dtype) for s in srcs]

    def body(*refs):
        ins, lands = refs[:n], refs[n:2 * n]
        sends, recvs = refs[2 * n + 1:3 * n + 1], refs[3 * n + 1:4 * n + 1]
        token = refs[-1]
        x, y, c = _place()
        mine = _index(x, y, c)
        for i in range(n):
            per_peer = [_peer_copies(ins[i], lands[i], sends[i], recvs[i], k, peer, mine, slab_src) for k, peer in enumerate(_peers(x, y, c))]
            for piece in zip(*per_peer):
                for cp in piece:
                    cp.start()
        token[...] = jnp.zeros_like(token)

    sem = pltpu.SemaphoreType.DMA((7,))
    res = pl.pallas_call(
        body, name=name,
        out_shape=([sem] * (2 * n) + [pltpu.HBM(s.shape, s.dtype) for s in srcs] + [pltpu.HBM(z.shape, z.dtype) for z in zones]
                   + [jax.ShapeDtypeStruct((8, 128), F32)]),
        in_specs=[HBM_SPEC] * (2 * n) + [pl.BlockSpec(memory_space=pl.ANY)],
        out_specs=[SEM_SPEC] * (2 * n) + [HBM_SPEC] * (2 * n) + [pl.BlockSpec(memory_space=pltpu.VMEM)],
        input_output_aliases={i: 2 * n + i for i in range(2 * n)},
        compiler_params=pltpu.CompilerParams(has_side_effects=EFFECT),
    )(*[pltpu.with_memory_space_constraint(s, pltpu.HBM) for s in srcs],
      *[pltpu.with_memory_space_constraint(lax.empty(z.shape, z.dtype), pltpu.HBM) for z in zones], after)
    return [(res[i], res[n + i], res[2 * n + i], res[3 * n + i]) for i in range(n)], res[-1][0:1, 0:1]


def copies_wait(name, started, slab_src, after):
    n = len(started)

    def body(*refs):
        ins, lands = refs[:n], refs[n:2 * n]
        sends, recvs = refs[2 * n:3 * n], refs[3 * n:4 * n]
        x, y, c = _place()
        mine = _index(x, y, c)
        for i in range(n):
            for k, peer in enumerate(_peers(x, y, c)):
                arrival = pltpu.make_async_remote_copy(src_ref=ins[i].at[mine] if slab_src else ins[i], dst_ref=lands[i].at[_index(*peer)],
                                                       send_sem=sends[i].at[k], recv_sem=recvs[i].at[k], device_id=peer, device_id_type=MESH)
                arrival.wait_send()
                arrival.wait_recv()

    srcs = [s[2] for s in started]
    lands = [s[3] for s in started]
    afters = list(after) if isinstance(after, (list, tuple)) else [after]
    res = pl.pallas_call(
        body, name=name,
        out_shape=[pltpu.HBM(s.shape, s.dtype) for s in srcs] + [pltpu.HBM(z.shape, z.dtype) for z in lands],
        in_specs=[HBM_SPEC] * (2 * n) + [SEM_SPEC] * (2 * n) + [pl.BlockSpec(memory_space=pl.ANY)] * len(afters),
        out_specs=[HBM_SPEC] * (2 * n),
        input_output_aliases={i: i for i in range(2 * n)},
        compiler_params=pltpu.CompilerParams(has_side_effects=EFFECT),
    )(*srcs, *lands, *[s[0] for s in started], *[s[1] for s in started], *afters)
    me = _index(*_place())
    own = [lax.dynamic_index_in_dim(s, me, 0, keepdims=True) if slab_src else s[None] for s in res[:n]]
    return [lax.dynamic_update_slice_in_dim(z, o, me, 0) for z, o in zip(res[n:], own)]


WEIGHTS = ["meta_tokens", "norm1_w", "w_in", "ssd_conv_w", "ssd_conv_b", "ssd_dt_bias", "ssd_a_log", "ssd_d", "ssd_norm_w", "lru_conv_w",
           "lru_conv_b", "lru_wa", "lru_ba", "lru_wx", "lru_bx", "lru_lambda", "lru_norm_w", "w_out", "norm2_w", "w_gate", "w_up", "w_down",
           "final_norm_w"]
BIG = ["w_in", "w_out", "w_gate", "w_up", "w_down"]
COLUMN_SHARDED = ["w_in", "w_gate", "w_up"]
BIG_TILE = {"w_in": (578, 256), "w_out": (128, 1024), "w_gate": (176, 1024), "w_up": (176, 1024), "w_down": (176, 1024)}


def _pair_blocks(w):
    w = w.reshape(8, 2, 64, 64)
    z = jnp.zeros((8, 64, 64), w.dtype)
    return jnp.concatenate([jnp.concatenate([w[:, 0], z], axis=2), jnp.concatenate([z, w[:, 1]], axis=2)], axis=1)


def _unpair_blocks(w2):
    return jnp.stack([w2[:, :64, :64], w2[:, 64:, 64:]], axis=1).reshape(16, 64, 64)


def _per_group(v):
    return jnp.pad(v.reshape(2, 1, 8), ((0, 0), (0, 0), (0, 120)))


def _pad_cols(v, n):
    return jnp.pad(v, ((0, 0), (0, n - v.shape[1])))


def local_step(x, target, meta, ssd_cw, lru_cw, w_in, fetch, send, p):
    z120 = jnp.zeros((120, D), BF)
    w_dt = jnp.concatenate([w_in[2560:2568], z120, w_in[2568:2576], z120], axis=0)
    bias2, alog2, d2 = _per_group(p["ssd_dt_bias"]), _per_group(p["ssd_a_log"]), _per_group(p["ssd_d"])
    wa2 = _pair_blocks(p["lru_wa"]).astype(BF)
    wx2 = _pair_blocks(p["lru_wx"]).astype(BF)
    lru = (lru_cw, p["lru_conv_b"], wa2, p["lru_ba"], wx2, p["lru_bx"], p["lru_lambda"])

    h0 = jnp.concatenate([jnp.zeros((NPAD, D), F32), meta, x], axis=0)
    proj, u1 = in_proj(h0, p["norm1_w"], w_in, w_dt)
    xbc_act = conv_silu_fwd(proj, ssd_cw, p["ssd_conv_b"])
    yn_ssd, y_pre, h_prev = ssd_fwd(xbc_act, proj, bias2, alog2, d2, p["ssd_norm_w"])
    a, u = lru_gates_fwd(proj, *lru)
    hseq = lru_scan_fwd(a, u)
    (w_out,) = fetch(["w_out"], hseq)
    h1, cat = out_proj(yn_ssd, proj, hseq, p["lru_norm_w"], w_out, h0)
    w_gate, w_up = fetch(["w_gate", "w_up"], h1)
    gt, up, act, u2 = gate_up(h1, p["norm2_w"], w_gate, w_up)
    (w_down,) = fetch(["w_down"], act)
    dh2, dh2_b, loss, d_fnw = down_loss(act, w_down, h1, target, p["final_norm_w"])

    dgt, dup = swiglu_bwd(dh2_b, w_down, gt, up)
    g_down = matmul_tn("dw_down", act, dh2_b, 1408, 512)
    g_gate = matmul_tn("dw_gate", dgt, u2, 1408, 512)
    g_up = matmul_tn("dw_up", dup, u2, 1408, 512)
    sent = send({"w_down": g_down, "w_gate": g_gate, "w_up": g_up})
    dh1, dh1_b, d_n2 = gate_up_bwd(dgt, dup, w_gate, w_up, h1, p["norm2_w"] + sent, dh2)
    sent = send({"w_out": matmul_tn("dw_out", cat, dh1_b, 512, 1024)})
    dyn, dh_out, dg_b, d_lnw = out_proj_bwd(dh1_b, w_out, proj, hseq, p["lru_norm_w"] + sent)

    dhs = lru_scan_bwd(a, dh_out)
    dxl_b, d_lcw, d_lcb, dwa2, d_ba, dwx2, d_bx, d_lam = lru_gates_bwd(dhs, hseq, proj, *lru)
    dz_b, dx, d_b, d_c, ddt_b, dpar, d_snw = ssd_bwd(dyn, xbc_act, proj, y_pre, h_prev, bias2, alog2, d2, p["ssd_norm_w"])
    dxbc_b, d_scw, d_scb = conv_silu_bwd(dx, d_b, d_c, proj, ssd_cw, p["ssd_conv_b"])
    g_dt = matmul_tn("dw_in_dt", ddt_b, u1, 256, 512)
    g_in = jnp.concatenate([matmul_tn("dw_in_z", dz_b, u1, 512, 1024), matmul_tn("dw_in_xbc", dxbc_b, u1, 512, 1024), g_dt[0:8], g_dt[128:136],
                            matmul_tn("dw_in_g", dg_b, u1, 512, 1024), matmul_tn("dw_in_xl", dxl_b, u1, 512, 1024)], axis=0)
    sent = send({"w_in": g_in})
    dh0, d_n1 = in_proj_bwd(dz_b, dg_b, dxl_b, dxbc_b, ddt_b, w_in, w_dt, h0, p["norm1_w"] + sent, dh1)
    small = {"norm1_w": d_n1, "ssd_conv_b": d_scb, "ssd_dt_bias": dpar[:, 0, :8].reshape(1, 16), "ssd_a_log": dpar[:, 1, :8].reshape(1, 16),
             "ssd_d": dpar[:, 2, :8].reshape(1, 16), "ssd_norm_w": d_snw, "lru_conv_b": d_lcb, "lru_ba": d_ba, "lru_bx": d_bx,
             "lru_lambda": d_lam, "lru_norm_w": d_lnw, "norm2_w": d_n2, "final_norm_w": d_fnw,
             "lru_wa": _unpair_blocks(dwa2), "lru_wx": _unpair_blocks(dwx2), "meta_tokens": dh0[NPAD:NPAD + N_META],
             "ssd_conv_w": d_scw, "lru_conv_w": d_lcw}
    return loss, dh0[NPAD + N_META:], small


def _pack_small(small, loss):
    rows = [_pad_cols(small[name], -(-n // 1024) * 1024).reshape(-1, 1024) for name, n in SIMPLE]
    rows += [small["lru_wa"].reshape(64, 1024), small["lru_wx"].reshape(64, 1024), small["meta_tokens"],
             _pad_cols(small["ssd_conv_w"], 2048).reshape(8, 1024), small["lru_conv_w"], _pad_cols(loss[:, 0:1], 1024)]
    sm = jnp.concatenate(rows, axis=0)
    return jnp.pad(sm, ((0, SM_ROWS - sm.shape[0]), (0, 0)))


def _slabs(g):
    return g.reshape(8, g.shape[0] // 8, g.shape[1])


def _unslab(g):
    return g.reshape(8 * g.shape[1], g.shape[2])


def kernel(x, meta_tokens, norm1_w, w_in, ssd_conv_w, ssd_conv_b, ssd_dt_bias, ssd_a_log, ssd_d, ssd_norm_w, lru_conv_w, lru_conv_b, lru_wa, lru_ba, lru_wx, lru_bx, lru_lambda, lru_norm_w, w_out, norm2_w, w_gate, w_up, w_down, final_norm_w, loss_target, m_meta_tokens, m_norm1_w, m_w_in, m_ssd_conv_w, m_ssd_conv_b, m_ssd_dt_bias, m_ssd_a_log, m_ssd_d, m_ssd_norm_w, m_lru_conv_w, m_lru_conv_b, m_lru_wa, m_lru_ba, m_lru_wx, m_lru_bx, m_lru_lambda, m_lru_norm_w, m_w_out, m_norm2_w, m_w_gate, m_w_up, m_w_down, m_final_norm_w, v_meta_tokens, v_norm1_w, v_w_in, v_ssd_conv_w, v_ssd_conv_b, v_ssd_dt_bias, v_ssd_a_log, v_ssd_d, v_ssd_norm_w, v_lru_conv_w, v_lru_conv_b, v_lru_wa, v_lru_ba, v_lru_wx, v_lru_bx, v_lru_lambda, v_lru_norm_w, v_w_out, v_norm2_w, v_w_gate, v_w_up, v_w_down, v_final_norm_w):
    w = dict(meta_tokens=meta_tokens, norm1_w=norm1_w, w_in=w_in[0], ssd_conv_w=ssd_conv_w[0], ssd_conv_b=ssd_conv_b, ssd_dt_bias=ssd_dt_bias,
             ssd_a_log=ssd_a_log, ssd_d=ssd_d, ssd_norm_w=ssd_norm_w, lru_conv_w=lru_conv_w[0], lru_conv_b=lru_conv_b, lru_wa=lru_wa[0],
             lru_ba=lru_ba, lru_wx=lru_wx[0], lru_bx=lru_bx, lru_lambda=lru_lambda, lru_norm_w=lru_norm_w, w_out=w_out[0], norm2_w=norm2_w,
             w_gate=w_gate[0], w_up=w_up[0], w_down=w_down[0], final_norm_w=final_norm_w.reshape(1, D))
    m = dict(meta_tokens=m_meta_tokens, norm1_w=m_norm1_w, w_in=m_w_in[0], ssd_conv_w=m_ssd_conv_w[0], ssd_conv_b=m_ssd_conv_b,
             ssd_dt_bias=m_ssd_dt_bias, ssd_a_log=m_ssd_a_log, ssd_d=m_ssd_d, ssd_norm_w=m_ssd_norm_w, lru_conv_w=m_lru_conv_w[0],
             lru_conv_b=m_lru_conv_b, lru_wa=m_lru_wa[0], lru_ba=m_lru_ba, lru_wx=m_lru_wx[0], lru_bx=m_lru_bx, lru_lambda=m_lru_lambda,
             lru_norm_w=m_lru_norm_w, w_out=m_w_out[0], norm2_w=m_norm2_w, w_gate=m_w_gate[0], w_up=m_w_up[0], w_down=m_w_down[0],
             final_norm_w=m_final_norm_w.reshape(1, D))
    v = dict(meta_tokens=v_meta_tokens, norm1_w=v_norm1_w, w_in=v_w_in[0], ssd_conv_w=v_ssd_conv_w[0], ssd_conv_b=v_ssd_conv_b,
             ssd_dt_bias=v_ssd_dt_bias, ssd_a_log=v_ssd_a_log, ssd_d=v_ssd_d, ssd_norm_w=v_ssd_norm_w, lru_conv_w=v_lru_conv_w[0],
             lru_conv_b=v_lru_conv_b, lru_wa=v_lru_wa[0], lru_ba=v_lru_ba, lru_wx=v_lru_wx[0], lru_bx=v_lru_bx, lru_lambda=v_lru_lambda,
             lru_norm_w=v_lru_norm_w, w_out=v_w_out[0], norm2_w=v_norm2_w, w_gate=v_w_gate[0], w_up=v_w_up[0], w_down=v_w_down[0],
             final_norm_w=v_final_norm_w.reshape(1, D))
    shapes = dict(meta_tokens=meta_tokens.shape, norm1_w=norm1_w.shape, w_in=w_in.shape, ssd_conv_w=ssd_conv_w.shape,
                  ssd_conv_b=ssd_conv_b.shape, ssd_dt_bias=ssd_dt_bias.shape, ssd_a_log=ssd_a_log.shape, ssd_d=ssd_d.shape,
                  ssd_norm_w=ssd_norm_w.shape, lru_conv_w=lru_conv_w.shape, lru_conv_b=lru_conv_b.shape, lru_wa=lru_wa.shape,
                  lru_ba=lru_ba.shape, lru_wx=lru_wx.shape, lru_bx=lru_bx.shape, lru_lambda=lru_lambda.shape, lru_norm_w=lru_norm_w.shape,
                  w_out=w_out.shape, norm2_w=norm2_w.shape, w_gate=w_gate.shape, w_up=w_up.shape, w_down=w_down.shape,
                  final_norm_w=final_norm_w.shape)
    me = _index(*_place())
    for n in COLUMN_SHARDED:
        w[n], m[n], v[n] = w[n].T, m[n].T, v[n].T

    small_shard = jnp.concatenate([w["meta_tokens"], _pad_cols(w["ssd_conv_w"], 256).reshape(8, 128), w["lru_conv_w"],
                                   jnp.zeros((4, 128), F32)], axis=0)
    g_in, gs = all_gather("gather_w_in", [w["w_in"].astype(BF), small_shard])
    later = ["w_out", "w_gate", "w_up", "w_down"]
    started, behind = copies_start("gather_rest_start", [w[n].astype(BF) for n in later], False, gs)
    started = dict(zip(later, started))
    meta_full = gs[:, 0:16].transpose(1, 0, 2).reshape(N_META, D)
    ssd_cw = gs[:, 16:24].reshape(8, 4, 256)[:, :, :192].transpose(1, 0, 2).reshape(4, XBC)
    lru_cw = gs[:, 24:28].transpose(1, 0, 2).reshape(4, LRU_W)

    def fetch(names, after):
        got = copies_wait("gather_" + names[0] + "_wait", [started[n] for n in names], False, after)
        return [_unslab(g) for g in got]

    in_flight = {}

    def send(grads):
        names = list(grads)
        st, token = copies_start("grads_" + names[0] + "_start", [grads[n] if n == "small" else _slabs(grads[n]) for n in names], True,
                                 grads[names[0]])
        in_flight.update(zip(names, st))
        return token

    loss, grad_x, small = local_step(x[0], loss_target[0], meta_full, ssd_cw, lru_cw, _unslab(g_in), fetch, send,
                                     {**w, "norm1_w": w["norm1_w"] + behind})
    send({"small": _pack_small(small, loss).reshape(8, SM_ROWS // 8, 1024)})

    out = {}
    early = ["w_down", "w_gate", "w_up", "w_out"]
    recv = dict(zip(early, copies_wait("grads_early_wait", [in_flight[n] for n in early], True, in_flight["small"][2])))
    for n in early:
        out[n] = adamw_shard("adamw_" + n, recv[n], w[n], m[n], v[n], *BIG_TILE[n])
    recv_in, recv_small = copies_wait("grads_late_wait", [in_flight["w_in"], in_flight["small"]], True, [out[n][0] for n in early])
    out["w_in"] = adamw_shard("adamw_w_in", recv_in, w["w_in"], m["w_in"], v["w_in"], *BIG_TILE["w_in"])
    for n in COLUMN_SHARDED:
        out[n] = [o.T for o in out[n]]
    sm = all_gather("gather_small_grads", [sum_slabs(recv_small)])[0].reshape(SM_ROWS, 1024)
    special_g =[sm[SM_WA:SM_WA + 64].reshape(16, 64, 64), sm[SM_WX:SM_WX + 64].reshape(16, 64, 64),
                 lax.dynamic_slice(sm[SM_META:SM_META + 16], (0, 128 * me), (16, 128)),
                 lax.dynamic_slice(sm[SM_SCW:SM_SCW + 8].reshape(4, 2048), (0, 192 * me), (4, 192)),
                 lax.dynamic_slice(sm[SM_LCW:SM_LCW + 4], (0, 128 * me), (4, 128))]
    names = [n for n, _ in SIMPLE] + SPECIAL
    res = adamw_small(sm, special_g, [w[n] for n in names], [m[n] for n in names], [v[n] for n in names])
    for k, (n, _) in enumerate(SIMPLE):
        out[n] = res[4 * k:4 * k + 4]
    for k, n in enumerate(SPECIAL):
        o = 4 * len(SIMPLE) + 3 * k
        out[n] = [special_g[k]] + list(res[o:o + 3])
    loss_total = sm[SM_LOSS, 0]
    flat = [loss_total, grad_x[None]]
    for k in range(4):
        flat += [out[n][k].reshape(shapes[n]) for n in WEIGHTS]
    return tuple(flat)
```

```python
import math

import jax
import jax.numpy as jnp
from jax import lax
from jax.experimental import pallas as pl
from jax.experimental.pallas import tpu as pltpu

F32 = jnp.float32
BF = jnp.bfloat16

D = 1024
SEQ = 2048
N_META = 16
Q = 128
NPAD = 112
T = NPAD + N_META + SEQ
NCH = T // Q
RC = 544
D_FF = 2816
SSD_W = 1024
LRU_W = 1024
XBC = 1536
IN_COLS = 4624
PZ, PG, PXL, PXBC, PDT = 0, 1024, 2048, 3072, 4608
NP_IN = 4864
EPS = 1e-6
LRU_C = 8.0
VMEM_LIMIT = 56 * 1024 * 1024

ADAM_LR, ADAM_B1, ADAM_B2, ADAM_EPS, ADAM_WD, ADAM_STEP = 0.001, 0.9, 0.999, 1e-08, 0.01, 10

NT_DIMS = (((1,), (1,)), ((), ()))
TN_DIMS = (((0,), (0,)), ((), ()))
MESH = pl.DeviceIdType.MESH


def _params(n_grid=1, limit=VMEM_LIMIT):
    return pltpu.CompilerParams(dimension_semantics=("arbitrary",) * n_grid, vmem_limit_bytes=limit)


def _spec(shape, imap, single=False):
    if single:
        return pl.BlockSpec(shape, imap, pipeline_mode=pl.Buffered(1))
    return pl.BlockSpec(shape, imap)


def _sigmoid(x):
    return 0.5 * jnp.tanh(0.5 * x) + 0.5


def _sigmoid_gate(x):
    return 1.0 / (1.0 + jnp.exp(-x))


def _softplus(x):
    return jnp.maximum(x, 0.0) + jnp.log(1.0 + jnp.exp(-jnp.abs(x)))


def _rms_stats(h):
    return lax.rsqrt(jnp.mean(h * h, axis=-1, keepdims=True) + EPS)


def _rms(h, w):
    return (h * _rms_stats(h)) * w


def _rms_bwd(du, h, w):
    r = _rms_stats(h)
    n = h * r
    dn = du * w
    dh = r * (dn - n * jnp.mean(dn * n, axis=-1, keepdims=True))
    return dh, du * n


_G0 = math.sqrt(2.0 / math.pi)


def _gelu(x):
    return 0.5 * x * (1.0 + jnp.tanh(_G0 * (x + 0.044715 * (x * x * x))))


def _gelu_grad(x):
    t = jnp.tanh(_G0 * (x + 0.044715 * (x * x * x)))
    return 0.5 * (1.0 + t) + 0.5 * x * (1.0 - t * t) * (_G0 * (1.0 + 3.0 * 0.044715 * (x * x)))


def _rows(shape, r0=0):
    return lax.broadcasted_iota(jnp.int32, shape, 0) + r0


def _lanes(shape):
    return lax.broadcasted_iota(jnp.int32, shape, 1)


HALO = 8


def _fill_padded(pad_ref, x_ref):
    pad_ref[0:HALO, :] = jnp.zeros((HALO, pad_ref.shape[1]), F32)
    pad_ref[T + HALO:T + 2 * HALO, :] = jnp.zeros((HALO, pad_ref.shape[1]), F32)

    def step(c, carry):
        r0 = pl.multiple_of(c * Q, Q)
        pad_ref[pl.ds(r0 + HALO, Q), :] = x_ref[pl.ds(r0, Q), :]
        return carry

    lax.fori_loop(0, NCH, step, 0)


def _back(pad_ref, r0):
    win = pad_ref[pl.ds(r0, Q + HALO), :]
    return lambda s: win[HALO:, :] if s == 0 else pltpu.roll(win, s, axis=0)[HALO:, :]


def _ahead(pad_ref, r0):
    win = pad_ref[pl.ds(r0 + HALO, Q + HALO), :]
    return lambda s: win[:Q, :] if s == 0 else pltpu.roll(win, Q + HALO - s, axis=0)[:Q, :]


def _conv(back, w, b):
    y = b + w[3:4, :] * back(0)
    for k in range(3):
        y = y + w[k:k + 1, :] * back(3 - k)
    return y


def _conv_bwd_x(ahead, w):
    dx = w[3:4, :] * ahead(0)
    for k in range(3):
        dx = dx + w[k:k + 1, :] * ahead(3 - k)
    return dx


def _conv_bwd_w(dy, back):
    dws = [jnp.sum(dy * back(3 - k), axis=0, keepdims=True) for k in range(4)]
    return jnp.concatenate(dws, axis=0), jnp.sum(dy, axis=0, keepdims=True)


def _chunks(fn, unrolled=False):
    if unrolled:
        for c in range(NCH):
            fn(c * Q)
        return

    def step(c, carry):
        fn(pl.multiple_of(c * Q, Q))
        return carry

    lax.fori_loop(0, NCH, step, 0)


HALF = RC // 2


def _col_tiles(n, tn, fn):
    def step(j, carry):
        fn(pl.multiple_of(j * tn, tn))
        return carry

    lax.fori_loop(0, n // tn, step, 0)


def _rows_spec(cols, block_col=0):
    return _spec((RC, cols), lambda i: (i, block_col))


def _whole(shape):
    return _spec(shape, lambda i: tuple(0 for _ in shape), single=True)


def _vec(cols):
    return _spec((1, cols), lambda i: (0, 0))


def _zero_at_first(*refs):
    @pl.when(pl.program_id(0) == 0)
    def _():
        for r in refs:
            r[...] = jnp.zeros_like(r)


IN_RUNS = ((PZ, 0, 1024), (PG, 2576, 2048), (PXBC, 1024, XBC))


def _in_tiles(fn):
    for pcol, wrow, width in IN_RUNS:
        def step(j, carry, pcol=pcol, wrow=wrow):
            fn(pl.multiple_of(pcol + j * 512, 512), pl.multiple_of(wrow + j * 512, 16))
            return carry

        lax.fori_loop(0, width // 512, step, 0)


def in_proj(h0, wn, w_t, w_dt):
    def body(h_ref, wn_ref, w_ref, wdt_ref, o_ref, u_ref):
        for r in (0, HALF):
            u_ref[r:r + HALF, :] = _rms(h_ref[r:r + HALF, :], wn_ref[...]).astype(BF)

        def tile(pcol, wrow):
            o_ref[:, pl.ds(pcol, 512)] = lax.dot_general(u_ref[...], w_ref[pl.ds(wrow, 512), :], NT_DIMS, preferred_element_type=F32)

        _in_tiles(tile)
        o_ref[:, PDT:PDT + 256] = lax.dot_general(u_ref[...], wdt_ref[...], NT_DIMS, preferred_element_type=F32)

    return pl.pallas_call(
        body, grid=(T // RC,), in_specs=[_rows_spec(D), _vec(D), _whole((IN_COLS, D)), _whole((256, D))],
        out_specs=[_rows_spec(NP_IN), _rows_spec(D)],
        out_shape=[jax.ShapeDtypeStruct((T, NP_IN), F32), jax.ShapeDtypeStruct((T, D), BF)],
        compiler_params=_params(), name="in_proj")(h0, wn, w_t, w_dt)


def out_proj(yn_ssd, proj, hseq, lru_nw, w_out, h0):
    def body(y_ref, g_ref, h_ref, wn_ref, w_ref, r_ref, o_ref, cat_ref):
        cat_ref[:, 0:SSD_W] = y_ref[...]
        for r in (0, HALF):
            y = _gelu(g_ref[r:r + HALF, :]) * h_ref[r:r + HALF, :]
            cat_ref[r:r + HALF, SSD_W:] = _rms(y, wn_ref[...]).astype(BF)

        def tile(c0):
            o_ref[:, pl.ds(c0, 512)] = r_ref[:, pl.ds(c0, 512)] + jnp.dot(cat_ref[...], w_ref[:, pl.ds(c0, 512)], preferred_element_type=F32)

        _col_tiles(D, 512, tile)

    return pl.pallas_call(
        body, grid=(T // RC,),
        in_specs=[_rows_spec(SSD_W), _rows_spec(LRU_W, PG // LRU_W), _rows_spec(LRU_W), _vec(LRU_W), _whole((SSD_W + LRU_W, D)), _rows_spec(D)],
        out_specs=[_rows_spec(D), _rows_spec(SSD_W + LRU_W)],
        out_shape=[jax.ShapeDtypeStruct((T, D), F32), jax.ShapeDtypeStruct((T, SSD_W + LRU_W), BF)],
        compiler_params=_params(), name="out_proj")(yn_ssd, proj, hseq, lru_nw, w_out, h0)


def out_proj_bwd(dh1_b, w_out, proj, hseq, lru_nw):
    def body(d_ref, w_ref, g_ref, h_ref, wn_ref, dy_ref, dh_ref, dg_ref, dw_ref, dl_scr):
        _zero_at_first(dw_ref)

        def tile(c0):
            dy_ref[:, pl.ds(c0, 512)] = lax.dot_general(d_ref[...], w_ref[pl.ds(c0, 512), :], NT_DIMS, preferred_element_type=F32)
            dl_scr[:, pl.ds(c0, 512)] = lax.dot_general(d_ref[...], w_ref[pl.ds(SSD_W + c0, 512), :], NT_DIMS, preferred_element_type=F32)

        _col_tiles(SSD_W, 512, tile)
        for r in (0, HALF):
            g = g_ref[r:r + HALF, :]
            h = h_ref[r:r + HALF, :]
            ge = _gelu(g)
            dy, dw = _rms_bwd(dl_scr[r:r + HALF, :], ge * h, wn_ref[...])
            dw_ref[...] += jnp.sum(dw, axis=0, keepdims=True)
            dh_ref[r:r + HALF, :] = dy * ge
            dg_ref[r:r + HALF, :] = (dy * h * _gelu_grad(g)).astype(BF)

    return pl.pallas_call(
        body, grid=(T // RC,),
        in_specs=[_rows_spec(D), _whole((SSD_W + LRU_W, D)), _rows_spec(LRU_W, PG // LRU_W), _rows_spec(LRU_W), _vec(LRU_W)],
        out_specs=[_rows_spec(SSD_W), _rows_spec(LRU_W), _rows_spec(LRU_W), _vec(LRU_W)],
        out_shape=[jax.ShapeDtypeStruct((T, SSD_W), F32), jax.ShapeDtypeStruct((T, LRU_W), F32), jax.ShapeDtypeStruct((T, LRU_W), BF),
                   jax.ShapeDtypeStruct((1, LRU_W), F32)],
        scratch_shapes=[pltpu.VMEM((RC, LRU_W), F32)],
        compiler_params=_params(), name="out_proj_bwd")(dh1_b, w_out, proj, hseq, lru_nw)


def in_proj_bwd(dz, dg, dxl, dxbc, ddt, w_t, w_dt, h0, wn, dh1):
    first = NPAD + N_META

    def body(dz_ref, dg_ref, dxl_ref, dxbc_ref, ddt_ref, w_ref, wdt_ref, h_ref, wn_ref, r_ref, gx_hbm, meta_ref, dw_ref, du_scr, o_ref, sem):
        i = pl.program_id(0)
        _zero_at_first(dw_ref)
        du_scr[...] = jnp.dot(ddt_ref[...], wdt_ref[...], preferred_element_type=F32)
        for d_ref, wrow, width in ((dz_ref, 0, 1024), (dxbc_ref, 1024, XBC), (dg_ref, 2576, 1024), (dxl_ref, 3600, 1024)):
            def step(j, carry, d_ref=d_ref, wrow=wrow):
                c0 = pl.multiple_of(j * 512, 512)
                du_scr[...] += jnp.dot(d_ref[:, pl.ds(c0, 512)], w_ref[pl.ds(pl.multiple_of(wrow + c0, 16), 512), :], preferred_element_type=F32)
                return carry

            lax.fori_loop(0, width // 512, step, 0)
        for r in (0, HALF):
            dh, dw = _rms_bwd(du_scr[r:r + HALF, :], h_ref[r:r + HALF, :], wn_ref[...])
            dw_ref[...] += jnp.sum(dw, axis=0, keepdims=True)
            o_ref[r:r + HALF, :] = dh + r_ref[r:r + HALF, :]

        @pl.when(i == 0)
        def _():
            meta_ref[...] = o_ref[NPAD:first, :]
            head = pltpu.make_async_copy(o_ref.at[pl.ds(first, RC - first)], gx_hbm.at[pl.ds(0, RC - first)], sem)
            head.start()
            head.wait()

        @pl.when(i > 0)
        def _():
            rest = pltpu.make_async_copy(o_ref, gx_hbm.at[pl.ds(pl.multiple_of(i * RC - first, 32), RC)], sem)
            rest.start()
            rest.wait()

    return pl.pallas_call(
        body, grid=(T // RC,),
        in_specs=[_rows_spec(SSD_W), _rows_spec(LRU_W), _rows_spec(LRU_W), _rows_spec(XBC), _rows_spec(256), _whole((IN_COLS, D)),
                  _whole((256, D)), _rows_spec(D), _vec(D), _rows_spec(D)],
        out_specs=[pl.BlockSpec(memory_space=pl.ANY), _spec((N_META, D), lambda i: (0, 0)), _vec(D)],
        out_shape=[jax.ShapeDtypeStruct((SEQ, D), F32), jax.ShapeDtypeStruct((N_META, D), F32), jax.ShapeDtypeStruct((1, D), F32)],
        scratch_shapes=[pltpu.VMEM((RC, D), F32), pltpu.VMEM((RC, D), F32), pltpu.SemaphoreType.DMA],
        compiler_params=_params(), name="in_proj_bwd")(dz, dg, dxl, dxbc, ddt, w_t, w_dt, h0, wn, dh1)


def matmul_tn(name, a, b, tm, tn):
    m, n = a.shape[1], b.shape[1]

    def body(a_ref, b_ref, o_ref, acc_ref):
        acc_ref[...] = jnp.zeros_like(acc_ref)

        def mm(r0):
            acc_ref[...] += lax.dot_general(a_ref[pl.ds(r0, RC), :], b_ref[pl.ds(r0, RC), :], TN_DIMS, preferred_element_type=F32)

        _col_tiles(T, RC, mm)
        o_ref[...] = acc_ref[...].astype(BF)

    return pl.pallas_call(
        body, grid=(m // tm, n // tn),
        in_specs=[_spec((T, tm), lambda i, j: (0, i)), _spec((T, tn), lambda i, j: (0, j))],
        out_specs=_spec((tm, tn), lambda i, j: (i, j)),
        out_shape=jax.ShapeDtypeStruct((m, n), BF),
        scratch_shapes=[pltpu.VMEM((tm, tn), F32)],
        compiler_params=_params(2), name=name)(a, b)


def conv_silu_fwd(proj, cw, cb):
    def body(x_ref, w_ref, b_ref, o_ref, xpad):
        _fill_padded(xpad, x_ref)

        def chunk(r0):
            pre = _conv(_back(xpad, r0), w_ref[...], b_ref[...])
            o_ref[pl.ds(r0, Q), :] = pre * _sigmoid(pre)

        _chunks(chunk)

    c0 = PXBC // 128
    return pl.pallas_call(
        body, grid=(XBC // 128,),
        in_specs=[_spec((T, 128), lambda c: (0, c0 + c)), _spec((4, 128), lambda c: (0, c)), _spec((1, 128), lambda c: (0, c))],
        out_specs=_spec((T, 128), lambda c: (0, c)),
        out_shape=jax.ShapeDtypeStruct((T, XBC), F32), scratch_shapes=[pltpu.VMEM((T + 2 * HALO, 128), F32)],
        compiler_params=_params(), name="conv_silu_fwd")(proj, cw, cb)


def conv_silu_bwd(dx, d_b, d_c, proj, cw, cb):
    def body(dx_ref, db_ref, dc_ref, x_ref, w_ref, b_ref, o_ref, dw_ref, dbias_ref, xpad, dpad):
        tile = pl.program_id(0)
        _fill_padded(xpad, x_ref)
        dpad[0:HALO, :] = jnp.zeros((HALO, 128), F32)
        dpad[T + HALO:T + 2 * HALO, :] = jnp.zeros((HALO, 128), F32)
        dw_ref[...] = jnp.zeros_like(dw_ref)
        dbias_ref[...] = jnp.zeros_like(dbias_ref)

        def first(r0):
            back = _back(xpad, r0)
            pre = _conv(back, w_ref[...], b_ref[...])
            sg = _sigmoid(pre)
            rows = pl.ds(r0, Q)
            d = jnp.where(tile < 8, dx_ref[rows, :], jnp.where(tile < 10, db_ref[rows, :], dc_ref[rows, :]))
            dpre = d * (sg * (1.0 + pre * (1.0 - sg)))
            dpad[pl.ds(r0 + HALO, Q), :] = dpre
            dw, dbias = _conv_bwd_w(dpre, back)
            dw_ref[...] += dw
            dbias_ref[...] += dbias

        _chunks(first)

        def second(r0):
            o_ref[pl.ds(r0, Q), :] = _conv_bwd_x(_ahead(dpad, r0), w_ref[...]).astype(BF)

        _chunks(second)

    c0 = PXBC // 128
    pad = pltpu.VMEM((T + 2 * HALO, 128), F32)
    return pl.pallas_call(
        body, grid=(XBC // 128,),
        in_specs=[_spec((T, 128), lambda c: (0, jnp.minimum(c, 7))), _spec((T, 128), lambda c: (0, jnp.clip(c - 8, 0, 1))),
                  _spec((T, 128), lambda c: (0, jnp.clip(c - 10, 0, 1))), _spec((T, 128), lambda c: (0, c0 + c)),
                  _spec((4, 128), lambda c: (0, c)), _spec((1, 128), lambda c: (0, c))],
        out_specs=[_spec((T, 128), lambda c: (0, c)), _spec((4, 128), lambda c: (0, c)), _spec((1, 128), lambda c: (0, c))],
        out_shape=[jax.ShapeDtypeStruct((T, XBC), BF), jax.ShapeDtypeStruct((4, XBC), F32), jax.ShapeDtypeStruct((1, XBC), F32)],
        scratch_shapes=[pad, pad], compiler_params=_params(), name="conv_silu_bwd")(dx, d_b, d_c, proj, cw, cb)


def _ssd_chunk_common(row0, dt_ref, b_ref, c_ref, bias, a_neg):
    shape = (Q, Q)
    lane = _lanes(shape)
    sub = _rows(shape)
    live = (_rows(shape, row0) >= NPAD) & (lane < 8)
    dtr = dt_ref[:, :]
    dt = jnp.where(live, _softplus(dtr + bias), 0.0)
    d_a = dt * a_neg
    tri = (sub >= lane).astype(F32)
    cs = jnp.dot(tri, d_a, precision=lax.Precision.HIGHEST, preferred_element_type=F32)
    cs_t = cs.T
    b_f = b_ref[:, :]
    bc = b_f.astype(BF)
    cc = c_ref[:, :].astype(BF)
    cb = lax.dot_general(cc, bc, NT_DIMS, preferred_element_type=F32)
    cs_last = cs[Q - 1:Q, :]
    return dict(lane=lane, sub=sub, live=live, dtr=dtr, dt=dt, cs=cs, cs_t=cs_t, bc=bc, cc=cc, cb=cb, bc_t=b_f.T.astype(BF),
                ecs=jnp.exp(cs), dsm=jnp.exp(cs_last - cs), gam=jnp.exp(cs_last))


def _pair(lane_even, mat, j):
    return jnp.where(lane_even, mat[:, j:j + 1], mat[:, j + 1:j + 2])


def _pair_row(lane_even, mat, j):
    return jnp.where(lane_even[0:1, :], mat[:, j:j + 1], mat[:, j + 1:j + 2])


def _head_decay(cm, j):
    seg = cm["cs"][:, j:j + 1] - cm["cs_t"][j:j + 1, :]
    return jnp.exp(jnp.where(cm["sub"] >= cm["lane"], seg, -jnp.inf))


def _head_decay_t(cm, j):
    seg = cm["cs_t"][j:j + 1, :] - cm["cs"][:, j:j + 1]
    return jnp.exp(jnp.where(cm["lane"] >= cm["sub"], seg, -jnp.inf))


def ssd_fwd(xbc_act, proj, dt_bias2, a_log2, d2, norm_w):
    def body(x_all, b_all, c_all, dt_all, z_all, bias_all, alog_all, d_all, nw_all, yn_all, y_all, hp_all, h_all):
        @pl.when(pl.program_id(0) == 0)
        def _():
            h_all[...] = jnp.zeros_like(h_all)

        for g in range(2):
            wide, thin = slice(512 * g, 512 * g + 512), slice(128 * g, 128 * g + 128)
            group(x_all.at[:, wide], b_all.at[:, thin], c_all.at[:, thin], dt_all.at[:, thin], z_all.at[:, wide], bias_all.at[g],
                  alog_all.at[g], d_all.at[g], nw_all.at[:, wide], yn_all.at[:, wide], y_all.at[:, wide], hp_all.at[g, 0], h_all.at[g])

    def group(x_ref, b_ref, c_ref, dt_ref, z_ref, bias_ref, alog_ref, d_ref, nw_ref, yn_ref, y_ref, hp_ref, h_scr):
        bias = bias_ref[...]
        a_neg = -jnp.exp(alog_ref[...])
        dsk = d_ref[...]
        cm = _ssd_chunk_common(pl.program_id(0) * Q, dt_ref, b_ref, c_ref, bias, a_neg)
        lane_even = cm["lane"] < 64
        for p in range(4):
            je, jo = 2 * p, 2 * p + 1
            xp = x_ref[:, 128 * p:128 * p + 128]
            xdt = xp * _pair(lane_even, cm["dt"], je)
            xdt_b = xdt.astype(BF)
            m_e = (cm["cb"] * _head_decay(cm, je)).astype(BF)
            m_o = (cm["cb"] * _head_decay(cm, jo)).astype(BF)
            zero = jnp.zeros_like(xdt_b)
            yd = (jnp.dot(m_e, jnp.where(lane_even, xdt_b, zero), preferred_element_type=F32)
                  + jnp.dot(m_o, jnp.where(lane_even, zero, xdt_b), preferred_element_type=F32))
            hp = h_scr[p]
            hp_ref[p] = hp
            yo = jnp.dot(cm["cc"], hp.astype(BF), preferred_element_type=F32) * _pair(lane_even, cm["ecs"], je)
            y_ref[:, 128 * p:128 * p + 128] = yd + yo + xp * _pair_row(lane_even, dsk, je)
            st = jnp.dot(cm["bc_t"], (xdt * _pair(lane_even, cm["dsm"], je)).astype(BF), preferred_element_type=F32)
            h_scr[p] = hp * _pair_row(lane_even, cm["gam"], je) + st
        zc = z_ref[:, :]
        gated = y_ref[:, :] * (zc * _sigmoid(zc))
        yn_ref[:, :] = _rms(gated, nw_ref[...]).astype(BF)

    par = _spec((2, 1, 128), lambda c: (0, 0, 0))
    wide = _spec((Q, SSD_W), lambda c: (c, 0))
    return pl.pallas_call(
        body, grid=(NCH,),
        in_specs=[wide, _spec((Q, 256), lambda c: (c, 4)), _spec((Q, 256), lambda c: (c, 5)), _spec((Q, 256), lambda c: (c, PDT // 256)),
                  wide, par, par, par, _spec((1, SSD_W), lambda c: (0, 0))],
        out_specs=[wide, wide, _spec((2, 1, 4, 128, 128), lambda c: (0, c, 0, 0, 0))],
        out_shape=[jax.ShapeDtypeStruct((T, SSD_W), BF), jax.ShapeDtypeStruct((T, SSD_W), F32),
                   jax.ShapeDtypeStruct((2, NCH, 4, 128, 128), F32)],
        scratch_shapes=[pltpu.VMEM((2, 4, 128, 128), F32)],
        compiler_params=_params(), name="ssd_fwd")(xbc_act, xbc_act, xbc_act, proj, proj, dt_bias2, a_log2, d2, norm_w)


def ssd_bwd(dyn, xbc_act, proj, y_pre, h_prev, dt_bias2, a_log2, d2, norm_w):
    def body(dyn_all, x_all, b_all, c_all, dt_all, z_all, y_all, hp_all, bias_all, alog_all, d_all, nw_all,
             dz_all, dx_all, db_all, dc_all, ddt_all, dpar_all, dnw_all, dh_all, acc_all):
        @pl.when(pl.program_id(0) == 0)
        def _():
            dh_all[...] = jnp.zeros_like(dh_all)
            acc_all[...] = jnp.zeros_like(acc_all)
            dnw_all[...] = jnp.zeros_like(dnw_all)

        for g in range(2):
            wide, thin = slice(512 * g, 512 * g + 512), slice(128 * g, 128 * g + 128)
            group(dyn_all.at[:, wide], x_all.at[:, wide], b_all.at[:, thin], c_all.at[:, thin], dt_all.at[:, thin], z_all.at[:, wide],
                  y_all.at[:, wide], hp_all.at[g, 0], bias_all.at[g], alog_all.at[g], d_all.at[g], nw_all.at[:, wide],
                  dz_all.at[:, wide], dx_all.at[:, wide], db_all.at[:, thin], dc_all.at[:, thin], ddt_all.at[:, thin], dpar_all.at[g],
                  dnw_all.at[:, wide], dh_all.at[g], acc_all.at[g])

    def group(dyn_ref, x_ref, b_ref, c_ref, dt_ref, z_ref, y_ref, hp_ref, bias_ref, alog_ref, d_ref, nw_ref,
              dz_ref, dx_ref, db_ref, dc_ref, ddt_ref, dpar_ref, dnw_ref, dh_scr, acc_scr):
        ci = pl.program_id(0)
        bias = bias_ref[...]
        a_neg = -jnp.exp(alog_ref[...])
        dsk = d_ref[...]
        cm = _ssd_chunk_common((NCH - 1 - ci) * Q, dt_ref, b_ref, c_ref, bias, a_neg)
        lane, sub = cm["lane"], cm["sub"]
        lane_even = lane < 64
        cc_t = c_ref[:, :].T.astype(BF)
        cb_t = lax.dot_general(cm["bc"], cm["cc"], NT_DIMS, preferred_element_type=F32)
        zc = z_ref[:, :]
        yc = y_ref[:, :]
        sg = _sigmoid(zc)
        sz = zc * sg
        dgated, dnw = _rms_bwd(dyn_ref[:, :], yc * sz, nw_ref[...])
        dnw_ref[...] += jnp.sum(dnw, axis=0, keepdims=True)
        dz_ref[:, :] = (dgated * yc * (sg * (1.0 + zc * (1.0 - sg)))).astype(BF)
        dy_all = dgated * sz
        dcb = jnp.zeros((Q, Q), F32)
        dcb_t = jnp.zeros((Q, Q), F32)
        db_acc = jnp.zeros((Q, Q), F32)
        dc_acc = jnp.zeros((Q, Q), F32)
        dcs = jnp.zeros((Q, Q), F32)
        ddt = jnp.zeros((Q, Q), F32)
        for p in range(4):
            je, jo = 2 * p, 2 * p + 1
            xp = x_ref[:, 128 * p:128 * p + 128]
            dy = dy_all[:, 128 * p:128 * p + 128]
            dt_p = _pair(lane_even, cm["dt"], je)
            xdt = xp * dt_p
            xdt_b = xdt.astype(BF)
            dy_b = dy.astype(BF)
            zero = jnp.zeros_like(dy_b)
            hp = hp_ref[p]
            hp_b = hp.astype(BF)
            dh = dh_scr[p]
            dh_b = dh.astype(BF)
            acc_scr[p:p + 1, :] += jnp.sum(dy * xp, axis=0, keepdims=True)
            dxp = dy * _pair_row(lane_even, dsk, je)
            e_p = _pair(lane_even, cm["ecs"], je)
            g_p = jnp.dot(cm["cc"], hp_b, preferred_element_type=F32)
            dg_b = (dy * e_p).astype(BF)
            de = dy * g_p * e_p
            dc_acc = dc_acc + lax.dot_general(dg_b, hp_b, NT_DIMS, preferred_element_type=F32)
            dh_in = jnp.dot(cc_t, dg_b, preferred_element_type=F32)
            ds_p = _pair(lane_even, cm["dsm"], je)
            r_p = jnp.dot(cm["bc"], dh_b, preferred_element_type=F32)
            dxdt = r_p * ds_p
            tt = r_p * xdt * ds_p
            db_acc = db_acc + lax.dot_general((xdt * ds_p).astype(BF), dh_b, NT_DIMS, preferred_element_type=F32)
            dgam_m = jnp.sum(dh * hp, axis=0, keepdims=True)
            for j, even in ((je, True), (jo, False)):
                sel = lane_even if even else jnp.logical_not(lane_even)
                dy_j = jnp.where(sel, dy_b, zero)
                l_j = _head_decay(cm, j)
                l_jt = _head_decay_t(cm, j)
                m_j = cm["cb"] * l_j
                m_jt = cb_t * l_jt
                dm = lax.dot_general(dy_j, xdt_b, NT_DIMS, preferred_element_type=F32)
                dm_t = lax.dot_general(xdt_b, dy_j, NT_DIMS, preferred_element_type=F32)
                dxdt = dxdt + jnp.dot(m_jt.astype(BF), dy_j, preferred_element_type=F32)
                dcb = dcb + dm * l_j
                dcb_t = dcb_t + dm_t * l_jt
                t_j = jnp.where(sel, tt, 0.0)
                col = jnp.sum(dm * m_j - dm_t * m_jt + (jnp.where(sel, de, 0.0) - t_j), axis=1, keepdims=True)
                gam_j = cm["gam"][:, j:j + 1]
                last = (jnp.sum(jnp.sum(t_j, axis=0, keepdims=True), axis=1, keepdims=True)
                        + jnp.sum(jnp.where(sel[0:1, :], dgam_m, 0.0), axis=1, keepdims=True) * gam_j)
                col = col + jnp.where(sub[:, 0:1] == Q - 1, last, 0.0)
                dcs = dcs + jnp.where(lane == j, col, 0.0)
            dh_scr[p] = dh_in + dh * _pair_row(lane_even, cm["gam"], je)
            dx_ref[:, 128 * p:128 * p + 128] = dxp + dxdt * dt_p
            dd = dxdt * xp
            ddt = ddt + jnp.where(lane == je, jnp.sum(jnp.where(lane_even, dd, 0.0), axis=1, keepdims=True), 0.0)
            ddt = ddt + jnp.where(lane == jo, jnp.sum(jnp.where(lane_even, 0.0, dd), axis=1, keepdims=True), 0.0)
        dc_ref[:, :] = dc_acc + jnp.dot(dcb.astype(BF), cm["bc"], preferred_element_type=F32)
        db_ref[:, :] = db_acc + jnp.dot(dcb_t.astype(BF), cm["cc"], preferred_element_type=F32)
        tri_t = (sub <= lane).astype(F32)
        dd_a = jnp.dot(tri_t, dcs, precision=lax.Precision.HIGHEST, preferred_element_type=F32)
        ddt = ddt + dd_a * a_neg
        acc_scr[5:6, :] += jnp.sum(dd_a * cm["dt"], axis=0, keepdims=True)
        draw = jnp.where(cm["live"], ddt * _sigmoid_gate(cm["dtr"] + bias), 0.0)
        acc_scr[4:5, :] += jnp.sum(draw, axis=0, keepdims=True)
        ddt_ref[:, :] = draw.astype(BF)

        @pl.when(ci == NCH - 1)
        def _():
            lane1 = _lanes((1, 128))
            dd = jnp.zeros((1, 128), F32)
            for p in range(4):
                row = acc_scr[p:p + 1, :]
                dd = dd + jnp.where(lane1 == 2 * p, jnp.sum(jnp.where(lane1 < 64, row, 0.0), axis=1, keepdims=True), 0.0)
                dd = dd + jnp.where(lane1 == 2 * p + 1, jnp.sum(jnp.where(lane1 < 64, 0.0, row), axis=1, keepdims=True), 0.0)
            dpar_ref[...] = jnp.concatenate([acc_scr[4:5, :], acc_scr[5:6, :] * a_neg, dd, jnp.zeros((5, 128), F32)], axis=0)

    par = _spec((2, 1, 128), lambda c: (0, 0, 0))
    wide = _spec((Q, SSD_W), lambda c: (NCH - 1 - c, 0))
    thin = _spec((Q, 256), lambda c: (NCH - 1 - c, 0))
    vec = _spec((1, SSD_W), lambda c: (0, 0))
    return pl.pallas_call(
        body, grid=(NCH,),
        in_specs=[wide, wide, _spec((Q, 256), lambda c: (NCH - 1 - c, 4)), _spec((Q, 256), lambda c: (NCH - 1 - c, 5)),
                  _spec((Q, 256), lambda c: (NCH - 1 - c, PDT // 256)), wide, wide,
                  _spec((2, 1, 4, 128, 128), lambda c: (0, NCH - 1 - c, 0, 0, 0)), par, par, par, vec],
        out_specs=[wide, wide, thin, thin, thin, _spec((2, 8, 128), lambda c: (0, 0, 0)), vec],
        out_shape=[jax.ShapeDtypeStruct((T, SSD_W), BF), jax.ShapeDtypeStruct((T, SSD_W), F32), jax.ShapeDtypeStruct((T, 256), F32),
                   jax.ShapeDtypeStruct((T, 256), F32), jax.ShapeDtypeStruct((T, 256), BF), jax.ShapeDtypeStruct((2, 8, 128), F32),
                   jax.ShapeDtypeStruct((1, SSD_W), F32)],
        scratch_shapes=[pltpu.VMEM((2, 4, 128, 128), F32), pltpu.VMEM((2, 8, 128), F32)],
        compiler_params=_params(), name="ssd_bwd")(dyn, xbc_act, xbc_act, xbc_act, proj, proj, y_pre, h_prev, dt_bias2, a_log2, d2, norm_w)


def _lru_gates(back, cw, cb, wa, ba, wx, bx, lam):
    xr = _conv(back, cw, cb)
    xr_b = xr.astype(BF)
    r = _sigmoid_gate(jnp.dot(xr_b, wa, preferred_element_type=F32) + ba)
    i = _sigmoid_gate(jnp.dot(xr_b, wx, preferred_element_type=F32) + bx)
    sp = _softplus(-lam)
    la = (-LRU_C) * r * sp
    a = jnp.exp(la)
    mult2 = -jnp.tanh(la) * (a * a + 1.0)
    return xr, xr_b, r, i, sp, a, jnp.sqrt(mult2), mult2


def lru_gates_fwd(proj, cw, cb, wa2, ba, wx2, bx, lam):
    def body(x_ref, cw_ref, cb_ref, wa_ref, ba_ref, wx_ref, bx_ref, lam_ref, a_ref, u_ref, xpad):
        _fill_padded(xpad, x_ref)

        def chunk(r0):
            xr, _, _, i, _, a, mult, _ = _lru_gates(_back(xpad, r0), cw_ref[...], cb_ref[...], wa_ref[0], ba_ref[...], wx_ref[0], bx_ref[...],
                                                 lam_ref[...])
            a_ref[pl.ds(r0, Q), :] = a
            u_ref[pl.ds(r0, Q), :] = jnp.where(_rows(a.shape, r0) >= NPAD, mult * (i * xr), 0.0)

        _chunks(chunk, unrolled=True)

    c0 = PXL // 128
    vec = _spec((1, 128), lambda c: (0, c))
    mat = _spec((1, 128, 128), lambda c: (c, 0, 0))
    return pl.pallas_call(
        body, grid=(8,),
        in_specs=[_spec((T, 128), lambda c: (0, c0 + c)), _spec((4, 128), lambda c: (0, c)), vec, mat, vec, mat, vec, vec],
        out_specs=[_spec((T, 128), lambda c: (0, c)), _spec((T, 128), lambda c: (0, c))],
        out_shape=[jax.ShapeDtypeStruct((T, LRU_W), F32), jax.ShapeDtypeStruct((T, LRU_W), F32)],
        scratch_shapes=[pltpu.VMEM((T + 2 * HALO, 128), F32)],
        compiler_params=_params(), name="lru_gates_fwd")(proj, cw, cb, wa2, ba, wx2, bx, lam)


def lru_scan_fwd(a, u):
    def body(a_ref, u_ref, h_ref):
        def step(i, h):
            base = pl.multiple_of(i * 8, 8)
            for k in range(8):
                h = a_ref[pl.ds(base + k, 1), :] * h + u_ref[pl.ds(base + k, 1), :]
                h_ref[pl.ds(base + k, 1), :] = h
            return h

        lax.fori_loop(0, T // 8, step, jnp.zeros((1, LRU_W), F32))

    return pl.pallas_call(body, out_shape=jax.ShapeDtypeStruct((T, LRU_W), F32), compiler_params=_params(0), name="lru_scan_fwd")(a, u)


def lru_scan_bwd(a, dh_out):
    def body(a_ref, d_ref, o_ref):
        def step(i, carry):
            base = pl.multiple_of(T - 8 - i * 8, 8)
            for k in range(7, -1, -1):
                carry = d_ref[pl.ds(base + k, 1), :] + carry
                o_ref[pl.ds(base + k, 1), :] = carry
                carry = carry * a_ref[pl.ds(base + k, 1), :]
            return carry

        lax.fori_loop(0, T // 8, step, jnp.zeros((1, LRU_W), F32))

    return pl.pallas_call(body, out_shape=jax.ShapeDtypeStruct((T, LRU_W), F32), compiler_params=_params(0), name="lru_scan_bwd")(a, dh_out)


def lru_gates_bwd(dhs, hseq, proj, cw, cb, wa2, ba, wx2, bx, lam):
    def body(dh_ref, h_ref, x_ref, cw_ref, cb_ref, wa_ref, ba_ref, wx_ref, bx_ref, lam_ref,
             dx_ref, dcw_ref, dcb_ref, dwa_ref, dba_ref, dwx_ref, dbx_ref, dlam_ref, xpad, hpad, dpad):
        _fill_padded(xpad, x_ref)
        _fill_padded(hpad, h_ref)
        dpad[0:HALO, :] = jnp.zeros((HALO, 128), F32)
        dpad[T + HALO:T + 2 * HALO, :] = jnp.zeros((HALO, 128), F32)
        for ref in (dcw_ref, dcb_ref, dwa_ref, dba_ref, dwx_ref, dbx_ref, dlam_ref):
            ref[...] = jnp.zeros_like(ref)
        lam = lam_ref[...]

        def first(r0):
            back = _back(xpad, r0)
            xr, xr_b, r, i, sp, a, mult, mult2 = _lru_gates(back, cw_ref[...], cb_ref[...], wa_ref[0], ba_ref[...], wx_ref[0], bx_ref[...], lam)
            dh = dh_ref[pl.ds(r0, Q), :]
            da = dh * _back(hpad, r0)(1)
            du = jnp.where(_rows(dh.shape, r0) >= NPAD, dh, 0.0)
            dmult = du * (i * xr)
            di = du * (mult * xr)
            dxr = du * (mult * i)
            dla = da * a - dmult * (a * a) * lax.rsqrt(mult2)
            dr = dla * ((-LRU_C) * sp)
            dlam_ref[...] += jnp.sum(dla * ((-LRU_C) * r), axis=0, keepdims=True)
            dpr = dr * r * (1.0 - r)
            dpi = di * i * (1.0 - i)
            dba_ref[...] += jnp.sum(dpr, axis=0, keepdims=True)
            dbx_ref[...] += jnp.sum(dpi, axis=0, keepdims=True)
            dpr_b = dpr.astype(BF)
            dpi_b = dpi.astype(BF)
            dxr = (dxr + lax.dot_general(dpr_b, wa_ref[0], NT_DIMS, preferred_element_type=F32)
                   + lax.dot_general(dpi_b, wx_ref[0], NT_DIMS, preferred_element_type=F32))
            dwa_ref[0] += lax.dot_general(xr_b, dpr_b, TN_DIMS, preferred_element_type=F32)
            dwx_ref[0] += lax.dot_general(xr_b, dpi_b, TN_DIMS, preferred_element_type=F32)
            dpad[pl.ds(r0 + HALO, Q), :] = dxr
            dcw, dcb = _conv_bwd_w(dxr, back)
            dcw_ref[...] += dcw
            dcb_ref[...] += dcb

        _chunks(first, unrolled=True)
        dlam_ref[...] = -dlam_ref[...] * _sigmoid_gate(-lam)

        def second(r0):
            dx_ref[pl.ds(r0, Q), :] = _conv_bwd_x(_ahead(dpad, r0), cw_ref[...]).astype(BF)

        _chunks(second)

    c0 = PXL // 128
    vec = _spec((1, 128), lambda c: (0, c))
    mat = _spec((1, 128, 128), lambda c: (c, 0, 0))
    col = _spec((T, 128), lambda c: (0, c))
    vshape = jax.ShapeDtypeStruct((1, LRU_W), F32)
    mshape = jax.ShapeDtypeStruct((8, 128, 128), F32)
    pad = pltpu.VMEM((T + 2 * HALO, 128), F32)
    return pl.pallas_call(
        body, grid=(8,),
        in_specs=[col, col, _spec((T, 128), lambda c: (0, c0 + c)), _spec((4, 128), lambda c: (0, c)), vec, mat, vec, mat, vec, vec],
        out_specs=[col, _spec((4, 128), lambda c: (0, c)), vec, mat, vec, mat, vec, vec],
        out_shape=[jax.ShapeDtypeStruct((T, LRU_W), BF), jax.ShapeDtypeStruct((4, LRU_W), F32), vshape, mshape, vshape, mshape, vshape, vshape],
        scratch_shapes=[pad, pad, pad], compiler_params=_params(), name="lru_gates_bwd")(dhs, hseq, proj, cw, cb, wa2, ba, wx2, bx, lam)


def gate_up(h1, wn, w_gate, w_up):
    def body(h_ref, wn_ref, wg_ref, wu_ref, gt_ref, up_ref, act_ref, u_ref):
        for r in (0, HALF):
            u_ref[r:r + HALF, :] = _rms(h_ref[r:r + HALF, :], wn_ref[...]).astype(BF)

        def tile(c0):
            cols = pl.ds(c0, 256)
            gt = lax.dot_general(u_ref[...], wg_ref[cols, :], NT_DIMS, preferred_element_type=F32)
            up = lax.dot_general(u_ref[...], wu_ref[cols, :], NT_DIMS, preferred_element_type=F32)
            gt_ref[:, cols] = gt.astype(BF)
            up_ref[:, cols] = up.astype(BF)
            act_ref[:, cols] = (gt * _sigmoid(gt) * up).astype(BF)

        _col_tiles(D_FF, 256, tile)

    big = jax.ShapeDtypeStruct((T, D_FF), BF)
    return pl.pallas_call(
        body, grid=(T // RC,), in_specs=[_rows_spec(D), _vec(D), _whole((D_FF, D)), _whole((D_FF, D))],
        out_specs=[_rows_spec(D_FF), _rows_spec(D_FF), _rows_spec(D_FF), _rows_spec(D)],
        out_shape=[big, big, big, jax.ShapeDtypeStruct((T, D), BF)],
        compiler_params=_params(), name="gate_up")(h1, wn, w_gate, w_up)


def down_loss(act, w_down, h1, target, wf):
    first = NPAD + N_META

    def body(a_ref, w_ref, r_ref, t_hbm, wf_ref, d_ref, db_ref, l_ref, dw_ref, h_scr, t_ref, t_sem):
        i = pl.program_id(0)
        _zero_at_first(l_ref, dw_ref)
        head = pltpu.make_async_copy(t_hbm.at[pl.ds(0, RC - first)], t_ref.at[pl.ds(first, RC - first)], t_sem)
        rest = pltpu.make_async_copy(t_hbm.at[pl.ds(pl.multiple_of(jnp.maximum(i * RC - first, 0), 32), RC)], t_ref, t_sem)

        @pl.when(i == 0)
        def _():
            t_ref[0:first, :] = jnp.zeros((first, D), F32)
            head.start()

        @pl.when(i > 0)
        def _():
            rest.start()

        def tile(c0):
            cols = pl.ds(c0, 512)
            h_scr[:, cols] = r_ref[:, cols] + jnp.dot(a_ref[...], w_ref[:, cols], preferred_element_type=F32)

        _col_tiles(D, 512, tile)

        @pl.when(i == 0)
        def _():
            head.wait()

        @pl.when(i > 0)
        def _():
            rest.wait()

        for r in (0, HALF):
            h = h_scr[r:r + HALF, :]
            live = _rows((HALF, D), i * RC + r) >= first
            err = jnp.where(live, _rms(h, wf_ref[...]) - t_ref[r:r + HALF, :], 0.0)
            l_ref[...] += 0.5 * jnp.sum(jnp.sum(err * err, axis=1, keepdims=True) * (1.0 / D), axis=0, keepdims=True)
            dh, dw = _rms_bwd(err * (1.0 / D), h, wf_ref[...])
            dw_ref[...] += jnp.sum(dw, axis=0, keepdims=True)
            d_ref[r:r + HALF, :] = dh
            db_ref[r:r + HALF, :] = dh.astype(BF)

    return pl.pallas_call(
        body, grid=(T // RC,),
        in_specs=[_rows_spec(D_FF), _whole((D_FF, D)), _rows_spec(D), pl.BlockSpec(memory_space=pl.ANY), _vec(D)],
        out_specs=[_rows_spec(D), _rows_spec(D), _spec((1, 128), lambda i: (0, 0)), _vec(D)],
        out_shape=[jax.ShapeDtypeStruct((T, D), F32), jax.ShapeDtypeStruct((T, D), BF), jax.ShapeDtypeStruct((1, 128), F32),
                   jax.ShapeDtypeStruct((1, D), F32)],
        scratch_shapes=[pltpu.VMEM((RC, D), F32), pltpu.VMEM((RC, D), F32), pltpu.SemaphoreType.DMA],
        compiler_params=_params(), name="down_loss")(act, w_down, h1, target, wf)


def swiglu_bwd(dh2_b, w_down, gt, up):
    def body(d_ref, w_ref, gt_ref, up_ref, dg_ref, du_ref):
        def tile(c0):
            cols = pl.ds(c0, 256)
            dact = lax.dot_general(d_ref[...], w_ref[cols, :], NT_DIMS, preferred_element_type=F32)
            gt_ = gt_ref[:, cols].astype(F32)
            up_ = up_ref[:, cols].astype(F32)
            sg = _sigmoid(gt_)
            dg_ref[:, cols] = (dact * up_ * (sg * (1.0 + gt_ * (1.0 - sg)))).astype(BF)
            du_ref[:, cols] = (dact * (gt_ * sg)).astype(BF)

        _col_tiles(D_FF, 256, tile)

    big = jax.ShapeDtypeStruct((T, D_FF), BF)
    return pl.pallas_call(
        body, grid=(T // RC,), in_specs=[_rows_spec(D), _whole((D_FF, D)), _rows_spec(D_FF), _rows_spec(D_FF)],
        out_specs=[_rows_spec(D_FF), _rows_spec(D_FF)], out_shape=[big, big], compiler_params=_params(), name="swiglu_bwd")(dh2_b, w_down, gt, up)


def gate_up_bwd(dgt, dup, w_gate, w_up, h1, wn, dh2):
    def body(dg_ref, du_ref, wg_ref, wu_ref, h_ref, wn_ref, r_ref, d_ref, db_ref, dw_ref, du_scr):
        _zero_at_first(dw_ref)

        du_scr[...] = jnp.zeros_like(du_scr)

        def tile(c0):
            k = pl.ds(c0, 256)
            du_scr[...] += (jnp.dot(dg_ref[:, k], wg_ref[k, :], preferred_element_type=F32)
                            + jnp.dot(du_ref[:, k], wu_ref[k, :], preferred_element_type=F32))

        _col_tiles(D_FF, 256, tile)
        for r in (0, HALF):
            dh, dw = _rms_bwd(du_scr[r:r + HALF, :], h_ref[r:r + HALF, :], wn_ref[...])
            dw_ref[...] += jnp.sum(dw, axis=0, keepdims=True)
            dh = dh + r_ref[r:r + HALF, :]
            d_ref[r:r + HALF, :] = dh
            db_ref[r:r + HALF, :] = dh.astype(BF)

    return pl.pallas_call(
        body, grid=(T // RC,),
        in_specs=[_rows_spec(D_FF), _rows_spec(D_FF), _whole((D_FF, D)), _whole((D_FF, D)), _rows_spec(D), _vec(D), _rows_spec(D)],
        out_specs=[_rows_spec(D), _rows_spec(D), _vec(D)],
        out_shape=[jax.ShapeDtypeStruct((T, D), F32), jax.ShapeDtypeStruct((T, D), BF), jax.ShapeDtypeStruct((1, D), F32)],
        scratch_shapes=[pltpu.VMEM((RC, D), F32)],
        compiler_params=_params(), name="gate_up_bwd")(dgt, dup, w_gate, w_up, h1, wn, dh2)


def _adamw(w, g, m, v):
    m = ADAM_B1 * m + (1.0 - ADAM_B1) * g
    v = ADAM_B2 * v + (1.0 - ADAM_B2) * (g * g)
    m_hat = m / (1.0 - ADAM_B1 ** ADAM_STEP)
    v_hat = v / (1.0 - ADAM_B2 ** ADAM_STEP)
    delta = -ADAM_LR * (m_hat / (jnp.sqrt(v_hat) + ADAM_EPS) + ADAM_WD * w)
    return delta, m, v


def adamw_shard(name, recv, w, m, v, tr, tc):
    r, c = w.shape

    def body(p_ref, w_ref, m_ref, v_ref, g_ref, d_ref, mo_ref, vo_ref):
        g = p_ref[0].astype(F32)
        for s in range(1, 8):
            g = g + p_ref[s].astype(F32)
        g_ref[...] = g
        d_ref[...], mo_ref[...], vo_ref[...] = _adamw(w_ref[...], g, m_ref[...], v_ref[...])

    tile = _spec((tr, tc), lambda i, j: (i, j))
    shape = jax.ShapeDtypeStruct((r, c), F32)
    return pl.pallas_call(
        body, grid=(r // tr, c // tc), in_specs=[_spec((8, tr, tc), lambda i, j: (0, i, j)), tile, tile, tile],
        out_specs=[tile] * 4, out_shape=[shape] * 4, compiler_params=_params(2), name=name)(recv, w, m, v)


def sum_slabs(recv):
    def body(p_ref, o_ref):
        g = p_ref[0]
        for s in range(1, 8):
            g = g + p_ref[s]
        o_ref[...] = g

    return pl.pallas_call(body, out_shape=jax.ShapeDtypeStruct(recv.shape[1:], F32), compiler_params=_params(0), name="sum_slabs")(recv)


SIMPLE = [("norm1_w", 1024), ("ssd_conv_b", 1536), ("ssd_dt_bias", 16), ("ssd_a_log", 16), ("ssd_d", 16), ("ssd_norm_w", 1024),
          ("lru_conv_b", 1024), ("lru_ba", 1024), ("lru_bx", 1024), ("lru_lambda", 1024), ("lru_norm_w", 1024), ("norm2_w", 1024),
          ("final_norm_w", 1024)]
SPECIAL = ["lru_wa", "lru_wx", "meta_tokens", "ssd_conv_w", "lru_conv_w"]
SM_ROWS = 176
SM_WA, SM_WX, SM_META, SM_SCW, SM_LCW, SM_LOSS = 14, 78, 142, 158, 166, 170


def _simple_rows():
    rows, r = {}, 0
    for name, n in SIMPLE:
        rows[name] = r
        r += -(-n // 1024)
    return rows


def adamw_small(sm, special_g, ws, ms, vs):
    rows = _simple_rows()
    ns, nx = len(SIMPLE), len(SPECIAL)

    def body(*refs):
        sm_ref = refs[0]
        gx = refs[1:1 + nx]
        wr = refs[1 + nx:1 + nx + ns + nx]
        mr = refs[1 + nx + ns + nx:1 + nx + 2 * (ns + nx)]
        vr = refs[1 + nx + 2 * (ns + nx):1 + nx + 3 * (ns + nx)]
        outs = refs[1 + nx + 3 * (ns + nx):]
        o = 0
        for k, (name, n) in enumerate(SIMPLE):
            r0 = rows[name]
            for c0 in range(0, n, 1024):
                wd = min(1024, n - c0)
                g = sm_ref[r0 + c0 // 1024:r0 + c0 // 1024 + 1, 0:wd]
                sl = (slice(None), slice(c0, c0 + wd))
                d, m2, v2 = _adamw(wr[k][sl], g, mr[k][sl], vr[k][sl])
                outs[o][sl] = g
                outs[o + 1][sl] = d
                outs[o + 2][sl] = m2
                outs[o + 3][sl] = v2
            o += 4
        for k in range(nx):
            d, m2, v2 = _adamw(wr[ns + k][...], gx[k][...], mr[ns + k][...], vr[ns + k][...])
            outs[o][...] = d
            outs[o + 1][...] = m2
            outs[o + 2][...] = v2
            o += 3

    out_shape = []
    for k in range(ns):
        out_shape += [jax.ShapeDtypeStruct(ws[k].shape, F32)] * 4
    for k in range(nx):
        out_shape += [jax.ShapeDtypeStruct(ws[ns + k].shape, F32)] * 3
    return pl.pallas_call(body, out_shape=out_shape, compiler_params=_params(0), name="adamw_small")(sm, *special_g, *ws, *ms, *vs)


def _place():
    return lax.axis_index("x"), lax.axis_index("y"), lax.axis_index("c")


def _index(px, py, pc):
    return 4 * px + 2 * py + pc


def all_gather(name, shards):
    n = len(shards)
    hbm = pl.BlockSpec(memory_space=pl.ANY)

    def body(*refs):
        ins, outs = refs[:n], refs[n:2 * n]
        send_sems, recv_sems, local_sems = refs[2 * n:]
        x, y, c = _place()
        me, sibling = (x, y, c), (x, y, 1 - c)
        chips = [(1 - x, y), (x, 1 - y), (1 - x, 1 - y)]

        def copy(i, k, block, to, src=None):
            dst = outs[i].at[_index(*block)]
            return pltpu.make_async_remote_copy(src_ref=dst if src is None else src, dst_ref=dst, send_sem=send_sems.at[7 * i + k],
                                                recv_sem=recv_sems.at[7 * i + k], device_id=to, device_id_type=MESH)

        mine = [pltpu.make_async_copy(ins[i], outs[i].at[_index(*me)], local_sems.at[i]) for i in range(n)]
        for cp in mine:
            cp.start()
        first = []
        for i in range(n):
            first += [copy(i, 1 + j, me, (*chip, c), src=ins[i]) for j, chip in enumerate(chips)]
            first.append(copy(i, 0, me, sibling, src=ins[i]))
        for cp in first:
            cp.start()
        passed = []
        for i in range(n):
            for j, chip in enumerate(chips):
                copy(i, 1 + j, (*chip, c), me).wait_recv()
                cp = copy(i, 4 + j, (*chip, c), sibling)
                cp.start()
                passed.append(cp)
        for i in range(n):
            copy(i, 0, sibling, me).wait_recv()
            for j, chip in enumerate(chips):
                copy(i, 4 + j, (*chip, 1 - c), me).wait_recv()
        for cp in first + passed:
            cp.wait_send()
        for cp in mine:
            cp.wait()

    return pl.pallas_call(
        body, in_specs=[hbm] * n, out_specs=[hbm] * n,
        out_shape=[jax.ShapeDtypeStruct((8,) + s.shape, s.dtype) for s in shards],
        scratch_shapes=[pltpu.SemaphoreType.DMA((7 * n,)), pltpu.SemaphoreType.DMA((7 * n,)), pltpu.SemaphoreType.DMA((n,))],
        name=name)(*shards)


HBM_SPEC = pl.BlockSpec(memory_space=pltpu.HBM)
SEM_SPEC = pl.BlockSpec(memory_space=pltpu.SEMAPHORE)
EFFECT = pltpu.SideEffectType.DATAFLOW_SIDE_EFFECTING


def _peers(x, y, c):
    return [((1 - x) if k & 4 else x, (1 - y) if k & 2 else y, (1 - c) if k & 1 else c) for k in range(1, 8)]


def _pieces(rows):
    for n in (4, 2):
        if rows % (16 * n) == 0:
            return [(r * (rows // n), rows // n) for r in range(n)]
    return [(0, rows)]


def _peer_copies(src, land, send_sems, recv_sems, k, peer, mine, slab_src):
    block = src.at[_index(*peer)] if slab_src else src
    return [pltpu.make_async_remote_copy(src_ref=block.at[pl.ds(r0, nr)], dst_ref=land.at[mine, pl.ds(r0, nr)], send_sem=send_sems.at[k],
                                         recv_sem=recv_sems.at[k], device_id=peer, device_id_type=MESH)
            for r0, nr in _pieces(block.shape[0])]


def copies_start(name, srcs, slab_src, after):
    n = len(srcs)
    zones = [jax.ShapeDtypeStruct(s.shape if slab_src else (8,) + s.shape, s.dtype) for s in srcs]

    def body(*refs):
        ins, lands = refs[:n], refs[n:2 * n]
        sends, recvs = refs[2 * n + 1:3 * n + 1], refs[3 * n + 1:4 * n + 1]
        token = refs[-1]
        x, y, c = _place()
        mine = _index(x, y, c)
        for i in range(n):
            per_peer = [_peer_copies(ins[i], lands[i], sends[i], recvs[i], k, peer, mine, slab_src) for k, peer in enumerate(_peers(x, y, c))]
            for piece in zip(*per_peer):
                for cp in piece:
                    cp.start()
        token[...] = jnp.zeros_like(token)

    sem = pltpu.SemaphoreType.DMA((7,))
    res = pl.pallas_call(
        body, name=name,
        out_shape=([sem] * (2 * n) + [pltpu.HBM(s.shape, s.dtype) for s in srcs] + [pltpu.HBM(z.shape, z.dtype) for z in zones]
                   + [jax.ShapeDtypeStruct((8, 128), F32)]),
        in_specs=[HBM_SPEC] * (2 * n) + [pl.BlockSpec(memory_space=pl.ANY)],
        out_specs=[SEM_SPEC] * (2 * n) + [HBM_SPEC] * (2 * n) + [pl.BlockSpec(memory_space=pltpu.VMEM)],
        input_output_aliases={i: 2 * n + i for i in range(2 * n)},
        compiler_params=pltpu.CompilerParams(has_side_effects=EFFECT),
    )(*[pltpu.with_memory_space_constraint(s, pltpu.HBM) for s in srcs],
      *[pltpu.with_memory_space_constraint(lax.empty(z.shape, z.dtype), pltpu.HBM) for z in zones], after)
    return [(res[i], res[n + i], res[2 * n + i], res[3 * n + i]) for i in range(n)], res[-1][0:1, 0:1]


def copies_wait(name, started, slab_src, after):
    n = len(started)

    def body(*refs):
        ins, lands = refs[:n], refs[n:2 * n]
        sends, recvs = refs[2 * n:3 * n], refs[3 * n:4 * n]
        x, y, c = _place()
        mine = _index(x, y, c)
        for i in range(n):
            for k, peer in enumerate(_peers(x, y, c)):
                arrival = pltpu.make_async_remote_copy(src_ref=ins[i].at[mine] if slab_src else ins[i], dst_ref=lands[i].at[_index(*peer)],
                                                       send_sem=sends[i].at[k], recv_sem=recvs[i].at[k], device_id=peer, device_id_type=MESH)
                arrival.wait_send()
                arrival.wait_recv()

    srcs = [s[2] for s in started]
    lands = [s[3] for s in started]
    afters = list(after) if isinstance(after, (list, tuple)) else [after]
    res = pl.pallas_call(
        body, name=name,
        out_shape=[pltpu.HBM(s.shape, s.dtype) for s in srcs] + [pltpu.HBM(z.shape, z.dtype) for z in lands],
        in_specs=[HBM_SPEC] * (2 * n) + [SEM_SPEC] * (2 * n) + [pl.BlockSpec(memory_space=pl.ANY)] * len(afters),
        out_specs=[HBM_SPEC] * (2 * n),
        input_output_aliases={i: i for i in range(2 * n)},
        compiler_params=pltpu.CompilerParams(has_side_effects=EFFECT),
    )(*srcs, *lands, *[s[0] for s in started], *[s[1] for s in started], *afters)
    me = _index(*_place())
    own = [lax.dynamic_index_in_dim(s, me, 0, keepdims=True) if slab_src else s[None] for s in res[:n]]
    return [lax.dynamic_update_slice_in_dim(z, o, me, 0) for z, o in zip(res[n:], own)]


WEIGHTS = ["meta_tokens", "norm1_w", "w_in", "ssd_conv_w", "ssd_conv_b", "ssd_dt_bias", "ssd_a_log", "ssd_d", "ssd_norm_w", "lru_conv_w",
           "lru_conv_b", "lru_wa", "lru_ba", "lru_wx", "lru_bx", "lru_lambda", "lru_norm_w", "w_out", "norm2_w", "w_gate", "w_up", "w_down",
           "final_norm_w"]
BIG = ["w_in", "w_out", "w_gate", "w_up", "w_down"]
COLUMN_SHARDED = ["w_in", "w_gate", "w_up"]
BIG_TILE = {"w_in": (578, 256), "w_out": (128, 1024), "w_gate": (176, 1024), "w_up": (176, 1024), "w_down": (176, 1024)}


def _pair_blocks(w):
    w = w.reshape(8, 2, 64, 64)
    z = jnp.zeros((8, 64, 64), w.dtype)
    return jnp.concatenate([jnp.concatenate([w[:, 0], z], axis=2), jnp.concatenate([z, w[:, 1]], axis=2)], axis=1)


def _unpair_blocks(w2):
    return jnp.stack([w2[:, :64, :64], w2[:, 64:, 64:]], axis=1).reshape(16, 64, 64)


def _per_group(v):
    return jnp.pad(v.reshape(2, 1, 8), ((0, 0), (0, 0), (0, 120)))


def _pad_cols(v, n):
    return jnp.pad(v, ((0, 0), (0, n - v.shape[1])))


def local_step(x, target, meta, ssd_cw, lru_cw, w_in, fetch, send, p):
    z120 = jnp.zeros((120, D), BF)
    w_dt = jnp.concatenate([w_in[2560:2568], z120, w_in[2568:2576], z120], axis=0)
    bias2, alog2, d2 = _per_group(p["ssd_dt_bias"]), _per_group(p["ssd_a_log"]), _per_group(p["ssd_d"])
    wa2 = _pair_blocks(p["lru_wa"]).astype(BF)
    wx2 = _pair_blocks(p["lru_wx"]).astype(BF)
    lru = (lru_cw, p["lru_conv_b"], wa2, p["lru_ba"], wx2, p["lru_bx"], p["lru_lambda"])

    h0 = jnp.concatenate([jnp.zeros((NPAD, D), F32), meta, x], axis=0)
    proj, u1 = in_proj(h0, p["norm1_w"], w_in, w_dt)
    xbc_act = conv_silu_fwd(proj, ssd_cw, p["ssd_conv_b"])
    yn_ssd, y_pre, h_prev = ssd_fwd(xbc_act, proj, bias2, alog2, d2, p["ssd_norm_w"])
    a, u = lru_gates_fwd(proj, *lru)
    hseq = lru_scan_fwd(a, u)
    (w_out,) = fetch(["w_out"], hseq)
    h1, cat = out_proj(yn_ssd, proj, hseq, p["lru_norm_w"], w_out, h0)
    w_gate, w_up = fetch(["w_gate", "w_up"], h1)
    gt, up, act, u2 = gate_up(h1, p["norm2_w"], w_gate, w_up)
    (w_down,) = fetch(["w_down"], act)
    dh2, dh2_b, loss, d_fnw = down_loss(act, w_down, h1, target, p["final_norm_w"])

    dgt, dup = swiglu_bwd(dh2_b, w_down, gt, up)
    g_down = matmul_tn("dw_down", act, dh2_b, 1408, 512)
    g_gate = matmul_tn("dw_gate", dgt, u2, 1408, 512)
    g_up = matmul_tn("dw_up", dup, u2, 1408, 512)
    sent = send({"w_down": g_down, "w_gate": g_gate, "w_up": g_up})
    dh1, dh1_b, d_n2 = gate_up_bwd(dgt, dup, w_gate, w_up, h1, p["norm2_w"] + sent, dh2)
    sent = send({"w_out": matmul_tn("dw_out", cat, dh1_b, 512, 1024)})
    dyn, dh_out, dg_b, d_lnw = out_proj_bwd(dh1_b, w_out, proj, hseq, p["lru_norm_w"] + sent)

    dhs = lru_scan_bwd(a, dh_out)
    dxl_b, d_lcw, d_lcb, dwa2, d_ba, dwx2, d_bx, d_lam = lru_gates_bwd(dhs, hseq, proj, *lru)
    dz_b, dx, d_b, d_c, ddt_b, dpar, d_snw = ssd_bwd(dyn, xbc_act, proj, y_pre, h_prev, bias2, alog2, d2, p["ssd_norm_w"])
    dxbc_b, d_scw, d_scb = conv_silu_bwd(dx, d_b, d_c, proj, ssd_cw, p["ssd_conv_b"])
    g_dt = matmul_tn("dw_in_dt", ddt_b, u1, 256, 512)
    g_in = jnp.concatenate([matmul_tn("dw_in_z", dz_b, u1, 512, 1024), matmul_tn("dw_in_xbc", dxbc_b, u1, 512, 1024), g_dt[0:8], g_dt[128:136],
                            matmul_tn("dw_in_g", dg_b, u1, 512, 1024), matmul_tn("dw_in_xl", dxl_b, u1, 512, 1024)], axis=0)
    sent = send({"w_in": g_in})
    grad_x, d_meta, d_n1 = in_proj_bwd(dz_b, dg_b, dxl_b, dxbc_b, ddt_b, w_in, w_dt, h0, p["norm1_w"] + sent, dh1)
    small = {"norm1_w": d_n1, "ssd_conv_b": d_scb, "ssd_dt_bias": dpar[:, 0, :8].reshape(1, 16), "ssd_a_log": dpar[:, 1, :8].reshape(1, 16),
             "ssd_d": dpar[:, 2, :8].reshape(1, 16), "ssd_norm_w": d_snw, "lru_conv_b": d_lcb, "lru_ba": d_ba, "lru_bx": d_bx,
             "lru_lambda": d_lam, "lru_norm_w": d_lnw, "norm2_w": d_n2, "final_norm_w": d_fnw,
             "lru_wa": _unpair_blocks(dwa2), "lru_wx": _unpair_blocks(dwx2), "meta_tokens": d_meta,
             "ssd_conv_w": d_scw, "lru_conv_w": d_lcw}
    return loss, grad_x, small


def _pack_small(small, loss):
    rows = [_pad_cols(small[name], -(-n // 1024) * 1024).reshape(-1, 1024) for name, n in SIMPLE]
    rows += [small["lru_wa"].reshape(64, 1024), small["lru_wx"].reshape(64, 1024), small["meta_tokens"],
             _pad_cols(small["ssd_conv_w"], 2048).reshape(8, 1024), small["lru_conv_w"], _pad_cols(loss[:, 0:1], 1024)]
    sm = jnp.concatenate(rows, axis=0)
    return jnp.pad(sm, ((0, SM_ROWS - sm.shape[0]), (0, 0)))


def _slabs(g):
    return g.reshape(8, g.shape[0] // 8, g.shape[1])


def _unslab(g):
    return g.reshape(8 * g.shape[1], g.shape[2])


def kernel(x, meta_tokens, norm1_w, w_in, ssd_conv_w, ssd_conv_b, ssd_dt_bias, ssd_a_log, ssd_d, ssd_norm_w, lru_conv_w, lru_conv_b, lru_wa, lru_ba, lru_wx, lru_bx, lru_lambda, lru_norm_w, w_out, norm2_w, w_gate, w_up, w_down, final_norm_w, loss_target, m_meta_tokens, m_norm1_w, m_w_in, m_ssd_conv_w, m_ssd_conv_b, m_ssd_dt_bias, m_ssd_a_log, m_ssd_d, m_ssd_norm_w, m_lru_conv_w, m_lru_conv_b, m_lru_wa, m_lru_ba, m_lru_wx, m_lru_bx, m_lru_lambda, m_lru_norm_w, m_w_out, m_norm2_w, m_w_gate, m_w_up, m_w_down, m_final_norm_w, v_meta_tokens, v_norm1_w, v_w_in, v_ssd_conv_w, v_ssd_conv_b, v_ssd_dt_bias, v_ssd_a_log, v_ssd_d, v_ssd_norm_w, v_lru_conv_w, v_lru_conv_b, v_lru_wa, v_lru_ba, v_lru_wx, v_lru_bx, v_lru_lambda, v_lru_norm_w, v_w_out, v_norm2_w, v_w_gate, v_w_up, v_w_down, v_final_norm_w):
    w = dict(meta_tokens=meta_tokens, norm1_w=norm1_w, w_in=w_in[0], ssd_conv_w=ssd_conv_w[0], ssd_conv_b=ssd_conv_b, ssd_dt_bias=ssd_dt_bias,
             ssd_a_log=ssd_a_log, ssd_d=ssd_d, ssd_norm_w=ssd_norm_w, lru_conv_w=lru_conv_w[0], lru_conv_b=lru_conv_b, lru_wa=lru_wa[0],
             lru_ba=lru_ba, lru_wx=lru_wx[0], lru_bx=lru_bx, lru_lambda=lru_lambda, lru_norm_w=lru_norm_w, w_out=w_out[0], norm2_w=norm2_w,
             w_gate=w_gate[0], w_up=w_up[0], w_down=w_down[0], final_norm_w=final_norm_w.reshape(1, D))
    m = dict(meta_tokens=m_meta_tokens, norm1_w=m_norm1_w, w_in=m_w_in[0], ssd_conv_w=m_ssd_conv_w[0], ssd_conv_b=m_ssd_conv_b,
             ssd_dt_bias=m_ssd_dt_bias, ssd_a_log=m_ssd_a_log, ssd_d=m_ssd_d, ssd_norm_w=m_ssd_norm_w, lru_conv_w=m_lru_conv_w[0],
             lru_conv_b=m_lru_conv_b, lru_wa=m_lru_wa[0], lru_ba=m_lru_ba, lru_wx=m_lru_wx[0], lru_bx=m_lru_bx, lru_lambda=m_lru_lambda,
             lru_norm_w=m_lru_norm_w, w_out=m_w_out[0], norm2_w=m_norm2_w, w_gate=m_w_gate[0], w_up=m_w_up[0], w_down=m_w_down[0],
             final_norm_w=m_final_norm_w.reshape(1, D))
    v = dict(meta_tokens=v_meta_tokens, norm1_w=v_norm1_w, w_in=v_w_in[0], ssd_conv_w=v_ssd_conv_w[0], ssd_conv_b=v_ssd_conv_b,
             ssd_dt_bias=v_ssd_dt_bias, ssd_a_log=v_ssd_a_log, ssd_d=v_ssd_d, ssd_norm_w=v_ssd_norm_w, lru_conv_w=v_lru_conv_w[0],
             lru_conv_b=v_lru_conv_b, lru_wa=v_lru_wa[0], lru_ba=v_lru_ba, lru_wx=v_lru_wx[0], lru_bx=v_lru_bx, lru_lambda=v_lru_lambda,
             lru_norm_w=v_lru_norm_w, w_out=v_w_out[0], norm2_w=v_norm2_w, w_gate=v_w_gate[0], w_up=v_w_up[0], w_down=v_w_down[0],
             final_norm_w=v_final_norm_w.reshape(1, D))
    shapes = dict(meta_tokens=meta_tokens.shape, norm1_w=norm1_w.shape, w_in=w_in.shape, ssd_conv_w=ssd_conv_w.shape,
                  ssd_conv_b=ssd_conv_b.shape, ssd_dt_bias=ssd_dt_bias.shape, ssd_a_log=ssd_a_log.shape, ssd_d=ssd_d.shape,
                  ssd_norm_w=ssd_norm_w.shape, lru_conv_w=lru_conv_w.shape, lru_conv_b=lru_conv_b.shape, lru_wa=lru_wa.shape,
                  lru_ba=lru_ba.shape, lru_wx=lru_wx.shape, lru_bx=lru_bx.shape, lru_lambda=lru_lambda.shape, lru_norm_w=lru_norm_w.shape,
                  w_out=w_out.shape, norm2_w=norm2_w.shape, w_gate=w_gate.shape, w_up=w_up.shape, w_down=w_down.shape,
                  final_norm_w=final_norm_w.shape)
    me = _index(*_place())
    for n in COLUMN_SHARDED:
        w[n], m[n], v[n] = w[n].T, m[n].T, v[n].T

    small_shard = jnp.concatenate([w["meta_tokens"], _pad_cols(w["ssd_conv_w"], 256).reshape(8, 128), w["lru_conv_w"],
                                   jnp.zeros((4, 128), F32)], axis=0)
    g_in, gs = all_gather("gather_w_in", [w["w_in"].astype(BF), small_shard])
    later = ["w_out", "w_gate", "w_up", "w_down"]
    started, behind = copies_start("gather_rest_start", [w[n].astype(BF) for n in later], False, gs)
    started = dict(zip(later, started))
    meta_full = gs[:, 0:16].transpose(1, 0, 2).reshape(N_META, D)
    ssd_cw = gs[:, 16:24].reshape(8, 4, 256)[:, :, :192].transpose(1, 0, 2).reshape(4, XBC)
    lru_cw = gs[:, 24:28].transpose(1, 0, 2).reshape(4, LRU_W)

    def fetch(names, after):
        got = copies_wait("gather_" + names[0] + "_wait", [started[n] for n in names], False, after)
        return [_unslab(g) for g in got]

    in_flight = {}

    def send(grads):
        names = list(grads)
        st, token = copies_start("grads_" + names[0] + "_start", [grads[n] if n == "small" else _slabs(grads[n]) for n in names], True,
                                 grads[names[0]])
        in_flight.update(zip(names, st))
        return token

    loss, grad_x, small = local_step(x[0], loss_target[0], meta_full, ssd_cw, lru_cw, _unslab(g_in), fetch, send,
                                     {**w, "norm1_w": w["norm1_w"] + behind})
    send({"small": _pack_small(small, loss).reshape(8, SM_ROWS // 8, 1024)})

    out = {}
    early = ["w_down", "w_gate", "w_up", "w_out"]
    recv = dict(zip(early, copies_wait("grads_early_wait", [in_flight[n] for n in early], True, in_flight["small"][2])))
    for n in early:
        out[n] = adamw_shard("adamw_" + n, recv[n], w[n], m[n], v[n], *BIG_TILE[n])
    recv_in, recv_small = copies_wait("grads_late_wait", [in_flight["w_in"], in_flight["small"]], True, [out[n][0] for n in early])
    out["w_in"] = adamw_shard("adamw_w_in", recv_in, w["w_in"], m["w_in"], v["w_in"], *BIG_TILE["w_in"])
    for n in COLUMN_SHARDED:
        out[n] = [o.T for o in out[n]]
    sm = all_gather("gather_small_grads", [sum_slabs(recv_small)])[0].reshape(SM_ROWS, 1024)
    special_g =[sm[SM_WA:SM_WA + 64].reshape(16, 64, 64), sm[SM_WX:SM_WX + 64].reshape(16, 64, 64),
                 lax.dynamic_slice(sm[SM_META:SM_META + 16], (0, 128 * me), (16, 128)),
                 lax.dynamic_slice(sm[SM_SCW:SM_SCW + 8].reshape(4, 2048), (0, 192 * me), (4, 192)),
                 lax.dynamic_slice(sm[SM_LCW:SM_LCW + 4], (0, 128 * me), (4, 128))]
    names = [n for n, _ in SIMPLE] + SPECIAL
    res = adamw_small(sm, special_g, [w[n] for n in names], [m[n] for n in names], [v[n] for n in names])
    for k, (n, _) in enumerate(SIMPLE):
        out[n] = res[4 * k:4 * k + 4]
    for k, n in enumerate(SPECIAL):
        o = 4 * len(SIMPLE) + 3 * k
        out[n] = [special_g[k]] + list(res[o:o + 3])
    loss_total = sm[SM_LOSS, 0]
    flat = [loss_total, grad_x[None]]
    for k in range(4):
        flat += [out[n][k].reshape(shapes[n]) for n in WEIGHTS]
    return tuple(flat)
```

```python
import math

import jax
import jax.numpy as jnp
from jax import lax
from jax.experimental import pallas as pl
from jax.experimental.pallas import tpu as pltpu

F32 = jnp.float32
BF = jnp.bfloat16

D = 1024
SEQ = 2048
N_META = 16
Q = 128
NPAD = 112
T = NPAD + N_META + SEQ
NCH = T // Q
RC = 544
D_FF = 2816
SSD_W = 1024
LRU_W = 1024
XBC = 1536
IN_COLS = 4624
PZ, PG, PXL, PXBC, PDT = 0, 1024, 2048, 3072, 4608
NP_IN = 4864
EPS = 1e-6
LRU_C = 8.0
VMEM_LIMIT = 56 * 1024 * 1024

ADAM_LR, ADAM_B1, ADAM_B2, ADAM_EPS, ADAM_WD, ADAM_STEP = 0.001, 0.9, 0.999, 1e-08, 0.01, 10

NT_DIMS = (((1,), (1,)), ((), ()))
TN_DIMS = (((0,), (0,)), ((), ()))
MESH = pl.DeviceIdType.MESH


def _params(n_grid=1, limit=VMEM_LIMIT):
    return pltpu.CompilerParams(dimension_semantics=("arbitrary",) * n_grid, vmem_limit_bytes=limit)


def _spec(shape, imap, single=False):
    if single:
        return pl.BlockSpec(shape, imap, pipeline_mode=pl.Buffered(1))
    return pl.BlockSpec(shape, imap)


def _sigmoid(x):
    return 0.5 * jnp.tanh(0.5 * x) + 0.5


def _sigmoid_gate(x):
    return 1.0 / (1.0 + jnp.exp(-x))


def _softplus(x):
    return jnp.maximum(x, 0.0) + jnp.log(1.0 + jnp.exp(-jnp.abs(x)))


def _rms_stats(h):
    return lax.rsqrt(jnp.mean(h * h, axis=-1, keepdims=True) + EPS)


def _rms(h, w):
    return (h * _rms_stats(h)) * w


def _rms_bwd(du, h, w):
    r = _rms_stats(h)
    n = h * r
    dn = du * w
    dh = r * (dn - n * jnp.mean(dn * n, axis=-1, keepdims=True))
    return dh, du * n


_G0 = math.sqrt(2.0 / math.pi)


def _gelu(x):
    return 0.5 * x * (1.0 + jnp.tanh(_G0 * (x + 0.044715 * (x * x * x))))


def _gelu_grad(x):
    t = jnp.tanh(_G0 * (x + 0.044715 * (x * x * x)))
    return 0.5 * (1.0 + t) + 0.5 * x * (1.0 - t * t) * (_G0 * (1.0 + 3.0 * 0.044715 * (x * x)))


def _rows(shape, r0=0):
    return lax.broadcasted_iota(jnp.int32, shape, 0) + r0


def _lanes(shape):
    return lax.broadcasted_iota(jnp.int32, shape, 1)


HALO = 8


def _fill_padded(pad_ref, x_ref):
    pad_ref[0:HALO, :] = jnp.zeros((HALO, pad_ref.shape[1]), F32)
    pad_ref[T + HALO:T + 2 * HALO, :] = jnp.zeros((HALO, pad_ref.shape[1]), F32)

    def step(c, carry):
        r0 = pl.multiple_of(c * Q, Q)
        pad_ref[pl.ds(r0 + HALO, Q), :] = x_ref[pl.ds(r0, Q), :]
        return carry

    lax.fori_loop(0, NCH, step, 0)


def _back(pad_ref, r0):
    win = pad_ref[pl.ds(r0, Q + HALO), :]
    return lambda s: win[HALO:, :] if s == 0 else pltpu.roll(win, s, axis=0)[HALO:, :]


def _ahead(pad_ref, r0):
    win = pad_ref[pl.ds(r0 + HALO, Q + HALO), :]
    return lambda s: win[:Q, :] if s == 0 else pltpu.roll(win, Q + HALO - s, axis=0)[:Q, :]


def _conv(back, w, b):
    y = b + w[3:4, :] * back(0)
    for k in range(3):
        y = y + w[k:k + 1, :] * back(3 - k)
    return y


def _conv_bwd_x(ahead, w):
    dx = w[3:4, :] * ahead(0)
    for k in range(3):
        dx = dx + w[k:k + 1, :] * ahead(3 - k)
    return dx


def _conv_bwd_w(dy, back):
    dws = [jnp.sum(dy * back(3 - k), axis=0, keepdims=True) for k in range(4)]
    return jnp.concatenate(dws, axis=0), jnp.sum(dy, axis=0, keepdims=True)


def _chunks(fn, unrolled=False):
    if unrolled:
        for c in range(NCH):
            fn(c * Q)
        return

    def step(c, carry):
        fn(pl.multiple_of(c * Q, Q))
        return carry

    lax.fori_loop(0, NCH, step, 0)


HALF = RC // 2


def _col_tiles(n, tn, fn):
    def step(j, carry):
        fn(pl.multiple_of(j * tn, tn))
        return carry

    lax.fori_loop(0, n // tn, step, 0)


def _rows_spec(cols, block_col=0):
    return _spec((RC, cols), lambda i: (i, block_col))


def _whole(shape):
    return _spec(shape, lambda i: tuple(0 for _ in shape), single=True)


def _vec(cols):
    return _spec((1, cols), lambda i: (0, 0))


def _zero_at_first(*refs):
    @pl.when(pl.program_id(0) == 0)
    def _():
        for r in refs:
            r[...] = jnp.zeros_like(r)


IN_RUNS = ((PZ, 0, 1024), (PG, 2576, 2048), (PXBC, 1024, XBC))


def _in_tiles(fn):
    for pcol, wrow, width in IN_RUNS:
        def step(j, carry, pcol=pcol, wrow=wrow):
            fn(pl.multiple_of(pcol + j * 512, 512), pl.multiple_of(wrow + j * 512, 16))
            return carry

        lax.fori_loop(0, width // 512, step, 0)


def in_proj(h0, wn, w_t, w_dt):
    def body(h_ref, wn_ref, w_ref, wdt_ref, o_ref, u_ref):
        for r in (0, HALF):
            u_ref[r:r + HALF, :] = _rms(h_ref[r:r + HALF, :], wn_ref[...]).astype(BF)

        def tile(pcol, wrow):
            o_ref[:, pl.ds(pcol, 512)] = lax.dot_general(u_ref[...], w_ref[pl.ds(wrow, 512), :], NT_DIMS, preferred_element_type=F32)

        _in_tiles(tile)
        o_ref[:, PDT:PDT + 256] = lax.dot_general(u_ref[...], wdt_ref[...], NT_DIMS, preferred_element_type=F32)

    return pl.pallas_call(
        body, grid=(T // RC,), in_specs=[_rows_spec(D), _vec(D), _whole((IN_COLS, D)), _whole((256, D))],
        out_specs=[_rows_spec(NP_IN), _rows_spec(D)],
        out_shape=[jax.ShapeDtypeStruct((T, NP_IN), F32), jax.ShapeDtypeStruct((T, D), BF)],
        compiler_params=_params(), name="in_proj")(h0, wn, w_t, w_dt)


def out_proj(yn_ssd, proj, hseq, lru_nw, w_out, h0):
    def body(y_ref, g_ref, h_ref, wn_ref, w_ref, r_ref, o_ref, cat_ref):
        cat_ref[:, 0:SSD_W] = y_ref[...]
        for r in (0, HALF):
            y = _gelu(g_ref[r:r + HALF, :]) * h_ref[r:r + HALF, :]
            cat_ref[r:r + HALF, SSD_W:] = _rms(y, wn_ref[...]).astype(BF)

        def tile(c0):
            o_ref[:, pl.ds(c0, 512)] = r_ref[:, pl.ds(c0, 512)] + jnp.dot(cat_ref[...], w_ref[:, pl.ds(c0, 512)], preferred_element_type=F32)

        _col_tiles(D, 512, tile)

    return pl.pallas_call(
        body, grid=(T // RC,),
        in_specs=[_rows_spec(SSD_W), _rows_spec(LRU_W, PG // LRU_W), _rows_spec(LRU_W), _vec(LRU_W), _whole((SSD_W + LRU_W, D)), _rows_spec(D)],
        out_specs=[_rows_spec(D), _rows_spec(SSD_W + LRU_W)],
        out_shape=[jax.ShapeDtypeStruct((T, D), F32), jax.ShapeDtypeStruct((T, SSD_W + LRU_W), BF)],
        compiler_params=_params(), name="out_proj")(yn_ssd, proj, hseq, lru_nw, w_out, h0)


def out_proj_bwd(dh1_b, w_out, proj, hseq, lru_nw):
    def body(d_ref, w_ref, g_ref, h_ref, wn_ref, dy_ref, dh_ref, dg_ref, dw_ref, dl_scr):
        _zero_at_first(dw_ref)

        def tile(c0):
            dy_ref[:, pl.ds(c0, 512)] = lax.dot_general(d_ref[...], w_ref[pl.ds(c0, 512), :], NT_DIMS, preferred_element_type=F32)
            dl_scr[:, pl.ds(c0, 512)] = lax.dot_general(d_ref[...], w_ref[pl.ds(SSD_W + c0, 512), :], NT_DIMS, preferred_element_type=F32)

        _col_tiles(SSD_W, 512, tile)
        for r in (0, HALF):
            g = g_ref[r:r + HALF, :]
            h = h_ref[r:r + HALF, :]
            ge = _gelu(g)
            dy, dw = _rms_bwd(dl_scr[r:r + HALF, :], ge * h, wn_ref[...])
            dw_ref[...] += jnp.sum(dw, axis=0, keepdims=True)
            dh_ref[r:r + HALF, :] = dy * ge
            dg_ref[r:r + HALF, :] = (dy * h * _gelu_grad(g)).astype(BF)

    return pl.pallas_call(
        body, grid=(T // RC,),
        in_specs=[_rows_spec(D), _whole((SSD_W + LRU_W, D)), _rows_spec(LRU_W, PG // LRU_W), _rows_spec(LRU_W), _vec(LRU_W)],
        out_specs=[_rows_spec(SSD_W), _rows_spec(LRU_W), _rows_spec(LRU_W), _vec(LRU_W)],
        out_shape=[jax.ShapeDtypeStruct((T, SSD_W), F32), jax.ShapeDtypeStruct((T, LRU_W), F32), jax.ShapeDtypeStruct((T, LRU_W), BF),
                   jax.ShapeDtypeStruct((1, LRU_W), F32)],
        scratch_shapes=[pltpu.VMEM((RC, LRU_W), F32)],
        compiler_params=_params(), name="out_proj_bwd")(dh1_b, w_out, proj, hseq, lru_nw)


def in_proj_bwd(dz, dg, dxl, dxbc, ddt, w_t, w_dt, h0, wn, dh1):
    first = NPAD + N_META

    def body(dz_ref, dg_ref, dxl_ref, dxbc_ref, ddt_ref, w_ref, wdt_ref, h_ref, wn_ref, r_ref, gx_hbm, meta_ref, dw_ref, du_scr, o_ref, sem):
        i = pl.program_id(0)
        _zero_at_first(dw_ref)
        du_scr[...] = jnp.dot(ddt_ref[...], wdt_ref[...], preferred_element_type=F32)
        for d_ref, wrow, width in ((dz_ref, 0, 1024), (dxbc_ref, 1024, XBC), (dg_ref, 2576, 1024), (dxl_ref, 3600, 1024)):
            def step(j, carry, d_ref=d_ref, wrow=wrow):
                c0 = pl.multiple_of(j * 512, 512)
                du_scr[...] += jnp.dot(d_ref[:, pl.ds(c0, 512)], w_ref[pl.ds(pl.multiple_of(wrow + c0, 16), 512), :], preferred_element_type=F32)
                return carry

            lax.fori_loop(0, width // 512, step, 0)
        for r in (0, HALF):
            dh, dw = _rms_bwd(du_scr[r:r + HALF, :], h_ref[r:r + HALF, :], wn_ref[...])
            dw_ref[...] += jnp.sum(dw, axis=0, keepdims=True)
            o_ref[r:r + HALF, :] = dh + r_ref[r:r + HALF, :]

        @pl.when(i == 0)
        def _():
            meta_ref[...] = o_ref[NPAD:first, :]
            head = pltpu.make_async_copy(o_ref.at[pl.ds(first, RC - first)], gx_hbm.at[pl.ds(0, RC - first)], sem)
            head.start()
            head.wait()

        @pl.when(i > 0)
        def _():
            rest = pltpu.make_async_copy(o_ref, gx_hbm.at[pl.ds(pl.multiple_of(i * RC - first, 32), RC)], sem)
            rest.start()
            rest.wait()

    return pl.pallas_call(
        body, grid=(T // RC,),
        in_specs=[_rows_spec(SSD_W), _rows_spec(LRU_W), _rows_spec(LRU_W), _rows_spec(XBC), _rows_spec(256), _whole((IN_COLS, D)),
                  _whole((256, D)), _rows_spec(D), _vec(D), _rows_spec(D)],
        out_specs=[pl.BlockSpec(memory_space=pl.ANY), _spec((N_META, D), lambda i: (0, 0)), _vec(D)],
        out_shape=[jax.ShapeDtypeStruct((SEQ, D), F32), jax.ShapeDtypeStruct((N_META, D), F32), jax.ShapeDtypeStruct((1, D), F32)],
        scratch_shapes=[pltpu.VMEM((RC, D), F32), pltpu.VMEM((RC, D), F32), pltpu.SemaphoreType.DMA],
        compiler_params=_params(), name="in_proj_bwd")(dz, dg, dxl, dxbc, ddt, w_t, w_dt, h0, wn, dh1)


def matmul_tn(name, a, b, tm, tn):
    m, n = a.shape[1], b.shape[1]

    def body(a_ref, b_ref, o_ref, acc_ref):
        acc_ref[...] = jnp.zeros_like(acc_ref)

        def mm(r0):
            acc_ref[...] += lax.dot_general(a_ref[pl.ds(r0, RC), :], b_ref[pl.ds(r0, RC), :], TN_DIMS, preferred_element_type=F32)

        _col_tiles(T, RC, mm)
        o_ref[...] = acc_ref[...].astype(BF)

    return pl.pallas_call(
        body, grid=(m // tm, n // tn),
        in_specs=[_spec((T, tm), lambda i, j: (0, i)), _spec((T, tn), lambda i, j: (0, j))],
        out_specs=_spec((tm, tn), lambda i, j: (i, j)),
        out_shape=jax.ShapeDtypeStruct((m, n), BF),
        scratch_shapes=[pltpu.VMEM((tm, tn), F32)],
        compiler_params=_params(2), name=name)(a, b)


def conv_silu_fwd(proj, cw, cb):
    def body(x_ref, w_ref, b_ref, o_ref, xpad):
        _fill_padded(xpad, x_ref)

        def chunk(r0):
            pre = _conv(_back(xpad, r0), w_ref[...], b_ref[...])
            o_ref[pl.ds(r0, Q), :] = pre * _sigmoid(pre)

        _chunks(chunk)

    c0 = PXBC // 128
    return pl.pallas_call(
        body, grid=(XBC // 128,),
        in_specs=[_spec((T, 128), lambda c: (0, c0 + c)), _spec((4, 128), lambda c: (0, c)), _spec((1, 128), lambda c: (0, c))],
        out_specs=_spec((T, 128), lambda c: (0, c)),
        out_shape=jax.ShapeDtypeStruct((T, XBC), F32), scratch_shapes=[pltpu.VMEM((T + 2 * HALO, 128), F32)],
        compiler_params=_params(), name="conv_silu_fwd")(proj, cw, cb)


def conv_silu_bwd(dx, d_b, d_c, proj, cw, cb):
    def body(dx_ref, db_ref, dc_ref, x_ref, w_ref, b_ref, o_ref, dw_ref, dbias_ref, xpad, dpad):
        tile = pl.program_id(0)
        _fill_padded(xpad, x_ref)
        dpad[0:HALO, :] = jnp.zeros((HALO, 128), F32)
        dpad[T + HALO:T + 2 * HALO, :] = jnp.zeros((HALO, 128), F32)
        dw_ref[...] = jnp.zeros_like(dw_ref)
        dbias_ref[...] = jnp.zeros_like(dbias_ref)

        def first(r0):
            back = _back(xpad, r0)
            pre = _conv(back, w_ref[...], b_ref[...])
            sg = _sigmoid(pre)
            rows = pl.ds(r0, Q)
            d = jnp.where(tile < 8, dx_ref[rows, :], jnp.where(tile < 10, db_ref[rows, :], dc_ref[rows, :]))
            dpre = d * (sg * (1.0 + pre * (1.0 - sg)))
            dpad[pl.ds(r0 + HALO, Q), :] = dpre
            dw, dbias = _conv_bwd_w(dpre, back)
            dw_ref[...] += dw
            dbias_ref[...] += dbias

        _chunks(first)

        def second(r0):
            o_ref[pl.ds(r0, Q), :] = _conv_bwd_x(_ahead(dpad, r0), w_ref[...]).astype(BF)

        _chunks(second)

    c0 = PXBC // 128
    pad = pltpu.VMEM((T + 2 * HALO, 128), F32)
    return pl.pallas_call(
        body, grid=(XBC // 128,),
        in_specs=[_spec((T, 128), lambda c: (0, jnp.minimum(c, 7))), _spec((T, 128), lambda c: (0, jnp.clip(c - 8, 0, 1))),
                  _spec((T, 128), lambda c: (0, jnp.clip(c - 10, 0, 1))), _spec((T, 128), lambda c: (0, c0 + c)),
                  _spec((4, 128), lambda c: (0, c)), _spec((1, 128), lambda c: (0, c))],
        out_specs=[_spec((T, 128), lambda c: (0, c)), _spec((4, 128), lambda c: (0, c)), _spec((1, 128), lambda c: (0, c))],
        out_shape=[jax.ShapeDtypeStruct((T, XBC), BF), jax.ShapeDtypeStruct((4, XBC), F32), jax.ShapeDtypeStruct((1, XBC), F32)],
        scratch_shapes=[pad, pad], compiler_params=_params(), name="conv_silu_bwd")(dx, d_b, d_c, proj, cw, cb)


def _ssd_chunk_common(row0, dt_ref, b_ref, c_ref, bias, a_neg):
    shape = (Q, Q)
    lane = _lanes(shape)
    sub = _rows(shape)
    live = (_rows(shape, row0) >= NPAD) & (lane < 8)
    dtr = dt_ref[:, :]
    dt = jnp.where(live, _softplus(dtr + bias), 0.0)
    d_a = dt * a_neg
    tri = (sub >= lane).astype(F32)
    cs = jnp.dot(tri, d_a, precision=lax.Precision.HIGHEST, preferred_element_type=F32)
    cs_t = cs.T
    b_f = b_ref[:, :]
    bc = b_f.astype(BF)
    cc = c_ref[:, :].astype(BF)
    cb = lax.dot_general(cc, bc, NT_DIMS, preferred_element_type=F32)
    cs_last = cs[Q - 1:Q, :]
    return dict(lane=lane, sub=sub, live=live, dtr=dtr, dt=dt, cs=cs, cs_t=cs_t, bc=bc, cc=cc, cb=cb, bc_t=b_f.T.astype(BF),
                ecs=jnp.exp(cs), dsm=jnp.exp(cs_last - cs), gam=jnp.exp(cs_last))


def _pair(lane_even, mat, j):
    return jnp.where(lane_even, mat[:, j:j + 1], mat[:, j + 1:j + 2])


def _pair_row(lane_even, mat, j):
    return jnp.where(lane_even[0:1, :], mat[:, j:j + 1], mat[:, j + 1:j + 2])


def _head_decay(cm, j):
    seg = cm["cs"][:, j:j + 1] - cm["cs_t"][j:j + 1, :]
    return jnp.exp(jnp.where(cm["sub"] >= cm["lane"], seg, -jnp.inf))


def _head_decay_t(cm, j):
    seg = cm["cs_t"][j:j + 1, :] - cm["cs"][:, j:j + 1]
    return jnp.exp(jnp.where(cm["lane"] >= cm["sub"], seg, -jnp.inf))


def ssd_fwd(xbc_act, proj, dt_bias2, a_log2, d2, norm_w):
    def body(x_all, b_all, c_all, dt_all, z_all, bias_all, alog_all, d_all, nw_all, yn_all, y_all, hp_all, h_all):
        @pl.when(pl.program_id(0) == 0)
        def _():
            h_all[...] = jnp.zeros_like(h_all)

        for g in range(2):
            wide, thin = slice(512 * g, 512 * g + 512), slice(128 * g, 128 * g + 128)
            group(x_all.at[:, wide], b_all.at[:, thin], c_all.at[:, thin], dt_all.at[:, thin], z_all.at[:, wide], bias_all.at[g],
                  alog_all.at[g], d_all.at[g], nw_all.at[:, wide], yn_all.at[:, wide], y_all.at[:, wide], hp_all.at[g, 0], h_all.at[g])

    def group(x_ref, b_ref, c_ref, dt_ref, z_ref, bias_ref, alog_ref, d_ref, nw_ref, yn_ref, y_ref, hp_ref, h_scr):
        bias = bias_ref[...]
        a_neg = -jnp.exp(alog_ref[...])
        dsk = d_ref[...]
        cm = _ssd_chunk_common(pl.program_id(0) * Q, dt_ref, b_ref, c_ref, bias, a_neg)
        lane_even = cm["lane"] < 64
        for p in range(4):
            je, jo = 2 * p, 2 * p + 1
            xp = x_ref[:, 128 * p:128 * p + 128]
            xdt = xp * _pair(lane_even, cm["dt"], je)
            xdt_b = xdt.astype(BF)
            m_e = (cm["cb"] * _head_decay(cm, je)).astype(BF)
            m_o = (cm["cb"] * _head_decay(cm, jo)).astype(BF)
            zero = jnp.zeros_like(xdt_b)
            yd = (jnp.dot(m_e, jnp.where(lane_even, xdt_b, zero), preferred_element_type=F32)
                  + jnp.dot(m_o, jnp.where(lane_even, zero, xdt_b), preferred_element_type=F32))
            hp = h_scr[p]
            hp_ref[p] = hp
            yo = jnp.dot(cm["cc"], hp.astype(BF), preferred_element_type=F32) * _pair(lane_even, cm["ecs"], je)
            y_ref[:, 128 * p:128 * p + 128] = yd + yo + xp * _pair_row(lane_even, dsk, je)
            st = jnp.dot(cm["bc_t"], (xdt * _pair(lane_even, cm["dsm"], je)).astype(BF), preferred_element_type=F32)
            h_scr[p] = hp * _pair_row(lane_even, cm["gam"], je) + st
        zc = z_ref[:, :]
        gated = y_ref[:, :] * (zc * _sigmoid(zc))
        yn_ref[:, :] = _rms(gated, nw_ref[...]).astype(BF)

    par = _spec((2, 1, 128), lambda c: (0, 0, 0))
    wide = _spec((Q, SSD_W), lambda c: (c, 0))
    return pl.pallas_call(
        body, grid=(NCH,),
        in_specs=[wide, _spec((Q, 256), lambda c: (c, 4)), _spec((Q, 256), lambda c: (c, 5)), _spec((Q, 256), lambda c: (c, PDT // 256)),
                  wide, par, par, par, _spec((1, SSD_W), lambda c: (0, 0))],
        out_specs=[wide, wide, _spec((2, 1, 4, 128, 128), lambda c: (0, c, 0, 0, 0))],
        out_shape=[jax.ShapeDtypeStruct((T, SSD_W), BF), jax.ShapeDtypeStruct((T, SSD_W), F32),
                   jax.ShapeDtypeStruct((2, NCH, 4, 128, 128), F32)],
        scratch_shapes=[pltpu.VMEM((2, 4, 128, 128), F32)],
        compiler_params=_params(), name="ssd_fwd")(xbc_act, xbc_act, xbc_act, proj, proj, dt_bias2, a_log2, d2, norm_w)


def ssd_bwd(dyn, xbc_act, proj, y_pre, h_prev, dt_bias2, a_log2, d2, norm_w):
    def body(dyn_all, x_all, b_all, c_all, dt_all, z_all, y_all, hp_all, bias_all, alog_all, d_all, nw_all,
             dz_all, dx_all, db_all, dc_all, ddt_all, dpar_all, dnw_all, dh_all, acc_all):
        @pl.when(pl.program_id(0) == 0)
        def _():
            dh_all[...] = jnp.zeros_like(dh_all)
            acc_all[...] = jnp.zeros_like(acc_all)
            dnw_all[...] = jnp.zeros_like(dnw_all)

        for g in range(2):
            wide, thin = slice(512 * g, 512 * g + 512), slice(128 * g, 128 * g + 128)
            group(dyn_all.at[:, wide], x_all.at[:, wide], b_all.at[:, thin], c_all.at[:, thin], dt_all.at[:, thin], z_all.at[:, wide],
                  y_all.at[:, wide], hp_all.at[g, 0], bias_all.at[g], alog_all.at[g], d_all.at[g], nw_all.at[:, wide],
                  dz_all.at[:, wide], dx_all.at[:, wide], db_all.at[:, thin], dc_all.at[:, thin], ddt_all.at[:, thin], dpar_all.at[g],
                  dnw_all.at[:, wide], dh_all.at[g], acc_all.at[g])

    def group(dyn_ref, x_ref, b_ref, c_ref, dt_ref, z_ref, y_ref, hp_ref, bias_ref, alog_ref, d_ref, nw_ref,
              dz_ref, dx_ref, db_ref, dc_ref, ddt_ref, dpar_ref, dnw_ref, dh_scr, acc_scr):
        ci = pl.program_id(0)
        bias = bias_ref[...]
        a_neg = -jnp.exp(alog_ref[...])
        dsk = d_ref[...]
        cm = _ssd_chunk_common((NCH - 1 - ci) * Q, dt_ref, b_ref, c_ref, bias, a_neg)
        lane, sub = cm["lane"], cm["sub"]
        lane_even = lane < 64
        cc_t = c_ref[:, :].T.astype(BF)
        cb_t = lax.dot_general(cm["bc"], cm["cc"], NT_DIMS, preferred_element_type=F32)
        zc = z_ref[:, :]
        yc = y_ref[:, :]
        sg = _sigmoid(zc)
        sz = zc * sg
        dgated, dnw = _rms_bwd(dyn_ref[:, :], yc * sz, nw_ref[...])
        dnw_ref[...] += jnp.sum(dnw, axis=0, keepdims=True)
        dz_ref[:, :] = (dgated * yc * (sg * (1.0 + zc * (1.0 - sg)))).astype(BF)
        dy_all = dgated * sz
        dcb = jnp.zeros((Q, Q), F32)
        dcb_t = jnp.zeros((Q, Q), F32)
        db_acc = jnp.zeros((Q, Q), F32)
        dc_acc = jnp.zeros((Q, Q), F32)
        dcs = jnp.zeros((Q, Q), F32)
        ddt = jnp.zeros((Q, Q), F32)
        for p in range(4):
            je, jo = 2 * p, 2 * p + 1
            xp = x_ref[:, 128 * p:128 * p + 128]
            dy = dy_all[:, 128 * p:128 * p + 128]
            dt_p = _pair(lane_even, cm["dt"], je)
            xdt = xp * dt_p
            xdt_b = xdt.astype(BF)
            dy_b = dy.astype(BF)
            zero = jnp.zeros_like(dy_b)
            hp = hp_ref[p]
            hp_b = hp.astype(BF)
            dh = dh_scr[p]
            dh_b = dh.astype(BF)
            acc_scr[p:p + 1, :] += jnp.sum(dy * xp, axis=0, keepdims=True)
            dxp = dy * _pair_row(lane_even, dsk, je)
            e_p = _pair(lane_even, cm["ecs"], je)
            g_p = jnp.dot(cm["cc"], hp_b, preferred_element_type=F32)
            dg_b = (dy * e_p).astype(BF)
            de = dy * g_p * e_p
            dc_acc = dc_acc + lax.dot_general(dg_b, hp_b, NT_DIMS, preferred_element_type=F32)
            dh_in = jnp.dot(cc_t, dg_b, preferred_element_type=F32)
            ds_p = _pair(lane_even, cm["dsm"], je)
            r_p = jnp.dot(cm["bc"], dh_b, preferred_element_type=F32)
            dxdt = r_p * ds_p
            tt = r_p * xdt * ds_p
            db_acc = db_acc + lax.dot_general((xdt * ds_p).astype(BF), dh_b, NT_DIMS, preferred_element_type=F32)
            dgam_m = jnp.sum(dh * hp, axis=0, keepdims=True)
            for j, even in ((je, True), (jo, False)):
                sel = lane_even if even else jnp.logical_not(lane_even)
                dy_j = jnp.where(sel, dy_b, zero)
                l_j = _head_decay(cm, j)
                l_jt = _head_decay_t(cm, j)
                m_j = cm["cb"] * l_j
                m_jt = cb_t * l_jt
                dm = lax.dot_general(dy_j, xdt_b, NT_DIMS, preferred_element_type=F32)
                dm_t = lax.dot_general(xdt_b, dy_j, NT_DIMS, preferred_element_type=F32)
                dxdt = dxdt + jnp.dot(m_jt.astype(BF), dy_j, preferred_element_type=F32)
                dcb = dcb + dm * l_j
                dcb_t = dcb_t + dm_t * l_jt
                t_j = jnp.where(sel, tt, 0.0)
                col = jnp.sum(dm * m_j - dm_t * m_jt + (jnp.where(sel, de, 0.0) - t_j), axis=1, keepdims=True)
                gam_j = cm["gam"][:, j:j + 1]
                last = (jnp.sum(jnp.sum(t_j, axis=0, keepdims=True), axis=1, keepdims=True)
                        + jnp.sum(jnp.where(sel[0:1, :], dgam_m, 0.0), axis=1, keepdims=True) * gam_j)
                col = col + jnp.where(sub[:, 0:1] == Q - 1, last, 0.0)
                dcs = dcs + jnp.where(lane == j, col, 0.0)
            dh_scr[p] = dh_in + dh * _pair_row(lane_even, cm["gam"], je)
            dx_ref[:, 128 * p:128 * p + 128] = dxp + dxdt * dt_p
            dd = dxdt * xp
            ddt = ddt + jnp.where(lane == je, jnp.sum(jnp.where(lane_even, dd, 0.0), axis=1, keepdims=True), 0.0)
            ddt = ddt + jnp.where(lane == jo, jnp.sum(jnp.where(lane_even, 0.0, dd), axis=1, keepdims=True), 0.0)
        dc_ref[:, :] = dc_acc + jnp.dot(dcb.astype(BF), cm["bc"], preferred_element_type=F32)
        db_ref[:, :] = db_acc + jnp.dot(dcb_t.astype(BF), cm["cc"], preferred_element_type=F32)
        tri_t = (sub <= lane).astype(F32)
        dd_a = jnp.dot(tri_t, dcs, precision=lax.Precision.HIGHEST, preferred_element_type=F32)
        ddt = ddt + dd_a * a_neg
        acc_scr[5:6, :] += jnp.sum(dd_a * cm["dt"], axis=0, keepdims=True)
        draw = jnp.where(cm["live"], ddt * _sigmoid_gate(cm["dtr"] + bias), 0.0)
        acc_scr[4:5, :] += jnp.sum(draw, axis=0, keepdims=True)
        ddt_ref[:, :] = draw.astype(BF)

        @pl.when(ci == NCH - 1)
        def _():
            lane1 = _lanes((1, 128))
            dd = jnp.zeros((1, 128), F32)
            for p in range(4):
                row = acc_scr[p:p + 1, :]
                dd = dd + jnp.where(lane1 == 2 * p, jnp.sum(jnp.where(lane1 < 64, row, 0.0), axis=1, keepdims=True), 0.0)
                dd = dd + jnp.where(lane1 == 2 * p + 1, jnp.sum(jnp.where(lane1 < 64, 0.0, row), axis=1, keepdims=True), 0.0)
            dpar_ref[...] = jnp.concatenate([acc_scr[4:5, :], acc_scr[5:6, :] * a_neg, dd, jnp.zeros((5, 128), F32)], axis=0)

    par = _spec((2, 1, 128), lambda c: (0, 0, 0))
    wide = _spec((Q, SSD_W), lambda c: (NCH - 1 - c, 0))
    thin = _spec((Q, 256), lambda c: (NCH - 1 - c, 0))
    vec = _spec((1, SSD_W), lambda c: (0, 0))
    return pl.pallas_call(
        body, grid=(NCH,),
        in_specs=[wide, wide, _spec((Q, 256), lambda c: (NCH - 1 - c, 4)), _spec((Q, 256), lambda c: (NCH - 1 - c, 5)),
                  _spec((Q, 256), lambda c: (NCH - 1 - c, PDT // 256)), wide, wide,
                  _spec((2, 1, 4, 128, 128), lambda c: (0, NCH - 1 - c, 0, 0, 0)), par, par, par, vec],
        out_specs=[wide, wide, thin, thin, thin, _spec((2, 8, 128), lambda c: (0, 0, 0)), vec],
        out_shape=[jax.ShapeDtypeStruct((T, SSD_W), BF), jax.ShapeDtypeStruct((T, SSD_W), F32), jax.ShapeDtypeStruct((T, 256), F32),
                   jax.ShapeDtypeStruct((T, 256), F32), jax.ShapeDtypeStruct((T, 256), BF), jax.ShapeDtypeStruct((2, 8, 128), F32),
                   jax.ShapeDtypeStruct((1, SSD_W), F32)],
        scratch_shapes=[pltpu.VMEM((2, 4, 128, 128), F32), pltpu.VMEM((2, 8, 128), F32)],
        compiler_params=_params(), name="ssd_bwd")(dyn, xbc_act, xbc_act, xbc_act, proj, proj, y_pre, h_prev, dt_bias2, a_log2, d2, norm_w)


def _lru_gates(back, cw, cb, wa, ba, wx, bx, lam):
    xr = _conv(back, cw, cb)
    xr_b = xr.astype(BF)
    r = _sigmoid_gate(jnp.dot(xr_b, wa, preferred_element_type=F32) + ba)
    i = _sigmoid_gate(jnp.dot(xr_b, wx, preferred_element_type=F32) + bx)
    sp = _softplus(-lam)
    la = (-LRU_C) * r * sp
    a = jnp.exp(la)
    mult2 = -jnp.tanh(la) * (a * a + 1.0)
    return xr, xr_b, r, i, sp, a, jnp.sqrt(mult2), mult2


def lru_gates_fwd(proj, cw, cb, wa2, ba, wx2, bx, lam):
    def body(x_ref, cw_ref, cb_ref, wa_ref, ba_ref, wx_ref, bx_ref, lam_ref, a_ref, u_ref, xpad):
        _fill_padded(xpad, x_ref)

        def chunk(r0):
            xr, _, _, i, _, a, mult, _ = _lru_gates(_back(xpad, r0), cw_ref[...], cb_ref[...], wa_ref[0], ba_ref[...], wx_ref[0], bx_ref[...],
                                                 lam_ref[...])
            a_ref[pl.ds(r0, Q), :] = a
            u_ref[pl.ds(r0, Q), :] = jnp.where(_rows(a.shape, r0) >= NPAD, mult * (i * xr), 0.0)

        _chunks(chunk, unrolled=True)

    c0 = PXL // 128
    vec = _spec((1, 128), lambda c: (0, c))
    mat = _spec((1, 128, 128), lambda c: (c, 0, 0))
    return pl.pallas_call(
        body, grid=(8,),
        in_specs=[_spec((T, 128), lambda c: (0, c0 + c)), _spec((4, 128), lambda c: (0, c)), vec, mat, vec, mat, vec, vec],
        out_specs=[_spec((T, 128), lambda c: (0, c)), _spec((T, 128), lambda c: (0, c))],
        out_shape=[jax.ShapeDtypeStruct((T, LRU_W), F32), jax.ShapeDtypeStruct((T, LRU_W), F32)],
        scratch_shapes=[pltpu.VMEM((T + 2 * HALO, 128), F32)],
        compiler_params=_params(), name="lru_gates_fwd")(proj, cw, cb, wa2, ba, wx2, bx, lam)


def lru_scan_fwd(a, u):
    def body(a_ref, u_ref, h_ref):
        def step(i, h):
            base = pl.multiple_of(i * 8, 8)
            for k in range(8):
                h = a_ref[pl.ds(base + k, 1), :] * h + u_ref[pl.ds(base + k, 1), :]
                h_ref[pl.ds(base + k, 1), :] = h
            return h

        lax.fori_loop(0, T // 8, step, jnp.zeros((1, LRU_W), F32))

    return pl.pallas_call(body, out_shape=jax.ShapeDtypeStruct((T, LRU_W), F32), compiler_params=_params(0), name="lru_scan_fwd")(a, u)


def lru_scan_bwd(a, dh_out):
    def body(a_ref, d_ref, o_ref):
        def step(i, carry):
            base = pl.multiple_of(T - 8 - i * 8, 8)
            for k in range(7, -1, -1):
                carry = d_ref[pl.ds(base + k, 1), :] + carry
                o_ref[pl.ds(base + k, 1), :] = carry
                carry = carry * a_ref[pl.ds(base + k, 1), :]
            return carry

        lax.fori_loop(0, T // 8, step, jnp.zeros((1, LRU_W), F32))

    return pl.pallas_call(body, out_shape=jax.ShapeDtypeStruct((T, LRU_W), F32), compiler_params=_params(0), name="lru_scan_bwd")(a, dh_out)


def lru_gates_bwd(dhs, hseq, proj, cw, cb, wa2, ba, wx2, bx, lam):
    def body(dh_ref, h_ref, x_ref, cw_ref, cb_ref, wa_ref, ba_ref, wx_ref, bx_ref, lam_ref,
             dx_ref, dcw_ref, dcb_ref, dwa_ref, dba_ref, dwx_ref, dbx_ref, dlam_ref, xpad, hpad, dpad):
        _fill_padded(xpad, x_ref)
        _fill_padded(hpad, h_ref)
        dpad[0:HALO, :] = jnp.zeros((HALO, 128), F32)
        dpad[T + HALO:T + 2 * HALO, :] = jnp.zeros((HALO, 128), F32)
        for ref in (dcw_ref, dcb_ref, dwa_ref, dba_ref, dwx_ref, dbx_ref, dlam_ref):
            ref[...] = jnp.zeros_like(ref)
        lam = lam_ref[...]

        def first(r0):
            back = _back(xpad, r0)
            xr, xr_b, r, i, sp, a, mult, mult2 = _lru_gates(back, cw_ref[...], cb_ref[...], wa_ref[0], ba_ref[...], wx_ref[0], bx_ref[...], lam)
            dh = dh_ref[pl.ds(r0, Q), :]
            da = dh * _back(hpad, r0)(1)
            du = jnp.where(_rows(dh.shape, r0) >= NPAD, dh, 0.0)
            dmult = du * (i * xr)
            di = du * (mult * xr)
            dxr = du * (mult * i)
            dla = da * a - dmult * (a * a) * lax.rsqrt(mult2)
            dr = dla * ((-LRU_C) * sp)
            dlam_ref[...] += jnp.sum(dla * ((-LRU_C) * r), axis=0, keepdims=True)
            dpr = dr * r * (1.0 - r)
            dpi = di * i * (1.0 - i)
            dba_ref[...] += jnp.sum(dpr, axis=0, keepdims=True)
            dbx_ref[...] += jnp.sum(dpi, axis=0, keepdims=True)
            dpr_b = dpr.astype(BF)
            dpi_b = dpi.astype(BF)
            dxr = (dxr + lax.dot_general(dpr_b, wa_ref[0], NT_DIMS, preferred_element_type=F32)
                   + lax.dot_general(dpi_b, wx_ref[0], NT_DIMS, preferred_element_type=F32))
            dwa_ref[0] += lax.dot_general(xr_b, dpr_b, TN_DIMS, preferred_element_type=F32)
            dwx_ref[0] += lax.dot_general(xr_b, dpi_b, TN_DIMS, preferred_element_type=F32)
            dpad[pl.ds(r0 + HALO, Q), :] = dxr
            dcw, dcb = _conv_bwd_w(dxr, back)
            dcw_ref[...] += dcw
            dcb_ref[...] += dcb

        _chunks(first, unrolled=True)
        dlam_ref[...] = -dlam_ref[...] * _sigmoid_gate(-lam)

        def second(r0):
            dx_ref[pl.ds(r0, Q), :] = _conv_bwd_x(_ahead(dpad, r0), cw_ref[...]).astype(BF)

        _chunks(second)

    c0 = PXL // 128
    vec = _spec((1, 128), lambda c: (0, c))
    mat = _spec((1, 128, 128), lambda c: (c, 0, 0))
    col = _spec((T, 128), lambda c: (0, c))
    vshape = jax.ShapeDtypeStruct((1, LRU_W), F32)
    mshape = jax.ShapeDtypeStruct((8, 128, 128), F32)
    pad = pltpu.VMEM((T + 2 * HALO, 128), F32)
    return pl.pallas_call(
        body, grid=(8,),
        in_specs=[col, col, _spec((T, 128), lambda c: (0, c0 + c)), _spec((4, 128), lambda c: (0, c)), vec, mat, vec, mat, vec, vec],
        out_specs=[col, _spec((4, 128), lambda c: (0, c)), vec, mat, vec, mat, vec, vec],
        out_shape=[jax.ShapeDtypeStruct((T, LRU_W), BF), jax.ShapeDtypeStruct((4, LRU_W), F32), vshape, mshape, vshape, mshape, vshape, vshape],
        scratch_shapes=[pad, pad, pad], compiler_params=_params(), name="lru_gates_bwd")(dhs, hseq, proj, cw, cb, wa2, ba, wx2, bx, lam)


def gate_up(h1, wn, w_gate, w_up):
    def body(h_ref, wn_ref, wg_ref, wu_ref, gt_ref, up_ref, act_ref, u_ref):
        for r in (0, HALF):
            u_ref[r:r + HALF, :] = _rms(h_ref[r:r + HALF, :], wn_ref[...]).astype(BF)

        def tile(c0):
            cols = pl.ds(c0, 256)
            gt = lax.dot_general(u_ref[...], wg_ref[cols, :], NT_DIMS, preferred_element_type=F32)
            up = lax.dot_general(u_ref[...], wu_ref[cols, :], NT_DIMS, preferred_element_type=F32)
            gt_ref[:, cols] = gt.astype(BF)
            up_ref[:, cols] = up.astype(BF)
            act_ref[:, cols] = (gt * _sigmoid(gt) * up).astype(BF)

        _col_tiles(D_FF, 256, tile)

    big = jax.ShapeDtypeStruct((T, D_FF), BF)
    return pl.pallas_call(
        body, grid=(T // RC,), in_specs=[_rows_spec(D), _vec(D), _whole((D_FF, D)), _whole((D_FF, D))],
        out_specs=[_rows_spec(D_FF), _rows_spec(D_FF), _rows_spec(D_FF), _rows_spec(D)],
        out_shape=[big, big, big, jax.ShapeDtypeStruct((T, D), BF)],
        compiler_params=_params(), name="gate_up")(h1, wn, w_gate, w_up)


def down_loss(act, w_down, h1, target, wf):
    first = NPAD + N_META

    def body(a_ref, w_ref, r_ref, t_hbm, wf_ref, d_ref, db_ref, l_ref, dw_ref, h_scr, t_ref, t_sem):
        i = pl.program_id(0)
        _zero_at_first(l_ref, dw_ref)
        head = pltpu.make_async_copy(t_hbm.at[pl.ds(0, RC - first)], t_ref.at[pl.ds(first, RC - first)], t_sem)
        rest = pltpu.make_async_copy(t_hbm.at[pl.ds(pl.multiple_of(jnp.maximum(i * RC - first, 0), 32), RC)], t_ref, t_sem)

        @pl.when(i == 0)
        def _():
            t_ref[0:first, :] = jnp.zeros((first, D), F32)
            head.start()

        @pl.when(i > 0)
        def _():
            rest.start()

        def tile(c0):
            cols = pl.ds(c0, 512)
            h_scr[:, cols] = r_ref[:, cols] + jnp.dot(a_ref[...], w_ref[:, cols], preferred_element_type=F32)

        _col_tiles(D, 512, tile)

        @pl.when(i == 0)
        def _():
            head.wait()

        @pl.when(i > 0)
        def _():
            rest.wait()

        for r in (0, HALF):
            h = h_scr[r:r + HALF, :]
            live = _rows((HALF, D), i * RC + r) >= first
            err = jnp.where(live, _rms(h, wf_ref[...]) - t_ref[r:r + HALF, :], 0.0)
            l_ref[...] += 0.5 * jnp.sum(jnp.sum(err * err, axis=1, keepdims=True) * (1.0 / D), axis=0, keepdims=True)
            dh, dw = _rms_bwd(err * (1.0 / D), h, wf_ref[...])
            dw_ref[...] += jnp.sum(dw, axis=0, keepdims=True)
            d_ref[r:r + HALF, :] = dh
            db_ref[r:r + HALF, :] = dh.astype(BF)

    return pl.pallas_call(
        body, grid=(T // RC,),
        in_specs=[_rows_spec(D_FF), _whole((D_FF, D)), _rows_spec(D), pl.BlockSpec(memory_space=pl.ANY), _vec(D)],
        out_specs=[_rows_spec(D), _rows_spec(D), _spec((1, 128), lambda i: (0, 0)), _vec(D)],
        out_shape=[jax.ShapeDtypeStruct((T, D), F32), jax.ShapeDtypeStruct((T, D), BF), jax.ShapeDtypeStruct((1, 128), F32),
                   jax.ShapeDtypeStruct((1, D), F32)],
        scratch_shapes=[pltpu.VMEM((RC, D), F32), pltpu.VMEM((RC, D), F32), pltpu.SemaphoreType.DMA],
        compiler_params=_params(), name="down_loss")(act, w_down, h1, target, wf)


def swiglu_bwd(dh2_b, w_down, gt, up):
    def body(d_ref, w_ref, gt_ref, up_ref, dg_ref, du_ref):
        def tile(c0):
            cols = pl.ds(c0, 256)
            dact = lax.dot_general(d_ref[...], w_ref[cols, :], NT_DIMS, preferred_element_type=F32)
            gt_ = gt_ref[:, cols].astype(F32)
            up_ = up_ref[:, cols].astype(F32)
            sg = _sigmoid(gt_)
            dg_ref[:, cols] = (dact * up_ * (sg * (1.0 + gt_ * (1.0 - sg)))).astype(BF)
            du_ref[:, cols] = (dact * (gt_ * sg)).astype(BF)

        _col_tiles(D_FF, 256, tile)

    big = jax.ShapeDtypeStruct((T, D_FF), BF)
    return pl.pallas_call(
        body, grid=(T // RC,), in_specs=[_rows_spec(D), _whole((D_FF, D)), _rows_spec(D_FF), _rows_spec(D_FF)],
        out_specs=[_rows_spec(D_FF), _rows_spec(D_FF)], out_shape=[big, big], compiler_params=_params(), name="swiglu_bwd")(dh2_b, w_down, gt, up)


def gate_up_bwd(dgt, dup, w_gate, w_up, h1, wn, dh2):
    def body(dg_ref, du_ref, wg_ref, wu_ref, h_ref, wn_ref, r_ref, d_ref, db_ref, dw_ref, du_scr):
        _zero_at_first(dw_ref)

        du_scr[...] = jnp.zeros_like(du_scr)

        def tile(c0):
            k = pl.ds(c0, 256)
            du_scr[...] += (jnp.dot(dg_ref[:, k], wg_ref[k, :], preferred_element_type=F32)
                            + jnp.dot(du_ref[:, k], wu_ref[k, :], preferred_element_type=F32))

        _col_tiles(D_FF, 256, tile)
        for r in (0, HALF):
            dh, dw = _rms_bwd(du_scr[r:r + HALF, :], h_ref[r:r + HALF, :], wn_ref[...])
            dw_ref[...] += jnp.sum(dw, axis=0, keepdims=True)
            dh = dh + r_ref[r:r + HALF, :]
            d_ref[r:r + HALF, :] = dh
            db_ref[r:r + HALF, :] = dh.astype(BF)

    return pl.pallas_call(
        body, grid=(T // RC,),
        in_specs=[_rows_spec(D_FF), _rows_spec(D_FF), _whole((D_FF, D)), _whole((D_FF, D)), _rows_spec(D), _vec(D), _rows_spec(D)],
        out_specs=[_rows_spec(D), _rows_spec(D), _vec(D)],
        out_shape=[jax.ShapeDtypeStruct((T, D), F32), jax.ShapeDtypeStruct((T, D), BF), jax.ShapeDtypeStruct((1, D), F32)],
        scratch_shapes=[pltpu.VMEM((RC, D), F32)],
        compiler_params=_params(), name="gate_up_bwd")(dgt, dup, w_gate, w_up, h1, wn, dh2)


def _adamw(w, g, m, v):
    m = ADAM_B1 * m + (1.0 - ADAM_B1) * g
    v = ADAM_B2 * v + (1.0 - ADAM_B2) * (g * g)
    m_hat = m / (1.0 - ADAM_B1 ** ADAM_STEP)
    v_hat = v / (1.0 - ADAM_B2 ** ADAM_STEP)
    delta = -ADAM_LR * (m_hat / (jnp.sqrt(v_hat) + ADAM_EPS) + ADAM_WD * w)
    return delta, m, v


def adamw_shard(name, recv, w, m, v, tr, tc):
    r, c = w.shape

    def body(p_ref, w_ref, m_ref, v_ref, g_ref, d_ref, mo_ref, vo_ref):
        g = p_ref[0].astype(F32)
        for s in range(1, 8):
            g = g + p_ref[s].astype(F32)
        g_ref[...] = g
        d_ref[...], mo_ref[...], vo_ref[...] = _adamw(w_ref[...], g, m_ref[...], v_ref[...])

    tile = _spec((tr, tc), lambda i, j: (i, j))
    shape = jax.ShapeDtypeStruct((r, c), F32)
    return pl.pallas_call(
        body, grid=(r // tr, c // tc), in_specs=[_spec((8, tr, tc), lambda i, j: (0, i, j)), tile, tile, tile],
        out_specs=[tile] * 4, out_shape=[shape] * 4, compiler_params=_params(2), name=name)(recv, w, m, v)


def adamw_w_in(recv, w, m, v):
    rows = w.shape[0] // 8

    def body(p_ref, w_ref, m_ref, v_ref, g_ref, d_ref, mo_ref, vo_ref):
        for q in range(8):
            cols = slice(128 * q, 128 * q + 128)
            g = p_ref[0, :, cols].astype(F32)
            for s in range(1, 8):
                g = g + p_ref[s, :, cols].astype(F32)
            part = pl.ds(q, rows, stride=8)
            g_ref[part, :] = g
            d_ref[part, :], mo_ref[part, :], vo_ref[part, :] = _adamw(w_ref[part, :], g, m_ref[part, :], v_ref[part, :])

    shape = jax.ShapeDtypeStruct(w.shape, F32)
    return pl.pallas_call(body, out_shape=[shape] * 4, compiler_params=_params(0), name="adamw_w_in")(recv, w, m, v)


def sum_slabs(recv):
    def body(p_ref, o_ref):
        g = p_ref[0]
        for s in range(1, 8):
            g = g + p_ref[s]
        o_ref[...] = g

    return pl.pallas_call(body, out_shape=jax.ShapeDtypeStruct(recv.shape[1:], F32), compiler_params=_params(0), name="sum_slabs")(recv)


SIMPLE = [("norm1_w", 1024), ("ssd_conv_b", 1536), ("ssd_dt_bias", 16), ("ssd_a_log", 16), ("ssd_d", 16), ("ssd_norm_w", 1024),
          ("lru_conv_b", 1024), ("lru_ba", 1024), ("lru_bx", 1024), ("lru_lambda", 1024), ("lru_norm_w", 1024), ("norm2_w", 1024),
          ("final_norm_w", 1024)]
SPECIAL = ["lru_wa", "lru_wx", "meta_tokens", "ssd_conv_w", "lru_conv_w"]
SM_ROWS = 176
SM_WA, SM_WX, SM_META, SM_SCW, SM_LCW, SM_LOSS = 14, 78, 142, 158, 166, 170


def _simple_rows():
    rows, r = {}, 0
    for name, n in SIMPLE:
        rows[name] = r
        r += -(-n // 1024)
    return rows


def adamw_small(sm, special_g, ws, ms, vs):
    rows = _simple_rows()
    ns, nx = len(SIMPLE), len(SPECIAL)

    def body(*refs):
        sm_ref = refs[0]
        gx = refs[1:1 + nx]
        wr = refs[1 + nx:1 + nx + ns + nx]
        mr = refs[1 + nx + ns + nx:1 + nx + 2 * (ns + nx)]
        vr = refs[1 + nx + 2 * (ns + nx):1 + nx + 3 * (ns + nx)]
        outs = refs[1 + nx + 3 * (ns + nx):]
        o = 0
        for k, (name, n) in enumerate(SIMPLE):
            r0 = rows[name]
            for c0 in range(0, n, 1024):
                wd = min(1024, n - c0)
                g = sm_ref[r0 + c0 // 1024:r0 + c0 // 1024 + 1, 0:wd]
                sl = (slice(None), slice(c0, c0 + wd))
                d, m2, v2 = _adamw(wr[k][sl], g, mr[k][sl], vr[k][sl])
                outs[o][sl] = g
                outs[o + 1][sl] = d
                outs[o + 2][sl] = m2
                outs[o + 3][sl] = v2
            o += 4
        for k in range(nx):
            d, m2, v2 = _adamw(wr[ns + k][...], gx[k][...], mr[ns + k][...], vr[ns + k][...])
            outs[o][...] = d
            outs[o + 1][...] = m2
            outs[o + 2][...] = v2
            o += 3

    out_shape = []
    for k in range(ns):
        out_shape += [jax.ShapeDtypeStruct(ws[k].shape, F32)] * 4
    for k in range(nx):
        out_shape += [jax.ShapeDtypeStruct(ws[ns + k].shape, F32)] * 3
    return pl.pallas_call(body, out_shape=out_shape, compiler_params=_params(0), name="adamw_small")(sm, *special_g, *ws, *ms, *vs)


def _place():
    return lax.axis_index("x"), lax.axis_index("y"), lax.axis_index("c")


def _index(px, py, pc):
    return 4 * px + 2 * py + pc


def all_gather(name, shards):
    n = len(shards)
    hbm = pl.BlockSpec(memory_space=pl.ANY)

    def body(*refs):
        ins, outs = refs[:n], refs[n:2 * n]
        send_sems, recv_sems, local_sems = refs[2 * n:]
        x, y, c = _place()
        me, sibling = (x, y, c), (x, y, 1 - c)
        chips = [(1 - x, y), (x, 1 - y), (1 - x, 1 - y)]

        def copy(i, k, block, to, src=None):
            dst = outs[i].at[_index(*block)]
            return pltpu.make_async_remote_copy(src_ref=dst if src is None else src, dst_ref=dst, send_sem=send_sems.at[7 * i + k],
                                                recv_sem=recv_sems.at[7 * i + k], device_id=to, device_id_type=MESH)

        mine = [pltpu.make_async_copy(ins[i], outs[i].at[_index(*me)], local_sems.at[i]) for i in range(n)]
        for cp in mine:
            cp.start()
        first = []
        for i in range(n):
            first += [copy(i, 1 + j, me, (*chip, c), src=ins[i]) for j, chip in enumerate(chips)]
            first.append(copy(i, 0, me, sibling, src=ins[i]))
        for cp in first:
            cp.start()
        passed = []
        for i in range(n):
            for j, chip in enumerate(chips):
                copy(i, 1 + j, (*chip, c), me).wait_recv()
                cp = copy(i, 4 + j, (*chip, c), sibling)
                cp.start()
                passed.append(cp)
        for i in range(n):
            copy(i, 0, sibling, me).wait_recv()
            for j, chip in enumerate(chips):
                copy(i, 4 + j, (*chip, 1 - c), me).wait_recv()
        for cp in first + passed:
            cp.wait_send()
        for cp in mine:
            cp.wait()

    return pl.pallas_call(
        body, in_specs=[hbm] * n, out_specs=[hbm] * n,
        out_shape=[jax.ShapeDtypeStruct((8,) + s.shape, s.dtype) for s in shards],
        scratch_shapes=[pltpu.SemaphoreType.DMA((7 * n,)), pltpu.SemaphoreType.DMA((7 * n,)), pltpu.SemaphoreType.DMA((n,))],
        name=name)(*shards)


HBM_SPEC = pl.BlockSpec(memory_space=pltpu.HBM)
SEM_SPEC = pl.BlockSpec(memory_space=pltpu.SEMAPHORE)
EFFECT = pltpu.SideEffectType.DATAFLOW_SIDE_EFFECTING


def _peers(x, y, c):
    return [((1 - x) if k & 4 else x, (1 - y) if k & 2 else y, (1 - c) if k & 1 else c) for k in range(1, 8)]


def _pieces(rows):
    for n in (4, 2):
        if rows % (16 * n) == 0:
            return [(r * (rows // n), rows // n) for r in range(n)]
    return [(0, rows)]


def _peer_copies(src, land, send_sems, recv_sems, k, peer, mine, slab_src):
    block = src.at[_index(*peer)] if slab_src else src
    return [pltpu.make_async_remote_copy(src_ref=block.at[pl.ds(r0, nr)], dst_ref=land.at[mine, pl.ds(r0, nr)], send_sem=send_sems.at[k],
                                         recv_sem=recv_sems.at[k], device_id=peer, device_id_type=MESH)
            for r0, nr in _pieces(block.shape[0])]


def copies_start(name, srcs, slab_src, after):
    n = len(srcs)
    zones = [jax.ShapeDtypeStruct(s.shape if slab_src else (8,) + s.shape, s.dtype) for s in srcs]
    afters = [] if after is None else [after]

    def body(*refs):
        ins, lands = refs[:n], refs[n:2 * n]
        first = 2 * n + len(afters)
        sends, recvs = refs[first:first + n], refs[first + n:first + 2 * n]
        token = refs[-1]
        x, y, c = _place()
        mine = _index(x, y, c)
        for i in range(n):
            per_peer = [_peer_copies(ins[i], lands[i], sends[i], recvs[i], k, peer, mine, slab_src) for k, peer in enumerate(_peers(x, y, c))]
            for piece in zip(*per_peer):
                for cp in piece:
                    cp.start()
        token[...] = jnp.zeros_like(token)

    sem = pltpu.SemaphoreType.DMA((7,))
    res = pl.pallas_call(
        body, name=name,
        out_shape=([sem] * (2 * n) + [pltpu.HBM(s.shape, s.dtype) for s in srcs] + [pltpu.HBM(z.shape, z.dtype) for z in zones]
                   + [jax.ShapeDtypeStruct((8, 128), F32)]),
        in_specs=[HBM_SPEC] * (2 * n) + [pl.BlockSpec(memory_space=pl.ANY)] * len(afters),
        out_specs=[SEM_SPEC] * (2 * n) + [HBM_SPEC] * (2 * n) + [pl.BlockSpec(memory_space=pltpu.VMEM)],
        input_output_aliases={i: 2 * n + i for i in range(2 * n)},
        compiler_params=pltpu.CompilerParams(has_side_effects=EFFECT),
    )(*[pltpu.with_memory_space_constraint(s, pltpu.HBM) for s in srcs],
      *[pltpu.with_memory_space_constraint(lax.empty(z.shape, z.dtype), pltpu.HBM) for z in zones], *afters)
    return [(res[i], res[n + i], res[2 * n + i], res[3 * n + i]) for i in range(n)], res[-1][0:1, 0:1]


def copies_wait(name, started, slab_src, after):
    n = len(started)

    def body(*refs):
        ins, lands = refs[:n], refs[n:2 * n]
        sends, recvs = refs[2 * n:3 * n], refs[3 * n:4 * n]
        x, y, c = _place()
        mine = _index(x, y, c)
        for i in range(n):
            for k, peer in enumerate(_peers(x, y, c)):
                arrival = pltpu.make_async_remote_copy(src_ref=ins[i].at[mine] if slab_src else ins[i], dst_ref=lands[i].at[_index(*peer)],
                                                       send_sem=sends[i].at[k], recv_sem=recvs[i].at[k], device_id=peer, device_id_type=MESH)
                arrival.wait_send()
                arrival.wait_recv()

    srcs = [s[2] for s in started]
    lands = [s[3] for s in started]
    afters = list(after) if isinstance(after, (list, tuple)) else [after]
    res = pl.pallas_call(
        body, name=name,
        out_shape=[pltpu.HBM(s.shape, s.dtype) for s in srcs] + [pltpu.HBM(z.shape, z.dtype) for z in lands],
        in_specs=[HBM_SPEC] * (2 * n) + [SEM_SPEC] * (2 * n) + [pl.BlockSpec(memory_space=pl.ANY)] * len(afters),
        out_specs=[HBM_SPEC] * (2 * n),
        input_output_aliases={i: i for i in range(2 * n)},
        compiler_params=pltpu.CompilerParams(has_side_effects=EFFECT),
    )(*srcs, *lands, *[s[0] for s in started], *[s[1] for s in started], *afters)
    me = _index(*_place())
    own = [lax.dynamic_index_in_dim(s, me, 0, keepdims=True) if slab_src else s[None] for s in res[:n]]
    return [lax.dynamic_update_slice_in_dim(z, o, me, 0) for z, o in zip(res[n:], own)]


WEIGHTS = ["meta_tokens", "norm1_w", "w_in", "ssd_conv_w", "ssd_conv_b", "ssd_dt_bias", "ssd_a_log", "ssd_d", "ssd_norm_w", "lru_conv_w",
           "lru_conv_b", "lru_wa", "lru_ba", "lru_wx", "lru_bx", "lru_lambda", "lru_norm_w", "w_out", "norm2_w", "w_gate", "w_up", "w_down",
           "final_norm_w"]
BIG = ["w_in", "w_out", "w_gate", "w_up", "w_down"]
COLUMN_SHARDED = ["w_in", "w_gate", "w_up"]
BIG_TILE = {"w_in": (578, 256), "w_out": (128, 1024), "w_gate": (176, 1024), "w_up": (176, 1024), "w_down": (176, 1024)}


def _pair_blocks(w):
    w = w.reshape(8, 2, 64, 64)
    z = jnp.zeros((8, 64, 64), w.dtype)
    return jnp.concatenate([jnp.concatenate([w[:, 0], z], axis=2), jnp.concatenate([z, w[:, 1]], axis=2)], axis=1)


def _unpair_blocks(w2):
    return jnp.stack([w2[:, :64, :64], w2[:, 64:, 64:]], axis=1).reshape(16, 64, 64)


def _per_group(v):
    return jnp.pad(v.reshape(2, 1, 8), ((0, 0), (0, 0), (0, 120)))


def _pad_cols(v, n):
    return jnp.pad(v, ((0, 0), (0, n - v.shape[1])))


def local_step(x, target, meta, ssd_cw, lru_cw, w_in, fetch, send, p):
    z120 = jnp.zeros((120, D), BF)
    w_dt = jnp.concatenate([w_in[2560:2568], z120, w_in[2568:2576], z120], axis=0)
    bias2, alog2, d2 = _per_group(p["ssd_dt_bias"]), _per_group(p["ssd_a_log"]), _per_group(p["ssd_d"])
    wa2 = _pair_blocks(p["lru_wa"]).astype(BF)
    wx2 = _pair_blocks(p["lru_wx"]).astype(BF)
    lru = (lru_cw, p["lru_conv_b"], wa2, p["lru_ba"], wx2, p["lru_bx"], p["lru_lambda"])

    h0 = jnp.concatenate([jnp.zeros((NPAD, D), F32), meta, x], axis=0)
    proj, u1 = in_proj(h0, p["norm1_w"], w_in, w_dt)
    xbc_act = conv_silu_fwd(proj, ssd_cw, p["ssd_conv_b"])
    yn_ssd, y_pre, h_prev = ssd_fwd(xbc_act, proj, bias2, alog2, d2, p["ssd_norm_w"])
    a, u = lru_gates_fwd(proj, *lru)
    hseq = lru_scan_fwd(a, u)
    (w_out,) = fetch(["w_out"], hseq)
    h1, cat = out_proj(yn_ssd, proj, hseq, p["lru_norm_w"], w_out, h0)
    w_gate, w_up = fetch(["w_gate", "w_up"], h1)
    gt, up, act, u2 = gate_up(h1, p["norm2_w"], w_gate, w_up)
    (w_down,) = fetch(["w_down"], act)
    dh2, dh2_b, loss, d_fnw = down_loss(act, w_down, h1, target, p["final_norm_w"])

    dgt, dup = swiglu_bwd(dh2_b, w_down, gt, up)
    g_down = matmul_tn("dw_down", act, dh2_b, 1408, 512)
    g_gate = matmul_tn("dw_gate", dgt, u2, 1408, 512)
    g_up = matmul_tn("dw_up", dup, u2, 1408, 512)
    sent = send({"w_down": g_down, "w_gate": g_gate, "w_up": g_up})
    dh1, dh1_b, d_n2 = gate_up_bwd(dgt, dup, w_gate, w_up, h1, p["norm2_w"] + sent, dh2)
    sent = send({"w_out": matmul_tn("dw_out", cat, dh1_b, 512, 1024)})
    dyn, dh_out, dg_b, d_lnw = out_proj_bwd(dh1_b, w_out, proj, hseq, p["lru_norm_w"] + sent)

    dhs = lru_scan_bwd(a, dh_out)
    dxl_b, d_lcw, d_lcb, dwa2, d_ba, dwx2, d_bx, d_lam = lru_gates_bwd(dhs, hseq, proj, *lru)
    dz_b, dx, d_b, d_c, ddt_b, dpar, d_snw = ssd_bwd(dyn, xbc_act, proj, y_pre, h_prev, bias2, alog2, d2, p["ssd_norm_w"])
    dxbc_b, d_scw, d_scb = conv_silu_bwd(dx, d_b, d_c, proj, ssd_cw, p["ssd_conv_b"])
    g_dt = matmul_tn("dw_in_dt", ddt_b, u1, 256, 512)
    g_in = jnp.concatenate([matmul_tn("dw_in_z", dz_b, u1, 512, 1024), matmul_tn("dw_in_xbc", dxbc_b, u1, 512, 1024), g_dt[0:8], g_dt[128:136],
                            matmul_tn("dw_in_g", dg_b, u1, 512, 1024), matmul_tn("dw_in_xl", dxl_b, u1, 512, 1024)], axis=0)
    sent = send({"w_in": g_in})
    grad_x, d_meta, d_n1 = in_proj_bwd(dz_b, dg_b, dxl_b, dxbc_b, ddt_b, w_in, w_dt, h0, p["norm1_w"] + sent, dh1)
    small = {"norm1_w": d_n1, "ssd_conv_b": d_scb, "ssd_dt_bias": dpar[:, 0, :8].reshape(1, 16), "ssd_a_log": dpar[:, 1, :8].reshape(1, 16),
             "ssd_d": dpar[:, 2, :8].reshape(1, 16), "ssd_norm_w": d_snw, "lru_conv_b": d_lcb, "lru_ba": d_ba, "lru_bx": d_bx,
             "lru_lambda": d_lam, "lru_norm_w": d_lnw, "norm2_w": d_n2, "final_norm_w": d_fnw,
             "lru_wa": _unpair_blocks(dwa2), "lru_wx": _unpair_blocks(dwx2), "meta_tokens": d_meta,
             "ssd_conv_w": d_scw, "lru_conv_w": d_lcw}
    return loss, grad_x, small


def _pack_small(small, loss):
    rows = [_pad_cols(small[name], -(-n // 1024) * 1024).reshape(-1, 1024) for name, n in SIMPLE]
    rows += [small["lru_wa"].reshape(64, 1024), small["lru_wx"].reshape(64, 1024), small["meta_tokens"],
             _pad_cols(small["ssd_conv_w"], 2048).reshape(8, 1024), small["lru_conv_w"], _pad_cols(loss[:, 0:1], 1024)]
    sm = jnp.concatenate(rows, axis=0)
    return jnp.pad(sm, ((0, SM_ROWS - sm.shape[0]), (0, 0)))


def _slabs(g):
    return g.reshape(8, g.shape[0] // 8, g.shape[1])


def _unslab(g):
    return g.reshape(8 * g.shape[1], g.shape[2])


def kernel(x, meta_tokens, norm1_w, w_in, ssd_conv_w, ssd_conv_b, ssd_dt_bias, ssd_a_log, ssd_d, ssd_norm_w, lru_conv_w, lru_conv_b, lru_wa, lru_ba, lru_wx, lru_bx, lru_lambda, lru_norm_w, w_out, norm2_w, w_gate, w_up, w_down, final_norm_w, loss_target, m_meta_tokens, m_norm1_w, m_w_in, m_ssd_conv_w, m_ssd_conv_b, m_ssd_dt_bias, m_ssd_a_log, m_ssd_d, m_ssd_norm_w, m_lru_conv_w, m_lru_conv_b, m_lru_wa, m_lru_ba, m_lru_wx, m_lru_bx, m_lru_lambda, m_lru_norm_w, m_w_out, m_norm2_w, m_w_gate, m_w_up, m_w_down, m_final_norm_w, v_meta_tokens, v_norm1_w, v_w_in, v_ssd_conv_w, v_ssd_conv_b, v_ssd_dt_bias, v_ssd_a_log, v_ssd_d, v_ssd_norm_w, v_lru_conv_w, v_lru_conv_b, v_lru_wa, v_lru_ba, v_lru_wx, v_lru_bx, v_lru_lambda, v_lru_norm_w, v_w_out, v_norm2_w, v_w_gate, v_w_up, v_w_down, v_final_norm_w):
    w = dict(meta_tokens=meta_tokens, norm1_w=norm1_w, w_in=w_in[0], ssd_conv_w=ssd_conv_w[0], ssd_conv_b=ssd_conv_b, ssd_dt_bias=ssd_dt_bias,
             ssd_a_log=ssd_a_log, ssd_d=ssd_d, ssd_norm_w=ssd_norm_w, lru_conv_w=lru_conv_w[0], lru_conv_b=lru_conv_b, lru_wa=lru_wa[0],
             lru_ba=lru_ba, lru_wx=lru_wx[0], lru_bx=lru_bx, lru_lambda=lru_lambda, lru_norm_w=lru_norm_w, w_out=w_out[0], norm2_w=norm2_w,
             w_gate=w_gate[0], w_up=w_up[0], w_down=w_down[0], final_norm_w=final_norm_w.reshape(1, D))
    m = dict(meta_tokens=m_meta_tokens, norm1_w=m_norm1_w, w_in=m_w_in[0], ssd_conv_w=m_ssd_conv_w[0], ssd_conv_b=m_ssd_conv_b,
             ssd_dt_bias=m_ssd_dt_bias, ssd_a_log=m_ssd_a_log, ssd_d=m_ssd_d, ssd_norm_w=m_ssd_norm_w, lru_conv_w=m_lru_conv_w[0],
             lru_conv_b=m_lru_conv_b, lru_wa=m_lru_wa[0], lru_ba=m_lru_ba, lru_wx=m_lru_wx[0], lru_bx=m_lru_bx, lru_lambda=m_lru_lambda,
             lru_norm_w=m_lru_norm_w, w_out=m_w_out[0], norm2_w=m_norm2_w, w_gate=m_w_gate[0], w_up=m_w_up[0], w_down=m_w_down[0],
             final_norm_w=m_final_norm_w.reshape(1, D))
    v = dict(meta_tokens=v_meta_tokens, norm1_w=v_norm1_w, w_in=v_w_in[0], ssd_conv_w=v_ssd_conv_w[0], ssd_conv_b=v_ssd_conv_b,
             ssd_dt_bias=v_ssd_dt_bias, ssd_a_log=v_ssd_a_log, ssd_d=v_ssd_d, ssd_norm_w=v_ssd_norm_w, lru_conv_w=v_lru_conv_w[0],
             lru_conv_b=v_lru_conv_b, lru_wa=v_lru_wa[0], lru_ba=v_lru_ba, lru_wx=v_lru_wx[0], lru_bx=v_lru_bx, lru_lambda=v_lru_lambda,
             lru_norm_w=v_lru_norm_w, w_out=v_w_out[0], norm2_w=v_norm2_w, w_gate=v_w_gate[0], w_up=v_w_up[0], w_down=v_w_down[0],
             final_norm_w=v_final_norm_w.reshape(1, D))
    shapes = dict(meta_tokens=meta_tokens.shape, norm1_w=norm1_w.shape, w_in=w_in.shape, ssd_conv_w=ssd_conv_w.shape,
                  ssd_conv_b=ssd_conv_b.shape, ssd_dt_bias=ssd_dt_bias.shape, ssd_a_log=ssd_a_log.shape, ssd_d=ssd_d.shape,
                  ssd_norm_w=ssd_norm_w.shape, lru_conv_w=lru_conv_w.shape, lru_conv_b=lru_conv_b.shape, lru_wa=lru_wa.shape,
                  lru_ba=lru_ba.shape, lru_wx=lru_wx.shape, lru_bx=lru_bx.shape, lru_lambda=lru_lambda.shape, lru_norm_w=lru_norm_w.shape,
                  w_out=w_out.shape, norm2_w=norm2_w.shape, w_gate=w_gate.shape, w_up=w_up.shape, w_down=w_down.shape,
                  final_norm_w=final_norm_w.shape)
    me = _index(*_place())
    for n in COLUMN_SHARDED:
        w[n], m[n], v[n] = w[n].T, m[n].T, v[n].T

    small_shard = jnp.concatenate([w["meta_tokens"], _pad_cols(w["ssd_conv_w"], 256).reshape(8, 128), w["lru_conv_w"],
                                   jnp.zeros((4, 128), F32)], axis=0)
    g_in, gs = all_gather("gather_w_in", [w["w_in"].astype(BF), small_shard])
    later = ["w_out", "w_gate", "w_up", "w_down"]
    started, behind = copies_start("gather_rest_start", [w[n].astype(BF) for n in later], False, gs)
    started = dict(zip(later, started))
    meta_full = gs[:, 0:16].transpose(1, 0, 2).reshape(N_META, D)
    ssd_cw = gs[:, 16:24].reshape(8, 4, 256)[:, :, :192].transpose(1, 0, 2).reshape(4, XBC)
    lru_cw = gs[:, 24:28].transpose(1, 0, 2).reshape(4, LRU_W)

    def fetch(names, after):
        got = copies_wait("gather_" + names[0] + "_wait", [started[n] for n in names], False, after)
        return [_unslab(g) for g in got]

    in_flight = {}

    def send(grads):
        names = list(grads)
        st, token = copies_start("grads_" + names[0] + "_start", [grads[n] if n == "small" else _slabs(grads[n]) for n in names], True, None)
        in_flight.update(zip(names, st))
        return token

    loss, grad_x, small = local_step(x[0], loss_target[0], meta_full, ssd_cw, lru_cw, _unslab(g_in), fetch, send,
                                     {**w, "norm1_w": w["norm1_w"] + behind})
    send({"small": _pack_small(small, loss).reshape(8, SM_ROWS // 8, 1024)})

    out = {}
    early = ["w_down", "w_gate", "w_up", "w_out"]
    recv = dict(zip(early, copies_wait("grads_early_wait", [in_flight[n] for n in early], True, in_flight["small"][2])))
    for n in early:
        out[n] = adamw_shard("adamw_" + n, recv[n], w[n], m[n], v[n], *BIG_TILE[n])
    recv_in, recv_small = copies_wait("grads_late_wait", [in_flight["w_in"], in_flight["small"]], True, [out[n][0] for n in early])
    untiled = (IN_COLS, 128)
    out["w_in"] = [o.reshape(IN_COLS // 8, D)
                   for o in adamw_w_in(recv_in, w["w_in"].reshape(untiled), m["w_in"].reshape(untiled), v["w_in"].reshape(untiled))]
    for n in COLUMN_SHARDED:
        out[n] = [o.T for o in out[n]]
    sm = all_gather("gather_small_grads", [sum_slabs(recv_small)])[0].reshape(SM_ROWS, 1024)
    special_g =[sm[SM_WA:SM_WA + 64].reshape(16, 64, 64), sm[SM_WX:SM_WX + 64].reshape(16, 64, 64),
                 lax.dynamic_slice(sm[SM_META:SM_META + 16], (0, 128 * me), (16, 128)),
                 lax.dynamic_slice(sm[SM_SCW:SM_SCW + 8].reshape(4, 2048), (0, 192 * me), (4, 192)),
                 lax.dynamic_slice(sm[SM_LCW:SM_LCW + 4], (0, 128 * me), (4, 128))]
    names = [n for n, _ in SIMPLE] + SPECIAL
    res = adamw_small(sm, special_g, [w[n] for n in names], [m[n] for n in names], [v[n] for n in names])
    for k, (n, _) in enumerate(SIMPLE):
        out[n] = res[4 * k:4 * k + 4]
    for k, n in enumerate(SPECIAL):
        o = 4 * len(SIMPLE) + 3 * k
        out[n] = [special_g[k]] + list(res[o:o + 3])
    loss_total = sm[SM_LOSS, 0]
    flat = [loss_total, grad_x[None]]
    for k in range(4):
        flat += [out[n][k].reshape(shapes[n]) for n in WEIGHTS]
    return tuple(flat)
```

```python
import math

import jax
import jax.numpy as jnp
from jax import lax
from jax.experimental import pallas as pl
from jax.experimental.pallas import tpu as pltpu

F32 = jnp.float32
BF = jnp.bfloat16

D = 1024
SEQ = 2048
N_META = 16
Q = 128
NPAD = 112
T = NPAD + N_META + SEQ
NCH = T // Q
RC = 544
D_FF = 2816
SSD_W = 1024
LRU_W = 1024
XBC = 1536
IN_COLS = 4624
PZ, PG, PXL, PXBC = 0, 1024, 2048, 3072
NP_IN = 4608
EPS = 1e-6
LRU_C = 8.0
VMEM_LIMIT = 56 * 1024 * 1024

ADAM_LR, ADAM_B1, ADAM_B2, ADAM_EPS, ADAM_WD, ADAM_STEP = 0.001, 0.9, 0.999, 1e-08, 0.01, 10

NT_DIMS = (((1,), (1,)), ((), ()))
TN_DIMS = (((0,), (0,)), ((), ()))
MESH = pl.DeviceIdType.MESH


def _params(n_grid=1, limit=VMEM_LIMIT):
    return pltpu.CompilerParams(dimension_semantics=("arbitrary",) * n_grid, vmem_limit_bytes=limit)


def _spec(shape, imap, single=False):
    if single:
        return pl.BlockSpec(shape, imap, pipeline_mode=pl.Buffered(1))
    return pl.BlockSpec(shape, imap)


def _sigmoid(x):
    return 0.5 * jnp.tanh(0.5 * x) + 0.5


def _sigmoid_gate(x):
    return 1.0 / (1.0 + jnp.exp(-x))


def _softplus(x):
    return jnp.maximum(x, 0.0) + jnp.log(1.0 + jnp.exp(-jnp.abs(x)))


def _rms_stats(h):
    return lax.rsqrt(jnp.mean(h * h, axis=-1, keepdims=True) + EPS)


def _rms(h, w):
    return (h * _rms_stats(h)) * w


def _rms_bwd(du, h, w):
    r = _rms_stats(h)
    n = h * r
    dn = du * w
    dh = r * (dn - n * jnp.mean(dn * n, axis=-1, keepdims=True))
    return dh, du * n


_G0 = math.sqrt(2.0 / math.pi)


def _gelu(x):
    return 0.5 * x * (1.0 + jnp.tanh(_G0 * (x + 0.044715 * (x * x * x))))


def _gelu_grad(x):
    t = jnp.tanh(_G0 * (x + 0.044715 * (x * x * x)))
    return 0.5 * (1.0 + t) + 0.5 * x * (1.0 - t * t) * (_G0 * (1.0 + 3.0 * 0.044715 * (x * x)))


def _rows(shape, r0=0):
    return lax.broadcasted_iota(jnp.int32, shape, 0) + r0


def _lanes(shape):
    return lax.broadcasted_iota(jnp.int32, shape, 1)


HALO = 8


def _fill_padded(pad_ref, x_ref):
    pad_ref[0:HALO, :] = jnp.zeros((HALO, pad_ref.shape[1]), F32)
    pad_ref[T + HALO:T + 2 * HALO, :] = jnp.zeros((HALO, pad_ref.shape[1]), F32)

    def step(c, carry):
        r0 = pl.multiple_of(c * Q, Q)
        pad_ref[pl.ds(r0 + HALO, Q), :] = x_ref[pl.ds(r0, Q), :].astype(F32)
        return carry

    lax.fori_loop(0, NCH, step, 0)


def _back(pad_ref, r0):
    win = pad_ref[pl.ds(r0, Q + HALO), :]
    return lambda s: win[HALO:, :] if s == 0 else pltpu.roll(win, s, axis=0)[HALO:, :]


def _ahead(pad_ref, r0):
    win = pad_ref[pl.ds(r0 + HALO, Q + HALO), :]
    return lambda s: win[:Q, :] if s == 0 else pltpu.roll(win, Q + HALO - s, axis=0)[:Q, :]


def _conv(back, w, b):
    y = b + w[3:4, :] * back(0)
    for k in range(3):
        y = y + w[k:k + 1, :] * back(3 - k)
    return y


def _conv_bwd_x(ahead, w):
    dx = w[3:4, :] * ahead(0)
    for k in range(3):
        dx = dx + w[k:k + 1, :] * ahead(3 - k)
    return dx


def _conv_bwd_w(dy, back):
    dws = [jnp.sum(dy * back(3 - k), axis=0, keepdims=True) for k in range(4)]
    return jnp.concatenate(dws, axis=0), jnp.sum(dy, axis=0, keepdims=True)


def _chunks(fn, unrolled=False):
    if unrolled:
        for c in range(NCH):
            fn(c * Q)
        return

    def step(c, carry):
        fn(pl.multiple_of(c * Q, Q))
        return carry

    lax.fori_loop(0, NCH, step, 0)


HALF = RC // 2


def _col_tiles(n, tn, fn):
    def step(j, carry):
        fn(pl.multiple_of(j * tn, tn))
        return carry

    lax.fori_loop(0, n // tn, step, 0)


def _rows_spec(cols, block_col=0):
    return _spec((RC, cols), lambda i: (i, block_col))


def _whole(shape):
    return _spec(shape, lambda i: tuple(0 for _ in shape), single=True)


def _vec(cols):
    return _spec((1, cols), lambda i: (0, 0))


def _zero_at_first(*refs):
    @pl.when(pl.program_id(0) == 0)
    def _():
        for r in refs:
            r[...] = jnp.zeros_like(r)


IN_RUNS = ((PZ, 0, 1024), (PG, 2576, 2048), (PXBC, 1024, XBC))


def _in_tiles(fn):
    for pcol, wrow, width in IN_RUNS:
        def step(j, carry, pcol=pcol, wrow=wrow):
            fn(pl.multiple_of(pcol + j * 512, 512), pl.multiple_of(wrow + j * 512, 16))
            return carry

        lax.fori_loop(0, width // 512, step, 0)


def in_proj(h0, wn, w_t, w_dt):
    def body(h_ref, wn_ref, w_ref, wdt_ref, o_ref, dt_ref, u_ref):
        for r in (0, HALF):
            u_ref[r:r + HALF, :] = _rms(h_ref[r:r + HALF, :], wn_ref[...]).astype(BF)

        def tile(pcol, wrow):
            o_ref[:, pl.ds(pcol, 512)] = lax.dot_general(u_ref[...], w_ref[pl.ds(wrow, 512), :], NT_DIMS, preferred_element_type=F32).astype(BF)

        _in_tiles(tile)
        dt_ref[...] = lax.dot_general(u_ref[...], wdt_ref[...], NT_DIMS, preferred_element_type=F32)

    return pl.pallas_call(
        body, grid=(T // RC,), in_specs=[_rows_spec(D), _vec(D), _whole((IN_COLS, D)), _whole((256, D))],
        out_specs=[_rows_spec(NP_IN), _rows_spec(256), _rows_spec(D)],
        out_shape=[jax.ShapeDtypeStruct((T, NP_IN), BF), jax.ShapeDtypeStruct((T, 256), F32), jax.ShapeDtypeStruct((T, D), BF)],
        compiler_params=_params(), name="in_proj")(h0, wn, w_t, w_dt)


def out_proj(yn_ssd, proj, hseq, lru_nw, w_out, h0):
    def body(y_ref, g_ref, h_ref, wn_ref, w_ref, r_ref, o_ref, cat_ref):
        cat_ref[:, 0:SSD_W] = y_ref[...]
        for r in (0, HALF):
            y = _gelu(g_ref[r:r + HALF, :].astype(F32)) * h_ref[r:r + HALF, :]
            cat_ref[r:r + HALF, SSD_W:] = _rms(y, wn_ref[...]).astype(BF)

        def tile(c0):
            o_ref[:, pl.ds(c0, 512)] = r_ref[:, pl.ds(c0, 512)] + jnp.dot(cat_ref[...], w_ref[:, pl.ds(c0, 512)], preferred_element_type=F32)

        _col_tiles(D, 512, tile)

    return pl.pallas_call(
        body, grid=(T // RC,),
        in_specs=[_rows_spec(SSD_W), _rows_spec(LRU_W, PG // LRU_W), _rows_spec(LRU_W), _vec(LRU_W), _whole((SSD_W + LRU_W, D)), _rows_spec(D)],
        out_specs=[_rows_spec(D), _rows_spec(SSD_W + LRU_W)],
        out_shape=[jax.ShapeDtypeStruct((T, D), F32), jax.ShapeDtypeStruct((T, SSD_W + LRU_W), BF)],
        compiler_params=_params(), name="out_proj")(yn_ssd, proj, hseq, lru_nw, w_out, h0)


def out_proj_bwd(dh1_b, w_out, proj, hseq, lru_nw):
    def body(d_ref, w_ref, g_ref, h_ref, wn_ref, dy_ref, dh_ref, dg_ref, dw_ref, dl_scr):
        _zero_at_first(dw_ref)

        def tile(c0):
            dy_ref[:, pl.ds(c0, 512)] = lax.dot_general(d_ref[...], w_ref[pl.ds(c0, 512), :], NT_DIMS, preferred_element_type=F32)
            dl_scr[:, pl.ds(c0, 512)] = lax.dot_general(d_ref[...], w_ref[pl.ds(SSD_W + c0, 512), :], NT_DIMS, preferred_element_type=F32)

        _col_tiles(SSD_W, 512, tile)
        for r in (0, HALF):
            g = g_ref[r:r + HALF, :].astype(F32)
            h = h_ref[r:r + HALF, :]
            ge = _gelu(g)
            dy, dw = _rms_bwd(dl_scr[r:r + HALF, :], ge * h, wn_ref[...])
            dw_ref[...] += jnp.sum(dw, axis=0, keepdims=True)
            dh_ref[r:r + HALF, :] = dy * ge
            dg_ref[r:r + HALF, :] = (dy * h * _gelu_grad(g)).astype(BF)

    return pl.pallas_call(
        body, grid=(T // RC,),
        in_specs=[_rows_spec(D), _whole((SSD_W + LRU_W, D)), _rows_spec(LRU_W, PG // LRU_W), _rows_spec(LRU_W), _vec(LRU_W)],
        out_specs=[_rows_spec(SSD_W), _rows_spec(LRU_W), _rows_spec(LRU_W), _vec(LRU_W)],
        out_shape=[jax.ShapeDtypeStruct((T, SSD_W), F32), jax.ShapeDtypeStruct((T, LRU_W), F32), jax.ShapeDtypeStruct((T, LRU_W), BF),
                   jax.ShapeDtypeStruct((1, LRU_W), F32)],
        scratch_shapes=[pltpu.VMEM((RC, LRU_W), F32)],
        compiler_params=_params(), name="out_proj_bwd")(dh1_b, w_out, proj, hseq, lru_nw)


def in_proj_bwd(dz, dg, dxl, dxbc, ddt, w_t, w_dt, h0, wn, dh1):
    first = NPAD + N_META

    def body(dz_ref, dg_ref, dxl_ref, dxbc_ref, ddt_ref, w_ref, wdt_ref, h_ref, wn_ref, r_ref, gx_hbm, meta_ref, dw_ref, du_scr, o_ref, sem):
        i = pl.program_id(0)
        _zero_at_first(dw_ref)
        du_scr[...] = jnp.dot(ddt_ref[...], wdt_ref[...], preferred_element_type=F32)
        for d_ref, wrow, width in ((dz_ref, 0, 1024), (dxbc_ref, 1024, XBC), (dg_ref, 2576, 1024), (dxl_ref, 3600, 1024)):
            def step(j, carry, d_ref=d_ref, wrow=wrow):
                c0 = pl.multiple_of(j * 512, 512)
                du_scr[...] += jnp.dot(d_ref[:, pl.ds(c0, 512)], w_ref[pl.ds(pl.multiple_of(wrow + c0, 16), 512), :], preferred_element_type=F32)
                return carry

            lax.fori_loop(0, width // 512, step, 0)
        for r in (0, HALF):
            dh, dw = _rms_bwd(du_scr[r:r + HALF, :], h_ref[r:r + HALF, :], wn_ref[...])
            dw_ref[...] += jnp.sum(dw, axis=0, keepdims=True)
            o_ref[r:r + HALF, :] = dh + r_ref[r:r + HALF, :]

        @pl.when(i == 0)
        def _():
            meta_ref[...] = o_ref[NPAD:first, :]
            head = pltpu.make_async_copy(o_ref.at[pl.ds(first, RC - first)], gx_hbm.at[pl.ds(0, RC - first)], sem)
            head.start()
            head.wait()

        @pl.when(i > 0)
        def _():
            rest = pltpu.make_async_copy(o_ref, gx_hbm.at[pl.ds(pl.multiple_of(i * RC - first, 32), RC)], sem)
            rest.start()
            rest.wait()

    return pl.pallas_call(
        body, grid=(T // RC,),
        in_specs=[_rows_spec(SSD_W), _rows_spec(LRU_W), _rows_spec(LRU_W), _rows_spec(XBC), _rows_spec(256), _whole((IN_COLS, D)),
                  _whole((256, D)), _rows_spec(D), _vec(D), _rows_spec(D)],
        out_specs=[pl.BlockSpec(memory_space=pl.ANY), _spec((N_META, D), lambda i: (0, 0)), _vec(D)],
        out_shape=[jax.ShapeDtypeStruct((SEQ, D), F32), jax.ShapeDtypeStruct((N_META, D), F32), jax.ShapeDtypeStruct((1, D), F32)],
        scratch_shapes=[pltpu.VMEM((RC, D), F32), pltpu.VMEM((RC, D), F32), pltpu.SemaphoreType.DMA],
        compiler_params=_params(), name="in_proj_bwd")(dz, dg, dxl, dxbc, ddt, w_t, w_dt, h0, wn, dh1)


def matmul_tn(name, a, b, tm, tn):
    m, n = a.shape[1], b.shape[1]

    def body(a_ref, b_ref, o_ref, acc_ref):
        acc_ref[...] = jnp.zeros_like(acc_ref)

        def mm(r0):
            acc_ref[...] += lax.dot_general(a_ref[pl.ds(r0, RC), :], b_ref[pl.ds(r0, RC), :], TN_DIMS, preferred_element_type=F32)

        _col_tiles(T, RC, mm)
        o_ref[...] = acc_ref[...].astype(BF)

    return pl.pallas_call(
        body, grid=(m // tm, n // tn),
        in_specs=[_spec((T, tm), lambda i, j: (0, i)), _spec((T, tn), lambda i, j: (0, j))],
        out_specs=_spec((tm, tn), lambda i, j: (i, j)),
        out_shape=jax.ShapeDtypeStruct((m, n), BF),
        scratch_shapes=[pltpu.VMEM((tm, tn), F32)],
        compiler_params=_params(2), name=name)(a, b)


def conv_silu_fwd(proj, cw, cb):
    def body(x_ref, w_ref, b_ref, o_ref, xpad):
        _fill_padded(xpad, x_ref)

        def chunk(r0):
            pre = _conv(_back(xpad, r0), w_ref[...], b_ref[...])
            o_ref[pl.ds(r0, Q), :] = pre * _sigmoid(pre)

        _chunks(chunk)

    c0 = PXBC // 128
    return pl.pallas_call(
        body, grid=(XBC // 128,),
        in_specs=[_spec((T, 128), lambda c: (0, c0 + c)), _spec((4, 128), lambda c: (0, c)), _spec((1, 128), lambda c: (0, c))],
        out_specs=_spec((T, 128), lambda c: (0, c)),
        out_shape=jax.ShapeDtypeStruct((T, XBC), F32), scratch_shapes=[pltpu.VMEM((T + 2 * HALO, 128), F32)],
        compiler_params=_params(), name="conv_silu_fwd")(proj, cw, cb)


def conv_silu_bwd(dx, d_b, d_c, proj, cw, cb):
    def body(dx_ref, db_ref, dc_ref, x_ref, w_ref, b_ref, o_ref, dw_ref, dbias_ref, xpad, dpad):
        tile = pl.program_id(0)
        _fill_padded(xpad, x_ref)
        dpad[0:HALO, :] = jnp.zeros((HALO, 128), F32)
        dpad[T + HALO:T + 2 * HALO, :] = jnp.zeros((HALO, 128), F32)
        dw_ref[...] = jnp.zeros_like(dw_ref)
        dbias_ref[...] = jnp.zeros_like(dbias_ref)

        def first(r0):
            back = _back(xpad, r0)
            pre = _conv(back, w_ref[...], b_ref[...])
            sg = _sigmoid(pre)
            rows = pl.ds(r0, Q)
            d = jnp.where(tile < 8, dx_ref[rows, :], jnp.where(tile < 10, db_ref[rows, :], dc_ref[rows, :]))
            dpre = d * (sg * (1.0 + pre * (1.0 - sg)))
            dpad[pl.ds(r0 + HALO, Q), :] = dpre
            dw, dbias = _conv_bwd_w(dpre, back)
            dw_ref[...] += dw
            dbias_ref[...] += dbias

        _chunks(first)

        def second(r0):
            o_ref[pl.ds(r0, Q), :] = _conv_bwd_x(_ahead(dpad, r0), w_ref[...]).astype(BF)

        _chunks(second)

    c0 = PXBC // 128
    pad = pltpu.VMEM((T + 2 * HALO, 128), F32)
    return pl.pallas_call(
        body, grid=(XBC // 128,),
        in_specs=[_spec((T, 128), lambda c: (0, jnp.minimum(c, 7))), _spec((T, 128), lambda c: (0, jnp.clip(c - 8, 0, 1))),
                  _spec((T, 128), lambda c: (0, jnp.clip(c - 10, 0, 1))), _spec((T, 128), lambda c: (0, c0 + c)),
                  _spec((4, 128), lambda c: (0, c)), _spec((1, 128), lambda c: (0, c))],
        out_specs=[_spec((T, 128), lambda c: (0, c)), _spec((4, 128), lambda c: (0, c)), _spec((1, 128), lambda c: (0, c))],
        out_shape=[jax.ShapeDtypeStruct((T, XBC), BF), jax.ShapeDtypeStruct((4, XBC), F32), jax.ShapeDtypeStruct((1, XBC), F32)],
        scratch_shapes=[pad, pad], compiler_params=_params(), name="conv_silu_bwd")(dx, d_b, d_c, proj, cw, cb)


def _ssd_chunk_common(row0, dt_ref, b_ref, c_ref, bias, a_neg):
    shape = (Q, Q)
    lane = _lanes(shape)
    sub = _rows(shape)
    live = (_rows(shape, row0) >= NPAD) & (lane < 8)
    dtr = dt_ref[:, :]
    dt = jnp.where(live, _softplus(dtr + bias), 0.0)
    d_a = dt * a_neg
    tri = (sub >= lane).astype(F32)
    cs = jnp.dot(tri, d_a, precision=lax.Precision.HIGHEST, preferred_element_type=F32)
    cs_t = cs.T
    b_f = b_ref[:, :]
    bc = b_f.astype(BF)
    cc = c_ref[:, :].astype(BF)
    cb = lax.dot_general(cc, bc, NT_DIMS, preferred_element_type=F32)
    cs_last = cs[Q - 1:Q, :]
    return dict(lane=lane, sub=sub, live=live, dtr=dtr, dt=dt, cs=cs, cs_t=cs_t, bc=bc, cc=cc, cb=cb, bc_t=b_f.T.astype(BF),
                ecs=jnp.exp(cs), dsm=jnp.exp(cs_last - cs), gam=jnp.exp(cs_last))


def _pair(lane_even, mat, j):
    return jnp.where(lane_even, mat[:, j:j + 1], mat[:, j + 1:j + 2])


def _pair_row(lane_even, mat, j):
    return jnp.where(lane_even[0:1, :], mat[:, j:j + 1], mat[:, j + 1:j + 2])


def _head_decay(cm, j):
    seg = cm["cs"][:, j:j + 1] - cm["cs_t"][j:j + 1, :]
    return jnp.exp(jnp.where(cm["sub"] >= cm["lane"], seg, -jnp.inf))


def _head_decay_t(cm, j):
    seg = cm["cs_t"][j:j + 1, :] - cm["cs"][:, j:j + 1]
    return jnp.exp(jnp.where(cm["lane"] >= cm["sub"], seg, -jnp.inf))


def ssd_fwd(xbc_act, proj, dt_raw, dt_bias2, a_log2, d2, norm_w):
    def body(x_all, b_all, c_all, dt_all, z_all, bias_all, alog_all, d_all, nw_all, yn_all, y_all, hp_all, h_all):
        @pl.when(pl.program_id(0) == 0)
        def _():
            h_all[...] = jnp.zeros_like(h_all)

        for g in range(2):
            wide, thin = slice(512 * g, 512 * g + 512), slice(128 * g, 128 * g + 128)
            group(x_all.at[:, wide], b_all.at[:, thin], c_all.at[:, thin], dt_all.at[:, thin], z_all.at[:, wide], bias_all.at[g],
                  alog_all.at[g], d_all.at[g], nw_all.at[:, wide], yn_all.at[:, wide], y_all.at[:, wide], hp_all.at[g, 0], h_all.at[g])

    def group(x_ref, b_ref, c_ref, dt_ref, z_ref, bias_ref, alog_ref, d_ref, nw_ref, yn_ref, y_ref, hp_ref, h_scr):
        bias = bias_ref[...]
        a_neg = -jnp.exp(alog_ref[...])
        dsk = d_ref[...]
        cm = _ssd_chunk_common(pl.program_id(0) * Q, dt_ref, b_ref, c_ref, bias, a_neg)
        lane_even = cm["lane"] < 64
        for p in range(4):
            je, jo = 2 * p, 2 * p + 1
            xp = x_ref[:, 128 * p:128 * p + 128]
            xdt = xp * _pair(lane_even, cm["dt"], je)
            xdt_b = xdt.astype(BF)
            m_e = (cm["cb"] * _head_decay(cm, je)).astype(BF)
            m_o = (cm["cb"] * _head_decay(cm, jo)).astype(BF)
            zero = jnp.zeros_like(xdt_b)
            yd = (jnp.dot(m_e, jnp.where(lane_even, xdt_b, zero), preferred_element_type=F32)
                  + jnp.dot(m_o, jnp.where(lane_even, zero, xdt_b), preferred_element_type=F32))
            hp = h_scr[p]
            hp_ref[p] = hp
            yo = jnp.dot(cm["cc"], hp.astype(BF), preferred_element_type=F32) * _pair(lane_even, cm["ecs"], je)
            y_ref[:, 128 * p:128 * p + 128] = yd + yo + xp * _pair_row(lane_even, dsk, je)
            st = jnp.dot(cm["bc_t"], (xdt * _pair(lane_even, cm["dsm"], je)).astype(BF), preferred_element_type=F32)
            h_scr[p] = hp * _pair_row(lane_even, cm["gam"], je) + st
        zc = z_ref[:, :].astype(F32)
        gated = y_ref[:, :] * (zc * _sigmoid(zc))
        yn_ref[:, :] = _rms(gated, nw_ref[...]).astype(BF)

    par = _spec((2, 1, 128), lambda c: (0, 0, 0))
    wide = _spec((Q, SSD_W), lambda c: (c, 0))
    return pl.pallas_call(
        body, grid=(NCH,),
        in_specs=[wide, _spec((Q, 256), lambda c: (c, 4)), _spec((Q, 256), lambda c: (c, 5)), _spec((Q, 256), lambda c: (c, 0)),
                  wide, par, par, par, _spec((1, SSD_W), lambda c: (0, 0))],
        out_specs=[wide, wide, _spec((2, 1, 4, 128, 128), lambda c: (0, c, 0, 0, 0))],
        out_shape=[jax.ShapeDtypeStruct((T, SSD_W), BF), jax.ShapeDtypeStruct((T, SSD_W), F32),
                   jax.ShapeDtypeStruct((2, NCH, 4, 128, 128), F32)],
        scratch_shapes=[pltpu.VMEM((2, 4, 128, 128), F32)],
        compiler_params=_params(), name="ssd_fwd")(xbc_act, xbc_act, xbc_act, dt_raw, proj, dt_bias2, a_log2, d2, norm_w)


def ssd_bwd(dyn, xbc_act, proj, dt_raw, y_pre, h_prev, dt_bias2, a_log2, d2, norm_w):
    def body(dyn_all, x_all, b_all, c_all, dt_all, z_all, y_all, hp_all, bias_all, alog_all, d_all, nw_all,
             dz_all, dx_all, db_all, dc_all, ddt_all, dpar_all, dnw_all, dh_all, acc_all):
        @pl.when(pl.program_id(0) == 0)
        def _():
            dh_all[...] = jnp.zeros_like(dh_all)
            acc_all[...] = jnp.zeros_like(acc_all)
            dnw_all[...] = jnp.zeros_like(dnw_all)

        for g in range(2):
            wide, thin = slice(512 * g, 512 * g + 512), slice(128 * g, 128 * g + 128)
            group(dyn_all.at[:, wide], x_all.at[:, wide], b_all.at[:, thin], c_all.at[:, thin], dt_all.at[:, thin], z_all.at[:, wide],
                  y_all.at[:, wide], hp_all.at[g, 0], bias_all.at[g], alog_all.at[g], d_all.at[g], nw_all.at[:, wide],
                  dz_all.at[:, wide], dx_all.at[:, wide], db_all.at[:, thin], dc_all.at[:, thin], ddt_all.at[:, thin], dpar_all.at[g],
                  dnw_all.at[:, wide], dh_all.at[g], acc_all.at[g])

    def group(dyn_ref, x_ref, b_ref, c_ref, dt_ref, z_ref, y_ref, hp_ref, bias_ref, alog_ref, d_ref, nw_ref,
              dz_ref, dx_ref, db_ref, dc_ref, ddt_ref, dpar_ref, dnw_ref, dh_scr, acc_scr):
        ci = pl.program_id(0)
        bias = bias_ref[...]
        a_neg = -jnp.exp(alog_ref[...])
        dsk = d_ref[...]
        cm = _ssd_chunk_common((NCH - 1 - ci) * Q, dt_ref, b_ref, c_ref, bias, a_neg)
        lane, sub = cm["lane"], cm["sub"]
        lane_even = lane < 64
        cc_t = c_ref[:, :].T.astype(BF)
        cb_t = lax.dot_general(cm["bc"], cm["cc"], NT_DIMS, preferred_element_type=F32)
        zc = z_ref[:, :].astype(F32)
        yc = y_ref[:, :]
        sg = _sigmoid(zc)
        sz = zc * sg
        dgated, dnw = _rms_bwd(dyn_ref[:, :], yc * sz, nw_ref[...])
        dnw_ref[...] += jnp.sum(dnw, axis=0, keepdims=True)
        dz_ref[:, :] = (dgated * yc * (sg * (1.0 + zc * (1.0 - sg)))).astype(BF)
        dy_all = dgated * sz
        dcb = jnp.zeros((Q, Q), F32)
        dcb_t = jnp.zeros((Q, Q), F32)
        db_acc = jnp.zeros((Q, Q), F32)
        dc_acc = jnp.zeros((Q, Q), F32)
        dcs = jnp.zeros((Q, Q), F32)
        ddt = jnp.zeros((Q, Q), F32)
        for p in range(4):
            je, jo = 2 * p, 2 * p + 1
            xp = x_ref[:, 128 * p:128 * p + 128]
            dy = dy_all[:, 128 * p:128 * p + 128]
            dt_p = _pair(lane_even, cm["dt"], je)
            xdt = xp * dt_p
            xdt_b = xdt.astype(BF)
            dy_b = dy.astype(BF)
            zero = jnp.zeros_like(dy_b)
            hp = hp_ref[p]
            hp_b = hp.astype(BF)
            dh = dh_scr[p]
            dh_b = dh.astype(BF)
            acc_scr[p:p + 1, :] += jnp.sum(dy * xp, axis=0, keepdims=True)
            dxp = dy * _pair_row(lane_even, dsk, je)
            e_p = _pair(lane_even, cm["ecs"], je)
            g_p = jnp.dot(cm["cc"], hp_b, preferred_element_type=F32)
            dg_b = (dy * e_p).astype(BF)
            de = dy * g_p * e_p
            dc_acc = dc_acc + lax.dot_general(dg_b, hp_b, NT_DIMS, preferred_element_type=F32)
            dh_in = jnp.dot(cc_t, dg_b, preferred_element_type=F32)
            ds_p = _pair(lane_even, cm["dsm"], je)
            r_p = jnp.dot(cm["bc"], dh_b, preferred_element_type=F32)
            dxdt = r_p * ds_p
            tt = r_p * xdt * ds_p
            db_acc = db_acc + lax.dot_general((xdt * ds_p).astype(BF), dh_b, NT_DIMS, preferred_element_type=F32)
            dgam_m = jnp.sum(dh * hp, axis=0, keepdims=True)
            for j, even in ((je, True), (jo, False)):
                sel = lane_even if even else jnp.logical_not(lane_even)
                dy_j = jnp.where(sel, dy_b, zero)
                l_j = _head_decay(cm, j)
                l_jt = _head_decay_t(cm, j)
                m_j = cm["cb"] * l_j
                m_jt = cb_t * l_jt
                dm = lax.dot_general(dy_j, xdt_b, NT_DIMS, preferred_element_type=F32)
                dm_t = lax.dot_general(xdt_b, dy_j, NT_DIMS, preferred_element_type=F32)
                dxdt = dxdt + jnp.dot(m_jt.astype(BF), dy_j, preferred_element_type=F32)
                dcb = dcb + dm * l_j
                dcb_t = dcb_t + dm_t * l_jt
                t_j = jnp.where(sel, tt, 0.0)
                col = jnp.sum(dm * m_j - dm_t * m_jt + (jnp.where(sel, de, 0.0) - t_j), axis=1, keepdims=True)
                gam_j = cm["gam"][:, j:j + 1]
                last = (jnp.sum(jnp.sum(t_j, axis=0, keepdims=True), axis=1, keepdims=True)
                        + jnp.sum(jnp.where(sel[0:1, :], dgam_m, 0.0), axis=1, keepdims=True) * gam_j)
                col = col + jnp.where(sub[:, 0:1] == Q - 1, last, 0.0)
                dcs = dcs + jnp.where(lane == j, col, 0.0)
            dh_scr[p] = dh_in + dh * _pair_row(lane_even, cm["gam"], je)
            dx_ref[:, 128 * p:128 * p + 128] = dxp + dxdt * dt_p
            dd = dxdt * xp
            ddt = ddt + jnp.where(lane == je, jnp.sum(jnp.where(lane_even, dd, 0.0), axis=1, keepdims=True), 0.0)
            ddt = ddt + jnp.where(lane == jo, jnp.sum(jnp.where(lane_even, 0.0, dd), axis=1, keepdims=True), 0.0)
        dc_ref[:, :] = dc_acc + jnp.dot(dcb.astype(BF), cm["bc"], preferred_element_type=F32)
        db_ref[:, :] = db_acc + jnp.dot(dcb_t.astype(BF), cm["cc"], preferred_element_type=F32)
        tri_t = (sub <= lane).astype(F32)
        dd_a = jnp.dot(tri_t, dcs, precision=lax.Precision.HIGHEST, preferred_element_type=F32)
        ddt = ddt + dd_a * a_neg
        acc_scr[5:6, :] += jnp.sum(dd_a * cm["dt"], axis=0, keepdims=True)
        draw = jnp.where(cm["live"], ddt * _sigmoid_gate(cm["dtr"] + bias), 0.0)
        acc_scr[4:5, :] += jnp.sum(draw, axis=0, keepdims=True)
        ddt_ref[:, :] = draw.astype(BF)

        @pl.when(ci == NCH - 1)
        def _():
            lane1 = _lanes((1, 128))
            dd = jnp.zeros((1, 128), F32)
            for p in range(4):
                row = acc_scr[p:p + 1, :]
                dd = dd + jnp.where(lane1 == 2 * p, jnp.sum(jnp.where(lane1 < 64, row, 0.0), axis=1, keepdims=True), 0.0)
                dd = dd + jnp.where(lane1 == 2 * p + 1, jnp.sum(jnp.where(lane1 < 64, 0.0, row), axis=1, keepdims=True), 0.0)
            dpar_ref[...] = jnp.concatenate([acc_scr[4:5, :], acc_scr[5:6, :] * a_neg, dd, jnp.zeros((5, 128), F32)], axis=0)

    par = _spec((2, 1, 128), lambda c: (0, 0, 0))
    wide = _spec((Q, SSD_W), lambda c: (NCH - 1 - c, 0))
    thin = _spec((Q, 256), lambda c: (NCH - 1 - c, 0))
    vec = _spec((1, SSD_W), lambda c: (0, 0))
    return pl.pallas_call(
        body, grid=(NCH,),
        in_specs=[wide, wide, _spec((Q, 256), lambda c: (NCH - 1 - c, 4)), _spec((Q, 256), lambda c: (NCH - 1 - c, 5)),
                  _spec((Q, 256), lambda c: (NCH - 1 - c, 0)), wide, wide,
                  _spec((2, 1, 4, 128, 128), lambda c: (0, NCH - 1 - c, 0, 0, 0)), par, par, par, vec],
        out_specs=[wide, wide, thin, thin, thin, _spec((2, 8, 128), lambda c: (0, 0, 0)), vec],
        out_shape=[jax.ShapeDtypeStruct((T, SSD_W), BF), jax.ShapeDtypeStruct((T, SSD_W), F32), jax.ShapeDtypeStruct((T, 256), F32),
                   jax.ShapeDtypeStruct((T, 256), F32), jax.ShapeDtypeStruct((T, 256), BF), jax.ShapeDtypeStruct((2, 8, 128), F32),
                   jax.ShapeDtypeStruct((1, SSD_W), F32)],
        scratch_shapes=[pltpu.VMEM((2, 4, 128, 128), F32), pltpu.VMEM((2, 8, 128), F32)],
        compiler_params=_params(), name="ssd_bwd")(dyn, xbc_act, xbc_act, xbc_act, dt_raw, proj, y_pre, h_prev, dt_bias2, a_log2, d2, norm_w)


def _lru_gates(back, cw, cb, wa, ba, wx, bx, lam):
    xr = _conv(back, cw, cb)
    xr_b = xr.astype(BF)
    r = _sigmoid_gate(jnp.dot(xr_b, wa, preferred_element_type=F32) + ba)
    i = _sigmoid_gate(jnp.dot(xr_b, wx, preferred_element_type=F32) + bx)
    sp = _softplus(-lam)
    la = (-LRU_C) * r * sp
    a = jnp.exp(la)
    mult2 = -jnp.tanh(la) * (a * a + 1.0)
    return xr, xr_b, r, i, sp, a, jnp.sqrt(mult2), mult2


def lru_gates_fwd(proj, cw, cb, wa2, ba, wx2, bx, lam):
    def body(x_ref, cw_ref, cb_ref, wa_ref, ba_ref, wx_ref, bx_ref, lam_ref, a_ref, u_ref, xpad):
        _fill_padded(xpad, x_ref)

        def chunk(r0):
            xr, _, _, i, _, a, mult, _ = _lru_gates(_back(xpad, r0), cw_ref[...], cb_ref[...], wa_ref[0], ba_ref[...], wx_ref[0], bx_ref[...],
                                                 lam_ref[...])
            a_ref[pl.ds(r0, Q), :] = a
            u_ref[pl.ds(r0, Q), :] = jnp.where(_rows(a.shape, r0) >= NPAD, mult * (i * xr), 0.0)

        _chunks(chunk, unrolled=True)

    c0 = PXL // 128
    vec = _spec((1, 128), lambda c: (0, c))
    mat = _spec((1, 128, 128), lambda c: (c, 0, 0))
    return pl.pallas_call(
        body, grid=(8,),
        in_specs=[_spec((T, 128), lambda c: (0, c0 + c)), _spec((4, 128), lambda c: (0, c)), vec, mat, vec, mat, vec, vec],
        out_specs=[_spec((T, 128), lambda c: (0, c)), _spec((T, 128), lambda c: (0, c))],
        out_shape=[jax.ShapeDtypeStruct((T, LRU_W), F32), jax.ShapeDtypeStruct((T, LRU_W), F32)],
        scratch_shapes=[pltpu.VMEM((T + 2 * HALO, 128), F32)],
        compiler_params=_params(), name="lru_gates_fwd")(proj, cw, cb, wa2, ba, wx2, bx, lam)


def lru_scan_fwd(a, u):
    def body(a_ref, u_ref, h_ref):
        def step(i, h):
            base = pl.multiple_of(i * 8, 8)
            for k in range(8):
                h = a_ref[pl.ds(base + k, 1), :] * h + u_ref[pl.ds(base + k, 1), :]
                h_ref[pl.ds(base + k, 1), :] = h
            return h

        lax.fori_loop(0, T // 8, step, jnp.zeros((1, LRU_W), F32))

    return pl.pallas_call(body, out_shape=jax.ShapeDtypeStruct((T, LRU_W), F32), compiler_params=_params(0), name="lru_scan_fwd")(a, u)


def lru_scan_bwd(a, dh_out):
    def body(a_ref, d_ref, o_ref):
        def step(i, carry):
            base = pl.multiple_of(T - 8 - i * 8, 8)
            for k in range(7, -1, -1):
                carry = d_ref[pl.ds(base + k, 1), :] + carry
                o_ref[pl.ds(base + k, 1), :] = carry
                carry = carry * a_ref[pl.ds(base + k, 1), :]
            return carry

        lax.fori_loop(0, T // 8, step, jnp.zeros((1, LRU_W), F32))

    return pl.pallas_call(body, out_shape=jax.ShapeDtypeStruct((T, LRU_W), F32), compiler_params=_params(0), name="lru_scan_bwd")(a, dh_out)


def lru_gates_bwd(dhs, hseq, proj, cw, cb, wa2, ba, wx2, bx, lam):
    def body(dh_ref, h_ref, x_ref, cw_ref, cb_ref, wa_ref, ba_ref, wx_ref, bx_ref, lam_ref,
             dx_ref, dcw_ref, dcb_ref, dwa_ref, dba_ref, dwx_ref, dbx_ref, dlam_ref, xpad, hpad, dpad):
        _fill_padded(xpad, x_ref)
        _fill_padded(hpad, h_ref)
        dpad[0:HALO, :] = jnp.zeros((HALO, 128), F32)
        dpad[T + HALO:T + 2 * HALO, :] = jnp.zeros((HALO, 128), F32)
        for ref in (dcw_ref, dcb_ref, dwa_ref, dba_ref, dwx_ref, dbx_ref, dlam_ref):
            ref[...] = jnp.zeros_like(ref)
        lam = lam_ref[...]

        def first(r0):
            back = _back(xpad, r0)
            xr, xr_b, r, i, sp, a, mult, mult2 = _lru_gates(back, cw_ref[...], cb_ref[...], wa_ref[0], ba_ref[...], wx_ref[0], bx_ref[...], lam)
            dh = dh_ref[pl.ds(r0, Q), :]
            da = dh * _back(hpad, r0)(1)
            du = jnp.where(_rows(dh.shape, r0) >= NPAD, dh, 0.0)
            dmult = du * (i * xr)
            di = du * (mult * xr)
            dxr = du * (mult * i)
            dla = da * a - dmult * (a * a) * lax.rsqrt(mult2)
            dr = dla * ((-LRU_C) * sp)
            dlam_ref[...] += jnp.sum(dla * ((-LRU_C) * r), axis=0, keepdims=True)
            dpr = dr * r * (1.0 - r)
            dpi = di * i * (1.0 - i)
            dba_ref[...] += jnp.sum(dpr, axis=0, keepdims=True)
            dbx_ref[...] += jnp.sum(dpi, axis=0, keepdims=True)
            dpr_b = dpr.astype(BF)
            dpi_b = dpi.astype(BF)
            dxr = (dxr + lax.dot_general(dpr_b, wa_ref[0], NT_DIMS, preferred_element_type=F32)
                   + lax.dot_general(dpi_b, wx_ref[0], NT_DIMS, preferred_element_type=F32))
            dwa_ref[0] += lax.dot_general(xr_b, dpr_b, TN_DIMS, preferred_element_type=F32)
            dwx_ref[0] += lax.dot_general(xr_b, dpi_b, TN_DIMS, preferred_element_type=F32)
            dpad[pl.ds(r0 + HALO, Q), :] = dxr
            dcw, dcb = _conv_bwd_w(dxr, back)
            dcw_ref[...] += dcw
            dcb_ref[...] += dcb

        _chunks(first, unrolled=True)
        dlam_ref[...] = -dlam_ref[...] * _sigmoid_gate(-lam)

        def second(r0):
            dx_ref[pl.ds(r0, Q), :] = _conv_bwd_x(_ahead(dpad, r0), cw_ref[...]).astype(BF)

        _chunks(second)

    c0 = PXL // 128
    vec = _spec((1, 128), lambda c: (0, c))
    mat = _spec((1, 128, 128), lambda c: (c, 0, 0))
    col = _spec((T, 128), lambda c: (0, c))
    vshape = jax.ShapeDtypeStruct((1, LRU_W), F32)
    mshape = jax.ShapeDtypeStruct((8, 128, 128), F32)
    pad = pltpu.VMEM((T + 2 * HALO, 128), F32)
    return pl.pallas_call(
        body, grid=(8,),
        in_specs=[col, col, _spec((T, 128), lambda c: (0, c0 + c)), _spec((4, 128), lambda c: (0, c)), vec, mat, vec, mat, vec, vec],
        out_specs=[col, _spec((4, 128), lambda c: (0, c)), vec, mat, vec, mat, vec, vec],
        out_shape=[jax.ShapeDtypeStruct((T, LRU_W), BF), jax.ShapeDtypeStruct((4, LRU_W), F32), vshape, mshape, vshape, mshape, vshape, vshape],
        scratch_shapes=[pad, pad, pad], compiler_params=_params(), name="lru_gates_bwd")(dhs, hseq, proj, cw, cb, wa2, ba, wx2, bx, lam)


def gate_up(h1, wn, w_gate, w_up):
    def body(h_ref, wn_ref, wg_ref, wu_ref, gt_ref, up_ref, act_ref, u_ref):
        for r in (0, HALF):
            u_ref[r:r + HALF, :] = _rms(h_ref[r:r + HALF, :], wn_ref[...]).astype(BF)

        def tile(c0):
            cols = pl.ds(c0, 256)
            gt = lax.dot_general(u_ref[...], wg_ref[cols, :], NT_DIMS, preferred_element_type=F32)
            up = lax.dot_general(u_ref[...], wu_ref[cols, :], NT_DIMS, preferred_element_type=F32)
            gt_ref[:, cols] = gt.astype(BF)
            up_ref[:, cols] = up.astype(BF)
            act_ref[:, cols] = (gt * _sigmoid(gt) * up).astype(BF)

        _col_tiles(D_FF, 256, tile)

    big = jax.ShapeDtypeStruct((T, D_FF), BF)
    return pl.pallas_call(
        body, grid=(T // RC,), in_specs=[_rows_spec(D), _vec(D), _whole((D_FF, D)), _whole((D_FF, D))],
        out_specs=[_rows_spec(D_FF), _rows_spec(D_FF), _rows_spec(D_FF), _rows_spec(D)],
        out_shape=[big, big, big, jax.ShapeDtypeStruct((T, D), BF)],
        compiler_params=_params(), name="gate_up")(h1, wn, w_gate, w_up)


def down_loss(act, w_down, h1, target, wf):
    first = NPAD + N_META

    def body(a_ref, w_ref, r_ref, t_hbm, wf_ref, d_ref, db_ref, l_ref, dw_ref, h_scr, t_ref, t_sem):
        i = pl.program_id(0)
        _zero_at_first(l_ref, dw_ref)
        head = pltpu.make_async_copy(t_hbm.at[pl.ds(0, RC - first)], t_ref.at[pl.ds(first, RC - first)], t_sem)
        rest = pltpu.make_async_copy(t_hbm.at[pl.ds(pl.multiple_of(jnp.maximum(i * RC - first, 0), 32), RC)], t_ref, t_sem)

        @pl.when(i == 0)
        def _():
            t_ref[0:first, :] = jnp.zeros((first, D), F32)
            head.start()

        @pl.when(i > 0)
        def _():
            rest.start()

        def tile(c0):
            cols = pl.ds(c0, 512)
            h_scr[:, cols] = r_ref[:, cols] + jnp.dot(a_ref[...], w_ref[:, cols], preferred_element_type=F32)

        _col_tiles(D, 512, tile)

        @pl.when(i == 0)
        def _():
            head.wait()

        @pl.when(i > 0)
        def _():
            rest.wait()

        for r in (0, HALF):
            h = h_scr[r:r + HALF, :]
            live = _rows((HALF, D), i * RC + r) >= first
            err = jnp.where(live, _rms(h, wf_ref[...]) - t_ref[r:r + HALF, :], 0.0)
            l_ref[...] += 0.5 * jnp.sum(jnp.sum(err * err, axis=1, keepdims=True) * (1.0 / D), axis=0, keepdims=True)
            dh, dw = _rms_bwd(err * (1.0 / D), h, wf_ref[...])
            dw_ref[...] += jnp.sum(dw, axis=0, keepdims=True)
            d_ref[r:r + HALF, :] = dh
            db_ref[r:r + HALF, :] = dh.astype(BF)

    return pl.pallas_call(
        body, grid=(T // RC,),
        in_specs=[_rows_spec(D_FF), _whole((D_FF, D)), _rows_spec(D), pl.BlockSpec(memory_space=pl.ANY), _vec(D)],
        out_specs=[_rows_spec(D), _rows_spec(D), _spec((1, 128), lambda i: (0, 0)), _vec(D)],
        out_shape=[jax.ShapeDtypeStruct((T, D), F32), jax.ShapeDtypeStruct((T, D), BF), jax.ShapeDtypeStruct((1, 128), F32),
                   jax.ShapeDtypeStruct((1, D), F32)],
        scratch_shapes=[pltpu.VMEM((RC, D), F32), pltpu.VMEM((RC, D), F32), pltpu.SemaphoreType.DMA],
        compiler_params=_params(), name="down_loss")(act, w_down, h1, target, wf)


def swiglu_bwd(dh2_b, w_down, gt, up):
    def body(d_ref, w_ref, gt_ref, up_ref, dg_ref, du_ref):
        def tile(c0):
            cols = pl.ds(c0, 256)
            dact = lax.dot_general(d_ref[...], w_ref[cols, :], NT_DIMS, preferred_element_type=F32)
            gt_ = gt_ref[:, cols].astype(F32)
            up_ = up_ref[:, cols].astype(F32)
            sg = _sigmoid(gt_)
            dg_ref[:, cols] = (dact * up_ * (sg * (1.0 + gt_ * (1.0 - sg)))).astype(BF)
            du_ref[:, cols] = (dact * (gt_ * sg)).astype(BF)

        _col_tiles(D_FF, 256, tile)

    big = jax.ShapeDtypeStruct((T, D_FF), BF)
    return pl.pallas_call(
        body, grid=(T // RC,), in_specs=[_rows_spec(D), _whole((D_FF, D)), _rows_spec(D_FF), _rows_spec(D_FF)],
        out_specs=[_rows_spec(D_FF), _rows_spec(D_FF)], out_shape=[big, big], compiler_params=_params(), name="swiglu_bwd")(dh2_b, w_down, gt, up)


def gate_up_bwd(dgt, dup, w_gate, w_up, h1, wn, dh2):
    def body(dg_ref, du_ref, wg_ref, wu_ref, h_ref, wn_ref, r_ref, d_ref, db_ref, dw_ref, du_scr):
        _zero_at_first(dw_ref)

        du_scr[...] = jnp.zeros_like(du_scr)

        def tile(c0):
            k = pl.ds(c0, 256)
            du_scr[...] += (jnp.dot(dg_ref[:, k], wg_ref[k, :], preferred_element_type=F32)
                            + jnp.dot(du_ref[:, k], wu_ref[k, :], preferred_element_type=F32))

        _col_tiles(D_FF, 256, tile)
        for r in (0, HALF):
            dh, dw = _rms_bwd(du_scr[r:r + HALF, :], h_ref[r:r + HALF, :], wn_ref[...])
            dw_ref[...] += jnp.sum(dw, axis=0, keepdims=True)
            dh = dh + r_ref[r:r + HALF, :]
            d_ref[r:r + HALF, :] = dh
            db_ref[r:r + HALF, :] = dh.astype(BF)

    return pl.pallas_call(
        body, grid=(T // RC,),
        in_specs=[_rows_spec(D_FF), _rows_spec(D_FF), _whole((D_FF, D)), _whole((D_FF, D)), _rows_spec(D), _vec(D), _rows_spec(D)],
        out_specs=[_rows_spec(D), _rows_spec(D), _vec(D)],
        out_shape=[jax.ShapeDtypeStruct((T, D), F32), jax.ShapeDtypeStruct((T, D), BF), jax.ShapeDtypeStruct((1, D), F32)],
        scratch_shapes=[pltpu.VMEM((RC, D), F32)],
        compiler_params=_params(), name="gate_up_bwd")(dgt, dup, w_gate, w_up, h1, wn, dh2)


def _adamw(w, g, m, v):
    m = ADAM_B1 * m + (1.0 - ADAM_B1) * g
    v = ADAM_B2 * v + (1.0 - ADAM_B2) * (g * g)
    m_hat = m / (1.0 - ADAM_B1 ** ADAM_STEP)
    v_hat = v / (1.0 - ADAM_B2 ** ADAM_STEP)
    delta = -ADAM_LR * (m_hat / (jnp.sqrt(v_hat) + ADAM_EPS) + ADAM_WD * w)
    return delta, m, v


def adamw_shard(name, recv, w, m, v, tr, tc):
    r, c = w.shape

    def body(p_ref, w_ref, m_ref, v_ref, g_ref, d_ref, mo_ref, vo_ref):
        g = p_ref[0].astype(F32)
        for s in range(1, 8):
            g = g + p_ref[s].astype(F32)
        g_ref[...] = g
        d_ref[...], mo_ref[...], vo_ref[...] = _adamw(w_ref[...], g, m_ref[...], v_ref[...])

    tile = _spec((tr, tc), lambda i, j: (i, j))
    shape = jax.ShapeDtypeStruct((r, c), F32)
    return pl.pallas_call(
        body, grid=(r // tr, c // tc), in_specs=[_spec((8, tr, tc), lambda i, j: (0, i, j)), tile, tile, tile],
        out_specs=[tile] * 4, out_shape=[shape] * 4, compiler_params=_params(2), name=name)(recv, w, m, v)


def adamw_w_in(recv, w, m, v):
    rows = w.shape[0] // 8

    def body(p_ref, w_ref, m_ref, v_ref, g_ref, d_ref, mo_ref, vo_ref):
        for q in range(8):
            cols = slice(128 * q, 128 * q + 128)
            g = p_ref[0, :, cols].astype(F32)
            for s in range(1, 8):
                g = g + p_ref[s, :, cols].astype(F32)
            part = pl.ds(q, rows, stride=8)
            g_ref[part, :] = g
            d_ref[part, :], mo_ref[part, :], vo_ref[part, :] = _adamw(w_ref[part, :], g, m_ref[part, :], v_ref[part, :])

    shape = jax.ShapeDtypeStruct(w.shape, F32)
    return pl.pallas_call(body, out_shape=[shape] * 4, compiler_params=_params(0), name="adamw_w_in")(recv, w, m, v)


def sum_slabs(recv):
    def body(p_ref, o_ref):
        g = p_ref[0]
        for s in range(1, 8):
            g = g + p_ref[s]
        o_ref[...] = g

    return pl.pallas_call(body, out_shape=jax.ShapeDtypeStruct(recv.shape[1:], F32), compiler_params=_params(0), name="sum_slabs")(recv)


SIMPLE = [("norm1_w", 1024), ("ssd_conv_b", 1536), ("ssd_dt_bias", 16), ("ssd_a_log", 16), ("ssd_d", 16), ("ssd_norm_w", 1024),
          ("lru_conv_b", 1024), ("lru_ba", 1024), ("lru_bx", 1024), ("lru_lambda", 1024), ("lru_norm_w", 1024), ("norm2_w", 1024),
          ("final_norm_w", 1024)]
SPECIAL = ["lru_wa", "lru_wx", "meta_tokens", "ssd_conv_w", "lru_conv_w"]
SM_ROWS = 176
SM_WA, SM_WX, SM_META, SM_SCW, SM_LCW, SM_LOSS = 14, 78, 142, 158, 166, 170


def _simple_rows():
    rows, r = {}, 0
    for name, n in SIMPLE:
        rows[name] = r
        r += -(-n // 1024)
    return rows


def adamw_small(sm, special_g, ws, ms, vs):
    rows = _simple_rows()
    ns, nx = len(SIMPLE), len(SPECIAL)

    def body(*refs):
        sm_ref = refs[0]
        gx = refs[1:1 + nx]
        wr = refs[1 + nx:1 + nx + ns + nx]
        mr = refs[1 + nx + ns + nx:1 + nx + 2 * (ns + nx)]
        vr = refs[1 + nx + 2 * (ns + nx):1 + nx + 3 * (ns + nx)]
        outs = refs[1 + nx + 3 * (ns + nx):]
        o = 0
        for k, (name, n) in enumerate(SIMPLE):
            r0 = rows[name]
            for c0 in range(0, n, 1024):
                wd = min(1024, n - c0)
                g = sm_ref[r0 + c0 // 1024:r0 + c0 // 1024 + 1, 0:wd]
                sl = (slice(None), slice(c0, c0 + wd))
                d, m2, v2 = _adamw(wr[k][sl], g, mr[k][sl], vr[k][sl])
                outs[o][sl] = g
                outs[o + 1][sl] = d
                outs[o + 2][sl] = m2
                outs[o + 3][sl] = v2
            o += 4
        for k in range(nx):
            d, m2, v2 = _adamw(wr[ns + k][...], gx[k][...], mr[ns + k][...], vr[ns + k][...])
            outs[o][...] = d
            outs[o + 1][...] = m2
            outs[o + 2][...] = v2
            o += 3

    out_shape = []
    for k in range(ns):
        out_shape += [jax.ShapeDtypeStruct(ws[k].shape, F32)] * 4
    for k in range(nx):
        out_shape += [jax.ShapeDtypeStruct(ws[ns + k].shape, F32)] * 3
    return pl.pallas_call(body, out_shape=out_shape, compiler_params=_params(0), name="adamw_small")(sm, *special_g, *ws, *ms, *vs)


def _place():
    return lax.axis_index("x"), lax.axis_index("y"), lax.axis_index("c")


def _index(px, py, pc):
    return 4 * px + 2 * py + pc


def all_gather(name, shards):
    n = len(shards)
    hbm = pl.BlockSpec(memory_space=pl.ANY)

    def body(*refs):
        ins, outs = refs[:n], refs[n:2 * n]
        send_sems, recv_sems, local_sems = refs[2 * n:]
        x, y, c = _place()
        me, sibling = (x, y, c), (x, y, 1 - c)
        chips = [(1 - x, y), (x, 1 - y), (1 - x, 1 - y)]

        def copy(i, k, block, to, src=None):
            dst = outs[i].at[_index(*block)]
            return pltpu.make_async_remote_copy(src_ref=dst if src is None else src, dst_ref=dst, send_sem=send_sems.at[7 * i + k],
                                                recv_sem=recv_sems.at[7 * i + k], device_id=to, device_id_type=MESH)

        mine = [pltpu.make_async_copy(ins[i], outs[i].at[_index(*me)], local_sems.at[i]) for i in range(n)]
        for cp in mine:
            cp.start()
        first = []
        for i in range(n):
            first += [copy(i, 1 + j, me, (*chip, c), src=ins[i]) for j, chip in enumerate(chips)]
            first.append(copy(i, 0, me, sibling, src=ins[i]))
        for cp in first:
            cp.start()
        passed = []
        for i in range(n):
            for j, chip in enumerate(chips):
                copy(i, 1 + j, (*chip, c), me).wait_recv()
                cp = copy(i, 4 + j, (*chip, c), sibling)
                cp.start()
                passed.append(cp)
        for i in range(n):
            copy(i, 0, sibling, me).wait_recv()
            for j, chip in enumerate(chips):
                copy(i, 4 + j, (*chip, 1 - c), me).wait_recv()
        for cp in first + passed:
            cp.wait_send()
        for cp in mine:
            cp.wait()

    return pl.pallas_call(
        body, in_specs=[hbm] * n, out_specs=[hbm] * n,
        out_shape=[jax.ShapeDtypeStruct((8,) + s.shape, s.dtype) for s in shards],
        scratch_shapes=[pltpu.SemaphoreType.DMA((7 * n,)), pltpu.SemaphoreType.DMA((7 * n,)), pltpu.SemaphoreType.DMA((n,))],
        name=name)(*shards)


HBM_SPEC = pl.BlockSpec(memory_space=pltpu.HBM)
SEM_SPEC = pl.BlockSpec(memory_space=pltpu.SEMAPHORE)
EFFECT = pltpu.SideEffectType.DATAFLOW_SIDE_EFFECTING


def _peers(x, y, c):
    return [((1 - x) if k & 4 else x, (1 - y) if k & 2 else y, (1 - c) if k & 1 else c) for k in range(1, 8)]


def _pieces(rows):
    for n in (4, 2):
        if rows % (16 * n) == 0:
            return [(r * (rows // n), rows // n) for r in range(n)]
    return [(0, rows)]


def _peer_copies(src, land, send_sems, recv_sems, k, peer, mine, slab_src):
    block = src.at[_index(*peer)] if slab_src else src
    return [pltpu.make_async_remote_copy(src_ref=block.at[pl.ds(r0, nr)], dst_ref=land.at[mine, pl.ds(r0, nr)], send_sem=send_sems.at[k],
                                         recv_sem=recv_sems.at[k], device_id=peer, device_id_type=MESH)
            for r0, nr in _pieces(block.shape[0])]


def copies_start(name, srcs, slab_src, after):
    n = len(srcs)
    zones = [jax.ShapeDtypeStruct(s.shape if slab_src else (8,) + s.shape, s.dtype) for s in srcs]
    afters = [] if after is None else [after]

    def body(*refs):
        ins, lands = refs[:n], refs[n:2 * n]
        first = 2 * n + len(afters)
        sends, recvs = refs[first:first + n], refs[first + n:first + 2 * n]
        token = refs[-1]
        x, y, c = _place()
        mine = _index(x, y, c)
        for i in range(n):
            per_peer = [_peer_copies(ins[i], lands[i], sends[i], recvs[i], k, peer, mine, slab_src) for k, peer in enumerate(_peers(x, y, c))]
            for piece in zip(*per_peer):
                for cp in piece:
                    cp.start()
        token[...] = jnp.zeros_like(token)

    sem = pltpu.SemaphoreType.DMA((7,))
    res = pl.pallas_call(
        body, name=name,
        out_shape=([sem] * (2 * n) + [pltpu.HBM(s.shape, s.dtype) for s in srcs] + [pltpu.HBM(z.shape, z.dtype) for z in zones]
                   + [jax.ShapeDtypeStruct((8, 128), F32)]),
        in_specs=[HBM_SPEC] * (2 * n) + [pl.BlockSpec(memory_space=pl.ANY)] * len(afters),
        out_specs=[SEM_SPEC] * (2 * n) + [HBM_SPEC] * (2 * n) + [pl.BlockSpec(memory_space=pltpu.VMEM)],
        input_output_aliases={i: 2 * n + i for i in range(2 * n)},
        compiler_params=pltpu.CompilerParams(has_side_effects=EFFECT),
    )(*[pltpu.with_memory_space_constraint(s, pltpu.HBM) for s in srcs],
      *[pltpu.with_memory_space_constraint(lax.empty(z.shape, z.dtype), pltpu.HBM) for z in zones], *afters)
    return [(res[i], res[n + i], res[2 * n + i], res[3 * n + i]) for i in range(n)], res[-1][0:1, 0:1]


def copies_wait(name, started, slab_src, after):
    n = len(started)

    def body(*refs):
        ins, lands = refs[:n], refs[n:2 * n]
        sends, recvs = refs[2 * n:3 * n], refs[3 * n:4 * n]
        x, y, c = _place()
        mine = _index(x, y, c)
        for i in range(n):
            for k, peer in enumerate(_peers(x, y, c)):
                arrival = pltpu.make_async_remote_copy(src_ref=ins[i].at[mine] if slab_src else ins[i], dst_ref=lands[i].at[_index(*peer)],
                                                       send_sem=sends[i].at[k], recv_sem=recvs[i].at[k], device_id=peer, device_id_type=MESH)
                arrival.wait_send()
                arrival.wait_recv()

    srcs = [s[2] for s in started]
    lands = [s[3] for s in started]
    afters = list(after) if isinstance(after, (list, tuple)) else [after]
    res = pl.pallas_call(
        body, name=name,
        out_shape=[pltpu.HBM(s.shape, s.dtype) for s in srcs] + [pltpu.HBM(z.shape, z.dtype) for z in lands],
        in_specs=[HBM_SPEC] * (2 * n) + [SEM_SPEC] * (2 * n) + [pl.BlockSpec(memory_space=pl.ANY)] * len(afters),
        out_specs=[HBM_SPEC] * (2 * n),
        input_output_aliases={i: i for i in range(2 * n)},
        compiler_params=pltpu.CompilerParams(has_side_effects=EFFECT),
    )(*srcs, *lands, *[s[0] for s in started], *[s[1] for s in started], *afters)
    me = _index(*_place())
    own = [lax.dynamic_index_in_dim(s, me, 0, keepdims=True) if slab_src else s[None] for s in res[:n]]
    return [lax.dynamic_update_slice_in_dim(z, o, me, 0) for z, o in zip(res[n:], own)]


WEIGHTS = ["meta_tokens", "norm1_w", "w_in", "ssd_conv_w", "ssd_conv_b", "ssd_dt_bias", "ssd_a_log", "ssd_d", "ssd_norm_w", "lru_conv_w",
           "lru_conv_b", "lru_wa", "lru_ba", "lru_wx", "lru_bx", "lru_lambda", "lru_norm_w", "w_out", "norm2_w", "w_gate", "w_up", "w_down",
           "final_norm_w"]
BIG = ["w_in", "w_out", "w_gate", "w_up", "w_down"]
COLUMN_SHARDED = ["w_in", "w_gate", "w_up"]
BIG_TILE = {"w_in": (578, 256), "w_out": (128, 1024), "w_gate": (176, 1024), "w_up": (176, 1024), "w_down": (176, 1024)}


def _pair_blocks(w):
    w = w.reshape(8, 2, 64, 64)
    z = jnp.zeros((8, 64, 64), w.dtype)
    return jnp.concatenate([jnp.concatenate([w[:, 0], z], axis=2), jnp.concatenate([z, w[:, 1]], axis=2)], axis=1)


def _unpair_blocks(w2):
    return jnp.stack([w2[:, :64, :64], w2[:, 64:, 64:]], axis=1).reshape(16, 64, 64)


def _per_group(v):
    return jnp.pad(v.reshape(2, 1, 8), ((0, 0), (0, 0), (0, 120)))


def _pad_cols(v, n):
    return jnp.pad(v, ((0, 0), (0, n - v.shape[1])))


def local_step(x, target, meta, ssd_cw, lru_cw, w_in, fetch, send, p):
    z120 = jnp.zeros((120, D), BF)
    w_dt = jnp.concatenate([w_in[2560:2568], z120, w_in[2568:2576], z120], axis=0)
    bias2, alog2, d2 = _per_group(p["ssd_dt_bias"]), _per_group(p["ssd_a_log"]), _per_group(p["ssd_d"])
    wa2 = _pair_blocks(p["lru_wa"]).astype(BF)
    wx2 = _pair_blocks(p["lru_wx"]).astype(BF)
    lru = (lru_cw, p["lru_conv_b"], wa2, p["lru_ba"], wx2, p["lru_bx"], p["lru_lambda"])

    h0 = jnp.concatenate([jnp.zeros((NPAD, D), F32), meta, x], axis=0)
    proj, dt_raw, u1 = in_proj(h0, p["norm1_w"], w_in, w_dt)
    xbc_act = conv_silu_fwd(proj, ssd_cw, p["ssd_conv_b"])
    yn_ssd, y_pre, h_prev = ssd_fwd(xbc_act, proj, dt_raw, bias2, alog2, d2, p["ssd_norm_w"])
    a, u = lru_gates_fwd(proj, *lru)
    hseq = lru_scan_fwd(a, u)
    (w_out,) = fetch(["w_out"], hseq)
    h1, cat = out_proj(yn_ssd, proj, hseq, p["lru_norm_w"], w_out, h0)
    w_gate, w_up = fetch(["w_gate", "w_up"], h1)
    gt, up, act, u2 = gate_up(h1, p["norm2_w"], w_gate, w_up)
    (w_down,) = fetch(["w_down"], act)
    dh2, dh2_b, loss, d_fnw = down_loss(act, w_down, h1, target, p["final_norm_w"])

    dgt, dup = swiglu_bwd(dh2_b, w_down, gt, up)
    g_down = matmul_tn("dw_down", act, dh2_b, 1408, 512)
    g_gate = matmul_tn("dw_gate", dgt, u2, 1408, 512)
    g_up = matmul_tn("dw_up", dup, u2, 1408, 512)
    sent = send({"w_down": g_down, "w_gate": g_gate, "w_up": g_up})
    dh1, dh1_b, d_n2 = gate_up_bwd(dgt, dup, w_gate, w_up, h1, p["norm2_w"] + sent, dh2)
    sent = send({"w_out": matmul_tn("dw_out", cat, dh1_b, 512, 1024)})
    dyn, dh_out, dg_b, d_lnw = out_proj_bwd(dh1_b, w_out, proj, hseq, p["lru_norm_w"] + sent)

    dhs = lru_scan_bwd(a, dh_out)
    dxl_b, d_lcw, d_lcb, dwa2, d_ba, dwx2, d_bx, d_lam = lru_gates_bwd(dhs, hseq, proj, *lru)
    dz_b, dx, d_b, d_c, ddt_b, dpar, d_snw = ssd_bwd(dyn, xbc_act, proj, dt_raw, y_pre, h_prev, bias2, alog2, d2, p["ssd_norm_w"])
    dxbc_b, d_scw, d_scb = conv_silu_bwd(dx, d_b, d_c, proj, ssd_cw, p["ssd_conv_b"])
    g_dt = matmul_tn("dw_in_dt", ddt_b, u1, 256, 512)
    g_in = jnp.concatenate([matmul_tn("dw_in_z", dz_b, u1, 512, 1024), matmul_tn("dw_in_xbc", dxbc_b, u1, 512, 1024), g_dt[0:8], g_dt[128:136],
                            matmul_tn("dw_in_g", dg_b, u1, 512, 1024), matmul_tn("dw_in_xl", dxl_b, u1, 512, 1024)], axis=0)
    sent = send({"w_in": g_in})
    grad_x, d_meta, d_n1 = in_proj_bwd(dz_b, dg_b, dxl_b, dxbc_b, ddt_b, w_in, w_dt, h0, p["norm1_w"] + sent, dh1)
    small = {"norm1_w": d_n1, "ssd_conv_b": d_scb, "ssd_dt_bias": dpar[:, 0, :8].reshape(1, 16), "ssd_a_log": dpar[:, 1, :8].reshape(1, 16),
             "ssd_d": dpar[:, 2, :8].reshape(1, 16), "ssd_norm_w": d_snw, "lru_conv_b": d_lcb, "lru_ba": d_ba, "lru_bx": d_bx,
             "lru_lambda": d_lam, "lru_norm_w": d_lnw, "norm2_w": d_n2, "final_norm_w": d_fnw,
             "lru_wa": _unpair_blocks(dwa2), "lru_wx": _unpair_blocks(dwx2), "meta_tokens": d_meta,
             "ssd_conv_w": d_scw, "lru_conv_w": d_lcw}
    return loss, grad_x, small


def _pack_small(small, loss):
    rows = [_pad_cols(small[name], -(-n // 1024) * 1024).reshape(-1, 1024) for name, n in SIMPLE]
    rows += [small["lru_wa"].reshape(64, 1024), small["lru_wx"].reshape(64, 1024), small["meta_tokens"],
             _pad_cols(small["ssd_conv_w"], 2048).reshape(8, 1024), small["lru_conv_w"], _pad_cols(loss[:, 0:1], 1024)]
    sm = jnp.concatenate(rows, axis=0)
    return jnp.pad(sm, ((0, SM_ROWS - sm.shape[0]), (0, 0)))


def _slabs(g):
    return g.reshape(8, g.shape[0] // 8, g.shape[1])


def _unslab(g):
    return g.reshape(8 * g.shape[1], g.shape[2])


def kernel(x, meta_tokens, norm1_w, w_in, ssd_conv_w, ssd_conv_b, ssd_dt_bias, ssd_a_log, ssd_d, ssd_norm_w, lru_conv_w, lru_conv_b, lru_wa, lru_ba, lru_wx, lru_bx, lru_lambda, lru_norm_w, w_out, norm2_w, w_gate, w_up, w_down, final_norm_w, loss_target, m_meta_tokens, m_norm1_w, m_w_in, m_ssd_conv_w, m_ssd_conv_b, m_ssd_dt_bias, m_ssd_a_log, m_ssd_d, m_ssd_norm_w, m_lru_conv_w, m_lru_conv_b, m_lru_wa, m_lru_ba, m_lru_wx, m_lru_bx, m_lru_lambda, m_lru_norm_w, m_w_out, m_norm2_w, m_w_gate, m_w_up, m_w_down, m_final_norm_w, v_meta_tokens, v_norm1_w, v_w_in, v_ssd_conv_w, v_ssd_conv_b, v_ssd_dt_bias, v_ssd_a_log, v_ssd_d, v_ssd_norm_w, v_lru_conv_w, v_lru_conv_b, v_lru_wa, v_lru_ba, v_lru_wx, v_lru_bx, v_lru_lambda, v_lru_norm_w, v_w_out, v_norm2_w, v_w_gate, v_w_up, v_w_down, v_final_norm_w):
    w = dict(meta_tokens=meta_tokens, norm1_w=norm1_w, w_in=w_in[0], ssd_conv_w=ssd_conv_w[0], ssd_conv_b=ssd_conv_b, ssd_dt_bias=ssd_dt_bias,
             ssd_a_log=ssd_a_log, ssd_d=ssd_d, ssd_norm_w=ssd_norm_w, lru_conv_w=lru_conv_w[0], lru_conv_b=lru_conv_b, lru_wa=lru_wa[0],
             lru_ba=lru_ba, lru_wx=lru_wx[0], lru_bx=lru_bx, lru_lambda=lru_lambda, lru_norm_w=lru_norm_w, w_out=w_out[0], norm2_w=norm2_w,
             w_gate=w_gate[0], w_up=w_up[0], w_down=w_down[0], final_norm_w=final_norm_w.reshape(1, D))
    m = dict(meta_tokens=m_meta_tokens, norm1_w=m_norm1_w, w_in=m_w_in[0], ssd_conv_w=m_ssd_conv_w[0], ssd_conv_b=m_ssd_conv_b,
             ssd_dt_bias=m_ssd_dt_bias, ssd_a_log=m_ssd_a_log, ssd_d=m_ssd_d, ssd_norm_w=m_ssd_norm_w, lru_conv_w=m_lru_conv_w[0],
             lru_conv_b=m_lru_conv_b, lru_wa=m_lru_wa[0], lru_ba=m_lru_ba, lru_wx=m_lru_wx[0], lru_bx=m_lru_bx, lru_lambda=m_lru_lambda,
             lru_norm_w=m_lru_norm_w, w_out=m_w_out[0], norm2_w=m_norm2_w, w_gate=m_w_gate[0], w_up=m_w_up[0], w_down=m_w_down[0],
             final_norm_w=m_final_norm_w.reshape(1, D))
    v = dict(meta_tokens=v_meta_tokens, norm1_w=v_norm1_w, w_in=v_w_in[0], ssd_conv_w=v_ssd_conv_w[0], ssd_conv_b=v_ssd_conv_b,
             ssd_dt_bias=v_ssd_dt_bias, ssd_a_log=v_ssd_a_log, ssd_d=v_ssd_d, ssd_norm_w=v_ssd_norm_w, lru_conv_w=v_lru_conv_w[0],
             lru_conv_b=v_lru_conv_b, lru_wa=v_lru_wa[0], lru_ba=v_lru_ba, lru_wx=v_lru_wx[0], lru_bx=v_lru_bx, lru_lambda=v_lru_lambda,
             lru_norm_w=v_lru_norm_w, w_out=v_w_out[0], norm2_w=v_norm2_w, w_gate=v_w_gate[0], w_up=v_w_up[0], w_down=v_w_down[0],
             final_norm_w=v_final_norm_w.reshape(1, D))
    shapes = dict(meta_tokens=meta_tokens.shape, norm1_w=norm1_w.shape, w_in=w_in.shape, ssd_conv_w=ssd_conv_w.shape,
                  ssd_conv_b=ssd_conv_b.shape, ssd_dt_bias=ssd_dt_bias.shape, ssd_a_log=ssd_a_log.shape, ssd_d=ssd_d.shape,
                  ssd_norm_w=ssd_norm_w.shape, lru_conv_w=lru_conv_w.shape, lru_conv_b=lru_conv_b.shape, lru_wa=lru_wa.shape,
                  lru_ba=lru_ba.shape, lru_wx=lru_wx.shape, lru_bx=lru_bx.shape, lru_lambda=lru_lambda.shape, lru_norm_w=lru_norm_w.shape,
                  w_out=w_out.shape, norm2_w=norm2_w.shape, w_gate=w_gate.shape, w_up=w_up.shape, w_down=w_down.shape,
                  final_norm_w=final_norm_w.shape)
    me = _index(*_place())
    for n in COLUMN_SHARDED:
        w[n], m[n], v[n] = w[n].T, m[n].T, v[n].T

    small_shard = jnp.concatenate([w["meta_tokens"], _pad_cols(w["ssd_conv_w"], 256).reshape(8, 128), w["lru_conv_w"],
                                   jnp.zeros((4, 128), F32)], axis=0)
    g_in, gs = all_gather("gather_w_in", [w["w_in"].astype(BF), small_shard])
    later = ["w_out", "w_gate", "w_up", "w_down"]
    started, behind = copies_start("gather_rest_start", [w[n].astype(BF) for n in later], False, gs)
    started = dict(zip(later, started))
    meta_full = gs[:, 0:16].transpose(1, 0, 2).reshape(N_META, D)
    ssd_cw = gs[:, 16:24].reshape(8, 4, 256)[:, :, :192].transpose(1, 0, 2).reshape(4, XBC)
    lru_cw = gs[:, 24:28].transpose(1, 0, 2).reshape(4, LRU_W)

    def fetch(names, after):
        got = copies_wait("gather_" + names[0] + "_wait", [started[n] for n in names], False, after)
        return [_unslab(g) for g in got]

    in_flight = {}

    def send(grads):
        names = list(grads)
        st, token = copies_start("grads_" + names[0] + "_start", [grads[n] if n == "small" else _slabs(grads[n]) for n in names], True, None)
        in_flight.update(zip(names, st))
        return token

    loss, grad_x, small = local_step(x[0], loss_target[0], meta_full, ssd_cw, lru_cw, _unslab(g_in), fetch, send,
                                     {**w, "norm1_w": w["norm1_w"] + behind})
    send({"small": _pack_small(small, loss).reshape(8, SM_ROWS // 8, 1024)})

    out = {}
    early = ["w_down", "w_gate", "w_up", "w_out"]
    recv = dict(zip(early, copies_wait("grads_early_wait", [in_flight[n] for n in early], True, in_flight["small"][2])))
    for n in early:
        out[n] = adamw_shard("adamw_" + n, recv[n], w[n], m[n], v[n], *BIG_TILE[n])
    recv_in, recv_small = copies_wait("grads_late_wait", [in_flight["w_in"], in_flight["small"]], True, [out[n][0] for n in early])
    untiled = (IN_COLS, 128)
    out["w_in"] = [o.reshape(IN_COLS // 8, D)
                   for o in adamw_w_in(recv_in, w["w_in"].reshape(untiled), m["w_in"].reshape(untiled), v["w_in"].reshape(untiled))]
    for n in COLUMN_SHARDED:
        out[n] = [o.T for o in out[n]]
    sm = all_gather("gather_small_grads", [sum_slabs(recv_small)])[0].reshape(SM_ROWS, 1024)
    special_g =[sm[SM_WA:SM_WA + 64].reshape(16, 64, 64), sm[SM_WX:SM_WX + 64].reshape(16, 64, 64),
                 lax.dynamic_slice(sm[SM_META:SM_META + 16], (0, 128 * me), (16, 128)),
                 lax.dynamic_slice(sm[SM_SCW:SM_SCW + 8].reshape(4, 2048), (0, 192 * me), (4, 192)),
                 lax.dynamic_slice(sm[SM_LCW:SM_LCW + 4], (0, 128 * me), (4, 128))]
    names = [n for n, _ in SIMPLE] + SPECIAL
    res = adamw_small(sm, special_g, [w[n] for n in names], [m[n] for n in names], [v[n] for n in names])
    for k, (n, _) in enumerate(SIMPLE):
        out[n] = res[4 * k:4 * k + 4]
    for k, n in enumerate(SPECIAL):
        o = 4 * len(SIMPLE) + 3 * k
        out[n] = [special_g[k]] + list(res[o:o + 3])
    loss_total = sm[SM_LOSS, 0]
    flat = [loss_total, grad_x[None]]
    for k in range(4):
        flat += [out[n][k].reshape(shapes[n]) for n in WEIGHTS]
    return tuple(flat)
```

```python
import math

import jax
import jax.numpy as jnp
from jax import lax
from jax.experimental import pallas as pl
from jax.experimental.pallas import tpu as pltpu

F32 = jnp.float32
BF = jnp.bfloat16

D = 1024
SEQ = 2048
N_META = 16
Q = 128
NPAD = 112
T = NPAD + N_META + SEQ
NCH = T // Q
RC = 544
D_FF = 2816
SSD_W = 1024
LRU_W = 1024
XBC = 1536
IN_COLS = 4624
PZ, PG, PXL, PXBC = 0, 1024, 2048, 3072
NP_IN = 4608
EPS = 1e-6
LRU_C = 8.0
VMEM_LIMIT = 56 * 1024 * 1024

ADAM_LR, ADAM_B1, ADAM_B2, ADAM_EPS, ADAM_WD, ADAM_STEP = 0.001, 0.9, 0.999, 1e-08, 0.01, 10

NT_DIMS = (((1,), (1,)), ((), ()))
TN_DIMS = (((0,), (0,)), ((), ()))
MESH = pl.DeviceIdType.MESH


def _params(n_grid=1, limit=VMEM_LIMIT):
    return pltpu.CompilerParams(dimension_semantics=("arbitrary",) * n_grid, vmem_limit_bytes=limit)


def _spec(shape, imap, single=False):
    if single:
        return pl.BlockSpec(shape, imap, pipeline_mode=pl.Buffered(1))
    return pl.BlockSpec(shape, imap)


def _sigmoid(x):
    return 0.5 * jnp.tanh(0.5 * x) + 0.5


def _sigmoid_gate(x):
    return 1.0 / (1.0 + jnp.exp(-x))


def _softplus(x):
    return jnp.maximum(x, 0.0) + jnp.log(1.0 + jnp.exp(-jnp.abs(x)))


def _rms_stats(h):
    return lax.rsqrt(jnp.mean(h * h, axis=-1, keepdims=True) + EPS)


def _rms(h, w):
    return (h * _rms_stats(h)) * w


def _rms_bwd(du, h, w):
    r = _rms_stats(h)
    n = h * r
    dn = du * w
    dh = r * (dn - n * jnp.mean(dn * n, axis=-1, keepdims=True))
    return dh, du * n


_G0 = math.sqrt(2.0 / math.pi)


def _gelu(x):
    return 0.5 * x * (1.0 + jnp.tanh(_G0 * (x + 0.044715 * (x * x * x))))


def _gelu_grad(x):
    t = jnp.tanh(_G0 * (x + 0.044715 * (x * x * x)))
    return 0.5 * (1.0 + t) + 0.5 * x * (1.0 - t * t) * (_G0 * (1.0 + 3.0 * 0.044715 * (x * x)))


def _rows(shape, r0=0):
    return lax.broadcasted_iota(jnp.int32, shape, 0) + r0


def _lanes(shape):
    return lax.broadcasted_iota(jnp.int32, shape, 1)


HALO = 8


def _fill_padded(pad_ref, x_ref):
    pad_ref[0:HALO, :] = jnp.zeros((HALO, pad_ref.shape[1]), F32)
    pad_ref[T + HALO:T + 2 * HALO, :] = jnp.zeros((HALO, pad_ref.shape[1]), F32)

    def step(c, carry):
        r0 = pl.multiple_of(c * Q, Q)
        pad_ref[pl.ds(r0 + HALO, Q), :] = x_ref[pl.ds(r0, Q), :].astype(F32)
        return carry

    lax.fori_loop(0, NCH, step, 0)


def _back(pad_ref, r0):
    win = pad_ref[pl.ds(r0, Q + HALO), :]
    return lambda s: win[HALO:, :] if s == 0 else pltpu.roll(win, s, axis=0)[HALO:, :]


def _ahead(pad_ref, r0):
    win = pad_ref[pl.ds(r0 + HALO, Q + HALO), :]
    return lambda s: win[:Q, :] if s == 0 else pltpu.roll(win, Q + HALO - s, axis=0)[:Q, :]


def _conv(back, w, b):
    y = b + w[3:4, :] * back(0)
    for k in range(3):
        y = y + w[k:k + 1, :] * back(3 - k)
    return y


def _conv_bwd_x(ahead, w):
    dx = w[3:4, :] * ahead(0)
    for k in range(3):
        dx = dx + w[k:k + 1, :] * ahead(3 - k)
    return dx


def _conv_bwd_w(dy, back):
    dws = [jnp.sum(dy * back(3 - k), axis=0, keepdims=True) for k in range(4)]
    return jnp.concatenate(dws, axis=0), jnp.sum(dy, axis=0, keepdims=True)


def _chunks(fn, unrolled=False):
    if unrolled:
        for c in range(NCH):
            fn(c * Q)
        return

    def step(c, carry):
        fn(pl.multiple_of(c * Q, Q))
        return carry

    lax.fori_loop(0, NCH, step, 0)


HALF = RC // 2


def _col_tiles(n, tn, fn):
    def step(j, carry):
        fn(pl.multiple_of(j * tn, tn))
        return carry

    lax.fori_loop(0, n // tn, step, 0)


def _rows_spec(cols, block_col=0):
    return _spec((RC, cols), lambda i: (i, block_col))


def _whole(shape):
    return _spec(shape, lambda i: tuple(0 for _ in shape), single=True)


def _vec(cols):
    return _spec((1, cols), lambda i: (0, 0))


def _zero_at_first(*refs):
    @pl.when(pl.program_id(0) == 0)
    def _():
        for r in refs:
            r[...] = jnp.zeros_like(r)


IN_RUNS = ((PZ, 0, 1024), (PG, 2576, 2048), (PXBC, 1024, XBC))


def _in_tiles(fn):
    for pcol, wrow, width in IN_RUNS:
        def step(j, carry, pcol=pcol, wrow=wrow):
            fn(pl.multiple_of(pcol + j * 512, 512), pl.multiple_of(wrow + j * 512, 16))
            return carry

        lax.fori_loop(0, width // 512, step, 0)


def in_proj(h0, wn, w_t, w_dt):
    def body(h_ref, wn_ref, w_ref, wdt_ref, o_ref, dt_ref, u_ref):
        for r in (0, HALF):
            u_ref[r:r + HALF, :] = _rms(h_ref[r:r + HALF, :], wn_ref[...]).astype(BF)

        def tile(pcol, wrow):
            o_ref[:, pl.ds(pcol, 512)] = lax.dot_general(u_ref[...], w_ref[pl.ds(wrow, 512), :], NT_DIMS, preferred_element_type=F32).astype(BF)

        _in_tiles(tile)
        dt_ref[...] = lax.dot_general(u_ref[...], wdt_ref[...], NT_DIMS, preferred_element_type=F32)

    return pl.pallas_call(
        body, grid=(T // RC,), in_specs=[_rows_spec(D), _vec(D), _whole((IN_COLS, D)), _whole((256, D))],
        out_specs=[_rows_spec(NP_IN), _rows_spec(256), _rows_spec(D)],
        out_shape=[jax.ShapeDtypeStruct((T, NP_IN), BF), jax.ShapeDtypeStruct((T, 256), F32), jax.ShapeDtypeStruct((T, D), BF)],
        compiler_params=_params(), name="in_proj")(h0, wn, w_t, w_dt)


def out_proj(yn_ssd, proj, hseq, lru_nw, w_out, h0):
    def body(y_ref, g_ref, h_ref, wn_ref, w_ref, r_ref, o_ref, cat_ref):
        cat_ref[:, 0:SSD_W] = y_ref[...]
        for r in (0, HALF):
            y = _gelu(g_ref[r:r + HALF, :].astype(F32)) * h_ref[r:r + HALF, :]
            cat_ref[r:r + HALF, SSD_W:] = _rms(y, wn_ref[...]).astype(BF)

        def tile(c0):
            o_ref[:, pl.ds(c0, 512)] = r_ref[:, pl.ds(c0, 512)] + jnp.dot(cat_ref[...], w_ref[:, pl.ds(c0, 512)], preferred_element_type=F32)

        _col_tiles(D, 512, tile)

    return pl.pallas_call(
        body, grid=(T // RC,),
        in_specs=[_rows_spec(SSD_W), _rows_spec(LRU_W, PG // LRU_W), _rows_spec(LRU_W), _vec(LRU_W), _whole((SSD_W + LRU_W, D)), _rows_spec(D)],
        out_specs=[_rows_spec(D), _rows_spec(SSD_W + LRU_W)],
        out_shape=[jax.ShapeDtypeStruct((T, D), F32), jax.ShapeDtypeStruct((T, SSD_W + LRU_W), BF)],
        compiler_params=_params(), name="out_proj")(yn_ssd, proj, hseq, lru_nw, w_out, h0)


def out_proj_bwd(dh1_b, w_out, proj, hseq, lru_nw):
    def body(d_ref, w_ref, g_ref, h_ref, wn_ref, dy_ref, dh_ref, dg_ref, dw_ref, dl_scr):
        _zero_at_first(dw_ref)

        def tile(c0):
            dy_ref[:, pl.ds(c0, 512)] = lax.dot_general(d_ref[...], w_ref[pl.ds(c0, 512), :], NT_DIMS, preferred_element_type=F32)
            dl_scr[:, pl.ds(c0, 512)] = lax.dot_general(d_ref[...], w_ref[pl.ds(SSD_W + c0, 512), :], NT_DIMS, preferred_element_type=F32)

        _col_tiles(SSD_W, 512, tile)
        for r in (0, HALF):
            g = g_ref[r:r + HALF, :].astype(F32)
            h = h_ref[r:r + HALF, :]
            ge = _gelu(g)
            dy, dw = _rms_bwd(dl_scr[r:r + HALF, :], ge * h, wn_ref[...])
            dw_ref[...] += jnp.sum(dw, axis=0, keepdims=True)
            dh_ref[r:r + HALF, :] = dy * ge
            dg_ref[r:r + HALF, :] = (dy * h * _gelu_grad(g)).astype(BF)

    return pl.pallas_call(
        body, grid=(T // RC,),
        in_specs=[_rows_spec(D), _whole((SSD_W + LRU_W, D)), _rows_spec(LRU_W, PG // LRU_W), _rows_spec(LRU_W), _vec(LRU_W)],
        out_specs=[_rows_spec(SSD_W), _rows_spec(LRU_W), _rows_spec(LRU_W), _vec(LRU_W)],
        out_shape=[jax.ShapeDtypeStruct((T, SSD_W), F32), jax.ShapeDtypeStruct((T, LRU_W), F32), jax.ShapeDtypeStruct((T, LRU_W), BF),
                   jax.ShapeDtypeStruct((1, LRU_W), F32)],
        scratch_shapes=[pltpu.VMEM((RC, LRU_W), F32)],
        compiler_params=_params(), name="out_proj_bwd")(dh1_b, w_out, proj, hseq, lru_nw)


def in_proj_bwd(dz, dg, dxl, dxbc, ddt, w_t, w_dt, h0, wn, dh1):
    first = NPAD + N_META

    def body(dz_ref, dg_ref, dxl_ref, dxbc_ref, ddt_ref, w_ref, wdt_ref, h_ref, wn_ref, r_ref, gx_hbm, meta_ref, dw_ref, du_scr, o_ref, sem):
        i = pl.program_id(0)
        _zero_at_first(dw_ref)
        du_scr[...] = jnp.dot(ddt_ref[...], wdt_ref[...], preferred_element_type=F32)
        for d_ref, wrow, width in ((dz_ref, 0, 1024), (dxbc_ref, 1024, XBC), (dg_ref, 2576, 1024), (dxl_ref, 3600, 1024)):
            def step(j, carry, d_ref=d_ref, wrow=wrow):
                c0 = pl.multiple_of(j * 512, 512)
                du_scr[...] += jnp.dot(d_ref[:, pl.ds(c0, 512)], w_ref[pl.ds(pl.multiple_of(wrow + c0, 16), 512), :], preferred_element_type=F32)
                return carry

            lax.fori_loop(0, width // 512, step, 0)
        for r in (0, HALF):
            dh, dw = _rms_bwd(du_scr[r:r + HALF, :], h_ref[r:r + HALF, :], wn_ref[...])
            dw_ref[...] += jnp.sum(dw, axis=0, keepdims=True)
            o_ref[r:r + HALF, :] = dh + r_ref[r:r + HALF, :]

        @pl.when(i == 0)
        def _():
            meta_ref[...] = o_ref[NPAD:first, :]
            head = pltpu.make_async_copy(o_ref.at[pl.ds(first, RC - first)], gx_hbm.at[pl.ds(0, RC - first)], sem)
            head.start()
            head.wait()

        @pl.when(i > 0)
        def _():
            rest = pltpu.make_async_copy(o_ref, gx_hbm.at[pl.ds(pl.multiple_of(i * RC - first, 32), RC)], sem)
            rest.start()
            rest.wait()

    return pl.pallas_call(
        body, grid=(T // RC,),
        in_specs=[_rows_spec(SSD_W), _rows_spec(LRU_W), _rows_spec(LRU_W), _rows_spec(XBC), _rows_spec(256), _whole((IN_COLS, D)),
                  _whole((256, D)), _rows_spec(D), _vec(D), _rows_spec(D)],
        out_specs=[pl.BlockSpec(memory_space=pl.ANY), _spec((N_META, D), lambda i: (0, 0)), _vec(D)],
        out_shape=[jax.ShapeDtypeStruct((SEQ, D), F32), jax.ShapeDtypeStruct((N_META, D), F32), jax.ShapeDtypeStruct((1, D), F32)],
        scratch_shapes=[pltpu.VMEM((RC, D), F32), pltpu.VMEM((RC, D), F32), pltpu.SemaphoreType.DMA],
        compiler_params=_params(), name="in_proj_bwd")(dz, dg, dxl, dxbc, ddt, w_t, w_dt, h0, wn, dh1)


def in_proj_wgrad(parts, u1):
    tm = 256
    tiles = [p.shape[1] // tm for p in parts]
    starts = [sum(tiles[:k]) for k in range(len(parts))]

    def body(*refs):
        a_refs, u_ref, o_ref = refs[:len(parts)], refs[len(parts)], refs[len(parts) + 1]
        step = pl.program_id(0)
        for a_ref, start, n in zip(a_refs, starts, tiles):
            @pl.when((step >= start) & (step < start + n))
            def _(a_ref=a_ref):
                o_ref[...] = lax.dot_general(a_ref[...], u_ref[...], TN_DIMS, preferred_element_type=F32).astype(BF)

    def tile_of(start, n):
        return lambda j: (0, jnp.clip(j - start, 0, n - 1))

    return pl.pallas_call(
        body, grid=(sum(tiles),),
        in_specs=[_spec((T, tm), tile_of(s, n)) for s, n in zip(starts, tiles)] + [_spec((T, D), lambda j: (0, 0), single=True)],
        out_specs=_spec((tm, D), lambda j: (j, 0)),
        out_shape=jax.ShapeDtypeStruct((tm * sum(tiles), D), BF),
        compiler_params=_params(), name="in_proj_wgrad")(*parts, u1)


def matmul_tn(name, a, b, tm, tn):
    m, n = a.shape[1], b.shape[1]

    def body(a_ref, b_ref, o_ref, acc_ref):
        acc_ref[...] = jnp.zeros_like(acc_ref)

        def mm(r0):
            acc_ref[...] += lax.dot_general(a_ref[pl.ds(r0, RC), :], b_ref[pl.ds(r0, RC), :], TN_DIMS, preferred_element_type=F32)

        _col_tiles(T, RC, mm)
        o_ref[...] = acc_ref[...].astype(BF)

    return pl.pallas_call(
        body, grid=(m // tm, n // tn),
        in_specs=[_spec((T, tm), lambda i, j: (0, i)), _spec((T, tn), lambda i, j: (0, j))],
        out_specs=_spec((tm, tn), lambda i, j: (i, j)),
        out_shape=jax.ShapeDtypeStruct((m, n), BF),
        scratch_shapes=[pltpu.VMEM((tm, tn), F32)],
        compiler_params=_params(2), name=name)(a, b)


def conv_silu_fwd(proj, cw, cb):
    def body(x_ref, w_ref, b_ref, o_ref, xpad):
        _fill_padded(xpad, x_ref)

        def chunk(r0):
            pre = _conv(_back(xpad, r0), w_ref[...], b_ref[...])
            o_ref[pl.ds(r0, Q), :] = pre * _sigmoid(pre)

        _chunks(chunk)

    c0 = PXBC // 128
    return pl.pallas_call(
        body, grid=(XBC // 128,),
        in_specs=[_spec((T, 128), lambda c: (0, c0 + c)), _spec((4, 128), lambda c: (0, c)), _spec((1, 128), lambda c: (0, c))],
        out_specs=_spec((T, 128), lambda c: (0, c)),
        out_shape=jax.ShapeDtypeStruct((T, XBC), F32), scratch_shapes=[pltpu.VMEM((T + 2 * HALO, 128), F32)],
        compiler_params=_params(), name="conv_silu_fwd")(proj, cw, cb)


def conv_silu_bwd(dx, d_b, d_c, proj, cw, cb):
    def body(dx_ref, db_ref, dc_ref, x_ref, w_ref, b_ref, o_ref, dw_ref, dbias_ref, xpad, dpad):
        tile = pl.program_id(0)
        _fill_padded(xpad, x_ref)
        dpad[0:HALO, :] = jnp.zeros((HALO, 128), F32)
        dpad[T + HALO:T + 2 * HALO, :] = jnp.zeros((HALO, 128), F32)
        dw_ref[...] = jnp.zeros_like(dw_ref)
        dbias_ref[...] = jnp.zeros_like(dbias_ref)

        def first(r0):
            back = _back(xpad, r0)
            pre = _conv(back, w_ref[...], b_ref[...])
            sg = _sigmoid(pre)
            rows = pl.ds(r0, Q)
            d = jnp.where(tile < 8, dx_ref[rows, :], jnp.where(tile < 10, db_ref[rows, :], dc_ref[rows, :]))
            dpre = d * (sg * (1.0 + pre * (1.0 - sg)))
            dpad[pl.ds(r0 + HALO, Q), :] = dpre
            dw, dbias = _conv_bwd_w(dpre, back)
            dw_ref[...] += dw
            dbias_ref[...] += dbias

        _chunks(first)

        def second(r0):
            o_ref[pl.ds(r0, Q), :] = _conv_bwd_x(_ahead(dpad, r0), w_ref[...]).astype(BF)

        _chunks(second)

    c0 = PXBC // 128
    pad = pltpu.VMEM((T + 2 * HALO, 128), F32)
    return pl.pallas_call(
        body, grid=(XBC // 128,),
        in_specs=[_spec((T, 128), lambda c: (0, jnp.minimum(c, 7))), _spec((T, 128), lambda c: (0, jnp.clip(c - 8, 0, 1))),
                  _spec((T, 128), lambda c: (0, jnp.clip(c - 10, 0, 1))), _spec((T, 128), lambda c: (0, c0 + c)),
                  _spec((4, 128), lambda c: (0, c)), _spec((1, 128), lambda c: (0, c))],
        out_specs=[_spec((T, 128), lambda c: (0, c)), _spec((4, 128), lambda c: (0, c)), _spec((1, 128), lambda c: (0, c))],
        out_shape=[jax.ShapeDtypeStruct((T, XBC), BF), jax.ShapeDtypeStruct((4, XBC), F32), jax.ShapeDtypeStruct((1, XBC), F32)],
        scratch_shapes=[pad, pad], compiler_params=_params(), name="conv_silu_bwd")(dx, d_b, d_c, proj, cw, cb)


def _ssd_chunk_common(row0, dt_ref, b_ref, c_ref, bias, a_neg):
    shape = (Q, Q)
    lane = _lanes(shape)
    sub = _rows(shape)
    live = (_rows(shape, row0) >= NPAD) & (lane < 8)
    dtr = dt_ref[:, :]
    dt = jnp.where(live, _softplus(dtr + bias), 0.0)
    d_a = dt * a_neg
    tri = (sub >= lane).astype(F32)
    cs = jnp.dot(tri, d_a, precision=lax.Precision.HIGHEST, preferred_element_type=F32)
    cs_t = cs.T
    b_f = b_ref[:, :]
    bc = b_f.astype(BF)
    cc = c_ref[:, :].astype(BF)
    cb = lax.dot_general(cc, bc, NT_DIMS, preferred_element_type=F32)
    cs_last = cs[Q - 1:Q, :]
    return dict(lane=lane, sub=sub, live=live, dtr=dtr, dt=dt, cs=cs, cs_t=cs_t, bc=bc, cc=cc, cb=cb, bc_t=b_f.T.astype(BF),
                ecs=jnp.exp(cs), dsm=jnp.exp(cs_last - cs), gam=jnp.exp(cs_last))


def _pair(lane_even, mat, j):
    return jnp.where(lane_even, mat[:, j:j + 1], mat[:, j + 1:j + 2])


def _pair_row(lane_even, mat, j):
    return jnp.where(lane_even[0:1, :], mat[:, j:j + 1], mat[:, j + 1:j + 2])


def _head_decay(cm, j):
    seg = cm["cs"][:, j:j + 1] - cm["cs_t"][j:j + 1, :]
    return jnp.exp(jnp.where(cm["sub"] >= cm["lane"], seg, -jnp.inf))


def _head_decay_t(cm, j):
    seg = cm["cs_t"][j:j + 1, :] - cm["cs"][:, j:j + 1]
    return jnp.exp(jnp.where(cm["lane"] >= cm["sub"], seg, -jnp.inf))


def ssd_fwd(xbc_act, proj, dt_raw, dt_bias2, a_log2, d2, norm_w):
    def body(x_all, b_all, c_all, dt_all, z_all, bias_all, alog_all, d_all, nw_all, yn_all, y_all, hp_all, h_all):
        @pl.when(pl.program_id(0) == 0)
        def _():
            h_all[...] = jnp.zeros_like(h_all)

        for g in range(2):
            wide, thin = slice(512 * g, 512 * g + 512), slice(128 * g, 128 * g + 128)
            group(x_all.at[:, wide], b_all.at[:, thin], c_all.at[:, thin], dt_all.at[:, thin], z_all.at[:, wide], bias_all.at[g],
                  alog_all.at[g], d_all.at[g], nw_all.at[:, wide], yn_all.at[:, wide], y_all.at[:, wide], hp_all.at[g, 0], h_all.at[g])

    def group(x_ref, b_ref, c_ref, dt_ref, z_ref, bias_ref, alog_ref, d_ref, nw_ref, yn_ref, y_ref, hp_ref, h_scr):
        bias = bias_ref[...]
        a_neg = -jnp.exp(alog_ref[...])
        dsk = d_ref[...]
        cm = _ssd_chunk_common(pl.program_id(0) * Q, dt_ref, b_ref, c_ref, bias, a_neg)
        lane_even = cm["lane"] < 64
        for p in range(4):
            je, jo = 2 * p, 2 * p + 1
            xp = x_ref[:, 128 * p:128 * p + 128]
            xdt = xp * _pair(lane_even, cm["dt"], je)
            xdt_b = xdt.astype(BF)
            m_e = (cm["cb"] * _head_decay(cm, je)).astype(BF)
            m_o = (cm["cb"] * _head_decay(cm, jo)).astype(BF)
            zero = jnp.zeros_like(xdt_b)
            yd = (jnp.dot(m_e, jnp.where(lane_even, xdt_b, zero), preferred_element_type=F32)
                  + jnp.dot(m_o, jnp.where(lane_even, zero, xdt_b), preferred_element_type=F32))
            hp = h_scr[p]
            hp_ref[p] = hp
            yo = jnp.dot(cm["cc"], hp.astype(BF), preferred_element_type=F32) * _pair(lane_even, cm["ecs"], je)
            y_ref[:, 128 * p:128 * p + 128] = yd + yo + xp * _pair_row(lane_even, dsk, je)
            st = jnp.dot(cm["bc_t"], (xdt * _pair(lane_even, cm["dsm"], je)).astype(BF), preferred_element_type=F32)
            h_scr[p] = hp * _pair_row(lane_even, cm["gam"], je) + st
        zc = z_ref[:, :].astype(F32)
        gated = y_ref[:, :] * (zc * _sigmoid(zc))
        yn_ref[:, :] = _rms(gated, nw_ref[...]).astype(BF)

    par = _spec((2, 1, 128), lambda c: (0, 0, 0))
    wide = _spec((Q, SSD_W), lambda c: (c, 0))
    return pl.pallas_call(
        body, grid=(NCH,),
        in_specs=[wide, _spec((Q, 256), lambda c: (c, 4)), _spec((Q, 256), lambda c: (c, 5)), _spec((Q, 256), lambda c: (c, 0)),
                  wide, par, par, par, _spec((1, SSD_W), lambda c: (0, 0))],
        out_specs=[wide, wide, _spec((2, 1, 4, 128, 128), lambda c: (0, c, 0, 0, 0))],
        out_shape=[jax.ShapeDtypeStruct((T, SSD_W), BF), jax.ShapeDtypeStruct((T, SSD_W), F32),
                   jax.ShapeDtypeStruct((2, NCH, 4, 128, 128), F32)],
        scratch_shapes=[pltpu.VMEM((2, 4, 128, 128), F32)],
        compiler_params=_params(), name="ssd_fwd")(xbc_act, xbc_act, xbc_act, dt_raw, proj, dt_bias2, a_log2, d2, norm_w)


def ssd_bwd(dyn, xbc_act, proj, dt_raw, y_pre, h_prev, dt_bias2, a_log2, d2, norm_w):
    def body(dyn_all, x_all, b_all, c_all, dt_all, z_all, y_all, hp_all, bias_all, alog_all, d_all, nw_all,
             dz_all, dx_all, db_all, dc_all, ddt_all, dpar_all, dnw_all, dh_all, acc_all):
        @pl.when(pl.program_id(0) == 0)
        def _():
            dh_all[...] = jnp.zeros_like(dh_all)
            acc_all[...] = jnp.zeros_like(acc_all)
            dnw_all[...] = jnp.zeros_like(dnw_all)

        for g in range(2):
            wide, thin = slice(512 * g, 512 * g + 512), slice(128 * g, 128 * g + 128)
            group(dyn_all.at[:, wide], x_all.at[:, wide], b_all.at[:, thin], c_all.at[:, thin], dt_all.at[:, thin], z_all.at[:, wide],
                  y_all.at[:, wide], hp_all.at[g, 0], bias_all.at[g], alog_all.at[g], d_all.at[g], nw_all.at[:, wide],
                  dz_all.at[:, wide], dx_all.at[:, wide], db_all.at[:, thin], dc_all.at[:, thin], ddt_all.at[:, thin], dpar_all.at[g],
                  dnw_all.at[:, wide], dh_all.at[g], acc_all.at[g])

    def group(dyn_ref, x_ref, b_ref, c_ref, dt_ref, z_ref, y_ref, hp_ref, bias_ref, alog_ref, d_ref, nw_ref,
              dz_ref, dx_ref, db_ref, dc_ref, ddt_ref, dpar_ref, dnw_ref, dh_scr, acc_scr):
        ci = pl.program_id(0)
        bias = bias_ref[...]
        a_neg = -jnp.exp(alog_ref[...])
        dsk = d_ref[...]
        cm = _ssd_chunk_common((NCH - 1 - ci) * Q, dt_ref, b_ref, c_ref, bias, a_neg)
        lane, sub = cm["lane"], cm["sub"]
        lane_even = lane < 64
        cc_t = c_ref[:, :].T.astype(BF)
        cb_t = lax.dot_general(cm["bc"], cm["cc"], NT_DIMS, preferred_element_type=F32)
        zc = z_ref[:, :].astype(F32)
        yc = y_ref[:, :]
        sg = _sigmoid(zc)
        sz = zc * sg
        dgated, dnw = _rms_bwd(dyn_ref[:, :], yc * sz, nw_ref[...])
        dnw_ref[...] += jnp.sum(dnw, axis=0, keepdims=True)
        dz_ref[:, :] = (dgated * yc * (sg * (1.0 + zc * (1.0 - sg)))).astype(BF)
        dy_all = dgated * sz
        dcb = jnp.zeros((Q, Q), F32)
        dcb_t = jnp.zeros((Q, Q), F32)
        db_acc = jnp.zeros((Q, Q), F32)
        dc_acc = jnp.zeros((Q, Q), F32)
        dcs = jnp.zeros((Q, Q), F32)
        ddt = jnp.zeros((Q, Q), F32)
        for p in range(4):
            je, jo = 2 * p, 2 * p + 1
            xp = x_ref[:, 128 * p:128 * p + 128]
            dy = dy_all[:, 128 * p:128 * p + 128]
            dt_p = _pair(lane_even, cm["dt"], je)
            xdt = xp * dt_p
            xdt_b = xdt.astype(BF)
            dy_b = dy.astype(BF)
            zero = jnp.zeros_like(dy_b)
            hp = hp_ref[p]
            hp_b = hp.astype(BF)
            dh = dh_scr[p]
            dh_b = dh.astype(BF)
            acc_scr[p:p + 1, :] += jnp.sum(dy * xp, axis=0, keepdims=True)
            dxp = dy * _pair_row(lane_even, dsk, je)
            e_p = _pair(lane_even, cm["ecs"], je)
            g_p = jnp.dot(cm["cc"], hp_b, preferred_element_type=F32)
            dg_b = (dy * e_p).astype(BF)
            de = dy * g_p * e_p
            dc_acc = dc_acc + lax.dot_general(dg_b, hp_b, NT_DIMS, preferred_element_type=F32)
            dh_in = jnp.dot(cc_t, dg_b, preferred_element_type=F32)
            ds_p = _pair(lane_even, cm["dsm"], je)
            r_p = jnp.dot(cm["bc"], dh_b, preferred_element_type=F32)
            dxdt = r_p * ds_p
            tt = r_p * xdt * ds_p
            db_acc = db_acc + lax.dot_general((xdt * ds_p).astype(BF), dh_b, NT_DIMS, preferred_element_type=F32)
            dgam_m = jnp.sum(dh * hp, axis=0, keepdims=True)
            for j, even in ((je, True), (jo, False)):
                sel = lane_even if even else jnp.logical_not(lane_even)
                dy_j = jnp.where(sel, dy_b, zero)
                l_j = _head_decay(cm, j)
                l_jt = _head_decay_t(cm, j)
                m_j = cm["cb"] * l_j
                m_jt = cb_t * l_jt
                dm = lax.dot_general(dy_j, xdt_b, NT_DIMS, preferred_element_type=F32)
                dm_t = lax.dot_general(xdt_b, dy_j, NT_DIMS, preferred_element_type=F32)
                dxdt = dxdt + jnp.dot(m_jt.astype(BF), dy_j, preferred_element_type=F32)
                dcb = dcb + dm * l_j
                dcb_t = dcb_t + dm_t * l_jt
                t_j = jnp.where(sel, tt, 0.0)
                col = jnp.sum(dm * m_j - dm_t * m_jt + (jnp.where(sel, de, 0.0) - t_j), axis=1, keepdims=True)
                gam_j = cm["gam"][:, j:j + 1]
                last = (jnp.sum(jnp.sum(t_j, axis=0, keepdims=True), axis=1, keepdims=True)
                        + jnp.sum(jnp.where(sel[0:1, :], dgam_m, 0.0), axis=1, keepdims=True) * gam_j)
                col = col + jnp.where(sub[:, 0:1] == Q - 1, last, 0.0)
                dcs = dcs + jnp.where(lane == j, col, 0.0)
            dh_scr[p] = dh_in + dh * _pair_row(lane_even, cm["gam"], je)
            dx_ref[:, 128 * p:128 * p + 128] = dxp + dxdt * dt_p
            dd = dxdt * xp
            ddt = ddt + jnp.where(lane == je, jnp.sum(jnp.where(lane_even, dd, 0.0), axis=1, keepdims=True), 0.0)
            ddt = ddt + jnp.where(lane == jo, jnp.sum(jnp.where(lane_even, 0.0, dd), axis=1, keepdims=True), 0.0)
        dc_ref[:, :] = dc_acc + jnp.dot(dcb.astype(BF), cm["bc"], preferred_element_type=F32)
        db_ref[:, :] = db_acc + jnp.dot(dcb_t.astype(BF), cm["cc"], preferred_element_type=F32)
        tri_t = (sub <= lane).astype(F32)
        dd_a = jnp.dot(tri_t, dcs, precision=lax.Precision.HIGHEST, preferred_element_type=F32)
        ddt = ddt + dd_a * a_neg
        acc_scr[5:6, :] += jnp.sum(dd_a * cm["dt"], axis=0, keepdims=True)
        draw = jnp.where(cm["live"], ddt * _sigmoid_gate(cm["dtr"] + bias), 0.0)
        acc_scr[4:5, :] += jnp.sum(draw, axis=0, keepdims=True)
        ddt_ref[:, :] = draw.astype(BF)

        @pl.when(ci == NCH - 1)
        def _():
            lane1 = _lanes((1, 128))
            dd = jnp.zeros((1, 128), F32)
            for p in range(4):
                row = acc_scr[p:p + 1, :]
                dd = dd + jnp.where(lane1 == 2 * p, jnp.sum(jnp.where(lane1 < 64, row, 0.0), axis=1, keepdims=True), 0.0)
                dd = dd + jnp.where(lane1 == 2 * p + 1, jnp.sum(jnp.where(lane1 < 64, 0.0, row), axis=1, keepdims=True), 0.0)
            dpar_ref[...] = jnp.concatenate([acc_scr[4:5, :], acc_scr[5:6, :] * a_neg, dd, jnp.zeros((5, 128), F32)], axis=0)

    par = _spec((2, 1, 128), lambda c: (0, 0, 0))
    wide = _spec((Q, SSD_W), lambda c: (NCH - 1 - c, 0))
    thin = _spec((Q, 256), lambda c: (NCH - 1 - c, 0))
    vec = _spec((1, SSD_W), lambda c: (0, 0))
    return pl.pallas_call(
        body, grid=(NCH,),
        in_specs=[wide, wide, _spec((Q, 256), lambda c: (NCH - 1 - c, 4)), _spec((Q, 256), lambda c: (NCH - 1 - c, 5)),
                  _spec((Q, 256), lambda c: (NCH - 1 - c, 0)), wide, wide,
                  _spec((2, 1, 4, 128, 128), lambda c: (0, NCH - 1 - c, 0, 0, 0)), par, par, par, vec],
        out_specs=[wide, wide, thin, thin, thin, _spec((2, 8, 128), lambda c: (0, 0, 0)), vec],
        out_shape=[jax.ShapeDtypeStruct((T, SSD_W), BF), jax.ShapeDtypeStruct((T, SSD_W), F32), jax.ShapeDtypeStruct((T, 256), F32),
                   jax.ShapeDtypeStruct((T, 256), F32), jax.ShapeDtypeStruct((T, 256), BF), jax.ShapeDtypeStruct((2, 8, 128), F32),
                   jax.ShapeDtypeStruct((1, SSD_W), F32)],
        scratch_shapes=[pltpu.VMEM((2, 4, 128, 128), F32), pltpu.VMEM((2, 8, 128), F32)],
        compiler_params=_params(), name="ssd_bwd")(dyn, xbc_act, xbc_act, xbc_act, dt_raw, proj, y_pre, h_prev, dt_bias2, a_log2, d2, norm_w)


def _lru_gates(back, cw, cb, wa, ba, wx, bx, lam):
    xr = _conv(back, cw, cb)
    xr_b = xr.astype(BF)
    r = _sigmoid_gate(jnp.dot(xr_b, wa, preferred_element_type=F32) + ba)
    i = _sigmoid_gate(jnp.dot(xr_b, wx, preferred_element_type=F32) + bx)
    sp = _softplus(-lam)
    la = (-LRU_C) * r * sp
    a = jnp.exp(la)
    mult2 = -jnp.tanh(la) * (a * a + 1.0)
    return xr, xr_b, r, i, sp, a, jnp.sqrt(mult2), mult2


def lru_gates_fwd(proj, cw, cb, wa2, ba, wx2, bx, lam):
    def body(x_ref, cw_ref, cb_ref, wa_ref, ba_ref, wx_ref, bx_ref, lam_ref, a_ref, u_ref, xpad):
        _fill_padded(xpad, x_ref)

        def chunk(r0):
            xr, _, _, i, _, a, mult, _ = _lru_gates(_back(xpad, r0), cw_ref[...], cb_ref[...], wa_ref[0], ba_ref[...], wx_ref[0], bx_ref[...],
                                                 lam_ref[...])
            a_ref[pl.ds(r0, Q), :] = a
            u_ref[pl.ds(r0, Q), :] = jnp.where(_rows(a.shape, r0) >= NPAD, mult * (i * xr), 0.0)

        _chunks(chunk, unrolled=True)

    c0 = PXL // 128
    vec = _spec((1, 128), lambda c: (0, c))
    mat = _spec((1, 128, 128), lambda c: (c, 0, 0))
    return pl.pallas_call(
        body, grid=(8,),
        in_specs=[_spec((T, 128), lambda c: (0, c0 + c)), _spec((4, 128), lambda c: (0, c)), vec, mat, vec, mat, vec, vec],
        out_specs=[_spec((T, 128), lambda c: (0, c)), _spec((T, 128), lambda c: (0, c))],
        out_shape=[jax.ShapeDtypeStruct((T, LRU_W), F32), jax.ShapeDtypeStruct((T, LRU_W), F32)],
        scratch_shapes=[pltpu.VMEM((T + 2 * HALO, 128), F32)],
        compiler_params=_params(), name="lru_gates_fwd")(proj, cw, cb, wa2, ba, wx2, bx, lam)


def lru_scan_fwd(a, u):
    def body(a_ref, u_ref, h_ref):
        def step(i, h):
            base = pl.multiple_of(i * 8, 8)
            for k in range(8):
                h = a_ref[pl.ds(base + k, 1), :] * h + u_ref[pl.ds(base + k, 1), :]
                h_ref[pl.ds(base + k, 1), :] = h
            return h

        lax.fori_loop(0, T // 8, step, jnp.zeros((1, LRU_W), F32))

    return pl.pallas_call(body, out_shape=jax.ShapeDtypeStruct((T, LRU_W), F32), compiler_params=_params(0), name="lru_scan_fwd")(a, u)


def lru_scan_bwd(a, dh_out):
    def body(a_ref, d_ref, o_ref):
        def step(i, carry):
            base = pl.multiple_of(T - 8 - i * 8, 8)
            for k in range(7, -1, -1):
                carry = d_ref[pl.ds(base + k, 1), :] + carry
                o_ref[pl.ds(base + k, 1), :] = carry
                carry = carry * a_ref[pl.ds(base + k, 1), :]
            return carry

        lax.fori_loop(0, T // 8, step, jnp.zeros((1, LRU_W), F32))

    return pl.pallas_call(body, out_shape=jax.ShapeDtypeStruct((T, LRU_W), F32), compiler_params=_params(0), name="lru_scan_bwd")(a, dh_out)


def lru_gates_bwd(dhs, hseq, proj, cw, cb, wa2, ba, wx2, bx, lam):
    def body(dh_ref, h_ref, x_ref, cw_ref, cb_ref, wa_ref, ba_ref, wx_ref, bx_ref, lam_ref,
             dx_ref, dcw_ref, dcb_ref, dwa_ref, dba_ref, dwx_ref, dbx_ref, dlam_ref, xpad, hpad, dpad):
        _fill_padded(xpad, x_ref)
        _fill_padded(hpad, h_ref)
        dpad[0:HALO, :] = jnp.zeros((HALO, 128), F32)
        dpad[T + HALO:T + 2 * HALO, :] = jnp.zeros((HALO, 128), F32)
        for ref in (dcw_ref, dcb_ref, dwa_ref, dba_ref, dwx_ref, dbx_ref, dlam_ref):
            ref[...] = jnp.zeros_like(ref)
        lam = lam_ref[...]

        def first(r0):
            back = _back(xpad, r0)
            xr, xr_b, r, i, sp, a, mult, mult2 = _lru_gates(back, cw_ref[...], cb_ref[...], wa_ref[0], ba_ref[...], wx_ref[0], bx_ref[...], lam)
            dh = dh_ref[pl.ds(r0, Q), :]
            da = dh * _back(hpad, r0)(1)
            du = jnp.where(_rows(dh.shape, r0) >= NPAD, dh, 0.0)
            dmult = du * (i * xr)
            di = du * (mult * xr)
            dxr = du * (mult * i)
            dla = da * a - dmult * (a * a) * lax.rsqrt(mult2)
            dr = dla * ((-LRU_C) * sp)
            dlam_ref[...] += jnp.sum(dla * ((-LRU_C) * r), axis=0, keepdims=True)
            dpr = dr * r * (1.0 - r)
            dpi = di * i * (1.0 - i)
            dba_ref[...] += jnp.sum(dpr, axis=0, keepdims=True)
            dbx_ref[...] += jnp.sum(dpi, axis=0, keepdims=True)
            dpr_b = dpr.astype(BF)
            dpi_b = dpi.astype(BF)
            dxr = (dxr + lax.dot_general(dpr_b, wa_ref[0], NT_DIMS, preferred_element_type=F32)
                   + lax.dot_general(dpi_b, wx_ref[0], NT_DIMS, preferred_element_type=F32))
            dwa_ref[0] += lax.dot_general(xr_b, dpr_b, TN_DIMS, preferred_element_type=F32)
            dwx_ref[0] += lax.dot_general(xr_b, dpi_b, TN_DIMS, preferred_element_type=F32)
            dpad[pl.ds(r0 + HALO, Q), :] = dxr
            dcw, dcb = _conv_bwd_w(dxr, back)
            dcw_ref[...] += dcw
            dcb_ref[...] += dcb

        _chunks(first, unrolled=True)
        dlam_ref[...] = -dlam_ref[...] * _sigmoid_gate(-lam)

        def second(r0):
            dx_ref[pl.ds(r0, Q), :] = _conv_bwd_x(_ahead(dpad, r0), cw_ref[...]).astype(BF)

        _chunks(second)

    c0 = PXL // 128
    vec = _spec((1, 128), lambda c: (0, c))
    mat = _spec((1, 128, 128), lambda c: (c, 0, 0))
    col = _spec((T, 128), lambda c: (0, c))
    vshape = jax.ShapeDtypeStruct((1, LRU_W), F32)
    mshape = jax.ShapeDtypeStruct((8, 128, 128), F32)
    pad = pltpu.VMEM((T + 2 * HALO, 128), F32)
    return pl.pallas_call(
        body, grid=(8,),
        in_specs=[col, col, _spec((T, 128), lambda c: (0, c0 + c)), _spec((4, 128), lambda c: (0, c)), vec, mat, vec, mat, vec, vec],
        out_specs=[col, _spec((4, 128), lambda c: (0, c)), vec, mat, vec, mat, vec, vec],
        out_shape=[jax.ShapeDtypeStruct((T, LRU_W), BF), jax.ShapeDtypeStruct((4, LRU_W), F32), vshape, mshape, vshape, mshape, vshape, vshape],
        scratch_shapes=[pad, pad, pad], compiler_params=_params(), name="lru_gates_bwd")(dhs, hseq, proj, cw, cb, wa2, ba, wx2, bx, lam)


def gate_up(h1, wn, w_gate, w_up):
    def body(h_ref, wn_ref, wg_ref, wu_ref, gt_ref, up_ref, act_ref, u_ref):
        for r in (0, HALF):
            u_ref[r:r + HALF, :] = _rms(h_ref[r:r + HALF, :], wn_ref[...]).astype(BF)

        def tile(c0):
            cols = pl.ds(c0, 256)
            gt = lax.dot_general(u_ref[...], wg_ref[cols, :], NT_DIMS, preferred_element_type=F32)
            up = lax.dot_general(u_ref[...], wu_ref[cols, :], NT_DIMS, preferred_element_type=F32)
            gt_ref[:, cols] = gt.astype(BF)
            up_ref[:, cols] = up.astype(BF)
            act_ref[:, cols] = (gt * _sigmoid(gt) * up).astype(BF)

        _col_tiles(D_FF, 256, tile)

    big = jax.ShapeDtypeStruct((T, D_FF), BF)
    return pl.pallas_call(
        body, grid=(T // RC,), in_specs=[_rows_spec(D), _vec(D), _whole((D_FF, D)), _whole((D_FF, D))],
        out_specs=[_rows_spec(D_FF), _rows_spec(D_FF), _rows_spec(D_FF), _rows_spec(D)],
        out_shape=[big, big, big, jax.ShapeDtypeStruct((T, D), BF)],
        compiler_params=_params(), name="gate_up")(h1, wn, w_gate, w_up)


def down_loss(act, w_down, h1, target, wf):
    first = NPAD + N_META

    def body(a_ref, w_ref, r_ref, t_hbm, wf_ref, d_ref, db_ref, l_ref, dw_ref, h_scr, t_ref, t_sem):
        i = pl.program_id(0)
        _zero_at_first(l_ref, dw_ref)
        head = pltpu.make_async_copy(t_hbm.at[pl.ds(0, RC - first)], t_ref.at[pl.ds(first, RC - first)], t_sem)
        rest = pltpu.make_async_copy(t_hbm.at[pl.ds(pl.multiple_of(jnp.maximum(i * RC - first, 0), 32), RC)], t_ref, t_sem)

        @pl.when(i == 0)
        def _():
            t_ref[0:first, :] = jnp.zeros((first, D), F32)
            head.start()

        @pl.when(i > 0)
        def _():
            rest.start()

        def tile(c0):
            cols = pl.ds(c0, 512)
            h_scr[:, cols] = r_ref[:, cols] + jnp.dot(a_ref[...], w_ref[:, cols], preferred_element_type=F32)

        _col_tiles(D, 512, tile)

        @pl.when(i == 0)
        def _():
            head.wait()

        @pl.when(i > 0)
        def _():
            rest.wait()

        for r in (0, HALF):
            h = h_scr[r:r + HALF, :]
            live = _rows((HALF, D), i * RC + r) >= first
            err = jnp.where(live, _rms(h, wf_ref[...]) - t_ref[r:r + HALF, :], 0.0)
            l_ref[...] += 0.5 * jnp.sum(jnp.sum(err * err, axis=1, keepdims=True) * (1.0 / D), axis=0, keepdims=True)
            dh, dw = _rms_bwd(err * (1.0 / D), h, wf_ref[...])
            dw_ref[...] += jnp.sum(dw, axis=0, keepdims=True)
            d_ref[r:r + HALF, :] = dh
            db_ref[r:r + HALF, :] = dh.astype(BF)

    return pl.pallas_call(
        body, grid=(T // RC,),
        in_specs=[_rows_spec(D_FF), _whole((D_FF, D)), _rows_spec(D), pl.BlockSpec(memory_space=pl.ANY), _vec(D)],
        out_specs=[_rows_spec(D), _rows_spec(D), _spec((1, 128), lambda i: (0, 0)), _vec(D)],
        out_shape=[jax.ShapeDtypeStruct((T, D), F32), jax.ShapeDtypeStruct((T, D), BF), jax.ShapeDtypeStruct((1, 128), F32),
                   jax.ShapeDtypeStruct((1, D), F32)],
        scratch_shapes=[pltpu.VMEM((RC, D), F32), pltpu.VMEM((RC, D), F32), pltpu.SemaphoreType.DMA],
        compiler_params=_params(), name="down_loss")(act, w_down, h1, target, wf)


def swiglu_bwd(dh2_b, w_down, gt, up, act, u2):
    tn = 256

    def body(d_ref, u_ref, w_ref, gt_ref, up_ref, act_ref, dg_ref, du_ref, gd_ref, gg_ref, gu_ref, acc_d, acc_g, acc_u):
        for acc in (acc_d, acc_g, acc_u):
            acc[...] = jnp.zeros_like(acc)

        def rows(r0):
            part = pl.ds(r0, RC)
            d = d_ref[part, :]
            dact = lax.dot_general(d, w_ref[...], NT_DIMS, preferred_element_type=F32)
            gt_ = gt_ref[part, :].astype(F32)
            up_ = up_ref[part, :].astype(F32)
            sg = _sigmoid(gt_)
            dgt = (dact * up_ * (sg * (1.0 + gt_ * (1.0 - sg)))).astype(BF)
            dup = (dact * (gt_ * sg)).astype(BF)
            dg_ref[part, :] = dgt
            du_ref[part, :] = dup
            u = u_ref[part, :]
            acc_d[...] += lax.dot_general(act_ref[part, :], d, TN_DIMS, preferred_element_type=F32)
            acc_g[...] += lax.dot_general(dgt, u, TN_DIMS, preferred_element_type=F32)
            acc_u[...] += lax.dot_general(dup, u, TN_DIMS, preferred_element_type=F32)

        _col_tiles(T, RC, rows)
        gd_ref[...] = acc_d[...].astype(BF)
        gg_ref[...] = acc_g[...].astype(BF)
        gu_ref[...] = acc_u[...].astype(BF)

    resident = _spec((T, D), lambda j: (0, 0), single=True)
    cols = _spec((T, tn), lambda j: (0, j))
    wrow = _spec((tn, D), lambda j: (j, 0))
    big = jax.ShapeDtypeStruct((T, D_FF), BF)
    grad = jax.ShapeDtypeStruct((D_FF, D), BF)
    return pl.pallas_call(
        body, grid=(D_FF // tn,), in_specs=[resident, resident, wrow, cols, cols, cols],
        out_specs=[cols, cols, wrow, wrow, wrow], out_shape=[big, big, grad, grad, grad],
        scratch_shapes=[pltpu.VMEM((tn, D), F32)] * 3,
        compiler_params=_params(), name="swiglu_bwd")(dh2_b, u2, w_down, gt, up, act)


def gate_up_bwd(dgt, dup, w_gate, w_up, h1, wn, dh2):
    def body(dg_ref, du_ref, wg_ref, wu_ref, h_ref, wn_ref, r_ref, d_ref, db_ref, dw_ref, du_scr):
        _zero_at_first(dw_ref)

        du_scr[...] = jnp.zeros_like(du_scr)

        def tile(c0):
            k = pl.ds(c0, 256)
            du_scr[...] += (jnp.dot(dg_ref[:, k], wg_ref[k, :], preferred_element_type=F32)
                            + jnp.dot(du_ref[:, k], wu_ref[k, :], preferred_element_type=F32))

        _col_tiles(D_FF, 256, tile)
        for r in (0, HALF):
            dh, dw = _rms_bwd(du_scr[r:r + HALF, :], h_ref[r:r + HALF, :], wn_ref[...])
            dw_ref[...] += jnp.sum(dw, axis=0, keepdims=True)
            dh = dh + r_ref[r:r + HALF, :]
            d_ref[r:r + HALF, :] = dh
            db_ref[r:r + HALF, :] = dh.astype(BF)

    return pl.pallas_call(
        body, grid=(T // RC,),
        in_specs=[_rows_spec(D_FF), _rows_spec(D_FF), _whole((D_FF, D)), _whole((D_FF, D)), _rows_spec(D), _vec(D), _rows_spec(D)],
        out_specs=[_rows_spec(D), _rows_spec(D), _vec(D)],
        out_shape=[jax.ShapeDtypeStruct((T, D), F32), jax.ShapeDtypeStruct((T, D), BF), jax.ShapeDtypeStruct((1, D), F32)],
        scratch_shapes=[pltpu.VMEM((RC, D), F32)],
        compiler_params=_params(), name="gate_up_bwd")(dgt, dup, w_gate, w_up, h1, wn, dh2)


def _adamw(w, g, m, v):
    m = ADAM_B1 * m + (1.0 - ADAM_B1) * g
    v = ADAM_B2 * v + (1.0 - ADAM_B2) * (g * g)
    m_hat = m / (1.0 - ADAM_B1 ** ADAM_STEP)
    v_hat = v / (1.0 - ADAM_B2 ** ADAM_STEP)
    delta = -ADAM_LR * (m_hat / (jnp.sqrt(v_hat) + ADAM_EPS) + ADAM_WD * w)
    return delta, m, v


def adamw_shard(name, recv, w, m, v, tr, tc):
    r, c = w.shape

    def body(p_ref, w_ref, m_ref, v_ref, g_ref, d_ref, mo_ref, vo_ref):
        g = p_ref[0].astype(F32)
        for s in range(1, 8):
            g = g + p_ref[s].astype(F32)
        g_ref[...] = g
        d_ref[...], mo_ref[...], vo_ref[...] = _adamw(w_ref[...], g, m_ref[...], v_ref[...])

    tile = _spec((tr, tc), lambda i, j: (i, j))
    shape = jax.ShapeDtypeStruct((r, c), F32)
    return pl.pallas_call(
        body, grid=(r // tr, c // tc), in_specs=[_spec((8, tr, tc), lambda i, j: (0, i, j)), tile, tile, tile],
        out_specs=[tile] * 4, out_shape=[shape] * 4, compiler_params=_params(2), name=name)(recv, w, m, v)


def adamw_w_in(recv, w, m, v):
    rows = w.shape[0] // 8

    def body(p_ref, w_ref, m_ref, v_ref, g_ref, d_ref, mo_ref, vo_ref):
        for q in range(8):
            cols = slice(128 * q, 128 * q + 128)
            g = p_ref[0, :, cols].astype(F32)
            for s in range(1, 8):
                g = g + p_ref[s, :, cols].astype(F32)
            part = pl.ds(q, rows, stride=8)
            g_ref[part, :] = g
            d_ref[part, :], mo_ref[part, :], vo_ref[part, :] = _adamw(w_ref[part, :], g, m_ref[part, :], v_ref[part, :])

    shape = jax.ShapeDtypeStruct(w.shape, F32)
    return pl.pallas_call(body, out_shape=[shape] * 4, compiler_params=_params(0), name="adamw_w_in")(recv, w, m, v)


def sum_slabs(recv):
    def body(p_ref, o_ref):
        g = p_ref[0]
        for s in range(1, 8):
            g = g + p_ref[s]
        o_ref[...] = g

    return pl.pallas_call(body, out_shape=jax.ShapeDtypeStruct(recv.shape[1:], F32), compiler_params=_params(0), name="sum_slabs")(recv)


SIMPLE = [("norm1_w", 1024), ("ssd_conv_b", 1536), ("ssd_dt_bias", 16), ("ssd_a_log", 16), ("ssd_d", 16), ("ssd_norm_w", 1024),
          ("lru_conv_b", 1024), ("lru_ba", 1024), ("lru_bx", 1024), ("lru_lambda", 1024), ("lru_norm_w", 1024), ("norm2_w", 1024),
          ("final_norm_w", 1024)]
SPECIAL = ["lru_wa", "lru_wx", "meta_tokens", "ssd_conv_w", "lru_conv_w"]
SM_ROWS = 176
SM_WA, SM_WX, SM_META, SM_SCW, SM_LCW, SM_LOSS = 14, 78, 142, 158, 166, 170


def _simple_rows():
    rows, r = {}, 0
    for name, n in SIMPLE:
        rows[name] = r
        r += -(-n // 1024)
    return rows


def adamw_small(sm, special_g, ws, ms, vs):
    rows = _simple_rows()
    ns, nx = len(SIMPLE), len(SPECIAL)

    def body(*refs):
        sm_ref = refs[0]
        gx = refs[1:1 + nx]
        wr = refs[1 + nx:1 + nx + ns + nx]
        mr = refs[1 + nx + ns + nx:1 + nx + 2 * (ns + nx)]
        vr = refs[1 + nx + 2 * (ns + nx):1 + nx + 3 * (ns + nx)]
        outs = refs[1 + nx + 3 * (ns + nx):]
        o = 0
        for k, (name, n) in enumerate(SIMPLE):
            r0 = rows[name]
            for c0 in range(0, n, 1024):
                wd = min(1024, n - c0)
                g = sm_ref[r0 + c0 // 1024:r0 + c0 // 1024 + 1, 0:wd]
                sl = (slice(None), slice(c0, c0 + wd))
                d, m2, v2 = _adamw(wr[k][sl], g, mr[k][sl], vr[k][sl])
                outs[o][sl] = g
                outs[o + 1][sl] = d
                outs[o + 2][sl] = m2
                outs[o + 3][sl] = v2
            o += 4
        for k in range(nx):
            d, m2, v2 = _adamw(wr[ns + k][...], gx[k][...], mr[ns + k][...], vr[ns + k][...])
            outs[o][...] = d
            outs[o + 1][...] = m2
            outs[o + 2][...] = v2
            o += 3

    out_shape = []
    for k in range(ns):
        out_shape += [jax.ShapeDtypeStruct(ws[k].shape, F32)] * 4
    for k in range(nx):
        out_shape += [jax.ShapeDtypeStruct(ws[ns + k].shape, F32)] * 3
    return pl.pallas_call(body, out_shape=out_shape, compiler_params=_params(0), name="adamw_small")(sm, *special_g, *ws, *ms, *vs)


def _place():
    return lax.axis_index("x"), lax.axis_index("y"), lax.axis_index("c")


def _index(px, py, pc):
    return 4 * px + 2 * py + pc


def all_gather(name, shards):
    n = len(shards)
    hbm = pl.BlockSpec(memory_space=pl.ANY)

    def body(*refs):
        ins, outs = refs[:n], refs[n:2 * n]
        send_sems, recv_sems, local_sems = refs[2 * n:]
        x, y, c = _place()
        me, sibling = (x, y, c), (x, y, 1 - c)
        chips = [(1 - x, y), (x, 1 - y), (1 - x, 1 - y)]

        def copy(i, k, block, to, src=None):
            dst = outs[i].at[_index(*block)]
            return pltpu.make_async_remote_copy(src_ref=dst if src is None else src, dst_ref=dst, send_sem=send_sems.at[7 * i + k],
                                                recv_sem=recv_sems.at[7 * i + k], device_id=to, device_id_type=MESH)

        mine = [pltpu.make_async_copy(ins[i], outs[i].at[_index(*me)], local_sems.at[i]) for i in range(n)]
        for cp in mine:
            cp.start()
        first = []
        for i in range(n):
            first += [copy(i, 1 + j, me, (*chip, c), src=ins[i]) for j, chip in enumerate(chips)]
            first.append(copy(i, 0, me, sibling, src=ins[i]))
        for cp in first:
            cp.start()
        passed = []
        for i in range(n):
            for j, chip in enumerate(chips):
                copy(i, 1 + j, (*chip, c), me).wait_recv()
                cp = copy(i, 4 + j, (*chip, c), sibling)
                cp.start()
                passed.append(cp)
        for i in range(n):
            copy(i, 0, sibling, me).wait_recv()
            for j, chip in enumerate(chips):
                copy(i, 4 + j, (*chip, 1 - c), me).wait_recv()
        for cp in first + passed:
            cp.wait_send()
        for cp in mine:
            cp.wait()

    return pl.pallas_call(
        body, in_specs=[hbm] * n, out_specs=[hbm] * n,
        out_shape=[jax.ShapeDtypeStruct((8,) + s.shape, s.dtype) for s in shards],
        scratch_shapes=[pltpu.SemaphoreType.DMA((7 * n,)), pltpu.SemaphoreType.DMA((7 * n,)), pltpu.SemaphoreType.DMA((n,))],
        name=name)(*shards)


HBM_SPEC = pl.BlockSpec(memory_space=pltpu.HBM)
SEM_SPEC = pl.BlockSpec(memory_space=pltpu.SEMAPHORE)
EFFECT = pltpu.SideEffectType.DATAFLOW_SIDE_EFFECTING


def _peers(x, y, c):
    return [((1 - x) if k & 4 else x, (1 - y) if k & 2 else y, (1 - c) if k & 1 else c) for k in range(1, 8)]


def _pieces(rows):
    for n in (4, 2):
        if rows % (16 * n) == 0:
            return [(r * (rows // n), rows // n) for r in range(n)]
    return [(0, rows)]


def _peer_copies(src, land, send_sems, recv_sems, k, peer, mine, slab_src):
    block = src.at[_index(*peer)] if slab_src else src
    return [pltpu.make_async_remote_copy(src_ref=block.at[pl.ds(r0, nr)], dst_ref=land.at[mine, pl.ds(r0, nr)], send_sem=send_sems.at[k],
                                         recv_sem=recv_sems.at[k], device_id=peer, device_id_type=MESH)
            for r0, nr in _pieces(block.shape[0])]


def copies_start(name, srcs, slab_src, after):
    n = len(srcs)
    zones = [jax.ShapeDtypeStruct(s.shape if slab_src else (8,) + s.shape, s.dtype) for s in srcs]
    afters = [] if after is None else [after]

    def body(*refs):
        ins, lands = refs[:n], refs[n:2 * n]
        first = 2 * n + len(afters)
        sends, recvs = refs[first:first + n], refs[first + n:first + 2 * n]
        token = refs[-1]
        x, y, c = _place()
        mine = _index(x, y, c)
        for i in range(n):
            per_peer = [_peer_copies(ins[i], lands[i], sends[i], recvs[i], k, peer, mine, slab_src) for k, peer in enumerate(_peers(x, y, c))]
            for piece in zip(*per_peer):
                for cp in piece:
                    cp.start()
        token[...] = jnp.zeros_like(token)

    sem = pltpu.SemaphoreType.DMA((7,))
    res = pl.pallas_call(
        body, name=name,
        out_shape=([sem] * (2 * n) + [pltpu.HBM(s.shape, s.dtype) for s in srcs] + [pltpu.HBM(z.shape, z.dtype) for z in zones]
                   + [jax.ShapeDtypeStruct((8, 128), F32)]),
        in_specs=[HBM_SPEC] * (2 * n) + [pl.BlockSpec(memory_space=pl.ANY)] * len(afters),
        out_specs=[SEM_SPEC] * (2 * n) + [HBM_SPEC] * (2 * n) + [pl.BlockSpec(memory_space=pltpu.VMEM)],
        input_output_aliases={i: 2 * n + i for i in range(2 * n)},
        compiler_params=pltpu.CompilerParams(has_side_effects=EFFECT),
    )(*[pltpu.with_memory_space_constraint(s, pltpu.HBM) for s in srcs],
      *[pltpu.with_memory_space_constraint(lax.empty(z.shape, z.dtype), pltpu.HBM) for z in zones], *afters)
    return [(res[i], res[n + i], res[2 * n + i], res[3 * n + i]) for i in range(n)], res[-1][0:1, 0:1]


def copies_wait(name, started, slab_src, after):
    n = len(started)

    def body(*refs):
        ins, lands = refs[:n], refs[n:2 * n]
        sends, recvs = refs[2 * n:3 * n], refs[3 * n:4 * n]
        x, y, c = _place()
        mine = _index(x, y, c)
        for i in range(n):
            for k, peer in enumerate(_peers(x, y, c)):
                arrival = pltpu.make_async_remote_copy(src_ref=ins[i].at[mine] if slab_src else ins[i], dst_ref=lands[i].at[_index(*peer)],
                                                       send_sem=sends[i].at[k], recv_sem=recvs[i].at[k], device_id=peer, device_id_type=MESH)
                arrival.wait_send()
                arrival.wait_recv()

    srcs = [s[2] for s in started]
    lands = [s[3] for s in started]
    afters = list(after) if isinstance(after, (list, tuple)) else [after]
    res = pl.pallas_call(
        body, name=name,
        out_shape=[pltpu.HBM(s.shape, s.dtype) for s in srcs] + [pltpu.HBM(z.shape, z.dtype) for z in lands],
        in_specs=[HBM_SPEC] * (2 * n) + [SEM_SPEC] * (2 * n) + [pl.BlockSpec(memory_space=pl.ANY)] * len(afters),
        out_specs=[HBM_SPEC] * (2 * n),
        input_output_aliases={i: i for i in range(2 * n)},
        compiler_params=pltpu.CompilerParams(has_side_effects=EFFECT),
    )(*srcs, *lands, *[s[0] for s in started], *[s[1] for s in started], *afters)
    me = _index(*_place())
    own = [lax.dynamic_index_in_dim(s, me, 0, keepdims=True) if slab_src else s[None] for s in res[:n]]
    return [lax.dynamic_update_slice_in_dim(z, o, me, 0) for z, o in zip(res[n:], own)]


WEIGHTS = ["meta_tokens", "norm1_w", "w_in", "ssd_conv_w", "ssd_conv_b", "ssd_dt_bias", "ssd_a_log", "ssd_d", "ssd_norm_w", "lru_conv_w",
           "lru_conv_b", "lru_wa", "lru_ba", "lru_wx", "lru_bx", "lru_lambda", "lru_norm_w", "w_out", "norm2_w", "w_gate", "w_up", "w_down",
           "final_norm_w"]
BIG = ["w_in", "w_out", "w_gate", "w_up", "w_down"]
COLUMN_SHARDED = ["w_in", "w_gate", "w_up"]
BIG_TILE = {"w_in": (578, 256), "w_out": (128, 1024), "w_gate": (176, 1024), "w_up": (176, 1024), "w_down": (176, 1024)}


def _pair_blocks(w):
    w = w.reshape(8, 2, 64, 64)
    z = jnp.zeros((8, 64, 64), w.dtype)
    return jnp.concatenate([jnp.concatenate([w[:, 0], z], axis=2), jnp.concatenate([z, w[:, 1]], axis=2)], axis=1)


def _unpair_blocks(w2):
    return jnp.stack([w2[:, :64, :64], w2[:, 64:, 64:]], axis=1).reshape(16, 64, 64)


def _per_group(v):
    return jnp.pad(v.reshape(2, 1, 8), ((0, 0), (0, 0), (0, 120)))


def _pad_cols(v, n):
    return jnp.pad(v, ((0, 0), (0, n - v.shape[1])))


def local_step(x, target, meta, ssd_cw, lru_cw, w_in, fetch, send, p):
    z120 = jnp.zeros((120, D), BF)
    w_dt = jnp.concatenate([w_in[2560:2568], z120, w_in[2568:2576], z120], axis=0)
    bias2, alog2, d2 = _per_group(p["ssd_dt_bias"]), _per_group(p["ssd_a_log"]), _per_group(p["ssd_d"])
    wa2 = _pair_blocks(p["lru_wa"]).astype(BF)
    wx2 = _pair_blocks(p["lru_wx"]).astype(BF)
    lru = (lru_cw, p["lru_conv_b"], wa2, p["lru_ba"], wx2, p["lru_bx"], p["lru_lambda"])

    h0 = jnp.concatenate([jnp.zeros((NPAD, D), F32), meta, x], axis=0)
    proj, dt_raw, u1 = in_proj(h0, p["norm1_w"], w_in, w_dt)
    xbc_act = conv_silu_fwd(proj, ssd_cw, p["ssd_conv_b"])
    yn_ssd, y_pre, h_prev = ssd_fwd(xbc_act, proj, dt_raw, bias2, alog2, d2, p["ssd_norm_w"])
    a, u = lru_gates_fwd(proj, *lru)
    hseq = lru_scan_fwd(a, u)
    (w_out,) = fetch(["w_out"], hseq)
    h1, cat = out_proj(yn_ssd, proj, hseq, p["lru_norm_w"], w_out, h0)
    w_gate, w_up = fetch(["w_gate", "w_up"], h1)
    gt, up, act, u2 = gate_up(h1, p["norm2_w"], w_gate, w_up)
    (w_down,) = fetch(["w_down"], act)
    dh2, dh2_b, loss, d_fnw = down_loss(act, w_down, h1, target, p["final_norm_w"])

    dgt, dup, g_down, g_gate, g_up = swiglu_bwd(dh2_b, w_down, gt, up, act, u2)
    sent = send({"w_down": g_down, "w_gate": g_gate, "w_up": g_up})
    dh1, dh1_b, d_n2 = gate_up_bwd(dgt, dup, w_gate, w_up, h1, p["norm2_w"] + sent, dh2)
    sent = send({"w_out": matmul_tn("dw_out", cat, dh1_b, 512, 1024)})
    dyn, dh_out, dg_b, d_lnw = out_proj_bwd(dh1_b, w_out, proj, hseq, p["lru_norm_w"] + sent)

    dhs = lru_scan_bwd(a, dh_out)
    dxl_b, d_lcw, d_lcb, dwa2, d_ba, dwx2, d_bx, d_lam = lru_gates_bwd(dhs, hseq, proj, *lru)
    dz_b, dx, d_b, d_c, ddt_b, dpar, d_snw = ssd_bwd(dyn, xbc_act, proj, dt_raw, y_pre, h_prev, bias2, alog2, d2, p["ssd_norm_w"])
    dxbc_b, d_scw, d_scb = conv_silu_bwd(dx, d_b, d_c, proj, ssd_cw, p["ssd_conv_b"])
    g_p = in_proj_wgrad([dz_b, dg_b, dxl_b, dxbc_b, ddt_b], u1)
    g_in = jnp.concatenate([g_p[PZ:PZ + 1024], g_p[PXBC:PXBC + XBC], g_p[NP_IN:NP_IN + 8], g_p[NP_IN + 128:NP_IN + 136],
                            g_p[PG:PG + 1024], g_p[PXL:PXL + 1024]], axis=0)
    sent = send({"w_in": g_in})
    grad_x, d_meta, d_n1 = in_proj_bwd(dz_b, dg_b, dxl_b, dxbc_b, ddt_b, w_in, w_dt, h0, p["norm1_w"] + sent, dh1)
    small = {"norm1_w": d_n1, "ssd_conv_b": d_scb, "ssd_dt_bias": dpar[:, 0, :8].reshape(1, 16), "ssd_a_log": dpar[:, 1, :8].reshape(1, 16),
             "ssd_d": dpar[:, 2, :8].reshape(1, 16), "ssd_norm_w": d_snw, "lru_conv_b": d_lcb, "lru_ba": d_ba, "lru_bx": d_bx,
             "lru_lambda": d_lam, "lru_norm_w": d_lnw, "norm2_w": d_n2, "final_norm_w": d_fnw,
             "lru_wa": _unpair_blocks(dwa2), "lru_wx": _unpair_blocks(dwx2), "meta_tokens": d_meta,
             "ssd_conv_w": d_scw, "lru_conv_w": d_lcw}
    return loss, grad_x, small


def _pack_small(small, loss):
    rows = [_pad_cols(small[name], -(-n // 1024) * 1024).reshape(-1, 1024) for name, n in SIMPLE]
    rows += [small["lru_wa"].reshape(64, 1024), small["lru_wx"].reshape(64, 1024), small["meta_tokens"],
             _pad_cols(small["ssd_conv_w"], 2048).reshape(8, 1024), small["lru_conv_w"], _pad_cols(loss[:, 0:1], 1024)]
    sm = jnp.concatenate(rows, axis=0)
    return jnp.pad(sm, ((0, SM_ROWS - sm.shape[0]), (0, 0)))


def _slabs(g):
    return g.reshape(8, g.shape[0] // 8, g.shape[1])


def _unslab(g):
    return g.reshape(8 * g.shape[1], g.shape[2])


def kernel(x, meta_tokens, norm1_w, w_in, ssd_conv_w, ssd_conv_b, ssd_dt_bias, ssd_a_log, ssd_d, ssd_norm_w, lru_conv_w, lru_conv_b, lru_wa, lru_ba, lru_wx, lru_bx, lru_lambda, lru_norm_w, w_out, norm2_w, w_gate, w_up, w_down, final_norm_w, loss_target, m_meta_tokens, m_norm1_w, m_w_in, m_ssd_conv_w, m_ssd_conv_b, m_ssd_dt_bias, m_ssd_a_log, m_ssd_d, m_ssd_norm_w, m_lru_conv_w, m_lru_conv_b, m_lru_wa, m_lru_ba, m_lru_wx, m_lru_bx, m_lru_lambda, m_lru_norm_w, m_w_out, m_norm2_w, m_w_gate, m_w_up, m_w_down, m_final_norm_w, v_meta_tokens, v_norm1_w, v_w_in, v_ssd_conv_w, v_ssd_conv_b, v_ssd_dt_bias, v_ssd_a_log, v_ssd_d, v_ssd_norm_w, v_lru_conv_w, v_lru_conv_b, v_lru_wa, v_lru_ba, v_lru_wx, v_lru_bx, v_lru_lambda, v_lru_norm_w, v_w_out, v_norm2_w, v_w_gate, v_w_up, v_w_down, v_final_norm_w):
    w = dict(meta_tokens=meta_tokens, norm1_w=norm1_w, w_in=w_in[0], ssd_conv_w=ssd_conv_w[0], ssd_conv_b=ssd_conv_b, ssd_dt_bias=ssd_dt_bias,
             ssd_a_log=ssd_a_log, ssd_d=ssd_d, ssd_norm_w=ssd_norm_w, lru_conv_w=lru_conv_w[0], lru_conv_b=lru_conv_b, lru_wa=lru_wa[0],
             lru_ba=lru_ba, lru_wx=lru_wx[0], lru_bx=lru_bx, lru_lambda=lru_lambda, lru_norm_w=lru_norm_w, w_out=w_out[0], norm2_w=norm2_w,
             w_gate=w_gate[0], w_up=w_up[0], w_down=w_down[0], final_norm_w=final_norm_w.reshape(1, D))
    m = dict(meta_tokens=m_meta_tokens, norm1_w=m_norm1_w, w_in=m_w_in[0], ssd_conv_w=m_ssd_conv_w[0], ssd_conv_b=m_ssd_conv_b,
             ssd_dt_bias=m_ssd_dt_bias, ssd_a_log=m_ssd_a_log, ssd_d=m_ssd_d, ssd_norm_w=m_ssd_norm_w, lru_conv_w=m_lru_conv_w[0],
             lru_conv_b=m_lru_conv_b, lru_wa=m_lru_wa[0], lru_ba=m_lru_ba, lru_wx=m_lru_wx[0], lru_bx=m_lru_bx, lru_lambda=m_lru_lambda,
             lru_norm_w=m_lru_norm_w, w_out=m_w_out[0], norm2_w=m_norm2_w, w_gate=m_w_gate[0], w_up=m_w_up[0], w_down=m_w_down[0],
             final_norm_w=m_final_norm_w.reshape(1, D))
    v = dict(meta_tokens=v_meta_tokens, norm1_w=v_norm1_w, w_in=v_w_in[0], ssd_conv_w=v_ssd_conv_w[0], ssd_conv_b=v_ssd_conv_b,
             ssd_dt_bias=v_ssd_dt_bias, ssd_a_log=v_ssd_a_log, ssd_d=v_ssd_d, ssd_norm_w=v_ssd_norm_w, lru_conv_w=v_lru_conv_w[0],
             lru_conv_b=v_lru_conv_b, lru_wa=v_lru_wa[0], lru_ba=v_lru_ba, lru_wx=v_lru_wx[0], lru_bx=v_lru_bx, lru_lambda=v_lru_lambda,
             lru_norm_w=v_lru_norm_w, w_out=v_w_out[0], norm2_w=v_norm2_w, w_gate=v_w_gate[0], w_up=v_w_up[0], w_down=v_w_down[0],
             final_norm_w=v_final_norm_w.reshape(1, D))
    shapes = dict(meta_tokens=meta_tokens.shape, norm1_w=norm1_w.shape, w_in=w_in.shape, ssd_conv_w=ssd_conv_w.shape,
                  ssd_conv_b=ssd_conv_b.shape, ssd_dt_bias=ssd_dt_bias.shape, ssd_a_log=ssd_a_log.shape, ssd_d=ssd_d.shape,
                  ssd_norm_w=ssd_norm_w.shape, lru_conv_w=lru_conv_w.shape, lru_conv_b=lru_conv_b.shape, lru_wa=lru_wa.shape,
                  lru_ba=lru_ba.shape, lru_wx=lru_wx.shape, lru_bx=lru_bx.shape, lru_lambda=lru_lambda.shape, lru_norm_w=lru_norm_w.shape,
                  w_out=w_out.shape, norm2_w=norm2_w.shape, w_gate=w_gate.shape, w_up=w_up.shape, w_down=w_down.shape,
                  final_norm_w=final_norm_w.shape)
    me = _index(*_place())
    for n in COLUMN_SHARDED:
        w[n], m[n], v[n] = w[n].T, m[n].T, v[n].T

    small_shard = jnp.concatenate([w["meta_tokens"], _pad_cols(w["ssd_conv_w"], 256).reshape(8, 128), w["lru_conv_w"],
                                   jnp.zeros((4, 128), F32)], axis=0)
    g_in, gs = all_gather("gather_w_in", [w["w_in"].astype(BF), small_shard])
    later = ["w_out", "w_gate", "w_up", "w_down"]
    started, behind = copies_start("gather_rest_start", [w[n].astype(BF) for n in later], False, gs)
    started = dict(zip(later, started))
    meta_full = gs[:, 0:16].transpose(1, 0, 2).reshape(N_META, D)
    ssd_cw = gs[:, 16:24].reshape(8, 4, 256)[:, :, :192].transpose(1, 0, 2).reshape(4, XBC)
    lru_cw = gs[:, 24:28].transpose(1, 0, 2).reshape(4, LRU_W)

    def fetch(names, after):
        got = copies_wait("gather_" + names[0] + "_wait", [started[n] for n in names], False, after)
        return [_unslab(g) for g in got]

    in_flight = {}

    def send(grads):
        names = list(grads)
        st, token = copies_start("grads_" + names[0] + "_start", [grads[n] if n == "small" else _slabs(grads[n]) for n in names], True, None)
        in_flight.update(zip(names, st))
        return token

    loss, grad_x, small = local_step(x[0], loss_target[0], meta_full, ssd_cw, lru_cw, _unslab(g_in), fetch, send,
                                     {**w, "norm1_w": w["norm1_w"] + behind})
    send({"small": _pack_small(small, loss).reshape(8, SM_ROWS // 8, 1024)})

    out = {}
    early = ["w_down", "w_gate", "w_up", "w_out"]
    recv = dict(zip(early, copies_wait("grads_early_wait", [in_flight[n] for n in early], True, in_flight["small"][2])))
    for n in early:
        out[n] = adamw_shard("adamw_" + n, recv[n], w[n], m[n], v[n], *BIG_TILE[n])
    recv_in, recv_small = copies_wait("grads_late_wait", [in_flight["w_in"], in_flight["small"]], True, [out[n][0] for n in early])
    untiled = (IN_COLS, 128)
    out["w_in"] = [o.reshape(IN_COLS // 8, D)
                   for o in adamw_w_in(recv_in, w["w_in"].reshape(untiled), m["w_in"].reshape(untiled), v["w_in"].reshape(untiled))]
    for n in COLUMN_SHARDED:
        out[n] = [o.T for o in out[n]]
    sm = all_gather("gather_small_grads", [sum_slabs(recv_small)])[0].reshape(SM_ROWS, 1024)
    special_g =[sm[SM_WA:SM_WA + 64].reshape(16, 64, 64), sm[SM_WX:SM_WX + 64].reshape(16, 64, 64),
                 lax.dynamic_slice(sm[SM_META:SM_META + 16], (0, 128 * me), (16, 128)),
                 lax.dynamic_slice(sm[SM_SCW:SM_SCW + 8].reshape(4, 2048), (0, 192 * me), (4, 192)),
                 lax.dynamic_slice(sm[SM_LCW:SM_LCW + 4], (0, 128 * me), (4, 128))]
    names = [n for n, _ in SIMPLE] + SPECIAL
    res = adamw_small(sm, special_g, [w[n] for n in names], [m[n] for n in names], [v[n] for n in names])
    for k, (n, _) in enumerate(SIMPLE):
        out[n] = res[4 * k:4 * k + 4]
    for k, n in enumerate(SPECIAL):
        o = 4 * len(SIMPLE) + 3 * k
        out[n] = [special_g[k]] + list(res[o:o + 3])
    loss_total = sm[SM_LOSS, 0]
    flat = [loss_total, grad_x[None]]
    for k in range(4):
        flat += [out[n][k].reshape(shapes[n]) for n in WEIGHTS]
    return tuple(flat)
```

```python
import math

import jax
import jax.numpy as jnp
from jax import lax
from jax.experimental import pallas as pl
from jax.experimental.pallas import tpu as pltpu

F32 = jnp.float32
BF = jnp.bfloat16

D = 1024
SEQ = 2048
N_META = 16
Q = 128
NPAD = 112
T = NPAD + N_META + SEQ
NCH = T // Q
RC = 544
D_FF = 2816
SSD_W = 1024
LRU_W = 1024
XBC = 1536
IN_COLS = 4624
PZ, PG, PXL, PXBC = 0, 1024, 2048, 3072
NP_IN = 4608
EPS = 1e-6
LRU_C = 8.0
VMEM_LIMIT = 56 * 1024 * 1024

ADAM_LR, ADAM_B1, ADAM_B2, ADAM_EPS, ADAM_WD, ADAM_STEP = 0.001, 0.9, 0.999, 1e-08, 0.01, 10

NT_DIMS = (((1,), (1,)), ((), ()))
TN_DIMS = (((0,), (0,)), ((), ()))
MESH = pl.DeviceIdType.MESH


def _params(n_grid=1, limit=VMEM_LIMIT):
    return pltpu.CompilerParams(dimension_semantics=("arbitrary",) * n_grid, vmem_limit_bytes=limit)


def _spec(shape, imap, single=False):
    if single:
        return pl.BlockSpec(shape, imap, pipeline_mode=pl.Buffered(1))
    return pl.BlockSpec(shape, imap)


def _sigmoid(x):
    return 0.5 * jnp.tanh(0.5 * x) + 0.5


def _sigmoid_gate(x):
    return 1.0 / (1.0 + jnp.exp(-x))


def _softplus(x):
    return jnp.maximum(x, 0.0) + jnp.log(1.0 + jnp.exp(-jnp.abs(x)))


def _rms_stats(h):
    return lax.rsqrt(jnp.mean(h * h, axis=-1, keepdims=True) + EPS)


def _rms(h, w):
    return (h * _rms_stats(h)) * w


def _rms_bwd(du, h, w):
    r = _rms_stats(h)
    n = h * r
    dn = du * w
    dh = r * (dn - n * jnp.mean(dn * n, axis=-1, keepdims=True))
    return dh, du * n


_G0 = math.sqrt(2.0 / math.pi)


def _gelu(x):
    return 0.5 * x * (1.0 + jnp.tanh(_G0 * (x + 0.044715 * (x * x * x))))


def _gelu_grad(x):
    t = jnp.tanh(_G0 * (x + 0.044715 * (x * x * x)))
    return 0.5 * (1.0 + t) + 0.5 * x * (1.0 - t * t) * (_G0 * (1.0 + 3.0 * 0.044715 * (x * x)))


def _rows(shape, r0=0):
    return lax.broadcasted_iota(jnp.int32, shape, 0) + r0


def _lanes(shape):
    return lax.broadcasted_iota(jnp.int32, shape, 1)


HALO = 8


def _fill_padded(pad_ref, x_ref):
    pad_ref[0:HALO, :] = jnp.zeros((HALO, pad_ref.shape[1]), F32)
    pad_ref[T + HALO:T + 2 * HALO, :] = jnp.zeros((HALO, pad_ref.shape[1]), F32)

    def step(c, carry):
        r0 = pl.multiple_of(c * Q, Q)
        pad_ref[pl.ds(r0 + HALO, Q), :] = x_ref[pl.ds(r0, Q), :].astype(F32)
        return carry

    lax.fori_loop(0, NCH, step, 0)


def _back(pad_ref, r0):
    win = pad_ref[pl.ds(r0, Q + HALO), :]
    return lambda s: win[HALO:, :] if s == 0 else pltpu.roll(win, s, axis=0)[HALO:, :]


def _ahead(pad_ref, r0):
    win = pad_ref[pl.ds(r0 + HALO, Q + HALO), :]
    return lambda s: win[:Q, :] if s == 0 else pltpu.roll(win, Q + HALO - s, axis=0)[:Q, :]


def _conv(back, w, b):
    y = b + w[3:4, :] * back(0)
    for k in range(3):
        y = y + w[k:k + 1, :] * back(3 - k)
    return y


def _conv_bwd_x(ahead, w):
    dx = w[3:4, :] * ahead(0)
    for k in range(3):
        dx = dx + w[k:k + 1, :] * ahead(3 - k)
    return dx


def _conv_bwd_w(dy, back):
    dws = [jnp.sum(dy * back(3 - k), axis=0, keepdims=True) for k in range(4)]
    return jnp.concatenate(dws, axis=0), jnp.sum(dy, axis=0, keepdims=True)


def _chunks(fn, unrolled=False):
    if unrolled:
        for c in range(NCH):
            fn(c * Q)
        return

    def step(c, carry):
        fn(pl.multiple_of(c * Q, Q))
        return carry

    lax.fori_loop(0, NCH, step, 0)


HALF = RC // 2


def _col_tiles(n, tn, fn):
    def step(j, carry):
        fn(pl.multiple_of(j * tn, tn))
        return carry

    lax.fori_loop(0, n // tn, step, 0)


def _rows_spec(cols, block_col=0):
    return _spec((RC, cols), lambda i: (i, block_col))


def _whole(shape):
    return _spec(shape, lambda i: tuple(0 for _ in shape), single=True)


def _vec(cols):
    return _spec((1, cols), lambda i: (0, 0))


def _zero_at_first(*refs):
    @pl.when(pl.program_id(0) == 0)
    def _():
        for r in refs:
            r[...] = jnp.zeros_like(r)


IN_RUNS = ((PZ, 0, 1024), (PG, 2576, 2048), (PXBC, 1024, XBC))


def _in_tiles(fn):
    for pcol, wrow, width in IN_RUNS:
        def step(j, carry, pcol=pcol, wrow=wrow):
            fn(pl.multiple_of(pcol + j * 512, 512), pl.multiple_of(wrow + j * 512, 16))
            return carry

        lax.fori_loop(0, width // 512, step, 0)


def in_proj(h0, wn, w_t, w_dt):
    def body(h_ref, wn_ref, w_ref, wdt_ref, o_ref, dt_ref, u_ref):
        for r in (0, HALF):
            u_ref[r:r + HALF, :] = _rms(h_ref[r:r + HALF, :], wn_ref[...]).astype(BF)

        def tile(pcol, wrow):
            o_ref[:, pl.ds(pcol, 512)] = lax.dot_general(u_ref[...], w_ref[pl.ds(wrow, 512), :], NT_DIMS, preferred_element_type=F32).astype(BF)

        _in_tiles(tile)
        dt_ref[...] = lax.dot_general(u_ref[...], wdt_ref[...], NT_DIMS, preferred_element_type=F32)

    return pl.pallas_call(
        body, grid=(T // RC,), in_specs=[_rows_spec(D), _vec(D), _whole((IN_COLS, D)), _whole((256, D))],
        out_specs=[_rows_spec(NP_IN), _rows_spec(256), _rows_spec(D)],
        out_shape=[jax.ShapeDtypeStruct((T, NP_IN), BF), jax.ShapeDtypeStruct((T, 256), F32), jax.ShapeDtypeStruct((T, D), BF)],
        compiler_params=_params(), name="in_proj")(h0, wn, w_t, w_dt)


def out_proj(yn_ssd, proj, hseq, lru_nw, w_out, h0):
    def body(y_ref, g_ref, h_ref, wn_ref, w_ref, r_ref, o_ref, cat_ref):
        cat_ref[:, 0:SSD_W] = y_ref[...]
        for r in (0, HALF):
            y = _gelu(g_ref[r:r + HALF, :].astype(F32)) * h_ref[r:r + HALF, :]
            cat_ref[r:r + HALF, SSD_W:] = _rms(y, wn_ref[...]).astype(BF)

        def tile(c0):
            o_ref[:, pl.ds(c0, 512)] = r_ref[:, pl.ds(c0, 512)] + jnp.dot(cat_ref[...], w_ref[:, pl.ds(c0, 512)], preferred_element_type=F32)

        _col_tiles(D, 512, tile)

    return pl.pallas_call(
        body, grid=(T // RC,),
        in_specs=[_rows_spec(SSD_W), _rows_spec(LRU_W, PG // LRU_W), _rows_spec(LRU_W), _vec(LRU_W), _whole((SSD_W + LRU_W, D)), _rows_spec(D)],
        out_specs=[_rows_spec(D), _rows_spec(SSD_W + LRU_W)],
        out_shape=[jax.ShapeDtypeStruct((T, D), F32), jax.ShapeDtypeStruct((T, SSD_W + LRU_W), BF)],
        compiler_params=_params(), name="out_proj")(yn_ssd, proj, hseq, lru_nw, w_out, h0)


def out_proj_bwd(dh1_b, w_out, cat, proj, hseq, lru_nw):
    def body(d_ref, w_ref, cat_ref, g_ref, h_ref, wn_ref, dy_ref, dh_ref, dg_ref, dw_ref, gw_ref, dl_scr, gw_scr):
        _zero_at_first(dw_ref, gw_scr)

        def tile(c0):
            dy_ref[:, pl.ds(c0, 512)] = lax.dot_general(d_ref[...], w_ref[pl.ds(c0, 512), :], NT_DIMS, preferred_element_type=F32)
            dl_scr[:, pl.ds(c0, 512)] = lax.dot_general(d_ref[...], w_ref[pl.ds(SSD_W + c0, 512), :], NT_DIMS, preferred_element_type=F32)

        _col_tiles(SSD_W, 512, tile)

        def wgrad(c0):
            gw_scr[pl.ds(c0, 256), :] += lax.dot_general(cat_ref[:, pl.ds(c0, 256)], d_ref[...], TN_DIMS, preferred_element_type=F32)

        _col_tiles(SSD_W + LRU_W, 256, wgrad)

        @pl.when(pl.program_id(0) == T // RC - 1)
        def _():
            gw_ref[...] = gw_scr[...].astype(BF)

        for r in (0, HALF):
            g = g_ref[r:r + HALF, :].astype(F32)
            h = h_ref[r:r + HALF, :]
            ge = _gelu(g)
            dy, dw = _rms_bwd(dl_scr[r:r + HALF, :], ge * h, wn_ref[...])
            dw_ref[...] += jnp.sum(dw, axis=0, keepdims=True)
            dh_ref[r:r + HALF, :] = dy * ge
            dg_ref[r:r + HALF, :] = (dy * h * _gelu_grad(g)).astype(BF)

    return pl.pallas_call(
        body, grid=(T // RC,),
        in_specs=[_rows_spec(D), _whole((SSD_W + LRU_W, D)), _rows_spec(SSD_W + LRU_W), _rows_spec(LRU_W, PG // LRU_W), _rows_spec(LRU_W),
                  _vec(LRU_W)],
        out_specs=[_rows_spec(SSD_W), _rows_spec(LRU_W), _rows_spec(LRU_W), _vec(LRU_W), _whole((SSD_W + LRU_W, D))],
        out_shape=[jax.ShapeDtypeStruct((T, SSD_W), F32), jax.ShapeDtypeStruct((T, LRU_W), F32), jax.ShapeDtypeStruct((T, LRU_W), BF),
                   jax.ShapeDtypeStruct((1, LRU_W), F32), jax.ShapeDtypeStruct((SSD_W + LRU_W, D), BF)],
        scratch_shapes=[pltpu.VMEM((RC, LRU_W), F32), pltpu.VMEM((SSD_W + LRU_W, D), F32)],
        compiler_params=_params(), name="out_proj_bwd")(dh1_b, w_out, cat, proj, hseq, lru_nw)


def in_proj_bwd(dz, dg, dxl, dxbc, ddt, w_t, w_dt, h0, wn, dh1):
    first = NPAD + N_META

    def body(dz_ref, dg_ref, dxl_ref, dxbc_ref, ddt_ref, w_ref, wdt_ref, h_ref, wn_ref, r_ref, gx_hbm, meta_ref, dw_ref, du_scr, o_ref, sem):
        i = pl.program_id(0)
        _zero_at_first(dw_ref)
        du_scr[...] = jnp.dot(ddt_ref[...], wdt_ref[...], preferred_element_type=F32)
        for d_ref, wrow, width in ((dz_ref, 0, 1024), (dxbc_ref, 1024, XBC), (dg_ref, 2576, 1024), (dxl_ref, 3600, 1024)):
            def step(j, carry, d_ref=d_ref, wrow=wrow):
                c0 = pl.multiple_of(j * 512, 512)
                du_scr[...] += jnp.dot(d_ref[:, pl.ds(c0, 512)], w_ref[pl.ds(pl.multiple_of(wrow + c0, 16), 512), :], preferred_element_type=F32)
                return carry

            lax.fori_loop(0, width // 512, step, 0)
        for r in (0, HALF):
            dh, dw = _rms_bwd(du_scr[r:r + HALF, :], h_ref[r:r + HALF, :], wn_ref[...])
            dw_ref[...] += jnp.sum(dw, axis=0, keepdims=True)
            o_ref[r:r + HALF, :] = dh + r_ref[r:r + HALF, :]

        @pl.when(i == 0)
        def _():
            meta_ref[...] = o_ref[NPAD:first, :]
            head = pltpu.make_async_copy(o_ref.at[pl.ds(first, RC - first)], gx_hbm.at[pl.ds(0, RC - first)], sem)
            head.start()
            head.wait()

        @pl.when(i > 0)
        def _():
            rest = pltpu.make_async_copy(o_ref, gx_hbm.at[pl.ds(pl.multiple_of(i * RC - first, 32), RC)], sem)
            rest.start()
            rest.wait()

    return pl.pallas_call(
        body, grid=(T // RC,),
        in_specs=[_rows_spec(SSD_W), _rows_spec(LRU_W), _rows_spec(LRU_W), _rows_spec(XBC), _rows_spec(256), _whole((IN_COLS, D)),
                  _whole((256, D)), _rows_spec(D), _vec(D), _rows_spec(D)],
        out_specs=[pl.BlockSpec(memory_space=pl.ANY), _spec((N_META, D), lambda i: (0, 0)), _vec(D)],
        out_shape=[jax.ShapeDtypeStruct((SEQ, D), F32), jax.ShapeDtypeStruct((N_META, D), F32), jax.ShapeDtypeStruct((1, D), F32)],
        scratch_shapes=[pltpu.VMEM((RC, D), F32), pltpu.VMEM((RC, D), F32), pltpu.SemaphoreType.DMA],
        compiler_params=_params(), name="in_proj_bwd")(dz, dg, dxl, dxbc, ddt, w_t, w_dt, h0, wn, dh1)


def in_proj_wgrad(parts, u1):
    tm = 256
    tiles = [p.shape[1] // tm for p in parts]
    starts = [sum(tiles[:k]) for k in range(len(parts))]

    def body(*refs):
        a_refs, u_ref, o_ref = refs[:len(parts)], refs[len(parts)], refs[len(parts) + 1]
        step = pl.program_id(0)
        for a_ref, start, n in zip(a_refs, starts, tiles):
            @pl.when((step >= start) & (step < start + n))
            def _(a_ref=a_ref):
                o_ref[...] = lax.dot_general(a_ref[...], u_ref[...], TN_DIMS, preferred_element_type=F32).astype(BF)

    def tile_of(start, n):
        return lambda j: (0, jnp.clip(j - start, 0, n - 1))

    return pl.pallas_call(
        body, grid=(sum(tiles),),
        in_specs=[_spec((T, tm), tile_of(s, n)) for s, n in zip(starts, tiles)] + [_spec((T, D), lambda j: (0, 0), single=True)],
        out_specs=_spec((tm, D), lambda j: (j, 0)),
        out_shape=jax.ShapeDtypeStruct((tm * sum(tiles), D), BF),
        compiler_params=_params(), name="in_proj_wgrad")(*parts, u1)


def conv_silu_fwd(proj, cw, cb):
    def body(x_ref, w_ref, b_ref, o_ref, xpad):
        _fill_padded(xpad, x_ref)

        def chunk(r0):
            pre = _conv(_back(xpad, r0), w_ref[...], b_ref[...])
            o_ref[pl.ds(r0, Q), :] = pre * _sigmoid(pre)

        _chunks(chunk)

    c0 = PXBC // 128
    return pl.pallas_call(
        body, grid=(XBC // 128,),
        in_specs=[_spec((T, 128), lambda c: (0, c0 + c)), _spec((4, 128), lambda c: (0, c)), _spec((1, 128), lambda c: (0, c))],
        out_specs=_spec((T, 128), lambda c: (0, c)),
        out_shape=jax.ShapeDtypeStruct((T, XBC), F32), scratch_shapes=[pltpu.VMEM((T + 2 * HALO, 128), F32)],
        compiler_params=_params(), name="conv_silu_fwd")(proj, cw, cb)


def conv_silu_bwd(dx, d_b, d_c, proj, cw, cb):
    def body(dx_ref, db_ref, dc_ref, x_ref, w_ref, b_ref, o_ref, dw_ref, dbias_ref, xpad, dpad):
        tile = pl.program_id(0)
        _fill_padded(xpad, x_ref)
        dpad[0:HALO, :] = jnp.zeros((HALO, 128), F32)
        dpad[T + HALO:T + 2 * HALO, :] = jnp.zeros((HALO, 128), F32)
        dw_ref[...] = jnp.zeros_like(dw_ref)
        dbias_ref[...] = jnp.zeros_like(dbias_ref)

        def first(r0):
            back = _back(xpad, r0)
            pre = _conv(back, w_ref[...], b_ref[...])
            sg = _sigmoid(pre)
            rows = pl.ds(r0, Q)
            d = jnp.where(tile < 8, dx_ref[rows, :], jnp.where(tile < 10, db_ref[rows, :], dc_ref[rows, :]))
            dpre = d * (sg * (1.0 + pre * (1.0 - sg)))
            dpad[pl.ds(r0 + HALO, Q), :] = dpre
            dw, dbias = _conv_bwd_w(dpre, back)
            dw_ref[...] += dw
            dbias_ref[...] += dbias

        _chunks(first)

        def second(r0):
            o_ref[pl.ds(r0, Q), :] = _conv_bwd_x(_ahead(dpad, r0), w_ref[...]).astype(BF)

        _chunks(second)

    c0 = PXBC // 128
    pad = pltpu.VMEM((T + 2 * HALO, 128), F32)
    return pl.pallas_call(
        body, grid=(XBC // 128,),
        in_specs=[_spec((T, 128), lambda c: (0, jnp.minimum(c, 7))), _spec((T, 128), lambda c: (0, jnp.clip(c - 8, 0, 1))),
                  _spec((T, 128), lambda c: (0, jnp.clip(c - 10, 0, 1))), _spec((T, 128), lambda c: (0, c0 + c)),
                  _spec((4, 128), lambda c: (0, c)), _spec((1, 128), lambda c: (0, c))],
        out_specs=[_spec((T, 128), lambda c: (0, c)), _spec((4, 128), lambda c: (0, c)), _spec((1, 128), lambda c: (0, c))],
        out_shape=[jax.ShapeDtypeStruct((T, XBC), BF), jax.ShapeDtypeStruct((4, XBC), F32), jax.ShapeDtypeStruct((1, XBC), F32)],
        scratch_shapes=[pad, pad], compiler_params=_params(), name="conv_silu_bwd")(dx, d_b, d_c, proj, cw, cb)


def _ssd_chunk_common(row0, dt_ref, b_ref, c_ref, bias, a_neg):
    shape = (Q, Q)
    lane = _lanes(shape)
    sub = _rows(shape)
    live = (_rows(shape, row0) >= NPAD) & (lane < 8)
    dtr = dt_ref[:, :]
    dt = jnp.where(live, _softplus(dtr + bias), 0.0)
    d_a = dt * a_neg
    tri = (sub >= lane).astype(F32)
    cs = jnp.dot(tri, d_a, precision=lax.Precision.HIGHEST, preferred_element_type=F32)
    cs_t = cs.T
    b_f = b_ref[:, :]
    bc = b_f.astype(BF)
    cc = c_ref[:, :].astype(BF)
    cb = lax.dot_general(cc, bc, NT_DIMS, preferred_element_type=F32)
    cs_last = cs[Q - 1:Q, :]
    return dict(lane=lane, sub=sub, live=live, dtr=dtr, dt=dt, cs=cs, cs_t=cs_t, bc=bc, cc=cc, cb=cb, bc_t=b_f.T.astype(BF),
                ecs=jnp.exp(cs), dsm=jnp.exp(cs_last - cs), gam=jnp.exp(cs_last))


def _pair(lane_even, mat, j):
    return jnp.where(lane_even, mat[:, j:j + 1], mat[:, j + 1:j + 2])


def _pair_row(lane_even, mat, j):
    return jnp.where(lane_even[0:1, :], mat[:, j:j + 1], mat[:, j + 1:j + 2])


def _head_decay(cm, j):
    seg = cm["cs"][:, j:j + 1] - cm["cs_t"][j:j + 1, :]
    return jnp.exp(jnp.where(cm["sub"] >= cm["lane"], seg, -jnp.inf))


def _head_decay_t(cm, j):
    seg = cm["cs_t"][j:j + 1, :] - cm["cs"][:, j:j + 1]
    return jnp.exp(jnp.where(cm["lane"] >= cm["sub"], seg, -jnp.inf))


def ssd_fwd(xbc_act, proj, dt_raw, dt_bias2, a_log2, d2, norm_w):
    def body(x_all, b_all, c_all, dt_all, z_all, bias_all, alog_all, d_all, nw_all, yn_all, y_all, hp_all, h_all):
        @pl.when(pl.program_id(0) == 0)
        def _():
            h_all[...] = jnp.zeros_like(h_all)

        for g in range(2):
            wide, thin = slice(512 * g, 512 * g + 512), slice(128 * g, 128 * g + 128)
            group(x_all.at[:, wide], b_all.at[:, thin], c_all.at[:, thin], dt_all.at[:, thin], z_all.at[:, wide], bias_all.at[g],
                  alog_all.at[g], d_all.at[g], nw_all.at[:, wide], yn_all.at[:, wide], y_all.at[:, wide], hp_all.at[g, 0], h_all.at[g])

    def group(x_ref, b_ref, c_ref, dt_ref, z_ref, bias_ref, alog_ref, d_ref, nw_ref, yn_ref, y_ref, hp_ref, h_scr):
        bias = bias_ref[...]
        a_neg = -jnp.exp(alog_ref[...])
        dsk = d_ref[...]
        cm = _ssd_chunk_common(pl.program_id(0) * Q, dt_ref, b_ref, c_ref, bias, a_neg)
        lane_even = cm["lane"] < 64
        for p in range(4):
            je, jo = 2 * p, 2 * p + 1
            xp = x_ref[:, 128 * p:128 * p + 128]
            xdt = xp * _pair(lane_even, cm["dt"], je)
            xdt_b = xdt.astype(BF)
            m_e = (cm["cb"] * _head_decay(cm, je)).astype(BF)
            m_o = (cm["cb"] * _head_decay(cm, jo)).astype(BF)
            zero = jnp.zeros_like(xdt_b)
            yd = (jnp.dot(m_e, jnp.where(lane_even, xdt_b, zero), preferred_element_type=F32)
                  + jnp.dot(m_o, jnp.where(lane_even, zero, xdt_b), preferred_element_type=F32))
            hp = h_scr[p]
            hp_ref[p] = hp
            yo = jnp.dot(cm["cc"], hp.astype(BF), preferred_element_type=F32) * _pair(lane_even, cm["ecs"], je)
            y_ref[:, 128 * p:128 * p + 128] = yd + yo + xp * _pair_row(lane_even, dsk, je)
            st = jnp.dot(cm["bc_t"], (xdt * _pair(lane_even, cm["dsm"], je)).astype(BF), preferred_element_type=F32)
            h_scr[p] = hp * _pair_row(lane_even, cm["gam"], je) + st
        zc = z_ref[:, :].astype(F32)
        gated = y_ref[:, :] * (zc * _sigmoid(zc))
        yn_ref[:, :] = _rms(gated, nw_ref[...]).astype(BF)

    par = _spec((2, 1, 128), lambda c: (0, 0, 0))
    wide = _spec((Q, SSD_W), lambda c: (c, 0))
    return pl.pallas_call(
        body, grid=(NCH,),
        in_specs=[wide, _spec((Q, 256), lambda c: (c, 4)), _spec((Q, 256), lambda c: (c, 5)), _spec((Q, 256), lambda c: (c, 0)),
                  wide, par, par, par, _spec((1, SSD_W), lambda c: (0, 0))],
        out_specs=[wide, wide, _spec((2, 1, 4, 128, 128), lambda c: (0, c, 0, 0, 0))],
        out_shape=[jax.ShapeDtypeStruct((T, SSD_W), BF), jax.ShapeDtypeStruct((T, SSD_W), F32),
                   jax.ShapeDtypeStruct((2, NCH, 4, 128, 128), F32)],
        scratch_shapes=[pltpu.VMEM((2, 4, 128, 128), F32)],
        compiler_params=_params(), name="ssd_fwd")(xbc_act, xbc_act, xbc_act, dt_raw, proj, dt_bias2, a_log2, d2, norm_w)


def ssd_bwd(dyn, xbc_act, proj, dt_raw, y_pre, h_prev, dt_bias2, a_log2, d2, norm_w):
    def body(dyn_all, x_all, b_all, c_all, dt_all, z_all, y_all, hp_all, bias_all, alog_all, d_all, nw_all,
             dz_all, dx_all, db_all, dc_all, ddt_all, dpar_all, dnw_all, dh_all, acc_all):
        @pl.when(pl.program_id(0) == 0)
        def _():
            dh_all[...] = jnp.zeros_like(dh_all)
            acc_all[...] = jnp.zeros_like(acc_all)
            dnw_all[...] = jnp.zeros_like(dnw_all)

        for g in range(2):
            wide, thin = slice(512 * g, 512 * g + 512), slice(128 * g, 128 * g + 128)
            group(dyn_all.at[:, wide], x_all.at[:, wide], b_all.at[:, thin], c_all.at[:, thin], dt_all.at[:, thin], z_all.at[:, wide],
                  y_all.at[:, wide], hp_all.at[g, 0], bias_all.at[g], alog_all.at[g], d_all.at[g], nw_all.at[:, wide],
                  dz_all.at[:, wide], dx_all.at[:, wide], db_all.at[:, thin], dc_all.at[:, thin], ddt_all.at[:, thin], dpar_all.at[g],
                  dnw_all.at[:, wide], dh_all.at[g], acc_all.at[g])

    def group(dyn_ref, x_ref, b_ref, c_ref, dt_ref, z_ref, y_ref, hp_ref, bias_ref, alog_ref, d_ref, nw_ref,
              dz_ref, dx_ref, db_ref, dc_ref, ddt_ref, dpar_ref, dnw_ref, dh_scr, acc_scr):
        ci = pl.program_id(0)
        bias = bias_ref[...]
        a_neg = -jnp.exp(alog_ref[...])
        dsk = d_ref[...]
        cm = _ssd_chunk_common((NCH - 1 - ci) * Q, dt_ref, b_ref, c_ref, bias, a_neg)
        lane, sub = cm["lane"], cm["sub"]
        lane_even = lane < 64
        cc_t = c_ref[:, :].T.astype(BF)
        cb_t = lax.dot_general(cm["bc"], cm["cc"], NT_DIMS, preferred_element_type=F32)
        zc = z_ref[:, :].astype(F32)
        yc = y_ref[:, :]
        sg = _sigmoid(zc)
        sz = zc * sg
        dgated, dnw = _rms_bwd(dyn_ref[:, :], yc * sz, nw_ref[...])
        dnw_ref[...] += jnp.sum(dnw, axis=0, keepdims=True)
        dz_ref[:, :] = (dgated * yc * (sg * (1.0 + zc * (1.0 - sg)))).astype(BF)
        dy_all = dgated * sz
        dcb = jnp.zeros((Q, Q), F32)
        dcb_t = jnp.zeros((Q, Q), F32)
        db_acc = jnp.zeros((Q, Q), F32)
        dc_acc = jnp.zeros((Q, Q), F32)
        dcs = jnp.zeros((Q, Q), F32)
        ddt = jnp.zeros((Q, Q), F32)
        for p in range(4):
            je, jo = 2 * p, 2 * p + 1
            xp = x_ref[:, 128 * p:128 * p + 128]
            dy = dy_all[:, 128 * p:128 * p + 128]
            dt_p = _pair(lane_even, cm["dt"], je)
            xdt = xp * dt_p
            xdt_b = xdt.astype(BF)
            dy_b = dy.astype(BF)
            zero = jnp.zeros_like(dy_b)
            hp = hp_ref[p]
            hp_b = hp.astype(BF)
            dh = dh_scr[p]
            dh_b = dh.astype(BF)
            acc_scr[p:p + 1, :] += jnp.sum(dy * xp, axis=0, keepdims=True)
            dxp = dy * _pair_row(lane_even, dsk, je)
            e_p = _pair(lane_even, cm["ecs"], je)
            g_p = jnp.dot(cm["cc"], hp_b, preferred_element_type=F32)
            dg_b = (dy * e_p).astype(BF)
            de = dy * g_p * e_p
            dc_acc = dc_acc + lax.dot_general(dg_b, hp_b, NT_DIMS, preferred_element_type=F32)
            dh_in = jnp.dot(cc_t, dg_b, preferred_element_type=F32)
            ds_p = _pair(lane_even, cm["dsm"], je)
            r_p = jnp.dot(cm["bc"], dh_b, preferred_element_type=F32)
            dxdt = r_p * ds_p
            tt = r_p * xdt * ds_p
            db_acc = db_acc + lax.dot_general((xdt * ds_p).astype(BF), dh_b, NT_DIMS, preferred_element_type=F32)
            dgam_m = jnp.sum(dh * hp, axis=0, keepdims=True)
            for j, even in ((je, True), (jo, False)):
                sel = lane_even if even else jnp.logical_not(lane_even)
                dy_j = jnp.where(sel, dy_b, zero)
                l_j = _head_decay(cm, j)
                l_jt = _head_decay_t(cm, j)
                m_j = cm["cb"] * l_j
                m_jt = cb_t * l_jt
                dm = lax.dot_general(dy_j, xdt_b, NT_DIMS, preferred_element_type=F32)
                dm_t = lax.dot_general(xdt_b, dy_j, NT_DIMS, preferred_element_type=F32)
                dxdt = dxdt + jnp.dot(m_jt.astype(BF), dy_j, preferred_element_type=F32)
                dcb = dcb + dm * l_j
                dcb_t = dcb_t + dm_t * l_jt
                t_j = jnp.where(sel, tt, 0.0)
                col = jnp.sum(dm * m_j - dm_t * m_jt + (jnp.where(sel, de, 0.0) - t_j), axis=1, keepdims=True)
                gam_j = cm["gam"][:, j:j + 1]
                last = (jnp.sum(jnp.sum(t_j, axis=0, keepdims=True), axis=1, keepdims=True)
                        + jnp.sum(jnp.where(sel[0:1, :], dgam_m, 0.0), axis=1, keepdims=True) * gam_j)
                col = col + jnp.where(sub[:, 0:1] == Q - 1, last, 0.0)
                dcs = dcs + jnp.where(lane == j, col, 0.0)
            dh_scr[p] = dh_in + dh * _pair_row(lane_even, cm["gam"], je)
            dx_ref[:, 128 * p:128 * p + 128] = dxp + dxdt * dt_p
            dd = dxdt * xp
            ddt = ddt + jnp.where(lane == je, jnp.sum(jnp.where(lane_even, dd, 0.0), axis=1, keepdims=True), 0.0)
            ddt = ddt + jnp.where(lane == jo, jnp.sum(jnp.where(lane_even, 0.0, dd), axis=1, keepdims=True), 0.0)
        dc_ref[:, :] = dc_acc + jnp.dot(dcb.astype(BF), cm["bc"], preferred_element_type=F32)
        db_ref[:, :] = db_acc + jnp.dot(dcb_t.astype(BF), cm["cc"], preferred_element_type=F32)
        tri_t = (sub <= lane).astype(F32)
        dd_a = jnp.dot(tri_t, dcs, precision=lax.Precision.HIGHEST, preferred_element_type=F32)
        ddt = ddt + dd_a * a_neg
        acc_scr[5:6, :] += jnp.sum(dd_a * cm["dt"], axis=0, keepdims=True)
        draw = jnp.where(cm["live"], ddt * _sigmoid_gate(cm["dtr"] + bias), 0.0)
        acc_scr[4:5, :] += jnp.sum(draw, axis=0, keepdims=True)
        ddt_ref[:, :] = draw.astype(BF)

        @pl.when(ci == NCH - 1)
        def _():
            lane1 = _lanes((1, 128))
            dd = jnp.zeros((1, 128), F32)
            for p in range(4):
                row = acc_scr[p:p + 1, :]
                dd = dd + jnp.where(lane1 == 2 * p, jnp.sum(jnp.where(lane1 < 64, row, 0.0), axis=1, keepdims=True), 0.0)
                dd = dd + jnp.where(lane1 == 2 * p + 1, jnp.sum(jnp.where(lane1 < 64, 0.0, row), axis=1, keepdims=True), 0.0)
            dpar_ref[...] = jnp.concatenate([acc_scr[4:5, :], acc_scr[5:6, :] * a_neg, dd, jnp.zeros((5, 128), F32)], axis=0)

    par = _spec((2, 1, 128), lambda c: (0, 0, 0))
    wide = _spec((Q, SSD_W), lambda c: (NCH - 1 - c, 0))
    thin = _spec((Q, 256), lambda c: (NCH - 1 - c, 0))
    vec = _spec((1, SSD_W), lambda c: (0, 0))
    return pl.pallas_call(
        body, grid=(NCH,),
        in_specs=[wide, wide, _spec((Q, 256), lambda c: (NCH - 1 - c, 4)), _spec((Q, 256), lambda c: (NCH - 1 - c, 5)),
                  _spec((Q, 256), lambda c: (NCH - 1 - c, 0)), wide, wide,
                  _spec((2, 1, 4, 128, 128), lambda c: (0, NCH - 1 - c, 0, 0, 0)), par, par, par, vec],
        out_specs=[wide, wide, thin, thin, thin, _spec((2, 8, 128), lambda c: (0, 0, 0)), vec],
        out_shape=[jax.ShapeDtypeStruct((T, SSD_W), BF), jax.ShapeDtypeStruct((T, SSD_W), F32), jax.ShapeDtypeStruct((T, 256), F32),
                   jax.ShapeDtypeStruct((T, 256), F32), jax.ShapeDtypeStruct((T, 256), BF), jax.ShapeDtypeStruct((2, 8, 128), F32),
                   jax.ShapeDtypeStruct((1, SSD_W), F32)],
        scratch_shapes=[pltpu.VMEM((2, 4, 128, 128), F32), pltpu.VMEM((2, 8, 128), F32)],
        compiler_params=_params(), name="ssd_bwd")(dyn, xbc_act, xbc_act, xbc_act, dt_raw, proj, y_pre, h_prev, dt_bias2, a_log2, d2, norm_w)


def _lru_gates(back, cw, cb, wa, ba, wx, bx, lam):
    xr = _conv(back, cw, cb)
    xr_b = xr.astype(BF)
    r = _sigmoid_gate(jnp.dot(xr_b, wa, preferred_element_type=F32) + ba)
    i = _sigmoid_gate(jnp.dot(xr_b, wx, preferred_element_type=F32) + bx)
    sp = _softplus(-lam)
    la = (-LRU_C) * r * sp
    a = jnp.exp(la)
    mult2 = -jnp.tanh(la) * (a * a + 1.0)
    return xr, xr_b, r, i, sp, a, jnp.sqrt(mult2), mult2


def lru_gates_fwd(proj, cw, cb, wa2, ba, wx2, bx, lam):
    def body(x_ref, cw_ref, cb_ref, wa_ref, ba_ref, wx_ref, bx_ref, lam_ref, a_ref, u_ref, xpad):
        _fill_padded(xpad, x_ref)

        def chunk(r0):
            xr, _, _, i, _, a, mult, _ = _lru_gates(_back(xpad, r0), cw_ref[...], cb_ref[...], wa_ref[0], ba_ref[...], wx_ref[0], bx_ref[...],
                                                 lam_ref[...])
            a_ref[pl.ds(r0, Q), :] = a
            u_ref[pl.ds(r0, Q), :] = jnp.where(_rows(a.shape, r0) >= NPAD, mult * (i * xr), 0.0)

        _chunks(chunk, unrolled=True)

    c0 = PXL // 128
    vec = _spec((1, 128), lambda c: (0, c))
    mat = _spec((1, 128, 128), lambda c: (c, 0, 0))
    return pl.pallas_call(
        body, grid=(8,),
        in_specs=[_spec((T, 128), lambda c: (0, c0 + c)), _spec((4, 128), lambda c: (0, c)), vec, mat, vec, mat, vec, vec],
        out_specs=[_spec((T, 128), lambda c: (0, c)), _spec((T, 128), lambda c: (0, c))],
        out_shape=[jax.ShapeDtypeStruct((T, LRU_W), F32), jax.ShapeDtypeStruct((T, LRU_W), F32)],
        scratch_shapes=[pltpu.VMEM((T + 2 * HALO, 128), F32)],
        compiler_params=_params(), name="lru_gates_fwd")(proj, cw, cb, wa2, ba, wx2, bx, lam)


def lru_scan_fwd(a, u):
    def body(a_ref, u_ref, h_ref):
        def step(i, h):
            base = pl.multiple_of(i * 8, 8)
            for k in range(8):
                h = a_ref[pl.ds(base + k, 1), :] * h + u_ref[pl.ds(base + k, 1), :]
                h_ref[pl.ds(base + k, 1), :] = h
            return h

        lax.fori_loop(0, T // 8, step, jnp.zeros((1, LRU_W), F32))

    return pl.pallas_call(body, out_shape=jax.ShapeDtypeStruct((T, LRU_W), F32), compiler_params=_params(0), name="lru_scan_fwd")(a, u)


def lru_scan_bwd(a, dh_out):
    def body(a_ref, d_ref, o_ref):
        def step(i, carry):
            base = pl.multiple_of(T - 8 - i * 8, 8)
            for k in range(7, -1, -1):
                carry = d_ref[pl.ds(base + k, 1), :] + carry
                o_ref[pl.ds(base + k, 1), :] = carry
                carry = carry * a_ref[pl.ds(base + k, 1), :]
            return carry

        lax.fori_loop(0, T // 8, step, jnp.zeros((1, LRU_W), F32))

    return pl.pallas_call(body, out_shape=jax.ShapeDtypeStruct((T, LRU_W), F32), compiler_params=_params(0), name="lru_scan_bwd")(a, dh_out)


def lru_gates_bwd(dhs, hseq, proj, cw, cb, wa2, ba, wx2, bx, lam):
    def body(dh_ref, h_ref, x_ref, cw_ref, cb_ref, wa_ref, ba_ref, wx_ref, bx_ref, lam_ref,
             dx_ref, dcw_ref, dcb_ref, dwa_ref, dba_ref, dwx_ref, dbx_ref, dlam_ref, xpad, hpad, dpad):
        _fill_padded(xpad, x_ref)
        _fill_padded(hpad, h_ref)
        dpad[0:HALO, :] = jnp.zeros((HALO, 128), F32)
        dpad[T + HALO:T + 2 * HALO, :] = jnp.zeros((HALO, 128), F32)
        for ref in (dcw_ref, dcb_ref, dwa_ref, dba_ref, dwx_ref, dbx_ref, dlam_ref):
            ref[...] = jnp.zeros_like(ref)
        lam = lam_ref[...]

        def first(r0):
            back = _back(xpad, r0)
            xr, xr_b, r, i, sp, a, mult, mult2 = _lru_gates(back, cw_ref[...], cb_ref[...], wa_ref[0], ba_ref[...], wx_ref[0], bx_ref[...], lam)
            dh = dh_ref[pl.ds(r0, Q), :]
            da = dh * _back(hpad, r0)(1)
            du = jnp.where(_rows(dh.shape, r0) >= NPAD, dh, 0.0)
            dmult = du * (i * xr)
            di = du * (mult * xr)
            dxr = du * (mult * i)
            dla = da * a - dmult * (a * a) * lax.rsqrt(mult2)
            dr = dla * ((-LRU_C) * sp)
            dlam_ref[...] += jnp.sum(dla * ((-LRU_C) * r), axis=0, keepdims=True)
            dpr = dr * r * (1.0 - r)
            dpi = di * i * (1.0 - i)
            dba_ref[...] += jnp.sum(dpr, axis=0, keepdims=True)
            dbx_ref[...] += jnp.sum(dpi, axis=0, keepdims=True)
            dpr_b = dpr.astype(BF)
            dpi_b = dpi.astype(BF)
            dxr = (dxr + lax.dot_general(dpr_b, wa_ref[0], NT_DIMS, preferred_element_type=F32)
                   + lax.dot_general(dpi_b, wx_ref[0], NT_DIMS, preferred_element_type=F32))
            dwa_ref[0] += lax.dot_general(xr_b, dpr_b, TN_DIMS, preferred_element_type=F32)
            dwx_ref[0] += lax.dot_general(xr_b, dpi_b, TN_DIMS, preferred_element_type=F32)
            dpad[pl.ds(r0 + HALO, Q), :] = dxr
            dcw, dcb = _conv_bwd_w(dxr, back)
            dcw_ref[...] += dcw
            dcb_ref[...] += dcb

        _chunks(first, unrolled=True)
        dlam_ref[...] = -dlam_ref[...] * _sigmoid_gate(-lam)

        def second(r0):
            dx_ref[pl.ds(r0, Q), :] = _conv_bwd_x(_ahead(dpad, r0), cw_ref[...]).astype(BF)

        _chunks(second)

    c0 = PXL // 128
    vec = _spec((1, 128), lambda c: (0, c))
    mat = _spec((1, 128, 128), lambda c: (c, 0, 0))
    col = _spec((T, 128), lambda c: (0, c))
    vshape = jax.ShapeDtypeStruct((1, LRU_W), F32)
    mshape = jax.ShapeDtypeStruct((8, 128, 128), F32)
    pad = pltpu.VMEM((T + 2 * HALO, 128), F32)
    return pl.pallas_call(
        body, grid=(8,),
        in_specs=[col, col, _spec((T, 128), lambda c: (0, c0 + c)), _spec((4, 128), lambda c: (0, c)), vec, mat, vec, mat, vec, vec],
        out_specs=[col, _spec((4, 128), lambda c: (0, c)), vec, mat, vec, mat, vec, vec],
        out_shape=[jax.ShapeDtypeStruct((T, LRU_W), BF), jax.ShapeDtypeStruct((4, LRU_W), F32), vshape, mshape, vshape, mshape, vshape, vshape],
        scratch_shapes=[pad, pad, pad], compiler_params=_params(), name="lru_gates_bwd")(dhs, hseq, proj, cw, cb, wa2, ba, wx2, bx, lam)


def gate_up(h1, wn, w_gate, w_up):
    def body(h_ref, wn_ref, wg_ref, wu_ref, gt_ref, up_ref, act_ref, u_ref):
        for r in (0, HALF):
            u_ref[r:r + HALF, :] = _rms(h_ref[r:r + HALF, :], wn_ref[...]).astype(BF)

        def tile(c0):
            cols = pl.ds(c0, 256)
            gt = lax.dot_general(u_ref[...], wg_ref[cols, :], NT_DIMS, preferred_element_type=F32)
            up = lax.dot_general(u_ref[...], wu_ref[cols, :], NT_DIMS, preferred_element_type=F32)
            gt_ref[:, cols] = gt.astype(BF)
            up_ref[:, cols] = up.astype(BF)
            act_ref[:, cols] = (gt * _sigmoid(gt) * up).astype(BF)

        _col_tiles(D_FF, 256, tile)

    big = jax.ShapeDtypeStruct((T, D_FF), BF)
    return pl.pallas_call(
        body, grid=(T // RC,), in_specs=[_rows_spec(D), _vec(D), _whole((D_FF, D)), _whole((D_FF, D))],
        out_specs=[_rows_spec(D_FF), _rows_spec(D_FF), _rows_spec(D_FF), _rows_spec(D)],
        out_shape=[big, big, big, jax.ShapeDtypeStruct((T, D), BF)],
        compiler_params=_params(), name="gate_up")(h1, wn, w_gate, w_up)


def down_loss(act, w_down, h1, target, wf):
    first = NPAD + N_META

    def body(a_ref, w_ref, r_ref, t_hbm, wf_ref, d_ref, db_ref, l_ref, dw_ref, h_scr, t_ref, t_sem):
        i = pl.program_id(0)
        _zero_at_first(l_ref, dw_ref)
        head = pltpu.make_async_copy(t_hbm.at[pl.ds(0, RC - first)], t_ref.at[pl.ds(first, RC - first)], t_sem)
        rest = pltpu.make_async_copy(t_hbm.at[pl.ds(pl.multiple_of(jnp.maximum(i * RC - first, 0), 32), RC)], t_ref, t_sem)

        @pl.when(i == 0)
        def _():
            t_ref[0:first, :] = jnp.zeros((first, D), F32)
            head.start()

        @pl.when(i > 0)
        def _():
            rest.start()

        def tile(c0):
            cols = pl.ds(c0, 512)
            h_scr[:, cols] = r_ref[:, cols] + jnp.dot(a_ref[...], w_ref[:, cols], preferred_element_type=F32)

        _col_tiles(D, 512, tile)

        @pl.when(i == 0)
        def _():
            head.wait()

        @pl.when(i > 0)
        def _():
            rest.wait()

        for r in (0, HALF):
            h = h_scr[r:r + HALF, :]
            live = _rows((HALF, D), i * RC + r) >= first
            err = jnp.where(live, _rms(h, wf_ref[...]) - t_ref[r:r + HALF, :], 0.0)
            l_ref[...] += 0.5 * jnp.sum(jnp.sum(err * err, axis=1, keepdims=True) * (1.0 / D), axis=0, keepdims=True)
            dh, dw = _rms_bwd(err * (1.0 / D), h, wf_ref[...])
            dw_ref[...] += jnp.sum(dw, axis=0, keepdims=True)
            d_ref[r:r + HALF, :] = dh
            db_ref[r:r + HALF, :] = dh.astype(BF)

    return pl.pallas_call(
        body, grid=(T // RC,),
        in_specs=[_rows_spec(D_FF), _whole((D_FF, D)), _rows_spec(D), pl.BlockSpec(memory_space=pl.ANY), _vec(D)],
        out_specs=[_rows_spec(D), _rows_spec(D), _spec((1, 128), lambda i: (0, 0)), _vec(D)],
        out_shape=[jax.ShapeDtypeStruct((T, D), F32), jax.ShapeDtypeStruct((T, D), BF), jax.ShapeDtypeStruct((1, 128), F32),
                   jax.ShapeDtypeStruct((1, D), F32)],
        scratch_shapes=[pltpu.VMEM((RC, D), F32), pltpu.VMEM((RC, D), F32), pltpu.SemaphoreType.DMA],
        compiler_params=_params(), name="down_loss")(act, w_down, h1, target, wf)


def swiglu_bwd(dh2_b, w_down, gt, up, act, u2):
    tn = 256

    def body(d_ref, u_ref, w_ref, gt_ref, up_ref, act_ref, dg_ref, du_ref, gd_ref, gg_ref, gu_ref, acc_d, acc_g, acc_u):
        for acc in (acc_d, acc_g, acc_u):
            acc[...] = jnp.zeros_like(acc)

        def rows(r0):
            part = pl.ds(r0, RC)
            d = d_ref[part, :]
            dact = lax.dot_general(d, w_ref[...], NT_DIMS, preferred_element_type=F32)
            gt_ = gt_ref[part, :].astype(F32)
            up_ = up_ref[part, :].astype(F32)
            sg = _sigmoid(gt_)
            dgt = (dact * up_ * (sg * (1.0 + gt_ * (1.0 - sg)))).astype(BF)
            dup = (dact * (gt_ * sg)).astype(BF)
            dg_ref[part, :] = dgt
            du_ref[part, :] = dup
            u = u_ref[part, :]
            acc_d[...] += lax.dot_general(act_ref[part, :], d, TN_DIMS, preferred_element_type=F32)
            acc_g[...] += lax.dot_general(dgt, u, TN_DIMS, preferred_element_type=F32)
            acc_u[...] += lax.dot_general(dup, u, TN_DIMS, preferred_element_type=F32)

        _col_tiles(T, RC, rows)
        gd_ref[...] = acc_d[...].astype(BF)
        gg_ref[...] = acc_g[...].astype(BF)
        gu_ref[...] = acc_u[...].astype(BF)

    resident = _spec((T, D), lambda j: (0, 0), single=True)
    cols = _spec((T, tn), lambda j: (0, j))
    wrow = _spec((tn, D), lambda j: (j, 0))
    big = jax.ShapeDtypeStruct((T, D_FF), BF)
    grad = jax.ShapeDtypeStruct((D_FF, D), BF)
    return pl.pallas_call(
        body, grid=(D_FF // tn,), in_specs=[resident, resident, wrow, cols, cols, cols],
        out_specs=[cols, cols, wrow, wrow, wrow], out_shape=[big, big, grad, grad, grad],
        scratch_shapes=[pltpu.VMEM((tn, D), F32)] * 3,
        compiler_params=_params(), name="swiglu_bwd")(dh2_b, u2, w_down, gt, up, act)


def gate_up_bwd(dgt, dup, w_gate, w_up, h1, wn, dh2):
    def body(dg_ref, du_ref, wg_ref, wu_ref, h_ref, wn_ref, r_ref, d_ref, db_ref, dw_ref, du_scr):
        _zero_at_first(dw_ref)

        du_scr[...] = jnp.zeros_like(du_scr)

        def tile(c0):
            k = pl.ds(c0, 256)
            du_scr[...] += (jnp.dot(dg_ref[:, k], wg_ref[k, :], preferred_element_type=F32)
                            + jnp.dot(du_ref[:, k], wu_ref[k, :], preferred_element_type=F32))

        _col_tiles(D_FF, 256, tile)
        for r in (0, HALF):
            dh, dw = _rms_bwd(du_scr[r:r + HALF, :], h_ref[r:r + HALF, :], wn_ref[...])
            dw_ref[...] += jnp.sum(dw, axis=0, keepdims=True)
            dh = dh + r_ref[r:r + HALF, :]
            d_ref[r:r + HALF, :] = dh
            db_ref[r:r + HALF, :] = dh.astype(BF)

    return pl.pallas_call(
        body, grid=(T // RC,),
        in_specs=[_rows_spec(D_FF), _rows_spec(D_FF), _whole((D_FF, D)), _whole((D_FF, D)), _rows_spec(D), _vec(D), _rows_spec(D)],
        out_specs=[_rows_spec(D), _rows_spec(D), _vec(D)],
        out_shape=[jax.ShapeDtypeStruct((T, D), F32), jax.ShapeDtypeStruct((T, D), BF), jax.ShapeDtypeStruct((1, D), F32)],
        scratch_shapes=[pltpu.VMEM((RC, D), F32)],
        compiler_params=_params(), name="gate_up_bwd")(dgt, dup, w_gate, w_up, h1, wn, dh2)


def _adamw(w, g, m, v):
    m = ADAM_B1 * m + (1.0 - ADAM_B1) * g
    v = ADAM_B2 * v + (1.0 - ADAM_B2) * (g * g)
    m_hat = m / (1.0 - ADAM_B1 ** ADAM_STEP)
    v_hat = v / (1.0 - ADAM_B2 ** ADAM_STEP)
    delta = -ADAM_LR * (m_hat / (jnp.sqrt(v_hat) + ADAM_EPS) + ADAM_WD * w)
    return delta, m, v


def adamw_shards(name, recvs, ws, ms, vs):
    n = len(ws)

    def body(*refs):
        ins, outs = refs[:4 * n], refs[4 * n:]
        for k in range(n):
            p_ref, w_ref, m_ref, v_ref = ins[k], ins[n + k], ins[2 * n + k], ins[3 * n + k]
            g = p_ref[0].astype(F32)
            for s in range(1, 8):
                g = g + p_ref[s].astype(F32)
            outs[4 * k][...] = g
            outs[4 * k + 1][...], outs[4 * k + 2][...], outs[4 * k + 3][...] = _adamw(w_ref[...], g, m_ref[...], v_ref[...])

    tiles = [_spec((w.shape[0] // 2, w.shape[1]), lambda i: (i, 0)) for w in ws]
    recv_tiles = [_spec((8, w.shape[0] // 2, w.shape[1]), lambda i: (0, i, 0)) for w in ws]
    res = pl.pallas_call(
        body, grid=(2,), in_specs=recv_tiles + tiles * 3,
        out_specs=[t for t in tiles for _ in range(4)],
        out_shape=[jax.ShapeDtypeStruct(w.shape, F32) for w in ws for _ in range(4)],
        compiler_params=_params(), name=name)(*recvs, *ws, *ms, *vs)
    return [list(res[4 * k:4 * k + 4]) for k in range(n)]


def adamw_w_in(recv, w, m, v):
    rows = w.shape[0] // 8

    def body(p_ref, w_ref, m_ref, v_ref, g_ref, d_ref, mo_ref, vo_ref):
        for q in range(8):
            cols = slice(128 * q, 128 * q + 128)
            g = p_ref[0, :, cols].astype(F32)
            for s in range(1, 8):
                g = g + p_ref[s, :, cols].astype(F32)
            part = pl.ds(q, rows, stride=8)
            g_ref[part, :] = g
            d_ref[part, :], mo_ref[part, :], vo_ref[part, :] = _adamw(w_ref[part, :], g, m_ref[part, :], v_ref[part, :])

    shape = jax.ShapeDtypeStruct(w.shape, F32)
    return pl.pallas_call(body, out_shape=[shape] * 4, compiler_params=_params(0), name="adamw_w_in")(recv, w, m, v)


def sum_slabs(recv):
    def body(p_ref, o_ref):
        g = p_ref[0]
        for s in range(1, 8):
            g = g + p_ref[s]
        o_ref[...] = g

    return pl.pallas_call(body, out_shape=jax.ShapeDtypeStruct(recv.shape[1:], F32), compiler_params=_params(0), name="sum_slabs")(recv)


SIMPLE = [("norm1_w", 1024), ("ssd_conv_b", 1536), ("ssd_dt_bias", 16), ("ssd_a_log", 16), ("ssd_d", 16), ("ssd_norm_w", 1024),
          ("lru_conv_b", 1024), ("lru_ba", 1024), ("lru_bx", 1024), ("lru_lambda", 1024), ("lru_norm_w", 1024), ("norm2_w", 1024),
          ("final_norm_w", 1024)]
SPECIAL = ["lru_wa", "lru_wx", "meta_tokens", "ssd_conv_w", "lru_conv_w"]
SM_ROWS = 176
SM_WA, SM_WX, SM_META, SM_SCW, SM_LCW, SM_LOSS = 14, 78, 142, 158, 166, 170


def _simple_rows():
    rows, r = {}, 0
    for name, n in SIMPLE:
        rows[name] = r
        r += -(-n // 1024)
    return rows


def adamw_small(sm, special_g, ws, ms, vs):
    rows = _simple_rows()
    ns, nx = len(SIMPLE), len(SPECIAL)

    def body(*refs):
        sm_ref = refs[0]
        gx = refs[1:1 + nx]
        wr = refs[1 + nx:1 + nx + ns + nx]
        mr = refs[1 + nx + ns + nx:1 + nx + 2 * (ns + nx)]
        vr = refs[1 + nx + 2 * (ns + nx):1 + nx + 3 * (ns + nx)]
        outs = refs[1 + nx + 3 * (ns + nx):]
        o = 0
        for k, (name, n) in enumerate(SIMPLE):
            r0 = rows[name]
            for c0 in range(0, n, 1024):
                wd = min(1024, n - c0)
                g = sm_ref[r0 + c0 // 1024:r0 + c0 // 1024 + 1, 0:wd]
                sl = (slice(None), slice(c0, c0 + wd))
                d, m2, v2 = _adamw(wr[k][sl], g, mr[k][sl], vr[k][sl])
                outs[o][sl] = g
                outs[o + 1][sl] = d
                outs[o + 2][sl] = m2
                outs[o + 3][sl] = v2
            o += 4
        for k in range(nx):
            d, m2, v2 = _adamw(wr[ns + k][...], gx[k][...], mr[ns + k][...], vr[ns + k][...])
            outs[o][...] = d
            outs[o + 1][...] = m2
            outs[o + 2][...] = v2
            o += 3

    out_shape = []
    for k in range(ns):
        out_shape += [jax.ShapeDtypeStruct(ws[k].shape, F32)] * 4
    for k in range(nx):
        out_shape += [jax.ShapeDtypeStruct(ws[ns + k].shape, F32)] * 3
    return pl.pallas_call(body, out_shape=out_shape, compiler_params=_params(0), name="adamw_small")(sm, *special_g, *ws, *ms, *vs)


def _place():
    return lax.axis_index("x"), lax.axis_index("y"), lax.axis_index("c")


def _index(px, py, pc):
    return 4 * px + 2 * py + pc


def all_gather(name, shards):
    n = len(shards)
    hbm = pl.BlockSpec(memory_space=pl.ANY)

    def body(*refs):
        ins, outs = refs[:n], refs[n:2 * n]
        send_sems, recv_sems, local_sems = refs[2 * n:]
        x, y, c = _place()
        me, sibling = (x, y, c), (x, y, 1 - c)
        chips = [(1 - x, y), (x, 1 - y), (1 - x, 1 - y)]

        def copy(i, k, block, to, src=None):
            dst = outs[i].at[_index(*block)]
            return pltpu.make_async_remote_copy(src_ref=dst if src is None else src, dst_ref=dst, send_sem=send_sems.at[7 * i + k],
                                                recv_sem=recv_sems.at[7 * i + k], device_id=to, device_id_type=MESH)

        mine = [pltpu.make_async_copy(ins[i], outs[i].at[_index(*me)], local_sems.at[i]) for i in range(n)]
        for cp in mine:
            cp.start()
        first = []
        for i in range(n):
            first += [copy(i, 1 + j, me, (*chip, c), src=ins[i]) for j, chip in enumerate(chips)]
            first.append(copy(i, 0, me, sibling, src=ins[i]))
        for cp in first:
            cp.start()
        passed = []
        for i in range(n):
            for j, chip in enumerate(chips):
                copy(i, 1 + j, (*chip, c), me).wait_recv()
                cp = copy(i, 4 + j, (*chip, c), sibling)
                cp.start()
                passed.append(cp)
        for i in range(n):
            copy(i, 0, sibling, me).wait_recv()
            for j, chip in enumerate(chips):
                copy(i, 4 + j, (*chip, 1 - c), me).wait_recv()
        for cp in first + passed:
            cp.wait_send()
        for cp in mine:
            cp.wait()

    return pl.pallas_call(
        body, in_specs=[hbm] * n, out_specs=[hbm] * n,
        out_shape=[jax.ShapeDtypeStruct((8,) + s.shape, s.dtype) for s in shards],
        scratch_shapes=[pltpu.SemaphoreType.DMA((7 * n,)), pltpu.SemaphoreType.DMA((7 * n,)), pltpu.SemaphoreType.DMA((n,))],
        name=name)(*shards)


HBM_SPEC = pl.BlockSpec(memory_space=pltpu.HBM)
SEM_SPEC = pl.BlockSpec(memory_space=pltpu.SEMAPHORE)
EFFECT = pltpu.SideEffectType.DATAFLOW_SIDE_EFFECTING


def _peers(x, y, c):
    return [((1 - x) if k & 4 else x, (1 - y) if k & 2 else y, (1 - c) if k & 1 else c) for k in range(1, 8)]


def _pieces(rows):
    for n in (4, 2):
        if rows % (16 * n) == 0:
            return [(r * (rows // n), rows // n) for r in range(n)]
    return [(0, rows)]


def _peer_copies(src, land, send_sems, recv_sems, k, peer, mine, slab_src):
    block = src.at[_index(*peer)] if slab_src else src
    return [pltpu.make_async_remote_copy(src_ref=block.at[pl.ds(r0, nr)], dst_ref=land.at[mine, pl.ds(r0, nr)], send_sem=send_sems.at[k],
                                         recv_sem=recv_sems.at[k], device_id=peer, device_id_type=MESH)
            for r0, nr in _pieces(block.shape[0])]


def copies_start(name, srcs, slab_src, after):
    n = len(srcs)
    zones = [jax.ShapeDtypeStruct(s.shape if slab_src else (8,) + s.shape, s.dtype) for s in srcs]
    afters = [] if after is None else [after]

    def body(*refs):
        ins, lands = refs[:n], refs[n:2 * n]
        first = 2 * n + len(afters)
        sends, recvs = refs[first:first + n], refs[first + n:first + 2 * n]
        token = refs[-1]
        x, y, c = _place()
        mine = _index(x, y, c)
        for i in range(n):
            per_peer = [_peer_copies(ins[i], lands[i], sends[i], recvs[i], k, peer, mine, slab_src) for k, peer in enumerate(_peers(x, y, c))]
            for piece in zip(*per_peer):
                for cp in piece:
                    cp.start()
        token[...] = jnp.zeros_like(token)

    sem = pltpu.SemaphoreType.DMA((7,))
    res = pl.pallas_call(
        body, name=name,
        out_shape=([sem] * (2 * n) + [pltpu.HBM(s.shape, s.dtype) for s in srcs] + [pltpu.HBM(z.shape, z.dtype) for z in zones]
                   + [jax.ShapeDtypeStruct((8, 128), F32)]),
        in_specs=[HBM_SPEC] * (2 * n) + [pl.BlockSpec(memory_space=pl.ANY)] * len(afters),
        out_specs=[SEM_SPEC] * (2 * n) + [HBM_SPEC] * (2 * n) + [pl.BlockSpec(memory_space=pltpu.VMEM)],
        input_output_aliases={i: 2 * n + i for i in range(2 * n)},
        compiler_params=pltpu.CompilerParams(has_side_effects=EFFECT),
    )(*[pltpu.with_memory_space_constraint(s, pltpu.HBM) for s in srcs],
      *[pltpu.with_memory_space_constraint(lax.empty(z.shape, z.dtype), pltpu.HBM) for z in zones], *afters)
    return [(res[i], res[n + i], res[2 * n + i], res[3 * n + i]) for i in range(n)], res[-1][0:1, 0:1]


def copies_wait(name, started, slab_src, after):
    n = len(started)

    def body(*refs):
        ins, lands = refs[:n], refs[n:2 * n]
        sends, recvs = refs[2 * n:3 * n], refs[3 * n:4 * n]
        x, y, c = _place()
        mine = _index(x, y, c)
        for i in range(n):
            for k, peer in enumerate(_peers(x, y, c)):
                arrival = pltpu.make_async_remote_copy(src_ref=ins[i].at[mine] if slab_src else ins[i], dst_ref=lands[i].at[_index(*peer)],
                                                       send_sem=sends[i].at[k], recv_sem=recvs[i].at[k], device_id=peer, device_id_type=MESH)
                arrival.wait_send()
                arrival.wait_recv()

    srcs = [s[2] for s in started]
    lands = [s[3] for s in started]
    afters = list(after) if isinstance(after, (list, tuple)) else [after]
    res = pl.pallas_call(
        body, name=name,
        out_shape=[pltpu.HBM(s.shape, s.dtype) for s in srcs] + [pltpu.HBM(z.shape, z.dtype) for z in lands],
        in_specs=[HBM_SPEC] * (2 * n) + [SEM_SPEC] * (2 * n) + [pl.BlockSpec(memory_space=pl.ANY)] * len(afters),
        out_specs=[HBM_SPEC] * (2 * n),
        input_output_aliases={i: i for i in range(2 * n)},
        compiler_params=pltpu.CompilerParams(has_side_effects=EFFECT),
    )(*srcs, *lands, *[s[0] for s in started], *[s[1] for s in started], *afters)
    me = _index(*_place())
    own = [lax.dynamic_index_in_dim(s, me, 0, keepdims=True) if slab_src else s[None] for s in res[:n]]
    return [lax.dynamic_update_slice_in_dim(z, o, me, 0) for z, o in zip(res[n:], own)]


WEIGHTS = ["meta_tokens", "norm1_w", "w_in", "ssd_conv_w", "ssd_conv_b", "ssd_dt_bias", "ssd_a_log", "ssd_d", "ssd_norm_w", "lru_conv_w",
           "lru_conv_b", "lru_wa", "lru_ba", "lru_wx", "lru_bx", "lru_lambda", "lru_norm_w", "w_out", "norm2_w", "w_gate", "w_up", "w_down",
           "final_norm_w"]
BIG = ["w_in", "w_out", "w_gate", "w_up", "w_down"]
COLUMN_SHARDED = ["w_in", "w_gate", "w_up"]


def _pair_blocks(w):
    w = w.reshape(8, 2, 64, 64)
    z = jnp.zeros((8, 64, 64), w.dtype)
    return jnp.concatenate([jnp.concatenate([w[:, 0], z], axis=2), jnp.concatenate([z, w[:, 1]], axis=2)], axis=1)


def _unpair_blocks(w2):
    return jnp.stack([w2[:, :64, :64], w2[:, 64:, 64:]], axis=1).reshape(16, 64, 64)


def _per_group(v):
    return jnp.pad(v.reshape(2, 1, 8), ((0, 0), (0, 0), (0, 120)))


def _pad_cols(v, n):
    return jnp.pad(v, ((0, 0), (0, n - v.shape[1])))


def local_step(x, target, meta, ssd_cw, lru_cw, w_in, fetch, send, p):
    z120 = jnp.zeros((120, D), BF)
    w_dt = jnp.concatenate([w_in[2560:2568], z120, w_in[2568:2576], z120], axis=0)
    bias2, alog2, d2 = _per_group(p["ssd_dt_bias"]), _per_group(p["ssd_a_log"]), _per_group(p["ssd_d"])
    wa2 = _pair_blocks(p["lru_wa"]).astype(BF)
    wx2 = _pair_blocks(p["lru_wx"]).astype(BF)
    lru = (lru_cw, p["lru_conv_b"], wa2, p["lru_ba"], wx2, p["lru_bx"], p["lru_lambda"])

    h0 = jnp.concatenate([jnp.zeros((NPAD, D), F32), meta, x], axis=0)
    proj, dt_raw, u1 = in_proj(h0, p["norm1_w"], w_in, w_dt)
    xbc_act = conv_silu_fwd(proj, ssd_cw, p["ssd_conv_b"])
    yn_ssd, y_pre, h_prev = ssd_fwd(xbc_act, proj, dt_raw, bias2, alog2, d2, p["ssd_norm_w"])
    a, u = lru_gates_fwd(proj, *lru)
    hseq = lru_scan_fwd(a, u)
    (w_out,) = fetch(["w_out"], hseq)
    h1, cat = out_proj(yn_ssd, proj, hseq, p["lru_norm_w"], w_out, h0)
    w_gate, w_up = fetch(["w_gate", "w_up"], h1)
    gt, up, act, u2 = gate_up(h1, p["norm2_w"], w_gate, w_up)
    (w_down,) = fetch(["w_down"], act)
    dh2, dh2_b, loss, d_fnw = down_loss(act, w_down, h1, target, p["final_norm_w"])

    dgt, dup, g_down, g_gate, g_up = swiglu_bwd(dh2_b, w_down, gt, up, act, u2)
    sent = send({"w_down": g_down, "w_gate": g_gate, "w_up": g_up})
    dh1, dh1_b, d_n2 = gate_up_bwd(dgt, dup, w_gate, w_up, h1, p["norm2_w"] + sent, dh2)
    dyn, dh_out, dg_b, d_lnw, g_out = out_proj_bwd(dh1_b, w_out, cat, proj, hseq, p["lru_norm_w"] + sent)
    sent = send({"w_out": g_out})

    dhs = lru_scan_bwd(a, dh_out)
    dxl_b, d_lcw, d_lcb, dwa2, d_ba, dwx2, d_bx, d_lam = lru_gates_bwd(dhs, hseq, proj, *lru)
    dz_b, dx, d_b, d_c, ddt_b, dpar, d_snw = ssd_bwd(dyn, xbc_act, proj, dt_raw, y_pre, h_prev, bias2, alog2, d2, p["ssd_norm_w"] + sent)
    dxbc_b, d_scw, d_scb = conv_silu_bwd(dx, d_b, d_c, proj, ssd_cw, p["ssd_conv_b"])
    g_p = in_proj_wgrad([dz_b, dg_b, dxl_b, dxbc_b, ddt_b], u1)
    g_in = jnp.concatenate([g_p[PZ:PZ + 1024], g_p[PXBC:PXBC + XBC], g_p[NP_IN:NP_IN + 8], g_p[NP_IN + 128:NP_IN + 136],
                            g_p[PG:PG + 1024], g_p[PXL:PXL + 1024]], axis=0)
    sent = send({"w_in": g_in})
    grad_x, d_meta, d_n1 = in_proj_bwd(dz_b, dg_b, dxl_b, dxbc_b, ddt_b, w_in, w_dt, h0, p["norm1_w"] + sent, dh1)
    small = {"norm1_w": d_n1, "ssd_conv_b": d_scb, "ssd_dt_bias": dpar[:, 0, :8].reshape(1, 16), "ssd_a_log": dpar[:, 1, :8].reshape(1, 16),
             "ssd_d": dpar[:, 2, :8].reshape(1, 16), "ssd_norm_w": d_snw, "lru_conv_b": d_lcb, "lru_ba": d_ba, "lru_bx": d_bx,
             "lru_lambda": d_lam, "lru_norm_w": d_lnw, "norm2_w": d_n2, "final_norm_w": d_fnw,
             "lru_wa": _unpair_blocks(dwa2), "lru_wx": _unpair_blocks(dwx2), "meta_tokens": d_meta,
             "ssd_conv_w": d_scw, "lru_conv_w": d_lcw}
    return loss, grad_x, small


def _pack_small(small, loss):
    rows = [_pad_cols(small[name], -(-n // 1024) * 1024).reshape(-1, 1024) for name, n in SIMPLE]
    rows += [small["lru_wa"].reshape(64, 1024), small["lru_wx"].reshape(64, 1024), small["meta_tokens"],
             _pad_cols(small["ssd_conv_w"], 2048).reshape(8, 1024), small["lru_conv_w"], _pad_cols(loss[:, 0:1], 1024)]
    sm = jnp.concatenate(rows, axis=0)
    return jnp.pad(sm, ((0, SM_ROWS - sm.shape[0]), (0, 0)))


def _slabs(g):
    return g.reshape(8, g.shape[0] // 8, g.shape[1])


def _unslab(g):
    return g.reshape(8 * g.shape[1], g.shape[2])


def kernel(x, meta_tokens, norm1_w, w_in, ssd_conv_w, ssd_conv_b, ssd_dt_bias, ssd_a_log, ssd_d, ssd_norm_w, lru_conv_w, lru_conv_b, lru_wa, lru_ba, lru_wx, lru_bx, lru_lambda, lru_norm_w, w_out, norm2_w, w_gate, w_up, w_down, final_norm_w, loss_target, m_meta_tokens, m_norm1_w, m_w_in, m_ssd_conv_w, m_ssd_conv_b, m_ssd_dt_bias, m_ssd_a_log, m_ssd_d, m_ssd_norm_w, m_lru_conv_w, m_lru_conv_b, m_lru_wa, m_lru_ba, m_lru_wx, m_lru_bx, m_lru_lambda, m_lru_norm_w, m_w_out, m_norm2_w, m_w_gate, m_w_up, m_w_down, m_final_norm_w, v_meta_tokens, v_norm1_w, v_w_in, v_ssd_conv_w, v_ssd_conv_b, v_ssd_dt_bias, v_ssd_a_log, v_ssd_d, v_ssd_norm_w, v_lru_conv_w, v_lru_conv_b, v_lru_wa, v_lru_ba, v_lru_wx, v_lru_bx, v_lru_lambda, v_lru_norm_w, v_w_out, v_norm2_w, v_w_gate, v_w_up, v_w_down, v_final_norm_w):
    w = dict(meta_tokens=meta_tokens, norm1_w=norm1_w, w_in=w_in[0], ssd_conv_w=ssd_conv_w[0], ssd_conv_b=ssd_conv_b, ssd_dt_bias=ssd_dt_bias,
             ssd_a_log=ssd_a_log, ssd_d=ssd_d, ssd_norm_w=ssd_norm_w, lru_conv_w=lru_conv_w[0], lru_conv_b=lru_conv_b, lru_wa=lru_wa[0],
             lru_ba=lru_ba, lru_wx=lru_wx[0], lru_bx=lru_bx, lru_lambda=lru_lambda, lru_norm_w=lru_norm_w, w_out=w_out[0], norm2_w=norm2_w,
             w_gate=w_gate[0], w_up=w_up[0], w_down=w_down[0], final_norm_w=final_norm_w.reshape(1, D))
    m = dict(meta_tokens=m_meta_tokens, norm1_w=m_norm1_w, w_in=m_w_in[0], ssd_conv_w=m_ssd_conv_w[0], ssd_conv_b=m_ssd_conv_b,
             ssd_dt_bias=m_ssd_dt_bias, ssd_a_log=m_ssd_a_log, ssd_d=m_ssd_d, ssd_norm_w=m_ssd_norm_w, lru_conv_w=m_lru_conv_w[0],
             lru_conv_b=m_lru_conv_b, lru_wa=m_lru_wa[0], lru_ba=m_lru_ba, lru_wx=m_lru_wx[0], lru_bx=m_lru_bx, lru_lambda=m_lru_lambda,
             lru_norm_w=m_lru_norm_w, w_out=m_w_out[0], norm2_w=m_norm2_w, w_gate=m_w_gate[0], w_up=m_w_up[0], w_down=m_w_down[0],
             final_norm_w=m_final_norm_w.reshape(1, D))
    v = dict(meta_tokens=v_meta_tokens, norm1_w=v_norm1_w, w_in=v_w_in[0], ssd_conv_w=v_ssd_conv_w[0], ssd_conv_b=v_ssd_conv_b,
             ssd_dt_bias=v_ssd_dt_bias, ssd_a_log=v_ssd_a_log, ssd_d=v_ssd_d, ssd_norm_w=v_ssd_norm_w, lru_conv_w=v_lru_conv_w[0],
             lru_conv_b=v_lru_conv_b, lru_wa=v_lru_wa[0], lru_ba=v_lru_ba, lru_wx=v_lru_wx[0], lru_bx=v_lru_bx, lru_lambda=v_lru_lambda,
             lru_norm_w=v_lru_norm_w, w_out=v_w_out[0], norm2_w=v_norm2_w, w_gate=v_w_gate[0], w_up=v_w_up[0], w_down=v_w_down[0],
             final_norm_w=v_final_norm_w.reshape(1, D))
    shapes = dict(meta_tokens=meta_tokens.shape, norm1_w=norm1_w.shape, w_in=w_in.shape, ssd_conv_w=ssd_conv_w.shape,
                  ssd_conv_b=ssd_conv_b.shape, ssd_dt_bias=ssd_dt_bias.shape, ssd_a_log=ssd_a_log.shape, ssd_d=ssd_d.shape,
                  ssd_norm_w=ssd_norm_w.shape, lru_conv_w=lru_conv_w.shape, lru_conv_b=lru_conv_b.shape, lru_wa=lru_wa.shape,
                  lru_ba=lru_ba.shape, lru_wx=lru_wx.shape, lru_bx=lru_bx.shape, lru_lambda=lru_lambda.shape, lru_norm_w=lru_norm_w.shape,
                  w_out=w_out.shape, norm2_w=norm2_w.shape, w_gate=w_gate.shape, w_up=w_up.shape, w_down=w_down.shape,
                  final_norm_w=final_norm_w.shape)
    me = _index(*_place())
    for n in COLUMN_SHARDED:
        w[n], m[n], v[n] = w[n].T, m[n].T, v[n].T

    small_shard = jnp.concatenate([w["meta_tokens"], _pad_cols(w["ssd_conv_w"], 256).reshape(8, 128), w["lru_conv_w"],
                                   jnp.zeros((4, 128), F32)], axis=0)
    g_in, gs = all_gather("gather_w_in", [w["w_in"].astype(BF), small_shard])
    later = ["w_out", "w_gate", "w_up", "w_down"]
    started, behind = copies_start("gather_rest_start", [w[n].astype(BF) for n in later], False, gs)
    started = dict(zip(later, started))
    meta_full = gs[:, 0:16].transpose(1, 0, 2).reshape(N_META, D)
    ssd_cw = gs[:, 16:24].reshape(8, 4, 256)[:, :, :192].transpose(1, 0, 2).reshape(4, XBC)
    lru_cw = gs[:, 24:28].transpose(1, 0, 2).reshape(4, LRU_W)

    def fetch(names, after):
        got = copies_wait("gather_" + names[0] + "_wait", [started[n] for n in names], False, after)
        return [_unslab(g) for g in got]

    in_flight = {}

    def send(grads):
        names = list(grads)
        st, token = copies_start("grads_" + names[0] + "_start", [grads[n] if n == "small" else _slabs(grads[n]) for n in names], True, None)
        in_flight.update(zip(names, st))
        return token

    loss, grad_x, small = local_step(x[0], loss_target[0], meta_full, ssd_cw, lru_cw, _unslab(g_in), fetch, send,
                                     {**w, "norm1_w": w["norm1_w"] + behind})
    send({"small": _pack_small(small, loss).reshape(8, SM_ROWS // 8, 1024)})

    out = {}
    early = ["w_down", "w_gate", "w_up", "w_out"]
    recv = dict(zip(early, copies_wait("grads_early_wait", [in_flight[n] for n in early], True, in_flight["small"][2])))
    for pair in (early[:2], early[2:]):
        done = adamw_shards("adamw_" + pair[0], [recv[n] for n in pair], [w[n] for n in pair], [m[n] for n in pair], [v[n] for n in pair])
        out.update(zip(pair, done))
    recv_in, recv_small = copies_wait("grads_late_wait", [in_flight["w_in"], in_flight["small"]], True, [out[n][0] for n in early])
    untiled = (IN_COLS, 128)
    out["w_in"] = [o.reshape(IN_COLS // 8, D)
                   for o in adamw_w_in(recv_in, w["w_in"].reshape(untiled), m["w_in"].reshape(untiled), v["w_in"].reshape(untiled))]
    for n in COLUMN_SHARDED:
        out[n] = [o.T for o in out[n]]
    sm = all_gather("gather_small_grads", [sum_slabs(recv_small)])[0].reshape(SM_ROWS, 1024)
    special_g =[sm[SM_WA:SM_WA + 64].reshape(16, 64, 64), sm[SM_WX:SM_WX + 64].reshape(16, 64, 64),
                 lax.dynamic_slice(sm[SM_META:SM_META + 16], (0, 128 * me), (16, 128)),
                 lax.dynamic_slice(sm[SM_SCW:SM_SCW + 8].reshape(4, 2048), (0, 192 * me), (4, 192)),
                 lax.dynamic_slice(sm[SM_LCW:SM_LCW + 4], (0, 128 * me), (4, 128))]
    names = [n for n, _ in SIMPLE] + SPECIAL
    res = adamw_small(sm, special_g, [w[n] for n in names], [m[n] for n in names], [v[n] for n in names])
    for k, (n, _) in enumerate(SIMPLE):
        out[n] = res[4 * k:4 * k + 4]
    for k, n in enumerate(SPECIAL):
        o = 4 * len(SIMPLE) + 3 * k
        out[n] = [special_g[k]] + list(res[o:o + 3])
    loss_total = sm[SM_LOSS, 0]
    flat = [loss_total, grad_x[None]]
    for k in range(4):
        flat += [out[n][k].reshape(shapes[n]) for n in WEIGHTS]
    return tuple(flat)
```

```python
import math

import jax
import jax.numpy as jnp
from jax import lax
from jax.experimental import pallas as pl
from jax.experimental.pallas import tpu as pltpu

F32 = jnp.float32
BF = jnp.bfloat16

D = 1024
SEQ = 2048
N_META = 16
Q = 128
NPAD = 112
T = NPAD + N_META + SEQ
NCH = T // Q
RC = 544
D_FF = 2816
SSD_W = 1024
LRU_W = 1024
XBC = 1536
IN_COLS = 4624
PZ, PG, PXL, PXBC = 0, 1024, 2048, 3072
NP_IN = 4608
EPS = 1e-6
LRU_C = 8.0
VMEM_LIMIT = 56 * 1024 * 1024

ADAM_LR, ADAM_B1, ADAM_B2, ADAM_EPS, ADAM_WD, ADAM_STEP = 0.001, 0.9, 0.999, 1e-08, 0.01, 10

NT_DIMS = (((1,), (1,)), ((), ()))
TN_DIMS = (((0,), (0,)), ((), ()))
MESH = pl.DeviceIdType.MESH


def _params(n_grid=1, limit=VMEM_LIMIT):
    return pltpu.CompilerParams(dimension_semantics=("arbitrary",) * n_grid, vmem_limit_bytes=limit)


def _spec(shape, imap, single=False):
    if single:
        return pl.BlockSpec(shape, imap, pipeline_mode=pl.Buffered(1))
    return pl.BlockSpec(shape, imap)


def _sigmoid(x):
    return 0.5 * jnp.tanh(0.5 * x) + 0.5


def _sigmoid_gate(x):
    return 1.0 / (1.0 + jnp.exp(-x))


def _softplus(x):
    return jnp.maximum(x, 0.0) + jnp.log(1.0 + jnp.exp(-jnp.abs(x)))


def _rms_stats(h):
    return lax.rsqrt(jnp.mean(h * h, axis=-1, keepdims=True) + EPS)


def _rms(h, w):
    return (h * _rms_stats(h)) * w


def _rms_bwd(du, h, w):
    r = _rms_stats(h)
    n = h * r
    dn = du * w
    dh = r * (dn - n * jnp.mean(dn * n, axis=-1, keepdims=True))
    return dh, du * n


_G0 = math.sqrt(2.0 / math.pi)


def _gelu(x):
    return 0.5 * x * (1.0 + jnp.tanh(_G0 * (x + 0.044715 * (x * x * x))))


def _gelu_grad(x):
    t = jnp.tanh(_G0 * (x + 0.044715 * (x * x * x)))
    return 0.5 * (1.0 + t) + 0.5 * x * (1.0 - t * t) * (_G0 * (1.0 + 3.0 * 0.044715 * (x * x)))


def _rows(shape, r0=0):
    return lax.broadcasted_iota(jnp.int32, shape, 0) + r0


def _lanes(shape):
    return lax.broadcasted_iota(jnp.int32, shape, 1)


HALO = 8


def _fill_padded(pad_ref, x_ref):
    pad_ref[0:HALO, :] = jnp.zeros((HALO, pad_ref.shape[1]), F32)
    pad_ref[T + HALO:T + 2 * HALO, :] = jnp.zeros((HALO, pad_ref.shape[1]), F32)

    def step(c, carry):
        r0 = pl.multiple_of(c * Q, Q)
        pad_ref[pl.ds(r0 + HALO, Q), :] = x_ref[pl.ds(r0, Q), :].astype(F32)
        return carry

    lax.fori_loop(0, NCH, step, 0)


def _back(pad_ref, r0):
    win = pad_ref[pl.ds(r0, Q + HALO), :]
    return lambda s: win[HALO:, :] if s == 0 else pltpu.roll(win, s, axis=0)[HALO:, :]


def _ahead(pad_ref, r0):
    win = pad_ref[pl.ds(r0 + HALO, Q + HALO), :]
    return lambda s: win[:Q, :] if s == 0 else pltpu.roll(win, Q + HALO - s, axis=0)[:Q, :]


def _conv(back, w, b):
    y = b + w[3:4, :] * back(0)
    for k in range(3):
        y = y + w[k:k + 1, :] * back(3 - k)
    return y


def _conv_bwd_x(ahead, w):
    dx = w[3:4, :] * ahead(0)
    for k in range(3):
        dx = dx + w[k:k + 1, :] * ahead(3 - k)
    return dx


def _conv_bwd_w(dy, back):
    dws = [jnp.sum(dy * back(3 - k), axis=0, keepdims=True) for k in range(4)]
    return jnp.concatenate(dws, axis=0), jnp.sum(dy, axis=0, keepdims=True)


def _chunks(fn, unrolled=False):
    if unrolled:
        for c in range(NCH):
            fn(c * Q)
        return

    def step(c, carry):
        fn(pl.multiple_of(c * Q, Q))
        return carry

    lax.fori_loop(0, NCH, step, 0)


HALF = RC // 2


def _col_tiles(n, tn, fn):
    def step(j, carry):
        fn(pl.multiple_of(j * tn, tn))
        return carry

    lax.fori_loop(0, n // tn, step, 0)


def _rows_spec(cols, block_col=0):
    return _spec((RC, cols), lambda i: (i, block_col))


def _whole(shape):
    return _spec(shape, lambda i: tuple(0 for _ in shape), single=True)


def _vec(cols):
    return _spec((1, cols), lambda i: (0, 0))


def _zero_at_first(*refs):
    @pl.when(pl.program_id(0) == 0)
    def _():
        for r in refs:
            r[...] = jnp.zeros_like(r)


IN_RUNS = ((PZ, 0, 1024), (PG, 2576, 2048), (PXBC, 1024, XBC))


def _in_tiles(fn):
    for pcol, wrow, width in IN_RUNS:
        def step(j, carry, pcol=pcol, wrow=wrow):
            fn(pl.multiple_of(pcol + j * 512, 512), pl.multiple_of(wrow + j * 512, 16))
            return carry

        lax.fori_loop(0, width // 512, step, 0)


def in_proj(h0, wn, w_t, w_dt):
    def body(h_ref, wn_ref, w_ref, wdt_ref, o_ref, dt_ref, u_ref):
        for r in (0, HALF):
            u_ref[r:r + HALF, :] = _rms(h_ref[r:r + HALF, :], wn_ref[...]).astype(BF)

        def tile(pcol, wrow):
            o_ref[:, pl.ds(pcol, 512)] = lax.dot_general(u_ref[...], w_ref[pl.ds(wrow, 512), :], NT_DIMS, preferred_element_type=F32).astype(BF)

        _in_tiles(tile)
        dt_ref[...] = lax.dot_general(u_ref[...], wdt_ref[...], NT_DIMS, preferred_element_type=F32)

    return pl.pallas_call(
        body, grid=(T // RC,), in_specs=[_rows_spec(D), _vec(D), _whole((IN_COLS, D)), _whole((256, D))],
        out_specs=[_rows_spec(NP_IN), _rows_spec(256), _rows_spec(D)],
        out_shape=[jax.ShapeDtypeStruct((T, NP_IN), BF), jax.ShapeDtypeStruct((T, 256), F32), jax.ShapeDtypeStruct((T, D), BF)],
        compiler_params=_params(), name="in_proj")(h0, wn, w_t, w_dt)


def out_proj(yn_ssd, proj, hseq, lru_nw, w_out, h0):
    def body(y_ref, g_ref, h_ref, wn_ref, w_ref, r_ref, o_ref, cat_ref):
        cat_ref[:, 0:SSD_W] = y_ref[...]
        for r in (0, HALF):
            y = _gelu(g_ref[r:r + HALF, :].astype(F32)) * h_ref[r:r + HALF, :]
            cat_ref[r:r + HALF, SSD_W:] = _rms(y, wn_ref[...]).astype(BF)

        def tile(c0):
            o_ref[:, pl.ds(c0, 512)] = r_ref[:, pl.ds(c0, 512)] + jnp.dot(cat_ref[...], w_ref[:, pl.ds(c0, 512)], preferred_element_type=F32)

        _col_tiles(D, 512, tile)

    return pl.pallas_call(
        body, grid=(T // RC,),
        in_specs=[_rows_spec(SSD_W), _rows_spec(LRU_W, PG // LRU_W), _rows_spec(LRU_W), _vec(LRU_W), _whole((SSD_W + LRU_W, D)), _rows_spec(D)],
        out_specs=[_rows_spec(D), _rows_spec(SSD_W + LRU_W)],
        out_shape=[jax.ShapeDtypeStruct((T, D), F32), jax.ShapeDtypeStruct((T, SSD_W + LRU_W), BF)],
        compiler_params=_params(), name="out_proj")(yn_ssd, proj, hseq, lru_nw, w_out, h0)


def out_proj_bwd(dh1_b, w_out, proj, hseq, lru_nw):
    def body(d_ref, w_ref, g_ref, h_ref, wn_ref, dy_ref, dh_ref, dg_ref, dw_ref, dl_scr):
        _zero_at_first(dw_ref)

        def tile(c0):
            dy_ref[:, pl.ds(c0, 512)] = lax.dot_general(d_ref[...], w_ref[pl.ds(c0, 512), :], NT_DIMS, preferred_element_type=F32)
            dl_scr[:, pl.ds(c0, 512)] = lax.dot_general(d_ref[...], w_ref[pl.ds(SSD_W + c0, 512), :], NT_DIMS, preferred_element_type=F32)

        _col_tiles(SSD_W, 512, tile)

        for r in (0, HALF):
            g = g_ref[r:r + HALF, :].astype(F32)
            h = h_ref[r:r + HALF, :]
            ge = _gelu(g)
            dy, dw = _rms_bwd(dl_scr[r:r + HALF, :], ge * h, wn_ref[...])
            dw_ref[...] += jnp.sum(dw, axis=0, keepdims=True)
            dh_ref[r:r + HALF, :] = dy * ge
            dg_ref[r:r + HALF, :] = (dy * h * _gelu_grad(g)).astype(BF)

    return pl.pallas_call(
        body, grid=(T // RC,),
        in_specs=[_rows_spec(D), _whole((SSD_W + LRU_W, D)), _rows_spec(LRU_W, PG // LRU_W), _rows_spec(LRU_W), _vec(LRU_W)],
        out_specs=[_rows_spec(SSD_W), _rows_spec(LRU_W), _rows_spec(LRU_W), _vec(LRU_W)],
        out_shape=[jax.ShapeDtypeStruct((T, SSD_W), F32), jax.ShapeDtypeStruct((T, LRU_W), F32), jax.ShapeDtypeStruct((T, LRU_W), BF),
                   jax.ShapeDtypeStruct((1, LRU_W), F32)],
        scratch_shapes=[pltpu.VMEM((RC, LRU_W), F32)],
        compiler_params=_params(), name="out_proj_bwd")(dh1_b, w_out, proj, hseq, lru_nw)


def in_proj_bwd(dz, dg, dxl, dxbc, ddt, w_t, w_dt, h0, wn, dh1):
    first = NPAD + N_META

    def body(dz_ref, dg_ref, dxl_ref, dxbc_ref, ddt_ref, w_ref, wdt_ref, h_ref, wn_ref, r_ref, gx_hbm, meta_ref, dw_ref, du_scr, o_ref, sem):
        i = pl.program_id(0)
        _zero_at_first(dw_ref)
        du_scr[...] = jnp.dot(ddt_ref[...], wdt_ref[...], preferred_element_type=F32)
        for d_ref, wrow, width in ((dz_ref, 0, 1024), (dxbc_ref, 1024, XBC), (dg_ref, 2576, 1024), (dxl_ref, 3600, 1024)):
            def step(j, carry, d_ref=d_ref, wrow=wrow):
                c0 = pl.multiple_of(j * 512, 512)
                du_scr[...] += jnp.dot(d_ref[:, pl.ds(c0, 512)], w_ref[pl.ds(pl.multiple_of(wrow + c0, 16), 512), :], preferred_element_type=F32)
                return carry

            lax.fori_loop(0, width // 512, step, 0)
        for r in (0, HALF):
            dh, dw = _rms_bwd(du_scr[r:r + HALF, :], h_ref[r:r + HALF, :], wn_ref[...])
            dw_ref[...] += jnp.sum(dw, axis=0, keepdims=True)
            o_ref[r:r + HALF, :] = dh + r_ref[r:r + HALF, :]

        @pl.when(i == 0)
        def _():
            meta_ref[...] = o_ref[NPAD:first, :]
            head = pltpu.make_async_copy(o_ref.at[pl.ds(first, RC - first)], gx_hbm.at[pl.ds(0, RC - first)], sem)
            head.start()
            head.wait()

        @pl.when(i > 0)
        def _():
            rest = pltpu.make_async_copy(o_ref, gx_hbm.at[pl.ds(pl.multiple_of(i * RC - first, 32), RC)], sem)
            rest.start()
            rest.wait()

    return pl.pallas_call(
        body, grid=(T // RC,),
        in_specs=[_rows_spec(SSD_W), _rows_spec(LRU_W), _rows_spec(LRU_W), _rows_spec(XBC), _rows_spec(256), _whole((IN_COLS, D)),
                  _whole((256, D)), _rows_spec(D), _vec(D), _rows_spec(D)],
        out_specs=[pl.BlockSpec(memory_space=pl.ANY), _spec((N_META, D), lambda i: (0, 0)), _vec(D)],
        out_shape=[jax.ShapeDtypeStruct((SEQ, D), F32), jax.ShapeDtypeStruct((N_META, D), F32), jax.ShapeDtypeStruct((1, D), F32)],
        scratch_shapes=[pltpu.VMEM((RC, D), F32), pltpu.VMEM((RC, D), F32), pltpu.SemaphoreType.DMA],
        compiler_params=_params(), name="in_proj_bwd")(dz, dg, dxl, dxbc, ddt, w_t, w_dt, h0, wn, dh1)


def weight_grad(name, parts, u1):
    tm = 256
    tiles = [p.shape[1] // tm for p in parts]
    starts = [sum(tiles[:k]) for k in range(len(parts))]

    def body(*refs):
        a_refs, u_ref, o_ref = refs[:len(parts)], refs[len(parts)], refs[len(parts) + 1]
        step = pl.program_id(0)
        for a_ref, start, n in zip(a_refs, starts, tiles):
            @pl.when((step >= start) & (step < start + n))
            def _(a_ref=a_ref):
                o_ref[...] = lax.dot_general(a_ref[...], u_ref[...], TN_DIMS, preferred_element_type=F32).astype(BF)

    def tile_of(start, n):
        return lambda j: (0, jnp.clip(j - start, 0, n - 1))

    return pl.pallas_call(
        body, grid=(sum(tiles),),
        in_specs=[_spec((T, tm), tile_of(s, n)) for s, n in zip(starts, tiles)] + [_spec((T, D), lambda j: (0, 0), single=True)],
        out_specs=_spec((tm, D), lambda j: (j, 0)),
        out_shape=jax.ShapeDtypeStruct((tm * sum(tiles), D), BF),
        compiler_params=_params(), name=name)(*parts, u1)


def conv_silu_fwd(proj, cw, cb):
    def body(x_ref, w_ref, b_ref, o_ref, xpad):
        _fill_padded(xpad, x_ref)

        def chunk(r0):
            pre = _conv(_back(xpad, r0), w_ref[...], b_ref[...])
            o_ref[pl.ds(r0, Q), :] = pre * _sigmoid(pre)

        _chunks(chunk)

    c0 = PXBC // 128
    return pl.pallas_call(
        body, grid=(XBC // 128,),
        in_specs=[_spec((T, 128), lambda c: (0, c0 + c)), _spec((4, 128), lambda c: (0, c)), _spec((1, 128), lambda c: (0, c))],
        out_specs=_spec((T, 128), lambda c: (0, c)),
        out_shape=jax.ShapeDtypeStruct((T, XBC), F32), scratch_shapes=[pltpu.VMEM((T + 2 * HALO, 128), F32)],
        compiler_params=_params(), name="conv_silu_fwd")(proj, cw, cb)


def conv_silu_bwd(dx, d_b, d_c, proj, cw, cb):
    def body(dx_ref, db_ref, dc_ref, x_ref, w_ref, b_ref, o_ref, dw_ref, dbias_ref, xpad, dpad):
        tile = pl.program_id(0)
        _fill_padded(xpad, x_ref)
        dpad[0:HALO, :] = jnp.zeros((HALO, 128), F32)
        dpad[T + HALO:T + 2 * HALO, :] = jnp.zeros((HALO, 128), F32)
        dw_ref[...] = jnp.zeros_like(dw_ref)
        dbias_ref[...] = jnp.zeros_like(dbias_ref)

        def first(r0):
            back = _back(xpad, r0)
            pre = _conv(back, w_ref[...], b_ref[...])
            sg = _sigmoid(pre)
            rows = pl.ds(r0, Q)
            d = jnp.where(tile < 8, dx_ref[rows, :], jnp.where(tile < 10, db_ref[rows, :], dc_ref[rows, :]))
            dpre = d * (sg * (1.0 + pre * (1.0 - sg)))
            dpad[pl.ds(r0 + HALO, Q), :] = dpre
            dw, dbias = _conv_bwd_w(dpre, back)
            dw_ref[...] += dw
            dbias_ref[...] += dbias

        _chunks(first)

        def second(r0):
            o_ref[pl.ds(r0, Q), :] = _conv_bwd_x(_ahead(dpad, r0), w_ref[...]).astype(BF)

        _chunks(second)

    c0 = PXBC // 128
    pad = pltpu.VMEM((T + 2 * HALO, 128), F32)
    return pl.pallas_call(
        body, grid=(XBC // 128,),
        in_specs=[_spec((T, 128), lambda c: (0, jnp.minimum(c, 7))), _spec((T, 128), lambda c: (0, jnp.clip(c - 8, 0, 1))),
                  _spec((T, 128), lambda c: (0, jnp.clip(c - 10, 0, 1))), _spec((T, 128), lambda c: (0, c0 + c)),
                  _spec((4, 128), lambda c: (0, c)), _spec((1, 128), lambda c: (0, c))],
        out_specs=[_spec((T, 128), lambda c: (0, c)), _spec((4, 128), lambda c: (0, c)), _spec((1, 128), lambda c: (0, c))],
        out_shape=[jax.ShapeDtypeStruct((T, XBC), BF), jax.ShapeDtypeStruct((4, XBC), F32), jax.ShapeDtypeStruct((1, XBC), F32)],
        scratch_shapes=[pad, pad], compiler_params=_params(), name="conv_silu_bwd")(dx, d_b, d_c, proj, cw, cb)


def _ssd_chunk_common(row0, dt_ref, b_ref, c_ref, bias, a_neg):
    shape = (Q, Q)
    lane = _lanes(shape)
    sub = _rows(shape)
    live = (_rows(shape, row0) >= NPAD) & (lane < 8)
    dtr = dt_ref[:, :]
    dt = jnp.where(live, _softplus(dtr + bias), 0.0)
    d_a = dt * a_neg
    tri = (sub >= lane).astype(F32)
    cs = jnp.dot(tri, d_a, precision=lax.Precision.HIGHEST, preferred_element_type=F32)
    cs_t = cs.T
    b_f = b_ref[:, :]
    bc = b_f.astype(BF)
    cc = c_ref[:, :].astype(BF)
    cb = lax.dot_general(cc, bc, NT_DIMS, preferred_element_type=F32)
    cs_last = cs[Q - 1:Q, :]
    return dict(lane=lane, sub=sub, live=live, dtr=dtr, dt=dt, cs=cs, cs_t=cs_t, bc=bc, cc=cc, cb=cb, bc_t=b_f.T.astype(BF),
                ecs=jnp.exp(cs), dsm=jnp.exp(cs_last - cs), gam=jnp.exp(cs_last))


def _pair(lane_even, mat, j):
    return jnp.where(lane_even, mat[:, j:j + 1], mat[:, j + 1:j + 2])


def _pair_row(lane_even, mat, j):
    return jnp.where(lane_even[0:1, :], mat[:, j:j + 1], mat[:, j + 1:j + 2])


def _head_decay(cm, j):
    seg = cm["cs"][:, j:j + 1] - cm["cs_t"][j:j + 1, :]
    return jnp.exp(jnp.where(cm["sub"] >= cm["lane"], seg, -jnp.inf))


def _head_decay_t(cm, j):
    seg = cm["cs_t"][j:j + 1, :] - cm["cs"][:, j:j + 1]
    return jnp.exp(jnp.where(cm["lane"] >= cm["sub"], seg, -jnp.inf))


def ssd_fwd(xbc_act, proj, dt_raw, dt_bias2, a_log2, d2, norm_w):
    def body(x_all, b_all, c_all, dt_all, z_all, bias_all, alog_all, d_all, nw_all, yn_all, y_all, hp_all, h_all):
        @pl.when(pl.program_id(0) == 0)
        def _():
            h_all[...] = jnp.zeros_like(h_all)

        for g in range(2):
            wide, thin = slice(512 * g, 512 * g + 512), slice(128 * g, 128 * g + 128)
            group(x_all.at[:, wide], b_all.at[:, thin], c_all.at[:, thin], dt_all.at[:, thin], z_all.at[:, wide], bias_all.at[g],
                  alog_all.at[g], d_all.at[g], nw_all.at[:, wide], yn_all.at[:, wide], y_all.at[:, wide], hp_all.at[g, 0], h_all.at[g])

    def group(x_ref, b_ref, c_ref, dt_ref, z_ref, bias_ref, alog_ref, d_ref, nw_ref, yn_ref, y_ref, hp_ref, h_scr):
        bias = bias_ref[...]
        a_neg = -jnp.exp(alog_ref[...])
        dsk = d_ref[...]
        cm = _ssd_chunk_common(pl.program_id(0) * Q, dt_ref, b_ref, c_ref, bias, a_neg)
        lane_even = cm["lane"] < 64
        for p in range(4):
            je, jo = 2 * p, 2 * p + 1
            xp = x_ref[:, 128 * p:128 * p + 128]
            xdt = xp * _pair(lane_even, cm["dt"], je)
            xdt_b = xdt.astype(BF)
            m_e = (cm["cb"] * _head_decay(cm, je)).astype(BF)
            m_o = (cm["cb"] * _head_decay(cm, jo)).astype(BF)
            zero = jnp.zeros_like(xdt_b)
            yd = (jnp.dot(m_e, jnp.where(lane_even, xdt_b, zero), preferred_element_type=F32)
                  + jnp.dot(m_o, jnp.where(lane_even, zero, xdt_b), preferred_element_type=F32))
            hp = h_scr[p]
            hp_ref[p] = hp
            yo = jnp.dot(cm["cc"], hp.astype(BF), preferred_element_type=F32) * _pair(lane_even, cm["ecs"], je)
            y_ref[:, 128 * p:128 * p + 128] = yd + yo + xp * _pair_row(lane_even, dsk, je)
            st = jnp.dot(cm["bc_t"], (xdt * _pair(lane_even, cm["dsm"], je)).astype(BF), preferred_element_type=F32)
            h_scr[p] = hp * _pair_row(lane_even, cm["gam"], je) + st
        zc = z_ref[:, :].astype(F32)
        gated = y_ref[:, :] * (zc * _sigmoid(zc))
        yn_ref[:, :] = _rms(gated, nw_ref[...]).astype(BF)

    par = _spec((2, 1, 128), lambda c: (0, 0, 0))
    wide = _spec((Q, SSD_W), lambda c: (c, 0))
    return pl.pallas_call(
        body, grid=(NCH,),
        in_specs=[wide, _spec((Q, 256), lambda c: (c, 4)), _spec((Q, 256), lambda c: (c, 5)), _spec((Q, 256), lambda c: (c, 0)),
                  wide, par, par, par, _spec((1, SSD_W), lambda c: (0, 0))],
        out_specs=[wide, wide, _spec((2, 1, 4, 128, 128), lambda c: (0, c, 0, 0, 0))],
        out_shape=[jax.ShapeDtypeStruct((T, SSD_W), BF), jax.ShapeDtypeStruct((T, SSD_W), F32),
                   jax.ShapeDtypeStruct((2, NCH, 4, 128, 128), F32)],
        scratch_shapes=[pltpu.VMEM((2, 4, 128, 128), F32)],
        compiler_params=_params(), name="ssd_fwd")(xbc_act, xbc_act, xbc_act, dt_raw, proj, dt_bias2, a_log2, d2, norm_w)


def ssd_bwd(dyn, xbc_act, proj, dt_raw, y_pre, h_prev, dt_bias2, a_log2, d2, norm_w):
    def body(dyn_all, x_all, b_all, c_all, dt_all, z_all, y_all, hp_all, bias_all, alog_all, d_all, nw_all,
             dz_all, dx_all, db_all, dc_all, ddt_all, dpar_all, dnw_all, dh_all, acc_all):
        @pl.when(pl.program_id(0) == 0)
        def _():
            dh_all[...] = jnp.zeros_like(dh_all)
            acc_all[...] = jnp.zeros_like(acc_all)
            dnw_all[...] = jnp.zeros_like(dnw_all)

        for g in range(2):
            wide, thin = slice(512 * g, 512 * g + 512), slice(128 * g, 128 * g + 128)
            group(dyn_all.at[:, wide], x_all.at[:, wide], b_all.at[:, thin], c_all.at[:, thin], dt_all.at[:, thin], z_all.at[:, wide],
                  y_all.at[:, wide], hp_all.at[g, 0], bias_all.at[g], alog_all.at[g], d_all.at[g], nw_all.at[:, wide],
                  dz_all.at[:, wide], dx_all.at[:, wide], db_all.at[:, thin], dc_all.at[:, thin], ddt_all.at[:, thin], dpar_all.at[g],
                  dnw_all.at[:, wide], dh_all.at[g], acc_all.at[g])

    def group(dyn_ref, x_ref, b_ref, c_ref, dt_ref, z_ref, y_ref, hp_ref, bias_ref, alog_ref, d_ref, nw_ref,
              dz_ref, dx_ref, db_ref, dc_ref, ddt_ref, dpar_ref, dnw_ref, dh_scr, acc_scr):
        ci = pl.program_id(0)
        bias = bias_ref[...]
        a_neg = -jnp.exp(alog_ref[...])
        dsk = d_ref[...]
        cm = _ssd_chunk_common((NCH - 1 - ci) * Q, dt_ref, b_ref, c_ref, bias, a_neg)
        lane, sub = cm["lane"], cm["sub"]
        lane_even = lane < 64
        cc_t = c_ref[:, :].T.astype(BF)
        cb_t = lax.dot_general(cm["bc"], cm["cc"], NT_DIMS, preferred_element_type=F32)
        zc = z_ref[:, :].astype(F32)
        yc = y_ref[:, :]
        sg = _sigmoid(zc)
        sz = zc * sg
        dgated, dnw = _rms_bwd(dyn_ref[:, :], yc * sz, nw_ref[...])
        dnw_ref[...] += jnp.sum(dnw, axis=0, keepdims=True)
        dz_ref[:, :] = (dgated * yc * (sg * (1.0 + zc * (1.0 - sg)))).astype(BF)
        dy_all = dgated * sz
        dcb = jnp.zeros((Q, Q), F32)
        dcb_t = jnp.zeros((Q, Q), F32)
        db_acc = jnp.zeros((Q, Q), F32)
        dc_acc = jnp.zeros((Q, Q), F32)
        dcs = jnp.zeros((Q, Q), F32)
        ddt = jnp.zeros((Q, Q), F32)
        for p in range(4):
            je, jo = 2 * p, 2 * p + 1
            xp = x_ref[:, 128 * p:128 * p + 128]
            dy = dy_all[:, 128 * p:128 * p + 128]
            dt_p = _pair(lane_even, cm["dt"], je)
            xdt = xp * dt_p
            xdt_b = xdt.astype(BF)
            dy_b = dy.astype(BF)
            zero = jnp.zeros_like(dy_b)
            hp = hp_ref[p]
            hp_b = hp.astype(BF)
            dh = dh_scr[p]
            dh_b = dh.astype(BF)
            acc_scr[p:p + 1, :] += jnp.sum(dy * xp, axis=0, keepdims=True)
            dxp = dy * _pair_row(lane_even, dsk, je)
            e_p = _pair(lane_even, cm["ecs"], je)
            g_p = jnp.dot(cm["cc"], hp_b, preferred_element_type=F32)
            dg_b = (dy * e_p).astype(BF)
            de = dy * g_p * e_p
            dc_acc = dc_acc + lax.dot_general(dg_b, hp_b, NT_DIMS, preferred_element_type=F32)
            dh_in = jnp.dot(cc_t, dg_b, preferred_element_type=F32)
            ds_p = _pair(lane_even, cm["dsm"], je)
            r_p = jnp.dot(cm["bc"], dh_b, preferred_element_type=F32)
            dxdt = r_p * ds_p
            tt = r_p * xdt * ds_p
            db_acc = db_acc + lax.dot_general((xdt * ds_p).astype(BF), dh_b, NT_DIMS, preferred_element_type=F32)
            dgam_m = jnp.sum(dh * hp, axis=0, keepdims=True)
            for j, even in ((je, True), (jo, False)):
                sel = lane_even if even else jnp.logical_not(lane_even)
                dy_j = jnp.where(sel, dy_b, zero)
                l_j = _head_decay(cm, j)
                l_jt = _head_decay_t(cm, j)
                m_j = cm["cb"] * l_j
                m_jt = cb_t * l_jt
                dm = lax.dot_general(dy_j, xdt_b, NT_DIMS, preferred_element_type=F32)
                dm_t = lax.dot_general(xdt_b, dy_j, NT_DIMS, preferred_element_type=F32)
                dxdt = dxdt + jnp.dot(m_jt.astype(BF), dy_j, preferred_element_type=F32)
                dcb = dcb + dm * l_j
                dcb_t = dcb_t + dm_t * l_jt
                t_j = jnp.where(sel, tt, 0.0)
                col = jnp.sum(dm * m_j - dm_t * m_jt + (jnp.where(sel, de, 0.0) - t_j), axis=1, keepdims=True)
                gam_j = cm["gam"][:, j:j + 1]
                last = (jnp.sum(jnp.sum(t_j, axis=0, keepdims=True), axis=1, keepdims=True)
                        + jnp.sum(jnp.where(sel[0:1, :], dgam_m, 0.0), axis=1, keepdims=True) * gam_j)
                col = col + jnp.where(sub[:, 0:1] == Q - 1, last, 0.0)
                dcs = dcs + jnp.where(lane == j, col, 0.0)
            dh_scr[p] = dh_in + dh * _pair_row(lane_even, cm["gam"], je)
            dx_ref[:, 128 * p:128 * p + 128] = dxp + dxdt * dt_p
            dd = dxdt * xp
            ddt = ddt + jnp.where(lane == je, jnp.sum(jnp.where(lane_even, dd, 0.0), axis=1, keepdims=True), 0.0)
            ddt = ddt + jnp.where(lane == jo, jnp.sum(jnp.where(lane_even, 0.0, dd), axis=1, keepdims=True), 0.0)
        dc_ref[:, :] = dc_acc + jnp.dot(dcb.astype(BF), cm["bc"], preferred_element_type=F32)
        db_ref[:, :] = db_acc + jnp.dot(dcb_t.astype(BF), cm["cc"], preferred_element_type=F32)
        tri_t = (sub <= lane).astype(F32)
        dd_a = jnp.dot(tri_t, dcs, precision=lax.Precision.HIGHEST, preferred_element_type=F32)
        ddt = ddt + dd_a * a_neg
        acc_scr[5:6, :] += jnp.sum(dd_a * cm["dt"], axis=0, keepdims=True)
        draw = jnp.where(cm["live"], ddt * _sigmoid_gate(cm["dtr"] + bias), 0.0)
        acc_scr[4:5, :] += jnp.sum(draw, axis=0, keepdims=True)
        ddt_ref[:, :] = draw.astype(BF)

        @pl.when(ci == NCH - 1)
        def _():
            lane1 = _lanes((1, 128))
            dd = jnp.zeros((1, 128), F32)
            for p in range(4):
                row = acc_scr[p:p + 1, :]
                dd = dd + jnp.where(lane1 == 2 * p, jnp.sum(jnp.where(lane1 < 64, row, 0.0), axis=1, keepdims=True), 0.0)
                dd = dd + jnp.where(lane1 == 2 * p + 1, jnp.sum(jnp.where(lane1 < 64, 0.0, row), axis=1, keepdims=True), 0.0)
            dpar_ref[...] = jnp.concatenate([acc_scr[4:5, :], acc_scr[5:6, :] * a_neg, dd, jnp.zeros((5, 128), F32)], axis=0)

    par = _spec((2, 1, 128), lambda c: (0, 0, 0))
    wide = _spec((Q, SSD_W), lambda c: (NCH - 1 - c, 0))
    thin = _spec((Q, 256), lambda c: (NCH - 1 - c, 0))
    vec = _spec((1, SSD_W), lambda c: (0, 0))
    return pl.pallas_call(
        body, grid=(NCH,),
        in_specs=[wide, wide, _spec((Q, 256), lambda c: (NCH - 1 - c, 4)), _spec((Q, 256), lambda c: (NCH - 1 - c, 5)),
                  _spec((Q, 256), lambda c: (NCH - 1 - c, 0)), wide, wide,
                  _spec((2, 1, 4, 128, 128), lambda c: (0, NCH - 1 - c, 0, 0, 0)), par, par, par, vec],
        out_specs=[wide, wide, thin, thin, thin, _spec((2, 8, 128), lambda c: (0, 0, 0)), vec],
        out_shape=[jax.ShapeDtypeStruct((T, SSD_W), BF), jax.ShapeDtypeStruct((T, SSD_W), F32), jax.ShapeDtypeStruct((T, 256), F32),
                   jax.ShapeDtypeStruct((T, 256), F32), jax.ShapeDtypeStruct((T, 256), BF), jax.ShapeDtypeStruct((2, 8, 128), F32),
                   jax.ShapeDtypeStruct((1, SSD_W), F32)],
        scratch_shapes=[pltpu.VMEM((2, 4, 128, 128), F32), pltpu.VMEM((2, 8, 128), F32)],
        compiler_params=_params(), name="ssd_bwd")(dyn, xbc_act, xbc_act, xbc_act, dt_raw, proj, y_pre, h_prev, dt_bias2, a_log2, d2, norm_w)


def _lru_gates(back, cw, cb, wa, ba, wx, bx, lam):
    xr = _conv(back, cw, cb)
    xr_b = xr.astype(BF)
    r = _sigmoid_gate(jnp.dot(xr_b, wa, preferred_element_type=F32) + ba)
    i = _sigmoid_gate(jnp.dot(xr_b, wx, preferred_element_type=F32) + bx)
    sp = _softplus(-lam)
    la = (-LRU_C) * r * sp
    a = jnp.exp(la)
    mult2 = -jnp.tanh(la) * (a * a + 1.0)
    return xr, xr_b, r, i, sp, a, jnp.sqrt(mult2), mult2


def lru_gates_fwd(proj, cw, cb, wa2, ba, wx2, bx, lam):
    def body(x_ref, cw_ref, cb_ref, wa_ref, ba_ref, wx_ref, bx_ref, lam_ref, a_ref, u_ref, xpad):
        _fill_padded(xpad, x_ref)

        def chunk(r0):
            xr, _, _, i, _, a, mult, _ = _lru_gates(_back(xpad, r0), cw_ref[...], cb_ref[...], wa_ref[0], ba_ref[...], wx_ref[0], bx_ref[...],
                                                 lam_ref[...])
            a_ref[pl.ds(r0, Q), :] = a
            u_ref[pl.ds(r0, Q), :] = jnp.where(_rows(a.shape, r0) >= NPAD, mult * (i * xr), 0.0)

        _chunks(chunk, unrolled=True)

    c0 = PXL // 128
    vec = _spec((1, 128), lambda c: (0, c))
    mat = _spec((1, 128, 128), lambda c: (c, 0, 0))
    return pl.pallas_call(
        body, grid=(8,),
        in_specs=[_spec((T, 128), lambda c: (0, c0 + c)), _spec((4, 128), lambda c: (0, c)), vec, mat, vec, mat, vec, vec],
        out_specs=[_spec((T, 128), lambda c: (0, c)), _spec((T, 128), lambda c: (0, c))],
        out_shape=[jax.ShapeDtypeStruct((T, LRU_W), F32), jax.ShapeDtypeStruct((T, LRU_W), F32)],
        scratch_shapes=[pltpu.VMEM((T + 2 * HALO, 128), F32)],
        compiler_params=_params(), name="lru_gates_fwd")(proj, cw, cb, wa2, ba, wx2, bx, lam)


def lru_scan_fwd(a, u):
    def body(a_ref, u_ref, h_ref):
        def step(i, h):
            base = pl.multiple_of(i * 8, 8)
            for k in range(8):
                h = a_ref[pl.ds(base + k, 1), :] * h + u_ref[pl.ds(base + k, 1), :]
                h_ref[pl.ds(base + k, 1), :] = h
            return h

        lax.fori_loop(0, T // 8, step, jnp.zeros((1, LRU_W), F32))

    return pl.pallas_call(body, out_shape=jax.ShapeDtypeStruct((T, LRU_W), F32), compiler_params=_params(0), name="lru_scan_fwd")(a, u)


def lru_scan_bwd(a, dh_out):
    def body(a_ref, d_ref, o_ref):
        def step(i, carry):
            base = pl.multiple_of(T - 8 - i * 8, 8)
            for k in range(7, -1, -1):
                carry = d_ref[pl.ds(base + k, 1), :] + carry
                o_ref[pl.ds(base + k, 1), :] = carry
                carry = carry * a_ref[pl.ds(base + k, 1), :]
            return carry

        lax.fori_loop(0, T // 8, step, jnp.zeros((1, LRU_W), F32))

    return pl.pallas_call(body, out_shape=jax.ShapeDtypeStruct((T, LRU_W), F32), compiler_params=_params(0), name="lru_scan_bwd")(a, dh_out)


def lru_gates_bwd(dhs, hseq, proj, cw, cb, wa2, ba, wx2, bx, lam):
    def body(dh_ref, h_ref, x_ref, cw_ref, cb_ref, wa_ref, ba_ref, wx_ref, bx_ref, lam_ref,
             dx_ref, dcw_ref, dcb_ref, dwa_ref, dba_ref, dwx_ref, dbx_ref, dlam_ref, xpad, hpad, dpad):
        _fill_padded(xpad, x_ref)
        _fill_padded(hpad, h_ref)
        dpad[0:HALO, :] = jnp.zeros((HALO, 128), F32)
        dpad[T + HALO:T + 2 * HALO, :] = jnp.zeros((HALO, 128), F32)
        for ref in (dcw_ref, dcb_ref, dwa_ref, dba_ref, dwx_ref, dbx_ref, dlam_ref):
            ref[...] = jnp.zeros_like(ref)
        lam = lam_ref[...]

        def first(r0):
            back = _back(xpad, r0)
            xr, xr_b, r, i, sp, a, mult, mult2 = _lru_gates(back, cw_ref[...], cb_ref[...], wa_ref[0], ba_ref[...], wx_ref[0], bx_ref[...], lam)
            dh = dh_ref[pl.ds(r0, Q), :]
            da = dh * _back(hpad, r0)(1)
            du = jnp.where(_rows(dh.shape, r0) >= NPAD, dh, 0.0)
            dmult = du * (i * xr)
            di = du * (mult * xr)
            dxr = du * (mult * i)
            dla = da * a - dmult * (a * a) * lax.rsqrt(mult2)
            dr = dla * ((-LRU_C) * sp)
            dlam_ref[...] += jnp.sum(dla * ((-LRU_C) * r), axis=0, keepdims=True)
            dpr = dr * r * (1.0 - r)
            dpi = di * i * (1.0 - i)
            dba_ref[...] += jnp.sum(dpr, axis=0, keepdims=True)
            dbx_ref[...] += jnp.sum(dpi, axis=0, keepdims=True)
            dpr_b = dpr.astype(BF)
            dpi_b = dpi.astype(BF)
            dxr = (dxr + lax.dot_general(dpr_b, wa_ref[0], NT_DIMS, preferred_element_type=F32)
                   + lax.dot_general(dpi_b, wx_ref[0], NT_DIMS, preferred_element_type=F32))
            dwa_ref[0] += lax.dot_general(xr_b, dpr_b, TN_DIMS, preferred_element_type=F32)
            dwx_ref[0] += lax.dot_general(xr_b, dpi_b, TN_DIMS, preferred_element_type=F32)
            dpad[pl.ds(r0 + HALO, Q), :] = dxr
            dcw, dcb = _conv_bwd_w(dxr, back)
            dcw_ref[...] += dcw
            dcb_ref[...] += dcb

        _chunks(first, unrolled=True)
        dlam_ref[...] = -dlam_ref[...] * _sigmoid_gate(-lam)

        def second(r0):
            dx_ref[pl.ds(r0, Q), :] = _conv_bwd_x(_ahead(dpad, r0), cw_ref[...]).astype(BF)

        _chunks(second)

    c0 = PXL // 128
    vec = _spec((1, 128), lambda c: (0, c))
    mat = _spec((1, 128, 128), lambda c: (c, 0, 0))
    col = _spec((T, 128), lambda c: (0, c))
    vshape = jax.ShapeDtypeStruct((1, LRU_W), F32)
    mshape = jax.ShapeDtypeStruct((8, 128, 128), F32)
    pad = pltpu.VMEM((T + 2 * HALO, 128), F32)
    return pl.pallas_call(
        body, grid=(8,),
        in_specs=[col, col, _spec((T, 128), lambda c: (0, c0 + c)), _spec((4, 128), lambda c: (0, c)), vec, mat, vec, mat, vec, vec],
        out_specs=[col, _spec((4, 128), lambda c: (0, c)), vec, mat, vec, mat, vec, vec],
        out_shape=[jax.ShapeDtypeStruct((T, LRU_W), BF), jax.ShapeDtypeStruct((4, LRU_W), F32), vshape, mshape, vshape, mshape, vshape, vshape],
        scratch_shapes=[pad, pad, pad], compiler_params=_params(), name="lru_gates_bwd")(dhs, hseq, proj, cw, cb, wa2, ba, wx2, bx, lam)


def gate_up(h1, wn, w_gate, w_up):
    def body(h_ref, wn_ref, wg_ref, wu_ref, gt_ref, up_ref, act_ref, u_ref):
        for r in (0, HALF):
            u_ref[r:r + HALF, :] = _rms(h_ref[r:r + HALF, :], wn_ref[...]).astype(BF)

        def tile(c0):
            cols = pl.ds(c0, 256)
            gt = lax.dot_general(u_ref[...], wg_ref[cols, :], NT_DIMS, preferred_element_type=F32)
            up = lax.dot_general(u_ref[...], wu_ref[cols, :], NT_DIMS, preferred_element_type=F32)
            gt_ref[:, cols] = gt.astype(BF)
            up_ref[:, cols] = up.astype(BF)
            act_ref[:, cols] = (gt * _sigmoid(gt) * up).astype(BF)

        _col_tiles(D_FF, 256, tile)

    big = jax.ShapeDtypeStruct((T, D_FF), BF)
    return pl.pallas_call(
        body, grid=(T // RC,), in_specs=[_rows_spec(D), _vec(D), _whole((D_FF, D)), _whole((D_FF, D))],
        out_specs=[_rows_spec(D_FF), _rows_spec(D_FF), _rows_spec(D_FF), _rows_spec(D)],
        out_shape=[big, big, big, jax.ShapeDtypeStruct((T, D), BF)],
        compiler_params=_params(), name="gate_up")(h1, wn, w_gate, w_up)


def down_loss(act, w_down, h1, target, wf):
    first = NPAD + N_META

    def body(a_ref, w_ref, r_ref, t_hbm, wf_ref, d_ref, db_ref, l_ref, dw_ref, h_scr, t_ref, t_sem):
        i = pl.program_id(0)
        _zero_at_first(l_ref, dw_ref)
        head = pltpu.make_async_copy(t_hbm.at[pl.ds(0, RC - first)], t_ref.at[pl.ds(first, RC - first)], t_sem)
        rest = pltpu.make_async_copy(t_hbm.at[pl.ds(pl.multiple_of(jnp.maximum(i * RC - first, 0), 32), RC)], t_ref, t_sem)

        @pl.when(i == 0)
        def _():
            t_ref[0:first, :] = jnp.zeros((first, D), F32)
            head.start()

        @pl.when(i > 0)
        def _():
            rest.start()

        def tile(c0):
            cols = pl.ds(c0, 512)
            h_scr[:, cols] = r_ref[:, cols] + jnp.dot(a_ref[...], w_ref[:, cols], preferred_element_type=F32)

        _col_tiles(D, 512, tile)

        @pl.when(i == 0)
        def _():
            head.wait()

        @pl.when(i > 0)
        def _():
            rest.wait()

        for r in (0, HALF):
            h = h_scr[r:r + HALF, :]
            live = _rows((HALF, D), i * RC + r) >= first
            err = jnp.where(live, _rms(h, wf_ref[...]) - t_ref[r:r + HALF, :], 0.0)
            l_ref[...] += 0.5 * jnp.sum(jnp.sum(err * err, axis=1, keepdims=True) * (1.0 / D), axis=0, keepdims=True)
            dh, dw = _rms_bwd(err * (1.0 / D), h, wf_ref[...])
            dw_ref[...] += jnp.sum(dw, axis=0, keepdims=True)
            d_ref[r:r + HALF, :] = dh
            db_ref[r:r + HALF, :] = dh.astype(BF)

    return pl.pallas_call(
        body, grid=(T // RC,),
        in_specs=[_rows_spec(D_FF), _whole((D_FF, D)), _rows_spec(D), pl.BlockSpec(memory_space=pl.ANY), _vec(D)],
        out_specs=[_rows_spec(D), _rows_spec(D), _spec((1, 128), lambda i: (0, 0)), _vec(D)],
        out_shape=[jax.ShapeDtypeStruct((T, D), F32), jax.ShapeDtypeStruct((T, D), BF), jax.ShapeDtypeStruct((1, 128), F32),
                   jax.ShapeDtypeStruct((1, D), F32)],
        scratch_shapes=[pltpu.VMEM((RC, D), F32), pltpu.VMEM((RC, D), F32), pltpu.SemaphoreType.DMA],
        compiler_params=_params(), name="down_loss")(act, w_down, h1, target, wf)


def swiglu_bwd(dh2_b, w_down, gt, up, act, u2):
    tn = 256

    def body(d_ref, u_ref, w_ref, gt_ref, up_ref, act_ref, dg_ref, du_ref, gd_ref, gg_ref, gu_ref, acc_d, acc_g, acc_u):
        for acc in (acc_d, acc_g, acc_u):
            acc[...] = jnp.zeros_like(acc)

        def rows(r0):
            part = pl.ds(r0, RC)
            d = d_ref[part, :]
            dact = lax.dot_general(d, w_ref[...], NT_DIMS, preferred_element_type=F32)
            gt_ = gt_ref[part, :].astype(F32)
            up_ = up_ref[part, :].astype(F32)
            sg = _sigmoid(gt_)
            dgt = (dact * up_ * (sg * (1.0 + gt_ * (1.0 - sg)))).astype(BF)
            dup = (dact * (gt_ * sg)).astype(BF)
            dg_ref[part, :] = dgt
            du_ref[part, :] = dup
            u = u_ref[part, :]
            acc_d[...] += lax.dot_general(act_ref[part, :], d, TN_DIMS, preferred_element_type=F32)
            acc_g[...] += lax.dot_general(dgt, u, TN_DIMS, preferred_element_type=F32)
            acc_u[...] += lax.dot_general(dup, u, TN_DIMS, preferred_element_type=F32)

        _col_tiles(T, RC, rows)
        gd_ref[...] = acc_d[...].astype(BF)
        gg_ref[...] = acc_g[...].astype(BF)
        gu_ref[...] = acc_u[...].astype(BF)

    resident = _spec((T, D), lambda j: (0, 0), single=True)
    cols = _spec((T, tn), lambda j: (0, j))
    wrow = _spec((tn, D), lambda j: (j, 0))
    big = jax.ShapeDtypeStruct((T, D_FF), BF)
    grad = jax.ShapeDtypeStruct((D_FF, D), BF)
    return pl.pallas_call(
        body, grid=(D_FF // tn,), in_specs=[resident, resident, wrow, cols, cols, cols],
        out_specs=[cols, cols, wrow, wrow, wrow], out_shape=[big, big, grad, grad, grad],
        scratch_shapes=[pltpu.VMEM((tn, D), F32)] * 3,
        compiler_params=_params(), name="swiglu_bwd")(dh2_b, u2, w_down, gt, up, act)


def gate_up_bwd(dgt, dup, w_gate, w_up, h1, wn, dh2):
    def body(dg_ref, du_ref, wg_ref, wu_ref, h_ref, wn_ref, r_ref, d_ref, db_ref, dw_ref, du_scr):
        _zero_at_first(dw_ref)

        du_scr[...] = jnp.zeros_like(du_scr)

        def tile(c0):
            k = pl.ds(c0, 256)
            du_scr[...] += (jnp.dot(dg_ref[:, k], wg_ref[k, :], preferred_element_type=F32)
                            + jnp.dot(du_ref[:, k], wu_ref[k, :], preferred_element_type=F32))

        _col_tiles(D_FF, 256, tile)
        for r in (0, HALF):
            dh, dw = _rms_bwd(du_scr[r:r + HALF, :], h_ref[r:r + HALF, :], wn_ref[...])
            dw_ref[...] += jnp.sum(dw, axis=0, keepdims=True)
            dh = dh + r_ref[r:r + HALF, :]
            d_ref[r:r + HALF, :] = dh
            db_ref[r:r + HALF, :] = dh.astype(BF)

    return pl.pallas_call(
        body, grid=(T // RC,),
        in_specs=[_rows_spec(D_FF), _rows_spec(D_FF), _whole((D_FF, D)), _whole((D_FF, D)), _rows_spec(D), _vec(D), _rows_spec(D)],
        out_specs=[_rows_spec(D), _rows_spec(D), _vec(D)],
        out_shape=[jax.ShapeDtypeStruct((T, D), F32), jax.ShapeDtypeStruct((T, D), BF), jax.ShapeDtypeStruct((1, D), F32)],
        scratch_shapes=[pltpu.VMEM((RC, D), F32)],
        compiler_params=_params(), name="gate_up_bwd")(dgt, dup, w_gate, w_up, h1, wn, dh2)


def _adamw(w, g, m, v):
    m = ADAM_B1 * m + (1.0 - ADAM_B1) * g
    v = ADAM_B2 * v + (1.0 - ADAM_B2) * (g * g)
    m_hat = m / (1.0 - ADAM_B1 ** ADAM_STEP)
    v_hat = v / (1.0 - ADAM_B2 ** ADAM_STEP)
    delta = -ADAM_LR * (m_hat / (jnp.sqrt(v_hat) + ADAM_EPS) + ADAM_WD * w)
    return delta, m, v


def adamw_shards(name, recvs, ws, ms, vs):
    n = len(ws)

    def body(*refs):
        ins, outs = refs[:4 * n], refs[4 * n:]
        for k in range(n):
            p_ref, w_ref, m_ref, v_ref = ins[k], ins[n + k], ins[2 * n + k], ins[3 * n + k]
            g = p_ref[0].astype(F32)
            for s in range(1, 8):
                g = g + p_ref[s].astype(F32)
            outs[4 * k][...] = g
            outs[4 * k + 1][...], outs[4 * k + 2][...], outs[4 * k + 3][...] = _adamw(w_ref[...], g, m_ref[...], v_ref[...])

    tiles = [_spec((w.shape[0] // 2, w.shape[1]), lambda i: (i, 0)) for w in ws]
    recv_tiles = [_spec((8, w.shape[0] // 2, w.shape[1]), lambda i: (0, i, 0)) for w in ws]
    res = pl.pallas_call(
        body, grid=(2,), in_specs=recv_tiles + tiles * 3,
        out_specs=[t for t in tiles for _ in range(4)],
        out_shape=[jax.ShapeDtypeStruct(w.shape, F32) for w in ws for _ in range(4)],
        compiler_params=_params(), name=name)(*recvs, *ws, *ms, *vs)
    return [list(res[4 * k:4 * k + 4]) for k in range(n)]


def adamw_w_in(recv, w, m, v):
    rows = w.shape[0] // 8

    def body(p_ref, w_ref, m_ref, v_ref, g_ref, d_ref, mo_ref, vo_ref):
        for q in range(8):
            cols = slice(128 * q, 128 * q + 128)
            g = p_ref[0, :, cols].astype(F32)
            for s in range(1, 8):
                g = g + p_ref[s, :, cols].astype(F32)
            part = pl.ds(q, rows, stride=8)
            g_ref[part, :] = g
            d_ref[part, :], mo_ref[part, :], vo_ref[part, :] = _adamw(w_ref[part, :], g, m_ref[part, :], v_ref[part, :])

    shape = jax.ShapeDtypeStruct(w.shape, F32)
    return pl.pallas_call(body, out_shape=[shape] * 4, compiler_params=_params(0), name="adamw_w_in")(recv, w, m, v)


def sum_slabs(recv):
    def body(p_ref, o_ref):
        g = p_ref[0]
        for s in range(1, 8):
            g = g + p_ref[s]
        o_ref[...] = g

    return pl.pallas_call(body, out_shape=jax.ShapeDtypeStruct(recv.shape[1:], F32), compiler_params=_params(0), name="sum_slabs")(recv)


SIMPLE = [("norm1_w", 1024), ("ssd_conv_b", 1536), ("ssd_dt_bias", 16), ("ssd_a_log", 16), ("ssd_d", 16), ("ssd_norm_w", 1024),
          ("lru_conv_b", 1024), ("lru_ba", 1024), ("lru_bx", 1024), ("lru_lambda", 1024), ("lru_norm_w", 1024), ("norm2_w", 1024),
          ("final_norm_w", 1024)]
SPECIAL = ["lru_wa", "lru_wx", "meta_tokens", "ssd_conv_w", "lru_conv_w"]
SM_ROWS = 176
SM_WA, SM_WX, SM_META, SM_SCW, SM_LCW, SM_LOSS = 14, 78, 142, 158, 166, 170


def _simple_rows():
    rows, r = {}, 0
    for name, n in SIMPLE:
        rows[name] = r
        r += -(-n // 1024)
    return rows


def adamw_small(sm, special_g, ws, ms, vs):
    rows = _simple_rows()
    ns, nx = len(SIMPLE), len(SPECIAL)

    def body(*refs):
        sm_ref = refs[0]
        gx = refs[1:1 + nx]
        wr = refs[1 + nx:1 + nx + ns + nx]
        mr = refs[1 + nx + ns + nx:1 + nx + 2 * (ns + nx)]
        vr = refs[1 + nx + 2 * (ns + nx):1 + nx + 3 * (ns + nx)]
        outs = refs[1 + nx + 3 * (ns + nx):]
        o = 0
        for k, (name, n) in enumerate(SIMPLE):
            r0 = rows[name]
            for c0 in range(0, n, 1024):
                wd = min(1024, n - c0)
                g = sm_ref[r0 + c0 // 1024:r0 + c0 // 1024 + 1, 0:wd]
                sl = (slice(None), slice(c0, c0 + wd))
                d, m2, v2 = _adamw(wr[k][sl], g, mr[k][sl], vr[k][sl])
                outs[o][sl] = g
                outs[o + 1][sl] = d
                outs[o + 2][sl] = m2
                outs[o + 3][sl] = v2
            o += 4
        for k in range(nx):
            d, m2, v2 = _adamw(wr[ns + k][...], gx[k][...], mr[ns + k][...], vr[ns + k][...])
            outs[o][...] = d
            outs[o + 1][...] = m2
            outs[o + 2][...] = v2
            o += 3

    out_shape = []
    for k in range(ns):
        out_shape += [jax.ShapeDtypeStruct(ws[k].shape, F32)] * 4
    for k in range(nx):
        out_shape += [jax.ShapeDtypeStruct(ws[ns + k].shape, F32)] * 3
    return pl.pallas_call(body, out_shape=out_shape, compiler_params=_params(0), name="adamw_small")(sm, *special_g, *ws, *ms, *vs)


def _place():
    return lax.axis_index("x"), lax.axis_index("y"), lax.axis_index("c")


def _index(px, py, pc):
    return 4 * px + 2 * py + pc


def all_gather(name, shards):
    n = len(shards)
    hbm = pl.BlockSpec(memory_space=pl.ANY)

    def body(*refs):
        ins, outs = refs[:n], refs[n:2 * n]
        send_sems, recv_sems, local_sems = refs[2 * n:]
        x, y, c = _place()
        me, sibling = (x, y, c), (x, y, 1 - c)
        chips = [(1 - x, y), (x, 1 - y), (1 - x, 1 - y)]

        def copy(i, k, block, to, src=None):
            dst = outs[i].at[_index(*block)]
            return pltpu.make_async_remote_copy(src_ref=dst if src is None else src, dst_ref=dst, send_sem=send_sems.at[7 * i + k],
                                                recv_sem=recv_sems.at[7 * i + k], device_id=to, device_id_type=MESH)

        mine = [pltpu.make_async_copy(ins[i], outs[i].at[_index(*me)], local_sems.at[i]) for i in range(n)]
        for cp in mine:
            cp.start()
        first = []
        for i in range(n):
            first += [copy(i, 1 + j, me, (*chip, c), src=ins[i]) for j, chip in enumerate(chips)]
            first.append(copy(i, 0, me, sibling, src=ins[i]))
        for cp in first:
            cp.start()
        passed = []
        for i in range(n):
            for j, chip in enumerate(chips):
                copy(i, 1 + j, (*chip, c), me).wait_recv()
                cp = copy(i, 4 + j, (*chip, c), sibling)
                cp.start()
                passed.append(cp)
        for i in range(n):
            copy(i, 0, sibling, me).wait_recv()
            for j, chip in enumerate(chips):
                copy(i, 4 + j, (*chip, 1 - c), me).wait_recv()
        for cp in first + passed:
            cp.wait_send()
        for cp in mine:
            cp.wait()

    return pl.pallas_call(
        body, in_specs=[hbm] * n, out_specs=[hbm] * n,
        out_shape=[jax.ShapeDtypeStruct((8,) + s.shape, s.dtype) for s in shards],
        scratch_shapes=[pltpu.SemaphoreType.DMA((7 * n,)), pltpu.SemaphoreType.DMA((7 * n,)), pltpu.SemaphoreType.DMA((n,))],
        name=name)(*shards)


HBM_SPEC = pl.BlockSpec(memory_space=pltpu.HBM)
SEM_SPEC = pl.BlockSpec(memory_space=pltpu.SEMAPHORE)
EFFECT = pltpu.SideEffectType.DATAFLOW_SIDE_EFFECTING


def _peers(x, y, c):
    return [((1 - x) if k & 4 else x, (1 - y) if k & 2 else y, (1 - c) if k & 1 else c) for k in range(1, 8)]


def _pieces(rows):
    for n in (4, 2):
        if rows % (16 * n) == 0:
            return [(r * (rows // n), rows // n) for r in range(n)]
    return [(0, rows)]


def _peer_copies(src, land, send_sems, recv_sems, k, peer, mine, slab_src):
    block = src.at[_index(*peer)] if slab_src else src
    return [pltpu.make_async_remote_copy(src_ref=block.at[pl.ds(r0, nr)], dst_ref=land.at[mine, pl.ds(r0, nr)], send_sem=send_sems.at[k],
                                         recv_sem=recv_sems.at[k], device_id=peer, device_id_type=MESH)
            for r0, nr in _pieces(block.shape[0])]


def copies_start(name, srcs, slab_src, after):
    n = len(srcs)
    zones = [jax.ShapeDtypeStruct(s.shape if slab_src else (8,) + s.shape, s.dtype) for s in srcs]
    afters = [] if after is None else [after]

    def body(*refs):
        ins, lands = refs[:n], refs[n:2 * n]
        first = 2 * n + len(afters)
        sends, recvs = refs[first:first + n], refs[first + n:first + 2 * n]
        token = refs[-1]
        x, y, c = _place()
        mine = _index(x, y, c)
        for i in range(n):
            per_peer = [_peer_copies(ins[i], lands[i], sends[i], recvs[i], k, peer, mine, slab_src) for k, peer in enumerate(_peers(x, y, c))]
            for piece in zip(*per_peer):
                for cp in piece:
                    cp.start()
        token[...] = jnp.zeros_like(token)

    sem = pltpu.SemaphoreType.DMA((7,))
    res = pl.pallas_call(
        body, name=name,
        out_shape=([sem] * (2 * n) + [pltpu.HBM(s.shape, s.dtype) for s in srcs] + [pltpu.HBM(z.shape, z.dtype) for z in zones]
                   + [jax.ShapeDtypeStruct((8, 128), F32)]),
        in_specs=[HBM_SPEC] * (2 * n) + [pl.BlockSpec(memory_space=pl.ANY)] * len(afters),
        out_specs=[SEM_SPEC] * (2 * n) + [HBM_SPEC] * (2 * n) + [pl.BlockSpec(memory_space=pltpu.VMEM)],
        input_output_aliases={i: 2 * n + i for i in range(2 * n)},
        compiler_params=pltpu.CompilerParams(has_side_effects=EFFECT),
    )(*[pltpu.with_memory_space_constraint(s, pltpu.HBM) for s in srcs],
      *[pltpu.with_memory_space_constraint(lax.empty(z.shape, z.dtype), pltpu.HBM) for z in zones], *afters)
    return [(res[i], res[n + i], res[2 * n + i], res[3 * n + i]) for i in range(n)], res[-1][0:1, 0:1]


def copies_wait(name, started, slab_src, after):
    n = len(started)

    def body(*refs):
        ins, lands = refs[:n], refs[n:2 * n]
        sends, recvs = refs[2 * n:3 * n], refs[3 * n:4 * n]
        x, y, c = _place()
        mine = _index(x, y, c)
        for i in range(n):
            for k, peer in enumerate(_peers(x, y, c)):
                arrival = pltpu.make_async_remote_copy(src_ref=ins[i].at[mine] if slab_src else ins[i], dst_ref=lands[i].at[_index(*peer)],
                                                       send_sem=sends[i].at[k], recv_sem=recvs[i].at[k], device_id=peer, device_id_type=MESH)
                arrival.wait_send()
                arrival.wait_recv()

    srcs = [s[2] for s in started]
    lands = [s[3] for s in started]
    afters = list(after) if isinstance(after, (list, tuple)) else [after]
    res = pl.pallas_call(
        body, name=name,
        out_shape=[pltpu.HBM(s.shape, s.dtype) for s in srcs] + [pltpu.HBM(z.shape, z.dtype) for z in lands],
        in_specs=[HBM_SPEC] * (2 * n) + [SEM_SPEC] * (2 * n) + [pl.BlockSpec(memory_space=pl.ANY)] * len(afters),
        out_specs=[HBM_SPEC] * (2 * n),
        input_output_aliases={i: i for i in range(2 * n)},
        compiler_params=pltpu.CompilerParams(has_side_effects=EFFECT),
    )(*srcs, *lands, *[s[0] for s in started], *[s[1] for s in started], *afters)
    me = _index(*_place())
    own = [lax.dynamic_index_in_dim(s, me, 0, keepdims=True) if slab_src else s[None] for s in res[:n]]
    return [lax.dynamic_update_slice_in_dim(z, o, me, 0) for z, o in zip(res[n:], own)]


WEIGHTS = ["meta_tokens", "norm1_w", "w_in", "ssd_conv_w", "ssd_conv_b", "ssd_dt_bias", "ssd_a_log", "ssd_d", "ssd_norm_w", "lru_conv_w",
           "lru_conv_b", "lru_wa", "lru_ba", "lru_wx", "lru_bx", "lru_lambda", "lru_norm_w", "w_out", "norm2_w", "w_gate", "w_up", "w_down",
           "final_norm_w"]
BIG = ["w_in", "w_out", "w_gate", "w_up", "w_down"]
COLUMN_SHARDED = ["w_in", "w_gate", "w_up"]


def _pair_blocks(w):
    w = w.reshape(8, 2, 64, 64)
    z = jnp.zeros((8, 64, 64), w.dtype)
    return jnp.concatenate([jnp.concatenate([w[:, 0], z], axis=2), jnp.concatenate([z, w[:, 1]], axis=2)], axis=1)


def _unpair_blocks(w2):
    return jnp.stack([w2[:, :64, :64], w2[:, 64:, 64:]], axis=1).reshape(16, 64, 64)


def _per_group(v):
    return jnp.pad(v.reshape(2, 1, 8), ((0, 0), (0, 0), (0, 120)))


def _pad_cols(v, n):
    return jnp.pad(v, ((0, 0), (0, n - v.shape[1])))


def local_step(x, target, meta, ssd_cw, lru_cw, w_in, fetch, send, p):
    z120 = jnp.zeros((120, D), BF)
    w_dt = jnp.concatenate([w_in[2560:2568], z120, w_in[2568:2576], z120], axis=0)
    bias2, alog2, d2 = _per_group(p["ssd_dt_bias"]), _per_group(p["ssd_a_log"]), _per_group(p["ssd_d"])
    wa2 = _pair_blocks(p["lru_wa"]).astype(BF)
    wx2 = _pair_blocks(p["lru_wx"]).astype(BF)
    lru = (lru_cw, p["lru_conv_b"], wa2, p["lru_ba"], wx2, p["lru_bx"], p["lru_lambda"])

    h0 = jnp.concatenate([jnp.zeros((NPAD, D), F32), meta, x], axis=0)
    proj, dt_raw, u1 = in_proj(h0, p["norm1_w"], w_in, w_dt)
    xbc_act = conv_silu_fwd(proj, ssd_cw, p["ssd_conv_b"])
    yn_ssd, y_pre, h_prev = ssd_fwd(xbc_act, proj, dt_raw, bias2, alog2, d2, p["ssd_norm_w"])
    a, u = lru_gates_fwd(proj, *lru)
    hseq = lru_scan_fwd(a, u)
    (w_out,) = fetch(["w_out"], hseq)
    h1, cat = out_proj(yn_ssd, proj, hseq, p["lru_norm_w"], w_out, h0)
    w_gate, w_up = fetch(["w_gate", "w_up"], h1)
    gt, up, act, u2 = gate_up(h1, p["norm2_w"], w_gate, w_up)
    (w_down,) = fetch(["w_down"], act)
    dh2, dh2_b, loss, d_fnw = down_loss(act, w_down, h1, target, p["final_norm_w"])

    dgt, dup, g_down, g_gate, g_up = swiglu_bwd(dh2_b, w_down, gt, up, act, u2)
    sent = send({"w_down": g_down, "w_gate": g_gate, "w_up": g_up})
    dh1, dh1_b, d_n2 = gate_up_bwd(dgt, dup, w_gate, w_up, h1, p["norm2_w"] + sent, dh2)
    sent = send({"w_out": weight_grad("dw_out", [cat], dh1_b)})
    dyn, dh_out, dg_b, d_lnw = out_proj_bwd(dh1_b, w_out, proj, hseq, p["lru_norm_w"] + sent)

    dhs = lru_scan_bwd(a, dh_out)
    dxl_b, d_lcw, d_lcb, dwa2, d_ba, dwx2, d_bx, d_lam = lru_gates_bwd(dhs, hseq, proj, *lru)
    dz_b, dx, d_b, d_c, ddt_b, dpar, d_snw = ssd_bwd(dyn, xbc_act, proj, dt_raw, y_pre, h_prev, bias2, alog2, d2, p["ssd_norm_w"] + sent)
    dxbc_b, d_scw, d_scb = conv_silu_bwd(dx, d_b, d_c, proj, ssd_cw, p["ssd_conv_b"])
    g_p = weight_grad("dw_in", [dz_b, dg_b, dxl_b, dxbc_b, ddt_b], u1)
    g_in = jnp.concatenate([g_p[PZ:PZ + 1024], g_p[PXBC:PXBC + XBC], g_p[NP_IN:NP_IN + 8], g_p[NP_IN + 128:NP_IN + 136],
                            g_p[PG:PG + 1024], g_p[PXL:PXL + 1024]], axis=0)
    sent = send({"w_in": g_in})
    grad_x, d_meta, d_n1 = in_proj_bwd(dz_b, dg_b, dxl_b, dxbc_b, ddt_b, w_in, w_dt, h0, p["norm1_w"] + sent, dh1)
    small = {"norm1_w": d_n1, "ssd_conv_b": d_scb, "ssd_dt_bias": dpar[:, 0, :8].reshape(1, 16), "ssd_a_log": dpar[:, 1, :8].reshape(1, 16),
             "ssd_d": dpar[:, 2, :8].reshape(1, 16), "ssd_norm_w": d_snw, "lru_conv_b": d_lcb, "lru_ba": d_ba, "lru_bx": d_bx,
             "lru_lambda": d_lam, "lru_norm_w": d_lnw, "norm2_w": d_n2, "final_norm_w": d_fnw,
             "lru_wa": _unpair_blocks(dwa2), "lru_wx": _unpair_blocks(dwx2), "meta_tokens": d_meta,
             "ssd_conv_w": d_scw, "lru_conv_w": d_lcw}
    return loss, grad_x, small


def _pack_small(small, loss):
    rows = [_pad_cols(small[name], -(-n // 1024) * 1024).reshape(-1, 1024) for name, n in SIMPLE]
    rows += [small["lru_wa"].reshape(64, 1024), small["lru_wx"].reshape(64, 1024), small["meta_tokens"],
             _pad_cols(small["ssd_conv_w"], 2048).reshape(8, 1024), small["lru_conv_w"], _pad_cols(loss[:, 0:1], 1024)]
    sm = jnp.concatenate(rows, axis=0)
    return jnp.pad(sm, ((0, SM_ROWS - sm.shape[0]), (0, 0)))


def _slabs(g):
    return g.reshape(8, g.shape[0] // 8, g.shape[1])


def _unslab(g):
    return g.reshape(8 * g.shape[1], g.shape[2])


def kernel(x, meta_tokens, norm1_w, w_in, ssd_conv_w, ssd_conv_b, ssd_dt_bias, ssd_a_log, ssd_d, ssd_norm_w, lru_conv_w, lru_conv_b, lru_wa, lru_ba, lru_wx, lru_bx, lru_lambda, lru_norm_w, w_out, norm2_w, w_gate, w_up, w_down, final_norm_w, loss_target, m_meta_tokens, m_norm1_w, m_w_in, m_ssd_conv_w, m_ssd_conv_b, m_ssd_dt_bias, m_ssd_a_log, m_ssd_d, m_ssd_norm_w, m_lru_conv_w, m_lru_conv_b, m_lru_wa, m_lru_ba, m_lru_wx, m_lru_bx, m_lru_lambda, m_lru_norm_w, m_w_out, m_norm2_w, m_w_gate, m_w_up, m_w_down, m_final_norm_w, v_meta_tokens, v_norm1_w, v_w_in, v_ssd_conv_w, v_ssd_conv_b, v_ssd_dt_bias, v_ssd_a_log, v_ssd_d, v_ssd_norm_w, v_lru_conv_w, v_lru_conv_b, v_lru_wa, v_lru_ba, v_lru_wx, v_lru_bx, v_lru_lambda, v_lru_norm_w, v_w_out, v_norm2_w, v_w_gate, v_w_up, v_w_down, v_final_norm_w):
    w = dict(meta_tokens=meta_tokens, norm1_w=norm1_w, w_in=w_in[0], ssd_conv_w=ssd_conv_w[0], ssd_conv_b=ssd_conv_b, ssd_dt_bias=ssd_dt_bias,
             ssd_a_log=ssd_a_log, ssd_d=ssd_d, ssd_norm_w=ssd_norm_w, lru_conv_w=lru_conv_w[0], lru_conv_b=lru_conv_b, lru_wa=lru_wa[0],
             lru_ba=lru_ba, lru_wx=lru_wx[0], lru_bx=lru_bx, lru_lambda=lru_lambda, lru_norm_w=lru_norm_w, w_out=w_out[0], norm2_w=norm2_w,
             w_gate=w_gate[0], w_up=w_up[0], w_down=w_down[0], final_norm_w=final_norm_w.reshape(1, D))
    m = dict(meta_tokens=m_meta_tokens, norm1_w=m_norm1_w, w_in=m_w_in[0], ssd_conv_w=m_ssd_conv_w[0], ssd_conv_b=m_ssd_conv_b,
             ssd_dt_bias=m_ssd_dt_bias, ssd_a_log=m_ssd_a_log, ssd_d=m_ssd_d, ssd_norm_w=m_ssd_norm_w, lru_conv_w=m_lru_conv_w[0],
             lru_conv_b=m_lru_conv_b, lru_wa=m_lru_wa[0], lru_ba=m_lru_ba, lru_wx=m_lru_wx[0], lru_bx=m_lru_bx, lru_lambda=m_lru_lambda,
             lru_norm_w=m_lru_norm_w, w_out=m_w_out[0], norm2_w=m_norm2_w, w_gate=m_w_gate[0], w_up=m_w_up[0], w_down=m_w_down[0],
             final_norm_w=m_final_norm_w.reshape(1, D))
    v = dict(meta_tokens=v_meta_tokens, norm1_w=v_norm1_w, w_in=v_w_in[0], ssd_conv_w=v_ssd_conv_w[0], ssd_conv_b=v_ssd_conv_b,
             ssd_dt_bias=v_ssd_dt_bias, ssd_a_log=v_ssd_a_log, ssd_d=v_ssd_d, ssd_norm_w=v_ssd_norm_w, lru_conv_w=v_lru_conv_w[0],
             lru_conv_b=v_lru_conv_b, lru_wa=v_lru_wa[0], lru_ba=v_lru_ba, lru_wx=v_lru_wx[0], lru_bx=v_lru_bx, lru_lambda=v_lru_lambda,
             lru_norm_w=v_lru_norm_w, w_out=v_w_out[0], norm2_w=v_norm2_w, w_gate=v_w_gate[0], w_up=v_w_up[0], w_down=v_w_down[0],
             final_norm_w=v_final_norm_w.reshape(1, D))
    shapes = dict(meta_tokens=meta_tokens.shape, norm1_w=norm1_w.shape, w_in=w_in.shape, ssd_conv_w=ssd_conv_w.shape,
                  ssd_conv_b=ssd_conv_b.shape, ssd_dt_bias=ssd_dt_bias.shape, ssd_a_log=ssd_a_log.shape, ssd_d=ssd_d.shape,
                  ssd_norm_w=ssd_norm_w.shape, lru_conv_w=lru_conv_w.shape, lru_conv_b=lru_conv_b.shape, lru_wa=lru_wa.shape,
                  lru_ba=lru_ba.shape, lru_wx=lru_wx.shape, lru_bx=lru_bx.shape, lru_lambda=lru_lambda.shape, lru_norm_w=lru_norm_w.shape,
                  w_out=w_out.shape, norm2_w=norm2_w.shape, w_gate=w_gate.shape, w_up=w_up.shape, w_down=w_down.shape,
                  final_norm_w=final_norm_w.shape)
    me = _index(*_place())
    for n in COLUMN_SHARDED:
        w[n], m[n], v[n] = w[n].T, m[n].T, v[n].T

    small_shard = jnp.concatenate([w["meta_tokens"], _pad_cols(w["ssd_conv_w"], 256).reshape(8, 128), w["lru_conv_w"],
                                   jnp.zeros((4, 128), F32)], axis=0)
    g_in, gs = all_gather("gather_w_in", [w["w_in"].astype(BF), small_shard])
    later = ["w_out", "w_gate", "w_up", "w_down"]
    started, behind = copies_start("gather_rest_start", [w[n].astype(BF) for n in later], False, gs)
    started = dict(zip(later, started))
    meta_full = gs[:, 0:16].transpose(1, 0, 2).reshape(N_META, D)
    ssd_cw = gs[:, 16:24].reshape(8, 4, 256)[:, :, :192].transpose(1, 0, 2).reshape(4, XBC)
    lru_cw = gs[:, 24:28].transpose(1, 0, 2).reshape(4, LRU_W)

    def fetch(names, after):
        got = copies_wait("gather_" + names[0] + "_wait", [started[n] for n in names], False, after)
        return [_unslab(g) for g in got]

    in_flight = {}

    def send(grads):
        names = list(grads)
        st, token = copies_start("grads_" + names[0] + "_start", [grads[n] if n == "small" else _slabs(grads[n]) for n in names], True, None)
        in_flight.update(zip(names, st))
        return token

    loss, grad_x, small = local_step(x[0], loss_target[0], meta_full, ssd_cw, lru_cw, _unslab(g_in), fetch, send,
                                     {**w, "norm1_w": w["norm1_w"] + behind})
    send({"small": _pack_small(small, loss).reshape(8, SM_ROWS // 8, 1024)})

    out = {}
    early = ["w_down", "w_gate", "w_up", "w_out"]
    recv = dict(zip(early, copies_wait("grads_early_wait", [in_flight[n] for n in early], True, in_flight["small"][2])))
    for pair in (early[:2], early[2:]):
        done = adamw_shards("adamw_" + pair[0], [recv[n] for n in pair], [w[n] for n in pair], [m[n] for n in pair], [v[n] for n in pair])
        out.update(zip(pair, done))
    recv_in, recv_small = copies_wait("grads_late_wait", [in_flight["w_in"], in_flight["small"]], True, [out[n][0] for n in early])
    untiled = (IN_COLS, 128)
    out["w_in"] = [o.reshape(IN_COLS // 8, D)
                   for o in adamw_w_in(recv_in, w["w_in"].reshape(untiled), m["w_in"].reshape(untiled), v["w_in"].reshape(untiled))]
    for n in COLUMN_SHARDED:
        out[n] = [o.T for o in out[n]]
    sm = all_gather("gather_small_grads", [sum_slabs(recv_small)])[0].reshape(SM_ROWS, 1024)
    special_g =[sm[SM_WA:SM_WA + 64].reshape(16, 64, 64), sm[SM_WX:SM_WX + 64].reshape(16, 64, 64),
                 lax.dynamic_slice(sm[SM_META:SM_META + 16], (0, 128 * me), (16, 128)),
                 lax.dynamic_slice(sm[SM_SCW:SM_SCW + 8].reshape(4, 2048), (0, 192 * me), (4, 192)),
                 lax.dynamic_slice(sm[SM_LCW:SM_LCW + 4], (0, 128 * me), (4, 128))]
    names = [n for n, _ in SIMPLE] + SPECIAL
    res = adamw_small(sm, special_g, [w[n] for n in names], [m[n] for n in names], [v[n] for n in names])
    for k, (n, _) in enumerate(SIMPLE):
        out[n] = res[4 * k:4 * k + 4]
    for k, n in enumerate(SPECIAL):
        o = 4 * len(SIMPLE) + 3 * k
        out[n] = [special_g[k]] + list(res[o:o + 3])
    loss_total = sm[SM_LOSS, 0]
    flat = [loss_total, grad_x[None]]
    for k in range(4):
        flat += [out[n][k].reshape(shapes[n]) for n in WEIGHTS]
    return tuple(flat)
```

```python
import math

import jax
import jax.numpy as jnp
from jax import lax
from jax.experimental import pallas as pl
from jax.experimental.pallas import tpu as pltpu

F32 = jnp.float32
BF = jnp.bfloat16

D = 1024
SEQ = 2048
N_META = 16
Q = 128
NPAD = 112
T = NPAD + N_META + SEQ
NCH = T // Q
RC = 544
D_FF = 2816
SSD_W = 1024
LRU_W = 1024
XBC = 1536
IN_COLS = 4624
PZ, PG, PXL, PXBC = 0, 1024, 2048, 3072
NP_IN = 4608
EPS = 1e-6
LRU_C = 8.0
VMEM_LIMIT = 56 * 1024 * 1024

ADAM_LR, ADAM_B1, ADAM_B2, ADAM_EPS, ADAM_WD, ADAM_STEP = 0.001, 0.9, 0.999, 1e-08, 0.01, 10

NT_DIMS = (((1,), (1,)), ((), ()))
TN_DIMS = (((0,), (0,)), ((), ()))
MESH = pl.DeviceIdType.MESH


def _params(n_grid=1, limit=VMEM_LIMIT):
    return pltpu.CompilerParams(dimension_semantics=("arbitrary",) * n_grid, vmem_limit_bytes=limit)


def _spec(shape, imap, single=False):
    if single:
        return pl.BlockSpec(shape, imap, pipeline_mode=pl.Buffered(1))
    return pl.BlockSpec(shape, imap)


def _sigmoid(x):
    return 0.5 * jnp.tanh(0.5 * x) + 0.5


def _sigmoid_gate(x):
    return 1.0 / (1.0 + jnp.exp(-x))


def _softplus(x):
    return jnp.maximum(x, 0.0) + jnp.log(1.0 + jnp.exp(-jnp.abs(x)))


def _rms_stats(h):
    return lax.rsqrt(jnp.mean(h * h, axis=-1, keepdims=True) + EPS)


def _rms(h, w):
    return (h * _rms_stats(h)) * w


def _rms_bwd(du, h, w):
    r = _rms_stats(h)
    n = h * r
    dn = du * w
    dh = r * (dn - n * jnp.mean(dn * n, axis=-1, keepdims=True))
    return dh, du * n


_G0 = math.sqrt(2.0 / math.pi)


def _gelu(x):
    return 0.5 * x * (1.0 + jnp.tanh(_G0 * (x + 0.044715 * (x * x * x))))


def _gelu_grad(x):
    t = jnp.tanh(_G0 * (x + 0.044715 * (x * x * x)))
    return 0.5 * (1.0 + t) + 0.5 * x * (1.0 - t * t) * (_G0 * (1.0 + 3.0 * 0.044715 * (x * x)))


def _rows(shape, r0=0):
    return lax.broadcasted_iota(jnp.int32, shape, 0) + r0


def _lanes(shape):
    return lax.broadcasted_iota(jnp.int32, shape, 1)


HALO = 8


def _fill_padded(pad_ref, x_ref):
    pad_ref[0:HALO, :] = jnp.zeros((HALO, pad_ref.shape[1]), F32)
    pad_ref[T + HALO:T + 2 * HALO, :] = jnp.zeros((HALO, pad_ref.shape[1]), F32)

    def step(c, carry):
        r0 = pl.multiple_of(c * Q, Q)
        pad_ref[pl.ds(r0 + HALO, Q), :] = x_ref[pl.ds(r0, Q), :].astype(F32)
        return carry

    lax.fori_loop(0, NCH, step, 0)


def _back(pad_ref, r0):
    win = pad_ref[pl.ds(r0, Q + HALO), :]
    return lambda s: win[HALO:, :] if s == 0 else pltpu.roll(win, s, axis=0)[HALO:, :]


def _ahead(pad_ref, r0):
    win = pad_ref[pl.ds(r0 + HALO, Q + HALO), :]
    return lambda s: win[:Q, :] if s == 0 else pltpu.roll(win, Q + HALO - s, axis=0)[:Q, :]


def _conv(back, w, b):
    y = b + w[3:4, :] * back(0)
    for k in range(3):
        y = y + w[k:k + 1, :] * back(3 - k)
    return y


def _conv_bwd_x(ahead, w):
    dx = w[3:4, :] * ahead(0)
    for k in range(3):
        dx = dx + w[k:k + 1, :] * ahead(3 - k)
    return dx


def _conv_bwd_w(dy, back):
    dws = [jnp.sum(dy * back(3 - k), axis=0, keepdims=True) for k in range(4)]
    return jnp.concatenate(dws, axis=0), jnp.sum(dy, axis=0, keepdims=True)


def _chunks(fn, unrolled=False):
    if unrolled:
        for c in range(NCH):
            fn(c * Q)
        return

    def step(c, carry):
        fn(pl.multiple_of(c * Q, Q))
        return carry

    lax.fori_loop(0, NCH, step, 0)


HALF = RC // 2


def _col_tiles(n, tn, fn):
    def step(j, carry):
        fn(pl.multiple_of(j * tn, tn))
        return carry

    lax.fori_loop(0, n // tn, step, 0)


def _rows_spec(cols, block_col=0):
    return _spec((RC, cols), lambda i: (i, block_col))


def _whole(shape):
    return _spec(shape, lambda i: tuple(0 for _ in shape), single=True)


def _vec(cols):
    return _spec((1, cols), lambda i: (0, 0))


def _zero_at_first(*refs):
    @pl.when(pl.program_id(0) == 0)
    def _():
        for r in refs:
            r[...] = jnp.zeros_like(r)


IN_RUNS = ((PZ, 0, 1024), (PG, 2576, 2048), (PXBC, 1024, XBC))


def _in_tiles(fn):
    for pcol, wrow, width in IN_RUNS:
        def step(j, carry, pcol=pcol, wrow=wrow):
            fn(pl.multiple_of(pcol + j * 512, 512), pl.multiple_of(wrow + j * 512, 16))
            return carry

        lax.fori_loop(0, width // 512, step, 0)


def in_proj(h0, wn, w_t, w_dt):
    def body(h_ref, wn_ref, w_ref, wdt_ref, o_ref, dt_ref, u_ref):
        for r in (0, HALF):
            u_ref[r:r + HALF, :] = _rms(h_ref[r:r + HALF, :], wn_ref[...]).astype(BF)

        def tile(pcol, wrow):
            o_ref[:, pl.ds(pcol, 512)] = lax.dot_general(u_ref[...], w_ref[pl.ds(wrow, 512), :], NT_DIMS, preferred_element_type=F32).astype(BF)

        _in_tiles(tile)
        dt_ref[...] = lax.dot_general(u_ref[...], wdt_ref[...], NT_DIMS, preferred_element_type=F32)

    return pl.pallas_call(
        body, grid=(T // RC,), in_specs=[_rows_spec(D), _vec(D), _whole((IN_COLS, D)), _whole((256, D))],
        out_specs=[_rows_spec(NP_IN), _rows_spec(256), _rows_spec(D)],
        out_shape=[jax.ShapeDtypeStruct((T, NP_IN), BF), jax.ShapeDtypeStruct((T, 256), F32), jax.ShapeDtypeStruct((T, D), BF)],
        compiler_params=_params(), name="in_proj")(h0, wn, w_t, w_dt)


def out_proj(yn_ssd, proj, hseq, lru_nw, w_out, h0):
    def body(y_ref, g_ref, h_ref, wn_ref, w_ref, r_ref, o_ref, cat_ref):
        cat_ref[:, 0:SSD_W] = y_ref[...]
        for r in (0, HALF):
            y = _gelu(g_ref[r:r + HALF, :].astype(F32)) * h_ref[r:r + HALF, :]
            cat_ref[r:r + HALF, SSD_W:] = _rms(y, wn_ref[...]).astype(BF)

        def tile(c0):
            o_ref[:, pl.ds(c0, 512)] = r_ref[:, pl.ds(c0, 512)] + jnp.dot(cat_ref[...], w_ref[:, pl.ds(c0, 512)], preferred_element_type=F32)

        _col_tiles(D, 512, tile)

    return pl.pallas_call(
        body, grid=(T // RC,),
        in_specs=[_rows_spec(SSD_W), _rows_spec(LRU_W, PG // LRU_W), _rows_spec(LRU_W), _vec(LRU_W), _whole((SSD_W + LRU_W, D)), _rows_spec(D)],
        out_specs=[_rows_spec(D), _rows_spec(SSD_W + LRU_W)],
        out_shape=[jax.ShapeDtypeStruct((T, D), F32), jax.ShapeDtypeStruct((T, SSD_W + LRU_W), BF)],
        compiler_params=_params(), name="out_proj")(yn_ssd, proj, hseq, lru_nw, w_out, h0)


def out_proj_bwd(dh1_b, w_out, proj, hseq, lru_nw):
    def body(d_ref, w_ref, g_ref, h_ref, wn_ref, dy_ref, dh_ref, dg_ref, dw_ref, dl_scr):
        _zero_at_first(dw_ref)

        def tile(c0):
            dy_ref[:, pl.ds(c0, 512)] = lax.dot_general(d_ref[...], w_ref[pl.ds(c0, 512), :], NT_DIMS, preferred_element_type=F32)
            dl_scr[:, pl.ds(c0, 512)] = lax.dot_general(d_ref[...], w_ref[pl.ds(SSD_W + c0, 512), :], NT_DIMS, preferred_element_type=F32)

        _col_tiles(SSD_W, 512, tile)

        for r in (0, HALF):
            g = g_ref[r:r + HALF, :].astype(F32)
            h = h_ref[r:r + HALF, :]
            ge = _gelu(g)
            dy, dw = _rms_bwd(dl_scr[r:r + HALF, :], ge * h, wn_ref[...])
            dw_ref[...] += jnp.sum(dw, axis=0, keepdims=True)
            dh_ref[r:r + HALF, :] = dy * ge
            dg_ref[r:r + HALF, :] = (dy * h * _gelu_grad(g)).astype(BF)

    return pl.pallas_call(
        body, grid=(T // RC,),
        in_specs=[_rows_spec(D), _whole((SSD_W + LRU_W, D)), _rows_spec(LRU_W, PG // LRU_W), _rows_spec(LRU_W), _vec(LRU_W)],
        out_specs=[_rows_spec(SSD_W), _rows_spec(LRU_W), _rows_spec(LRU_W), _vec(LRU_W)],
        out_shape=[jax.ShapeDtypeStruct((T, SSD_W), F32), jax.ShapeDtypeStruct((T, LRU_W), F32), jax.ShapeDtypeStruct((T, LRU_W), BF),
                   jax.ShapeDtypeStruct((1, LRU_W), F32)],
        scratch_shapes=[pltpu.VMEM((RC, LRU_W), F32)],
        compiler_params=_params(), name="out_proj_bwd")(dh1_b, w_out, proj, hseq, lru_nw)


def in_proj_bwd(dz, dg, dxl, dxbc, ddt, w_t, w_dt, h0, wn, dh1):
    first = NPAD + N_META

    def body(dz_ref, dg_ref, dxl_ref, dxbc_ref, ddt_ref, w_ref, wdt_ref, h_ref, wn_ref, r_ref, gx_hbm, meta_ref, dw_ref, du_scr, o_ref, sem):
        i = pl.program_id(0)
        _zero_at_first(dw_ref)
        du_scr[...] = jnp.dot(ddt_ref[...], wdt_ref[...], preferred_element_type=F32)
        for d_ref, wrow, width in ((dz_ref, 0, 1024), (dxbc_ref, 1024, XBC), (dg_ref, 2576, 1024), (dxl_ref, 3600, 1024)):
            def step(j, carry, d_ref=d_ref, wrow=wrow):
                c0 = pl.multiple_of(j * 512, 512)
                du_scr[...] += jnp.dot(d_ref[:, pl.ds(c0, 512)], w_ref[pl.ds(pl.multiple_of(wrow + c0, 16), 512), :], preferred_element_type=F32)
                return carry

            lax.fori_loop(0, width // 512, step, 0)
        for r in (0, HALF):
            dh, dw = _rms_bwd(du_scr[r:r + HALF, :], h_ref[r:r + HALF, :], wn_ref[...])
            dw_ref[...] += jnp.sum(dw, axis=0, keepdims=True)
            o_ref[r:r + HALF, :] = dh + r_ref[r:r + HALF, :]

        @pl.when(i == 0)
        def _():
            meta_ref[...] = o_ref[NPAD:first, :]
            head = pltpu.make_async_copy(o_ref.at[pl.ds(first, RC - first)], gx_hbm.at[pl.ds(0, RC - first)], sem)
            head.start()
            head.wait()

        @pl.when(i > 0)
        def _():
            rest = pltpu.make_async_copy(o_ref, gx_hbm.at[pl.ds(pl.multiple_of(i * RC - first, 32), RC)], sem)
            rest.start()
            rest.wait()

    return pl.pallas_call(
        body, grid=(T // RC,),
        in_specs=[_rows_spec(SSD_W), _rows_spec(LRU_W), _rows_spec(LRU_W), _rows_spec(XBC), _rows_spec(256), _whole((IN_COLS, D)),
                  _whole((256, D)), _rows_spec(D), _vec(D), _rows_spec(D)],
        out_specs=[pl.BlockSpec(memory_space=pl.ANY), _spec((N_META, D), lambda i: (0, 0)), _vec(D)],
        out_shape=[jax.ShapeDtypeStruct((SEQ, D), F32), jax.ShapeDtypeStruct((N_META, D), F32), jax.ShapeDtypeStruct((1, D), F32)],
        scratch_shapes=[pltpu.VMEM((RC, D), F32), pltpu.VMEM((RC, D), F32), pltpu.SemaphoreType.DMA],
        compiler_params=_params(), name="in_proj_bwd")(dz, dg, dxl, dxbc, ddt, w_t, w_dt, h0, wn, dh1)


def weight_grad(name, parts, u1):
    tm = 256
    tiles = [p.shape[1] // tm for p in parts]
    starts = [sum(tiles[:k]) for k in range(len(parts))]

    def body(*refs):
        a_refs, u_ref, o_ref = refs[:len(parts)], refs[len(parts)], refs[len(parts) + 1]
        step = pl.program_id(0)
        for a_ref, start, n in zip(a_refs, starts, tiles):
            @pl.when((step >= start) & (step < start + n))
            def _(a_ref=a_ref):
                o_ref[...] = lax.dot_general(a_ref[...], u_ref[...], TN_DIMS, preferred_element_type=F32).astype(BF)

    def tile_of(start, n):
        return lambda j: (0, jnp.clip(j - start, 0, n - 1))

    return pl.pallas_call(
        body, grid=(sum(tiles),),
        in_specs=[_spec((T, tm), tile_of(s, n)) for s, n in zip(starts, tiles)] + [_spec((T, D), lambda j: (0, 0), single=True)],
        out_specs=_spec((tm, D), lambda j: (j, 0)),
        out_shape=jax.ShapeDtypeStruct((tm * sum(tiles), D), BF),
        compiler_params=_params(), name=name)(*parts, u1)


def _ssd_chunk_common(row0, dt_ref, b_ref, c_ref, bias, a_neg):
    shape = (Q, Q)
    lane = _lanes(shape)
    sub = _rows(shape)
    live = (_rows(shape, row0) >= NPAD) & (lane < 8)
    dtr = dt_ref[:, :]
    dt = jnp.where(live, _softplus(dtr + bias), 0.0)
    d_a = dt * a_neg
    tri = (sub >= lane).astype(F32)
    cs = jnp.dot(tri, d_a, precision=lax.Precision.HIGHEST, preferred_element_type=F32)
    cs_t = cs.T
    b_f = b_ref[:, :]
    bc = b_f.astype(BF)
    cc = c_ref[:, :].astype(BF)
    cb = lax.dot_general(cc, bc, NT_DIMS, preferred_element_type=F32)
    cs_last = cs[Q - 1:Q, :]
    return dict(lane=lane, sub=sub, live=live, dtr=dtr, dt=dt, cs=cs, cs_t=cs_t, bc=bc, cc=cc, cb=cb, bc_t=b_f.T.astype(BF),
                ecs=jnp.exp(cs), dsm=jnp.exp(cs_last - cs), gam=jnp.exp(cs_last))


def _pair(lane_even, mat, j):
    return jnp.where(lane_even, mat[:, j:j + 1], mat[:, j + 1:j + 2])


def _pair_row(lane_even, mat, j):
    return jnp.where(lane_even[0:1, :], mat[:, j:j + 1], mat[:, j + 1:j + 2])


def _head_decay(cm, j):
    seg = cm["cs"][:, j:j + 1] - cm["cs_t"][j:j + 1, :]
    return jnp.exp(jnp.where(cm["sub"] >= cm["lane"], seg, -jnp.inf))


def _head_decay_t(cm, j):
    seg = cm["cs_t"][j:j + 1, :] - cm["cs"][:, j:j + 1]
    return jnp.exp(jnp.where(cm["lane"] >= cm["sub"], seg, -jnp.inf))


def _conv_window(raw_ref, halo_ref, pad_scr):
    pad_scr[0:HALO, :] = halo_ref[...].astype(F32)[halo_ref.shape[0] - HALO:, :]
    pad_scr[HALO:HALO + Q, :] = raw_ref[...].astype(F32)
    win = pad_scr[...]
    return lambda s: win[HALO:, :] if s == 0 else pltpu.roll(win, s, axis=0)[HALO:, :]


def _xbc_cols(g):
    return slice(512 * g, 512 * g + 512), slice(SSD_W + 128 * g, SSD_W + 128 * g + 128), slice(SSD_W + 256 + 128 * g, SSD_W + 384 + 128 * g)


def ssd_fwd(proj, dt_raw, conv_w, conv_b, dt_bias2, a_log2, d2, norm_w):
    def body(raw_ref, halo_ref, dt_all, z_all, cw_ref, cb_ref, bias_all, alog_all, d_all, nw_all, yn_all, y_all, hp_all,
             h_all, pad_scr, act_scr):
        @pl.when(pl.program_id(0) == 0)
        def _():
            h_all[...] = jnp.zeros_like(h_all)

        pre = _conv(_conv_window(raw_ref, halo_ref, pad_scr), cw_ref[...], cb_ref[...])
        act_scr[...] = pre * _sigmoid(pre)
        for g in range(2):
            wide, thin = slice(512 * g, 512 * g + 512), slice(128 * g, 128 * g + 128)
            xs, bs, cs = _xbc_cols(g)
            group(act_scr.at[:, xs], act_scr.at[:, bs], act_scr.at[:, cs], dt_all.at[:, thin], z_all.at[:, wide], bias_all.at[g],
                  alog_all.at[g], d_all.at[g], nw_all.at[:, wide], yn_all.at[:, wide], y_all.at[:, wide], hp_all.at[g, 0], h_all.at[g])

    def group(x_ref, b_ref, c_ref, dt_ref, z_ref, bias_ref, alog_ref, d_ref, nw_ref, yn_ref, y_ref, hp_ref, h_scr):
        bias = bias_ref[...]
        a_neg = -jnp.exp(alog_ref[...])
        dsk = d_ref[...]
        cm = _ssd_chunk_common(pl.program_id(0) * Q, dt_ref, b_ref, c_ref, bias, a_neg)
        lane_even = cm["lane"] < 64
        for p in range(4):
            je, jo = 2 * p, 2 * p + 1
            xp = x_ref[:, 128 * p:128 * p + 128]
            xdt = xp * _pair(lane_even, cm["dt"], je)
            xdt_b = xdt.astype(BF)
            m_e = (cm["cb"] * _head_decay(cm, je)).astype(BF)
            m_o = (cm["cb"] * _head_decay(cm, jo)).astype(BF)
            zero = jnp.zeros_like(xdt_b)
            yd = (jnp.dot(m_e, jnp.where(lane_even, xdt_b, zero), preferred_element_type=F32)
                  + jnp.dot(m_o, jnp.where(lane_even, zero, xdt_b), preferred_element_type=F32))
            hp = h_scr[p]
            hp_ref[p] = hp
            yo = jnp.dot(cm["cc"], hp.astype(BF), preferred_element_type=F32) * _pair(lane_even, cm["ecs"], je)
            y_ref[:, 128 * p:128 * p + 128] = yd + yo + xp * _pair_row(lane_even, dsk, je)
            st = jnp.dot(cm["bc_t"], (xdt * _pair(lane_even, cm["dsm"], je)).astype(BF), preferred_element_type=F32)
            h_scr[p] = hp * _pair_row(lane_even, cm["gam"], je) + st
        zc = z_ref[:, :].astype(F32)
        gated = y_ref[:, :] * (zc * _sigmoid(zc))
        yn_ref[:, :] = _rms(gated, nw_ref[...]).astype(BF)

    par = _spec((2, 1, 128), lambda c: (0, 0, 0))
    wide = _spec((Q, SSD_W), lambda c: (c, 0))
    xbc = PXBC // XBC
    halo = 2 * HALO
    return pl.pallas_call(
        body, grid=(NCH,),
        in_specs=[_spec((Q, XBC), lambda c: (c, xbc)), _spec((halo, XBC), lambda c: (jnp.maximum(c * (Q // halo) - 1, 0), xbc)),
                  _spec((Q, 256), lambda c: (c, 0)), wide, _spec((4, XBC), lambda c: (0, 0)), _spec((1, XBC), lambda c: (0, 0)),
                  par, par, par, _spec((1, SSD_W), lambda c: (0, 0))],
        out_specs=[wide, wide, _spec((2, 1, 4, 128, 128), lambda c: (0, c, 0, 0, 0))],
        out_shape=[jax.ShapeDtypeStruct((T, SSD_W), BF), jax.ShapeDtypeStruct((T, SSD_W), F32),
                   jax.ShapeDtypeStruct((2, NCH, 4, 128, 128), F32)],
        scratch_shapes=[pltpu.VMEM((2, 4, 128, 128), F32), pltpu.VMEM((Q + HALO, XBC), F32), pltpu.VMEM((Q, XBC), F32)],
        compiler_params=_params(), name="ssd_fwd")(proj, proj, dt_raw, proj, conv_w, conv_b, dt_bias2, a_log2, d2, norm_w)


def ssd_bwd(dyn, proj, dt_raw, conv_w, conv_b, y_pre, h_prev, dt_bias2, a_log2, d2, norm_w):
    def body(dyn_all, raw_ref, halo_ref, dt_all, z_all, y_all, hp_all, cw_ref, cb_ref, bias_all, alog_all, d_all, nw_all,
             dz_all, dxbc_ref, ddt_all, dpar_all, dnw_all, dcw_ref, dcb_ref, dh_all, acc_all, pad_scr, act_scr, dsilu_scr, dact_scr, dpad_scr):
        @pl.when(pl.program_id(0) == 0)
        def _():
            dh_all[...] = jnp.zeros_like(dh_all)
            acc_all[...] = jnp.zeros_like(acc_all)
            dnw_all[...] = jnp.zeros_like(dnw_all)
            dcw_ref[...] = jnp.zeros_like(dcw_ref)
            dcb_ref[...] = jnp.zeros_like(dcb_ref)
            dpad_scr[Q:Q + HALO, :] = jnp.zeros((HALO, XBC), F32)

        back = _conv_window(raw_ref, halo_ref, pad_scr)
        pre = _conv(back, cw_ref[...], cb_ref[...])
        sg = _sigmoid(pre)
        act_scr[...] = pre * sg
        dsilu_scr[...] = sg * (1.0 + pre * (1.0 - sg))
        for g in range(2):
            wide, thin = slice(512 * g, 512 * g + 512), slice(128 * g, 128 * g + 128)
            xs, bs, cs = _xbc_cols(g)
            group(dyn_all.at[:, wide], act_scr.at[:, xs], act_scr.at[:, bs], act_scr.at[:, cs], dt_all.at[:, thin], z_all.at[:, wide],
                  y_all.at[:, wide], hp_all.at[g, 0], bias_all.at[g], alog_all.at[g], d_all.at[g], nw_all.at[:, wide],
                  dz_all.at[:, wide], dact_scr.at[:, xs], dact_scr.at[:, bs], dact_scr.at[:, cs], ddt_all.at[:, thin], dpar_all.at[g],
                  dnw_all.at[:, wide], dh_all.at[g], acc_all.at[g])
        dpre = dact_scr[...] * dsilu_scr[...]
        dcw, dcb = _conv_bwd_w(dpre, back)
        dcw_ref[...] += dcw
        dcb_ref[...] += dcb
        dpad_scr[0:Q, :] = dpre
        win = dpad_scr[...]
        dxbc_ref[...] = _conv_bwd_x(lambda s: win[:Q, :] if s == 0 else pltpu.roll(win, Q + HALO - s, axis=0)[:Q, :], cw_ref[...]).astype(BF)
        dpad_scr[Q:Q + HALO, :] = dpre[0:HALO, :]

    def group(dyn_ref, x_ref, b_ref, c_ref, dt_ref, z_ref, y_ref, hp_ref, bias_ref, alog_ref, d_ref, nw_ref,
              dz_ref, dx_ref, db_ref, dc_ref, ddt_ref, dpar_ref, dnw_ref, dh_scr, acc_scr):
        ci = pl.program_id(0)
        bias = bias_ref[...]
        a_neg = -jnp.exp(alog_ref[...])
        dsk = d_ref[...]
        cm = _ssd_chunk_common((NCH - 1 - ci) * Q, dt_ref, b_ref, c_ref, bias, a_neg)
        lane, sub = cm["lane"], cm["sub"]
        lane_even = lane < 64
        cc_t = c_ref[:, :].T.astype(BF)
        cb_t = lax.dot_general(cm["bc"], cm["cc"], NT_DIMS, preferred_element_type=F32)
        zc = z_ref[:, :].astype(F32)
        yc = y_ref[:, :]
        sg = _sigmoid(zc)
        sz = zc * sg
        dgated, dnw = _rms_bwd(dyn_ref[:, :], yc * sz, nw_ref[...])
        dnw_ref[...] += jnp.sum(dnw, axis=0, keepdims=True)
        dz_ref[:, :] = (dgated * yc * (sg * (1.0 + zc * (1.0 - sg)))).astype(BF)
        dy_all = dgated * sz
        dcb = jnp.zeros((Q, Q), F32)
        dcb_t = jnp.zeros((Q, Q), F32)
        db_acc = jnp.zeros((Q, Q), F32)
        dc_acc = jnp.zeros((Q, Q), F32)
        dcs = jnp.zeros((Q, Q), F32)
        ddt = jnp.zeros((Q, Q), F32)
        for p in range(4):
            je, jo = 2 * p, 2 * p + 1
            xp = x_ref[:, 128 * p:128 * p + 128]
            dy = dy_all[:, 128 * p:128 * p + 128]
            dt_p = _pair(lane_even, cm["dt"], je)
            xdt = xp * dt_p
            xdt_b = xdt.astype(BF)
            dy_b = dy.astype(BF)
            zero = jnp.zeros_like(dy_b)
            hp = hp_ref[p]
            hp_b = hp.astype(BF)
            dh = dh_scr[p]
            dh_b = dh.astype(BF)
            acc_scr[p:p + 1, :] += jnp.sum(dy * xp, axis=0, keepdims=True)
            dxp = dy * _pair_row(lane_even, dsk, je)
            e_p = _pair(lane_even, cm["ecs"], je)
            g_p = jnp.dot(cm["cc"], hp_b, preferred_element_type=F32)
            dg_b = (dy * e_p).astype(BF)
            de = dy * g_p * e_p
            dc_acc = dc_acc + lax.dot_general(dg_b, hp_b, NT_DIMS, preferred_element_type=F32)
            dh_in = jnp.dot(cc_t, dg_b, preferred_element_type=F32)
            ds_p = _pair(lane_even, cm["dsm"], je)
            r_p = jnp.dot(cm["bc"], dh_b, preferred_element_type=F32)
            dxdt = r_p * ds_p
            tt = r_p * xdt * ds_p
            db_acc = db_acc + lax.dot_general((xdt * ds_p).astype(BF), dh_b, NT_DIMS, preferred_element_type=F32)
            dgam_m = jnp.sum(dh * hp, axis=0, keepdims=True)
            for j, even in ((je, True), (jo, False)):
                sel = lane_even if even else jnp.logical_not(lane_even)
                dy_j = jnp.where(sel, dy_b, zero)
                l_j = _head_decay(cm, j)
                l_jt = _head_decay_t(cm, j)
                m_j = cm["cb"] * l_j
                m_jt = cb_t * l_jt
                dm = lax.dot_general(dy_j, xdt_b, NT_DIMS, preferred_element_type=F32)
                dm_t = lax.dot_general(xdt_b, dy_j, NT_DIMS, preferred_element_type=F32)
                dxdt = dxdt + jnp.dot(m_jt.astype(BF), dy_j, preferred_element_type=F32)
                dcb = dcb + dm * l_j
                dcb_t = dcb_t + dm_t * l_jt
                t_j = jnp.where(sel, tt, 0.0)
                col = jnp.sum(dm * m_j - dm_t * m_jt + (jnp.where(sel, de, 0.0) - t_j), axis=1, keepdims=True)
                gam_j = cm["gam"][:, j:j + 1]
                last = (jnp.sum(jnp.sum(t_j, axis=0, keepdims=True), axis=1, keepdims=True)
                        + jnp.sum(jnp.where(sel[0:1, :], dgam_m, 0.0), axis=1, keepdims=True) * gam_j)
                col = col + jnp.where(sub[:, 0:1] == Q - 1, last, 0.0)
                dcs = dcs + jnp.where(lane == j, col, 0.0)
            dh_scr[p] = dh_in + dh * _pair_row(lane_even, cm["gam"], je)
            dx_ref[:, 128 * p:128 * p + 128] = dxp + dxdt * dt_p
            dd = dxdt * xp
            ddt = ddt + jnp.where(lane == je, jnp.sum(jnp.where(lane_even, dd, 0.0), axis=1, keepdims=True), 0.0)
            ddt = ddt + jnp.where(lane == jo, jnp.sum(jnp.where(lane_even, 0.0, dd), axis=1, keepdims=True), 0.0)
        dc_ref[:, :] = dc_acc + jnp.dot(dcb.astype(BF), cm["bc"], preferred_element_type=F32)
        db_ref[:, :] = db_acc + jnp.dot(dcb_t.astype(BF), cm["cc"], preferred_element_type=F32)
        tri_t = (sub <= lane).astype(F32)
        dd_a = jnp.dot(tri_t, dcs, precision=lax.Precision.HIGHEST, preferred_element_type=F32)
        ddt = ddt + dd_a * a_neg
        acc_scr[5:6, :] += jnp.sum(dd_a * cm["dt"], axis=0, keepdims=True)
        draw = jnp.where(cm["live"], ddt * _sigmoid_gate(cm["dtr"] + bias), 0.0)
        acc_scr[4:5, :] += jnp.sum(draw, axis=0, keepdims=True)
        ddt_ref[:, :] = draw.astype(BF)

        @pl.when(ci == NCH - 1)
        def _():
            lane1 = _lanes((1, 128))
            dd = jnp.zeros((1, 128), F32)
            for p in range(4):
                row = acc_scr[p:p + 1, :]
                dd = dd + jnp.where(lane1 == 2 * p, jnp.sum(jnp.where(lane1 < 64, row, 0.0), axis=1, keepdims=True), 0.0)
                dd = dd + jnp.where(lane1 == 2 * p + 1, jnp.sum(jnp.where(lane1 < 64, 0.0, row), axis=1, keepdims=True), 0.0)
            dpar_ref[...] = jnp.concatenate([acc_scr[4:5, :], acc_scr[5:6, :] * a_neg, dd, jnp.zeros((5, 128), F32)], axis=0)

    par = _spec((2, 1, 128), lambda c: (0, 0, 0))
    wide = _spec((Q, SSD_W), lambda c: (NCH - 1 - c, 0))
    thin = _spec((Q, 256), lambda c: (NCH - 1 - c, 0))
    vec = _spec((1, SSD_W), lambda c: (0, 0))
    xbc = PXBC // XBC
    halo = 2 * HALO
    chunk = pltpu.VMEM((Q, XBC), F32)
    padded = pltpu.VMEM((Q + HALO, XBC), F32)
    return pl.pallas_call(
        body, grid=(NCH,),
        in_specs=[wide, _spec((Q, XBC), lambda c: (NCH - 1 - c, xbc)),
                  _spec((halo, XBC), lambda c: (jnp.maximum((NCH - 1 - c) * (Q // halo) - 1, 0), xbc)), thin, wide, wide,
                  _spec((2, 1, 4, 128, 128), lambda c: (0, NCH - 1 - c, 0, 0, 0)), _spec((4, XBC), lambda c: (0, 0)),
                  _spec((1, XBC), lambda c: (0, 0)), par, par, par, vec],
        out_specs=[wide, _spec((Q, XBC), lambda c: (NCH - 1 - c, 0)), thin, _spec((2, 8, 128), lambda c: (0, 0, 0)), vec,
                   _spec((4, XBC), lambda c: (0, 0)), _spec((1, XBC), lambda c: (0, 0))],
        out_shape=[jax.ShapeDtypeStruct((T, SSD_W), BF), jax.ShapeDtypeStruct((T, XBC), BF), jax.ShapeDtypeStruct((T, 256), BF),
                   jax.ShapeDtypeStruct((2, 8, 128), F32), jax.ShapeDtypeStruct((1, SSD_W), F32), jax.ShapeDtypeStruct((4, XBC), F32),
                   jax.ShapeDtypeStruct((1, XBC), F32)],
        scratch_shapes=[pltpu.VMEM((2, 4, 128, 128), F32), pltpu.VMEM((2, 8, 128), F32), padded, chunk, chunk, chunk, padded],
        compiler_params=_params(), name="ssd_bwd")(dyn, proj, proj, dt_raw, proj, y_pre, h_prev, conv_w, conv_b, dt_bias2, a_log2, d2, norm_w)


def _lru_gates(back, cw, cb, wa, ba, wx, bx, lam):
    xr = _conv(back, cw, cb)
    xr_b = xr.astype(BF)
    r = _sigmoid_gate(jnp.dot(xr_b, wa, preferred_element_type=F32) + ba)
    i = _sigmoid_gate(jnp.dot(xr_b, wx, preferred_element_type=F32) + bx)
    sp = _softplus(-lam)
    la = (-LRU_C) * r * sp
    a = jnp.exp(la)
    mult2 = -jnp.tanh(la) * (a * a + 1.0)
    return xr, xr_b, r, i, sp, a, jnp.sqrt(mult2), mult2


def lru_gates_fwd(proj, cw, cb, wa2, ba, wx2, bx, lam):
    def body(x_ref, cw_ref, cb_ref, wa_ref, ba_ref, wx_ref, bx_ref, lam_ref, a_ref, u_ref, xpad):
        _fill_padded(xpad, x_ref)

        def chunk(r0):
            xr, _, _, i, _, a, mult, _ = _lru_gates(_back(xpad, r0), cw_ref[...], cb_ref[...], wa_ref[0], ba_ref[...], wx_ref[0], bx_ref[...],
                                                 lam_ref[...])
            a_ref[pl.ds(r0, Q), :] = a
            u_ref[pl.ds(r0, Q), :] = jnp.where(_rows(a.shape, r0) >= NPAD, mult * (i * xr), 0.0)

        _chunks(chunk, unrolled=True)

    c0 = PXL // 128
    vec = _spec((1, 128), lambda c: (0, c))
    mat = _spec((1, 128, 128), lambda c: (c, 0, 0))
    return pl.pallas_call(
        body, grid=(8,),
        in_specs=[_spec((T, 128), lambda c: (0, c0 + c)), _spec((4, 128), lambda c: (0, c)), vec, mat, vec, mat, vec, vec],
        out_specs=[_spec((T, 128), lambda c: (0, c)), _spec((T, 128), lambda c: (0, c))],
        out_shape=[jax.ShapeDtypeStruct((T, LRU_W), F32), jax.ShapeDtypeStruct((T, LRU_W), F32)],
        scratch_shapes=[pltpu.VMEM((T + 2 * HALO, 128), F32)],
        compiler_params=_params(), name="lru_gates_fwd")(proj, cw, cb, wa2, ba, wx2, bx, lam)


def lru_scan_fwd(a, u):
    def body(a_ref, u_ref, h_ref):
        def step(i, h):
            base = pl.multiple_of(i * 8, 8)
            for k in range(8):
                h = a_ref[pl.ds(base + k, 1), :] * h + u_ref[pl.ds(base + k, 1), :]
                h_ref[pl.ds(base + k, 1), :] = h
            return h

        lax.fori_loop(0, T // 8, step, jnp.zeros((1, LRU_W), F32))

    return pl.pallas_call(body, out_shape=jax.ShapeDtypeStruct((T, LRU_W), F32), compiler_params=_params(0), name="lru_scan_fwd")(a, u)


def lru_scan_bwd(a, dh_out):
    def body(a_ref, d_ref, o_ref):
        def step(i, carry):
            base = pl.multiple_of(T - 8 - i * 8, 8)
            for k in range(7, -1, -1):
                carry = d_ref[pl.ds(base + k, 1), :] + carry
                o_ref[pl.ds(base + k, 1), :] = carry
                carry = carry * a_ref[pl.ds(base + k, 1), :]
            return carry

        lax.fori_loop(0, T // 8, step, jnp.zeros((1, LRU_W), F32))

    return pl.pallas_call(body, out_shape=jax.ShapeDtypeStruct((T, LRU_W), F32), compiler_params=_params(0), name="lru_scan_bwd")(a, dh_out)


def lru_gates_bwd(dhs, hseq, proj, cw, cb, wa2, ba, wx2, bx, lam):
    def body(dh_ref, h_ref, x_ref, cw_ref, cb_ref, wa_ref, ba_ref, wx_ref, bx_ref, lam_ref,
             dx_ref, dcw_ref, dcb_ref, dwa_ref, dba_ref, dwx_ref, dbx_ref, dlam_ref, xpad, hpad, dpad):
        _fill_padded(xpad, x_ref)
        _fill_padded(hpad, h_ref)
        dpad[0:HALO, :] = jnp.zeros((HALO, 128), F32)
        dpad[T + HALO:T + 2 * HALO, :] = jnp.zeros((HALO, 128), F32)
        for ref in (dcw_ref, dcb_ref, dwa_ref, dba_ref, dwx_ref, dbx_ref, dlam_ref):
            ref[...] = jnp.zeros_like(ref)
        lam = lam_ref[...]

        def first(r0):
            back = _back(xpad, r0)
            xr, xr_b, r, i, sp, a, mult, mult2 = _lru_gates(back, cw_ref[...], cb_ref[...], wa_ref[0], ba_ref[...], wx_ref[0], bx_ref[...], lam)
            dh = dh_ref[pl.ds(r0, Q), :]
            da = dh * _back(hpad, r0)(1)
            du = jnp.where(_rows(dh.shape, r0) >= NPAD, dh, 0.0)
            dmult = du * (i * xr)
            di = du * (mult * xr)
            dxr = du * (mult * i)
            dla = da * a - dmult * (a * a) * lax.rsqrt(mult2)
            dr = dla * ((-LRU_C) * sp)
            dlam_ref[...] += jnp.sum(dla * ((-LRU_C) * r), axis=0, keepdims=True)
            dpr = dr * r * (1.0 - r)
            dpi = di * i * (1.0 - i)
            dba_ref[...] += jnp.sum(dpr, axis=0, keepdims=True)
            dbx_ref[...] += jnp.sum(dpi, axis=0, keepdims=True)
            dpr_b = dpr.astype(BF)
            dpi_b = dpi.astype(BF)
            dxr = (dxr + lax.dot_general(dpr_b, wa_ref[0], NT_DIMS, preferred_element_type=F32)
                   + lax.dot_general(dpi_b, wx_ref[0], NT_DIMS, preferred_element_type=F32))
            dwa_ref[0] += lax.dot_general(xr_b, dpr_b, TN_DIMS, preferred_element_type=F32)
            dwx_ref[0] += lax.dot_general(xr_b, dpi_b, TN_DIMS, preferred_element_type=F32)
            dpad[pl.ds(r0 + HALO, Q), :] = dxr
            dcw, dcb = _conv_bwd_w(dxr, back)
            dcw_ref[...] += dcw
            dcb_ref[...] += dcb

        _chunks(first, unrolled=True)
        dlam_ref[...] = -dlam_ref[...] * _sigmoid_gate(-lam)

        def second(r0):
            dx_ref[pl.ds(r0, Q), :] = _conv_bwd_x(_ahead(dpad, r0), cw_ref[...]).astype(BF)

        _chunks(second)

    c0 = PXL // 128
    vec = _spec((1, 128), lambda c: (0, c))
    mat = _spec((1, 128, 128), lambda c: (c, 0, 0))
    col = _spec((T, 128), lambda c: (0, c))
    vshape = jax.ShapeDtypeStruct((1, LRU_W), F32)
    mshape = jax.ShapeDtypeStruct((8, 128, 128), F32)
    pad = pltpu.VMEM((T + 2 * HALO, 128), F32)
    return pl.pallas_call(
        body, grid=(8,),
        in_specs=[col, col, _spec((T, 128), lambda c: (0, c0 + c)), _spec((4, 128), lambda c: (0, c)), vec, mat, vec, mat, vec, vec],
        out_specs=[col, _spec((4, 128), lambda c: (0, c)), vec, mat, vec, mat, vec, vec],
        out_shape=[jax.ShapeDtypeStruct((T, LRU_W), BF), jax.ShapeDtypeStruct((4, LRU_W), F32), vshape, mshape, vshape, mshape, vshape, vshape],
        scratch_shapes=[pad, pad, pad], compiler_params=_params(), name="lru_gates_bwd")(dhs, hseq, proj, cw, cb, wa2, ba, wx2, bx, lam)


def gate_up(h1, wn, w_gate, w_up):
    def body(h_ref, wn_ref, wg_ref, wu_ref, gt_ref, up_ref, act_ref, u_ref):
        for r in (0, HALF):
            u_ref[r:r + HALF, :] = _rms(h_ref[r:r + HALF, :], wn_ref[...]).astype(BF)

        def tile(c0):
            cols = pl.ds(c0, 256)
            gt = lax.dot_general(u_ref[...], wg_ref[cols, :], NT_DIMS, preferred_element_type=F32)
            up = lax.dot_general(u_ref[...], wu_ref[cols, :], NT_DIMS, preferred_element_type=F32)
            gt_ref[:, cols] = gt.astype(BF)
            up_ref[:, cols] = up.astype(BF)
            act_ref[:, cols] = (gt * _sigmoid(gt) * up).astype(BF)

        _col_tiles(D_FF, 256, tile)

    big = jax.ShapeDtypeStruct((T, D_FF), BF)
    return pl.pallas_call(
        body, grid=(T // RC,), in_specs=[_rows_spec(D), _vec(D), _whole((D_FF, D)), _whole((D_FF, D))],
        out_specs=[_rows_spec(D_FF), _rows_spec(D_FF), _rows_spec(D_FF), _rows_spec(D)],
        out_shape=[big, big, big, jax.ShapeDtypeStruct((T, D), BF)],
        compiler_params=_params(), name="gate_up")(h1, wn, w_gate, w_up)


def down_loss(act, w_down, h1, target, wf):
    first = NPAD + N_META

    def body(a_ref, w_ref, r_ref, t_hbm, wf_ref, d_ref, db_ref, l_ref, dw_ref, h_scr, t_ref, t_sem):
        i = pl.program_id(0)
        _zero_at_first(l_ref, dw_ref)
        head = pltpu.make_async_copy(t_hbm.at[pl.ds(0, RC - first)], t_ref.at[pl.ds(first, RC - first)], t_sem)
        rest = pltpu.make_async_copy(t_hbm.at[pl.ds(pl.multiple_of(jnp.maximum(i * RC - first, 0), 32), RC)], t_ref, t_sem)

        @pl.when(i == 0)
        def _():
            t_ref[0:first, :] = jnp.zeros((first, D), F32)
            head.start()

        @pl.when(i > 0)
        def _():
            rest.start()

        def tile(c0):
            cols = pl.ds(c0, 512)
            h_scr[:, cols] = r_ref[:, cols] + jnp.dot(a_ref[...], w_ref[:, cols], preferred_element_type=F32)

        _col_tiles(D, 512, tile)

        @pl.when(i == 0)
        def _():
            head.wait()

        @pl.when(i > 0)
        def _():
            rest.wait()

        for r in (0, HALF):
            h = h_scr[r:r + HALF, :]
            live = _rows((HALF, D), i * RC + r) >= first
            err = jnp.where(live, _rms(h, wf_ref[...]) - t_ref[r:r + HALF, :], 0.0)
            l_ref[...] += 0.5 * jnp.sum(jnp.sum(err * err, axis=1, keepdims=True) * (1.0 / D), axis=0, keepdims=True)
            dh, dw = _rms_bwd(err * (1.0 / D), h, wf_ref[...])
            dw_ref[...] += jnp.sum(dw, axis=0, keepdims=True)
            d_ref[r:r + HALF, :] = dh
            db_ref[r:r + HALF, :] = dh.astype(BF)

    return pl.pallas_call(
        body, grid=(T // RC,),
        in_specs=[_rows_spec(D_FF), _whole((D_FF, D)), _rows_spec(D), pl.BlockSpec(memory_space=pl.ANY), _vec(D)],
        out_specs=[_rows_spec(D), _rows_spec(D), _spec((1, 128), lambda i: (0, 0)), _vec(D)],
        out_shape=[jax.ShapeDtypeStruct((T, D), F32), jax.ShapeDtypeStruct((T, D), BF), jax.ShapeDtypeStruct((1, 128), F32),
                   jax.ShapeDtypeStruct((1, D), F32)],
        scratch_shapes=[pltpu.VMEM((RC, D), F32), pltpu.VMEM((RC, D), F32), pltpu.SemaphoreType.DMA],
        compiler_params=_params(), name="down_loss")(act, w_down, h1, target, wf)


def swiglu_bwd(dh2_b, w_down, gt, up, act, u2):
    tn = 256

    def body(d_ref, u_ref, w_ref, gt_ref, up_ref, act_ref, dg_ref, du_ref, gd_ref, gg_ref, gu_ref, acc_d, acc_g, acc_u):
        for acc in (acc_d, acc_g, acc_u):
            acc[...] = jnp.zeros_like(acc)

        def rows(r0):
            part = pl.ds(r0, RC)
            d = d_ref[part, :]
            dact = lax.dot_general(d, w_ref[...], NT_DIMS, preferred_element_type=F32)
            gt_ = gt_ref[part, :].astype(F32)
            up_ = up_ref[part, :].astype(F32)
            sg = _sigmoid(gt_)
            dgt = (dact * up_ * (sg * (1.0 + gt_ * (1.0 - sg)))).astype(BF)
            dup = (dact * (gt_ * sg)).astype(BF)
            dg_ref[part, :] = dgt
            du_ref[part, :] = dup
            u = u_ref[part, :]
            acc_d[...] += lax.dot_general(act_ref[part, :], d, TN_DIMS, preferred_element_type=F32)
            acc_g[...] += lax.dot_general(dgt, u, TN_DIMS, preferred_element_type=F32)
            acc_u[...] += lax.dot_general(dup, u, TN_DIMS, preferred_element_type=F32)

        _col_tiles(T, RC, rows)
        gd_ref[...] = acc_d[...].astype(BF)
        gg_ref[...] = acc_g[...].astype(BF)
        gu_ref[...] = acc_u[...].astype(BF)

    resident = _spec((T, D), lambda j: (0, 0), single=True)
    cols = _spec((T, tn), lambda j: (0, j))
    wrow = _spec((tn, D), lambda j: (j, 0))
    big = jax.ShapeDtypeStruct((T, D_FF), BF)
    grad = jax.ShapeDtypeStruct((D_FF, D), BF)
    return pl.pallas_call(
        body, grid=(D_FF // tn,), in_specs=[resident, resident, wrow, cols, cols, cols],
        out_specs=[cols, cols, wrow, wrow, wrow], out_shape=[big, big, grad, grad, grad],
        scratch_shapes=[pltpu.VMEM((tn, D), F32)] * 3,
        compiler_params=_params(), name="swiglu_bwd")(dh2_b, u2, w_down, gt, up, act)


def gate_up_bwd(dgt, dup, w_gate, w_up, h1, wn, dh2):
    def body(dg_ref, du_ref, wg_ref, wu_ref, h_ref, wn_ref, r_ref, d_ref, db_ref, dw_ref, du_scr):
        _zero_at_first(dw_ref)

        du_scr[...] = jnp.zeros_like(du_scr)

        def tile(c0):
            k = pl.ds(c0, 256)
            du_scr[...] += (jnp.dot(dg_ref[:, k], wg_ref[k, :], preferred_element_type=F32)
                            + jnp.dot(du_ref[:, k], wu_ref[k, :], preferred_element_type=F32))

        _col_tiles(D_FF, 256, tile)
        for r in (0, HALF):
            dh, dw = _rms_bwd(du_scr[r:r + HALF, :], h_ref[r:r + HALF, :], wn_ref[...])
            dw_ref[...] += jnp.sum(dw, axis=0, keepdims=True)
            dh = dh + r_ref[r:r + HALF, :]
            d_ref[r:r + HALF, :] = dh
            db_ref[r:r + HALF, :] = dh.astype(BF)

    return pl.pallas_call(
        body, grid=(T // RC,),
        in_specs=[_rows_spec(D_FF), _rows_spec(D_FF), _whole((D_FF, D)), _whole((D_FF, D)), _rows_spec(D), _vec(D), _rows_spec(D)],
        out_specs=[_rows_spec(D), _rows_spec(D), _vec(D)],
        out_shape=[jax.ShapeDtypeStruct((T, D), F32), jax.ShapeDtypeStruct((T, D), BF), jax.ShapeDtypeStruct((1, D), F32)],
        scratch_shapes=[pltpu.VMEM((RC, D), F32)],
        compiler_params=_params(), name="gate_up_bwd")(dgt, dup, w_gate, w_up, h1, wn, dh2)


def _adamw(w, g, m, v):
    m = ADAM_B1 * m + (1.0 - ADAM_B1) * g
    v = ADAM_B2 * v + (1.0 - ADAM_B2) * (g * g)
    m_hat = m / (1.0 - ADAM_B1 ** ADAM_STEP)
    v_hat = v / (1.0 - ADAM_B2 ** ADAM_STEP)
    delta = -ADAM_LR * (m_hat / (jnp.sqrt(v_hat) + ADAM_EPS) + ADAM_WD * w)
    return delta, m, v


def adamw_shards(name, recvs, ws, ms, vs):
    n = len(ws)

    def body(*refs):
        ins, outs = refs[:4 * n], refs[4 * n:]
        for k in range(n):
            p_ref, w_ref, m_ref, v_ref = ins[k], ins[n + k], ins[2 * n + k], ins[3 * n + k]
            g = p_ref[0].astype(F32)
            for s in range(1, 8):
                g = g + p_ref[s].astype(F32)
            outs[4 * k][...] = g
            outs[4 * k + 1][...], outs[4 * k + 2][...], outs[4 * k + 3][...] = _adamw(w_ref[...], g, m_ref[...], v_ref[...])

    tiles = [_spec((w.shape[0] // 2, w.shape[1]), lambda i: (i, 0)) for w in ws]
    recv_tiles = [_spec((8, w.shape[0] // 2, w.shape[1]), lambda i: (0, i, 0)) for w in ws]
    res = pl.pallas_call(
        body, grid=(2,), in_specs=recv_tiles + tiles * 3,
        out_specs=[t for t in tiles for _ in range(4)],
        out_shape=[jax.ShapeDtypeStruct(w.shape, F32) for w in ws for _ in range(4)],
        compiler_params=_params(), name=name)(*recvs, *ws, *ms, *vs)
    return [list(res[4 * k:4 * k + 4]) for k in range(n)]


def adamw_w_in(recv, w, m, v):
    rows = w.shape[0] // 8

    def body(p_ref, w_ref, m_ref, v_ref, g_ref, d_ref, mo_ref, vo_ref):
        for q in range(8):
            cols = slice(128 * q, 128 * q + 128)
            g = p_ref[0, :, cols].astype(F32)
            for s in range(1, 8):
                g = g + p_ref[s, :, cols].astype(F32)
            part = pl.ds(q, rows, stride=8)
            g_ref[part, :] = g
            d_ref[part, :], mo_ref[part, :], vo_ref[part, :] = _adamw(w_ref[part, :], g, m_ref[part, :], v_ref[part, :])

    shape = jax.ShapeDtypeStruct(w.shape, F32)
    return pl.pallas_call(body, out_shape=[shape] * 4, compiler_params=_params(0), name="adamw_w_in")(recv, w, m, v)


def sum_slabs(recv):
    def body(p_ref, o_ref):
        g = p_ref[0]
        for s in range(1, 8):
            g = g + p_ref[s]
        o_ref[...] = g

    return pl.pallas_call(body, out_shape=jax.ShapeDtypeStruct(recv.shape[1:], F32), compiler_params=_params(0), name="sum_slabs")(recv)


SIMPLE = [("norm1_w", 1024), ("ssd_conv_b", 1536), ("ssd_dt_bias", 16), ("ssd_a_log", 16), ("ssd_d", 16), ("ssd_norm_w", 1024),
          ("lru_conv_b", 1024), ("lru_ba", 1024), ("lru_bx", 1024), ("lru_lambda", 1024), ("lru_norm_w", 1024), ("norm2_w", 1024),
          ("final_norm_w", 1024)]
SPECIAL = ["lru_wa", "lru_wx", "meta_tokens", "ssd_conv_w", "lru_conv_w"]
SM_ROWS = 176
SM_WA, SM_WX, SM_META, SM_SCW, SM_LCW, SM_LOSS = 14, 78, 142, 158, 166, 170


def _simple_rows():
    rows, r = {}, 0
    for name, n in SIMPLE:
        rows[name] = r
        r += -(-n // 1024)
    return rows


def adamw_small(sm, special_g, ws, ms, vs):
    rows = _simple_rows()
    ns, nx = len(SIMPLE), len(SPECIAL)

    def body(*refs):
        sm_ref = refs[0]
        gx = refs[1:1 + nx]
        wr = refs[1 + nx:1 + nx + ns + nx]
        mr = refs[1 + nx + ns + nx:1 + nx + 2 * (ns + nx)]
        vr = refs[1 + nx + 2 * (ns + nx):1 + nx + 3 * (ns + nx)]
        outs = refs[1 + nx + 3 * (ns + nx):]
        o = 0
        for k, (name, n) in enumerate(SIMPLE):
            r0 = rows[name]
            for c0 in range(0, n, 1024):
                wd = min(1024, n - c0)
                g = sm_ref[r0 + c0 // 1024:r0 + c0 // 1024 + 1, 0:wd]
                sl = (slice(None), slice(c0, c0 + wd))
                d, m2, v2 = _adamw(wr[k][sl], g, mr[k][sl], vr[k][sl])
                outs[o][sl] = g
                outs[o + 1][sl] = d
                outs[o + 2][sl] = m2
                outs[o + 3][sl] = v2
            o += 4
        for k in range(nx):
            d, m2, v2 = _adamw(wr[ns + k][...], gx[k][...], mr[ns + k][...], vr[ns + k][...])
            outs[o][...] = d
            outs[o + 1][...] = m2
            outs[o + 2][...] = v2
            o += 3

    out_shape = []
    for k in range(ns):
        out_shape += [jax.ShapeDtypeStruct(ws[k].shape, F32)] * 4
    for k in range(nx):
        out_shape += [jax.ShapeDtypeStruct(ws[ns + k].shape, F32)] * 3
    return pl.pallas_call(body, out_shape=out_shape, compiler_params=_params(0), name="adamw_small")(sm, *special_g, *ws, *ms, *vs)


def _place():
    return lax.axis_index("x"), lax.axis_index("y"), lax.axis_index("c")


def _index(px, py, pc):
    return 4 * px + 2 * py + pc


def all_gather(name, shards):
    n = len(shards)
    hbm = pl.BlockSpec(memory_space=pl.ANY)

    def body(*refs):
        ins, outs = refs[:n], refs[n:2 * n]
        send_sems, recv_sems, local_sems = refs[2 * n:]
        x, y, c = _place()
        me, sibling = (x, y, c), (x, y, 1 - c)
        chips = [(1 - x, y), (x, 1 - y), (1 - x, 1 - y)]

        def copy(i, k, block, to, src=None):
            dst = outs[i].at[_index(*block)]
            return pltpu.make_async_remote_copy(src_ref=dst if src is None else src, dst_ref=dst, send_sem=send_sems.at[7 * i + k],
                                                recv_sem=recv_sems.at[7 * i + k], device_id=to, device_id_type=MESH)

        mine = [pltpu.make_async_copy(ins[i], outs[i].at[_index(*me)], local_sems.at[i]) for i in range(n)]
        for cp in mine:
            cp.start()
        first = []
        for i in range(n):
            first += [copy(i, 1 + j, me, (*chip, c), src=ins[i]) for j, chip in enumerate(chips)]
            first.append(copy(i, 0, me, sibling, src=ins[i]))
        for cp in first:
            cp.start()
        passed = []
        for i in range(n):
            for j, chip in enumerate(chips):
                copy(i, 1 + j, (*chip, c), me).wait_recv()
                cp = copy(i, 4 + j, (*chip, c), sibling)
                cp.start()
                passed.append(cp)
        for i in range(n):
            copy(i, 0, sibling, me).wait_recv()
            for j, chip in enumerate(chips):
                copy(i, 4 + j, (*chip, 1 - c), me).wait_recv()
        for cp in first + passed:
            cp.wait_send()
        for cp in mine:
            cp.wait()

    return pl.pallas_call(
        body, in_specs=[hbm] * n, out_specs=[hbm] * n,
        out_shape=[jax.ShapeDtypeStruct((8,) + s.shape, s.dtype) for s in shards],
        scratch_shapes=[pltpu.SemaphoreType.DMA((7 * n,)), pltpu.SemaphoreType.DMA((7 * n,)), pltpu.SemaphoreType.DMA((n,))],
        name=name)(*shards)


HBM_SPEC = pl.BlockSpec(memory_space=pltpu.HBM)
SEM_SPEC = pl.BlockSpec(memory_space=pltpu.SEMAPHORE)
EFFECT = pltpu.SideEffectType.DATAFLOW_SIDE_EFFECTING


def _peers(x, y, c):
    return [((1 - x) if k & 4 else x, (1 - y) if k & 2 else y, (1 - c) if k & 1 else c) for k in range(1, 8)]


def _pieces(rows):
    for n in (4, 2):
        if rows % (16 * n) == 0:
            return [(r * (rows // n), rows // n) for r in range(n)]
    return [(0, rows)]


def _peer_copies(src, land, send_sems, recv_sems, k, peer, mine, slab_src):
    block = src.at[_index(*peer)] if slab_src else src
    return [pltpu.make_async_remote_copy(src_ref=block.at[pl.ds(r0, nr)], dst_ref=land.at[mine, pl.ds(r0, nr)], send_sem=send_sems.at[k],
                                         recv_sem=recv_sems.at[k], device_id=peer, device_id_type=MESH)
            for r0, nr in _pieces(block.shape[0])]


def copies_start(name, srcs, slab_src, after):
    n = len(srcs)
    zones = [jax.ShapeDtypeStruct(s.shape if slab_src else (8,) + s.shape, s.dtype) for s in srcs]
    afters = [] if after is None else [after]

    def body(*refs):
        ins, lands = refs[:n], refs[n:2 * n]
        first = 2 * n + len(afters)
        sends, recvs = refs[first:first + n], refs[first + n:first + 2 * n]
        token = refs[-1]
        x, y, c = _place()
        mine = _index(x, y, c)
        for i in range(n):
            per_peer = [_peer_copies(ins[i], lands[i], sends[i], recvs[i], k, peer, mine, slab_src) for k, peer in enumerate(_peers(x, y, c))]
            for piece in zip(*per_peer):
                for cp in piece:
                    cp.start()
        token[...] = jnp.zeros_like(token)

    sem = pltpu.SemaphoreType.DMA((7,))
    res = pl.pallas_call(
        body, name=name,
        out_shape=([sem] * (2 * n) + [pltpu.HBM(s.shape, s.dtype) for s in srcs] + [pltpu.HBM(z.shape, z.dtype) for z in zones]
                   + [jax.ShapeDtypeStruct((8, 128), F32)]),
        in_specs=[HBM_SPEC] * (2 * n) + [pl.BlockSpec(memory_space=pl.ANY)] * len(afters),
        out_specs=[SEM_SPEC] * (2 * n) + [HBM_SPEC] * (2 * n) + [pl.BlockSpec(memory_space=pltpu.VMEM)],
        input_output_aliases={i: 2 * n + i for i in range(2 * n)},
        compiler_params=pltpu.CompilerParams(has_side_effects=EFFECT),
    )(*[pltpu.with_memory_space_constraint(s, pltpu.HBM) for s in srcs],
      *[pltpu.with_memory_space_constraint(lax.empty(z.shape, z.dtype), pltpu.HBM) for z in zones], *afters)
    return [(res[i], res[n + i], res[2 * n + i], res[3 * n + i]) for i in range(n)], res[-1][0:1, 0:1]


def copies_wait(name, started, slab_src, after):
    n = len(started)

    def body(*refs):
        ins, lands = refs[:n], refs[n:2 * n]
        sends, recvs = refs[2 * n:3 * n], refs[3 * n:4 * n]
        x, y, c = _place()
        mine = _index(x, y, c)
        for i in range(n):
            for k, peer in enumerate(_peers(x, y, c)):
                arrival = pltpu.make_async_remote_copy(src_ref=ins[i].at[mine] if slab_src else ins[i], dst_ref=lands[i].at[_index(*peer)],
                                                       send_sem=sends[i].at[k], recv_sem=recvs[i].at[k], device_id=peer, device_id_type=MESH)
                arrival.wait_send()
                arrival.wait_recv()

    srcs = [s[2] for s in started]
    lands = [s[3] for s in started]
    afters = list(after) if isinstance(after, (list, tuple)) else [after]
    res = pl.pallas_call(
        body, name=name,
        out_shape=[pltpu.HBM(s.shape, s.dtype) for s in srcs] + [pltpu.HBM(z.shape, z.dtype) for z in lands],
        in_specs=[HBM_SPEC] * (2 * n) + [SEM_SPEC] * (2 * n) + [pl.BlockSpec(memory_space=pl.ANY)] * len(afters),
        out_specs=[HBM_SPEC] * (2 * n),
        input_output_aliases={i: i for i in range(2 * n)},
        compiler_params=pltpu.CompilerParams(has_side_effects=EFFECT),
    )(*srcs, *lands, *[s[0] for s in started], *[s[1] for s in started], *afters)
    me = _index(*_place())
    own = [lax.dynamic_index_in_dim(s, me, 0, keepdims=True) if slab_src else s[None] for s in res[:n]]
    return [lax.dynamic_update_slice_in_dim(z, o, me, 0) for z, o in zip(res[n:], own)]


WEIGHTS = ["meta_tokens", "norm1_w", "w_in", "ssd_conv_w", "ssd_conv_b", "ssd_dt_bias", "ssd_a_log", "ssd_d", "ssd_norm_w", "lru_conv_w",
           "lru_conv_b", "lru_wa", "lru_ba", "lru_wx", "lru_bx", "lru_lambda", "lru_norm_w", "w_out", "norm2_w", "w_gate", "w_up", "w_down",
           "final_norm_w"]
BIG = ["w_in", "w_out", "w_gate", "w_up", "w_down"]
COLUMN_SHARDED = ["w_in", "w_gate", "w_up"]


def _pair_blocks(w):
    w = w.reshape(8, 2, 64, 64)
    z = jnp.zeros((8, 64, 64), w.dtype)
    return jnp.concatenate([jnp.concatenate([w[:, 0], z], axis=2), jnp.concatenate([z, w[:, 1]], axis=2)], axis=1)


def _unpair_blocks(w2):
    return jnp.stack([w2[:, :64, :64], w2[:, 64:, 64:]], axis=1).reshape(16, 64, 64)


def _per_group(v):
    return jnp.pad(v.reshape(2, 1, 8), ((0, 0), (0, 0), (0, 120)))


def _pad_cols(v, n):
    return jnp.pad(v, ((0, 0), (0, n - v.shape[1])))


def local_step(x, target, meta, ssd_cw, lru_cw, w_in, fetch, send, p):
    z120 = jnp.zeros((120, D), BF)
    w_dt = jnp.concatenate([w_in[2560:2568], z120, w_in[2568:2576], z120], axis=0)
    bias2, alog2, d2 = _per_group(p["ssd_dt_bias"]), _per_group(p["ssd_a_log"]), _per_group(p["ssd_d"])
    wa2 = _pair_blocks(p["lru_wa"]).astype(BF)
    wx2 = _pair_blocks(p["lru_wx"]).astype(BF)
    lru = (lru_cw, p["lru_conv_b"], wa2, p["lru_ba"], wx2, p["lru_bx"], p["lru_lambda"])

    h0 = jnp.concatenate([jnp.zeros((NPAD, D), F32), meta, x], axis=0)
    proj, dt_raw, u1 = in_proj(h0, p["norm1_w"], w_in, w_dt)
    yn_ssd, y_pre, h_prev = ssd_fwd(proj, dt_raw, ssd_cw, p["ssd_conv_b"], bias2, alog2, d2, p["ssd_norm_w"])
    a, u = lru_gates_fwd(proj, *lru)
    hseq = lru_scan_fwd(a, u)
    (w_out,) = fetch(["w_out"], hseq)
    h1, cat = out_proj(yn_ssd, proj, hseq, p["lru_norm_w"], w_out, h0)
    w_gate, w_up = fetch(["w_gate", "w_up"], h1)
    gt, up, act, u2 = gate_up(h1, p["norm2_w"], w_gate, w_up)
    (w_down,) = fetch(["w_down"], act)
    dh2, dh2_b, loss, d_fnw = down_loss(act, w_down, h1, target, p["final_norm_w"])

    dgt, dup, g_down, g_gate, g_up = swiglu_bwd(dh2_b, w_down, gt, up, act, u2)
    sent = send({"w_down": g_down, "w_gate": g_gate, "w_up": g_up})
    dh1, dh1_b, d_n2 = gate_up_bwd(dgt, dup, w_gate, w_up, h1, p["norm2_w"] + sent, dh2)
    sent = send({"w_out": weight_grad("dw_out", [cat], dh1_b)})
    dyn, dh_out, dg_b, d_lnw = out_proj_bwd(dh1_b, w_out, proj, hseq, p["lru_norm_w"] + sent)

    dhs = lru_scan_bwd(a, dh_out)
    dxl_b, d_lcw, d_lcb, dwa2, d_ba, dwx2, d_bx, d_lam = lru_gates_bwd(dhs, hseq, proj, *lru)
    dz_b, dxbc_b, ddt_b, dpar, d_snw, d_scw, d_scb = ssd_bwd(dyn, proj, dt_raw, ssd_cw, p["ssd_conv_b"], y_pre, h_prev, bias2, alog2, d2,
                                                             p["ssd_norm_w"] + sent)
    g_p = weight_grad("dw_in", [dz_b, dg_b, dxl_b, dxbc_b, ddt_b], u1)
    g_in = jnp.concatenate([g_p[PZ:PZ + 1024], g_p[PXBC:PXBC + XBC], g_p[NP_IN:NP_IN + 8], g_p[NP_IN + 128:NP_IN + 136],
                            g_p[PG:PG + 1024], g_p[PXL:PXL + 1024]], axis=0)
    sent = send({"w_in": g_in})
    grad_x, d_meta, d_n1 = in_proj_bwd(dz_b, dg_b, dxl_b, dxbc_b, ddt_b, w_in, w_dt, h0, p["norm1_w"] + sent, dh1)
    small = {"norm1_w": d_n1, "ssd_conv_b": d_scb, "ssd_dt_bias": dpar[:, 0, :8].reshape(1, 16), "ssd_a_log": dpar[:, 1, :8].reshape(1, 16),
             "ssd_d": dpar[:, 2, :8].reshape(1, 16), "ssd_norm_w": d_snw, "lru_conv_b": d_lcb, "lru_ba": d_ba, "lru_bx": d_bx,
             "lru_lambda": d_lam, "lru_norm_w": d_lnw, "norm2_w": d_n2, "final_norm_w": d_fnw,
             "lru_wa": _unpair_blocks(dwa2), "lru_wx": _unpair_blocks(dwx2), "meta_tokens": d_meta,
             "ssd_conv_w": d_scw, "lru_conv_w": d_lcw}
    return loss, grad_x, small


def _pack_small(small, loss):
    rows = [_pad_cols(small[name], -(-n // 1024) * 1024).reshape(-1, 1024) for name, n in SIMPLE]
    rows += [small["lru_wa"].reshape(64, 1024), small["lru_wx"].reshape(64, 1024), small["meta_tokens"],
             _pad_cols(small["ssd_conv_w"], 2048).reshape(8, 1024), small["lru_conv_w"], _pad_cols(loss[:, 0:1], 1024)]
    sm = jnp.concatenate(rows, axis=0)
    return jnp.pad(sm, ((0, SM_ROWS - sm.shape[0]), (0, 0)))


def _slabs(g):
    return g.reshape(8, g.shape[0] // 8, g.shape[1])


def _unslab(g):
    return g.reshape(8 * g.shape[1], g.shape[2])


def kernel(x, meta_tokens, norm1_w, w_in, ssd_conv_w, ssd_conv_b, ssd_dt_bias, ssd_a_log, ssd_d, ssd_norm_w, lru_conv_w, lru_conv_b, lru_wa, lru_ba, lru_wx, lru_bx, lru_lambda, lru_norm_w, w_out, norm2_w, w_gate, w_up, w_down, final_norm_w, loss_target, m_meta_tokens, m_norm1_w, m_w_in, m_ssd_conv_w, m_ssd_conv_b, m_ssd_dt_bias, m_ssd_a_log, m_ssd_d, m_ssd_norm_w, m_lru_conv_w, m_lru_conv_b, m_lru_wa, m_lru_ba, m_lru_wx, m_lru_bx, m_lru_lambda, m_lru_norm_w, m_w_out, m_norm2_w, m_w_gate, m_w_up, m_w_down, m_final_norm_w, v_meta_tokens, v_norm1_w, v_w_in, v_ssd_conv_w, v_ssd_conv_b, v_ssd_dt_bias, v_ssd_a_log, v_ssd_d, v_ssd_norm_w, v_lru_conv_w, v_lru_conv_b, v_lru_wa, v_lru_ba, v_lru_wx, v_lru_bx, v_lru_lambda, v_lru_norm_w, v_w_out, v_norm2_w, v_w_gate, v_w_up, v_w_down, v_final_norm_w):
    w = dict(meta_tokens=meta_tokens, norm1_w=norm1_w, w_in=w_in[0], ssd_conv_w=ssd_conv_w[0], ssd_conv_b=ssd_conv_b, ssd_dt_bias=ssd_dt_bias,
             ssd_a_log=ssd_a_log, ssd_d=ssd_d, ssd_norm_w=ssd_norm_w, lru_conv_w=lru_conv_w[0], lru_conv_b=lru_conv_b, lru_wa=lru_wa[0],
             lru_ba=lru_ba, lru_wx=lru_wx[0], lru_bx=lru_bx, lru_lambda=lru_lambda, lru_norm_w=lru_norm_w, w_out=w_out[0], norm2_w=norm2_w,
             w_gate=w_gate[0], w_up=w_up[0], w_down=w_down[0], final_norm_w=final_norm_w.reshape(1, D))
    m = dict(meta_tokens=m_meta_tokens, norm1_w=m_norm1_w, w_in=m_w_in[0], ssd_conv_w=m_ssd_conv_w[0], ssd_conv_b=m_ssd_conv_b,
             ssd_dt_bias=m_ssd_dt_bias, ssd_a_log=m_ssd_a_log, ssd_d=m_ssd_d, ssd_norm_w=m_ssd_norm_w, lru_conv_w=m_lru_conv_w[0],
             lru_conv_b=m_lru_conv_b, lru_wa=m_lru_wa[0], lru_ba=m_lru_ba, lru_wx=m_lru_wx[0], lru_bx=m_lru_bx, lru_lambda=m_lru_lambda,
             lru_norm_w=m_lru_norm_w, w_out=m_w_out[0], norm2_w=m_norm2_w, w_gate=m_w_gate[0], w_up=m_w_up[0], w_down=m_w_down[0],
             final_norm_w=m_final_norm_w.reshape(1, D))
    v = dict(meta_tokens=v_meta_tokens, norm1_w=v_norm1_w, w_in=v_w_in[0], ssd_conv_w=v_ssd_conv_w[0], ssd_conv_b=v_ssd_conv_b,
             ssd_dt_bias=v_ssd_dt_bias, ssd_a_log=v_ssd_a_log, ssd_d=v_ssd_d, ssd_norm_w=v_ssd_norm_w, lru_conv_w=v_lru_conv_w[0],
             lru_conv_b=v_lru_conv_b, lru_wa=v_lru_wa[0], lru_ba=v_lru_ba, lru_wx=v_lru_wx[0], lru_bx=v_lru_bx, lru_lambda=v_lru_lambda,
             lru_norm_w=v_lru_norm_w, w_out=v_w_out[0], norm2_w=v_norm2_w, w_gate=v_w_gate[0], w_up=v_w_up[0], w_down=v_w_down[0],
             final_norm_w=v_final_norm_w.reshape(1, D))
    shapes = dict(meta_tokens=meta_tokens.shape, norm1_w=norm1_w.shape, w_in=w_in.shape, ssd_conv_w=ssd_conv_w.shape,
                  ssd_conv_b=ssd_conv_b.shape, ssd_dt_bias=ssd_dt_bias.shape, ssd_a_log=ssd_a_log.shape, ssd_d=ssd_d.shape,
                  ssd_norm_w=ssd_norm_w.shape, lru_conv_w=lru_conv_w.shape, lru_conv_b=lru_conv_b.shape, lru_wa=lru_wa.shape,
                  lru_ba=lru_ba.shape, lru_wx=lru_wx.shape, lru_bx=lru_bx.shape, lru_lambda=lru_lambda.shape, lru_norm_w=lru_norm_w.shape,
                  w_out=w_out.shape, norm2_w=norm2_w.shape, w_gate=w_gate.shape, w_up=w_up.shape, w_down=w_down.shape,
                  final_norm_w=final_norm_w.shape)
    me = _index(*_place())
    for n in COLUMN_SHARDED:
        w[n], m[n], v[n] = w[n].T, m[n].T, v[n].T

    small_shard = jnp.concatenate([w["meta_tokens"], _pad_cols(w["ssd_conv_w"], 256).reshape(8, 128), w["lru_conv_w"],
                                   jnp.zeros((4, 128), F32)], axis=0)
    g_in, gs = all_gather("gather_w_in", [w["w_in"].astype(BF), small_shard])
    later = ["w_out", "w_gate", "w_up", "w_down"]
    started, behind = copies_start("gather_rest_start", [w[n].astype(BF) for n in later], False, gs)
    started = dict(zip(later, started))
    meta_full = gs[:, 0:16].transpose(1, 0, 2).reshape(N_META, D)
    ssd_cw = gs[:, 16:24].reshape(8, 4, 256)[:, :, :192].transpose(1, 0, 2).reshape(4, XBC)
    lru_cw = gs[:, 24:28].transpose(1, 0, 2).reshape(4, LRU_W)

    def fetch(names, after):
        got = copies_wait("gather_" + names[0] + "_wait", [started[n] for n in names], False, after)
        return [_unslab(g) for g in got]

    in_flight = {}

    def send(grads):
        names = list(grads)
        st, token = copies_start("grads_" + names[0] + "_start", [grads[n] if n == "small" else _slabs(grads[n]) for n in names], True, None)
        in_flight.update(zip(names, st))
        return token

    loss, grad_x, small = local_step(x[0], loss_target[0], meta_full, ssd_cw, lru_cw, _unslab(g_in), fetch, send,
                                     {**w, "norm1_w": w["norm1_w"] + behind})
    send({"small": _pack_small(small, loss).reshape(8, SM_ROWS // 8, 1024)})

    out = {}
    early = ["w_down", "w_gate", "w_up", "w_out"]
    recv = dict(zip(early, copies_wait("grads_early_wait", [in_flight[n] for n in early], True, in_flight["small"][2])))
    for pair in (early[:2], early[2:]):
        done = adamw_shards("adamw_" + pair[0], [recv[n] for n in pair], [w[n] for n in pair], [m[n] for n in pair], [v[n] for n in pair])
        out.update(zip(pair, done))
    recv_in, recv_small = copies_wait("grads_late_wait", [in_flight["w_in"], in_flight["small"]], True, [out[n][0] for n in early])
    untiled = (IN_COLS, 128)
    out["w_in"] = [o.reshape(IN_COLS // 8, D)
                   for o in adamw_w_in(recv_in, w["w_in"].reshape(untiled), m["w_in"].reshape(untiled), v["w_in"].reshape(untiled))]
    for n in COLUMN_SHARDED:
        out[n] = [o.T for o in out[n]]
    sm = all_gather("gather_small_grads", [sum_slabs(recv_small)])[0].reshape(SM_ROWS, 1024)
    special_g =[sm[SM_WA:SM_WA + 64].reshape(16, 64, 64), sm[SM_WX:SM_WX + 64].reshape(16, 64, 64),
                 lax.dynamic_slice(sm[SM_META:SM_META + 16], (0, 128 * me), (16, 128)),
                 lax.dynamic_slice(sm[SM_SCW:SM_SCW + 8].reshape(4, 2048), (0, 192 * me), (4, 192)),
                 lax.dynamic_slice(sm[SM_LCW:SM_LCW + 4], (0, 128 * me), (4, 128))]
    names = [n for n, _ in SIMPLE] + SPECIAL
    res = adamw_small(sm, special_g, [w[n] for n in names], [m[n] for n in names], [v[n] for n in names])
    for k, (n, _) in enumerate(SIMPLE):
        out[n] = res[4 * k:4 * k + 4]
    for k, n in enumerate(SPECIAL):
        o = 4 * len(SIMPLE) + 3 * k
        out[n] = [special_g[k]] + list(res[o:o + 3])
    loss_total = sm[SM_LOSS, 0]
    flat = [loss_total, grad_x[None]]
    for k in range(4):
        flat += [out[n][k].reshape(shapes[n]) for n in WEIGHTS]
    return tuple(flat)
```

```python
import math

import jax
import jax.numpy as jnp
from jax import lax
from jax.experimental import pallas as pl
from jax.experimental.pallas import tpu as pltpu

F32 = jnp.float32
BF = jnp.bfloat16

D = 1024
SEQ = 2048
N_META = 16
Q = 128
NPAD = 112
T = NPAD + N_META + SEQ
NCH = T // Q
RC = 544
D_FF = 2816
SSD_W = 1024
LRU_W = 1024
XBC = 1536
IN_COLS = 4624
PZ, PG, PXL, PXBC = 0, 1024, 2048, 3072
NP_IN = 4608
EPS = 1e-6
LRU_C = 8.0
VMEM_LIMIT = 56 * 1024 * 1024

ADAM_LR, ADAM_B1, ADAM_B2, ADAM_EPS, ADAM_WD, ADAM_STEP = 0.001, 0.9, 0.999, 1e-08, 0.01, 10

NT_DIMS = (((1,), (1,)), ((), ()))
TN_DIMS = (((0,), (0,)), ((), ()))
MESH = pl.DeviceIdType.MESH


def _params(n_grid=1, limit=VMEM_LIMIT):
    return pltpu.CompilerParams(dimension_semantics=("arbitrary",) * n_grid, vmem_limit_bytes=limit)


def _spec(shape, imap, single=False):
    if single:
        return pl.BlockSpec(shape, imap, pipeline_mode=pl.Buffered(1))
    return pl.BlockSpec(shape, imap)


def _sigmoid(x):
    return 0.5 * jnp.tanh(0.5 * x) + 0.5


def _sigmoid_gate(x):
    return 1.0 / (1.0 + jnp.exp(-x))


def _softplus(x):
    return jnp.maximum(x, 0.0) + jnp.log(1.0 + jnp.exp(-jnp.abs(x)))


def _rms_stats(h):
    return lax.rsqrt(jnp.mean(h * h, axis=-1, keepdims=True) + EPS)


def _rms(h, w):
    return (h * _rms_stats(h)) * w


def _rms_bwd(du, h, w):
    r = _rms_stats(h)
    n = h * r
    dn = du * w
    dh = r * (dn - n * jnp.mean(dn * n, axis=-1, keepdims=True))
    return dh, du * n


_G0 = math.sqrt(2.0 / math.pi)


def _gelu(x):
    return 0.5 * x * (1.0 + jnp.tanh(_G0 * (x + 0.044715 * (x * x * x))))


def _gelu_grad(x):
    t = jnp.tanh(_G0 * (x + 0.044715 * (x * x * x)))
    return 0.5 * (1.0 + t) + 0.5 * x * (1.0 - t * t) * (_G0 * (1.0 + 3.0 * 0.044715 * (x * x)))


def _rows(shape, r0=0):
    return lax.broadcasted_iota(jnp.int32, shape, 0) + r0


def _lanes(shape):
    return lax.broadcasted_iota(jnp.int32, shape, 1)


HALO = 8


def _fill_padded(pad_ref, x_ref):
    pad_ref[0:HALO, :] = jnp.zeros((HALO, pad_ref.shape[1]), F32)
    pad_ref[T + HALO:T + 2 * HALO, :] = jnp.zeros((HALO, pad_ref.shape[1]), F32)

    def step(c, carry):
        r0 = pl.multiple_of(c * Q, Q)
        pad_ref[pl.ds(r0 + HALO, Q), :] = x_ref[pl.ds(r0, Q), :].astype(F32)
        return carry

    lax.fori_loop(0, NCH, step, 0)


def _back(pad_ref, r0):
    win = pad_ref[pl.ds(r0, Q + HALO), :]
    return lambda s: win[HALO:, :] if s == 0 else pltpu.roll(win, s, axis=0)[HALO:, :]


def _ahead(pad_ref, r0):
    win = pad_ref[pl.ds(r0 + HALO, Q + HALO), :]
    return lambda s: win[:Q, :] if s == 0 else pltpu.roll(win, Q + HALO - s, axis=0)[:Q, :]


def _conv(back, w, b):
    y = b + w[3:4, :] * back(0)
    for k in range(3):
        y = y + w[k:k + 1, :] * back(3 - k)
    return y


def _conv_bwd_x(ahead, w):
    dx = w[3:4, :] * ahead(0)
    for k in range(3):
        dx = dx + w[k:k + 1, :] * ahead(3 - k)
    return dx


def _conv_bwd_w(dy, back):
    dws = [jnp.sum(dy * back(3 - k), axis=0, keepdims=True) for k in range(4)]
    return jnp.concatenate(dws, axis=0), jnp.sum(dy, axis=0, keepdims=True)


def _chunks(fn, unrolled=False):
    if unrolled:
        for c in range(NCH):
            fn(c * Q)
        return

    def step(c, carry):
        fn(pl.multiple_of(c * Q, Q))
        return carry

    lax.fori_loop(0, NCH, step, 0)


HALF = RC // 2


def _col_tiles(n, tn, fn):
    def step(j, carry):
        fn(pl.multiple_of(j * tn, tn))
        return carry

    lax.fori_loop(0, n // tn, step, 0)


def _rows_spec(cols, block_col=0):
    return _spec((RC, cols), lambda i: (i, block_col))


def _whole(shape):
    return _spec(shape, lambda i: tuple(0 for _ in shape), single=True)


def _vec(cols):
    return _spec((1, cols), lambda i: (0, 0))


def _zero_at_first(*refs):
    @pl.when(pl.program_id(0) == 0)
    def _():
        for r in refs:
            r[...] = jnp.zeros_like(r)


IN_RUNS = ((PZ, 0, 1024), (PG, 2576, 2048), (PXBC, 1024, XBC))


def _in_tiles(fn):
    for pcol, wrow, width in IN_RUNS:
        def step(j, carry, pcol=pcol, wrow=wrow):
            fn(pl.multiple_of(pcol + j * 512, 512), pl.multiple_of(wrow + j * 512, 16))
            return carry

        lax.fori_loop(0, width // 512, step, 0)


def in_proj(h0, wn, w_t, w_dt):
    def body(h_ref, wn_ref, w_ref, wdt_ref, o_ref, dt_ref, u_ref):
        for r in (0, HALF):
            u_ref[r:r + HALF, :] = _rms(h_ref[r:r + HALF, :], wn_ref[...]).astype(BF)

        def tile(pcol, wrow):
            o_ref[:, pl.ds(pcol, 512)] = lax.dot_general(u_ref[...], w_ref[pl.ds(wrow, 512), :], NT_DIMS, preferred_element_type=F32).astype(BF)

        _in_tiles(tile)
        dt_ref[...] = lax.dot_general(u_ref[...], wdt_ref[...], NT_DIMS, preferred_element_type=F32)

    return pl.pallas_call(
        body, grid=(T // RC,), in_specs=[_rows_spec(D), _vec(D), _whole((IN_COLS, D)), _whole((256, D))],
        out_specs=[_rows_spec(NP_IN), _rows_spec(256), _rows_spec(D)],
        out_shape=[jax.ShapeDtypeStruct((T, NP_IN), BF), jax.ShapeDtypeStruct((T, 256), F32), jax.ShapeDtypeStruct((T, D), BF)],
        compiler_params=_params(), name="in_proj")(h0, wn, w_t, w_dt)


def out_proj(yn_ssd, proj, hseq, lru_nw, w_out, h0):
    def body(y_ref, g_ref, h_ref, wn_ref, w_ref, r_ref, o_ref, cat_ref):
        cat_ref[:, 0:SSD_W] = y_ref[...]
        for r in (0, HALF):
            y = _gelu(g_ref[r:r + HALF, :].astype(F32)) * h_ref[r:r + HALF, :]
            cat_ref[r:r + HALF, SSD_W:] = _rms(y, wn_ref[...]).astype(BF)

        def tile(c0):
            o_ref[:, pl.ds(c0, 512)] = r_ref[:, pl.ds(c0, 512)] + jnp.dot(cat_ref[...], w_ref[:, pl.ds(c0, 512)], preferred_element_type=F32)

        _col_tiles(D, 512, tile)

    return pl.pallas_call(
        body, grid=(T // RC,),
        in_specs=[_rows_spec(SSD_W), _rows_spec(LRU_W, PG // LRU_W), _rows_spec(LRU_W), _vec(LRU_W), _whole((SSD_W + LRU_W, D)), _rows_spec(D)],
        out_specs=[_rows_spec(D), _rows_spec(SSD_W + LRU_W)],
        out_shape=[jax.ShapeDtypeStruct((T, D), F32), jax.ShapeDtypeStruct((T, SSD_W + LRU_W), BF)],
        compiler_params=_params(), name="out_proj")(yn_ssd, proj, hseq, lru_nw, w_out, h0)


def out_proj_bwd(dh1_b, w_out, proj, hseq, lru_nw):
    def body(d_ref, w_ref, g_ref, h_ref, wn_ref, dy_ref, dh_ref, dg_ref, dw_ref, dl_scr):
        _zero_at_first(dw_ref)

        def tile(c0):
            dy_ref[:, pl.ds(c0, 512)] = lax.dot_general(d_ref[...], w_ref[pl.ds(c0, 512), :], NT_DIMS, preferred_element_type=F32)
            dl_scr[:, pl.ds(c0, 512)] = lax.dot_general(d_ref[...], w_ref[pl.ds(SSD_W + c0, 512), :], NT_DIMS, preferred_element_type=F32)

        _col_tiles(SSD_W, 512, tile)

        for r in (0, HALF):
            g = g_ref[r:r + HALF, :].astype(F32)
            h = h_ref[r:r + HALF, :]
            ge = _gelu(g)
            dy, dw = _rms_bwd(dl_scr[r:r + HALF, :], ge * h, wn_ref[...])
            dw_ref[...] += jnp.sum(dw, axis=0, keepdims=True)
            dh_ref[r:r + HALF, :] = dy * ge
            dg_ref[r:r + HALF, :] = (dy * h * _gelu_grad(g)).astype(BF)

    return pl.pallas_call(
        body, grid=(T // RC,),
        in_specs=[_rows_spec(D), _whole((SSD_W + LRU_W, D)), _rows_spec(LRU_W, PG // LRU_W), _rows_spec(LRU_W), _vec(LRU_W)],
        out_specs=[_rows_spec(SSD_W), _rows_spec(LRU_W), _rows_spec(LRU_W), _vec(LRU_W)],
        out_shape=[jax.ShapeDtypeStruct((T, SSD_W), F32), jax.ShapeDtypeStruct((T, LRU_W), F32), jax.ShapeDtypeStruct((T, LRU_W), BF),
                   jax.ShapeDtypeStruct((1, LRU_W), F32)],
        scratch_shapes=[pltpu.VMEM((RC, LRU_W), F32)],
        compiler_params=_params(), name="out_proj_bwd")(dh1_b, w_out, proj, hseq, lru_nw)


def in_proj_bwd(dz, dg, dxl, dxbc, ddt, w_t, w_dt, h0, wn, dh1):
    first = NPAD + N_META

    def body(dz_ref, dg_ref, dxl_ref, dxbc_ref, ddt_ref, w_ref, wdt_ref, h_ref, wn_ref, r_ref, gx_hbm, meta_ref, dw_ref, du_scr, o_ref, sem):
        i = pl.program_id(0)
        _zero_at_first(dw_ref)
        du_scr[...] = jnp.dot(ddt_ref[...], wdt_ref[...], preferred_element_type=F32)
        for d_ref, wrow, width in ((dz_ref, 0, 1024), (dxbc_ref, 1024, XBC), (dg_ref, 2576, 1024), (dxl_ref, 3600, 1024)):
            def step(j, carry, d_ref=d_ref, wrow=wrow):
                c0 = pl.multiple_of(j * 512, 512)
                du_scr[...] += jnp.dot(d_ref[:, pl.ds(c0, 512)], w_ref[pl.ds(pl.multiple_of(wrow + c0, 16), 512), :], preferred_element_type=F32)
                return carry

            lax.fori_loop(0, width // 512, step, 0)
        for r in (0, HALF):
            dh, dw = _rms_bwd(du_scr[r:r + HALF, :], h_ref[r:r + HALF, :], wn_ref[...])
            dw_ref[...] += jnp.sum(dw, axis=0, keepdims=True)
            o_ref[r:r + HALF, :] = dh + r_ref[r:r + HALF, :]

        @pl.when(i == 0)
        def _():
            meta_ref[...] = o_ref[NPAD:first, :]
            head = pltpu.make_async_copy(o_ref.at[pl.ds(first, RC - first)], gx_hbm.at[pl.ds(0, RC - first)], sem)
            head.start()
            head.wait()

        @pl.when(i > 0)
        def _():
            rest = pltpu.make_async_copy(o_ref, gx_hbm.at[pl.ds(pl.multiple_of(i * RC - first, 32), RC)], sem)
            rest.start()
            rest.wait()

    return pl.pallas_call(
        body, grid=(T // RC,),
        in_specs=[_rows_spec(SSD_W), _rows_spec(LRU_W), _rows_spec(LRU_W), _rows_spec(XBC), _rows_spec(256), _whole((IN_COLS, D)),
                  _whole((256, D)), _rows_spec(D), _vec(D), _rows_spec(D)],
        out_specs=[pl.BlockSpec(memory_space=pl.ANY), _spec((N_META, D), lambda i: (0, 0)), _vec(D)],
        out_shape=[jax.ShapeDtypeStruct((SEQ, D), F32), jax.ShapeDtypeStruct((N_META, D), F32), jax.ShapeDtypeStruct((1, D), F32)],
        scratch_shapes=[pltpu.VMEM((RC, D), F32), pltpu.VMEM((RC, D), F32), pltpu.SemaphoreType.DMA],
        compiler_params=_params(), name="in_proj_bwd")(dz, dg, dxl, dxbc, ddt, w_t, w_dt, h0, wn, dh1)


def weight_grad(name, parts, u1):
    tm = 256
    tiles = [p.shape[1] // tm for p in parts]
    starts = [sum(tiles[:k]) for k in range(len(parts))]

    def body(*refs):
        a_refs, u_ref, o_ref = refs[:len(parts)], refs[len(parts)], refs[len(parts) + 1]
        step = pl.program_id(0)
        for a_ref, start, n in zip(a_refs, starts, tiles):
            @pl.when((step >= start) & (step < start + n))
            def _(a_ref=a_ref):
                o_ref[...] = lax.dot_general(a_ref[...], u_ref[...], TN_DIMS, preferred_element_type=F32).astype(BF)

    def tile_of(start, n):
        return lambda j: (0, jnp.clip(j - start, 0, n - 1))

    return pl.pallas_call(
        body, grid=(sum(tiles),),
        in_specs=[_spec((T, tm), tile_of(s, n)) for s, n in zip(starts, tiles)] + [_spec((T, D), lambda j: (0, 0), single=True)],
        out_specs=_spec((tm, D), lambda j: (j, 0)),
        out_shape=jax.ShapeDtypeStruct((tm * sum(tiles), D), BF),
        compiler_params=_params(), name=name)(*parts, u1)


def _ssd_chunk_common(row0, dt_ref, b_ref, c_ref, bias, a_neg):
    shape = (Q, Q)
    lane = _lanes(shape)
    sub = _rows(shape)
    live = (_rows(shape, row0) >= NPAD) & (lane < 8)
    dtr = dt_ref[:, :]
    dt = jnp.where(live, _softplus(dtr + bias), 0.0)
    d_a = dt * a_neg
    tri = (sub >= lane).astype(F32)
    cs = jnp.dot(tri, d_a, precision=lax.Precision.HIGHEST, preferred_element_type=F32)
    cs_t = cs.T
    b_f = b_ref[:, :]
    bc = b_f.astype(BF)
    cc = c_ref[:, :].astype(BF)
    cb = lax.dot_general(cc, bc, NT_DIMS, preferred_element_type=F32)
    cs_last = cs[Q - 1:Q, :]
    return dict(lane=lane, sub=sub, live=live, dtr=dtr, dt=dt, cs=cs, cs_t=cs_t, bc=bc, cc=cc, cb=cb, bc_t=b_f.T.astype(BF),
                ecs=jnp.exp(cs), dsm=jnp.exp(cs_last - cs), gam=jnp.exp(cs_last))


def _pair(lane_even, mat, j):
    return jnp.where(lane_even, mat[:, j:j + 1], mat[:, j + 1:j + 2])


def _pair_row(lane_even, mat, j):
    return jnp.where(lane_even[0:1, :], mat[:, j:j + 1], mat[:, j + 1:j + 2])


def _head_decay(cm, j):
    seg = cm["cs"][:, j:j + 1] - cm["cs_t"][j:j + 1, :]
    return jnp.exp(jnp.where(cm["sub"] >= cm["lane"], seg, -jnp.inf))


def _head_decay_t(cm, j):
    seg = cm["cs_t"][j:j + 1, :] - cm["cs"][:, j:j + 1]
    return jnp.exp(jnp.where(cm["lane"] >= cm["sub"], seg, -jnp.inf))


def _conv_window(raw_ref, halo_ref, pad_scr):
    pad_scr[0:HALO, :] = halo_ref[...].astype(F32)[halo_ref.shape[0] - HALO:, :]
    pad_scr[HALO:HALO + Q, :] = raw_ref[...].astype(F32)
    win = pad_scr[...]
    return lambda s: win[HALO:, :] if s == 0 else pltpu.roll(win, s, axis=0)[HALO:, :]


def _xbc_cols(g):
    return slice(512 * g, 512 * g + 512), slice(SSD_W + 128 * g, SSD_W + 128 * g + 128), slice(SSD_W + 256 + 128 * g, SSD_W + 384 + 128 * g)


def ssd_fwd(proj, dt_raw, conv_w, conv_b, dt_bias2, a_log2, d2, norm_w):
    def body(raw_ref, halo_ref, dt_all, z_all, cw_ref, cb_ref, bias_all, alog_all, d_all, nw_all, yn_all, y_all, hp_all,
             h_all, pad_scr, act_scr):
        @pl.when(pl.program_id(0) == 0)
        def _():
            h_all[...] = jnp.zeros_like(h_all)

        pre = _conv(_conv_window(raw_ref, halo_ref, pad_scr), cw_ref[...], cb_ref[...])
        act_scr[...] = pre * _sigmoid(pre)
        for g in range(2):
            wide, thin = slice(512 * g, 512 * g + 512), slice(128 * g, 128 * g + 128)
            xs, bs, cs = _xbc_cols(g)
            group(act_scr.at[:, xs], act_scr.at[:, bs], act_scr.at[:, cs], dt_all.at[:, thin], z_all.at[:, wide], bias_all.at[g],
                  alog_all.at[g], d_all.at[g], nw_all.at[:, wide], yn_all.at[:, wide], y_all.at[:, wide], hp_all.at[g, 0], h_all.at[g])

    def group(x_ref, b_ref, c_ref, dt_ref, z_ref, bias_ref, alog_ref, d_ref, nw_ref, yn_ref, y_ref, hp_ref, h_scr):
        bias = bias_ref[...]
        a_neg = -jnp.exp(alog_ref[...])
        dsk = d_ref[...]
        cm = _ssd_chunk_common(pl.program_id(0) * Q, dt_ref, b_ref, c_ref, bias, a_neg)
        lane_even = cm["lane"] < 64
        for p in range(4):
            je, jo = 2 * p, 2 * p + 1
            xp = x_ref[:, 128 * p:128 * p + 128]
            xdt = xp * _pair(lane_even, cm["dt"], je)
            xdt_b = xdt.astype(BF)
            m_e = (cm["cb"] * _head_decay(cm, je)).astype(BF)
            m_o = (cm["cb"] * _head_decay(cm, jo)).astype(BF)
            zero = jnp.zeros_like(xdt_b)
            yd = (jnp.dot(m_e, jnp.where(lane_even, xdt_b, zero), preferred_element_type=F32)
                  + jnp.dot(m_o, jnp.where(lane_even, zero, xdt_b), preferred_element_type=F32))
            hp = h_scr[p]
            hp_ref[p] = hp
            yo = jnp.dot(cm["cc"], hp.astype(BF), preferred_element_type=F32) * _pair(lane_even, cm["ecs"], je)
            y_ref[:, 128 * p:128 * p + 128] = yd + yo + xp * _pair_row(lane_even, dsk, je)
            st = jnp.dot(cm["bc_t"], (xdt * _pair(lane_even, cm["dsm"], je)).astype(BF), preferred_element_type=F32)
            h_scr[p] = hp * _pair_row(lane_even, cm["gam"], je) + st
        zc = z_ref[:, :].astype(F32)
        gated = y_ref[:, :] * (zc * _sigmoid(zc))
        yn_ref[:, :] = _rms(gated, nw_ref[...]).astype(BF)

    par = _spec((2, 1, 128), lambda c: (0, 0, 0))
    wide = _spec((Q, SSD_W), lambda c: (c, 0))
    xbc = PXBC // XBC
    halo = 2 * HALO
    return pl.pallas_call(
        body, grid=(NCH,),
        in_specs=[_spec((Q, XBC), lambda c: (c, xbc)), _spec((halo, XBC), lambda c: (jnp.maximum(c * (Q // halo) - 1, 0), xbc)),
                  _spec((Q, 256), lambda c: (c, 0)), wide, _spec((4, XBC), lambda c: (0, 0)), _spec((1, XBC), lambda c: (0, 0)),
                  par, par, par, _spec((1, SSD_W), lambda c: (0, 0))],
        out_specs=[wide, wide, _spec((2, 1, 4, 128, 128), lambda c: (0, c, 0, 0, 0))],
        out_shape=[jax.ShapeDtypeStruct((T, SSD_W), BF), jax.ShapeDtypeStruct((T, SSD_W), F32),
                   jax.ShapeDtypeStruct((2, NCH, 4, 128, 128), F32)],
        scratch_shapes=[pltpu.VMEM((2, 4, 128, 128), F32), pltpu.VMEM((Q + HALO, XBC), F32), pltpu.VMEM((Q, XBC), F32)],
        compiler_params=_params(), name="ssd_fwd")(proj, proj, dt_raw, proj, conv_w, conv_b, dt_bias2, a_log2, d2, norm_w)


def ssd_bwd(dyn, proj, dt_raw, conv_w, conv_b, y_pre, h_prev, dt_bias2, a_log2, d2, norm_w):
    def body(dyn_all, raw_ref, halo_ref, dt_all, z_all, y_all, hp_all, cw_ref, cb_ref, bias_all, alog_all, d_all, nw_all,
             dz_all, dxbc_ref, ddt_all, dpar_all, dnw_all, dcw_ref, dcb_ref, dh_all, acc_all, pad_scr, act_scr, dsilu_scr, dact_scr, dpad_scr):
        @pl.when(pl.program_id(0) == 0)
        def _():
            dh_all[...] = jnp.zeros_like(dh_all)
            acc_all[...] = jnp.zeros_like(acc_all)
            dnw_all[...] = jnp.zeros_like(dnw_all)
            dcw_ref[...] = jnp.zeros_like(dcw_ref)
            dcb_ref[...] = jnp.zeros_like(dcb_ref)
            dpad_scr[Q:Q + HALO, :] = jnp.zeros((HALO, XBC), F32)

        back = _conv_window(raw_ref, halo_ref, pad_scr)
        pre = _conv(back, cw_ref[...], cb_ref[...])
        sg = _sigmoid(pre)
        act_scr[...] = pre * sg
        dsilu_scr[...] = sg * (1.0 + pre * (1.0 - sg))
        for g in range(2):
            wide, thin = slice(512 * g, 512 * g + 512), slice(128 * g, 128 * g + 128)
            xs, bs, cs = _xbc_cols(g)
            group(dyn_all.at[:, wide], act_scr.at[:, xs], act_scr.at[:, bs], act_scr.at[:, cs], dt_all.at[:, thin], z_all.at[:, wide],
                  y_all.at[:, wide], hp_all.at[g, 0], bias_all.at[g], alog_all.at[g], d_all.at[g], nw_all.at[:, wide],
                  dz_all.at[:, wide], dact_scr.at[:, xs], dact_scr.at[:, bs], dact_scr.at[:, cs], ddt_all.at[:, thin], dpar_all.at[g],
                  dnw_all.at[:, wide], dh_all.at[g], acc_all.at[g])
        dpre = dact_scr[...] * dsilu_scr[...]
        dcw, dcb = _conv_bwd_w(dpre, back)
        dcw_ref[...] += dcw
        dcb_ref[...] += dcb
        dpad_scr[0:Q, :] = dpre
        win = dpad_scr[...]
        dxbc_ref[...] = _conv_bwd_x(lambda s: win[:Q, :] if s == 0 else pltpu.roll(win, Q + HALO - s, axis=0)[:Q, :], cw_ref[...]).astype(BF)
        dpad_scr[Q:Q + HALO, :] = dpre[0:HALO, :]

    def group(dyn_ref, x_ref, b_ref, c_ref, dt_ref, z_ref, y_ref, hp_ref, bias_ref, alog_ref, d_ref, nw_ref,
              dz_ref, dx_ref, db_ref, dc_ref, ddt_ref, dpar_ref, dnw_ref, dh_scr, acc_scr):
        ci = pl.program_id(0)
        bias = bias_ref[...]
        a_neg = -jnp.exp(alog_ref[...])
        dsk = d_ref[...]
        cm = _ssd_chunk_common((NCH - 1 - ci) * Q, dt_ref, b_ref, c_ref, bias, a_neg)
        lane, sub = cm["lane"], cm["sub"]
        lane_even = lane < 64
        cc_t = c_ref[:, :].T.astype(BF)
        cb_t = lax.dot_general(cm["bc"], cm["cc"], NT_DIMS, preferred_element_type=F32)
        zc = z_ref[:, :].astype(F32)
        yc = y_ref[:, :]
        sg = _sigmoid(zc)
        sz = zc * sg
        dgated, dnw = _rms_bwd(dyn_ref[:, :], yc * sz, nw_ref[...])
        dnw_ref[...] += jnp.sum(dnw, axis=0, keepdims=True)
        dz_ref[:, :] = (dgated * yc * (sg * (1.0 + zc * (1.0 - sg)))).astype(BF)
        dy_all = dgated * sz
        dcb = jnp.zeros((Q, Q), F32)
        dcb_t = jnp.zeros((Q, Q), F32)
        db_acc = jnp.zeros((Q, Q), F32)
        dc_acc = jnp.zeros((Q, Q), F32)
        dcs = jnp.zeros((Q, Q), F32)
        ddt = jnp.zeros((Q, Q), F32)
        for p in range(4):
            je, jo = 2 * p, 2 * p + 1
            xp = x_ref[:, 128 * p:128 * p + 128]
            dy = dy_all[:, 128 * p:128 * p + 128]
            dt_p = _pair(lane_even, cm["dt"], je)
            xdt = xp * dt_p
            xdt_b = xdt.astype(BF)
            dy_b = dy.astype(BF)
            zero = jnp.zeros_like(dy_b)
            hp = hp_ref[p]
            hp_b = hp.astype(BF)
            dh = dh_scr[p]
            dh_b = dh.astype(BF)
            acc_scr[p:p + 1, :] += jnp.sum(dy * xp, axis=0, keepdims=True)
            dxp = dy * _pair_row(lane_even, dsk, je)
            e_p = _pair(lane_even, cm["ecs"], je)
            g_p = jnp.dot(cm["cc"], hp_b, preferred_element_type=F32)
            dg_b = (dy * e_p).astype(BF)
            de = dy * g_p * e_p
            dc_acc = dc_acc + lax.dot_general(dg_b, hp_b, NT_DIMS, preferred_element_type=F32)
            dh_in = jnp.dot(cc_t, dg_b, preferred_element_type=F32)
            ds_p = _pair(lane_even, cm["dsm"], je)
            r_p = jnp.dot(cm["bc"], dh_b, preferred_element_type=F32)
            dxdt = r_p * ds_p
            tt = r_p * xdt * ds_p
            db_acc = db_acc + lax.dot_general((xdt * ds_p).astype(BF), dh_b, NT_DIMS, preferred_element_type=F32)
            dgam_m = jnp.sum(dh * hp, axis=0, keepdims=True)
            for j, even in ((je, True), (jo, False)):
                sel = lane_even if even else jnp.logical_not(lane_even)
                dy_j = jnp.where(sel, dy_b, zero)
                l_j = _head_decay(cm, j)
                l_jt = _head_decay_t(cm, j)
                m_j = cm["cb"] * l_j
                m_jt = cb_t * l_jt
                dm = lax.dot_general(dy_j, xdt_b, NT_DIMS, preferred_element_type=F32)
                dm_t = lax.dot_general(xdt_b, dy_j, NT_DIMS, preferred_element_type=F32)
                dxdt = dxdt + jnp.dot(m_jt.astype(BF), dy_j, preferred_element_type=F32)
                dcb = dcb + dm * l_j
                dcb_t = dcb_t + dm_t * l_jt
                t_j = jnp.where(sel, tt, 0.0)
                col = jnp.sum(dm * m_j - dm_t * m_jt + (jnp.where(sel, de, 0.0) - t_j), axis=1, keepdims=True)
                gam_j = cm["gam"][:, j:j + 1]
                last = (jnp.sum(jnp.sum(t_j, axis=0, keepdims=True), axis=1, keepdims=True)
                        + jnp.sum(jnp.where(sel[0:1, :], dgam_m, 0.0), axis=1, keepdims=True) * gam_j)
                col = col + jnp.where(sub[:, 0:1] == Q - 1, last, 0.0)
                dcs = dcs + jnp.where(lane == j, col, 0.0)
            dh_scr[p] = dh_in + dh * _pair_row(lane_even, cm["gam"], je)
            dx_ref[:, 128 * p:128 * p + 128] = dxp + dxdt * dt_p
            dd = dxdt * xp
            ddt = ddt + jnp.where(lane == je, jnp.sum(jnp.where(lane_even, dd, 0.0), axis=1, keepdims=True), 0.0)
            ddt = ddt + jnp.where(lane == jo, jnp.sum(jnp.where(lane_even, 0.0, dd), axis=1, keepdims=True), 0.0)
        dc_ref[:, :] = dc_acc + jnp.dot(dcb.astype(BF), cm["bc"], preferred_element_type=F32)
        db_ref[:, :] = db_acc + jnp.dot(dcb_t.astype(BF), cm["cc"], preferred_element_type=F32)
        tri_t = (sub <= lane).astype(F32)
        dd_a = jnp.dot(tri_t, dcs, precision=lax.Precision.HIGHEST, preferred_element_type=F32)
        ddt = ddt + dd_a * a_neg
        acc_scr[5:6, :] += jnp.sum(dd_a * cm["dt"], axis=0, keepdims=True)
        draw = jnp.where(cm["live"], ddt * _sigmoid_gate(cm["dtr"] + bias), 0.0)
        acc_scr[4:5, :] += jnp.sum(draw, axis=0, keepdims=True)
        ddt_ref[:, :] = draw.astype(BF)

        @pl.when(ci == NCH - 1)
        def _():
            lane1 = _lanes((1, 128))
            dd = jnp.zeros((1, 128), F32)
            for p in range(4):
                row = acc_scr[p:p + 1, :]
                dd = dd + jnp.where(lane1 == 2 * p, jnp.sum(jnp.where(lane1 < 64, row, 0.0), axis=1, keepdims=True), 0.0)
                dd = dd + jnp.where(lane1 == 2 * p + 1, jnp.sum(jnp.where(lane1 < 64, 0.0, row), axis=1, keepdims=True), 0.0)
            dpar_ref[...] = jnp.concatenate([acc_scr[4:5, :], acc_scr[5:6, :] * a_neg, dd, jnp.zeros((5, 128), F32)], axis=0)

    par = _spec((2, 1, 128), lambda c: (0, 0, 0))
    wide = _spec((Q, SSD_W), lambda c: (NCH - 1 - c, 0))
    thin = _spec((Q, 256), lambda c: (NCH - 1 - c, 0))
    vec = _spec((1, SSD_W), lambda c: (0, 0))
    xbc = PXBC // XBC
    halo = 2 * HALO
    chunk = pltpu.VMEM((Q, XBC), F32)
    padded = pltpu.VMEM((Q + HALO, XBC), F32)
    return pl.pallas_call(
        body, grid=(NCH,),
        in_specs=[wide, _spec((Q, XBC), lambda c: (NCH - 1 - c, xbc)),
                  _spec((halo, XBC), lambda c: (jnp.maximum((NCH - 1 - c) * (Q // halo) - 1, 0), xbc)), thin, wide, wide,
                  _spec((2, 1, 4, 128, 128), lambda c: (0, NCH - 1 - c, 0, 0, 0)), _spec((4, XBC), lambda c: (0, 0)),
                  _spec((1, XBC), lambda c: (0, 0)), par, par, par, vec],
        out_specs=[wide, _spec((Q, XBC), lambda c: (NCH - 1 - c, 0)), thin, _spec((2, 8, 128), lambda c: (0, 0, 0)), vec,
                   _spec((4, XBC), lambda c: (0, 0)), _spec((1, XBC), lambda c: (0, 0))],
        out_shape=[jax.ShapeDtypeStruct((T, SSD_W), BF), jax.ShapeDtypeStruct((T, XBC), BF), jax.ShapeDtypeStruct((T, 256), BF),
                   jax.ShapeDtypeStruct((2, 8, 128), F32), jax.ShapeDtypeStruct((1, SSD_W), F32), jax.ShapeDtypeStruct((4, XBC), F32),
                   jax.ShapeDtypeStruct((1, XBC), F32)],
        scratch_shapes=[pltpu.VMEM((2, 4, 128, 128), F32), pltpu.VMEM((2, 8, 128), F32), padded, chunk, chunk, chunk, padded],
        compiler_params=_params(), name="ssd_bwd")(dyn, proj, proj, dt_raw, proj, y_pre, h_prev, conv_w, conv_b, dt_bias2, a_log2, d2, norm_w)


def _lru_gates(back, cw, cb, wa, ba, wx, bx, lam):
    xr = _conv(back, cw, cb)
    xr_b = xr.astype(BF)
    r = _sigmoid_gate(jnp.dot(xr_b, wa, preferred_element_type=F32) + ba)
    i = _sigmoid_gate(jnp.dot(xr_b, wx, preferred_element_type=F32) + bx)
    sp = _softplus(-lam)
    la = (-LRU_C) * r * sp
    a = jnp.exp(la)
    mult2 = -jnp.tanh(la) * (a * a + 1.0)
    return xr, xr_b, r, i, sp, a, jnp.sqrt(mult2), mult2


SEG_LEN = 68
SEGS = T // SEG_LEN


def _seg_rows(j, k, off=0):
    return pl.ds(off + j * 8 * SEG_LEN + k, 8, stride=SEG_LEN)


def _segmented_scan(mul_ref, mul_row0, add_ref, out_ref, loc_scr, prod_scr, carry_scr, reverse):
    groups = SEGS // 8
    off = mul_row0 + (1 if reverse else 0)

    def local(i, carry):
        k = SEG_LEN - 1 - i if reverse else i
        new = []
        for j in range(groups):
            h, p = carry[2 * j], carry[2 * j + 1]
            m = mul_ref[_seg_rows(j, k, off), :]
            h = m * h + add_ref[_seg_rows(j, k), :]
            p = m * p
            loc_scr[_seg_rows(j, k), :] = h
            prod_scr[_seg_rows(j, k), :] = p
            new += [h, p]
        return tuple(new)

    lax.fori_loop(0, SEG_LEN, local, (jnp.zeros((8, 128), F32), jnp.ones((8, 128), F32)) * groups)

    def chain(i, c):
        s = SEGS - 1 - i if reverse else i
        carry_scr[pl.ds(s, 1), :] = c
        edge = s * SEG_LEN + (0 if reverse else SEG_LEN - 1)
        return loc_scr[pl.ds(edge, 1), :] + prod_scr[pl.ds(edge, 1), :] * c

    lax.fori_loop(0, SEGS, chain, jnp.zeros((1, 128), F32))

    def fold(k, carry):
        for j in range(groups):
            rows = _seg_rows(j, k)
            out_ref[rows, :] = loc_scr[rows, :] + prod_scr[rows, :] * carry_scr[8 * j:8 * j + 8, :]
        return carry

    lax.fori_loop(0, SEG_LEN, fold, 0)


def lru_fwd(proj, cw, cb, wa2, ba, wx2, bx, lam):
    def body(x_ref, cw_ref, cb_ref, wa_ref, ba_ref, wx_ref, bx_ref, lam_ref, h_ref, a_ref, xpad, u_scr, loc_scr, prod_scr, carry_scr):
        _fill_padded(xpad, x_ref)

        def chunk(r0):
            xr, _, _, i, _, a, mult, _ = _lru_gates(_back(xpad, r0), cw_ref[...], cb_ref[...], wa_ref[0], ba_ref[...], wx_ref[0], bx_ref[...],
                                                 lam_ref[...])
            a_ref[pl.ds(r0, Q), :] = a
            u_scr[pl.ds(r0, Q), :] = jnp.where(_rows(a.shape, r0) >= NPAD, mult * (i * xr), 0.0)

        _chunks(chunk, unrolled=True)
        _segmented_scan(a_ref, 0, u_scr, h_ref, loc_scr, prod_scr, carry_scr, reverse=False)

    c0 = PXL // 128
    vec = _spec((1, 128), lambda c: (0, c))
    mat = _spec((1, 128, 128), lambda c: (c, 0, 0))
    seq = pltpu.VMEM((T, 128), F32)
    return pl.pallas_call(
        body, grid=(8,),
        in_specs=[_spec((T, 128), lambda c: (0, c0 + c)), _spec((4, 128), lambda c: (0, c)), vec, mat, vec, mat, vec, vec],
        out_specs=[_spec((T, 128), lambda c: (0, c)), _spec((T, 128), lambda c: (0, c))],
        out_shape=[jax.ShapeDtypeStruct((T, LRU_W), F32), jax.ShapeDtypeStruct((T, LRU_W), F32)],
        scratch_shapes=[pltpu.VMEM((T + 2 * HALO, 128), F32), seq, seq, seq, pltpu.VMEM((SEGS, 128), F32)],
        compiler_params=_params(), name="lru_fwd")(proj, cw, cb, wa2, ba, wx2, bx, lam)


def lru_bwd(dh_out, a, hseq, proj, cw, cb, wa2, ba, wx2, bx, lam):
    def body(d_ref, a_ref, h_ref, x_ref, cw_ref, cb_ref, wa_ref, ba_ref, wx_ref, bx_ref, lam_ref,
             dx_ref, dcw_ref, dcb_ref, dwa_ref, dba_ref, dwx_ref, dbx_ref, dlam_ref, xpad, hpad, dpad, dh_ref, loc_scr, prod_scr, carry_scr):
        _fill_padded(dpad, a_ref)
        _segmented_scan(dpad, HALO, d_ref, dh_ref, loc_scr, prod_scr, carry_scr, reverse=True)
        _fill_padded(xpad, x_ref)
        _fill_padded(hpad, h_ref)
        dpad[0:HALO, :] = jnp.zeros((HALO, 128), F32)
        dpad[T + HALO:T + 2 * HALO, :] = jnp.zeros((HALO, 128), F32)
        for ref in (dcw_ref, dcb_ref, dwa_ref, dba_ref, dwx_ref, dbx_ref, dlam_ref):
            ref[...] = jnp.zeros_like(ref)
        lam = lam_ref[...]

        def first(r0):
            back = _back(xpad, r0)
            xr, xr_b, r, i, sp, a, mult, mult2 = _lru_gates(back, cw_ref[...], cb_ref[...], wa_ref[0], ba_ref[...], wx_ref[0], bx_ref[...], lam)
            dh = dh_ref[pl.ds(r0, Q), :]
            da = dh * _back(hpad, r0)(1)
            du = jnp.where(_rows(dh.shape, r0) >= NPAD, dh, 0.0)
            dmult = du * (i * xr)
            di = du * (mult * xr)
            dxr = du * (mult * i)
            dla = da * a - dmult * (a * a) * lax.rsqrt(mult2)
            dr = dla * ((-LRU_C) * sp)
            dlam_ref[...] += jnp.sum(dla * ((-LRU_C) * r), axis=0, keepdims=True)
            dpr = dr * r * (1.0 - r)
            dpi = di * i * (1.0 - i)
            dba_ref[...] += jnp.sum(dpr, axis=0, keepdims=True)
            dbx_ref[...] += jnp.sum(dpi, axis=0, keepdims=True)
            dpr_b = dpr.astype(BF)
            dpi_b = dpi.astype(BF)
            dxr = (dxr + lax.dot_general(dpr_b, wa_ref[0], NT_DIMS, preferred_element_type=F32)
                   + lax.dot_general(dpi_b, wx_ref[0], NT_DIMS, preferred_element_type=F32))
            dwa_ref[0] += lax.dot_general(xr_b, dpr_b, TN_DIMS, preferred_element_type=F32)
            dwx_ref[0] += lax.dot_general(xr_b, dpi_b, TN_DIMS, preferred_element_type=F32)
            dpad[pl.ds(r0 + HALO, Q), :] = dxr
            dcw, dcb = _conv_bwd_w(dxr, back)
            dcw_ref[...] += dcw
            dcb_ref[...] += dcb

        _chunks(first, unrolled=True)
        dlam_ref[...] = -dlam_ref[...] * _sigmoid_gate(-lam)

        def second(r0):
            dx_ref[pl.ds(r0, Q), :] = _conv_bwd_x(_ahead(dpad, r0), cw_ref[...]).astype(BF)

        _chunks(second)

    c0 = PXL // 128
    vec = _spec((1, 128), lambda c: (0, c))
    mat = _spec((1, 128, 128), lambda c: (c, 0, 0))
    col = _spec((T, 128), lambda c: (0, c))
    vshape = jax.ShapeDtypeStruct((1, LRU_W), F32)
    mshape = jax.ShapeDtypeStruct((8, 128, 128), F32)
    pad = pltpu.VMEM((T + 2 * HALO, 128), F32)
    seq = pltpu.VMEM((T, 128), F32)
    return pl.pallas_call(
        body, grid=(8,),
        in_specs=[col, col, col, _spec((T, 128), lambda c: (0, c0 + c)), _spec((4, 128), lambda c: (0, c)), vec, mat, vec, mat, vec, vec],
        out_specs=[col, _spec((4, 128), lambda c: (0, c)), vec, mat, vec, mat, vec, vec],
        out_shape=[jax.ShapeDtypeStruct((T, LRU_W), BF), jax.ShapeDtypeStruct((4, LRU_W), F32), vshape, mshape, vshape, mshape, vshape, vshape],
        scratch_shapes=[pad, pad, pad, seq, seq, seq, pltpu.VMEM((SEGS, 128), F32)],
        compiler_params=_params(), name="lru_bwd")(dh_out, a, hseq, proj, cw, cb, wa2, ba, wx2, bx, lam)


def gate_up(h1, wn, w_gate, w_up):
    def body(h_ref, wn_ref, wg_ref, wu_ref, gt_ref, up_ref, act_ref, u_ref):
        for r in (0, HALF):
            u_ref[r:r + HALF, :] = _rms(h_ref[r:r + HALF, :], wn_ref[...]).astype(BF)

        def tile(c0):
            cols = pl.ds(c0, 256)
            gt = lax.dot_general(u_ref[...], wg_ref[cols, :], NT_DIMS, preferred_element_type=F32)
            up = lax.dot_general(u_ref[...], wu_ref[cols, :], NT_DIMS, preferred_element_type=F32)
            gt_ref[:, cols] = gt.astype(BF)
            up_ref[:, cols] = up.astype(BF)
            act_ref[:, cols] = (gt * _sigmoid(gt) * up).astype(BF)

        _col_tiles(D_FF, 256, tile)

    big = jax.ShapeDtypeStruct((T, D_FF), BF)
    return pl.pallas_call(
        body, grid=(T // RC,), in_specs=[_rows_spec(D), _vec(D), _whole((D_FF, D)), _whole((D_FF, D))],
        out_specs=[_rows_spec(D_FF), _rows_spec(D_FF), _rows_spec(D_FF), _rows_spec(D)],
        out_shape=[big, big, big, jax.ShapeDtypeStruct((T, D), BF)],
        compiler_params=_params(), name="gate_up")(h1, wn, w_gate, w_up)


def down_loss(act, w_down, h1, target, wf):
    first = NPAD + N_META

    def body(a_ref, w_ref, r_ref, t_hbm, wf_ref, d_ref, db_ref, l_ref, dw_ref, h_scr, t_ref, t_sem):
        i = pl.program_id(0)
        _zero_at_first(l_ref, dw_ref)
        head = pltpu.make_async_copy(t_hbm.at[pl.ds(0, RC - first)], t_ref.at[pl.ds(first, RC - first)], t_sem)
        rest = pltpu.make_async_copy(t_hbm.at[pl.ds(pl.multiple_of(jnp.maximum(i * RC - first, 0), 32), RC)], t_ref, t_sem)

        @pl.when(i == 0)
        def _():
            t_ref[0:first, :] = jnp.zeros((first, D), F32)
            head.start()

        @pl.when(i > 0)
        def _():
            rest.start()

        def tile(c0):
            cols = pl.ds(c0, 512)
            h_scr[:, cols] = r_ref[:, cols] + jnp.dot(a_ref[...], w_ref[:, cols], preferred_element_type=F32)

        _col_tiles(D, 512, tile)

        @pl.when(i == 0)
        def _():
            head.wait()

        @pl.when(i > 0)
        def _():
            rest.wait()

        for r in (0, HALF):
            h = h_scr[r:r + HALF, :]
            live = _rows((HALF, D), i * RC + r) >= first
            err = jnp.where(live, _rms(h, wf_ref[...]) - t_ref[r:r + HALF, :], 0.0)
            l_ref[...] += 0.5 * jnp.sum(jnp.sum(err * err, axis=1, keepdims=True) * (1.0 / D), axis=0, keepdims=True)
            dh, dw = _rms_bwd(err * (1.0 / D), h, wf_ref[...])
            dw_ref[...] += jnp.sum(dw, axis=0, keepdims=True)
            d_ref[r:r + HALF, :] = dh
            db_ref[r:r + HALF, :] = dh.astype(BF)

    return pl.pallas_call(
        body, grid=(T // RC,),
        in_specs=[_rows_spec(D_FF), _whole((D_FF, D)), _rows_spec(D), pl.BlockSpec(memory_space=pl.ANY), _vec(D)],
        out_specs=[_rows_spec(D), _rows_spec(D), _spec((1, 128), lambda i: (0, 0)), _vec(D)],
        out_shape=[jax.ShapeDtypeStruct((T, D), F32), jax.ShapeDtypeStruct((T, D), BF), jax.ShapeDtypeStruct((1, 128), F32),
                   jax.ShapeDtypeStruct((1, D), F32)],
        scratch_shapes=[pltpu.VMEM((RC, D), F32), pltpu.VMEM((RC, D), F32), pltpu.SemaphoreType.DMA],
        compiler_params=_params(), name="down_loss")(act, w_down, h1, target, wf)


def swiglu_bwd(dh2_b, w_down, gt, up, act, u2):
    tn = 256

    def body(d_ref, u_ref, w_ref, gt_ref, up_ref, act_ref, dg_ref, du_ref, gd_ref, gg_ref, gu_ref, acc_d, acc_g, acc_u):
        for acc in (acc_d, acc_g, acc_u):
            acc[...] = jnp.zeros_like(acc)

        def rows(r0):
            part = pl.ds(r0, RC)
            d = d_ref[part, :]
            dact = lax.dot_general(d, w_ref[...], NT_DIMS, preferred_element_type=F32)
            gt_ = gt_ref[part, :].astype(F32)
            up_ = up_ref[part, :].astype(F32)
            sg = _sigmoid(gt_)
            dgt = (dact * up_ * (sg * (1.0 + gt_ * (1.0 - sg)))).astype(BF)
            dup = (dact * (gt_ * sg)).astype(BF)
            dg_ref[part, :] = dgt
            du_ref[part, :] = dup
            u = u_ref[part, :]
            acc_d[...] += lax.dot_general(act_ref[part, :], d, TN_DIMS, preferred_element_type=F32)
            acc_g[...] += lax.dot_general(dgt, u, TN_DIMS, preferred_element_type=F32)
            acc_u[...] += lax.dot_general(dup, u, TN_DIMS, preferred_element_type=F32)

        _col_tiles(T, RC, rows)
        gd_ref[...] = acc_d[...].astype(BF)
        gg_ref[...] = acc_g[...].astype(BF)
        gu_ref[...] = acc_u[...].astype(BF)

    resident = _spec((T, D), lambda j: (0, 0), single=True)
    cols = _spec((T, tn), lambda j: (0, j))
    wrow = _spec((tn, D), lambda j: (j, 0))
    big = jax.ShapeDtypeStruct((T, D_FF), BF)
    grad = jax.ShapeDtypeStruct((D_FF, D), BF)
    return pl.pallas_call(
        body, grid=(D_FF // tn,), in_specs=[resident, resident, wrow, cols, cols, cols],
        out_specs=[cols, cols, wrow, wrow, wrow], out_shape=[big, big, grad, grad, grad],
        scratch_shapes=[pltpu.VMEM((tn, D), F32)] * 3,
        compiler_params=_params(), name="swiglu_bwd")(dh2_b, u2, w_down, gt, up, act)


def gate_up_bwd(dgt, dup, w_gate, w_up, h1, wn, dh2):
    def body(dg_ref, du_ref, wg_ref, wu_ref, h_ref, wn_ref, r_ref, d_ref, db_ref, dw_ref, du_scr):
        _zero_at_first(dw_ref)

        du_scr[...] = jnp.zeros_like(du_scr)

        def tile(c0):
            k = pl.ds(c0, 256)
            du_scr[...] += (jnp.dot(dg_ref[:, k], wg_ref[k, :], preferred_element_type=F32)
                            + jnp.dot(du_ref[:, k], wu_ref[k, :], preferred_element_type=F32))

        _col_tiles(D_FF, 256, tile)
        for r in (0, HALF):
            dh, dw = _rms_bwd(du_scr[r:r + HALF, :], h_ref[r:r + HALF, :], wn_ref[...])
            dw_ref[...] += jnp.sum(dw, axis=0, keepdims=True)
            dh = dh + r_ref[r:r + HALF, :]
            d_ref[r:r + HALF, :] = dh
            db_ref[r:r + HALF, :] = dh.astype(BF)

    return pl.pallas_call(
        body, grid=(T // RC,),
        in_specs=[_rows_spec(D_FF), _rows_spec(D_FF), _whole((D_FF, D)), _whole((D_FF, D)), _rows_spec(D), _vec(D), _rows_spec(D)],
        out_specs=[_rows_spec(D), _rows_spec(D), _vec(D)],
        out_shape=[jax.ShapeDtypeStruct((T, D), F32), jax.ShapeDtypeStruct((T, D), BF), jax.ShapeDtypeStruct((1, D), F32)],
        scratch_shapes=[pltpu.VMEM((RC, D), F32)],
        compiler_params=_params(), name="gate_up_bwd")(dgt, dup, w_gate, w_up, h1, wn, dh2)


def _adamw(w, g, m, v):
    m = ADAM_B1 * m + (1.0 - ADAM_B1) * g
    v = ADAM_B2 * v + (1.0 - ADAM_B2) * (g * g)
    m_hat = m / (1.0 - ADAM_B1 ** ADAM_STEP)
    v_hat = v / (1.0 - ADAM_B2 ** ADAM_STEP)
    delta = -ADAM_LR * (m_hat / (jnp.sqrt(v_hat) + ADAM_EPS) + ADAM_WD * w)
    return delta, m, v


def adamw_shards(name, recvs, ws, ms, vs):
    n = len(ws)

    def body(*refs):
        ins, outs = refs[:4 * n], refs[4 * n:]
        for k in range(n):
            p_ref, w_ref, m_ref, v_ref = ins[k], ins[n + k], ins[2 * n + k], ins[3 * n + k]
            g = p_ref[0].astype(F32)
            for s in range(1, 8):
                g = g + p_ref[s].astype(F32)
            outs[4 * k][...] = g
            outs[4 * k + 1][...], outs[4 * k + 2][...], outs[4 * k + 3][...] = _adamw(w_ref[...], g, m_ref[...], v_ref[...])

    tiles = [_spec((w.shape[0] // 2, w.shape[1]), lambda i: (i, 0)) for w in ws]
    recv_tiles = [_spec((8, w.shape[0] // 2, w.shape[1]), lambda i: (0, i, 0)) for w in ws]
    res = pl.pallas_call(
        body, grid=(2,), in_specs=recv_tiles + tiles * 3,
        out_specs=[t for t in tiles for _ in range(4)],
        out_shape=[jax.ShapeDtypeStruct(w.shape, F32) for w in ws for _ in range(4)],
        compiler_params=_params(), name=name)(*recvs, *ws, *ms, *vs)
    return [list(res[4 * k:4 * k + 4]) for k in range(n)]


def adamw_w_in(recv, w, m, v):
    rows = w.shape[0] // 8

    def body(p_ref, w_ref, m_ref, v_ref, g_ref, d_ref, mo_ref, vo_ref):
        for q in range(8):
            cols = slice(128 * q, 128 * q + 128)
            g = p_ref[0, :, cols].astype(F32)
            for s in range(1, 8):
                g = g + p_ref[s, :, cols].astype(F32)
            part = pl.ds(q, rows, stride=8)
            g_ref[part, :] = g
            d_ref[part, :], mo_ref[part, :], vo_ref[part, :] = _adamw(w_ref[part, :], g, m_ref[part, :], v_ref[part, :])

    shape = jax.ShapeDtypeStruct(w.shape, F32)
    return pl.pallas_call(body, out_shape=[shape] * 4, compiler_params=_params(0), name="adamw_w_in")(recv, w, m, v)


def sum_slabs(recv):
    def body(p_ref, o_ref):
        g = p_ref[0]
        for s in range(1, 8):
            g = g + p_ref[s]
        o_ref[...] = g

    return pl.pallas_call(body, out_shape=jax.ShapeDtypeStruct(recv.shape[1:], F32), compiler_params=_params(0), name="sum_slabs")(recv)


SIMPLE = [("norm1_w", 1024), ("ssd_conv_b", 1536), ("ssd_dt_bias", 16), ("ssd_a_log", 16), ("ssd_d", 16), ("ssd_norm_w", 1024),
          ("lru_conv_b", 1024), ("lru_ba", 1024), ("lru_bx", 1024), ("lru_lambda", 1024), ("lru_norm_w", 1024), ("norm2_w", 1024),
          ("final_norm_w", 1024)]
SPECIAL = ["lru_wa", "lru_wx", "meta_tokens", "ssd_conv_w", "lru_conv_w"]
SM_ROWS = 176
SM_WA, SM_WX, SM_META, SM_SCW, SM_LCW, SM_LOSS = 14, 78, 142, 158, 166, 170


def _simple_rows():
    rows, r = {}, 0
    for name, n in SIMPLE:
        rows[name] = r
        r += -(-n // 1024)
    return rows


def adamw_small(sm, special_g, ws, ms, vs):
    rows = _simple_rows()
    ns, nx = len(SIMPLE), len(SPECIAL)

    def body(*refs):
        sm_ref = refs[0]
        gx = refs[1:1 + nx]
        wr = refs[1 + nx:1 + nx + ns + nx]
        mr = refs[1 + nx + ns + nx:1 + nx + 2 * (ns + nx)]
        vr = refs[1 + nx + 2 * (ns + nx):1 + nx + 3 * (ns + nx)]
        outs = refs[1 + nx + 3 * (ns + nx):]
        o = 0
        for k, (name, n) in enumerate(SIMPLE):
            r0 = rows[name]
            for c0 in range(0, n, 1024):
                wd = min(1024, n - c0)
                g = sm_ref[r0 + c0 // 1024:r0 + c0 // 1024 + 1, 0:wd]
                sl = (slice(None), slice(c0, c0 + wd))
                d, m2, v2 = _adamw(wr[k][sl], g, mr[k][sl], vr[k][sl])
                outs[o][sl] = g
                outs[o + 1][sl] = d
                outs[o + 2][sl] = m2
                outs[o + 3][sl] = v2
            o += 4
        for k in range(nx):
            d, m2, v2 = _adamw(wr[ns + k][...], gx[k][...], mr[ns + k][...], vr[ns + k][...])
            outs[o][...] = d
            outs[o + 1][...] = m2
            outs[o + 2][...] = v2
            o += 3

    out_shape = []
    for k in range(ns):
        out_shape += [jax.ShapeDtypeStruct(ws[k].shape, F32)] * 4
    for k in range(nx):
        out_shape += [jax.ShapeDtypeStruct(ws[ns + k].shape, F32)] * 3
    return pl.pallas_call(body, out_shape=out_shape, compiler_params=_params(0), name="adamw_small")(sm, *special_g, *ws, *ms, *vs)


def _place():
    return lax.axis_index("x"), lax.axis_index("y"), lax.axis_index("c")


def _index(px, py, pc):
    return 4 * px + 2 * py + pc


def all_gather(name, shards):
    n = len(shards)
    hbm = pl.BlockSpec(memory_space=pl.ANY)

    def body(*refs):
        ins, outs = refs[:n], refs[n:2 * n]
        send_sems, recv_sems, local_sems = refs[2 * n:]
        x, y, c = _place()
        me, sibling = (x, y, c), (x, y, 1 - c)
        chips = [(1 - x, y), (x, 1 - y), (1 - x, 1 - y)]

        def copy(i, k, block, to, src=None):
            dst = outs[i].at[_index(*block)]
            return pltpu.make_async_remote_copy(src_ref=dst if src is None else src, dst_ref=dst, send_sem=send_sems.at[7 * i + k],
                                                recv_sem=recv_sems.at[7 * i + k], device_id=to, device_id_type=MESH)

        mine = [pltpu.make_async_copy(ins[i], outs[i].at[_index(*me)], local_sems.at[i]) for i in range(n)]
        for cp in mine:
            cp.start()
        first = []
        for i in range(n):
            first += [copy(i, 1 + j, me, (*chip, c), src=ins[i]) for j, chip in enumerate(chips)]
            first.append(copy(i, 0, me, sibling, src=ins[i]))
        for cp in first:
            cp.start()
        passed = []
        for i in range(n):
            for j, chip in enumerate(chips):
                copy(i, 1 + j, (*chip, c), me).wait_recv()
                cp = copy(i, 4 + j, (*chip, c), sibling)
                cp.start()
                passed.append(cp)
        for i in range(n):
            copy(i, 0, sibling, me).wait_recv()
            for j, chip in enumerate(chips):
                copy(i, 4 + j, (*chip, 1 - c), me).wait_recv()
        for cp in first + passed:
            cp.wait_send()
        for cp in mine:
            cp.wait()

    return pl.pallas_call(
        body, in_specs=[hbm] * n, out_specs=[hbm] * n,
        out_shape=[jax.ShapeDtypeStruct((8,) + s.shape, s.dtype) for s in shards],
        scratch_shapes=[pltpu.SemaphoreType.DMA((7 * n,)), pltpu.SemaphoreType.DMA((7 * n,)), pltpu.SemaphoreType.DMA((n,))],
        name=name)(*shards)


HBM_SPEC = pl.BlockSpec(memory_space=pltpu.HBM)
SEM_SPEC = pl.BlockSpec(memory_space=pltpu.SEMAPHORE)
EFFECT = pltpu.SideEffectType.DATAFLOW_SIDE_EFFECTING


def _peers(x, y, c):
    return [((1 - x) if k & 4 else x, (1 - y) if k & 2 else y, (1 - c) if k & 1 else c) for k in range(1, 8)]


def _pieces(rows):
    for n in (4, 2):
        if rows % (16 * n) == 0:
            return [(r * (rows // n), rows // n) for r in range(n)]
    return [(0, rows)]


def _peer_copies(src, land, send_sems, recv_sems, k, peer, mine, slab_src):
    block = src.at[_index(*peer)] if slab_src else src
    return [pltpu.make_async_remote_copy(src_ref=block.at[pl.ds(r0, nr)], dst_ref=land.at[mine, pl.ds(r0, nr)], send_sem=send_sems.at[k],
                                         recv_sem=recv_sems.at[k], device_id=peer, device_id_type=MESH)
            for r0, nr in _pieces(block.shape[0])]


def copies_start(name, srcs, slab_src, after):
    n = len(srcs)
    zones = [jax.ShapeDtypeStruct(s.shape if slab_src else (8,) + s.shape, s.dtype) for s in srcs]
    afters = [] if after is None else [after]

    def body(*refs):
        ins, lands = refs[:n], refs[n:2 * n]
        first = 2 * n + len(afters)
        sends, recvs = refs[first:first + n], refs[first + n:first + 2 * n]
        token = refs[-1]
        x, y, c = _place()
        mine = _index(x, y, c)
        for i in range(n):
            per_peer = [_peer_copies(ins[i], lands[i], sends[i], recvs[i], k, peer, mine, slab_src) for k, peer in enumerate(_peers(x, y, c))]
            for piece in zip(*per_peer):
                for cp in piece:
                    cp.start()
        token[...] = jnp.zeros_like(token)

    sem = pltpu.SemaphoreType.DMA((7,))
    res = pl.pallas_call(
        body, name=name,
        out_shape=([sem] * (2 * n) + [pltpu.HBM(s.shape, s.dtype) for s in srcs] + [pltpu.HBM(z.shape, z.dtype) for z in zones]
                   + [jax.ShapeDtypeStruct((8, 128), F32)]),
        in_specs=[HBM_SPEC] * (2 * n) + [pl.BlockSpec(memory_space=pl.ANY)] * len(afters),
        out_specs=[SEM_SPEC] * (2 * n) + [HBM_SPEC] * (2 * n) + [pl.BlockSpec(memory_space=pltpu.VMEM)],
        input_output_aliases={i: 2 * n + i for i in range(2 * n)},
        compiler_params=pltpu.CompilerParams(has_side_effects=EFFECT),
    )(*[pltpu.with_memory_space_constraint(s, pltpu.HBM) for s in srcs],
      *[pltpu.with_memory_space_constraint(lax.empty(z.shape, z.dtype), pltpu.HBM) for z in zones], *afters)
    return [(res[i], res[n + i], res[2 * n + i], res[3 * n + i]) for i in range(n)], res[-1][0:1, 0:1]


def copies_wait(name, started, slab_src, after):
    n = len(started)

    def body(*refs):
        ins, lands = refs[:n], refs[n:2 * n]
        sends, recvs = refs[2 * n:3 * n], refs[3 * n:4 * n]
        x, y, c = _place()
        mine = _index(x, y, c)
        for i in range(n):
            for k, peer in enumerate(_peers(x, y, c)):
                arrival = pltpu.make_async_remote_copy(src_ref=ins[i].at[mine] if slab_src else ins[i], dst_ref=lands[i].at[_index(*peer)],
                                                       send_sem=sends[i].at[k], recv_sem=recvs[i].at[k], device_id=peer, device_id_type=MESH)
                arrival.wait_send()
                arrival.wait_recv()

    srcs = [s[2] for s in started]
    lands = [s[3] for s in started]
    afters = list(after) if isinstance(after, (list, tuple)) else [after]
    res = pl.pallas_call(
        body, name=name,
        out_shape=[pltpu.HBM(s.shape, s.dtype) for s in srcs] + [pltpu.HBM(z.shape, z.dtype) for z in lands],
        in_specs=[HBM_SPEC] * (2 * n) + [SEM_SPEC] * (2 * n) + [pl.BlockSpec(memory_space=pl.ANY)] * len(afters),
        out_specs=[HBM_SPEC] * (2 * n),
        input_output_aliases={i: i for i in range(2 * n)},
        compiler_params=pltpu.CompilerParams(has_side_effects=EFFECT),
    )(*srcs, *lands, *[s[0] for s in started], *[s[1] for s in started], *afters)
    me = _index(*_place())
    own = [lax.dynamic_index_in_dim(s, me, 0, keepdims=True) if slab_src else s[None] for s in res[:n]]
    return [lax.dynamic_update_slice_in_dim(z, o, me, 0) for z, o in zip(res[n:], own)]


WEIGHTS = ["meta_tokens", "norm1_w", "w_in", "ssd_conv_w", "ssd_conv_b", "ssd_dt_bias", "ssd_a_log", "ssd_d", "ssd_norm_w", "lru_conv_w",
           "lru_conv_b", "lru_wa", "lru_ba", "lru_wx", "lru_bx", "lru_lambda", "lru_norm_w", "w_out", "norm2_w", "w_gate", "w_up", "w_down",
           "final_norm_w"]
BIG = ["w_in", "w_out", "w_gate", "w_up", "w_down"]
COLUMN_SHARDED = ["w_in", "w_gate", "w_up"]


def _pair_blocks(w):
    w = w.reshape(8, 2, 64, 64)
    z = jnp.zeros((8, 64, 64), w.dtype)
    return jnp.concatenate([jnp.concatenate([w[:, 0], z], axis=2), jnp.concatenate([z, w[:, 1]], axis=2)], axis=1)


def _unpair_blocks(w2):
    return jnp.stack([w2[:, :64, :64], w2[:, 64:, 64:]], axis=1).reshape(16, 64, 64)


def _per_group(v):
    return jnp.pad(v.reshape(2, 1, 8), ((0, 0), (0, 0), (0, 120)))


def _pad_cols(v, n):
    return jnp.pad(v, ((0, 0), (0, n - v.shape[1])))


def local_step(x, target, meta, ssd_cw, lru_cw, w_in, fetch, send, p):
    z120 = jnp.zeros((120, D), BF)
    w_dt = jnp.concatenate([w_in[2560:2568], z120, w_in[2568:2576], z120], axis=0)
    bias2, alog2, d2 = _per_group(p["ssd_dt_bias"]), _per_group(p["ssd_a_log"]), _per_group(p["ssd_d"])
    wa2 = _pair_blocks(p["lru_wa"]).astype(BF)
    wx2 = _pair_blocks(p["lru_wx"]).astype(BF)
    lru = (lru_cw, p["lru_conv_b"], wa2, p["lru_ba"], wx2, p["lru_bx"], p["lru_lambda"])

    h0 = jnp.concatenate([jnp.zeros((NPAD, D), F32), meta, x], axis=0)
    proj, dt_raw, u1 = in_proj(h0, p["norm1_w"], w_in, w_dt)
    yn_ssd, y_pre, h_prev = ssd_fwd(proj, dt_raw, ssd_cw, p["ssd_conv_b"], bias2, alog2, d2, p["ssd_norm_w"])
    hseq, a = lru_fwd(proj, *lru)
    (w_out,) = fetch(["w_out"], hseq)
    h1, cat = out_proj(yn_ssd, proj, hseq, p["lru_norm_w"], w_out, h0)
    w_gate, w_up = fetch(["w_gate", "w_up"], h1)
    gt, up, act, u2 = gate_up(h1, p["norm2_w"], w_gate, w_up)
    (w_down,) = fetch(["w_down"], act)
    dh2, dh2_b, loss, d_fnw = down_loss(act, w_down, h1, target, p["final_norm_w"])

    dgt, dup, g_down, g_gate, g_up = swiglu_bwd(dh2_b, w_down, gt, up, act, u2)
    sent = send({"w_down": g_down, "w_gate": g_gate, "w_up": g_up})
    dh1, dh1_b, d_n2 = gate_up_bwd(dgt, dup, w_gate, w_up, h1, p["norm2_w"] + sent, dh2)
    sent = send({"w_out": weight_grad("dw_out", [cat], dh1_b)})
    dyn, dh_out, dg_b, d_lnw = out_proj_bwd(dh1_b, w_out, proj, hseq, p["lru_norm_w"] + sent)

    dxl_b, d_lcw, d_lcb, dwa2, d_ba, dwx2, d_bx, d_lam = lru_bwd(dh_out, a, hseq, proj, *lru)
    dz_b, dxbc_b, ddt_b, dpar, d_snw, d_scw, d_scb = ssd_bwd(dyn, proj, dt_raw, ssd_cw, p["ssd_conv_b"], y_pre, h_prev, bias2, alog2, d2,
                                                             p["ssd_norm_w"] + sent)
    g_p = weight_grad("dw_in", [dz_b, dg_b, dxl_b, dxbc_b, ddt_b], u1)
    g_in = jnp.concatenate([g_p[PZ:PZ + 1024], g_p[PXBC:PXBC + XBC], g_p[NP_IN:NP_IN + 8], g_p[NP_IN + 128:NP_IN + 136],
                            g_p[PG:PG + 1024], g_p[PXL:PXL + 1024]], axis=0)
    sent = send({"w_in": g_in})
    grad_x, d_meta, d_n1 = in_proj_bwd(dz_b, dg_b, dxl_b, dxbc_b, ddt_b, w_in, w_dt, h0, p["norm1_w"] + sent, dh1)
    small = {"norm1_w": d_n1, "ssd_conv_b": d_scb, "ssd_dt_bias": dpar[:, 0, :8].reshape(1, 16), "ssd_a_log": dpar[:, 1, :8].reshape(1, 16),
             "ssd_d": dpar[:, 2, :8].reshape(1, 16), "ssd_norm_w": d_snw, "lru_conv_b": d_lcb, "lru_ba": d_ba, "lru_bx": d_bx,
             "lru_lambda": d_lam, "lru_norm_w": d_lnw, "norm2_w": d_n2, "final_norm_w": d_fnw,
             "lru_wa": _unpair_blocks(dwa2), "lru_wx": _unpair_blocks(dwx2), "meta_tokens": d_meta,
             "ssd_conv_w": d_scw, "lru_conv_w": d_lcw}
    return loss, grad_x, small


def _pack_small(small, loss):
    rows = [_pad_cols(small[name], -(-n // 1024) * 1024).reshape(-1, 1024) for name, n in SIMPLE]
    rows += [small["lru_wa"].reshape(64, 1024), small["lru_wx"].reshape(64, 1024), small["meta_tokens"],
             _pad_cols(small["ssd_conv_w"], 2048).reshape(8, 1024), small["lru_conv_w"], _pad_cols(loss[:, 0:1], 1024)]
    sm = jnp.concatenate(rows, axis=0)
    return jnp.pad(sm, ((0, SM_ROWS - sm.shape[0]), (0, 0)))


def _slabs(g):
    return g.reshape(8, g.shape[0] // 8, g.shape[1])


def _unslab(g):
    return g.reshape(8 * g.shape[1], g.shape[2])


def kernel(x, meta_tokens, norm1_w, w_in, ssd_conv_w, ssd_conv_b, ssd_dt_bias, ssd_a_log, ssd_d, ssd_norm_w, lru_conv_w, lru_conv_b, lru_wa, lru_ba, lru_wx, lru_bx, lru_lambda, lru_norm_w, w_out, norm2_w, w_gate, w_up, w_down, final_norm_w, loss_target, m_meta_tokens, m_norm1_w, m_w_in, m_ssd_conv_w, m_ssd_conv_b, m_ssd_dt_bias, m_ssd_a_log, m_ssd_d, m_ssd_norm_w, m_lru_conv_w, m_lru_conv_b, m_lru_wa, m_lru_ba, m_lru_wx, m_lru_bx, m_lru_lambda, m_lru_norm_w, m_w_out, m_norm2_w, m_w_gate, m_w_up, m_w_down, m_final_norm_w, v_meta_tokens, v_norm1_w, v_w_in, v_ssd_conv_w, v_ssd_conv_b, v_ssd_dt_bias, v_ssd_a_log, v_ssd_d, v_ssd_norm_w, v_lru_conv_w, v_lru_conv_b, v_lru_wa, v_lru_ba, v_lru_wx, v_lru_bx, v_lru_lambda, v_lru_norm_w, v_w_out, v_norm2_w, v_w_gate, v_w_up, v_w_down, v_final_norm_w):
    w = dict(meta_tokens=meta_tokens, norm1_w=norm1_w, w_in=w_in[0], ssd_conv_w=ssd_conv_w[0], ssd_conv_b=ssd_conv_b, ssd_dt_bias=ssd_dt_bias,
             ssd_a_log=ssd_a_log, ssd_d=ssd_d, ssd_norm_w=ssd_norm_w, lru_conv_w=lru_conv_w[0], lru_conv_b=lru_conv_b, lru_wa=lru_wa[0],
             lru_ba=lru_ba, lru_wx=lru_wx[0], lru_bx=lru_bx, lru_lambda=lru_lambda, lru_norm_w=lru_norm_w, w_out=w_out[0], norm2_w=norm2_w,
             w_gate=w_gate[0], w_up=w_up[0], w_down=w_down[0], final_norm_w=final_norm_w.reshape(1, D))
    m = dict(meta_tokens=m_meta_tokens, norm1_w=m_norm1_w, w_in=m_w_in[0], ssd_conv_w=m_ssd_conv_w[0], ssd_conv_b=m_ssd_conv_b,
             ssd_dt_bias=m_ssd_dt_bias, ssd_a_log=m_ssd_a_log, ssd_d=m_ssd_d, ssd_norm_w=m_ssd_norm_w, lru_conv_w=m_lru_conv_w[0],
             lru_conv_b=m_lru_conv_b, lru_wa=m_lru_wa[0], lru_ba=m_lru_ba, lru_wx=m_lru_wx[0], lru_bx=m_lru_bx, lru_lambda=m_lru_lambda,
             lru_norm_w=m_lru_norm_w, w_out=m_w_out[0], norm2_w=m_norm2_w, w_gate=m_w_gate[0], w_up=m_w_up[0], w_down=m_w_down[0],
             final_norm_w=m_final_norm_w.reshape(1, D))
    v = dict(meta_tokens=v_meta_tokens, norm1_w=v_norm1_w, w_in=v_w_in[0], ssd_conv_w=v_ssd_conv_w[0], ssd_conv_b=v_ssd_conv_b,
             ssd_dt_bias=v_ssd_dt_bias, ssd_a_log=v_ssd_a_log, ssd_d=v_ssd_d, ssd_norm_w=v_ssd_norm_w, lru_conv_w=v_lru_conv_w[0],
             lru_conv_b=v_lru_conv_b, lru_wa=v_lru_wa[0], lru_ba=v_lru_ba, lru_wx=v_lru_wx[0], lru_bx=v_lru_bx, lru_lambda=v_lru_lambda,
             lru_norm_w=v_lru_norm_w, w_out=v_w_out[0], norm2_w=v_norm2_w, w_gate=v_w_gate[0], w_up=v_w_up[0], w_down=v_w_down[0],
             final_norm_w=v_final_norm_w.reshape(1, D))
    shapes = dict(meta_tokens=meta_tokens.shape, norm1_w=norm1_w.shape, w_in=w_in.shape, ssd_conv_w=ssd_conv_w.shape,
                  ssd_conv_b=ssd_conv_b.shape, ssd_dt_bias=ssd_dt_bias.shape, ssd_a_log=ssd_a_log.shape, ssd_d=ssd_d.shape,
                  ssd_norm_w=ssd_norm_w.shape, lru_conv_w=lru_conv_w.shape, lru_conv_b=lru_conv_b.shape, lru_wa=lru_wa.shape,
                  lru_ba=lru_ba.shape, lru_wx=lru_wx.shape, lru_bx=lru_bx.shape, lru_lambda=lru_lambda.shape, lru_norm_w=lru_norm_w.shape,
                  w_out=w_out.shape, norm2_w=norm2_w.shape, w_gate=w_gate.shape, w_up=w_up.shape, w_down=w_down.shape,
                  final_norm_w=final_norm_w.shape)
    me = _index(*_place())
    for n in COLUMN_SHARDED:
        w[n], m[n], v[n] = w[n].T, m[n].T, v[n].T

    small_shard = jnp.concatenate([w["meta_tokens"], _pad_cols(w["ssd_conv_w"], 256).reshape(8, 128), w["lru_conv_w"],
                                   jnp.zeros((4, 128), F32)], axis=0)
    g_in, gs = all_gather("gather_w_in", [w["w_in"].astype(BF), small_shard])
    later = ["w_out", "w_gate", "w_up", "w_down"]
    started, behind = copies_start("gather_rest_start", [w[n].astype(BF) for n in later], False, gs)
    started = dict(zip(later, started))
    meta_full = gs[:, 0:16].transpose(1, 0, 2).reshape(N_META, D)
    ssd_cw = gs[:, 16:24].reshape(8, 4, 256)[:, :, :192].transpose(1, 0, 2).reshape(4, XBC)
    lru_cw = gs[:, 24:28].transpose(1, 0, 2).reshape(4, LRU_W)

    def fetch(names, after):
        got = copies_wait("gather_" + names[0] + "_wait", [started[n] for n in names], False, after)
        return [_unslab(g) for g in got]

    in_flight = {}

    def send(grads):
        names = list(grads)
        st, token = copies_start("grads_" + names[0] + "_start", [grads[n] if n == "small" else _slabs(grads[n]) for n in names], True, None)
        in_flight.update(zip(names, st))
        return token

    loss, grad_x, small = local_step(x[0], loss_target[0], meta_full, ssd_cw, lru_cw, _unslab(g_in), fetch, send,
                                     {**w, "norm1_w": w["norm1_w"] + behind})
    send({"small": _pack_small(small, loss).reshape(8, SM_ROWS // 8, 1024)})

    out = {}
    early = ["w_down", "w_gate", "w_up", "w_out"]
    recv = dict(zip(early, copies_wait("grads_early_wait", [in_flight[n] for n in early], True, in_flight["small"][2])))
    for pair in (early[:2], early[2:]):
        done = adamw_shards("adamw_" + pair[0], [recv[n] for n in pair], [w[n] for n in pair], [m[n] for n in pair], [v[n] for n in pair])
        out.update(zip(pair, done))
    recv_in, recv_small = copies_wait("grads_late_wait", [in_flight["w_in"], in_flight["small"]], True, [out[n][0] for n in early])
    untiled = (IN_COLS, 128)
    out["w_in"] = [o.reshape(IN_COLS // 8, D)
                   for o in adamw_w_in(recv_in, w["w_in"].reshape(untiled), m["w_in"].reshape(untiled), v["w_in"].reshape(untiled))]
    for n in COLUMN_SHARDED:
        out[n] = [o.T for o in out[n]]
    sm = all_gather("gather_small_grads", [sum_slabs(recv_small)])[0].reshape(SM_ROWS, 1024)
    special_g =[sm[SM_WA:SM_WA + 64].reshape(16, 64, 64), sm[SM_WX:SM_WX + 64].reshape(16, 64, 64),
                 lax.dynamic_slice(sm[SM_META:SM_META + 16], (0, 128 * me), (16, 128)),
                 lax.dynamic_slice(sm[SM_SCW:SM_SCW + 8].reshape(4, 2048), (0, 192 * me), (4, 192)),
                 lax.dynamic_slice(sm[SM_LCW:SM_LCW + 4], (0, 128 * me), (4, 128))]
    names = [n for n, _ in SIMPLE] + SPECIAL
    res = adamw_small(sm, special_g, [w[n] for n in names], [m[n] for n in names], [v[n] for n in names])
    for k, (n, _) in enumerate(SIMPLE):
        out[n] = res[4 * k:4 * k + 4]
    for k, n in enumerate(SPECIAL):
        o = 4 * len(SIMPLE) + 3 * k
        out[n] = [special_g[k]] + list(res[o:o + 3])
    loss_total = sm[SM_LOSS, 0]
    flat = [loss_total, grad_x[None]]
    for k in range(4):
        flat += [out[n][k].reshape(shapes[n]) for n in WEIGHTS]
    return tuple(flat)
```

```python
import math

import jax
import jax.numpy as jnp
from jax import lax
from jax.experimental import pallas as pl
from jax.experimental.pallas import tpu as pltpu

F32 = jnp.float32
BF = jnp.bfloat16

D = 1024
SEQ = 2048
N_META = 16
Q = 128
NPAD = 112
T = NPAD + N_META + SEQ
NCH = T // Q
RC = 544
D_FF = 2816
SSD_W = 1024
LRU_W = 1024
XBC = 1536
IN_COLS = 4624
PZ, PG, PXL, PXBC = 0, 1024, 2048, 3072
NP_IN = 4608
EPS = 1e-6
LRU_C = 8.0
VMEM_LIMIT = 56 * 1024 * 1024

ADAM_LR, ADAM_B1, ADAM_B2, ADAM_EPS, ADAM_WD, ADAM_STEP = 0.001, 0.9, 0.999, 1e-08, 0.01, 10

NT_DIMS = (((1,), (1,)), ((), ()))
TN_DIMS = (((0,), (0,)), ((), ()))
MESH = pl.DeviceIdType.MESH


def _params(n_grid=1, limit=VMEM_LIMIT):
    return pltpu.CompilerParams(dimension_semantics=("arbitrary",) * n_grid, vmem_limit_bytes=limit)


def _spec(shape, imap, single=False):
    if single:
        return pl.BlockSpec(shape, imap, pipeline_mode=pl.Buffered(1))
    return pl.BlockSpec(shape, imap)


def _sigmoid(x):
    return 0.5 * jnp.tanh(0.5 * x) + 0.5


def _sigmoid_gate(x):
    return 1.0 / (1.0 + jnp.exp(-x))


def _softplus(x):
    return jnp.maximum(x, 0.0) + jnp.log(1.0 + jnp.exp(-jnp.abs(x)))


def _rms_stats(h):
    return lax.rsqrt(jnp.mean(h * h, axis=-1, keepdims=True) + EPS)


def _rms(h, w):
    return (h * _rms_stats(h)) * w


def _rms_bwd(du, h, w):
    r = _rms_stats(h)
    n = h * r
    dn = du * w
    dh = r * (dn - n * jnp.mean(dn * n, axis=-1, keepdims=True))
    return dh, du * n


_G0 = math.sqrt(2.0 / math.pi)


def _gelu(x):
    return 0.5 * x * (1.0 + jnp.tanh(_G0 * (x + 0.044715 * (x * x * x))))


def _gelu_grad(x):
    t = jnp.tanh(_G0 * (x + 0.044715 * (x * x * x)))
    return 0.5 * (1.0 + t) + 0.5 * x * (1.0 - t * t) * (_G0 * (1.0 + 3.0 * 0.044715 * (x * x)))


def _rows(shape, r0=0):
    return lax.broadcasted_iota(jnp.int32, shape, 0) + r0


def _lanes(shape):
    return lax.broadcasted_iota(jnp.int32, shape, 1)


HALO = 8


def _fill_padded(pad_ref, x_ref):
    pad_ref[0:HALO, :] = jnp.zeros((HALO, pad_ref.shape[1]), F32)
    pad_ref[T + HALO:T + 2 * HALO, :] = jnp.zeros((HALO, pad_ref.shape[1]), F32)

    def step(c, carry):
        r0 = pl.multiple_of(c * Q, Q)
        pad_ref[pl.ds(r0 + HALO, Q), :] = x_ref[pl.ds(r0, Q), :].astype(F32)
        return carry

    lax.fori_loop(0, NCH, step, 0)


def _back(pad_ref, r0):
    win = pad_ref[pl.ds(r0, Q + HALO), :]
    return lambda s: win[HALO:, :] if s == 0 else pltpu.roll(win, s, axis=0)[HALO:, :]


def _ahead(pad_ref, r0):
    win = pad_ref[pl.ds(r0 + HALO, Q + HALO), :]
    return lambda s: win[:Q, :] if s == 0 else pltpu.roll(win, Q + HALO - s, axis=0)[:Q, :]


def _conv(back, w, b):
    y = b + w[3:4, :] * back(0)
    for k in range(3):
        y = y + w[k:k + 1, :] * back(3 - k)
    return y


def _conv_bwd_x(ahead, w):
    dx = w[3:4, :] * ahead(0)
    for k in range(3):
        dx = dx + w[k:k + 1, :] * ahead(3 - k)
    return dx


def _conv_bwd_w(dy, back):
    dws = [jnp.sum(dy * back(3 - k), axis=0, keepdims=True) for k in range(4)]
    return jnp.concatenate(dws, axis=0), jnp.sum(dy, axis=0, keepdims=True)


def _chunks(fn, unrolled=False):
    if unrolled:
        for c in range(NCH):
            fn(c * Q)
        return

    def step(c, carry):
        fn(pl.multiple_of(c * Q, Q))
        return carry

    lax.fori_loop(0, NCH, step, 0)


HALF = RC // 2


def _col_tiles(n, tn, fn):
    def step(j, carry):
        fn(pl.multiple_of(j * tn, tn))
        return carry

    lax.fori_loop(0, n // tn, step, 0)


def _rows_spec(cols, block_col=0):
    return _spec((RC, cols), lambda i: (i, block_col))


def _whole(shape):
    return _spec(shape, lambda i: tuple(0 for _ in shape), single=True)


def _vec(cols):
    return _spec((1, cols), lambda i: (0, 0))


def _zero_at_first(*refs):
    @pl.when(pl.program_id(0) == 0)
    def _():
        for r in refs:
            r[...] = jnp.zeros_like(r)


IN_RUNS = ((PZ, 0, 1024), (PG, 2576, 2048), (PXBC, 1024, XBC))


def _in_tiles(fn):
    for pcol, wrow, width in IN_RUNS:
        def step(j, carry, pcol=pcol, wrow=wrow):
            fn(pl.multiple_of(pcol + j * 512, 512), pl.multiple_of(wrow + j * 512, 16))
            return carry

        lax.fori_loop(0, width // 512, step, 0)


def in_proj(h0, wn, w_t, w_dt):
    def body(h_ref, wn_ref, w_ref, wdt_ref, o_ref, dt_ref, u_ref):
        for r in (0, HALF):
            u_ref[r:r + HALF, :] = _rms(h_ref[r:r + HALF, :], wn_ref[...]).astype(BF)

        def tile(pcol, wrow):
            o_ref[:, pl.ds(pcol, 512)] = lax.dot_general(u_ref[...], w_ref[pl.ds(wrow, 512), :], NT_DIMS, preferred_element_type=F32).astype(BF)

        _in_tiles(tile)
        dt_ref[...] = lax.dot_general(u_ref[...], wdt_ref[...], NT_DIMS, preferred_element_type=F32)

    return pl.pallas_call(
        body, grid=(T // RC,), in_specs=[_rows_spec(D), _vec(D), _whole((IN_COLS, D)), _whole((256, D))],
        out_specs=[_rows_spec(NP_IN), _rows_spec(256), _rows_spec(D)],
        out_shape=[jax.ShapeDtypeStruct((T, NP_IN), BF), jax.ShapeDtypeStruct((T, 256), F32), jax.ShapeDtypeStruct((T, D), BF)],
        compiler_params=_params(), name="in_proj")(h0, wn, w_t, w_dt)


def out_proj(yn_ssd, proj, hseq, lru_nw, w_out, h0):
    def body(y_ref, g_ref, h_ref, wn_ref, w_ref, r_ref, o_ref, cat_ref):
        cat_ref[:, 0:SSD_W] = y_ref[...]
        for r in (0, HALF):
            y = _gelu(g_ref[r:r + HALF, :].astype(F32)) * h_ref[r:r + HALF, :]
            cat_ref[r:r + HALF, SSD_W:] = _rms(y, wn_ref[...]).astype(BF)

        def tile(c0):
            o_ref[:, pl.ds(c0, 512)] = r_ref[:, pl.ds(c0, 512)] + jnp.dot(cat_ref[...], w_ref[:, pl.ds(c0, 512)], preferred_element_type=F32)

        _col_tiles(D, 512, tile)

    return pl.pallas_call(
        body, grid=(T // RC,),
        in_specs=[_rows_spec(SSD_W), _rows_spec(LRU_W, PG // LRU_W), _rows_spec(LRU_W), _vec(LRU_W), _whole((SSD_W + LRU_W, D)), _rows_spec(D)],
        out_specs=[_rows_spec(D), _rows_spec(SSD_W + LRU_W)],
        out_shape=[jax.ShapeDtypeStruct((T, D), F32), jax.ShapeDtypeStruct((T, SSD_W + LRU_W), BF)],
        compiler_params=_params(), name="out_proj")(yn_ssd, proj, hseq, lru_nw, w_out, h0)


def out_proj_bwd(dh1_b, w_out, proj, hseq, lru_nw):
    def body(d_ref, w_ref, g_ref, h_ref, wn_ref, dy_ref, dh_ref, dg_ref, dw_ref, dl_scr):
        _zero_at_first(dw_ref)

        def tile(c0):
            dy_ref[:, pl.ds(c0, 512)] = lax.dot_general(d_ref[...], w_ref[pl.ds(c0, 512), :], NT_DIMS, preferred_element_type=F32)
            dl_scr[:, pl.ds(c0, 512)] = lax.dot_general(d_ref[...], w_ref[pl.ds(SSD_W + c0, 512), :], NT_DIMS, preferred_element_type=F32)

        _col_tiles(SSD_W, 512, tile)

        for r in (0, HALF):
            g = g_ref[r:r + HALF, :].astype(F32)
            h = h_ref[r:r + HALF, :]
            ge = _gelu(g)
            dy, dw = _rms_bwd(dl_scr[r:r + HALF, :], ge * h, wn_ref[...])
            dw_ref[...] += jnp.sum(dw, axis=0, keepdims=True)
            dh_ref[r:r + HALF, :] = dy * ge
            dg_ref[r:r + HALF, :] = (dy * h * _gelu_grad(g)).astype(BF)

    return pl.pallas_call(
        body, grid=(T // RC,),
        in_specs=[_rows_spec(D), _whole((SSD_W + LRU_W, D)), _rows_spec(LRU_W, PG // LRU_W), _rows_spec(LRU_W), _vec(LRU_W)],
        out_specs=[_rows_spec(SSD_W), _rows_spec(LRU_W), _rows_spec(LRU_W), _vec(LRU_W)],
        out_shape=[jax.ShapeDtypeStruct((T, SSD_W), F32), jax.ShapeDtypeStruct((T, LRU_W), F32), jax.ShapeDtypeStruct((T, LRU_W), BF),
                   jax.ShapeDtypeStruct((1, LRU_W), F32)],
        scratch_shapes=[pltpu.VMEM((RC, LRU_W), F32)],
        compiler_params=_params(), name="out_proj_bwd")(dh1_b, w_out, proj, hseq, lru_nw)


def in_proj_bwd(dz, dg, dxl, dxbc, ddt, w_t, w_dt, h0, wn, dh1):
    first = NPAD + N_META

    def body(dz_ref, dg_ref, dxl_ref, dxbc_ref, ddt_ref, w_ref, wdt_ref, h_ref, wn_ref, r_ref, gx_hbm, meta_ref, dw_ref, du_scr, o_ref, sem):
        i = pl.program_id(0)
        _zero_at_first(dw_ref)
        du_scr[...] = jnp.dot(ddt_ref[...], wdt_ref[...], preferred_element_type=F32)
        for d_ref, wrow, width in ((dz_ref, 0, 1024), (dxbc_ref, 1024, XBC), (dg_ref, 2576, 1024), (dxl_ref, 3600, 1024)):
            def step(j, carry, d_ref=d_ref, wrow=wrow):
                c0 = pl.multiple_of(j * 512, 512)
                du_scr[...] += jnp.dot(d_ref[:, pl.ds(c0, 512)], w_ref[pl.ds(pl.multiple_of(wrow + c0, 16), 512), :], preferred_element_type=F32)
                return carry

            lax.fori_loop(0, width // 512, step, 0)
        for r in (0, HALF):
            dh, dw = _rms_bwd(du_scr[r:r + HALF, :], h_ref[r:r + HALF, :], wn_ref[...])
            dw_ref[...] += jnp.sum(dw, axis=0, keepdims=True)
            o_ref[r:r + HALF, :] = dh + r_ref[r:r + HALF, :]

        @pl.when(i == 0)
        def _():
            meta_ref[...] = o_ref[NPAD:first, :]
            head = pltpu.make_async_copy(o_ref.at[pl.ds(first, RC - first)], gx_hbm.at[pl.ds(0, RC - first)], sem)
            head.start()
            head.wait()

        @pl.when(i > 0)
        def _():
            rest = pltpu.make_async_copy(o_ref, gx_hbm.at[pl.ds(pl.multiple_of(i * RC - first, 32), RC)], sem)
            rest.start()
            rest.wait()

    return pl.pallas_call(
        body, grid=(T // RC,),
        in_specs=[_rows_spec(SSD_W), _rows_spec(LRU_W), _rows_spec(LRU_W), _rows_spec(XBC), _rows_spec(256), _whole((IN_COLS, D)),
                  _whole((256, D)), _rows_spec(D), _vec(D), _rows_spec(D)],
        out_specs=[pl.BlockSpec(memory_space=pl.ANY), _spec((N_META, D), lambda i: (0, 0)), _vec(D)],
        out_shape=[jax.ShapeDtypeStruct((SEQ, D), F32), jax.ShapeDtypeStruct((N_META, D), F32), jax.ShapeDtypeStruct((1, D), F32)],
        scratch_shapes=[pltpu.VMEM((RC, D), F32), pltpu.VMEM((RC, D), F32), pltpu.SemaphoreType.DMA],
        compiler_params=_params(), name="in_proj_bwd")(dz, dg, dxl, dxbc, ddt, w_t, w_dt, h0, wn, dh1)


GRAD_TILE = 256


def weight_grad(name, a, u1):
    tm = GRAD_TILE

    def body(a_ref, u_ref, o_ref):
        o_ref[...] = lax.dot_general(a_ref[...], u_ref[...], TN_DIMS, preferred_element_type=F32).astype(BF)

    return pl.pallas_call(
        body, grid=(a.shape[1] // tm,),
        in_specs=[_spec((T, tm), lambda j: (0, j)), _spec((T, D), lambda j: (0, 0), single=True)],
        out_specs=_spec((tm, D), lambda j: (j, 0)),
        out_shape=jax.ShapeDtypeStruct((a.shape[1], D), BF),
        compiler_params=_params(), name=name)(a, u1)


def in_weight_grad(dz, dg, dxl, dxbc, ddt, u1):
    tm = GRAD_TILE
    per_row = D // 128
    parts = (dz, dg, dxl, dxbc, ddt)
    first_rows = (0, 2576, 3600, 1024, 2560)
    tiles = [p.shape[1] // tm for p in parts]
    starts = [sum(tiles[:k]) for k in range(len(parts))]
    last = sum(tiles) - 1
    dt_lines = 8 * per_row

    def body(*refs):
        a_refs, u_ref, o_hbm, mix_scr, stage, sems = refs[:5], refs[5], refs[6], refs[7], refs[8], refs[9]
        step = pl.program_id(0)
        slot = step % 2
        line0 = 0
        for a_ref, start, n, first in zip(a_refs, starts, tiles, first_rows):
            here = (step >= start) & (step < start + n)
            line0 = jnp.where(here, per_row * (first + tm * (step - start)), line0)

            @pl.when(here)
            def _(a_ref=a_ref):
                res = lax.dot_general(a_ref[...], u_ref[...], TN_DIMS, preferred_element_type=F32)
                for q in range(per_row):
                    mix_scr[pl.ds(q, tm, stride=per_row), :] = res[:, 128 * q:128 * q + 128]

        def tile_copy(of_slot, to):
            return pltpu.make_async_copy(stage.at[of_slot], o_hbm.at[pl.ds(to, per_row * tm)], sems.at[of_slot])

        @pl.when(step >= 2)
        def _():
            tile_copy(slot, 0).wait()

        stage[slot] = mix_scr[...].astype(BF)

        @pl.when(step < last)
        def _():
            tile_copy(slot, pl.multiple_of(line0, 128)).start()

        @pl.when(step == last)
        def _():
            halves = [pltpu.make_async_copy(stage.at[slot, pl.ds(128 * per_row * k, dt_lines)],
                                            o_hbm.at[pl.ds(per_row * (first_rows[-1] + 8 * k), dt_lines)], sems.at[2 + k]) for k in range(2)]
            for cp in halves:
                cp.start()
            tile_copy(1 - slot, 0).wait()
            for cp in halves:
                cp.wait()

    def tile_of(start, n):
        return lambda j: (0, jnp.clip(j - start, 0, n - 1))

    return pl.pallas_call(
        body, grid=(last + 1,),
        in_specs=[_spec((T, tm), tile_of(s, n)) for s, n in zip(starts, tiles)] + [_spec((T, D), lambda j: (0, 0), single=True)],
        out_specs=pl.BlockSpec(memory_space=pl.ANY),
        out_shape=jax.ShapeDtypeStruct((per_row * IN_COLS, 128), BF),
        scratch_shapes=[pltpu.VMEM((per_row * tm, 128), F32), pltpu.VMEM((2, per_row * tm, 128), BF), pltpu.SemaphoreType.DMA((4,))],
        compiler_params=_params(), name="dw_in")(*parts, u1)


def _ssd_chunk_common(row0, dt_ref, b_ref, c_ref, bias, a_neg):
    shape = (Q, Q)
    lane = _lanes(shape)
    sub = _rows(shape)
    live = (_rows(shape, row0) >= NPAD) & (lane < 8)
    dtr = dt_ref[:, :]
    dt = jnp.where(live, _softplus(dtr + bias), 0.0)
    d_a = dt * a_neg
    tri = (sub >= lane).astype(F32)
    cs = jnp.dot(tri, d_a, precision=lax.Precision.HIGHEST, preferred_element_type=F32)
    cs_t = cs.T
    b_f = b_ref[:, :]
    bc = b_f.astype(BF)
    cc = c_ref[:, :].astype(BF)
    cb = lax.dot_general(cc, bc, NT_DIMS, preferred_element_type=F32)
    cs_last = cs[Q - 1:Q, :]
    return dict(lane=lane, sub=sub, live=live, dtr=dtr, dt=dt, cs=cs, cs_t=cs_t, bc=bc, cc=cc, cb=cb, bc_t=b_f.T.astype(BF),
                ecs=jnp.exp(cs), dsm=jnp.exp(cs_last - cs), gam=jnp.exp(cs_last))


def _pair(lane_even, mat, j):
    return jnp.where(lane_even, mat[:, j:j + 1], mat[:, j + 1:j + 2])


def _pair_row(lane_even, mat, j):
    return jnp.where(lane_even[0:1, :], mat[:, j:j + 1], mat[:, j + 1:j + 2])


def _head_decay(cm, j):
    seg = cm["cs"][:, j:j + 1] - cm["cs_t"][j:j + 1, :]
    return jnp.exp(jnp.where(cm["sub"] >= cm["lane"], seg, -jnp.inf))


def _head_decay_t(cm, j):
    seg = cm["cs_t"][j:j + 1, :] - cm["cs"][:, j:j + 1]
    return jnp.exp(jnp.where(cm["lane"] >= cm["sub"], seg, -jnp.inf))


def _conv_window(raw_ref, halo_ref, pad_scr):
    pad_scr[0:HALO, :] = halo_ref[...].astype(F32)[halo_ref.shape[0] - HALO:, :]
    pad_scr[HALO:HALO + Q, :] = raw_ref[...].astype(F32)
    win = pad_scr[...]
    return lambda s: win[HALO:, :] if s == 0 else pltpu.roll(win, s, axis=0)[HALO:, :]


def _xbc_cols(g):
    return slice(512 * g, 512 * g + 512), slice(SSD_W + 128 * g, SSD_W + 128 * g + 128), slice(SSD_W + 256 + 128 * g, SSD_W + 384 + 128 * g)


def ssd_fwd(proj, dt_raw, conv_w, conv_b, dt_bias2, a_log2, d2, norm_w):
    def body(raw_ref, halo_ref, dt_all, z_all, cw_ref, cb_ref, bias_all, alog_all, d_all, nw_all, yn_all, y_all, hp_all,
             h_all, pad_scr, act_scr):
        @pl.when(pl.program_id(0) == 0)
        def _():
            h_all[...] = jnp.zeros_like(h_all)

        pre = _conv(_conv_window(raw_ref, halo_ref, pad_scr), cw_ref[...], cb_ref[...])
        act_scr[...] = pre * _sigmoid(pre)
        for g in range(2):
            wide, thin = slice(512 * g, 512 * g + 512), slice(128 * g, 128 * g + 128)
            xs, bs, cs = _xbc_cols(g)
            group(act_scr.at[:, xs], act_scr.at[:, bs], act_scr.at[:, cs], dt_all.at[:, thin], z_all.at[:, wide], bias_all.at[g],
                  alog_all.at[g], d_all.at[g], nw_all.at[:, wide], yn_all.at[:, wide], y_all.at[:, wide], hp_all.at[g, 0], h_all.at[g])

    def group(x_ref, b_ref, c_ref, dt_ref, z_ref, bias_ref, alog_ref, d_ref, nw_ref, yn_ref, y_ref, hp_ref, h_scr):
        bias = bias_ref[...]
        a_neg = -jnp.exp(alog_ref[...])
        dsk = d_ref[...]
        cm = _ssd_chunk_common(pl.program_id(0) * Q, dt_ref, b_ref, c_ref, bias, a_neg)
        lane_even = cm["lane"] < 64
        for p in range(4):
            je, jo = 2 * p, 2 * p + 1
            xp = x_ref[:, 128 * p:128 * p + 128]
            xdt = xp * _pair(lane_even, cm["dt"], je)
            xdt_b = xdt.astype(BF)
            m_e = (cm["cb"] * _head_decay(cm, je)).astype(BF)
            m_o = (cm["cb"] * _head_decay(cm, jo)).astype(BF)
            zero = jnp.zeros_like(xdt_b)
            yd = (jnp.dot(m_e, jnp.where(lane_even, xdt_b, zero), preferred_element_type=F32)
                  + jnp.dot(m_o, jnp.where(lane_even, zero, xdt_b), preferred_element_type=F32))
            hp = h_scr[p]
            hp_ref[p] = hp
            yo = jnp.dot(cm["cc"], hp.astype(BF), preferred_element_type=F32) * _pair(lane_even, cm["ecs"], je)
            y_ref[:, 128 * p:128 * p + 128] = yd + yo + xp * _pair_row(lane_even, dsk, je)
            st = jnp.dot(cm["bc_t"], (xdt * _pair(lane_even, cm["dsm"], je)).astype(BF), preferred_element_type=F32)
            h_scr[p] = hp * _pair_row(lane_even, cm["gam"], je) + st
        zc = z_ref[:, :].astype(F32)
        gated = y_ref[:, :] * (zc * _sigmoid(zc))
        yn_ref[:, :] = _rms(gated, nw_ref[...]).astype(BF)

    par = _spec((2, 1, 128), lambda c: (0, 0, 0))
    wide = _spec((Q, SSD_W), lambda c: (c, 0))
    xbc = PXBC // XBC
    halo = 2 * HALO
    return pl.pallas_call(
        body, grid=(NCH,),
        in_specs=[_spec((Q, XBC), lambda c: (c, xbc)), _spec((halo, XBC), lambda c: (jnp.maximum(c * (Q // halo) - 1, 0), xbc)),
                  _spec((Q, 256), lambda c: (c, 0)), wide, _spec((4, XBC), lambda c: (0, 0)), _spec((1, XBC), lambda c: (0, 0)),
                  par, par, par, _spec((1, SSD_W), lambda c: (0, 0))],
        out_specs=[wide, wide, _spec((2, 1, 4, 128, 128), lambda c: (0, c, 0, 0, 0))],
        out_shape=[jax.ShapeDtypeStruct((T, SSD_W), BF), jax.ShapeDtypeStruct((T, SSD_W), F32),
                   jax.ShapeDtypeStruct((2, NCH, 4, 128, 128), F32)],
        scratch_shapes=[pltpu.VMEM((2, 4, 128, 128), F32), pltpu.VMEM((Q + HALO, XBC), F32), pltpu.VMEM((Q, XBC), F32)],
        compiler_params=_params(), name="ssd_fwd")(proj, proj, dt_raw, proj, conv_w, conv_b, dt_bias2, a_log2, d2, norm_w)


def ssd_bwd(dyn, proj, dt_raw, conv_w, conv_b, y_pre, h_prev, dt_bias2, a_log2, d2, norm_w):
    def body(dyn_all, raw_ref, halo_ref, dt_all, z_all, y_all, hp_all, cw_ref, cb_ref, bias_all, alog_all, d_all, nw_all,
             dz_all, dxbc_ref, ddt_all, dpar_all, dnw_all, dcw_ref, dcb_ref, dh_all, acc_all, pad_scr, act_scr, dsilu_scr, dact_scr, dpad_scr):
        @pl.when(pl.program_id(0) == 0)
        def _():
            dh_all[...] = jnp.zeros_like(dh_all)
            acc_all[...] = jnp.zeros_like(acc_all)
            dnw_all[...] = jnp.zeros_like(dnw_all)
            dcw_ref[...] = jnp.zeros_like(dcw_ref)
            dcb_ref[...] = jnp.zeros_like(dcb_ref)
            dpad_scr[Q:Q + HALO, :] = jnp.zeros((HALO, XBC), F32)

        back = _conv_window(raw_ref, halo_ref, pad_scr)
        pre = _conv(back, cw_ref[...], cb_ref[...])
        sg = _sigmoid(pre)
        act_scr[...] = pre * sg
        dsilu_scr[...] = sg * (1.0 + pre * (1.0 - sg))
        for g in range(2):
            wide, thin = slice(512 * g, 512 * g + 512), slice(128 * g, 128 * g + 128)
            xs, bs, cs = _xbc_cols(g)
            group(dyn_all.at[:, wide], act_scr.at[:, xs], act_scr.at[:, bs], act_scr.at[:, cs], dt_all.at[:, thin], z_all.at[:, wide],
                  y_all.at[:, wide], hp_all.at[g, 0], bias_all.at[g], alog_all.at[g], d_all.at[g], nw_all.at[:, wide],
                  dz_all.at[:, wide], dact_scr.at[:, xs], dact_scr.at[:, bs], dact_scr.at[:, cs], ddt_all.at[:, thin], dpar_all.at[g],
                  dnw_all.at[:, wide], dh_all.at[g], acc_all.at[g])
        dpre = dact_scr[...] * dsilu_scr[...]
        dcw, dcb = _conv_bwd_w(dpre, back)
        dcw_ref[...] += dcw
        dcb_ref[...] += dcb
        dpad_scr[0:Q, :] = dpre
        win = dpad_scr[...]
        dxbc_ref[...] = _conv_bwd_x(lambda s: win[:Q, :] if s == 0 else pltpu.roll(win, Q + HALO - s, axis=0)[:Q, :], cw_ref[...]).astype(BF)
        dpad_scr[Q:Q + HALO, :] = dpre[0:HALO, :]

    def group(dyn_ref, x_ref, b_ref, c_ref, dt_ref, z_ref, y_ref, hp_ref, bias_ref, alog_ref, d_ref, nw_ref,
              dz_ref, dx_ref, db_ref, dc_ref, ddt_ref, dpar_ref, dnw_ref, dh_scr, acc_scr):
        ci = pl.program_id(0)
        bias = bias_ref[...]
        a_neg = -jnp.exp(alog_ref[...])
        dsk = d_ref[...]
        cm = _ssd_chunk_common((NCH - 1 - ci) * Q, dt_ref, b_ref, c_ref, bias, a_neg)
        lane, sub = cm["lane"], cm["sub"]
        lane_even = lane < 64
        cc_t = c_ref[:, :].T.astype(BF)
        cb_t = lax.dot_general(cm["bc"], cm["cc"], NT_DIMS, preferred_element_type=F32)
        zc = z_ref[:, :].astype(F32)
        yc = y_ref[:, :]
        sg = _sigmoid(zc)
        sz = zc * sg
        dgated, dnw = _rms_bwd(dyn_ref[:, :], yc * sz, nw_ref[...])
        dnw_ref[...] += jnp.sum(dnw, axis=0, keepdims=True)
        dz_ref[:, :] = (dgated * yc * (sg * (1.0 + zc * (1.0 - sg)))).astype(BF)
        dy_all = dgated * sz
        dcb = jnp.zeros((Q, Q), F32)
        dcb_t = jnp.zeros((Q, Q), F32)
        db_acc = jnp.zeros((Q, Q), F32)
        dc_acc = jnp.zeros((Q, Q), F32)
        dcs = jnp.zeros((Q, Q), F32)
        ddt = jnp.zeros((Q, Q), F32)
        for p in range(4):
            je, jo = 2 * p, 2 * p + 1
            xp = x_ref[:, 128 * p:128 * p + 128]
            dy = dy_all[:, 128 * p:128 * p + 128]
            dt_p = _pair(lane_even, cm["dt"], je)
            xdt = xp * dt_p
            xdt_b = xdt.astype(BF)
            dy_b = dy.astype(BF)
            zero = jnp.zeros_like(dy_b)
            hp = hp_ref[p]
            hp_b = hp.astype(BF)
            dh = dh_scr[p]
            dh_b = dh.astype(BF)
            acc_scr[p:p + 1, :] += jnp.sum(dy * xp, axis=0, keepdims=True)
            dxp = dy * _pair_row(lane_even, dsk, je)
            e_p = _pair(lane_even, cm["ecs"], je)
            g_p = jnp.dot(cm["cc"], hp_b, preferred_element_type=F32)
            dg_b = (dy * e_p).astype(BF)
            de = dy * g_p * e_p
            dc_acc = dc_acc + lax.dot_general(dg_b, hp_b, NT_DIMS, preferred_element_type=F32)
            dh_in = jnp.dot(cc_t, dg_b, preferred_element_type=F32)
            ds_p = _pair(lane_even, cm["dsm"], je)
            r_p = jnp.dot(cm["bc"], dh_b, preferred_element_type=F32)
            dxdt = r_p * ds_p
            tt = r_p * xdt * ds_p
            db_acc = db_acc + lax.dot_general((xdt * ds_p).astype(BF), dh_b, NT_DIMS, preferred_element_type=F32)
            dgam_m = jnp.sum(dh * hp, axis=0, keepdims=True)
            for j, even in ((je, True), (jo, False)):
                sel = lane_even if even else jnp.logical_not(lane_even)
                dy_j = jnp.where(sel, dy_b, zero)
                l_j = _head_decay(cm, j)
                l_jt = _head_decay_t(cm, j)
                m_j = cm["cb"] * l_j
                m_jt = cb_t * l_jt
                dm = lax.dot_general(dy_j, xdt_b, NT_DIMS, preferred_element_type=F32)
                dm_t = lax.dot_general(xdt_b, dy_j, NT_DIMS, preferred_element_type=F32)
                dxdt = dxdt + jnp.dot(m_jt.astype(BF), dy_j, preferred_element_type=F32)
                dcb = dcb + dm * l_j
                dcb_t = dcb_t + dm_t * l_jt
                t_j = jnp.where(sel, tt, 0.0)
                col = jnp.sum(dm * m_j - dm_t * m_jt + (jnp.where(sel, de, 0.0) - t_j), axis=1, keepdims=True)
                gam_j = cm["gam"][:, j:j + 1]
                last = (jnp.sum(jnp.sum(t_j, axis=0, keepdims=True), axis=1, keepdims=True)
                        + jnp.sum(jnp.where(sel[0:1, :], dgam_m, 0.0), axis=1, keepdims=True) * gam_j)
                col = col + jnp.where(sub[:, 0:1] == Q - 1, last, 0.0)
                dcs = dcs + jnp.where(lane == j, col, 0.0)
            dh_scr[p] = dh_in + dh * _pair_row(lane_even, cm["gam"], je)
            dx_ref[:, 128 * p:128 * p + 128] = dxp + dxdt * dt_p
            dd = dxdt * xp
            ddt = ddt + jnp.where(lane == je, jnp.sum(jnp.where(lane_even, dd, 0.0), axis=1, keepdims=True), 0.0)
            ddt = ddt + jnp.where(lane == jo, jnp.sum(jnp.where(lane_even, 0.0, dd), axis=1, keepdims=True), 0.0)
        dc_ref[:, :] = dc_acc + jnp.dot(dcb.astype(BF), cm["bc"], preferred_element_type=F32)
        db_ref[:, :] = db_acc + jnp.dot(dcb_t.astype(BF), cm["cc"], preferred_element_type=F32)
        tri_t = (sub <= lane).astype(F32)
        dd_a = jnp.dot(tri_t, dcs, precision=lax.Precision.HIGHEST, preferred_element_type=F32)
        ddt = ddt + dd_a * a_neg
        acc_scr[5:6, :] += jnp.sum(dd_a * cm["dt"], axis=0, keepdims=True)
        draw = jnp.where(cm["live"], ddt * _sigmoid_gate(cm["dtr"] + bias), 0.0)
        acc_scr[4:5, :] += jnp.sum(draw, axis=0, keepdims=True)
        ddt_ref[:, :] = draw.astype(BF)

        @pl.when(ci == NCH - 1)
        def _():
            lane1 = _lanes((1, 128))
            dd = jnp.zeros((1, 128), F32)
            for p in range(4):
                row = acc_scr[p:p + 1, :]
                dd = dd + jnp.where(lane1 == 2 * p, jnp.sum(jnp.where(lane1 < 64, row, 0.0), axis=1, keepdims=True), 0.0)
                dd = dd + jnp.where(lane1 == 2 * p + 1, jnp.sum(jnp.where(lane1 < 64, 0.0, row), axis=1, keepdims=True), 0.0)
            dpar_ref[...] = jnp.concatenate([acc_scr[4:5, :], acc_scr[5:6, :] * a_neg, dd, jnp.zeros((5, 128), F32)], axis=0)

    par = _spec((2, 1, 128), lambda c: (0, 0, 0))
    wide = _spec((Q, SSD_W), lambda c: (NCH - 1 - c, 0))
    thin = _spec((Q, 256), lambda c: (NCH - 1 - c, 0))
    vec = _spec((1, SSD_W), lambda c: (0, 0))
    xbc = PXBC // XBC
    halo = 2 * HALO
    chunk = pltpu.VMEM((Q, XBC), F32)
    padded = pltpu.VMEM((Q + HALO, XBC), F32)
    return pl.pallas_call(
        body, grid=(NCH,),
        in_specs=[wide, _spec((Q, XBC), lambda c: (NCH - 1 - c, xbc)),
                  _spec((halo, XBC), lambda c: (jnp.maximum((NCH - 1 - c) * (Q // halo) - 1, 0), xbc)), thin, wide, wide,
                  _spec((2, 1, 4, 128, 128), lambda c: (0, NCH - 1 - c, 0, 0, 0)), _spec((4, XBC), lambda c: (0, 0)),
                  _spec((1, XBC), lambda c: (0, 0)), par, par, par, vec],
        out_specs=[wide, _spec((Q, XBC), lambda c: (NCH - 1 - c, 0)), thin, _spec((2, 8, 128), lambda c: (0, 0, 0)), vec,
                   _spec((4, XBC), lambda c: (0, 0)), _spec((1, XBC), lambda c: (0, 0))],
        out_shape=[jax.ShapeDtypeStruct((T, SSD_W), BF), jax.ShapeDtypeStruct((T, XBC), BF), jax.ShapeDtypeStruct((T, 256), BF),
                   jax.ShapeDtypeStruct((2, 8, 128), F32), jax.ShapeDtypeStruct((1, SSD_W), F32), jax.ShapeDtypeStruct((4, XBC), F32),
                   jax.ShapeDtypeStruct((1, XBC), F32)],
        scratch_shapes=[pltpu.VMEM((2, 4, 128, 128), F32), pltpu.VMEM((2, 8, 128), F32), padded, chunk, chunk, chunk, padded],
        compiler_params=_params(), name="ssd_bwd")(dyn, proj, proj, dt_raw, proj, y_pre, h_prev, conv_w, conv_b, dt_bias2, a_log2, d2, norm_w)


def _lru_gates(back, cw, cb, wa, ba, wx, bx, lam):
    xr = _conv(back, cw, cb)
    xr_b = xr.astype(BF)
    r = _sigmoid_gate(jnp.dot(xr_b, wa, preferred_element_type=F32) + ba)
    i = _sigmoid_gate(jnp.dot(xr_b, wx, preferred_element_type=F32) + bx)
    sp = _softplus(-lam)
    la = (-LRU_C) * r * sp
    a = jnp.exp(la)
    mult2 = -jnp.tanh(la) * (a * a + 1.0)
    return xr, xr_b, r, i, sp, a, jnp.sqrt(mult2), mult2


SEG_LEN = 68
SEGS = T // SEG_LEN


def _seg_rows(j, k, off=0):
    return pl.ds(off + j * 8 * SEG_LEN + k, 8, stride=SEG_LEN)


def _segmented_scan(mul_ref, mul_row0, add_ref, out_ref, loc_scr, prod_scr, carry_scr, reverse):
    groups = SEGS // 8
    off = mul_row0 + (1 if reverse else 0)

    def local(i, carry):
        k = SEG_LEN - 1 - i if reverse else i
        new = []
        for j in range(groups):
            h, p = carry[2 * j], carry[2 * j + 1]
            m = mul_ref[_seg_rows(j, k, off), :]
            h = m * h + add_ref[_seg_rows(j, k), :]
            p = m * p
            loc_scr[_seg_rows(j, k), :] = h
            prod_scr[_seg_rows(j, k), :] = p
            new += [h, p]
        return tuple(new)

    lax.fori_loop(0, SEG_LEN, local, (jnp.zeros((8, 128), F32), jnp.ones((8, 128), F32)) * groups)

    def chain(i, c):
        s = SEGS - 1 - i if reverse else i
        carry_scr[pl.ds(s, 1), :] = c
        edge = s * SEG_LEN + (0 if reverse else SEG_LEN - 1)
        return loc_scr[pl.ds(edge, 1), :] + prod_scr[pl.ds(edge, 1), :] * c

    lax.fori_loop(0, SEGS, chain, jnp.zeros((1, 128), F32))

    def fold(k, carry):
        for j in range(groups):
            rows = _seg_rows(j, k)
            out_ref[rows, :] = loc_scr[rows, :] + prod_scr[rows, :] * carry_scr[8 * j:8 * j + 8, :]
        return carry

    lax.fori_loop(0, SEG_LEN, fold, 0)


def lru_fwd(proj, cw, cb, wa2, ba, wx2, bx, lam):
    def body(x_ref, cw_ref, cb_ref, wa_ref, ba_ref, wx_ref, bx_ref, lam_ref, h_ref, a_ref, xpad, u_scr, loc_scr, prod_scr, carry_scr):
        _fill_padded(xpad, x_ref)

        def chunk(r0):
            xr, _, _, i, _, a, mult, _ = _lru_gates(_back(xpad, r0), cw_ref[...], cb_ref[...], wa_ref[0], ba_ref[...], wx_ref[0], bx_ref[...],
                                                 lam_ref[...])
            a_ref[pl.ds(r0, Q), :] = a
            u_scr[pl.ds(r0, Q), :] = jnp.where(_rows(a.shape, r0) >= NPAD, mult * (i * xr), 0.0)

        _chunks(chunk, unrolled=True)
        _segmented_scan(a_ref, 0, u_scr, h_ref, loc_scr, prod_scr, carry_scr, reverse=False)

    c0 = PXL // 128
    vec = _spec((1, 128), lambda c: (0, c))
    mat = _spec((1, 128, 128), lambda c: (c, 0, 0))
    seq = pltpu.VMEM((T, 128), F32)
    return pl.pallas_call(
        body, grid=(8,),
        in_specs=[_spec((T, 128), lambda c: (0, c0 + c)), _spec((4, 128), lambda c: (0, c)), vec, mat, vec, mat, vec, vec],
        out_specs=[_spec((T, 128), lambda c: (0, c)), _spec((T, 128), lambda c: (0, c))],
        out_shape=[jax.ShapeDtypeStruct((T, LRU_W), F32), jax.ShapeDtypeStruct((T, LRU_W), F32)],
        scratch_shapes=[pltpu.VMEM((T + 2 * HALO, 128), F32), seq, seq, seq, pltpu.VMEM((SEGS, 128), F32)],
        compiler_params=_params(), name="lru_fwd")(proj, cw, cb, wa2, ba, wx2, bx, lam)


def lru_bwd(dh_out, a, hseq, proj, cw, cb, wa2, ba, wx2, bx, lam):
    def body(d_ref, a_ref, h_ref, x_ref, cw_ref, cb_ref, wa_ref, ba_ref, wx_ref, bx_ref, lam_ref,
             dx_ref, dcw_ref, dcb_ref, dwa_ref, dba_ref, dwx_ref, dbx_ref, dlam_ref, xpad, hpad, dpad, dh_ref, loc_scr, prod_scr, carry_scr):
        _fill_padded(dpad, a_ref)
        _segmented_scan(dpad, HALO, d_ref, dh_ref, loc_scr, prod_scr, carry_scr, reverse=True)
        _fill_padded(xpad, x_ref)
        _fill_padded(hpad, h_ref)
        dpad[0:HALO, :] = jnp.zeros((HALO, 128), F32)
        dpad[T + HALO:T + 2 * HALO, :] = jnp.zeros((HALO, 128), F32)
        for ref in (dcw_ref, dcb_ref, dwa_ref, dba_ref, dwx_ref, dbx_ref, dlam_ref):
            ref[...] = jnp.zeros_like(ref)
        lam = lam_ref[...]

        def first(r0):
            back = _back(xpad, r0)
            xr, xr_b, r, i, sp, a, mult, mult2 = _lru_gates(back, cw_ref[...], cb_ref[...], wa_ref[0], ba_ref[...], wx_ref[0], bx_ref[...], lam)
            dh = dh_ref[pl.ds(r0, Q), :]
            da = dh * _back(hpad, r0)(1)
            du = jnp.where(_rows(dh.shape, r0) >= NPAD, dh, 0.0)
            dmult = du * (i * xr)
            di = du * (mult * xr)
            dxr = du * (mult * i)
            dla = da * a - dmult * (a * a) * lax.rsqrt(mult2)
            dr = dla * ((-LRU_C) * sp)
            dlam_ref[...] += jnp.sum(dla * ((-LRU_C) * r), axis=0, keepdims=True)
            dpr = dr * r * (1.0 - r)
            dpi = di * i * (1.0 - i)
            dba_ref[...] += jnp.sum(dpr, axis=0, keepdims=True)
            dbx_ref[...] += jnp.sum(dpi, axis=0, keepdims=True)
            dpr_b = dpr.astype(BF)
            dpi_b = dpi.astype(BF)
            dxr = (dxr + lax.dot_general(dpr_b, wa_ref[0], NT_DIMS, preferred_element_type=F32)
                   + lax.dot_general(dpi_b, wx_ref[0], NT_DIMS, preferred_element_type=F32))
            dwa_ref[0] += lax.dot_general(xr_b, dpr_b, TN_DIMS, preferred_element_type=F32)
            dwx_ref[0] += lax.dot_general(xr_b, dpi_b, TN_DIMS, preferred_element_type=F32)
            dpad[pl.ds(r0 + HALO, Q), :] = dxr
            dcw, dcb = _conv_bwd_w(dxr, back)
            dcw_ref[...] += dcw
            dcb_ref[...] += dcb

        _chunks(first, unrolled=True)
        dlam_ref[...] = -dlam_ref[...] * _sigmoid_gate(-lam)

        def second(r0):
            dx_ref[pl.ds(r0, Q), :] = _conv_bwd_x(_ahead(dpad, r0), cw_ref[...]).astype(BF)

        _chunks(second)

    c0 = PXL // 128
    vec = _spec((1, 128), lambda c: (0, c))
    mat = _spec((1, 128, 128), lambda c: (c, 0, 0))
    col = _spec((T, 128), lambda c: (0, c))
    vshape = jax.ShapeDtypeStruct((1, LRU_W), F32)
    mshape = jax.ShapeDtypeStruct((8, 128, 128), F32)
    pad = pltpu.VMEM((T + 2 * HALO, 128), F32)
    seq = pltpu.VMEM((T, 128), F32)
    return pl.pallas_call(
        body, grid=(8,),
        in_specs=[col, col, col, _spec((T, 128), lambda c: (0, c0 + c)), _spec((4, 128), lambda c: (0, c)), vec, mat, vec, mat, vec, vec],
        out_specs=[col, _spec((4, 128), lambda c: (0, c)), vec, mat, vec, mat, vec, vec],
        out_shape=[jax.ShapeDtypeStruct((T, LRU_W), BF), jax.ShapeDtypeStruct((4, LRU_W), F32), vshape, mshape, vshape, mshape, vshape, vshape],
        scratch_shapes=[pad, pad, pad, seq, seq, seq, pltpu.VMEM((SEGS, 128), F32)],
        compiler_params=_params(), name="lru_bwd")(dh_out, a, hseq, proj, cw, cb, wa2, ba, wx2, bx, lam)


def gate_up(h1, wn, w_gate, w_up):
    def body(h_ref, wn_ref, wg_ref, wu_ref, gt_ref, up_ref, act_ref, u_ref):
        for r in (0, HALF):
            u_ref[r:r + HALF, :] = _rms(h_ref[r:r + HALF, :], wn_ref[...]).astype(BF)

        def tile(c0):
            cols = pl.ds(c0, 256)
            gt = lax.dot_general(u_ref[...], wg_ref[cols, :], NT_DIMS, preferred_element_type=F32)
            up = lax.dot_general(u_ref[...], wu_ref[cols, :], NT_DIMS, preferred_element_type=F32)
            gt_ref[:, cols] = gt.astype(BF)
            up_ref[:, cols] = up.astype(BF)
            act_ref[:, cols] = (gt * _sigmoid(gt) * up).astype(BF)

        _col_tiles(D_FF, 256, tile)

    big = jax.ShapeDtypeStruct((T, D_FF), BF)
    return pl.pallas_call(
        body, grid=(T // RC,), in_specs=[_rows_spec(D), _vec(D), _whole((D_FF, D)), _whole((D_FF, D))],
        out_specs=[_rows_spec(D_FF), _rows_spec(D_FF), _rows_spec(D_FF), _rows_spec(D)],
        out_shape=[big, big, big, jax.ShapeDtypeStruct((T, D), BF)],
        compiler_params=_params(), name="gate_up")(h1, wn, w_gate, w_up)


def down_loss(act, w_down, h1, target, wf):
    first = NPAD + N_META

    def body(a_ref, w_ref, r_ref, t_hbm, wf_ref, d_ref, db_ref, l_ref, dw_ref, h_scr, t_ref, t_sem):
        i = pl.program_id(0)
        _zero_at_first(l_ref, dw_ref)
        head = pltpu.make_async_copy(t_hbm.at[pl.ds(0, RC - first)], t_ref.at[pl.ds(first, RC - first)], t_sem)
        rest = pltpu.make_async_copy(t_hbm.at[pl.ds(pl.multiple_of(jnp.maximum(i * RC - first, 0), 32), RC)], t_ref, t_sem)

        @pl.when(i == 0)
        def _():
            t_ref[0:first, :] = jnp.zeros((first, D), F32)
            head.start()

        @pl.when(i > 0)
        def _():
            rest.start()

        def tile(c0):
            cols = pl.ds(c0, 512)
            h_scr[:, cols] = r_ref[:, cols] + jnp.dot(a_ref[...], w_ref[:, cols], preferred_element_type=F32)

        _col_tiles(D, 512, tile)

        @pl.when(i == 0)
        def _():
            head.wait()

        @pl.when(i > 0)
        def _():
            rest.wait()

        for r in (0, HALF):
            h = h_scr[r:r + HALF, :]
            live = _rows((HALF, D), i * RC + r) >= first
            err = jnp.where(live, _rms(h, wf_ref[...]) - t_ref[r:r + HALF, :], 0.0)
            l_ref[...] += 0.5 * jnp.sum(jnp.sum(err * err, axis=1, keepdims=True) * (1.0 / D), axis=0, keepdims=True)
            dh, dw = _rms_bwd(err * (1.0 / D), h, wf_ref[...])
            dw_ref[...] += jnp.sum(dw, axis=0, keepdims=True)
            d_ref[r:r + HALF, :] = dh
            db_ref[r:r + HALF, :] = dh.astype(BF)

    return pl.pallas_call(
        body, grid=(T // RC,),
        in_specs=[_rows_spec(D_FF), _whole((D_FF, D)), _rows_spec(D), pl.BlockSpec(memory_space=pl.ANY), _vec(D)],
        out_specs=[_rows_spec(D), _rows_spec(D), _spec((1, 128), lambda i: (0, 0)), _vec(D)],
        out_shape=[jax.ShapeDtypeStruct((T, D), F32), jax.ShapeDtypeStruct((T, D), BF), jax.ShapeDtypeStruct((1, 128), F32),
                   jax.ShapeDtypeStruct((1, D), F32)],
        scratch_shapes=[pltpu.VMEM((RC, D), F32), pltpu.VMEM((RC, D), F32), pltpu.SemaphoreType.DMA],
        compiler_params=_params(), name="down_loss")(act, w_down, h1, target, wf)


def swiglu_bwd(dh2_b, w_down, gt, up, act, u2):
    tn = 256

    def body(d_ref, u_ref, w_ref, gt_ref, up_ref, act_ref, dg_ref, du_ref, gd_ref, gg_ref, gu_ref, acc_d, acc_g, acc_u):
        for acc in (acc_d, acc_g, acc_u):
            acc[...] = jnp.zeros_like(acc)

        def rows(r0):
            part = pl.ds(r0, RC)
            d = d_ref[part, :]
            dact = lax.dot_general(d, w_ref[...], NT_DIMS, preferred_element_type=F32)
            gt_ = gt_ref[part, :].astype(F32)
            up_ = up_ref[part, :].astype(F32)
            sg = _sigmoid(gt_)
            dgt = (dact * up_ * (sg * (1.0 + gt_ * (1.0 - sg)))).astype(BF)
            dup = (dact * (gt_ * sg)).astype(BF)
            dg_ref[part, :] = dgt
            du_ref[part, :] = dup
            u = u_ref[part, :]
            acc_d[...] += lax.dot_general(act_ref[part, :], d, TN_DIMS, preferred_element_type=F32)
            acc_g[...] += lax.dot_general(dgt, u, TN_DIMS, preferred_element_type=F32)
            acc_u[...] += lax.dot_general(dup, u, TN_DIMS, preferred_element_type=F32)

        _col_tiles(T, RC, rows)
        gd_ref[...] = acc_d[...].astype(BF)
        gg_ref[...] = acc_g[...].astype(BF)
        gu_ref[...] = acc_u[...].astype(BF)

    resident = _spec((T, D), lambda j: (0, 0), single=True)
    cols = _spec((T, tn), lambda j: (0, j))
    wrow = _spec((tn, D), lambda j: (j, 0))
    big = jax.ShapeDtypeStruct((T, D_FF), BF)
    grad = jax.ShapeDtypeStruct((D_FF, D), BF)
    return pl.pallas_call(
        body, grid=(D_FF // tn,), in_specs=[resident, resident, wrow, cols, cols, cols],
        out_specs=[cols, cols, wrow, wrow, wrow], out_shape=[big, big, grad, grad, grad],
        scratch_shapes=[pltpu.VMEM((tn, D), F32)] * 3,
        compiler_params=_params(), name="swiglu_bwd")(dh2_b, u2, w_down, gt, up, act)


def gate_up_bwd(dgt, dup, w_gate, w_up, h1, wn, dh2):
    def body(dg_ref, du_ref, wg_ref, wu_ref, h_ref, wn_ref, r_ref, d_ref, db_ref, dw_ref, du_scr):
        _zero_at_first(dw_ref)

        du_scr[...] = jnp.zeros_like(du_scr)

        def tile(c0):
            k = pl.ds(c0, 256)
            du_scr[...] += (jnp.dot(dg_ref[:, k], wg_ref[k, :], preferred_element_type=F32)
                            + jnp.dot(du_ref[:, k], wu_ref[k, :], preferred_element_type=F32))

        _col_tiles(D_FF, 256, tile)
        for r in (0, HALF):
            dh, dw = _rms_bwd(du_scr[r:r + HALF, :], h_ref[r:r + HALF, :], wn_ref[...])
            dw_ref[...] += jnp.sum(dw, axis=0, keepdims=True)
            dh = dh + r_ref[r:r + HALF, :]
            d_ref[r:r + HALF, :] = dh
            db_ref[r:r + HALF, :] = dh.astype(BF)

    return pl.pallas_call(
        body, grid=(T // RC,),
        in_specs=[_rows_spec(D_FF), _rows_spec(D_FF), _whole((D_FF, D)), _whole((D_FF, D)), _rows_spec(D), _vec(D), _rows_spec(D)],
        out_specs=[_rows_spec(D), _rows_spec(D), _vec(D)],
        out_shape=[jax.ShapeDtypeStruct((T, D), F32), jax.ShapeDtypeStruct((T, D), BF), jax.ShapeDtypeStruct((1, D), F32)],
        scratch_shapes=[pltpu.VMEM((RC, D), F32)],
        compiler_params=_params(), name="gate_up_bwd")(dgt, dup, w_gate, w_up, h1, wn, dh2)


def _adamw(w, g, m, v):
    m = ADAM_B1 * m + (1.0 - ADAM_B1) * g
    v = ADAM_B2 * v + (1.0 - ADAM_B2) * (g * g)
    m_hat = m / (1.0 - ADAM_B1 ** ADAM_STEP)
    v_hat = v / (1.0 - ADAM_B2 ** ADAM_STEP)
    delta = -ADAM_LR * (m_hat / (jnp.sqrt(v_hat) + ADAM_EPS) + ADAM_WD * w)
    return delta, m, v


def adamw_shards(name, recvs, ws, ms, vs):
    n = len(ws)

    def body(*refs):
        ins, outs = refs[:4 * n], refs[4 * n:]
        for k in range(n):
            p_ref, w_ref, m_ref, v_ref = ins[k], ins[n + k], ins[2 * n + k], ins[3 * n + k]
            g = p_ref[0].astype(F32)
            for s in range(1, 8):
                g = g + p_ref[s].astype(F32)
            outs[4 * k][...] = g
            outs[4 * k + 1][...], outs[4 * k + 2][...], outs[4 * k + 3][...] = _adamw(w_ref[...], g, m_ref[...], v_ref[...])

    tiles = [_spec((w.shape[0] // 2, w.shape[1]), lambda i: (i, 0)) for w in ws]
    recv_tiles = [_spec((8, w.shape[0] // 2, w.shape[1]), lambda i: (0, i, 0)) for w in ws]
    res = pl.pallas_call(
        body, grid=(2,), in_specs=recv_tiles + tiles * 3,
        out_specs=[t for t in tiles for _ in range(4)],
        out_shape=[jax.ShapeDtypeStruct(w.shape, F32) for w in ws for _ in range(4)],
        compiler_params=_params(), name=name)(*recvs, *ws, *ms, *vs)
    return [list(res[4 * k:4 * k + 4]) for k in range(n)]


def adamw_w_in(recv, w, m, v):
    rows = 34
    per_row = D // 128

    def body(p_ref, w_ref, m_ref, v_ref, g_ref, d_ref, mo_ref, vo_ref):
        def chunk(c, carry):
            lines = pl.ds(pl.multiple_of(c * per_row * rows, 16), per_row * rows)
            g = p_ref[0, lines, :].astype(F32)
            for s in range(1, 8):
                g = g + p_ref[s, lines, :].astype(F32)
            g = g.reshape(rows, per_row, 128)
            part = pl.ds(c * rows, rows)
            g_ref[part] = g
            d_ref[part], mo_ref[part], vo_ref[part] = _adamw(w_ref[part], g, m_ref[part], v_ref[part])
            return carry

        lax.fori_loop(0, w.shape[0] // rows, chunk, 0)

    shape = jax.ShapeDtypeStruct(w.shape, F32)
    return pl.pallas_call(body, out_shape=[shape] * 4, compiler_params=_params(0), name="adamw_w_in")(recv, w, m, v)


def sum_slabs(recv):
    def body(p_ref, o_ref):
        g = p_ref[0]
        for s in range(1, 8):
            g = g + p_ref[s]
        o_ref[...] = g

    return pl.pallas_call(body, out_shape=jax.ShapeDtypeStruct(recv.shape[1:], F32), compiler_params=_params(0), name="sum_slabs")(recv)


SIMPLE = [("norm1_w", 1024), ("ssd_conv_b", 1536), ("ssd_dt_bias", 16), ("ssd_a_log", 16), ("ssd_d", 16), ("ssd_norm_w", 1024),
          ("lru_conv_b", 1024), ("lru_ba", 1024), ("lru_bx", 1024), ("lru_lambda", 1024), ("lru_norm_w", 1024), ("norm2_w", 1024),
          ("final_norm_w", 1024)]
SPECIAL = ["lru_wa", "lru_wx", "meta_tokens", "ssd_conv_w", "lru_conv_w"]
SM_ROWS = 176
SM_WA, SM_WX, SM_META, SM_SCW, SM_LCW, SM_LOSS = 14, 78, 142, 158, 166, 170


def _simple_rows():
    rows, r = {}, 0
    for name, n in SIMPLE:
        rows[name] = r
        r += -(-n // 1024)
    return rows


def adamw_small(sm, special_g, ws, ms, vs):
    rows = _simple_rows()
    ns, nx = len(SIMPLE), len(SPECIAL)

    def body(*refs):
        sm_ref = refs[0]
        gx = refs[1:1 + nx]
        wr = refs[1 + nx:1 + nx + ns + nx]
        mr = refs[1 + nx + ns + nx:1 + nx + 2 * (ns + nx)]
        vr = refs[1 + nx + 2 * (ns + nx):1 + nx + 3 * (ns + nx)]
        outs = refs[1 + nx + 3 * (ns + nx):]
        o = 0
        for k, (name, n) in enumerate(SIMPLE):
            r0 = rows[name]
            for c0 in range(0, n, 1024):
                wd = min(1024, n - c0)
                g = sm_ref[r0 + c0 // 1024:r0 + c0 // 1024 + 1, 0:wd]
                sl = (slice(None), slice(c0, c0 + wd))
                d, m2, v2 = _adamw(wr[k][sl], g, mr[k][sl], vr[k][sl])
                outs[o][sl] = g
                outs[o + 1][sl] = d
                outs[o + 2][sl] = m2
                outs[o + 3][sl] = v2
            o += 4
        for k in range(nx):
            d, m2, v2 = _adamw(wr[ns + k][...], gx[k][...], mr[ns + k][...], vr[ns + k][...])
            outs[o][...] = d
            outs[o + 1][...] = m2
            outs[o + 2][...] = v2
            o += 3

    out_shape = []
    for k in range(ns):
        out_shape += [jax.ShapeDtypeStruct(ws[k].shape, F32)] * 4
    for k in range(nx):
        out_shape += [jax.ShapeDtypeStruct(ws[ns + k].shape, F32)] * 3
    return pl.pallas_call(body, out_shape=out_shape, compiler_params=_params(0), name="adamw_small")(sm, *special_g, *ws, *ms, *vs)


def _place():
    return lax.axis_index("x"), lax.axis_index("y"), lax.axis_index("c")


def _index(px, py, pc):
    return 4 * px + 2 * py + pc


def all_gather(name, shards):
    n = len(shards)
    hbm = pl.BlockSpec(memory_space=pl.ANY)

    def body(*refs):
        ins, outs = refs[:n], refs[n:2 * n]
        send_sems, recv_sems, local_sems = refs[2 * n:]
        x, y, c = _place()
        me, sibling = (x, y, c), (x, y, 1 - c)
        chips = [(1 - x, y), (x, 1 - y), (1 - x, 1 - y)]

        def copy(i, k, block, to, src=None):
            dst = outs[i].at[_index(*block)]
            return pltpu.make_async_remote_copy(src_ref=dst if src is None else src, dst_ref=dst, send_sem=send_sems.at[7 * i + k],
                                                recv_sem=recv_sems.at[7 * i + k], device_id=to, device_id_type=MESH)

        mine = [pltpu.make_async_copy(ins[i], outs[i].at[_index(*me)], local_sems.at[i]) for i in range(n)]
        for cp in mine:
            cp.start()
        first = []
        for i in range(n):
            first += [copy(i, 1 + j, me, (*chip, c), src=ins[i]) for j, chip in enumerate(chips)]
            first.append(copy(i, 0, me, sibling, src=ins[i]))
        for cp in first:
            cp.start()
        passed = []
        for i in range(n):
            for j, chip in enumerate(chips):
                copy(i, 1 + j, (*chip, c), me).wait_recv()
                cp = copy(i, 4 + j, (*chip, c), sibling)
                cp.start()
                passed.append(cp)
        for i in range(n):
            copy(i, 0, sibling, me).wait_recv()
            for j, chip in enumerate(chips):
                copy(i, 4 + j, (*chip, 1 - c), me).wait_recv()
        for cp in first + passed:
            cp.wait_send()
        for cp in mine:
            cp.wait()

    return pl.pallas_call(
        body, in_specs=[hbm] * n, out_specs=[hbm] * n,
        out_shape=[jax.ShapeDtypeStruct((8,) + s.shape, s.dtype) for s in shards],
        scratch_shapes=[pltpu.SemaphoreType.DMA((7 * n,)), pltpu.SemaphoreType.DMA((7 * n,)), pltpu.SemaphoreType.DMA((n,))],
        name=name)(*shards)


HBM_SPEC = pl.BlockSpec(memory_space=pltpu.HBM)
SEM_SPEC = pl.BlockSpec(memory_space=pltpu.SEMAPHORE)
EFFECT = pltpu.SideEffectType.DATAFLOW_SIDE_EFFECTING


def _peers(x, y, c):
    return [((1 - x) if k & 4 else x, (1 - y) if k & 2 else y, (1 - c) if k & 1 else c) for k in range(1, 8)]


def _pieces(rows):
    for n in (4, 2):
        if rows % (16 * n) == 0:
            return [(r * (rows // n), rows // n) for r in range(n)]
    return [(0, rows)]


def _peer_copies(src, land, send_sems, recv_sems, k, peer, mine, slab_src):
    block = src.at[_index(*peer)] if slab_src else src
    return [pltpu.make_async_remote_copy(src_ref=block.at[pl.ds(r0, nr)], dst_ref=land.at[mine, pl.ds(r0, nr)], send_sem=send_sems.at[k],
                                         recv_sem=recv_sems.at[k], device_id=peer, device_id_type=MESH)
            for r0, nr in _pieces(block.shape[0])]


def copies_start(name, srcs, slab_src, after):
    n = len(srcs)
    zones = [jax.ShapeDtypeStruct(s.shape if slab_src else (8,) + s.shape, s.dtype) for s in srcs]
    afters = [] if after is None else [after]

    def body(*refs):
        ins, lands = refs[:n], refs[n:2 * n]
        first = 2 * n + len(afters)
        sends, recvs = refs[first:first + n], refs[first + n:first + 2 * n]
        token = refs[-1]
        x, y, c = _place()
        mine = _index(x, y, c)
        for i in range(n):
            per_peer = [_peer_copies(ins[i], lands[i], sends[i], recvs[i], k, peer, mine, slab_src) for k, peer in enumerate(_peers(x, y, c))]
            for piece in zip(*per_peer):
                for cp in piece:
                    cp.start()
        token[...] = jnp.zeros_like(token)

    sem = pltpu.SemaphoreType.DMA((7,))
    res = pl.pallas_call(
        body, name=name,
        out_shape=([sem] * (2 * n) + [pltpu.HBM(s.shape, s.dtype) for s in srcs] + [pltpu.HBM(z.shape, z.dtype) for z in zones]
                   + [jax.ShapeDtypeStruct((8, 128), F32)]),
        in_specs=[HBM_SPEC] * (2 * n) + [pl.BlockSpec(memory_space=pl.ANY)] * len(afters),
        out_specs=[SEM_SPEC] * (2 * n) + [HBM_SPEC] * (2 * n) + [pl.BlockSpec(memory_space=pltpu.VMEM)],
        input_output_aliases={i: 2 * n + i for i in range(2 * n)},
        compiler_params=pltpu.CompilerParams(has_side_effects=EFFECT),
    )(*[pltpu.with_memory_space_constraint(s, pltpu.HBM) for s in srcs],
      *[pltpu.with_memory_space_constraint(lax.empty(z.shape, z.dtype), pltpu.HBM) for z in zones], *afters)
    return [(res[i], res[n + i], res[2 * n + i], res[3 * n + i]) for i in range(n)], res[-1][0:1, 0:1]


def copies_wait(name, started, slab_src, after):
    n = len(started)

    def body(*refs):
        ins, lands = refs[:n], refs[n:2 * n]
        sends, recvs = refs[2 * n:3 * n], refs[3 * n:4 * n]
        x, y, c = _place()
        mine = _index(x, y, c)
        for i in range(n):
            for k, peer in enumerate(_peers(x, y, c)):
                arrival = pltpu.make_async_remote_copy(src_ref=ins[i].at[mine] if slab_src else ins[i], dst_ref=lands[i].at[_index(*peer)],
                                                       send_sem=sends[i].at[k], recv_sem=recvs[i].at[k], device_id=peer, device_id_type=MESH)
                arrival.wait_send()
                arrival.wait_recv()

    srcs = [s[2] for s in started]
    lands = [s[3] for s in started]
    afters = list(after) if isinstance(after, (list, tuple)) else [after]
    res = pl.pallas_call(
        body, name=name,
        out_shape=[pltpu.HBM(s.shape, s.dtype) for s in srcs] + [pltpu.HBM(z.shape, z.dtype) for z in lands],
        in_specs=[HBM_SPEC] * (2 * n) + [SEM_SPEC] * (2 * n) + [pl.BlockSpec(memory_space=pl.ANY)] * len(afters),
        out_specs=[HBM_SPEC] * (2 * n),
        input_output_aliases={i: i for i in range(2 * n)},
        compiler_params=pltpu.CompilerParams(has_side_effects=EFFECT),
    )(*srcs, *lands, *[s[0] for s in started], *[s[1] for s in started], *afters)
    me = _index(*_place())
    own = [lax.dynamic_index_in_dim(s, me, 0, keepdims=True) if slab_src else s[None] for s in res[:n]]
    return [lax.dynamic_update_slice_in_dim(z, o, me, 0) for z, o in zip(res[n:], own)]


WEIGHTS = ["meta_tokens", "norm1_w", "w_in", "ssd_conv_w", "ssd_conv_b", "ssd_dt_bias", "ssd_a_log", "ssd_d", "ssd_norm_w", "lru_conv_w",
           "lru_conv_b", "lru_wa", "lru_ba", "lru_wx", "lru_bx", "lru_lambda", "lru_norm_w", "w_out", "norm2_w", "w_gate", "w_up", "w_down",
           "final_norm_w"]
BIG = ["w_in", "w_out", "w_gate", "w_up", "w_down"]
COLUMN_SHARDED = ["w_in", "w_gate", "w_up"]


def _pair_blocks(w):
    w = w.reshape(8, 2, 64, 64)
    z = jnp.zeros((8, 64, 64), w.dtype)
    return jnp.concatenate([jnp.concatenate([w[:, 0], z], axis=2), jnp.concatenate([z, w[:, 1]], axis=2)], axis=1)


def _unpair_blocks(w2):
    return jnp.stack([w2[:, :64, :64], w2[:, 64:, 64:]], axis=1).reshape(16, 64, 64)


def _per_group(v):
    return jnp.pad(v.reshape(2, 1, 8), ((0, 0), (0, 0), (0, 120)))


def _pad_cols(v, n):
    return jnp.pad(v, ((0, 0), (0, n - v.shape[1])))


def local_step(x, target, meta, ssd_cw, lru_cw, w_in, fetch, send, p):
    z120 = jnp.zeros((120, D), BF)
    w_dt = jnp.concatenate([w_in[2560:2568], z120, w_in[2568:2576], z120], axis=0)
    bias2, alog2, d2 = _per_group(p["ssd_dt_bias"]), _per_group(p["ssd_a_log"]), _per_group(p["ssd_d"])
    wa2 = _pair_blocks(p["lru_wa"]).astype(BF)
    wx2 = _pair_blocks(p["lru_wx"]).astype(BF)
    lru = (lru_cw, p["lru_conv_b"], wa2, p["lru_ba"], wx2, p["lru_bx"], p["lru_lambda"])

    h0 = jnp.concatenate([jnp.zeros((NPAD, D), F32), meta, x], axis=0)
    proj, dt_raw, u1 = in_proj(h0, p["norm1_w"], w_in, w_dt)
    yn_ssd, y_pre, h_prev = ssd_fwd(proj, dt_raw, ssd_cw, p["ssd_conv_b"], bias2, alog2, d2, p["ssd_norm_w"])
    hseq, a = lru_fwd(proj, *lru)
    (w_out,) = fetch(["w_out"], hseq)
    h1, cat = out_proj(yn_ssd, proj, hseq, p["lru_norm_w"], w_out, h0)
    w_gate, w_up = fetch(["w_gate", "w_up"], h1)
    gt, up, act, u2 = gate_up(h1, p["norm2_w"], w_gate, w_up)
    (w_down,) = fetch(["w_down"], act)
    dh2, dh2_b, loss, d_fnw = down_loss(act, w_down, h1, target, p["final_norm_w"])

    dgt, dup, g_down, g_gate, g_up = swiglu_bwd(dh2_b, w_down, gt, up, act, u2)
    sent = send({"w_down": g_down, "w_gate": g_gate, "w_up": g_up})
    dh1, dh1_b, d_n2 = gate_up_bwd(dgt, dup, w_gate, w_up, h1, p["norm2_w"] + sent, dh2)
    sent = send({"w_out": weight_grad("dw_out", cat, dh1_b)})
    dyn, dh_out, dg_b, d_lnw = out_proj_bwd(dh1_b, w_out, proj, hseq, p["lru_norm_w"] + sent)

    dxl_b, d_lcw, d_lcb, dwa2, d_ba, dwx2, d_bx, d_lam = lru_bwd(dh_out, a, hseq, proj, *lru)
    dz_b, dxbc_b, ddt_b, dpar, d_snw, d_scw, d_scb = ssd_bwd(dyn, proj, dt_raw, ssd_cw, p["ssd_conv_b"], y_pre, h_prev, bias2, alog2, d2,
                                                             p["ssd_norm_w"] + sent)
    sent = send({"w_in": in_weight_grad(dz_b, dg_b, dxl_b, dxbc_b, ddt_b, u1)})
    grad_x, d_meta, d_n1 = in_proj_bwd(dz_b, dg_b, dxl_b, dxbc_b, ddt_b, w_in, w_dt, h0, p["norm1_w"] + sent, dh1)
    small = {"norm1_w": d_n1, "ssd_conv_b": d_scb, "ssd_dt_bias": dpar[:, 0, :8].reshape(1, 16), "ssd_a_log": dpar[:, 1, :8].reshape(1, 16),
             "ssd_d": dpar[:, 2, :8].reshape(1, 16), "ssd_norm_w": d_snw, "lru_conv_b": d_lcb, "lru_ba": d_ba, "lru_bx": d_bx,
             "lru_lambda": d_lam, "lru_norm_w": d_lnw, "norm2_w": d_n2, "final_norm_w": d_fnw,
             "lru_wa": _unpair_blocks(dwa2), "lru_wx": _unpair_blocks(dwx2), "meta_tokens": d_meta,
             "ssd_conv_w": d_scw, "lru_conv_w": d_lcw}
    return loss, grad_x, small


def _pack_small(small, loss):
    rows = [_pad_cols(small[name], -(-n // 1024) * 1024).reshape(-1, 1024) for name, n in SIMPLE]
    rows += [small["lru_wa"].reshape(64, 1024), small["lru_wx"].reshape(64, 1024), small["meta_tokens"],
             _pad_cols(small["ssd_conv_w"], 2048).reshape(8, 1024), small["lru_conv_w"], _pad_cols(loss[:, 0:1], 1024)]
    sm = jnp.concatenate(rows, axis=0)
    return jnp.pad(sm, ((0, SM_ROWS - sm.shape[0]), (0, 0)))


def _slabs(g):
    return g.reshape(8, g.shape[0] // 8, g.shape[1])


def _unslab(g):
    return g.reshape(8 * g.shape[1], g.shape[2])


def kernel(x, meta_tokens, norm1_w, w_in, ssd_conv_w, ssd_conv_b, ssd_dt_bias, ssd_a_log, ssd_d, ssd_norm_w, lru_conv_w, lru_conv_b, lru_wa, lru_ba, lru_wx, lru_bx, lru_lambda, lru_norm_w, w_out, norm2_w, w_gate, w_up, w_down, final_norm_w, loss_target, m_meta_tokens, m_norm1_w, m_w_in, m_ssd_conv_w, m_ssd_conv_b, m_ssd_dt_bias, m_ssd_a_log, m_ssd_d, m_ssd_norm_w, m_lru_conv_w, m_lru_conv_b, m_lru_wa, m_lru_ba, m_lru_wx, m_lru_bx, m_lru_lambda, m_lru_norm_w, m_w_out, m_norm2_w, m_w_gate, m_w_up, m_w_down, m_final_norm_w, v_meta_tokens, v_norm1_w, v_w_in, v_ssd_conv_w, v_ssd_conv_b, v_ssd_dt_bias, v_ssd_a_log, v_ssd_d, v_ssd_norm_w, v_lru_conv_w, v_lru_conv_b, v_lru_wa, v_lru_ba, v_lru_wx, v_lru_bx, v_lru_lambda, v_lru_norm_w, v_w_out, v_norm2_w, v_w_gate, v_w_up, v_w_down, v_final_norm_w):
    w = dict(meta_tokens=meta_tokens, norm1_w=norm1_w, w_in=w_in[0], ssd_conv_w=ssd_conv_w[0], ssd_conv_b=ssd_conv_b, ssd_dt_bias=ssd_dt_bias,
             ssd_a_log=ssd_a_log, ssd_d=ssd_d, ssd_norm_w=ssd_norm_w, lru_conv_w=lru_conv_w[0], lru_conv_b=lru_conv_b, lru_wa=lru_wa[0],
             lru_ba=lru_ba, lru_wx=lru_wx[0], lru_bx=lru_bx, lru_lambda=lru_lambda, lru_norm_w=lru_norm_w, w_out=w_out[0], norm2_w=norm2_w,
             w_gate=w_gate[0], w_up=w_up[0], w_down=w_down[0], final_norm_w=final_norm_w.reshape(1, D))
    m = dict(meta_tokens=m_meta_tokens, norm1_w=m_norm1_w, w_in=m_w_in[0], ssd_conv_w=m_ssd_conv_w[0], ssd_conv_b=m_ssd_conv_b,
             ssd_dt_bias=m_ssd_dt_bias, ssd_a_log=m_ssd_a_log, ssd_d=m_ssd_d, ssd_norm_w=m_ssd_norm_w, lru_conv_w=m_lru_conv_w[0],
             lru_conv_b=m_lru_conv_b, lru_wa=m_lru_wa[0], lru_ba=m_lru_ba, lru_wx=m_lru_wx[0], lru_bx=m_lru_bx, lru_lambda=m_lru_lambda,
             lru_norm_w=m_lru_norm_w, w_out=m_w_out[0], norm2_w=m_norm2_w, w_gate=m_w_gate[0], w_up=m_w_up[0], w_down=m_w_down[0],
             final_norm_w=m_final_norm_w.reshape(1, D))
    v = dict(meta_tokens=v_meta_tokens, norm1_w=v_norm1_w, w_in=v_w_in[0], ssd_conv_w=v_ssd_conv_w[0], ssd_conv_b=v_ssd_conv_b,
             ssd_dt_bias=v_ssd_dt_bias, ssd_a_log=v_ssd_a_log, ssd_d=v_ssd_d, ssd_norm_w=v_ssd_norm_w, lru_conv_w=v_lru_conv_w[0],
             lru_conv_b=v_lru_conv_b, lru_wa=v_lru_wa[0], lru_ba=v_lru_ba, lru_wx=v_lru_wx[0], lru_bx=v_lru_bx, lru_lambda=v_lru_lambda,
             lru_norm_w=v_lru_norm_w, w_out=v_w_out[0], norm2_w=v_norm2_w, w_gate=v_w_gate[0], w_up=v_w_up[0], w_down=v_w_down[0],
             final_norm_w=v_final_norm_w.reshape(1, D))
    shapes = dict(meta_tokens=meta_tokens.shape, norm1_w=norm1_w.shape, w_in=w_in.shape, ssd_conv_w=ssd_conv_w.shape,
                  ssd_conv_b=ssd_conv_b.shape, ssd_dt_bias=ssd_dt_bias.shape, ssd_a_log=ssd_a_log.shape, ssd_d=ssd_d.shape,
                  ssd_norm_w=ssd_norm_w.shape, lru_conv_w=lru_conv_w.shape, lru_conv_b=lru_conv_b.shape, lru_wa=lru_wa.shape,
                  lru_ba=lru_ba.shape, lru_wx=lru_wx.shape, lru_bx=lru_bx.shape, lru_lambda=lru_lambda.shape, lru_norm_w=lru_norm_w.shape,
                  w_out=w_out.shape, norm2_w=norm2_w.shape, w_gate=w_gate.shape, w_up=w_up.shape, w_down=w_down.shape,
                  final_norm_w=final_norm_w.shape)
    me = _index(*_place())
    for n in COLUMN_SHARDED:
        w[n], m[n], v[n] = w[n].T, m[n].T, v[n].T

    small_shard = jnp.concatenate([w["meta_tokens"], _pad_cols(w["ssd_conv_w"], 256).reshape(8, 128), w["lru_conv_w"],
                                   jnp.zeros((4, 128), F32)], axis=0)
    g_in, gs = all_gather("gather_w_in", [w["w_in"].astype(BF), small_shard])
    later = ["w_out", "w_gate", "w_up", "w_down"]
    started, behind = copies_start("gather_rest_start", [w[n].astype(BF) for n in later], False, gs)
    started = dict(zip(later, started))
    meta_full = gs[:, 0:16].transpose(1, 0, 2).reshape(N_META, D)
    ssd_cw = gs[:, 16:24].reshape(8, 4, 256)[:, :, :192].transpose(1, 0, 2).reshape(4, XBC)
    lru_cw = gs[:, 24:28].transpose(1, 0, 2).reshape(4, LRU_W)

    def fetch(names, after):
        got = copies_wait("gather_" + names[0] + "_wait", [started[n] for n in names], False, after)
        return [_unslab(g) for g in got]

    in_flight = {}

    def send(grads):
        names = list(grads)
        st, token = copies_start("grads_" + names[0] + "_start", [grads[n] if n == "small" else _slabs(grads[n]) for n in names], True, None)
        in_flight.update(zip(names, st))
        return token

    loss, grad_x, small = local_step(x[0], loss_target[0], meta_full, ssd_cw, lru_cw, _unslab(g_in), fetch, send,
                                     {**w, "norm1_w": w["norm1_w"] + behind})
    send({"small": _pack_small(small, loss).reshape(8, SM_ROWS // 8, 1024)})

    out = {}
    early = ["w_down", "w_gate", "w_up", "w_out"]
    recv = dict(zip(early, copies_wait("grads_early_wait", [in_flight[n] for n in early], True, in_flight["small"][2])))
    for pair in (early[:2], early[2:]):
        done = adamw_shards("adamw_" + pair[0], [recv[n] for n in pair], [w[n] for n in pair], [m[n] for n in pair], [v[n] for n in pair])
        out.update(zip(pair, done))
    recv_in, recv_small = copies_wait("grads_late_wait", [in_flight["w_in"], in_flight["small"]], True, [out[n][0] for n in early])
    def lines(a):
        return jnp.transpose(a.reshape(D // 128, 128, IN_COLS // 8), (2, 0, 1))

    out["w_in"] = [jnp.transpose(o, (1, 2, 0)).reshape(D, IN_COLS // 8) for o in adamw_w_in(recv_in, lines(w_in), lines(m_w_in), lines(v_w_in))]
    for n in ("w_gate", "w_up"):
        out[n] = [o.T for o in out[n]]
    sm = all_gather("gather_small_grads", [sum_slabs(recv_small)])[0].reshape(SM_ROWS, 1024)
    special_g =[sm[SM_WA:SM_WA + 64].reshape(16, 64, 64), sm[SM_WX:SM_WX + 64].reshape(16, 64, 64),
                 lax.dynamic_slice(sm[SM_META:SM_META + 16], (0, 128 * me), (16, 128)),
                 lax.dynamic_slice(sm[SM_SCW:SM_SCW + 8].reshape(4, 2048), (0, 192 * me), (4, 192)),
                 lax.dynamic_slice(sm[SM_LCW:SM_LCW + 4], (0, 128 * me), (4, 128))]
    names = [n for n, _ in SIMPLE] + SPECIAL
    res = adamw_small(sm, special_g, [w[n] for n in names], [m[n] for n in names], [v[n] for n in names])
    for k, (n, _) in enumerate(SIMPLE):
        out[n] = res[4 * k:4 * k + 4]
    for k, n in enumerate(SPECIAL):
        o = 4 * len(SIMPLE) + 3 * k
        out[n] = [special_g[k]] + list(res[o:o + 3])
    loss_total = sm[SM_LOSS, 0]
    flat = [loss_total, grad_x[None]]
    for k in range(4):
        flat += [out[n][k].reshape(shapes[n]) for n in WEIGHTS]
    return tuple(flat)
```

```python
import math

import jax
import jax.numpy as jnp
from jax import lax
from jax.experimental import pallas as pl
from jax.experimental.pallas import tpu as pltpu

F32 = jnp.float32
BF = jnp.bfloat16

D = 1024
SEQ = 2048
N_META = 16
Q = 128
NPAD = 112
T = NPAD + N_META + SEQ
NCH = T // Q
RC = 544
D_FF = 2816
SSD_W = 1024
LRU_W = 1024
XBC = 1536
IN_COLS = 4624
PZ, PG, PXL, PXBC = 0, 1024, 2048, 3072
NP_IN = 4608
EPS = 1e-6
LRU_C = 8.0
VMEM_LIMIT = 56 * 1024 * 1024

ADAM_LR, ADAM_B1, ADAM_B2, ADAM_EPS, ADAM_WD, ADAM_STEP = 0.001, 0.9, 0.999, 1e-08, 0.01, 10

NT_DIMS = (((1,), (1,)), ((), ()))
TN_DIMS = (((0,), (0,)), ((), ()))
MESH = pl.DeviceIdType.MESH


def _params(n_grid=1, limit=VMEM_LIMIT):
    return pltpu.CompilerParams(dimension_semantics=("arbitrary",) * n_grid, vmem_limit_bytes=limit)


def _spec(shape, imap, single=False):
    if single:
        return pl.BlockSpec(shape, imap, pipeline_mode=pl.Buffered(1))
    return pl.BlockSpec(shape, imap)


def _sigmoid(x):
    return 0.5 * jnp.tanh(0.5 * x) + 0.5


def _sigmoid_gate(x):
    return 1.0 / (1.0 + jnp.exp(-x))


def _softplus(x):
    return jnp.maximum(x, 0.0) + jnp.log(1.0 + jnp.exp(-jnp.abs(x)))


def _rms_stats(h):
    return lax.rsqrt(jnp.mean(h * h, axis=-1, keepdims=True) + EPS)


def _rms(h, w):
    return (h * _rms_stats(h)) * w


def _rms_bwd(du, h, w):
    r = _rms_stats(h)
    n = h * r
    dn = du * w
    dh = r * (dn - n * jnp.mean(dn * n, axis=-1, keepdims=True))
    return dh, du * n


_G0 = math.sqrt(2.0 / math.pi)


def _gelu(x):
    return 0.5 * x * (1.0 + jnp.tanh(_G0 * (x + 0.044715 * (x * x * x))))


def _gelu_grad(x):
    t = jnp.tanh(_G0 * (x + 0.044715 * (x * x * x)))
    return 0.5 * (1.0 + t) + 0.5 * x * (1.0 - t * t) * (_G0 * (1.0 + 3.0 * 0.044715 * (x * x)))


def _rows(shape, r0=0):
    return lax.broadcasted_iota(jnp.int32, shape, 0) + r0


def _lanes(shape):
    return lax.broadcasted_iota(jnp.int32, shape, 1)


HALO = 8


def _fill_padded(pad_ref, x_ref):
    pad_ref[0:HALO, :] = jnp.zeros((HALO, pad_ref.shape[1]), F32)
    pad_ref[T + HALO:T + 2 * HALO, :] = jnp.zeros((HALO, pad_ref.shape[1]), F32)

    def step(c, carry):
        r0 = pl.multiple_of(c * Q, Q)
        pad_ref[pl.ds(r0 + HALO, Q), :] = x_ref[pl.ds(r0, Q), :].astype(F32)
        return carry

    lax.fori_loop(0, NCH, step, 0)


def _back(pad_ref, r0):
    win = pad_ref[pl.ds(r0, Q + HALO), :]
    return lambda s: win[HALO:, :] if s == 0 else pltpu.roll(win, s, axis=0)[HALO:, :]


def _ahead(pad_ref, r0):
    win = pad_ref[pl.ds(r0 + HALO, Q + HALO), :]
    return lambda s: win[:Q, :] if s == 0 else pltpu.roll(win, Q + HALO - s, axis=0)[:Q, :]


def _conv(back, w, b):
    y = b + w[3:4, :] * back(0)
    for k in range(3):
        y = y + w[k:k + 1, :] * back(3 - k)
    return y


def _conv_bwd_x(ahead, w):
    dx = w[3:4, :] * ahead(0)
    for k in range(3):
        dx = dx + w[k:k + 1, :] * ahead(3 - k)
    return dx


def _conv_bwd_w(dy, back):
    dws = [jnp.sum(dy * back(3 - k), axis=0, keepdims=True) for k in range(4)]
    return jnp.concatenate(dws, axis=0), jnp.sum(dy, axis=0, keepdims=True)


def _chunks(fn, unrolled=False):
    if unrolled:
        for c in range(NCH):
            fn(c * Q)
        return

    def step(c, carry):
        fn(pl.multiple_of(c * Q, Q))
        return carry

    lax.fori_loop(0, NCH, step, 0)


HALF = RC // 2


def _col_tiles(n, tn, fn):
    def step(j, carry):
        fn(pl.multiple_of(j * tn, tn))
        return carry

    lax.fori_loop(0, n // tn, step, 0)


def _rows_spec(cols, block_col=0):
    return _spec((RC, cols), lambda i: (i, block_col))


def _whole(shape):
    return _spec(shape, lambda i: tuple(0 for _ in shape), single=True)


def _vec(cols):
    return _spec((1, cols), lambda i: (0, 0))


def _zero_at_first(*refs):
    @pl.when(pl.program_id(0) == 0)
    def _():
        for r in refs:
            r[...] = jnp.zeros_like(r)


IN_RUNS = ((PZ, 0, 1024), (PG, 2576, 2048), (PXBC, 1024, XBC))


def _in_tiles(fn):
    for pcol, wrow, width in IN_RUNS:
        def step(j, carry, pcol=pcol, wrow=wrow):
            fn(pl.multiple_of(pcol + j * 512, 512), pl.multiple_of(wrow + j * 512, 16))
            return carry

        lax.fori_loop(0, width // 512, step, 0)


def in_proj(h0, wn, w_t, w_dt):
    def body(h_ref, wn_ref, w_ref, wdt_ref, o_ref, dt_ref, u_ref):
        for r in (0, HALF):
            u_ref[r:r + HALF, :] = _rms(h_ref[r:r + HALF, :], wn_ref[...]).astype(BF)

        def tile(pcol, wrow):
            o_ref[:, pl.ds(pcol, 512)] = lax.dot_general(u_ref[...], w_ref[pl.ds(wrow, 512), :], NT_DIMS, preferred_element_type=F32).astype(BF)

        _in_tiles(tile)
        dt_ref[...] = lax.dot_general(u_ref[...], wdt_ref[...], NT_DIMS, preferred_element_type=F32)

    return pl.pallas_call(
        body, grid=(T // RC,), in_specs=[_rows_spec(D), _vec(D), _whole((IN_COLS, D)), _whole((256, D))],
        out_specs=[_rows_spec(NP_IN), _rows_spec(256), _rows_spec(D)],
        out_shape=[jax.ShapeDtypeStruct((T, NP_IN), BF), jax.ShapeDtypeStruct((T, 256), F32), jax.ShapeDtypeStruct((T, D), BF)],
        compiler_params=_params(), name="in_proj")(h0, wn, w_t, w_dt)


def out_proj(yn_ssd, proj, hseq, lru_nw, w_out, h0):
    def body(y_ref, g_ref, h_ref, wn_ref, w_ref, r_ref, o_ref, cat_ref):
        cat_ref[:, 0:SSD_W] = y_ref[...]
        for r in (0, HALF):
            y = _gelu(g_ref[r:r + HALF, :].astype(F32)) * h_ref[r:r + HALF, :]
            cat_ref[r:r + HALF, SSD_W:] = _rms(y, wn_ref[...]).astype(BF)

        def tile(c0):
            o_ref[:, pl.ds(c0, 512)] = r_ref[:, pl.ds(c0, 512)] + jnp.dot(cat_ref[...], w_ref[:, pl.ds(c0, 512)], preferred_element_type=F32)

        _col_tiles(D, 512, tile)

    return pl.pallas_call(
        body, grid=(T // RC,),
        in_specs=[_rows_spec(SSD_W), _rows_spec(LRU_W, PG // LRU_W), _rows_spec(LRU_W), _vec(LRU_W), _whole((SSD_W + LRU_W, D)), _rows_spec(D)],
        out_specs=[_rows_spec(D), _rows_spec(SSD_W + LRU_W)],
        out_shape=[jax.ShapeDtypeStruct((T, D), F32), jax.ShapeDtypeStruct((T, SSD_W + LRU_W), BF)],
        compiler_params=_params(), name="out_proj")(yn_ssd, proj, hseq, lru_nw, w_out, h0)


def out_proj_bwd(dh1_b, w_out, proj, hseq, lru_nw):
    def body(d_ref, w_ref, g_ref, h_ref, wn_ref, dy_ref, dh_ref, dg_ref, dw_ref, dl_scr):
        _zero_at_first(dw_ref)

        def tile(c0):
            dy_ref[:, pl.ds(c0, 512)] = lax.dot_general(d_ref[...], w_ref[pl.ds(c0, 512), :], NT_DIMS, preferred_element_type=F32)
            dl_scr[:, pl.ds(c0, 512)] = lax.dot_general(d_ref[...], w_ref[pl.ds(SSD_W + c0, 512), :], NT_DIMS, preferred_element_type=F32)

        _col_tiles(SSD_W, 512, tile)

        for r in (0, HALF):
            g = g_ref[r:r + HALF, :].astype(F32)
            h = h_ref[r:r + HALF, :]
            ge = _gelu(g)
            dy, dw = _rms_bwd(dl_scr[r:r + HALF, :], ge * h, wn_ref[...])
            dw_ref[...] += jnp.sum(dw, axis=0, keepdims=True)
            dh_ref[r:r + HALF, :] = dy * ge
            dg_ref[r:r + HALF, :] = (dy * h * _gelu_grad(g)).astype(BF)

    return pl.pallas_call(
        body, grid=(T // RC,),
        in_specs=[_rows_spec(D), _whole((SSD_W + LRU_W, D)), _rows_spec(LRU_W, PG // LRU_W), _rows_spec(LRU_W), _vec(LRU_W)],
        out_specs=[_rows_spec(SSD_W), _rows_spec(LRU_W), _rows_spec(LRU_W), _vec(LRU_W)],
        out_shape=[jax.ShapeDtypeStruct((T, SSD_W), F32), jax.ShapeDtypeStruct((T, LRU_W), F32), jax.ShapeDtypeStruct((T, LRU_W), BF),
                   jax.ShapeDtypeStruct((1, LRU_W), F32)],
        scratch_shapes=[pltpu.VMEM((RC, LRU_W), F32)],
        compiler_params=_params(), name="out_proj_bwd")(dh1_b, w_out, proj, hseq, lru_nw)


def in_proj_bwd(dz, dg, dxl, dxbc, ddt, w_t, w_dt, h0, wn, dh1):
    first = NPAD + N_META

    def body(dz_ref, dg_ref, dxl_ref, dxbc_ref, ddt_ref, w_ref, wdt_ref, h_ref, wn_ref, r_ref, gx_hbm, meta_ref, dw_ref, du_scr, o_ref, sem):
        i = pl.program_id(0)
        _zero_at_first(dw_ref)
        du_scr[...] = jnp.dot(ddt_ref[...], wdt_ref[...], preferred_element_type=F32)
        for d_ref, wrow, width in ((dz_ref, 0, 1024), (dxbc_ref, 1024, XBC), (dg_ref, 2576, 1024), (dxl_ref, 3600, 1024)):
            def step(j, carry, d_ref=d_ref, wrow=wrow):
                c0 = pl.multiple_of(j * 512, 512)
                du_scr[...] += jnp.dot(d_ref[:, pl.ds(c0, 512)], w_ref[pl.ds(pl.multiple_of(wrow + c0, 16), 512), :], preferred_element_type=F32)
                return carry

            lax.fori_loop(0, width // 512, step, 0)
        for r in (0, HALF):
            dh, dw = _rms_bwd(du_scr[r:r + HALF, :], h_ref[r:r + HALF, :], wn_ref[...])
            dw_ref[...] += jnp.sum(dw, axis=0, keepdims=True)
            o_ref[r:r + HALF, :] = dh + r_ref[r:r + HALF, :]

        @pl.when(i == 0)
        def _():
            meta_ref[...] = o_ref[NPAD:first, :]
            head = pltpu.make_async_copy(o_ref.at[pl.ds(first, RC - first)], gx_hbm.at[pl.ds(0, RC - first)], sem)
            head.start()
            head.wait()

        @pl.when(i > 0)
        def _():
            rest = pltpu.make_async_copy(o_ref, gx_hbm.at[pl.ds(pl.multiple_of(i * RC - first, 32), RC)], sem)
            rest.start()
            rest.wait()

    return pl.pallas_call(
        body, grid=(T // RC,),
        in_specs=[_rows_spec(SSD_W), _rows_spec(LRU_W), _rows_spec(LRU_W), _rows_spec(XBC), _rows_spec(256), _whole((IN_COLS, D)),
                  _whole((256, D)), _rows_spec(D), _vec(D), _rows_spec(D)],
        out_specs=[pl.BlockSpec(memory_space=pl.ANY), _spec((N_META, D), lambda i: (0, 0)), _vec(D)],
        out_shape=[jax.ShapeDtypeStruct((SEQ, D), F32), jax.ShapeDtypeStruct((N_META, D), F32), jax.ShapeDtypeStruct((1, D), F32)],
        scratch_shapes=[pltpu.VMEM((RC, D), F32), pltpu.VMEM((RC, D), F32), pltpu.SemaphoreType.DMA],
        compiler_params=_params(), name="in_proj_bwd")(dz, dg, dxl, dxbc, ddt, w_t, w_dt, h0, wn, dh1)


GRAD_TILE = 256


def weight_grad(name, a, u1):
    tm = GRAD_TILE

    def body(a_ref, u_ref, o_ref):
        o_ref[...] = lax.dot_general(a_ref[...], u_ref[...], TN_DIMS, preferred_element_type=F32).astype(BF)

    return pl.pallas_call(
        body, grid=(a.shape[1] // tm,),
        in_specs=[_spec((T, tm), lambda j: (0, j)), _spec((T, D), lambda j: (0, 0), single=True)],
        out_specs=_spec((tm, D), lambda j: (j, 0)),
        out_shape=jax.ShapeDtypeStruct((a.shape[1], D), BF),
        compiler_params=_params(), name=name)(a, u1)


def in_weight_grad(dz, dg, dxl, dxbc, ddt, u1):
    tm = GRAD_TILE
    per_row = D // 128
    parts = (dz, dg, dxl, dxbc, ddt)
    first_rows = (0, 2576, 3600, 1024, 2560)
    tiles = [p.shape[1] // tm for p in parts]
    starts = [sum(tiles[:k]) for k in range(len(parts))]
    last = sum(tiles) - 1
    dt_lines = 8 * per_row

    def body(*refs):
        a_refs, u_ref, o_hbm, mix_scr, stage, sems = refs[:5], refs[5], refs[6], refs[7], refs[8], refs[9]
        step = pl.program_id(0)
        slot = step % 2
        line0 = 0
        for a_ref, start, n, first in zip(a_refs, starts, tiles, first_rows):
            here = (step >= start) & (step < start + n)
            line0 = jnp.where(here, per_row * (first + tm * (step - start)), line0)

            @pl.when(here)
            def _(a_ref=a_ref):
                res = lax.dot_general(a_ref[...], u_ref[...], TN_DIMS, preferred_element_type=F32)
                for q in range(per_row):
                    mix_scr[pl.ds(q, tm, stride=per_row), :] = res[:, 128 * q:128 * q + 128]

        def tile_copy(of_slot, to):
            return pltpu.make_async_copy(stage.at[of_slot], o_hbm.at[pl.ds(to, per_row * tm)], sems.at[of_slot])

        @pl.when(step >= 2)
        def _():
            tile_copy(slot, 0).wait()

        stage[slot] = mix_scr[...].astype(BF)

        @pl.when(step < last)
        def _():
            tile_copy(slot, pl.multiple_of(line0, 128)).start()

        @pl.when(step == last)
        def _():
            halves = [pltpu.make_async_copy(stage.at[slot, pl.ds(128 * per_row * k, dt_lines)],
                                            o_hbm.at[pl.ds(per_row * (first_rows[-1] + 8 * k), dt_lines)], sems.at[2 + k]) for k in range(2)]
            for cp in halves:
                cp.start()
            tile_copy(1 - slot, 0).wait()
            for cp in halves:
                cp.wait()

    def tile_of(start, n):
        return lambda j: (0, jnp.clip(j - start, 0, n - 1))

    return pl.pallas_call(
        body, grid=(last + 1,),
        in_specs=[_spec((T, tm), tile_of(s, n)) for s, n in zip(starts, tiles)] + [_spec((T, D), lambda j: (0, 0), single=True)],
        out_specs=pl.BlockSpec(memory_space=pl.ANY),
        out_shape=jax.ShapeDtypeStruct((per_row * IN_COLS, 128), BF),
        scratch_shapes=[pltpu.VMEM((per_row * tm, 128), F32), pltpu.VMEM((2, per_row * tm, 128), BF), pltpu.SemaphoreType.DMA((4,))],
        compiler_params=_params(), name="dw_in")(*parts, u1)


def _ssd_chunk_common(row0, dt_ref, b_ref, c_ref, bias, a_neg):
    shape = (Q, Q)
    lane = _lanes(shape)
    sub = _rows(shape)
    live = (_rows(shape, row0) >= NPAD) & (lane < 8)
    dtr = dt_ref[:, :]
    dt = jnp.where(live, _softplus(dtr + bias), 0.0)
    d_a = dt * a_neg
    tri = (sub >= lane).astype(F32)
    cs = jnp.dot(tri, d_a, precision=lax.Precision.HIGHEST, preferred_element_type=F32)
    cs_t = cs.T
    b_f = b_ref[:, :]
    bc = b_f.astype(BF)
    cc = c_ref[:, :].astype(BF)
    cb = lax.dot_general(cc, bc, NT_DIMS, preferred_element_type=F32)
    cs_last = cs[Q - 1:Q, :]
    return dict(lane=lane, sub=sub, live=live, dtr=dtr, dt=dt, cs=cs, cs_t=cs_t, bc=bc, cc=cc, cb=cb, bc_t=b_f.T.astype(BF),
                ecs=jnp.exp(cs), dsm=jnp.exp(cs_last - cs), gam=jnp.exp(cs_last))


def _pair(lane_even, mat, j):
    return jnp.where(lane_even, mat[:, j:j + 1], mat[:, j + 1:j + 2])


def _pair_row(lane_even, mat, j):
    return jnp.where(lane_even[0:1, :], mat[:, j:j + 1], mat[:, j + 1:j + 2])


def _head_decay(cm, j):
    seg = cm["cs"][:, j:j + 1] - cm["cs_t"][j:j + 1, :]
    return jnp.exp(jnp.where(cm["sub"] >= cm["lane"], seg, -jnp.inf))


def _head_decay_t(cm, j):
    seg = cm["cs_t"][j:j + 1, :] - cm["cs"][:, j:j + 1]
    return jnp.exp(jnp.where(cm["lane"] >= cm["sub"], seg, -jnp.inf))


def _conv_window(raw_ref, halo_ref, pad_scr):
    pad_scr[0:HALO, :] = halo_ref[...].astype(F32)[halo_ref.shape[0] - HALO:, :]
    pad_scr[HALO:HALO + Q, :] = raw_ref[...].astype(F32)
    win = pad_scr[...]
    return lambda s: win[HALO:, :] if s == 0 else pltpu.roll(win, s, axis=0)[HALO:, :]


def _xbc_cols(g):
    return slice(512 * g, 512 * g + 512), slice(SSD_W + 128 * g, SSD_W + 128 * g + 128), slice(SSD_W + 256 + 128 * g, SSD_W + 384 + 128 * g)


def ssd_fwd(proj, dt_raw, conv_w, conv_b, dt_bias2, a_log2, d2, norm_w):
    def body(raw_ref, halo_ref, dt_all, z_all, cw_ref, cb_ref, bias_all, alog_all, d_all, nw_all, yn_all, y_all, hp_all,
             h_all, pad_scr, act_scr):
        @pl.when(pl.program_id(0) == 0)
        def _():
            h_all[...] = jnp.zeros_like(h_all)

        pre = _conv(_conv_window(raw_ref, halo_ref, pad_scr), cw_ref[...], cb_ref[...])
        act_scr[...] = pre * _sigmoid(pre)
        for g in range(2):
            wide, thin = slice(512 * g, 512 * g + 512), slice(128 * g, 128 * g + 128)
            xs, bs, cs = _xbc_cols(g)
            group(act_scr.at[:, xs], act_scr.at[:, bs], act_scr.at[:, cs], dt_all.at[:, thin], z_all.at[:, wide], bias_all.at[g],
                  alog_all.at[g], d_all.at[g], nw_all.at[:, wide], yn_all.at[:, wide], y_all.at[:, wide], hp_all.at[g, 0], h_all.at[g])

    def group(x_ref, b_ref, c_ref, dt_ref, z_ref, bias_ref, alog_ref, d_ref, nw_ref, yn_ref, y_ref, hp_ref, h_scr):
        bias = bias_ref[...]
        a_neg = -jnp.exp(alog_ref[...])
        dsk = d_ref[...]
        cm = _ssd_chunk_common(pl.program_id(0) * Q, dt_ref, b_ref, c_ref, bias, a_neg)
        lane_even = cm["lane"] < 64
        for p in range(4):
            je, jo = 2 * p, 2 * p + 1
            xp = x_ref[:, 128 * p:128 * p + 128]
            xdt = xp * _pair(lane_even, cm["dt"], je)
            xdt_b = xdt.astype(BF)
            m_e = (cm["cb"] * _head_decay(cm, je)).astype(BF)
            m_o = (cm["cb"] * _head_decay(cm, jo)).astype(BF)
            zero = jnp.zeros_like(xdt_b)
            yd = (jnp.dot(m_e, jnp.where(lane_even, xdt_b, zero), preferred_element_type=F32)
                  + jnp.dot(m_o, jnp.where(lane_even, zero, xdt_b), preferred_element_type=F32))
            hp = h_scr[p]
            hp_ref[p] = hp
            yo = jnp.dot(cm["cc"], hp.astype(BF), preferred_element_type=F32) * _pair(lane_even, cm["ecs"], je)
            y_ref[:, 128 * p:128 * p + 128] = yd + yo + xp * _pair_row(lane_even, dsk, je)
            st = jnp.dot(cm["bc_t"], (xdt * _pair(lane_even, cm["dsm"], je)).astype(BF), preferred_element_type=F32)
            h_scr[p] = hp * _pair_row(lane_even, cm["gam"], je) + st
        zc = z_ref[:, :].astype(F32)
        gated = y_ref[:, :] * (zc * _sigmoid(zc))
        yn_ref[:, :] = _rms(gated, nw_ref[...]).astype(BF)

    par = _spec((2, 1, 128), lambda c: (0, 0, 0))
    wide = _spec((Q, SSD_W), lambda c: (c, 0))
    xbc = PXBC // XBC
    halo = 2 * HALO
    return pl.pallas_call(
        body, grid=(NCH,),
        in_specs=[_spec((Q, XBC), lambda c: (c, xbc)), _spec((halo, XBC), lambda c: (jnp.maximum(c * (Q // halo) - 1, 0), xbc)),
                  _spec((Q, 256), lambda c: (c, 0)), wide, _spec((4, XBC), lambda c: (0, 0)), _spec((1, XBC), lambda c: (0, 0)),
                  par, par, par, _spec((1, SSD_W), lambda c: (0, 0))],
        out_specs=[wide, wide, _spec((2, 1, 4, 128, 128), lambda c: (0, c, 0, 0, 0))],
        out_shape=[jax.ShapeDtypeStruct((T, SSD_W), BF), jax.ShapeDtypeStruct((T, SSD_W), F32),
                   jax.ShapeDtypeStruct((2, NCH, 4, 128, 128), F32)],
        scratch_shapes=[pltpu.VMEM((2, 4, 128, 128), F32), pltpu.VMEM((Q + HALO, XBC), F32), pltpu.VMEM((Q, XBC), F32)],
        compiler_params=_params(), name="ssd_fwd")(proj, proj, dt_raw, proj, conv_w, conv_b, dt_bias2, a_log2, d2, norm_w)


def ssd_bwd(dyn, proj, dt_raw, conv_w, conv_b, y_pre, h_prev, dt_bias2, a_log2, d2, norm_w):
    def body(dyn_all, raw_ref, halo_ref, dt_all, z_all, y_all, hp_all, cw_ref, cb_ref, bias_all, alog_all, d_all, nw_all,
             dz_all, dxbc_ref, ddt_all, dpar_all, dnw_all, dcw_ref, dcb_ref, dh_all, acc_all, pad_scr, act_scr, dsilu_scr, dact_scr, dpad_scr):
        @pl.when(pl.program_id(0) == 0)
        def _():
            dh_all[...] = jnp.zeros_like(dh_all)
            acc_all[...] = jnp.zeros_like(acc_all)
            dnw_all[...] = jnp.zeros_like(dnw_all)
            dcw_ref[...] = jnp.zeros_like(dcw_ref)
            dcb_ref[...] = jnp.zeros_like(dcb_ref)
            dpad_scr[Q:Q + HALO, :] = jnp.zeros((HALO, XBC), F32)

        back = _conv_window(raw_ref, halo_ref, pad_scr)
        pre = _conv(back, cw_ref[...], cb_ref[...])
        sg = _sigmoid(pre)
        act_scr[...] = pre * sg
        dsilu_scr[...] = sg * (1.0 + pre * (1.0 - sg))
        for g in range(2):
            wide, thin = slice(512 * g, 512 * g + 512), slice(128 * g, 128 * g + 128)
            xs, bs, cs = _xbc_cols(g)
            group(dyn_all.at[:, wide], act_scr.at[:, xs], act_scr.at[:, bs], act_scr.at[:, cs], dt_all.at[:, thin], z_all.at[:, wide],
                  y_all.at[:, wide], hp_all.at[g, 0], bias_all.at[g], alog_all.at[g], d_all.at[g], nw_all.at[:, wide],
                  dz_all.at[:, wide], dact_scr.at[:, xs], dact_scr.at[:, bs], dact_scr.at[:, cs], ddt_all.at[:, thin], dpar_all.at[g],
                  dnw_all.at[:, wide], dh_all.at[g], acc_all.at[g])
        dpre = dact_scr[...] * dsilu_scr[...]
        dcw, dcb = _conv_bwd_w(dpre, back)
        dcw_ref[...] += dcw
        dcb_ref[...] += dcb
        dpad_scr[0:Q, :] = dpre
        win = dpad_scr[...]
        dxbc_ref[...] = _conv_bwd_x(lambda s: win[:Q, :] if s == 0 else pltpu.roll(win, Q + HALO - s, axis=0)[:Q, :], cw_ref[...]).astype(BF)
        dpad_scr[Q:Q + HALO, :] = dpre[0:HALO, :]

    def group(dyn_ref, x_ref, b_ref, c_ref, dt_ref, z_ref, y_ref, hp_ref, bias_ref, alog_ref, d_ref, nw_ref,
              dz_ref, dx_ref, db_ref, dc_ref, ddt_ref, dpar_ref, dnw_ref, dh_scr, acc_scr):
        ci = pl.program_id(0)
        bias = bias_ref[...]
        a_neg = -jnp.exp(alog_ref[...])
        dsk = d_ref[...]
        cm = _ssd_chunk_common((NCH - 1 - ci) * Q, dt_ref, b_ref, c_ref, bias, a_neg)
        lane, sub = cm["lane"], cm["sub"]
        lane_even = lane < 64
        cc_t = c_ref[:, :].T.astype(BF)
        cb_t = lax.dot_general(cm["bc"], cm["cc"], NT_DIMS, preferred_element_type=F32)
        zc = z_ref[:, :].astype(F32)
        yc = y_ref[:, :]
        sg = _sigmoid(zc)
        sz = zc * sg
        dgated, dnw = _rms_bwd(dyn_ref[:, :], yc * sz, nw_ref[...])
        dnw_ref[...] += jnp.sum(dnw, axis=0, keepdims=True)
        dz_ref[:, :] = (dgated * yc * (sg * (1.0 + zc * (1.0 - sg)))).astype(BF)
        dy_all = dgated * sz
        dcb = jnp.zeros((Q, Q), F32)
        dcb_t = jnp.zeros((Q, Q), F32)
        db_acc = jnp.zeros((Q, Q), F32)
        dc_acc = jnp.zeros((Q, Q), F32)
        dcs = jnp.zeros((Q, Q), F32)
        ddt = jnp.zeros((Q, Q), F32)
        for p in range(4):
            je, jo = 2 * p, 2 * p + 1
            xp = x_ref[:, 128 * p:128 * p + 128]
            dy = dy_all[:, 128 * p:128 * p + 128]
            dt_p = _pair(lane_even, cm["dt"], je)
            xdt = xp * dt_p
            xdt_b = xdt.astype(BF)
            dy_b = dy.astype(BF)
            zero = jnp.zeros_like(dy_b)
            hp = hp_ref[p]
            hp_b = hp.astype(BF)
            dh = dh_scr[p]
            dh_b = dh.astype(BF)
            acc_scr[p:p + 1, :] += jnp.sum(dy * xp, axis=0, keepdims=True)
            dxp = dy * _pair_row(lane_even, dsk, je)
            e_p = _pair(lane_even, cm["ecs"], je)
            g_p = jnp.dot(cm["cc"], hp_b, preferred_element_type=F32)
            dg_b = (dy * e_p).astype(BF)
            de = dy * g_p * e_p
            dc_acc = dc_acc + lax.dot_general(dg_b, hp_b, NT_DIMS, preferred_element_type=F32)
            dh_in = jnp.dot(cc_t, dg_b, preferred_element_type=F32)
            ds_p = _pair(lane_even, cm["dsm"], je)
            r_p = jnp.dot(cm["bc"], dh_b, preferred_element_type=F32)
            dxdt = r_p * ds_p
            tt = r_p * xdt * ds_p
            db_acc = db_acc + lax.dot_general((xdt * ds_p).astype(BF), dh_b, NT_DIMS, preferred_element_type=F32)
            dgam_m = jnp.sum(dh * hp, axis=0, keepdims=True)
            for j, even in ((je, True), (jo, False)):
                sel = lane_even if even else jnp.logical_not(lane_even)
                dy_j = jnp.where(sel, dy_b, zero)
                l_j = _head_decay(cm, j)
                l_jt = _head_decay_t(cm, j)
                m_j = cm["cb"] * l_j
                m_jt = cb_t * l_jt
                dm = lax.dot_general(dy_j, xdt_b, NT_DIMS, preferred_element_type=F32)
                dm_t = lax.dot_general(xdt_b, dy_j, NT_DIMS, preferred_element_type=F32)
                dxdt = dxdt + jnp.dot(m_jt.astype(BF), dy_j, preferred_element_type=F32)
                dcb = dcb + dm * l_j
                dcb_t = dcb_t + dm_t * l_jt
                t_j = jnp.where(sel, tt, 0.0)
                col = jnp.sum(dm * m_j - dm_t * m_jt + (jnp.where(sel, de, 0.0) - t_j), axis=1, keepdims=True)
                gam_j = cm["gam"][:, j:j + 1]
                last = (jnp.sum(jnp.sum(t_j, axis=0, keepdims=True), axis=1, keepdims=True)
                        + jnp.sum(jnp.where(sel[0:1, :], dgam_m, 0.0), axis=1, keepdims=True) * gam_j)
                col = col + jnp.where(sub[:, 0:1] == Q - 1, last, 0.0)
                dcs = dcs + jnp.where(lane == j, col, 0.0)
            dh_scr[p] = dh_in + dh * _pair_row(lane_even, cm["gam"], je)
            dx_ref[:, 128 * p:128 * p + 128] = dxp + dxdt * dt_p
            dd = dxdt * xp
            ddt = ddt + jnp.where(lane == je, jnp.sum(jnp.where(lane_even, dd, 0.0), axis=1, keepdims=True), 0.0)
            ddt = ddt + jnp.where(lane == jo, jnp.sum(jnp.where(lane_even, 0.0, dd), axis=1, keepdims=True), 0.0)
        dc_ref[:, :] = dc_acc + jnp.dot(dcb.astype(BF), cm["bc"], preferred_element_type=F32)
        db_ref[:, :] = db_acc + jnp.dot(dcb_t.astype(BF), cm["cc"], preferred_element_type=F32)
        tri_t = (sub <= lane).astype(F32)
        dd_a = jnp.dot(tri_t, dcs, precision=lax.Precision.HIGHEST, preferred_element_type=F32)
        ddt = ddt + dd_a * a_neg
        acc_scr[5:6, :] += jnp.sum(dd_a * cm["dt"], axis=0, keepdims=True)
        draw = jnp.where(cm["live"], ddt * _sigmoid_gate(cm["dtr"] + bias), 0.0)
        acc_scr[4:5, :] += jnp.sum(draw, axis=0, keepdims=True)
        ddt_ref[:, :] = draw.astype(BF)

        @pl.when(ci == NCH - 1)
        def _():
            lane1 = _lanes((1, 128))
            dd = jnp.zeros((1, 128), F32)
            for p in range(4):
                row = acc_scr[p:p + 1, :]
                dd = dd + jnp.where(lane1 == 2 * p, jnp.sum(jnp.where(lane1 < 64, row, 0.0), axis=1, keepdims=True), 0.0)
                dd = dd + jnp.where(lane1 == 2 * p + 1, jnp.sum(jnp.where(lane1 < 64, 0.0, row), axis=1, keepdims=True), 0.0)
            dpar_ref[...] = jnp.concatenate([acc_scr[4:5, :], acc_scr[5:6, :] * a_neg, dd, jnp.zeros((5, 128), F32)], axis=0)

    par = _spec((2, 1, 128), lambda c: (0, 0, 0))
    wide = _spec((Q, SSD_W), lambda c: (NCH - 1 - c, 0))
    thin = _spec((Q, 256), lambda c: (NCH - 1 - c, 0))
    vec = _spec((1, SSD_W), lambda c: (0, 0))
    xbc = PXBC // XBC
    halo = 2 * HALO
    chunk = pltpu.VMEM((Q, XBC), F32)
    padded = pltpu.VMEM((Q + HALO, XBC), F32)
    return pl.pallas_call(
        body, grid=(NCH,),
        in_specs=[wide, _spec((Q, XBC), lambda c: (NCH - 1 - c, xbc)),
                  _spec((halo, XBC), lambda c: (jnp.maximum((NCH - 1 - c) * (Q // halo) - 1, 0), xbc)), thin, wide, wide,
                  _spec((2, 1, 4, 128, 128), lambda c: (0, NCH - 1 - c, 0, 0, 0)), _spec((4, XBC), lambda c: (0, 0)),
                  _spec((1, XBC), lambda c: (0, 0)), par, par, par, vec],
        out_specs=[wide, _spec((Q, XBC), lambda c: (NCH - 1 - c, 0)), thin, _spec((2, 8, 128), lambda c: (0, 0, 0)), vec,
                   _spec((4, XBC), lambda c: (0, 0)), _spec((1, XBC), lambda c: (0, 0))],
        out_shape=[jax.ShapeDtypeStruct((T, SSD_W), BF), jax.ShapeDtypeStruct((T, XBC), BF), jax.ShapeDtypeStruct((T, 256), BF),
                   jax.ShapeDtypeStruct((2, 8, 128), F32), jax.ShapeDtypeStruct((1, SSD_W), F32), jax.ShapeDtypeStruct((4, XBC), F32),
                   jax.ShapeDtypeStruct((1, XBC), F32)],
        scratch_shapes=[pltpu.VMEM((2, 4, 128, 128), F32), pltpu.VMEM((2, 8, 128), F32), padded, chunk, chunk, chunk, padded],
        compiler_params=_params(), name="ssd_bwd")(dyn, proj, proj, dt_raw, proj, y_pre, h_prev, conv_w, conv_b, dt_bias2, a_log2, d2, norm_w)


def _lru_gates(back, cw, cb, wa, ba, wx, bx, lam):
    xr = _conv(back, cw, cb)
    xr_b = xr.astype(BF)
    r = _sigmoid_gate(jnp.dot(xr_b, wa, preferred_element_type=F32) + ba)
    i = _sigmoid_gate(jnp.dot(xr_b, wx, preferred_element_type=F32) + bx)
    sp = _softplus(-lam)
    la = (-LRU_C) * r * sp
    a = jnp.exp(la)
    mult2 = -jnp.tanh(la) * (a * a + 1.0)
    return xr, xr_b, r, i, sp, a, jnp.sqrt(mult2), mult2


SEG_LEN = 68
SEGS = T // SEG_LEN


def _seg_rows(j, k, off=0):
    return pl.ds(off + j * 8 * SEG_LEN + k, 8, stride=SEG_LEN)


def _segmented_scan(mul_ref, mul_row0, add_ref, out_ref, loc_scr, prod_scr, carry_scr, reverse):
    groups = SEGS // 8
    off = mul_row0 + (1 if reverse else 0)

    def local(i, carry):
        k = SEG_LEN - 1 - i if reverse else i
        new = []
        for j in range(groups):
            h, p = carry[2 * j], carry[2 * j + 1]
            m = mul_ref[_seg_rows(j, k, off), :]
            h = m * h + add_ref[_seg_rows(j, k), :]
            p = m * p
            loc_scr[_seg_rows(j, k), :] = h
            prod_scr[_seg_rows(j, k), :] = p
            new += [h, p]
        return tuple(new)

    lax.fori_loop(0, SEG_LEN, local, (jnp.zeros((8, 128), F32), jnp.ones((8, 128), F32)) * groups)

    def chain(i, c):
        s = SEGS - 1 - i if reverse else i
        carry_scr[pl.ds(s, 1), :] = c
        edge = s * SEG_LEN + (0 if reverse else SEG_LEN - 1)
        return loc_scr[pl.ds(edge, 1), :] + prod_scr[pl.ds(edge, 1), :] * c

    lax.fori_loop(0, SEGS, chain, jnp.zeros((1, 128), F32))

    def fold(k, carry):
        for j in range(groups):
            rows = _seg_rows(j, k)
            out_ref[rows, :] = loc_scr[rows, :] + prod_scr[rows, :] * carry_scr[8 * j:8 * j + 8, :]
        return carry

    lax.fori_loop(0, SEG_LEN, fold, 0)


def lru_fwd(proj, cw, cb, wa2, ba, wx2, bx, lam):
    def body(x_ref, cw_ref, cb_ref, wa_ref, ba_ref, wx_ref, bx_ref, lam_ref, h_ref, a_ref, xpad, u_scr, loc_scr, prod_scr, carry_scr):
        _fill_padded(xpad, x_ref)

        def chunk(r0):
            xr, _, _, i, _, a, mult, _ = _lru_gates(_back(xpad, r0), cw_ref[...], cb_ref[...], wa_ref[0], ba_ref[...], wx_ref[0], bx_ref[...],
                                                 lam_ref[...])
            a_ref[pl.ds(r0, Q), :] = a
            u_scr[pl.ds(r0, Q), :] = jnp.where(_rows(a.shape, r0) >= NPAD, mult * (i * xr), 0.0)

        _chunks(chunk, unrolled=True)
        _segmented_scan(a_ref, 0, u_scr, h_ref, loc_scr, prod_scr, carry_scr, reverse=False)

    c0 = PXL // 128
    vec = _spec((1, 128), lambda c: (0, c))
    mat = _spec((1, 128, 128), lambda c: (c, 0, 0))
    seq = pltpu.VMEM((T, 128), F32)
    return pl.pallas_call(
        body, grid=(8,),
        in_specs=[_spec((T, 128), lambda c: (0, c0 + c)), _spec((4, 128), lambda c: (0, c)), vec, mat, vec, mat, vec, vec],
        out_specs=[_spec((T, 128), lambda c: (0, c)), _spec((T, 128), lambda c: (0, c))],
        out_shape=[jax.ShapeDtypeStruct((T, LRU_W), F32), jax.ShapeDtypeStruct((T, LRU_W), F32)],
        scratch_shapes=[pltpu.VMEM((T + 2 * HALO, 128), F32), seq, seq, seq, pltpu.VMEM((SEGS, 128), F32)],
        compiler_params=_params(), name="lru_fwd")(proj, cw, cb, wa2, ba, wx2, bx, lam)


def lru_bwd(dh_out, a, hseq, proj, cw, cb, wa2, ba, wx2, bx, lam):
    def body(d_ref, a_ref, h_ref, x_ref, cw_ref, cb_ref, wa_ref, ba_ref, wx_ref, bx_ref, lam_ref,
             dx_ref, dcw_ref, dcb_ref, dwa_ref, dba_ref, dwx_ref, dbx_ref, dlam_ref, xpad, hpad, dpad, dh_ref, loc_scr, prod_scr, carry_scr):
        _fill_padded(dpad, a_ref)
        _segmented_scan(dpad, HALO, d_ref, dh_ref, loc_scr, prod_scr, carry_scr, reverse=True)
        _fill_padded(xpad, x_ref)
        _fill_padded(hpad, h_ref)
        dpad[0:HALO, :] = jnp.zeros((HALO, 128), F32)
        dpad[T + HALO:T + 2 * HALO, :] = jnp.zeros((HALO, 128), F32)
        for ref in (dcw_ref, dcb_ref, dwa_ref, dba_ref, dwx_ref, dbx_ref, dlam_ref):
            ref[...] = jnp.zeros_like(ref)
        lam = lam_ref[...]

        def first(r0):
            back = _back(xpad, r0)
            xr, xr_b, r, i, sp, a, mult, mult2 = _lru_gates(back, cw_ref[...], cb_ref[...], wa_ref[0], ba_ref[...], wx_ref[0], bx_ref[...], lam)
            dh = dh_ref[pl.ds(r0, Q), :]
            da = dh * _back(hpad, r0)(1)
            du = jnp.where(_rows(dh.shape, r0) >= NPAD, dh, 0.0)
            dmult = du * (i * xr)
            di = du * (mult * xr)
            dxr = du * (mult * i)
            dla = da * a - dmult * (a * a) * lax.rsqrt(mult2)
            dr = dla * ((-LRU_C) * sp)
            dlam_ref[...] += jnp.sum(dla * ((-LRU_C) * r), axis=0, keepdims=True)
            dpr = dr * r * (1.0 - r)
            dpi = di * i * (1.0 - i)
            dba_ref[...] += jnp.sum(dpr, axis=0, keepdims=True)
            dbx_ref[...] += jnp.sum(dpi, axis=0, keepdims=True)
            dpr_b = dpr.astype(BF)
            dpi_b = dpi.astype(BF)
            dxr = (dxr + lax.dot_general(dpr_b, wa_ref[0], NT_DIMS, preferred_element_type=F32)
                   + lax.dot_general(dpi_b, wx_ref[0], NT_DIMS, preferred_element_type=F32))
            dwa_ref[0] += lax.dot_general(xr_b, dpr_b, TN_DIMS, preferred_element_type=F32)
            dwx_ref[0] += lax.dot_general(xr_b, dpi_b, TN_DIMS, preferred_element_type=F32)
            dpad[pl.ds(r0 + HALO, Q), :] = dxr
            dcw, dcb = _conv_bwd_w(dxr, back)
            dcw_ref[...] += dcw
            dcb_ref[...] += dcb

        _chunks(first, unrolled=True)
        dlam_ref[...] = -dlam_ref[...] * _sigmoid_gate(-lam)

        def second(r0):
            dx_ref[pl.ds(r0, Q), :] = _conv_bwd_x(_ahead(dpad, r0), cw_ref[...]).astype(BF)

        _chunks(second)

    c0 = PXL // 128
    vec = _spec((1, 128), lambda c: (0, c))
    mat = _spec((1, 128, 128), lambda c: (c, 0, 0))
    col = _spec((T, 128), lambda c: (0, c))
    vshape = jax.ShapeDtypeStruct((1, LRU_W), F32)
    mshape = jax.ShapeDtypeStruct((8, 128, 128), F32)
    pad = pltpu.VMEM((T + 2 * HALO, 128), F32)
    seq = pltpu.VMEM((T, 128), F32)
    return pl.pallas_call(
        body, grid=(8,),
        in_specs=[col, col, col, _spec((T, 128), lambda c: (0, c0 + c)), _spec((4, 128), lambda c: (0, c)), vec, mat, vec, mat, vec, vec],
        out_specs=[col, _spec((4, 128), lambda c: (0, c)), vec, mat, vec, mat, vec, vec],
        out_shape=[jax.ShapeDtypeStruct((T, LRU_W), BF), jax.ShapeDtypeStruct((4, LRU_W), F32), vshape, mshape, vshape, mshape, vshape, vshape],
        scratch_shapes=[pad, pad, pad, seq, seq, seq, pltpu.VMEM((SEGS, 128), F32)],
        compiler_params=_params(), name="lru_bwd")(dh_out, a, hseq, proj, cw, cb, wa2, ba, wx2, bx, lam)


def gate_up(h1, wn, w_gate, w_up):
    def body(h_ref, wn_ref, wg_ref, wu_ref, gt_ref, up_ref, act_ref, u_ref):
        for r in (0, HALF):
            u_ref[r:r + HALF, :] = _rms(h_ref[r:r + HALF, :], wn_ref[...]).astype(BF)

        def tile(c0):
            cols = pl.ds(c0, 256)
            gt = lax.dot_general(u_ref[...], wg_ref[cols, :], NT_DIMS, preferred_element_type=F32)
            up = lax.dot_general(u_ref[...], wu_ref[cols, :], NT_DIMS, preferred_element_type=F32)
            gt_ref[:, cols] = gt.astype(BF)
            up_ref[:, cols] = up.astype(BF)
            act_ref[:, cols] = (gt * _sigmoid(gt) * up).astype(BF)

        _col_tiles(D_FF, 256, tile)

    big = jax.ShapeDtypeStruct((T, D_FF), BF)
    return pl.pallas_call(
        body, grid=(T // RC,), in_specs=[_rows_spec(D), _vec(D), _whole((D_FF, D)), _whole((D_FF, D))],
        out_specs=[_rows_spec(D_FF), _rows_spec(D_FF), _rows_spec(D_FF), _rows_spec(D)],
        out_shape=[big, big, big, jax.ShapeDtypeStruct((T, D), BF)],
        compiler_params=_params(), name="gate_up")(h1, wn, w_gate, w_up)


def down_loss(act, w_down, h1, target, wf):
    first = NPAD + N_META

    def body(a_ref, w_ref, r_ref, t_hbm, wf_ref, d_ref, db_ref, l_ref, dw_ref, h_scr, t_ref, t_sem):
        i = pl.program_id(0)
        _zero_at_first(l_ref, dw_ref)
        head = pltpu.make_async_copy(t_hbm.at[pl.ds(0, RC - first)], t_ref.at[pl.ds(first, RC - first)], t_sem)
        rest = pltpu.make_async_copy(t_hbm.at[pl.ds(pl.multiple_of(jnp.maximum(i * RC - first, 0), 32), RC)], t_ref, t_sem)

        @pl.when(i == 0)
        def _():
            t_ref[0:first, :] = jnp.zeros((first, D), F32)
            head.start()

        @pl.when(i > 0)
        def _():
            rest.start()

        def tile(c0):
            cols = pl.ds(c0, 512)
            h_scr[:, cols] = r_ref[:, cols] + jnp.dot(a_ref[...], w_ref[:, cols], preferred_element_type=F32)

        _col_tiles(D, 512, tile)

        @pl.when(i == 0)
        def _():
            head.wait()

        @pl.when(i > 0)
        def _():
            rest.wait()

        for r in (0, HALF):
            h = h_scr[r:r + HALF, :]
            live = _rows((HALF, D), i * RC + r) >= first
            err = jnp.where(live, _rms(h, wf_ref[...]) - t_ref[r:r + HALF, :], 0.0)
            l_ref[...] += 0.5 * jnp.sum(jnp.sum(err * err, axis=1, keepdims=True) * (1.0 / D), axis=0, keepdims=True)
            dh, dw = _rms_bwd(err * (1.0 / D), h, wf_ref[...])
            dw_ref[...] += jnp.sum(dw, axis=0, keepdims=True)
            d_ref[r:r + HALF, :] = dh
            db_ref[r:r + HALF, :] = dh.astype(BF)

    return pl.pallas_call(
        body, grid=(T // RC,),
        in_specs=[_rows_spec(D_FF), _whole((D_FF, D)), _rows_spec(D), pl.BlockSpec(memory_space=pl.ANY), _vec(D)],
        out_specs=[_rows_spec(D), _rows_spec(D), _spec((1, 128), lambda i: (0, 0)), _vec(D)],
        out_shape=[jax.ShapeDtypeStruct((T, D), F32), jax.ShapeDtypeStruct((T, D), BF), jax.ShapeDtypeStruct((1, 128), F32),
                   jax.ShapeDtypeStruct((1, D), F32)],
        scratch_shapes=[pltpu.VMEM((RC, D), F32), pltpu.VMEM((RC, D), F32), pltpu.SemaphoreType.DMA],
        compiler_params=_params(), name="down_loss")(act, w_down, h1, target, wf)


def swiglu_bwd(dh2_b, w_down, gt, up, act, u2):
    tn = 256

    def body(d_ref, u_ref, w_ref, gt_ref, up_ref, act_ref, dg_ref, du_ref, gd_ref, gg_ref, gu_ref):
        def rows(r0):
            part = pl.ds(r0, RC)
            dact = lax.dot_general(d_ref[part, :], w_ref[...], NT_DIMS, preferred_element_type=F32)
            gt_ = gt_ref[part, :].astype(F32)
            up_ = up_ref[part, :].astype(F32)
            sg = _sigmoid(gt_)
            dg_ref[part, :] = (dact * up_ * (sg * (1.0 + gt_ * (1.0 - sg)))).astype(BF)
            du_ref[part, :] = (dact * (gt_ * sg)).astype(BF)

        _col_tiles(T, RC, rows)
        gd_ref[...] = lax.dot_general(act_ref[...], d_ref[...], TN_DIMS, preferred_element_type=F32).astype(BF)
        gg_ref[...] = lax.dot_general(dg_ref[...], u_ref[...], TN_DIMS, preferred_element_type=F32).astype(BF)
        gu_ref[...] = lax.dot_general(du_ref[...], u_ref[...], TN_DIMS, preferred_element_type=F32).astype(BF)

    resident = _spec((T, D), lambda j: (0, 0), single=True)
    cols = _spec((T, tn), lambda j: (0, j))
    wrow = _spec((tn, D), lambda j: (j, 0))
    big = jax.ShapeDtypeStruct((T, D_FF), BF)
    grad = jax.ShapeDtypeStruct((D_FF, D), BF)
    return pl.pallas_call(
        body, grid=(D_FF // tn,), in_specs=[resident, resident, wrow, cols, cols, cols],
        out_specs=[cols, cols, wrow, wrow, wrow], out_shape=[big, big, grad, grad, grad],
        compiler_params=_params(), name="swiglu_bwd")(dh2_b, u2, w_down, gt, up, act)


def gate_up_bwd(dgt, dup, w_gate, w_up, h1, wn, dh2):
    def body(dg_ref, du_ref, wg_ref, wu_ref, h_ref, wn_ref, r_ref, d_ref, db_ref, dw_ref, du_scr):
        _zero_at_first(dw_ref)

        du_scr[...] = jnp.zeros_like(du_scr)

        def tile(c0):
            k = pl.ds(c0, 256)
            du_scr[...] += (jnp.dot(dg_ref[:, k], wg_ref[k, :], preferred_element_type=F32)
                            + jnp.dot(du_ref[:, k], wu_ref[k, :], preferred_element_type=F32))

        _col_tiles(D_FF, 256, tile)
        for r in (0, HALF):
            dh, dw = _rms_bwd(du_scr[r:r + HALF, :], h_ref[r:r + HALF, :], wn_ref[...])
            dw_ref[...] += jnp.sum(dw, axis=0, keepdims=True)
            dh = dh + r_ref[r:r + HALF, :]
            d_ref[r:r + HALF, :] = dh
            db_ref[r:r + HALF, :] = dh.astype(BF)

    return pl.pallas_call(
        body, grid=(T // RC,),
        in_specs=[_rows_spec(D_FF), _rows_spec(D_FF), _whole((D_FF, D)), _whole((D_FF, D)), _rows_spec(D), _vec(D), _rows_spec(D)],
        out_specs=[_rows_spec(D), _rows_spec(D), _vec(D)],
        out_shape=[jax.ShapeDtypeStruct((T, D), F32), jax.ShapeDtypeStruct((T, D), BF), jax.ShapeDtypeStruct((1, D), F32)],
        scratch_shapes=[pltpu.VMEM((RC, D), F32)],
        compiler_params=_params(), name="gate_up_bwd")(dgt, dup, w_gate, w_up, h1, wn, dh2)


def _adamw(w, g, m, v):
    m = ADAM_B1 * m + (1.0 - ADAM_B1) * g
    v = ADAM_B2 * v + (1.0 - ADAM_B2) * (g * g)
    m_hat = m / (1.0 - ADAM_B1 ** ADAM_STEP)
    v_hat = v / (1.0 - ADAM_B2 ** ADAM_STEP)
    delta = -ADAM_LR * (m_hat / (jnp.sqrt(v_hat) + ADAM_EPS) + ADAM_WD * w)
    return delta, m, v


def adamw_shards(name, recvs, ws, ms, vs):
    n = len(ws)

    def body(*refs):
        ins, outs = refs[:4 * n], refs[4 * n:]
        for k in range(n):
            p_ref, w_ref, m_ref, v_ref = ins[k], ins[n + k], ins[2 * n + k], ins[3 * n + k]
            g = p_ref[0].astype(F32)
            for s in range(1, 8):
                g = g + p_ref[s].astype(F32)
            outs[4 * k][...] = g
            outs[4 * k + 1][...], outs[4 * k + 2][...], outs[4 * k + 3][...] = _adamw(w_ref[...], g, m_ref[...], v_ref[...])

    tiles = [_spec((w.shape[0] // 2, w.shape[1]), lambda i: (i, 0)) for w in ws]
    recv_tiles = [_spec((8, w.shape[0] // 2, w.shape[1]), lambda i: (0, i, 0)) for w in ws]
    res = pl.pallas_call(
        body, grid=(2,), in_specs=recv_tiles + tiles * 3,
        out_specs=[t for t in tiles for _ in range(4)],
        out_shape=[jax.ShapeDtypeStruct(w.shape, F32) for w in ws for _ in range(4)],
        compiler_params=_params(), name=name)(*recvs, *ws, *ms, *vs)
    return [list(res[4 * k:4 * k + 4]) for k in range(n)]


def adamw_w_in(recv, w, m, v):
    rows = 34
    per_row = D // 128

    def body(p_ref, w_ref, m_ref, v_ref, g_ref, d_ref, mo_ref, vo_ref):
        def chunk(c, carry):
            lines = pl.ds(pl.multiple_of(c * per_row * rows, 16), per_row * rows)
            g = p_ref[0, lines, :].astype(F32)
            for s in range(1, 8):
                g = g + p_ref[s, lines, :].astype(F32)
            g = g.reshape(rows, per_row, 128)
            part = pl.ds(c * rows, rows)
            g_ref[part] = g
            d_ref[part], mo_ref[part], vo_ref[part] = _adamw(w_ref[part], g, m_ref[part], v_ref[part])
            return carry

        lax.fori_loop(0, w.shape[0] // rows, chunk, 0)

    shape = jax.ShapeDtypeStruct(w.shape, F32)
    return pl.pallas_call(body, out_shape=[shape] * 4, compiler_params=_params(0), name="adamw_w_in")(recv, w, m, v)


def sum_slabs(recv):
    def body(p_ref, o_ref):
        g = p_ref[0]
        for s in range(1, 8):
            g = g + p_ref[s]
        o_ref[...] = g

    return pl.pallas_call(body, out_shape=jax.ShapeDtypeStruct(recv.shape[1:], F32), compiler_params=_params(0), name="sum_slabs")(recv)


SIMPLE = [("norm1_w", 1024), ("ssd_conv_b", 1536), ("ssd_dt_bias", 16), ("ssd_a_log", 16), ("ssd_d", 16), ("ssd_norm_w", 1024),
          ("lru_conv_b", 1024), ("lru_ba", 1024), ("lru_bx", 1024), ("lru_lambda", 1024), ("lru_norm_w", 1024), ("norm2_w", 1024),
          ("final_norm_w", 1024)]
SPECIAL = ["lru_wa", "lru_wx", "meta_tokens", "ssd_conv_w", "lru_conv_w"]
SM_ROWS = 176
SM_WA, SM_WX, SM_META, SM_SCW, SM_LCW, SM_LOSS = 14, 78, 142, 158, 166, 170


def _simple_rows():
    rows, r = {}, 0
    for name, n in SIMPLE:
        rows[name] = r
        r += -(-n // 1024)
    return rows


def adamw_small(sm, special_g, ws, ms, vs):
    rows = _simple_rows()
    ns, nx = len(SIMPLE), len(SPECIAL)

    def body(*refs):
        sm_ref = refs[0]
        gx = refs[1:1 + nx]
        wr = refs[1 + nx:1 + nx + ns + nx]
        mr = refs[1 + nx + ns + nx:1 + nx + 2 * (ns + nx)]
        vr = refs[1 + nx + 2 * (ns + nx):1 + nx + 3 * (ns + nx)]
        outs = refs[1 + nx + 3 * (ns + nx):]
        o = 0
        for k, (name, n) in enumerate(SIMPLE):
            r0 = rows[name]
            for c0 in range(0, n, 1024):
                wd = min(1024, n - c0)
                g = sm_ref[r0 + c0 // 1024:r0 + c0 // 1024 + 1, 0:wd]
                sl = (slice(None), slice(c0, c0 + wd))
                d, m2, v2 = _adamw(wr[k][sl], g, mr[k][sl], vr[k][sl])
                outs[o][sl] = g
                outs[o + 1][sl] = d
                outs[o + 2][sl] = m2
                outs[o + 3][sl] = v2
            o += 4
        for k in range(nx):
            d, m2, v2 = _adamw(wr[ns + k][...], gx[k][...], mr[ns + k][...], vr[ns + k][...])
            outs[o][...] = d
            outs[o + 1][...] = m2
            outs[o + 2][...] = v2
            o += 3

    out_shape = []
    for k in range(ns):
        out_shape += [jax.ShapeDtypeStruct(ws[k].shape, F32)] * 4
    for k in range(nx):
        out_shape += [jax.ShapeDtypeStruct(ws[ns + k].shape, F32)] * 3
    return pl.pallas_call(body, out_shape=out_shape, compiler_params=_params(0), name="adamw_small")(sm, *special_g, *ws, *ms, *vs)


def _place():
    return lax.axis_index("x"), lax.axis_index("y"), lax.axis_index("c")


def _index(px, py, pc):
    return 4 * px + 2 * py + pc


def all_gather(name, shards):
    n = len(shards)
    hbm = pl.BlockSpec(memory_space=pl.ANY)

    def body(*refs):
        ins, outs = refs[:n], refs[n:2 * n]
        send_sems, recv_sems, local_sems = refs[2 * n:]
        x, y, c = _place()
        me, sibling = (x, y, c), (x, y, 1 - c)
        chips = [(1 - x, y), (x, 1 - y), (1 - x, 1 - y)]

        def copy(i, k, block, to, src=None):
            dst = outs[i].at[_index(*block)]
            return pltpu.make_async_remote_copy(src_ref=dst if src is None else src, dst_ref=dst, send_sem=send_sems.at[7 * i + k],
                                                recv_sem=recv_sems.at[7 * i + k], device_id=to, device_id_type=MESH)

        mine = [pltpu.make_async_copy(ins[i], outs[i].at[_index(*me)], local_sems.at[i]) for i in range(n)]
        for cp in mine:
            cp.start()
        first = []
        for i in range(n):
            first += [copy(i, 1 + j, me, (*chip, c), src=ins[i]) for j, chip in enumerate(chips)]
            first.append(copy(i, 0, me, sibling, src=ins[i]))
        for cp in first:
            cp.start()
        passed = []
        for i in range(n):
            for j, chip in enumerate(chips):
                copy(i, 1 + j, (*chip, c), me).wait_recv()
                cp = copy(i, 4 + j, (*chip, c), sibling)
                cp.start()
                passed.append(cp)
        for i in range(n):
            copy(i, 0, sibling, me).wait_recv()
            for j, chip in enumerate(chips):
                copy(i, 4 + j, (*chip, 1 - c), me).wait_recv()
        for cp in first + passed:
            cp.wait_send()
        for cp in mine:
            cp.wait()

    return pl.pallas_call(
        body, in_specs=[hbm] * n, out_specs=[hbm] * n,
        out_shape=[jax.ShapeDtypeStruct((8,) + s.shape, s.dtype) for s in shards],
        scratch_shapes=[pltpu.SemaphoreType.DMA((7 * n,)), pltpu.SemaphoreType.DMA((7 * n,)), pltpu.SemaphoreType.DMA((n,))],
        name=name)(*shards)


HBM_SPEC = pl.BlockSpec(memory_space=pltpu.HBM)
SEM_SPEC = pl.BlockSpec(memory_space=pltpu.SEMAPHORE)
EFFECT = pltpu.SideEffectType.DATAFLOW_SIDE_EFFECTING


def _peers(x, y, c):
    return [((1 - x) if k & 4 else x, (1 - y) if k & 2 else y, (1 - c) if k & 1 else c) for k in range(1, 8)]


def _pieces(rows):
    for n in (4, 2):
        if rows % (16 * n) == 0:
            return [(r * (rows // n), rows // n) for r in range(n)]
    return [(0, rows)]


def _peer_copies(src, land, send_sems, recv_sems, k, peer, mine, slab_src):
    block = src.at[_index(*peer)] if slab_src else src
    return [pltpu.make_async_remote_copy(src_ref=block.at[pl.ds(r0, nr)], dst_ref=land.at[mine, pl.ds(r0, nr)], send_sem=send_sems.at[k],
                                         recv_sem=recv_sems.at[k], device_id=peer, device_id_type=MESH)
            for r0, nr in _pieces(block.shape[0])]


def copies_start(name, srcs, slab_src, after):
    n = len(srcs)
    zones = [jax.ShapeDtypeStruct(s.shape if slab_src else (8,) + s.shape, s.dtype) for s in srcs]
    afters = [] if after is None else [after]

    def body(*refs):
        ins, lands = refs[:n], refs[n:2 * n]
        first = 2 * n + len(afters)
        sends, recvs = refs[first:first + n], refs[first + n:first + 2 * n]
        token = refs[-1]
        x, y, c = _place()
        mine = _index(x, y, c)
        for i in range(n):
            per_peer = [_peer_copies(ins[i], lands[i], sends[i], recvs[i], k, peer, mine, slab_src) for k, peer in enumerate(_peers(x, y, c))]
            for piece in zip(*per_peer):
                for cp in piece:
                    cp.start()
        token[...] = jnp.zeros_like(token)

    sem = pltpu.SemaphoreType.DMA((7,))
    res = pl.pallas_call(
        body, name=name,
        out_shape=([sem] * (2 * n) + [pltpu.HBM(s.shape, s.dtype) for s in srcs] + [pltpu.HBM(z.shape, z.dtype) for z in zones]
                   + [jax.ShapeDtypeStruct((8, 128), F32)]),
        in_specs=[HBM_SPEC] * (2 * n) + [pl.BlockSpec(memory_space=pl.ANY)] * len(afters),
        out_specs=[SEM_SPEC] * (2 * n) + [HBM_SPEC] * (2 * n) + [pl.BlockSpec(memory_space=pltpu.VMEM)],
        input_output_aliases={i: 2 * n + i for i in range(2 * n)},
        compiler_params=pltpu.CompilerParams(has_side_effects=EFFECT),
    )(*[pltpu.with_memory_space_constraint(s, pltpu.HBM) for s in srcs],
      *[pltpu.with_memory_space_constraint(lax.empty(z.shape, z.dtype), pltpu.HBM) for z in zones], *afters)
    return [(res[i], res[n + i], res[2 * n + i], res[3 * n + i]) for i in range(n)], res[-1][0:1, 0:1]


def copies_wait(name, started, slab_src, after):
    n = len(started)

    def body(*refs):
        ins, lands = refs[:n], refs[n:2 * n]
        sends, recvs = refs[2 * n:3 * n], refs[3 * n:4 * n]
        x, y, c = _place()
        mine = _index(x, y, c)
        for i in range(n):
            for k, peer in enumerate(_peers(x, y, c)):
                arrival = pltpu.make_async_remote_copy(src_ref=ins[i].at[mine] if slab_src else ins[i], dst_ref=lands[i].at[_index(*peer)],
                                                       send_sem=sends[i].at[k], recv_sem=recvs[i].at[k], device_id=peer, device_id_type=MESH)
                arrival.wait_send()
                arrival.wait_recv()

    srcs = [s[2] for s in started]
    lands = [s[3] for s in started]
    afters = list(after) if isinstance(after, (list, tuple)) else [after]
    res = pl.pallas_call(
        body, name=name,
        out_shape=[pltpu.HBM(s.shape, s.dtype) for s in srcs] + [pltpu.HBM(z.shape, z.dtype) for z in lands],
        in_specs=[HBM_SPEC] * (2 * n) + [SEM_SPEC] * (2 * n) + [pl.BlockSpec(memory_space=pl.ANY)] * len(afters),
        out_specs=[HBM_SPEC] * (2 * n),
        input_output_aliases={i: i for i in range(2 * n)},
        compiler_params=pltpu.CompilerParams(has_side_effects=EFFECT),
    )(*srcs, *lands, *[s[0] for s in started], *[s[1] for s in started], *afters)
    me = _index(*_place())
    own = [lax.dynamic_index_in_dim(s, me, 0, keepdims=True) if slab_src else s[None] for s in res[:n]]
    return [lax.dynamic_update_slice_in_dim(z, o, me, 0) for z, o in zip(res[n:], own)]


WEIGHTS = ["meta_tokens", "norm1_w", "w_in", "ssd_conv_w", "ssd_conv_b", "ssd_dt_bias", "ssd_a_log", "ssd_d", "ssd_norm_w", "lru_conv_w",
           "lru_conv_b", "lru_wa", "lru_ba", "lru_wx", "lru_bx", "lru_lambda", "lru_norm_w", "w_out", "norm2_w", "w_gate", "w_up", "w_down",
           "final_norm_w"]
BIG = ["w_in", "w_out", "w_gate", "w_up", "w_down"]
COLUMN_SHARDED = ["w_in", "w_gate", "w_up"]


def _pair_blocks(w):
    w = w.reshape(8, 2, 64, 64)
    z = jnp.zeros((8, 64, 64), w.dtype)
    return jnp.concatenate([jnp.concatenate([w[:, 0], z], axis=2), jnp.concatenate([z, w[:, 1]], axis=2)], axis=1)


def _unpair_blocks(w2):
    return jnp.stack([w2[:, :64, :64], w2[:, 64:, 64:]], axis=1).reshape(16, 64, 64)


def _per_group(v):
    return jnp.pad(v.reshape(2, 1, 8), ((0, 0), (0, 0), (0, 120)))


def _pad_cols(v, n):
    return jnp.pad(v, ((0, 0), (0, n - v.shape[1])))


def local_step(x, target, meta, ssd_cw, lru_cw, w_in, fetch, send, p):
    z120 = jnp.zeros((120, D), BF)
    w_dt = jnp.concatenate([w_in[2560:2568], z120, w_in[2568:2576], z120], axis=0)
    bias2, alog2, d2 = _per_group(p["ssd_dt_bias"]), _per_group(p["ssd_a_log"]), _per_group(p["ssd_d"])
    wa2 = _pair_blocks(p["lru_wa"]).astype(BF)
    wx2 = _pair_blocks(p["lru_wx"]).astype(BF)
    lru = (lru_cw, p["lru_conv_b"], wa2, p["lru_ba"], wx2, p["lru_bx"], p["lru_lambda"])

    h0 = jnp.concatenate([jnp.zeros((NPAD, D), F32), meta, x], axis=0)
    proj, dt_raw, u1 = in_proj(h0, p["norm1_w"], w_in, w_dt)
    yn_ssd, y_pre, h_prev = ssd_fwd(proj, dt_raw, ssd_cw, p["ssd_conv_b"], bias2, alog2, d2, p["ssd_norm_w"])
    hseq, a = lru_fwd(proj, *lru)
    (w_out,) = fetch(["w_out"], hseq)
    h1, cat = out_proj(yn_ssd, proj, hseq, p["lru_norm_w"], w_out, h0)
    w_gate, w_up = fetch(["w_gate", "w_up"], h1)
    gt, up, act, u2 = gate_up(h1, p["norm2_w"], w_gate, w_up)
    (w_down,) = fetch(["w_down"], act)
    dh2, dh2_b, loss, d_fnw = down_loss(act, w_down, h1, target, p["final_norm_w"])

    dgt, dup, g_down, g_gate, g_up = swiglu_bwd(dh2_b, w_down, gt, up, act, u2)
    sent = send({"w_down": g_down, "w_gate": g_gate, "w_up": g_up})
    dh1, dh1_b, d_n2 = gate_up_bwd(dgt, dup, w_gate, w_up, h1, p["norm2_w"] + sent, dh2)
    sent = send({"w_out": weight_grad("dw_out", cat, dh1_b)})
    dyn, dh_out, dg_b, d_lnw = out_proj_bwd(dh1_b, w_out, proj, hseq, p["lru_norm_w"] + sent)

    dxl_b, d_lcw, d_lcb, dwa2, d_ba, dwx2, d_bx, d_lam = lru_bwd(dh_out, a, hseq, proj, *lru)
    dz_b, dxbc_b, ddt_b, dpar, d_snw, d_scw, d_scb = ssd_bwd(dyn, proj, dt_raw, ssd_cw, p["ssd_conv_b"], y_pre, h_prev, bias2, alog2, d2,
                                                             p["ssd_norm_w"] + sent)
    sent = send({"w_in": in_weight_grad(dz_b, dg_b, dxl_b, dxbc_b, ddt_b, u1)})
    grad_x, d_meta, d_n1 = in_proj_bwd(dz_b, dg_b, dxl_b, dxbc_b, ddt_b, w_in, w_dt, h0, p["norm1_w"] + sent, dh1)
    small = {"norm1_w": d_n1, "ssd_conv_b": d_scb, "ssd_dt_bias": dpar[:, 0, :8].reshape(1, 16), "ssd_a_log": dpar[:, 1, :8].reshape(1, 16),
             "ssd_d": dpar[:, 2, :8].reshape(1, 16), "ssd_norm_w": d_snw, "lru_conv_b": d_lcb, "lru_ba": d_ba, "lru_bx": d_bx,
             "lru_lambda": d_lam, "lru_norm_w": d_lnw, "norm2_w": d_n2, "final_norm_w": d_fnw,
             "lru_wa": _unpair_blocks(dwa2), "lru_wx": _unpair_blocks(dwx2), "meta_tokens": d_meta,
             "ssd_conv_w": d_scw, "lru_conv_w": d_lcw}
    return loss, grad_x, small


def _pack_small(small, loss):
    rows = [_pad_cols(small[name], -(-n // 1024) * 1024).reshape(-1, 1024) for name, n in SIMPLE]
    rows += [small["lru_wa"].reshape(64, 1024), small["lru_wx"].reshape(64, 1024), small["meta_tokens"],
             _pad_cols(small["ssd_conv_w"], 2048).reshape(8, 1024), small["lru_conv_w"], _pad_cols(loss[:, 0:1], 1024)]
    sm = jnp.concatenate(rows, axis=0)
    return jnp.pad(sm, ((0, SM_ROWS - sm.shape[0]), (0, 0)))


def _slabs(g):
    return g.reshape(8, g.shape[0] // 8, g.shape[1])


def _unslab(g):
    return g.reshape(8 * g.shape[1], g.shape[2])


def kernel(x, meta_tokens, norm1_w, w_in, ssd_conv_w, ssd_conv_b, ssd_dt_bias, ssd_a_log, ssd_d, ssd_norm_w, lru_conv_w, lru_conv_b, lru_wa, lru_ba, lru_wx, lru_bx, lru_lambda, lru_norm_w, w_out, norm2_w, w_gate, w_up, w_down, final_norm_w, loss_target, m_meta_tokens, m_norm1_w, m_w_in, m_ssd_conv_w, m_ssd_conv_b, m_ssd_dt_bias, m_ssd_a_log, m_ssd_d, m_ssd_norm_w, m_lru_conv_w, m_lru_conv_b, m_lru_wa, m_lru_ba, m_lru_wx, m_lru_bx, m_lru_lambda, m_lru_norm_w, m_w_out, m_norm2_w, m_w_gate, m_w_up, m_w_down, m_final_norm_w, v_meta_tokens, v_norm1_w, v_w_in, v_ssd_conv_w, v_ssd_conv_b, v_ssd_dt_bias, v_ssd_a_log, v_ssd_d, v_ssd_norm_w, v_lru_conv_w, v_lru_conv_b, v_lru_wa, v_lru_ba, v_lru_wx, v_lru_bx, v_lru_lambda, v_lru_norm_w, v_w_out, v_norm2_w, v_w_gate, v_w_up, v_w_down, v_final_norm_w):
    w = dict(meta_tokens=meta_tokens, norm1_w=norm1_w, w_in=w_in[0], ssd_conv_w=ssd_conv_w[0], ssd_conv_b=ssd_conv_b, ssd_dt_bias=ssd_dt_bias,
             ssd_a_log=ssd_a_log, ssd_d=ssd_d, ssd_norm_w=ssd_norm_w, lru_conv_w=lru_conv_w[0], lru_conv_b=lru_conv_b, lru_wa=lru_wa[0],
             lru_ba=lru_ba, lru_wx=lru_wx[0], lru_bx=lru_bx, lru_lambda=lru_lambda, lru_norm_w=lru_norm_w, w_out=w_out[0], norm2_w=norm2_w,
             w_gate=w_gate[0], w_up=w_up[0], w_down=w_down[0], final_norm_w=final_norm_w.reshape(1, D))
    m = dict(meta_tokens=m_meta_tokens, norm1_w=m_norm1_w, w_in=m_w_in[0], ssd_conv_w=m_ssd_conv_w[0], ssd_conv_b=m_ssd_conv_b,
             ssd_dt_bias=m_ssd_dt_bias, ssd_a_log=m_ssd_a_log, ssd_d=m_ssd_d, ssd_norm_w=m_ssd_norm_w, lru_conv_w=m_lru_conv_w[0],
             lru_conv_b=m_lru_conv_b, lru_wa=m_lru_wa[0], lru_ba=m_lru_ba, lru_wx=m_lru_wx[0], lru_bx=m_lru_bx, lru_lambda=m_lru_lambda,
             lru_norm_w=m_lru_norm_w, w_out=m_w_out[0], norm2_w=m_norm2_w, w_gate=m_w_gate[0], w_up=m_w_up[0], w_down=m_w_down[0],
             final_norm_w=m_final_norm_w.reshape(1, D))
    v = dict(meta_tokens=v_meta_tokens, norm1_w=v_norm1_w, w_in=v_w_in[0], ssd_conv_w=v_ssd_conv_w[0], ssd_conv_b=v_ssd_conv_b,
             ssd_dt_bias=v_ssd_dt_bias, ssd_a_log=v_ssd_a_log, ssd_d=v_ssd_d, ssd_norm_w=v_ssd_norm_w, lru_conv_w=v_lru_conv_w[0],
             lru_conv_b=v_lru_conv_b, lru_wa=v_lru_wa[0], lru_ba=v_lru_ba, lru_wx=v_lru_wx[0], lru_bx=v_lru_bx, lru_lambda=v_lru_lambda,
             lru_norm_w=v_lru_norm_w, w_out=v_w_out[0], norm2_w=v_norm2_w, w_gate=v_w_gate[0], w_up=v_w_up[0], w_down=v_w_down[0],
             final_norm_w=v_final_norm_w.reshape(1, D))
    shapes = dict(meta_tokens=meta_tokens.shape, norm1_w=norm1_w.shape, w_in=w_in.shape, ssd_conv_w=ssd_conv_w.shape,
                  ssd_conv_b=ssd_conv_b.shape, ssd_dt_bias=ssd_dt_bias.shape, ssd_a_log=ssd_a_log.shape, ssd_d=ssd_d.shape,
                  ssd_norm_w=ssd_norm_w.shape, lru_conv_w=lru_conv_w.shape, lru_conv_b=lru_conv_b.shape, lru_wa=lru_wa.shape,
                  lru_ba=lru_ba.shape, lru_wx=lru_wx.shape, lru_bx=lru_bx.shape, lru_lambda=lru_lambda.shape, lru_norm_w=lru_norm_w.shape,
                  w_out=w_out.shape, norm2_w=norm2_w.shape, w_gate=w_gate.shape, w_up=w_up.shape, w_down=w_down.shape,
                  final_norm_w=final_norm_w.shape)
    me = _index(*_place())
    for n in COLUMN_SHARDED:
        w[n], m[n], v[n] = w[n].T, m[n].T, v[n].T

    small_shard = jnp.concatenate([w["meta_tokens"], _pad_cols(w["ssd_conv_w"], 256).reshape(8, 128), w["lru_conv_w"],
                                   jnp.zeros((4, 128), F32)], axis=0)
    g_in, gs = all_gather("gather_w_in", [w["w_in"].astype(BF), small_shard])
    later = ["w_out", "w_gate", "w_up", "w_down"]
    started, behind = copies_start("gather_rest_start", [w[n].astype(BF) for n in later], False, gs)
    started = dict(zip(later, started))
    meta_full = gs[:, 0:16].transpose(1, 0, 2).reshape(N_META, D)
    ssd_cw = gs[:, 16:24].reshape(8, 4, 256)[:, :, :192].transpose(1, 0, 2).reshape(4, XBC)
    lru_cw = gs[:, 24:28].transpose(1, 0, 2).reshape(4, LRU_W)

    def fetch(names, after):
        got = copies_wait("gather_" + names[0] + "_wait", [started[n] for n in names], False, after)
        return [_unslab(g) for g in got]

    in_flight = {}

    def send(grads):
        names = list(grads)
        st, token = copies_start("grads_" + names[0] + "_start", [grads[n] if n == "small" else _slabs(grads[n]) for n in names], True, None)
        in_flight.update(zip(names, st))
        return token

    loss, grad_x, small = local_step(x[0], loss_target[0], meta_full, ssd_cw, lru_cw, _unslab(g_in), fetch, send,
                                     {**w, "norm1_w": w["norm1_w"] + behind})
    send({"small": _pack_small(small, loss).reshape(8, SM_ROWS // 8, 1024)})

    out = {}
    early = ["w_down", "w_gate", "w_up", "w_out"]
    recv = dict(zip(early, copies_wait("grads_early_wait", [in_flight[n] for n in early], True, in_flight["small"][2])))
    for pair in (early[:2], early[2:]):
        done = adamw_shards("adamw_" + pair[0], [recv[n] for n in pair], [w[n] for n in pair], [m[n] for n in pair], [v[n] for n in pair])
        out.update(zip(pair, done))
    recv_in, recv_small = copies_wait("grads_late_wait", [in_flight["w_in"], in_flight["small"]], True, [out[n][0] for n in early])
    def lines(a):
        return jnp.transpose(a.reshape(D // 128, 128, IN_COLS // 8), (2, 0, 1))

    out["w_in"] = [jnp.transpose(o, (1, 2, 0)).reshape(D, IN_COLS // 8) for o in adamw_w_in(recv_in, lines(w_in), lines(m_w_in), lines(v_w_in))]
    for n in ("w_gate", "w_up"):
        out[n] = [o.T for o in out[n]]
    sm = all_gather("gather_small_grads", [sum_slabs(recv_small)])[0].reshape(SM_ROWS, 1024)
    special_g =[sm[SM_WA:SM_WA + 64].reshape(16, 64, 64), sm[SM_WX:SM_WX + 64].reshape(16, 64, 64),
                 lax.dynamic_slice(sm[SM_META:SM_META + 16], (0, 128 * me), (16, 128)),
                 lax.dynamic_slice(sm[SM_SCW:SM_SCW + 8].reshape(4, 2048), (0, 192 * me), (4, 192)),
                 lax.dynamic_slice(sm[SM_LCW:SM_LCW + 4], (0, 128 * me), (4, 128))]
    names = [n for n, _ in SIMPLE] + SPECIAL
    res = adamw_small(sm, special_g, [w[n] for n in names], [m[n] for n in names], [v[n] for n in names])
    for k, (n, _) in enumerate(SIMPLE):
        out[n] = res[4 * k:4 * k + 4]
    for k, n in enumerate(SPECIAL):
        o = 4 * len(SIMPLE) + 3 * k
        out[n] = [special_g[k]] + list(res[o:o + 3])
    loss_total = sm[SM_LOSS, 0]
    flat = [loss_total, grad_x[None]]
    for k in range(4):
        flat += [out[n][k].reshape(shapes[n]) for n in WEIGHTS]
    return tuple(flat)
```

```python
import math

import jax
import jax.numpy as jnp
from jax import lax
from jax.experimental import pallas as pl
from jax.experimental.pallas import tpu as pltpu

F32 = jnp.float32
BF = jnp.bfloat16

D = 1024
SEQ = 2048
N_META = 16
Q = 128
NPAD = 112
T = NPAD + N_META + SEQ
NCH = T // Q
RC = 544
D_FF = 2816
SSD_W = 1024
LRU_W = 1024
XBC = 1536
IN_COLS = 4624
PZ, PG, PXL, PXBC = 0, 1024, 2048, 3072
NP_IN = 4608
EPS = 1e-6
LRU_C = 8.0
VMEM_LIMIT = 56 * 1024 * 1024

ADAM_LR, ADAM_B1, ADAM_B2, ADAM_EPS, ADAM_WD, ADAM_STEP = 0.001, 0.9, 0.999, 1e-08, 0.01, 10

NT_DIMS = (((1,), (1,)), ((), ()))
TN_DIMS = (((0,), (0,)), ((), ()))
MESH = pl.DeviceIdType.MESH


def _params(n_grid=1, limit=VMEM_LIMIT):
    return pltpu.CompilerParams(dimension_semantics=("arbitrary",) * n_grid, vmem_limit_bytes=limit)


def _spec(shape, imap, single=False):
    if single:
        return pl.BlockSpec(shape, imap, pipeline_mode=pl.Buffered(1))
    return pl.BlockSpec(shape, imap)


def _sigmoid(x):
    return 0.5 * jnp.tanh(0.5 * x) + 0.5


def _sigmoid_gate(x):
    return 1.0 / (1.0 + jnp.exp(-x))


def _softplus(x):
    return jnp.maximum(x, 0.0) + jnp.log(1.0 + jnp.exp(-jnp.abs(x)))


def _rms_stats(h):
    return lax.rsqrt(jnp.mean(h * h, axis=-1, keepdims=True) + EPS)


def _rms(h, w):
    return (h * _rms_stats(h)) * w


def _rms_bwd(du, h, w):
    r = _rms_stats(h)
    n = h * r
    dn = du * w
    dh = r * (dn - n * jnp.mean(dn * n, axis=-1, keepdims=True))
    return dh, du * n


_G0 = math.sqrt(2.0 / math.pi)


def _gelu(x):
    return 0.5 * x * (1.0 + jnp.tanh(_G0 * (x + 0.044715 * (x * x * x))))


def _gelu_grad(x):
    t = jnp.tanh(_G0 * (x + 0.044715 * (x * x * x)))
    return 0.5 * (1.0 + t) + 0.5 * x * (1.0 - t * t) * (_G0 * (1.0 + 3.0 * 0.044715 * (x * x)))


def _rows(shape, r0=0):
    return lax.broadcasted_iota(jnp.int32, shape, 0) + r0


def _lanes(shape):
    return lax.broadcasted_iota(jnp.int32, shape, 1)


HALO = 8


def _fill_padded(pad_ref, x_ref):
    pad_ref[0:HALO, :] = jnp.zeros((HALO, pad_ref.shape[1]), F32)
    pad_ref[T + HALO:T + 2 * HALO, :] = jnp.zeros((HALO, pad_ref.shape[1]), F32)

    def step(c, carry):
        r0 = pl.multiple_of(c * Q, Q)
        pad_ref[pl.ds(r0 + HALO, Q), :] = x_ref[pl.ds(r0, Q), :].astype(F32)
        return carry

    lax.fori_loop(0, NCH, step, 0)


def _back(pad_ref, r0):
    win = pad_ref[pl.ds(r0, Q + HALO), :]
    return lambda s: win[HALO:, :] if s == 0 else pltpu.roll(win, s, axis=0)[HALO:, :]


def _ahead(pad_ref, r0):
    win = pad_ref[pl.ds(r0 + HALO, Q + HALO), :]
    return lambda s: win[:Q, :] if s == 0 else pltpu.roll(win, Q + HALO - s, axis=0)[:Q, :]


def _conv(back, w, b):
    y = b + w[3:4, :] * back(0)
    for k in range(3):
        y = y + w[k:k + 1, :] * back(3 - k)
    return y


def _conv_bwd_x(ahead, w):
    dx = w[3:4, :] * ahead(0)
    for k in range(3):
        dx = dx + w[k:k + 1, :] * ahead(3 - k)
    return dx


def _conv_bwd_w(dy, back):
    dws = [jnp.sum(dy * back(3 - k), axis=0, keepdims=True) for k in range(4)]
    return jnp.concatenate(dws, axis=0), jnp.sum(dy, axis=0, keepdims=True)


def _chunks(fn, unrolled=False):
    if unrolled:
        for c in range(NCH):
            fn(c * Q)
        return

    def step(c, carry):
        fn(pl.multiple_of(c * Q, Q))
        return carry

    lax.fori_loop(0, NCH, step, 0)


HALF = RC // 2


def _col_tiles(n, tn, fn):
    def step(j, carry):
        fn(pl.multiple_of(j * tn, tn))
        return carry

    lax.fori_loop(0, n // tn, step, 0)


def _rows_spec(cols, block_col=0):
    return _spec((RC, cols), lambda i: (i, block_col))


def _whole(shape):
    return _spec(shape, lambda i: tuple(0 for _ in shape), single=True)


def _vec(cols):
    return _spec((1, cols), lambda i: (0, 0))


def _zero_at_first(*refs):
    @pl.when(pl.program_id(0) == 0)
    def _():
        for r in refs:
            r[...] = jnp.zeros_like(r)


ANY_SPEC = pl.BlockSpec(memory_space=pl.ANY)


def _arriving(src, dst, sems, starts, rows):
    n, ahead = len(starts), 2
    first = pl.program_id(0) == 0

    def piece(k):
        r0 = starts[0]
        for j in range(1, n):
            r0 = jnp.where(k == j, starts[j], r0)
        at = pl.ds(pl.multiple_of(r0, 16), rows)
        return pltpu.make_async_copy(src.at[at], dst.at[at], sems.at[k])

    @pl.when(first)
    def _():
        for k in range(min(ahead, n)):
            piece(k).start()

    def ready(k):
        k = jnp.asarray(k, jnp.int32)

        @pl.when(first)
        def _():
            piece(k).wait()

            @pl.when(k + ahead < n)
            def _():
                piece(k + ahead).start()

    return ready


IN_RUNS = ((PZ, 0, 1024), (PXBC, 1024, XBC), (PG, 2576, 2048))
IN_TILE = 512
IN_TILE_ROWS = [wrow + IN_TILE * j for _, wrow, width in IN_RUNS for j in range(width // IN_TILE)]


def _in_tiles(fn):
    done = 0
    for pcol, wrow, width in IN_RUNS:
        def step(j, carry, pcol=pcol, wrow=wrow, done=done):
            fn(pl.multiple_of(pcol + j * IN_TILE, IN_TILE), pl.multiple_of(wrow + j * IN_TILE, 16), done + j)
            return carry

        lax.fori_loop(0, width // IN_TILE, step, 0)
        done += width // IN_TILE


def in_proj(h0, wn, w_t, w_dt):
    def body(h_ref, wn_ref, w_hbm, wdt_ref, o_ref, dt_ref, u_ref, w_ref, w_sems):
        ready = _arriving(w_hbm, w_ref, w_sems, IN_TILE_ROWS, IN_TILE)
        for r in (0, HALF):
            u_ref[r:r + HALF, :] = _rms(h_ref[r:r + HALF, :], wn_ref[...]).astype(BF)

        def tile(pcol, wrow, k):
            ready(k)
            o_ref[:, pl.ds(pcol, IN_TILE)] = lax.dot_general(u_ref[...], w_ref[pl.ds(wrow, IN_TILE), :], NT_DIMS,
                                                             preferred_element_type=F32).astype(BF)

        _in_tiles(tile)
        dt_ref[...] = lax.dot_general(u_ref[...], wdt_ref[...], NT_DIMS, preferred_element_type=F32)

    return pl.pallas_call(
        body, grid=(T // RC,), in_specs=[_rows_spec(D), _vec(D), ANY_SPEC, _whole((256, D))],
        out_specs=[_rows_spec(NP_IN), _rows_spec(256), _rows_spec(D)],
        out_shape=[jax.ShapeDtypeStruct((T, NP_IN), BF), jax.ShapeDtypeStruct((T, 256), F32), jax.ShapeDtypeStruct((T, D), BF)],
        scratch_shapes=[pltpu.VMEM((IN_COLS, D), BF), pltpu.SemaphoreType.DMA((len(IN_TILE_ROWS),))],
        compiler_params=_params(), name="in_proj")(h0, wn, w_t, w_dt)


def out_proj(yn_ssd, proj, hseq, lru_nw, w_out, h0):
    def body(y_ref, g_ref, h_ref, wn_ref, w_ref, r_ref, o_ref, cat_ref):
        cat_ref[:, 0:SSD_W] = y_ref[...]
        for r in (0, HALF):
            y = _gelu(g_ref[r:r + HALF, :].astype(F32)) * h_ref[r:r + HALF, :]
            cat_ref[r:r + HALF, SSD_W:] = _rms(y, wn_ref[...]).astype(BF)

        def tile(c0):
            o_ref[:, pl.ds(c0, 512)] = r_ref[:, pl.ds(c0, 512)] + jnp.dot(cat_ref[...], w_ref[:, pl.ds(c0, 512)], preferred_element_type=F32)

        _col_tiles(D, 512, tile)

    return pl.pallas_call(
        body, grid=(T // RC,),
        in_specs=[_rows_spec(SSD_W), _rows_spec(LRU_W, PG // LRU_W), _rows_spec(LRU_W), _vec(LRU_W), _whole((SSD_W + LRU_W, D)), _rows_spec(D)],
        out_specs=[_rows_spec(D), _rows_spec(SSD_W + LRU_W)],
        out_shape=[jax.ShapeDtypeStruct((T, D), F32), jax.ShapeDtypeStruct((T, SSD_W + LRU_W), BF)],
        compiler_params=_params(), name="out_proj")(yn_ssd, proj, hseq, lru_nw, w_out, h0)


def out_proj_bwd(dh1_b, w_out, proj, hseq, lru_nw):
    def body(d_ref, w_ref, g_ref, h_ref, wn_ref, dy_ref, dh_ref, dg_ref, dw_ref, dl_scr):
        _zero_at_first(dw_ref)

        def tile(c0):
            dy_ref[:, pl.ds(c0, 512)] = lax.dot_general(d_ref[...], w_ref[pl.ds(c0, 512), :], NT_DIMS, preferred_element_type=F32)
            dl_scr[:, pl.ds(c0, 512)] = lax.dot_general(d_ref[...], w_ref[pl.ds(SSD_W + c0, 512), :], NT_DIMS, preferred_element_type=F32)

        _col_tiles(SSD_W, 512, tile)

        for r in (0, HALF):
            g = g_ref[r:r + HALF, :].astype(F32)
            h = h_ref[r:r + HALF, :]
            ge = _gelu(g)
            dy, dw = _rms_bwd(dl_scr[r:r + HALF, :], ge * h, wn_ref[...])
            dw_ref[...] += jnp.sum(dw, axis=0, keepdims=True)
            dh_ref[r:r + HALF, :] = dy * ge
            dg_ref[r:r + HALF, :] = (dy * h * _gelu_grad(g)).astype(BF)

    return pl.pallas_call(
        body, grid=(T // RC,),
        in_specs=[_rows_spec(D), _whole((SSD_W + LRU_W, D)), _rows_spec(LRU_W, PG // LRU_W), _rows_spec(LRU_W), _vec(LRU_W)],
        out_specs=[_rows_spec(SSD_W), _rows_spec(LRU_W), _rows_spec(LRU_W), _vec(LRU_W)],
        out_shape=[jax.ShapeDtypeStruct((T, SSD_W), F32), jax.ShapeDtypeStruct((T, LRU_W), F32), jax.ShapeDtypeStruct((T, LRU_W), BF),
                   jax.ShapeDtypeStruct((1, LRU_W), F32)],
        scratch_shapes=[pltpu.VMEM((RC, LRU_W), F32)],
        compiler_params=_params(), name="out_proj_bwd")(dh1_b, w_out, proj, hseq, lru_nw)


def in_proj_bwd(dz, dg, dxl, dxbc, ddt, w_t, w_dt, h0, wn, dh1):
    first = NPAD + N_META

    def body(dz_ref, dg_ref, dxl_ref, dxbc_ref, ddt_ref, w_hbm, wdt_ref, h_ref, wn_ref, r_ref, gx_hbm, meta_ref, dw_ref, du_scr, o_ref, sem,
             w_ref, w_sems):
        i = pl.program_id(0)
        ready = _arriving(w_hbm, w_ref, w_sems, IN_TILE_ROWS, IN_TILE)
        _zero_at_first(dw_ref)
        du_scr[...] = jnp.dot(ddt_ref[...], wdt_ref[...], preferred_element_type=F32)
        done = 0
        for d_ref, wrow, width in ((dz_ref, 0, 1024), (dxbc_ref, 1024, XBC), (dg_ref, 2576, 1024), (dxl_ref, 3600, 1024)):
            def step(j, carry, d_ref=d_ref, wrow=wrow, done=done):
                c0 = pl.multiple_of(j * IN_TILE, IN_TILE)
                ready(done + j)
                du_scr[...] += jnp.dot(d_ref[:, pl.ds(c0, IN_TILE)], w_ref[pl.ds(pl.multiple_of(wrow + c0, 16), IN_TILE), :],
                                       preferred_element_type=F32)
                return carry

            lax.fori_loop(0, width // IN_TILE, step, 0)
            done += width // IN_TILE
        for r in (0, HALF):
            dh, dw = _rms_bwd(du_scr[r:r + HALF, :], h_ref[r:r + HALF, :], wn_ref[...])
            dw_ref[...] += jnp.sum(dw, axis=0, keepdims=True)
            o_ref[r:r + HALF, :] = dh + r_ref[r:r + HALF, :]

        @pl.when(i == 0)
        def _():
            meta_ref[...] = o_ref[NPAD:first, :]
            head = pltpu.make_async_copy(o_ref.at[pl.ds(first, RC - first)], gx_hbm.at[pl.ds(0, RC - first)], sem)
            head.start()
            head.wait()

        @pl.when(i > 0)
        def _():
            rest = pltpu.make_async_copy(o_ref, gx_hbm.at[pl.ds(pl.multiple_of(i * RC - first, 32), RC)], sem)
            rest.start()
            rest.wait()

    return pl.pallas_call(
        body, grid=(T // RC,),
        in_specs=[_rows_spec(SSD_W), _rows_spec(LRU_W), _rows_spec(LRU_W), _rows_spec(XBC), _rows_spec(256), ANY_SPEC,
                  _whole((256, D)), _rows_spec(D), _vec(D), _rows_spec(D)],
        out_specs=[ANY_SPEC, _spec((N_META, D), lambda i: (0, 0)), _vec(D)],
        out_shape=[jax.ShapeDtypeStruct((SEQ, D), F32), jax.ShapeDtypeStruct((N_META, D), F32), jax.ShapeDtypeStruct((1, D), F32)],
        scratch_shapes=[pltpu.VMEM((RC, D), F32), pltpu.VMEM((RC, D), F32), pltpu.SemaphoreType.DMA,
                        pltpu.VMEM((IN_COLS, D), BF), pltpu.SemaphoreType.DMA((len(IN_TILE_ROWS),))],
        compiler_params=_params(), name="in_proj_bwd")(dz, dg, dxl, dxbc, ddt, w_t, w_dt, h0, wn, dh1)


GRAD_TILE = 256


def weight_grad(name, a, u1):
    tm = GRAD_TILE

    def body(a_ref, u_ref, o_ref):
        o_ref[...] = lax.dot_general(a_ref[...], u_ref[...], TN_DIMS, preferred_element_type=F32).astype(BF)

    return pl.pallas_call(
        body, grid=(a.shape[1] // tm,),
        in_specs=[_spec((T, tm), lambda j: (0, j)), _spec((T, D), lambda j: (0, 0), single=True)],
        out_specs=_spec((tm, D), lambda j: (j, 0)),
        out_shape=jax.ShapeDtypeStruct((a.shape[1], D), BF),
        compiler_params=_params(), name=name)(a, u1)


def in_weight_grad(dz, dg, dxl, dxbc, ddt, u1):
    tm = GRAD_TILE
    per_row = D // 128
    parts = (dz, dg, dxl, dxbc, ddt)
    first_rows = (0, 2576, 3600, 1024, 2560)
    tiles = [p.shape[1] // tm for p in parts]
    starts = [sum(tiles[:k]) for k in range(len(parts))]
    last = sum(tiles) - 1
    dt_lines = 8 * per_row

    def body(*refs):
        a_refs, u_ref, o_hbm, mix_scr, stage, sems = refs[:5], refs[5], refs[6], refs[7], refs[8], refs[9]
        step = pl.program_id(0)
        slot = step % 2
        line0 = 0
        for a_ref, start, n, first in zip(a_refs, starts, tiles, first_rows):
            here = (step >= start) & (step < start + n)
            line0 = jnp.where(here, per_row * (first + tm * (step - start)), line0)

            @pl.when(here)
            def _(a_ref=a_ref):
                res = lax.dot_general(a_ref[...], u_ref[...], TN_DIMS, preferred_element_type=F32)
                for q in range(per_row):
                    mix_scr[pl.ds(q, tm, stride=per_row), :] = res[:, 128 * q:128 * q + 128]

        def tile_copy(of_slot, to):
            return pltpu.make_async_copy(stage.at[of_slot], o_hbm.at[pl.ds(to, per_row * tm)], sems.at[of_slot])

        @pl.when(step >= 2)
        def _():
            tile_copy(slot, 0).wait()

        stage[slot] = mix_scr[...].astype(BF)

        @pl.when(step < last)
        def _():
            tile_copy(slot, pl.multiple_of(line0, 128)).start()

        @pl.when(step == last)
        def _():
            halves = [pltpu.make_async_copy(stage.at[slot, pl.ds(128 * per_row * k, dt_lines)],
                                            o_hbm.at[pl.ds(per_row * (first_rows[-1] + 8 * k), dt_lines)], sems.at[2 + k]) for k in range(2)]
            for cp in halves:
                cp.start()
            tile_copy(1 - slot, 0).wait()
            for cp in halves:
                cp.wait()

    def tile_of(start, n):
        return lambda j: (0, jnp.clip(j - start, 0, n - 1))

    return pl.pallas_call(
        body, grid=(last + 1,),
        in_specs=[_spec((T, tm), tile_of(s, n)) for s, n in zip(starts, tiles)] + [_spec((T, D), lambda j: (0, 0), single=True)],
        out_specs=pl.BlockSpec(memory_space=pl.ANY),
        out_shape=jax.ShapeDtypeStruct((per_row * IN_COLS, 128), BF),
        scratch_shapes=[pltpu.VMEM((per_row * tm, 128), F32), pltpu.VMEM((2, per_row * tm, 128), BF), pltpu.SemaphoreType.DMA((4,))],
        compiler_params=_params(), name="dw_in")(*parts, u1)


def _ssd_chunk_common(row0, dt_ref, b_ref, c_ref, bias, a_neg):
    shape = (Q, Q)
    lane = _lanes(shape)
    sub = _rows(shape)
    live = (_rows(shape, row0) >= NPAD) & (lane < 8)
    dtr = dt_ref[:, :]
    dt = jnp.where(live, _softplus(dtr + bias), 0.0)
    d_a = dt * a_neg
    tri = (sub >= lane).astype(F32)
    cs = jnp.dot(tri, d_a, precision=lax.Precision.HIGHEST, preferred_element_type=F32)
    cs_t = cs.T
    b_f = b_ref[:, :]
    bc = b_f.astype(BF)
    cc = c_ref[:, :].astype(BF)
    cb = lax.dot_general(cc, bc, NT_DIMS, preferred_element_type=F32)
    cs_last = cs[Q - 1:Q, :]
    return dict(lane=lane, sub=sub, live=live, dtr=dtr, dt=dt, cs=cs, cs_t=cs_t, bc=bc, cc=cc, cb=cb, bc_t=b_f.T.astype(BF),
                ecs=jnp.exp(cs), dsm=jnp.exp(cs_last - cs), gam=jnp.exp(cs_last))


def _pair(lane_even, mat, j):
    return jnp.where(lane_even, mat[:, j:j + 1], mat[:, j + 1:j + 2])


def _pair_row(lane_even, mat, j):
    return jnp.where(lane_even[0:1, :], mat[:, j:j + 1], mat[:, j + 1:j + 2])


def _head_decay(cm, j):
    seg = cm["cs"][:, j:j + 1] - cm["cs_t"][j:j + 1, :]
    return jnp.exp(jnp.where(cm["sub"] >= cm["lane"], seg, -jnp.inf))


def _head_decay_t(cm, j):
    seg = cm["cs_t"][j:j + 1, :] - cm["cs"][:, j:j + 1]
    return jnp.exp(jnp.where(cm["lane"] >= cm["sub"], seg, -jnp.inf))


def _conv_window(raw_ref, halo_ref, pad_scr):
    pad_scr[0:HALO, :] = halo_ref[...].astype(F32)[halo_ref.shape[0] - HALO:, :]
    pad_scr[HALO:HALO + Q, :] = raw_ref[...].astype(F32)
    win = pad_scr[...]
    return lambda s: win[HALO:, :] if s == 0 else pltpu.roll(win, s, axis=0)[HALO:, :]


def _xbc_cols(g):
    return slice(512 * g, 512 * g + 512), slice(SSD_W + 128 * g, SSD_W + 128 * g + 128), slice(SSD_W + 256 + 128 * g, SSD_W + 384 + 128 * g)


def ssd_fwd(proj, dt_raw, conv_w, conv_b, dt_bias2, a_log2, d2, norm_w):
    def body(raw_ref, halo_ref, dt_all, z_all, cw_ref, cb_ref, bias_all, alog_all, d_all, nw_all, yn_all, y_all, hp_all,
             h_all, pad_scr, act_scr):
        @pl.when(pl.program_id(0) == 0)
        def _():
            h_all[...] = jnp.zeros_like(h_all)

        pre = _conv(_conv_window(raw_ref, halo_ref, pad_scr), cw_ref[...], cb_ref[...])
        act_scr[...] = pre * _sigmoid(pre)
        for g in range(2):
            wide, thin = slice(512 * g, 512 * g + 512), slice(128 * g, 128 * g + 128)
            xs, bs, cs = _xbc_cols(g)
            group(act_scr.at[:, xs], act_scr.at[:, bs], act_scr.at[:, cs], dt_all.at[:, thin], z_all.at[:, wide], bias_all.at[g],
                  alog_all.at[g], d_all.at[g], nw_all.at[:, wide], yn_all.at[:, wide], y_all.at[:, wide], hp_all.at[g, 0], h_all.at[g])

    def group(x_ref, b_ref, c_ref, dt_ref, z_ref, bias_ref, alog_ref, d_ref, nw_ref, yn_ref, y_ref, hp_ref, h_scr):
        bias = bias_ref[...]
        a_neg = -jnp.exp(alog_ref[...])
        dsk = d_ref[...]
        cm = _ssd_chunk_common(pl.program_id(0) * Q, dt_ref, b_ref, c_ref, bias, a_neg)
        lane_even = cm["lane"] < 64
        for p in range(4):
            je, jo = 2 * p, 2 * p + 1
            xp = x_ref[:, 128 * p:128 * p + 128]
            xdt = xp * _pair(lane_even, cm["dt"], je)
            xdt_b = xdt.astype(BF)
            m_e = (cm["cb"] * _head_decay(cm, je)).astype(BF)
            m_o = (cm["cb"] * _head_decay(cm, jo)).astype(BF)
            zero = jnp.zeros_like(xdt_b)
            yd = (jnp.dot(m_e, jnp.where(lane_even, xdt_b, zero), preferred_element_type=F32)
                  + jnp.dot(m_o, jnp.where(lane_even, zero, xdt_b), preferred_element_type=F32))
            hp = h_scr[p]
            hp_ref[p] = hp
            yo = jnp.dot(cm["cc"], hp.astype(BF), preferred_element_type=F32) * _pair(lane_even, cm["ecs"], je)
            y_ref[:, 128 * p:128 * p + 128] = yd + yo + xp * _pair_row(lane_even, dsk, je)
            st = jnp.dot(cm["bc_t"], (xdt * _pair(lane_even, cm["dsm"], je)).astype(BF), preferred_element_type=F32)
            h_scr[p] = hp * _pair_row(lane_even, cm["gam"], je) + st
        zc = z_ref[:, :].astype(F32)
        gated = y_ref[:, :] * (zc * _sigmoid(zc))
        yn_ref[:, :] = _rms(gated, nw_ref[...]).astype(BF)

    par = _spec((2, 1, 128), lambda c: (0, 0, 0))
    wide = _spec((Q, SSD_W), lambda c: (c, 0))
    xbc = PXBC // XBC
    halo = 2 * HALO
    return pl.pallas_call(
        body, grid=(NCH,),
        in_specs=[_spec((Q, XBC), lambda c: (c, xbc)), _spec((halo, XBC), lambda c: (jnp.maximum(c * (Q // halo) - 1, 0), xbc)),
                  _spec((Q, 256), lambda c: (c, 0)), wide, _spec((4, XBC), lambda c: (0, 0)), _spec((1, XBC), lambda c: (0, 0)),
                  par, par, par, _spec((1, SSD_W), lambda c: (0, 0))],
        out_specs=[wide, wide, _spec((2, 1, 4, 128, 128), lambda c: (0, c, 0, 0, 0))],
        out_shape=[jax.ShapeDtypeStruct((T, SSD_W), BF), jax.ShapeDtypeStruct((T, SSD_W), F32),
                   jax.ShapeDtypeStruct((2, NCH, 4, 128, 128), F32)],
        scratch_shapes=[pltpu.VMEM((2, 4, 128, 128), F32), pltpu.VMEM((Q + HALO, XBC), F32), pltpu.VMEM((Q, XBC), F32)],
        compiler_params=_params(), name="ssd_fwd")(proj, proj, dt_raw, proj, conv_w, conv_b, dt_bias2, a_log2, d2, norm_w)


def ssd_bwd(dyn, proj, dt_raw, conv_w, conv_b, y_pre, h_prev, dt_bias2, a_log2, d2, norm_w):
    def body(dyn_all, raw_ref, halo_ref, dt_all, z_all, y_all, hp_all, cw_ref, cb_ref, bias_all, alog_all, d_all, nw_all,
             dz_all, dxbc_ref, ddt_all, dpar_all, dnw_all, dcw_ref, dcb_ref, dh_all, acc_all, pad_scr, act_scr, dsilu_scr, dact_scr, dpad_scr):
        @pl.when(pl.program_id(0) == 0)
        def _():
            dh_all[...] = jnp.zeros_like(dh_all)
            acc_all[...] = jnp.zeros_like(acc_all)
            dnw_all[...] = jnp.zeros_like(dnw_all)
            dcw_ref[...] = jnp.zeros_like(dcw_ref)
            dcb_ref[...] = jnp.zeros_like(dcb_ref)
            dpad_scr[Q:Q + HALO, :] = jnp.zeros((HALO, XBC), F32)

        back = _conv_window(raw_ref, halo_ref, pad_scr)
        pre = _conv(back, cw_ref[...], cb_ref[...])
        sg = _sigmoid(pre)
        act_scr[...] = pre * sg
        dsilu_scr[...] = sg * (1.0 + pre * (1.0 - sg))
        for g in range(2):
            wide, thin = slice(512 * g, 512 * g + 512), slice(128 * g, 128 * g + 128)
            xs, bs, cs = _xbc_cols(g)
            group(dyn_all.at[:, wide], act_scr.at[:, xs], act_scr.at[:, bs], act_scr.at[:, cs], dt_all.at[:, thin], z_all.at[:, wide],
                  y_all.at[:, wide], hp_all.at[g, 0], bias_all.at[g], alog_all.at[g], d_all.at[g], nw_all.at[:, wide],
                  dz_all.at[:, wide], dact_scr.at[:, xs], dact_scr.at[:, bs], dact_scr.at[:, cs], ddt_all.at[:, thin], dpar_all.at[g],
                  dnw_all.at[:, wide], dh_all.at[g], acc_all.at[g])
        dpre = dact_scr[...] * dsilu_scr[...]
        dcw, dcb = _conv_bwd_w(dpre, back)
        dcw_ref[...] += dcw
        dcb_ref[...] += dcb
        dpad_scr[0:Q, :] = dpre
        win = dpad_scr[...]
        dxbc_ref[...] = _conv_bwd_x(lambda s: win[:Q, :] if s == 0 else pltpu.roll(win, Q + HALO - s, axis=0)[:Q, :], cw_ref[...]).astype(BF)
        dpad_scr[Q:Q + HALO, :] = dpre[0:HALO, :]

    def group(dyn_ref, x_ref, b_ref, c_ref, dt_ref, z_ref, y_ref, hp_ref, bias_ref, alog_ref, d_ref, nw_ref,
              dz_ref, dx_ref, db_ref, dc_ref, ddt_ref, dpar_ref, dnw_ref, dh_scr, acc_scr):
        ci = pl.program_id(0)
        bias = bias_ref[...]
        a_neg = -jnp.exp(alog_ref[...])
        dsk = d_ref[...]
        cm = _ssd_chunk_common((NCH - 1 - ci) * Q, dt_ref, b_ref, c_ref, bias, a_neg)
        lane, sub = cm["lane"], cm["sub"]
        lane_even = lane < 64
        cc_t = c_ref[:, :].T.astype(BF)
        cb_t = lax.dot_general(cm["bc"], cm["cc"], NT_DIMS, preferred_element_type=F32)
        zc = z_ref[:, :].astype(F32)
        yc = y_ref[:, :]
        sg = _sigmoid(zc)
        sz = zc * sg
        dgated, dnw = _rms_bwd(dyn_ref[:, :], yc * sz, nw_ref[...])
        dnw_ref[...] += jnp.sum(dnw, axis=0, keepdims=True)
        dz_ref[:, :] = (dgated * yc * (sg * (1.0 + zc * (1.0 - sg)))).astype(BF)
        dy_all = dgated * sz
        dcb = jnp.zeros((Q, Q), F32)
        dcb_t = jnp.zeros((Q, Q), F32)
        db_acc = jnp.zeros((Q, Q), F32)
        dc_acc = jnp.zeros((Q, Q), F32)
        dcs = jnp.zeros((Q, Q), F32)
        ddt = jnp.zeros((Q, Q), F32)
        for p in range(4):
            je, jo = 2 * p, 2 * p + 1
            xp = x_ref[:, 128 * p:128 * p + 128]
            dy = dy_all[:, 128 * p:128 * p + 128]
            dt_p = _pair(lane_even, cm["dt"], je)
            xdt = xp * dt_p
            xdt_b = xdt.astype(BF)
            dy_b = dy.astype(BF)
            zero = jnp.zeros_like(dy_b)
            hp = hp_ref[p]
            hp_b = hp.astype(BF)
            dh = dh_scr[p]
            dh_b = dh.astype(BF)
            acc_scr[p:p + 1, :] += jnp.sum(dy * xp, axis=0, keepdims=True)
            dxp = dy * _pair_row(lane_even, dsk, je)
            e_p = _pair(lane_even, cm["ecs"], je)
            g_p = jnp.dot(cm["cc"], hp_b, preferred_element_type=F32)
            dg_b = (dy * e_p).astype(BF)
            de = dy * g_p * e_p
            dc_acc = dc_acc + lax.dot_general(dg_b, hp_b, NT_DIMS, preferred_element_type=F32)
            dh_in = jnp.dot(cc_t, dg_b, preferred_element_type=F32)
            ds_p = _pair(lane_even, cm["dsm"], je)
            r_p = jnp.dot(cm["bc"], dh_b, preferred_element_type=F32)
            dxdt = r_p * ds_p
            tt = r_p * xdt * ds_p
            db_acc = db_acc + lax.dot_general((xdt * ds_p).astype(BF), dh_b, NT_DIMS, preferred_element_type=F32)
            dgam_m = jnp.sum(dh * hp, axis=0, keepdims=True)
            for j, even in ((je, True), (jo, False)):
                sel = lane_even if even else jnp.logical_not(lane_even)
                dy_j = jnp.where(sel, dy_b, zero)
                l_j = _head_decay(cm, j)
                l_jt = _head_decay_t(cm, j)
                m_j = cm["cb"] * l_j
                m_jt = cb_t * l_jt
                dm = lax.dot_general(dy_j, xdt_b, NT_DIMS, preferred_element_type=F32)
                dm_t = lax.dot_general(xdt_b, dy_j, NT_DIMS, preferred_element_type=F32)
                dxdt = dxdt + jnp.dot(m_jt.astype(BF), dy_j, preferred_element_type=F32)
                dcb = dcb + dm * l_j
                dcb_t = dcb_t + dm_t * l_jt
                t_j = jnp.where(sel, tt, 0.0)
                col = jnp.sum(dm * m_j - dm_t * m_jt + (jnp.where(sel, de, 0.0) - t_j), axis=1, keepdims=True)
                gam_j = cm["gam"][:, j:j + 1]
                last = (jnp.sum(jnp.sum(t_j, axis=0, keepdims=True), axis=1, keepdims=True)
                        + jnp.sum(jnp.where(sel[0:1, :], dgam_m, 0.0), axis=1, keepdims=True) * gam_j)
                col = col + jnp.where(sub[:, 0:1] == Q - 1, last, 0.0)
                dcs = dcs + jnp.where(lane == j, col, 0.0)
            dh_scr[p] = dh_in + dh * _pair_row(lane_even, cm["gam"], je)
            dx_ref[:, 128 * p:128 * p + 128] = dxp + dxdt * dt_p
            dd = dxdt * xp
            ddt = ddt + jnp.where(lane == je, jnp.sum(jnp.where(lane_even, dd, 0.0), axis=1, keepdims=True), 0.0)
            ddt = ddt + jnp.where(lane == jo, jnp.sum(jnp.where(lane_even, 0.0, dd), axis=1, keepdims=True), 0.0)
        dc_ref[:, :] = dc_acc + jnp.dot(dcb.astype(BF), cm["bc"], preferred_element_type=F32)
        db_ref[:, :] = db_acc + jnp.dot(dcb_t.astype(BF), cm["cc"], preferred_element_type=F32)
        tri_t = (sub <= lane).astype(F32)
        dd_a = jnp.dot(tri_t, dcs, precision=lax.Precision.HIGHEST, preferred_element_type=F32)
        ddt = ddt + dd_a * a_neg
        acc_scr[5:6, :] += jnp.sum(dd_a * cm["dt"], axis=0, keepdims=True)
        draw = jnp.where(cm["live"], ddt * _sigmoid_gate(cm["dtr"] + bias), 0.0)
        acc_scr[4:5, :] += jnp.sum(draw, axis=0, keepdims=True)
        ddt_ref[:, :] = draw.astype(BF)

        @pl.when(ci == NCH - 1)
        def _():
            lane1 = _lanes((1, 128))
            dd = jnp.zeros((1, 128), F32)
            for p in range(4):
                row = acc_scr[p:p + 1, :]
                dd = dd + jnp.where(lane1 == 2 * p, jnp.sum(jnp.where(lane1 < 64, row, 0.0), axis=1, keepdims=True), 0.0)
                dd = dd + jnp.where(lane1 == 2 * p + 1, jnp.sum(jnp.where(lane1 < 64, 0.0, row), axis=1, keepdims=True), 0.0)
            dpar_ref[...] = jnp.concatenate([acc_scr[4:5, :], acc_scr[5:6, :] * a_neg, dd, jnp.zeros((5, 128), F32)], axis=0)

    par = _spec((2, 1, 128), lambda c: (0, 0, 0))
    wide = _spec((Q, SSD_W), lambda c: (NCH - 1 - c, 0))
    thin = _spec((Q, 256), lambda c: (NCH - 1 - c, 0))
    vec = _spec((1, SSD_W), lambda c: (0, 0))
    xbc = PXBC // XBC
    halo = 2 * HALO
    chunk = pltpu.VMEM((Q, XBC), F32)
    padded = pltpu.VMEM((Q + HALO, XBC), F32)
    return pl.pallas_call(
        body, grid=(NCH,),
        in_specs=[wide, _spec((Q, XBC), lambda c: (NCH - 1 - c, xbc)),
                  _spec((halo, XBC), lambda c: (jnp.maximum((NCH - 1 - c) * (Q // halo) - 1, 0), xbc)), thin, wide, wide,
                  _spec((2, 1, 4, 128, 128), lambda c: (0, NCH - 1 - c, 0, 0, 0)), _spec((4, XBC), lambda c: (0, 0)),
                  _spec((1, XBC), lambda c: (0, 0)), par, par, par, vec],
        out_specs=[wide, _spec((Q, XBC), lambda c: (NCH - 1 - c, 0)), thin, _spec((2, 8, 128), lambda c: (0, 0, 0)), vec,
                   _spec((4, XBC), lambda c: (0, 0)), _spec((1, XBC), lambda c: (0, 0))],
        out_shape=[jax.ShapeDtypeStruct((T, SSD_W), BF), jax.ShapeDtypeStruct((T, XBC), BF), jax.ShapeDtypeStruct((T, 256), BF),
                   jax.ShapeDtypeStruct((2, 8, 128), F32), jax.ShapeDtypeStruct((1, SSD_W), F32), jax.ShapeDtypeStruct((4, XBC), F32),
                   jax.ShapeDtypeStruct((1, XBC), F32)],
        scratch_shapes=[pltpu.VMEM((2, 4, 128, 128), F32), pltpu.VMEM((2, 8, 128), F32), padded, chunk, chunk, chunk, padded],
        compiler_params=_params(), name="ssd_bwd")(dyn, proj, proj, dt_raw, proj, y_pre, h_prev, conv_w, conv_b, dt_bias2, a_log2, d2, norm_w)


def _lru_gates(back, cw, cb, wa, ba, wx, bx, lam):
    xr = _conv(back, cw, cb)
    xr_b = xr.astype(BF)
    r = _sigmoid_gate(jnp.dot(xr_b, wa, preferred_element_type=F32) + ba)
    i = _sigmoid_gate(jnp.dot(xr_b, wx, preferred_element_type=F32) + bx)
    sp = _softplus(-lam)
    la = (-LRU_C) * r * sp
    a = jnp.exp(la)
    mult2 = -jnp.tanh(la) * (a * a + 1.0)
    return xr, xr_b, r, i, sp, a, jnp.sqrt(mult2), mult2


SEG_LEN = 68
SEGS = T // SEG_LEN


def _seg_rows(j, k, off=0):
    return pl.ds(off + j * 8 * SEG_LEN + k, 8, stride=SEG_LEN)


def _segmented_scan(mul_ref, mul_row0, add_ref, out_ref, loc_scr, prod_scr, carry_scr, reverse):
    groups = SEGS // 8
    off = mul_row0 + (1 if reverse else 0)

    def local(i, carry):
        k = SEG_LEN - 1 - i if reverse else i
        new = []
        for j in range(groups):
            h, p = carry[2 * j], carry[2 * j + 1]
            m = mul_ref[_seg_rows(j, k, off), :]
            h = m * h + add_ref[_seg_rows(j, k), :]
            p = m * p
            loc_scr[_seg_rows(j, k), :] = h
            prod_scr[_seg_rows(j, k), :] = p
            new += [h, p]
        return tuple(new)

    lax.fori_loop(0, SEG_LEN, local, (jnp.zeros((8, 128), F32), jnp.ones((8, 128), F32)) * groups)

    def chain(i, c):
        s = SEGS - 1 - i if reverse else i
        carry_scr[pl.ds(s, 1), :] = c
        edge = s * SEG_LEN + (0 if reverse else SEG_LEN - 1)
        return loc_scr[pl.ds(edge, 1), :] + prod_scr[pl.ds(edge, 1), :] * c

    lax.fori_loop(0, SEGS, chain, jnp.zeros((1, 128), F32))

    def fold(k, carry):
        for j in range(groups):
            rows = _seg_rows(j, k)
            out_ref[rows, :] = loc_scr[rows, :] + prod_scr[rows, :] * carry_scr[8 * j:8 * j + 8, :]
        return carry

    lax.fori_loop(0, SEG_LEN, fold, 0)


def lru_fwd(proj, cw, cb, wa2, ba, wx2, bx, lam):
    def body(x_ref, cw_ref, cb_ref, wa_ref, ba_ref, wx_ref, bx_ref, lam_ref, h_ref, a_ref, xpad, u_scr, loc_scr, prod_scr, carry_scr):
        _fill_padded(xpad, x_ref)

        def chunk(r0):
            xr, _, _, i, _, a, mult, _ = _lru_gates(_back(xpad, r0), cw_ref[...], cb_ref[...], wa_ref[0], ba_ref[...], wx_ref[0], bx_ref[...],
                                                 lam_ref[...])
            a_ref[pl.ds(r0, Q), :] = a
            u_scr[pl.ds(r0, Q), :] = jnp.where(_rows(a.shape, r0) >= NPAD, mult * (i * xr), 0.0)

        _chunks(chunk, unrolled=True)
        _segmented_scan(a_ref, 0, u_scr, h_ref, loc_scr, prod_scr, carry_scr, reverse=False)

    c0 = PXL // 128
    vec = _spec((1, 128), lambda c: (0, c))
    mat = _spec((1, 128, 128), lambda c: (c, 0, 0))
    seq = pltpu.VMEM((T, 128), F32)
    return pl.pallas_call(
        body, grid=(8,),
        in_specs=[_spec((T, 128), lambda c: (0, c0 + c)), _spec((4, 128), lambda c: (0, c)), vec, mat, vec, mat, vec, vec],
        out_specs=[_spec((T, 128), lambda c: (0, c)), _spec((T, 128), lambda c: (0, c))],
        out_shape=[jax.ShapeDtypeStruct((T, LRU_W), F32), jax.ShapeDtypeStruct((T, LRU_W), F32)],
        scratch_shapes=[pltpu.VMEM((T + 2 * HALO, 128), F32), seq, seq, seq, pltpu.VMEM((SEGS, 128), F32)],
        compiler_params=_params(), name="lru_fwd")(proj, cw, cb, wa2, ba, wx2, bx, lam)


def lru_bwd(dh_out, a, hseq, proj, cw, cb, wa2, ba, wx2, bx, lam):
    def body(d_ref, a_ref, h_ref, x_ref, cw_ref, cb_ref, wa_ref, ba_ref, wx_ref, bx_ref, lam_ref,
             dx_ref, dcw_ref, dcb_ref, dwa_ref, dba_ref, dwx_ref, dbx_ref, dlam_ref, xpad, hpad, dpad, dh_ref, loc_scr, prod_scr, carry_scr):
        _fill_padded(dpad, a_ref)
        _segmented_scan(dpad, HALO, d_ref, dh_ref, loc_scr, prod_scr, carry_scr, reverse=True)
        _fill_padded(xpad, x_ref)
        _fill_padded(hpad, h_ref)
        dpad[0:HALO, :] = jnp.zeros((HALO, 128), F32)
        dpad[T + HALO:T + 2 * HALO, :] = jnp.zeros((HALO, 128), F32)
        for ref in (dcw_ref, dcb_ref, dwa_ref, dba_ref, dwx_ref, dbx_ref, dlam_ref):
            ref[...] = jnp.zeros_like(ref)
        lam = lam_ref[...]

        def first(r0):
            back = _back(xpad, r0)
            xr, xr_b, r, i, sp, a, mult, mult2 = _lru_gates(back, cw_ref[...], cb_ref[...], wa_ref[0], ba_ref[...], wx_ref[0], bx_ref[...], lam)
            dh = dh_ref[pl.ds(r0, Q), :]
            da = dh * _back(hpad, r0)(1)
            du = jnp.where(_rows(dh.shape, r0) >= NPAD, dh, 0.0)
            dmult = du * (i * xr)
            di = du * (mult * xr)
            dxr = du * (mult * i)
            dla = da * a - dmult * (a * a) * lax.rsqrt(mult2)
            dr = dla * ((-LRU_C) * sp)
            dlam_ref[...] += jnp.sum(dla * ((-LRU_C) * r), axis=0, keepdims=True)
            dpr = dr * r * (1.0 - r)
            dpi = di * i * (1.0 - i)
            dba_ref[...] += jnp.sum(dpr, axis=0, keepdims=True)
            dbx_ref[...] += jnp.sum(dpi, axis=0, keepdims=True)
            dpr_b = dpr.astype(BF)
            dpi_b = dpi.astype(BF)
            dxr = (dxr + lax.dot_general(dpr_b, wa_ref[0], NT_DIMS, preferred_element_type=F32)
                   + lax.dot_general(dpi_b, wx_ref[0], NT_DIMS, preferred_element_type=F32))
            dwa_ref[0] += lax.dot_general(xr_b, dpr_b, TN_DIMS, preferred_element_type=F32)
            dwx_ref[0] += lax.dot_general(xr_b, dpi_b, TN_DIMS, preferred_element_type=F32)
            dpad[pl.ds(r0 + HALO, Q), :] = dxr
            dcw, dcb = _conv_bwd_w(dxr, back)
            dcw_ref[...] += dcw
            dcb_ref[...] += dcb

        _chunks(first, unrolled=True)
        dlam_ref[...] = -dlam_ref[...] * _sigmoid_gate(-lam)

        def second(r0):
            dx_ref[pl.ds(r0, Q), :] = _conv_bwd_x(_ahead(dpad, r0), cw_ref[...]).astype(BF)

        _chunks(second)

    c0 = PXL // 128
    vec = _spec((1, 128), lambda c: (0, c))
    mat = _spec((1, 128, 128), lambda c: (c, 0, 0))
    col = _spec((T, 128), lambda c: (0, c))
    vshape = jax.ShapeDtypeStruct((1, LRU_W), F32)
    mshape = jax.ShapeDtypeStruct((8, 128, 128), F32)
    pad = pltpu.VMEM((T + 2 * HALO, 128), F32)
    seq = pltpu.VMEM((T, 128), F32)
    return pl.pallas_call(
        body, grid=(8,),
        in_specs=[col, col, col, _spec((T, 128), lambda c: (0, c0 + c)), _spec((4, 128), lambda c: (0, c)), vec, mat, vec, mat, vec, vec],
        out_specs=[col, _spec((4, 128), lambda c: (0, c)), vec, mat, vec, mat, vec, vec],
        out_shape=[jax.ShapeDtypeStruct((T, LRU_W), BF), jax.ShapeDtypeStruct((4, LRU_W), F32), vshape, mshape, vshape, mshape, vshape, vshape],
        scratch_shapes=[pad, pad, pad, seq, seq, seq, pltpu.VMEM((SEGS, 128), F32)],
        compiler_params=_params(), name="lru_bwd")(dh_out, a, hseq, proj, cw, cb, wa2, ba, wx2, bx, lam)


FF_TILE = 256
FF_TILE_ROWS = list(range(0, D_FF, FF_TILE))


def gate_up(h1, wn, w_gate, w_up):
    def body(h_ref, wn_ref, wg_hbm, wu_hbm, gt_ref, up_ref, act_ref, u_ref, wg_ref, wu_ref, wg_sems, wu_sems):
        gate_ready = _arriving(wg_hbm, wg_ref, wg_sems, FF_TILE_ROWS, FF_TILE)
        up_ready = _arriving(wu_hbm, wu_ref, wu_sems, FF_TILE_ROWS, FF_TILE)
        for r in (0, HALF):
            u_ref[r:r + HALF, :] = _rms(h_ref[r:r + HALF, :], wn_ref[...]).astype(BF)

        def tile(c0):
            cols = pl.ds(c0, FF_TILE)
            gate_ready(c0 // FF_TILE)
            up_ready(c0 // FF_TILE)
            gt = lax.dot_general(u_ref[...], wg_ref[cols, :], NT_DIMS, preferred_element_type=F32)
            up = lax.dot_general(u_ref[...], wu_ref[cols, :], NT_DIMS, preferred_element_type=F32)
            gt_ref[:, cols] = gt.astype(BF)
            up_ref[:, cols] = up.astype(BF)
            act_ref[:, cols] = (gt * _sigmoid(gt) * up).astype(BF)

        _col_tiles(D_FF, FF_TILE, tile)

    big = jax.ShapeDtypeStruct((T, D_FF), BF)
    return pl.pallas_call(
        body, grid=(T // RC,), in_specs=[_rows_spec(D), _vec(D), ANY_SPEC, ANY_SPEC],
        out_specs=[_rows_spec(D_FF), _rows_spec(D_FF), _rows_spec(D_FF), _rows_spec(D)],
        out_shape=[big, big, big, jax.ShapeDtypeStruct((T, D), BF)],
        scratch_shapes=[pltpu.VMEM((D_FF, D), BF)] * 2 + [pltpu.SemaphoreType.DMA((len(FF_TILE_ROWS),))] * 2,
        compiler_params=_params(), name="gate_up")(h1, wn, w_gate, w_up)


def down_loss(act, w_down, h1, target, wf):
    first = NPAD + N_META

    def body(a_ref, w_ref, r_ref, t_hbm, wf_ref, d_ref, db_ref, l_ref, dw_ref, h_scr, t_ref, t_sem):
        i = pl.program_id(0)
        _zero_at_first(l_ref, dw_ref)
        head = pltpu.make_async_copy(t_hbm.at[pl.ds(0, RC - first)], t_ref.at[pl.ds(first, RC - first)], t_sem)
        rest = pltpu.make_async_copy(t_hbm.at[pl.ds(pl.multiple_of(jnp.maximum(i * RC - first, 0), 32), RC)], t_ref, t_sem)

        @pl.when(i == 0)
        def _():
            t_ref[0:first, :] = jnp.zeros((first, D), F32)
            head.start()

        @pl.when(i > 0)
        def _():
            rest.start()

        def tile(c0):
            cols = pl.ds(c0, 512)
            h_scr[:, cols] = r_ref[:, cols] + jnp.dot(a_ref[...], w_ref[:, cols], preferred_element_type=F32)

        _col_tiles(D, 512, tile)

        @pl.when(i == 0)
        def _():
            head.wait()

        @pl.when(i > 0)
        def _():
            rest.wait()

        for r in (0, HALF):
            h = h_scr[r:r + HALF, :]
            live = _rows((HALF, D), i * RC + r) >= first
            err = jnp.where(live, _rms(h, wf_ref[...]) - t_ref[r:r + HALF, :], 0.0)
            l_ref[...] += 0.5 * jnp.sum(jnp.sum(err * err, axis=1, keepdims=True) * (1.0 / D), axis=0, keepdims=True)
            dh, dw = _rms_bwd(err * (1.0 / D), h, wf_ref[...])
            dw_ref[...] += jnp.sum(dw, axis=0, keepdims=True)
            d_ref[r:r + HALF, :] = dh
            db_ref[r:r + HALF, :] = dh.astype(BF)

    return pl.pallas_call(
        body, grid=(T // RC,),
        in_specs=[_rows_spec(D_FF), _whole((D_FF, D)), _rows_spec(D), pl.BlockSpec(memory_space=pl.ANY), _vec(D)],
        out_specs=[_rows_spec(D), _rows_spec(D), _spec((1, 128), lambda i: (0, 0)), _vec(D)],
        out_shape=[jax.ShapeDtypeStruct((T, D), F32), jax.ShapeDtypeStruct((T, D), BF), jax.ShapeDtypeStruct((1, 128), F32),
                   jax.ShapeDtypeStruct((1, D), F32)],
        scratch_shapes=[pltpu.VMEM((RC, D), F32), pltpu.VMEM((RC, D), F32), pltpu.SemaphoreType.DMA],
        compiler_params=_params(), name="down_loss")(act, w_down, h1, target, wf)


def swiglu_bwd(dh2_b, w_down, gt, up, act, u2):
    tn = 256

    def body(d_hbm, u_hbm, w_ref, gt_ref, up_ref, act_ref, dg_ref, du_ref, gd_ref, gg_ref, gu_ref, d_ref, u_ref, d_sems, u_sems):
        chunks = list(range(0, T, RC))
        d_ready = _arriving(d_hbm, d_ref, d_sems, chunks, RC)
        u_ready = _arriving(u_hbm, u_ref, u_sems, chunks, RC)

        def rows(r0):
            part = pl.ds(r0, RC)
            d_ready(r0 // RC)
            dact = lax.dot_general(d_ref[part, :], w_ref[...], NT_DIMS, preferred_element_type=F32)
            gt_ = gt_ref[part, :].astype(F32)
            up_ = up_ref[part, :].astype(F32)
            sg = _sigmoid(gt_)
            dg_ref[part, :] = (dact * up_ * (sg * (1.0 + gt_ * (1.0 - sg)))).astype(BF)
            du_ref[part, :] = (dact * (gt_ * sg)).astype(BF)

        _col_tiles(T, RC, rows)
        for k in range(len(chunks)):
            u_ready(k)
        gd_ref[...] = lax.dot_general(act_ref[...], d_ref[...], TN_DIMS, preferred_element_type=F32).astype(BF)
        gg_ref[...] = lax.dot_general(dg_ref[...], u_ref[...], TN_DIMS, preferred_element_type=F32).astype(BF)
        gu_ref[...] = lax.dot_general(du_ref[...], u_ref[...], TN_DIMS, preferred_element_type=F32).astype(BF)

    cols = _spec((T, tn), lambda j: (0, j))
    wrow = _spec((tn, D), lambda j: (j, 0))
    big = jax.ShapeDtypeStruct((T, D_FF), BF)
    grad = jax.ShapeDtypeStruct((D_FF, D), BF)
    return pl.pallas_call(
        body, grid=(D_FF // tn,), in_specs=[ANY_SPEC, ANY_SPEC, wrow, cols, cols, cols],
        out_specs=[cols, cols, wrow, wrow, wrow], out_shape=[big, big, grad, grad, grad],
        scratch_shapes=[pltpu.VMEM((T, D), BF)] * 2 + [pltpu.SemaphoreType.DMA((T // RC,))] * 2,
        compiler_params=_params(), name="swiglu_bwd")(dh2_b, u2, w_down, gt, up, act)


def gate_up_bwd(dgt, dup, w_gate, w_up, h1, wn, dh2):
    def body(dg_ref, du_ref, wg_hbm, wu_hbm, h_ref, wn_ref, r_ref, d_ref, db_ref, dw_ref, du_scr, wg_ref, wu_ref, wg_sems, wu_sems):
        gate_ready = _arriving(wg_hbm, wg_ref, wg_sems, FF_TILE_ROWS, FF_TILE)
        up_ready = _arriving(wu_hbm, wu_ref, wu_sems, FF_TILE_ROWS, FF_TILE)
        _zero_at_first(dw_ref)

        du_scr[...] = jnp.zeros_like(du_scr)

        def tile(c0):
            k = pl.ds(c0, FF_TILE)
            gate_ready(c0 // FF_TILE)
            up_ready(c0 // FF_TILE)
            du_scr[...] += (jnp.dot(dg_ref[:, k], wg_ref[k, :], preferred_element_type=F32)
                            + jnp.dot(du_ref[:, k], wu_ref[k, :], preferred_element_type=F32))

        _col_tiles(D_FF, FF_TILE, tile)
        for r in (0, HALF):
            dh, dw = _rms_bwd(du_scr[r:r + HALF, :], h_ref[r:r + HALF, :], wn_ref[...])
            dw_ref[...] += jnp.sum(dw, axis=0, keepdims=True)
            dh = dh + r_ref[r:r + HALF, :]
            d_ref[r:r + HALF, :] = dh
            db_ref[r:r + HALF, :] = dh.astype(BF)

    return pl.pallas_call(
        body, grid=(T // RC,),
        in_specs=[_rows_spec(D_FF), _rows_spec(D_FF), ANY_SPEC, ANY_SPEC, _rows_spec(D), _vec(D), _rows_spec(D)],
        out_specs=[_rows_spec(D), _rows_spec(D), _vec(D)],
        out_shape=[jax.ShapeDtypeStruct((T, D), F32), jax.ShapeDtypeStruct((T, D), BF), jax.ShapeDtypeStruct((1, D), F32)],
        scratch_shapes=[pltpu.VMEM((RC, D), F32)] + [pltpu.VMEM((D_FF, D), BF)] * 2 + [pltpu.SemaphoreType.DMA((len(FF_TILE_ROWS),))] * 2,
        compiler_params=_params(), name="gate_up_bwd")(dgt, dup, w_gate, w_up, h1, wn, dh2)


def _adamw(w, g, m, v):
    m = ADAM_B1 * m + (1.0 - ADAM_B1) * g
    v = ADAM_B2 * v + (1.0 - ADAM_B2) * (g * g)
    m_hat = m / (1.0 - ADAM_B1 ** ADAM_STEP)
    v_hat = v / (1.0 - ADAM_B2 ** ADAM_STEP)
    delta = -ADAM_LR * (m_hat / (jnp.sqrt(v_hat) + ADAM_EPS) + ADAM_WD * w)
    return delta, m, v


def adamw_shards(name, recvs, ws, ms, vs):
    n = len(ws)

    def body(*refs):
        ins, outs = refs[:4 * n], refs[4 * n:]
        for k in range(n):
            p_ref, w_ref, m_ref, v_ref = ins[k], ins[n + k], ins[2 * n + k], ins[3 * n + k]
            g = p_ref[0].astype(F32)
            for s in range(1, 8):
                g = g + p_ref[s].astype(F32)
            outs[4 * k][...] = g
            outs[4 * k + 1][...], outs[4 * k + 2][...], outs[4 * k + 3][...] = _adamw(w_ref[...], g, m_ref[...], v_ref[...])

    tiles = [_spec((w.shape[0] // 2, w.shape[1]), lambda i: (i, 0)) for w in ws]
    recv_tiles = [_spec((8, w.shape[0] // 2, w.shape[1]), lambda i: (0, i, 0)) for w in ws]
    res = pl.pallas_call(
        body, grid=(2,), in_specs=recv_tiles + tiles * 3,
        out_specs=[t for t in tiles for _ in range(4)],
        out_shape=[jax.ShapeDtypeStruct(w.shape, F32) for w in ws for _ in range(4)],
        compiler_params=_params(), name=name)(*recvs, *ws, *ms, *vs)
    return [list(res[4 * k:4 * k + 4]) for k in range(n)]


def adamw_w_in(recv, w, m, v):
    rows = 34
    per_row = D // 128

    def body(p_ref, w_ref, m_ref, v_ref, g_ref, d_ref, mo_ref, vo_ref):
        def chunk(c, carry):
            lines = pl.ds(pl.multiple_of(c * per_row * rows, 16), per_row * rows)
            g = p_ref[0, lines, :].astype(F32)
            for s in range(1, 8):
                g = g + p_ref[s, lines, :].astype(F32)
            g = g.reshape(rows, per_row, 128)
            part = pl.ds(c * rows, rows)
            g_ref[part] = g
            d_ref[part], mo_ref[part], vo_ref[part] = _adamw(w_ref[part], g, m_ref[part], v_ref[part])
            return carry

        lax.fori_loop(0, w.shape[0] // rows, chunk, 0)

    shape = jax.ShapeDtypeStruct(w.shape, F32)
    return pl.pallas_call(body, out_shape=[shape] * 4, compiler_params=_params(0), name="adamw_w_in")(recv, w, m, v)


def sum_slabs(recv):
    def body(p_ref, o_ref):
        g = p_ref[0]
        for s in range(1, 8):
            g = g + p_ref[s]
        o_ref[...] = g

    return pl.pallas_call(body, out_shape=jax.ShapeDtypeStruct(recv.shape[1:], F32), compiler_params=_params(0), name="sum_slabs")(recv)


SIMPLE = [("norm1_w", 1024), ("ssd_conv_b", 1536), ("ssd_dt_bias", 16), ("ssd_a_log", 16), ("ssd_d", 16), ("ssd_norm_w", 1024),
          ("lru_conv_b", 1024), ("lru_ba", 1024), ("lru_bx", 1024), ("lru_lambda", 1024), ("lru_norm_w", 1024), ("norm2_w", 1024),
          ("final_norm_w", 1024)]
SPECIAL = ["lru_wa", "lru_wx", "meta_tokens", "ssd_conv_w", "lru_conv_w"]
SM_ROWS = 176
SM_WA, SM_WX, SM_META, SM_SCW, SM_LCW, SM_LOSS = 14, 78, 142, 158, 166, 170


def _simple_rows():
    rows, r = {}, 0
    for name, n in SIMPLE:
        rows[name] = r
        r += -(-n // 1024)
    return rows


def adamw_small(sm, special_g, ws, ms, vs):
    rows = _simple_rows()
    ns, nx = len(SIMPLE), len(SPECIAL)

    def body(*refs):
        sm_ref = refs[0]
        gx = refs[1:1 + nx]
        wr = refs[1 + nx:1 + nx + ns + nx]
        mr = refs[1 + nx + ns + nx:1 + nx + 2 * (ns + nx)]
        vr = refs[1 + nx + 2 * (ns + nx):1 + nx + 3 * (ns + nx)]
        outs = refs[1 + nx + 3 * (ns + nx):]
        o = 0
        for k, (name, n) in enumerate(SIMPLE):
            r0 = rows[name]
            for c0 in range(0, n, 1024):
                wd = min(1024, n - c0)
                g = sm_ref[r0 + c0 // 1024:r0 + c0 // 1024 + 1, 0:wd]
                sl = (slice(None), slice(c0, c0 + wd))
                d, m2, v2 = _adamw(wr[k][sl], g, mr[k][sl], vr[k][sl])
                outs[o][sl] = g
                outs[o + 1][sl] = d
                outs[o + 2][sl] = m2
                outs[o + 3][sl] = v2
            o += 4
        for k in range(nx):
            d, m2, v2 = _adamw(wr[ns + k][...], gx[k][...], mr[ns + k][...], vr[ns + k][...])
            outs[o][...] = d
            outs[o + 1][...] = m2
            outs[o + 2][...] = v2
            o += 3

    out_shape = []
    for k in range(ns):
        out_shape += [jax.ShapeDtypeStruct(ws[k].shape, F32)] * 4
    for k in range(nx):
        out_shape += [jax.ShapeDtypeStruct(ws[ns + k].shape, F32)] * 3
    return pl.pallas_call(body, out_shape=out_shape, compiler_params=_params(0), name="adamw_small")(sm, *special_g, *ws, *ms, *vs)


def _place():
    return lax.axis_index("x"), lax.axis_index("y"), lax.axis_index("c")


def _index(px, py, pc):
    return 4 * px + 2 * py + pc


def all_gather(name, shards):
    n = len(shards)
    hbm = pl.BlockSpec(memory_space=pl.ANY)

    def body(*refs):
        ins, outs = refs[:n], refs[n:2 * n]
        send_sems, recv_sems, local_sems = refs[2 * n:]
        x, y, c = _place()
        me, sibling = (x, y, c), (x, y, 1 - c)
        chips = [(1 - x, y), (x, 1 - y), (1 - x, 1 - y)]

        def copy(i, k, block, to, src=None):
            dst = outs[i].at[_index(*block)]
            return pltpu.make_async_remote_copy(src_ref=dst if src is None else src, dst_ref=dst, send_sem=send_sems.at[7 * i + k],
                                                recv_sem=recv_sems.at[7 * i + k], device_id=to, device_id_type=MESH)

        mine = [pltpu.make_async_copy(ins[i], outs[i].at[_index(*me)], local_sems.at[i]) for i in range(n)]
        for cp in mine:
            cp.start()
        first = []
        for i in range(n):
            first += [copy(i, 1 + j, me, (*chip, c), src=ins[i]) for j, chip in enumerate(chips)]
            first.append(copy(i, 0, me, sibling, src=ins[i]))
        for cp in first:
            cp.start()
        passed = []
        for i in range(n):
            for j, chip in enumerate(chips):
                copy(i, 1 + j, (*chip, c), me).wait_recv()
                cp = copy(i, 4 + j, (*chip, c), sibling)
                cp.start()
                passed.append(cp)
        for i in range(n):
            copy(i, 0, sibling, me).wait_recv()
            for j, chip in enumerate(chips):
                copy(i, 4 + j, (*chip, 1 - c), me).wait_recv()
        for cp in first + passed:
            cp.wait_send()
        for cp in mine:
            cp.wait()

    return pl.pallas_call(
        body, in_specs=[hbm] * n, out_specs=[hbm] * n,
        out_shape=[jax.ShapeDtypeStruct((8,) + s.shape, s.dtype) for s in shards],
        scratch_shapes=[pltpu.SemaphoreType.DMA((7 * n,)), pltpu.SemaphoreType.DMA((7 * n,)), pltpu.SemaphoreType.DMA((n,))],
        name=name)(*shards)


HBM_SPEC = pl.BlockSpec(memory_space=pltpu.HBM)
SEM_SPEC = pl.BlockSpec(memory_space=pltpu.SEMAPHORE)
EFFECT = pltpu.SideEffectType.DATAFLOW_SIDE_EFFECTING


def _peers(x, y, c):
    return [((1 - x) if k & 4 else x, (1 - y) if k & 2 else y, (1 - c) if k & 1 else c) for k in range(1, 8)]


def _pieces(rows):
    for n in (4, 2):
        if rows % (16 * n) == 0:
            return [(r * (rows // n), rows // n) for r in range(n)]
    return [(0, rows)]


def _peer_copies(src, land, send_sems, recv_sems, k, peer, mine, slab_src):
    block = src.at[_index(*peer)] if slab_src else src
    return [pltpu.make_async_remote_copy(src_ref=block.at[pl.ds(r0, nr)], dst_ref=land.at[mine, pl.ds(r0, nr)], send_sem=send_sems.at[k],
                                         recv_sem=recv_sems.at[k], device_id=peer, device_id_type=MESH)
            for r0, nr in _pieces(block.shape[0])]


def copies_start(name, srcs, slab_src, after):
    n = len(srcs)
    zones = [jax.ShapeDtypeStruct(s.shape if slab_src else (8,) + s.shape, s.dtype) for s in srcs]
    afters = [] if after is None else [after]

    def body(*refs):
        ins, lands = refs[:n], refs[n:2 * n]
        first = 2 * n + len(afters)
        sends, recvs = refs[first:first + n], refs[first + n:first + 2 * n]
        token = refs[-1]
        x, y, c = _place()
        mine = _index(x, y, c)
        for i in range(n):
            per_peer = [_peer_copies(ins[i], lands[i], sends[i], recvs[i], k, peer, mine, slab_src) for k, peer in enumerate(_peers(x, y, c))]
            for piece in zip(*per_peer):
                for cp in piece:
                    cp.start()
        token[...] = jnp.zeros_like(token)

    sem = pltpu.SemaphoreType.DMA((7,))
    res = pl.pallas_call(
        body, name=name,
        out_shape=([sem] * (2 * n) + [pltpu.HBM(s.shape, s.dtype) for s in srcs] + [pltpu.HBM(z.shape, z.dtype) for z in zones]
                   + [jax.ShapeDtypeStruct((8, 128), F32)]),
        in_specs=[HBM_SPEC] * (2 * n) + [pl.BlockSpec(memory_space=pl.ANY)] * len(afters),
        out_specs=[SEM_SPEC] * (2 * n) + [HBM_SPEC] * (2 * n) + [pl.BlockSpec(memory_space=pltpu.VMEM)],
        input_output_aliases={i: 2 * n + i for i in range(2 * n)},
        compiler_params=pltpu.CompilerParams(has_side_effects=EFFECT),
    )(*[pltpu.with_memory_space_constraint(s, pltpu.HBM) for s in srcs],
      *[pltpu.with_memory_space_constraint(lax.empty(z.shape, z.dtype), pltpu.HBM) for z in zones], *afters)
    return [(res[i], res[n + i], res[2 * n + i], res[3 * n + i]) for i in range(n)], res[-1][0:1, 0:1]


def copies_wait(name, started, slab_src, after):
    n = len(started)

    def body(*refs):
        ins, lands = refs[:n], refs[n:2 * n]
        sends, recvs = refs[2 * n:3 * n], refs[3 * n:4 * n]
        x, y, c = _place()
        mine = _index(x, y, c)
        for i in range(n):
            for k, peer in enumerate(_peers(x, y, c)):
                arrival = pltpu.make_async_remote_copy(src_ref=ins[i].at[mine] if slab_src else ins[i], dst_ref=lands[i].at[_index(*peer)],
                                                       send_sem=sends[i].at[k], recv_sem=recvs[i].at[k], device_id=peer, device_id_type=MESH)
                arrival.wait_send()
                arrival.wait_recv()

    srcs = [s[2] for s in started]
    lands = [s[3] for s in started]
    afters = list(after) if isinstance(after, (list, tuple)) else [after]
    res = pl.pallas_call(
        body, name=name,
        out_shape=[pltpu.HBM(s.shape, s.dtype) for s in srcs] + [pltpu.HBM(z.shape, z.dtype) for z in lands],
        in_specs=[HBM_SPEC] * (2 * n) + [SEM_SPEC] * (2 * n) + [pl.BlockSpec(memory_space=pl.ANY)] * len(afters),
        out_specs=[HBM_SPEC] * (2 * n),
        input_output_aliases={i: i for i in range(2 * n)},
        compiler_params=pltpu.CompilerParams(has_side_effects=EFFECT),
    )(*srcs, *lands, *[s[0] for s in started], *[s[1] for s in started], *afters)
    me = _index(*_place())
    own = [lax.dynamic_index_in_dim(s, me, 0, keepdims=True) if slab_src else s[None] for s in res[:n]]
    return [lax.dynamic_update_slice_in_dim(z, o, me, 0) for z, o in zip(res[n:], own)]


WEIGHTS = ["meta_tokens", "norm1_w", "w_in", "ssd_conv_w", "ssd_conv_b", "ssd_dt_bias", "ssd_a_log", "ssd_d", "ssd_norm_w", "lru_conv_w",
           "lru_conv_b", "lru_wa", "lru_ba", "lru_wx", "lru_bx", "lru_lambda", "lru_norm_w", "w_out", "norm2_w", "w_gate", "w_up", "w_down",
           "final_norm_w"]
BIG = ["w_in", "w_out", "w_gate", "w_up", "w_down"]
COLUMN_SHARDED = ["w_in", "w_gate", "w_up"]


def _pair_blocks(w):
    w = w.reshape(8, 2, 64, 64)
    z = jnp.zeros((8, 64, 64), w.dtype)
    return jnp.concatenate([jnp.concatenate([w[:, 0], z], axis=2), jnp.concatenate([z, w[:, 1]], axis=2)], axis=1)


def _unpair_blocks(w2):
    return jnp.stack([w2[:, :64, :64], w2[:, 64:, 64:]], axis=1).reshape(16, 64, 64)


def _per_group(v):
    return jnp.pad(v.reshape(2, 1, 8), ((0, 0), (0, 0), (0, 120)))


def _pad_cols(v, n):
    return jnp.pad(v, ((0, 0), (0, n - v.shape[1])))


def local_step(x, target, meta, ssd_cw, lru_cw, w_in, fetch, send, p):
    z120 = jnp.zeros((120, D), BF)
    w_dt = jnp.concatenate([w_in[2560:2568], z120, w_in[2568:2576], z120], axis=0)
    bias2, alog2, d2 = _per_group(p["ssd_dt_bias"]), _per_group(p["ssd_a_log"]), _per_group(p["ssd_d"])
    wa2 = _pair_blocks(p["lru_wa"]).astype(BF)
    wx2 = _pair_blocks(p["lru_wx"]).astype(BF)
    lru = (lru_cw, p["lru_conv_b"], wa2, p["lru_ba"], wx2, p["lru_bx"], p["lru_lambda"])

    h0 = jnp.concatenate([jnp.zeros((NPAD, D), F32), meta, x], axis=0)
    proj, dt_raw, u1 = in_proj(h0, p["norm1_w"], w_in, w_dt)
    yn_ssd, y_pre, h_prev = ssd_fwd(proj, dt_raw, ssd_cw, p["ssd_conv_b"], bias2, alog2, d2, p["ssd_norm_w"])
    hseq, a = lru_fwd(proj, *lru)
    (w_out,) = fetch(["w_out"], hseq)
    h1, cat = out_proj(yn_ssd, proj, hseq, p["lru_norm_w"], w_out, h0)
    w_gate, w_up = fetch(["w_gate", "w_up"], h1)
    gt, up, act, u2 = gate_up(h1, p["norm2_w"], w_gate, w_up)
    (w_down,) = fetch(["w_down"], act)
    dh2, dh2_b, loss, d_fnw = down_loss(act, w_down, h1, target, p["final_norm_w"])

    dgt, dup, g_down, g_gate, g_up = swiglu_bwd(dh2_b, w_down, gt, up, act, u2)
    sent = send({"w_down": g_down, "w_gate": g_gate, "w_up": g_up})
    dh1, dh1_b, d_n2 = gate_up_bwd(dgt, dup, w_gate, w_up, h1, p["norm2_w"] + sent, dh2)
    sent = send({"w_out": weight_grad("dw_out", cat, dh1_b)})
    dyn, dh_out, dg_b, d_lnw = out_proj_bwd(dh1_b, w_out, proj, hseq, p["lru_norm_w"] + sent)

    dxl_b, d_lcw, d_lcb, dwa2, d_ba, dwx2, d_bx, d_lam = lru_bwd(dh_out, a, hseq, proj, *lru)
    dz_b, dxbc_b, ddt_b, dpar, d_snw, d_scw, d_scb = ssd_bwd(dyn, proj, dt_raw, ssd_cw, p["ssd_conv_b"], y_pre, h_prev, bias2, alog2, d2,
                                                             p["ssd_norm_w"] + sent)
    sent = send({"w_in": in_weight_grad(dz_b, dg_b, dxl_b, dxbc_b, ddt_b, u1)})
    grad_x, d_meta, d_n1 = in_proj_bwd(dz_b, dg_b, dxl_b, dxbc_b, ddt_b, w_in, w_dt, h0, p["norm1_w"] + sent, dh1)
    small = {"norm1_w": d_n1, "ssd_conv_b": d_scb, "ssd_dt_bias": dpar[:, 0, :8].reshape(1, 16), "ssd_a_log": dpar[:, 1, :8].reshape(1, 16),
             "ssd_d": dpar[:, 2, :8].reshape(1, 16), "ssd_norm_w": d_snw, "lru_conv_b": d_lcb, "lru_ba": d_ba, "lru_bx": d_bx,
             "lru_lambda": d_lam, "lru_norm_w": d_lnw, "norm2_w": d_n2, "final_norm_w": d_fnw,
             "lru_wa": _unpair_blocks(dwa2), "lru_wx": _unpair_blocks(dwx2), "meta_tokens": d_meta,
             "ssd_conv_w": d_scw, "lru_conv_w": d_lcw}
    return loss, grad_x, small


def _pack_small(small, loss):
    rows = [_pad_cols(small[name], -(-n // 1024) * 1024).reshape(-1, 1024) for name, n in SIMPLE]
    rows += [small["lru_wa"].reshape(64, 1024), small["lru_wx"].reshape(64, 1024), small["meta_tokens"],
             _pad_cols(small["ssd_conv_w"], 2048).reshape(8, 1024), small["lru_conv_w"], _pad_cols(loss[:, 0:1], 1024)]
    sm = jnp.concatenate(rows, axis=0)
    return jnp.pad(sm, ((0, SM_ROWS - sm.shape[0]), (0, 0)))


def _slabs(g):
    return g.reshape(8, g.shape[0] // 8, g.shape[1])


def _unslab(g):
    return g.reshape(8 * g.shape[1], g.shape[2])


def kernel(x, meta_tokens, norm1_w, w_in, ssd_conv_w, ssd_conv_b, ssd_dt_bias, ssd_a_log, ssd_d, ssd_norm_w, lru_conv_w, lru_conv_b, lru_wa, lru_ba, lru_wx, lru_bx, lru_lambda, lru_norm_w, w_out, norm2_w, w_gate, w_up, w_down, final_norm_w, loss_target, m_meta_tokens, m_norm1_w, m_w_in, m_ssd_conv_w, m_ssd_conv_b, m_ssd_dt_bias, m_ssd_a_log, m_ssd_d, m_ssd_norm_w, m_lru_conv_w, m_lru_conv_b, m_lru_wa, m_lru_ba, m_lru_wx, m_lru_bx, m_lru_lambda, m_lru_norm_w, m_w_out, m_norm2_w, m_w_gate, m_w_up, m_w_down, m_final_norm_w, v_meta_tokens, v_norm1_w, v_w_in, v_ssd_conv_w, v_ssd_conv_b, v_ssd_dt_bias, v_ssd_a_log, v_ssd_d, v_ssd_norm_w, v_lru_conv_w, v_lru_conv_b, v_lru_wa, v_lru_ba, v_lru_wx, v_lru_bx, v_lru_lambda, v_lru_norm_w, v_w_out, v_norm2_w, v_w_gate, v_w_up, v_w_down, v_final_norm_w):
    w = dict(meta_tokens=meta_tokens, norm1_w=norm1_w, w_in=w_in[0], ssd_conv_w=ssd_conv_w[0], ssd_conv_b=ssd_conv_b, ssd_dt_bias=ssd_dt_bias,
             ssd_a_log=ssd_a_log, ssd_d=ssd_d, ssd_norm_w=ssd_norm_w, lru_conv_w=lru_conv_w[0], lru_conv_b=lru_conv_b, lru_wa=lru_wa[0],
             lru_ba=lru_ba, lru_wx=lru_wx[0], lru_bx=lru_bx, lru_lambda=lru_lambda, lru_norm_w=lru_norm_w, w_out=w_out[0], norm2_w=norm2_w,
             w_gate=w_gate[0], w_up=w_up[0], w_down=w_down[0], final_norm_w=final_norm_w.reshape(1, D))
    m = dict(meta_tokens=m_meta_tokens, norm1_w=m_norm1_w, w_in=m_w_in[0], ssd_conv_w=m_ssd_conv_w[0], ssd_conv_b=m_ssd_conv_b,
             ssd_dt_bias=m_ssd_dt_bias, ssd_a_log=m_ssd_a_log, ssd_d=m_ssd_d, ssd_norm_w=m_ssd_norm_w, lru_conv_w=m_lru_conv_w[0],
             lru_conv_b=m_lru_conv_b, lru_wa=m_lru_wa[0], lru_ba=m_lru_ba, lru_wx=m_lru_wx[0], lru_bx=m_lru_bx, lru_lambda=m_lru_lambda,
             lru_norm_w=m_lru_norm_w, w_out=m_w_out[0], norm2_w=m_norm2_w, w_gate=m_w_gate[0], w_up=m_w_up[0], w_down=m_w_down[0],
             final_norm_w=m_final_norm_w.reshape(1, D))
    v = dict(meta_tokens=v_meta_tokens, norm1_w=v_norm1_w, w_in=v_w_in[0], ssd_conv_w=v_ssd_conv_w[0], ssd_conv_b=v_ssd_conv_b,
             ssd_dt_bias=v_ssd_dt_bias, ssd_a_log=v_ssd_a_log, ssd_d=v_ssd_d, ssd_norm_w=v_ssd_norm_w, lru_conv_w=v_lru_conv_w[0],
             lru_conv_b=v_lru_conv_b, lru_wa=v_lru_wa[0], lru_ba=v_lru_ba, lru_wx=v_lru_wx[0], lru_bx=v_lru_bx, lru_lambda=v_lru_lambda,
             lru_norm_w=v_lru_norm_w, w_out=v_w_out[0], norm2_w=v_norm2_w, w_gate=v_w_gate[0], w_up=v_w_up[0], w_down=v_w_down[0],
             final_norm_w=v_final_norm_w.reshape(1, D))
    shapes = dict(meta_tokens=meta_tokens.shape, norm1_w=norm1_w.shape, w_in=w_in.shape, ssd_conv_w=ssd_conv_w.shape,
                  ssd_conv_b=ssd_conv_b.shape, ssd_dt_bias=ssd_dt_bias.shape, ssd_a_log=ssd_a_log.shape, ssd_d=ssd_d.shape,
                  ssd_norm_w=ssd_norm_w.shape, lru_conv_w=lru_conv_w.shape, lru_conv_b=lru_conv_b.shape, lru_wa=lru_wa.shape,
                  lru_ba=lru_ba.shape, lru_wx=lru_wx.shape, lru_bx=lru_bx.shape, lru_lambda=lru_lambda.shape, lru_norm_w=lru_norm_w.shape,
                  w_out=w_out.shape, norm2_w=norm2_w.shape, w_gate=w_gate.shape, w_up=w_up.shape, w_down=w_down.shape,
                  final_norm_w=final_norm_w.shape)
    me = _index(*_place())
    for n in COLUMN_SHARDED:
        w[n], m[n], v[n] = w[n].T, m[n].T, v[n].T

    small_shard = jnp.concatenate([w["meta_tokens"], _pad_cols(w["ssd_conv_w"], 256).reshape(8, 128), w["lru_conv_w"],
                                   jnp.zeros((4, 128), F32)], axis=0)
    g_in, gs = all_gather("gather_w_in", [w["w_in"].astype(BF), small_shard])
    later = ["w_out", "w_gate", "w_up", "w_down"]
    started, behind = copies_start("gather_rest_start", [w[n].astype(BF) for n in later], False, gs)
    started = dict(zip(later, started))
    meta_full = gs[:, 0:16].transpose(1, 0, 2).reshape(N_META, D)
    ssd_cw = gs[:, 16:24].reshape(8, 4, 256)[:, :, :192].transpose(1, 0, 2).reshape(4, XBC)
    lru_cw = gs[:, 24:28].transpose(1, 0, 2).reshape(4, LRU_W)

    def fetch(names, after):
        got = copies_wait("gather_" + names[0] + "_wait", [started[n] for n in names], False, after)
        return [_unslab(g) for g in got]

    in_flight = {}

    def send(grads):
        names = list(grads)
        st, token = copies_start("grads_" + names[0] + "_start", [grads[n] if n == "small" else _slabs(grads[n]) for n in names], True, None)
        in_flight.update(zip(names, st))
        return token

    loss, grad_x, small = local_step(x[0], loss_target[0], meta_full, ssd_cw, lru_cw, _unslab(g_in), fetch, send,
                                     {**w, "norm1_w": w["norm1_w"] + behind})
    send({"small": _pack_small(small, loss).reshape(8, SM_ROWS // 8, 1024)})

    out = {}
    early = ["w_down", "w_gate", "w_up", "w_out"]
    recv = dict(zip(early, copies_wait("grads_early_wait", [in_flight[n] for n in early], True, in_flight["small"][2])))
    for pair in (early[:2], early[2:]):
        done = adamw_shards("adamw_" + pair[0], [recv[n] for n in pair], [w[n] for n in pair], [m[n] for n in pair], [v[n] for n in pair])
        out.update(zip(pair, done))
    recv_in, recv_small = copies_wait("grads_late_wait", [in_flight["w_in"], in_flight["small"]], True, [out[n][0] for n in early])
    def lines(a):
        return jnp.transpose(a.reshape(D // 128, 128, IN_COLS // 8), (2, 0, 1))

    out["w_in"] = [jnp.transpose(o, (1, 2, 0)).reshape(D, IN_COLS // 8) for o in adamw_w_in(recv_in, lines(w_in), lines(m_w_in), lines(v_w_in))]
    for n in ("w_gate", "w_up"):
        out[n] = [o.T for o in out[n]]
    sm = all_gather("gather_small_grads", [sum_slabs(recv_small)])[0].reshape(SM_ROWS, 1024)
    special_g =[sm[SM_WA:SM_WA + 64].reshape(16, 64, 64), sm[SM_WX:SM_WX + 64].reshape(16, 64, 64),
                 lax.dynamic_slice(sm[SM_META:SM_META + 16], (0, 128 * me), (16, 128)),
                 lax.dynamic_slice(sm[SM_SCW:SM_SCW + 8].reshape(4, 2048), (0, 192 * me), (4, 192)),
                 lax.dynamic_slice(sm[SM_LCW:SM_LCW + 4], (0, 128 * me), (4, 128))]
    names = [n for n, _ in SIMPLE] + SPECIAL
    res = adamw_small(sm, special_g, [w[n] for n in names], [m[n] for n in names], [v[n] for n in names])
    for k, (n, _) in enumerate(SIMPLE):
        out[n] = res[4 * k:4 * k + 4]
    for k, n in enumerate(SPECIAL):
        o = 4 * len(SIMPLE) + 3 * k
        out[n] = [special_g[k]] + list(res[o:o + 3])
    loss_total = sm[SM_LOSS, 0]
    flat = [loss_total, grad_x[None]]
    for k in range(4):
        flat += [out[n][k].reshape(shapes[n]) for n in WEIGHTS]
    return tuple(flat)
```

```python
import math

import jax
import jax.numpy as jnp
from jax import lax
from jax.experimental import pallas as pl
from jax.experimental.pallas import tpu as pltpu

F32 = jnp.float32
BF = jnp.bfloat16

D = 1024
SEQ = 2048
N_META = 16
Q = 128
NPAD = 112
T = NPAD + N_META + SEQ
NCH = T // Q
RC = 544
D_FF = 2816
SSD_W = 1024
LRU_W = 1024
XBC = 1536
IN_COLS = 4624
PZ, PG, PXL, PXBC = 0, 1024, 2048, 3072
NP_IN = 4608
EPS = 1e-6
LRU_C = 8.0
VMEM_LIMIT = 56 * 1024 * 1024

ADAM_LR, ADAM_B1, ADAM_B2, ADAM_EPS, ADAM_WD, ADAM_STEP = 0.001, 0.9, 0.999, 1e-08, 0.01, 10

NT_DIMS = (((1,), (1,)), ((), ()))
TN_DIMS = (((0,), (0,)), ((), ()))
MESH = pl.DeviceIdType.MESH


def _params(n_grid=1, limit=VMEM_LIMIT):
    return pltpu.CompilerParams(dimension_semantics=("arbitrary",) * n_grid, vmem_limit_bytes=limit)


def _spec(shape, imap, single=False):
    if single:
        return pl.BlockSpec(shape, imap, pipeline_mode=pl.Buffered(1))
    return pl.BlockSpec(shape, imap)


def _sigmoid(x):
    return 0.5 * jnp.tanh(0.5 * x) + 0.5


def _sigmoid_gate(x):
    return 1.0 / (1.0 + jnp.exp(-x))


def _softplus(x):
    return jnp.maximum(x, 0.0) + jnp.log(1.0 + jnp.exp(-jnp.abs(x)))


def _rms_stats(h):
    return lax.rsqrt(jnp.mean(h * h, axis=-1, keepdims=True) + EPS)


def _rms(h, w):
    return (h * _rms_stats(h)) * w


def _rms_bwd(du, h, w):
    r = _rms_stats(h)
    n = h * r
    dn = du * w
    dh = r * (dn - n * jnp.mean(dn * n, axis=-1, keepdims=True))
    return dh, du * n


_G0 = math.sqrt(2.0 / math.pi)


def _gelu(x):
    return 0.5 * x * (1.0 + jnp.tanh(_G0 * (x + 0.044715 * (x * x * x))))


def _gelu_grad(x):
    t = jnp.tanh(_G0 * (x + 0.044715 * (x * x * x)))
    return 0.5 * (1.0 + t) + 0.5 * x * (1.0 - t * t) * (_G0 * (1.0 + 3.0 * 0.044715 * (x * x)))


def _rows(shape, r0=0):
    return lax.broadcasted_iota(jnp.int32, shape, 0) + r0


def _lanes(shape):
    return lax.broadcasted_iota(jnp.int32, shape, 1)


HALO = 8


def _fill_padded(pad_ref, x_ref):
    pad_ref[0:HALO, :] = jnp.zeros((HALO, pad_ref.shape[1]), F32)
    pad_ref[T + HALO:T + 2 * HALO, :] = jnp.zeros((HALO, pad_ref.shape[1]), F32)

    def step(c, carry):
        r0 = pl.multiple_of(c * Q, Q)
        pad_ref[pl.ds(r0 + HALO, Q), :] = x_ref[pl.ds(r0, Q), :].astype(F32)
        return carry

    lax.fori_loop(0, NCH, step, 0)


def _back(pad_ref, r0):
    win = pad_ref[pl.ds(r0, Q + HALO), :]
    return lambda s: win[HALO:, :] if s == 0 else pltpu.roll(win, s, axis=0)[HALO:, :]


def _ahead(pad_ref, r0):
    win = pad_ref[pl.ds(r0 + HALO, Q + HALO), :]
    return lambda s: win[:Q, :] if s == 0 else pltpu.roll(win, Q + HALO - s, axis=0)[:Q, :]


def _conv(back, w, b):
    y = b + w[3:4, :] * back(0)
    for k in range(3):
        y = y + w[k:k + 1, :] * back(3 - k)
    return y


def _conv_bwd_x(ahead, w):
    dx = w[3:4, :] * ahead(0)
    for k in range(3):
        dx = dx + w[k:k + 1, :] * ahead(3 - k)
    return dx


def _conv_bwd_w(dy, back):
    dws = [jnp.sum(dy * back(3 - k), axis=0, keepdims=True) for k in range(4)]
    return jnp.concatenate(dws, axis=0), jnp.sum(dy, axis=0, keepdims=True)


def _chunks(fn, unrolled=False):
    if unrolled:
        for c in range(NCH):
            fn(c * Q)
        return

    def step(c, carry):
        fn(pl.multiple_of(c * Q, Q))
        return carry

    lax.fori_loop(0, NCH, step, 0)


HALF = RC // 2


def _col_tiles(n, tn, fn):
    def step(j, carry):
        fn(pl.multiple_of(j * tn, tn))
        return carry

    lax.fori_loop(0, n // tn, step, 0)


def _rows_spec(cols, block_col=0):
    return _spec((RC, cols), lambda i: (i, block_col))


def _whole(shape):
    return _spec(shape, lambda i: tuple(0 for _ in shape), single=True)


def _vec(cols):
    return _spec((1, cols), lambda i: (0, 0))


def _zero_at_first(*refs):
    @pl.when(pl.program_id(0) == 0)
    def _():
        for r in refs:
            r[...] = jnp.zeros_like(r)


ANY_SPEC = pl.BlockSpec(memory_space=pl.ANY)


def _arriving(src, dst, sems, starts, rows):
    n, ahead = len(starts), 2
    first = pl.program_id(0) == 0

    def piece(k):
        r0 = starts[0]
        for j in range(1, n):
            r0 = jnp.where(k == j, starts[j], r0)
        at = pl.ds(pl.multiple_of(r0, 16), rows)
        return pltpu.make_async_copy(src.at[at], dst.at[at], sems.at[k])

    @pl.when(first)
    def _():
        for k in range(min(ahead, n)):
            piece(k).start()

    def ready(k):
        k = jnp.asarray(k, jnp.int32)

        @pl.when(first)
        def _():
            piece(k).wait()

            @pl.when(k + ahead < n)
            def _():
                piece(k + ahead).start()

    return ready


IN_RUNS = ((PZ, 0, 1024), (PXBC, 1024, XBC), (PG, 2576, 2048))
IN_TILE = 512
IN_TILE_ROWS = [wrow + IN_TILE * j for _, wrow, width in IN_RUNS for j in range(width // IN_TILE)]


def _in_tiles(fn):
    done = 0
    for pcol, wrow, width in IN_RUNS:
        def step(j, carry, pcol=pcol, wrow=wrow, done=done):
            fn(pl.multiple_of(pcol + j * IN_TILE, IN_TILE), pl.multiple_of(wrow + j * IN_TILE, 16), done + j)
            return carry

        lax.fori_loop(0, width // IN_TILE, step, 0)
        done += width // IN_TILE


def in_proj(h0, wn, w_t, w_dt):
    def body(h_ref, wn_ref, w_hbm, wdt_ref, o_ref, dt_ref, u_ref, w_ref, w_sems):
        ready = _arriving(w_hbm, w_ref, w_sems, IN_TILE_ROWS, IN_TILE)
        for r in (0, HALF):
            u_ref[r:r + HALF, :] = _rms(h_ref[r:r + HALF, :], wn_ref[...]).astype(BF)

        def tile(pcol, wrow, k):
            ready(k)
            o_ref[:, pl.ds(pcol, IN_TILE)] = lax.dot_general(u_ref[...], w_ref[pl.ds(wrow, IN_TILE), :], NT_DIMS,
                                                             preferred_element_type=F32).astype(BF)

        _in_tiles(tile)
        dt_ref[...] = lax.dot_general(u_ref[...], wdt_ref[...], NT_DIMS, preferred_element_type=F32)

    return pl.pallas_call(
        body, grid=(T // RC,), in_specs=[_rows_spec(D), _vec(D), ANY_SPEC, _whole((256, D))],
        out_specs=[_rows_spec(NP_IN), _rows_spec(256), _rows_spec(D)],
        out_shape=[jax.ShapeDtypeStruct((T, NP_IN), BF), jax.ShapeDtypeStruct((T, 256), F32), jax.ShapeDtypeStruct((T, D), BF)],
        scratch_shapes=[pltpu.VMEM((IN_COLS, D), BF), pltpu.SemaphoreType.DMA((len(IN_TILE_ROWS),))],
        compiler_params=_params(), name="in_proj")(h0, wn, w_t, w_dt)


def out_proj(yn_ssd, proj, hseq, lru_nw, w_out, h0):
    def body(y_ref, g_ref, h_ref, wn_ref, w_ref, r_ref, o_ref, cat_ref):
        cat_ref[:, 0:SSD_W] = y_ref[...]
        for r in (0, HALF):
            y = _gelu(g_ref[r:r + HALF, :].astype(F32)) * h_ref[r:r + HALF, :]
            cat_ref[r:r + HALF, SSD_W:] = _rms(y, wn_ref[...]).astype(BF)

        def tile(c0):
            o_ref[:, pl.ds(c0, 512)] = r_ref[:, pl.ds(c0, 512)] + jnp.dot(cat_ref[...], w_ref[:, pl.ds(c0, 512)], preferred_element_type=F32)

        _col_tiles(D, 512, tile)

    return pl.pallas_call(
        body, grid=(T // RC,),
        in_specs=[_rows_spec(SSD_W), _rows_spec(LRU_W, PG // LRU_W), _rows_spec(LRU_W), _vec(LRU_W), _whole((SSD_W + LRU_W, D)), _rows_spec(D)],
        out_specs=[_rows_spec(D), _rows_spec(SSD_W + LRU_W)],
        out_shape=[jax.ShapeDtypeStruct((T, D), F32), jax.ShapeDtypeStruct((T, SSD_W + LRU_W), BF)],
        compiler_params=_params(), name="out_proj")(yn_ssd, proj, hseq, lru_nw, w_out, h0)


def out_proj_bwd(dh1_b, w_out, proj, hseq, lru_nw):
    def body(d_ref, w_ref, g_ref, h_ref, wn_ref, dy_ref, dh_ref, dg_ref, dw_ref, dl_scr):
        _zero_at_first(dw_ref)

        def tile(c0):
            dy_ref[:, pl.ds(c0, 512)] = lax.dot_general(d_ref[...], w_ref[pl.ds(c0, 512), :], NT_DIMS, preferred_element_type=F32)
            dl_scr[:, pl.ds(c0, 512)] = lax.dot_general(d_ref[...], w_ref[pl.ds(SSD_W + c0, 512), :], NT_DIMS, preferred_element_type=F32)

        _col_tiles(SSD_W, 512, tile)

        for r in (0, HALF):
            g = g_ref[r:r + HALF, :].astype(F32)
            h = h_ref[r:r + HALF, :]
            ge = _gelu(g)
            dy, dw = _rms_bwd(dl_scr[r:r + HALF, :], ge * h, wn_ref[...])
            dw_ref[...] += jnp.sum(dw, axis=0, keepdims=True)
            dh_ref[r:r + HALF, :] = dy * ge
            dg_ref[r:r + HALF, :] = (dy * h * _gelu_grad(g)).astype(BF)

    return pl.pallas_call(
        body, grid=(T // RC,),
        in_specs=[_rows_spec(D), _whole((SSD_W + LRU_W, D)), _rows_spec(LRU_W, PG // LRU_W), _rows_spec(LRU_W), _vec(LRU_W)],
        out_specs=[_rows_spec(SSD_W), _rows_spec(LRU_W), _rows_spec(LRU_W), _vec(LRU_W)],
        out_shape=[jax.ShapeDtypeStruct((T, SSD_W), F32), jax.ShapeDtypeStruct((T, LRU_W), F32), jax.ShapeDtypeStruct((T, LRU_W), BF),
                   jax.ShapeDtypeStruct((1, LRU_W), F32)],
        scratch_shapes=[pltpu.VMEM((RC, LRU_W), F32)],
        compiler_params=_params(), name="out_proj_bwd")(dh1_b, w_out, proj, hseq, lru_nw)


def in_proj_bwd(dz, dg, dxl, dxbc, ddt, w_t, w_dt, h0, wn, dh1):
    first = NPAD + N_META

    def body(dz_ref, dg_ref, dxl_ref, dxbc_ref, ddt_ref, w_hbm, wdt_ref, h_ref, wn_ref, r_ref, gx_hbm, meta_ref, dw_ref, du_scr, o_ref, sem,
             w_ref, w_sems):
        i = pl.program_id(0)
        ready = _arriving(w_hbm, w_ref, w_sems, IN_TILE_ROWS, IN_TILE)
        _zero_at_first(dw_ref)
        du_scr[...] = jnp.dot(ddt_ref[...], wdt_ref[...], preferred_element_type=F32)
        done = 0
        for d_ref, wrow, width in ((dz_ref, 0, 1024), (dxbc_ref, 1024, XBC), (dg_ref, 2576, 1024), (dxl_ref, 3600, 1024)):
            def step(j, carry, d_ref=d_ref, wrow=wrow, done=done):
                c0 = pl.multiple_of(j * IN_TILE, IN_TILE)
                ready(done + j)
                du_scr[...] += jnp.dot(d_ref[:, pl.ds(c0, IN_TILE)], w_ref[pl.ds(pl.multiple_of(wrow + c0, 16), IN_TILE), :],
                                       preferred_element_type=F32)
                return carry

            lax.fori_loop(0, width // IN_TILE, step, 0)
            done += width // IN_TILE
        for r in (0, HALF):
            dh, dw = _rms_bwd(du_scr[r:r + HALF, :], h_ref[r:r + HALF, :], wn_ref[...])
            dw_ref[...] += jnp.sum(dw, axis=0, keepdims=True)
            o_ref[r:r + HALF, :] = dh + r_ref[r:r + HALF, :]

        @pl.when(i == 0)
        def _():
            meta_ref[...] = o_ref[NPAD:first, :]
            head = pltpu.make_async_copy(o_ref.at[pl.ds(first, RC - first)], gx_hbm.at[pl.ds(0, RC - first)], sem)
            head.start()
            head.wait()

        @pl.when(i > 0)
        def _():
            rest = pltpu.make_async_copy(o_ref, gx_hbm.at[pl.ds(pl.multiple_of(i * RC - first, 32), RC)], sem)
            rest.start()
            rest.wait()

    return pl.pallas_call(
        body, grid=(T // RC,),
        in_specs=[_rows_spec(SSD_W), _rows_spec(LRU_W), _rows_spec(LRU_W), _rows_spec(XBC), _rows_spec(256), ANY_SPEC,
                  _whole((256, D)), _rows_spec(D), _vec(D), _rows_spec(D)],
        out_specs=[ANY_SPEC, _spec((N_META, D), lambda i: (0, 0)), _vec(D)],
        out_shape=[jax.ShapeDtypeStruct((SEQ, D), F32), jax.ShapeDtypeStruct((N_META, D), F32), jax.ShapeDtypeStruct((1, D), F32)],
        scratch_shapes=[pltpu.VMEM((RC, D), F32), pltpu.VMEM((RC, D), F32), pltpu.SemaphoreType.DMA,
                        pltpu.VMEM((IN_COLS, D), BF), pltpu.SemaphoreType.DMA((len(IN_TILE_ROWS),))],
        compiler_params=_params(), name="in_proj_bwd")(dz, dg, dxl, dxbc, ddt, w_t, w_dt, h0, wn, dh1)


GRAD_TILE = 256


def weight_grad(name, a, u1):
    tm = GRAD_TILE

    def body(a_ref, u_ref, o_ref):
        o_ref[...] = lax.dot_general(a_ref[...], u_ref[...], TN_DIMS, preferred_element_type=F32).astype(BF)

    return pl.pallas_call(
        body, grid=(a.shape[1] // tm,),
        in_specs=[_spec((T, tm), lambda j: (0, j)), _spec((T, D), lambda j: (0, 0), single=True)],
        out_specs=_spec((tm, D), lambda j: (j, 0)),
        out_shape=jax.ShapeDtypeStruct((a.shape[1], D), BF),
        compiler_params=_params(), name=name)(a, u1)


def in_weight_grad(dz, dg, dxl, dxbc, ddt, u1):
    tm = GRAD_TILE
    per_row = D // 128
    parts = (dz, dg, dxl, dxbc, ddt)
    first_rows = (0, 2576, 3600, 1024, 2560)
    tiles = [p.shape[1] // tm for p in parts]
    starts = [sum(tiles[:k]) for k in range(len(parts))]
    last = sum(tiles) - 1
    dt_lines = 8 * per_row

    def body(*refs):
        a_refs, u_ref, o_hbm, mix_scr, stage, sems = refs[:5], refs[5], refs[6], refs[7], refs[8], refs[9]
        step = pl.program_id(0)
        slot = step % 2
        line0 = 0
        for a_ref, start, n, first in zip(a_refs, starts, tiles, first_rows):
            here = (step >= start) & (step < start + n)
            line0 = jnp.where(here, per_row * (first + tm * (step - start)), line0)

            @pl.when(here)
            def _(a_ref=a_ref):
                res = lax.dot_general(a_ref[...], u_ref[...], TN_DIMS, preferred_element_type=F32)
                for q in range(per_row):
                    mix_scr[pl.ds(q, tm, stride=per_row), :] = res[:, 128 * q:128 * q + 128]

        def tile_copy(of_slot, to):
            return pltpu.make_async_copy(stage.at[of_slot], o_hbm.at[pl.ds(to, per_row * tm)], sems.at[of_slot])

        @pl.when(step >= 2)
        def _():
            tile_copy(slot, 0).wait()

        stage[slot] = mix_scr[...].astype(BF)

        @pl.when(step < last)
        def _():
            tile_copy(slot, pl.multiple_of(line0, 128)).start()

        @pl.when(step == last)
        def _():
            halves = [pltpu.make_async_copy(stage.at[slot, pl.ds(128 * per_row * k, dt_lines)],
                                            o_hbm.at[pl.ds(per_row * (first_rows[-1] + 8 * k), dt_lines)], sems.at[2 + k]) for k in range(2)]
            for cp in halves:
                cp.start()
            tile_copy(1 - slot, 0).wait()
            for cp in halves:
                cp.wait()

    def tile_of(start, n):
        return lambda j: (0, jnp.clip(j - start, 0, n - 1))

    return pl.pallas_call(
        body, grid=(last + 1,),
        in_specs=[_spec((T, tm), tile_of(s, n)) for s, n in zip(starts, tiles)] + [_spec((T, D), lambda j: (0, 0), single=True)],
        out_specs=pl.BlockSpec(memory_space=pl.ANY),
        out_shape=jax.ShapeDtypeStruct((per_row * IN_COLS, 128), BF),
        scratch_shapes=[pltpu.VMEM((per_row * tm, 128), F32), pltpu.VMEM((2, per_row * tm, 128), BF), pltpu.SemaphoreType.DMA((4,))],
        compiler_params=_params(), name="dw_in")(*parts, u1)


def _ssd_chunk_common(row0, dt_ref, b_ref, c_ref, bias, a_neg):
    shape = (Q, Q)
    lane = _lanes(shape)
    sub = _rows(shape)
    live = (_rows(shape, row0) >= NPAD) & (lane < 8)
    dtr = dt_ref[:, :]
    dt = jnp.where(live, _softplus(dtr + bias), 0.0)
    d_a = dt * a_neg
    tri = (sub >= lane).astype(F32)
    cs = jnp.dot(tri, d_a, precision=lax.Precision.HIGHEST, preferred_element_type=F32)
    cs_t = cs.T
    b_f = b_ref[:, :]
    bc = b_f.astype(BF)
    cc = c_ref[:, :].astype(BF)
    cb = lax.dot_general(cc, bc, NT_DIMS, preferred_element_type=F32)
    cs_last = cs[Q - 1:Q, :]
    return dict(lane=lane, sub=sub, live=live, dtr=dtr, dt=dt, cs=cs, cs_t=cs_t, bc=bc, cc=cc, cb=cb, bc_t=b_f.T.astype(BF),
                ecs=jnp.exp(cs), dsm=jnp.exp(cs_last - cs), gam=jnp.exp(cs_last))


def _pair(lane_even, mat, j):
    return jnp.where(lane_even, mat[:, j:j + 1], mat[:, j + 1:j + 2])


def _pair_row(lane_even, mat, j):
    return jnp.where(lane_even[0:1, :], mat[:, j:j + 1], mat[:, j + 1:j + 2])


def _head_decay(cm, j):
    seg = cm["cs"][:, j:j + 1] - cm["cs_t"][j:j + 1, :]
    return jnp.exp(jnp.where(cm["sub"] >= cm["lane"], seg, -jnp.inf))


def _head_decay_t(cm, j):
    seg = cm["cs_t"][j:j + 1, :] - cm["cs"][:, j:j + 1]
    return jnp.exp(jnp.where(cm["lane"] >= cm["sub"], seg, -jnp.inf))


def _conv_window(raw_ref, halo_ref, pad_scr):
    pad_scr[0:HALO, :] = halo_ref[...].astype(F32)[halo_ref.shape[0] - HALO:, :]
    pad_scr[HALO:HALO + Q, :] = raw_ref[...].astype(F32)
    win = pad_scr[...]
    return lambda s: win[HALO:, :] if s == 0 else pltpu.roll(win, s, axis=0)[HALO:, :]


def _xbc_cols(g):
    return slice(512 * g, 512 * g + 512), slice(SSD_W + 128 * g, SSD_W + 128 * g + 128), slice(SSD_W + 256 + 128 * g, SSD_W + 384 + 128 * g)


def ssd_fwd(proj, dt_raw, conv_w, conv_b, dt_bias2, a_log2, d2, norm_w):
    def body(raw_ref, halo_ref, dt_all, z_all, cw_ref, cb_ref, bias_all, alog_all, d_all, nw_all, yn_all, y_all, hp_all,
             h_all, pad_scr, act_scr):
        @pl.when(pl.program_id(0) == 0)
        def _():
            h_all[...] = jnp.zeros_like(h_all)

        pre = _conv(_conv_window(raw_ref, halo_ref, pad_scr), cw_ref[...], cb_ref[...])
        act_scr[...] = pre * _sigmoid(pre)
        for g in range(2):
            wide, thin = slice(512 * g, 512 * g + 512), slice(128 * g, 128 * g + 128)
            xs, bs, cs = _xbc_cols(g)
            group(act_scr.at[:, xs], act_scr.at[:, bs], act_scr.at[:, cs], dt_all.at[:, thin], z_all.at[:, wide], bias_all.at[g],
                  alog_all.at[g], d_all.at[g], nw_all.at[:, wide], yn_all.at[:, wide], y_all.at[:, wide], hp_all.at[g, 0], h_all.at[g])

    def group(x_ref, b_ref, c_ref, dt_ref, z_ref, bias_ref, alog_ref, d_ref, nw_ref, yn_ref, y_ref, hp_ref, h_scr):
        bias = bias_ref[...]
        a_neg = -jnp.exp(alog_ref[...])
        dsk = d_ref[...]
        cm = _ssd_chunk_common(pl.program_id(0) * Q, dt_ref, b_ref, c_ref, bias, a_neg)
        lane_even = cm["lane"] < 64
        for p in range(4):
            je, jo = 2 * p, 2 * p + 1
            xp = x_ref[:, 128 * p:128 * p + 128]
            xdt = xp * _pair(lane_even, cm["dt"], je)
            xdt_b = xdt.astype(BF)
            m_e = (cm["cb"] * _head_decay(cm, je)).astype(BF)
            m_o = (cm["cb"] * _head_decay(cm, jo)).astype(BF)
            zero = jnp.zeros_like(xdt_b)
            yd = (jnp.dot(m_e, jnp.where(lane_even, xdt_b, zero), preferred_element_type=F32)
                  + jnp.dot(m_o, jnp.where(lane_even, zero, xdt_b), preferred_element_type=F32))
            hp = h_scr[p]
            hp_ref[p] = hp
            yo = jnp.dot(cm["cc"], hp.astype(BF), preferred_element_type=F32) * _pair(lane_even, cm["ecs"], je)
            y_ref[:, 128 * p:128 * p + 128] = yd + yo + xp * _pair_row(lane_even, dsk, je)
            st = jnp.dot(cm["bc_t"], (xdt * _pair(lane_even, cm["dsm"], je)).astype(BF), preferred_element_type=F32)
            h_scr[p] = hp * _pair_row(lane_even, cm["gam"], je) + st
        zc = z_ref[:, :].astype(F32)
        gated = y_ref[:, :] * (zc * _sigmoid(zc))
        yn_ref[:, :] = _rms(gated, nw_ref[...]).astype(BF)

    par = _spec((2, 1, 128), lambda c: (0, 0, 0))
    wide = _spec((Q, SSD_W), lambda c: (c, 0))
    xbc = PXBC // XBC
    halo = 2 * HALO
    return pl.pallas_call(
        body, grid=(NCH,),
        in_specs=[_spec((Q, XBC), lambda c: (c, xbc)), _spec((halo, XBC), lambda c: (jnp.maximum(c * (Q // halo) - 1, 0), xbc)),
                  _spec((Q, 256), lambda c: (c, 0)), wide, _spec((4, XBC), lambda c: (0, 0)), _spec((1, XBC), lambda c: (0, 0)),
                  par, par, par, _spec((1, SSD_W), lambda c: (0, 0))],
        out_specs=[wide, wide, _spec((2, 1, 4, 128, 128), lambda c: (0, c, 0, 0, 0))],
        out_shape=[jax.ShapeDtypeStruct((T, SSD_W), BF), jax.ShapeDtypeStruct((T, SSD_W), F32),
                   jax.ShapeDtypeStruct((2, NCH, 4, 128, 128), F32)],
        scratch_shapes=[pltpu.VMEM((2, 4, 128, 128), F32), pltpu.VMEM((Q + HALO, XBC), F32), pltpu.VMEM((Q, XBC), F32)],
        compiler_params=_params(), name="ssd_fwd")(proj, proj, dt_raw, proj, conv_w, conv_b, dt_bias2, a_log2, d2, norm_w)


def ssd_bwd(dyn, proj, dt_raw, conv_w, conv_b, y_pre, h_prev, dt_bias2, a_log2, d2, norm_w):
    def body(dyn_all, raw_ref, halo_ref, dt_all, z_all, y_all, hp_all, cw_ref, cb_ref, bias_all, alog_all, d_all, nw_all,
             dz_all, dxbc_ref, ddt_all, dpar_all, dnw_all, dcw_ref, dcb_ref, dh_all, acc_all, pad_scr, act_scr, dsilu_scr, dact_scr, dpad_scr):
        @pl.when(pl.program_id(0) == 0)
        def _():
            dh_all[...] = jnp.zeros_like(dh_all)
            acc_all[...] = jnp.zeros_like(acc_all)
            dnw_all[...] = jnp.zeros_like(dnw_all)
            dcw_ref[...] = jnp.zeros_like(dcw_ref)
            dcb_ref[...] = jnp.zeros_like(dcb_ref)
            dpad_scr[Q:Q + HALO, :] = jnp.zeros((HALO, XBC), F32)

        back = _conv_window(raw_ref, halo_ref, pad_scr)
        pre = _conv(back, cw_ref[...], cb_ref[...])
        sg = _sigmoid(pre)
        act_scr[...] = pre * sg
        dsilu_scr[...] = sg * (1.0 + pre * (1.0 - sg))
        for g in range(2):
            wide, thin = slice(512 * g, 512 * g + 512), slice(128 * g, 128 * g + 128)
            xs, bs, cs = _xbc_cols(g)
            group(dyn_all.at[:, wide], act_scr.at[:, xs], act_scr.at[:, bs], act_scr.at[:, cs], dt_all.at[:, thin], z_all.at[:, wide],
                  y_all.at[:, wide], hp_all.at[g, 0], bias_all.at[g], alog_all.at[g], d_all.at[g], nw_all.at[:, wide],
                  dz_all.at[:, wide], dact_scr.at[:, xs], dact_scr.at[:, bs], dact_scr.at[:, cs], ddt_all.at[:, thin], dpar_all.at[g],
                  dnw_all.at[:, wide], dh_all.at[g], acc_all.at[g])
        dpre = dact_scr[...] * dsilu_scr[...]
        dcw, dcb = _conv_bwd_w(dpre, back)
        dcw_ref[...] += dcw
        dcb_ref[...] += dcb
        dpad_scr[0:Q, :] = dpre
        win = dpad_scr[...]
        dxbc_ref[...] = _conv_bwd_x(lambda s: win[:Q, :] if s == 0 else pltpu.roll(win, Q + HALO - s, axis=0)[:Q, :], cw_ref[...]).astype(BF)
        dpad_scr[Q:Q + HALO, :] = dpre[0:HALO, :]

    def group(dyn_ref, x_ref, b_ref, c_ref, dt_ref, z_ref, y_ref, hp_ref, bias_ref, alog_ref, d_ref, nw_ref,
              dz_ref, dx_ref, db_ref, dc_ref, ddt_ref, dpar_ref, dnw_ref, dh_scr, acc_scr):
        ci = pl.program_id(0)
        bias = bias_ref[...]
        a_neg = -jnp.exp(alog_ref[...])
        dsk = d_ref[...]
        cm = _ssd_chunk_common((NCH - 1 - ci) * Q, dt_ref, b_ref, c_ref, bias, a_neg)
        lane, sub = cm["lane"], cm["sub"]
        lane_even = lane < 64
        cc_t = c_ref[:, :].T.astype(BF)
        cb_t = lax.dot_general(cm["bc"], cm["cc"], NT_DIMS, preferred_element_type=F32)
        zc = z_ref[:, :].astype(F32)
        yc = y_ref[:, :]
        sg = _sigmoid(zc)
        sz = zc * sg
        dgated, dnw = _rms_bwd(dyn_ref[:, :], yc * sz, nw_ref[...])
        dnw_ref[...] += jnp.sum(dnw, axis=0, keepdims=True)
        dz_ref[:, :] = (dgated * yc * (sg * (1.0 + zc * (1.0 - sg)))).astype(BF)
        dy_all = dgated * sz
        dcb = jnp.zeros((Q, Q), F32)
        dcb_t = jnp.zeros((Q, Q), F32)
        db_acc = jnp.zeros((Q, Q), F32)
        dc_acc = jnp.zeros((Q, Q), F32)
        dcs = jnp.zeros((Q, Q), F32)
        ddt = jnp.zeros((Q, Q), F32)
        for p in range(4):
            je, jo = 2 * p, 2 * p + 1
            xp = x_ref[:, 128 * p:128 * p + 128]
            dy = dy_all[:, 128 * p:128 * p + 128]
            dt_p = _pair(lane_even, cm["dt"], je)
            xdt = xp * dt_p
            xdt_b = xdt.astype(BF)
            dy_b = dy.astype(BF)
            zero = jnp.zeros_like(dy_b)
            hp = hp_ref[p]
            hp_b = hp.astype(BF)
            dh = dh_scr[p]
            dh_b = dh.astype(BF)
            acc_scr[p:p + 1, :] += jnp.sum(dy * xp, axis=0, keepdims=True)
            dxp = dy * _pair_row(lane_even, dsk, je)
            e_p = _pair(lane_even, cm["ecs"], je)
            g_p = jnp.dot(cm["cc"], hp_b, preferred_element_type=F32)
            dg_b = (dy * e_p).astype(BF)
            de = dy * g_p * e_p
            dc_acc = dc_acc + lax.dot_general(dg_b, hp_b, NT_DIMS, preferred_element_type=F32)
            dh_in = jnp.dot(cc_t, dg_b, preferred_element_type=F32)
            ds_p = _pair(lane_even, cm["dsm"], je)
            r_p = jnp.dot(cm["bc"], dh_b, preferred_element_type=F32)
            dxdt = r_p * ds_p
            tt = r_p * xdt * ds_p
            db_acc = db_acc + lax.dot_general((xdt * ds_p).astype(BF), dh_b, NT_DIMS, preferred_element_type=F32)
            dgam_m = jnp.sum(dh * hp, axis=0, keepdims=True)
            for j, even in ((je, True), (jo, False)):
                sel = lane_even if even else jnp.logical_not(lane_even)
                dy_j = jnp.where(sel, dy_b, zero)
                l_j = _head_decay(cm, j)
                l_jt = _head_decay_t(cm, j)
                m_j = cm["cb"] * l_j
                m_jt = cb_t * l_jt
                dm = lax.dot_general(dy_j, xdt_b, NT_DIMS, preferred_element_type=F32)
                dm_t = lax.dot_general(xdt_b, dy_j, NT_DIMS, preferred_element_type=F32)
                dxdt = dxdt + jnp.dot(m_jt.astype(BF), dy_j, preferred_element_type=F32)
                dcb = dcb + dm * l_j
                dcb_t = dcb_t + dm_t * l_jt
                t_j = jnp.where(sel, tt, 0.0)
                col = jnp.sum(dm * m_j - dm_t * m_jt + (jnp.where(sel, de, 0.0) - t_j), axis=1, keepdims=True)
                gam_j = cm["gam"][:, j:j + 1]
                last = (jnp.sum(jnp.sum(t_j, axis=0, keepdims=True), axis=1, keepdims=True)
                        + jnp.sum(jnp.where(sel[0:1, :], dgam_m, 0.0), axis=1, keepdims=True) * gam_j)
                col = col + jnp.where(sub[:, 0:1] == Q - 1, last, 0.0)
                dcs = dcs + jnp.where(lane == j, col, 0.0)
            dh_scr[p] = dh_in + dh * _pair_row(lane_even, cm["gam"], je)
            dx_ref[:, 128 * p:128 * p + 128] = dxp + dxdt * dt_p
            dd = dxdt * xp
            ddt = ddt + jnp.where(lane == je, jnp.sum(jnp.where(lane_even, dd, 0.0), axis=1, keepdims=True), 0.0)
            ddt = ddt + jnp.where(lane == jo, jnp.sum(jnp.where(lane_even, 0.0, dd), axis=1, keepdims=True), 0.0)
        dc_ref[:, :] = dc_acc + jnp.dot(dcb.astype(BF), cm["bc"], preferred_element_type=F32)
        db_ref[:, :] = db_acc + jnp.dot(dcb_t.astype(BF), cm["cc"], preferred_element_type=F32)
        tri_t = (sub <= lane).astype(F32)
        dd_a = jnp.dot(tri_t, dcs, precision=lax.Precision.HIGHEST, preferred_element_type=F32)
        ddt = ddt + dd_a * a_neg
        acc_scr[5:6, :] += jnp.sum(dd_a * cm["dt"], axis=0, keepdims=True)
        draw = jnp.where(cm["live"], ddt * _sigmoid_gate(cm["dtr"] + bias), 0.0)
        acc_scr[4:5, :] += jnp.sum(draw, axis=0, keepdims=True)
        ddt_ref[:, :] = draw.astype(BF)

        @pl.when(ci == NCH - 1)
        def _():
            lane1 = _lanes((1, 128))
            dd = jnp.zeros((1, 128), F32)
            for p in range(4):
                row = acc_scr[p:p + 1, :]
                dd = dd + jnp.where(lane1 == 2 * p, jnp.sum(jnp.where(lane1 < 64, row, 0.0), axis=1, keepdims=True), 0.0)
                dd = dd + jnp.where(lane1 == 2 * p + 1, jnp.sum(jnp.where(lane1 < 64, 0.0, row), axis=1, keepdims=True), 0.0)
            dpar_ref[...] = jnp.concatenate([acc_scr[4:5, :], acc_scr[5:6, :] * a_neg, dd, jnp.zeros((5, 128), F32)], axis=0)

    par = _spec((2, 1, 128), lambda c: (0, 0, 0))
    wide = _spec((Q, SSD_W), lambda c: (NCH - 1 - c, 0))
    thin = _spec((Q, 256), lambda c: (NCH - 1 - c, 0))
    vec = _spec((1, SSD_W), lambda c: (0, 0))
    xbc = PXBC // XBC
    halo = 2 * HALO
    chunk = pltpu.VMEM((Q, XBC), F32)
    padded = pltpu.VMEM((Q + HALO, XBC), F32)
    return pl.pallas_call(
        body, grid=(NCH,),
        in_specs=[wide, _spec((Q, XBC), lambda c: (NCH - 1 - c, xbc)),
                  _spec((halo, XBC), lambda c: (jnp.maximum((NCH - 1 - c) * (Q // halo) - 1, 0), xbc)), thin, wide, wide,
                  _spec((2, 1, 4, 128, 128), lambda c: (0, NCH - 1 - c, 0, 0, 0)), _spec((4, XBC), lambda c: (0, 0)),
                  _spec((1, XBC), lambda c: (0, 0)), par, par, par, vec],
        out_specs=[wide, _spec((Q, XBC), lambda c: (NCH - 1 - c, 0)), thin, _spec((2, 8, 128), lambda c: (0, 0, 0)), vec,
                   _spec((4, XBC), lambda c: (0, 0)), _spec((1, XBC), lambda c: (0, 0))],
        out_shape=[jax.ShapeDtypeStruct((T, SSD_W), BF), jax.ShapeDtypeStruct((T, XBC), BF), jax.ShapeDtypeStruct((T, 256), BF),
                   jax.ShapeDtypeStruct((2, 8, 128), F32), jax.ShapeDtypeStruct((1, SSD_W), F32), jax.ShapeDtypeStruct((4, XBC), F32),
                   jax.ShapeDtypeStruct((1, XBC), F32)],
        scratch_shapes=[pltpu.VMEM((2, 4, 128, 128), F32), pltpu.VMEM((2, 8, 128), F32), padded, chunk, chunk, chunk, padded],
        compiler_params=_params(), name="ssd_bwd")(dyn, proj, proj, dt_raw, proj, y_pre, h_prev, conv_w, conv_b, dt_bias2, a_log2, d2, norm_w)


def _lru_gates(back, cw, cb, wa, ba, wx, bx, lam):
    xr = _conv(back, cw, cb)
    xr_b = xr.astype(BF)
    r = _sigmoid_gate(jnp.dot(xr_b, wa, preferred_element_type=F32) + ba)
    i = _sigmoid_gate(jnp.dot(xr_b, wx, preferred_element_type=F32) + bx)
    sp = _softplus(-lam)
    la = (-LRU_C) * r * sp
    a = jnp.exp(la)
    mult2 = -jnp.tanh(la) * (a * a + 1.0)
    return xr, xr_b, r, i, sp, a, jnp.sqrt(mult2), mult2


SEG_LEN = 68
SEGS = T // SEG_LEN


def _seg_rows(j, k, off=0):
    return pl.ds(off + j * 8 * SEG_LEN + k, 8, stride=SEG_LEN)


def _segmented_scan(mul_ref, mul_row0, add_ref, out_ref, loc_scr, prod_scr, carry_scr, reverse):
    groups = SEGS // 8
    off = mul_row0 + (1 if reverse else 0)

    def local(i, carry):
        k = SEG_LEN - 1 - i if reverse else i
        new = []
        for j in range(groups):
            h, p = carry[2 * j], carry[2 * j + 1]
            m = mul_ref[_seg_rows(j, k, off), :]
            h = m * h + add_ref[_seg_rows(j, k), :]
            p = m * p
            loc_scr[_seg_rows(j, k), :] = h
            prod_scr[_seg_rows(j, k), :] = p
            new += [h, p]
        return tuple(new)

    lax.fori_loop(0, SEG_LEN, local, (jnp.zeros((8, 128), F32), jnp.ones((8, 128), F32)) * groups)

    def chain(i, c):
        s = SEGS - 1 - i if reverse else i
        carry_scr[pl.ds(s, 1), :] = c
        edge = s * SEG_LEN + (0 if reverse else SEG_LEN - 1)
        return loc_scr[pl.ds(edge, 1), :] + prod_scr[pl.ds(edge, 1), :] * c

    lax.fori_loop(0, SEGS, chain, jnp.zeros((1, 128), F32))

    def fold(k, carry):
        for j in range(groups):
            rows = _seg_rows(j, k)
            out_ref[rows, :] = loc_scr[rows, :] + prod_scr[rows, :] * carry_scr[8 * j:8 * j + 8, :]
        return carry

    lax.fori_loop(0, SEG_LEN, fold, 0)


def lru_fwd(proj, cw, cb, wa2, ba, wx2, bx, lam):
    def body(x_ref, cw_ref, cb_ref, wa_ref, ba_ref, wx_ref, bx_ref, lam_ref, h_ref, a_ref, xpad, u_scr, loc_scr, prod_scr, carry_scr):
        _fill_padded(xpad, x_ref)

        def chunk(r0):
            xr, _, _, i, _, a, mult, _ = _lru_gates(_back(xpad, r0), cw_ref[...], cb_ref[...], wa_ref[0], ba_ref[...], wx_ref[0], bx_ref[...],
                                                 lam_ref[...])
            a_ref[pl.ds(r0, Q), :] = a
            u_scr[pl.ds(r0, Q), :] = jnp.where(_rows(a.shape, r0) >= NPAD, mult * (i * xr), 0.0)

        _chunks(chunk, unrolled=True)
        _segmented_scan(a_ref, 0, u_scr, h_ref, loc_scr, prod_scr, carry_scr, reverse=False)

    c0 = PXL // 128
    vec = _spec((1, 128), lambda c: (0, c))
    mat = _spec((1, 128, 128), lambda c: (c, 0, 0))
    seq = pltpu.VMEM((T, 128), F32)
    return pl.pallas_call(
        body, grid=(8,),
        in_specs=[_spec((T, 128), lambda c: (0, c0 + c)), _spec((4, 128), lambda c: (0, c)), vec, mat, vec, mat, vec, vec],
        out_specs=[_spec((T, 128), lambda c: (0, c)), _spec((T, 128), lambda c: (0, c))],
        out_shape=[jax.ShapeDtypeStruct((T, LRU_W), F32), jax.ShapeDtypeStruct((T, LRU_W), F32)],
        scratch_shapes=[pltpu.VMEM((T + 2 * HALO, 128), F32), seq, seq, seq, pltpu.VMEM((SEGS, 128), F32)],
        compiler_params=_params(), name="lru_fwd")(proj, cw, cb, wa2, ba, wx2, bx, lam)


def lru_bwd(dh_out, a, hseq, proj, cw, cb, wa2, ba, wx2, bx, lam):
    def body(d_ref, a_ref, h_ref, x_ref, cw_ref, cb_ref, wa_ref, ba_ref, wx_ref, bx_ref, lam_ref,
             dx_ref, dcw_ref, dcb_ref, dwa_ref, dba_ref, dwx_ref, dbx_ref, dlam_ref, xpad, hpad, dpad, dh_ref, loc_scr, prod_scr, carry_scr):
        _fill_padded(dpad, a_ref)
        _segmented_scan(dpad, HALO, d_ref, dh_ref, loc_scr, prod_scr, carry_scr, reverse=True)
        _fill_padded(xpad, x_ref)
        _fill_padded(hpad, h_ref)
        dpad[0:HALO, :] = jnp.zeros((HALO, 128), F32)
        dpad[T + HALO:T + 2 * HALO, :] = jnp.zeros((HALO, 128), F32)
        for ref in (dcw_ref, dcb_ref, dwa_ref, dba_ref, dwx_ref, dbx_ref, dlam_ref):
            ref[...] = jnp.zeros_like(ref)
        lam = lam_ref[...]

        def first(r0):
            back = _back(xpad, r0)
            xr, xr_b, r, i, sp, a, mult, mult2 = _lru_gates(back, cw_ref[...], cb_ref[...], wa_ref[0], ba_ref[...], wx_ref[0], bx_ref[...], lam)
            dh = dh_ref[pl.ds(r0, Q), :]
            da = dh * _back(hpad, r0)(1)
            du = jnp.where(_rows(dh.shape, r0) >= NPAD, dh, 0.0)
            dmult = du * (i * xr)
            di = du * (mult * xr)
            dxr = du * (mult * i)
            dla = da * a - dmult * (a * a) * lax.rsqrt(mult2)
            dr = dla * ((-LRU_C) * sp)
            dlam_ref[...] += jnp.sum(dla * ((-LRU_C) * r), axis=0, keepdims=True)
            dpr = dr * r * (1.0 - r)
            dpi = di * i * (1.0 - i)
            dba_ref[...] += jnp.sum(dpr, axis=0, keepdims=True)
            dbx_ref[...] += jnp.sum(dpi, axis=0, keepdims=True)
            dpr_b = dpr.astype(BF)
            dpi_b = dpi.astype(BF)
            dxr = (dxr + lax.dot_general(dpr_b, wa_ref[0], NT_DIMS, preferred_element_type=F32)
                   + lax.dot_general(dpi_b, wx_ref[0], NT_DIMS, preferred_element_type=F32))
            dwa_ref[0] += lax.dot_general(xr_b, dpr_b, TN_DIMS, preferred_element_type=F32)
            dwx_ref[0] += lax.dot_general(xr_b, dpi_b, TN_DIMS, preferred_element_type=F32)
            dpad[pl.ds(r0 + HALO, Q), :] = dxr
            dcw, dcb = _conv_bwd_w(dxr, back)
            dcw_ref[...] += dcw
            dcb_ref[...] += dcb

        _chunks(first, unrolled=True)
        dlam_ref[...] = -dlam_ref[...] * _sigmoid_gate(-lam)

        def second(r0):
            dx_ref[pl.ds(r0, Q), :] = _conv_bwd_x(_ahead(dpad, r0), cw_ref[...]).astype(BF)

        _chunks(second)

    c0 = PXL // 128
    vec = _spec((1, 128), lambda c: (0, c))
    mat = _spec((1, 128, 128), lambda c: (c, 0, 0))
    col = _spec((T, 128), lambda c: (0, c))
    vshape = jax.ShapeDtypeStruct((1, LRU_W), F32)
    mshape = jax.ShapeDtypeStruct((8, 128, 128), F32)
    pad = pltpu.VMEM((T + 2 * HALO, 128), F32)
    seq = pltpu.VMEM((T, 128), F32)
    return pl.pallas_call(
        body, grid=(8,),
        in_specs=[col, col, col, _spec((T, 128), lambda c: (0, c0 + c)), _spec((4, 128), lambda c: (0, c)), vec, mat, vec, mat, vec, vec],
        out_specs=[col, _spec((4, 128), lambda c: (0, c)), vec, mat, vec, mat, vec, vec],
        out_shape=[jax.ShapeDtypeStruct((T, LRU_W), BF), jax.ShapeDtypeStruct((4, LRU_W), F32), vshape, mshape, vshape, mshape, vshape, vshape],
        scratch_shapes=[pad, pad, pad, seq, seq, seq, pltpu.VMEM((SEGS, 128), F32)],
        compiler_params=_params(), name="lru_bwd")(dh_out, a, hseq, proj, cw, cb, wa2, ba, wx2, bx, lam)


FF_TILE = 256
FF_TILE_ROWS = list(range(0, D_FF, FF_TILE))


def gate_up(h1, wn, w_gate, w_up):
    def body(h_ref, wn_ref, wg_hbm, wu_hbm, gt_ref, up_ref, act_ref, u_ref, wg_ref, wu_ref, wg_sems, wu_sems):
        gate_ready = _arriving(wg_hbm, wg_ref, wg_sems, FF_TILE_ROWS, FF_TILE)
        up_ready = _arriving(wu_hbm, wu_ref, wu_sems, FF_TILE_ROWS, FF_TILE)
        for r in (0, HALF):
            u_ref[r:r + HALF, :] = _rms(h_ref[r:r + HALF, :], wn_ref[...]).astype(BF)

        def tile(c0):
            cols = pl.ds(c0, FF_TILE)
            gate_ready(c0 // FF_TILE)
            up_ready(c0 // FF_TILE)
            gt = lax.dot_general(u_ref[...], wg_ref[cols, :], NT_DIMS, preferred_element_type=F32)
            up = lax.dot_general(u_ref[...], wu_ref[cols, :], NT_DIMS, preferred_element_type=F32)
            gt_ref[:, cols] = gt.astype(BF)
            up_ref[:, cols] = up.astype(BF)
            act_ref[:, cols] = (gt * _sigmoid(gt) * up).astype(BF)

        _col_tiles(D_FF, FF_TILE, tile)

    big = jax.ShapeDtypeStruct((T, D_FF), BF)
    return pl.pallas_call(
        body, grid=(T // RC,), in_specs=[_rows_spec(D), _vec(D), ANY_SPEC, ANY_SPEC],
        out_specs=[_rows_spec(D_FF), _rows_spec(D_FF), _rows_spec(D_FF), _rows_spec(D)],
        out_shape=[big, big, big, jax.ShapeDtypeStruct((T, D), BF)],
        scratch_shapes=[pltpu.VMEM((D_FF, D), BF)] * 2 + [pltpu.SemaphoreType.DMA((len(FF_TILE_ROWS),))] * 2,
        compiler_params=_params(), name="gate_up")(h1, wn, w_gate, w_up)


def down_loss(act, w_down, h1, target, wf):
    first = NPAD + N_META

    def body(a_ref, w_ref, r_ref, t_hbm, wf_ref, d_ref, db_ref, l_ref, dw_ref, h_scr, t_ref, t_sem):
        i = pl.program_id(0)
        _zero_at_first(l_ref, dw_ref)
        head = pltpu.make_async_copy(t_hbm.at[pl.ds(0, RC - first)], t_ref.at[pl.ds(first, RC - first)], t_sem)
        rest = pltpu.make_async_copy(t_hbm.at[pl.ds(pl.multiple_of(jnp.maximum(i * RC - first, 0), 32), RC)], t_ref, t_sem)

        @pl.when(i == 0)
        def _():
            t_ref[0:first, :] = jnp.zeros((first, D), F32)
            head.start()

        @pl.when(i > 0)
        def _():
            rest.start()

        def tile(c0):
            cols = pl.ds(c0, 512)
            h_scr[:, cols] = r_ref[:, cols] + jnp.dot(a_ref[...], w_ref[:, cols], preferred_element_type=F32)

        _col_tiles(D, 512, tile)

        @pl.when(i == 0)
        def _():
            head.wait()

        @pl.when(i > 0)
        def _():
            rest.wait()

        for r in (0, HALF):
            h = h_scr[r:r + HALF, :]
            live = _rows((HALF, D), i * RC + r) >= first
            err = jnp.where(live, _rms(h, wf_ref[...]) - t_ref[r:r + HALF, :], 0.0)
            l_ref[...] += 0.5 * jnp.sum(jnp.sum(err * err, axis=1, keepdims=True) * (1.0 / D), axis=0, keepdims=True)
            dh, dw = _rms_bwd(err * (1.0 / D), h, wf_ref[...])
            dw_ref[...] += jnp.sum(dw, axis=0, keepdims=True)
            d_ref[r:r + HALF, :] = dh
            db_ref[r:r + HALF, :] = dh.astype(BF)

    return pl.pallas_call(
        body, grid=(T // RC,),
        in_specs=[_rows_spec(D_FF), _whole((D_FF, D)), _rows_spec(D), pl.BlockSpec(memory_space=pl.ANY), _vec(D)],
        out_specs=[_rows_spec(D), _rows_spec(D), _spec((1, 128), lambda i: (0, 0)), _vec(D)],
        out_shape=[jax.ShapeDtypeStruct((T, D), F32), jax.ShapeDtypeStruct((T, D), BF), jax.ShapeDtypeStruct((1, 128), F32),
                   jax.ShapeDtypeStruct((1, D), F32)],
        scratch_shapes=[pltpu.VMEM((RC, D), F32), pltpu.VMEM((RC, D), F32), pltpu.SemaphoreType.DMA],
        compiler_params=_params(), name="down_loss")(act, w_down, h1, target, wf)


def swiglu_bwd(dh2_b, w_down, gt, up, act, u2):
    tn = 256

    def body(d_hbm, u_hbm, w_ref, gt_ref, up_ref, act_ref, dg_ref, du_ref, gd_ref, gg_ref, gu_ref, d_ref, u_ref, d_sems, u_sems):
        chunks = list(range(0, T, RC))
        d_ready = _arriving(d_hbm, d_ref, d_sems, chunks, RC)
        u_ready = _arriving(u_hbm, u_ref, u_sems, chunks, RC)

        def rows(r0):
            part = pl.ds(r0, RC)
            d_ready(r0 // RC)
            dact = lax.dot_general(d_ref[part, :], w_ref[...], NT_DIMS, preferred_element_type=F32)
            gt_ = gt_ref[part, :].astype(F32)
            up_ = up_ref[part, :].astype(F32)
            sg = _sigmoid(gt_)
            dg_ref[part, :] = (dact * up_ * (sg * (1.0 + gt_ * (1.0 - sg)))).astype(BF)
            du_ref[part, :] = (dact * (gt_ * sg)).astype(BF)

        _col_tiles(T, RC, rows)
        for k in range(len(chunks)):
            u_ready(k)
        gd_ref[...] = lax.dot_general(act_ref[...], d_ref[...], TN_DIMS, preferred_element_type=F32).astype(BF)
        gg_ref[...] = lax.dot_general(dg_ref[...], u_ref[...], TN_DIMS, preferred_element_type=F32).astype(BF)
        gu_ref[...] = lax.dot_general(du_ref[...], u_ref[...], TN_DIMS, preferred_element_type=F32).astype(BF)

    cols = _spec((T, tn), lambda j: (0, j))
    wrow = _spec((tn, D), lambda j: (j, 0))
    big = jax.ShapeDtypeStruct((T, D_FF), BF)
    grad = jax.ShapeDtypeStruct((D_FF, D), BF)
    return pl.pallas_call(
        body, grid=(D_FF // tn,), in_specs=[ANY_SPEC, ANY_SPEC, wrow, cols, cols, cols],
        out_specs=[cols, cols, wrow, wrow, wrow], out_shape=[big, big, grad, grad, grad],
        scratch_shapes=[pltpu.VMEM((T, D), BF)] * 2 + [pltpu.SemaphoreType.DMA((T // RC,))] * 2,
        compiler_params=_params(), name="swiglu_bwd")(dh2_b, u2, w_down, gt, up, act)


def gate_up_bwd(dgt, dup, w_gate, w_up, h1, wn, dh2):
    def body(dg_ref, du_ref, wg_hbm, wu_hbm, h_ref, wn_ref, r_ref, d_ref, db_ref, dw_ref, du_scr, wg_ref, wu_ref, wg_sems, wu_sems):
        gate_ready = _arriving(wg_hbm, wg_ref, wg_sems, FF_TILE_ROWS, FF_TILE)
        up_ready = _arriving(wu_hbm, wu_ref, wu_sems, FF_TILE_ROWS, FF_TILE)
        _zero_at_first(dw_ref)

        du_scr[...] = jnp.zeros_like(du_scr)

        def tile(c0):
            k = pl.ds(c0, FF_TILE)
            gate_ready(c0 // FF_TILE)
            up_ready(c0 // FF_TILE)
            du_scr[...] += (jnp.dot(dg_ref[:, k], wg_ref[k, :], preferred_element_type=F32)
                            + jnp.dot(du_ref[:, k], wu_ref[k, :], preferred_element_type=F32))

        _col_tiles(D_FF, FF_TILE, tile)
        for r in (0, HALF):
            dh, dw = _rms_bwd(du_scr[r:r + HALF, :], h_ref[r:r + HALF, :], wn_ref[...])
            dw_ref[...] += jnp.sum(dw, axis=0, keepdims=True)
            dh = dh + r_ref[r:r + HALF, :]
            d_ref[r:r + HALF, :] = dh
            db_ref[r:r + HALF, :] = dh.astype(BF)

    return pl.pallas_call(
        body, grid=(T // RC,),
        in_specs=[_rows_spec(D_FF), _rows_spec(D_FF), ANY_SPEC, ANY_SPEC, _rows_spec(D), _vec(D), _rows_spec(D)],
        out_specs=[_rows_spec(D), _rows_spec(D), _vec(D)],
        out_shape=[jax.ShapeDtypeStruct((T, D), F32), jax.ShapeDtypeStruct((T, D), BF), jax.ShapeDtypeStruct((1, D), F32)],
        scratch_shapes=[pltpu.VMEM((RC, D), F32)] + [pltpu.VMEM((D_FF, D), BF)] * 2 + [pltpu.SemaphoreType.DMA((len(FF_TILE_ROWS),))] * 2,
        compiler_params=_params(), name="gate_up_bwd")(dgt, dup, w_gate, w_up, h1, wn, dh2)


def _adamw(w, g, m, v):
    m = ADAM_B1 * m + (1.0 - ADAM_B1) * g
    v = ADAM_B2 * v + (1.0 - ADAM_B2) * (g * g)
    m_hat = m / (1.0 - ADAM_B1 ** ADAM_STEP)
    v_hat = v / (1.0 - ADAM_B2 ** ADAM_STEP)
    delta = -ADAM_LR * (m_hat / (jnp.sqrt(v_hat) + ADAM_EPS) + ADAM_WD * w)
    return delta, m, v


def adamw_shards(name, recvs, ws, ms, vs):
    n = len(ws)

    def body(*refs):
        ins, outs = refs[:4 * n], refs[4 * n:]
        for k in range(n):
            p_ref, w_ref, m_ref, v_ref = ins[k], ins[n + k], ins[2 * n + k], ins[3 * n + k]
            g = p_ref[0].astype(F32)
            for s in range(1, 8):
                g = g + p_ref[s].astype(F32)
            outs[4 * k][...] = g
            outs[4 * k + 1][...], outs[4 * k + 2][...], outs[4 * k + 3][...] = _adamw(w_ref[...], g, m_ref[...], v_ref[...])

    tiles = [_spec((w.shape[0] // 2, w.shape[1]), lambda i: (i, 0)) for w in ws]
    recv_tiles = [_spec((8, w.shape[0] // 2, w.shape[1]), lambda i: (0, i, 0)) for w in ws]
    res = pl.pallas_call(
        body, grid=(2,), in_specs=recv_tiles + tiles * 3,
        out_specs=[t for t in tiles for _ in range(4)],
        out_shape=[jax.ShapeDtypeStruct(w.shape, F32) for w in ws for _ in range(4)],
        compiler_params=_params(), name=name)(*recvs, *ws, *ms, *vs)
    return [list(res[4 * k:4 * k + 4]) for k in range(n)]


def adamw_w_in(recv, w, m, v):
    rows = 34
    per_row = D // 128

    def body(p_ref, w_ref, m_ref, v_ref, g_ref, d_ref, mo_ref, vo_ref):
        def chunk(c, carry):
            lines = pl.ds(pl.multiple_of(c * per_row * rows, 16), per_row * rows)
            g = p_ref[0, lines, :].astype(F32)
            for s in range(1, 8):
                g = g + p_ref[s, lines, :].astype(F32)
            g = g.reshape(rows, per_row, 128)
            part = pl.ds(c * rows, rows)
            g_ref[part] = g
            d_ref[part], mo_ref[part], vo_ref[part] = _adamw(w_ref[part], g, m_ref[part], v_ref[part])
            return carry

        lax.fori_loop(0, w.shape[0] // rows, chunk, 0)

    shape = jax.ShapeDtypeStruct(w.shape, F32)
    return pl.pallas_call(body, out_shape=[shape] * 4, compiler_params=_params(0), name="adamw_w_in")(recv, w, m, v)


def sum_slabs(recv):
    def body(p_ref, o_ref):
        g = p_ref[0]
        for s in range(1, 8):
            g = g + p_ref[s]
        o_ref[...] = g

    return pl.pallas_call(body, out_shape=jax.ShapeDtypeStruct(recv.shape[1:], F32), compiler_params=_params(0), name="sum_slabs")(recv)


SIMPLE = [("norm1_w", 1024), ("ssd_conv_b", 1536), ("ssd_dt_bias", 16), ("ssd_a_log", 16), ("ssd_d", 16), ("ssd_norm_w", 1024),
          ("lru_conv_b", 1024), ("lru_ba", 1024), ("lru_bx", 1024), ("lru_lambda", 1024), ("lru_norm_w", 1024), ("norm2_w", 1024),
          ("final_norm_w", 1024)]
SPECIAL = ["lru_wa", "lru_wx", "meta_tokens", "ssd_conv_w", "lru_conv_w"]
SM_ROWS = 176
SM_WA, SM_WX, SM_META, SM_SCW, SM_LCW, SM_LOSS = 14, 78, 142, 158, 166, 170


def _simple_rows():
    rows, r = {}, 0
    for name, n in SIMPLE:
        rows[name] = r
        r += -(-n // 1024)
    return rows


def adamw_small(sm, special_g, ws, ms, vs):
    rows = _simple_rows()
    ns, nx = len(SIMPLE), len(SPECIAL)

    def body(*refs):
        sm_ref = refs[0]
        gx = refs[1:1 + nx]
        wr = refs[1 + nx:1 + nx + ns + nx]
        mr = refs[1 + nx + ns + nx:1 + nx + 2 * (ns + nx)]
        vr = refs[1 + nx + 2 * (ns + nx):1 + nx + 3 * (ns + nx)]
        outs = refs[1 + nx + 3 * (ns + nx):]
        o = 0
        for k, (name, n) in enumerate(SIMPLE):
            r0 = rows[name]
            for c0 in range(0, n, 1024):
                wd = min(1024, n - c0)
                g = sm_ref[r0 + c0 // 1024:r0 + c0 // 1024 + 1, 0:wd]
                sl = (slice(None), slice(c0, c0 + wd))
                d, m2, v2 = _adamw(wr[k][sl], g, mr[k][sl], vr[k][sl])
                outs[o][sl] = g
                outs[o + 1][sl] = d
                outs[o + 2][sl] = m2
                outs[o + 3][sl] = v2
            o += 4
        for k in range(nx):
            d, m2, v2 = _adamw(wr[ns + k][...], gx[k][...], mr[ns + k][...], vr[ns + k][...])
            outs[o][...] = d
            outs[o + 1][...] = m2
            outs[o + 2][...] = v2
            o += 3

    out_shape = []
    for k in range(ns):
        out_shape += [jax.ShapeDtypeStruct(ws[k].shape, F32)] * 4
    for k in range(nx):
        out_shape += [jax.ShapeDtypeStruct(ws[ns + k].shape, F32)] * 3
    return pl.pallas_call(body, out_shape=out_shape, compiler_params=_params(0), name="adamw_small")(sm, *special_g, *ws, *ms, *vs)


def _place():
    return lax.axis_index("x"), lax.axis_index("y"), lax.axis_index("c")


def _index(px, py, pc):
    return 4 * px + 2 * py + pc


def all_gather(name, shards):
    n = len(shards)
    hbm = pl.BlockSpec(memory_space=pl.ANY)

    def body(*refs):
        ins, outs = refs[:n], refs[n:2 * n]
        send_sems, recv_sems, local_sems = refs[2 * n:]
        x, y, c = _place()
        me, sibling = (x, y, c), (x, y, 1 - c)
        chips = [(1 - x, y), (x, 1 - y), (1 - x, 1 - y)]

        def copy(i, k, block, to, src=None):
            dst = outs[i].at[_index(*block)]
            return pltpu.make_async_remote_copy(src_ref=dst if src is None else src, dst_ref=dst, send_sem=send_sems.at[7 * i + k],
                                                recv_sem=recv_sems.at[7 * i + k], device_id=to, device_id_type=MESH)

        mine = [pltpu.make_async_copy(ins[i], outs[i].at[_index(*me)], local_sems.at[i]) for i in range(n)]
        for cp in mine:
            cp.start()
        first = []
        for i in range(n):
            first += [copy(i, 1 + j, me, (*chip, c), src=ins[i]) for j, chip in enumerate(chips)]
            first.append(copy(i, 0, me, sibling, src=ins[i]))
        for cp in first:
            cp.start()
        passed = []
        for i in range(n):
            for j, chip in enumerate(chips):
                copy(i, 1 + j, (*chip, c), me).wait_recv()
                cp = copy(i, 4 + j, (*chip, c), sibling)
                cp.start()
                passed.append(cp)
        for i in range(n):
            copy(i, 0, sibling, me).wait_recv()
            for j, chip in enumerate(chips):
                copy(i, 4 + j, (*chip, 1 - c), me).wait_recv()
        for cp in first + passed:
            cp.wait_send()
        for cp in mine:
            cp.wait()

    return pl.pallas_call(
        body, in_specs=[hbm] * n, out_specs=[hbm] * n,
        out_shape=[jax.ShapeDtypeStruct((8,) + s.shape, s.dtype) for s in shards],
        scratch_shapes=[pltpu.SemaphoreType.DMA((7 * n,)), pltpu.SemaphoreType.DMA((7 * n,)), pltpu.SemaphoreType.DMA((n,))],
        name=name)(*shards)


HBM_SPEC = pl.BlockSpec(memory_space=pltpu.HBM)
SEM_SPEC = pl.BlockSpec(memory_space=pltpu.SEMAPHORE)
EFFECT = pltpu.SideEffectType.DATAFLOW_SIDE_EFFECTING


def _peers(x, y, c):
    return [((1 - x) if k & 4 else x, (1 - y) if k & 2 else y, (1 - c) if k & 1 else c) for k in range(1, 8)]


def _pieces(rows):
    for n in (4, 2):
        if rows % (16 * n) == 0:
            return [(r * (rows // n), rows // n) for r in range(n)]
    return [(0, rows)]


def _peer_copies(src, land, send_sems, recv_sems, k, peer, mine, slab_src):
    block = src.at[_index(*peer)] if slab_src else src
    return [pltpu.make_async_remote_copy(src_ref=block.at[pl.ds(r0, nr)], dst_ref=land.at[mine, pl.ds(r0, nr)], send_sem=send_sems.at[k],
                                         recv_sem=recv_sems.at[k], device_id=peer, device_id_type=MESH)
            for r0, nr in _pieces(block.shape[0])]


def copies_start(name, srcs, slab_src, after):
    n = len(srcs)
    zones = [jax.ShapeDtypeStruct(s.shape if slab_src else (8,) + s.shape, s.dtype) for s in srcs]
    afters = [] if after is None else [after]

    def body(*refs):
        ins, lands = refs[:n], refs[n:2 * n]
        first = 2 * n + len(afters)
        sends, recvs = refs[first:first + n], refs[first + n:first + 2 * n]
        token = refs[-1]
        x, y, c = _place()
        mine = _index(x, y, c)
        for i in range(n):
            per_peer = [_peer_copies(ins[i], lands[i], sends[i], recvs[i], k, peer, mine, slab_src) for k, peer in enumerate(_peers(x, y, c))]
            for piece in zip(*per_peer):
                for cp in piece:
                    cp.start()
        token[...] = jnp.zeros_like(token)

    sem = pltpu.SemaphoreType.DMA((7,))
    res = pl.pallas_call(
        body, name=name,
        out_shape=([sem] * (2 * n) + [pltpu.HBM(s.shape, s.dtype) for s in srcs] + [pltpu.HBM(z.shape, z.dtype) for z in zones]
                   + [jax.ShapeDtypeStruct((8, 128), F32)]),
        in_specs=[HBM_SPEC] * (2 * n) + [pl.BlockSpec(memory_space=pl.ANY)] * len(afters),
        out_specs=[SEM_SPEC] * (2 * n) + [HBM_SPEC] * (2 * n) + [pl.BlockSpec(memory_space=pltpu.VMEM)],
        input_output_aliases={i: 2 * n + i for i in range(2 * n)},
        compiler_params=pltpu.CompilerParams(has_side_effects=EFFECT),
    )(*[pltpu.with_memory_space_constraint(s, pltpu.HBM) for s in srcs],
      *[pltpu.with_memory_space_constraint(lax.empty(z.shape, z.dtype), pltpu.HBM) for z in zones], *afters)
    return [(res[i], res[n + i], res[2 * n + i], res[3 * n + i]) for i in range(n)], res[-1][0:1, 0:1]


def copies_wait(name, started, slab_src, after):
    n = len(started)

    def body(*refs):
        ins, lands = refs[:n], refs[n:2 * n]
        sends, recvs = refs[2 * n:3 * n], refs[3 * n:4 * n]
        x, y, c = _place()
        mine = _index(x, y, c)
        for i in range(n):
            for k, peer in enumerate(_peers(x, y, c)):
                arrival = pltpu.make_async_remote_copy(src_ref=ins[i].at[mine] if slab_src else ins[i], dst_ref=lands[i].at[_index(*peer)],
                                                       send_sem=sends[i].at[k], recv_sem=recvs[i].at[k], device_id=peer, device_id_type=MESH)
                arrival.wait_send()
                arrival.wait_recv()

    srcs = [s[2] for s in started]
    lands = [s[3] for s in started]
    afters = list(after) if isinstance(after, (list, tuple)) else [after]
    res = pl.pallas_call(
        body, name=name,
        out_shape=[pltpu.HBM(s.shape, s.dtype) for s in srcs] + [pltpu.HBM(z.shape, z.dtype) for z in lands],
        in_specs=[HBM_SPEC] * (2 * n) + [SEM_SPEC] * (2 * n) + [pl.BlockSpec(memory_space=pl.ANY)] * len(afters),
        out_specs=[HBM_SPEC] * (2 * n),
        input_output_aliases={i: i for i in range(2 * n)},
        compiler_params=pltpu.CompilerParams(has_side_effects=EFFECT),
    )(*srcs, *lands, *[s[0] for s in started], *[s[1] for s in started], *afters)
    me = _index(*_place())
    own = [lax.dynamic_index_in_dim(s, me, 0, keepdims=True) if slab_src else s[None] for s in res[:n]]
    return [lax.dynamic_update_slice_in_dim(z, o, me, 0) for z, o in zip(res[n:], own)]


def _hop(src, land, send_sems, recv_sems, k, block, to):
    dst = land.at[_index(*block)]
    return pltpu.make_async_remote_copy(src_ref=dst if src is None else src, dst_ref=dst, send_sem=send_sems.at[k], recv_sem=recv_sems.at[k],
                                        device_id=to, device_id_type=MESH)


def _other_chips(x, y):
    return [(1 - x, y), (x, 1 - y), (1 - x, 1 - y)]


def gather_start(name, shards, after):
    n = len(shards)

    def body(*refs):
        ins, lands = refs[:n], refs[n:2 * n]
        sends, recvs = refs[2 * n + 1:3 * n + 1], refs[3 * n + 1:4 * n + 1]
        token = refs[-1]
        x, y, c = _place()
        for i in range(n):
            for j, chip in enumerate(_other_chips(x, y)):
                _hop(ins[i], lands[i], sends[i], recvs[i], 1 + j, (x, y, c), (*chip, c)).start()
            _hop(ins[i], lands[i], sends[i], recvs[i], 0, (x, y, c), (x, y, 1 - c)).start()
        token[...] = jnp.zeros_like(token)

    own, passing = pltpu.SemaphoreType.DMA((4,)), pltpu.SemaphoreType.DMA((3,))
    zones = [jax.ShapeDtypeStruct((8,) + s.shape, s.dtype) for s in shards]
    res = pl.pallas_call(
        body, name=name,
        out_shape=([own] * (2 * n) + [passing] * (2 * n) + [pltpu.HBM(s.shape, s.dtype) for s in shards]
                   + [pltpu.HBM(z.shape, z.dtype) for z in zones] + [jax.ShapeDtypeStruct((8, 128), F32)]),
        in_specs=[HBM_SPEC] * (2 * n) + [ANY_SPEC],
        out_specs=[SEM_SPEC] * (4 * n) + [HBM_SPEC] * (2 * n) + [pl.BlockSpec(memory_space=pltpu.VMEM)],
        input_output_aliases={i: 4 * n + i for i in range(2 * n)},
        compiler_params=pltpu.CompilerParams(has_side_effects=EFFECT),
    )(*[pltpu.with_memory_space_constraint(s, pltpu.HBM) for s in shards],
      *[pltpu.with_memory_space_constraint(lax.empty(z.shape, z.dtype), pltpu.HBM) for z in zones], after)
    return [[res[4 * n + i], res[5 * n + i], res[i], res[n + i], res[2 * n + i], res[3 * n + i]] for i in range(n)], res[-1][0:1, 0:1]


def gather_stage(name, pass_on, finish, after):
    arrays = pass_on + finish
    n = len(arrays)

    def body(*refs):
        ins, lands = refs[:n], refs[n:2 * n]
        sems = [refs[(2 + q) * n:(3 + q) * n] for q in range(4)]
        x, y, c = _place()
        me, sibling = (x, y, c), (x, y, 1 - c)
        for i in range(len(pass_on)):
            send, recv, send_on, recv_on = (q[i] for q in sems)
            for j, chip in enumerate(_other_chips(x, y)):
                _hop(None, lands[i], send, recv, 1 + j, (*chip, c), me).wait_recv()
                _hop(None, lands[i], send_on, recv_on, j, (*chip, c), sibling).start()
        for i in range(len(pass_on), n):
            send, recv, send_on, recv_on = (q[i] for q in sems)
            _hop(ins[i], lands[i], send, recv, 0, sibling, me).wait_recv()
            for j, chip in enumerate(_other_chips(x, y)):
                _hop(None, lands[i], send_on, recv_on, j, (*chip, 1 - c), me).wait_recv()
            _hop(ins[i], lands[i], send, recv, 0, me, sibling).wait_send()
            for j, chip in enumerate(_other_chips(x, y)):
                _hop(ins[i], lands[i], send, recv, 1 + j, me, (*chip, c)).wait_send()
                _hop(None, lands[i], send_on, recv_on, j, (*chip, c), sibling).wait_send()

    res = pl.pallas_call(
        body, name=name,
        out_shape=[pltpu.HBM(a[0].shape, a[0].dtype) for a in arrays] + [pltpu.HBM(a[1].shape, a[1].dtype) for a in arrays],
        in_specs=[HBM_SPEC] * (2 * n) + [SEM_SPEC] * (4 * n) + [ANY_SPEC],
        out_specs=[HBM_SPEC] * (2 * n),
        input_output_aliases={i: i for i in range(2 * n)},
        compiler_params=pltpu.CompilerParams(has_side_effects=EFFECT),
    )(*[a[0] for a in arrays], *[a[1] for a in arrays], *[a[2 + q] for q in range(4) for a in arrays], after)
    for i, a in enumerate(arrays):
        a[0], a[1] = res[i], res[n + i]
    me = _index(*_place())
    return [lax.dynamic_update_slice_in_dim(a[1], a[0][None], me, 0) for a in finish]


WEIGHTS = ["meta_tokens", "norm1_w", "w_in", "ssd_conv_w", "ssd_conv_b", "ssd_dt_bias", "ssd_a_log", "ssd_d", "ssd_norm_w", "lru_conv_w",
           "lru_conv_b", "lru_wa", "lru_ba", "lru_wx", "lru_bx", "lru_lambda", "lru_norm_w", "w_out", "norm2_w", "w_gate", "w_up", "w_down",
           "final_norm_w"]
BIG = ["w_in", "w_out", "w_gate", "w_up", "w_down"]
COLUMN_SHARDED = ["w_in", "w_gate", "w_up"]


def _pair_blocks(w):
    w = w.reshape(8, 2, 64, 64)
    z = jnp.zeros((8, 64, 64), w.dtype)
    return jnp.concatenate([jnp.concatenate([w[:, 0], z], axis=2), jnp.concatenate([z, w[:, 1]], axis=2)], axis=1)


def _unpair_blocks(w2):
    return jnp.stack([w2[:, :64, :64], w2[:, 64:, 64:]], axis=1).reshape(16, 64, 64)


def _per_group(v):
    return jnp.pad(v.reshape(2, 1, 8), ((0, 0), (0, 0), (0, 120)))


def _pad_cols(v, n):
    return jnp.pad(v, ((0, 0), (0, n - v.shape[1])))


def local_step(x, target, meta, ssd_cw, lru_cw, w_in, fetch, send, p):
    z120 = jnp.zeros((120, D), BF)
    w_dt = jnp.concatenate([w_in[2560:2568], z120, w_in[2568:2576], z120], axis=0)
    bias2, alog2, d2 = _per_group(p["ssd_dt_bias"]), _per_group(p["ssd_a_log"]), _per_group(p["ssd_d"])
    wa2 = _pair_blocks(p["lru_wa"]).astype(BF)
    wx2 = _pair_blocks(p["lru_wx"]).astype(BF)
    lru = (lru_cw, p["lru_conv_b"], wa2, p["lru_ba"], wx2, p["lru_bx"], p["lru_lambda"])

    h0 = jnp.concatenate([jnp.zeros((NPAD, D), F32), meta, x], axis=0)
    proj, dt_raw, u1 = in_proj(h0, p["norm1_w"], w_in, w_dt)
    yn_ssd, y_pre, h_prev = ssd_fwd(proj, dt_raw, ssd_cw, p["ssd_conv_b"], bias2, alog2, d2, p["ssd_norm_w"])
    fetch([], yn_ssd)
    hseq, a = lru_fwd(proj, *lru)
    (w_out,) = fetch(["w_out"], hseq)
    h1, cat = out_proj(yn_ssd, proj, hseq, p["lru_norm_w"], w_out, h0)
    w_gate, w_up = fetch(["w_gate", "w_up"], h1)
    gt, up, act, u2 = gate_up(h1, p["norm2_w"], w_gate, w_up)
    (w_down,) = fetch(["w_down"], act)
    dh2, dh2_b, loss, d_fnw = down_loss(act, w_down, h1, target, p["final_norm_w"])

    dgt, dup, g_down, g_gate, g_up = swiglu_bwd(dh2_b, w_down, gt, up, act, u2)
    sent = send({"w_down": g_down, "w_gate": g_gate, "w_up": g_up})
    dh1, dh1_b, d_n2 = gate_up_bwd(dgt, dup, w_gate, w_up, h1, p["norm2_w"] + sent, dh2)
    sent = send({"w_out": weight_grad("dw_out", cat, dh1_b)})
    dyn, dh_out, dg_b, d_lnw = out_proj_bwd(dh1_b, w_out, proj, hseq, p["lru_norm_w"] + sent)

    dxl_b, d_lcw, d_lcb, dwa2, d_ba, dwx2, d_bx, d_lam = lru_bwd(dh_out, a, hseq, proj, *lru)
    dz_b, dxbc_b, ddt_b, dpar, d_snw, d_scw, d_scb = ssd_bwd(dyn, proj, dt_raw, ssd_cw, p["ssd_conv_b"], y_pre, h_prev, bias2, alog2, d2,
                                                             p["ssd_norm_w"] + sent)
    sent = send({"w_in": in_weight_grad(dz_b, dg_b, dxl_b, dxbc_b, ddt_b, u1)})
    grad_x, d_meta, d_n1 = in_proj_bwd(dz_b, dg_b, dxl_b, dxbc_b, ddt_b, w_in, w_dt, h0, p["norm1_w"] + sent, dh1)
    small = {"norm1_w": d_n1, "ssd_conv_b": d_scb, "ssd_dt_bias": dpar[:, 0, :8].reshape(1, 16), "ssd_a_log": dpar[:, 1, :8].reshape(1, 16),
             "ssd_d": dpar[:, 2, :8].reshape(1, 16), "ssd_norm_w": d_snw, "lru_conv_b": d_lcb, "lru_ba": d_ba, "lru_bx": d_bx,
             "lru_lambda": d_lam, "lru_norm_w": d_lnw, "norm2_w": d_n2, "final_norm_w": d_fnw,
             "lru_wa": _unpair_blocks(dwa2), "lru_wx": _unpair_blocks(dwx2), "meta_tokens": d_meta,
             "ssd_conv_w": d_scw, "lru_conv_w": d_lcw}
    return loss, grad_x, small


def _pack_small(small, loss):
    rows = [_pad_cols(small[name], -(-n // 1024) * 1024).reshape(-1, 1024) for name, n in SIMPLE]
    rows += [small["lru_wa"].reshape(64, 1024), small["lru_wx"].reshape(64, 1024), small["meta_tokens"],
             _pad_cols(small["ssd_conv_w"], 2048).reshape(8, 1024), small["lru_conv_w"], _pad_cols(loss[:, 0:1], 1024)]
    sm = jnp.concatenate(rows, axis=0)
    return jnp.pad(sm, ((0, SM_ROWS - sm.shape[0]), (0, 0)))


def _slabs(g):
    return g.reshape(8, g.shape[0] // 8, g.shape[1])


def _unslab(g):
    return g.reshape(8 * g.shape[1], g.shape[2])


def kernel(x, meta_tokens, norm1_w, w_in, ssd_conv_w, ssd_conv_b, ssd_dt_bias, ssd_a_log, ssd_d, ssd_norm_w, lru_conv_w, lru_conv_b, lru_wa, lru_ba, lru_wx, lru_bx, lru_lambda, lru_norm_w, w_out, norm2_w, w_gate, w_up, w_down, final_norm_w, loss_target, m_meta_tokens, m_norm1_w, m_w_in, m_ssd_conv_w, m_ssd_conv_b, m_ssd_dt_bias, m_ssd_a_log, m_ssd_d, m_ssd_norm_w, m_lru_conv_w, m_lru_conv_b, m_lru_wa, m_lru_ba, m_lru_wx, m_lru_bx, m_lru_lambda, m_lru_norm_w, m_w_out, m_norm2_w, m_w_gate, m_w_up, m_w_down, m_final_norm_w, v_meta_tokens, v_norm1_w, v_w_in, v_ssd_conv_w, v_ssd_conv_b, v_ssd_dt_bias, v_ssd_a_log, v_ssd_d, v_ssd_norm_w, v_lru_conv_w, v_lru_conv_b, v_lru_wa, v_lru_ba, v_lru_wx, v_lru_bx, v_lru_lambda, v_lru_norm_w, v_w_out, v_norm2_w, v_w_gate, v_w_up, v_w_down, v_final_norm_w):
    w = dict(meta_tokens=meta_tokens, norm1_w=norm1_w, w_in=w_in[0], ssd_conv_w=ssd_conv_w[0], ssd_conv_b=ssd_conv_b, ssd_dt_bias=ssd_dt_bias,
             ssd_a_log=ssd_a_log, ssd_d=ssd_d, ssd_norm_w=ssd_norm_w, lru_conv_w=lru_conv_w[0], lru_conv_b=lru_conv_b, lru_wa=lru_wa[0],
             lru_ba=lru_ba, lru_wx=lru_wx[0], lru_bx=lru_bx, lru_lambda=lru_lambda, lru_norm_w=lru_norm_w, w_out=w_out[0], norm2_w=norm2_w,
             w_gate=w_gate[0], w_up=w_up[0], w_down=w_down[0], final_norm_w=final_norm_w.reshape(1, D))
    m = dict(meta_tokens=m_meta_tokens, norm1_w=m_norm1_w, w_in=m_w_in[0], ssd_conv_w=m_ssd_conv_w[0], ssd_conv_b=m_ssd_conv_b,
             ssd_dt_bias=m_ssd_dt_bias, ssd_a_log=m_ssd_a_log, ssd_d=m_ssd_d, ssd_norm_w=m_ssd_norm_w, lru_conv_w=m_lru_conv_w[0],
             lru_conv_b=m_lru_conv_b, lru_wa=m_lru_wa[0], lru_ba=m_lru_ba, lru_wx=m_lru_wx[0], lru_bx=m_lru_bx, lru_lambda=m_lru_lambda,
             lru_norm_w=m_lru_norm_w, w_out=m_w_out[0], norm2_w=m_norm2_w, w_gate=m_w_gate[0], w_up=m_w_up[0], w_down=m_w_down[0],
             final_norm_w=m_final_norm_w.reshape(1, D))
    v = dict(meta_tokens=v_meta_tokens, norm1_w=v_norm1_w, w_in=v_w_in[0], ssd_conv_w=v_ssd_conv_w[0], ssd_conv_b=v_ssd_conv_b,
             ssd_dt_bias=v_ssd_dt_bias, ssd_a_log=v_ssd_a_log, ssd_d=v_ssd_d, ssd_norm_w=v_ssd_norm_w, lru_conv_w=v_lru_conv_w[0],
             lru_conv_b=v_lru_conv_b, lru_wa=v_lru_wa[0], lru_ba=v_lru_ba, lru_wx=v_lru_wx[0], lru_bx=v_lru_bx, lru_lambda=v_lru_lambda,
             lru_norm_w=v_lru_norm_w, w_out=v_w_out[0], norm2_w=v_norm2_w, w_gate=v_w_gate[0], w_up=v_w_up[0], w_down=v_w_down[0],
             final_norm_w=v_final_norm_w.reshape(1, D))
    shapes = dict(meta_tokens=meta_tokens.shape, norm1_w=norm1_w.shape, w_in=w_in.shape, ssd_conv_w=ssd_conv_w.shape,
                  ssd_conv_b=ssd_conv_b.shape, ssd_dt_bias=ssd_dt_bias.shape, ssd_a_log=ssd_a_log.shape, ssd_d=ssd_d.shape,
                  ssd_norm_w=ssd_norm_w.shape, lru_conv_w=lru_conv_w.shape, lru_conv_b=lru_conv_b.shape, lru_wa=lru_wa.shape,
                  lru_ba=lru_ba.shape, lru_wx=lru_wx.shape, lru_bx=lru_bx.shape, lru_lambda=lru_lambda.shape, lru_norm_w=lru_norm_w.shape,
                  w_out=w_out.shape, norm2_w=norm2_w.shape, w_gate=w_gate.shape, w_up=w_up.shape, w_down=w_down.shape,
                  final_norm_w=final_norm_w.shape)
    me = _index(*_place())
    for n in COLUMN_SHARDED:
        w[n], m[n], v[n] = w[n].T, m[n].T, v[n].T

    small_shard = jnp.concatenate([w["meta_tokens"], _pad_cols(w["ssd_conv_w"], 256).reshape(8, 128), w["lru_conv_w"],
                                   jnp.zeros((4, 128), F32)], axis=0)
    g_in, gs = all_gather("gather_w_in", [w["w_in"].astype(BF), small_shard])
    later = ["w_out", "w_gate", "w_up", "w_down"]
    started, behind = gather_start("gather_rest_start", [w[n].astype(BF) for n in later], gs)
    started = dict(zip(later, started))
    passed_on = {None: ["w_out", "w_gate", "w_up"], "w_out": ["w_down"], "w_gate": [], "w_down": []}
    meta_full = gs[:, 0:16].transpose(1, 0, 2).reshape(N_META, D)
    ssd_cw = gs[:, 16:24].reshape(8, 4, 256)[:, :, :192].transpose(1, 0, 2).reshape(4, XBC)
    lru_cw = gs[:, 24:28].transpose(1, 0, 2).reshape(4, LRU_W)

    def fetch(names, after):
        first = names[0] if names else None
        got = gather_stage("gather_" + (first + "_wait" if names else "pass_on"), [started[n] for n in passed_on[first]],
                           [started[n] for n in names], after)
        return [_unslab(g) for g in got]

    in_flight = {}

    def send(grads):
        names = list(grads)
        st, token = copies_start("grads_" + names[0] + "_start", [grads[n] if n == "small" else _slabs(grads[n]) for n in names], True, None)
        in_flight.update(zip(names, st))
        return token

    loss, grad_x, small = local_step(x[0], loss_target[0], meta_full, ssd_cw, lru_cw, _unslab(g_in), fetch, send,
                                     {**w, "norm1_w": w["norm1_w"] + behind})
    send({"small": _pack_small(small, loss).reshape(8, SM_ROWS // 8, 1024)})

    out = {}
    early = ["w_down", "w_gate", "w_up", "w_out"]
    recv = dict(zip(early, copies_wait("grads_early_wait", [in_flight[n] for n in early], True, in_flight["small"][2])))
    for pair in (early[:2], early[2:]):
        done = adamw_shards("adamw_" + pair[0], [recv[n] for n in pair], [w[n] for n in pair], [m[n] for n in pair], [v[n] for n in pair])
        out.update(zip(pair, done))
    recv_in, recv_small = copies_wait("grads_late_wait", [in_flight["w_in"], in_flight["small"]], True, [out[n][0] for n in early])
    def lines(a):
        return jnp.transpose(a.reshape(D // 128, 128, IN_COLS // 8), (2, 0, 1))

    out["w_in"] = [jnp.transpose(o, (1, 2, 0)).reshape(D, IN_COLS // 8) for o in adamw_w_in(recv_in, lines(w_in), lines(m_w_in), lines(v_w_in))]
    for n in ("w_gate", "w_up"):
        out[n] = [o.T for o in out[n]]
    sm = all_gather("gather_small_grads", [sum_slabs(recv_small)])[0].reshape(SM_ROWS, 1024)
    special_g =[sm[SM_WA:SM_WA + 64].reshape(16, 64, 64), sm[SM_WX:SM_WX + 64].reshape(16, 64, 64),
                 lax.dynamic_slice(sm[SM_META:SM_META + 16], (0, 128 * me), (16, 128)),
                 lax.dynamic_slice(sm[SM_SCW:SM_SCW + 8].reshape(4, 2048), (0, 192 * me), (4, 192)),
                 lax.dynamic_slice(sm[SM_LCW:SM_LCW + 4], (0, 128 * me), (4, 128))]
    names = [n for n, _ in SIMPLE] + SPECIAL
    res = adamw_small(sm, special_g, [w[n] for n in names], [m[n] for n in names], [v[n] for n in names])
    for k, (n, _) in enumerate(SIMPLE):
        out[n] = res[4 * k:4 * k + 4]
    for k, n in enumerate(SPECIAL):
        o = 4 * len(SIMPLE) + 3 * k
        out[n] = [special_g[k]] + list(res[o:o + 3])
    loss_total = sm[SM_LOSS, 0]
    flat = [loss_total, grad_x[None]]
    for k in range(4):
        flat += [out[n][k].reshape(shapes[n]) for n in WEIGHTS]
    return tuple(flat)
```

```python
import math

import jax
import jax.numpy as jnp
from jax import lax
from jax.experimental import pallas as pl
from jax.experimental.pallas import tpu as pltpu

F32 = jnp.float32
BF = jnp.bfloat16

D = 1024
SEQ = 2048
N_META = 16
Q = 128
NPAD = 112
T = NPAD + N_META + SEQ
NCH = T // Q
RC = 544
D_FF = 2816
SSD_W = 1024
LRU_W = 1024
XBC = 1536
IN_COLS = 4624
PZ, PG, PXL, PXBC = 0, 1024, 2048, 3072
NP_IN = 4608
EPS = 1e-6
LRU_C = 8.0
VMEM_LIMIT = 56 * 1024 * 1024

ADAM_LR, ADAM_B1, ADAM_B2, ADAM_EPS, ADAM_WD, ADAM_STEP = 0.001, 0.9, 0.999, 1e-08, 0.01, 10

NT_DIMS = (((1,), (1,)), ((), ()))
TN_DIMS = (((0,), (0,)), ((), ()))
MESH = pl.DeviceIdType.MESH


def _params(n_grid=1, limit=VMEM_LIMIT):
    return pltpu.CompilerParams(dimension_semantics=("arbitrary",) * n_grid, vmem_limit_bytes=limit)


def _spec(shape, imap, single=False):
    if single:
        return pl.BlockSpec(shape, imap, pipeline_mode=pl.Buffered(1))
    return pl.BlockSpec(shape, imap)


def _sigmoid(x):
    return 0.5 * jnp.tanh(0.5 * x) + 0.5


def _sigmoid_gate(x):
    return 1.0 / (1.0 + jnp.exp(-x))


def _softplus(x):
    return jnp.maximum(x, 0.0) + jnp.log(1.0 + jnp.exp(-jnp.abs(x)))


def _rms_stats(h):
    return lax.rsqrt(jnp.mean(h * h, axis=-1, keepdims=True) + EPS)


def _rms(h, w):
    return (h * _rms_stats(h)) * w


def _rms_bwd(du, h, w):
    r = _rms_stats(h)
    n = h * r
    dn = du * w
    dh = r * (dn - n * jnp.mean(dn * n, axis=-1, keepdims=True))
    return dh, du * n


_G0 = math.sqrt(2.0 / math.pi)


def _gelu(x):
    return 0.5 * x * (1.0 + jnp.tanh(_G0 * (x + 0.044715 * (x * x * x))))


def _gelu_grad(x):
    t = jnp.tanh(_G0 * (x + 0.044715 * (x * x * x)))
    return 0.5 * (1.0 + t) + 0.5 * x * (1.0 - t * t) * (_G0 * (1.0 + 3.0 * 0.044715 * (x * x)))


def _rows(shape, r0=0):
    return lax.broadcasted_iota(jnp.int32, shape, 0) + r0


def _lanes(shape):
    return lax.broadcasted_iota(jnp.int32, shape, 1)


HALO = 8


def _fill_padded(pad_ref, x_ref):
    pad_ref[0:HALO, :] = jnp.zeros((HALO, pad_ref.shape[1]), F32)
    pad_ref[T + HALO:T + 2 * HALO, :] = jnp.zeros((HALO, pad_ref.shape[1]), F32)

    def step(c, carry):
        r0 = pl.multiple_of(c * Q, Q)
        pad_ref[pl.ds(r0 + HALO, Q), :] = x_ref[pl.ds(r0, Q), :].astype(F32)
        return carry

    lax.fori_loop(0, NCH, step, 0)


def _back(pad_ref, r0):
    win = pad_ref[pl.ds(r0, Q + HALO), :]
    return lambda s: win[HALO:, :] if s == 0 else pltpu.roll(win, s, axis=0)[HALO:, :]


def _ahead(pad_ref, r0):
    win = pad_ref[pl.ds(r0 + HALO, Q + HALO), :]
    return lambda s: win[:Q, :] if s == 0 else pltpu.roll(win, Q + HALO - s, axis=0)[:Q, :]


def _conv(back, w, b):
    y = b + w[3:4, :] * back(0)
    for k in range(3):
        y = y + w[k:k + 1, :] * back(3 - k)
    return y


def _conv_bwd_x(ahead, w):
    dx = w[3:4, :] * ahead(0)
    for k in range(3):
        dx = dx + w[k:k + 1, :] * ahead(3 - k)
    return dx


def _conv_bwd_w(dy, back):
    dws = [jnp.sum(dy * back(3 - k), axis=0, keepdims=True) for k in range(4)]
    return jnp.concatenate(dws, axis=0), jnp.sum(dy, axis=0, keepdims=True)


def _chunks(fn, unrolled=False):
    if unrolled:
        for c in range(NCH):
            fn(c * Q)
        return

    def step(c, carry):
        fn(pl.multiple_of(c * Q, Q))
        return carry

    lax.fori_loop(0, NCH, step, 0)


HALF = RC // 2


def _col_tiles(n, tn, fn):
    def step(j, carry):
        fn(pl.multiple_of(j * tn, tn))
        return carry

    lax.fori_loop(0, n // tn, step, 0)


def _rows_spec(cols, block_col=0):
    return _spec((RC, cols), lambda i: (i, block_col))


def _whole(shape):
    return _spec(shape, lambda i: tuple(0 for _ in shape), single=True)


def _vec(cols):
    return _spec((1, cols), lambda i: (0, 0))


def _zero_at_first(*refs):
    @pl.when(pl.program_id(0) == 0)
    def _():
        for r in refs:
            r[...] = jnp.zeros_like(r)


ANY_SPEC = pl.BlockSpec(memory_space=pl.ANY)


def _arriving(src, dst, sems, starts, rows):
    n, ahead = len(starts), 2
    first = pl.program_id(0) == 0

    def piece(k):
        r0 = starts[0]
        for j in range(1, n):
            r0 = jnp.where(k == j, starts[j], r0)
        at = pl.ds(pl.multiple_of(r0, 16), rows)
        return pltpu.make_async_copy(src.at[at], dst.at[at], sems.at[k])

    @pl.when(first)
    def _():
        for k in range(min(ahead, n)):
            piece(k).start()

    def ready(k):
        k = jnp.asarray(k, jnp.int32)

        @pl.when(first)
        def _():
            piece(k).wait()

            @pl.when(k + ahead < n)
            def _():
                piece(k + ahead).start()

    return ready


IN_RUNS = ((PZ, 0, 1024), (PXBC, 1024, XBC), (PG, 2576, 2048))
IN_TILE = 512
IN_TILE_ROWS = [wrow + IN_TILE * j for _, wrow, width in IN_RUNS for j in range(width // IN_TILE)]


def _in_tiles(fn):
    done = 0
    for pcol, wrow, width in IN_RUNS:
        def step(j, carry, pcol=pcol, wrow=wrow, done=done):
            fn(pl.multiple_of(pcol + j * IN_TILE, IN_TILE), pl.multiple_of(wrow + j * IN_TILE, 16), done + j)
            return carry

        lax.fori_loop(0, width // IN_TILE, step, 0)
        done += width // IN_TILE


def in_proj(h0, wn, w_t, w_dt, after):
    def body(h_ref, wn_ref, w_hbm, wdt_ref, _after, o_ref, dt_ref, u_ref, w_ref, w_sems):
        ready = _arriving(w_hbm, w_ref, w_sems, IN_TILE_ROWS, IN_TILE)
        for r in (0, HALF):
            u_ref[r:r + HALF, :] = _rms(h_ref[r:r + HALF, :], wn_ref[...]).astype(BF)

        def tile(pcol, wrow, k):
            ready(k)
            o_ref[:, pl.ds(pcol, IN_TILE)] = lax.dot_general(u_ref[...], w_ref[pl.ds(wrow, IN_TILE), :], NT_DIMS,
                                                             preferred_element_type=F32).astype(BF)

        _in_tiles(tile)
        dt_ref[...] = lax.dot_general(u_ref[...], wdt_ref[...], NT_DIMS, preferred_element_type=F32)

    return pl.pallas_call(
        body, grid=(T // RC,), in_specs=[_rows_spec(D), _vec(D), ANY_SPEC, _whole((256, D)), ANY_SPEC],
        out_specs=[_rows_spec(NP_IN), _rows_spec(256), _rows_spec(D)],
        out_shape=[jax.ShapeDtypeStruct((T, NP_IN), BF), jax.ShapeDtypeStruct((T, 256), F32), jax.ShapeDtypeStruct((T, D), BF)],
        scratch_shapes=[pltpu.VMEM((IN_COLS, D), BF), pltpu.SemaphoreType.DMA((len(IN_TILE_ROWS),))],
        compiler_params=_params(), name="in_proj")(h0, wn, w_t, w_dt, after)


def out_proj(yn_ssd, proj, hseq, lru_nw, w_out, h0):
    def body(y_ref, g_ref, h_ref, wn_ref, w_ref, r_ref, o_ref, cat_ref):
        cat_ref[:, 0:SSD_W] = y_ref[...]
        for r in (0, HALF):
            y = _gelu(g_ref[r:r + HALF, :].astype(F32)) * h_ref[r:r + HALF, :]
            cat_ref[r:r + HALF, SSD_W:] = _rms(y, wn_ref[...]).astype(BF)

        def tile(c0):
            o_ref[:, pl.ds(c0, 512)] = r_ref[:, pl.ds(c0, 512)] + jnp.dot(cat_ref[...], w_ref[:, pl.ds(c0, 512)], preferred_element_type=F32)

        _col_tiles(D, 512, tile)

    return pl.pallas_call(
        body, grid=(T // RC,),
        in_specs=[_rows_spec(SSD_W), _rows_spec(LRU_W, PG // LRU_W), _rows_spec(LRU_W), _vec(LRU_W), _whole((SSD_W + LRU_W, D)), _rows_spec(D)],
        out_specs=[_rows_spec(D), _rows_spec(SSD_W + LRU_W)],
        out_shape=[jax.ShapeDtypeStruct((T, D), F32), jax.ShapeDtypeStruct((T, SSD_W + LRU_W), BF)],
        compiler_params=_params(), name="out_proj")(yn_ssd, proj, hseq, lru_nw, w_out, h0)


def out_proj_bwd(dh1_b, w_out, proj, hseq, lru_nw, after):
    def body(d_ref, w_ref, g_ref, h_ref, wn_ref, _after, dy_ref, dh_ref, dg_ref, dw_ref, dl_scr):
        _zero_at_first(dw_ref)

        def tile(c0):
            dy_ref[:, pl.ds(c0, 512)] = lax.dot_general(d_ref[...], w_ref[pl.ds(c0, 512), :], NT_DIMS, preferred_element_type=F32)
            dl_scr[:, pl.ds(c0, 512)] = lax.dot_general(d_ref[...], w_ref[pl.ds(SSD_W + c0, 512), :], NT_DIMS, preferred_element_type=F32)

        _col_tiles(SSD_W, 512, tile)

        for r in (0, HALF):
            g = g_ref[r:r + HALF, :].astype(F32)
            h = h_ref[r:r + HALF, :]
            ge = _gelu(g)
            dy, dw = _rms_bwd(dl_scr[r:r + HALF, :], ge * h, wn_ref[...])
            dw_ref[...] += jnp.sum(dw, axis=0, keepdims=True)
            dh_ref[r:r + HALF, :] = dy * ge
            dg_ref[r:r + HALF, :] = (dy * h * _gelu_grad(g)).astype(BF)

    return pl.pallas_call(
        body, grid=(T // RC,),
        in_specs=[_rows_spec(D), _whole((SSD_W + LRU_W, D)), _rows_spec(LRU_W, PG // LRU_W), _rows_spec(LRU_W), _vec(LRU_W), ANY_SPEC],
        out_specs=[_rows_spec(SSD_W), _rows_spec(LRU_W), _rows_spec(LRU_W), _vec(LRU_W)],
        out_shape=[jax.ShapeDtypeStruct((T, SSD_W), F32), jax.ShapeDtypeStruct((T, LRU_W), F32), jax.ShapeDtypeStruct((T, LRU_W), BF),
                   jax.ShapeDtypeStruct((1, LRU_W), F32)],
        scratch_shapes=[pltpu.VMEM((RC, LRU_W), F32)],
        compiler_params=_params(), name="out_proj_bwd")(dh1_b, w_out, proj, hseq, lru_nw, after)


def in_proj_bwd(dz, dg, dxl, dxbc, ddt, w_t, w_dt, h0, wn, dh1, after):
    first = NPAD + N_META

    def body(dz_ref, dg_ref, dxl_ref, dxbc_ref, ddt_ref, w_hbm, wdt_ref, h_ref, wn_ref, r_ref, _after, gx_hbm, meta_ref, dw_ref, du_scr, o_ref, sem,
             w_ref, w_sems):
        i = pl.program_id(0)
        ready = _arriving(w_hbm, w_ref, w_sems, IN_TILE_ROWS, IN_TILE)
        _zero_at_first(dw_ref)
        du_scr[...] = jnp.dot(ddt_ref[...], wdt_ref[...], preferred_element_type=F32)
        done = 0
        for d_ref, wrow, width in ((dz_ref, 0, 1024), (dxbc_ref, 1024, XBC), (dg_ref, 2576, 1024), (dxl_ref, 3600, 1024)):
            def step(j, carry, d_ref=d_ref, wrow=wrow, done=done):
                c0 = pl.multiple_of(j * IN_TILE, IN_TILE)
                ready(done + j)
                du_scr[...] += jnp.dot(d_ref[:, pl.ds(c0, IN_TILE)], w_ref[pl.ds(pl.multiple_of(wrow + c0, 16), IN_TILE), :],
                                       preferred_element_type=F32)
                return carry

            lax.fori_loop(0, width // IN_TILE, step, 0)
            done += width // IN_TILE
        for r in (0, HALF):
            dh, dw = _rms_bwd(du_scr[r:r + HALF, :], h_ref[r:r + HALF, :], wn_ref[...])
            dw_ref[...] += jnp.sum(dw, axis=0, keepdims=True)
            o_ref[r:r + HALF, :] = dh + r_ref[r:r + HALF, :]

        @pl.when(i == 0)
        def _():
            meta_ref[...] = o_ref[NPAD:first, :]
            head = pltpu.make_async_copy(o_ref.at[pl.ds(first, RC - first)], gx_hbm.at[pl.ds(0, RC - first)], sem)
            head.start()
            head.wait()

        @pl.when(i > 0)
        def _():
            rest = pltpu.make_async_copy(o_ref, gx_hbm.at[pl.ds(pl.multiple_of(i * RC - first, 32), RC)], sem)
            rest.start()
            rest.wait()

    return pl.pallas_call(
        body, grid=(T // RC,),
        in_specs=[_rows_spec(SSD_W), _rows_spec(LRU_W), _rows_spec(LRU_W), _rows_spec(XBC), _rows_spec(256), ANY_SPEC,
                  _whole((256, D)), _rows_spec(D), _vec(D), _rows_spec(D), ANY_SPEC],
        out_specs=[ANY_SPEC, _spec((N_META, D), lambda i: (0, 0)), _vec(D)],
        out_shape=[jax.ShapeDtypeStruct((SEQ, D), F32), jax.ShapeDtypeStruct((N_META, D), F32), jax.ShapeDtypeStruct((1, D), F32)],
        scratch_shapes=[pltpu.VMEM((RC, D), F32), pltpu.VMEM((RC, D), F32), pltpu.SemaphoreType.DMA,
                        pltpu.VMEM((IN_COLS, D), BF), pltpu.SemaphoreType.DMA((len(IN_TILE_ROWS),))],
        compiler_params=_params(), name="in_proj_bwd")(dz, dg, dxl, dxbc, ddt, w_t, w_dt, h0, wn, dh1, after)


GRAD_TILE = 256


def weight_grad(name, a, u1):
    tm = GRAD_TILE

    def body(a_ref, u_ref, o_ref):
        o_ref[...] = lax.dot_general(a_ref[...], u_ref[...], TN_DIMS, preferred_element_type=F32).astype(BF)

    return pl.pallas_call(
        body, grid=(a.shape[1] // tm,),
        in_specs=[_spec((T, tm), lambda j: (0, j)), _spec((T, D), lambda j: (0, 0), single=True)],
        out_specs=_spec((tm, D), lambda j: (j, 0)),
        out_shape=jax.ShapeDtypeStruct((a.shape[1], D), BF),
        compiler_params=_params(), name=name)(a, u1)


def in_weight_grad(dz, dg, dxl, dxbc, ddt, u1):
    tm = GRAD_TILE
    per_row = D // 128
    parts = (dz, dg, dxl, dxbc, ddt)
    first_rows = (0, 2576, 3600, 1024, 2560)
    tiles = [p.shape[1] // tm for p in parts]
    starts = [sum(tiles[:k]) for k in range(len(parts))]
    last = sum(tiles) - 1
    dt_lines = 8 * per_row

    def body(*refs):
        a_refs, u_ref, o_hbm, mix_scr, stage, sems = refs[:5], refs[5], refs[6], refs[7], refs[8], refs[9]
        step = pl.program_id(0)
        slot = step % 2
        line0 = 0
        for a_ref, start, n, first in zip(a_refs, starts, tiles, first_rows):
            here = (step >= start) & (step < start + n)
            line0 = jnp.where(here, per_row * (first + tm * (step - start)), line0)

            @pl.when(here)
            def _(a_ref=a_ref):
                res = lax.dot_general(a_ref[...], u_ref[...], TN_DIMS, preferred_element_type=F32)
                for q in range(per_row):
                    mix_scr[pl.ds(q, tm, stride=per_row), :] = res[:, 128 * q:128 * q + 128]

        def tile_copy(of_slot, to):
            return pltpu.make_async_copy(stage.at[of_slot], o_hbm.at[pl.ds(to, per_row * tm)], sems.at[of_slot])

        @pl.when(step >= 2)
        def _():
            tile_copy(slot, 0).wait()

        stage[slot] = mix_scr[...].astype(BF)

        @pl.when(step < last)
        def _():
            tile_copy(slot, pl.multiple_of(line0, 128)).start()

        @pl.when(step == last)
        def _():
            halves = [pltpu.make_async_copy(stage.at[slot, pl.ds(128 * per_row * k, dt_lines)],
                                            o_hbm.at[pl.ds(per_row * (first_rows[-1] + 8 * k), dt_lines)], sems.at[2 + k]) for k in range(2)]
            for cp in halves:
                cp.start()
            tile_copy(1 - slot, 0).wait()
            for cp in halves:
                cp.wait()

    def tile_of(start, n):
        return lambda j: (0, jnp.clip(j - start, 0, n - 1))

    return pl.pallas_call(
        body, grid=(last + 1,),
        in_specs=[_spec((T, tm), tile_of(s, n)) for s, n in zip(starts, tiles)] + [_spec((T, D), lambda j: (0, 0), single=True)],
        out_specs=pl.BlockSpec(memory_space=pl.ANY),
        out_shape=jax.ShapeDtypeStruct((per_row * IN_COLS, 128), BF),
        scratch_shapes=[pltpu.VMEM((per_row * tm, 128), F32), pltpu.VMEM((2, per_row * tm, 128), BF), pltpu.SemaphoreType.DMA((4,))],
        compiler_params=_params(), name="dw_in")(*parts, u1)


def _ssd_chunk_common(row0, dt_ref, b_ref, c_ref, bias, a_neg):
    shape = (Q, Q)
    lane = _lanes(shape)
    sub = _rows(shape)
    live = (_rows(shape, row0) >= NPAD) & (lane < 8)
    dtr = dt_ref[:, :]
    dt = jnp.where(live, _softplus(dtr + bias), 0.0)
    d_a = dt * a_neg
    tri = (sub >= lane).astype(F32)
    cs = jnp.dot(tri, d_a, precision=lax.Precision.HIGHEST, preferred_element_type=F32)
    cs_t = cs.T
    b_f = b_ref[:, :]
    bc = b_f.astype(BF)
    cc = c_ref[:, :].astype(BF)
    cb = lax.dot_general(cc, bc, NT_DIMS, preferred_element_type=F32)
    cs_last = cs[Q - 1:Q, :]
    return dict(lane=lane, sub=sub, live=live, dtr=dtr, dt=dt, cs=cs, cs_t=cs_t, bc=bc, cc=cc, cb=cb, bc_t=b_f.T.astype(BF),
                ecs=jnp.exp(cs), dsm=jnp.exp(cs_last - cs), gam=jnp.exp(cs_last))


def _pair(lane_even, mat, j):
    return jnp.where(lane_even, mat[:, j:j + 1], mat[:, j + 1:j + 2])


def _pair_row(lane_even, mat, j):
    return jnp.where(lane_even[0:1, :], mat[:, j:j + 1], mat[:, j + 1:j + 2])


def _head_decay(cm, j):
    seg = cm["cs"][:, j:j + 1] - cm["cs_t"][j:j + 1, :]
    return jnp.exp(jnp.where(cm["sub"] >= cm["lane"], seg, -jnp.inf))


def _head_decay_t(cm, j):
    seg = cm["cs_t"][j:j + 1, :] - cm["cs"][:, j:j + 1]
    return jnp.exp(jnp.where(cm["lane"] >= cm["sub"], seg, -jnp.inf))


def _conv_window(raw_ref, halo_ref, pad_scr):
    pad_scr[0:HALO, :] = halo_ref[...].astype(F32)[halo_ref.shape[0] - HALO:, :]
    pad_scr[HALO:HALO + Q, :] = raw_ref[...].astype(F32)
    win = pad_scr[...]
    return lambda s: win[HALO:, :] if s == 0 else pltpu.roll(win, s, axis=0)[HALO:, :]


def _xbc_cols(g):
    return slice(512 * g, 512 * g + 512), slice(SSD_W + 128 * g, SSD_W + 128 * g + 128), slice(SSD_W + 256 + 128 * g, SSD_W + 384 + 128 * g)


def ssd_fwd(proj, dt_raw, conv_w, conv_b, dt_bias2, a_log2, d2, norm_w):
    def body(raw_ref, halo_ref, dt_all, z_all, cw_ref, cb_ref, bias_all, alog_all, d_all, nw_all, yn_all, y_all, hp_all,
             h_all, pad_scr, act_scr):
        @pl.when(pl.program_id(0) == 0)
        def _():
            h_all[...] = jnp.zeros_like(h_all)

        pre = _conv(_conv_window(raw_ref, halo_ref, pad_scr), cw_ref[...], cb_ref[...])
        act_scr[...] = pre * _sigmoid(pre)
        for g in range(2):
            wide, thin = slice(512 * g, 512 * g + 512), slice(128 * g, 128 * g + 128)
            xs, bs, cs = _xbc_cols(g)
            group(act_scr.at[:, xs], act_scr.at[:, bs], act_scr.at[:, cs], dt_all.at[:, thin], z_all.at[:, wide], bias_all.at[g],
                  alog_all.at[g], d_all.at[g], nw_all.at[:, wide], yn_all.at[:, wide], y_all.at[:, wide], hp_all.at[g, 0], h_all.at[g])

    def group(x_ref, b_ref, c_ref, dt_ref, z_ref, bias_ref, alog_ref, d_ref, nw_ref, yn_ref, y_ref, hp_ref, h_scr):
        bias = bias_ref[...]
        a_neg = -jnp.exp(alog_ref[...])
        dsk = d_ref[...]
        cm = _ssd_chunk_common(pl.program_id(0) * Q, dt_ref, b_ref, c_ref, bias, a_neg)
        lane_even = cm["lane"] < 64
        for p in range(4):
            je, jo = 2 * p, 2 * p + 1
            xp = x_ref[:, 128 * p:128 * p + 128]
            xdt = xp * _pair(lane_even, cm["dt"], je)
            xdt_b = xdt.astype(BF)
            m_e = (cm["cb"] * _head_decay(cm, je)).astype(BF)
            m_o = (cm["cb"] * _head_decay(cm, jo)).astype(BF)
            zero = jnp.zeros_like(xdt_b)
            yd = (jnp.dot(m_e, jnp.where(lane_even, xdt_b, zero), preferred_element_type=F32)
                  + jnp.dot(m_o, jnp.where(lane_even, zero, xdt_b), preferred_element_type=F32))
            hp = h_scr[p]
            hp_ref[p] = hp
            yo = jnp.dot(cm["cc"], hp.astype(BF), preferred_element_type=F32) * _pair(lane_even, cm["ecs"], je)
            y_ref[:, 128 * p:128 * p + 128] = yd + yo + xp * _pair_row(lane_even, dsk, je)
            st = jnp.dot(cm["bc_t"], (xdt * _pair(lane_even, cm["dsm"], je)).astype(BF), preferred_element_type=F32)
            h_scr[p] = hp * _pair_row(lane_even, cm["gam"], je) + st
        zc = z_ref[:, :].astype(F32)
        gated = y_ref[:, :] * (zc * _sigmoid(zc))
        yn_ref[:, :] = _rms(gated, nw_ref[...]).astype(BF)

    par = _spec((2, 1, 128), lambda c: (0, 0, 0))
    wide = _spec((Q, SSD_W), lambda c: (c, 0))
    xbc = PXBC // XBC
    halo = 2 * HALO
    return pl.pallas_call(
        body, grid=(NCH,),
        in_specs=[_spec((Q, XBC), lambda c: (c, xbc)), _spec((halo, XBC), lambda c: (jnp.maximum(c * (Q // halo) - 1, 0), xbc)),
                  _spec((Q, 256), lambda c: (c, 0)), wide, _spec((4, XBC), lambda c: (0, 0)), _spec((1, XBC), lambda c: (0, 0)),
                  par, par, par, _spec((1, SSD_W), lambda c: (0, 0))],
        out_specs=[wide, wide, _spec((2, 1, 4, 128, 128), lambda c: (0, c, 0, 0, 0))],
        out_shape=[jax.ShapeDtypeStruct((T, SSD_W), BF), jax.ShapeDtypeStruct((T, SSD_W), F32),
                   jax.ShapeDtypeStruct((2, NCH, 4, 128, 128), F32)],
        scratch_shapes=[pltpu.VMEM((2, 4, 128, 128), F32), pltpu.VMEM((Q + HALO, XBC), F32), pltpu.VMEM((Q, XBC), F32)],
        compiler_params=_params(), name="ssd_fwd")(proj, proj, dt_raw, proj, conv_w, conv_b, dt_bias2, a_log2, d2, norm_w)


def ssd_bwd(dyn, proj, dt_raw, conv_w, conv_b, y_pre, h_prev, dt_bias2, a_log2, d2, norm_w, after):
    def body(dyn_all, raw_ref, halo_ref, dt_all, z_all, y_all, hp_all, cw_ref, cb_ref, bias_all, alog_all, d_all, nw_all, _after,
             dz_all, dxbc_ref, ddt_all, dpar_all, dnw_all, dcw_ref, dcb_ref, dh_all, acc_all, pad_scr, act_scr, dsilu_scr, dact_scr, dpad_scr):
        @pl.when(pl.program_id(0) == 0)
        def _():
            dh_all[...] = jnp.zeros_like(dh_all)
            acc_all[...] = jnp.zeros_like(acc_all)
            dnw_all[...] = jnp.zeros_like(dnw_all)
            dcw_ref[...] = jnp.zeros_like(dcw_ref)
            dcb_ref[...] = jnp.zeros_like(dcb_ref)
            dpad_scr[Q:Q + HALO, :] = jnp.zeros((HALO, XBC), F32)

        back = _conv_window(raw_ref, halo_ref, pad_scr)
        pre = _conv(back, cw_ref[...], cb_ref[...])
        sg = _sigmoid(pre)
        act_scr[...] = pre * sg
        dsilu_scr[...] = sg * (1.0 + pre * (1.0 - sg))
        for g in range(2):
            wide, thin = slice(512 * g, 512 * g + 512), slice(128 * g, 128 * g + 128)
            xs, bs, cs = _xbc_cols(g)
            group(dyn_all.at[:, wide], act_scr.at[:, xs], act_scr.at[:, bs], act_scr.at[:, cs], dt_all.at[:, thin], z_all.at[:, wide],
                  y_all.at[:, wide], hp_all.at[g, 0], bias_all.at[g], alog_all.at[g], d_all.at[g], nw_all.at[:, wide],
                  dz_all.at[:, wide], dact_scr.at[:, xs], dact_scr.at[:, bs], dact_scr.at[:, cs], ddt_all.at[:, thin], dpar_all.at[g],
                  dnw_all.at[:, wide], dh_all.at[g], acc_all.at[g])
        dpre = dact_scr[...] * dsilu_scr[...]
        dcw, dcb = _conv_bwd_w(dpre, back)
        dcw_ref[...] += dcw
        dcb_ref[...] += dcb
        dpad_scr[0:Q, :] = dpre
        win = dpad_scr[...]
        dxbc_ref[...] = _conv_bwd_x(lambda s: win[:Q, :] if s == 0 else pltpu.roll(win, Q + HALO - s, axis=0)[:Q, :], cw_ref[...]).astype(BF)
        dpad_scr[Q:Q + HALO, :] = dpre[0:HALO, :]

    def group(dyn_ref, x_ref, b_ref, c_ref, dt_ref, z_ref, y_ref, hp_ref, bias_ref, alog_ref, d_ref, nw_ref,
              dz_ref, dx_ref, db_ref, dc_ref, ddt_ref, dpar_ref, dnw_ref, dh_scr, acc_scr):
        ci = pl.program_id(0)
        bias = bias_ref[...]
        a_neg = -jnp.exp(alog_ref[...])
        dsk = d_ref[...]
        cm = _ssd_chunk_common((NCH - 1 - ci) * Q, dt_ref, b_ref, c_ref, bias, a_neg)
        lane, sub = cm["lane"], cm["sub"]
        lane_even = lane < 64
        cc_t = c_ref[:, :].T.astype(BF)
        cb_t = lax.dot_general(cm["bc"], cm["cc"], NT_DIMS, preferred_element_type=F32)
        zc = z_ref[:, :].astype(F32)
        yc = y_ref[:, :]
        sg = _sigmoid(zc)
        sz = zc * sg
        dgated, dnw = _rms_bwd(dyn_ref[:, :], yc * sz, nw_ref[...])
        dnw_ref[...] += jnp.sum(dnw, axis=0, keepdims=True)
        dz_ref[:, :] = (dgated * yc * (sg * (1.0 + zc * (1.0 - sg)))).astype(BF)
        dy_all = dgated * sz
        dcb = jnp.zeros((Q, Q), F32)
        dcb_t = jnp.zeros((Q, Q), F32)
        db_acc = jnp.zeros((Q, Q), F32)
        dc_acc = jnp.zeros((Q, Q), F32)
        dcs = jnp.zeros((Q, Q), F32)
        ddt = jnp.zeros((Q, Q), F32)
        for p in range(4):
            je, jo = 2 * p, 2 * p + 1
            xp = x_ref[:, 128 * p:128 * p + 128]
            dy = dy_all[:, 128 * p:128 * p + 128]
            dt_p = _pair(lane_even, cm["dt"], je)
            xdt = xp * dt_p
            xdt_b = xdt.astype(BF)
            dy_b = dy.astype(BF)
            zero = jnp.zeros_like(dy_b)
            hp = hp_ref[p]
            hp_b = hp.astype(BF)
            dh = dh_scr[p]
            dh_b = dh.astype(BF)
            acc_scr[p:p + 1, :] += jnp.sum(dy * xp, axis=0, keepdims=True)
            dxp = dy * _pair_row(lane_even, dsk, je)
            e_p = _pair(lane_even, cm["ecs"], je)
            g_p = jnp.dot(cm["cc"], hp_b, preferred_element_type=F32)
            dg_b = (dy * e_p).astype(BF)
            de = dy * g_p * e_p
            dc_acc = dc_acc + lax.dot_general(dg_b, hp_b, NT_DIMS, preferred_element_type=F32)
            dh_in = jnp.dot(cc_t, dg_b, preferred_element_type=F32)
            ds_p = _pair(lane_even, cm["dsm"], je)
            r_p = jnp.dot(cm["bc"], dh_b, preferred_element_type=F32)
            dxdt = r_p * ds_p
            tt = r_p * xdt * ds_p
            db_acc = db_acc + lax.dot_general((xdt * ds_p).astype(BF), dh_b, NT_DIMS, preferred_element_type=F32)
            dgam_m = jnp.sum(dh * hp, axis=0, keepdims=True)
            for j, even in ((je, True), (jo, False)):
                sel = lane_even if even else jnp.logical_not(lane_even)
                dy_j = jnp.where(sel, dy_b, zero)
                l_j = _head_decay(cm, j)
                l_jt = _head_decay_t(cm, j)
                m_j = cm["cb"] * l_j
                m_jt = cb_t * l_jt
                dm = lax.dot_general(dy_j, xdt_b, NT_DIMS, preferred_element_type=F32)
                dm_t = lax.dot_general(xdt_b, dy_j, NT_DIMS, preferred_element_type=F32)
                dxdt = dxdt + jnp.dot(m_jt.astype(BF), dy_j, preferred_element_type=F32)
                dcb = dcb + dm * l_j
                dcb_t = dcb_t + dm_t * l_jt
                t_j = jnp.where(sel, tt, 0.0)
                col = jnp.sum(dm * m_j - dm_t * m_jt + (jnp.where(sel, de, 0.0) - t_j), axis=1, keepdims=True)
                gam_j = cm["gam"][:, j:j + 1]
                last = (jnp.sum(jnp.sum(t_j, axis=0, keepdims=True), axis=1, keepdims=True)
                        + jnp.sum(jnp.where(sel[0:1, :], dgam_m, 0.0), axis=1, keepdims=True) * gam_j)
                col = col + jnp.where(sub[:, 0:1] == Q - 1, last, 0.0)
                dcs = dcs + jnp.where(lane == j, col, 0.0)
            dh_scr[p] = dh_in + dh * _pair_row(lane_even, cm["gam"], je)
            dx_ref[:, 128 * p:128 * p + 128] = dxp + dxdt * dt_p
            dd = dxdt * xp
            ddt = ddt + jnp.where(lane == je, jnp.sum(jnp.where(lane_even, dd, 0.0), axis=1, keepdims=True), 0.0)
            ddt = ddt + jnp.where(lane == jo, jnp.sum(jnp.where(lane_even, 0.0, dd), axis=1, keepdims=True), 0.0)
        dc_ref[:, :] = dc_acc + jnp.dot(dcb.astype(BF), cm["bc"], preferred_element_type=F32)
        db_ref[:, :] = db_acc + jnp.dot(dcb_t.astype(BF), cm["cc"], preferred_element_type=F32)
        tri_t = (sub <= lane).astype(F32)
        dd_a = jnp.dot(tri_t, dcs, precision=lax.Precision.HIGHEST, preferred_element_type=F32)
        ddt = ddt + dd_a * a_neg
        acc_scr[5:6, :] += jnp.sum(dd_a * cm["dt"], axis=0, keepdims=True)
        draw = jnp.where(cm["live"], ddt * _sigmoid_gate(cm["dtr"] + bias), 0.0)
        acc_scr[4:5, :] += jnp.sum(draw, axis=0, keepdims=True)
        ddt_ref[:, :] = draw.astype(BF)

        @pl.when(ci == NCH - 1)
        def _():
            lane1 = _lanes((1, 128))
            dd = jnp.zeros((1, 128), F32)
            for p in range(4):
                row = acc_scr[p:p + 1, :]
                dd = dd + jnp.where(lane1 == 2 * p, jnp.sum(jnp.where(lane1 < 64, row, 0.0), axis=1, keepdims=True), 0.0)
                dd = dd + jnp.where(lane1 == 2 * p + 1, jnp.sum(jnp.where(lane1 < 64, 0.0, row), axis=1, keepdims=True), 0.0)
            dpar_ref[...] = jnp.concatenate([acc_scr[4:5, :], acc_scr[5:6, :] * a_neg, dd, jnp.zeros((5, 128), F32)], axis=0)

    par = _spec((2, 1, 128), lambda c: (0, 0, 0))
    wide = _spec((Q, SSD_W), lambda c: (NCH - 1 - c, 0))
    thin = _spec((Q, 256), lambda c: (NCH - 1 - c, 0))
    vec = _spec((1, SSD_W), lambda c: (0, 0))
    xbc = PXBC // XBC
    halo = 2 * HALO
    chunk = pltpu.VMEM((Q, XBC), F32)
    padded = pltpu.VMEM((Q + HALO, XBC), F32)
    return pl.pallas_call(
        body, grid=(NCH,),
        in_specs=[wide, _spec((Q, XBC), lambda c: (NCH - 1 - c, xbc)),
                  _spec((halo, XBC), lambda c: (jnp.maximum((NCH - 1 - c) * (Q // halo) - 1, 0), xbc)), thin, wide, wide,
                  _spec((2, 1, 4, 128, 128), lambda c: (0, NCH - 1 - c, 0, 0, 0)), _spec((4, XBC), lambda c: (0, 0)),
                  _spec((1, XBC), lambda c: (0, 0)), par, par, par, vec, ANY_SPEC],
        out_specs=[wide, _spec((Q, XBC), lambda c: (NCH - 1 - c, 0)), thin, _spec((2, 8, 128), lambda c: (0, 0, 0)), vec,
                   _spec((4, XBC), lambda c: (0, 0)), _spec((1, XBC), lambda c: (0, 0))],
        out_shape=[jax.ShapeDtypeStruct((T, SSD_W), BF), jax.ShapeDtypeStruct((T, XBC), BF), jax.ShapeDtypeStruct((T, 256), BF),
                   jax.ShapeDtypeStruct((2, 8, 128), F32), jax.ShapeDtypeStruct((1, SSD_W), F32), jax.ShapeDtypeStruct((4, XBC), F32),
                   jax.ShapeDtypeStruct((1, XBC), F32)],
        scratch_shapes=[pltpu.VMEM((2, 4, 128, 128), F32), pltpu.VMEM((2, 8, 128), F32), padded, chunk, chunk, chunk, padded],
        compiler_params=_params(), name="ssd_bwd")(dyn, proj, proj, dt_raw, proj, y_pre, h_prev, conv_w, conv_b, dt_bias2, a_log2, d2, norm_w, after)


def _lru_gates(back, cw, cb, wa, ba, wx, bx, lam):
    xr = _conv(back, cw, cb)
    xr_b = xr.astype(BF)
    r = _sigmoid_gate(jnp.dot(xr_b, wa, preferred_element_type=F32) + ba)
    i = _sigmoid_gate(jnp.dot(xr_b, wx, preferred_element_type=F32) + bx)
    sp = _softplus(-lam)
    la = (-LRU_C) * r * sp
    a = jnp.exp(la)
    mult2 = -jnp.tanh(la) * (a * a + 1.0)
    return xr, xr_b, r, i, sp, a, jnp.sqrt(mult2), mult2


SEG_LEN = 68
SEGS = T // SEG_LEN


def _seg_rows(j, k, off=0):
    return pl.ds(off + j * 8 * SEG_LEN + k, 8, stride=SEG_LEN)


def _segmented_scan(mul_ref, mul_row0, add_ref, out_ref, loc_scr, prod_scr, carry_scr, reverse):
    groups = SEGS // 8
    off = mul_row0 + (1 if reverse else 0)

    def local(i, carry):
        k = SEG_LEN - 1 - i if reverse else i
        new = []
        for j in range(groups):
            h, p = carry[2 * j], carry[2 * j + 1]
            m = mul_ref[_seg_rows(j, k, off), :]
            h = m * h + add_ref[_seg_rows(j, k), :]
            p = m * p
            loc_scr[_seg_rows(j, k), :] = h
            prod_scr[_seg_rows(j, k), :] = p
            new += [h, p]
        return tuple(new)

    lax.fori_loop(0, SEG_LEN, local, (jnp.zeros((8, 128), F32), jnp.ones((8, 128), F32)) * groups)

    def chain(i, c):
        s = SEGS - 1 - i if reverse else i
        carry_scr[pl.ds(s, 1), :] = c
        edge = s * SEG_LEN + (0 if reverse else SEG_LEN - 1)
        return loc_scr[pl.ds(edge, 1), :] + prod_scr[pl.ds(edge, 1), :] * c

    lax.fori_loop(0, SEGS, chain, jnp.zeros((1, 128), F32))

    def fold(k, carry):
        for j in range(groups):
            rows = _seg_rows(j, k)
            out_ref[rows, :] = loc_scr[rows, :] + prod_scr[rows, :] * carry_scr[8 * j:8 * j + 8, :]
        return carry

    lax.fori_loop(0, SEG_LEN, fold, 0)


def lru_fwd(proj, cw, cb, wa2, ba, wx2, bx, lam, after):
    def body(x_ref, cw_ref, cb_ref, wa_ref, ba_ref, wx_ref, bx_ref, lam_ref, _after, h_ref, a_ref, xpad, u_scr, loc_scr, prod_scr, carry_scr):
        _fill_padded(xpad, x_ref)

        def chunk(r0):
            xr, _, _, i, _, a, mult, _ = _lru_gates(_back(xpad, r0), cw_ref[...], cb_ref[...], wa_ref[0], ba_ref[...], wx_ref[0], bx_ref[...],
                                                 lam_ref[...])
            a_ref[pl.ds(r0, Q), :] = a
            u_scr[pl.ds(r0, Q), :] = jnp.where(_rows(a.shape, r0) >= NPAD, mult * (i * xr), 0.0)

        _chunks(chunk, unrolled=True)
        _segmented_scan(a_ref, 0, u_scr, h_ref, loc_scr, prod_scr, carry_scr, reverse=False)

    c0 = PXL // 128
    vec = _spec((1, 128), lambda c: (0, c))
    mat = _spec((1, 128, 128), lambda c: (c, 0, 0))
    seq = pltpu.VMEM((T, 128), F32)
    return pl.pallas_call(
        body, grid=(8,),
        in_specs=[_spec((T, 128), lambda c: (0, c0 + c)), _spec((4, 128), lambda c: (0, c)), vec, mat, vec, mat, vec, vec, ANY_SPEC],
        out_specs=[_spec((T, 128), lambda c: (0, c)), _spec((T, 128), lambda c: (0, c))],
        out_shape=[jax.ShapeDtypeStruct((T, LRU_W), F32), jax.ShapeDtypeStruct((T, LRU_W), F32)],
        scratch_shapes=[pltpu.VMEM((T + 2 * HALO, 128), F32), seq, seq, seq, pltpu.VMEM((SEGS, 128), F32)],
        compiler_params=_params(), name="lru_fwd")(proj, cw, cb, wa2, ba, wx2, bx, lam, after)


def lru_bwd(dh_out, a, hseq, proj, cw, cb, wa2, ba, wx2, bx, lam):
    def body(d_ref, a_ref, h_ref, x_ref, cw_ref, cb_ref, wa_ref, ba_ref, wx_ref, bx_ref, lam_ref,
             dx_ref, dcw_ref, dcb_ref, dwa_ref, dba_ref, dwx_ref, dbx_ref, dlam_ref, xpad, hpad, dpad, dh_ref, loc_scr, prod_scr, carry_scr):
        _fill_padded(dpad, a_ref)
        _segmented_scan(dpad, HALO, d_ref, dh_ref, loc_scr, prod_scr, carry_scr, reverse=True)
        _fill_padded(xpad, x_ref)
        _fill_padded(hpad, h_ref)
        dpad[0:HALO, :] = jnp.zeros((HALO, 128), F32)
        dpad[T + HALO:T + 2 * HALO, :] = jnp.zeros((HALO, 128), F32)
        for ref in (dcw_ref, dcb_ref, dwa_ref, dba_ref, dwx_ref, dbx_ref, dlam_ref):
            ref[...] = jnp.zeros_like(ref)
        lam = lam_ref[...]

        def first(r0):
            back = _back(xpad, r0)
            xr, xr_b, r, i, sp, a, mult, mult2 = _lru_gates(back, cw_ref[...], cb_ref[...], wa_ref[0], ba_ref[...], wx_ref[0], bx_ref[...], lam)
            dh = dh_ref[pl.ds(r0, Q), :]
            da = dh * _back(hpad, r0)(1)
            du = jnp.where(_rows(dh.shape, r0) >= NPAD, dh, 0.0)
            dmult = du * (i * xr)
            di = du * (mult * xr)
            dxr = du * (mult * i)
            dla = da * a - dmult * (a * a) * lax.rsqrt(mult2)
            dr = dla * ((-LRU_C) * sp)
            dlam_ref[...] += jnp.sum(dla * ((-LRU_C) * r), axis=0, keepdims=True)
            dpr = dr * r * (1.0 - r)
            dpi = di * i * (1.0 - i)
            dba_ref[...] += jnp.sum(dpr, axis=0, keepdims=True)
            dbx_ref[...] += jnp.sum(dpi, axis=0, keepdims=True)
            dpr_b = dpr.astype(BF)
            dpi_b = dpi.astype(BF)
            dxr = (dxr + lax.dot_general(dpr_b, wa_ref[0], NT_DIMS, preferred_element_type=F32)
                   + lax.dot_general(dpi_b, wx_ref[0], NT_DIMS, preferred_element_type=F32))
            dwa_ref[0] += lax.dot_general(xr_b, dpr_b, TN_DIMS, preferred_element_type=F32)
            dwx_ref[0] += lax.dot_general(xr_b, dpi_b, TN_DIMS, preferred_element_type=F32)
            dpad[pl.ds(r0 + HALO, Q), :] = dxr
            dcw, dcb = _conv_bwd_w(dxr, back)
            dcw_ref[...] += dcw
            dcb_ref[...] += dcb

        _chunks(first, unrolled=True)
        dlam_ref[...] = -dlam_ref[...] * _sigmoid_gate(-lam)

        def second(r0):
            dx_ref[pl.ds(r0, Q), :] = _conv_bwd_x(_ahead(dpad, r0), cw_ref[...]).astype(BF)

        _chunks(second)

    c0 = PXL // 128
    vec = _spec((1, 128), lambda c: (0, c))
    mat = _spec((1, 128, 128), lambda c: (c, 0, 0))
    col = _spec((T, 128), lambda c: (0, c))
    vshape = jax.ShapeDtypeStruct((1, LRU_W), F32)
    mshape = jax.ShapeDtypeStruct((8, 128, 128), F32)
    pad = pltpu.VMEM((T + 2 * HALO, 128), F32)
    seq = pltpu.VMEM((T, 128), F32)
    return pl.pallas_call(
        body, grid=(8,),
        in_specs=[col, col, col, _spec((T, 128), lambda c: (0, c0 + c)), _spec((4, 128), lambda c: (0, c)), vec, mat, vec, mat, vec, vec],
        out_specs=[col, _spec((4, 128), lambda c: (0, c)), vec, mat, vec, mat, vec, vec],
        out_shape=[jax.ShapeDtypeStruct((T, LRU_W), BF), jax.ShapeDtypeStruct((4, LRU_W), F32), vshape, mshape, vshape, mshape, vshape, vshape],
        scratch_shapes=[pad, pad, pad, seq, seq, seq, pltpu.VMEM((SEGS, 128), F32)],
        compiler_params=_params(), name="lru_bwd")(dh_out, a, hseq, proj, cw, cb, wa2, ba, wx2, bx, lam)


FF_TILE = 256
FF_TILE_ROWS = list(range(0, D_FF, FF_TILE))


def gate_up(h1, wn, w_gate, w_up):
    def body(h_ref, wn_ref, wg_hbm, wu_hbm, gt_ref, up_ref, act_ref, u_ref, wg_ref, wu_ref, wg_sems, wu_sems):
        gate_ready = _arriving(wg_hbm, wg_ref, wg_sems, FF_TILE_ROWS, FF_TILE)
        up_ready = _arriving(wu_hbm, wu_ref, wu_sems, FF_TILE_ROWS, FF_TILE)
        for r in (0, HALF):
            u_ref[r:r + HALF, :] = _rms(h_ref[r:r + HALF, :], wn_ref[...]).astype(BF)

        def tile(c0):
            cols = pl.ds(c0, FF_TILE)
            gate_ready(c0 // FF_TILE)
            up_ready(c0 // FF_TILE)
            gt = lax.dot_general(u_ref[...], wg_ref[cols, :], NT_DIMS, preferred_element_type=F32)
            up = lax.dot_general(u_ref[...], wu_ref[cols, :], NT_DIMS, preferred_element_type=F32)
            gt_ref[:, cols] = gt.astype(BF)
            up_ref[:, cols] = up.astype(BF)
            act_ref[:, cols] = (gt * _sigmoid(gt) * up).astype(BF)

        _col_tiles(D_FF, FF_TILE, tile)

    big = jax.ShapeDtypeStruct((T, D_FF), BF)
    return pl.pallas_call(
        body, grid=(T // RC,), in_specs=[_rows_spec(D), _vec(D), ANY_SPEC, ANY_SPEC],
        out_specs=[_rows_spec(D_FF), _rows_spec(D_FF), _rows_spec(D_FF), _rows_spec(D)],
        out_shape=[big, big, big, jax.ShapeDtypeStruct((T, D), BF)],
        scratch_shapes=[pltpu.VMEM((D_FF, D), BF)] * 2 + [pltpu.SemaphoreType.DMA((len(FF_TILE_ROWS),))] * 2,
        compiler_params=_params(), name="gate_up")(h1, wn, w_gate, w_up)


def down_loss(act, w_down, h1, target, wf):
    first = NPAD + N_META

    def body(a_ref, w_ref, r_ref, t_hbm, wf_ref, d_ref, db_ref, l_ref, dw_ref, h_scr, t_ref, t_sem):
        i = pl.program_id(0)
        _zero_at_first(l_ref, dw_ref)
        head = pltpu.make_async_copy(t_hbm.at[pl.ds(0, RC - first)], t_ref.at[pl.ds(first, RC - first)], t_sem)
        rest = pltpu.make_async_copy(t_hbm.at[pl.ds(pl.multiple_of(jnp.maximum(i * RC - first, 0), 32), RC)], t_ref, t_sem)

        @pl.when(i == 0)
        def _():
            t_ref[0:first, :] = jnp.zeros((first, D), F32)
            head.start()

        @pl.when(i > 0)
        def _():
            rest.start()

        def tile(c0):
            cols = pl.ds(c0, 512)
            h_scr[:, cols] = r_ref[:, cols] + jnp.dot(a_ref[...], w_ref[:, cols], preferred_element_type=F32)

        _col_tiles(D, 512, tile)

        @pl.when(i == 0)
        def _():
            head.wait()

        @pl.when(i > 0)
        def _():
            rest.wait()

        for r in (0, HALF):
            h = h_scr[r:r + HALF, :]
            live = _rows((HALF, D), i * RC + r) >= first
            err = jnp.where(live, _rms(h, wf_ref[...]) - t_ref[r:r + HALF, :], 0.0)
            l_ref[...] += 0.5 * jnp.sum(jnp.sum(err * err, axis=1, keepdims=True) * (1.0 / D), axis=0, keepdims=True)
            dh, dw = _rms_bwd(err * (1.0 / D), h, wf_ref[...])
            dw_ref[...] += jnp.sum(dw, axis=0, keepdims=True)
            d_ref[r:r + HALF, :] = dh
            db_ref[r:r + HALF, :] = dh.astype(BF)

    return pl.pallas_call(
        body, grid=(T // RC,),
        in_specs=[_rows_spec(D_FF), _whole((D_FF, D)), _rows_spec(D), pl.BlockSpec(memory_space=pl.ANY), _vec(D)],
        out_specs=[_rows_spec(D), _rows_spec(D), _spec((1, 128), lambda i: (0, 0)), _vec(D)],
        out_shape=[jax.ShapeDtypeStruct((T, D), F32), jax.ShapeDtypeStruct((T, D), BF), jax.ShapeDtypeStruct((1, 128), F32),
                   jax.ShapeDtypeStruct((1, D), F32)],
        scratch_shapes=[pltpu.VMEM((RC, D), F32), pltpu.VMEM((RC, D), F32), pltpu.SemaphoreType.DMA],
        compiler_params=_params(), name="down_loss")(act, w_down, h1, target, wf)


def swiglu_bwd(dh2_b, w_down, gt, up, act, u2):
    tn = 256

    def body(d_hbm, u_hbm, w_ref, gt_ref, up_ref, act_ref, dg_ref, du_ref, gd_ref, gg_ref, gu_ref, d_ref, u_ref, d_sems, u_sems):
        chunks = list(range(0, T, RC))
        d_ready = _arriving(d_hbm, d_ref, d_sems, chunks, RC)
        u_ready = _arriving(u_hbm, u_ref, u_sems, chunks, RC)

        def rows(r0):
            part = pl.ds(r0, RC)
            d_ready(r0 // RC)
            dact = lax.dot_general(d_ref[part, :], w_ref[...], NT_DIMS, preferred_element_type=F32)
            gt_ = gt_ref[part, :].astype(F32)
            up_ = up_ref[part, :].astype(F32)
            sg = _sigmoid(gt_)
            dg_ref[part, :] = (dact * up_ * (sg * (1.0 + gt_ * (1.0 - sg)))).astype(BF)
            du_ref[part, :] = (dact * (gt_ * sg)).astype(BF)

        _col_tiles(T, RC, rows)
        for k in range(len(chunks)):
            u_ready(k)
        gd_ref[...] = lax.dot_general(act_ref[...], d_ref[...], TN_DIMS, preferred_element_type=F32).astype(BF)
        gg_ref[...] = lax.dot_general(dg_ref[...], u_ref[...], TN_DIMS, preferred_element_type=F32).astype(BF)
        gu_ref[...] = lax.dot_general(du_ref[...], u_ref[...], TN_DIMS, preferred_element_type=F32).astype(BF)

    cols = _spec((T, tn), lambda j: (0, j))
    wrow = _spec((tn, D), lambda j: (j, 0))
    big = jax.ShapeDtypeStruct((T, D_FF), BF)
    grad = jax.ShapeDtypeStruct((D_FF, D), BF)
    return pl.pallas_call(
        body, grid=(D_FF // tn,), in_specs=[ANY_SPEC, ANY_SPEC, wrow, cols, cols, cols],
        out_specs=[cols, cols, wrow, wrow, wrow], out_shape=[big, big, grad, grad, grad],
        scratch_shapes=[pltpu.VMEM((T, D), BF)] * 2 + [pltpu.SemaphoreType.DMA((T // RC,))] * 2,
        compiler_params=_params(), name="swiglu_bwd")(dh2_b, u2, w_down, gt, up, act)


def gate_up_bwd(dgt, dup, w_gate, w_up, h1, wn, dh2, after):
    def body(dg_ref, du_ref, wg_hbm, wu_hbm, h_ref, wn_ref, r_ref, _after, d_ref, db_ref, dw_ref, du_scr, wg_ref, wu_ref, wg_sems, wu_sems):
        gate_ready = _arriving(wg_hbm, wg_ref, wg_sems, FF_TILE_ROWS, FF_TILE)
        up_ready = _arriving(wu_hbm, wu_ref, wu_sems, FF_TILE_ROWS, FF_TILE)
        _zero_at_first(dw_ref)

        du_scr[...] = jnp.zeros_like(du_scr)

        def tile(c0):
            k = pl.ds(c0, FF_TILE)
            gate_ready(c0 // FF_TILE)
            up_ready(c0 // FF_TILE)
            du_scr[...] += (jnp.dot(dg_ref[:, k], wg_ref[k, :], preferred_element_type=F32)
                            + jnp.dot(du_ref[:, k], wu_ref[k, :], preferred_element_type=F32))

        _col_tiles(D_FF, FF_TILE, tile)
        for r in (0, HALF):
            dh, dw = _rms_bwd(du_scr[r:r + HALF, :], h_ref[r:r + HALF, :], wn_ref[...])
            dw_ref[...] += jnp.sum(dw, axis=0, keepdims=True)
            dh = dh + r_ref[r:r + HALF, :]
            d_ref[r:r + HALF, :] = dh
            db_ref[r:r + HALF, :] = dh.astype(BF)

    return pl.pallas_call(
        body, grid=(T // RC,),
        in_specs=[_rows_spec(D_FF), _rows_spec(D_FF), ANY_SPEC, ANY_SPEC, _rows_spec(D), _vec(D), _rows_spec(D), ANY_SPEC],
        out_specs=[_rows_spec(D), _rows_spec(D), _vec(D)],
        out_shape=[jax.ShapeDtypeStruct((T, D), F32), jax.ShapeDtypeStruct((T, D), BF), jax.ShapeDtypeStruct((1, D), F32)],
        scratch_shapes=[pltpu.VMEM((RC, D), F32)] + [pltpu.VMEM((D_FF, D), BF)] * 2 + [pltpu.SemaphoreType.DMA((len(FF_TILE_ROWS),))] * 2,
        compiler_params=_params(), name="gate_up_bwd")(dgt, dup, w_gate, w_up, h1, wn, dh2, after)


def _adamw(w, g, m, v):
    m = ADAM_B1 * m + (1.0 - ADAM_B1) * g
    v = ADAM_B2 * v + (1.0 - ADAM_B2) * (g * g)
    m_hat = m / (1.0 - ADAM_B1 ** ADAM_STEP)
    v_hat = v / (1.0 - ADAM_B2 ** ADAM_STEP)
    delta = -ADAM_LR * (m_hat / (jnp.sqrt(v_hat) + ADAM_EPS) + ADAM_WD * w)
    return delta, m, v


def adamw_shards(name, recvs, ws, ms, vs):
    n = len(ws)

    def body(*refs):
        ins, outs = refs[:4 * n], refs[4 * n:]
        for k in range(n):
            p_ref, w_ref, m_ref, v_ref = ins[k], ins[n + k], ins[2 * n + k], ins[3 * n + k]
            g = p_ref[0].astype(F32)
            for s in range(1, 8):
                g = g + p_ref[s].astype(F32)
            outs[4 * k][...] = g
            outs[4 * k + 1][...], outs[4 * k + 2][...], outs[4 * k + 3][...] = _adamw(w_ref[...], g, m_ref[...], v_ref[...])

    tiles = [_spec((w.shape[0] // 2, w.shape[1]), lambda i: (i, 0)) for w in ws]
    recv_tiles = [_spec((8, w.shape[0] // 2, w.shape[1]), lambda i: (0, i, 0)) for w in ws]
    res = pl.pallas_call(
        body, grid=(2,), in_specs=recv_tiles + tiles * 3,
        out_specs=[t for t in tiles for _ in range(4)],
        out_shape=[jax.ShapeDtypeStruct(w.shape, F32) for w in ws for _ in range(4)],
        compiler_params=_params(), name=name)(*recvs, *ws, *ms, *vs)
    return [list(res[4 * k:4 * k + 4]) for k in range(n)]


def adamw_w_in(recv, w, m, v):
    rows = 34
    per_row = D // 128

    def body(p_ref, w_ref, m_ref, v_ref, g_ref, d_ref, mo_ref, vo_ref):
        def chunk(c, carry):
            lines = pl.ds(pl.multiple_of(c * per_row * rows, 16), per_row * rows)
            g = p_ref[0, lines, :].astype(F32)
            for s in range(1, 8):
                g = g + p_ref[s, lines, :].astype(F32)
            g = g.reshape(rows, per_row, 128)
            part = pl.ds(c * rows, rows)
            g_ref[part] = g
            d_ref[part], mo_ref[part], vo_ref[part] = _adamw(w_ref[part], g, m_ref[part], v_ref[part])
            return carry

        lax.fori_loop(0, w.shape[0] // rows, chunk, 0)

    shape = jax.ShapeDtypeStruct(w.shape, F32)
    return pl.pallas_call(body, out_shape=[shape] * 4, compiler_params=_params(0), name="adamw_w_in")(recv, w, m, v)


def sum_slabs(recv):
    def body(p_ref, o_ref):
        g = p_ref[0]
        for s in range(1, 8):
            g = g + p_ref[s]
        o_ref[...] = g

    return pl.pallas_call(body, out_shape=jax.ShapeDtypeStruct(recv.shape[1:], F32), compiler_params=_params(0), name="sum_slabs")(recv)


SIMPLE = [("norm1_w", 1024), ("ssd_conv_b", 1536), ("ssd_dt_bias", 16), ("ssd_a_log", 16), ("ssd_d", 16), ("ssd_norm_w", 1024),
          ("lru_conv_b", 1024), ("lru_ba", 1024), ("lru_bx", 1024), ("lru_lambda", 1024), ("lru_norm_w", 1024), ("norm2_w", 1024),
          ("final_norm_w", 1024)]
SPECIAL = ["lru_wa", "lru_wx", "meta_tokens", "ssd_conv_w", "lru_conv_w"]
SM_ROWS = 176
SM_WA, SM_WX, SM_META, SM_SCW, SM_LCW, SM_LOSS = 14, 78, 142, 158, 166, 170


def _simple_rows():
    rows, r = {}, 0
    for name, n in SIMPLE:
        rows[name] = r
        r += -(-n // 1024)
    return rows


def adamw_small(sm, special_g, ws, ms, vs):
    rows = _simple_rows()
    ns, nx = len(SIMPLE), len(SPECIAL)

    def body(*refs):
        sm_ref = refs[0]
        gx = refs[1:1 + nx]
        wr = refs[1 + nx:1 + nx + ns + nx]
        mr = refs[1 + nx + ns + nx:1 + nx + 2 * (ns + nx)]
        vr = refs[1 + nx + 2 * (ns + nx):1 + nx + 3 * (ns + nx)]
        outs = refs[1 + nx + 3 * (ns + nx):]
        o = 0
        for k, (name, n) in enumerate(SIMPLE):
            r0 = rows[name]
            for c0 in range(0, n, 1024):
                wd = min(1024, n - c0)
                g = sm_ref[r0 + c0 // 1024:r0 + c0 // 1024 + 1, 0:wd]
                sl = (slice(None), slice(c0, c0 + wd))
                d, m2, v2 = _adamw(wr[k][sl], g, mr[k][sl], vr[k][sl])
                outs[o][sl] = g
                outs[o + 1][sl] = d
                outs[o + 2][sl] = m2
                outs[o + 3][sl] = v2
            o += 4
        for k in range(nx):
            d, m2, v2 = _adamw(wr[ns + k][...], gx[k][...], mr[ns + k][...], vr[ns + k][...])
            outs[o][...] = d
            outs[o + 1][...] = m2
            outs[o + 2][...] = v2
            o += 3

    out_shape = []
    for k in range(ns):
        out_shape += [jax.ShapeDtypeStruct(ws[k].shape, F32)] * 4
    for k in range(nx):
        out_shape += [jax.ShapeDtypeStruct(ws[ns + k].shape, F32)] * 3
    return pl.pallas_call(body, out_shape=out_shape, compiler_params=_params(0), name="adamw_small")(sm, *special_g, *ws, *ms, *vs)


def _place():
    return lax.axis_index("x"), lax.axis_index("y"), lax.axis_index("c")


def _index(px, py, pc):
    return 4 * px + 2 * py + pc


def all_gather(name, shards):
    n = len(shards)
    hbm = pl.BlockSpec(memory_space=pl.ANY)

    def body(*refs):
        ins, outs = refs[:n], refs[n:2 * n]
        send_sems, recv_sems, local_sems = refs[2 * n:]
        x, y, c = _place()
        me, sibling = (x, y, c), (x, y, 1 - c)
        chips = [(1 - x, y), (x, 1 - y), (1 - x, 1 - y)]

        def copy(i, k, block, to, src=None):
            dst = outs[i].at[_index(*block)]
            return pltpu.make_async_remote_copy(src_ref=dst if src is None else src, dst_ref=dst, send_sem=send_sems.at[7 * i + k],
                                                recv_sem=recv_sems.at[7 * i + k], device_id=to, device_id_type=MESH)

        mine = [pltpu.make_async_copy(ins[i], outs[i].at[_index(*me)], local_sems.at[i]) for i in range(n)]
        for cp in mine:
            cp.start()
        first = []
        for i in range(n):
            first += [copy(i, 1 + j, me, (*chip, c), src=ins[i]) for j, chip in enumerate(chips)]
            first.append(copy(i, 0, me, sibling, src=ins[i]))
        for cp in first:
            cp.start()
        passed = []
        for i in range(n):
            for j, chip in enumerate(chips):
                copy(i, 1 + j, (*chip, c), me).wait_recv()
                cp = copy(i, 4 + j, (*chip, c), sibling)
                cp.start()
                passed.append(cp)
        for i in range(n):
            copy(i, 0, sibling, me).wait_recv()
            for j, chip in enumerate(chips):
                copy(i, 4 + j, (*chip, 1 - c), me).wait_recv()
        for cp in first + passed:
            cp.wait_send()
        for cp in mine:
            cp.wait()

    return pl.pallas_call(
        body, in_specs=[hbm] * n, out_specs=[hbm] * n,
        out_shape=[jax.ShapeDtypeStruct((8,) + s.shape, s.dtype) for s in shards],
        scratch_shapes=[pltpu.SemaphoreType.DMA((7 * n,)), pltpu.SemaphoreType.DMA((7 * n,)), pltpu.SemaphoreType.DMA((n,))],
        name=name)(*shards)


HBM_SPEC = pl.BlockSpec(memory_space=pltpu.HBM)
SEM_SPEC = pl.BlockSpec(memory_space=pltpu.SEMAPHORE)
EFFECT = pltpu.SideEffectType.DATAFLOW_SIDE_EFFECTING


def _peers(x, y, c):
    return [((1 - x) if k & 4 else x, (1 - y) if k & 2 else y, (1 - c) if k & 1 else c) for k in range(1, 8)]


def _pieces(rows):
    for n in (4, 2):
        if rows % (16 * n) == 0:
            return [(r * (rows // n), rows // n) for r in range(n)]
    return [(0, rows)]


def _peer_copies(src, land, send_sems, recv_sems, k, peer, mine, slab_src):
    block = src.at[_index(*peer)] if slab_src else src
    return [pltpu.make_async_remote_copy(src_ref=block.at[pl.ds(r0, nr)], dst_ref=land.at[mine, pl.ds(r0, nr)], send_sem=send_sems.at[k],
                                         recv_sem=recv_sems.at[k], device_id=peer, device_id_type=MESH)
            for r0, nr in _pieces(block.shape[0])]


def copies_start(name, srcs, slab_src, after):
    n = len(srcs)
    zones = [jax.ShapeDtypeStruct(s.shape if slab_src else (8,) + s.shape, s.dtype) for s in srcs]
    afters = [] if after is None else [after]

    def body(*refs):
        ins, lands = refs[:n], refs[n:2 * n]
        first = 2 * n + len(afters)
        sends, recvs = refs[first:first + n], refs[first + n:first + 2 * n]
        token = refs[-1]
        x, y, c = _place()
        mine = _index(x, y, c)
        for i in range(n):
            per_peer = [_peer_copies(ins[i], lands[i], sends[i], recvs[i], k, peer, mine, slab_src) for k, peer in enumerate(_peers(x, y, c))]
            for piece in zip(*per_peer):
                for cp in piece:
                    cp.start()
        token[...] = jnp.zeros_like(token)

    sem = pltpu.SemaphoreType.DMA((7,))
    res = pl.pallas_call(
        body, name=name,
        out_shape=([sem] * (2 * n) + [pltpu.HBM(s.shape, s.dtype) for s in srcs] + [pltpu.HBM(z.shape, z.dtype) for z in zones]
                   + [jax.ShapeDtypeStruct((8, 128), F32)]),
        in_specs=[HBM_SPEC] * (2 * n) + [pl.BlockSpec(memory_space=pl.ANY)] * len(afters),
        out_specs=[SEM_SPEC] * (2 * n) + [HBM_SPEC] * (2 * n) + [pl.BlockSpec(memory_space=pltpu.VMEM)],
        input_output_aliases={i: 2 * n + i for i in range(2 * n)},
        compiler_params=pltpu.CompilerParams(has_side_effects=EFFECT),
    )(*[pltpu.with_memory_space_constraint(s, pltpu.HBM) for s in srcs],
      *[pltpu.with_memory_space_constraint(lax.empty(z.shape, z.dtype), pltpu.HBM) for z in zones], *afters)
    return [(res[i], res[n + i], res[2 * n + i], res[3 * n + i]) for i in range(n)], res[-1]


def copies_wait(name, started, slab_src, after):
    n = len(started)

    def body(*refs):
        ins, lands = refs[:n], refs[n:2 * n]
        sends, recvs = refs[2 * n:3 * n], refs[3 * n:4 * n]
        x, y, c = _place()
        mine = _index(x, y, c)
        for i in range(n):
            for k, peer in enumerate(_peers(x, y, c)):
                arrival = pltpu.make_async_remote_copy(src_ref=ins[i].at[mine] if slab_src else ins[i], dst_ref=lands[i].at[_index(*peer)],
                                                       send_sem=sends[i].at[k], recv_sem=recvs[i].at[k], device_id=peer, device_id_type=MESH)
                arrival.wait_send()
                arrival.wait_recv()

    srcs = [s[2] for s in started]
    lands = [s[3] for s in started]
    afters = list(after) if isinstance(after, (list, tuple)) else [after]
    res = pl.pallas_call(
        body, name=name,
        out_shape=[pltpu.HBM(s.shape, s.dtype) for s in srcs] + [pltpu.HBM(z.shape, z.dtype) for z in lands],
        in_specs=[HBM_SPEC] * (2 * n) + [SEM_SPEC] * (2 * n) + [pl.BlockSpec(memory_space=pl.ANY)] * len(afters),
        out_specs=[HBM_SPEC] * (2 * n),
        input_output_aliases={i: i for i in range(2 * n)},
        compiler_params=pltpu.CompilerParams(has_side_effects=EFFECT),
    )(*srcs, *lands, *[s[0] for s in started], *[s[1] for s in started], *afters)
    me = _index(*_place())
    own = [lax.dynamic_index_in_dim(s, me, 0, keepdims=True) if slab_src else s[None] for s in res[:n]]
    return [lax.dynamic_update_slice_in_dim(z, o, me, 0) for z, o in zip(res[n:], own)]


def _hop(src, land, send_sems, recv_sems, k, block, to):
    dst = land.at[_index(*block)]
    return pltpu.make_async_remote_copy(src_ref=dst if src is None else src, dst_ref=dst, send_sem=send_sems.at[k], recv_sem=recv_sems.at[k],
                                        device_id=to, device_id_type=MESH)


def _other_chips(x, y):
    return [(1 - x, y), (x, 1 - y), (1 - x, 1 - y)]


def gather_start(name, shards, after):
    n = len(shards)

    def body(*refs):
        ins, lands = refs[:n], refs[n:2 * n]
        sends, recvs = refs[2 * n + 1:3 * n + 1], refs[3 * n + 1:4 * n + 1]
        token = refs[-1]
        x, y, c = _place()
        for i in range(n):
            for j, chip in enumerate(_other_chips(x, y)):
                _hop(ins[i], lands[i], sends[i], recvs[i], 1 + j, (x, y, c), (*chip, c)).start()
            _hop(ins[i], lands[i], sends[i], recvs[i], 0, (x, y, c), (x, y, 1 - c)).start()
        token[...] = jnp.zeros_like(token)

    own, passing = pltpu.SemaphoreType.DMA((4,)), pltpu.SemaphoreType.DMA((3,))
    zones = [jax.ShapeDtypeStruct((8,) + s.shape, s.dtype) for s in shards]
    res = pl.pallas_call(
        body, name=name,
        out_shape=([own] * (2 * n) + [passing] * (2 * n) + [pltpu.HBM(s.shape, s.dtype) for s in shards]
                   + [pltpu.HBM(z.shape, z.dtype) for z in zones] + [jax.ShapeDtypeStruct((8, 128), F32)]),
        in_specs=[HBM_SPEC] * (2 * n) + [ANY_SPEC],
        out_specs=[SEM_SPEC] * (4 * n) + [HBM_SPEC] * (2 * n) + [pl.BlockSpec(memory_space=pltpu.VMEM)],
        input_output_aliases={i: 4 * n + i for i in range(2 * n)},
        compiler_params=pltpu.CompilerParams(has_side_effects=EFFECT),
    )(*[pltpu.with_memory_space_constraint(s, pltpu.HBM) for s in shards],
      *[pltpu.with_memory_space_constraint(lax.empty(z.shape, z.dtype), pltpu.HBM) for z in zones], after)
    return [[res[4 * n + i], res[5 * n + i], res[i], res[n + i], res[2 * n + i], res[3 * n + i]] for i in range(n)], res[-1]


def gather_stage(name, pass_on, finish, after):
    arrays = pass_on + finish
    n = len(arrays)

    def body(*refs):
        ins, lands = refs[:n], refs[n:2 * n]
        sems = [refs[(2 + q) * n:(3 + q) * n] for q in range(4)]
        x, y, c = _place()
        me, sibling = (x, y, c), (x, y, 1 - c)
        for i in range(len(pass_on)):
            send, recv, send_on, recv_on = (q[i] for q in sems)
            for j, chip in enumerate(_other_chips(x, y)):
                _hop(None, lands[i], send, recv, 1 + j, (*chip, c), me).wait_recv()
                _hop(None, lands[i], send_on, recv_on, j, (*chip, c), sibling).start()
        for i in range(len(pass_on), n):
            send, recv, send_on, recv_on = (q[i] for q in sems)
            _hop(ins[i], lands[i], send, recv, 0, sibling, me).wait_recv()
            for j, chip in enumerate(_other_chips(x, y)):
                _hop(None, lands[i], send_on, recv_on, j, (*chip, 1 - c), me).wait_recv()
            _hop(ins[i], lands[i], send, recv, 0, me, sibling).wait_send()
            for j, chip in enumerate(_other_chips(x, y)):
                _hop(ins[i], lands[i], send, recv, 1 + j, me, (*chip, c)).wait_send()
                _hop(None, lands[i], send_on, recv_on, j, (*chip, c), sibling).wait_send()
        refs[-1][...] = jnp.zeros_like(refs[-1])

    res = pl.pallas_call(
        body, name=name,
        out_shape=([pltpu.HBM(a[0].shape, a[0].dtype) for a in arrays] + [pltpu.HBM(a[1].shape, a[1].dtype) for a in arrays]
                   + [jax.ShapeDtypeStruct((8, 128), F32)]),
        in_specs=[HBM_SPEC] * (2 * n) + [SEM_SPEC] * (4 * n) + [ANY_SPEC],
        out_specs=[HBM_SPEC] * (2 * n) + [pl.BlockSpec(memory_space=pltpu.VMEM)],
        input_output_aliases={i: i for i in range(2 * n)},
        compiler_params=pltpu.CompilerParams(has_side_effects=EFFECT),
    )(*[a[0] for a in arrays], *[a[1] for a in arrays], *[a[2 + q] for q in range(4) for a in arrays], after)
    for i, a in enumerate(arrays):
        a[0], a[1] = res[i], res[n + i]
    me = _index(*_place())
    return [lax.dynamic_update_slice_in_dim(a[1], a[0][None], me, 0) for a in finish], res[-1]


WEIGHTS = ["meta_tokens", "norm1_w", "w_in", "ssd_conv_w", "ssd_conv_b", "ssd_dt_bias", "ssd_a_log", "ssd_d", "ssd_norm_w", "lru_conv_w",
           "lru_conv_b", "lru_wa", "lru_ba", "lru_wx", "lru_bx", "lru_lambda", "lru_norm_w", "w_out", "norm2_w", "w_gate", "w_up", "w_down",
           "final_norm_w"]
BIG = ["w_in", "w_out", "w_gate", "w_up", "w_down"]
COLUMN_SHARDED = ["w_in", "w_gate", "w_up"]


def _pair_blocks(w):
    w = w.reshape(8, 2, 64, 64)
    z = jnp.zeros((8, 64, 64), w.dtype)
    return jnp.concatenate([jnp.concatenate([w[:, 0], z], axis=2), jnp.concatenate([z, w[:, 1]], axis=2)], axis=1)


def _unpair_blocks(w2):
    return jnp.stack([w2[:, :64, :64], w2[:, 64:, 64:]], axis=1).reshape(16, 64, 64)


def _per_group(v):
    return jnp.pad(v.reshape(2, 1, 8), ((0, 0), (0, 0), (0, 120)))


def _pad_cols(v, n):
    return jnp.pad(v, ((0, 0), (0, n - v.shape[1])))


def local_step(x, target, meta, ssd_cw, lru_cw, w_in, fetch, send, p, behind):
    z120 = jnp.zeros((120, D), BF)
    w_dt = jnp.concatenate([w_in[2560:2568], z120, w_in[2568:2576], z120], axis=0)
    bias2, alog2, d2 = _per_group(p["ssd_dt_bias"]), _per_group(p["ssd_a_log"]), _per_group(p["ssd_d"])
    wa2 = _pair_blocks(p["lru_wa"]).astype(BF)
    wx2 = _pair_blocks(p["lru_wx"]).astype(BF)
    lru = (lru_cw, p["lru_conv_b"], wa2, p["lru_ba"], wx2, p["lru_bx"], p["lru_lambda"])

    h0 = jnp.concatenate([jnp.zeros((NPAD, D), F32), meta, x], axis=0)
    proj, dt_raw, u1 = in_proj(h0, p["norm1_w"], w_in, w_dt, behind)
    yn_ssd, y_pre, h_prev = ssd_fwd(proj, dt_raw, ssd_cw, p["ssd_conv_b"], bias2, alog2, d2, p["ssd_norm_w"])
    _, moved = fetch([], yn_ssd)
    hseq, a = lru_fwd(proj, *lru, moved)
    (w_out,), _ = fetch(["w_out"], hseq)
    h1, cat = out_proj(yn_ssd, proj, hseq, p["lru_norm_w"], w_out, h0)
    (w_gate, w_up), _ = fetch(["w_gate", "w_up"], h1)
    gt, up, act, u2 = gate_up(h1, p["norm2_w"], w_gate, w_up)
    (w_down,), _ = fetch(["w_down"], act)
    dh2, dh2_b, loss, d_fnw = down_loss(act, w_down, h1, target, p["final_norm_w"])

    dgt, dup, g_down, g_gate, g_up = swiglu_bwd(dh2_b, w_down, gt, up, act, u2)
    sent = send({"w_down": g_down, "w_gate": g_gate, "w_up": g_up})
    dh1, dh1_b, d_n2 = gate_up_bwd(dgt, dup, w_gate, w_up, h1, p["norm2_w"], dh2, sent)
    sent = send({"w_out": weight_grad("dw_out", cat, dh1_b)})
    dyn, dh_out, dg_b, d_lnw = out_proj_bwd(dh1_b, w_out, proj, hseq, p["lru_norm_w"], sent)

    dxl_b, d_lcw, d_lcb, dwa2, d_ba, dwx2, d_bx, d_lam = lru_bwd(dh_out, a, hseq, proj, *lru)
    dz_b, dxbc_b, ddt_b, dpar, d_snw, d_scw, d_scb = ssd_bwd(dyn, proj, dt_raw, ssd_cw, p["ssd_conv_b"], y_pre, h_prev, bias2, alog2, d2,
                                                             p["ssd_norm_w"], sent)
    sent = send({"w_in": in_weight_grad(dz_b, dg_b, dxl_b, dxbc_b, ddt_b, u1)})
    grad_x, d_meta, d_n1 = in_proj_bwd(dz_b, dg_b, dxl_b, dxbc_b, ddt_b, w_in, w_dt, h0, p["norm1_w"], dh1, sent)
    small = {"norm1_w": d_n1, "ssd_conv_b": d_scb, "ssd_dt_bias": dpar[:, 0, :8].reshape(1, 16), "ssd_a_log": dpar[:, 1, :8].reshape(1, 16),
             "ssd_d": dpar[:, 2, :8].reshape(1, 16), "ssd_norm_w": d_snw, "lru_conv_b": d_lcb, "lru_ba": d_ba, "lru_bx": d_bx,
             "lru_lambda": d_lam, "lru_norm_w": d_lnw, "norm2_w": d_n2, "final_norm_w": d_fnw,
             "lru_wa": _unpair_blocks(dwa2), "lru_wx": _unpair_blocks(dwx2), "meta_tokens": d_meta,
             "ssd_conv_w": d_scw, "lru_conv_w": d_lcw}
    return loss, grad_x, small


def _pack_small(small, loss):
    rows = [_pad_cols(small[name], -(-n // 1024) * 1024).reshape(-1, 1024) for name, n in SIMPLE]
    rows += [small["lru_wa"].reshape(64, 1024), small["lru_wx"].reshape(64, 1024), small["meta_tokens"],
             _pad_cols(small["ssd_conv_w"], 2048).reshape(8, 1024), small["lru_conv_w"], _pad_cols(loss[:, 0:1], 1024)]
    sm = jnp.concatenate(rows, axis=0)
    return jnp.pad(sm, ((0, SM_ROWS - sm.shape[0]), (0, 0)))


def _slabs(g):
    return g.reshape(8, g.shape[0] // 8, g.shape[1])


def _unslab(g):
    return g.reshape(8 * g.shape[1], g.shape[2])


def kernel(x, meta_tokens, norm1_w, w_in, ssd_conv_w, ssd_conv_b, ssd_dt_bias, ssd_a_log, ssd_d, ssd_norm_w, lru_conv_w, lru_conv_b, lru_wa, lru_ba, lru_wx, lru_bx, lru_lambda, lru_norm_w, w_out, norm2_w, w_gate, w_up, w_down, final_norm_w, loss_target, m_meta_tokens, m_norm1_w, m_w_in, m_ssd_conv_w, m_ssd_conv_b, m_ssd_dt_bias, m_ssd_a_log, m_ssd_d, m_ssd_norm_w, m_lru_conv_w, m_lru_conv_b, m_lru_wa, m_lru_ba, m_lru_wx, m_lru_bx, m_lru_lambda, m_lru_norm_w, m_w_out, m_norm2_w, m_w_gate, m_w_up, m_w_down, m_final_norm_w, v_meta_tokens, v_norm1_w, v_w_in, v_ssd_conv_w, v_ssd_conv_b, v_ssd_dt_bias, v_ssd_a_log, v_ssd_d, v_ssd_norm_w, v_lru_conv_w, v_lru_conv_b, v_lru_wa, v_lru_ba, v_lru_wx, v_lru_bx, v_lru_lambda, v_lru_norm_w, v_w_out, v_norm2_w, v_w_gate, v_w_up, v_w_down, v_final_norm_w):
    w = dict(meta_tokens=meta_tokens, norm1_w=norm1_w, w_in=w_in[0], ssd_conv_w=ssd_conv_w[0], ssd_conv_b=ssd_conv_b, ssd_dt_bias=ssd_dt_bias,
             ssd_a_log=ssd_a_log, ssd_d=ssd_d, ssd_norm_w=ssd_norm_w, lru_conv_w=lru_conv_w[0], lru_conv_b=lru_conv_b, lru_wa=lru_wa[0],
             lru_ba=lru_ba, lru_wx=lru_wx[0], lru_bx=lru_bx, lru_lambda=lru_lambda, lru_norm_w=lru_norm_w, w_out=w_out[0], norm2_w=norm2_w,
             w_gate=w_gate[0], w_up=w_up[0], w_down=w_down[0], final_norm_w=final_norm_w.reshape(1, D))
    m = dict(meta_tokens=m_meta_tokens, norm1_w=m_norm1_w, w_in=m_w_in[0], ssd_conv_w=m_ssd_conv_w[0], ssd_conv_b=m_ssd_conv_b,
             ssd_dt_bias=m_ssd_dt_bias, ssd_a_log=m_ssd_a_log, ssd_d=m_ssd_d, ssd_norm_w=m_ssd_norm_w, lru_conv_w=m_lru_conv_w[0],
             lru_conv_b=m_lru_conv_b, lru_wa=m_lru_wa[0], lru_ba=m_lru_ba, lru_wx=m_lru_wx[0], lru_bx=m_lru_bx, lru_lambda=m_lru_lambda,
             lru_norm_w=m_lru_norm_w, w_out=m_w_out[0], norm2_w=m_norm2_w, w_gate=m_w_gate[0], w_up=m_w_up[0], w_down=m_w_down[0],
             final_norm_w=m_final_norm_w.reshape(1, D))
    v = dict(meta_tokens=v_meta_tokens, norm1_w=v_norm1_w, w_in=v_w_in[0], ssd_conv_w=v_ssd_conv_w[0], ssd_conv_b=v_ssd_conv_b,
             ssd_dt_bias=v_ssd_dt_bias, ssd_a_log=v_ssd_a_log, ssd_d=v_ssd_d, ssd_norm_w=v_ssd_norm_w, lru_conv_w=v_lru_conv_w[0],
             lru_conv_b=v_lru_conv_b, lru_wa=v_lru_wa[0], lru_ba=v_lru_ba, lru_wx=v_lru_wx[0], lru_bx=v_lru_bx, lru_lambda=v_lru_lambda,
             lru_norm_w=v_lru_norm_w, w_out=v_w_out[0], norm2_w=v_norm2_w, w_gate=v_w_gate[0], w_up=v_w_up[0], w_down=v_w_down[0],
             final_norm_w=v_final_norm_w.reshape(1, D))
    shapes = dict(meta_tokens=meta_tokens.shape, norm1_w=norm1_w.shape, w_in=w_in.shape, ssd_conv_w=ssd_conv_w.shape,
                  ssd_conv_b=ssd_conv_b.shape, ssd_dt_bias=ssd_dt_bias.shape, ssd_a_log=ssd_a_log.shape, ssd_d=ssd_d.shape,
                  ssd_norm_w=ssd_norm_w.shape, lru_conv_w=lru_conv_w.shape, lru_conv_b=lru_conv_b.shape, lru_wa=lru_wa.shape,
                  lru_ba=lru_ba.shape, lru_wx=lru_wx.shape, lru_bx=lru_bx.shape, lru_lambda=lru_lambda.shape, lru_norm_w=lru_norm_w.shape,
                  w_out=w_out.shape, norm2_w=norm2_w.shape, w_gate=w_gate.shape, w_up=w_up.shape, w_down=w_down.shape,
                  final_norm_w=final_norm_w.shape)
    me = _index(*_place())
    for n in COLUMN_SHARDED:
        w[n], m[n], v[n] = w[n].T, m[n].T, v[n].T

    small_shard = jnp.concatenate([w["meta_tokens"], _pad_cols(w["ssd_conv_w"], 256).reshape(8, 128), w["lru_conv_w"],
                                   jnp.zeros((4, 128), F32)], axis=0)
    g_in, gs = all_gather("gather_w_in", [w["w_in"].astype(BF), small_shard])
    later = ["w_out", "w_gate", "w_up", "w_down"]
    started, behind = gather_start("gather_rest_start", [w[n].astype(BF) for n in later], gs)
    started = dict(zip(later, started))
    passed_on = {None: ["w_out", "w_gate", "w_up"], "w_out": ["w_down"], "w_gate": [], "w_down": []}
    meta_full = gs[:, 0:16].transpose(1, 0, 2).reshape(N_META, D)
    ssd_cw = gs[:, 16:24].reshape(8, 4, 256)[:, :, :192].transpose(1, 0, 2).reshape(4, XBC)
    lru_cw = gs[:, 24:28].transpose(1, 0, 2).reshape(4, LRU_W)

    def fetch(names, after):
        first = names[0] if names else None
        got, zero = gather_stage("gather_" + (first + "_wait" if names else "pass_on"), [started[n] for n in passed_on[first]],
                                 [started[n] for n in names], after)
        return [_unslab(g) for g in got], zero

    in_flight = {}

    def send(grads):
        names = list(grads)
        st, token = copies_start("grads_" + names[0] + "_start", [grads[n] if n == "small" else _slabs(grads[n]) for n in names], True, None)
        in_flight.update(zip(names, st))
        return token

    loss, grad_x, small = local_step(x[0], loss_target[0], meta_full, ssd_cw, lru_cw, _unslab(g_in), fetch, send, w, behind)
    send({"small": _pack_small(small, loss).reshape(8, SM_ROWS // 8, 1024)})

    out = {}
    early = ["w_down", "w_gate", "w_up", "w_out"]
    recv = dict(zip(early, copies_wait("grads_early_wait", [in_flight[n] for n in early], True, in_flight["small"][2])))
    for pair in (early[:2], early[2:]):
        done = adamw_shards("adamw_" + pair[0], [recv[n] for n in pair], [w[n] for n in pair], [m[n] for n in pair], [v[n] for n in pair])
        out.update(zip(pair, done))
    recv_in, recv_small = copies_wait("grads_late_wait", [in_flight["w_in"], in_flight["small"]], True, [out[n][0] for n in early])
    def lines(a):
        return jnp.transpose(a.reshape(D // 128, 128, IN_COLS // 8), (2, 0, 1))

    out["w_in"] = [jnp.transpose(o, (1, 2, 0)).reshape(D, IN_COLS // 8) for o in adamw_w_in(recv_in, lines(w_in), lines(m_w_in), lines(v_w_in))]
    for n in ("w_gate", "w_up"):
        out[n] = [o.T for o in out[n]]
    sm = all_gather("gather_small_grads", [sum_slabs(recv_small)])[0].reshape(SM_ROWS, 1024)
    special_g =[sm[SM_WA:SM_WA + 64].reshape(16, 64, 64), sm[SM_WX:SM_WX + 64].reshape(16, 64, 64),
                 lax.dynamic_slice(sm[SM_META:SM_META + 16], (0, 128 * me), (16, 128)),
                 lax.dynamic_slice(sm[SM_SCW:SM_SCW + 8].reshape(4, 2048), (0, 192 * me), (4, 192)),
                 lax.dynamic_slice(sm[SM_LCW:SM_LCW + 4], (0, 128 * me), (4, 128))]
    names = [n for n, _ in SIMPLE] + SPECIAL
    res = adamw_small(sm, special_g, [w[n] for n in names], [m[n] for n in names], [v[n] for n in names])
    for k, (n, _) in enumerate(SIMPLE):
        out[n] = res[4 * k:4 * k + 4]
    for k, n in enumerate(SPECIAL):
        o = 4 * len(SIMPLE) + 3 * k
        out[n] = [special_g[k]] + list(res[o:o + 3])
    loss_total = sm[SM_LOSS, 0]
    flat = [loss_total, grad_x[None]]
    for k in range(4):
        flat += [out[n][k].reshape(shapes[n]) for n in WEIGHTS]
    return tuple(flat)
```

```python
import math

import jax
import jax.numpy as jnp
from jax import lax
from jax.experimental import pallas as pl
from jax.experimental.pallas import tpu as pltpu

F32 = jnp.float32
BF = jnp.bfloat16

D = 1024
SEQ = 2048
N_META = 16
Q = 128
NPAD = 112
T = NPAD + N_META + SEQ
NCH = T // Q
RC = 544
D_FF = 2816
SSD_W = 1024
LRU_W = 1024
XBC = 1536
IN_COLS = 4624
PZ, PG, PXL, PXBC = 0, 1024, 2048, 3072
NP_IN = 4608
EPS = 1e-6
LRU_C = 8.0
VMEM_LIMIT = 56 * 1024 * 1024

ADAM_LR, ADAM_B1, ADAM_B2, ADAM_EPS, ADAM_WD, ADAM_STEP = 0.001, 0.9, 0.999, 1e-08, 0.01, 10

NT_DIMS = (((1,), (1,)), ((), ()))
TN_DIMS = (((0,), (0,)), ((), ()))
MESH = pl.DeviceIdType.MESH


def _params(n_grid=1, limit=VMEM_LIMIT):
    return pltpu.CompilerParams(dimension_semantics=("arbitrary",) * n_grid, vmem_limit_bytes=limit)


def _spec(shape, imap, single=False):
    if single:
        return pl.BlockSpec(shape, imap, pipeline_mode=pl.Buffered(1))
    return pl.BlockSpec(shape, imap)


def _sigmoid(x):
    return 0.5 * jnp.tanh(0.5 * x) + 0.5


def _sigmoid_gate(x):
    return 1.0 / (1.0 + jnp.exp(-x))


def _softplus(x):
    return jnp.maximum(x, 0.0) + jnp.log(1.0 + jnp.exp(-jnp.abs(x)))


def _rms_stats(h):
    return lax.rsqrt(jnp.mean(h * h, axis=-1, keepdims=True) + EPS)


def _rms(h, w):
    return (h * _rms_stats(h)) * w


def _rms_bwd(du, h, w):
    r = _rms_stats(h)
    n = h * r
    dn = du * w
    dh = r * (dn - n * jnp.mean(dn * n, axis=-1, keepdims=True))
    return dh, du * n


_G0 = math.sqrt(2.0 / math.pi)


def _gelu(x):
    return 0.5 * x * (1.0 + jnp.tanh(_G0 * (x + 0.044715 * (x * x * x))))


def _gelu_grad(x):
    t = jnp.tanh(_G0 * (x + 0.044715 * (x * x * x)))
    return 0.5 * (1.0 + t) + 0.5 * x * (1.0 - t * t) * (_G0 * (1.0 + 3.0 * 0.044715 * (x * x)))


def _rows(shape, r0=0):
    return lax.broadcasted_iota(jnp.int32, shape, 0) + r0


def _lanes(shape):
    return lax.broadcasted_iota(jnp.int32, shape, 1)


HALO = 8


def _fill_padded(pad_ref, x_ref):
    pad_ref[0:HALO, :] = jnp.zeros((HALO, pad_ref.shape[1]), F32)
    pad_ref[T + HALO:T + 2 * HALO, :] = jnp.zeros((HALO, pad_ref.shape[1]), F32)

    def step(c, carry):
        r0 = pl.multiple_of(c * Q, Q)
        pad_ref[pl.ds(r0 + HALO, Q), :] = x_ref[pl.ds(r0, Q), :].astype(F32)
        return carry

    lax.fori_loop(0, NCH, step, 0)


def _back(pad_ref, r0):
    win = pad_ref[pl.ds(r0, Q + HALO), :]
    return lambda s: win[HALO:, :] if s == 0 else pltpu.roll(win, s, axis=0)[HALO:, :]


def _ahead(pad_ref, r0):
    win = pad_ref[pl.ds(r0 + HALO, Q + HALO), :]
    return lambda s: win[:Q, :] if s == 0 else pltpu.roll(win, Q + HALO - s, axis=0)[:Q, :]


def _conv(back, w, b):
    y = b + w[3:4, :] * back(0)
    for k in range(3):
        y = y + w[k:k + 1, :] * back(3 - k)
    return y


def _conv_bwd_x(ahead, w):
    dx = w[3:4, :] * ahead(0)
    for k in range(3):
        dx = dx + w[k:k + 1, :] * ahead(3 - k)
    return dx


def _conv_bwd_w(dy, back):
    dws = [jnp.sum(dy * back(3 - k), axis=0, keepdims=True) for k in range(4)]
    return jnp.concatenate(dws, axis=0), jnp.sum(dy, axis=0, keepdims=True)


def _chunks(fn, unrolled=False):
    if unrolled:
        for c in range(NCH):
            fn(c * Q)
        return

    def step(c, carry):
        fn(pl.multiple_of(c * Q, Q))
        return carry

    lax.fori_loop(0, NCH, step, 0)


HALF = RC // 2


def _col_tiles(n, tn, fn):
    def step(j, carry):
        fn(pl.multiple_of(j * tn, tn))
        return carry

    lax.fori_loop(0, n // tn, step, 0)


def _rows_spec(cols, block_col=0):
    return _spec((RC, cols), lambda i: (i, block_col))


def _whole(shape):
    return _spec(shape, lambda i: tuple(0 for _ in shape), single=True)


def _vec(cols):
    return _spec((1, cols), lambda i: (0, 0))


def _zero_at_first(*refs):
    @pl.when(pl.program_id(0) == 0)
    def _():
        for r in refs:
            r[...] = jnp.zeros_like(r)


ANY_SPEC = pl.BlockSpec(memory_space=pl.ANY)


def _arriving(src, dst, sems, starts, rows):
    n, ahead = len(starts), 2
    first = pl.program_id(0) == 0

    def piece(k):
        r0 = starts[0]
        for j in range(1, n):
            r0 = jnp.where(k == j, starts[j], r0)
        at = pl.ds(pl.multiple_of(r0, 16), rows)
        return pltpu.make_async_copy(src.at[at], dst.at[at], sems.at[k])

    @pl.when(first)
    def _():
        for k in range(min(ahead, n)):
            piece(k).start()

    def ready(k):
        k = jnp.asarray(k, jnp.int32)

        @pl.when(first)
        def _():
            piece(k).wait()

            @pl.when(k + ahead < n)
            def _():
                piece(k + ahead).start()

    return ready


IN_RUNS = ((PZ, 0, 1024), (PXBC, 1024, XBC), (PG, 2576, 2048))
IN_TILE = 512
IN_TILE_ROWS = [wrow + IN_TILE * j for _, wrow, width in IN_RUNS for j in range(width // IN_TILE)]


def _in_tiles(fn):
    done = 0
    for pcol, wrow, width in IN_RUNS:
        def step(j, carry, pcol=pcol, wrow=wrow, done=done):
            fn(pl.multiple_of(pcol + j * IN_TILE, IN_TILE), pl.multiple_of(wrow + j * IN_TILE, 16), done + j)
            return carry

        lax.fori_loop(0, width // IN_TILE, step, 0)
        done += width // IN_TILE


def in_proj(h0, wn, w_t, w_dt):
    def body(h_ref, wn_ref, w_hbm, wdt_ref, o_ref, dt_ref, u_ref, w_ref, w_sems):
        ready = _arriving(w_hbm, w_ref, w_sems, IN_TILE_ROWS, IN_TILE)
        for r in (0, HALF):
            u_ref[r:r + HALF, :] = _rms(h_ref[r:r + HALF, :], wn_ref[...]).astype(BF)

        def tile(pcol, wrow, k):
            ready(k)
            o_ref[:, pl.ds(pcol, IN_TILE)] = lax.dot_general(u_ref[...], w_ref[pl.ds(wrow, IN_TILE), :], NT_DIMS,
                                                             preferred_element_type=F32).astype(BF)

        _in_tiles(tile)
        dt_ref[...] = lax.dot_general(u_ref[...], wdt_ref[...], NT_DIMS, preferred_element_type=F32)

    return pl.pallas_call(
        body, grid=(T // RC,), in_specs=[_rows_spec(D), _vec(D), ANY_SPEC, _whole((256, D))],
        out_specs=[_rows_spec(NP_IN), _rows_spec(256), _rows_spec(D)],
        out_shape=[jax.ShapeDtypeStruct((T, NP_IN), BF), jax.ShapeDtypeStruct((T, 256), F32), jax.ShapeDtypeStruct((T, D), BF)],
        scratch_shapes=[pltpu.VMEM((IN_COLS, D), BF), pltpu.SemaphoreType.DMA((len(IN_TILE_ROWS),))],
        compiler_params=_params(), name="in_proj")(h0, wn, w_t, w_dt)


def out_proj(yn_ssd, proj, hseq, lru_nw, w_out, h0):
    def body(y_ref, g_ref, h_ref, wn_ref, w_ref, r_ref, o_ref, cat_ref):
        cat_ref[:, 0:SSD_W] = y_ref[...]
        for r in (0, HALF):
            y = _gelu(g_ref[r:r + HALF, :].astype(F32)) * h_ref[r:r + HALF, :]
            cat_ref[r:r + HALF, SSD_W:] = _rms(y, wn_ref[...]).astype(BF)

        def tile(c0):
            o_ref[:, pl.ds(c0, 512)] = r_ref[:, pl.ds(c0, 512)] + jnp.dot(cat_ref[...], w_ref[:, pl.ds(c0, 512)], preferred_element_type=F32)

        _col_tiles(D, 512, tile)

    return pl.pallas_call(
        body, grid=(T // RC,),
        in_specs=[_rows_spec(SSD_W), _rows_spec(LRU_W, PG // LRU_W), _rows_spec(LRU_W), _vec(LRU_W), _whole((SSD_W + LRU_W, D)), _rows_spec(D)],
        out_specs=[_rows_spec(D), _rows_spec(SSD_W + LRU_W)],
        out_shape=[jax.ShapeDtypeStruct((T, D), F32), jax.ShapeDtypeStruct((T, SSD_W + LRU_W), BF)],
        compiler_params=_params(), name="out_proj")(yn_ssd, proj, hseq, lru_nw, w_out, h0)


def out_proj_bwd(dh1_b, w_out, proj, hseq, lru_nw, after):
    def body(d_ref, w_ref, g_ref, h_ref, wn_ref, _after, dy_ref, dh_ref, dg_ref, dw_ref, dl_scr):
        _zero_at_first(dw_ref)

        def tile(c0):
            dy_ref[:, pl.ds(c0, 512)] = lax.dot_general(d_ref[...], w_ref[pl.ds(c0, 512), :], NT_DIMS, preferred_element_type=F32)
            dl_scr[:, pl.ds(c0, 512)] = lax.dot_general(d_ref[...], w_ref[pl.ds(SSD_W + c0, 512), :], NT_DIMS, preferred_element_type=F32)

        _col_tiles(SSD_W, 512, tile)

        for r in (0, HALF):
            g = g_ref[r:r + HALF, :].astype(F32)
            h = h_ref[r:r + HALF, :]
            ge = _gelu(g)
            dy, dw = _rms_bwd(dl_scr[r:r + HALF, :], ge * h, wn_ref[...])
            dw_ref[...] += jnp.sum(dw, axis=0, keepdims=True)
            dh_ref[r:r + HALF, :] = dy * ge
            dg_ref[r:r + HALF, :] = (dy * h * _gelu_grad(g)).astype(BF)

    return pl.pallas_call(
        body, grid=(T // RC,),
        in_specs=[_rows_spec(D), _whole((SSD_W + LRU_W, D)), _rows_spec(LRU_W, PG // LRU_W), _rows_spec(LRU_W), _vec(LRU_W), ANY_SPEC],
        out_specs=[_rows_spec(SSD_W), _rows_spec(LRU_W), _rows_spec(LRU_W), _vec(LRU_W)],
        out_shape=[jax.ShapeDtypeStruct((T, SSD_W), F32), jax.ShapeDtypeStruct((T, LRU_W), F32), jax.ShapeDtypeStruct((T, LRU_W), BF),
                   jax.ShapeDtypeStruct((1, LRU_W), F32)],
        scratch_shapes=[pltpu.VMEM((RC, LRU_W), F32)],
        compiler_params=_params(), name="out_proj_bwd")(dh1_b, w_out, proj, hseq, lru_nw, after)


def in_proj_bwd(dz, dg, dxl, dxbc, ddt, w_t, w_dt, h0, wn, dh1, after):
    first = NPAD + N_META

    def body(dz_ref, dg_ref, dxl_ref, dxbc_ref, ddt_ref, w_hbm, wdt_ref, h_ref, wn_ref, r_ref, _after, gx_hbm, meta_ref, dw_ref, du_scr, o_ref, sem,
             w_ref, w_sems):
        i = pl.program_id(0)
        ready = _arriving(w_hbm, w_ref, w_sems, IN_TILE_ROWS, IN_TILE)
        _zero_at_first(dw_ref)
        du_scr[...] = jnp.dot(ddt_ref[...], wdt_ref[...], preferred_element_type=F32)
        done = 0
        for d_ref, wrow, width in ((dz_ref, 0, 1024), (dxbc_ref, 1024, XBC), (dg_ref, 2576, 1024), (dxl_ref, 3600, 1024)):
            def step(j, carry, d_ref=d_ref, wrow=wrow, done=done):
                c0 = pl.multiple_of(j * IN_TILE, IN_TILE)
                ready(done + j)
                du_scr[...] += jnp.dot(d_ref[:, pl.ds(c0, IN_TILE)], w_ref[pl.ds(pl.multiple_of(wrow + c0, 16), IN_TILE), :],
                                       preferred_element_type=F32)
                return carry

            lax.fori_loop(0, width // IN_TILE, step, 0)
            done += width // IN_TILE
        for r in (0, HALF):
            dh, dw = _rms_bwd(du_scr[r:r + HALF, :], h_ref[r:r + HALF, :], wn_ref[...])
            dw_ref[...] += jnp.sum(dw, axis=0, keepdims=True)
            o_ref[r:r + HALF, :] = dh + r_ref[r:r + HALF, :]

        @pl.when(i == 0)
        def _():
            meta_ref[...] = o_ref[NPAD:first, :]
            head = pltpu.make_async_copy(o_ref.at[pl.ds(first, RC - first)], gx_hbm.at[pl.ds(0, RC - first)], sem)
            head.start()
            head.wait()

        @pl.when(i > 0)
        def _():
            rest = pltpu.make_async_copy(o_ref, gx_hbm.at[pl.ds(pl.multiple_of(i * RC - first, 32), RC)], sem)
            rest.start()
            rest.wait()

    return pl.pallas_call(
        body, grid=(T // RC,),
        in_specs=[_rows_spec(SSD_W), _rows_spec(LRU_W), _rows_spec(LRU_W), _rows_spec(XBC), _rows_spec(256), ANY_SPEC,
                  _whole((256, D)), _rows_spec(D), _vec(D), _rows_spec(D), ANY_SPEC],
        out_specs=[ANY_SPEC, _spec((N_META, D), lambda i: (0, 0)), _vec(D)],
        out_shape=[jax.ShapeDtypeStruct((SEQ, D), F32), jax.ShapeDtypeStruct((N_META, D), F32), jax.ShapeDtypeStruct((1, D), F32)],
        scratch_shapes=[pltpu.VMEM((RC, D), F32), pltpu.VMEM((RC, D), F32), pltpu.SemaphoreType.DMA,
                        pltpu.VMEM((IN_COLS, D), BF), pltpu.SemaphoreType.DMA((len(IN_TILE_ROWS),))],
        compiler_params=_params(), name="in_proj_bwd")(dz, dg, dxl, dxbc, ddt, w_t, w_dt, h0, wn, dh1, after)


GRAD_TILE = 256


def weight_grad(name, a, u1):
    tm = GRAD_TILE

    def body(a_ref, u_ref, o_ref):
        o_ref[...] = lax.dot_general(a_ref[...], u_ref[...], TN_DIMS, preferred_element_type=F32).astype(BF)

    return pl.pallas_call(
        body, grid=(a.shape[1] // tm,),
        in_specs=[_spec((T, tm), lambda j: (0, j)), _spec((T, D), lambda j: (0, 0), single=True)],
        out_specs=_spec((tm, D), lambda j: (j, 0)),
        out_shape=jax.ShapeDtypeStruct((a.shape[1], D), BF),
        compiler_params=_params(), name=name)(a, u1)


def in_weight_grad(dz, dg, dxl, dxbc, ddt, u1):
    tm = GRAD_TILE
    per_row = D // 128
    parts = (dz, dg, dxl, dxbc, ddt)
    first_rows = (0, 2576, 3600, 1024, 2560)
    tiles = [p.shape[1] // tm for p in parts]
    starts = [sum(tiles[:k]) for k in range(len(parts))]
    last = sum(tiles) - 1
    dt_lines = 8 * per_row

    def body(*refs):
        a_refs, u_ref, o_hbm, mix_scr, stage, sems = refs[:5], refs[5], refs[6], refs[7], refs[8], refs[9]
        step = pl.program_id(0)
        slot = step % 2
        line0 = 0
        for a_ref, start, n, first in zip(a_refs, starts, tiles, first_rows):
            here = (step >= start) & (step < start + n)
            line0 = jnp.where(here, per_row * (first + tm * (step - start)), line0)

            @pl.when(here)
            def _(a_ref=a_ref):
                res = lax.dot_general(a_ref[...], u_ref[...], TN_DIMS, preferred_element_type=F32)
                for q in range(per_row):
                    mix_scr[pl.ds(q, tm, stride=per_row), :] = res[:, 128 * q:128 * q + 128]

        def tile_copy(of_slot, to):
            return pltpu.make_async_copy(stage.at[of_slot], o_hbm.at[pl.ds(to, per_row * tm)], sems.at[of_slot])

        @pl.when(step >= 2)
        def _():
            tile_copy(slot, 0).wait()

        stage[slot] = mix_scr[...].astype(BF)

        @pl.when(step < last)
        def _():
            tile_copy(slot, pl.multiple_of(line0, 128)).start()

        @pl.when(step == last)
        def _():
            halves = [pltpu.make_async_copy(stage.at[slot, pl.ds(128 * per_row * k, dt_lines)],
                                            o_hbm.at[pl.ds(per_row * (first_rows[-1] + 8 * k), dt_lines)], sems.at[2 + k]) for k in range(2)]
            for cp in halves:
                cp.start()
            tile_copy(1 - slot, 0).wait()
            for cp in halves:
                cp.wait()

    def tile_of(start, n):
        return lambda j: (0, jnp.clip(j - start, 0, n - 1))

    return pl.pallas_call(
        body, grid=(last + 1,),
        in_specs=[_spec((T, tm), tile_of(s, n)) for s, n in zip(starts, tiles)] + [_spec((T, D), lambda j: (0, 0), single=True)],
        out_specs=pl.BlockSpec(memory_space=pl.ANY),
        out_shape=jax.ShapeDtypeStruct((per_row * IN_COLS, 128), BF),
        scratch_shapes=[pltpu.VMEM((per_row * tm, 128), F32), pltpu.VMEM((2, per_row * tm, 128), BF), pltpu.SemaphoreType.DMA((4,))],
        compiler_params=_params(), name="dw_in")(*parts, u1)


def _ssd_chunk_common(row0, dt_ref, b_ref, c_ref, bias, a_neg):
    shape = (Q, Q)
    lane = _lanes(shape)
    sub = _rows(shape)
    live = (_rows(shape, row0) >= NPAD) & (lane < 8)
    dtr = dt_ref[:, :]
    dt = jnp.where(live, _softplus(dtr + bias), 0.0)
    d_a = dt * a_neg
    tri = (sub >= lane).astype(F32)
    cs = jnp.dot(tri, d_a, precision=lax.Precision.HIGHEST, preferred_element_type=F32)
    cs_t = cs.T
    b_f = b_ref[:, :]
    bc = b_f.astype(BF)
    cc = c_ref[:, :].astype(BF)
    cb = lax.dot_general(cc, bc, NT_DIMS, preferred_element_type=F32)
    cs_last = cs[Q - 1:Q, :]
    return dict(lane=lane, sub=sub, live=live, dtr=dtr, dt=dt, cs=cs, cs_t=cs_t, bc=bc, cc=cc, cb=cb, bc_t=b_f.T.astype(BF),
                ecs=jnp.exp(cs), dsm=jnp.exp(cs_last - cs), gam=jnp.exp(cs_last))


def _pair(lane_even, mat, j):
    return jnp.where(lane_even, mat[:, j:j + 1], mat[:, j + 1:j + 2])


def _pair_row(lane_even, mat, j):
    return jnp.where(lane_even[0:1, :], mat[:, j:j + 1], mat[:, j + 1:j + 2])


def _head_decay(cm, j):
    seg = cm["cs"][:, j:j + 1] - cm["cs_t"][j:j + 1, :]
    return jnp.exp(jnp.where(cm["sub"] >= cm["lane"], seg, -jnp.inf))


def _head_decay_t(cm, j):
    seg = cm["cs_t"][j:j + 1, :] - cm["cs"][:, j:j + 1]
    return jnp.exp(jnp.where(cm["lane"] >= cm["sub"], seg, -jnp.inf))


def _conv_window(raw_ref, halo_ref, pad_scr):
    pad_scr[0:HALO, :] = halo_ref[...].astype(F32)[halo_ref.shape[0] - HALO:, :]
    pad_scr[HALO:HALO + Q, :] = raw_ref[...].astype(F32)
    win = pad_scr[...]
    return lambda s: win[HALO:, :] if s == 0 else pltpu.roll(win, s, axis=0)[HALO:, :]


def _xbc_cols(g):
    return slice(512 * g, 512 * g + 512), slice(SSD_W + 128 * g, SSD_W + 128 * g + 128), slice(SSD_W + 256 + 128 * g, SSD_W + 384 + 128 * g)


def ssd_fwd(proj, dt_raw, conv_w, conv_b, dt_bias2, a_log2, d2, norm_w):
    def body(raw_ref, halo_ref, dt_all, z_all, cw_ref, cb_ref, bias_all, alog_all, d_all, nw_all, yn_all, y_all, hp_all,
             h_all, pad_scr, act_scr):
        @pl.when(pl.program_id(0) == 0)
        def _():
            h_all[...] = jnp.zeros_like(h_all)

        pre = _conv(_conv_window(raw_ref, halo_ref, pad_scr), cw_ref[...], cb_ref[...])
        act_scr[...] = pre * _sigmoid(pre)
        for g in range(2):
            wide, thin = slice(512 * g, 512 * g + 512), slice(128 * g, 128 * g + 128)
            xs, bs, cs = _xbc_cols(g)
            group(act_scr.at[:, xs], act_scr.at[:, bs], act_scr.at[:, cs], dt_all.at[:, thin], z_all.at[:, wide], bias_all.at[g],
                  alog_all.at[g], d_all.at[g], nw_all.at[:, wide], yn_all.at[:, wide], y_all.at[:, wide], hp_all.at[g, 0], h_all.at[g])

    def group(x_ref, b_ref, c_ref, dt_ref, z_ref, bias_ref, alog_ref, d_ref, nw_ref, yn_ref, y_ref, hp_ref, h_scr):
        bias = bias_ref[...]
        a_neg = -jnp.exp(alog_ref[...])
        dsk = d_ref[...]
        cm = _ssd_chunk_common(pl.program_id(0) * Q, dt_ref, b_ref, c_ref, bias, a_neg)
        lane_even = cm["lane"] < 64
        for p in range(4):
            je, jo = 2 * p, 2 * p + 1
            xp = x_ref[:, 128 * p:128 * p + 128]
            xdt = xp * _pair(lane_even, cm["dt"], je)
            xdt_b = xdt.astype(BF)
            m_e = (cm["cb"] * _head_decay(cm, je)).astype(BF)
            m_o = (cm["cb"] * _head_decay(cm, jo)).astype(BF)
            zero = jnp.zeros_like(xdt_b)
            yd = (jnp.dot(m_e, jnp.where(lane_even, xdt_b, zero), preferred_element_type=F32)
                  + jnp.dot(m_o, jnp.where(lane_even, zero, xdt_b), preferred_element_type=F32))
            hp = h_scr[p]
            hp_ref[p] = hp
            yo = jnp.dot(cm["cc"], hp.astype(BF), preferred_element_type=F32) * _pair(lane_even, cm["ecs"], je)
            y_ref[:, 128 * p:128 * p + 128] = yd + yo + xp * _pair_row(lane_even, dsk, je)
            st = jnp.dot(cm["bc_t"], (xdt * _pair(lane_even, cm["dsm"], je)).astype(BF), preferred_element_type=F32)
            h_scr[p] = hp * _pair_row(lane_even, cm["gam"], je) + st
        zc = z_ref[:, :].astype(F32)
        gated = y_ref[:, :] * (zc * _sigmoid(zc))
        yn_ref[:, :] = _rms(gated, nw_ref[...]).astype(BF)

    par = _spec((2, 1, 128), lambda c: (0, 0, 0))
    wide = _spec((Q, SSD_W), lambda c: (c, 0))
    xbc = PXBC // XBC
    halo = 2 * HALO
    return pl.pallas_call(
        body, grid=(NCH,),
        in_specs=[_spec((Q, XBC), lambda c: (c, xbc)), _spec((halo, XBC), lambda c: (jnp.maximum(c * (Q // halo) - 1, 0), xbc)),
                  _spec((Q, 256), lambda c: (c, 0)), wide, _spec((4, XBC), lambda c: (0, 0)), _spec((1, XBC), lambda c: (0, 0)),
                  par, par, par, _spec((1, SSD_W), lambda c: (0, 0))],
        out_specs=[wide, wide, _spec((2, 1, 4, 128, 128), lambda c: (0, c, 0, 0, 0))],
        out_shape=[jax.ShapeDtypeStruct((T, SSD_W), BF), jax.ShapeDtypeStruct((T, SSD_W), F32),
                   jax.ShapeDtypeStruct((2, NCH, 4, 128, 128), F32)],
        scratch_shapes=[pltpu.VMEM((2, 4, 128, 128), F32), pltpu.VMEM((Q + HALO, XBC), F32), pltpu.VMEM((Q, XBC), F32)],
        compiler_params=_params(), name="ssd_fwd")(proj, proj, dt_raw, proj, conv_w, conv_b, dt_bias2, a_log2, d2, norm_w)


def ssd_bwd(dyn, proj, dt_raw, conv_w, conv_b, y_pre, h_prev, dt_bias2, a_log2, d2, norm_w, after):
    def body(dyn_all, raw_ref, halo_ref, dt_all, z_all, y_all, hp_all, cw_ref, cb_ref, bias_all, alog_all, d_all, nw_all, _after,
             dz_all, dxbc_ref, ddt_all, dpar_all, dnw_all, dcw_ref, dcb_ref, dh_all, acc_all, pad_scr, act_scr, dsilu_scr, dact_scr, dpad_scr):
        @pl.when(pl.program_id(0) == 0)
        def _():
            dh_all[...] = jnp.zeros_like(dh_all)
            acc_all[...] = jnp.zeros_like(acc_all)
            dnw_all[...] = jnp.zeros_like(dnw_all)
            dcw_ref[...] = jnp.zeros_like(dcw_ref)
            dcb_ref[...] = jnp.zeros_like(dcb_ref)
            dpad_scr[Q:Q + HALO, :] = jnp.zeros((HALO, XBC), F32)

        back = _conv_window(raw_ref, halo_ref, pad_scr)
        pre = _conv(back, cw_ref[...], cb_ref[...])
        sg = _sigmoid(pre)
        act_scr[...] = pre * sg
        dsilu_scr[...] = sg * (1.0 + pre * (1.0 - sg))
        for g in range(2):
            wide, thin = slice(512 * g, 512 * g + 512), slice(128 * g, 128 * g + 128)
            xs, bs, cs = _xbc_cols(g)
            group(dyn_all.at[:, wide], act_scr.at[:, xs], act_scr.at[:, bs], act_scr.at[:, cs], dt_all.at[:, thin], z_all.at[:, wide],
                  y_all.at[:, wide], hp_all.at[g, 0], bias_all.at[g], alog_all.at[g], d_all.at[g], nw_all.at[:, wide],
                  dz_all.at[:, wide], dact_scr.at[:, xs], dact_scr.at[:, bs], dact_scr.at[:, cs], ddt_all.at[:, thin], dpar_all.at[g],
                  dnw_all.at[:, wide], dh_all.at[g], acc_all.at[g])
        dpre = dact_scr[...] * dsilu_scr[...]
        dcw, dcb = _conv_bwd_w(dpre, back)
        dcw_ref[...] += dcw
        dcb_ref[...] += dcb
        dpad_scr[0:Q, :] = dpre
        win = dpad_scr[...]
        dxbc_ref[...] = _conv_bwd_x(lambda s: win[:Q, :] if s == 0 else pltpu.roll(win, Q + HALO - s, axis=0)[:Q, :], cw_ref[...]).astype(BF)
        dpad_scr[Q:Q + HALO, :] = dpre[0:HALO, :]

    def group(dyn_ref, x_ref, b_ref, c_ref, dt_ref, z_ref, y_ref, hp_ref, bias_ref, alog_ref, d_ref, nw_ref,
              dz_ref, dx_ref, db_ref, dc_ref, ddt_ref, dpar_ref, dnw_ref, dh_scr, acc_scr):
        ci = pl.program_id(0)
        bias = bias_ref[...]
        a_neg = -jnp.exp(alog_ref[...])
        dsk = d_ref[...]
        cm = _ssd_chunk_common((NCH - 1 - ci) * Q, dt_ref, b_ref, c_ref, bias, a_neg)
        lane, sub = cm["lane"], cm["sub"]
        lane_even = lane < 64
        cc_t = c_ref[:, :].T.astype(BF)
        cb_t = lax.dot_general(cm["bc"], cm["cc"], NT_DIMS, preferred_element_type=F32)
        zc = z_ref[:, :].astype(F32)
        yc = y_ref[:, :]
        sg = _sigmoid(zc)
        sz = zc * sg
        dgated, dnw = _rms_bwd(dyn_ref[:, :], yc * sz, nw_ref[...])
        dnw_ref[...] += jnp.sum(dnw, axis=0, keepdims=True)
        dz_ref[:, :] = (dgated * yc * (sg * (1.0 + zc * (1.0 - sg)))).astype(BF)
        dy_all = dgated * sz
        dcb = jnp.zeros((Q, Q), F32)
        dcb_t = jnp.zeros((Q, Q), F32)
        db_acc = jnp.zeros((Q, Q), F32)
        dc_acc = jnp.zeros((Q, Q), F32)
        dcs = jnp.zeros((Q, Q), F32)
        ddt = jnp.zeros((Q, Q), F32)
        for p in range(4):
            je, jo = 2 * p, 2 * p + 1
            xp = x_ref[:, 128 * p:128 * p + 128]
            dy = dy_all[:, 128 * p:128 * p + 128]
            dt_p = _pair(lane_even, cm["dt"], je)
            xdt = xp * dt_p
            xdt_b = xdt.astype(BF)
            dy_b = dy.astype(BF)
            zero = jnp.zeros_like(dy_b)
            hp = hp_ref[p]
            hp_b = hp.astype(BF)
            dh = dh_scr[p]
            dh_b = dh.astype(BF)
            acc_scr[p:p + 1, :] += jnp.sum(dy * xp, axis=0, keepdims=True)
            dxp = dy * _pair_row(lane_even, dsk, je)
            e_p = _pair(lane_even, cm["ecs"], je)
            g_p = jnp.dot(cm["cc"], hp_b, preferred_element_type=F32)
            dg_b = (dy * e_p).astype(BF)
            de = dy * g_p * e_p
            dc_acc = dc_acc + lax.dot_general(dg_b, hp_b, NT_DIMS, preferred_element_type=F32)
            dh_in = jnp.dot(cc_t, dg_b, preferred_element_type=F32)
            ds_p = _pair(lane_even, cm["dsm"], je)
            r_p = jnp.dot(cm["bc"], dh_b, preferred_element_type=F32)
            dxdt = r_p * ds_p
            tt = r_p * xdt * ds_p
            db_acc = db_acc + lax.dot_general((xdt * ds_p).astype(BF), dh_b, NT_DIMS, preferred_element_type=F32)
            dgam_m = jnp.sum(dh * hp, axis=0, keepdims=True)
            for j, even in ((je, True), (jo, False)):
                sel = lane_even if even else jnp.logical_not(lane_even)
                dy_j = jnp.where(sel, dy_b, zero)
                l_j = _head_decay(cm, j)
                l_jt = _head_decay_t(cm, j)
                m_j = cm["cb"] * l_j
                m_jt = cb_t * l_jt
                dm = lax.dot_general(dy_j, xdt_b, NT_DIMS, preferred_element_type=F32)
                dm_t = lax.dot_general(xdt_b, dy_j, NT_DIMS, preferred_element_type=F32)
                dxdt = dxdt + jnp.dot(m_jt.astype(BF), dy_j, preferred_element_type=F32)
                dcb = dcb + dm * l_j
                dcb_t = dcb_t + dm_t * l_jt
                t_j = jnp.where(sel, tt, 0.0)
                col = jnp.sum(dm * m_j - dm_t * m_jt + (jnp.where(sel, de, 0.0) - t_j), axis=1, keepdims=True)
                gam_j = cm["gam"][:, j:j + 1]
                last = (jnp.sum(jnp.sum(t_j, axis=0, keepdims=True), axis=1, keepdims=True)
                        + jnp.sum(jnp.where(sel[0:1, :], dgam_m, 0.0), axis=1, keepdims=True) * gam_j)
                col = col + jnp.where(sub[:, 0:1] == Q - 1, last, 0.0)
                dcs = dcs + jnp.where(lane == j, col, 0.0)
            dh_scr[p] = dh_in + dh * _pair_row(lane_even, cm["gam"], je)
            dx_ref[:, 128 * p:128 * p + 128] = dxp + dxdt * dt_p
            dd = dxdt * xp
            ddt = ddt + jnp.where(lane == je, jnp.sum(jnp.where(lane_even, dd, 0.0), axis=1, keepdims=True), 0.0)
            ddt = ddt + jnp.where(lane == jo, jnp.sum(jnp.where(lane_even, 0.0, dd), axis=1, keepdims=True), 0.0)
        dc_ref[:, :] = dc_acc + jnp.dot(dcb.astype(BF), cm["bc"], preferred_element_type=F32)
        db_ref[:, :] = db_acc + jnp.dot(dcb_t.astype(BF), cm["cc"], preferred_element_type=F32)
        tri_t = (sub <= lane).astype(F32)
        dd_a = jnp.dot(tri_t, dcs, precision=lax.Precision.HIGHEST, preferred_element_type=F32)
        ddt = ddt + dd_a * a_neg
        acc_scr[5:6, :] += jnp.sum(dd_a * cm["dt"], axis=0, keepdims=True)
        draw = jnp.where(cm["live"], ddt * _sigmoid_gate(cm["dtr"] + bias), 0.0)
        acc_scr[4:5, :] += jnp.sum(draw, axis=0, keepdims=True)
        ddt_ref[:, :] = draw.astype(BF)

        @pl.when(ci == NCH - 1)
        def _():
            lane1 = _lanes((1, 128))
            dd = jnp.zeros((1, 128), F32)
            for p in range(4):
                row = acc_scr[p:p + 1, :]
                dd = dd + jnp.where(lane1 == 2 * p, jnp.sum(jnp.where(lane1 < 64, row, 0.0), axis=1, keepdims=True), 0.0)
                dd = dd + jnp.where(lane1 == 2 * p + 1, jnp.sum(jnp.where(lane1 < 64, 0.0, row), axis=1, keepdims=True), 0.0)
            dpar_ref[...] = jnp.concatenate([acc_scr[4:5, :], acc_scr[5:6, :] * a_neg, dd, jnp.zeros((5, 128), F32)], axis=0)

    par = _spec((2, 1, 128), lambda c: (0, 0, 0))
    wide = _spec((Q, SSD_W), lambda c: (NCH - 1 - c, 0))
    thin = _spec((Q, 256), lambda c: (NCH - 1 - c, 0))
    vec = _spec((1, SSD_W), lambda c: (0, 0))
    xbc = PXBC // XBC
    halo = 2 * HALO
    chunk = pltpu.VMEM((Q, XBC), F32)
    padded = pltpu.VMEM((Q + HALO, XBC), F32)
    return pl.pallas_call(
        body, grid=(NCH,),
        in_specs=[wide, _spec((Q, XBC), lambda c: (NCH - 1 - c, xbc)),
                  _spec((halo, XBC), lambda c: (jnp.maximum((NCH - 1 - c) * (Q // halo) - 1, 0), xbc)), thin, wide, wide,
                  _spec((2, 1, 4, 128, 128), lambda c: (0, NCH - 1 - c, 0, 0, 0)), _spec((4, XBC), lambda c: (0, 0)),
                  _spec((1, XBC), lambda c: (0, 0)), par, par, par, vec, ANY_SPEC],
        out_specs=[wide, _spec((Q, XBC), lambda c: (NCH - 1 - c, 0)), thin, _spec((2, 8, 128), lambda c: (0, 0, 0)), vec,
                   _spec((4, XBC), lambda c: (0, 0)), _spec((1, XBC), lambda c: (0, 0))],
        out_shape=[jax.ShapeDtypeStruct((T, SSD_W), BF), jax.ShapeDtypeStruct((T, XBC), BF), jax.ShapeDtypeStruct((T, 256), BF),
                   jax.ShapeDtypeStruct((2, 8, 128), F32), jax.ShapeDtypeStruct((1, SSD_W), F32), jax.ShapeDtypeStruct((4, XBC), F32),
                   jax.ShapeDtypeStruct((1, XBC), F32)],
        scratch_shapes=[pltpu.VMEM((2, 4, 128, 128), F32), pltpu.VMEM((2, 8, 128), F32), padded, chunk, chunk, chunk, padded],
        compiler_params=_params(), name="ssd_bwd")(dyn, proj, proj, dt_raw, proj, y_pre, h_prev, conv_w, conv_b, dt_bias2, a_log2, d2, norm_w, after)


def _lru_gates(back, cw, cb, wa, ba, wx, bx, lam):
    xr = _conv(back, cw, cb)
    xr_b = xr.astype(BF)
    r = _sigmoid_gate(jnp.dot(xr_b, wa, preferred_element_type=F32) + ba)
    i = _sigmoid_gate(jnp.dot(xr_b, wx, preferred_element_type=F32) + bx)
    sp = _softplus(-lam)
    la = (-LRU_C) * r * sp
    a = jnp.exp(la)
    mult2 = -jnp.tanh(la) * (a * a + 1.0)
    return xr, xr_b, r, i, sp, a, jnp.sqrt(mult2), mult2


SEG_LEN = 68
SEGS = T // SEG_LEN


def _seg_rows(j, k, off=0):
    return pl.ds(off + j * 8 * SEG_LEN + k, 8, stride=SEG_LEN)


def _segmented_scan(mul_ref, mul_row0, add_ref, out_ref, loc_scr, prod_scr, carry_scr, reverse):
    groups = SEGS // 8
    off = mul_row0 + (1 if reverse else 0)

    def local(i, carry):
        k = SEG_LEN - 1 - i if reverse else i
        new = []
        for j in range(groups):
            h, p = carry[2 * j], carry[2 * j + 1]
            m = mul_ref[_seg_rows(j, k, off), :]
            h = m * h + add_ref[_seg_rows(j, k), :]
            p = m * p
            loc_scr[_seg_rows(j, k), :] = h
            prod_scr[_seg_rows(j, k), :] = p
            new += [h, p]
        return tuple(new)

    lax.fori_loop(0, SEG_LEN, local, (jnp.zeros((8, 128), F32), jnp.ones((8, 128), F32)) * groups)

    def chain(i, c):
        s = SEGS - 1 - i if reverse else i
        carry_scr[pl.ds(s, 1), :] = c
        edge = s * SEG_LEN + (0 if reverse else SEG_LEN - 1)
        return loc_scr[pl.ds(edge, 1), :] + prod_scr[pl.ds(edge, 1), :] * c

    lax.fori_loop(0, SEGS, chain, jnp.zeros((1, 128), F32))

    def fold(k, carry):
        for j in range(groups):
            rows = _seg_rows(j, k)
            out_ref[rows, :] = loc_scr[rows, :] + prod_scr[rows, :] * carry_scr[8 * j:8 * j + 8, :]
        return carry

    lax.fori_loop(0, SEG_LEN, fold, 0)


def lru_fwd(proj, cw, cb, wa2, ba, wx2, bx, lam, after):
    def body(x_ref, cw_ref, cb_ref, wa_ref, ba_ref, wx_ref, bx_ref, lam_ref, _after, h_ref, a_ref, xpad, u_scr, loc_scr, prod_scr, carry_scr):
        _fill_padded(xpad, x_ref)

        def chunk(r0):
            xr, _, _, i, _, a, mult, _ = _lru_gates(_back(xpad, r0), cw_ref[...], cb_ref[...], wa_ref[0], ba_ref[...], wx_ref[0], bx_ref[...],
                                                 lam_ref[...])
            a_ref[pl.ds(r0, Q), :] = a
            u_scr[pl.ds(r0, Q), :] = jnp.where(_rows(a.shape, r0) >= NPAD, mult * (i * xr), 0.0)

        _chunks(chunk, unrolled=True)
        _segmented_scan(a_ref, 0, u_scr, h_ref, loc_scr, prod_scr, carry_scr, reverse=False)

    c0 = PXL // 128
    vec = _spec((1, 128), lambda c: (0, c))
    mat = _spec((1, 128, 128), lambda c: (c, 0, 0))
    seq = pltpu.VMEM((T, 128), F32)
    return pl.pallas_call(
        body, grid=(8,),
        in_specs=[_spec((T, 128), lambda c: (0, c0 + c)), _spec((4, 128), lambda c: (0, c)), vec, mat, vec, mat, vec, vec, ANY_SPEC],
        out_specs=[_spec((T, 128), lambda c: (0, c)), _spec((T, 128), lambda c: (0, c))],
        out_shape=[jax.ShapeDtypeStruct((T, LRU_W), F32), jax.ShapeDtypeStruct((T, LRU_W), F32)],
        scratch_shapes=[pltpu.VMEM((T + 2 * HALO, 128), F32), seq, seq, seq, pltpu.VMEM((SEGS, 128), F32)],
        compiler_params=_params(), name="lru_fwd")(proj, cw, cb, wa2, ba, wx2, bx, lam, after)


def lru_bwd(dh_out, a, hseq, proj, cw, cb, wa2, ba, wx2, bx, lam):
    def body(d_ref, a_ref, h_ref, x_ref, cw_ref, cb_ref, wa_ref, ba_ref, wx_ref, bx_ref, lam_ref,
             dx_ref, dcw_ref, dcb_ref, dwa_ref, dba_ref, dwx_ref, dbx_ref, dlam_ref, xpad, hpad, dpad, dh_ref, loc_scr, prod_scr, carry_scr):
        _fill_padded(dpad, a_ref)
        _segmented_scan(dpad, HALO, d_ref, dh_ref, loc_scr, prod_scr, carry_scr, reverse=True)
        _fill_padded(xpad, x_ref)
        _fill_padded(hpad, h_ref)
        dpad[0:HALO, :] = jnp.zeros((HALO, 128), F32)
        dpad[T + HALO:T + 2 * HALO, :] = jnp.zeros((HALO, 128), F32)
        for ref in (dcw_ref, dcb_ref, dwa_ref, dba_ref, dwx_ref, dbx_ref, dlam_ref):
            ref[...] = jnp.zeros_like(ref)
        lam = lam_ref[...]

        def first(r0):
            back = _back(xpad, r0)
            xr, xr_b, r, i, sp, a, mult, mult2 = _lru_gates(back, cw_ref[...], cb_ref[...], wa_ref[0], ba_ref[...], wx_ref[0], bx_ref[...], lam)
            dh = dh_ref[pl.ds(r0, Q), :]
            da = dh * _back(hpad, r0)(1)
            du = jnp.where(_rows(dh.shape, r0) >= NPAD, dh, 0.0)
            dmult = du * (i * xr)
            di = du * (mult * xr)
            dxr = du * (mult * i)
            dla = da * a - dmult * (a * a) * lax.rsqrt(mult2)
            dr = dla * ((-LRU_C) * sp)
            dlam_ref[...] += jnp.sum(dla * ((-LRU_C) * r), axis=0, keepdims=True)
            dpr = dr * r * (1.0 - r)
            dpi = di * i * (1.0 - i)
            dba_ref[...] += jnp.sum(dpr, axis=0, keepdims=True)
            dbx_ref[...] += jnp.sum(dpi, axis=0, keepdims=True)
            dpr_b = dpr.astype(BF)
            dpi_b = dpi.astype(BF)
            dxr = (dxr + lax.dot_general(dpr_b, wa_ref[0], NT_DIMS, preferred_element_type=F32)
                   + lax.dot_general(dpi_b, wx_ref[0], NT_DIMS, preferred_element_type=F32))
            dwa_ref[0] += lax.dot_general(xr_b, dpr_b, TN_DIMS, preferred_element_type=F32)
            dwx_ref[0] += lax.dot_general(xr_b, dpi_b, TN_DIMS, preferred_element_type=F32)
            dpad[pl.ds(r0 + HALO, Q), :] = dxr
            dcw, dcb = _conv_bwd_w(dxr, back)
            dcw_ref[...] += dcw
            dcb_ref[...] += dcb

        _chunks(first, unrolled=True)
        dlam_ref[...] = -dlam_ref[...] * _sigmoid_gate(-lam)

        def second(r0):
            dx_ref[pl.ds(r0, Q), :] = _conv_bwd_x(_ahead(dpad, r0), cw_ref[...]).astype(BF)

        _chunks(second)

    c0 = PXL // 128
    vec = _spec((1, 128), lambda c: (0, c))
    mat = _spec((1, 128, 128), lambda c: (c, 0, 0))
    col = _spec((T, 128), lambda c: (0, c))
    vshape = jax.ShapeDtypeStruct((1, LRU_W), F32)
    mshape = jax.ShapeDtypeStruct((8, 128, 128), F32)
    pad = pltpu.VMEM((T + 2 * HALO, 128), F32)
    seq = pltpu.VMEM((T, 128), F32)
    return pl.pallas_call(
        body, grid=(8,),
        in_specs=[col, col, col, _spec((T, 128), lambda c: (0, c0 + c)), _spec((4, 128), lambda c: (0, c)), vec, mat, vec, mat, vec, vec],
        out_specs=[col, _spec((4, 128), lambda c: (0, c)), vec, mat, vec, mat, vec, vec],
        out_shape=[jax.ShapeDtypeStruct((T, LRU_W), BF), jax.ShapeDtypeStruct((4, LRU_W), F32), vshape, mshape, vshape, mshape, vshape, vshape],
        scratch_shapes=[pad, pad, pad, seq, seq, seq, pltpu.VMEM((SEGS, 128), F32)],
        compiler_params=_params(), name="lru_bwd")(dh_out, a, hseq, proj, cw, cb, wa2, ba, wx2, bx, lam)


FF_TILE = 256
FF_TILE_ROWS = list(range(0, D_FF, FF_TILE))


def gate_up(h1, wn, w_gate, w_up):
    def body(h_ref, wn_ref, wg_hbm, wu_hbm, gt_ref, up_ref, act_ref, u_ref, wg_ref, wu_ref, wg_sems, wu_sems):
        gate_ready = _arriving(wg_hbm, wg_ref, wg_sems, FF_TILE_ROWS, FF_TILE)
        up_ready = _arriving(wu_hbm, wu_ref, wu_sems, FF_TILE_ROWS, FF_TILE)
        for r in (0, HALF):
            u_ref[r:r + HALF, :] = _rms(h_ref[r:r + HALF, :], wn_ref[...]).astype(BF)

        def tile(c0):
            cols = pl.ds(c0, FF_TILE)
            gate_ready(c0 // FF_TILE)
            up_ready(c0 // FF_TILE)
            gt = lax.dot_general(u_ref[...], wg_ref[cols, :], NT_DIMS, preferred_element_type=F32)
            up = lax.dot_general(u_ref[...], wu_ref[cols, :], NT_DIMS, preferred_element_type=F32)
            gt_ref[:, cols] = gt.astype(BF)
            up_ref[:, cols] = up.astype(BF)
            act_ref[:, cols] = (gt * _sigmoid(gt) * up).astype(BF)

        _col_tiles(D_FF, FF_TILE, tile)

    big = jax.ShapeDtypeStruct((T, D_FF), BF)
    return pl.pallas_call(
        body, grid=(T // RC,), in_specs=[_rows_spec(D), _vec(D), ANY_SPEC, ANY_SPEC],
        out_specs=[_rows_spec(D_FF), _rows_spec(D_FF), _rows_spec(D_FF), _rows_spec(D)],
        out_shape=[big, big, big, jax.ShapeDtypeStruct((T, D), BF)],
        scratch_shapes=[pltpu.VMEM((D_FF, D), BF)] * 2 + [pltpu.SemaphoreType.DMA((len(FF_TILE_ROWS),))] * 2,
        compiler_params=_params(), name="gate_up")(h1, wn, w_gate, w_up)


def down_loss(act, w_down, h1, target, wf):
    first = NPAD + N_META

    def body(a_ref, w_ref, r_ref, t_hbm, wf_ref, d_ref, db_ref, l_ref, dw_ref, h_scr, t_ref, t_sem):
        i = pl.program_id(0)
        _zero_at_first(l_ref, dw_ref)
        head = pltpu.make_async_copy(t_hbm.at[pl.ds(0, RC - first)], t_ref.at[pl.ds(first, RC - first)], t_sem)
        rest = pltpu.make_async_copy(t_hbm.at[pl.ds(pl.multiple_of(jnp.maximum(i * RC - first, 0), 32), RC)], t_ref, t_sem)

        @pl.when(i == 0)
        def _():
            t_ref[0:first, :] = jnp.zeros((first, D), F32)
            head.start()

        @pl.when(i > 0)
        def _():
            rest.start()

        def tile(c0):
            cols = pl.ds(c0, 512)
            h_scr[:, cols] = r_ref[:, cols] + jnp.dot(a_ref[...], w_ref[:, cols], preferred_element_type=F32)

        _col_tiles(D, 512, tile)

        @pl.when(i == 0)
        def _():
            head.wait()

        @pl.when(i > 0)
        def _():
            rest.wait()

        for r in (0, HALF):
            h = h_scr[r:r + HALF, :]
            live = _rows((HALF, D), i * RC + r) >= first
            err = jnp.where(live, _rms(h, wf_ref[...]) - t_ref[r:r + HALF, :], 0.0)
            l_ref[...] += 0.5 * jnp.sum(jnp.sum(err * err, axis=1, keepdims=True) * (1.0 / D), axis=0, keepdims=True)
            dh, dw = _rms_bwd(err * (1.0 / D), h, wf_ref[...])
            dw_ref[...] += jnp.sum(dw, axis=0, keepdims=True)
            d_ref[r:r + HALF, :] = dh
            db_ref[r:r + HALF, :] = dh.astype(BF)

    return pl.pallas_call(
        body, grid=(T // RC,),
        in_specs=[_rows_spec(D_FF), _whole((D_FF, D)), _rows_spec(D), pl.BlockSpec(memory_space=pl.ANY), _vec(D)],
        out_specs=[_rows_spec(D), _rows_spec(D), _spec((1, 128), lambda i: (0, 0)), _vec(D)],
        out_shape=[jax.ShapeDtypeStruct((T, D), F32), jax.ShapeDtypeStruct((T, D), BF), jax.ShapeDtypeStruct((1, 128), F32),
                   jax.ShapeDtypeStruct((1, D), F32)],
        scratch_shapes=[pltpu.VMEM((RC, D), F32), pltpu.VMEM((RC, D), F32), pltpu.SemaphoreType.DMA],
        compiler_params=_params(), name="down_loss")(act, w_down, h1, target, wf)


def swiglu_bwd(dh2_b, w_down, gt, up, act, u2):
    tn = 256

    def body(d_hbm, u_hbm, w_ref, gt_ref, up_ref, act_ref, dg_ref, du_ref, gd_ref, gg_ref, gu_ref, d_ref, u_ref, d_sems, u_sems):
        chunks = list(range(0, T, RC))
        d_ready = _arriving(d_hbm, d_ref, d_sems, chunks, RC)
        u_ready = _arriving(u_hbm, u_ref, u_sems, chunks, RC)

        def rows(r0):
            part = pl.ds(r0, RC)
            d_ready(r0 // RC)
            dact = lax.dot_general(d_ref[part, :], w_ref[...], NT_DIMS, preferred_element_type=F32)
            gt_ = gt_ref[part, :].astype(F32)
            up_ = up_ref[part, :].astype(F32)
            sg = _sigmoid(gt_)
            dg_ref[part, :] = (dact * up_ * (sg * (1.0 + gt_ * (1.0 - sg)))).astype(BF)
            du_ref[part, :] = (dact * (gt_ * sg)).astype(BF)

        _col_tiles(T, RC, rows)
        for k in range(len(chunks)):
            u_ready(k)
        gd_ref[...] = lax.dot_general(act_ref[...], d_ref[...], TN_DIMS, preferred_element_type=F32).astype(BF)
        gg_ref[...] = lax.dot_general(dg_ref[...], u_ref[...], TN_DIMS, preferred_element_type=F32).astype(BF)
        gu_ref[...] = lax.dot_general(du_ref[...], u_ref[...], TN_DIMS, preferred_element_type=F32).astype(BF)

    cols = _spec((T, tn), lambda j: (0, j))
    wrow = _spec((tn, D), lambda j: (j, 0))
    big = jax.ShapeDtypeStruct((T, D_FF), BF)
    grad = jax.ShapeDtypeStruct((D_FF, D), BF)
    return pl.pallas_call(
        body, grid=(D_FF // tn,), in_specs=[ANY_SPEC, ANY_SPEC, wrow, cols, cols, cols],
        out_specs=[cols, cols, wrow, wrow, wrow], out_shape=[big, big, grad, grad, grad],
        scratch_shapes=[pltpu.VMEM((T, D), BF)] * 2 + [pltpu.SemaphoreType.DMA((T // RC,))] * 2,
        compiler_params=_params(), name="swiglu_bwd")(dh2_b, u2, w_down, gt, up, act)


def gate_up_bwd(dgt, dup, w_gate, w_up, h1, wn, dh2, after):
    def body(dg_ref, du_ref, wg_hbm, wu_hbm, h_ref, wn_ref, r_ref, _after, d_ref, db_ref, dw_ref, du_scr, wg_ref, wu_ref, wg_sems, wu_sems):
        gate_ready = _arriving(wg_hbm, wg_ref, wg_sems, FF_TILE_ROWS, FF_TILE)
        up_ready = _arriving(wu_hbm, wu_ref, wu_sems, FF_TILE_ROWS, FF_TILE)
        _zero_at_first(dw_ref)

        du_scr[...] = jnp.zeros_like(du_scr)

        def tile(c0):
            k = pl.ds(c0, FF_TILE)
            gate_ready(c0 // FF_TILE)
            up_ready(c0 // FF_TILE)
            du_scr[...] += (jnp.dot(dg_ref[:, k], wg_ref[k, :], preferred_element_type=F32)
                            + jnp.dot(du_ref[:, k], wu_ref[k, :], preferred_element_type=F32))

        _col_tiles(D_FF, FF_TILE, tile)
        for r in (0, HALF):
            dh, dw = _rms_bwd(du_scr[r:r + HALF, :], h_ref[r:r + HALF, :], wn_ref[...])
            dw_ref[...] += jnp.sum(dw, axis=0, keepdims=True)
            dh = dh + r_ref[r:r + HALF, :]
            d_ref[r:r + HALF, :] = dh
            db_ref[r:r + HALF, :] = dh.astype(BF)

    return pl.pallas_call(
        body, grid=(T // RC,),
        in_specs=[_rows_spec(D_FF), _rows_spec(D_FF), ANY_SPEC, ANY_SPEC, _rows_spec(D), _vec(D), _rows_spec(D), ANY_SPEC],
        out_specs=[_rows_spec(D), _rows_spec(D), _vec(D)],
        out_shape=[jax.ShapeDtypeStruct((T, D), F32), jax.ShapeDtypeStruct((T, D), BF), jax.ShapeDtypeStruct((1, D), F32)],
        scratch_shapes=[pltpu.VMEM((RC, D), F32)] + [pltpu.VMEM((D_FF, D), BF)] * 2 + [pltpu.SemaphoreType.DMA((len(FF_TILE_ROWS),))] * 2,
        compiler_params=_params(), name="gate_up_bwd")(dgt, dup, w_gate, w_up, h1, wn, dh2, after)


def _adamw(w, g, m, v):
    m = ADAM_B1 * m + (1.0 - ADAM_B1) * g
    v = ADAM_B2 * v + (1.0 - ADAM_B2) * (g * g)
    m_hat = m / (1.0 - ADAM_B1 ** ADAM_STEP)
    v_hat = v / (1.0 - ADAM_B2 ** ADAM_STEP)
    delta = -ADAM_LR * (m_hat / (jnp.sqrt(v_hat) + ADAM_EPS) + ADAM_WD * w)
    return delta, m, v


def adamw_shards(name, recvs, ws, ms, vs):
    n = len(ws)

    def body(*refs):
        ins, outs = refs[:4 * n], refs[4 * n:]
        for k in range(n):
            p_ref, w_ref, m_ref, v_ref = ins[k], ins[n + k], ins[2 * n + k], ins[3 * n + k]
            g = p_ref[0].astype(F32)
            for s in range(1, 8):
                g = g + p_ref[s].astype(F32)
            outs[4 * k][...] = g
            outs[4 * k + 1][...], outs[4 * k + 2][...], outs[4 * k + 3][...] = _adamw(w_ref[...], g, m_ref[...], v_ref[...])

    tiles = [_spec((w.shape[0] // 2, w.shape[1]), lambda i: (i, 0)) for w in ws]
    recv_tiles = [_spec((8, w.shape[0] // 2, w.shape[1]), lambda i: (0, i, 0)) for w in ws]
    res = pl.pallas_call(
        body, grid=(2,), in_specs=recv_tiles + tiles * 3,
        out_specs=[t for t in tiles for _ in range(4)],
        out_shape=[jax.ShapeDtypeStruct(w.shape, F32) for w in ws for _ in range(4)],
        compiler_params=_params(), name=name)(*recvs, *ws, *ms, *vs)
    return [list(res[4 * k:4 * k + 4]) for k in range(n)]


def adamw_w_in(recv, w, m, v):
    rows = 34
    per_row = D // 128

    def body(p_ref, w_ref, m_ref, v_ref, g_ref, d_ref, mo_ref, vo_ref):
        def chunk(c, carry):
            lines = pl.ds(pl.multiple_of(c * per_row * rows, 16), per_row * rows)
            g = p_ref[0, lines, :].astype(F32)
            for s in range(1, 8):
                g = g + p_ref[s, lines, :].astype(F32)
            g = g.reshape(rows, per_row, 128)
            part = pl.ds(c * rows, rows)
            g_ref[part] = g
            d_ref[part], mo_ref[part], vo_ref[part] = _adamw(w_ref[part], g, m_ref[part], v_ref[part])
            return carry

        lax.fori_loop(0, w.shape[0] // rows, chunk, 0)

    shape = jax.ShapeDtypeStruct(w.shape, F32)
    return pl.pallas_call(body, out_shape=[shape] * 4, compiler_params=_params(0), name="adamw_w_in")(recv, w, m, v)


def sum_slabs(recv):
    def body(p_ref, o_ref):
        g = p_ref[0]
        for s in range(1, 8):
            g = g + p_ref[s]
        o_ref[...] = g

    return pl.pallas_call(body, out_shape=jax.ShapeDtypeStruct(recv.shape[1:], F32), compiler_params=_params(0), name="sum_slabs")(recv)


SIMPLE = [("norm1_w", 1024), ("ssd_conv_b", 1536), ("ssd_dt_bias", 16), ("ssd_a_log", 16), ("ssd_d", 16), ("ssd_norm_w", 1024),
          ("lru_conv_b", 1024), ("lru_ba", 1024), ("lru_bx", 1024), ("lru_lambda", 1024), ("lru_norm_w", 1024), ("norm2_w", 1024),
          ("final_norm_w", 1024)]
SPECIAL = ["lru_wa", "lru_wx", "meta_tokens", "ssd_conv_w", "lru_conv_w"]
SM_ROWS = 176
SM_WA, SM_WX, SM_META, SM_SCW, SM_LCW, SM_LOSS = 14, 78, 142, 158, 166, 170


def _simple_rows():
    rows, r = {}, 0
    for name, n in SIMPLE:
        rows[name] = r
        r += -(-n // 1024)
    return rows


def adamw_small(sm, special_g, ws, ms, vs):
    rows = _simple_rows()
    ns, nx = len(SIMPLE), len(SPECIAL)

    def body(*refs):
        sm_ref = refs[0]
        gx = refs[1:1 + nx]
        wr = refs[1 + nx:1 + nx + ns + nx]
        mr = refs[1 + nx + ns + nx:1 + nx + 2 * (ns + nx)]
        vr = refs[1 + nx + 2 * (ns + nx):1 + nx + 3 * (ns + nx)]
        outs = refs[1 + nx + 3 * (ns + nx):]
        o = 0
        for k, (name, n) in enumerate(SIMPLE):
            r0 = rows[name]
            for c0 in range(0, n, 1024):
                wd = min(1024, n - c0)
                g = sm_ref[r0 + c0 // 1024:r0 + c0 // 1024 + 1, 0:wd]
                sl = (slice(None), slice(c0, c0 + wd))
                d, m2, v2 = _adamw(wr[k][sl], g, mr[k][sl], vr[k][sl])
                outs[o][sl] = g
                outs[o + 1][sl] = d
                outs[o + 2][sl] = m2
                outs[o + 3][sl] = v2
            o += 4
        for k in range(nx):
            d, m2, v2 = _adamw(wr[ns + k][...], gx[k][...], mr[ns + k][...], vr[ns + k][...])
            outs[o][...] = d
            outs[o + 1][...] = m2
            outs[o + 2][...] = v2
            o += 3

    out_shape = []
    for k in range(ns):
        out_shape += [jax.ShapeDtypeStruct(ws[k].shape, F32)] * 4
    for k in range(nx):
        out_shape += [jax.ShapeDtypeStruct(ws[ns + k].shape, F32)] * 3
    return pl.pallas_call(body, out_shape=out_shape, compiler_params=_params(0), name="adamw_small")(sm, *special_g, *ws, *ms, *vs)


def _place():
    return lax.axis_index("x"), lax.axis_index("y"), lax.axis_index("c")


def _index(px, py, pc):
    return 4 * px + 2 * py + pc


def all_gather(name, shards):
    n = len(shards)
    hbm = pl.BlockSpec(memory_space=pl.ANY)

    def body(*refs):
        ins, outs = refs[:n], refs[n:2 * n]
        send_sems, recv_sems, local_sems = refs[2 * n:]
        x, y, c = _place()
        me, sibling = (x, y, c), (x, y, 1 - c)
        chips = [(1 - x, y), (x, 1 - y), (1 - x, 1 - y)]

        def copy(i, k, block, to, src=None):
            dst = outs[i].at[_index(*block)]
            return pltpu.make_async_remote_copy(src_ref=dst if src is None else src, dst_ref=dst, send_sem=send_sems.at[7 * i + k],
                                                recv_sem=recv_sems.at[7 * i + k], device_id=to, device_id_type=MESH)

        mine = [pltpu.make_async_copy(ins[i], outs[i].at[_index(*me)], local_sems.at[i]) for i in range(n)]
        for cp in mine:
            cp.start()
        first = []
        for i in range(n):
            first += [copy(i, 1 + j, me, (*chip, c), src=ins[i]) for j, chip in enumerate(chips)]
            first.append(copy(i, 0, me, sibling, src=ins[i]))
        for cp in first:
            cp.start()
        passed = []
        for i in range(n):
            for j, chip in enumerate(chips):
                copy(i, 1 + j, (*chip, c), me).wait_recv()
                cp = copy(i, 4 + j, (*chip, c), sibling)
                cp.start()
                passed.append(cp)
        for i in range(n):
            copy(i, 0, sibling, me).wait_recv()
            for j, chip in enumerate(chips):
                copy(i, 4 + j, (*chip, 1 - c), me).wait_recv()
        for cp in first + passed:
            cp.wait_send()
        for cp in mine:
            cp.wait()

    return pl.pallas_call(
        body, in_specs=[hbm] * n, out_specs=[hbm] * n,
        out_shape=[jax.ShapeDtypeStruct((8,) + s.shape, s.dtype) for s in shards],
        scratch_shapes=[pltpu.SemaphoreType.DMA((7 * n,)), pltpu.SemaphoreType.DMA((7 * n,)), pltpu.SemaphoreType.DMA((n,))],
        name=name)(*shards)


HBM_SPEC = pl.BlockSpec(memory_space=pltpu.HBM)
SEM_SPEC = pl.BlockSpec(memory_space=pltpu.SEMAPHORE)
EFFECT = pltpu.SideEffectType.DATAFLOW_SIDE_EFFECTING


def _peers(x, y, c):
    return [((1 - x) if k & 4 else x, (1 - y) if k & 2 else y, (1 - c) if k & 1 else c) for k in range(1, 8)]


def _pieces(rows):
    for n in (4, 2):
        if rows % (16 * n) == 0:
            return [(r * (rows // n), rows // n) for r in range(n)]
    return [(0, rows)]


def _peer_copies(src, land, send_sems, recv_sems, k, peer, mine, slab_src):
    block = src.at[_index(*peer)] if slab_src else src
    return [pltpu.make_async_remote_copy(src_ref=block.at[pl.ds(r0, nr)], dst_ref=land.at[mine, pl.ds(r0, nr)], send_sem=send_sems.at[k],
                                         recv_sem=recv_sems.at[k], device_id=peer, device_id_type=MESH)
            for r0, nr in _pieces(block.shape[0])]


def copies_start(name, srcs, slab_src, after):
    n = len(srcs)
    zones = [jax.ShapeDtypeStruct(s.shape if slab_src else (8,) + s.shape, s.dtype) for s in srcs]
    afters = [] if after is None else [after]

    def body(*refs):
        ins, lands = refs[:n], refs[n:2 * n]
        first = 2 * n + len(afters)
        sends, recvs = refs[first:first + n], refs[first + n:first + 2 * n]
        token = refs[-1]
        x, y, c = _place()
        mine = _index(x, y, c)
        for i in range(n):
            per_peer = [_peer_copies(ins[i], lands[i], sends[i], recvs[i], k, peer, mine, slab_src) for k, peer in enumerate(_peers(x, y, c))]
            for piece in zip(*per_peer):
                for cp in piece:
                    cp.start()
        token[...] = jnp.zeros_like(token)

    sem = pltpu.SemaphoreType.DMA((7,))
    res = pl.pallas_call(
        body, name=name,
        out_shape=([sem] * (2 * n) + [pltpu.HBM(s.shape, s.dtype) for s in srcs] + [pltpu.HBM(z.shape, z.dtype) for z in zones]
                   + [jax.ShapeDtypeStruct((8, 128), F32)]),
        in_specs=[HBM_SPEC] * (2 * n) + [pl.BlockSpec(memory_space=pl.ANY)] * len(afters),
        out_specs=[SEM_SPEC] * (2 * n) + [HBM_SPEC] * (2 * n) + [pl.BlockSpec(memory_space=pltpu.VMEM)],
        input_output_aliases={i: 2 * n + i for i in range(2 * n)},
        compiler_params=pltpu.CompilerParams(has_side_effects=EFFECT),
    )(*[pltpu.with_memory_space_constraint(s, pltpu.HBM) for s in srcs],
      *[pltpu.with_memory_space_constraint(lax.empty(z.shape, z.dtype), pltpu.HBM) for z in zones], *afters)
    return [(res[i], res[n + i], res[2 * n + i], res[3 * n + i]) for i in range(n)], res[-1]


def copies_wait(name, started, slab_src, after):
    n = len(started)

    def body(*refs):
        ins, lands = refs[:n], refs[n:2 * n]
        sends, recvs = refs[2 * n:3 * n], refs[3 * n:4 * n]
        x, y, c = _place()
        mine = _index(x, y, c)
        for i in range(n):
            for k, peer in enumerate(_peers(x, y, c)):
                arrival = pltpu.make_async_remote_copy(src_ref=ins[i].at[mine] if slab_src else ins[i], dst_ref=lands[i].at[_index(*peer)],
                                                       send_sem=sends[i].at[k], recv_sem=recvs[i].at[k], device_id=peer, device_id_type=MESH)
                arrival.wait_send()
                arrival.wait_recv()

    srcs = [s[2] for s in started]
    lands = [s[3] for s in started]
    afters = list(after) if isinstance(after, (list, tuple)) else [after]
    res = pl.pallas_call(
        body, name=name,
        out_shape=[pltpu.HBM(s.shape, s.dtype) for s in srcs] + [pltpu.HBM(z.shape, z.dtype) for z in lands],
        in_specs=[HBM_SPEC] * (2 * n) + [SEM_SPEC] * (2 * n) + [pl.BlockSpec(memory_space=pl.ANY)] * len(afters),
        out_specs=[HBM_SPEC] * (2 * n),
        input_output_aliases={i: i for i in range(2 * n)},
        compiler_params=pltpu.CompilerParams(has_side_effects=EFFECT),
    )(*srcs, *lands, *[s[0] for s in started], *[s[1] for s in started], *afters)
    me = _index(*_place())
    own = [lax.dynamic_index_in_dim(s, me, 0, keepdims=True) if slab_src else s[None] for s in res[:n]]
    return [lax.dynamic_update_slice_in_dim(z, o, me, 0) for z, o in zip(res[n:], own)]


def _hop(src, land, send_sems, recv_sems, k, block, to):
    dst = land.at[_index(*block)]
    return pltpu.make_async_remote_copy(src_ref=dst if src is None else src, dst_ref=dst, send_sem=send_sems.at[k], recv_sem=recv_sems.at[k],
                                        device_id=to, device_id_type=MESH)


def _other_chips(x, y):
    return [(1 - x, y), (x, 1 - y), (1 - x, 1 - y)]


def gather_start(name, shards, through):
    n, m = len(shards), len(through)

    def body(*refs):
        ins, lands = refs[:n], refs[n:2 * n]
        sends, recvs = refs[2 * n + m:3 * n + m], refs[3 * n + m:4 * n + m]
        x, y, c = _place()
        for i in range(n):
            for j, chip in enumerate(_other_chips(x, y)):
                _hop(ins[i], lands[i], sends[i], recvs[i], 1 + j, (x, y, c), (*chip, c)).start()
            _hop(ins[i], lands[i], sends[i], recvs[i], 0, (x, y, c), (x, y, 1 - c)).start()

    own, passing = pltpu.SemaphoreType.DMA((4,)), pltpu.SemaphoreType.DMA((3,))
    zones = [jax.ShapeDtypeStruct((8,) + s.shape, s.dtype) for s in shards]
    res = pl.pallas_call(
        body, name=name,
        out_shape=([own] * (2 * n) + [passing] * (2 * n) + [pltpu.HBM(s.shape, s.dtype) for s in shards]
                   + [pltpu.HBM(z.shape, z.dtype) for z in zones] + [pltpu.HBM(t.shape, t.dtype) for t in through]),
        in_specs=[HBM_SPEC] * (2 * n + m),
        out_specs=[SEM_SPEC] * (4 * n) + [HBM_SPEC] * (2 * n + m),
        input_output_aliases={i: 4 * n + i for i in range(2 * n + m)},
        compiler_params=pltpu.CompilerParams(has_side_effects=EFFECT),
    )(*[pltpu.with_memory_space_constraint(s, pltpu.HBM) for s in shards],
      *[pltpu.with_memory_space_constraint(lax.empty(z.shape, z.dtype), pltpu.HBM) for z in zones],
      *[pltpu.with_memory_space_constraint(t, pltpu.HBM) for t in through])
    return [[res[4 * n + i], res[5 * n + i], res[i], res[n + i], res[2 * n + i], res[3 * n + i]] for i in range(n)], list(res[6 * n:])


def gather_stage(name, pass_on, finish, after):
    arrays = pass_on + finish
    n = len(arrays)

    def body(*refs):
        ins, lands = refs[:n], refs[n:2 * n]
        sems = [refs[(2 + q) * n:(3 + q) * n] for q in range(4)]
        x, y, c = _place()
        me, sibling = (x, y, c), (x, y, 1 - c)
        for i in range(len(pass_on)):
            send, recv, send_on, recv_on = (q[i] for q in sems)
            for j, chip in enumerate(_other_chips(x, y)):
                _hop(None, lands[i], send, recv, 1 + j, (*chip, c), me).wait_recv()
                _hop(None, lands[i], send_on, recv_on, j, (*chip, c), sibling).start()
        for i in range(len(pass_on), n):
            send, recv, send_on, recv_on = (q[i] for q in sems)
            _hop(ins[i], lands[i], send, recv, 0, sibling, me).wait_recv()
            for j, chip in enumerate(_other_chips(x, y)):
                _hop(None, lands[i], send_on, recv_on, j, (*chip, 1 - c), me).wait_recv()
            _hop(ins[i], lands[i], send, recv, 0, me, sibling).wait_send()
            for j, chip in enumerate(_other_chips(x, y)):
                _hop(ins[i], lands[i], send, recv, 1 + j, me, (*chip, c)).wait_send()
                _hop(None, lands[i], send_on, recv_on, j, (*chip, c), sibling).wait_send()
        refs[-1][...] = jnp.zeros_like(refs[-1])

    res = pl.pallas_call(
        body, name=name,
        out_shape=([pltpu.HBM(a[0].shape, a[0].dtype) for a in arrays] + [pltpu.HBM(a[1].shape, a[1].dtype) for a in arrays]
                   + [jax.ShapeDtypeStruct((8, 128), F32)]),
        in_specs=[HBM_SPEC] * (2 * n) + [SEM_SPEC] * (4 * n) + [ANY_SPEC],
        out_specs=[HBM_SPEC] * (2 * n) + [pl.BlockSpec(memory_space=pltpu.VMEM)],
        input_output_aliases={i: i for i in range(2 * n)},
        compiler_params=pltpu.CompilerParams(has_side_effects=EFFECT),
    )(*[a[0] for a in arrays], *[a[1] for a in arrays], *[a[2 + q] for q in range(4) for a in arrays], after)
    for i, a in enumerate(arrays):
        a[0], a[1] = res[i], res[n + i]
    me = _index(*_place())
    return [lax.dynamic_update_slice_in_dim(a[1], a[0][None], me, 0) for a in finish], res[-1]


WEIGHTS = ["meta_tokens", "norm1_w", "w_in", "ssd_conv_w", "ssd_conv_b", "ssd_dt_bias", "ssd_a_log", "ssd_d", "ssd_norm_w", "lru_conv_w",
           "lru_conv_b", "lru_wa", "lru_ba", "lru_wx", "lru_bx", "lru_lambda", "lru_norm_w", "w_out", "norm2_w", "w_gate", "w_up", "w_down",
           "final_norm_w"]
BIG = ["w_in", "w_out", "w_gate", "w_up", "w_down"]
COLUMN_SHARDED = ["w_in", "w_gate", "w_up"]


def _pair_blocks(w):
    w = w.reshape(8, 2, 64, 64)
    z = jnp.zeros((8, 64, 64), w.dtype)
    return jnp.concatenate([jnp.concatenate([w[:, 0], z], axis=2), jnp.concatenate([z, w[:, 1]], axis=2)], axis=1)


def _unpair_blocks(w2):
    return jnp.stack([w2[:, :64, :64], w2[:, 64:, 64:]], axis=1).reshape(16, 64, 64)


def _per_group(v):
    return jnp.pad(v.reshape(2, 1, 8), ((0, 0), (0, 0), (0, 120)))


def _pad_cols(v, n):
    return jnp.pad(v, ((0, 0), (0, n - v.shape[1])))


def local_step(x, target, meta, ssd_cw, lru_cw, w_in, fetch, send, p):
    z120 = jnp.zeros((120, D), BF)
    w_dt = jnp.concatenate([w_in[2560:2568], z120, w_in[2568:2576], z120], axis=0)
    bias2, alog2, d2 = _per_group(p["ssd_dt_bias"]), _per_group(p["ssd_a_log"]), _per_group(p["ssd_d"])
    wa2 = _pair_blocks(p["lru_wa"]).astype(BF)
    wx2 = _pair_blocks(p["lru_wx"]).astype(BF)
    lru = (lru_cw, p["lru_conv_b"], wa2, p["lru_ba"], wx2, p["lru_bx"], p["lru_lambda"])

    h0 = jnp.concatenate([jnp.zeros((NPAD, D), F32), meta, x], axis=0)
    proj, dt_raw, u1 = in_proj(h0, p["norm1_w"], w_in, w_dt)
    yn_ssd, y_pre, h_prev = ssd_fwd(proj, dt_raw, ssd_cw, p["ssd_conv_b"], bias2, alog2, d2, p["ssd_norm_w"])
    _, moved = fetch([], yn_ssd)
    hseq, a = lru_fwd(proj, *lru, moved)
    (w_out,), _ = fetch(["w_out"], hseq)
    h1, cat = out_proj(yn_ssd, proj, hseq, p["lru_norm_w"], w_out, h0)
    (w_gate, w_up), _ = fetch(["w_gate", "w_up"], h1)
    gt, up, act, u2 = gate_up(h1, p["norm2_w"], w_gate, w_up)
    (w_down,), _ = fetch(["w_down"], act)
    dh2, dh2_b, loss, d_fnw = down_loss(act, w_down, h1, target, p["final_norm_w"])

    dgt, dup, g_down, g_gate, g_up = swiglu_bwd(dh2_b, w_down, gt, up, act, u2)
    sent = send({"w_down": g_down, "w_gate": g_gate, "w_up": g_up})
    dh1, dh1_b, d_n2 = gate_up_bwd(dgt, dup, w_gate, w_up, h1, p["norm2_w"], dh2, sent)
    sent = send({"w_out": weight_grad("dw_out", cat, dh1_b)})
    dyn, dh_out, dg_b, d_lnw = out_proj_bwd(dh1_b, w_out, proj, hseq, p["lru_norm_w"], sent)

    dxl_b, d_lcw, d_lcb, dwa2, d_ba, dwx2, d_bx, d_lam = lru_bwd(dh_out, a, hseq, proj, *lru)
    dz_b, dxbc_b, ddt_b, dpar, d_snw, d_scw, d_scb = ssd_bwd(dyn, proj, dt_raw, ssd_cw, p["ssd_conv_b"], y_pre, h_prev, bias2, alog2, d2,
                                                             p["ssd_norm_w"], sent)
    sent = send({"w_in": in_weight_grad(dz_b, dg_b, dxl_b, dxbc_b, ddt_b, u1)})
    grad_x, d_meta, d_n1 = in_proj_bwd(dz_b, dg_b, dxl_b, dxbc_b, ddt_b, w_in, w_dt, h0, p["norm1_w"], dh1, sent)
    small = {"norm1_w": d_n1, "ssd_conv_b": d_scb, "ssd_dt_bias": dpar[:, 0, :8].reshape(1, 16), "ssd_a_log": dpar[:, 1, :8].reshape(1, 16),
             "ssd_d": dpar[:, 2, :8].reshape(1, 16), "ssd_norm_w": d_snw, "lru_conv_b": d_lcb, "lru_ba": d_ba, "lru_bx": d_bx,
             "lru_lambda": d_lam, "lru_norm_w": d_lnw, "norm2_w": d_n2, "final_norm_w": d_fnw,
             "lru_wa": _unpair_blocks(dwa2), "lru_wx": _unpair_blocks(dwx2), "meta_tokens": d_meta,
             "ssd_conv_w": d_scw, "lru_conv_w": d_lcw}
    return loss, grad_x, small


def _pack_small(small, loss):
    rows = [_pad_cols(small[name], -(-n // 1024) * 1024).reshape(-1, 1024) for name, n in SIMPLE]
    rows += [small["lru_wa"].reshape(64, 1024), small["lru_wx"].reshape(64, 1024), small["meta_tokens"],
             _pad_cols(small["ssd_conv_w"], 2048).reshape(8, 1024), small["lru_conv_w"], _pad_cols(loss[:, 0:1], 1024)]
    sm = jnp.concatenate(rows, axis=0)
    return jnp.pad(sm, ((0, SM_ROWS - sm.shape[0]), (0, 0)))


def _slabs(g):
    return g.reshape(8, g.shape[0] // 8, g.shape[1])


def _unslab(g):
    return g.reshape(8 * g.shape[1], g.shape[2])


def kernel(x, meta_tokens, norm1_w, w_in, ssd_conv_w, ssd_conv_b, ssd_dt_bias, ssd_a_log, ssd_d, ssd_norm_w, lru_conv_w, lru_conv_b, lru_wa, lru_ba, lru_wx, lru_bx, lru_lambda, lru_norm_w, w_out, norm2_w, w_gate, w_up, w_down, final_norm_w, loss_target, m_meta_tokens, m_norm1_w, m_w_in, m_ssd_conv_w, m_ssd_conv_b, m_ssd_dt_bias, m_ssd_a_log, m_ssd_d, m_ssd_norm_w, m_lru_conv_w, m_lru_conv_b, m_lru_wa, m_lru_ba, m_lru_wx, m_lru_bx, m_lru_lambda, m_lru_norm_w, m_w_out, m_norm2_w, m_w_gate, m_w_up, m_w_down, m_final_norm_w, v_meta_tokens, v_norm1_w, v_w_in, v_ssd_conv_w, v_ssd_conv_b, v_ssd_dt_bias, v_ssd_a_log, v_ssd_d, v_ssd_norm_w, v_lru_conv_w, v_lru_conv_b, v_lru_wa, v_lru_ba, v_lru_wx, v_lru_bx, v_lru_lambda, v_lru_norm_w, v_w_out, v_norm2_w, v_w_gate, v_w_up, v_w_down, v_final_norm_w):
    w = dict(meta_tokens=meta_tokens, norm1_w=norm1_w, w_in=w_in[0], ssd_conv_w=ssd_conv_w[0], ssd_conv_b=ssd_conv_b, ssd_dt_bias=ssd_dt_bias,
             ssd_a_log=ssd_a_log, ssd_d=ssd_d, ssd_norm_w=ssd_norm_w, lru_conv_w=lru_conv_w[0], lru_conv_b=lru_conv_b, lru_wa=lru_wa[0],
             lru_ba=lru_ba, lru_wx=lru_wx[0], lru_bx=lru_bx, lru_lambda=lru_lambda, lru_norm_w=lru_norm_w, w_out=w_out[0], norm2_w=norm2_w,
             w_gate=w_gate[0], w_up=w_up[0], w_down=w_down[0], final_norm_w=final_norm_w.reshape(1, D))
    m = dict(meta_tokens=m_meta_tokens, norm1_w=m_norm1_w, w_in=m_w_in[0], ssd_conv_w=m_ssd_conv_w[0], ssd_conv_b=m_ssd_conv_b,
             ssd_dt_bias=m_ssd_dt_bias, ssd_a_log=m_ssd_a_log, ssd_d=m_ssd_d, ssd_norm_w=m_ssd_norm_w, lru_conv_w=m_lru_conv_w[0],
             lru_conv_b=m_lru_conv_b, lru_wa=m_lru_wa[0], lru_ba=m_lru_ba, lru_wx=m_lru_wx[0], lru_bx=m_lru_bx, lru_lambda=m_lru_lambda,
             lru_norm_w=m_lru_norm_w, w_out=m_w_out[0], norm2_w=m_norm2_w, w_gate=m_w_gate[0], w_up=m_w_up[0], w_down=m_w_down[0],
             final_norm_w=m_final_norm_w.reshape(1, D))
    v = dict(meta_tokens=v_meta_tokens, norm1_w=v_norm1_w, w_in=v_w_in[0], ssd_conv_w=v_ssd_conv_w[0], ssd_conv_b=v_ssd_conv_b,
             ssd_dt_bias=v_ssd_dt_bias, ssd_a_log=v_ssd_a_log, ssd_d=v_ssd_d, ssd_norm_w=v_ssd_norm_w, lru_conv_w=v_lru_conv_w[0],
             lru_conv_b=v_lru_conv_b, lru_wa=v_lru_wa[0], lru_ba=v_lru_ba, lru_wx=v_lru_wx[0], lru_bx=v_lru_bx, lru_lambda=v_lru_lambda,
             lru_norm_w=v_lru_norm_w, w_out=v_w_out[0], norm2_w=v_norm2_w, w_gate=v_w_gate[0], w_up=v_w_up[0], w_down=v_w_down[0],
             final_norm_w=v_final_norm_w.reshape(1, D))
    shapes = dict(meta_tokens=meta_tokens.shape, norm1_w=norm1_w.shape, w_in=w_in.shape, ssd_conv_w=ssd_conv_w.shape,
                  ssd_conv_b=ssd_conv_b.shape, ssd_dt_bias=ssd_dt_bias.shape, ssd_a_log=ssd_a_log.shape, ssd_d=ssd_d.shape,
                  ssd_norm_w=ssd_norm_w.shape, lru_conv_w=lru_conv_w.shape, lru_conv_b=lru_conv_b.shape, lru_wa=lru_wa.shape,
                  lru_ba=lru_ba.shape, lru_wx=lru_wx.shape, lru_bx=lru_bx.shape, lru_lambda=lru_lambda.shape, lru_norm_w=lru_norm_w.shape,
                  w_out=w_out.shape, norm2_w=norm2_w.shape, w_gate=w_gate.shape, w_up=w_up.shape, w_down=w_down.shape,
                  final_norm_w=final_norm_w.shape)
    me = _index(*_place())
    for n in COLUMN_SHARDED:
        w[n], m[n], v[n] = w[n].T, m[n].T, v[n].T

    small_shard = jnp.concatenate([w["meta_tokens"], _pad_cols(w["ssd_conv_w"], 256).reshape(8, 128), w["lru_conv_w"],
                                   jnp.zeros((4, 128), F32)], axis=0)
    g_in, gs = all_gather("gather_w_in", [w["w_in"].astype(BF), small_shard])
    later = ["w_out", "w_gate", "w_up", "w_down"]
    started, (g_in, gs) = gather_start("gather_rest_start", [w[n].astype(BF) for n in later], [g_in, gs])
    started = dict(zip(later, started))
    passed_on = {None: ["w_out", "w_gate", "w_up"], "w_out": ["w_down"], "w_gate": [], "w_down": []}
    meta_full = gs[:, 0:16].transpose(1, 0, 2).reshape(N_META, D)
    ssd_cw = gs[:, 16:24].reshape(8, 4, 256)[:, :, :192].transpose(1, 0, 2).reshape(4, XBC)
    lru_cw = gs[:, 24:28].transpose(1, 0, 2).reshape(4, LRU_W)

    def fetch(names, after):
        first = names[0] if names else None
        got, zero = gather_stage("gather_" + (first + "_wait" if names else "pass_on"), [started[n] for n in passed_on[first]],
                                 [started[n] for n in names], after)
        return [_unslab(g) for g in got], zero

    in_flight = {}

    def send(grads):
        names = list(grads)
        st, token = copies_start("grads_" + names[0] + "_start", [grads[n] if n == "small" else _slabs(grads[n]) for n in names], True, None)
        in_flight.update(zip(names, st))
        return token

    loss, grad_x, small = local_step(x[0], loss_target[0], meta_full, ssd_cw, lru_cw, _unslab(g_in), fetch, send, w)
    send({"small": _pack_small(small, loss).reshape(8, SM_ROWS // 8, 1024)})

    out = {}
    early = ["w_down", "w_gate", "w_up", "w_out"]
    recv = dict(zip(early, copies_wait("grads_early_wait", [in_flight[n] for n in early], True, in_flight["small"][2])))
    for pair in (early[:2], early[2:]):
        done = adamw_shards("adamw_" + pair[0], [recv[n] for n in pair], [w[n] for n in pair], [m[n] for n in pair], [v[n] for n in pair])
        out.update(zip(pair, done))
    recv_in, recv_small = copies_wait("grads_late_wait", [in_flight["w_in"], in_flight["small"]], True, [out[n][0] for n in early])
    def lines(a):
        return jnp.transpose(a.reshape(D // 128, 128, IN_COLS // 8), (2, 0, 1))

    out["w_in"] = [jnp.transpose(o, (1, 2, 0)).reshape(D, IN_COLS // 8) for o in adamw_w_in(recv_in, lines(w_in), lines(m_w_in), lines(v_w_in))]
    for n in ("w_gate", "w_up"):
        out[n] = [o.T for o in out[n]]
    sm = all_gather("gather_small_grads", [sum_slabs(recv_small)])[0].reshape(SM_ROWS, 1024)
    special_g =[sm[SM_WA:SM_WA + 64].reshape(16, 64, 64), sm[SM_WX:SM_WX + 64].reshape(16, 64, 64),
                 lax.dynamic_slice(sm[SM_META:SM_META + 16], (0, 128 * me), (16, 128)),
                 lax.dynamic_slice(sm[SM_SCW:SM_SCW + 8].reshape(4, 2048), (0, 192 * me), (4, 192)),
                 lax.dynamic_slice(sm[SM_LCW:SM_LCW + 4], (0, 128 * me), (4, 128))]
    names = [n for n, _ in SIMPLE] + SPECIAL
    res = adamw_small(sm, special_g, [w[n] for n in names], [m[n] for n in names], [v[n] for n in names])
    for k, (n, _) in enumerate(SIMPLE):
        out[n] = res[4 * k:4 * k + 4]
    for k, n in enumerate(SPECIAL):
        o = 4 * len(SIMPLE) + 3 * k
        out[n] = [special_g[k]] + list(res[o:o + 3])
    loss_total = sm[SM_LOSS, 0]
    flat = [loss_total, grad_x[None]]
    for k in range(4):
        flat += [out[n][k].reshape(shapes[n]) for n in WEIGHTS]
    return tuple(flat)
```

```python
import math

import jax
import jax.numpy as jnp
from jax import lax
from jax.experimental import pallas as pl
from jax.experimental.pallas import tpu as pltpu

F32 = jnp.float32
BF = jnp.bfloat16

D = 1024
SEQ = 2048
N_META = 16
Q = 128
NPAD = 112
T = NPAD + N_META + SEQ
NCH = T // Q
RC = 544
D_FF = 2816
SSD_W = 1024
LRU_W = 1024
XBC = 1536
IN_COLS = 4624
PZ, PG, PXL, PXBC = 0, 1024, 2048, 3072
NP_IN = 4608
EPS = 1e-6
LRU_C = 8.0
VMEM_LIMIT = 56 * 1024 * 1024

ADAM_LR, ADAM_B1, ADAM_B2, ADAM_EPS, ADAM_WD, ADAM_STEP = 0.001, 0.9, 0.999, 1e-08, 0.01, 10

NT_DIMS = (((1,), (1,)), ((), ()))
TN_DIMS = (((0,), (0,)), ((), ()))
MESH = pl.DeviceIdType.MESH


def _params(n_grid=1, limit=VMEM_LIMIT):
    return pltpu.CompilerParams(dimension_semantics=("arbitrary",) * n_grid, vmem_limit_bytes=limit)


def _spec(shape, imap, single=False):
    if single:
        return pl.BlockSpec(shape, imap, pipeline_mode=pl.Buffered(1))
    return pl.BlockSpec(shape, imap)


def _sigmoid(x):
    return 0.5 * jnp.tanh(0.5 * x) + 0.5


def _sigmoid_gate(x):
    return 1.0 / (1.0 + jnp.exp(-x))


def _softplus(x):
    return jnp.maximum(x, 0.0) + jnp.log(1.0 + jnp.exp(-jnp.abs(x)))


def _rms_stats(h):
    return lax.rsqrt(jnp.mean(h * h, axis=-1, keepdims=True) + EPS)


def _rms(h, w):
    return (h * _rms_stats(h)) * w


def _rms_bwd(du, h, w):
    r = _rms_stats(h)
    n = h * r
    dn = du * w
    dh = r * (dn - n * jnp.mean(dn * n, axis=-1, keepdims=True))
    return dh, du * n


_G0 = math.sqrt(2.0 / math.pi)


def _gelu(x):
    return 0.5 * x * (1.0 + jnp.tanh(_G0 * (x + 0.044715 * (x * x * x))))


def _gelu_grad(x):
    t = jnp.tanh(_G0 * (x + 0.044715 * (x * x * x)))
    return 0.5 * (1.0 + t) + 0.5 * x * (1.0 - t * t) * (_G0 * (1.0 + 3.0 * 0.044715 * (x * x)))


def _rows(shape, r0=0):
    return lax.broadcasted_iota(jnp.int32, shape, 0) + r0


def _lanes(shape):
    return lax.broadcasted_iota(jnp.int32, shape, 1)


HALO = 8


def _fill_padded(pad_ref, x_ref):
    pad_ref[0:HALO, :] = jnp.zeros((HALO, pad_ref.shape[1]), F32)
    pad_ref[T + HALO:T + 2 * HALO, :] = jnp.zeros((HALO, pad_ref.shape[1]), F32)

    def step(c, carry):
        r0 = pl.multiple_of(c * Q, Q)
        pad_ref[pl.ds(r0 + HALO, Q), :] = x_ref[pl.ds(r0, Q), :].astype(F32)
        return carry

    lax.fori_loop(0, NCH, step, 0)


def _back(pad_ref, r0):
    win = pad_ref[pl.ds(r0, Q + HALO), :]
    return lambda s: win[HALO:, :] if s == 0 else pltpu.roll(win, s, axis=0)[HALO:, :]


def _ahead(pad_ref, r0):
    win = pad_ref[pl.ds(r0 + HALO, Q + HALO), :]
    return lambda s: win[:Q, :] if s == 0 else pltpu.roll(win, Q + HALO - s, axis=0)[:Q, :]


def _conv(back, w, b):
    y = b + w[3:4, :] * back(0)
    for k in range(3):
        y = y + w[k:k + 1, :] * back(3 - k)
    return y


def _conv_bwd_x(ahead, w):
    dx = w[3:4, :] * ahead(0)
    for k in range(3):
        dx = dx + w[k:k + 1, :] * ahead(3 - k)
    return dx


def _conv_bwd_w(dy, back):
    dws = [jnp.sum(dy * back(3 - k), axis=0, keepdims=True) for k in range(4)]
    return jnp.concatenate(dws, axis=0), jnp.sum(dy, axis=0, keepdims=True)


def _chunks(fn, unrolled=False):
    if unrolled:
        for c in range(NCH):
            fn(c * Q)
        return

    def step(c, carry):
        fn(pl.multiple_of(c * Q, Q))
        return carry

    lax.fori_loop(0, NCH, step, 0)


HALF = RC // 2


def _col_tiles(n, tn, fn):
    def step(j, carry):
        fn(pl.multiple_of(j * tn, tn))
        return carry

    lax.fori_loop(0, n // tn, step, 0)


def _rows_spec(cols, block_col=0):
    return _spec((RC, cols), lambda i: (i, block_col))


def _whole(shape):
    return _spec(shape, lambda i: tuple(0 for _ in shape), single=True)


def _vec(cols):
    return _spec((1, cols), lambda i: (0, 0))


def _zero_at_first(*refs):
    @pl.when(pl.program_id(0) == 0)
    def _():
        for r in refs:
            r[...] = jnp.zeros_like(r)


ANY_SPEC = pl.BlockSpec(memory_space=pl.ANY)


def _arriving(src, dst, sems, starts, rows):
    n, ahead = len(starts), 2
    first = pl.program_id(0) == 0

    def piece(k):
        r0 = starts[0]
        for j in range(1, n):
            r0 = jnp.where(k == j, starts[j], r0)
        at = pl.ds(pl.multiple_of(r0, 16), rows)
        return pltpu.make_async_copy(src.at[at], dst.at[at], sems.at[k])

    @pl.when(first)
    def _():
        for k in range(min(ahead, n)):
            piece(k).start()

    def ready(k):
        k = jnp.asarray(k, jnp.int32)

        @pl.when(first)
        def _():
            piece(k).wait()

            @pl.when(k + ahead < n)
            def _():
                piece(k + ahead).start()

    return ready


IN_RUNS = ((PZ, 0, 1024), (PXBC, 1024, XBC), (PG, 2576, 2048))
IN_TILE = 512
IN_TILE_ROWS = [wrow + IN_TILE * j for _, wrow, width in IN_RUNS for j in range(width // IN_TILE)]


def _in_tiles(fn):
    done = 0
    for pcol, wrow, width in IN_RUNS:
        def step(j, carry, pcol=pcol, wrow=wrow, done=done):
            fn(pl.multiple_of(pcol + j * IN_TILE, IN_TILE), pl.multiple_of(wrow + j * IN_TILE, 16), done + j)
            return carry

        lax.fori_loop(0, width // IN_TILE, step, 0)
        done += width // IN_TILE


def in_proj(h0, wn, w_t, w_dt):
    def body(h_ref, wn_ref, w_hbm, wdt_ref, o_ref, dt_ref, u_ref, w_ref, w_sems):
        ready = _arriving(w_hbm, w_ref, w_sems, IN_TILE_ROWS, IN_TILE)
        for r in (0, HALF):
            u_ref[r:r + HALF, :] = _rms(h_ref[r:r + HALF, :], wn_ref[...]).astype(BF)

        def tile(pcol, wrow, k):
            ready(k)
            o_ref[:, pl.ds(pcol, IN_TILE)] = lax.dot_general(u_ref[...], w_ref[pl.ds(wrow, IN_TILE), :], NT_DIMS,
                                                             preferred_element_type=F32).astype(BF)

        _in_tiles(tile)
        dt_ref[...] = lax.dot_general(u_ref[...], wdt_ref[...], NT_DIMS, preferred_element_type=F32)

    return pl.pallas_call(
        body, grid=(T // RC,), in_specs=[_rows_spec(D), _vec(D), ANY_SPEC, _whole((256, D))],
        out_specs=[_rows_spec(NP_IN), _rows_spec(256), _rows_spec(D)],
        out_shape=[jax.ShapeDtypeStruct((T, NP_IN), BF), jax.ShapeDtypeStruct((T, 256), F32), jax.ShapeDtypeStruct((T, D), BF)],
        scratch_shapes=[pltpu.VMEM((IN_COLS, D), BF), pltpu.SemaphoreType.DMA((len(IN_TILE_ROWS),))],
        compiler_params=_params(), name="in_proj")(h0, wn, w_t, w_dt)


def out_proj(yn_ssd, proj, hseq, lru_nw, w_out, h0):
    def body(y_ref, g_ref, h_ref, wn_ref, w_ref, r_ref, o_ref, cat_ref):
        cat_ref[:, 0:SSD_W] = y_ref[...]
        for r in (0, HALF):
            y = _gelu(g_ref[r:r + HALF, :].astype(F32)) * h_ref[r:r + HALF, :]
            cat_ref[r:r + HALF, SSD_W:] = _rms(y, wn_ref[...]).astype(BF)

        def tile(c0):
            o_ref[:, pl.ds(c0, 512)] = r_ref[:, pl.ds(c0, 512)] + jnp.dot(cat_ref[...], w_ref[:, pl.ds(c0, 512)], preferred_element_type=F32)

        _col_tiles(D, 512, tile)

    return pl.pallas_call(
        body, grid=(T // RC,),
        in_specs=[_rows_spec(SSD_W), _rows_spec(LRU_W, PG // LRU_W), _rows_spec(LRU_W), _vec(LRU_W), _whole((SSD_W + LRU_W, D)), _rows_spec(D)],
        out_specs=[_rows_spec(D), _rows_spec(SSD_W + LRU_W)],
        out_shape=[jax.ShapeDtypeStruct((T, D), F32), jax.ShapeDtypeStruct((T, SSD_W + LRU_W), BF)],
        compiler_params=_params(), name="out_proj")(yn_ssd, proj, hseq, lru_nw, w_out, h0)


def out_proj_bwd(dh1_b, w_out, proj, hseq, lru_nw, after):
    def body(d_ref, w_ref, g_ref, h_ref, wn_ref, _after, dy_ref, dh_ref, dg_ref, dw_ref, dl_scr):
        _zero_at_first(dw_ref)

        def tile(c0):
            dy_ref[:, pl.ds(c0, 512)] = lax.dot_general(d_ref[...], w_ref[pl.ds(c0, 512), :], NT_DIMS, preferred_element_type=F32)
            dl_scr[:, pl.ds(c0, 512)] = lax.dot_general(d_ref[...], w_ref[pl.ds(SSD_W + c0, 512), :], NT_DIMS, preferred_element_type=F32)

        _col_tiles(SSD_W, 512, tile)

        for r in (0, HALF):
            g = g_ref[r:r + HALF, :].astype(F32)
            h = h_ref[r:r + HALF, :]
            ge = _gelu(g)
            dy, dw = _rms_bwd(dl_scr[r:r + HALF, :], ge * h, wn_ref[...])
            dw_ref[...] += jnp.sum(dw, axis=0, keepdims=True)
            dh_ref[r:r + HALF, :] = dy * ge
            dg_ref[r:r + HALF, :] = (dy * h * _gelu_grad(g)).astype(BF)

    return pl.pallas_call(
        body, grid=(T // RC,),
        in_specs=[_rows_spec(D), _whole((SSD_W + LRU_W, D)), _rows_spec(LRU_W, PG // LRU_W), _rows_spec(LRU_W), _vec(LRU_W), ANY_SPEC],
        out_specs=[_rows_spec(SSD_W), _rows_spec(LRU_W), _rows_spec(LRU_W), _vec(LRU_W)],
        out_shape=[jax.ShapeDtypeStruct((T, SSD_W), F32), jax.ShapeDtypeStruct((T, LRU_W), F32), jax.ShapeDtypeStruct((T, LRU_W), BF),
                   jax.ShapeDtypeStruct((1, LRU_W), F32)],
        scratch_shapes=[pltpu.VMEM((RC, LRU_W), F32)],
        compiler_params=_params(), name="out_proj_bwd")(dh1_b, w_out, proj, hseq, lru_nw, after)


def in_proj_bwd(dz, dg, dxl, dxbc, ddt, w_t, w_dt, h0, wn, dh1, after):
    first = NPAD + N_META

    def body(dz_ref, dg_ref, dxl_ref, dxbc_ref, ddt_ref, w_hbm, wdt_ref, h_ref, wn_ref, r_ref, _after, gx_hbm, meta_ref, dw_ref, du_scr, o_ref, sem,
             w_ref, w_sems):
        i = pl.program_id(0)
        ready = _arriving(w_hbm, w_ref, w_sems, IN_TILE_ROWS, IN_TILE)
        _zero_at_first(dw_ref)
        du_scr[...] = jnp.dot(ddt_ref[...], wdt_ref[...], preferred_element_type=F32)
        done = 0
        for d_ref, wrow, width in ((dz_ref, 0, 1024), (dxbc_ref, 1024, XBC), (dg_ref, 2576, 1024), (dxl_ref, 3600, 1024)):
            def step(j, carry, d_ref=d_ref, wrow=wrow, done=done):
                c0 = pl.multiple_of(j * IN_TILE, IN_TILE)
                ready(done + j)
                du_scr[...] += jnp.dot(d_ref[:, pl.ds(c0, IN_TILE)], w_ref[pl.ds(pl.multiple_of(wrow + c0, 16), IN_TILE), :],
                                       preferred_element_type=F32)
                return carry

            lax.fori_loop(0, width // IN_TILE, step, 0)
            done += width // IN_TILE
        for r in (0, HALF):
            dh, dw = _rms_bwd(du_scr[r:r + HALF, :], h_ref[r:r + HALF, :], wn_ref[...])
            dw_ref[...] += jnp.sum(dw, axis=0, keepdims=True)
            o_ref[r:r + HALF, :] = dh + r_ref[r:r + HALF, :]

        @pl.when(i == 0)
        def _():
            meta_ref[...] = o_ref[NPAD:first, :]
            head = pltpu.make_async_copy(o_ref.at[pl.ds(first, RC - first)], gx_hbm.at[pl.ds(0, RC - first)], sem)
            head.start()
            head.wait()

        @pl.when(i > 0)
        def _():
            rest = pltpu.make_async_copy(o_ref, gx_hbm.at[pl.ds(pl.multiple_of(i * RC - first, 32), RC)], sem)
            rest.start()
            rest.wait()

    return pl.pallas_call(
        body, grid=(T // RC,),
        in_specs=[_rows_spec(SSD_W), _rows_spec(LRU_W), _rows_spec(LRU_W), _rows_spec(XBC), _rows_spec(256), ANY_SPEC,
                  _whole((256, D)), _rows_spec(D), _vec(D), _rows_spec(D), ANY_SPEC],
        out_specs=[ANY_SPEC, _spec((N_META, D), lambda i: (0, 0)), _vec(D)],
        out_shape=[jax.ShapeDtypeStruct((SEQ, D), F32), jax.ShapeDtypeStruct((N_META, D), F32), jax.ShapeDtypeStruct((1, D), F32)],
        scratch_shapes=[pltpu.VMEM((RC, D), F32), pltpu.VMEM((RC, D), F32), pltpu.SemaphoreType.DMA,
                        pltpu.VMEM((IN_COLS, D), BF), pltpu.SemaphoreType.DMA((len(IN_TILE_ROWS),))],
        compiler_params=_params(), name="in_proj_bwd")(dz, dg, dxl, dxbc, ddt, w_t, w_dt, h0, wn, dh1, after)


GRAD_TILE = 256


def weight_grad(name, a, u1):
    tm = GRAD_TILE

    def body(a_ref, u_ref, o_ref):
        o_ref[...] = lax.dot_general(a_ref[...], u_ref[...], TN_DIMS, preferred_element_type=F32).astype(BF)

    return pl.pallas_call(
        body, grid=(a.shape[1] // tm,),
        in_specs=[_spec((T, tm), lambda j: (0, j)), _spec((T, D), lambda j: (0, 0), single=True)],
        out_specs=_spec((tm, D), lambda j: (j, 0)),
        out_shape=jax.ShapeDtypeStruct((a.shape[1], D), BF),
        compiler_params=_params(), name=name)(a, u1)


def in_weight_grad(dz, dg, dxl, dxbc, ddt, u1):
    tm = GRAD_TILE
    per_row = D // 128
    parts = (dz, dg, dxl, dxbc, ddt)
    first_rows = (0, 2576, 3600, 1024, 2560)
    tiles = [p.shape[1] // tm for p in parts]
    starts = [sum(tiles[:k]) for k in range(len(parts))]
    last = sum(tiles) - 1
    dt_lines = 8 * per_row

    def body(*refs):
        a_refs, u_ref, o_hbm, mix_scr, stage, sems = refs[:5], refs[5], refs[6], refs[7], refs[8], refs[9]
        step = pl.program_id(0)
        slot = step % 2
        line0 = 0
        for a_ref, start, n, first in zip(a_refs, starts, tiles, first_rows):
            here = (step >= start) & (step < start + n)
            line0 = jnp.where(here, per_row * (first + tm * (step - start)), line0)

            @pl.when(here)
            def _(a_ref=a_ref):
                res = lax.dot_general(a_ref[...], u_ref[...], TN_DIMS, preferred_element_type=F32)
                for q in range(per_row):
                    mix_scr[pl.ds(q, tm, stride=per_row), :] = res[:, 128 * q:128 * q + 128]

        def tile_copy(of_slot, to):
            return pltpu.make_async_copy(stage.at[of_slot], o_hbm.at[pl.ds(to, per_row * tm)], sems.at[of_slot])

        @pl.when(step >= 2)
        def _():
            tile_copy(slot, 0).wait()

        stage[slot] = mix_scr[...].astype(BF)

        @pl.when(step < last)
        def _():
            tile_copy(slot, pl.multiple_of(line0, 128)).start()

        @pl.when(step == last)
        def _():
            halves = [pltpu.make_async_copy(stage.at[slot, pl.ds(128 * per_row * k, dt_lines)],
                                            o_hbm.at[pl.ds(per_row * (first_rows[-1] + 8 * k), dt_lines)], sems.at[2 + k]) for k in range(2)]
            for cp in halves:
                cp.start()
            tile_copy(1 - slot, 0).wait()
            for cp in halves:
                cp.wait()

    def tile_of(start, n):
        return lambda j: (0, jnp.clip(j - start, 0, n - 1))

    return pl.pallas_call(
        body, grid=(last + 1,),
        in_specs=[_spec((T, tm), tile_of(s, n)) for s, n in zip(starts, tiles)] + [_spec((T, D), lambda j: (0, 0), single=True)],
        out_specs=pl.BlockSpec(memory_space=pl.ANY),
        out_shape=jax.ShapeDtypeStruct((per_row * IN_COLS, 128), BF),
        scratch_shapes=[pltpu.VMEM((per_row * tm, 128), F32), pltpu.VMEM((2, per_row * tm, 128), BF), pltpu.SemaphoreType.DMA((4,))],
        compiler_params=_params(), name="dw_in")(*parts, u1)


def _ssd_chunk_common(row0, dt_ref, b_ref, c_ref, bias, a_neg):
    shape = (Q, Q)
    lane = _lanes(shape)
    sub = _rows(shape)
    live = (_rows(shape, row0) >= NPAD) & (lane < 8)
    dtr = dt_ref[:, :]
    dt = jnp.where(live, _softplus(dtr + bias), 0.0)
    d_a = dt * a_neg
    tri = (sub >= lane).astype(F32)
    cs = jnp.dot(tri, d_a, precision=lax.Precision.HIGHEST, preferred_element_type=F32)
    cs_t = cs.T
    b_f = b_ref[:, :]
    bc = b_f.astype(BF)
    cc = c_ref[:, :].astype(BF)
    cb = lax.dot_general(cc, bc, NT_DIMS, preferred_element_type=F32)
    cs_last = cs[Q - 1:Q, :]
    return dict(lane=lane, sub=sub, live=live, dtr=dtr, dt=dt, cs=cs, cs_t=cs_t, bc=bc, cc=cc, cb=cb, bc_t=b_f.T.astype(BF),
                ecs=jnp.exp(cs), dsm=jnp.exp(cs_last - cs), gam=jnp.exp(cs_last))


def _pair(lane_even, mat, j):
    return jnp.where(lane_even, mat[:, j:j + 1], mat[:, j + 1:j + 2])


def _pair_row(lane_even, mat, j):
    return jnp.where(lane_even[0:1, :], mat[:, j:j + 1], mat[:, j + 1:j + 2])


def _head_decay(cm, j):
    seg = cm["cs"][:, j:j + 1] - cm["cs_t"][j:j + 1, :]
    return jnp.exp(jnp.where(cm["sub"] >= cm["lane"], seg, -jnp.inf))


def _head_decay_t(cm, j):
    seg = cm["cs_t"][j:j + 1, :] - cm["cs"][:, j:j + 1]
    return jnp.exp(jnp.where(cm["lane"] >= cm["sub"], seg, -jnp.inf))


def _conv_window(raw_ref, halo_ref, pad_scr):
    pad_scr[0:HALO, :] = halo_ref[...].astype(F32)[halo_ref.shape[0] - HALO:, :]
    pad_scr[HALO:HALO + Q, :] = raw_ref[...].astype(F32)
    win = pad_scr[...]
    return lambda s: win[HALO:, :] if s == 0 else pltpu.roll(win, s, axis=0)[HALO:, :]


def _xbc_cols(g):
    return slice(512 * g, 512 * g + 512), slice(SSD_W + 128 * g, SSD_W + 128 * g + 128), slice(SSD_W + 256 + 128 * g, SSD_W + 384 + 128 * g)


def ssd_fwd(proj, dt_raw, conv_w, conv_b, dt_bias2, a_log2, d2, norm_w):
    def body(raw_ref, halo_ref, dt_all, z_all, cw_ref, cb_ref, bias_all, alog_all, d_all, nw_all, yn_all, y_all, hp_all,
             h_all, pad_scr, act_scr):
        @pl.when(pl.program_id(0) == 0)
        def _():
            h_all[...] = jnp.zeros_like(h_all)

        pre = _conv(_conv_window(raw_ref, halo_ref, pad_scr), cw_ref[...], cb_ref[...])
        act_scr[...] = pre * _sigmoid(pre)
        for g in range(2):
            wide, thin = slice(512 * g, 512 * g + 512), slice(128 * g, 128 * g + 128)
            xs, bs, cs = _xbc_cols(g)
            group(act_scr.at[:, xs], act_scr.at[:, bs], act_scr.at[:, cs], dt_all.at[:, thin], z_all.at[:, wide], bias_all.at[g],
                  alog_all.at[g], d_all.at[g], nw_all.at[:, wide], yn_all.at[:, wide], y_all.at[:, wide], hp_all.at[g, 0], h_all.at[g])

    def group(x_ref, b_ref, c_ref, dt_ref, z_ref, bias_ref, alog_ref, d_ref, nw_ref, yn_ref, y_ref, hp_ref, h_scr):
        bias = bias_ref[...]
        a_neg = -jnp.exp(alog_ref[...])
        dsk = d_ref[...]
        cm = _ssd_chunk_common(pl.program_id(0) * Q, dt_ref, b_ref, c_ref, bias, a_neg)
        lane_even = cm["lane"] < 64
        for p in range(4):
            je, jo = 2 * p, 2 * p + 1
            xp = x_ref[:, 128 * p:128 * p + 128]
            xdt = xp * _pair(lane_even, cm["dt"], je)
            xdt_b = xdt.astype(BF)
            m_e = (cm["cb"] * _head_decay(cm, je)).astype(BF)
            m_o = (cm["cb"] * _head_decay(cm, jo)).astype(BF)
            zero = jnp.zeros_like(xdt_b)
            yd = (jnp.dot(m_e, jnp.where(lane_even, xdt_b, zero), preferred_element_type=F32)
                  + jnp.dot(m_o, jnp.where(lane_even, zero, xdt_b), preferred_element_type=F32))
            hp = h_scr[p]
            hp_ref[p] = hp
            yo = jnp.dot(cm["cc"], hp.astype(BF), preferred_element_type=F32) * _pair(lane_even, cm["ecs"], je)
            y_ref[:, 128 * p:128 * p + 128] = yd + yo + xp * _pair_row(lane_even, dsk, je)
            st = jnp.dot(cm["bc_t"], (xdt * _pair(lane_even, cm["dsm"], je)).astype(BF), preferred_element_type=F32)
            h_scr[p] = hp * _pair_row(lane_even, cm["gam"], je) + st
        zc = z_ref[:, :].astype(F32)
        gated = y_ref[:, :] * (zc * _sigmoid(zc))
        yn_ref[:, :] = _rms(gated, nw_ref[...]).astype(BF)

    par = _spec((2, 1, 128), lambda c: (0, 0, 0))
    wide = _spec((Q, SSD_W), lambda c: (c, 0))
    xbc = PXBC // XBC
    halo = 2 * HALO
    return pl.pallas_call(
        body, grid=(NCH,),
        in_specs=[_spec((Q, XBC), lambda c: (c, xbc)), _spec((halo, XBC), lambda c: (jnp.maximum(c * (Q // halo) - 1, 0), xbc)),
                  _spec((Q, 256), lambda c: (c, 0)), wide, _spec((4, XBC), lambda c: (0, 0)), _spec((1, XBC), lambda c: (0, 0)),
                  par, par, par, _spec((1, SSD_W), lambda c: (0, 0))],
        out_specs=[wide, wide, _spec((2, 1, 4, 128, 128), lambda c: (0, c, 0, 0, 0))],
        out_shape=[jax.ShapeDtypeStruct((T, SSD_W), BF), jax.ShapeDtypeStruct((T, SSD_W), F32),
                   jax.ShapeDtypeStruct((2, NCH, 4, 128, 128), F32)],
        scratch_shapes=[pltpu.VMEM((2, 4, 128, 128), F32), pltpu.VMEM((Q + HALO, XBC), F32), pltpu.VMEM((Q, XBC), F32)],
        compiler_params=_params(), name="ssd_fwd")(proj, proj, dt_raw, proj, conv_w, conv_b, dt_bias2, a_log2, d2, norm_w)


def ssd_bwd(dyn, proj, dt_raw, conv_w, conv_b, y_pre, h_prev, dt_bias2, a_log2, d2, norm_w, after):
    def body(dyn_all, raw_ref, halo_ref, dt_all, z_all, y_all, hp_all, cw_ref, cb_ref, bias_all, alog_all, d_all, nw_all, _after,
             dz_all, dxbc_ref, ddt_all, dpar_all, dnw_all, dcw_ref, dcb_ref, dh_all, acc_all, pad_scr, act_scr, dsilu_scr, dact_scr, dpad_scr):
        @pl.when(pl.program_id(0) == 0)
        def _():
            dh_all[...] = jnp.zeros_like(dh_all)
            acc_all[...] = jnp.zeros_like(acc_all)
            dnw_all[...] = jnp.zeros_like(dnw_all)
            dcw_ref[...] = jnp.zeros_like(dcw_ref)
            dcb_ref[...] = jnp.zeros_like(dcb_ref)
            dpad_scr[Q:Q + HALO, :] = jnp.zeros((HALO, XBC), F32)

        back = _conv_window(raw_ref, halo_ref, pad_scr)
        pre = _conv(back, cw_ref[...], cb_ref[...])
        sg = _sigmoid(pre)
        act_scr[...] = pre * sg
        dsilu_scr[...] = sg * (1.0 + pre * (1.0 - sg))
        for g in range(2):
            wide, thin = slice(512 * g, 512 * g + 512), slice(128 * g, 128 * g + 128)
            xs, bs, cs = _xbc_cols(g)
            group(dyn_all.at[:, wide], act_scr.at[:, xs], act_scr.at[:, bs], act_scr.at[:, cs], dt_all.at[:, thin], z_all.at[:, wide],
                  y_all.at[:, wide], hp_all.at[g, 0], bias_all.at[g], alog_all.at[g], d_all.at[g], nw_all.at[:, wide],
                  dz_all.at[:, wide], dact_scr.at[:, xs], dact_scr.at[:, bs], dact_scr.at[:, cs], ddt_all.at[:, thin], dpar_all.at[g],
                  dnw_all.at[:, wide], dh_all.at[g], acc_all.at[g])
        dpre = dact_scr[...] * dsilu_scr[...]
        dcw, dcb = _conv_bwd_w(dpre, back)
        dcw_ref[...] += dcw
        dcb_ref[...] += dcb
        dpad_scr[0:Q, :] = dpre
        win = dpad_scr[...]
        dxbc_ref[...] = _conv_bwd_x(lambda s: win[:Q, :] if s == 0 else pltpu.roll(win, Q + HALO - s, axis=0)[:Q, :], cw_ref[...]).astype(BF)
        dpad_scr[Q:Q + HALO, :] = dpre[0:HALO, :]

    def group(dyn_ref, x_ref, b_ref, c_ref, dt_ref, z_ref, y_ref, hp_ref, bias_ref, alog_ref, d_ref, nw_ref,
              dz_ref, dx_ref, db_ref, dc_ref, ddt_ref, dpar_ref, dnw_ref, dh_scr, acc_scr):
        ci = pl.program_id(0)
        bias = bias_ref[...]
        a_neg = -jnp.exp(alog_ref[...])
        dsk = d_ref[...]
        cm = _ssd_chunk_common((NCH - 1 - ci) * Q, dt_ref, b_ref, c_ref, bias, a_neg)
        lane, sub = cm["lane"], cm["sub"]
        lane_even = lane < 64
        cc_t = c_ref[:, :].T.astype(BF)
        cb_t = lax.dot_general(cm["bc"], cm["cc"], NT_DIMS, preferred_element_type=F32)
        zc = z_ref[:, :].astype(F32)
        yc = y_ref[:, :]
        sg = _sigmoid(zc)
        sz = zc * sg
        dgated, dnw = _rms_bwd(dyn_ref[:, :], yc * sz, nw_ref[...])
        dnw_ref[...] += jnp.sum(dnw, axis=0, keepdims=True)
        dz_ref[:, :] = (dgated * yc * (sg * (1.0 + zc * (1.0 - sg)))).astype(BF)
        dy_all = dgated * sz
        dcb = jnp.zeros((Q, Q), F32)
        dcb_t = jnp.zeros((Q, Q), F32)
        db_acc = jnp.zeros((Q, Q), F32)
        dc_acc = jnp.zeros((Q, Q), F32)
        dcs = jnp.zeros((Q, Q), F32)
        ddt = jnp.zeros((Q, Q), F32)
        for p in range(4):
            je, jo = 2 * p, 2 * p + 1
            xp = x_ref[:, 128 * p:128 * p + 128]
            dy = dy_all[:, 128 * p:128 * p + 128]
            dt_p = _pair(lane_even, cm["dt"], je)
            xdt = xp * dt_p
            xdt_b = xdt.astype(BF)
            dy_b = dy.astype(BF)
            zero = jnp.zeros_like(dy_b)
            hp = hp_ref[p]
            hp_b = hp.astype(BF)
            dh = dh_scr[p]
            dh_b = dh.astype(BF)
            acc_scr[p:p + 1, :] += jnp.sum(dy * xp, axis=0, keepdims=True)
            dxp = dy * _pair_row(lane_even, dsk, je)
            e_p = _pair(lane_even, cm["ecs"], je)
            g_p = jnp.dot(cm["cc"], hp_b, preferred_element_type=F32)
            dg_b = (dy * e_p).astype(BF)
            de = dy * g_p * e_p
            dc_acc = dc_acc + lax.dot_general(dg_b, hp_b, NT_DIMS, preferred_element_type=F32)
            dh_in = jnp.dot(cc_t, dg_b, preferred_element_type=F32)
            ds_p = _pair(lane_even, cm["dsm"], je)
            r_p = jnp.dot(cm["bc"], dh_b, preferred_element_type=F32)
            dxdt = r_p * ds_p
            tt = r_p * xdt * ds_p
            db_acc = db_acc + lax.dot_general((xdt * ds_p).astype(BF), dh_b, NT_DIMS, preferred_element_type=F32)
            dgam_m = jnp.sum(dh * hp, axis=0, keepdims=True)
            for j, even in ((je, True), (jo, False)):
                sel = lane_even if even else jnp.logical_not(lane_even)
                dy_j = jnp.where(sel, dy_b, zero)
                l_j = _head_decay(cm, j)
                l_jt = _head_decay_t(cm, j)
                m_j = cm["cb"] * l_j
                m_jt = cb_t * l_jt
                dm = lax.dot_general(dy_j, xdt_b, NT_DIMS, preferred_element_type=F32)
                dm_t = lax.dot_general(xdt_b, dy_j, NT_DIMS, preferred_element_type=F32)
                dxdt = dxdt + jnp.dot(m_jt.astype(BF), dy_j, preferred_element_type=F32)
                dcb = dcb + dm * l_j
                dcb_t = dcb_t + dm_t * l_jt
                t_j = jnp.where(sel, tt, 0.0)
                col = jnp.sum(dm * m_j - dm_t * m_jt + (jnp.where(sel, de, 0.0) - t_j), axis=1, keepdims=True)
                gam_j = cm["gam"][:, j:j + 1]
                last = (jnp.sum(jnp.sum(t_j, axis=0, keepdims=True), axis=1, keepdims=True)
                        + jnp.sum(jnp.where(sel[0:1, :], dgam_m, 0.0), axis=1, keepdims=True) * gam_j)
                col = col + jnp.where(sub[:, 0:1] == Q - 1, last, 0.0)
                dcs = dcs + jnp.where(lane == j, col, 0.0)
            dh_scr[p] = dh_in + dh * _pair_row(lane_even, cm["gam"], je)
            dx_ref[:, 128 * p:128 * p + 128] = dxp + dxdt * dt_p
            dd = dxdt * xp
            ddt = ddt + jnp.where(lane == je, jnp.sum(jnp.where(lane_even, dd, 0.0), axis=1, keepdims=True), 0.0)
            ddt = ddt + jnp.where(lane == jo, jnp.sum(jnp.where(lane_even, 0.0, dd), axis=1, keepdims=True), 0.0)
        dc_ref[:, :] = dc_acc + jnp.dot(dcb.astype(BF), cm["bc"], preferred_element_type=F32)
        db_ref[:, :] = db_acc + jnp.dot(dcb_t.astype(BF), cm["cc"], preferred_element_type=F32)
        tri_t = (sub <= lane).astype(F32)
        dd_a = jnp.dot(tri_t, dcs, precision=lax.Precision.HIGHEST, preferred_element_type=F32)
        ddt = ddt + dd_a * a_neg
        acc_scr[5:6, :] += jnp.sum(dd_a * cm["dt"], axis=0, keepdims=True)
        draw = jnp.where(cm["live"], ddt * _sigmoid_gate(cm["dtr"] + bias), 0.0)
        acc_scr[4:5, :] += jnp.sum(draw, axis=0, keepdims=True)
        ddt_ref[:, :] = draw.astype(BF)

        @pl.when(ci == NCH - 1)
        def _():
            lane1 = _lanes((1, 128))
            dd = jnp.zeros((1, 128), F32)
            for p in range(4):
                row = acc_scr[p:p + 1, :]
                dd = dd + jnp.where(lane1 == 2 * p, jnp.sum(jnp.where(lane1 < 64, row, 0.0), axis=1, keepdims=True), 0.0)
                dd = dd + jnp.where(lane1 == 2 * p + 1, jnp.sum(jnp.where(lane1 < 64, 0.0, row), axis=1, keepdims=True), 0.0)
            dpar_ref[...] = jnp.concatenate([acc_scr[4:5, :], acc_scr[5:6, :] * a_neg, dd, jnp.zeros((5, 128), F32)], axis=0)

    par = _spec((2, 1, 128), lambda c: (0, 0, 0))
    wide = _spec((Q, SSD_W), lambda c: (NCH - 1 - c, 0))
    thin = _spec((Q, 256), lambda c: (NCH - 1 - c, 0))
    vec = _spec((1, SSD_W), lambda c: (0, 0))
    xbc = PXBC // XBC
    halo = 2 * HALO
    chunk = pltpu.VMEM((Q, XBC), F32)
    padded = pltpu.VMEM((Q + HALO, XBC), F32)
    return pl.pallas_call(
        body, grid=(NCH,),
        in_specs=[wide, _spec((Q, XBC), lambda c: (NCH - 1 - c, xbc)),
                  _spec((halo, XBC), lambda c: (jnp.maximum((NCH - 1 - c) * (Q // halo) - 1, 0), xbc)), thin, wide, wide,
                  _spec((2, 1, 4, 128, 128), lambda c: (0, NCH - 1 - c, 0, 0, 0)), _spec((4, XBC), lambda c: (0, 0)),
                  _spec((1, XBC), lambda c: (0, 0)), par, par, par, vec, ANY_SPEC],
        out_specs=[wide, _spec((Q, XBC), lambda c: (NCH - 1 - c, 0)), thin, _spec((2, 8, 128), lambda c: (0, 0, 0)), vec,
                   _spec((4, XBC), lambda c: (0, 0)), _spec((1, XBC), lambda c: (0, 0))],
        out_shape=[jax.ShapeDtypeStruct((T, SSD_W), BF), jax.ShapeDtypeStruct((T, XBC), BF), jax.ShapeDtypeStruct((T, 256), BF),
                   jax.ShapeDtypeStruct((2, 8, 128), F32), jax.ShapeDtypeStruct((1, SSD_W), F32), jax.ShapeDtypeStruct((4, XBC), F32),
                   jax.ShapeDtypeStruct((1, XBC), F32)],
        scratch_shapes=[pltpu.VMEM((2, 4, 128, 128), F32), pltpu.VMEM((2, 8, 128), F32), padded, chunk, chunk, chunk, padded],
        compiler_params=_params(), name="ssd_bwd")(dyn, proj, proj, dt_raw, proj, y_pre, h_prev, conv_w, conv_b, dt_bias2, a_log2, d2, norm_w, after)


def _lru_gates(back, cw, cb, wa, ba, wx, bx, lam):
    xr = _conv(back, cw, cb)
    xr_b = xr.astype(BF)
    r = _sigmoid_gate(jnp.dot(xr_b, wa, preferred_element_type=F32) + ba)
    i = _sigmoid_gate(jnp.dot(xr_b, wx, preferred_element_type=F32) + bx)
    sp = _softplus(-lam)
    la = (-LRU_C) * r * sp
    a = jnp.exp(la)
    mult2 = -jnp.tanh(la) * (a * a + 1.0)
    return xr, xr_b, r, i, sp, a, jnp.sqrt(mult2), mult2


SEG_LEN = 68
SEGS = T // SEG_LEN


def _seg_rows(j, k, off=0):
    return pl.ds(off + j * 8 * SEG_LEN + k, 8, stride=SEG_LEN)


def _segmented_scan(mul_ref, mul_row0, add_ref, out_ref, loc_scr, prod_scr, carry_scr, reverse):
    groups = SEGS // 8
    off = mul_row0 + (1 if reverse else 0)

    def local(i, carry):
        k = SEG_LEN - 1 - i if reverse else i
        new = []
        for j in range(groups):
            h, p = carry[2 * j], carry[2 * j + 1]
            m = mul_ref[_seg_rows(j, k, off), :]
            h = m * h + add_ref[_seg_rows(j, k), :]
            p = m * p
            loc_scr[_seg_rows(j, k), :] = h
            prod_scr[_seg_rows(j, k), :] = p
            new += [h, p]
        return tuple(new)

    lax.fori_loop(0, SEG_LEN, local, (jnp.zeros((8, 128), F32), jnp.ones((8, 128), F32)) * groups)

    def chain(i, c):
        s = SEGS - 1 - i if reverse else i
        carry_scr[pl.ds(s, 1), :] = c
        edge = s * SEG_LEN + (0 if reverse else SEG_LEN - 1)
        return loc_scr[pl.ds(edge, 1), :] + prod_scr[pl.ds(edge, 1), :] * c

    lax.fori_loop(0, SEGS, chain, jnp.zeros((1, 128), F32))

    def fold(k, carry):
        for j in range(groups):
            rows = _seg_rows(j, k)
            out_ref[rows, :] = loc_scr[rows, :] + prod_scr[rows, :] * carry_scr[8 * j:8 * j + 8, :]
        return carry

    lax.fori_loop(0, SEG_LEN, fold, 0)


def lru_fwd(proj, cw, cb, wa2, ba, wx2, bx, lam, after):
    def body(x_ref, cw_ref, cb_ref, wa_ref, ba_ref, wx_ref, bx_ref, lam_ref, _after, h_ref, a_ref, xpad, u_scr, loc_scr, prod_scr, carry_scr):
        _fill_padded(xpad, x_ref)

        def chunk(r0):
            xr, _, _, i, _, a, mult, _ = _lru_gates(_back(xpad, r0), cw_ref[...], cb_ref[...], wa_ref[0], ba_ref[...], wx_ref[0], bx_ref[...],
                                                 lam_ref[...])
            a_ref[pl.ds(r0, Q), :] = a
            u_scr[pl.ds(r0, Q), :] = jnp.where(_rows(a.shape, r0) >= NPAD, mult * (i * xr), 0.0)

        _chunks(chunk, unrolled=True)
        _segmented_scan(a_ref, 0, u_scr, h_ref, loc_scr, prod_scr, carry_scr, reverse=False)

    c0 = PXL // 128
    vec = _spec((1, 128), lambda c: (0, c))
    mat = _spec((1, 128, 128), lambda c: (c, 0, 0))
    seq = pltpu.VMEM((T, 128), F32)
    return pl.pallas_call(
        body, grid=(8,),
        in_specs=[_spec((T, 128), lambda c: (0, c0 + c)), _spec((4, 128), lambda c: (0, c)), vec, mat, vec, mat, vec, vec, ANY_SPEC],
        out_specs=[_spec((T, 128), lambda c: (0, c)), _spec((T, 128), lambda c: (0, c))],
        out_shape=[jax.ShapeDtypeStruct((T, LRU_W), F32), jax.ShapeDtypeStruct((T, LRU_W), F32)],
        scratch_shapes=[pltpu.VMEM((T + 2 * HALO, 128), F32), seq, seq, seq, pltpu.VMEM((SEGS, 128), F32)],
        compiler_params=_params(), name="lru_fwd")(proj, cw, cb, wa2, ba, wx2, bx, lam, after)


def lru_bwd(dh_out, a, hseq, proj, cw, cb, wa2, ba, wx2, bx, lam):
    def body(d_ref, a_ref, h_ref, x_ref, cw_ref, cb_ref, wa_ref, ba_ref, wx_ref, bx_ref, lam_ref,
             dx_ref, dcw_ref, dcb_ref, dwa_ref, dba_ref, dwx_ref, dbx_ref, dlam_ref, xpad, hpad, dpad, dh_ref, loc_scr, prod_scr, carry_scr):
        _fill_padded(dpad, a_ref)
        _segmented_scan(dpad, HALO, d_ref, dh_ref, loc_scr, prod_scr, carry_scr, reverse=True)
        _fill_padded(xpad, x_ref)
        _fill_padded(hpad, h_ref)
        dpad[0:HALO, :] = jnp.zeros((HALO, 128), F32)
        dpad[T + HALO:T + 2 * HALO, :] = jnp.zeros((HALO, 128), F32)
        for ref in (dcw_ref, dcb_ref, dwa_ref, dba_ref, dwx_ref, dbx_ref, dlam_ref):
            ref[...] = jnp.zeros_like(ref)
        lam = lam_ref[...]

        def first(r0):
            back = _back(xpad, r0)
            xr, xr_b, r, i, sp, a, mult, mult2 = _lru_gates(back, cw_ref[...], cb_ref[...], wa_ref[0], ba_ref[...], wx_ref[0], bx_ref[...], lam)
            dh = dh_ref[pl.ds(r0, Q), :]
            da = dh * _back(hpad, r0)(1)
            du = jnp.where(_rows(dh.shape, r0) >= NPAD, dh, 0.0)
            dmult = du * (i * xr)
            di = du * (mult * xr)
            dxr = du * (mult * i)
            dla = da * a - dmult * (a * a) * lax.rsqrt(mult2)
            dr = dla * ((-LRU_C) * sp)
            dlam_ref[...] += jnp.sum(dla * ((-LRU_C) * r), axis=0, keepdims=True)
            dpr = dr * r * (1.0 - r)
            dpi = di * i * (1.0 - i)
            dba_ref[...] += jnp.sum(dpr, axis=0, keepdims=True)
            dbx_ref[...] += jnp.sum(dpi, axis=0, keepdims=True)
            dpr_b = dpr.astype(BF)
            dpi_b = dpi.astype(BF)
            dxr = (dxr + lax.dot_general(dpr_b, wa_ref[0], NT_DIMS, preferred_element_type=F32)
                   + lax.dot_general(dpi_b, wx_ref[0], NT_DIMS, preferred_element_type=F32))
            dwa_ref[0] += lax.dot_general(xr_b, dpr_b, TN_DIMS, preferred_element_type=F32)
            dwx_ref[0] += lax.dot_general(xr_b, dpi_b, TN_DIMS, preferred_element_type=F32)
            dpad[pl.ds(r0 + HALO, Q), :] = dxr
            dcw, dcb = _conv_bwd_w(dxr, back)
            dcw_ref[...] += dcw
            dcb_ref[...] += dcb

        _chunks(first, unrolled=True)
        dlam_ref[...] = -dlam_ref[...] * _sigmoid_gate(-lam)

        def second(r0):
            dx_ref[pl.ds(r0, Q), :] = _conv_bwd_x(_ahead(dpad, r0), cw_ref[...]).astype(BF)

        _chunks(second)

    c0 = PXL // 128
    vec = _spec((1, 128), lambda c: (0, c))
    mat = _spec((1, 128, 128), lambda c: (c, 0, 0))
    col = _spec((T, 128), lambda c: (0, c))
    vshape = jax.ShapeDtypeStruct((1, LRU_W), F32)
    mshape = jax.ShapeDtypeStruct((8, 128, 128), F32)
    pad = pltpu.VMEM((T + 2 * HALO, 128), F32)
    seq = pltpu.VMEM((T, 128), F32)
    return pl.pallas_call(
        body, grid=(8,),
        in_specs=[col, col, col, _spec((T, 128), lambda c: (0, c0 + c)), _spec((4, 128), lambda c: (0, c)), vec, mat, vec, mat, vec, vec],
        out_specs=[col, _spec((4, 128), lambda c: (0, c)), vec, mat, vec, mat, vec, vec],
        out_shape=[jax.ShapeDtypeStruct((T, LRU_W), BF), jax.ShapeDtypeStruct((4, LRU_W), F32), vshape, mshape, vshape, mshape, vshape, vshape],
        scratch_shapes=[pad, pad, pad, seq, seq, seq, pltpu.VMEM((SEGS, 128), F32)],
        compiler_params=_params(), name="lru_bwd")(dh_out, a, hseq, proj, cw, cb, wa2, ba, wx2, bx, lam)


FF_TILE = 256
FF_TILE_ROWS = list(range(0, D_FF, FF_TILE))


def gate_up(h1, wn, w_gate, w_up):
    def body(h_ref, wn_ref, wg_hbm, wu_hbm, gt_ref, up_ref, act_ref, u_ref, wg_ref, wu_ref, wg_sems, wu_sems):
        gate_ready = _arriving(wg_hbm, wg_ref, wg_sems, FF_TILE_ROWS, FF_TILE)
        up_ready = _arriving(wu_hbm, wu_ref, wu_sems, FF_TILE_ROWS, FF_TILE)
        for r in (0, HALF):
            u_ref[r:r + HALF, :] = _rms(h_ref[r:r + HALF, :], wn_ref[...]).astype(BF)

        def tile(c0):
            cols = pl.ds(c0, FF_TILE)
            gate_ready(c0 // FF_TILE)
            up_ready(c0 // FF_TILE)
            gt = lax.dot_general(u_ref[...], wg_ref[cols, :], NT_DIMS, preferred_element_type=F32)
            up = lax.dot_general(u_ref[...], wu_ref[cols, :], NT_DIMS, preferred_element_type=F32)
            gt_ref[:, cols] = gt.astype(BF)
            up_ref[:, cols] = up.astype(BF)
            act_ref[:, cols] = (gt * _sigmoid(gt) * up).astype(BF)

        _col_tiles(D_FF, FF_TILE, tile)

    big = jax.ShapeDtypeStruct((T, D_FF), BF)
    return pl.pallas_call(
        body, grid=(T // RC,), in_specs=[_rows_spec(D), _vec(D), ANY_SPEC, ANY_SPEC],
        out_specs=[_rows_spec(D_FF), _rows_spec(D_FF), _rows_spec(D_FF), _rows_spec(D)],
        out_shape=[big, big, big, jax.ShapeDtypeStruct((T, D), BF)],
        scratch_shapes=[pltpu.VMEM((D_FF, D), BF)] * 2 + [pltpu.SemaphoreType.DMA((len(FF_TILE_ROWS),))] * 2,
        compiler_params=_params(), name="gate_up")(h1, wn, w_gate, w_up)


def down_loss(act, w_down, h1, target, wf):
    first = NPAD + N_META

    def body(a_ref, w_ref, r_ref, t_hbm, wf_ref, d_ref, db_ref, l_ref, dw_ref, h_scr, t_ref, t_sem):
        i = pl.program_id(0)
        _zero_at_first(l_ref, dw_ref)
        head = pltpu.make_async_copy(t_hbm.at[pl.ds(0, RC - first)], t_ref.at[pl.ds(first, RC - first)], t_sem)
        rest = pltpu.make_async_copy(t_hbm.at[pl.ds(pl.multiple_of(jnp.maximum(i * RC - first, 0), 32), RC)], t_ref, t_sem)

        @pl.when(i == 0)
        def _():
            t_ref[0:first, :] = jnp.zeros((first, D), F32)
            head.start()

        @pl.when(i > 0)
        def _():
            rest.start()

        def tile(c0):
            cols = pl.ds(c0, 512)
            h_scr[:, cols] = r_ref[:, cols] + jnp.dot(a_ref[...], w_ref[:, cols], preferred_element_type=F32)

        _col_tiles(D, 512, tile)

        @pl.when(i == 0)
        def _():
            head.wait()

        @pl.when(i > 0)
        def _():
            rest.wait()

        for r in (0, HALF):
            h = h_scr[r:r + HALF, :]
            live = _rows((HALF, D), i * RC + r) >= first
            err = jnp.where(live, _rms(h, wf_ref[...]) - t_ref[r:r + HALF, :], 0.0)
            l_ref[...] += 0.5 * jnp.sum(jnp.sum(err * err, axis=1, keepdims=True) * (1.0 / D), axis=0, keepdims=True)
            dh, dw = _rms_bwd(err * (1.0 / D), h, wf_ref[...])
            dw_ref[...] += jnp.sum(dw, axis=0, keepdims=True)
            d_ref[r:r + HALF, :] = dh
            db_ref[r:r + HALF, :] = dh.astype(BF)

    return pl.pallas_call(
        body, grid=(T // RC,),
        in_specs=[_rows_spec(D_FF), _whole((D_FF, D)), _rows_spec(D), pl.BlockSpec(memory_space=pl.ANY), _vec(D)],
        out_specs=[_rows_spec(D), _rows_spec(D), _spec((1, 128), lambda i: (0, 0)), _vec(D)],
        out_shape=[jax.ShapeDtypeStruct((T, D), F32), jax.ShapeDtypeStruct((T, D), BF), jax.ShapeDtypeStruct((1, 128), F32),
                   jax.ShapeDtypeStruct((1, D), F32)],
        scratch_shapes=[pltpu.VMEM((RC, D), F32), pltpu.VMEM((RC, D), F32), pltpu.SemaphoreType.DMA],
        compiler_params=_params(), name="down_loss")(act, w_down, h1, target, wf)


def swiglu_bwd(dh2_b, w_down, gt, up, act, u2):
    tn = 256

    def body(d_hbm, u_hbm, w_ref, gt_ref, up_ref, act_ref, dg_ref, du_ref, gd_ref, gg_ref, gu_ref, d_ref, u_ref, d_sems, u_sems):
        chunks = list(range(0, T, RC))
        d_ready = _arriving(d_hbm, d_ref, d_sems, chunks, RC)
        u_ready = _arriving(u_hbm, u_ref, u_sems, chunks, RC)

        def rows(r0):
            part = pl.ds(r0, RC)
            d_ready(r0 // RC)
            dact = lax.dot_general(d_ref[part, :], w_ref[...], NT_DIMS, preferred_element_type=F32)
            gt_ = gt_ref[part, :].astype(F32)
            up_ = up_ref[part, :].astype(F32)
            sg = _sigmoid(gt_)
            dg_ref[part, :] = (dact * up_ * (sg * (1.0 + gt_ * (1.0 - sg)))).astype(BF)
            du_ref[part, :] = (dact * (gt_ * sg)).astype(BF)

        _col_tiles(T, RC, rows)
        for k in range(len(chunks)):
            u_ready(k)
        gd_ref[...] = lax.dot_general(act_ref[...], d_ref[...], TN_DIMS, preferred_element_type=F32).astype(BF)
        gg_ref[...] = lax.dot_general(dg_ref[...], u_ref[...], TN_DIMS, preferred_element_type=F32).astype(BF)
        gu_ref[...] = lax.dot_general(du_ref[...], u_ref[...], TN_DIMS, preferred_element_type=F32).astype(BF)

    cols = _spec((T, tn), lambda j: (0, j))
    wrow = _spec((tn, D), lambda j: (j, 0))
    big = jax.ShapeDtypeStruct((T, D_FF), BF)
    grad = jax.ShapeDtypeStruct((D_FF, D), BF)
    return pl.pallas_call(
        body, grid=(D_FF // tn,), in_specs=[ANY_SPEC, ANY_SPEC, wrow, cols, cols, cols],
        out_specs=[cols, cols, wrow, wrow, wrow], out_shape=[big, big, grad, grad, grad],
        scratch_shapes=[pltpu.VMEM((T, D), BF)] * 2 + [pltpu.SemaphoreType.DMA((T // RC,))] * 2,
        compiler_params=_params(), name="swiglu_bwd")(dh2_b, u2, w_down, gt, up, act)


def gate_up_bwd(dgt, dup, w_gate, w_up, h1, wn, dh2, after):
    def body(dg_ref, du_ref, wg_hbm, wu_hbm, h_ref, wn_ref, r_ref, _after, d_ref, db_ref, dw_ref, du_scr, wg_ref, wu_ref, wg_sems, wu_sems):
        gate_ready = _arriving(wg_hbm, wg_ref, wg_sems, FF_TILE_ROWS, FF_TILE)
        up_ready = _arriving(wu_hbm, wu_ref, wu_sems, FF_TILE_ROWS, FF_TILE)
        _zero_at_first(dw_ref)

        du_scr[...] = jnp.zeros_like(du_scr)

        def tile(c0):
            k = pl.ds(c0, FF_TILE)
            gate_ready(c0 // FF_TILE)
            up_ready(c0 // FF_TILE)
            du_scr[...] += (jnp.dot(dg_ref[:, k], wg_ref[k, :], preferred_element_type=F32)
                            + jnp.dot(du_ref[:, k], wu_ref[k, :], preferred_element_type=F32))

        _col_tiles(D_FF, FF_TILE, tile)
        for r in (0, HALF):
            dh, dw = _rms_bwd(du_scr[r:r + HALF, :], h_ref[r:r + HALF, :], wn_ref[...])
            dw_ref[...] += jnp.sum(dw, axis=0, keepdims=True)
            dh = dh + r_ref[r:r + HALF, :]
            d_ref[r:r + HALF, :] = dh
            db_ref[r:r + HALF, :] = dh.astype(BF)

    return pl.pallas_call(
        body, grid=(T // RC,),
        in_specs=[_rows_spec(D_FF), _rows_spec(D_FF), ANY_SPEC, ANY_SPEC, _rows_spec(D), _vec(D), _rows_spec(D), ANY_SPEC],
        out_specs=[_rows_spec(D), _rows_spec(D), _vec(D)],
        out_shape=[jax.ShapeDtypeStruct((T, D), F32), jax.ShapeDtypeStruct((T, D), BF), jax.ShapeDtypeStruct((1, D), F32)],
        scratch_shapes=[pltpu.VMEM((RC, D), F32)] + [pltpu.VMEM((D_FF, D), BF)] * 2 + [pltpu.SemaphoreType.DMA((len(FF_TILE_ROWS),))] * 2,
        compiler_params=_params(), name="gate_up_bwd")(dgt, dup, w_gate, w_up, h1, wn, dh2, after)


def _adamw(w, g, m, v):
    m = ADAM_B1 * m + (1.0 - ADAM_B1) * g
    v = ADAM_B2 * v + (1.0 - ADAM_B2) * (g * g)
    m_hat = m / (1.0 - ADAM_B1 ** ADAM_STEP)
    v_hat = v / (1.0 - ADAM_B2 ** ADAM_STEP)
    delta = -ADAM_LR * (m_hat / (jnp.sqrt(v_hat) + ADAM_EPS) + ADAM_WD * w)
    return delta, m, v


def adamw_shards(name, recvs, ws, ms, vs):
    n = len(ws)

    def body(*refs):
        ins, outs = refs[:4 * n], refs[4 * n:]
        for k in range(n):
            p_ref, w_ref, m_ref, v_ref = ins[k], ins[n + k], ins[2 * n + k], ins[3 * n + k]
            g = p_ref[0].astype(F32)
            for s in range(1, 8):
                g = g + p_ref[s].astype(F32)
            outs[4 * k][...] = g
            outs[4 * k + 1][...], outs[4 * k + 2][...], outs[4 * k + 3][...] = _adamw(w_ref[...], g, m_ref[...], v_ref[...])

    tiles = [_spec((w.shape[0] // 2, w.shape[1]), lambda i: (i, 0)) for w in ws]
    recv_tiles = [_spec((8, w.shape[0] // 2, w.shape[1]), lambda i: (0, i, 0)) for w in ws]
    res = pl.pallas_call(
        body, grid=(2,), in_specs=recv_tiles + tiles * 3,
        out_specs=[t for t in tiles for _ in range(4)],
        out_shape=[jax.ShapeDtypeStruct(w.shape, F32) for w in ws for _ in range(4)],
        compiler_params=_params(), name=name)(*recvs, *ws, *ms, *vs)
    return [list(res[4 * k:4 * k + 4]) for k in range(n)]


def adamw_w_in(recv, w, m, v):
    rows = 34
    per_row = D // 128

    def body(p_ref, w_ref, m_ref, v_ref, g_ref, d_ref, mo_ref, vo_ref):
        def chunk(c, carry):
            lines = pl.ds(pl.multiple_of(c * per_row * rows, 16), per_row * rows)
            g = p_ref[0, lines, :].astype(F32)
            for s in range(1, 8):
                g = g + p_ref[s, lines, :].astype(F32)
            g = g.reshape(rows, per_row, 128)
            part = pl.ds(c * rows, rows)
            g_ref[part] = g
            d_ref[part], mo_ref[part], vo_ref[part] = _adamw(w_ref[part], g, m_ref[part], v_ref[part])
            return carry

        lax.fori_loop(0, w.shape[0] // rows, chunk, 0)

    shape = jax.ShapeDtypeStruct(w.shape, F32)
    return pl.pallas_call(body, out_shape=[shape] * 4, compiler_params=_params(0), name="adamw_w_in")(recv, w, m, v)


def sum_slabs(recv):
    def body(p_ref, o_ref):
        g = p_ref[0]
        for s in range(1, 8):
            g = g + p_ref[s]
        o_ref[...] = g

    return pl.pallas_call(body, out_shape=jax.ShapeDtypeStruct(recv.shape[1:], F32), compiler_params=_params(0), name="sum_slabs")(recv)


SIMPLE = [("norm1_w", 1024), ("ssd_conv_b", 1536), ("ssd_dt_bias", 16), ("ssd_a_log", 16), ("ssd_d", 16), ("ssd_norm_w", 1024),
          ("lru_conv_b", 1024), ("lru_ba", 1024), ("lru_bx", 1024), ("lru_lambda", 1024), ("lru_norm_w", 1024), ("norm2_w", 1024),
          ("final_norm_w", 1024)]
SPECIAL = ["lru_wa", "lru_wx", "meta_tokens", "ssd_conv_w", "lru_conv_w"]
SM_ROWS = 176
SM_WA, SM_WX, SM_META, SM_SCW, SM_LCW, SM_LOSS = 14, 78, 142, 158, 166, 170


def _simple_rows():
    rows, r = {}, 0
    for name, n in SIMPLE:
        rows[name] = r
        r += -(-n // 1024)
    return rows


def adamw_small(sm, special_g, ws, ms, vs):
    rows = _simple_rows()
    ns, nx = len(SIMPLE), len(SPECIAL)

    def body(*refs):
        sm_ref = refs[0]
        gx = refs[1:1 + nx]
        wr = refs[1 + nx:1 + nx + ns + nx]
        mr = refs[1 + nx + ns + nx:1 + nx + 2 * (ns + nx)]
        vr = refs[1 + nx + 2 * (ns + nx):1 + nx + 3 * (ns + nx)]
        outs = refs[1 + nx + 3 * (ns + nx):]
        o = 0
        for k, (name, n) in enumerate(SIMPLE):
            r0 = rows[name]
            for c0 in range(0, n, 1024):
                wd = min(1024, n - c0)
                g = sm_ref[r0 + c0 // 1024:r0 + c0 // 1024 + 1, 0:wd]
                sl = (slice(None), slice(c0, c0 + wd))
                d, m2, v2 = _adamw(wr[k][sl], g, mr[k][sl], vr[k][sl])
                outs[o][sl] = g
                outs[o + 1][sl] = d
                outs[o + 2][sl] = m2
                outs[o + 3][sl] = v2
            o += 4
        for k in range(nx):
            d, m2, v2 = _adamw(wr[ns + k][...], gx[k][...], mr[ns + k][...], vr[ns + k][...])
            outs[o][...] = d
            outs[o + 1][...] = m2
            outs[o + 2][...] = v2
            o += 3

    out_shape = []
    for k in range(ns):
        out_shape += [jax.ShapeDtypeStruct(ws[k].shape, F32)] * 4
    for k in range(nx):
        out_shape += [jax.ShapeDtypeStruct(ws[ns + k].shape, F32)] * 3
    return pl.pallas_call(body, out_shape=out_shape, compiler_params=_params(0), name="adamw_small")(sm, *special_g, *ws, *ms, *vs)


def _place():
    return lax.axis_index("x"), lax.axis_index("y"), lax.axis_index("c")


def _index(px, py, pc):
    return 4 * px + 2 * py + pc


def all_gather(name, shards):
    n = len(shards)
    hbm = pl.BlockSpec(memory_space=pl.ANY)

    def body(*refs):
        ins, outs = refs[:n], refs[n:2 * n]
        send_sems, recv_sems, local_sems = refs[2 * n:]
        x, y, c = _place()
        me, sibling = (x, y, c), (x, y, 1 - c)
        chips = [(1 - x, y), (x, 1 - y), (1 - x, 1 - y)]

        def copy(i, k, block, to, src=None):
            dst = outs[i].at[_index(*block)]
            return pltpu.make_async_remote_copy(src_ref=dst if src is None else src, dst_ref=dst, send_sem=send_sems.at[7 * i + k],
                                                recv_sem=recv_sems.at[7 * i + k], device_id=to, device_id_type=MESH)

        mine = [pltpu.make_async_copy(ins[i], outs[i].at[_index(*me)], local_sems.at[i]) for i in range(n)]
        for cp in mine:
            cp.start()
        first = []
        for i in range(n):
            first += [copy(i, 1 + j, me, (*chip, c), src=ins[i]) for j, chip in enumerate(chips)]
            first.append(copy(i, 0, me, sibling, src=ins[i]))
        for cp in first:
            cp.start()
        passed = []
        for i in range(n):
            for j, chip in enumerate(chips):
                copy(i, 1 + j, (*chip, c), me).wait_recv()
                cp = copy(i, 4 + j, (*chip, c), sibling)
                cp.start()
                passed.append(cp)
        for i in range(n):
            copy(i, 0, sibling, me).wait_recv()
            for j, chip in enumerate(chips):
                copy(i, 4 + j, (*chip, 1 - c), me).wait_recv()
        for cp in first + passed:
            cp.wait_send()
        for cp in mine:
            cp.wait()

    return pl.pallas_call(
        body, in_specs=[hbm] * n, out_specs=[hbm] * n,
        out_shape=[jax.ShapeDtypeStruct((8,) + s.shape, s.dtype) for s in shards],
        scratch_shapes=[pltpu.SemaphoreType.DMA((7 * n,)), pltpu.SemaphoreType.DMA((7 * n,)), pltpu.SemaphoreType.DMA((n,))],
        name=name)(*shards)


HBM_SPEC = pl.BlockSpec(memory_space=pltpu.HBM)
SEM_SPEC = pl.BlockSpec(memory_space=pltpu.SEMAPHORE)
EFFECT = pltpu.SideEffectType.DATAFLOW_SIDE_EFFECTING


def _peers(x, y, c):
    return [((1 - x) if k & 4 else x, (1 - y) if k & 2 else y, (1 - c) if k & 1 else c) for k in range(1, 8)]


def _pieces(rows):
    for n in (4, 2):
        if rows % (16 * n) == 0:
            return [(r * (rows // n), rows // n) for r in range(n)]
    return [(0, rows)]


def _peer_copies(src, land, send_sems, recv_sems, k, peer, mine):
    block = src.at[_index(*peer)]
    return [pltpu.make_async_remote_copy(src_ref=block.at[pl.ds(r0, nr)], dst_ref=land.at[mine, pl.ds(r0, nr)], send_sem=send_sems.at[k],
                                         recv_sem=recv_sems.at[k], device_id=peer, device_id_type=MESH)
            for r0, nr in _pieces(block.shape[0])]


OWN = 7


def _own_copy(src, land, send_sems, mine):
    return pltpu.make_async_copy(src.at[mine], land.at[mine], send_sems.at[OWN])


def copies_start(name, srcs):
    n = len(srcs)

    def body(*refs):
        ins, lands = refs[:n], refs[n:2 * n]
        sends, recvs = refs[2 * n:3 * n], refs[3 * n:4 * n]
        token = refs[-1]
        x, y, c = _place()
        mine = _index(x, y, c)
        for i in range(n):
            per_peer = [_peer_copies(ins[i], lands[i], sends[i], recvs[i], k, peer, mine) for k, peer in enumerate(_peers(x, y, c))]
            for piece in zip(*per_peer):
                for cp in piece:
                    cp.start()
            _own_copy(ins[i], lands[i], sends[i], mine).start()
        token[...] = jnp.zeros_like(token)

    res = pl.pallas_call(
        body, name=name,
        out_shape=([pltpu.SemaphoreType.DMA((8,))] * n + [pltpu.SemaphoreType.DMA((7,))] * n + [pltpu.HBM(s.shape, s.dtype) for s in srcs] * 2
                   + [jax.ShapeDtypeStruct((8, 128), F32)]),
        in_specs=[HBM_SPEC] * (2 * n),
        out_specs=[SEM_SPEC] * (2 * n) + [HBM_SPEC] * (2 * n) + [pl.BlockSpec(memory_space=pltpu.VMEM)],
        input_output_aliases={i: 2 * n + i for i in range(2 * n)},
        compiler_params=pltpu.CompilerParams(has_side_effects=EFFECT),
    )(*[pltpu.with_memory_space_constraint(s, pltpu.HBM) for s in srcs],
      *[pltpu.with_memory_space_constraint(lax.empty(s.shape, s.dtype), pltpu.HBM) for s in srcs])
    return [(res[i], res[n + i], res[2 * n + i], res[3 * n + i]) for i in range(n)], res[-1]


def copies_wait(name, started, after):
    n = len(started)

    def body(*refs):
        ins, lands = refs[:n], refs[n:2 * n]
        sends, recvs = refs[2 * n:3 * n], refs[3 * n:4 * n]
        x, y, c = _place()
        mine = _index(x, y, c)
        for i in range(n):
            for k, peer in enumerate(_peers(x, y, c)):
                arrival = pltpu.make_async_remote_copy(src_ref=ins[i].at[mine], dst_ref=lands[i].at[_index(*peer)],
                                                       send_sem=sends[i].at[k], recv_sem=recvs[i].at[k], device_id=peer, device_id_type=MESH)
                arrival.wait_send()
                arrival.wait_recv()
            _own_copy(ins[i], lands[i], sends[i], mine).wait()

    srcs = [s[2] for s in started]
    lands = [s[3] for s in started]
    res = pl.pallas_call(
        body, name=name,
        out_shape=[pltpu.HBM(s.shape, s.dtype) for s in srcs] + [pltpu.HBM(z.shape, z.dtype) for z in lands],
        in_specs=[HBM_SPEC] * (2 * n) + [SEM_SPEC] * (2 * n) + [ANY_SPEC] * len(after),
        out_specs=[HBM_SPEC] * (2 * n),
        input_output_aliases={i: i for i in range(2 * n)},
        compiler_params=pltpu.CompilerParams(has_side_effects=EFFECT),
    )(*srcs, *lands, *[s[0] for s in started], *[s[1] for s in started], *after)
    return list(res[n:])


def _hop(src, land, send_sems, recv_sems, k, block, to):
    dst = land.at[_index(*block)]
    return pltpu.make_async_remote_copy(src_ref=dst if src is None else src, dst_ref=dst, send_sem=send_sems.at[k], recv_sem=recv_sems.at[k],
                                        device_id=to, device_id_type=MESH)


def _own_block(src, land, send_sems, mine):
    return pltpu.make_async_copy(src, land.at[mine], send_sems.at[4])


def _other_chips(x, y):
    return [(1 - x, y), (x, 1 - y), (1 - x, 1 - y)]


def gather_start(name, shards, through):
    n, m = len(shards), len(through)

    def body(*refs):
        ins, lands = refs[:n], refs[n:2 * n]
        sends, recvs = refs[2 * n + m:3 * n + m], refs[3 * n + m:4 * n + m]
        x, y, c = _place()
        for i in range(n):
            for j, chip in enumerate(_other_chips(x, y)):
                _hop(ins[i], lands[i], sends[i], recvs[i], 1 + j, (x, y, c), (*chip, c)).start()
            _hop(ins[i], lands[i], sends[i], recvs[i], 0, (x, y, c), (x, y, 1 - c)).start()
            _own_block(ins[i], lands[i], sends[i], _index(x, y, c)).start()

    own, passing = pltpu.SemaphoreType.DMA((5,)), pltpu.SemaphoreType.DMA((3,))
    zones = [jax.ShapeDtypeStruct((8,) + s.shape, s.dtype) for s in shards]
    res = pl.pallas_call(
        body, name=name,
        out_shape=([own] * (2 * n) + [passing] * (2 * n) + [pltpu.HBM(s.shape, s.dtype) for s in shards]
                   + [pltpu.HBM(z.shape, z.dtype) for z in zones] + [pltpu.HBM(t.shape, t.dtype) for t in through]),
        in_specs=[HBM_SPEC] * (2 * n + m),
        out_specs=[SEM_SPEC] * (4 * n) + [HBM_SPEC] * (2 * n + m),
        input_output_aliases={i: 4 * n + i for i in range(2 * n + m)},
        compiler_params=pltpu.CompilerParams(has_side_effects=EFFECT),
    )(*[pltpu.with_memory_space_constraint(s, pltpu.HBM) for s in shards],
      *[pltpu.with_memory_space_constraint(lax.empty(z.shape, z.dtype), pltpu.HBM) for z in zones],
      *[pltpu.with_memory_space_constraint(t, pltpu.HBM) for t in through])
    return [[res[4 * n + i], res[5 * n + i], res[i], res[n + i], res[2 * n + i], res[3 * n + i]] for i in range(n)], list(res[6 * n:])


def gather_stage(name, pass_on, finish, after):
    arrays = pass_on + finish
    n = len(arrays)

    def body(*refs):
        ins, lands = refs[:n], refs[n:2 * n]
        sems = [refs[(2 + q) * n:(3 + q) * n] for q in range(4)]
        x, y, c = _place()
        me, sibling = (x, y, c), (x, y, 1 - c)
        for i in range(len(pass_on)):
            send, recv, send_on, recv_on = (q[i] for q in sems)
            for j, chip in enumerate(_other_chips(x, y)):
                _hop(None, lands[i], send, recv, 1 + j, (*chip, c), me).wait_recv()
                _hop(None, lands[i], send_on, recv_on, j, (*chip, c), sibling).start()
        for i in range(len(pass_on), n):
            send, recv, send_on, recv_on = (q[i] for q in sems)
            _hop(ins[i], lands[i], send, recv, 0, sibling, me).wait_recv()
            for j, chip in enumerate(_other_chips(x, y)):
                _hop(None, lands[i], send_on, recv_on, j, (*chip, 1 - c), me).wait_recv()
            _hop(ins[i], lands[i], send, recv, 0, me, sibling).wait_send()
            _own_block(ins[i], lands[i], send, _index(*me)).wait()
            for j, chip in enumerate(_other_chips(x, y)):
                _hop(ins[i], lands[i], send, recv, 1 + j, me, (*chip, c)).wait_send()
                _hop(None, lands[i], send_on, recv_on, j, (*chip, c), sibling).wait_send()
        refs[-1][...] = jnp.zeros_like(refs[-1])

    res = pl.pallas_call(
        body, name=name,
        out_shape=([pltpu.HBM(a[0].shape, a[0].dtype) for a in arrays] + [pltpu.HBM(a[1].shape, a[1].dtype) for a in arrays]
                   + [jax.ShapeDtypeStruct((8, 128), F32)]),
        in_specs=[HBM_SPEC] * (2 * n) + [SEM_SPEC] * (4 * n) + [ANY_SPEC],
        out_specs=[HBM_SPEC] * (2 * n) + [pl.BlockSpec(memory_space=pltpu.VMEM)],
        input_output_aliases={i: i for i in range(2 * n)},
        compiler_params=pltpu.CompilerParams(has_side_effects=EFFECT),
    )(*[a[0] for a in arrays], *[a[1] for a in arrays], *[a[2 + q] for q in range(4) for a in arrays], after)
    for i, a in enumerate(arrays):
        a[0], a[1] = res[i], res[n + i]
    return [a[1] for a in finish], res[-1]


WEIGHTS = ["meta_tokens", "norm1_w", "w_in", "ssd_conv_w", "ssd_conv_b", "ssd_dt_bias", "ssd_a_log", "ssd_d", "ssd_norm_w", "lru_conv_w",
           "lru_conv_b", "lru_wa", "lru_ba", "lru_wx", "lru_bx", "lru_lambda", "lru_norm_w", "w_out", "norm2_w", "w_gate", "w_up", "w_down",
           "final_norm_w"]
BIG = ["w_in", "w_out", "w_gate", "w_up", "w_down"]
COLUMN_SHARDED = ["w_in", "w_gate", "w_up"]


def _pair_blocks(w):
    w = w.reshape(8, 2, 64, 64)
    z = jnp.zeros((8, 64, 64), w.dtype)
    return jnp.concatenate([jnp.concatenate([w[:, 0], z], axis=2), jnp.concatenate([z, w[:, 1]], axis=2)], axis=1)


def _unpair_blocks(w2):
    return jnp.stack([w2[:, :64, :64], w2[:, 64:, 64:]], axis=1).reshape(16, 64, 64)


def _per_group(v):
    return jnp.pad(v.reshape(2, 1, 8), ((0, 0), (0, 0), (0, 120)))


def _pad_cols(v, n):
    return jnp.pad(v, ((0, 0), (0, n - v.shape[1])))


def local_step(x, target, meta, ssd_cw, lru_cw, w_in, fetch, send, p):
    z120 = jnp.zeros((120, D), BF)
    w_dt = jnp.concatenate([w_in[2560:2568], z120, w_in[2568:2576], z120], axis=0)
    bias2, alog2, d2 = _per_group(p["ssd_dt_bias"]), _per_group(p["ssd_a_log"]), _per_group(p["ssd_d"])
    wa2 = _pair_blocks(p["lru_wa"]).astype(BF)
    wx2 = _pair_blocks(p["lru_wx"]).astype(BF)
    lru = (lru_cw, p["lru_conv_b"], wa2, p["lru_ba"], wx2, p["lru_bx"], p["lru_lambda"])

    h0 = jnp.concatenate([jnp.zeros((NPAD, D), F32), meta, x], axis=0)
    proj, dt_raw, u1 = in_proj(h0, p["norm1_w"], w_in, w_dt)
    yn_ssd, y_pre, h_prev = ssd_fwd(proj, dt_raw, ssd_cw, p["ssd_conv_b"], bias2, alog2, d2, p["ssd_norm_w"])
    _, moved = fetch([], yn_ssd)
    hseq, a = lru_fwd(proj, *lru, moved)
    (w_out,), _ = fetch(["w_out"], hseq)
    h1, cat = out_proj(yn_ssd, proj, hseq, p["lru_norm_w"], w_out, h0)
    (w_gate, w_up), _ = fetch(["w_gate", "w_up"], h1)
    gt, up, act, u2 = gate_up(h1, p["norm2_w"], w_gate, w_up)
    (w_down,), _ = fetch(["w_down"], act)
    dh2, dh2_b, loss, d_fnw = down_loss(act, w_down, h1, target, p["final_norm_w"])

    dgt, dup, g_down, g_gate, g_up = swiglu_bwd(dh2_b, w_down, gt, up, act, u2)
    sent = send({"w_down": g_down, "w_gate": g_gate, "w_up": g_up})
    dh1, dh1_b, d_n2 = gate_up_bwd(dgt, dup, w_gate, w_up, h1, p["norm2_w"], dh2, sent)
    sent = send({"w_out": weight_grad("dw_out", cat, dh1_b)})
    dyn, dh_out, dg_b, d_lnw = out_proj_bwd(dh1_b, w_out, proj, hseq, p["lru_norm_w"], sent)

    dxl_b, d_lcw, d_lcb, dwa2, d_ba, dwx2, d_bx, d_lam = lru_bwd(dh_out, a, hseq, proj, *lru)
    dz_b, dxbc_b, ddt_b, dpar, d_snw, d_scw, d_scb = ssd_bwd(dyn, proj, dt_raw, ssd_cw, p["ssd_conv_b"], y_pre, h_prev, bias2, alog2, d2,
                                                             p["ssd_norm_w"], sent)
    sent = send({"w_in": in_weight_grad(dz_b, dg_b, dxl_b, dxbc_b, ddt_b, u1)})
    grad_x, d_meta, d_n1 = in_proj_bwd(dz_b, dg_b, dxl_b, dxbc_b, ddt_b, w_in, w_dt, h0, p["norm1_w"], dh1, sent)
    small = {"norm1_w": d_n1, "ssd_conv_b": d_scb, "ssd_dt_bias": dpar[:, 0, :8].reshape(1, 16), "ssd_a_log": dpar[:, 1, :8].reshape(1, 16),
             "ssd_d": dpar[:, 2, :8].reshape(1, 16), "ssd_norm_w": d_snw, "lru_conv_b": d_lcb, "lru_ba": d_ba, "lru_bx": d_bx,
             "lru_lambda": d_lam, "lru_norm_w": d_lnw, "norm2_w": d_n2, "final_norm_w": d_fnw,
             "lru_wa": _unpair_blocks(dwa2), "lru_wx": _unpair_blocks(dwx2), "meta_tokens": d_meta,
             "ssd_conv_w": d_scw, "lru_conv_w": d_lcw}
    return loss, grad_x, small


def _pack_small(small, loss):
    rows = [_pad_cols(small[name], -(-n // 1024) * 1024).reshape(-1, 1024) for name, n in SIMPLE]
    rows += [small["lru_wa"].reshape(64, 1024), small["lru_wx"].reshape(64, 1024), small["meta_tokens"],
             _pad_cols(small["ssd_conv_w"], 2048).reshape(8, 1024), small["lru_conv_w"], _pad_cols(loss[:, 0:1], 1024)]
    sm = jnp.concatenate(rows, axis=0)
    return jnp.pad(sm, ((0, SM_ROWS - sm.shape[0]), (0, 0)))


def _slabs(g):
    return g.reshape(8, g.shape[0] // 8, g.shape[1])


def _unslab(g):
    return g.reshape(8 * g.shape[1], g.shape[2])


def kernel(x, meta_tokens, norm1_w, w_in, ssd_conv_w, ssd_conv_b, ssd_dt_bias, ssd_a_log, ssd_d, ssd_norm_w, lru_conv_w, lru_conv_b, lru_wa, lru_ba, lru_wx, lru_bx, lru_lambda, lru_norm_w, w_out, norm2_w, w_gate, w_up, w_down, final_norm_w, loss_target, m_meta_tokens, m_norm1_w, m_w_in, m_ssd_conv_w, m_ssd_conv_b, m_ssd_dt_bias, m_ssd_a_log, m_ssd_d, m_ssd_norm_w, m_lru_conv_w, m_lru_conv_b, m_lru_wa, m_lru_ba, m_lru_wx, m_lru_bx, m_lru_lambda, m_lru_norm_w, m_w_out, m_norm2_w, m_w_gate, m_w_up, m_w_down, m_final_norm_w, v_meta_tokens, v_norm1_w, v_w_in, v_ssd_conv_w, v_ssd_conv_b, v_ssd_dt_bias, v_ssd_a_log, v_ssd_d, v_ssd_norm_w, v_lru_conv_w, v_lru_conv_b, v_lru_wa, v_lru_ba, v_lru_wx, v_lru_bx, v_lru_lambda, v_lru_norm_w, v_w_out, v_norm2_w, v_w_gate, v_w_up, v_w_down, v_final_norm_w):
    w = dict(meta_tokens=meta_tokens, norm1_w=norm1_w, w_in=w_in[0], ssd_conv_w=ssd_conv_w[0], ssd_conv_b=ssd_conv_b, ssd_dt_bias=ssd_dt_bias,
             ssd_a_log=ssd_a_log, ssd_d=ssd_d, ssd_norm_w=ssd_norm_w, lru_conv_w=lru_conv_w[0], lru_conv_b=lru_conv_b, lru_wa=lru_wa[0],
             lru_ba=lru_ba, lru_wx=lru_wx[0], lru_bx=lru_bx, lru_lambda=lru_lambda, lru_norm_w=lru_norm_w, w_out=w_out[0], norm2_w=norm2_w,
             w_gate=w_gate[0], w_up=w_up[0], w_down=w_down[0], final_norm_w=final_norm_w.reshape(1, D))
    m = dict(meta_tokens=m_meta_tokens, norm1_w=m_norm1_w, w_in=m_w_in[0], ssd_conv_w=m_ssd_conv_w[0], ssd_conv_b=m_ssd_conv_b,
             ssd_dt_bias=m_ssd_dt_bias, ssd_a_log=m_ssd_a_log, ssd_d=m_ssd_d, ssd_norm_w=m_ssd_norm_w, lru_conv_w=m_lru_conv_w[0],
             lru_conv_b=m_lru_conv_b, lru_wa=m_lru_wa[0], lru_ba=m_lru_ba, lru_wx=m_lru_wx[0], lru_bx=m_lru_bx, lru_lambda=m_lru_lambda,
             lru_norm_w=m_lru_norm_w, w_out=m_w_out[0], norm2_w=m_norm2_w, w_gate=m_w_gate[0], w_up=m_w_up[0], w_down=m_w_down[0],
             final_norm_w=m_final_norm_w.reshape(1, D))
    v = dict(meta_tokens=v_meta_tokens, norm1_w=v_norm1_w, w_in=v_w_in[0], ssd_conv_w=v_ssd_conv_w[0], ssd_conv_b=v_ssd_conv_b,
             ssd_dt_bias=v_ssd_dt_bias, ssd_a_log=v_ssd_a_log, ssd_d=v_ssd_d, ssd_norm_w=v_ssd_norm_w, lru_conv_w=v_lru_conv_w[0],
             lru_conv_b=v_lru_conv_b, lru_wa=v_lru_wa[0], lru_ba=v_lru_ba, lru_wx=v_lru_wx[0], lru_bx=v_lru_bx, lru_lambda=v_lru_lambda,
             lru_norm_w=v_lru_norm_w, w_out=v_w_out[0], norm2_w=v_norm2_w, w_gate=v_w_gate[0], w_up=v_w_up[0], w_down=v_w_down[0],
             final_norm_w=v_final_norm_w.reshape(1, D))
    shapes = dict(meta_tokens=meta_tokens.shape, norm1_w=norm1_w.shape, w_in=w_in.shape, ssd_conv_w=ssd_conv_w.shape,
                  ssd_conv_b=ssd_conv_b.shape, ssd_dt_bias=ssd_dt_bias.shape, ssd_a_log=ssd_a_log.shape, ssd_d=ssd_d.shape,
                  ssd_norm_w=ssd_norm_w.shape, lru_conv_w=lru_conv_w.shape, lru_conv_b=lru_conv_b.shape, lru_wa=lru_wa.shape,
                  lru_ba=lru_ba.shape, lru_wx=lru_wx.shape, lru_bx=lru_bx.shape, lru_lambda=lru_lambda.shape, lru_norm_w=lru_norm_w.shape,
                  w_out=w_out.shape, norm2_w=norm2_w.shape, w_gate=w_gate.shape, w_up=w_up.shape, w_down=w_down.shape,
                  final_norm_w=final_norm_w.shape)
    me = _index(*_place())
    for n in COLUMN_SHARDED:
        w[n], m[n], v[n] = w[n].T, m[n].T, v[n].T

    small_shard = jnp.concatenate([w["meta_tokens"], _pad_cols(w["ssd_conv_w"], 256).reshape(8, 128), w["lru_conv_w"],
                                   jnp.zeros((4, 128), F32)], axis=0)
    g_in, gs = all_gather("gather_w_in", [w["w_in"].astype(BF), small_shard])
    later = ["w_out", "w_gate", "w_up", "w_down"]
    started, (g_in, gs) = gather_start("gather_rest_start", [w[n].astype(BF) for n in later], [g_in, gs])
    started = dict(zip(later, started))
    passed_on = {None: ["w_out", "w_gate", "w_up"], "w_out": ["w_down"], "w_gate": [], "w_down": []}
    meta_full = gs[:, 0:16].transpose(1, 0, 2).reshape(N_META, D)
    ssd_cw = gs[:, 16:24].reshape(8, 4, 256)[:, :, :192].transpose(1, 0, 2).reshape(4, XBC)
    lru_cw = gs[:, 24:28].transpose(1, 0, 2).reshape(4, LRU_W)

    def fetch(names, after):
        first = names[0] if names else None
        got, zero = gather_stage("gather_" + (first + "_wait" if names else "pass_on"), [started[n] for n in passed_on[first]],
                                 [started[n] for n in names], after)
        return [_unslab(g) for g in got], zero

    in_flight = {}

    def send(grads):
        names = list(grads)
        st, token = copies_start("grads_" + names[0] + "_start", [grads[n] if n == "small" else _slabs(grads[n]) for n in names])
        in_flight.update(zip(names, st))
        return token

    loss, grad_x, small = local_step(x[0], loss_target[0], meta_full, ssd_cw, lru_cw, _unslab(g_in), fetch, send, w)
    send({"small": _pack_small(small, loss).reshape(8, SM_ROWS // 8, 1024)})

    out = {}
    early = ["w_down", "w_gate", "w_up", "w_out"]
    recv = dict(zip(early, copies_wait("grads_early_wait", [in_flight[n] for n in early], [in_flight["small"][2]])))
    for pair in (early[:2], early[2:]):
        done = adamw_shards("adamw_" + pair[0], [recv[n] for n in pair], [w[n] for n in pair], [m[n] for n in pair], [v[n] for n in pair])
        out.update(zip(pair, done))
    recv_in, recv_small = copies_wait("grads_late_wait", [in_flight["w_in"], in_flight["small"]], [out[n][0] for n in early])
    def lines(a):
        return jnp.transpose(a.reshape(D // 128, 128, IN_COLS // 8), (2, 0, 1))

    out["w_in"] = [jnp.transpose(o, (1, 2, 0)).reshape(D, IN_COLS // 8) for o in adamw_w_in(recv_in, lines(w_in), lines(m_w_in), lines(v_w_in))]
    for n in ("w_gate", "w_up"):
        out[n] = [o.T for o in out[n]]
    sm = all_gather("gather_small_grads", [sum_slabs(recv_small)])[0].reshape(SM_ROWS, 1024)
    special_g =[sm[SM_WA:SM_WA + 64].reshape(16, 64, 64), sm[SM_WX:SM_WX + 64].reshape(16, 64, 64),
                 lax.dynamic_slice(sm[SM_META:SM_META + 16], (0, 128 * me), (16, 128)),
                 lax.dynamic_slice(sm[SM_SCW:SM_SCW + 8].reshape(4, 2048), (0, 192 * me), (4, 192)),
                 lax.dynamic_slice(sm[SM_LCW:SM_LCW + 4], (0, 128 * me), (4, 128))]
    names = [n for n, _ in SIMPLE] + SPECIAL
    res = adamw_small(sm, special_g, [w[n] for n in names], [m[n] for n in names], [v[n] for n in names])
    for k, (n, _) in enumerate(SIMPLE):
        out[n] = res[4 * k:4 * k + 4]
    for k, n in enumerate(SPECIAL):
        o = 4 * len(SIMPLE) + 3 * k
        out[n] = [special_g[k]] + list(res[o:o + 3])
    loss_total = sm[SM_LOSS, 0]
    flat = [loss_total, grad_x[None]]
    for k in range(4):
        flat += [out[n][k].reshape(shapes[n]) for n in WEIGHTS]
    return tuple(flat)
```

```python
import functools
import math

import jax
import jax.numpy as jnp
from jax import lax
from jax.experimental import pallas as pl
from jax.experimental.pallas import tpu as pltpu

F32 = jnp.float32
BF = jnp.bfloat16

D = 1024
SEQ = 2048
N_META = 16
Q = 128
NPAD = 112
T = NPAD + N_META + SEQ
NCH = T // Q
RC = 544
D_FF = 2816
SSD_W = 1024
LRU_W = 1024
XBC = 1536
IN_COLS = 4624
PZ, PG, PXL, PXBC = 0, 1024, 2048, 3072
NP_IN = 4608
EPS = 1e-6
LRU_C = 8.0
VMEM_LIMIT = 56 * 1024 * 1024

ADAM_LR, ADAM_B1, ADAM_B2, ADAM_EPS, ADAM_WD, ADAM_STEP = 0.001, 0.9, 0.999, 1e-08, 0.01, 10

NT_DIMS = (((1,), (1,)), ((), ()))
TN_DIMS = (((0,), (0,)), ((), ()))
MESH = pl.DeviceIdType.MESH


def _params(n_grid=1, limit=VMEM_LIMIT):
    return pltpu.CompilerParams(dimension_semantics=("arbitrary",) * n_grid, vmem_limit_bytes=limit)


def _spec(shape, imap, single=False):
    if single:
        return pl.BlockSpec(shape, imap, pipeline_mode=pl.Buffered(1))
    return pl.BlockSpec(shape, imap)


def _sigmoid(x):
    return 0.5 * jnp.tanh(0.5 * x) + 0.5


def _sigmoid_gate(x):
    return 1.0 / (1.0 + jnp.exp(-x))


def _softplus(x):
    return jnp.maximum(x, 0.0) + jnp.log(1.0 + jnp.exp(-jnp.abs(x)))


def _rms_stats(h):
    return lax.rsqrt(jnp.mean(h * h, axis=-1, keepdims=True) + EPS)


def _rms(h, w):
    return (h * _rms_stats(h)) * w


def _rms_bwd(du, h, w):
    r = _rms_stats(h)
    n = h * r
    dn = du * w
    dh = r * (dn - n * jnp.mean(dn * n, axis=-1, keepdims=True))
    return dh, du * n


_G0 = math.sqrt(2.0 / math.pi)


def _gelu(x):
    return 0.5 * x * (1.0 + jnp.tanh(_G0 * (x + 0.044715 * (x * x * x))))


def _gelu_grad(x):
    t = jnp.tanh(_G0 * (x + 0.044715 * (x * x * x)))
    return 0.5 * (1.0 + t) + 0.5 * x * (1.0 - t * t) * (_G0 * (1.0 + 3.0 * 0.044715 * (x * x)))


def _rows(shape, r0=0):
    return lax.broadcasted_iota(jnp.int32, shape, 0) + r0


def _lanes(shape):
    return lax.broadcasted_iota(jnp.int32, shape, 1)


HALO = 8


def _fill_padded(pad_ref, x_ref):
    pad_ref[0:HALO, :] = jnp.zeros((HALO, pad_ref.shape[1]), F32)
    pad_ref[T + HALO:T + 2 * HALO, :] = jnp.zeros((HALO, pad_ref.shape[1]), F32)

    def step(c, carry):
        r0 = pl.multiple_of(c * Q, Q)
        pad_ref[pl.ds(r0 + HALO, Q), :] = x_ref[pl.ds(r0, Q), :].astype(F32)
        return carry

    lax.fori_loop(0, NCH, step, 0)


def _back(pad_ref, r0):
    win = pad_ref[pl.ds(r0, Q + HALO), :]
    return lambda s: win[HALO:, :] if s == 0 else pltpu.roll(win, s, axis=0)[HALO:, :]


def _ahead(pad_ref, r0):
    win = pad_ref[pl.ds(r0 + HALO, Q + HALO), :]
    return lambda s: win[:Q, :] if s == 0 else pltpu.roll(win, Q + HALO - s, axis=0)[:Q, :]


def _conv(back, w, b):
    y = b + w[3:4, :] * back(0)
    for k in range(3):
        y = y + w[k:k + 1, :] * back(3 - k)
    return y


def _conv_bwd_x(ahead, w):
    dx = w[3:4, :] * ahead(0)
    for k in range(3):
        dx = dx + w[k:k + 1, :] * ahead(3 - k)
    return dx


def _conv_bwd_w(dy, back):
    dws = [jnp.sum(dy * back(3 - k), axis=0, keepdims=True) for k in range(4)]
    return jnp.concatenate(dws, axis=0), jnp.sum(dy, axis=0, keepdims=True)


def _chunks(fn, unrolled=False):
    if unrolled:
        for c in range(NCH):
            fn(c * Q)
        return

    def step(c, carry):
        fn(pl.multiple_of(c * Q, Q))
        return carry

    lax.fori_loop(0, NCH, step, 0)


HALF = RC // 2


def _col_tiles(n, tn, fn):
    def step(j, carry):
        fn(pl.multiple_of(j * tn, tn))
        return carry

    lax.fori_loop(0, n // tn, step, 0)


def _rows_spec(cols, block_col=0):
    return _spec((RC, cols), lambda i: (i, block_col))


def _whole(shape):
    return _spec(shape, lambda i: tuple(0 for _ in shape), single=True)


def _vec(cols):
    return _spec((1, cols), lambda i: (0, 0))


def _zero_at_first(*refs):
    @pl.when(pl.program_id(0) == 0)
    def _():
        for r in refs:
            r[...] = jnp.zeros_like(r)


ANY_SPEC = pl.BlockSpec(memory_space=pl.ANY)


def _arriving(src, dst, sems, starts, rows):
    n, ahead = len(starts), 2
    first = pl.program_id(0) == 0

    def piece(k):
        r0 = starts[0]
        for j in range(1, n):
            r0 = jnp.where(k == j, starts[j], r0)
        at = pl.ds(pl.multiple_of(r0, 16), rows)
        return pltpu.make_async_copy(src.at[at], dst.at[at], sems.at[k])

    @pl.when(first)
    def _():
        for k in range(min(ahead, n)):
            piece(k).start()

    def ready(k):
        k = jnp.asarray(k, jnp.int32)

        @pl.when(first)
        def _():
            piece(k).wait()

            @pl.when(k + ahead < n)
            def _():
                piece(k + ahead).start()

    return ready


IN_RUNS = ((PZ, 0, 1024), (PXBC, 1024, XBC), (PG, 2576, 2048))
IN_TILE = 512
IN_TILE_ROWS = [wrow + IN_TILE * j for _, wrow, width in IN_RUNS for j in range(width // IN_TILE)]


def _in_tiles(fn):
    done = 0
    for pcol, wrow, width in IN_RUNS:
        def step(j, carry, pcol=pcol, wrow=wrow, done=done):
            fn(pl.multiple_of(pcol + j * IN_TILE, IN_TILE), pl.multiple_of(wrow + j * IN_TILE, 16), done + j)
            return carry

        lax.fori_loop(0, width // IN_TILE, step, 0)
        done += width // IN_TILE


def in_proj(h0, wn, w_t, w_dt):
    def body(h_ref, wn_ref, w_hbm, wdt_ref, o_ref, dt_ref, u_ref, w_ref, w_sems):
        ready = _arriving(w_hbm, w_ref, w_sems, IN_TILE_ROWS, IN_TILE)
        for r in (0, HALF):
            u_ref[r:r + HALF, :] = _rms(h_ref[r:r + HALF, :], wn_ref[...]).astype(BF)

        def tile(pcol, wrow, k):
            ready(k)
            o_ref[:, pl.ds(pcol, IN_TILE)] = lax.dot_general(u_ref[...], w_ref[pl.ds(wrow, IN_TILE), :], NT_DIMS,
                                                             preferred_element_type=F32).astype(BF)

        _in_tiles(tile)
        dt_ref[...] = lax.dot_general(u_ref[...], wdt_ref[...], NT_DIMS, preferred_element_type=F32)

    return pl.pallas_call(
        body, grid=(T // RC,), in_specs=[_rows_spec(D), _vec(D), ANY_SPEC, _whole((256, D))],
        out_specs=[_rows_spec(NP_IN), _rows_spec(256), _rows_spec(D)],
        out_shape=[jax.ShapeDtypeStruct((T, NP_IN), BF), jax.ShapeDtypeStruct((T, 256), F32), jax.ShapeDtypeStruct((T, D), BF)],
        scratch_shapes=[pltpu.VMEM((IN_COLS, D), BF), pltpu.SemaphoreType.DMA((len(IN_TILE_ROWS),))],
        compiler_params=_params(), name="in_proj")(h0, wn, w_t, w_dt)


def out_proj(yn_ssd, proj, hseq, lru_nw, w_out, h0):
    def body(y_ref, g_ref, h_ref, wn_ref, w_ref, r_ref, o_ref, cat_ref):
        cat_ref[:, 0:SSD_W] = y_ref[...]
        for r in (0, HALF):
            y = _gelu(g_ref[r:r + HALF, :].astype(F32)) * h_ref[r:r + HALF, :]
            cat_ref[r:r + HALF, SSD_W:] = _rms(y, wn_ref[...]).astype(BF)

        def tile(c0):
            o_ref[:, pl.ds(c0, 512)] = r_ref[:, pl.ds(c0, 512)] + jnp.dot(cat_ref[...], w_ref[:, pl.ds(c0, 512)], preferred_element_type=F32)

        _col_tiles(D, 512, tile)

    return pl.pallas_call(
        body, grid=(T // RC,),
        in_specs=[_rows_spec(SSD_W), _rows_spec(LRU_W, PG // LRU_W), _rows_spec(LRU_W), _vec(LRU_W), _whole((SSD_W + LRU_W, D)), _rows_spec(D)],
        out_specs=[_rows_spec(D), _rows_spec(SSD_W + LRU_W)],
        out_shape=[jax.ShapeDtypeStruct((T, D), F32), jax.ShapeDtypeStruct((T, SSD_W + LRU_W), BF)],
        compiler_params=_params(), name="out_proj")(yn_ssd, proj, hseq, lru_nw, w_out, h0)


def out_proj_bwd(dh1_b, w_out, proj, hseq, lru_nw, after):
    def body(d_ref, w_ref, g_ref, h_ref, wn_ref, _after, dy_ref, dh_ref, dg_ref, dw_ref, dl_scr):
        _zero_at_first(dw_ref)

        def tile(c0):
            dy_ref[:, pl.ds(c0, 512)] = lax.dot_general(d_ref[...], w_ref[pl.ds(c0, 512), :], NT_DIMS, preferred_element_type=F32)
            dl_scr[:, pl.ds(c0, 512)] = lax.dot_general(d_ref[...], w_ref[pl.ds(SSD_W + c0, 512), :], NT_DIMS, preferred_element_type=F32)

        _col_tiles(SSD_W, 512, tile)

        for r in (0, HALF):
            g = g_ref[r:r + HALF, :].astype(F32)
            h = h_ref[r:r + HALF, :]
            ge = _gelu(g)
            dy, dw = _rms_bwd(dl_scr[r:r + HALF, :], ge * h, wn_ref[...])
            dw_ref[...] += jnp.sum(dw, axis=0, keepdims=True)
            dh_ref[r:r + HALF, :] = dy * ge
            dg_ref[r:r + HALF, :] = (dy * h * _gelu_grad(g)).astype(BF)

    return pl.pallas_call(
        body, grid=(T // RC,),
        in_specs=[_rows_spec(D), _whole((SSD_W + LRU_W, D)), _rows_spec(LRU_W, PG // LRU_W), _rows_spec(LRU_W), _vec(LRU_W), ANY_SPEC],
        out_specs=[_rows_spec(SSD_W), _rows_spec(LRU_W), _rows_spec(LRU_W), _vec(LRU_W)],
        out_shape=[jax.ShapeDtypeStruct((T, SSD_W), F32), jax.ShapeDtypeStruct((T, LRU_W), F32), jax.ShapeDtypeStruct((T, LRU_W), BF),
                   jax.ShapeDtypeStruct((1, LRU_W), F32)],
        scratch_shapes=[pltpu.VMEM((RC, LRU_W), F32)],
        compiler_params=_params(), name="out_proj_bwd")(dh1_b, w_out, proj, hseq, lru_nw, after)


def in_proj_bwd(dz, dg, dxl, dxbc, ddt, w_t, w_dt, h0, wn, dh1, after):
    first = NPAD + N_META

    def body(dz_ref, dg_ref, dxl_ref, dxbc_ref, ddt_ref, w_hbm, wdt_ref, h_ref, wn_ref, r_ref, _after, gx_hbm, meta_ref, dw_ref, du_scr, o_ref, sem,
             w_ref, w_sems):
        i = pl.program_id(0)
        ready = _arriving(w_hbm, w_ref, w_sems, IN_TILE_ROWS, IN_TILE)
        _zero_at_first(dw_ref)
        du_scr[...] = jnp.dot(ddt_ref[...], wdt_ref[...], preferred_element_type=F32)
        done = 0
        for d_ref, wrow, width in ((dz_ref, 0, 1024), (dxbc_ref, 1024, XBC), (dg_ref, 2576, 1024), (dxl_ref, 3600, 1024)):
            def step(j, carry, d_ref=d_ref, wrow=wrow, done=done):
                c0 = pl.multiple_of(j * IN_TILE, IN_TILE)
                ready(done + j)
                du_scr[...] += jnp.dot(d_ref[:, pl.ds(c0, IN_TILE)], w_ref[pl.ds(pl.multiple_of(wrow + c0, 16), IN_TILE), :],
                                       preferred_element_type=F32)
                return carry

            lax.fori_loop(0, width // IN_TILE, step, 0)
            done += width // IN_TILE
        for r in (0, HALF):
            dh, dw = _rms_bwd(du_scr[r:r + HALF, :], h_ref[r:r + HALF, :], wn_ref[...])
            dw_ref[...] += jnp.sum(dw, axis=0, keepdims=True)
            o_ref[r:r + HALF, :] = dh + r_ref[r:r + HALF, :]

        @pl.when(i == 0)
        def _():
            meta_ref[...] = o_ref[NPAD:first, :]
            head = pltpu.make_async_copy(o_ref.at[pl.ds(first, RC - first)], gx_hbm.at[pl.ds(0, RC - first)], sem)
            head.start()
            head.wait()

        @pl.when(i > 0)
        def _():
            rest = pltpu.make_async_copy(o_ref, gx_hbm.at[pl.ds(pl.multiple_of(i * RC - first, 32), RC)], sem)
            rest.start()
            rest.wait()

    return pl.pallas_call(
        body, grid=(T // RC,),
        in_specs=[_rows_spec(SSD_W), _rows_spec(LRU_W), _rows_spec(LRU_W), _rows_spec(XBC), _rows_spec(256), ANY_SPEC,
                  _whole((256, D)), _rows_spec(D), _vec(D), _rows_spec(D), ANY_SPEC],
        out_specs=[ANY_SPEC, _spec((N_META, D), lambda i: (0, 0)), _vec(D)],
        out_shape=[jax.ShapeDtypeStruct((SEQ, D), F32), jax.ShapeDtypeStruct((N_META, D), F32), jax.ShapeDtypeStruct((1, D), F32)],
        scratch_shapes=[pltpu.VMEM((RC, D), F32), pltpu.VMEM((RC, D), F32), pltpu.SemaphoreType.DMA,
                        pltpu.VMEM((IN_COLS, D), BF), pltpu.SemaphoreType.DMA((len(IN_TILE_ROWS),))],
        compiler_params=_params(), name="in_proj_bwd")(dz, dg, dxl, dxbc, ddt, w_t, w_dt, h0, wn, dh1, after)


GRAD_TILE = 256


def weight_grad(name, a, u1):
    tm = GRAD_TILE

    def body(a_ref, u_ref, o_ref):
        o_ref[...] = lax.dot_general(a_ref[...], u_ref[...], TN_DIMS, preferred_element_type=F32).astype(BF)

    return pl.pallas_call(
        body, grid=(a.shape[1] // tm,),
        in_specs=[_spec((T, tm), lambda j: (0, j)), _spec((T, D), lambda j: (0, 0), single=True)],
        out_specs=_spec((tm, D), lambda j: (j, 0)),
        out_shape=jax.ShapeDtypeStruct((a.shape[1], D), BF),
        compiler_params=_params(), name=name)(a, u1)


def in_weight_grad(name, parts, first_rows, u1, ddt=None, into=None):
    tm = GRAD_TILE
    per_row = D // 128
    dt_row, dt_lines = 2560, 8 * per_row
    parts = list(parts) + ([] if ddt is None else [ddt])
    first_rows = list(first_rows) + ([] if ddt is None else [dt_row])
    tiles = [p.shape[1] // tm for p in parts]
    starts = [sum(tiles[:k]) for k in range(len(parts))]
    last = sum(tiles) - 1
    whole = last + (ddt is None)
    n_in = len(parts) + 1 + (into is not None)

    def body(*refs):
        a_refs, u_ref = refs[:len(parts)], refs[len(parts)]
        o_hbm, mix_scr, stage, sems = refs[n_in:]
        step = pl.program_id(0)
        slot = step % 2
        line0 = 0
        for a_ref, start, n, first in zip(a_refs, starts, tiles, first_rows):
            here = (step >= start) & (step < start + n)
            line0 = jnp.where(here, per_row * (first + tm * (step - start)), line0)

            @pl.when(here)
            def _(a_ref=a_ref):
                res = lax.dot_general(a_ref[...], u_ref[...], TN_DIMS, preferred_element_type=F32)
                for q in range(per_row):
                    mix_scr[pl.ds(q, tm, stride=per_row), :] = res[:, 128 * q:128 * q + 128]

        def tile_copy(of_slot, to):
            return pltpu.make_async_copy(stage.at[of_slot], o_hbm.at[pl.ds(to, per_row * tm)], sems.at[of_slot])

        @pl.when(step >= 2)
        def _():
            tile_copy(slot, 0).wait()

        stage[slot] = mix_scr[...].astype(BF)

        @pl.when(step < whole)
        def _():
            tile_copy(slot, pl.multiple_of(line0, 128)).start()

        @pl.when(step == last)
        def _():
            halves = [] if ddt is None else [
                pltpu.make_async_copy(stage.at[slot, pl.ds(128 * per_row * k, dt_lines)],
                                      o_hbm.at[pl.ds(per_row * (dt_row + 8 * k), dt_lines)], sems.at[2 + k]) for k in range(2)]
            for cp in halves:
                cp.start()
            tile_copy(1 - slot, 0).wait()
            if ddt is None:
                tile_copy(slot, 0).wait()
            for cp in halves:
                cp.wait()

    def tile_of(start, n):
        return lambda j: (0, jnp.clip(j - start, 0, n - 1))

    return pl.pallas_call(
        body, grid=(last + 1,),
        in_specs=([_spec((T, tm), tile_of(s, n)) for s, n in zip(starts, tiles)] + [_spec((T, D), lambda j: (0, 0), single=True)]
                  + [ANY_SPEC] * (into is not None)),
        out_specs=ANY_SPEC,
        out_shape=jax.ShapeDtypeStruct((per_row * IN_COLS, 128), BF),
        scratch_shapes=[pltpu.VMEM((per_row * tm, 128), F32), pltpu.VMEM((2, per_row * tm, 128), BF), pltpu.SemaphoreType.DMA((4,))],
        input_output_aliases={} if into is None else {n_in - 1: 0},
        compiler_params=_params(), name=name)(*parts, u1, *([] if into is None else [into]))


def _ssd_chunk_common(row0, dt_ref, b_ref, c_ref, bias, a_neg):
    shape = (Q, Q)
    lane = _lanes(shape)
    sub = _rows(shape)
    live = (_rows(shape, row0) >= NPAD) & (lane < 8)
    dtr = dt_ref[:, :]
    dt = jnp.where(live, _softplus(dtr + bias), 0.0)
    d_a = dt * a_neg
    tri = (sub >= lane).astype(F32)
    cs = jnp.dot(tri, d_a, precision=lax.Precision.HIGHEST, preferred_element_type=F32)
    cs_t = cs.T
    b_f = b_ref[:, :]
    bc = b_f.astype(BF)
    cc = c_ref[:, :].astype(BF)
    cb = lax.dot_general(cc, bc, NT_DIMS, preferred_element_type=F32)
    cs_last = cs[Q - 1:Q, :]
    return dict(lane=lane, sub=sub, live=live, dtr=dtr, dt=dt, cs=cs, cs_t=cs_t, bc=bc, cc=cc, cb=cb, bc_t=b_f.T.astype(BF),
                ecs=jnp.exp(cs), dsm=jnp.exp(cs_last - cs), gam=jnp.exp(cs_last))


def _pair(lane_even, mat, j):
    return jnp.where(lane_even, mat[:, j:j + 1], mat[:, j + 1:j + 2])


def _pair_row(lane_even, mat, j):
    return jnp.where(lane_even[0:1, :], mat[:, j:j + 1], mat[:, j + 1:j + 2])


def _head_decay(cm, j):
    seg = cm["cs"][:, j:j + 1] - cm["cs_t"][j:j + 1, :]
    return jnp.exp(jnp.where(cm["sub"] >= cm["lane"], seg, -jnp.inf))


def _head_decay_t(cm, j):
    seg = cm["cs_t"][j:j + 1, :] - cm["cs"][:, j:j + 1]
    return jnp.exp(jnp.where(cm["lane"] >= cm["sub"], seg, -jnp.inf))


def _conv_window(raw_ref, halo_ref, pad_scr):
    pad_scr[0:HALO, :] = halo_ref[...].astype(F32)[halo_ref.shape[0] - HALO:, :]
    pad_scr[HALO:HALO + Q, :] = raw_ref[...].astype(F32)
    win = pad_scr[...]
    return lambda s: win[HALO:, :] if s == 0 else pltpu.roll(win, s, axis=0)[HALO:, :]


def _xbc_cols(g):
    return slice(512 * g, 512 * g + 512), slice(SSD_W + 128 * g, SSD_W + 128 * g + 128), slice(SSD_W + 256 + 128 * g, SSD_W + 384 + 128 * g)


def ssd_fwd(proj, dt_raw, conv_w, conv_b, dt_bias2, a_log2, d2, norm_w):
    def body(raw_ref, halo_ref, dt_all, z_all, cw_ref, cb_ref, bias_all, alog_all, d_all, nw_all, yn_all, y_all, hp_all,
             h_all, pad_scr, act_scr):
        @pl.when(pl.program_id(0) == 0)
        def _():
            h_all[...] = jnp.zeros_like(h_all)

        pre = _conv(_conv_window(raw_ref, halo_ref, pad_scr), cw_ref[...], cb_ref[...])
        act_scr[...] = pre * _sigmoid(pre)
        for g in range(2):
            wide, thin = slice(512 * g, 512 * g + 512), slice(128 * g, 128 * g + 128)
            xs, bs, cs = _xbc_cols(g)
            group(act_scr.at[:, xs], act_scr.at[:, bs], act_scr.at[:, cs], dt_all.at[:, thin], z_all.at[:, wide], bias_all.at[g],
                  alog_all.at[g], d_all.at[g], nw_all.at[:, wide], yn_all.at[:, wide], y_all.at[:, wide], hp_all.at[g, 0], h_all.at[g])

    def group(x_ref, b_ref, c_ref, dt_ref, z_ref, bias_ref, alog_ref, d_ref, nw_ref, yn_ref, y_ref, hp_ref, h_scr):
        bias = bias_ref[...]
        a_neg = -jnp.exp(alog_ref[...])
        dsk = d_ref[...]
        cm = _ssd_chunk_common(pl.program_id(0) * Q, dt_ref, b_ref, c_ref, bias, a_neg)
        lane_even = cm["lane"] < 64
        for p in range(4):
            je, jo = 2 * p, 2 * p + 1
            xp = x_ref[:, 128 * p:128 * p + 128]
            xdt = xp * _pair(lane_even, cm["dt"], je)
            xdt_b = xdt.astype(BF)
            m_e = (cm["cb"] * _head_decay(cm, je)).astype(BF)
            m_o = (cm["cb"] * _head_decay(cm, jo)).astype(BF)
            zero = jnp.zeros_like(xdt_b)
            yd = (jnp.dot(m_e, jnp.where(lane_even, xdt_b, zero), preferred_element_type=F32)
                  + jnp.dot(m_o, jnp.where(lane_even, zero, xdt_b), preferred_element_type=F32))
            hp = h_scr[p]
            hp_ref[p] = hp
            yo = jnp.dot(cm["cc"], hp.astype(BF), preferred_element_type=F32) * _pair(lane_even, cm["ecs"], je)
            y_ref[:, 128 * p:128 * p + 128] = yd + yo + xp * _pair_row(lane_even, dsk, je)
            st = jnp.dot(cm["bc_t"], (xdt * _pair(lane_even, cm["dsm"], je)).astype(BF), preferred_element_type=F32)
            h_scr[p] = hp * _pair_row(lane_even, cm["gam"], je) + st
        zc = z_ref[:, :].astype(F32)
        gated = y_ref[:, :] * (zc * _sigmoid(zc))
        yn_ref[:, :] = _rms(gated, nw_ref[...]).astype(BF)

    par = _spec((2, 1, 128), lambda c: (0, 0, 0))
    wide = _spec((Q, SSD_W), lambda c: (c, 0))
    xbc = PXBC // XBC
    halo = 2 * HALO
    return pl.pallas_call(
        body, grid=(NCH,),
        in_specs=[_spec((Q, XBC), lambda c: (c, xbc)), _spec((halo, XBC), lambda c: (jnp.maximum(c * (Q // halo) - 1, 0), xbc)),
                  _spec((Q, 256), lambda c: (c, 0)), wide, _spec((4, XBC), lambda c: (0, 0)), _spec((1, XBC), lambda c: (0, 0)),
                  par, par, par, _spec((1, SSD_W), lambda c: (0, 0))],
        out_specs=[wide, wide, _spec((2, 1, 4, 128, 128), lambda c: (0, c, 0, 0, 0))],
        out_shape=[jax.ShapeDtypeStruct((T, SSD_W), BF), jax.ShapeDtypeStruct((T, SSD_W), F32),
                   jax.ShapeDtypeStruct((2, NCH, 4, 128, 128), F32)],
        scratch_shapes=[pltpu.VMEM((2, 4, 128, 128), F32), pltpu.VMEM((Q + HALO, XBC), F32), pltpu.VMEM((Q, XBC), F32)],
        compiler_params=_params(), name="ssd_fwd")(proj, proj, dt_raw, proj, conv_w, conv_b, dt_bias2, a_log2, d2, norm_w)


def ssd_bwd(dyn, proj, dt_raw, conv_w, conv_b, y_pre, h_prev, dt_bias2, a_log2, d2, norm_w, after):
    def body(dyn_all, raw_ref, halo_ref, dt_all, z_all, y_all, hp_all, cw_ref, cb_ref, bias_all, alog_all, d_all, nw_all, _after,
             dz_all, dxbc_ref, ddt_all, dpar_all, dnw_all, dcw_ref, dcb_ref, dh_all, acc_all, pad_scr, act_scr, dsilu_scr, dact_scr, dpad_scr):
        @pl.when(pl.program_id(0) == 0)
        def _():
            dh_all[...] = jnp.zeros_like(dh_all)
            acc_all[...] = jnp.zeros_like(acc_all)
            dnw_all[...] = jnp.zeros_like(dnw_all)
            dcw_ref[...] = jnp.zeros_like(dcw_ref)
            dcb_ref[...] = jnp.zeros_like(dcb_ref)
            dpad_scr[Q:Q + HALO, :] = jnp.zeros((HALO, XBC), F32)

        back = _conv_window(raw_ref, halo_ref, pad_scr)
        pre = _conv(back, cw_ref[...], cb_ref[...])
        sg = _sigmoid(pre)
        act_scr[...] = pre * sg
        dsilu_scr[...] = sg * (1.0 + pre * (1.0 - sg))
        for g in range(2):
            wide, thin = slice(512 * g, 512 * g + 512), slice(128 * g, 128 * g + 128)
            xs, bs, cs = _xbc_cols(g)
            group(dyn_all.at[:, wide], act_scr.at[:, xs], act_scr.at[:, bs], act_scr.at[:, cs], dt_all.at[:, thin], z_all.at[:, wide],
                  y_all.at[:, wide], hp_all.at[g, 0], bias_all.at[g], alog_all.at[g], d_all.at[g], nw_all.at[:, wide],
                  dz_all.at[:, wide], dact_scr.at[:, xs], dact_scr.at[:, bs], dact_scr.at[:, cs], ddt_all.at[:, thin], dpar_all.at[g],
                  dnw_all.at[:, wide], dh_all.at[g], acc_all.at[g])
        dpre = dact_scr[...] * dsilu_scr[...]
        dcw, dcb = _conv_bwd_w(dpre, back)
        dcw_ref[...] += dcw
        dcb_ref[...] += dcb
        dpad_scr[0:Q, :] = dpre
        win = dpad_scr[...]
        dxbc_ref[...] = _conv_bwd_x(lambda s: win[:Q, :] if s == 0 else pltpu.roll(win, Q + HALO - s, axis=0)[:Q, :], cw_ref[...]).astype(BF)
        dpad_scr[Q:Q + HALO, :] = dpre[0:HALO, :]

    def group(dyn_ref, x_ref, b_ref, c_ref, dt_ref, z_ref, y_ref, hp_ref, bias_ref, alog_ref, d_ref, nw_ref,
              dz_ref, dx_ref, db_ref, dc_ref, ddt_ref, dpar_ref, dnw_ref, dh_scr, acc_scr):
        ci = pl.program_id(0)
        bias = bias_ref[...]
        a_neg = -jnp.exp(alog_ref[...])
        dsk = d_ref[...]
        cm = _ssd_chunk_common((NCH - 1 - ci) * Q, dt_ref, b_ref, c_ref, bias, a_neg)
        lane, sub = cm["lane"], cm["sub"]
        lane_even = lane < 64
        cc_t = c_ref[:, :].T.astype(BF)
        cb_t = lax.dot_general(cm["bc"], cm["cc"], NT_DIMS, preferred_element_type=F32)
        zc = z_ref[:, :].astype(F32)
        yc = y_ref[:, :]
        sg = _sigmoid(zc)
        sz = zc * sg
        dgated, dnw = _rms_bwd(dyn_ref[:, :], yc * sz, nw_ref[...])
        dnw_ref[...] += jnp.sum(dnw, axis=0, keepdims=True)
        dz_ref[:, :] = (dgated * yc * (sg * (1.0 + zc * (1.0 - sg)))).astype(BF)
        dy_all = dgated * sz
        dcb = jnp.zeros((Q, Q), F32)
        dcb_t = jnp.zeros((Q, Q), F32)
        db_acc = jnp.zeros((Q, Q), F32)
        dc_acc = jnp.zeros((Q, Q), F32)
        dcs = jnp.zeros((Q, Q), F32)
        ddt = jnp.zeros((Q, Q), F32)
        for p in range(4):
            je, jo = 2 * p, 2 * p + 1
            xp = x_ref[:, 128 * p:128 * p + 128]
            dy = dy_all[:, 128 * p:128 * p + 128]
            dt_p = _pair(lane_even, cm["dt"], je)
            xdt = xp * dt_p
            xdt_b = xdt.astype(BF)
            dy_b = dy.astype(BF)
            zero = jnp.zeros_like(dy_b)
            hp = hp_ref[p]
            hp_b = hp.astype(BF)
            dh = dh_scr[p]
            dh_b = dh.astype(BF)
            acc_scr[p:p + 1, :] += jnp.sum(dy * xp, axis=0, keepdims=True)
            dxp = dy * _pair_row(lane_even, dsk, je)
            e_p = _pair(lane_even, cm["ecs"], je)
            g_p = jnp.dot(cm["cc"], hp_b, preferred_element_type=F32)
            dg_b = (dy * e_p).astype(BF)
            de = dy * g_p * e_p
            dc_acc = dc_acc + lax.dot_general(dg_b, hp_b, NT_DIMS, preferred_element_type=F32)
            dh_in = jnp.dot(cc_t, dg_b, preferred_element_type=F32)
            ds_p = _pair(lane_even, cm["dsm"], je)
            r_p = jnp.dot(cm["bc"], dh_b, preferred_element_type=F32)
            dxdt = r_p * ds_p
            tt = r_p * xdt * ds_p
            db_acc = db_acc + lax.dot_general((xdt * ds_p).astype(BF), dh_b, NT_DIMS, preferred_element_type=F32)
            dgam_m = jnp.sum(dh * hp, axis=0, keepdims=True)
            for j, even in ((je, True), (jo, False)):
                sel = lane_even if even else jnp.logical_not(lane_even)
                dy_j = jnp.where(sel, dy_b, zero)
                l_j = _head_decay(cm, j)
                l_jt = _head_decay_t(cm, j)
                m_j = cm["cb"] * l_j
                m_jt = cb_t * l_jt
                dm = lax.dot_general(dy_j, xdt_b, NT_DIMS, preferred_element_type=F32)
                dm_t = lax.dot_general(xdt_b, dy_j, NT_DIMS, preferred_element_type=F32)
                dxdt = dxdt + jnp.dot(m_jt.astype(BF), dy_j, preferred_element_type=F32)
                dcb = dcb + dm * l_j
                dcb_t = dcb_t + dm_t * l_jt
                t_j = jnp.where(sel, tt, 0.0)
                col = jnp.sum(dm * m_j - dm_t * m_jt + (jnp.where(sel, de, 0.0) - t_j), axis=1, keepdims=True)
                gam_j = cm["gam"][:, j:j + 1]
                last = (jnp.sum(jnp.sum(t_j, axis=0, keepdims=True), axis=1, keepdims=True)
                        + jnp.sum(jnp.where(sel[0:1, :], dgam_m, 0.0), axis=1, keepdims=True) * gam_j)
                col = col + jnp.where(sub[:, 0:1] == Q - 1, last, 0.0)
                dcs = dcs + jnp.where(lane == j, col, 0.0)
            dh_scr[p] = dh_in + dh * _pair_row(lane_even, cm["gam"], je)
            dx_ref[:, 128 * p:128 * p + 128] = dxp + dxdt * dt_p
            dd = dxdt * xp
            ddt = ddt + jnp.where(lane == je, jnp.sum(jnp.where(lane_even, dd, 0.0), axis=1, keepdims=True), 0.0)
            ddt = ddt + jnp.where(lane == jo, jnp.sum(jnp.where(lane_even, 0.0, dd), axis=1, keepdims=True), 0.0)
        dc_ref[:, :] = dc_acc + jnp.dot(dcb.astype(BF), cm["bc"], preferred_element_type=F32)
        db_ref[:, :] = db_acc + jnp.dot(dcb_t.astype(BF), cm["cc"], preferred_element_type=F32)
        tri_t = (sub <= lane).astype(F32)
        dd_a = jnp.dot(tri_t, dcs, precision=lax.Precision.HIGHEST, preferred_element_type=F32)
        ddt = ddt + dd_a * a_neg
        acc_scr[5:6, :] += jnp.sum(dd_a * cm["dt"], axis=0, keepdims=True)
        draw = jnp.where(cm["live"], ddt * _sigmoid_gate(cm["dtr"] + bias), 0.0)
        acc_scr[4:5, :] += jnp.sum(draw, axis=0, keepdims=True)
        ddt_ref[:, :] = draw.astype(BF)

        @pl.when(ci == NCH - 1)
        def _():
            lane1 = _lanes((1, 128))
            dd = jnp.zeros((1, 128), F32)
            for p in range(4):
                row = acc_scr[p:p + 1, :]
                dd = dd + jnp.where(lane1 == 2 * p, jnp.sum(jnp.where(lane1 < 64, row, 0.0), axis=1, keepdims=True), 0.0)
                dd = dd + jnp.where(lane1 == 2 * p + 1, jnp.sum(jnp.where(lane1 < 64, 0.0, row), axis=1, keepdims=True), 0.0)
            dpar_ref[...] = jnp.concatenate([acc_scr[4:5, :], acc_scr[5:6, :] * a_neg, dd, jnp.zeros((5, 128), F32)], axis=0)

    par = _spec((2, 1, 128), lambda c: (0, 0, 0))
    wide = _spec((Q, SSD_W), lambda c: (NCH - 1 - c, 0))
    thin = _spec((Q, 256), lambda c: (NCH - 1 - c, 0))
    vec = _spec((1, SSD_W), lambda c: (0, 0))
    xbc = PXBC // XBC
    halo = 2 * HALO
    chunk = pltpu.VMEM((Q, XBC), F32)
    padded = pltpu.VMEM((Q + HALO, XBC), F32)
    return pl.pallas_call(
        body, grid=(NCH,),
        in_specs=[wide, _spec((Q, XBC), lambda c: (NCH - 1 - c, xbc)),
                  _spec((halo, XBC), lambda c: (jnp.maximum((NCH - 1 - c) * (Q // halo) - 1, 0), xbc)), thin, wide, wide,
                  _spec((2, 1, 4, 128, 128), lambda c: (0, NCH - 1 - c, 0, 0, 0)), _spec((4, XBC), lambda c: (0, 0)),
                  _spec((1, XBC), lambda c: (0, 0)), par, par, par, vec, ANY_SPEC],
        out_specs=[wide, _spec((Q, XBC), lambda c: (NCH - 1 - c, 0)), thin, _spec((2, 8, 128), lambda c: (0, 0, 0)), vec,
                   _spec((4, XBC), lambda c: (0, 0)), _spec((1, XBC), lambda c: (0, 0))],
        out_shape=[jax.ShapeDtypeStruct((T, SSD_W), BF), jax.ShapeDtypeStruct((T, XBC), BF), jax.ShapeDtypeStruct((T, 256), BF),
                   jax.ShapeDtypeStruct((2, 8, 128), F32), jax.ShapeDtypeStruct((1, SSD_W), F32), jax.ShapeDtypeStruct((4, XBC), F32),
                   jax.ShapeDtypeStruct((1, XBC), F32)],
        scratch_shapes=[pltpu.VMEM((2, 4, 128, 128), F32), pltpu.VMEM((2, 8, 128), F32), padded, chunk, chunk, chunk, padded],
        compiler_params=_params(), name="ssd_bwd")(dyn, proj, proj, dt_raw, proj, y_pre, h_prev, conv_w, conv_b, dt_bias2, a_log2, d2, norm_w, after)


def _lru_gates(back, cw, cb, wa, ba, wx, bx, lam):
    xr = _conv(back, cw, cb)
    xr_b = xr.astype(BF)
    r = _sigmoid_gate(jnp.dot(xr_b, wa, preferred_element_type=F32) + ba)
    i = _sigmoid_gate(jnp.dot(xr_b, wx, preferred_element_type=F32) + bx)
    sp = _softplus(-lam)
    la = (-LRU_C) * r * sp
    a = jnp.exp(la)
    mult2 = -jnp.tanh(la) * (a * a + 1.0)
    return xr, xr_b, r, i, sp, a, jnp.sqrt(mult2), mult2


SEG_LEN = 68
SEGS = T // SEG_LEN


def _seg_rows(j, k, off=0):
    return pl.ds(off + j * 8 * SEG_LEN + k, 8, stride=SEG_LEN)


def _segmented_scan(mul_ref, mul_row0, add_ref, out_ref, loc_scr, prod_scr, carry_scr, reverse):
    groups = SEGS // 8
    off = mul_row0 + (1 if reverse else 0)

    def local(i, carry):
        k = SEG_LEN - 1 - i if reverse else i
        new = []
        for j in range(groups):
            h, p = carry[2 * j], carry[2 * j + 1]
            m = mul_ref[_seg_rows(j, k, off), :]
            h = m * h + add_ref[_seg_rows(j, k), :]
            p = m * p
            loc_scr[_seg_rows(j, k), :] = h
            prod_scr[_seg_rows(j, k), :] = p
            new += [h, p]
        return tuple(new)

    lax.fori_loop(0, SEG_LEN, local, (jnp.zeros((8, 128), F32), jnp.ones((8, 128), F32)) * groups)

    def chain(i, c):
        s = SEGS - 1 - i if reverse else i
        carry_scr[pl.ds(s, 1), :] = c
        edge = s * SEG_LEN + (0 if reverse else SEG_LEN - 1)
        return loc_scr[pl.ds(edge, 1), :] + prod_scr[pl.ds(edge, 1), :] * c

    lax.fori_loop(0, SEGS, chain, jnp.zeros((1, 128), F32))

    def fold(k, carry):
        for j in range(groups):
            rows = _seg_rows(j, k)
            out_ref[rows, :] = loc_scr[rows, :] + prod_scr[rows, :] * carry_scr[8 * j:8 * j + 8, :]
        return carry

    lax.fori_loop(0, SEG_LEN, fold, 0)


def lru_fwd(proj, cw, cb, wa2, ba, wx2, bx, lam, after):
    def body(x_ref, cw_ref, cb_ref, wa_ref, ba_ref, wx_ref, bx_ref, lam_ref, _after, h_ref, a_ref, xpad, u_scr, loc_scr, prod_scr, carry_scr):
        _fill_padded(xpad, x_ref)

        def chunk(r0):
            xr, _, _, i, _, a, mult, _ = _lru_gates(_back(xpad, r0), cw_ref[...], cb_ref[...], wa_ref[0], ba_ref[...], wx_ref[0], bx_ref[...],
                                                 lam_ref[...])
            a_ref[pl.ds(r0, Q), :] = a
            u_scr[pl.ds(r0, Q), :] = jnp.where(_rows(a.shape, r0) >= NPAD, mult * (i * xr), 0.0)

        _chunks(chunk, unrolled=True)
        _segmented_scan(a_ref, 0, u_scr, h_ref, loc_scr, prod_scr, carry_scr, reverse=False)

    c0 = PXL // 128
    vec = _spec((1, 128), lambda c: (0, c))
    mat = _spec((1, 128, 128), lambda c: (c, 0, 0))
    seq = pltpu.VMEM((T, 128), F32)
    return pl.pallas_call(
        body, grid=(8,),
        in_specs=[_spec((T, 128), lambda c: (0, c0 + c)), _spec((4, 128), lambda c: (0, c)), vec, mat, vec, mat, vec, vec, ANY_SPEC],
        out_specs=[_spec((T, 128), lambda c: (0, c)), _spec((T, 128), lambda c: (0, c))],
        out_shape=[jax.ShapeDtypeStruct((T, LRU_W), F32), jax.ShapeDtypeStruct((T, LRU_W), F32)],
        scratch_shapes=[pltpu.VMEM((T + 2 * HALO, 128), F32), seq, seq, seq, pltpu.VMEM((SEGS, 128), F32)],
        compiler_params=_params(), name="lru_fwd")(proj, cw, cb, wa2, ba, wx2, bx, lam, after)


def lru_bwd(dh_out, a, hseq, proj, cw, cb, wa2, ba, wx2, bx, lam):
    def body(d_ref, a_ref, h_ref, x_ref, cw_ref, cb_ref, wa_ref, ba_ref, wx_ref, bx_ref, lam_ref,
             dx_ref, dcw_ref, dcb_ref, dwa_ref, dba_ref, dwx_ref, dbx_ref, dlam_ref, xpad, hpad, dpad, dh_ref, loc_scr, prod_scr, carry_scr):
        _fill_padded(dpad, a_ref)
        _segmented_scan(dpad, HALO, d_ref, dh_ref, loc_scr, prod_scr, carry_scr, reverse=True)
        _fill_padded(xpad, x_ref)
        _fill_padded(hpad, h_ref)
        dpad[0:HALO, :] = jnp.zeros((HALO, 128), F32)
        dpad[T + HALO:T + 2 * HALO, :] = jnp.zeros((HALO, 128), F32)
        for ref in (dcw_ref, dcb_ref, dwa_ref, dba_ref, dwx_ref, dbx_ref, dlam_ref):
            ref[...] = jnp.zeros_like(ref)
        lam = lam_ref[...]

        def first(r0):
            back = _back(xpad, r0)
            xr, xr_b, r, i, sp, a, mult, mult2 = _lru_gates(back, cw_ref[...], cb_ref[...], wa_ref[0], ba_ref[...], wx_ref[0], bx_ref[...], lam)
            dh = dh_ref[pl.ds(r0, Q), :]
            da = dh * _back(hpad, r0)(1)
            du = jnp.where(_rows(dh.shape, r0) >= NPAD, dh, 0.0)
            dmult = du * (i * xr)
            di = du * (mult * xr)
            dxr = du * (mult * i)
            dla = da * a - dmult * (a * a) * lax.rsqrt(mult2)
            dr = dla * ((-LRU_C) * sp)
            dlam_ref[...] += jnp.sum(dla * ((-LRU_C) * r), axis=0, keepdims=True)
            dpr = dr * r * (1.0 - r)
            dpi = di * i * (1.0 - i)
            dba_ref[...] += jnp.sum(dpr, axis=0, keepdims=True)
            dbx_ref[...] += jnp.sum(dpi, axis=0, keepdims=True)
            dpr_b = dpr.astype(BF)
            dpi_b = dpi.astype(BF)
            dxr = (dxr + lax.dot_general(dpr_b, wa_ref[0], NT_DIMS, preferred_element_type=F32)
                   + lax.dot_general(dpi_b, wx_ref[0], NT_DIMS, preferred_element_type=F32))
            dwa_ref[0] += lax.dot_general(xr_b, dpr_b, TN_DIMS, preferred_element_type=F32)
            dwx_ref[0] += lax.dot_general(xr_b, dpi_b, TN_DIMS, preferred_element_type=F32)
            dpad[pl.ds(r0 + HALO, Q), :] = dxr
            dcw, dcb = _conv_bwd_w(dxr, back)
            dcw_ref[...] += dcw
            dcb_ref[...] += dcb

        _chunks(first, unrolled=True)
        dlam_ref[...] = -dlam_ref[...] * _sigmoid_gate(-lam)

        def second(r0):
            dx_ref[pl.ds(r0, Q), :] = _conv_bwd_x(_ahead(dpad, r0), cw_ref[...]).astype(BF)

        _chunks(second)

    c0 = PXL // 128
    vec = _spec((1, 128), lambda c: (0, c))
    mat = _spec((1, 128, 128), lambda c: (c, 0, 0))
    col = _spec((T, 128), lambda c: (0, c))
    vshape = jax.ShapeDtypeStruct((1, LRU_W), F32)
    mshape = jax.ShapeDtypeStruct((8, 128, 128), F32)
    pad = pltpu.VMEM((T + 2 * HALO, 128), F32)
    seq = pltpu.VMEM((T, 128), F32)
    return pl.pallas_call(
        body, grid=(8,),
        in_specs=[col, col, col, _spec((T, 128), lambda c: (0, c0 + c)), _spec((4, 128), lambda c: (0, c)), vec, mat, vec, mat, vec, vec],
        out_specs=[col, _spec((4, 128), lambda c: (0, c)), vec, mat, vec, mat, vec, vec],
        out_shape=[jax.ShapeDtypeStruct((T, LRU_W), BF), jax.ShapeDtypeStruct((4, LRU_W), F32), vshape, mshape, vshape, mshape, vshape, vshape],
        scratch_shapes=[pad, pad, pad, seq, seq, seq, pltpu.VMEM((SEGS, 128), F32)],
        compiler_params=_params(), name="lru_bwd")(dh_out, a, hseq, proj, cw, cb, wa2, ba, wx2, bx, lam)


FF_TILE = 256
FF_TILE_ROWS = list(range(0, D_FF, FF_TILE))


def gate_up(h1, wn, w_gate, w_up):
    def body(h_ref, wn_ref, wg_hbm, wu_hbm, gt_ref, up_ref, act_ref, u_ref, wg_ref, wu_ref, wg_sems, wu_sems):
        gate_ready = _arriving(wg_hbm, wg_ref, wg_sems, FF_TILE_ROWS, FF_TILE)
        up_ready = _arriving(wu_hbm, wu_ref, wu_sems, FF_TILE_ROWS, FF_TILE)
        for r in (0, HALF):
            u_ref[r:r + HALF, :] = _rms(h_ref[r:r + HALF, :], wn_ref[...]).astype(BF)

        def tile(c0):
            cols = pl.ds(c0, FF_TILE)
            gate_ready(c0 // FF_TILE)
            up_ready(c0 // FF_TILE)
            gt = lax.dot_general(u_ref[...], wg_ref[cols, :], NT_DIMS, preferred_element_type=F32)
            up = lax.dot_general(u_ref[...], wu_ref[cols, :], NT_DIMS, preferred_element_type=F32)
            gt_ref[:, cols] = gt.astype(BF)
            up_ref[:, cols] = up.astype(BF)
            act_ref[:, cols] = (gt * _sigmoid(gt) * up).astype(BF)

        _col_tiles(D_FF, FF_TILE, tile)

    big = jax.ShapeDtypeStruct((T, D_FF), BF)
    return pl.pallas_call(
        body, grid=(T // RC,), in_specs=[_rows_spec(D), _vec(D), ANY_SPEC, ANY_SPEC],
        out_specs=[_rows_spec(D_FF), _rows_spec(D_FF), _rows_spec(D_FF), _rows_spec(D)],
        out_shape=[big, big, big, jax.ShapeDtypeStruct((T, D), BF)],
        scratch_shapes=[pltpu.VMEM((D_FF, D), BF)] * 2 + [pltpu.SemaphoreType.DMA((len(FF_TILE_ROWS),))] * 2,
        compiler_params=_params(), name="gate_up")(h1, wn, w_gate, w_up)


def down_loss(act, w_down, h1, target, wf):
    first = NPAD + N_META

    def body(a_ref, w_ref, r_ref, t_hbm, wf_ref, d_ref, db_ref, l_ref, dw_ref, h_scr, t_ref, t_sem):
        i = pl.program_id(0)
        _zero_at_first(l_ref, dw_ref)
        head = pltpu.make_async_copy(t_hbm.at[pl.ds(0, RC - first)], t_ref.at[pl.ds(first, RC - first)], t_sem)
        rest = pltpu.make_async_copy(t_hbm.at[pl.ds(pl.multiple_of(jnp.maximum(i * RC - first, 0), 32), RC)], t_ref, t_sem)

        @pl.when(i == 0)
        def _():
            t_ref[0:first, :] = jnp.zeros((first, D), F32)
            head.start()

        @pl.when(i > 0)
        def _():
            rest.start()

        def tile(c0):
            cols = pl.ds(c0, 512)
            h_scr[:, cols] = r_ref[:, cols] + jnp.dot(a_ref[...], w_ref[:, cols], preferred_element_type=F32)

        _col_tiles(D, 512, tile)

        @pl.when(i == 0)
        def _():
            head.wait()

        @pl.when(i > 0)
        def _():
            rest.wait()

        for r in (0, HALF):
            h = h_scr[r:r + HALF, :]
            live = _rows((HALF, D), i * RC + r) >= first
            err = jnp.where(live, _rms(h, wf_ref[...]) - t_ref[r:r + HALF, :], 0.0)
            l_ref[...] += 0.5 * jnp.sum(jnp.sum(err * err, axis=1, keepdims=True) * (1.0 / D), axis=0, keepdims=True)
            dh, dw = _rms_bwd(err * (1.0 / D), h, wf_ref[...])
            dw_ref[...] += jnp.sum(dw, axis=0, keepdims=True)
            d_ref[r:r + HALF, :] = dh
            db_ref[r:r + HALF, :] = dh.astype(BF)

    return pl.pallas_call(
        body, grid=(T // RC,),
        in_specs=[_rows_spec(D_FF), _whole((D_FF, D)), _rows_spec(D), pl.BlockSpec(memory_space=pl.ANY), _vec(D)],
        out_specs=[_rows_spec(D), _rows_spec(D), _spec((1, 128), lambda i: (0, 0)), _vec(D)],
        out_shape=[jax.ShapeDtypeStruct((T, D), F32), jax.ShapeDtypeStruct((T, D), BF), jax.ShapeDtypeStruct((1, 128), F32),
                   jax.ShapeDtypeStruct((1, D), F32)],
        scratch_shapes=[pltpu.VMEM((RC, D), F32), pltpu.VMEM((RC, D), F32), pltpu.SemaphoreType.DMA],
        compiler_params=_params(), name="down_loss")(act, w_down, h1, target, wf)


def swiglu_bwd(dh2_b, w_down, gt, up, act, u2):
    tn = 256

    def body(d_hbm, u_hbm, w_ref, gt_ref, up_ref, act_ref, dg_ref, du_ref, gd_ref, gg_ref, gu_ref, d_ref, u_ref, d_sems, u_sems):
        chunks = list(range(0, T, RC))
        d_ready = _arriving(d_hbm, d_ref, d_sems, chunks, RC)
        u_ready = _arriving(u_hbm, u_ref, u_sems, chunks, RC)

        def rows(r0):
            part = pl.ds(r0, RC)
            d_ready(r0 // RC)
            dact = lax.dot_general(d_ref[part, :], w_ref[...], NT_DIMS, preferred_element_type=F32)
            gt_ = gt_ref[part, :].astype(F32)
            up_ = up_ref[part, :].astype(F32)
            sg = _sigmoid(gt_)
            dg_ref[part, :] = (dact * up_ * (sg * (1.0 + gt_ * (1.0 - sg)))).astype(BF)
            du_ref[part, :] = (dact * (gt_ * sg)).astype(BF)

        _col_tiles(T, RC, rows)
        for k in range(len(chunks)):
            u_ready(k)
        gd_ref[...] = lax.dot_general(act_ref[...], d_ref[...], TN_DIMS, preferred_element_type=F32).astype(BF)
        gg_ref[...] = lax.dot_general(dg_ref[...], u_ref[...], TN_DIMS, preferred_element_type=F32).astype(BF)
        gu_ref[...] = lax.dot_general(du_ref[...], u_ref[...], TN_DIMS, preferred_element_type=F32).astype(BF)

    cols = _spec((T, tn), lambda j: (0, j))
    wrow = _spec((tn, D), lambda j: (j, 0))
    big = jax.ShapeDtypeStruct((T, D_FF), BF)
    grad = jax.ShapeDtypeStruct((D_FF, D), BF)
    return pl.pallas_call(
        body, grid=(D_FF // tn,), in_specs=[ANY_SPEC, ANY_SPEC, wrow, cols, cols, cols],
        out_specs=[cols, cols, wrow, wrow, wrow], out_shape=[big, big, grad, grad, grad],
        scratch_shapes=[pltpu.VMEM((T, D), BF)] * 2 + [pltpu.SemaphoreType.DMA((T // RC,))] * 2,
        compiler_params=_params(), name="swiglu_bwd")(dh2_b, u2, w_down, gt, up, act)


def gate_up_bwd(dgt, dup, w_gate, w_up, h1, wn, dh2, after):
    def body(dg_ref, du_ref, wg_hbm, wu_hbm, h_ref, wn_ref, r_ref, _after, d_ref, db_ref, dw_ref, du_scr, wg_ref, wu_ref, wg_sems, wu_sems):
        gate_ready = _arriving(wg_hbm, wg_ref, wg_sems, FF_TILE_ROWS, FF_TILE)
        up_ready = _arriving(wu_hbm, wu_ref, wu_sems, FF_TILE_ROWS, FF_TILE)
        _zero_at_first(dw_ref)

        du_scr[...] = jnp.zeros_like(du_scr)

        def tile(c0):
            k = pl.ds(c0, FF_TILE)
            gate_ready(c0 // FF_TILE)
            up_ready(c0 // FF_TILE)
            du_scr[...] += (jnp.dot(dg_ref[:, k], wg_ref[k, :], preferred_element_type=F32)
                            + jnp.dot(du_ref[:, k], wu_ref[k, :], preferred_element_type=F32))

        _col_tiles(D_FF, FF_TILE, tile)
        for r in (0, HALF):
            dh, dw = _rms_bwd(du_scr[r:r + HALF, :], h_ref[r:r + HALF, :], wn_ref[...])
            dw_ref[...] += jnp.sum(dw, axis=0, keepdims=True)
            dh = dh + r_ref[r:r + HALF, :]
            d_ref[r:r + HALF, :] = dh
            db_ref[r:r + HALF, :] = dh.astype(BF)

    return pl.pallas_call(
        body, grid=(T // RC,),
        in_specs=[_rows_spec(D_FF), _rows_spec(D_FF), ANY_SPEC, ANY_SPEC, _rows_spec(D), _vec(D), _rows_spec(D), ANY_SPEC],
        out_specs=[_rows_spec(D), _rows_spec(D), _vec(D)],
        out_shape=[jax.ShapeDtypeStruct((T, D), F32), jax.ShapeDtypeStruct((T, D), BF), jax.ShapeDtypeStruct((1, D), F32)],
        scratch_shapes=[pltpu.VMEM((RC, D), F32)] + [pltpu.VMEM((D_FF, D), BF)] * 2 + [pltpu.SemaphoreType.DMA((len(FF_TILE_ROWS),))] * 2,
        compiler_params=_params(), name="gate_up_bwd")(dgt, dup, w_gate, w_up, h1, wn, dh2, after)


def _adamw(w, g, m, v):
    m = ADAM_B1 * m + (1.0 - ADAM_B1) * g
    v = ADAM_B2 * v + (1.0 - ADAM_B2) * (g * g)
    m_hat = m / (1.0 - ADAM_B1 ** ADAM_STEP)
    v_hat = v / (1.0 - ADAM_B2 ** ADAM_STEP)
    delta = -ADAM_LR * (m_hat / (jnp.sqrt(v_hat) + ADAM_EPS) + ADAM_WD * w)
    return delta, m, v


def adamw_shards(name, recvs, ws, ms, vs):
    n = len(ws)

    def body(*refs):
        ins, outs = refs[:4 * n], refs[4 * n:]
        for k in range(n):
            p_ref, w_ref, m_ref, v_ref = ins[k], ins[n + k], ins[2 * n + k], ins[3 * n + k]
            g = p_ref[0].astype(F32)
            for s in range(1, 8):
                g = g + p_ref[s].astype(F32)
            outs[4 * k][...] = g
            outs[4 * k + 1][...], outs[4 * k + 2][...], outs[4 * k + 3][...] = _adamw(w_ref[...], g, m_ref[...], v_ref[...])

    tiles = [_spec((w.shape[0] // 2, w.shape[1]), lambda i: (i, 0)) for w in ws]
    recv_tiles = [_spec((8, w.shape[0] // 2, w.shape[1]), lambda i: (0, i, 0)) for w in ws]
    res = pl.pallas_call(
        body, grid=(2,), in_specs=recv_tiles + tiles * 3,
        out_specs=[t for t in tiles for _ in range(4)],
        out_shape=[jax.ShapeDtypeStruct(w.shape, F32) for w in ws for _ in range(4)],
        compiler_params=_params(), name=name)(*recvs, *ws, *ms, *vs)
    return [list(res[4 * k:4 * k + 4]) for k in range(n)]


def adamw_w_in(recv, w, m, v):
    rows = 34
    per_row = D // 128

    def body(p_ref, w_ref, m_ref, v_ref, g_ref, d_ref, mo_ref, vo_ref):
        def chunk(c, carry):
            lines = pl.ds(pl.multiple_of(c * per_row * rows, 16), per_row * rows)
            g = p_ref[0, lines, :].astype(F32)
            for s in range(1, 8):
                g = g + p_ref[s, lines, :].astype(F32)
            g = g.reshape(rows, per_row, 128)
            part = pl.ds(c * rows, rows)
            g_ref[part] = g
            d_ref[part], mo_ref[part], vo_ref[part] = _adamw(w_ref[part], g, m_ref[part], v_ref[part])
            return carry

        lax.fori_loop(0, w.shape[0] // rows, chunk, 0)

    shape = jax.ShapeDtypeStruct(w.shape, F32)
    return pl.pallas_call(body, out_shape=[shape] * 4, compiler_params=_params(0), name="adamw_w_in")(recv, w, m, v)


def sum_slabs(recv):
    def body(p_ref, o_ref):
        g = p_ref[0]
        for s in range(1, 8):
            g = g + p_ref[s]
        o_ref[...] = g

    return pl.pallas_call(body, out_shape=jax.ShapeDtypeStruct(recv.shape[1:], F32), compiler_params=_params(0), name="sum_slabs")(recv)


SIMPLE = [("norm1_w", 1024), ("ssd_conv_b", 1536), ("ssd_dt_bias", 16), ("ssd_a_log", 16), ("ssd_d", 16), ("ssd_norm_w", 1024),
          ("lru_conv_b", 1024), ("lru_ba", 1024), ("lru_bx", 1024), ("lru_lambda", 1024), ("lru_norm_w", 1024), ("norm2_w", 1024),
          ("final_norm_w", 1024)]
SPECIAL = ["lru_wa", "lru_wx", "meta_tokens", "ssd_conv_w", "lru_conv_w"]
SM_ROWS = 176
SM_WA, SM_WX, SM_META, SM_SCW, SM_LCW, SM_LOSS = 14, 78, 142, 158, 166, 170


def _simple_rows():
    rows, r = {}, 0
    for name, n in SIMPLE:
        rows[name] = r
        r += -(-n // 1024)
    return rows


def adamw_small(sm, special_g, ws, ms, vs):
    rows = _simple_rows()
    ns, nx = len(SIMPLE), len(SPECIAL)

    def body(*refs):
        sm_ref = refs[0]
        gx = refs[1:1 + nx]
        wr = refs[1 + nx:1 + nx + ns + nx]
        mr = refs[1 + nx + ns + nx:1 + nx + 2 * (ns + nx)]
        vr = refs[1 + nx + 2 * (ns + nx):1 + nx + 3 * (ns + nx)]
        outs = refs[1 + nx + 3 * (ns + nx):]
        o = 0
        for k, (name, n) in enumerate(SIMPLE):
            r0 = rows[name]
            for c0 in range(0, n, 1024):
                wd = min(1024, n - c0)
                g = sm_ref[r0 + c0 // 1024:r0 + c0 // 1024 + 1, 0:wd]
                sl = (slice(None), slice(c0, c0 + wd))
                d, m2, v2 = _adamw(wr[k][sl], g, mr[k][sl], vr[k][sl])
                outs[o][sl] = g
                outs[o + 1][sl] = d
                outs[o + 2][sl] = m2
                outs[o + 3][sl] = v2
            o += 4
        for k in range(nx):
            d, m2, v2 = _adamw(wr[ns + k][...], gx[k][...], mr[ns + k][...], vr[ns + k][...])
            outs[o][...] = d
            outs[o + 1][...] = m2
            outs[o + 2][...] = v2
            o += 3

    out_shape = []
    for k in range(ns):
        out_shape += [jax.ShapeDtypeStruct(ws[k].shape, F32)] * 4
    for k in range(nx):
        out_shape += [jax.ShapeDtypeStruct(ws[ns + k].shape, F32)] * 3
    return pl.pallas_call(body, out_shape=out_shape, compiler_params=_params(0), name="adamw_small")(sm, *special_g, *ws, *ms, *vs)


def _place():
    return lax.axis_index("x"), lax.axis_index("y"), lax.axis_index("c")


def _index(px, py, pc):
    return 4 * px + 2 * py + pc


def all_gather(name, shards):
    n = len(shards)
    hbm = pl.BlockSpec(memory_space=pl.ANY)

    def body(*refs):
        ins, outs = refs[:n], refs[n:2 * n]
        send_sems, recv_sems, local_sems = refs[2 * n:]
        x, y, c = _place()
        me, sibling = (x, y, c), (x, y, 1 - c)
        chips = [(1 - x, y), (x, 1 - y), (1 - x, 1 - y)]

        def copy(i, k, block, to, src=None):
            dst = outs[i].at[_index(*block)]
            return pltpu.make_async_remote_copy(src_ref=dst if src is None else src, dst_ref=dst, send_sem=send_sems.at[7 * i + k],
                                                recv_sem=recv_sems.at[7 * i + k], device_id=to, device_id_type=MESH)

        mine = [pltpu.make_async_copy(ins[i], outs[i].at[_index(*me)], local_sems.at[i]) for i in range(n)]
        for cp in mine:
            cp.start()
        first = []
        for i in range(n):
            first += [copy(i, 1 + j, me, (*chip, c), src=ins[i]) for j, chip in enumerate(chips)]
            first.append(copy(i, 0, me, sibling, src=ins[i]))
        for cp in first:
            cp.start()
        passed = []
        for i in range(n):
            for j, chip in enumerate(chips):
                copy(i, 1 + j, (*chip, c), me).wait_recv()
                cp = copy(i, 4 + j, (*chip, c), sibling)
                cp.start()
                passed.append(cp)
        for i in range(n):
            copy(i, 0, sibling, me).wait_recv()
            for j, chip in enumerate(chips):
                copy(i, 4 + j, (*chip, 1 - c), me).wait_recv()
        for cp in first + passed:
            cp.wait_send()
        for cp in mine:
            cp.wait()

    return pl.pallas_call(
        body, in_specs=[hbm] * n, out_specs=[hbm] * n,
        out_shape=[jax.ShapeDtypeStruct((8,) + s.shape, s.dtype) for s in shards],
        scratch_shapes=[pltpu.SemaphoreType.DMA((7 * n,)), pltpu.SemaphoreType.DMA((7 * n,)), pltpu.SemaphoreType.DMA((n,))],
        name=name)(*shards)


HBM_SPEC = pl.BlockSpec(memory_space=pltpu.HBM)
SEM_SPEC = pl.BlockSpec(memory_space=pltpu.SEMAPHORE)
EFFECT = pltpu.SideEffectType.DATAFLOW_SIDE_EFFECTING


def _peers(x, y, c):
    return [((1 - x) if k & 4 else x, (1 - y) if k & 2 else y, (1 - c) if k & 1 else c) for k in range(1, 8)]


def _pieces(rows):
    for n in (4, 2):
        if rows % (16 * n) == 0:
            return [(r * (rows // n), rows // n) for r in range(n)]
    return [(0, rows)]


def _peer_copies(src, land, send_sems, recv_sems, k, peer, mine):
    block = src.at[_index(*peer)]
    return [pltpu.make_async_remote_copy(src_ref=block.at[pl.ds(r0, nr)], dst_ref=land.at[mine, pl.ds(r0, nr)], send_sem=send_sems.at[k],
                                         recv_sem=recv_sems.at[k], device_id=peer, device_id_type=MESH)
            for r0, nr in _pieces(block.shape[0])]


OWN = 7


def _own_copy(src, land, send_sems, mine):
    return pltpu.make_async_copy(src.at[mine], land.at[mine], send_sems.at[OWN])


def _start_slabs(src, land, send_sems, recv_sems, owners):
    x, y, c = _place()
    mine = _index(x, y, c)

    def if_owner(device, do):
        if len(owners) == 8:
            do()
        else:
            pl.when(functools.reduce(jnp.logical_or, [device == d for d in owners]))(do)

    per_peer = [_peer_copies(src, land, send_sems, recv_sems, k, peer, mine) for k, peer in enumerate(_peers(x, y, c))]
    for piece in zip(*per_peer):
        for cp, peer in zip(piece, _peers(x, y, c)):
            if_owner(_index(*peer), cp.start)
    if_owner(mine, _own_copy(src, land, send_sems, mine).start)


def copies_start(name, srcs, owners=range(8)):
    n = len(srcs)

    def body(*refs):
        ins, lands = refs[:n], refs[n:2 * n]
        sends, recvs = refs[2 * n:3 * n], refs[3 * n:4 * n]
        token = refs[-1]
        for i in range(n):
            _start_slabs(ins[i], lands[i], sends[i], recvs[i], owners)
        token[...] = jnp.zeros_like(token)

    res = pl.pallas_call(
        body, name=name,
        out_shape=([pltpu.SemaphoreType.DMA((8,))] * n + [pltpu.SemaphoreType.DMA((7,))] * n + [pltpu.HBM(s.shape, s.dtype) for s in srcs] * 2
                   + [jax.ShapeDtypeStruct((8, 128), F32)]),
        in_specs=[HBM_SPEC] * (2 * n),
        out_specs=[SEM_SPEC] * (2 * n) + [HBM_SPEC] * (2 * n) + [pl.BlockSpec(memory_space=pltpu.VMEM)],
        input_output_aliases={i: 2 * n + i for i in range(2 * n)},
        compiler_params=pltpu.CompilerParams(has_side_effects=EFFECT),
    )(*[pltpu.with_memory_space_constraint(s, pltpu.HBM) for s in srcs],
      *[pltpu.with_memory_space_constraint(lax.empty(s.shape, s.dtype), pltpu.HBM) for s in srcs])
    return [(res[i], res[n + i], res[2 * n + i], res[3 * n + i]) for i in range(n)], res[-1]


def copies_resume(name, state, src, owners):
    send_sems, recv_sems, _, zone = state

    def body(src_ref, land_ref, sends, recvs, src_out, land_out, token):
        _start_slabs(src_ref, land_ref, sends, recvs, owners)
        token[...] = jnp.zeros_like(token)

    res = pl.pallas_call(
        body, name=name,
        out_shape=[pltpu.HBM(src.shape, src.dtype), pltpu.HBM(zone.shape, zone.dtype), jax.ShapeDtypeStruct((8, 128), F32)],
        in_specs=[HBM_SPEC] * 2 + [SEM_SPEC] * 2,
        out_specs=[HBM_SPEC] * 2 + [pl.BlockSpec(memory_space=pltpu.VMEM)],
        input_output_aliases={0: 0, 1: 1},
        compiler_params=pltpu.CompilerParams(has_side_effects=EFFECT),
    )(pltpu.with_memory_space_constraint(src, pltpu.HBM), zone, send_sems, recv_sems)
    return (send_sems, recv_sems, res[0], res[1]), res[2]


def copies_wait(name, started, after):
    n = len(started)

    def body(*refs):
        ins, lands = refs[:n], refs[n:2 * n]
        sends, recvs = refs[2 * n:3 * n], refs[3 * n:4 * n]
        x, y, c = _place()
        mine = _index(x, y, c)
        for i in range(n):
            for k, peer in enumerate(_peers(x, y, c)):
                arrival = pltpu.make_async_remote_copy(src_ref=ins[i].at[mine], dst_ref=lands[i].at[_index(*peer)],
                                                       send_sem=sends[i].at[k], recv_sem=recvs[i].at[k], device_id=peer, device_id_type=MESH)
                arrival.wait_send()
                arrival.wait_recv()
            _own_copy(ins[i], lands[i], sends[i], mine).wait()

    srcs = [s[2] for s in started]
    lands = [s[3] for s in started]
    res = pl.pallas_call(
        body, name=name,
        out_shape=[pltpu.HBM(s.shape, s.dtype) for s in srcs] + [pltpu.HBM(z.shape, z.dtype) for z in lands],
        in_specs=[HBM_SPEC] * (2 * n) + [SEM_SPEC] * (2 * n) + [ANY_SPEC] * len(after),
        out_specs=[HBM_SPEC] * (2 * n),
        input_output_aliases={i: i for i in range(2 * n)},
        compiler_params=pltpu.CompilerParams(has_side_effects=EFFECT),
    )(*srcs, *lands, *[s[0] for s in started], *[s[1] for s in started], *after)
    return list(res[n:])


def _hop(src, land, send_sems, recv_sems, k, block, to):
    dst = land.at[_index(*block)]
    return pltpu.make_async_remote_copy(src_ref=dst if src is None else src, dst_ref=dst, send_sem=send_sems.at[k], recv_sem=recv_sems.at[k],
                                        device_id=to, device_id_type=MESH)


def _own_block(src, land, send_sems, mine):
    return pltpu.make_async_copy(src, land.at[mine], send_sems.at[4])


def _other_chips(x, y):
    return [(1 - x, y), (x, 1 - y), (1 - x, 1 - y)]


def gather_start(name, shards, through):
    n, m = len(shards), len(through)

    def body(*refs):
        ins, lands = refs[:n], refs[n:2 * n]
        sends, recvs = refs[2 * n + m:3 * n + m], refs[3 * n + m:4 * n + m]
        x, y, c = _place()
        for i in range(n):
            for j, chip in enumerate(_other_chips(x, y)):
                _hop(ins[i], lands[i], sends[i], recvs[i], 1 + j, (x, y, c), (*chip, c)).start()
            _hop(ins[i], lands[i], sends[i], recvs[i], 0, (x, y, c), (x, y, 1 - c)).start()
            _own_block(ins[i], lands[i], sends[i], _index(x, y, c)).start()

    own, passing = pltpu.SemaphoreType.DMA((5,)), pltpu.SemaphoreType.DMA((3,))
    zones = [jax.ShapeDtypeStruct((8,) + s.shape, s.dtype) for s in shards]
    res = pl.pallas_call(
        body, name=name,
        out_shape=([own] * (2 * n) + [passing] * (2 * n) + [pltpu.HBM(s.shape, s.dtype) for s in shards]
                   + [pltpu.HBM(z.shape, z.dtype) for z in zones] + [pltpu.HBM(t.shape, t.dtype) for t in through]),
        in_specs=[HBM_SPEC] * (2 * n + m),
        out_specs=[SEM_SPEC] * (4 * n) + [HBM_SPEC] * (2 * n + m),
        input_output_aliases={i: 4 * n + i for i in range(2 * n + m)},
        compiler_params=pltpu.CompilerParams(has_side_effects=EFFECT),
    )(*[pltpu.with_memory_space_constraint(s, pltpu.HBM) for s in shards],
      *[pltpu.with_memory_space_constraint(lax.empty(z.shape, z.dtype), pltpu.HBM) for z in zones],
      *[pltpu.with_memory_space_constraint(t, pltpu.HBM) for t in through])
    return [[res[4 * n + i], res[5 * n + i], res[i], res[n + i], res[2 * n + i], res[3 * n + i]] for i in range(n)], list(res[6 * n:])


def gather_stage(name, pass_on, finish, after):
    arrays = pass_on + finish
    n = len(arrays)

    def body(*refs):
        ins, lands = refs[:n], refs[n:2 * n]
        sems = [refs[(2 + q) * n:(3 + q) * n] for q in range(4)]
        x, y, c = _place()
        me, sibling = (x, y, c), (x, y, 1 - c)
        for i in range(len(pass_on)):
            send, recv, send_on, recv_on = (q[i] for q in sems)
            for j, chip in enumerate(_other_chips(x, y)):
                _hop(None, lands[i], send, recv, 1 + j, (*chip, c), me).wait_recv()
                _hop(None, lands[i], send_on, recv_on, j, (*chip, c), sibling).start()
        for i in range(len(pass_on), n):
            send, recv, send_on, recv_on = (q[i] for q in sems)
            _hop(ins[i], lands[i], send, recv, 0, sibling, me).wait_recv()
            for j, chip in enumerate(_other_chips(x, y)):
                _hop(None, lands[i], send_on, recv_on, j, (*chip, 1 - c), me).wait_recv()
            _hop(ins[i], lands[i], send, recv, 0, me, sibling).wait_send()
            _own_block(ins[i], lands[i], send, _index(*me)).wait()
            for j, chip in enumerate(_other_chips(x, y)):
                _hop(ins[i], lands[i], send, recv, 1 + j, me, (*chip, c)).wait_send()
                _hop(None, lands[i], send_on, recv_on, j, (*chip, c), sibling).wait_send()
        refs[-1][...] = jnp.zeros_like(refs[-1])

    res = pl.pallas_call(
        body, name=name,
        out_shape=([pltpu.HBM(a[0].shape, a[0].dtype) for a in arrays] + [pltpu.HBM(a[1].shape, a[1].dtype) for a in arrays]
                   + [jax.ShapeDtypeStruct((8, 128), F32)]),
        in_specs=[HBM_SPEC] * (2 * n) + [SEM_SPEC] * (4 * n) + [ANY_SPEC],
        out_specs=[HBM_SPEC] * (2 * n) + [pl.BlockSpec(memory_space=pltpu.VMEM)],
        input_output_aliases={i: i for i in range(2 * n)},
        compiler_params=pltpu.CompilerParams(has_side_effects=EFFECT),
    )(*[a[0] for a in arrays], *[a[1] for a in arrays], *[a[2 + q] for q in range(4) for a in arrays], after)
    for i, a in enumerate(arrays):
        a[0], a[1] = res[i], res[n + i]
    return [a[1] for a in finish], res[-1]


WEIGHTS = ["meta_tokens", "norm1_w", "w_in", "ssd_conv_w", "ssd_conv_b", "ssd_dt_bias", "ssd_a_log", "ssd_d", "ssd_norm_w", "lru_conv_w",
           "lru_conv_b", "lru_wa", "lru_ba", "lru_wx", "lru_bx", "lru_lambda", "lru_norm_w", "w_out", "norm2_w", "w_gate", "w_up", "w_down",
           "final_norm_w"]
BIG = ["w_in", "w_out", "w_gate", "w_up", "w_down"]
COLUMN_SHARDED = ["w_in", "w_gate", "w_up"]


def _pair_blocks(w):
    w = w.reshape(8, 2, 64, 64)
    z = jnp.zeros((8, 64, 64), w.dtype)
    return jnp.concatenate([jnp.concatenate([w[:, 0], z], axis=2), jnp.concatenate([z, w[:, 1]], axis=2)], axis=1)


def _unpair_blocks(w2):
    return jnp.stack([w2[:, :64, :64], w2[:, 64:, 64:]], axis=1).reshape(16, 64, 64)


def _per_group(v):
    return jnp.pad(v.reshape(2, 1, 8), ((0, 0), (0, 0), (0, 120)))


def _pad_cols(v, n):
    return jnp.pad(v, ((0, 0), (0, n - v.shape[1])))


def local_step(x, target, meta, ssd_cw, lru_cw, w_in, fetch, send, p):
    z120 = jnp.zeros((120, D), BF)
    w_dt = jnp.concatenate([w_in[2560:2568], z120, w_in[2568:2576], z120], axis=0)
    bias2, alog2, d2 = _per_group(p["ssd_dt_bias"]), _per_group(p["ssd_a_log"]), _per_group(p["ssd_d"])
    wa2 = _pair_blocks(p["lru_wa"]).astype(BF)
    wx2 = _pair_blocks(p["lru_wx"]).astype(BF)
    lru = (lru_cw, p["lru_conv_b"], wa2, p["lru_ba"], wx2, p["lru_bx"], p["lru_lambda"])

    h0 = jnp.concatenate([jnp.zeros((NPAD, D), F32), meta, x], axis=0)
    proj, dt_raw, u1 = in_proj(h0, p["norm1_w"], w_in, w_dt)
    yn_ssd, y_pre, h_prev = ssd_fwd(proj, dt_raw, ssd_cw, p["ssd_conv_b"], bias2, alog2, d2, p["ssd_norm_w"])
    _, moved = fetch([], yn_ssd)
    hseq, a = lru_fwd(proj, *lru, moved)
    (w_out,), _ = fetch(["w_out"], hseq)
    h1, cat = out_proj(yn_ssd, proj, hseq, p["lru_norm_w"], w_out, h0)
    (w_gate, w_up), _ = fetch(["w_gate", "w_up"], h1)
    gt, up, act, u2 = gate_up(h1, p["norm2_w"], w_gate, w_up)
    (w_down,), _ = fetch(["w_down"], act)
    dh2, dh2_b, loss, d_fnw = down_loss(act, w_down, h1, target, p["final_norm_w"])

    dgt, dup, g_down, g_gate, g_up = swiglu_bwd(dh2_b, w_down, gt, up, act, u2)
    sent = send({"w_down": g_down, "w_gate": g_gate, "w_up": g_up})
    dh1, dh1_b, d_n2 = gate_up_bwd(dgt, dup, w_gate, w_up, h1, p["norm2_w"], dh2, sent)
    sent = send({"w_out": weight_grad("dw_out", cat, dh1_b)})
    dyn, dh_out, dg_b, d_lnw = out_proj_bwd(dh1_b, w_out, proj, hseq, p["lru_norm_w"], sent)

    dxl_b, d_lcw, d_lcb, dwa2, d_ba, dwx2, d_bx, d_lam = lru_bwd(dh_out, a, hseq, proj, *lru)
    lru_row = 2576
    g_in = in_weight_grad("dw_in_lru", [dg_b, dxl_b], [lru_row, lru_row + LRU_W], u1)
    sent, g_in = send({"w_in": g_in}, [d for d in range(8) if d * (IN_COLS // 8) >= lru_row])
    dz_b, dxbc_b, ddt_b, dpar, d_snw, d_scw, d_scb = ssd_bwd(dyn, proj, dt_raw, ssd_cw, p["ssd_conv_b"], y_pre, h_prev, bias2, alog2, d2,
                                                             p["ssd_norm_w"], sent)
    sent = send({"w_in": in_weight_grad("dw_in_ssd", [dz_b, dxbc_b], [0, SSD_W], u1, ddt=ddt_b, into=g_in)})
    grad_x, d_meta, d_n1 = in_proj_bwd(dz_b, dg_b, dxl_b, dxbc_b, ddt_b, w_in, w_dt, h0, p["norm1_w"], dh1, sent)
    small = {"norm1_w": d_n1, "ssd_conv_b": d_scb, "ssd_dt_bias": dpar[:, 0, :8].reshape(1, 16), "ssd_a_log": dpar[:, 1, :8].reshape(1, 16),
             "ssd_d": dpar[:, 2, :8].reshape(1, 16), "ssd_norm_w": d_snw, "lru_conv_b": d_lcb, "lru_ba": d_ba, "lru_bx": d_bx,
             "lru_lambda": d_lam, "lru_norm_w": d_lnw, "norm2_w": d_n2, "final_norm_w": d_fnw,
             "lru_wa": _unpair_blocks(dwa2), "lru_wx": _unpair_blocks(dwx2), "meta_tokens": d_meta,
             "ssd_conv_w": d_scw, "lru_conv_w": d_lcw}
    return loss, grad_x, small


def _pack_small(small, loss):
    rows = [_pad_cols(small[name], -(-n // 1024) * 1024).reshape(-1, 1024) for name, n in SIMPLE]
    rows += [small["lru_wa"].reshape(64, 1024), small["lru_wx"].reshape(64, 1024), small["meta_tokens"],
             _pad_cols(small["ssd_conv_w"], 2048).reshape(8, 1024), small["lru_conv_w"], _pad_cols(loss[:, 0:1], 1024)]
    sm = jnp.concatenate(rows, axis=0)
    return jnp.pad(sm, ((0, SM_ROWS - sm.shape[0]), (0, 0)))


def _slabs(g):
    return g.reshape(8, g.shape[0] // 8, g.shape[1])


def _unslab(g):
    return g.reshape(8 * g.shape[1], g.shape[2])


def kernel(x, meta_tokens, norm1_w, w_in, ssd_conv_w, ssd_conv_b, ssd_dt_bias, ssd_a_log, ssd_d, ssd_norm_w, lru_conv_w, lru_conv_b, lru_wa, lru_ba, lru_wx, lru_bx, lru_lambda, lru_norm_w, w_out, norm2_w, w_gate, w_up, w_down, final_norm_w, loss_target, m_meta_tokens, m_norm1_w, m_w_in, m_ssd_conv_w, m_ssd_conv_b, m_ssd_dt_bias, m_ssd_a_log, m_ssd_d, m_ssd_norm_w, m_lru_conv_w, m_lru_conv_b, m_lru_wa, m_lru_ba, m_lru_wx, m_lru_bx, m_lru_lambda, m_lru_norm_w, m_w_out, m_norm2_w, m_w_gate, m_w_up, m_w_down, m_final_norm_w, v_meta_tokens, v_norm1_w, v_w_in, v_ssd_conv_w, v_ssd_conv_b, v_ssd_dt_bias, v_ssd_a_log, v_ssd_d, v_ssd_norm_w, v_lru_conv_w, v_lru_conv_b, v_lru_wa, v_lru_ba, v_lru_wx, v_lru_bx, v_lru_lambda, v_lru_norm_w, v_w_out, v_norm2_w, v_w_gate, v_w_up, v_w_down, v_final_norm_w):
    w = dict(meta_tokens=meta_tokens, norm1_w=norm1_w, w_in=w_in[0], ssd_conv_w=ssd_conv_w[0], ssd_conv_b=ssd_conv_b, ssd_dt_bias=ssd_dt_bias,
             ssd_a_log=ssd_a_log, ssd_d=ssd_d, ssd_norm_w=ssd_norm_w, lru_conv_w=lru_conv_w[0], lru_conv_b=lru_conv_b, lru_wa=lru_wa[0],
             lru_ba=lru_ba, lru_wx=lru_wx[0], lru_bx=lru_bx, lru_lambda=lru_lambda, lru_norm_w=lru_norm_w, w_out=w_out[0], norm2_w=norm2_w,
             w_gate=w_gate[0], w_up=w_up[0], w_down=w_down[0], final_norm_w=final_norm_w.reshape(1, D))
    m = dict(meta_tokens=m_meta_tokens, norm1_w=m_norm1_w, w_in=m_w_in[0], ssd_conv_w=m_ssd_conv_w[0], ssd_conv_b=m_ssd_conv_b,
             ssd_dt_bias=m_ssd_dt_bias, ssd_a_log=m_ssd_a_log, ssd_d=m_ssd_d, ssd_norm_w=m_ssd_norm_w, lru_conv_w=m_lru_conv_w[0],
             lru_conv_b=m_lru_conv_b, lru_wa=m_lru_wa[0], lru_ba=m_lru_ba, lru_wx=m_lru_wx[0], lru_bx=m_lru_bx, lru_lambda=m_lru_lambda,
             lru_norm_w=m_lru_norm_w, w_out=m_w_out[0], norm2_w=m_norm2_w, w_gate=m_w_gate[0], w_up=m_w_up[0], w_down=m_w_down[0],
             final_norm_w=m_final_norm_w.reshape(1, D))
    v = dict(meta_tokens=v_meta_tokens, norm1_w=v_norm1_w, w_in=v_w_in[0], ssd_conv_w=v_ssd_conv_w[0], ssd_conv_b=v_ssd_conv_b,
             ssd_dt_bias=v_ssd_dt_bias, ssd_a_log=v_ssd_a_log, ssd_d=v_ssd_d, ssd_norm_w=v_ssd_norm_w, lru_conv_w=v_lru_conv_w[0],
             lru_conv_b=v_lru_conv_b, lru_wa=v_lru_wa[0], lru_ba=v_lru_ba, lru_wx=v_lru_wx[0], lru_bx=v_lru_bx, lru_lambda=v_lru_lambda,
             lru_norm_w=v_lru_norm_w, w_out=v_w_out[0], norm2_w=v_norm2_w, w_gate=v_w_gate[0], w_up=v_w_up[0], w_down=v_w_down[0],
             final_norm_w=v_final_norm_w.reshape(1, D))
    shapes = dict(meta_tokens=meta_tokens.shape, norm1_w=norm1_w.shape, w_in=w_in.shape, ssd_conv_w=ssd_conv_w.shape,
                  ssd_conv_b=ssd_conv_b.shape, ssd_dt_bias=ssd_dt_bias.shape, ssd_a_log=ssd_a_log.shape, ssd_d=ssd_d.shape,
                  ssd_norm_w=ssd_norm_w.shape, lru_conv_w=lru_conv_w.shape, lru_conv_b=lru_conv_b.shape, lru_wa=lru_wa.shape,
                  lru_ba=lru_ba.shape, lru_wx=lru_wx.shape, lru_bx=lru_bx.shape, lru_lambda=lru_lambda.shape, lru_norm_w=lru_norm_w.shape,
                  w_out=w_out.shape, norm2_w=norm2_w.shape, w_gate=w_gate.shape, w_up=w_up.shape, w_down=w_down.shape,
                  final_norm_w=final_norm_w.shape)
    me = _index(*_place())
    for n in COLUMN_SHARDED:
        w[n], m[n], v[n] = w[n].T, m[n].T, v[n].T

    small_shard = jnp.concatenate([w["meta_tokens"], _pad_cols(w["ssd_conv_w"], 256).reshape(8, 128), w["lru_conv_w"],
                                   jnp.zeros((4, 128), F32)], axis=0)
    g_in, gs = all_gather("gather_w_in", [w["w_in"].astype(BF), small_shard])
    later = ["w_out", "w_gate", "w_up", "w_down"]
    started, (g_in, gs) = gather_start("gather_rest_start", [w[n].astype(BF) for n in later], [g_in, gs])
    started = dict(zip(later, started))
    passed_on = {None: ["w_out", "w_gate", "w_up"], "w_out": ["w_down"], "w_gate": [], "w_down": []}
    meta_full = gs[:, 0:16].transpose(1, 0, 2).reshape(N_META, D)
    ssd_cw = gs[:, 16:24].reshape(8, 4, 256)[:, :, :192].transpose(1, 0, 2).reshape(4, XBC)
    lru_cw = gs[:, 24:28].transpose(1, 0, 2).reshape(4, LRU_W)

    def fetch(names, after):
        first = names[0] if names else None
        got, zero = gather_stage("gather_" + (first + "_wait" if names else "pass_on"), [started[n] for n in passed_on[first]],
                                 [started[n] for n in names], after)
        return [_unslab(g) for g in got], zero

    in_flight = {}

    in_part = {}

    def send(grads, owners=None):
        names = list(grads)
        if names[0] in in_part:
            (name,) = names
            rest = [d for d in range(8) if d not in in_part[name]]
            in_flight[name], zero = copies_resume("grads_" + name + "_resume", in_flight[name], _slabs(grads[name]), rest)
            return zero
        st, zero = copies_start("grads_" + names[0] + "_start", [grads[n] if n == "small" else _slabs(grads[n]) for n in names],
                                range(8) if owners is None else owners)
        in_flight.update(zip(names, st))
        if owners is None:
            return zero
        in_part[names[0]] = owners
        return zero, st[0][2].reshape(grads[names[0]].shape)

    loss, grad_x, small = local_step(x[0], loss_target[0], meta_full, ssd_cw, lru_cw, _unslab(g_in), fetch, send, w)
    send({"small": _pack_small(small, loss).reshape(8, SM_ROWS // 8, 1024)})

    out = {}
    early = ["w_down", "w_gate", "w_up", "w_out"]
    recv = dict(zip(early, copies_wait("grads_early_wait", [in_flight[n] for n in early], [in_flight["small"][2]])))
    for pair in (early[:2], early[2:]):
        done = adamw_shards("adamw_" + pair[0], [recv[n] for n in pair], [w[n] for n in pair], [m[n] for n in pair], [v[n] for n in pair])
        out.update(zip(pair, done))
    recv_in, recv_small = copies_wait("grads_late_wait", [in_flight["w_in"], in_flight["small"]], [out[n][0] for n in early])
    def lines(a):
        return jnp.transpose(a.reshape(D // 128, 128, IN_COLS // 8), (2, 0, 1))

    out["w_in"] = [jnp.transpose(o, (1, 2, 0)).reshape(D, IN_COLS // 8) for o in adamw_w_in(recv_in, lines(w_in), lines(m_w_in), lines(v_w_in))]
    for n in ("w_gate", "w_up"):
        out[n] = [o.T for o in out[n]]
    sm = all_gather("gather_small_grads", [sum_slabs(recv_small)])[0].reshape(SM_ROWS, 1024)
    special_g =[sm[SM_WA:SM_WA + 64].reshape(16, 64, 64), sm[SM_WX:SM_WX + 64].reshape(16, 64, 64),
                 lax.dynamic_slice(sm[SM_META:SM_META + 16], (0, 128 * me), (16, 128)),
                 lax.dynamic_slice(sm[SM_SCW:SM_SCW + 8].reshape(4, 2048), (0, 192 * me), (4, 192)),
                 lax.dynamic_slice(sm[SM_LCW:SM_LCW + 4], (0, 128 * me), (4, 128))]
    names = [n for n, _ in SIMPLE] + SPECIAL
    res = adamw_small(sm, special_g, [w[n] for n in names], [m[n] for n in names], [v[n] for n in names])
    for k, (n, _) in enumerate(SIMPLE):
        out[n] = res[4 * k:4 * k + 4]
    for k, n in enumerate(SPECIAL):
        o = 4 * len(SIMPLE) + 3 * k
        out[n] = [special_g[k]] + list(res[o:o + 3])
    loss_total = sm[SM_LOSS, 0]
    flat = [loss_total, grad_x[None]]
    for k in range(4):
        flat += [out[n][k].reshape(shapes[n]) for n in WEIGHTS]
    return tuple(flat)
```

```python
import math

import jax
import jax.numpy as jnp
from jax import lax
from jax.experimental import pallas as pl
from jax.experimental.pallas import tpu as pltpu

F32 = jnp.float32
BF = jnp.bfloat16

D = 1024
SEQ = 2048
N_META = 16
Q = 128
NPAD = 112
T = NPAD + N_META + SEQ
NCH = T // Q
RC = 544
D_FF = 2816
SSD_W = 1024
LRU_W = 1024
XBC = 1536
IN_COLS = 4624
PZ, PG, PXL, PXBC = 0, 1024, 2048, 3072
NP_IN = 4608
EPS = 1e-6
LRU_C = 8.0
VMEM_LIMIT = 56 * 1024 * 1024

ADAM_LR, ADAM_B1, ADAM_B2, ADAM_EPS, ADAM_WD, ADAM_STEP = 0.001, 0.9, 0.999, 1e-08, 0.01, 10

NT_DIMS = (((1,), (1,)), ((), ()))
TN_DIMS = (((0,), (0,)), ((), ()))
MESH = pl.DeviceIdType.MESH


def _params(n_grid=1, limit=VMEM_LIMIT):
    return pltpu.CompilerParams(dimension_semantics=("arbitrary",) * n_grid, vmem_limit_bytes=limit)


def _spec(shape, imap, single=False):
    if single:
        return pl.BlockSpec(shape, imap, pipeline_mode=pl.Buffered(1))
    return pl.BlockSpec(shape, imap)


def _sigmoid(x):
    return 0.5 * jnp.tanh(0.5 * x) + 0.5


def _sigmoid_gate(x):
    return 1.0 / (1.0 + jnp.exp(-x))


def _softplus(x):
    return jnp.maximum(x, 0.0) + jnp.log(1.0 + jnp.exp(-jnp.abs(x)))


def _rms_stats(h):
    return lax.rsqrt(jnp.mean(h * h, axis=-1, keepdims=True) + EPS)


def _rms(h, w):
    return (h * _rms_stats(h)) * w


def _rms_bwd(du, h, w):
    r = _rms_stats(h)
    n = h * r
    dn = du * w
    dh = r * (dn - n * jnp.mean(dn * n, axis=-1, keepdims=True))
    return dh, du * n


_G0 = math.sqrt(2.0 / math.pi)


def _gelu(x):
    return 0.5 * x * (1.0 + jnp.tanh(_G0 * (x + 0.044715 * (x * x * x))))


def _gelu_grad(x):
    t = jnp.tanh(_G0 * (x + 0.044715 * (x * x * x)))
    return 0.5 * (1.0 + t) + 0.5 * x * (1.0 - t * t) * (_G0 * (1.0 + 3.0 * 0.044715 * (x * x)))


def _rows(shape, r0=0):
    return lax.broadcasted_iota(jnp.int32, shape, 0) + r0


def _lanes(shape):
    return lax.broadcasted_iota(jnp.int32, shape, 1)


HALO = 8


def _fill_padded(pad_ref, x_ref):
    pad_ref[0:HALO, :] = jnp.zeros((HALO, pad_ref.shape[1]), F32)
    pad_ref[T + HALO:T + 2 * HALO, :] = jnp.zeros((HALO, pad_ref.shape[1]), F32)

    def step(c, carry):
        r0 = pl.multiple_of(c * Q, Q)
        pad_ref[pl.ds(r0 + HALO, Q), :] = x_ref[pl.ds(r0, Q), :].astype(F32)
        return carry

    lax.fori_loop(0, NCH, step, 0)


def _back(pad_ref, r0):
    win = pad_ref[pl.ds(r0, Q + HALO), :]
    return lambda s: win[HALO:, :] if s == 0 else pltpu.roll(win, s, axis=0)[HALO:, :]


def _ahead(pad_ref, r0):
    win = pad_ref[pl.ds(r0 + HALO, Q + HALO), :]
    return lambda s: win[:Q, :] if s == 0 else pltpu.roll(win, Q + HALO - s, axis=0)[:Q, :]


def _conv(back, w, b):
    y = b + w[3:4, :] * back(0)
    for k in range(3):
        y = y + w[k:k + 1, :] * back(3 - k)
    return y


def _conv_bwd_x(ahead, w):
    dx = w[3:4, :] * ahead(0)
    for k in range(3):
        dx = dx + w[k:k + 1, :] * ahead(3 - k)
    return dx


def _conv_bwd_w(dy, back):
    dws = [jnp.sum(dy * back(3 - k), axis=0, keepdims=True) for k in range(4)]
    return jnp.concatenate(dws, axis=0), jnp.sum(dy, axis=0, keepdims=True)


def _chunks(fn, unrolled=False):
    if unrolled:
        for c in range(NCH):
            fn(c * Q)
        return

    def step(c, carry):
        fn(pl.multiple_of(c * Q, Q))
        return carry

    lax.fori_loop(0, NCH, step, 0)


HALF = RC // 2


def _col_tiles(n, tn, fn):
    def step(j, carry):
        fn(pl.multiple_of(j * tn, tn))
        return carry

    lax.fori_loop(0, n // tn, step, 0)


def _rows_spec(cols, block_col=0):
    return _spec((RC, cols), lambda i: (i, block_col))


def _whole(shape):
    return _spec(shape, lambda i: tuple(0 for _ in shape), single=True)


def _vec(cols):
    return _spec((1, cols), lambda i: (0, 0))


def _zero_at_first(*refs):
    @pl.when(pl.program_id(0) == 0)
    def _():
        for r in refs:
            r[...] = jnp.zeros_like(r)


ANY_SPEC = pl.BlockSpec(memory_space=pl.ANY)


def _arriving(src, dst, sems, starts, rows):
    n, ahead = len(starts), 2
    first = pl.program_id(0) == 0

    def piece(k):
        r0 = starts[0]
        for j in range(1, n):
            r0 = jnp.where(k == j, starts[j], r0)
        at = pl.ds(pl.multiple_of(r0, 16), rows)
        return pltpu.make_async_copy(src.at[at], dst.at[at], sems.at[k])

    @pl.when(first)
    def _():
        for k in range(min(ahead, n)):
            piece(k).start()

    def ready(k):
        k = jnp.asarray(k, jnp.int32)

        @pl.when(first)
        def _():
            piece(k).wait()

            @pl.when(k + ahead < n)
            def _():
                piece(k + ahead).start()

    return ready


IN_RUNS = ((PZ, 0, 1024), (PXBC, 1024, XBC), (PG, 2576, 2048))
IN_TILE = 512
IN_TILE_ROWS = [wrow + IN_TILE * j for _, wrow, width in IN_RUNS for j in range(width // IN_TILE)]


def _in_tiles(fn):
    done = 0
    for pcol, wrow, width in IN_RUNS:
        def step(j, carry, pcol=pcol, wrow=wrow, done=done):
            fn(pl.multiple_of(pcol + j * IN_TILE, IN_TILE), pl.multiple_of(wrow + j * IN_TILE, 16), done + j)
            return carry

        lax.fori_loop(0, width // IN_TILE, step, 0)
        done += width // IN_TILE


def in_proj(x, meta, wn, w_shards):
    first = NPAD + N_META
    steps = T // RC
    shard = IN_COLS // 8

    def body(x_hbm, meta_ref, wn_ref, g_hbm, o_ref, dt_ref, u_ref, h_ref, wt_hbm, wdt_ref, raw, w_ref, h_scr, g_sems, h_sems, out_sem):
        i = pl.program_id(0)
        slot = i % 2
        shards = [pltpu.make_async_copy(g_hbm.at[j], raw.at[j], g_sems.at[j]) for j in range(8)]
        head = pltpu.make_async_copy(x_hbm.at[pl.ds(0, RC - first)], h_scr.at[0, pl.ds(first, RC - first)], h_sems.at[0])
        put_back = pltpu.make_async_copy(w_ref, wt_hbm, out_sem)

        def rows_of(step):
            return pltpu.make_async_copy(x_hbm.at[pl.ds(pl.multiple_of(step * RC - first, 32), RC)], h_scr.at[step % 2], h_sems.at[step % 2])

        @pl.when(i == 0)
        def _():
            for cp in shards:
                cp.start()
            head.start()
            h_scr[0, 0:NPAD, :] = jnp.zeros((NPAD, D), F32)
            h_scr[0, NPAD:first, :] = meta_ref[...]

        @pl.when(i + 1 < steps)
        def _():
            rows_of(i + 1).start()

        @pl.when(i == 0)
        def _():
            head.wait()

        @pl.when(i > 0)
        def _():
            rows_of(i).wait()

        h_ref[...] = h_scr[slot]
        for r in (0, HALF):
            u_ref[r:r + HALF, :] = _rms(h_scr[slot, r:r + HALF, :], wn_ref[...]).astype(BF)

        @pl.when(i == 0)
        def _():
            for j, cp in enumerate(shards):
                cp.wait()
                w_ref[shard * j:shard * (j + 1), :] = raw[j]
            put_back.start()
            wdt_ref[...] = jnp.zeros_like(wdt_ref)
            for g in range(2):
                wdt_ref[128 * g:128 * g + 8, :] = w_ref[2560 + 8 * g:2568 + 8 * g, :]

        def tile(pcol, wrow, k):
            o_ref[:, pl.ds(pcol, IN_TILE)] = lax.dot_general(u_ref[...], w_ref[pl.ds(wrow, IN_TILE), :], NT_DIMS,
                                                             preferred_element_type=F32).astype(BF)

        _in_tiles(tile)
        dt_ref[...] = lax.dot_general(u_ref[...], wdt_ref[...], NT_DIMS, preferred_element_type=F32)

        @pl.when(i == steps - 1)
        def _():
            put_back.wait()

    return pl.pallas_call(
        body, grid=(steps,), in_specs=[ANY_SPEC, _spec((N_META, D), lambda i: (0, 0)), _vec(D), ANY_SPEC],
        out_specs=[_rows_spec(NP_IN), _rows_spec(256), _rows_spec(D), _rows_spec(D), ANY_SPEC, _spec((256, D), lambda i: (0, 0))],
        out_shape=[jax.ShapeDtypeStruct((T, NP_IN), BF), jax.ShapeDtypeStruct((T, 256), F32), jax.ShapeDtypeStruct((T, D), BF),
                   jax.ShapeDtypeStruct((T, D), F32), jax.ShapeDtypeStruct((IN_COLS, D), BF), jax.ShapeDtypeStruct((256, D), BF)],
        scratch_shapes=[pltpu.VMEM((8, shard, D), BF), pltpu.VMEM((IN_COLS, D), BF), pltpu.VMEM((2, RC, D), F32),
                        pltpu.SemaphoreType.DMA((8,)), pltpu.SemaphoreType.DMA((2,)), pltpu.SemaphoreType.DMA],
        compiler_params=_params(), name="in_proj")(x, meta, wn, w_shards)


def out_proj(yn_ssd, proj, hseq, lru_nw, w_out, h0):
    def body(y_ref, g_ref, h_ref, wn_ref, w_ref, r_ref, o_ref, cat_ref):
        cat_ref[:, 0:SSD_W] = y_ref[...]
        for r in (0, HALF):
            y = _gelu(g_ref[r:r + HALF, :].astype(F32)) * h_ref[r:r + HALF, :]
            cat_ref[r:r + HALF, SSD_W:] = _rms(y, wn_ref[...]).astype(BF)

        def tile(c0):
            o_ref[:, pl.ds(c0, 512)] = r_ref[:, pl.ds(c0, 512)] + jnp.dot(cat_ref[...], w_ref[:, pl.ds(c0, 512)], preferred_element_type=F32)

        _col_tiles(D, 512, tile)

    return pl.pallas_call(
        body, grid=(T // RC,),
        in_specs=[_rows_spec(SSD_W), _rows_spec(LRU_W, PG // LRU_W), _rows_spec(LRU_W), _vec(LRU_W), _whole((SSD_W + LRU_W, D)), _rows_spec(D)],
        out_specs=[_rows_spec(D), _rows_spec(SSD_W + LRU_W)],
        out_shape=[jax.ShapeDtypeStruct((T, D), F32), jax.ShapeDtypeStruct((T, SSD_W + LRU_W), BF)],
        compiler_params=_params(), name="out_proj")(yn_ssd, proj, hseq, lru_nw, w_out, h0)


def out_proj_bwd(dh1_b, w_out, proj, hseq, lru_nw, after):
    def body(d_ref, w_ref, g_ref, h_ref, wn_ref, _after, dy_ref, dh_ref, dg_ref, dw_ref, dl_scr):
        _zero_at_first(dw_ref)

        def tile(c0):
            dy_ref[:, pl.ds(c0, 512)] = lax.dot_general(d_ref[...], w_ref[pl.ds(c0, 512), :], NT_DIMS, preferred_element_type=F32)
            dl_scr[:, pl.ds(c0, 512)] = lax.dot_general(d_ref[...], w_ref[pl.ds(SSD_W + c0, 512), :], NT_DIMS, preferred_element_type=F32)

        _col_tiles(SSD_W, 512, tile)

        for r in (0, HALF):
            g = g_ref[r:r + HALF, :].astype(F32)
            h = h_ref[r:r + HALF, :]
            ge = _gelu(g)
            dy, dw = _rms_bwd(dl_scr[r:r + HALF, :], ge * h, wn_ref[...])
            dw_ref[...] += jnp.sum(dw, axis=0, keepdims=True)
            dh_ref[r:r + HALF, :] = dy * ge
            dg_ref[r:r + HALF, :] = (dy * h * _gelu_grad(g)).astype(BF)

    return pl.pallas_call(
        body, grid=(T // RC,),
        in_specs=[_rows_spec(D), _whole((SSD_W + LRU_W, D)), _rows_spec(LRU_W, PG // LRU_W), _rows_spec(LRU_W), _vec(LRU_W), ANY_SPEC],
        out_specs=[_rows_spec(SSD_W), _rows_spec(LRU_W), _rows_spec(LRU_W), _vec(LRU_W)],
        out_shape=[jax.ShapeDtypeStruct((T, SSD_W), F32), jax.ShapeDtypeStruct((T, LRU_W), F32), jax.ShapeDtypeStruct((T, LRU_W), BF),
                   jax.ShapeDtypeStruct((1, LRU_W), F32)],
        scratch_shapes=[pltpu.VMEM((RC, LRU_W), F32)],
        compiler_params=_params(), name="out_proj_bwd")(dh1_b, w_out, proj, hseq, lru_nw, after)


def in_proj_bwd(dz, dg, dxl, dxbc, ddt, w_t, w_dt, h0, wn, dh1, after):
    first = NPAD + N_META

    def body(dz_ref, dg_ref, dxl_ref, dxbc_ref, ddt_ref, w_hbm, wdt_ref, h_ref, wn_ref, r_ref, _after, gx_hbm, meta_ref, dw_ref, du_scr, o_ref, sem,
             w_ref, w_sems):
        i = pl.program_id(0)
        ready = _arriving(w_hbm, w_ref, w_sems, IN_TILE_ROWS, IN_TILE)
        _zero_at_first(dw_ref)
        du_scr[...] = jnp.dot(ddt_ref[...], wdt_ref[...], preferred_element_type=F32)
        done = 0
        for d_ref, wrow, width in ((dz_ref, 0, 1024), (dxbc_ref, 1024, XBC), (dg_ref, 2576, 1024), (dxl_ref, 3600, 1024)):
            def step(j, carry, d_ref=d_ref, wrow=wrow, done=done):
                c0 = pl.multiple_of(j * IN_TILE, IN_TILE)
                ready(done + j)
                du_scr[...] += jnp.dot(d_ref[:, pl.ds(c0, IN_TILE)], w_ref[pl.ds(pl.multiple_of(wrow + c0, 16), IN_TILE), :],
                                       preferred_element_type=F32)
                return carry

            lax.fori_loop(0, width // IN_TILE, step, 0)
            done += width // IN_TILE
        for r in (0, HALF):
            dh, dw = _rms_bwd(du_scr[r:r + HALF, :], h_ref[r:r + HALF, :], wn_ref[...])
            dw_ref[...] += jnp.sum(dw, axis=0, keepdims=True)
            o_ref[r:r + HALF, :] = dh + r_ref[r:r + HALF, :]

        @pl.when(i == 0)
        def _():
            meta_ref[...] = o_ref[NPAD:first, :]
            head = pltpu.make_async_copy(o_ref.at[pl.ds(first, RC - first)], gx_hbm.at[pl.ds(0, RC - first)], sem)
            head.start()
            head.wait()

        @pl.when(i > 0)
        def _():
            rest = pltpu.make_async_copy(o_ref, gx_hbm.at[pl.ds(pl.multiple_of(i * RC - first, 32), RC)], sem)
            rest.start()
            rest.wait()

    return pl.pallas_call(
        body, grid=(T // RC,),
        in_specs=[_rows_spec(SSD_W), _rows_spec(LRU_W), _rows_spec(LRU_W), _rows_spec(XBC), _rows_spec(256), ANY_SPEC,
                  _whole((256, D)), _rows_spec(D), _vec(D), _rows_spec(D), ANY_SPEC],
        out_specs=[ANY_SPEC, _spec((N_META, D), lambda i: (0, 0)), _vec(D)],
        out_shape=[jax.ShapeDtypeStruct((SEQ, D), F32), jax.ShapeDtypeStruct((N_META, D), F32), jax.ShapeDtypeStruct((1, D), F32)],
        scratch_shapes=[pltpu.VMEM((RC, D), F32), pltpu.VMEM((RC, D), F32), pltpu.SemaphoreType.DMA,
                        pltpu.VMEM((IN_COLS, D), BF), pltpu.SemaphoreType.DMA((len(IN_TILE_ROWS),))],
        compiler_params=_params(), name="in_proj_bwd")(dz, dg, dxl, dxbc, ddt, w_t, w_dt, h0, wn, dh1, after)


GRAD_TILE = 256


def weight_grad(name, a, u1):
    tm = GRAD_TILE

    def body(a_ref, u_ref, o_ref):
        o_ref[...] = lax.dot_general(a_ref[...], u_ref[...], TN_DIMS, preferred_element_type=F32).astype(BF)

    return pl.pallas_call(
        body, grid=(a.shape[1] // tm,),
        in_specs=[_spec((T, tm), lambda j: (0, j)), _spec((T, D), lambda j: (0, 0), single=True)],
        out_specs=_spec((tm, D), lambda j: (j, 0)),
        out_shape=jax.ShapeDtypeStruct((a.shape[1], D), BF),
        compiler_params=_params(), name=name)(a, u1)


def in_weight_grad(parts, first_rows, ddt, u1):
    tm = GRAD_TILE
    per_row = D // 128
    dt_row, dt_lines = 2560, 8 * per_row
    parts = list(parts) + [ddt]
    first_rows = list(first_rows) + [dt_row]
    tiles = [p.shape[1] // tm for p in parts]
    starts = [sum(tiles[:k]) for k in range(len(parts))]
    last = sum(tiles) - 1

    def body(*refs):
        a_refs, u_ref = refs[:len(parts)], refs[len(parts)]
        o_hbm, mix_scr, stage, sems = refs[len(parts) + 1:]
        step = pl.program_id(0)
        slot = step % 2
        line0 = 0
        for a_ref, start, n, first in zip(a_refs, starts, tiles, first_rows):
            here = (step >= start) & (step < start + n)
            line0 = jnp.where(here, per_row * (first + tm * (step - start)), line0)

            @pl.when(here)
            def _(a_ref=a_ref):
                res = lax.dot_general(a_ref[...], u_ref[...], TN_DIMS, preferred_element_type=F32)
                for q in range(per_row):
                    mix_scr[pl.ds(q, tm, stride=per_row), :] = res[:, 128 * q:128 * q + 128]

        def tile_copy(of_slot, to):
            return pltpu.make_async_copy(stage.at[of_slot], o_hbm.at[pl.ds(to, per_row * tm)], sems.at[of_slot])

        @pl.when(step >= 2)
        def _():
            tile_copy(slot, 0).wait()

        stage[slot] = mix_scr[...].astype(BF)

        @pl.when(step < last)
        def _():
            tile_copy(slot, pl.multiple_of(line0, 128)).start()

        @pl.when(step == last)
        def _():
            halves = [pltpu.make_async_copy(stage.at[slot, pl.ds(128 * per_row * k, dt_lines)],
                                            o_hbm.at[pl.ds(per_row * (dt_row + 8 * k), dt_lines)], sems.at[2 + k]) for k in range(2)]
            for cp in halves:
                cp.start()
            tile_copy(1 - slot, 0).wait()
            for cp in halves:
                cp.wait()

    def tile_of(start, n):
        return lambda j: (0, jnp.clip(j - start, 0, n - 1))

    return pl.pallas_call(
        body, grid=(last + 1,),
        in_specs=[_spec((T, tm), tile_of(s, n)) for s, n in zip(starts, tiles)] + [_spec((T, D), lambda j: (0, 0), single=True)],
        out_specs=ANY_SPEC,
        out_shape=jax.ShapeDtypeStruct((per_row * IN_COLS, 128), BF),
        scratch_shapes=[pltpu.VMEM((per_row * tm, 128), F32), pltpu.VMEM((2, per_row * tm, 128), BF), pltpu.SemaphoreType.DMA((4,))],
        compiler_params=_params(), name="dw_in")(*parts, u1)


def _ssd_chunk_common(row0, dt_ref, b_ref, c_ref, bias, a_neg):
    shape = (Q, Q)
    lane = _lanes(shape)
    sub = _rows(shape)
    live = (_rows(shape, row0) >= NPAD) & (lane < 8)
    dtr = dt_ref[:, :]
    dt = jnp.where(live, _softplus(dtr + bias), 0.0)
    d_a = dt * a_neg
    tri = (sub >= lane).astype(F32)
    cs = jnp.dot(tri, d_a, precision=lax.Precision.HIGHEST, preferred_element_type=F32)
    cs_t = cs.T
    b_f = b_ref[:, :]
    bc = b_f.astype(BF)
    cc = c_ref[:, :].astype(BF)
    cb = lax.dot_general(cc, bc, NT_DIMS, preferred_element_type=F32)
    cs_last = cs[Q - 1:Q, :]
    return dict(lane=lane, sub=sub, live=live, dtr=dtr, dt=dt, cs=cs, cs_t=cs_t, bc=bc, cc=cc, cb=cb, bc_t=b_f.T.astype(BF),
                ecs=jnp.exp(cs), dsm=jnp.exp(cs_last - cs), gam=jnp.exp(cs_last))


def _pair(lane_even, mat, j):
    return jnp.where(lane_even, mat[:, j:j + 1], mat[:, j + 1:j + 2])


def _pair_row(lane_even, mat, j):
    return jnp.where(lane_even[0:1, :], mat[:, j:j + 1], mat[:, j + 1:j + 2])


def _head_decay(cm, j):
    seg = cm["cs"][:, j:j + 1] - cm["cs_t"][j:j + 1, :]
    return jnp.exp(jnp.where(cm["sub"] >= cm["lane"], seg, -jnp.inf))


def _head_decay_t(cm, j):
    seg = cm["cs_t"][j:j + 1, :] - cm["cs"][:, j:j + 1]
    return jnp.exp(jnp.where(cm["lane"] >= cm["sub"], seg, -jnp.inf))


def _conv_window(raw_ref, halo_ref, pad_scr):
    pad_scr[0:HALO, :] = halo_ref[...].astype(F32)[halo_ref.shape[0] - HALO:, :]
    pad_scr[HALO:HALO + Q, :] = raw_ref[...].astype(F32)
    win = pad_scr[...]
    return lambda s: win[HALO:, :] if s == 0 else pltpu.roll(win, s, axis=0)[HALO:, :]


def _xbc_cols(g):
    return slice(512 * g, 512 * g + 512), slice(SSD_W + 128 * g, SSD_W + 128 * g + 128), slice(SSD_W + 256 + 128 * g, SSD_W + 384 + 128 * g)


def ssd_fwd(proj, dt_raw, conv_w, conv_b, dt_bias2, a_log2, d2, norm_w):
    def body(raw_ref, halo_ref, dt_all, z_all, cw_ref, cb_ref, bias_all, alog_all, d_all, nw_all, yn_all, y_all, hp_all,
             h_all, pad_scr, act_scr):
        @pl.when(pl.program_id(0) == 0)
        def _():
            h_all[...] = jnp.zeros_like(h_all)

        pre = _conv(_conv_window(raw_ref, halo_ref, pad_scr), cw_ref[...], cb_ref[...])
        act_scr[...] = pre * _sigmoid(pre)
        for g in range(2):
            wide, thin = slice(512 * g, 512 * g + 512), slice(128 * g, 128 * g + 128)
            xs, bs, cs = _xbc_cols(g)
            group(act_scr.at[:, xs], act_scr.at[:, bs], act_scr.at[:, cs], dt_all.at[:, thin], z_all.at[:, wide], bias_all.at[g],
                  alog_all.at[g], d_all.at[g], nw_all.at[:, wide], yn_all.at[:, wide], y_all.at[:, wide], hp_all.at[g, 0], h_all.at[g])

    def group(x_ref, b_ref, c_ref, dt_ref, z_ref, bias_ref, alog_ref, d_ref, nw_ref, yn_ref, y_ref, hp_ref, h_scr):
        bias = bias_ref[...]
        a_neg = -jnp.exp(alog_ref[...])
        dsk = d_ref[...]
        cm = _ssd_chunk_common(pl.program_id(0) * Q, dt_ref, b_ref, c_ref, bias, a_neg)
        lane_even = cm["lane"] < 64
        for p in range(4):
            je, jo = 2 * p, 2 * p + 1
            xp = x_ref[:, 128 * p:128 * p + 128]
            xdt = xp * _pair(lane_even, cm["dt"], je)
            xdt_b = xdt.astype(BF)
            m_e = (cm["cb"] * _head_decay(cm, je)).astype(BF)
            m_o = (cm["cb"] * _head_decay(cm, jo)).astype(BF)
            zero = jnp.zeros_like(xdt_b)
            yd = (jnp.dot(m_e, jnp.where(lane_even, xdt_b, zero), preferred_element_type=F32)
                  + jnp.dot(m_o, jnp.where(lane_even, zero, xdt_b), preferred_element_type=F32))
            hp = h_scr[p]
            hp_ref[p] = hp
            yo = jnp.dot(cm["cc"], hp.astype(BF), preferred_element_type=F32) * _pair(lane_even, cm["ecs"], je)
            y_ref[:, 128 * p:128 * p + 128] = yd + yo + xp * _pair_row(lane_even, dsk, je)
            st = jnp.dot(cm["bc_t"], (xdt * _pair(lane_even, cm["dsm"], je)).astype(BF), preferred_element_type=F32)
            h_scr[p] = hp * _pair_row(lane_even, cm["gam"], je) + st
        zc = z_ref[:, :].astype(F32)
        gated = y_ref[:, :] * (zc * _sigmoid(zc))
        yn_ref[:, :] = _rms(gated, nw_ref[...]).astype(BF)

    par = _spec((2, 1, 128), lambda c: (0, 0, 0))
    wide = _spec((Q, SSD_W), lambda c: (c, 0))
    xbc = PXBC // XBC
    halo = 2 * HALO
    return pl.pallas_call(
        body, grid=(NCH,),
        in_specs=[_spec((Q, XBC), lambda c: (c, xbc)), _spec((halo, XBC), lambda c: (jnp.maximum(c * (Q // halo) - 1, 0), xbc)),
                  _spec((Q, 256), lambda c: (c, 0)), wide, _spec((4, XBC), lambda c: (0, 0)), _spec((1, XBC), lambda c: (0, 0)),
                  par, par, par, _spec((1, SSD_W), lambda c: (0, 0))],
        out_specs=[wide, wide, _spec((2, 1, 4, 128, 128), lambda c: (0, c, 0, 0, 0))],
        out_shape=[jax.ShapeDtypeStruct((T, SSD_W), BF), jax.ShapeDtypeStruct((T, SSD_W), F32),
                   jax.ShapeDtypeStruct((2, NCH, 4, 128, 128), F32)],
        scratch_shapes=[pltpu.VMEM((2, 4, 128, 128), F32), pltpu.VMEM((Q + HALO, XBC), F32), pltpu.VMEM((Q, XBC), F32)],
        compiler_params=_params(), name="ssd_fwd")(proj, proj, dt_raw, proj, conv_w, conv_b, dt_bias2, a_log2, d2, norm_w)


def ssd_bwd(dyn, proj, dt_raw, conv_w, conv_b, y_pre, h_prev, dt_bias2, a_log2, d2, norm_w, after):
    def body(dyn_all, raw_ref, halo_ref, dt_all, z_all, y_all, hp_all, cw_ref, cb_ref, bias_all, alog_all, d_all, nw_all, _after,
             dz_all, dxbc_ref, ddt_all, dpar_all, dnw_all, dcw_ref, dcb_ref, dh_all, acc_all, pad_scr, act_scr, dsilu_scr, dact_scr, dpad_scr):
        @pl.when(pl.program_id(0) == 0)
        def _():
            dh_all[...] = jnp.zeros_like(dh_all)
            acc_all[...] = jnp.zeros_like(acc_all)
            dnw_all[...] = jnp.zeros_like(dnw_all)
            dcw_ref[...] = jnp.zeros_like(dcw_ref)
            dcb_ref[...] = jnp.zeros_like(dcb_ref)
            dpad_scr[Q:Q + HALO, :] = jnp.zeros((HALO, XBC), F32)

        back = _conv_window(raw_ref, halo_ref, pad_scr)
        pre = _conv(back, cw_ref[...], cb_ref[...])
        sg = _sigmoid(pre)
        act_scr[...] = pre * sg
        dsilu_scr[...] = sg * (1.0 + pre * (1.0 - sg))
        for g in range(2):
            wide, thin = slice(512 * g, 512 * g + 512), slice(128 * g, 128 * g + 128)
            xs, bs, cs = _xbc_cols(g)
            group(dyn_all.at[:, wide], act_scr.at[:, xs], act_scr.at[:, bs], act_scr.at[:, cs], dt_all.at[:, thin], z_all.at[:, wide],
                  y_all.at[:, wide], hp_all.at[g, 0], bias_all.at[g], alog_all.at[g], d_all.at[g], nw_all.at[:, wide],
                  dz_all.at[:, wide], dact_scr.at[:, xs], dact_scr.at[:, bs], dact_scr.at[:, cs], ddt_all.at[:, thin], dpar_all.at[g],
                  dnw_all.at[:, wide], dh_all.at[g], acc_all.at[g])
        dpre = dact_scr[...] * dsilu_scr[...]
        dcw, dcb = _conv_bwd_w(dpre, back)
        dcw_ref[...] += dcw
        dcb_ref[...] += dcb
        dpad_scr[0:Q, :] = dpre
        win = dpad_scr[...]
        dxbc_ref[...] = _conv_bwd_x(lambda s: win[:Q, :] if s == 0 else pltpu.roll(win, Q + HALO - s, axis=0)[:Q, :], cw_ref[...]).astype(BF)
        dpad_scr[Q:Q + HALO, :] = dpre[0:HALO, :]

    def group(dyn_ref, x_ref, b_ref, c_ref, dt_ref, z_ref, y_ref, hp_ref, bias_ref, alog_ref, d_ref, nw_ref,
              dz_ref, dx_ref, db_ref, dc_ref, ddt_ref, dpar_ref, dnw_ref, dh_scr, acc_scr):
        ci = pl.program_id(0)
        bias = bias_ref[...]
        a_neg = -jnp.exp(alog_ref[...])
        dsk = d_ref[...]
        cm = _ssd_chunk_common((NCH - 1 - ci) * Q, dt_ref, b_ref, c_ref, bias, a_neg)
        lane, sub = cm["lane"], cm["sub"]
        lane_even = lane < 64
        cc_t = c_ref[:, :].T.astype(BF)
        cb_t = lax.dot_general(cm["bc"], cm["cc"], NT_DIMS, preferred_element_type=F32)
        zc = z_ref[:, :].astype(F32)
        yc = y_ref[:, :]
        sg = _sigmoid(zc)
        sz = zc * sg
        dgated, dnw = _rms_bwd(dyn_ref[:, :], yc * sz, nw_ref[...])
        dnw_ref[...] += jnp.sum(dnw, axis=0, keepdims=True)
        dz_ref[:, :] = (dgated * yc * (sg * (1.0 + zc * (1.0 - sg)))).astype(BF)
        dy_all = dgated * sz
        dcb = jnp.zeros((Q, Q), F32)
        dcb_t = jnp.zeros((Q, Q), F32)
        db_acc = jnp.zeros((Q, Q), F32)
        dc_acc = jnp.zeros((Q, Q), F32)
        dcs = jnp.zeros((Q, Q), F32)
        ddt = jnp.zeros((Q, Q), F32)
        for p in range(4):
            je, jo = 2 * p, 2 * p + 1
            xp = x_ref[:, 128 * p:128 * p + 128]
            dy = dy_all[:, 128 * p:128 * p + 128]
            dt_p = _pair(lane_even, cm["dt"], je)
            xdt = xp * dt_p
            xdt_b = xdt.astype(BF)
            dy_b = dy.astype(BF)
            zero = jnp.zeros_like(dy_b)
            hp = hp_ref[p]
            hp_b = hp.astype(BF)
            dh = dh_scr[p]
            dh_b = dh.astype(BF)
            acc_scr[p:p + 1, :] += jnp.sum(dy * xp, axis=0, keepdims=True)
            dxp = dy * _pair_row(lane_even, dsk, je)
            e_p = _pair(lane_even, cm["ecs"], je)
            g_p = jnp.dot(cm["cc"], hp_b, preferred_element_type=F32)
            dg_b = (dy * e_p).astype(BF)
            de = dy * g_p * e_p
            dc_acc = dc_acc + lax.dot_general(dg_b, hp_b, NT_DIMS, preferred_element_type=F32)
            dh_in = jnp.dot(cc_t, dg_b, preferred_element_type=F32)
            ds_p = _pair(lane_even, cm["dsm"], je)
            r_p = jnp.dot(cm["bc"], dh_b, preferred_element_type=F32)
            dxdt = r_p * ds_p
            tt = r_p * xdt * ds_p
            db_acc = db_acc + lax.dot_general((xdt * ds_p).astype(BF), dh_b, NT_DIMS, preferred_element_type=F32)
            dgam_m = jnp.sum(dh * hp, axis=0, keepdims=True)
            for j, even in ((je, True), (jo, False)):
                sel = lane_even if even else jnp.logical_not(lane_even)
                dy_j = jnp.where(sel, dy_b, zero)
                l_j = _head_decay(cm, j)
                l_jt = _head_decay_t(cm, j)
                m_j = cm["cb"] * l_j
                m_jt = cb_t * l_jt
                dm = lax.dot_general(dy_j, xdt_b, NT_DIMS, preferred_element_type=F32)
                dm_t = lax.dot_general(xdt_b, dy_j, NT_DIMS, preferred_element_type=F32)
                dxdt = dxdt + jnp.dot(m_jt.astype(BF), dy_j, preferred_element_type=F32)
                dcb = dcb + dm * l_j
                dcb_t = dcb_t + dm_t * l_jt
                t_j = jnp.where(sel, tt, 0.0)
                col = jnp.sum(dm * m_j - dm_t * m_jt + (jnp.where(sel, de, 0.0) - t_j), axis=1, keepdims=True)
                gam_j = cm["gam"][:, j:j + 1]
                last = (jnp.sum(jnp.sum(t_j, axis=0, keepdims=True), axis=1, keepdims=True)
                        + jnp.sum(jnp.where(sel[0:1, :], dgam_m, 0.0), axis=1, keepdims=True) * gam_j)
                col = col + jnp.where(sub[:, 0:1] == Q - 1, last, 0.0)
                dcs = dcs + jnp.where(lane == j, col, 0.0)
            dh_scr[p] = dh_in + dh * _pair_row(lane_even, cm["gam"], je)
            dx_ref[:, 128 * p:128 * p + 128] = dxp + dxdt * dt_p
            dd = dxdt * xp
            ddt = ddt + jnp.where(lane == je, jnp.sum(jnp.where(lane_even, dd, 0.0), axis=1, keepdims=True), 0.0)
            ddt = ddt + jnp.where(lane == jo, jnp.sum(jnp.where(lane_even, 0.0, dd), axis=1, keepdims=True), 0.0)
        dc_ref[:, :] = dc_acc + jnp.dot(dcb.astype(BF), cm["bc"], preferred_element_type=F32)
        db_ref[:, :] = db_acc + jnp.dot(dcb_t.astype(BF), cm["cc"], preferred_element_type=F32)
        tri_t = (sub <= lane).astype(F32)
        dd_a = jnp.dot(tri_t, dcs, precision=lax.Precision.HIGHEST, preferred_element_type=F32)
        ddt = ddt + dd_a * a_neg
        acc_scr[5:6, :] += jnp.sum(dd_a * cm["dt"], axis=0, keepdims=True)
        draw = jnp.where(cm["live"], ddt * _sigmoid_gate(cm["dtr"] + bias), 0.0)
        acc_scr[4:5, :] += jnp.sum(draw, axis=0, keepdims=True)
        ddt_ref[:, :] = draw.astype(BF)

        @pl.when(ci == NCH - 1)
        def _():
            lane1 = _lanes((1, 128))
            dd = jnp.zeros((1, 128), F32)
            for p in range(4):
                row = acc_scr[p:p + 1, :]
                dd = dd + jnp.where(lane1 == 2 * p, jnp.sum(jnp.where(lane1 < 64, row, 0.0), axis=1, keepdims=True), 0.0)
                dd = dd + jnp.where(lane1 == 2 * p + 1, jnp.sum(jnp.where(lane1 < 64, 0.0, row), axis=1, keepdims=True), 0.0)
            dpar_ref[...] = jnp.concatenate([acc_scr[4:5, :], acc_scr[5:6, :] * a_neg, dd, jnp.zeros((5, 128), F32)], axis=0)

    par = _spec((2, 1, 128), lambda c: (0, 0, 0))
    wide = _spec((Q, SSD_W), lambda c: (NCH - 1 - c, 0))
    thin = _spec((Q, 256), lambda c: (NCH - 1 - c, 0))
    vec = _spec((1, SSD_W), lambda c: (0, 0))
    xbc = PXBC // XBC
    halo = 2 * HALO
    chunk = pltpu.VMEM((Q, XBC), F32)
    padded = pltpu.VMEM((Q + HALO, XBC), F32)
    return pl.pallas_call(
        body, grid=(NCH,),
        in_specs=[wide, _spec((Q, XBC), lambda c: (NCH - 1 - c, xbc)),
                  _spec((halo, XBC), lambda c: (jnp.maximum((NCH - 1 - c) * (Q // halo) - 1, 0), xbc)), thin, wide, wide,
                  _spec((2, 1, 4, 128, 128), lambda c: (0, NCH - 1 - c, 0, 0, 0)), _spec((4, XBC), lambda c: (0, 0)),
                  _spec((1, XBC), lambda c: (0, 0)), par, par, par, vec, ANY_SPEC],
        out_specs=[wide, _spec((Q, XBC), lambda c: (NCH - 1 - c, 0)), thin, _spec((2, 8, 128), lambda c: (0, 0, 0)), vec,
                   _spec((4, XBC), lambda c: (0, 0)), _spec((1, XBC), lambda c: (0, 0))],
        out_shape=[jax.ShapeDtypeStruct((T, SSD_W), BF), jax.ShapeDtypeStruct((T, XBC), BF), jax.ShapeDtypeStruct((T, 256), BF),
                   jax.ShapeDtypeStruct((2, 8, 128), F32), jax.ShapeDtypeStruct((1, SSD_W), F32), jax.ShapeDtypeStruct((4, XBC), F32),
                   jax.ShapeDtypeStruct((1, XBC), F32)],
        scratch_shapes=[pltpu.VMEM((2, 4, 128, 128), F32), pltpu.VMEM((2, 8, 128), F32), padded, chunk, chunk, chunk, padded],
        compiler_params=_params(), name="ssd_bwd")(dyn, proj, proj, dt_raw, proj, y_pre, h_prev, conv_w, conv_b, dt_bias2, a_log2, d2, norm_w, after)


def _lru_gates(back, cw, cb, wa, ba, wx, bx, lam):
    xr = _conv(back, cw, cb)
    xr_b = xr.astype(BF)
    r = _sigmoid_gate(jnp.dot(xr_b, wa, preferred_element_type=F32) + ba)
    i = _sigmoid_gate(jnp.dot(xr_b, wx, preferred_element_type=F32) + bx)
    sp = _softplus(-lam)
    la = (-LRU_C) * r * sp
    a = jnp.exp(la)
    mult2 = -jnp.tanh(la) * (a * a + 1.0)
    return xr, xr_b, r, i, sp, a, jnp.sqrt(mult2), mult2


SEG_LEN = 68
SEGS = T // SEG_LEN


def _seg_rows(j, k, off=0):
    return pl.ds(off + j * 8 * SEG_LEN + k, 8, stride=SEG_LEN)


def _segmented_scan(mul_ref, mul_row0, add_ref, out_ref, loc_scr, prod_scr, carry_scr, reverse):
    groups = SEGS // 8
    off = mul_row0 + (1 if reverse else 0)

    def local(i, carry):
        k = SEG_LEN - 1 - i if reverse else i
        new = []
        for j in range(groups):
            h, p = carry[2 * j], carry[2 * j + 1]
            m = mul_ref[_seg_rows(j, k, off), :]
            h = m * h + add_ref[_seg_rows(j, k), :]
            p = m * p
            loc_scr[_seg_rows(j, k), :] = h
            prod_scr[_seg_rows(j, k), :] = p
            new += [h, p]
        return tuple(new)

    lax.fori_loop(0, SEG_LEN, local, (jnp.zeros((8, 128), F32), jnp.ones((8, 128), F32)) * groups)

    def chain(i, c):
        s = SEGS - 1 - i if reverse else i
        carry_scr[pl.ds(s, 1), :] = c
        edge = s * SEG_LEN + (0 if reverse else SEG_LEN - 1)
        return loc_scr[pl.ds(edge, 1), :] + prod_scr[pl.ds(edge, 1), :] * c

    lax.fori_loop(0, SEGS, chain, jnp.zeros((1, 128), F32))

    def fold(k, carry):
        for j in range(groups):
            rows = _seg_rows(j, k)
            out_ref[rows, :] = loc_scr[rows, :] + prod_scr[rows, :] * carry_scr[8 * j:8 * j + 8, :]
        return carry

    lax.fori_loop(0, SEG_LEN, fold, 0)


def lru_fwd(proj, cw, cb, wa2, ba, wx2, bx, lam, after):
    def body(x_ref, cw_ref, cb_ref, wa_ref, ba_ref, wx_ref, bx_ref, lam_ref, _after, h_ref, a_ref, xpad, u_scr, loc_scr, prod_scr, carry_scr):
        _fill_padded(xpad, x_ref)

        def chunk(r0):
            xr, _, _, i, _, a, mult, _ = _lru_gates(_back(xpad, r0), cw_ref[...], cb_ref[...], wa_ref[0], ba_ref[...], wx_ref[0], bx_ref[...],
                                                 lam_ref[...])
            a_ref[pl.ds(r0, Q), :] = a
            u_scr[pl.ds(r0, Q), :] = jnp.where(_rows(a.shape, r0) >= NPAD, mult * (i * xr), 0.0)

        _chunks(chunk, unrolled=True)
        _segmented_scan(a_ref, 0, u_scr, h_ref, loc_scr, prod_scr, carry_scr, reverse=False)

    c0 = PXL // 128
    vec = _spec((1, 128), lambda c: (0, c))
    mat = _spec((1, 128, 128), lambda c: (c, 0, 0))
    seq = pltpu.VMEM((T, 128), F32)
    return pl.pallas_call(
        body, grid=(8,),
        in_specs=[_spec((T, 128), lambda c: (0, c0 + c)), _spec((4, 128), lambda c: (0, c)), vec, mat, vec, mat, vec, vec, ANY_SPEC],
        out_specs=[_spec((T, 128), lambda c: (0, c)), _spec((T, 128), lambda c: (0, c))],
        out_shape=[jax.ShapeDtypeStruct((T, LRU_W), F32), jax.ShapeDtypeStruct((T, LRU_W), F32)],
        scratch_shapes=[pltpu.VMEM((T + 2 * HALO, 128), F32), seq, seq, seq, pltpu.VMEM((SEGS, 128), F32)],
        compiler_params=_params(), name="lru_fwd")(proj, cw, cb, wa2, ba, wx2, bx, lam, after)


def lru_bwd(dh_out, a, hseq, proj, cw, cb, wa2, ba, wx2, bx, lam):
    def body(d_ref, a_ref, h_ref, x_ref, cw_ref, cb_ref, wa_ref, ba_ref, wx_ref, bx_ref, lam_ref,
             dx_ref, dcw_ref, dcb_ref, dwa_ref, dba_ref, dwx_ref, dbx_ref, dlam_ref, xpad, hpad, dpad, dh_ref, loc_scr, prod_scr, carry_scr):
        _fill_padded(dpad, a_ref)
        _segmented_scan(dpad, HALO, d_ref, dh_ref, loc_scr, prod_scr, carry_scr, reverse=True)
        _fill_padded(xpad, x_ref)
        _fill_padded(hpad, h_ref)
        dpad[0:HALO, :] = jnp.zeros((HALO, 128), F32)
        dpad[T + HALO:T + 2 * HALO, :] = jnp.zeros((HALO, 128), F32)
        for ref in (dcw_ref, dcb_ref, dwa_ref, dba_ref, dwx_ref, dbx_ref, dlam_ref):
            ref[...] = jnp.zeros_like(ref)
        lam = lam_ref[...]

        def first(r0):
            back = _back(xpad, r0)
            xr, xr_b, r, i, sp, a, mult, mult2 = _lru_gates(back, cw_ref[...], cb_ref[...], wa_ref[0], ba_ref[...], wx_ref[0], bx_ref[...], lam)
            dh = dh_ref[pl.ds(r0, Q), :]
            da = dh * _back(hpad, r0)(1)
            du = jnp.where(_rows(dh.shape, r0) >= NPAD, dh, 0.0)
            dmult = du * (i * xr)
            di = du * (mult * xr)
            dxr = du * (mult * i)
            dla = da * a - dmult * (a * a) * lax.rsqrt(mult2)
            dr = dla * ((-LRU_C) * sp)
            dlam_ref[...] += jnp.sum(dla * ((-LRU_C) * r), axis=0, keepdims=True)
            dpr = dr * r * (1.0 - r)
            dpi = di * i * (1.0 - i)
            dba_ref[...] += jnp.sum(dpr, axis=0, keepdims=True)
            dbx_ref[...] += jnp.sum(dpi, axis=0, keepdims=True)
            dpr_b = dpr.astype(BF)
            dpi_b = dpi.astype(BF)
            dxr = (dxr + lax.dot_general(dpr_b, wa_ref[0], NT_DIMS, preferred_element_type=F32)
                   + lax.dot_general(dpi_b, wx_ref[0], NT_DIMS, preferred_element_type=F32))
            dwa_ref[0] += lax.dot_general(xr_b, dpr_b, TN_DIMS, preferred_element_type=F32)
            dwx_ref[0] += lax.dot_general(xr_b, dpi_b, TN_DIMS, preferred_element_type=F32)
            dpad[pl.ds(r0 + HALO, Q), :] = dxr
            dcw, dcb = _conv_bwd_w(dxr, back)
            dcw_ref[...] += dcw
            dcb_ref[...] += dcb

        _chunks(first, unrolled=True)
        dlam_ref[...] = -dlam_ref[...] * _sigmoid_gate(-lam)

        def second(r0):
            dx_ref[pl.ds(r0, Q), :] = _conv_bwd_x(_ahead(dpad, r0), cw_ref[...]).astype(BF)

        _chunks(second)

    c0 = PXL // 128
    vec = _spec((1, 128), lambda c: (0, c))
    mat = _spec((1, 128, 128), lambda c: (c, 0, 0))
    col = _spec((T, 128), lambda c: (0, c))
    vshape = jax.ShapeDtypeStruct((1, LRU_W), F32)
    mshape = jax.ShapeDtypeStruct((8, 128, 128), F32)
    pad = pltpu.VMEM((T + 2 * HALO, 128), F32)
    seq = pltpu.VMEM((T, 128), F32)
    return pl.pallas_call(
        body, grid=(8,),
        in_specs=[col, col, col, _spec((T, 128), lambda c: (0, c0 + c)), _spec((4, 128), lambda c: (0, c)), vec, mat, vec, mat, vec, vec],
        out_specs=[col, _spec((4, 128), lambda c: (0, c)), vec, mat, vec, mat, vec, vec],
        out_shape=[jax.ShapeDtypeStruct((T, LRU_W), BF), jax.ShapeDtypeStruct((4, LRU_W), F32), vshape, mshape, vshape, mshape, vshape, vshape],
        scratch_shapes=[pad, pad, pad, seq, seq, seq, pltpu.VMEM((SEGS, 128), F32)],
        compiler_params=_params(), name="lru_bwd")(dh_out, a, hseq, proj, cw, cb, wa2, ba, wx2, bx, lam)


FF_TILE = 256
FF_TILE_ROWS = list(range(0, D_FF, FF_TILE))


def gate_up(h1, wn, w_gate, w_up):
    def body(h_ref, wn_ref, wg_hbm, wu_hbm, gt_ref, up_ref, act_ref, u_ref, wg_ref, wu_ref, wg_sems, wu_sems):
        gate_ready = _arriving(wg_hbm, wg_ref, wg_sems, FF_TILE_ROWS, FF_TILE)
        up_ready = _arriving(wu_hbm, wu_ref, wu_sems, FF_TILE_ROWS, FF_TILE)
        for r in (0, HALF):
            u_ref[r:r + HALF, :] = _rms(h_ref[r:r + HALF, :], wn_ref[...]).astype(BF)

        def tile(c0):
            cols = pl.ds(c0, FF_TILE)
            gate_ready(c0 // FF_TILE)
            up_ready(c0 // FF_TILE)
            gt = lax.dot_general(u_ref[...], wg_ref[cols, :], NT_DIMS, preferred_element_type=F32)
            up = lax.dot_general(u_ref[...], wu_ref[cols, :], NT_DIMS, preferred_element_type=F32)
            gt_ref[:, cols] = gt.astype(BF)
            up_ref[:, cols] = up.astype(BF)
            act_ref[:, cols] = (gt * _sigmoid(gt) * up).astype(BF)

        _col_tiles(D_FF, FF_TILE, tile)

    big = jax.ShapeDtypeStruct((T, D_FF), BF)
    return pl.pallas_call(
        body, grid=(T // RC,), in_specs=[_rows_spec(D), _vec(D), ANY_SPEC, ANY_SPEC],
        out_specs=[_rows_spec(D_FF), _rows_spec(D_FF), _rows_spec(D_FF), _rows_spec(D)],
        out_shape=[big, big, big, jax.ShapeDtypeStruct((T, D), BF)],
        scratch_shapes=[pltpu.VMEM((D_FF, D), BF)] * 2 + [pltpu.SemaphoreType.DMA((len(FF_TILE_ROWS),))] * 2,
        compiler_params=_params(), name="gate_up")(h1, wn, w_gate, w_up)


def down_loss(act, w_down, h1, target, wf):
    first = NPAD + N_META

    def body(a_ref, w_ref, r_ref, t_hbm, wf_ref, d_ref, db_ref, l_ref, dw_ref, h_scr, t_ref, t_sem):
        i = pl.program_id(0)
        _zero_at_first(l_ref, dw_ref)
        head = pltpu.make_async_copy(t_hbm.at[pl.ds(0, RC - first)], t_ref.at[pl.ds(first, RC - first)], t_sem)
        rest = pltpu.make_async_copy(t_hbm.at[pl.ds(pl.multiple_of(jnp.maximum(i * RC - first, 0), 32), RC)], t_ref, t_sem)

        @pl.when(i == 0)
        def _():
            t_ref[0:first, :] = jnp.zeros((first, D), F32)
            head.start()

        @pl.when(i > 0)
        def _():
            rest.start()

        def tile(c0):
            cols = pl.ds(c0, 512)
            h_scr[:, cols] = r_ref[:, cols] + jnp.dot(a_ref[...], w_ref[:, cols], preferred_element_type=F32)

        _col_tiles(D, 512, tile)

        @pl.when(i == 0)
        def _():
            head.wait()

        @pl.when(i > 0)
        def _():
            rest.wait()

        for r in (0, HALF):
            h = h_scr[r:r + HALF, :]
            live = _rows((HALF, D), i * RC + r) >= first
            err = jnp.where(live, _rms(h, wf_ref[...]) - t_ref[r:r + HALF, :], 0.0)
            l_ref[...] += 0.5 * jnp.sum(jnp.sum(err * err, axis=1, keepdims=True) * (1.0 / D), axis=0, keepdims=True)
            dh, dw = _rms_bwd(err * (1.0 / D), h, wf_ref[...])
            dw_ref[...] += jnp.sum(dw, axis=0, keepdims=True)
            d_ref[r:r + HALF, :] = dh
            db_ref[r:r + HALF, :] = dh.astype(BF)

    return pl.pallas_call(
        body, grid=(T // RC,),
        in_specs=[_rows_spec(D_FF), _whole((D_FF, D)), _rows_spec(D), pl.BlockSpec(memory_space=pl.ANY), _vec(D)],
        out_specs=[_rows_spec(D), _rows_spec(D), _spec((1, 128), lambda i: (0, 0)), _vec(D)],
        out_shape=[jax.ShapeDtypeStruct((T, D), F32), jax.ShapeDtypeStruct((T, D), BF), jax.ShapeDtypeStruct((1, 128), F32),
                   jax.ShapeDtypeStruct((1, D), F32)],
        scratch_shapes=[pltpu.VMEM((RC, D), F32), pltpu.VMEM((RC, D), F32), pltpu.SemaphoreType.DMA],
        compiler_params=_params(), name="down_loss")(act, w_down, h1, target, wf)


def swiglu_bwd(dh2_b, w_down, gt, up, act, u2):
    tn = 256

    def body(d_hbm, u_hbm, w_ref, gt_ref, up_ref, act_ref, dg_ref, du_ref, gd_ref, gg_ref, gu_ref, d_ref, u_ref, d_sems, u_sems):
        chunks = list(range(0, T, RC))
        d_ready = _arriving(d_hbm, d_ref, d_sems, chunks, RC)
        u_ready = _arriving(u_hbm, u_ref, u_sems, chunks, RC)

        def rows(r0):
            part = pl.ds(r0, RC)
            d_ready(r0 // RC)
            dact = lax.dot_general(d_ref[part, :], w_ref[...], NT_DIMS, preferred_element_type=F32)
            gt_ = gt_ref[part, :].astype(F32)
            up_ = up_ref[part, :].astype(F32)
            sg = _sigmoid(gt_)
            dg_ref[part, :] = (dact * up_ * (sg * (1.0 + gt_ * (1.0 - sg)))).astype(BF)
            du_ref[part, :] = (dact * (gt_ * sg)).astype(BF)

        _col_tiles(T, RC, rows)
        for k in range(len(chunks)):
            u_ready(k)
        gd_ref[...] = lax.dot_general(act_ref[...], d_ref[...], TN_DIMS, preferred_element_type=F32).astype(BF)
        gg_ref[...] = lax.dot_general(dg_ref[...], u_ref[...], TN_DIMS, preferred_element_type=F32).astype(BF)
        gu_ref[...] = lax.dot_general(du_ref[...], u_ref[...], TN_DIMS, preferred_element_type=F32).astype(BF)

    cols = _spec((T, tn), lambda j: (0, j))
    wrow = _spec((tn, D), lambda j: (j, 0))
    big = jax.ShapeDtypeStruct((T, D_FF), BF)
    grad = jax.ShapeDtypeStruct((D_FF, D), BF)
    return pl.pallas_call(
        body, grid=(D_FF // tn,), in_specs=[ANY_SPEC, ANY_SPEC, wrow, cols, cols, cols],
        out_specs=[cols, cols, wrow, wrow, wrow], out_shape=[big, big, grad, grad, grad],
        scratch_shapes=[pltpu.VMEM((T, D), BF)] * 2 + [pltpu.SemaphoreType.DMA((T // RC,))] * 2,
        compiler_params=_params(), name="swiglu_bwd")(dh2_b, u2, w_down, gt, up, act)


def gate_up_bwd(dgt, dup, w_gate, w_up, h1, wn, dh2, after):
    def body(dg_ref, du_ref, wg_hbm, wu_hbm, h_ref, wn_ref, r_ref, _after, d_ref, db_ref, dw_ref, du_scr, wg_ref, wu_ref, wg_sems, wu_sems):
        gate_ready = _arriving(wg_hbm, wg_ref, wg_sems, FF_TILE_ROWS, FF_TILE)
        up_ready = _arriving(wu_hbm, wu_ref, wu_sems, FF_TILE_ROWS, FF_TILE)
        _zero_at_first(dw_ref)

        du_scr[...] = jnp.zeros_like(du_scr)

        def tile(c0):
            k = pl.ds(c0, FF_TILE)
            gate_ready(c0 // FF_TILE)
            up_ready(c0 // FF_TILE)
            du_scr[...] += (jnp.dot(dg_ref[:, k], wg_ref[k, :], preferred_element_type=F32)
                            + jnp.dot(du_ref[:, k], wu_ref[k, :], preferred_element_type=F32))

        _col_tiles(D_FF, FF_TILE, tile)
        for r in (0, HALF):
            dh, dw = _rms_bwd(du_scr[r:r + HALF, :], h_ref[r:r + HALF, :], wn_ref[...])
            dw_ref[...] += jnp.sum(dw, axis=0, keepdims=True)
            dh = dh + r_ref[r:r + HALF, :]
            d_ref[r:r + HALF, :] = dh
            db_ref[r:r + HALF, :] = dh.astype(BF)

    return pl.pallas_call(
        body, grid=(T // RC,),
        in_specs=[_rows_spec(D_FF), _rows_spec(D_FF), ANY_SPEC, ANY_SPEC, _rows_spec(D), _vec(D), _rows_spec(D), ANY_SPEC],
        out_specs=[_rows_spec(D), _rows_spec(D), _vec(D)],
        out_shape=[jax.ShapeDtypeStruct((T, D), F32), jax.ShapeDtypeStruct((T, D), BF), jax.ShapeDtypeStruct((1, D), F32)],
        scratch_shapes=[pltpu.VMEM((RC, D), F32)] + [pltpu.VMEM((D_FF, D), BF)] * 2 + [pltpu.SemaphoreType.DMA((len(FF_TILE_ROWS),))] * 2,
        compiler_params=_params(), name="gate_up_bwd")(dgt, dup, w_gate, w_up, h1, wn, dh2, after)


def _adamw(w, g, m, v):
    m = ADAM_B1 * m + (1.0 - ADAM_B1) * g
    v = ADAM_B2 * v + (1.0 - ADAM_B2) * (g * g)
    m_hat = m / (1.0 - ADAM_B1 ** ADAM_STEP)
    v_hat = v / (1.0 - ADAM_B2 ** ADAM_STEP)
    delta = -ADAM_LR * (m_hat / (jnp.sqrt(v_hat) + ADAM_EPS) + ADAM_WD * w)
    return delta, m, v


def adamw_shards(name, recvs, ws, ms, vs):
    n = len(ws)

    def body(*refs):
        ins, outs = refs[:4 * n], refs[4 * n:]
        for k in range(n):
            p_ref, w_ref, m_ref, v_ref = ins[k], ins[n + k], ins[2 * n + k], ins[3 * n + k]
            g = p_ref[0].astype(F32)
            for s in range(1, 8):
                g = g + p_ref[s].astype(F32)
            outs[4 * k][...] = g
            outs[4 * k + 1][...], outs[4 * k + 2][...], outs[4 * k + 3][...] = _adamw(w_ref[...], g, m_ref[...], v_ref[...])

    tiles = [_spec((w.shape[0] // 2, w.shape[1]), lambda i: (i, 0)) for w in ws]
    recv_tiles = [_spec((8, w.shape[0] // 2, w.shape[1]), lambda i: (0, i, 0)) for w in ws]
    res = pl.pallas_call(
        body, grid=(2,), in_specs=recv_tiles + tiles * 3,
        out_specs=[t for t in tiles for _ in range(4)],
        out_shape=[jax.ShapeDtypeStruct(w.shape, F32) for w in ws for _ in range(4)],
        compiler_params=_params(), name=name)(*recvs, *ws, *ms, *vs)
    return [list(res[4 * k:4 * k + 4]) for k in range(n)]


def adamw_w_in(recv, w, m, v):
    rows = 34
    per_row = D // 128

    def body(p_ref, w_ref, m_ref, v_ref, g_ref, d_ref, mo_ref, vo_ref):
        def chunk(c, carry):
            lines = pl.ds(pl.multiple_of(c * per_row * rows, 16), per_row * rows)
            g = p_ref[0, lines, :].astype(F32)
            for s in range(1, 8):
                g = g + p_ref[s, lines, :].astype(F32)
            g = g.reshape(rows, per_row, 128)
            part = pl.ds(c * rows, rows)
            g_ref[part] = g
            d_ref[part], mo_ref[part], vo_ref[part] = _adamw(w_ref[part], g, m_ref[part], v_ref[part])
            return carry

        lax.fori_loop(0, w.shape[0] // rows, chunk, 0)

    shape = jax.ShapeDtypeStruct(w.shape, F32)
    return pl.pallas_call(body, out_shape=[shape] * 4, compiler_params=_params(0), name="adamw_w_in")(recv, w, m, v)


def sum_slabs(recv):
    def body(p_ref, o_ref):
        g = p_ref[0]
        for s in range(1, 8):
            g = g + p_ref[s]
        o_ref[...] = g

    return pl.pallas_call(body, out_shape=jax.ShapeDtypeStruct(recv.shape[1:], F32), compiler_params=_params(0), name="sum_slabs")(recv)


SIMPLE = [("norm1_w", 1024), ("ssd_conv_b", 1536), ("ssd_dt_bias", 16), ("ssd_a_log", 16), ("ssd_d", 16), ("ssd_norm_w", 1024),
          ("lru_conv_b", 1024), ("lru_ba", 1024), ("lru_bx", 1024), ("lru_lambda", 1024), ("lru_norm_w", 1024), ("norm2_w", 1024),
          ("final_norm_w", 1024)]
SPECIAL = ["lru_wa", "lru_wx", "meta_tokens", "ssd_conv_w", "lru_conv_w"]
SM_ROWS = 176
SM_WA, SM_WX, SM_META, SM_SCW, SM_LCW, SM_LOSS = 14, 78, 142, 158, 166, 170


def _simple_rows():
    rows, r = {}, 0
    for name, n in SIMPLE:
        rows[name] = r
        r += -(-n // 1024)
    return rows


def adamw_small(sm, special_g, ws, ms, vs):
    rows = _simple_rows()
    ns, nx = len(SIMPLE), len(SPECIAL)

    def body(*refs):
        sm_ref = refs[0]
        gx = refs[1:1 + nx]
        wr = refs[1 + nx:1 + nx + ns + nx]
        mr = refs[1 + nx + ns + nx:1 + nx + 2 * (ns + nx)]
        vr = refs[1 + nx + 2 * (ns + nx):1 + nx + 3 * (ns + nx)]
        outs = refs[1 + nx + 3 * (ns + nx):]
        o = 0
        for k, (name, n) in enumerate(SIMPLE):
            r0 = rows[name]
            for c0 in range(0, n, 1024):
                wd = min(1024, n - c0)
                g = sm_ref[r0 + c0 // 1024:r0 + c0 // 1024 + 1, 0:wd]
                sl = (slice(None), slice(c0, c0 + wd))
                d, m2, v2 = _adamw(wr[k][sl], g, mr[k][sl], vr[k][sl])
                outs[o][sl] = g
                outs[o + 1][sl] = d
                outs[o + 2][sl] = m2
                outs[o + 3][sl] = v2
            o += 4
        for k in range(nx):
            d, m2, v2 = _adamw(wr[ns + k][...], gx[k][...], mr[ns + k][...], vr[ns + k][...])
            outs[o][...] = d
            outs[o + 1][...] = m2
            outs[o + 2][...] = v2
            o += 3

    out_shape = []
    for k in range(ns):
        out_shape += [jax.ShapeDtypeStruct(ws[k].shape, F32)] * 4
    for k in range(nx):
        out_shape += [jax.ShapeDtypeStruct(ws[ns + k].shape, F32)] * 3
    return pl.pallas_call(body, out_shape=out_shape, compiler_params=_params(0), name="adamw_small")(sm, *special_g, *ws, *ms, *vs)


def _place():
    return lax.axis_index("x"), lax.axis_index("y"), lax.axis_index("c")


def _index(px, py, pc):
    return 4 * px + 2 * py + pc


def all_gather(name, shards):
    n = len(shards)
    hbm = pl.BlockSpec(memory_space=pl.ANY)

    def body(*refs):
        ins, outs = refs[:n], refs[n:2 * n]
        send_sems, recv_sems, local_sems = refs[2 * n:]
        x, y, c = _place()
        me, sibling = (x, y, c), (x, y, 1 - c)
        chips = [(1 - x, y), (x, 1 - y), (1 - x, 1 - y)]

        def copy(i, k, block, to, src=None):
            dst = outs[i].at[_index(*block)]
            return pltpu.make_async_remote_copy(src_ref=dst if src is None else src, dst_ref=dst, send_sem=send_sems.at[7 * i + k],
                                                recv_sem=recv_sems.at[7 * i + k], device_id=to, device_id_type=MESH)

        mine = [pltpu.make_async_copy(ins[i], outs[i].at[_index(*me)], local_sems.at[i]) for i in range(n)]
        for cp in mine:
            cp.start()
        first = []
        for i in range(n):
            first += [copy(i, 1 + j, me, (*chip, c), src=ins[i]) for j, chip in enumerate(chips)]
            first.append(copy(i, 0, me, sibling, src=ins[i]))
        for cp in first:
            cp.start()
        passed = []
        for i in range(n):
            for j, chip in enumerate(chips):
                copy(i, 1 + j, (*chip, c), me).wait_recv()
                cp = copy(i, 4 + j, (*chip, c), sibling)
                cp.start()
                passed.append(cp)
        for i in range(n):
            copy(i, 0, sibling, me).wait_recv()
            for j, chip in enumerate(chips):
                copy(i, 4 + j, (*chip, 1 - c), me).wait_recv()
        for cp in first + passed:
            cp.wait_send()
        for cp in mine:
            cp.wait()

    return pl.pallas_call(
        body, in_specs=[hbm] * n, out_specs=[hbm] * n,
        out_shape=[jax.ShapeDtypeStruct((8,) + s.shape, s.dtype) for s in shards],
        scratch_shapes=[pltpu.SemaphoreType.DMA((7 * n,)), pltpu.SemaphoreType.DMA((7 * n,)), pltpu.SemaphoreType.DMA((n,))],
        name=name)(*shards)


HBM_SPEC = pl.BlockSpec(memory_space=pltpu.HBM)
SEM_SPEC = pl.BlockSpec(memory_space=pltpu.SEMAPHORE)
EFFECT = pltpu.SideEffectType.DATAFLOW_SIDE_EFFECTING


def _peers(x, y, c):
    return [((1 - x) if k & 4 else x, (1 - y) if k & 2 else y, (1 - c) if k & 1 else c) for k in range(1, 8)]


def _pieces(rows):
    for n in (4, 2):
        if rows % (16 * n) == 0:
            return [(r * (rows // n), rows // n) for r in range(n)]
    return [(0, rows)]


def _peer_copies(src, land, send_sems, recv_sems, k, peer, mine):
    block = src.at[_index(*peer)]
    return [pltpu.make_async_remote_copy(src_ref=block.at[pl.ds(r0, nr)], dst_ref=land.at[mine, pl.ds(r0, nr)], send_sem=send_sems.at[k],
                                         recv_sem=recv_sems.at[k], device_id=peer, device_id_type=MESH)
            for r0, nr in _pieces(block.shape[0])]


OWN = 7


def _own_copy(src, land, send_sems, mine):
    return pltpu.make_async_copy(src.at[mine], land.at[mine], send_sems.at[OWN])


def copies_start(name, srcs):
    n = len(srcs)

    def body(*refs):
        ins, lands = refs[:n], refs[n:2 * n]
        sends, recvs = refs[2 * n:3 * n], refs[3 * n:4 * n]
        token = refs[-1]
        x, y, c = _place()
        mine = _index(x, y, c)
        for i in range(n):
            per_peer = [_peer_copies(ins[i], lands[i], sends[i], recvs[i], k, peer, mine) for k, peer in enumerate(_peers(x, y, c))]
            for piece in zip(*per_peer):
                for cp in piece:
                    cp.start()
            _own_copy(ins[i], lands[i], sends[i], mine).start()
        token[...] = jnp.zeros_like(token)

    res = pl.pallas_call(
        body, name=name,
        out_shape=([pltpu.SemaphoreType.DMA((8,))] * n + [pltpu.SemaphoreType.DMA((7,))] * n + [pltpu.HBM(s.shape, s.dtype) for s in srcs] * 2
                   + [jax.ShapeDtypeStruct((8, 128), F32)]),
        in_specs=[HBM_SPEC] * (2 * n),
        out_specs=[SEM_SPEC] * (2 * n) + [HBM_SPEC] * (2 * n) + [pl.BlockSpec(memory_space=pltpu.VMEM)],
        input_output_aliases={i: 2 * n + i for i in range(2 * n)},
        compiler_params=pltpu.CompilerParams(has_side_effects=EFFECT),
    )(*[pltpu.with_memory_space_constraint(s, pltpu.HBM) for s in srcs],
      *[pltpu.with_memory_space_constraint(lax.empty(s.shape, s.dtype), pltpu.HBM) for s in srcs])
    return [(res[i], res[n + i], res[2 * n + i], res[3 * n + i]) for i in range(n)], res[-1]


def copies_wait(name, started, after):
    n = len(started)

    def body(*refs):
        ins, lands = refs[:n], refs[n:2 * n]
        sends, recvs = refs[2 * n:3 * n], refs[3 * n:4 * n]
        x, y, c = _place()
        mine = _index(x, y, c)
        for i in range(n):
            for k, peer in enumerate(_peers(x, y, c)):
                arrival = pltpu.make_async_remote_copy(src_ref=ins[i].at[mine], dst_ref=lands[i].at[_index(*peer)],
                                                       send_sem=sends[i].at[k], recv_sem=recvs[i].at[k], device_id=peer, device_id_type=MESH)
                arrival.wait_send()
                arrival.wait_recv()
            _own_copy(ins[i], lands[i], sends[i], mine).wait()

    srcs = [s[2] for s in started]
    lands = [s[3] for s in started]
    res = pl.pallas_call(
        body, name=name,
        out_shape=[pltpu.HBM(s.shape, s.dtype) for s in srcs] + [pltpu.HBM(z.shape, z.dtype) for z in lands],
        in_specs=[HBM_SPEC] * (2 * n) + [SEM_SPEC] * (2 * n) + [ANY_SPEC] * len(after),
        out_specs=[HBM_SPEC] * (2 * n),
        input_output_aliases={i: i for i in range(2 * n)},
        compiler_params=pltpu.CompilerParams(has_side_effects=EFFECT),
    )(*srcs, *lands, *[s[0] for s in started], *[s[1] for s in started], *after)
    return list(res[n:])


def _hop(src, land, send_sems, recv_sems, k, block, to):
    dst = land.at[_index(*block)]
    return pltpu.make_async_remote_copy(src_ref=dst if src is None else src, dst_ref=dst, send_sem=send_sems.at[k], recv_sem=recv_sems.at[k],
                                        device_id=to, device_id_type=MESH)


def _own_block(src, land, send_sems, mine):
    return pltpu.make_async_copy(src, land.at[mine], send_sems.at[4])


def _other_chips(x, y):
    return [(1 - x, y), (x, 1 - y), (1 - x, 1 - y)]


def gather_start(name, shards, through):
    n, m = len(shards), len(through)

    def body(*refs):
        ins, lands = refs[:n], refs[n:2 * n]
        sends, recvs = refs[2 * n + m:3 * n + m], refs[3 * n + m:4 * n + m]
        x, y, c = _place()
        for i in range(n):
            for j, chip in enumerate(_other_chips(x, y)):
                _hop(ins[i], lands[i], sends[i], recvs[i], 1 + j, (x, y, c), (*chip, c)).start()
            _hop(ins[i], lands[i], sends[i], recvs[i], 0, (x, y, c), (x, y, 1 - c)).start()
            _own_block(ins[i], lands[i], sends[i], _index(x, y, c)).start()

    own, passing = pltpu.SemaphoreType.DMA((5,)), pltpu.SemaphoreType.DMA((3,))
    zones = [jax.ShapeDtypeStruct((8,) + s.shape, s.dtype) for s in shards]
    res = pl.pallas_call(
        body, name=name,
        out_shape=([own] * (2 * n) + [passing] * (2 * n) + [pltpu.HBM(s.shape, s.dtype) for s in shards]
                   + [pltpu.HBM(z.shape, z.dtype) for z in zones] + [pltpu.HBM(t.shape, t.dtype) for t in through]),
        in_specs=[HBM_SPEC] * (2 * n + m),
        out_specs=[SEM_SPEC] * (4 * n) + [HBM_SPEC] * (2 * n + m),
        input_output_aliases={i: 4 * n + i for i in range(2 * n + m)},
        compiler_params=pltpu.CompilerParams(has_side_effects=EFFECT),
    )(*[pltpu.with_memory_space_constraint(s, pltpu.HBM) for s in shards],
      *[pltpu.with_memory_space_constraint(lax.empty(z.shape, z.dtype), pltpu.HBM) for z in zones],
      *[pltpu.with_memory_space_constraint(t, pltpu.HBM) for t in through])
    return [[res[4 * n + i], res[5 * n + i], res[i], res[n + i], res[2 * n + i], res[3 * n + i]] for i in range(n)], list(res[6 * n:])


def gather_stage(name, pass_on, finish, after):
    arrays = pass_on + finish
    n = len(arrays)

    def body(*refs):
        ins, lands = refs[:n], refs[n:2 * n]
        sems = [refs[(2 + q) * n:(3 + q) * n] for q in range(4)]
        x, y, c = _place()
        me, sibling = (x, y, c), (x, y, 1 - c)
        for i in range(len(pass_on)):
            send, recv, send_on, recv_on = (q[i] for q in sems)
            for j, chip in enumerate(_other_chips(x, y)):
                _hop(None, lands[i], send, recv, 1 + j, (*chip, c), me).wait_recv()
                _hop(None, lands[i], send_on, recv_on, j, (*chip, c), sibling).start()
        for i in range(len(pass_on), n):
            send, recv, send_on, recv_on = (q[i] for q in sems)
            _hop(ins[i], lands[i], send, recv, 0, sibling, me).wait_recv()
            for j, chip in enumerate(_other_chips(x, y)):
                _hop(None, lands[i], send_on, recv_on, j, (*chip, 1 - c), me).wait_recv()
            _hop(ins[i], lands[i], send, recv, 0, me, sibling).wait_send()
            _own_block(ins[i], lands[i], send, _index(*me)).wait()
            for j, chip in enumerate(_other_chips(x, y)):
                _hop(ins[i], lands[i], send, recv, 1 + j, me, (*chip, c)).wait_send()
                _hop(None, lands[i], send_on, recv_on, j, (*chip, c), sibling).wait_send()
        refs[-1][...] = jnp.zeros_like(refs[-1])

    res = pl.pallas_call(
        body, name=name,
        out_shape=([pltpu.HBM(a[0].shape, a[0].dtype) for a in arrays] + [pltpu.HBM(a[1].shape, a[1].dtype) for a in arrays]
                   + [jax.ShapeDtypeStruct((8, 128), F32)]),
        in_specs=[HBM_SPEC] * (2 * n) + [SEM_SPEC] * (4 * n) + [ANY_SPEC],
        out_specs=[HBM_SPEC] * (2 * n) + [pl.BlockSpec(memory_space=pltpu.VMEM)],
        input_output_aliases={i: i for i in range(2 * n)},
        compiler_params=pltpu.CompilerParams(has_side_effects=EFFECT),
    )(*[a[0] for a in arrays], *[a[1] for a in arrays], *[a[2 + q] for q in range(4) for a in arrays], after)
    for i, a in enumerate(arrays):
        a[0], a[1] = res[i], res[n + i]
    return [a[1] for a in finish], res[-1]


WEIGHTS = ["meta_tokens", "norm1_w", "w_in", "ssd_conv_w", "ssd_conv_b", "ssd_dt_bias", "ssd_a_log", "ssd_d", "ssd_norm_w", "lru_conv_w",
           "lru_conv_b", "lru_wa", "lru_ba", "lru_wx", "lru_bx", "lru_lambda", "lru_norm_w", "w_out", "norm2_w", "w_gate", "w_up", "w_down",
           "final_norm_w"]
BIG = ["w_in", "w_out", "w_gate", "w_up", "w_down"]
COLUMN_SHARDED = ["w_in", "w_gate", "w_up"]


def _pair_blocks(w):
    w = w.reshape(8, 2, 64, 64)
    z = jnp.zeros((8, 64, 64), w.dtype)
    return jnp.concatenate([jnp.concatenate([w[:, 0], z], axis=2), jnp.concatenate([z, w[:, 1]], axis=2)], axis=1)


def _unpair_blocks(w2):
    return jnp.stack([w2[:, :64, :64], w2[:, 64:, 64:]], axis=1).reshape(16, 64, 64)


def _per_group(v):
    return jnp.pad(v.reshape(2, 1, 8), ((0, 0), (0, 0), (0, 120)))


def _pad_cols(v, n):
    return jnp.pad(v, ((0, 0), (0, n - v.shape[1])))


def local_step(x, target, meta, ssd_cw, lru_cw, w_in_shards, fetch, send, p):
    bias2, alog2, d2 = _per_group(p["ssd_dt_bias"]), _per_group(p["ssd_a_log"]), _per_group(p["ssd_d"])
    wa2 = _pair_blocks(p["lru_wa"]).astype(BF)
    wx2 = _pair_blocks(p["lru_wx"]).astype(BF)
    lru = (lru_cw, p["lru_conv_b"], wa2, p["lru_ba"], wx2, p["lru_bx"], p["lru_lambda"])

    proj, dt_raw, u1, h0, w_in, w_dt = in_proj(x, meta, p["norm1_w"], w_in_shards)
    yn_ssd, y_pre, h_prev = ssd_fwd(proj, dt_raw, ssd_cw, p["ssd_conv_b"], bias2, alog2, d2, p["ssd_norm_w"])
    _, moved = fetch([], yn_ssd)
    hseq, a = lru_fwd(proj, *lru, moved)
    (w_out,), _ = fetch(["w_out"], hseq)
    h1, cat = out_proj(yn_ssd, proj, hseq, p["lru_norm_w"], w_out, h0)
    (w_gate, w_up), _ = fetch(["w_gate", "w_up"], h1)
    gt, up, act, u2 = gate_up(h1, p["norm2_w"], w_gate, w_up)
    (w_down,), _ = fetch(["w_down"], act)
    dh2, dh2_b, loss, d_fnw = down_loss(act, w_down, h1, target, p["final_norm_w"])

    dgt, dup, g_down, g_gate, g_up = swiglu_bwd(dh2_b, w_down, gt, up, act, u2)
    sent = send({"w_down": g_down, "w_gate": g_gate, "w_up": g_up})
    dh1, dh1_b, d_n2 = gate_up_bwd(dgt, dup, w_gate, w_up, h1, p["norm2_w"], dh2, sent)
    sent = send({"w_out": weight_grad("dw_out", cat, dh1_b)})
    dyn, dh_out, dg_b, d_lnw = out_proj_bwd(dh1_b, w_out, proj, hseq, p["lru_norm_w"], sent)

    dxl_b, d_lcw, d_lcb, dwa2, d_ba, dwx2, d_bx, d_lam = lru_bwd(dh_out, a, hseq, proj, *lru)
    dz_b, dxbc_b, ddt_b, dpar, d_snw, d_scw, d_scb = ssd_bwd(dyn, proj, dt_raw, ssd_cw, p["ssd_conv_b"], y_pre, h_prev, bias2, alog2, d2,
                                                             p["ssd_norm_w"], sent)
    sent = send({"w_in": in_weight_grad([dz_b, dxbc_b, dg_b, dxl_b], [0, SSD_W, 2576, 2576 + LRU_W], ddt_b, u1)})
    grad_x, d_meta, d_n1 = in_proj_bwd(dz_b, dg_b, dxl_b, dxbc_b, ddt_b, w_in, w_dt, h0, p["norm1_w"], dh1, sent)
    small = {"norm1_w": d_n1, "ssd_conv_b": d_scb, "ssd_dt_bias": dpar[:, 0, :8].reshape(1, 16), "ssd_a_log": dpar[:, 1, :8].reshape(1, 16),
             "ssd_d": dpar[:, 2, :8].reshape(1, 16), "ssd_norm_w": d_snw, "lru_conv_b": d_lcb, "lru_ba": d_ba, "lru_bx": d_bx,
             "lru_lambda": d_lam, "lru_norm_w": d_lnw, "norm2_w": d_n2, "final_norm_w": d_fnw,
             "lru_wa": _unpair_blocks(dwa2), "lru_wx": _unpair_blocks(dwx2), "meta_tokens": d_meta,
             "ssd_conv_w": d_scw, "lru_conv_w": d_lcw}
    return loss, grad_x, small


def _pack_small(small, loss):
    rows = [_pad_cols(small[name], -(-n // 1024) * 1024).reshape(-1, 1024) for name, n in SIMPLE]
    rows += [small["lru_wa"].reshape(64, 1024), small["lru_wx"].reshape(64, 1024), small["meta_tokens"],
             _pad_cols(small["ssd_conv_w"], 2048).reshape(8, 1024), small["lru_conv_w"], _pad_cols(loss[:, 0:1], 1024)]
    sm = jnp.concatenate(rows, axis=0)
    return jnp.pad(sm, ((0, SM_ROWS - sm.shape[0]), (0, 0)))


def _slabs(g):
    return g.reshape(8, g.shape[0] // 8, g.shape[1])


def _unslab(g):
    return g.reshape(8 * g.shape[1], g.shape[2])


def kernel(x, meta_tokens, norm1_w, w_in, ssd_conv_w, ssd_conv_b, ssd_dt_bias, ssd_a_log, ssd_d, ssd_norm_w, lru_conv_w, lru_conv_b, lru_wa, lru_ba, lru_wx, lru_bx, lru_lambda, lru_norm_w, w_out, norm2_w, w_gate, w_up, w_down, final_norm_w, loss_target, m_meta_tokens, m_norm1_w, m_w_in, m_ssd_conv_w, m_ssd_conv_b, m_ssd_dt_bias, m_ssd_a_log, m_ssd_d, m_ssd_norm_w, m_lru_conv_w, m_lru_conv_b, m_lru_wa, m_lru_ba, m_lru_wx, m_lru_bx, m_lru_lambda, m_lru_norm_w, m_w_out, m_norm2_w, m_w_gate, m_w_up, m_w_down, m_final_norm_w, v_meta_tokens, v_norm1_w, v_w_in, v_ssd_conv_w, v_ssd_conv_b, v_ssd_dt_bias, v_ssd_a_log, v_ssd_d, v_ssd_norm_w, v_lru_conv_w, v_lru_conv_b, v_lru_wa, v_lru_ba, v_lru_wx, v_lru_bx, v_lru_lambda, v_lru_norm_w, v_w_out, v_norm2_w, v_w_gate, v_w_up, v_w_down, v_final_norm_w):
    w = dict(meta_tokens=meta_tokens, norm1_w=norm1_w, w_in=w_in[0], ssd_conv_w=ssd_conv_w[0], ssd_conv_b=ssd_conv_b, ssd_dt_bias=ssd_dt_bias,
             ssd_a_log=ssd_a_log, ssd_d=ssd_d, ssd_norm_w=ssd_norm_w, lru_conv_w=lru_conv_w[0], lru_conv_b=lru_conv_b, lru_wa=lru_wa[0],
             lru_ba=lru_ba, lru_wx=lru_wx[0], lru_bx=lru_bx, lru_lambda=lru_lambda, lru_norm_w=lru_norm_w, w_out=w_out[0], norm2_w=norm2_w,
             w_gate=w_gate[0], w_up=w_up[0], w_down=w_down[0], final_norm_w=final_norm_w.reshape(1, D))
    m = dict(meta_tokens=m_meta_tokens, norm1_w=m_norm1_w, w_in=m_w_in[0], ssd_conv_w=m_ssd_conv_w[0], ssd_conv_b=m_ssd_conv_b,
             ssd_dt_bias=m_ssd_dt_bias, ssd_a_log=m_ssd_a_log, ssd_d=m_ssd_d, ssd_norm_w=m_ssd_norm_w, lru_conv_w=m_lru_conv_w[0],
             lru_conv_b=m_lru_conv_b, lru_wa=m_lru_wa[0], lru_ba=m_lru_ba, lru_wx=m_lru_wx[0], lru_bx=m_lru_bx, lru_lambda=m_lru_lambda,
             lru_norm_w=m_lru_norm_w, w_out=m_w_out[0], norm2_w=m_norm2_w, w_gate=m_w_gate[0], w_up=m_w_up[0], w_down=m_w_down[0],
             final_norm_w=m_final_norm_w.reshape(1, D))
    v = dict(meta_tokens=v_meta_tokens, norm1_w=v_norm1_w, w_in=v_w_in[0], ssd_conv_w=v_ssd_conv_w[0], ssd_conv_b=v_ssd_conv_b,
             ssd_dt_bias=v_ssd_dt_bias, ssd_a_log=v_ssd_a_log, ssd_d=v_ssd_d, ssd_norm_w=v_ssd_norm_w, lru_conv_w=v_lru_conv_w[0],
             lru_conv_b=v_lru_conv_b, lru_wa=v_lru_wa[0], lru_ba=v_lru_ba, lru_wx=v_lru_wx[0], lru_bx=v_lru_bx, lru_lambda=v_lru_lambda,
             lru_norm_w=v_lru_norm_w, w_out=v_w_out[0], norm2_w=v_norm2_w, w_gate=v_w_gate[0], w_up=v_w_up[0], w_down=v_w_down[0],
             final_norm_w=v_final_norm_w.reshape(1, D))
    shapes = dict(meta_tokens=meta_tokens.shape, norm1_w=norm1_w.shape, w_in=w_in.shape, ssd_conv_w=ssd_conv_w.shape,
                  ssd_conv_b=ssd_conv_b.shape, ssd_dt_bias=ssd_dt_bias.shape, ssd_a_log=ssd_a_log.shape, ssd_d=ssd_d.shape,
                  ssd_norm_w=ssd_norm_w.shape, lru_conv_w=lru_conv_w.shape, lru_conv_b=lru_conv_b.shape, lru_wa=lru_wa.shape,
                  lru_ba=lru_ba.shape, lru_wx=lru_wx.shape, lru_bx=lru_bx.shape, lru_lambda=lru_lambda.shape, lru_norm_w=lru_norm_w.shape,
                  w_out=w_out.shape, norm2_w=norm2_w.shape, w_gate=w_gate.shape, w_up=w_up.shape, w_down=w_down.shape,
                  final_norm_w=final_norm_w.shape)
    me = _index(*_place())
    for n in COLUMN_SHARDED:
        w[n], m[n], v[n] = w[n].T, m[n].T, v[n].T

    small_shard = jnp.concatenate([w["meta_tokens"], _pad_cols(w["ssd_conv_w"], 256).reshape(8, 128), w["lru_conv_w"],
                                   jnp.zeros((4, 128), F32)], axis=0)
    g_in, gs = all_gather("gather_w_in", [w["w_in"].astype(BF), small_shard])
    later = ["w_out", "w_gate", "w_up", "w_down"]
    started, (g_in, gs) = gather_start("gather_rest_start", [w[n].astype(BF) for n in later], [g_in, gs])
    started = dict(zip(later, started))
    meta_full = gs[:, 0:16].transpose(1, 0, 2).reshape(N_META, D)
    ssd_cw = gs[:, 16:24].reshape(8, 4, 256)[:, :, :192].transpose(1, 0, 2).reshape(4, XBC)
    lru_cw = gs[:, 24:28].transpose(1, 0, 2).reshape(4, LRU_W)

    def fetch(names, after):
        got, zero = gather_stage("gather_" + (names[0] + "_wait" if names else "pass_on"), [] if names else list(started.values()),
                                 [started[n] for n in names], after)
        return [_unslab(g) for g in got], zero

    in_flight = {}

    def send(grads):
        names = list(grads)
        st, zero = copies_start("grads_" + names[0] + "_start", [grads[n] if n == "small" else _slabs(grads[n]) for n in names])
        in_flight.update(zip(names, st))
        return zero

    loss, grad_x, small = local_step(x[0], loss_target[0], meta_full, ssd_cw, lru_cw, g_in, fetch, send, w)
    send({"small": _pack_small(small, loss).reshape(8, SM_ROWS // 8, 1024)})

    out = {}
    early = ["w_down", "w_gate", "w_up", "w_out"]
    recv = dict(zip(early, copies_wait("grads_early_wait", [in_flight[n] for n in early], [in_flight["small"][2]])))
    for pair in (early[:2], early[2:]):
        done = adamw_shards("adamw_" + pair[0], [recv[n] for n in pair], [w[n] for n in pair], [m[n] for n in pair], [v[n] for n in pair])
        out.update(zip(pair, done))
    recv_in, recv_small = copies_wait("grads_late_wait", [in_flight["w_in"], in_flight["small"]], [out[n][0] for n in early])
    def lines(a):
        return jnp.transpose(a.reshape(D // 128, 128, IN_COLS // 8), (2, 0, 1))

    out["w_in"] = [jnp.transpose(o, (1, 2, 0)).reshape(D, IN_COLS // 8) for o in adamw_w_in(recv_in, lines(w_in), lines(m_w_in), lines(v_w_in))]
    for n in ("w_gate", "w_up"):
        out[n] = [o.T for o in out[n]]
    sm = all_gather("gather_small_grads", [sum_slabs(recv_small)])[0].reshape(SM_ROWS, 1024)
    special_g =[sm[SM_WA:SM_WA + 64].reshape(16, 64, 64), sm[SM_WX:SM_WX + 64].reshape(16, 64, 64),
                 lax.dynamic_slice(sm[SM_META:SM_META + 16], (0, 128 * me), (16, 128)),
                 lax.dynamic_slice(sm[SM_SCW:SM_SCW + 8].reshape(4, 2048), (0, 192 * me), (4, 192)),
                 lax.dynamic_slice(sm[SM_LCW:SM_LCW + 4], (0, 128 * me), (4, 128))]
    names = [n for n, _ in SIMPLE] + SPECIAL
    res = adamw_small(sm, special_g, [w[n] for n in names], [m[n] for n in names], [v[n] for n in names])
    for k, (n, _) in enumerate(SIMPLE):
        out[n] = res[4 * k:4 * k + 4]
    for k, n in enumerate(SPECIAL):
        o = 4 * len(SIMPLE) + 3 * k
        out[n] = [special_g[k]] + list(res[o:o + 3])
    loss_total = sm[SM_LOSS, 0]
    flat = [loss_total, grad_x[None]]
    for k in range(4):
        flat += [out[n][k].reshape(shapes[n]) for n in WEIGHTS]
    return tuple(flat)
```

```python
import math

import jax
import jax.numpy as jnp
from jax import lax
from jax.experimental import pallas as pl
from jax.experimental.pallas import tpu as pltpu

F32 = jnp.float32
BF = jnp.bfloat16

D = 1024
SEQ = 2048
N_META = 16
Q = 128
NPAD = 112
T = NPAD + N_META + SEQ
NCH = T // Q
RC = 544
D_FF = 2816
SSD_W = 1024
LRU_W = 1024
XBC = 1536
IN_COLS = 4624
PZ, PG, PXL, PXBC = 0, 1024, 2048, 3072
NP_IN = 4608
EPS = 1e-6
LRU_C = 8.0
VMEM_LIMIT = 56 * 1024 * 1024

ADAM_LR, ADAM_B1, ADAM_B2, ADAM_EPS, ADAM_WD, ADAM_STEP = 0.001, 0.9, 0.999, 1e-08, 0.01, 10

NT_DIMS = (((1,), (1,)), ((), ()))
TN_DIMS = (((0,), (0,)), ((), ()))
MESH = pl.DeviceIdType.MESH


def _params(n_grid=1, limit=VMEM_LIMIT):
    return pltpu.CompilerParams(dimension_semantics=("arbitrary",) * n_grid, vmem_limit_bytes=limit)


def _spec(shape, imap, single=False):
    if single:
        return pl.BlockSpec(shape, imap, pipeline_mode=pl.Buffered(1))
    return pl.BlockSpec(shape, imap)


def _sigmoid(x):
    return 0.5 * jnp.tanh(0.5 * x) + 0.5


def _sigmoid_gate(x):
    return 1.0 / (1.0 + jnp.exp(-x))


def _softplus(x):
    return jnp.maximum(x, 0.0) + jnp.log(1.0 + jnp.exp(-jnp.abs(x)))


def _rms_stats(h):
    return lax.rsqrt(jnp.mean(h * h, axis=-1, keepdims=True) + EPS)


def _rms(h, w):
    return (h * _rms_stats(h)) * w


def _rms_bwd(du, h, w):
    r = _rms_stats(h)
    n = h * r
    dn = du * w
    dh = r * (dn - n * jnp.mean(dn * n, axis=-1, keepdims=True))
    return dh, du * n


_G0 = math.sqrt(2.0 / math.pi)


def _gelu(x):
    return 0.5 * x * (1.0 + jnp.tanh(_G0 * (x + 0.044715 * (x * x * x))))


def _gelu_grad(x):
    t = jnp.tanh(_G0 * (x + 0.044715 * (x * x * x)))
    return 0.5 * (1.0 + t) + 0.5 * x * (1.0 - t * t) * (_G0 * (1.0 + 3.0 * 0.044715 * (x * x)))


def _rows(shape, r0=0):
    return lax.broadcasted_iota(jnp.int32, shape, 0) + r0


def _lanes(shape):
    return lax.broadcasted_iota(jnp.int32, shape, 1)


HALO = 8


def _fill_padded(pad_ref, x_ref):
    pad_ref[0:HALO, :] = jnp.zeros((HALO, pad_ref.shape[1]), F32)
    pad_ref[T + HALO:T + 2 * HALO, :] = jnp.zeros((HALO, pad_ref.shape[1]), F32)

    def step(c, carry):
        r0 = pl.multiple_of(c * Q, Q)
        pad_ref[pl.ds(r0 + HALO, Q), :] = x_ref[pl.ds(r0, Q), :].astype(F32)
        return carry

    lax.fori_loop(0, NCH, step, 0)


def _back(pad_ref, r0):
    win = pad_ref[pl.ds(r0, Q + HALO), :]
    return lambda s: win[HALO:, :] if s == 0 else pltpu.roll(win, s, axis=0)[HALO:, :]


def _ahead(pad_ref, r0):
    win = pad_ref[pl.ds(r0 + HALO, Q + HALO), :]
    return lambda s: win[:Q, :] if s == 0 else pltpu.roll(win, Q + HALO - s, axis=0)[:Q, :]


def _conv(back, w, b):
    y = b + w[3:4, :] * back(0)
    for k in range(3):
        y = y + w[k:k + 1, :] * back(3 - k)
    return y


def _conv_bwd_x(ahead, w):
    dx = w[3:4, :] * ahead(0)
    for k in range(3):
        dx = dx + w[k:k + 1, :] * ahead(3 - k)
    return dx


def _conv_bwd_w(dy, back):
    dws = [jnp.sum(dy * back(3 - k), axis=0, keepdims=True) for k in range(4)]
    return jnp.concatenate(dws, axis=0), jnp.sum(dy, axis=0, keepdims=True)


def _chunks(fn, unrolled=False):
    if unrolled:
        for c in range(NCH):
            fn(c * Q)
        return

    def step(c, carry):
        fn(pl.multiple_of(c * Q, Q))
        return carry

    lax.fori_loop(0, NCH, step, 0)


HALF = RC // 2


def _col_tiles(n, tn, fn):
    def step(j, carry):
        fn(pl.multiple_of(j * tn, tn))
        return carry

    lax.fori_loop(0, n // tn, step, 0)


def _rows_spec(cols, block_col=0):
    return _spec((RC, cols), lambda i: (i, block_col))


def _whole(shape):
    return _spec(shape, lambda i: tuple(0 for _ in shape), single=True)


def _vec(cols):
    return _spec((1, cols), lambda i: (0, 0))


def _zero_at_first(*refs):
    @pl.when(pl.program_id(0) == 0)
    def _():
        for r in refs:
            r[...] = jnp.zeros_like(r)


ANY_SPEC = pl.BlockSpec(memory_space=pl.ANY)


def _arriving(src, dst, sems, starts, rows):
    n, ahead = len(starts), 2
    first = pl.program_id(0) == 0

    def piece(k):
        r0 = starts[0]
        for j in range(1, n):
            r0 = jnp.where(k == j, starts[j], r0)
        at = pl.ds(pl.multiple_of(r0, 16), rows)
        return pltpu.make_async_copy(src.at[at], dst.at[at], sems.at[k])

    @pl.when(first)
    def _():
        for k in range(min(ahead, n)):
            piece(k).start()

    def ready(k):
        k = jnp.asarray(k, jnp.int32)

        @pl.when(first)
        def _():
            piece(k).wait()

            @pl.when(k + ahead < n)
            def _():
                piece(k + ahead).start()

    return ready


IN_RUNS = ((PZ, 0, 1024), (PXBC, 1024, XBC), (PG, 2576, 2048))
IN_TILE = 512
IN_TILE_ROWS = [wrow + IN_TILE * j for _, wrow, width in IN_RUNS for j in range(width // IN_TILE)]


def _in_tiles(fn):
    done = 0
    for pcol, wrow, width in IN_RUNS:
        def step(j, carry, pcol=pcol, wrow=wrow, done=done):
            fn(pl.multiple_of(pcol + j * IN_TILE, IN_TILE), pl.multiple_of(wrow + j * IN_TILE, 16), done + j)
            return carry

        lax.fori_loop(0, width // IN_TILE, step, 0)
        done += width // IN_TILE


def in_proj(x, meta, wn, w_shards):
    first = NPAD + N_META
    steps = T // RC
    shard = IN_COLS // 8

    def body(x_hbm, meta_ref, wn_ref, g_hbm, o_ref, dt_ref, u_ref, h_ref, wt_hbm, wdt_ref, raw, w_ref, h_scr, g_sems, h_sems, out_sem):
        i = pl.program_id(0)
        slot = i % 2
        shards = [pltpu.make_async_copy(g_hbm.at[j], raw.at[j], g_sems.at[j]) for j in range(8)]
        head = pltpu.make_async_copy(x_hbm.at[pl.ds(0, RC - first)], h_scr.at[0, pl.ds(first, RC - first)], h_sems.at[0])
        put_back = pltpu.make_async_copy(w_ref, wt_hbm, out_sem)

        def rows_of(step):
            return pltpu.make_async_copy(x_hbm.at[pl.ds(pl.multiple_of(step * RC - first, 32), RC)], h_scr.at[step % 2], h_sems.at[step % 2])

        @pl.when(i == 0)
        def _():
            for cp in shards:
                cp.start()
            head.start()
            h_scr[0, 0:NPAD, :] = jnp.zeros((NPAD, D), F32)
            h_scr[0, NPAD:first, :] = meta_ref[...]

        @pl.when(i + 1 < steps)
        def _():
            rows_of(i + 1).start()

        @pl.when(i == 0)
        def _():
            head.wait()

        @pl.when(i > 0)
        def _():
            rows_of(i).wait()

        h_ref[...] = h_scr[slot]
        for r in (0, HALF):
            u_ref[r:r + HALF, :] = _rms(h_scr[slot, r:r + HALF, :], wn_ref[...]).astype(BF)

        @pl.when(i == 0)
        def _():
            for j, cp in enumerate(shards):
                cp.wait()
                w_ref[shard * j:shard * (j + 1), :] = raw[j]
            put_back.start()
            wdt_ref[...] = jnp.zeros_like(wdt_ref)
            for g in range(2):
                wdt_ref[128 * g:128 * g + 8, :] = w_ref[2560 + 8 * g:2568 + 8 * g, :]

        def tile(pcol, wrow, k):
            o_ref[:, pl.ds(pcol, IN_TILE)] = lax.dot_general(u_ref[...], w_ref[pl.ds(wrow, IN_TILE), :], NT_DIMS,
                                                             preferred_element_type=F32).astype(BF)

        _in_tiles(tile)
        dt_ref[...] = lax.dot_general(u_ref[...], wdt_ref[...], NT_DIMS, preferred_element_type=F32)

        @pl.when(i == steps - 1)
        def _():
            put_back.wait()

    return pl.pallas_call(
        body, grid=(steps,), in_specs=[ANY_SPEC, _spec((N_META, D), lambda i: (0, 0)), _vec(D), ANY_SPEC],
        out_specs=[_rows_spec(NP_IN), _rows_spec(256), _rows_spec(D), _rows_spec(D), ANY_SPEC, _spec((256, D), lambda i: (0, 0))],
        out_shape=[jax.ShapeDtypeStruct((T, NP_IN), BF), jax.ShapeDtypeStruct((T, 256), F32), jax.ShapeDtypeStruct((T, D), BF),
                   jax.ShapeDtypeStruct((T, D), F32), jax.ShapeDtypeStruct((IN_COLS, D), BF), jax.ShapeDtypeStruct((256, D), BF)],
        scratch_shapes=[pltpu.VMEM((8, shard, D), BF), pltpu.VMEM((IN_COLS, D), BF), pltpu.VMEM((2, RC, D), F32),
                        pltpu.SemaphoreType.DMA((8,)), pltpu.SemaphoreType.DMA((2,)), pltpu.SemaphoreType.DMA],
        compiler_params=_params(), name="in_proj")(x, meta, wn, w_shards)


def out_proj(yn_ssd, proj, hseq, lru_nw, w_out, h0):
    def body(y_ref, g_ref, h_ref, wn_ref, w_ref, r_ref, o_ref, cat_ref):
        cat_ref[:, 0:SSD_W] = y_ref[...]
        for r in (0, HALF):
            y = _gelu(g_ref[r:r + HALF, :].astype(F32)) * h_ref[r:r + HALF, :]
            cat_ref[r:r + HALF, SSD_W:] = _rms(y, wn_ref[...]).astype(BF)

        def tile(c0):
            o_ref[:, pl.ds(c0, 512)] = r_ref[:, pl.ds(c0, 512)] + jnp.dot(cat_ref[...], w_ref[:, pl.ds(c0, 512)], preferred_element_type=F32)

        _col_tiles(D, 512, tile)

    return pl.pallas_call(
        body, grid=(T // RC,),
        in_specs=[_rows_spec(SSD_W), _rows_spec(LRU_W, PG // LRU_W), _rows_spec(LRU_W), _vec(LRU_W), _whole((SSD_W + LRU_W, D)), _rows_spec(D)],
        out_specs=[_rows_spec(D), _rows_spec(SSD_W + LRU_W)],
        out_shape=[jax.ShapeDtypeStruct((T, D), F32), jax.ShapeDtypeStruct((T, SSD_W + LRU_W), BF)],
        compiler_params=_params(), name="out_proj")(yn_ssd, proj, hseq, lru_nw, w_out, h0)


def out_proj_bwd(dh1_b, w_out, proj, hseq, lru_nw, after):
    def body(d_ref, w_ref, g_ref, h_ref, wn_ref, _after, dy_ref, dh_ref, dg_ref, dw_ref, dl_scr):
        _zero_at_first(dw_ref)

        def tile(c0):
            dy_ref[:, pl.ds(c0, 512)] = lax.dot_general(d_ref[...], w_ref[pl.ds(c0, 512), :], NT_DIMS, preferred_element_type=F32)
            dl_scr[:, pl.ds(c0, 512)] = lax.dot_general(d_ref[...], w_ref[pl.ds(SSD_W + c0, 512), :], NT_DIMS, preferred_element_type=F32)

        _col_tiles(SSD_W, 512, tile)

        for r in (0, HALF):
            g = g_ref[r:r + HALF, :].astype(F32)
            h = h_ref[r:r + HALF, :]
            ge = _gelu(g)
            dy, dw = _rms_bwd(dl_scr[r:r + HALF, :], ge * h, wn_ref[...])
            dw_ref[...] += jnp.sum(dw, axis=0, keepdims=True)
            dh_ref[r:r + HALF, :] = dy * ge
            dg_ref[r:r + HALF, :] = (dy * h * _gelu_grad(g)).astype(BF)

    return pl.pallas_call(
        body, grid=(T // RC,),
        in_specs=[_rows_spec(D), _whole((SSD_W + LRU_W, D)), _rows_spec(LRU_W, PG // LRU_W), _rows_spec(LRU_W), _vec(LRU_W), ANY_SPEC],
        out_specs=[_rows_spec(SSD_W), _rows_spec(LRU_W), _rows_spec(LRU_W), _vec(LRU_W)],
        out_shape=[jax.ShapeDtypeStruct((T, SSD_W), F32), jax.ShapeDtypeStruct((T, LRU_W), F32), jax.ShapeDtypeStruct((T, LRU_W), BF),
                   jax.ShapeDtypeStruct((1, LRU_W), F32)],
        scratch_shapes=[pltpu.VMEM((RC, LRU_W), F32)],
        compiler_params=_params(), name="out_proj_bwd")(dh1_b, w_out, proj, hseq, lru_nw, after)


def in_proj_bwd(dz, dg, dxl, dxbc, ddt, w_t, w_dt, h0, wn, dh1, after):
    first = NPAD + N_META

    def body(dz_ref, dg_ref, dxl_ref, dxbc_ref, ddt_ref, w_hbm, wdt_ref, h_ref, wn_ref, r_ref, _after, gx_hbm, meta_ref, dw_ref, du_scr, o_ref, sem,
             w_ref, w_sems):
        i = pl.program_id(0)
        ready = _arriving(w_hbm, w_ref, w_sems, IN_TILE_ROWS, IN_TILE)
        _zero_at_first(dw_ref)
        du_scr[...] = jnp.dot(ddt_ref[...], wdt_ref[...], preferred_element_type=F32)
        done = 0
        for d_ref, wrow, width in ((dz_ref, 0, 1024), (dxbc_ref, 1024, XBC), (dg_ref, 2576, 1024), (dxl_ref, 3600, 1024)):
            def step(j, carry, d_ref=d_ref, wrow=wrow, done=done):
                c0 = pl.multiple_of(j * IN_TILE, IN_TILE)
                ready(done + j)
                du_scr[...] += jnp.dot(d_ref[:, pl.ds(c0, IN_TILE)], w_ref[pl.ds(pl.multiple_of(wrow + c0, 16), IN_TILE), :],
                                       preferred_element_type=F32)
                return carry

            lax.fori_loop(0, width // IN_TILE, step, 0)
            done += width // IN_TILE
        for r in (0, HALF):
            dh, dw = _rms_bwd(du_scr[r:r + HALF, :], h_ref[r:r + HALF, :], wn_ref[...])
            dw_ref[...] += jnp.sum(dw, axis=0, keepdims=True)
            o_ref[r:r + HALF, :] = dh + r_ref[r:r + HALF, :]

        @pl.when(i == 0)
        def _():
            meta_ref[...] = o_ref[NPAD:first, :]
            head = pltpu.make_async_copy(o_ref.at[pl.ds(first, RC - first)], gx_hbm.at[pl.ds(0, RC - first)], sem)
            head.start()
            head.wait()

        @pl.when(i > 0)
        def _():
            rest = pltpu.make_async_copy(o_ref, gx_hbm.at[pl.ds(pl.multiple_of(i * RC - first, 32), RC)], sem)
            rest.start()
            rest.wait()

    return pl.pallas_call(
        body, grid=(T // RC,),
        in_specs=[_rows_spec(SSD_W), _rows_spec(LRU_W), _rows_spec(LRU_W), _rows_spec(XBC), _rows_spec(256), ANY_SPEC,
                  _whole((256, D)), _rows_spec(D), _vec(D), _rows_spec(D), ANY_SPEC],
        out_specs=[ANY_SPEC, _spec((N_META, D), lambda i: (0, 0)), _vec(D)],
        out_shape=[jax.ShapeDtypeStruct((SEQ, D), F32), jax.ShapeDtypeStruct((N_META, D), F32), jax.ShapeDtypeStruct((1, D), F32)],
        scratch_shapes=[pltpu.VMEM((RC, D), F32), pltpu.VMEM((RC, D), F32), pltpu.SemaphoreType.DMA,
                        pltpu.VMEM((IN_COLS, D), BF), pltpu.SemaphoreType.DMA((len(IN_TILE_ROWS),))],
        compiler_params=_params(), name="in_proj_bwd")(dz, dg, dxl, dxbc, ddt, w_t, w_dt, h0, wn, dh1, after)


GRAD_TILE = 256


def weight_grad(name, a, u1):
    tm = GRAD_TILE

    def body(a_ref, u_ref, o_ref):
        o_ref[...] = lax.dot_general(a_ref[...], u_ref[...], TN_DIMS, preferred_element_type=F32).astype(BF)

    return pl.pallas_call(
        body, grid=(a.shape[1] // tm,),
        in_specs=[_spec((T, tm), lambda j: (0, j)), _spec((T, D), lambda j: (0, 0), single=True)],
        out_specs=_spec((tm, D), lambda j: (j, 0)),
        out_shape=jax.ShapeDtypeStruct((a.shape[1], D), BF),
        compiler_params=_params(), name=name)(a, u1)


def in_weight_grad(parts, first_rows, ddt, u1):
    tm = GRAD_TILE
    per_row = D // 128
    dt_row, dt_lines = 2560, 8 * per_row
    parts = list(parts) + [ddt]
    first_rows = list(first_rows) + [dt_row]
    tiles = [p.shape[1] // tm for p in parts]
    starts = [sum(tiles[:k]) for k in range(len(parts))]
    last = sum(tiles) - 1

    def body(*refs):
        a_refs, u_ref = refs[:len(parts)], refs[len(parts)]
        o_hbm, mix_scr, stage, sems = refs[len(parts) + 1:]
        step = pl.program_id(0)
        slot = step % 2
        line0 = 0
        for a_ref, start, n, first in zip(a_refs, starts, tiles, first_rows):
            here = (step >= start) & (step < start + n)
            line0 = jnp.where(here, per_row * (first + tm * (step - start)), line0)

            @pl.when(here)
            def _(a_ref=a_ref):
                res = lax.dot_general(a_ref[...], u_ref[...], TN_DIMS, preferred_element_type=F32)
                for q in range(per_row):
                    mix_scr[pl.ds(q, tm, stride=per_row), :] = res[:, 128 * q:128 * q + 128]

        def tile_copy(of_slot, to):
            return pltpu.make_async_copy(stage.at[of_slot], o_hbm.at[pl.ds(to, per_row * tm)], sems.at[of_slot])

        @pl.when(step >= 2)
        def _():
            tile_copy(slot, 0).wait()

        stage[slot] = mix_scr[...].astype(BF)

        @pl.when(step < last)
        def _():
            tile_copy(slot, pl.multiple_of(line0, 128)).start()

        @pl.when(step == last)
        def _():
            halves = [pltpu.make_async_copy(stage.at[slot, pl.ds(128 * per_row * k, dt_lines)],
                                            o_hbm.at[pl.ds(per_row * (dt_row + 8 * k), dt_lines)], sems.at[2 + k]) for k in range(2)]
            for cp in halves:
                cp.start()
            tile_copy(1 - slot, 0).wait()
            for cp in halves:
                cp.wait()

    def tile_of(start, n):
        return lambda j: (0, jnp.clip(j - start, 0, n - 1))

    return pl.pallas_call(
        body, grid=(last + 1,),
        in_specs=[_spec((T, tm), tile_of(s, n)) for s, n in zip(starts, tiles)] + [_spec((T, D), lambda j: (0, 0), single=True)],
        out_specs=ANY_SPEC,
        out_shape=jax.ShapeDtypeStruct((per_row * IN_COLS, 128), BF),
        scratch_shapes=[pltpu.VMEM((per_row * tm, 128), F32), pltpu.VMEM((2, per_row * tm, 128), BF), pltpu.SemaphoreType.DMA((4,))],
        compiler_params=_params(), name="dw_in")(*parts, u1)


def _ssd_chunk_common(row0, dt_ref, b_ref, c_ref, bias, a_neg):
    shape = (Q, Q)
    lane = _lanes(shape)
    sub = _rows(shape)
    live = (_rows(shape, row0) >= NPAD) & (lane < 8)
    dtr = dt_ref[:, :]
    dt = jnp.where(live, _softplus(dtr + bias), 0.0)
    d_a = dt * a_neg
    tri = (sub >= lane).astype(F32)
    cs = jnp.dot(tri, d_a, precision=lax.Precision.HIGHEST, preferred_element_type=F32)
    cs_t = cs.T
    b_f = b_ref[:, :]
    bc = b_f.astype(BF)
    cc = c_ref[:, :].astype(BF)
    cb = lax.dot_general(cc, bc, NT_DIMS, preferred_element_type=F32)
    cs_last = cs[Q - 1:Q, :]
    return dict(lane=lane, sub=sub, live=live, dtr=dtr, dt=dt, cs=cs, cs_t=cs_t, bc=bc, cc=cc, cb=cb, bc_t=b_f.T.astype(BF),
                ecs=jnp.exp(cs), dsm=jnp.exp(cs_last - cs), gam=jnp.exp(cs_last))


def _pair(lane_even, mat, j):
    return jnp.where(lane_even, mat[:, j:j + 1], mat[:, j + 1:j + 2])


def _pair_row(lane_even, mat, j):
    return jnp.where(lane_even[0:1, :], mat[:, j:j + 1], mat[:, j + 1:j + 2])


def _head_decay(cm, j):
    seg = cm["cs"][:, j:j + 1] - cm["cs_t"][j:j + 1, :]
    return jnp.exp(jnp.where(cm["sub"] >= cm["lane"], seg, -jnp.inf))


def _head_decay_t(cm, j):
    seg = cm["cs_t"][j:j + 1, :] - cm["cs"][:, j:j + 1]
    return jnp.exp(jnp.where(cm["lane"] >= cm["sub"], seg, -jnp.inf))


def _conv_window(raw_ref, halo_ref, pad_scr):
    pad_scr[0:HALO, :] = halo_ref[...].astype(F32)[halo_ref.shape[0] - HALO:, :]
    pad_scr[HALO:HALO + Q, :] = raw_ref[...].astype(F32)
    win = pad_scr[...]
    return lambda s: win[HALO:, :] if s == 0 else pltpu.roll(win, s, axis=0)[HALO:, :]


def _xbc_cols(g):
    return slice(512 * g, 512 * g + 512), slice(SSD_W + 128 * g, SSD_W + 128 * g + 128), slice(SSD_W + 256 + 128 * g, SSD_W + 384 + 128 * g)


def ssd_fwd(proj, dt_raw, conv_w, conv_b, dt_bias2, a_log2, d2, norm_w):
    def body(raw_ref, halo_ref, dt_all, z_all, cw_ref, cb_ref, bias_all, alog_all, d_all, nw_all, yn_all, y_all, hp_all,
             h_all, pad_scr, act_scr):
        @pl.when(pl.program_id(0) == 0)
        def _():
            h_all[...] = jnp.zeros_like(h_all)

        pre = _conv(_conv_window(raw_ref, halo_ref, pad_scr), cw_ref[...], cb_ref[...])
        act_scr[...] = pre * _sigmoid(pre)
        for g in range(2):
            wide, thin = slice(512 * g, 512 * g + 512), slice(128 * g, 128 * g + 128)
            xs, bs, cs = _xbc_cols(g)
            group(act_scr.at[:, xs], act_scr.at[:, bs], act_scr.at[:, cs], dt_all.at[:, thin], z_all.at[:, wide], bias_all.at[g],
                  alog_all.at[g], d_all.at[g], nw_all.at[:, wide], yn_all.at[:, wide], y_all.at[:, wide], hp_all.at[g, 0], h_all.at[g])

    def group(x_ref, b_ref, c_ref, dt_ref, z_ref, bias_ref, alog_ref, d_ref, nw_ref, yn_ref, y_ref, hp_ref, h_scr):
        bias = bias_ref[...]
        a_neg = -jnp.exp(alog_ref[...])
        dsk = d_ref[...]
        cm = _ssd_chunk_common(pl.program_id(0) * Q, dt_ref, b_ref, c_ref, bias, a_neg)
        lane_even = cm["lane"] < 64
        for p in range(4):
            je, jo = 2 * p, 2 * p + 1
            xp = x_ref[:, 128 * p:128 * p + 128]
            xdt = xp * _pair(lane_even, cm["dt"], je)
            xdt_b = xdt.astype(BF)
            m_e = (cm["cb"] * _head_decay(cm, je)).astype(BF)
            m_o = (cm["cb"] * _head_decay(cm, jo)).astype(BF)
            zero = jnp.zeros_like(xdt_b)
            yd = (jnp.dot(m_e, jnp.where(lane_even, xdt_b, zero), preferred_element_type=F32)
                  + jnp.dot(m_o, jnp.where(lane_even, zero, xdt_b), preferred_element_type=F32))
            hp = h_scr[p]
            hp_ref[p] = hp
            yo = jnp.dot(cm["cc"], hp.astype(BF), preferred_element_type=F32) * _pair(lane_even, cm["ecs"], je)
            y_ref[:, 128 * p:128 * p + 128] = yd + yo + xp * _pair_row(lane_even, dsk, je)
            st = jnp.dot(cm["bc_t"], (xdt * _pair(lane_even, cm["dsm"], je)).astype(BF), preferred_element_type=F32)
            h_scr[p] = hp * _pair_row(lane_even, cm["gam"], je) + st
        zc = z_ref[:, :].astype(F32)
        gated = y_ref[:, :] * (zc * _sigmoid(zc))
        yn_ref[:, :] = _rms(gated, nw_ref[...]).astype(BF)

    par = _spec((2, 1, 128), lambda c: (0, 0, 0))
    wide = _spec((Q, SSD_W), lambda c: (c, 0))
    xbc = PXBC // XBC
    halo = 2 * HALO
    return pl.pallas_call(
        body, grid=(NCH,),
        in_specs=[_spec((Q, XBC), lambda c: (c, xbc)), _spec((halo, XBC), lambda c: (jnp.maximum(c * (Q // halo) - 1, 0), xbc)),
                  _spec((Q, 256), lambda c: (c, 0)), wide, _spec((4, XBC), lambda c: (0, 0)), _spec((1, XBC), lambda c: (0, 0)),
                  par, par, par, _spec((1, SSD_W), lambda c: (0, 0))],
        out_specs=[wide, wide, _spec((2, 1, 4, 128, 128), lambda c: (0, c, 0, 0, 0))],
        out_shape=[jax.ShapeDtypeStruct((T, SSD_W), BF), jax.ShapeDtypeStruct((T, SSD_W), F32),
                   jax.ShapeDtypeStruct((2, NCH, 4, 128, 128), F32)],
        scratch_shapes=[pltpu.VMEM((2, 4, 128, 128), F32), pltpu.VMEM((Q + HALO, XBC), F32), pltpu.VMEM((Q, XBC), F32)],
        compiler_params=_params(), name="ssd_fwd")(proj, proj, dt_raw, proj, conv_w, conv_b, dt_bias2, a_log2, d2, norm_w)


def ssd_bwd(dyn, proj, dt_raw, conv_w, conv_b, y_pre, h_prev, dt_bias2, a_log2, d2, norm_w, after):
    def body(dyn_all, raw_ref, halo_ref, dt_all, z_all, y_all, hp_all, cw_ref, cb_ref, bias_all, alog_all, d_all, nw_all, _after,
             dz_all, dxbc_ref, ddt_all, dpar_all, dnw_all, dcw_ref, dcb_ref, dh_all, acc_all, pad_scr, act_scr, dsilu_scr, dact_scr, dpad_scr):
        @pl.when(pl.program_id(0) == 0)
        def _():
            dh_all[...] = jnp.zeros_like(dh_all)
            acc_all[...] = jnp.zeros_like(acc_all)
            dnw_all[...] = jnp.zeros_like(dnw_all)
            dcw_ref[...] = jnp.zeros_like(dcw_ref)
            dcb_ref[...] = jnp.zeros_like(dcb_ref)
            dpad_scr[Q:Q + HALO, :] = jnp.zeros((HALO, XBC), F32)

        back = _conv_window(raw_ref, halo_ref, pad_scr)
        pre = _conv(back, cw_ref[...], cb_ref[...])
        sg = _sigmoid(pre)
        act_scr[...] = pre * sg
        dsilu_scr[...] = sg * (1.0 + pre * (1.0 - sg))
        for g in range(2):
            wide, thin = slice(512 * g, 512 * g + 512), slice(128 * g, 128 * g + 128)
            xs, bs, cs = _xbc_cols(g)
            group(dyn_all.at[:, wide], act_scr.at[:, xs], act_scr.at[:, bs], act_scr.at[:, cs], dt_all.at[:, thin], z_all.at[:, wide],
                  y_all.at[:, wide], hp_all.at[g, 0], bias_all.at[g], alog_all.at[g], d_all.at[g], nw_all.at[:, wide],
                  dz_all.at[:, wide], dact_scr.at[:, xs], dact_scr.at[:, bs], dact_scr.at[:, cs], ddt_all.at[:, thin], dpar_all.at[g],
                  dnw_all.at[:, wide], dh_all.at[g], acc_all.at[g])
        dpre = dact_scr[...] * dsilu_scr[...]
        dcw, dcb = _conv_bwd_w(dpre, back)
        dcw_ref[...] += dcw
        dcb_ref[...] += dcb
        dpad_scr[0:Q, :] = dpre
        win = dpad_scr[...]
        dxbc_ref[...] = _conv_bwd_x(lambda s: win[:Q, :] if s == 0 else pltpu.roll(win, Q + HALO - s, axis=0)[:Q, :], cw_ref[...]).astype(BF)
        dpad_scr[Q:Q + HALO, :] = dpre[0:HALO, :]

    def group(dyn_ref, x_ref, b_ref, c_ref, dt_ref, z_ref, y_ref, hp_ref, bias_ref, alog_ref, d_ref, nw_ref,
              dz_ref, dx_ref, db_ref, dc_ref, ddt_ref, dpar_ref, dnw_ref, dh_scr, acc_scr):
        ci = pl.program_id(0)
        bias = bias_ref[...]
        a_neg = -jnp.exp(alog_ref[...])
        dsk = d_ref[...]
        cm = _ssd_chunk_common((NCH - 1 - ci) * Q, dt_ref, b_ref, c_ref, bias, a_neg)
        lane, sub = cm["lane"], cm["sub"]
        lane_even = lane < 64
        cc_t = c_ref[:, :].T.astype(BF)
        cb_t = lax.dot_general(cm["bc"], cm["cc"], NT_DIMS, preferred_element_type=F32)
        zc = z_ref[:, :].astype(F32)
        yc = y_ref[:, :]
        sg = _sigmoid(zc)
        sz = zc * sg
        dgated, dnw = _rms_bwd(dyn_ref[:, :], yc * sz, nw_ref[...])
        dnw_ref[...] += jnp.sum(dnw, axis=0, keepdims=True)
        dz_ref[:, :] = (dgated * yc * (sg * (1.0 + zc * (1.0 - sg)))).astype(BF)
        dy_all = dgated * sz
        dcb = jnp.zeros((Q, Q), F32)
        dcb_t = jnp.zeros((Q, Q), F32)
        db_acc = jnp.zeros((Q, Q), F32)
        dc_acc = jnp.zeros((Q, Q), F32)
        dcs = jnp.zeros((Q, Q), F32)
        ddt = jnp.zeros((Q, Q), F32)
        for p in range(4):
            je, jo = 2 * p, 2 * p + 1
            xp = x_ref[:, 128 * p:128 * p + 128]
            dy = dy_all[:, 128 * p:128 * p + 128]
            dt_p = _pair(lane_even, cm["dt"], je)
            xdt = xp * dt_p
            xdt_b = xdt.astype(BF)
            dy_b = dy.astype(BF)
            zero = jnp.zeros_like(dy_b)
            hp = hp_ref[p]
            hp_b = hp.astype(BF)
            dh = dh_scr[p]
            dh_b = dh.astype(BF)
            acc_scr[p:p + 1, :] += jnp.sum(dy * xp, axis=0, keepdims=True)
            dxp = dy * _pair_row(lane_even, dsk, je)
            e_p = _pair(lane_even, cm["ecs"], je)
            g_p = jnp.dot(cm["cc"], hp_b, preferred_element_type=F32)
            dg_b = (dy * e_p).astype(BF)
            de = dy * g_p * e_p
            dc_acc = dc_acc + lax.dot_general(dg_b, hp_b, NT_DIMS, preferred_element_type=F32)
            dh_in = jnp.dot(cc_t, dg_b, preferred_element_type=F32)
            ds_p = _pair(lane_even, cm["dsm"], je)
            r_p = jnp.dot(cm["bc"], dh_b, preferred_element_type=F32)
            dxdt = r_p * ds_p
            tt = r_p * xdt * ds_p
            db_acc = db_acc + lax.dot_general((xdt * ds_p).astype(BF), dh_b, NT_DIMS, preferred_element_type=F32)
            dgam_m = jnp.sum(dh * hp, axis=0, keepdims=True)
            for j, even in ((je, True), (jo, False)):
                sel = lane_even if even else jnp.logical_not(lane_even)
                dy_j = jnp.where(sel, dy_b, zero)
                l_j = _head_decay(cm, j)
                l_jt = _head_decay_t(cm, j)
                m_j = cm["cb"] * l_j
                m_jt = cb_t * l_jt
                dm = lax.dot_general(dy_j, xdt_b, NT_DIMS, preferred_element_type=F32)
                dm_t = lax.dot_general(xdt_b, dy_j, NT_DIMS, preferred_element_type=F32)
                dxdt = dxdt + jnp.dot(m_jt.astype(BF), dy_j, preferred_element_type=F32)
                dcb = dcb + dm * l_j
                dcb_t = dcb_t + dm_t * l_jt
                t_j = jnp.where(sel, tt, 0.0)
                col = jnp.sum(dm * m_j - dm_t * m_jt + (jnp.where(sel, de, 0.0) - t_j), axis=1, keepdims=True)
                gam_j = cm["gam"][:, j:j + 1]
                last = (jnp.sum(jnp.sum(t_j, axis=0, keepdims=True), axis=1, keepdims=True)
                        + jnp.sum(jnp.where(sel[0:1, :], dgam_m, 0.0), axis=1, keepdims=True) * gam_j)
                col = col + jnp.where(sub[:, 0:1] == Q - 1, last, 0.0)
                dcs = dcs + jnp.where(lane == j, col, 0.0)
            dh_scr[p] = dh_in + dh * _pair_row(lane_even, cm["gam"], je)
            dx_ref[:, 128 * p:128 * p + 128] = dxp + dxdt * dt_p
            dd = dxdt * xp
            ddt = ddt + jnp.where(lane == je, jnp.sum(jnp.where(lane_even, dd, 0.0), axis=1, keepdims=True), 0.0)
            ddt = ddt + jnp.where(lane == jo, jnp.sum(jnp.where(lane_even, 0.0, dd), axis=1, keepdims=True), 0.0)
        dc_ref[:, :] = dc_acc + jnp.dot(dcb.astype(BF), cm["bc"], preferred_element_type=F32)
        db_ref[:, :] = db_acc + jnp.dot(dcb_t.astype(BF), cm["cc"], preferred_element_type=F32)
        tri_t = (sub <= lane).astype(F32)
        dd_a = jnp.dot(tri_t, dcs, precision=lax.Precision.HIGHEST, preferred_element_type=F32)
        ddt = ddt + dd_a * a_neg
        acc_scr[5:6, :] += jnp.sum(dd_a * cm["dt"], axis=0, keepdims=True)
        draw = jnp.where(cm["live"], ddt * _sigmoid_gate(cm["dtr"] + bias), 0.0)
        acc_scr[4:5, :] += jnp.sum(draw, axis=0, keepdims=True)
        ddt_ref[:, :] = draw.astype(BF)

        @pl.when(ci == NCH - 1)
        def _():
            lane1 = _lanes((1, 128))
            dd = jnp.zeros((1, 128), F32)
            for p in range(4):
                row = acc_scr[p:p + 1, :]
                dd = dd + jnp.where(lane1 == 2 * p, jnp.sum(jnp.where(lane1 < 64, row, 0.0), axis=1, keepdims=True), 0.0)
                dd = dd + jnp.where(lane1 == 2 * p + 1, jnp.sum(jnp.where(lane1 < 64, 0.0, row), axis=1, keepdims=True), 0.0)
            dpar_ref[...] = jnp.concatenate([acc_scr[4:5, :], acc_scr[5:6, :] * a_neg, dd, jnp.zeros((5, 128), F32)], axis=0)

    par = _spec((2, 1, 128), lambda c: (0, 0, 0))
    wide = _spec((Q, SSD_W), lambda c: (NCH - 1 - c, 0))
    thin = _spec((Q, 256), lambda c: (NCH - 1 - c, 0))
    vec = _spec((1, SSD_W), lambda c: (0, 0))
    xbc = PXBC // XBC
    halo = 2 * HALO
    chunk = pltpu.VMEM((Q, XBC), F32)
    padded = pltpu.VMEM((Q + HALO, XBC), F32)
    return pl.pallas_call(
        body, grid=(NCH,),
        in_specs=[wide, _spec((Q, XBC), lambda c: (NCH - 1 - c, xbc)),
                  _spec((halo, XBC), lambda c: (jnp.maximum((NCH - 1 - c) * (Q // halo) - 1, 0), xbc)), thin, wide, wide,
                  _spec((2, 1, 4, 128, 128), lambda c: (0, NCH - 1 - c, 0, 0, 0)), _spec((4, XBC), lambda c: (0, 0)),
                  _spec((1, XBC), lambda c: (0, 0)), par, par, par, vec, ANY_SPEC],
        out_specs=[wide, _spec((Q, XBC), lambda c: (NCH - 1 - c, 0)), thin, _spec((2, 8, 128), lambda c: (0, 0, 0)), vec,
                   _spec((4, XBC), lambda c: (0, 0)), _spec((1, XBC), lambda c: (0, 0))],
        out_shape=[jax.ShapeDtypeStruct((T, SSD_W), BF), jax.ShapeDtypeStruct((T, XBC), BF), jax.ShapeDtypeStruct((T, 256), BF),
                   jax.ShapeDtypeStruct((2, 8, 128), F32), jax.ShapeDtypeStruct((1, SSD_W), F32), jax.ShapeDtypeStruct((4, XBC), F32),
                   jax.ShapeDtypeStruct((1, XBC), F32)],
        scratch_shapes=[pltpu.VMEM((2, 4, 128, 128), F32), pltpu.VMEM((2, 8, 128), F32), padded, chunk, chunk, chunk, padded],
        compiler_params=_params(), name="ssd_bwd")(dyn, proj, proj, dt_raw, proj, y_pre, h_prev, conv_w, conv_b, dt_bias2, a_log2, d2, norm_w, after)


def _lru_gates(back, cw, cb, wa, ba, wx, bx, lam):
    xr = _conv(back, cw, cb)
    xr_b = xr.astype(BF)
    r = _sigmoid_gate(jnp.dot(xr_b, wa, preferred_element_type=F32) + ba)
    i = _sigmoid_gate(jnp.dot(xr_b, wx, preferred_element_type=F32) + bx)
    sp = _softplus(-lam)
    la = (-LRU_C) * r * sp
    a = jnp.exp(la)
    mult2 = -jnp.tanh(la) * (a * a + 1.0)
    return xr, xr_b, r, i, sp, a, jnp.sqrt(mult2), mult2


SEG_LEN = 68
SEGS = T // SEG_LEN


def _seg_rows(j, k, off=0):
    return pl.ds(off + j * 8 * SEG_LEN + k, 8, stride=SEG_LEN)


def _segmented_scan(mul_ref, mul_row0, add_ref, out_ref, loc_scr, prod_scr, carry_scr, reverse):
    groups = SEGS // 8
    off = mul_row0 + (1 if reverse else 0)

    def local(i, carry):
        k = SEG_LEN - 1 - i if reverse else i
        new = []
        for j in range(groups):
            h, p = carry[2 * j], carry[2 * j + 1]
            m = mul_ref[_seg_rows(j, k, off), :]
            h = m * h + add_ref[_seg_rows(j, k), :]
            p = m * p
            loc_scr[_seg_rows(j, k), :] = h
            prod_scr[_seg_rows(j, k), :] = p
            new += [h, p]
        return tuple(new)

    lax.fori_loop(0, SEG_LEN, local, (jnp.zeros((8, 128), F32), jnp.ones((8, 128), F32)) * groups)

    def chain(i, c):
        s = SEGS - 1 - i if reverse else i
        carry_scr[pl.ds(s, 1), :] = c
        edge = s * SEG_LEN + (0 if reverse else SEG_LEN - 1)
        return loc_scr[pl.ds(edge, 1), :] + prod_scr[pl.ds(edge, 1), :] * c

    lax.fori_loop(0, SEGS, chain, jnp.zeros((1, 128), F32))

    def fold(k, carry):
        for j in range(groups):
            rows = _seg_rows(j, k)
            out_ref[rows, :] = loc_scr[rows, :] + prod_scr[rows, :] * carry_scr[8 * j:8 * j + 8, :]
        return carry

    lax.fori_loop(0, SEG_LEN, fold, 0)


def lru_fwd(proj, cw, cb, wa2, ba, wx2, bx, lam, after):
    def body(x_ref, cw_ref, cb_ref, wa_ref, ba_ref, wx_ref, bx_ref, lam_ref, _after, h_ref, a_ref, xpad, u_scr, loc_scr, prod_scr, carry_scr):
        _fill_padded(xpad, x_ref)

        def chunk(r0):
            xr, _, _, i, _, a, mult, _ = _lru_gates(_back(xpad, r0), cw_ref[...], cb_ref[...], wa_ref[0], ba_ref[...], wx_ref[0], bx_ref[...],
                                                 lam_ref[...])
            a_ref[pl.ds(r0, Q), :] = a
            u_scr[pl.ds(r0, Q), :] = jnp.where(_rows(a.shape, r0) >= NPAD, mult * (i * xr), 0.0)

        _chunks(chunk, unrolled=True)
        _segmented_scan(a_ref, 0, u_scr, h_ref, loc_scr, prod_scr, carry_scr, reverse=False)

    c0 = PXL // 128
    vec = _spec((1, 128), lambda c: (0, c))
    mat = _spec((1, 128, 128), lambda c: (c, 0, 0))
    seq = pltpu.VMEM((T, 128), F32)
    return pl.pallas_call(
        body, grid=(8,),
        in_specs=[_spec((T, 128), lambda c: (0, c0 + c)), _spec((4, 128), lambda c: (0, c)), vec, mat, vec, mat, vec, vec, ANY_SPEC],
        out_specs=[_spec((T, 128), lambda c: (0, c)), _spec((T, 128), lambda c: (0, c))],
        out_shape=[jax.ShapeDtypeStruct((T, LRU_W), F32), jax.ShapeDtypeStruct((T, LRU_W), F32)],
        scratch_shapes=[pltpu.VMEM((T + 2 * HALO, 128), F32), seq, seq, seq, pltpu.VMEM((SEGS, 128), F32)],
        compiler_params=_params(), name="lru_fwd")(proj, cw, cb, wa2, ba, wx2, bx, lam, after)


def lru_bwd(dh_out, a, hseq, proj, cw, cb, wa2, ba, wx2, bx, lam):
    def body(d_ref, a_ref, h_ref, x_ref, cw_ref, cb_ref, wa_ref, ba_ref, wx_ref, bx_ref, lam_ref,
             dx_ref, dcw_ref, dcb_ref, dwa_ref, dba_ref, dwx_ref, dbx_ref, dlam_ref, xpad, hpad, dpad, dh_ref, loc_scr, prod_scr, carry_scr):
        _fill_padded(dpad, a_ref)
        _segmented_scan(dpad, HALO, d_ref, dh_ref, loc_scr, prod_scr, carry_scr, reverse=True)
        _fill_padded(xpad, x_ref)
        _fill_padded(hpad, h_ref)
        dpad[0:HALO, :] = jnp.zeros((HALO, 128), F32)
        dpad[T + HALO:T + 2 * HALO, :] = jnp.zeros((HALO, 128), F32)
        for ref in (dcw_ref, dcb_ref, dwa_ref, dba_ref, dwx_ref, dbx_ref, dlam_ref):
            ref[...] = jnp.zeros_like(ref)
        lam = lam_ref[...]

        def first(r0):
            back = _back(xpad, r0)
            xr, xr_b, r, i, sp, a, mult, mult2 = _lru_gates(back, cw_ref[...], cb_ref[...], wa_ref[0], ba_ref[...], wx_ref[0], bx_ref[...], lam)
            dh = dh_ref[pl.ds(r0, Q), :]
            da = dh * _back(hpad, r0)(1)
            du = jnp.where(_rows(dh.shape, r0) >= NPAD, dh, 0.0)
            dmult = du * (i * xr)
            di = du * (mult * xr)
            dxr = du * (mult * i)
            dla = da * a - dmult * (a * a) * lax.rsqrt(mult2)
            dr = dla * ((-LRU_C) * sp)
            dlam_ref[...] += jnp.sum(dla * ((-LRU_C) * r), axis=0, keepdims=True)
            dpr = dr * r * (1.0 - r)
            dpi = di * i * (1.0 - i)
            dba_ref[...] += jnp.sum(dpr, axis=0, keepdims=True)
            dbx_ref[...] += jnp.sum(dpi, axis=0, keepdims=True)
            dpr_b = dpr.astype(BF)
            dpi_b = dpi.astype(BF)
            dxr = (dxr + lax.dot_general(dpr_b, wa_ref[0], NT_DIMS, preferred_element_type=F32)
                   + lax.dot_general(dpi_b, wx_ref[0], NT_DIMS, preferred_element_type=F32))
            dwa_ref[0] += lax.dot_general(xr_b, dpr_b, TN_DIMS, preferred_element_type=F32)
            dwx_ref[0] += lax.dot_general(xr_b, dpi_b, TN_DIMS, preferred_element_type=F32)
            dpad[pl.ds(r0 + HALO, Q), :] = dxr
            dcw, dcb = _conv_bwd_w(dxr, back)
            dcw_ref[...] += dcw
            dcb_ref[...] += dcb

        _chunks(first, unrolled=True)
        dlam_ref[...] = -dlam_ref[...] * _sigmoid_gate(-lam)

        def second(r0):
            dx_ref[pl.ds(r0, Q), :] = _conv_bwd_x(_ahead(dpad, r0), cw_ref[...]).astype(BF)

        _chunks(second)

    c0 = PXL // 128
    vec = _spec((1, 128), lambda c: (0, c))
    mat = _spec((1, 128, 128), lambda c: (c, 0, 0))
    col = _spec((T, 128), lambda c: (0, c))
    vshape = jax.ShapeDtypeStruct((1, LRU_W), F32)
    mshape = jax.ShapeDtypeStruct((8, 128, 128), F32)
    pad = pltpu.VMEM((T + 2 * HALO, 128), F32)
    seq = pltpu.VMEM((T, 128), F32)
    return pl.pallas_call(
        body, grid=(8,),
        in_specs=[col, col, col, _spec((T, 128), lambda c: (0, c0 + c)), _spec((4, 128), lambda c: (0, c)), vec, mat, vec, mat, vec, vec],
        out_specs=[col, _spec((4, 128), lambda c: (0, c)), vec, mat, vec, mat, vec, vec],
        out_shape=[jax.ShapeDtypeStruct((T, LRU_W), BF), jax.ShapeDtypeStruct((4, LRU_W), F32), vshape, mshape, vshape, mshape, vshape, vshape],
        scratch_shapes=[pad, pad, pad, seq, seq, seq, pltpu.VMEM((SEGS, 128), F32)],
        compiler_params=_params(), name="lru_bwd")(dh_out, a, hseq, proj, cw, cb, wa2, ba, wx2, bx, lam)


FF_TILE = 256
FF_TILE_ROWS = list(range(0, D_FF, FF_TILE))


def gate_up(h1, wn, w_gate, w_up):
    def body(h_ref, wn_ref, wg_hbm, wu_hbm, gt_ref, up_ref, act_ref, u_ref, wg_ref, wu_ref, wg_sems, wu_sems):
        gate_ready = _arriving(wg_hbm, wg_ref, wg_sems, FF_TILE_ROWS, FF_TILE)
        up_ready = _arriving(wu_hbm, wu_ref, wu_sems, FF_TILE_ROWS, FF_TILE)
        for r in (0, HALF):
            u_ref[r:r + HALF, :] = _rms(h_ref[r:r + HALF, :], wn_ref[...]).astype(BF)

        def tile(c0):
            cols = pl.ds(c0, FF_TILE)
            gate_ready(c0 // FF_TILE)
            up_ready(c0 // FF_TILE)
            gt = lax.dot_general(u_ref[...], wg_ref[cols, :], NT_DIMS, preferred_element_type=F32)
            up = lax.dot_general(u_ref[...], wu_ref[cols, :], NT_DIMS, preferred_element_type=F32)
            gt_ref[:, cols] = gt.astype(BF)
            up_ref[:, cols] = up.astype(BF)
            act_ref[:, cols] = (gt * _sigmoid(gt) * up).astype(BF)

        _col_tiles(D_FF, FF_TILE, tile)

    big = jax.ShapeDtypeStruct((T, D_FF), BF)
    return pl.pallas_call(
        body, grid=(T // RC,), in_specs=[_rows_spec(D), _vec(D), ANY_SPEC, ANY_SPEC],
        out_specs=[_rows_spec(D_FF), _rows_spec(D_FF), _rows_spec(D_FF), _rows_spec(D)],
        out_shape=[big, big, big, jax.ShapeDtypeStruct((T, D), BF)],
        scratch_shapes=[pltpu.VMEM((D_FF, D), BF)] * 2 + [pltpu.SemaphoreType.DMA((len(FF_TILE_ROWS),))] * 2,
        compiler_params=_params(), name="gate_up")(h1, wn, w_gate, w_up)


def down_loss(act, w_down, h1, target, wf):
    first = NPAD + N_META

    def body(a_ref, w_ref, r_ref, t_hbm, wf_ref, d_ref, db_ref, l_ref, dw_ref, h_scr, t_ref, t_sem):
        i = pl.program_id(0)
        _zero_at_first(l_ref, dw_ref)
        head = pltpu.make_async_copy(t_hbm.at[pl.ds(0, RC - first)], t_ref.at[pl.ds(first, RC - first)], t_sem)
        rest = pltpu.make_async_copy(t_hbm.at[pl.ds(pl.multiple_of(jnp.maximum(i * RC - first, 0), 32), RC)], t_ref, t_sem)

        @pl.when(i == 0)
        def _():
            t_ref[0:first, :] = jnp.zeros((first, D), F32)
            head.start()

        @pl.when(i > 0)
        def _():
            rest.start()

        def tile(c0):
            cols = pl.ds(c0, 512)
            h_scr[:, cols] = r_ref[:, cols] + jnp.dot(a_ref[...], w_ref[:, cols], preferred_element_type=F32)

        _col_tiles(D, 512, tile)

        @pl.when(i == 0)
        def _():
            head.wait()

        @pl.when(i > 0)
        def _():
            rest.wait()

        for r in (0, HALF):
            h = h_scr[r:r + HALF, :]
            live = _rows((HALF, D), i * RC + r) >= first
            err = jnp.where(live, _rms(h, wf_ref[...]) - t_ref[r:r + HALF, :], 0.0)
            l_ref[...] += 0.5 * jnp.sum(jnp.sum(err * err, axis=1, keepdims=True) * (1.0 / D), axis=0, keepdims=True)
            dh, dw = _rms_bwd(err * (1.0 / D), h, wf_ref[...])
            dw_ref[...] += jnp.sum(dw, axis=0, keepdims=True)
            d_ref[r:r + HALF, :] = dh
            db_ref[r:r + HALF, :] = dh.astype(BF)

    return pl.pallas_call(
        body, grid=(T // RC,),
        in_specs=[_rows_spec(D_FF), _whole((D_FF, D)), _rows_spec(D), pl.BlockSpec(memory_space=pl.ANY), _vec(D)],
        out_specs=[_rows_spec(D), _rows_spec(D), _spec((1, 128), lambda i: (0, 0)), _vec(D)],
        out_shape=[jax.ShapeDtypeStruct((T, D), F32), jax.ShapeDtypeStruct((T, D), BF), jax.ShapeDtypeStruct((1, 128), F32),
                   jax.ShapeDtypeStruct((1, D), F32)],
        scratch_shapes=[pltpu.VMEM((RC, D), F32), pltpu.VMEM((RC, D), F32), pltpu.SemaphoreType.DMA],
        compiler_params=_params(), name="down_loss")(act, w_down, h1, target, wf)


def swiglu_bwd(dh2_b, w_down, gt, up, act, u2):
    tn = 256

    def body(d_hbm, u_hbm, w_ref, gt_ref, up_ref, act_ref, dg_ref, du_ref, gd_ref, gg_ref, gu_ref, d_ref, u_ref, d_sems, u_sems):
        chunks = list(range(0, T, RC))
        d_ready = _arriving(d_hbm, d_ref, d_sems, chunks, RC)
        u_ready = _arriving(u_hbm, u_ref, u_sems, chunks, RC)

        def rows(r0):
            part = pl.ds(r0, RC)
            d_ready(r0 // RC)
            dact = lax.dot_general(d_ref[part, :], w_ref[...], NT_DIMS, preferred_element_type=F32)
            gt_ = gt_ref[part, :].astype(F32)
            up_ = up_ref[part, :].astype(F32)
            sg = _sigmoid(gt_)
            dg_ref[part, :] = (dact * up_ * (sg * (1.0 + gt_ * (1.0 - sg)))).astype(BF)
            du_ref[part, :] = (dact * (gt_ * sg)).astype(BF)

        _col_tiles(T, RC, rows)
        for k in range(len(chunks)):
            u_ready(k)
        gd_ref[...] = lax.dot_general(act_ref[...], d_ref[...], TN_DIMS, preferred_element_type=F32).astype(BF)
        gg_ref[...] = lax.dot_general(dg_ref[...], u_ref[...], TN_DIMS, preferred_element_type=F32).astype(BF)
        gu_ref[...] = lax.dot_general(du_ref[...], u_ref[...], TN_DIMS, preferred_element_type=F32).astype(BF)

    cols = _spec((T, tn), lambda j: (0, j))
    wrow = _spec((tn, D), lambda j: (j, 0))
    big = jax.ShapeDtypeStruct((T, D_FF), BF)
    grad = jax.ShapeDtypeStruct((D_FF, D), BF)
    return pl.pallas_call(
        body, grid=(D_FF // tn,), in_specs=[ANY_SPEC, ANY_SPEC, wrow, cols, cols, cols],
        out_specs=[cols, cols, wrow, wrow, wrow], out_shape=[big, big, grad, grad, grad],
        scratch_shapes=[pltpu.VMEM((T, D), BF)] * 2 + [pltpu.SemaphoreType.DMA((T // RC,))] * 2,
        compiler_params=_params(), name="swiglu_bwd")(dh2_b, u2, w_down, gt, up, act)


def gate_up_bwd(dgt, dup, w_gate, w_up, h1, wn, dh2, after):
    def body(dg_ref, du_ref, wg_hbm, wu_hbm, h_ref, wn_ref, r_ref, _after, d_ref, db_ref, dw_ref, du_scr, wg_ref, wu_ref, wg_sems, wu_sems):
        gate_ready = _arriving(wg_hbm, wg_ref, wg_sems, FF_TILE_ROWS, FF_TILE)
        up_ready = _arriving(wu_hbm, wu_ref, wu_sems, FF_TILE_ROWS, FF_TILE)
        _zero_at_first(dw_ref)

        du_scr[...] = jnp.zeros_like(du_scr)

        def tile(c0):
            k = pl.ds(c0, FF_TILE)
            gate_ready(c0 // FF_TILE)
            up_ready(c0 // FF_TILE)
            du_scr[...] += (jnp.dot(dg_ref[:, k], wg_ref[k, :], preferred_element_type=F32)
                            + jnp.dot(du_ref[:, k], wu_ref[k, :], preferred_element_type=F32))

        _col_tiles(D_FF, FF_TILE, tile)
        for r in (0, HALF):
            dh, dw = _rms_bwd(du_scr[r:r + HALF, :], h_ref[r:r + HALF, :], wn_ref[...])
            dw_ref[...] += jnp.sum(dw, axis=0, keepdims=True)
            dh = dh + r_ref[r:r + HALF, :]
            d_ref[r:r + HALF, :] = dh
            db_ref[r:r + HALF, :] = dh.astype(BF)

    return pl.pallas_call(
        body, grid=(T // RC,),
        in_specs=[_rows_spec(D_FF), _rows_spec(D_FF), ANY_SPEC, ANY_SPEC, _rows_spec(D), _vec(D), _rows_spec(D), ANY_SPEC],
        out_specs=[_rows_spec(D), _rows_spec(D), _vec(D)],
        out_shape=[jax.ShapeDtypeStruct((T, D), F32), jax.ShapeDtypeStruct((T, D), BF), jax.ShapeDtypeStruct((1, D), F32)],
        scratch_shapes=[pltpu.VMEM((RC, D), F32)] + [pltpu.VMEM((D_FF, D), BF)] * 2 + [pltpu.SemaphoreType.DMA((len(FF_TILE_ROWS),))] * 2,
        compiler_params=_params(), name="gate_up_bwd")(dgt, dup, w_gate, w_up, h1, wn, dh2, after)


def _adamw(w, g, m, v):
    m = ADAM_B1 * m + (1.0 - ADAM_B1) * g
    v = ADAM_B2 * v + (1.0 - ADAM_B2) * (g * g)
    m_hat = m / (1.0 - ADAM_B1 ** ADAM_STEP)
    v_hat = v / (1.0 - ADAM_B2 ** ADAM_STEP)
    delta = -ADAM_LR * (m_hat / (jnp.sqrt(v_hat) + ADAM_EPS) + ADAM_WD * w)
    return delta, m, v


def adamw_shards(name, recvs, ws, ms, vs):
    n = len(ws)

    def body(*refs):
        ins, outs = refs[:4 * n], refs[4 * n:]
        for k in range(n):
            p_ref, w_ref, m_ref, v_ref = ins[k], ins[n + k], ins[2 * n + k], ins[3 * n + k]
            g = p_ref[0].astype(F32)
            for s in range(1, 8):
                g = g + p_ref[s].astype(F32)
            outs[4 * k][...] = g
            outs[4 * k + 1][...], outs[4 * k + 2][...], outs[4 * k + 3][...] = _adamw(w_ref[...], g, m_ref[...], v_ref[...])

    tiles = [_spec((w.shape[0] // 2, w.shape[1]), lambda i: (i, 0)) for w in ws]
    recv_tiles = [_spec((8, w.shape[0] // 2, w.shape[1]), lambda i: (0, i, 0)) for w in ws]
    res = pl.pallas_call(
        body, grid=(2,), in_specs=recv_tiles + tiles * 3,
        out_specs=[t for t in tiles for _ in range(4)],
        out_shape=[jax.ShapeDtypeStruct(w.shape, F32) for w in ws for _ in range(4)],
        compiler_params=_params(), name=name)(*recvs, *ws, *ms, *vs)
    return [list(res[4 * k:4 * k + 4]) for k in range(n)]


def adamw_w_in(recv, w, m, v):
    rows = 34
    per_row = D // 128

    def body(p_ref, w_ref, m_ref, v_ref, g_ref, d_ref, mo_ref, vo_ref):
        def chunk(c, carry):
            lines = pl.ds(pl.multiple_of(c * per_row * rows, 16), per_row * rows)
            g = p_ref[0, lines, :].astype(F32)
            for s in range(1, 8):
                g = g + p_ref[s, lines, :].astype(F32)
            g = g.reshape(rows, per_row, 128)
            part = pl.ds(c * rows, rows)
            g_ref[part] = g
            d_ref[part], mo_ref[part], vo_ref[part] = _adamw(w_ref[part], g, m_ref[part], v_ref[part])
            return carry

        lax.fori_loop(0, w.shape[0] // rows, chunk, 0)

    shape = jax.ShapeDtypeStruct(w.shape, F32)
    return pl.pallas_call(body, out_shape=[shape] * 4, compiler_params=_params(0), name="adamw_w_in")(recv, w, m, v)


def sum_slabs(recv):
    def body(p_ref, o_ref):
        g = p_ref[0]
        for s in range(1, 8):
            g = g + p_ref[s]
        o_ref[...] = g

    return pl.pallas_call(body, out_shape=jax.ShapeDtypeStruct(recv.shape[1:], F32), compiler_params=_params(0), name="sum_slabs")(recv)


SIMPLE = [("norm1_w", 1024), ("ssd_conv_b", 1536), ("ssd_dt_bias", 16), ("ssd_a_log", 16), ("ssd_d", 16), ("ssd_norm_w", 1024),
          ("lru_conv_b", 1024), ("lru_ba", 1024), ("lru_bx", 1024), ("lru_lambda", 1024), ("lru_norm_w", 1024), ("norm2_w", 1024),
          ("final_norm_w", 1024)]
SPECIAL = ["lru_wa", "lru_wx", "meta_tokens", "ssd_conv_w", "lru_conv_w"]
SM_ROWS = 176
SM_WA, SM_WX, SM_META, SM_SCW, SM_LCW, SM_LOSS = 14, 78, 142, 158, 166, 170


def _simple_rows():
    rows, r = {}, 0
    for name, n in SIMPLE:
        rows[name] = r
        r += -(-n // 1024)
    return rows


def adamw_small(sm, special_g, ws, ms, vs):
    rows = _simple_rows()
    ns, nx = len(SIMPLE), len(SPECIAL)

    def body(*refs):
        sm_ref = refs[0]
        gx = refs[1:1 + nx]
        wr = refs[1 + nx:1 + nx + ns + nx]
        mr = refs[1 + nx + ns + nx:1 + nx + 2 * (ns + nx)]
        vr = refs[1 + nx + 2 * (ns + nx):1 + nx + 3 * (ns + nx)]
        outs = refs[1 + nx + 3 * (ns + nx):]
        o = 0
        for k, (name, n) in enumerate(SIMPLE):
            r0 = rows[name]
            for c0 in range(0, n, 1024):
                wd = min(1024, n - c0)
                g = sm_ref[r0 + c0 // 1024:r0 + c0 // 1024 + 1, 0:wd]
                sl = (slice(None), slice(c0, c0 + wd))
                d, m2, v2 = _adamw(wr[k][sl], g, mr[k][sl], vr[k][sl])
                outs[o][sl] = g
                outs[o + 1][sl] = d
                outs[o + 2][sl] = m2
                outs[o + 3][sl] = v2
            o += 4
        for k in range(nx):
            d, m2, v2 = _adamw(wr[ns + k][...], gx[k][...], mr[ns + k][...], vr[ns + k][...])
            outs[o][...] = d
            outs[o + 1][...] = m2
            outs[o + 2][...] = v2
            o += 3

    out_shape = []
    for k in range(ns):
        out_shape += [jax.ShapeDtypeStruct(ws[k].shape, F32)] * 4
    for k in range(nx):
        out_shape += [jax.ShapeDtypeStruct(ws[ns + k].shape, F32)] * 3
    return pl.pallas_call(body, out_shape=out_shape, compiler_params=_params(0), name="adamw_small")(sm, *special_g, *ws, *ms, *vs)


def _place():
    return lax.axis_index("x"), lax.axis_index("y"), lax.axis_index("c")


def _index(px, py, pc):
    return 4 * px + 2 * py + pc


def all_gather(name, shards):
    n = len(shards)
    hbm = pl.BlockSpec(memory_space=pl.ANY)

    def body(*refs):
        ins, outs = refs[:n], refs[n:2 * n]
        send_sems, recv_sems, local_sems = refs[2 * n:]
        x, y, c = _place()
        me, sibling = (x, y, c), (x, y, 1 - c)
        chips = [(1 - x, y), (x, 1 - y), (1 - x, 1 - y)]

        def copy(i, k, block, to, src=None):
            dst = outs[i].at[_index(*block)]
            return pltpu.make_async_remote_copy(src_ref=dst if src is None else src, dst_ref=dst, send_sem=send_sems.at[7 * i + k],
                                                recv_sem=recv_sems.at[7 * i + k], device_id=to, device_id_type=MESH)

        mine = [pltpu.make_async_copy(ins[i], outs[i].at[_index(*me)], local_sems.at[i]) for i in range(n)]
        for cp in mine:
            cp.start()
        first = []
        for i in range(n):
            first += [copy(i, 1 + j, me, (*chip, c), src=ins[i]) for j, chip in enumerate(chips)]
            first.append(copy(i, 0, me, sibling, src=ins[i]))
        for cp in first:
            cp.start()
        passed = []
        for i in range(n):
            for j, chip in enumerate(chips):
                copy(i, 1 + j, (*chip, c), me).wait_recv()
                cp = copy(i, 4 + j, (*chip, c), sibling)
                cp.start()
                passed.append(cp)
        for i in range(n):
            copy(i, 0, sibling, me).wait_recv()
            for j, chip in enumerate(chips):
                copy(i, 4 + j, (*chip, 1 - c), me).wait_recv()
        for cp in first + passed:
            cp.wait_send()
        for cp in mine:
            cp.wait()

    return pl.pallas_call(
        body, in_specs=[hbm] * n, out_specs=[hbm] * n,
        out_shape=[jax.ShapeDtypeStruct((8,) + s.shape, s.dtype) for s in shards],
        scratch_shapes=[pltpu.SemaphoreType.DMA((7 * n,)), pltpu.SemaphoreType.DMA((7 * n,)), pltpu.SemaphoreType.DMA((n,))],
        name=name)(*shards)


HBM_SPEC = pl.BlockSpec(memory_space=pltpu.HBM)
SEM_SPEC = pl.BlockSpec(memory_space=pltpu.SEMAPHORE)
EFFECT = pltpu.SideEffectType.DATAFLOW_SIDE_EFFECTING


def _peers(x, y, c):
    return [((1 - x) if k & 4 else x, (1 - y) if k & 2 else y, (1 - c) if k & 1 else c) for k in range(1, 8)]


def _pieces(rows):
    for n in (4, 2):
        if rows % (16 * n) == 0:
            return [(r * (rows // n), rows // n) for r in range(n)]
    return [(0, rows)]


def _peer_copies(src, land, send_sems, recv_sems, k, peer, mine):
    block = src.at[_index(*peer)]
    return [pltpu.make_async_remote_copy(src_ref=block.at[pl.ds(r0, nr)], dst_ref=land.at[mine, pl.ds(r0, nr)], send_sem=send_sems.at[k],
                                         recv_sem=recv_sems.at[k], device_id=peer, device_id_type=MESH)
            for r0, nr in _pieces(block.shape[0])]


OWN = 7


def _own_copy(src, land, send_sems, mine):
    return pltpu.make_async_copy(src.at[mine], land.at[mine], send_sems.at[OWN])


def copies_start(name, srcs):
    n = len(srcs)

    def body(*refs):
        ins, lands = refs[:n], refs[n:2 * n]
        sends, recvs = refs[2 * n:3 * n], refs[3 * n:4 * n]
        token = refs[-1]
        x, y, c = _place()
        mine = _index(x, y, c)
        for i in range(n):
            per_peer = [_peer_copies(ins[i], lands[i], sends[i], recvs[i], k, peer, mine) for k, peer in enumerate(_peers(x, y, c))]
            for piece in zip(*per_peer):
                for cp in piece:
                    cp.start()
            _own_copy(ins[i], lands[i], sends[i], mine).start()
        token[...] = jnp.zeros_like(token)

    res = pl.pallas_call(
        body, name=name,
        out_shape=([pltpu.SemaphoreType.DMA((8,))] * n + [pltpu.SemaphoreType.DMA((7,))] * n + [pltpu.HBM(s.shape, s.dtype) for s in srcs] * 2
                   + [jax.ShapeDtypeStruct((8, 128), F32)]),
        in_specs=[HBM_SPEC] * (2 * n),
        out_specs=[SEM_SPEC] * (2 * n) + [HBM_SPEC] * (2 * n) + [pl.BlockSpec(memory_space=pltpu.VMEM)],
        input_output_aliases={i: 2 * n + i for i in range(2 * n)},
        compiler_params=pltpu.CompilerParams(has_side_effects=EFFECT),
    )(*[pltpu.with_memory_space_constraint(s, pltpu.HBM) for s in srcs],
      *[pltpu.with_memory_space_constraint(lax.empty(s.shape, s.dtype), pltpu.HBM) for s in srcs])
    return [(res[i], res[n + i], res[2 * n + i], res[3 * n + i]) for i in range(n)], res[-1]


def copies_wait(name, started, after):
    n = len(started)

    def body(*refs):
        ins, lands = refs[:n], refs[n:2 * n]
        sends, recvs = refs[2 * n:3 * n], refs[3 * n:4 * n]
        x, y, c = _place()
        mine = _index(x, y, c)
        for i in range(n):
            for k, peer in enumerate(_peers(x, y, c)):
                arrival = pltpu.make_async_remote_copy(src_ref=ins[i].at[mine], dst_ref=lands[i].at[_index(*peer)],
                                                       send_sem=sends[i].at[k], recv_sem=recvs[i].at[k], device_id=peer, device_id_type=MESH)
                arrival.wait_send()
                arrival.wait_recv()
            _own_copy(ins[i], lands[i], sends[i], mine).wait()

    srcs = [s[2] for s in started]
    lands = [s[3] for s in started]
    res = pl.pallas_call(
        body, name=name,
        out_shape=[pltpu.HBM(s.shape, s.dtype) for s in srcs] + [pltpu.HBM(z.shape, z.dtype) for z in lands],
        in_specs=[HBM_SPEC] * (2 * n) + [SEM_SPEC] * (2 * n) + [ANY_SPEC] * len(after),
        out_specs=[HBM_SPEC] * (2 * n),
        input_output_aliases={i: i for i in range(2 * n)},
        compiler_params=pltpu.CompilerParams(has_side_effects=EFFECT),
    )(*srcs, *lands, *[s[0] for s in started], *[s[1] for s in started], *after)
    return list(res[n:])


def _hop(src, land, send_sems, recv_sems, k, block, to):
    dst = land.at[_index(*block)]
    return pltpu.make_async_remote_copy(src_ref=dst if src is None else src, dst_ref=dst, send_sem=send_sems.at[k], recv_sem=recv_sems.at[k],
                                        device_id=to, device_id_type=MESH)


def _own_block(src, land, send_sems, mine):
    return pltpu.make_async_copy(src, land.at[mine], send_sems.at[4])


def _other_chips(x, y):
    return [(1 - x, y), (x, 1 - y), (1 - x, 1 - y)]


def gather_start(name, shards, through):
    n, m = len(shards), len(through)

    def body(*refs):
        ins, lands = refs[:n], refs[n:2 * n]
        sends, recvs = refs[2 * n + m:3 * n + m], refs[3 * n + m:4 * n + m]
        x, y, c = _place()
        for i in range(n):
            for j, chip in enumerate(_other_chips(x, y)):
                _hop(ins[i], lands[i], sends[i], recvs[i], 1 + j, (x, y, c), (*chip, c)).start()
            _hop(ins[i], lands[i], sends[i], recvs[i], 0, (x, y, c), (x, y, 1 - c)).start()
            _own_block(ins[i], lands[i], sends[i], _index(x, y, c)).start()

    own, passing = pltpu.SemaphoreType.DMA((5,)), pltpu.SemaphoreType.DMA((3,))
    zones = [jax.ShapeDtypeStruct((8,) + s.shape, s.dtype) for s in shards]
    res = pl.pallas_call(
        body, name=name,
        out_shape=([own] * (2 * n) + [passing] * (2 * n) + [pltpu.HBM(s.shape, s.dtype) for s in shards]
                   + [pltpu.HBM(z.shape, z.dtype) for z in zones] + [pltpu.HBM(t.shape, t.dtype) for t in through]),
        in_specs=[HBM_SPEC] * (2 * n + m),
        out_specs=[SEM_SPEC] * (4 * n) + [HBM_SPEC] * (2 * n + m),
        input_output_aliases={i: 4 * n + i for i in range(2 * n + m)},
        compiler_params=pltpu.CompilerParams(has_side_effects=EFFECT),
    )(*[pltpu.with_memory_space_constraint(s, pltpu.HBM) for s in shards],
      *[pltpu.with_memory_space_constraint(lax.empty(z.shape, z.dtype), pltpu.HBM) for z in zones],
      *[pltpu.with_memory_space_constraint(t, pltpu.HBM) for t in through])
    return [[res[4 * n + i], res[5 * n + i], res[i], res[n + i], res[2 * n + i], res[3 * n + i]] for i in range(n)], list(res[6 * n:])


def gather_stage(name, pass_on, finish, after):
    arrays = pass_on + finish
    n = len(arrays)

    def body(*refs):
        ins, lands = refs[:n], refs[n:2 * n]
        sems = [refs[(2 + q) * n:(3 + q) * n] for q in range(4)]
        x, y, c = _place()
        me, sibling = (x, y, c), (x, y, 1 - c)
        for i in range(len(pass_on)):
            send, recv, send_on, recv_on = (q[i] for q in sems)
            for j, chip in enumerate(_other_chips(x, y)):
                _hop(None, lands[i], send, recv, 1 + j, (*chip, c), me).wait_recv()
                _hop(None, lands[i], send_on, recv_on, j, (*chip, c), sibling).start()
        for i in range(len(pass_on), n):
            send, recv, send_on, recv_on = (q[i] for q in sems)
            _hop(ins[i], lands[i], send, recv, 0, sibling, me).wait_recv()
            for j, chip in enumerate(_other_chips(x, y)):
                _hop(None, lands[i], send_on, recv_on, j, (*chip, 1 - c), me).wait_recv()
            _hop(ins[i], lands[i], send, recv, 0, me, sibling).wait_send()
            _own_block(ins[i], lands[i], send, _index(*me)).wait()
            for j, chip in enumerate(_other_chips(x, y)):
                _hop(ins[i], lands[i], send, recv, 1 + j, me, (*chip, c)).wait_send()
                _hop(None, lands[i], send_on, recv_on, j, (*chip, c), sibling).wait_send()
        refs[-1][...] = jnp.zeros_like(refs[-1])

    res = pl.pallas_call(
        body, name=name,
        out_shape=([pltpu.HBM(a[0].shape, a[0].dtype) for a in arrays] + [pltpu.HBM(a[1].shape, a[1].dtype) for a in arrays]
                   + [jax.ShapeDtypeStruct((8, 128), F32)]),
        in_specs=[HBM_SPEC] * (2 * n) + [SEM_SPEC] * (4 * n) + [ANY_SPEC],
        out_specs=[HBM_SPEC] * (2 * n) + [pl.BlockSpec(memory_space=pltpu.VMEM)],
        input_output_aliases={i: i for i in range(2 * n)},
        compiler_params=pltpu.CompilerParams(has_side_effects=EFFECT),
    )(*[a[0] for a in arrays], *[a[1] for a in arrays], *[a[2 + q] for q in range(4) for a in arrays], after)
    for i, a in enumerate(arrays):
        a[0], a[1] = res[i], res[n + i]
    return [a[1] for a in finish], res[-1]


WEIGHTS = ["meta_tokens", "norm1_w", "w_in", "ssd_conv_w", "ssd_conv_b", "ssd_dt_bias", "ssd_a_log", "ssd_d", "ssd_norm_w", "lru_conv_w",
           "lru_conv_b", "lru_wa", "lru_ba", "lru_wx", "lru_bx", "lru_lambda", "lru_norm_w", "w_out", "norm2_w", "w_gate", "w_up", "w_down",
           "final_norm_w"]
BIG = ["w_in", "w_out", "w_gate", "w_up", "w_down"]
COLUMN_SHARDED = ["w_in", "w_gate", "w_up"]


def _pair_blocks(w):
    w = w.reshape(8, 2, 64, 64)
    z = jnp.zeros((8, 64, 64), w.dtype)
    return jnp.concatenate([jnp.concatenate([w[:, 0], z], axis=2), jnp.concatenate([z, w[:, 1]], axis=2)], axis=1)


def _unpair_blocks(w2):
    return jnp.stack([w2[:, :64, :64], w2[:, 64:, 64:]], axis=1).reshape(16, 64, 64)


def _per_group(v):
    return jnp.pad(v.reshape(2, 1, 8), ((0, 0), (0, 0), (0, 120)))


def _pad_cols(v, n):
    return jnp.pad(v, ((0, 0), (0, n - v.shape[1])))


def local_step(x, target, meta, ssd_cw, lru_cw, w_in_shards, fetch, send, p):
    bias2, alog2, d2 = _per_group(p["ssd_dt_bias"]), _per_group(p["ssd_a_log"]), _per_group(p["ssd_d"])
    wa2 = _pair_blocks(p["lru_wa"]).astype(BF)
    wx2 = _pair_blocks(p["lru_wx"]).astype(BF)
    lru = (lru_cw, p["lru_conv_b"], wa2, p["lru_ba"], wx2, p["lru_bx"], p["lru_lambda"])

    proj, dt_raw, u1, h0, w_in, w_dt = in_proj(x, meta, p["norm1_w"], w_in_shards)
    yn_ssd, y_pre, h_prev = ssd_fwd(proj, dt_raw, ssd_cw, p["ssd_conv_b"], bias2, alog2, d2, p["ssd_norm_w"])
    _, moved = fetch([], yn_ssd)
    hseq, a = lru_fwd(proj, *lru, moved)
    (w_out,), _ = fetch(["w_out"], hseq)
    h1, cat = out_proj(yn_ssd, proj, hseq, p["lru_norm_w"], w_out, h0)
    (w_gate, w_up), _ = fetch(["w_gate", "w_up"], h1)
    gt, up, act, u2 = gate_up(h1, p["norm2_w"], w_gate, w_up)
    (w_down,), _ = fetch(["w_down"], act)
    dh2, dh2_b, loss, d_fnw = down_loss(act, w_down, h1, target, p["final_norm_w"])

    dgt, dup, g_down, g_gate, g_up = swiglu_bwd(dh2_b, w_down, gt, up, act, u2)
    sent = send({"w_down": g_down, "w_gate": g_gate, "w_up": g_up})
    dh1, dh1_b, d_n2 = gate_up_bwd(dgt, dup, w_gate, w_up, h1, p["norm2_w"], dh2, sent)
    sent = send({"w_out": weight_grad("dw_out", cat, dh1_b)})
    dyn, dh_out, dg_b, d_lnw = out_proj_bwd(dh1_b, w_out, proj, hseq, p["lru_norm_w"], sent)

    dxl_b, d_lcw, d_lcb, dwa2, d_ba, dwx2, d_bx, d_lam = lru_bwd(dh_out, a, hseq, proj, *lru)
    dz_b, dxbc_b, ddt_b, dpar, d_snw, d_scw, d_scb = ssd_bwd(dyn, proj, dt_raw, ssd_cw, p["ssd_conv_b"], y_pre, h_prev, bias2, alog2, d2,
                                                             p["ssd_norm_w"], sent)
    sent = send({"w_in": in_weight_grad([dz_b, dxbc_b, dg_b, dxl_b], [0, SSD_W, 2576, 2576 + LRU_W], ddt_b, u1)})
    grad_x, d_meta, d_n1 = in_proj_bwd(dz_b, dg_b, dxl_b, dxbc_b, ddt_b, w_in, w_dt, h0, p["norm1_w"], dh1, sent)
    small = {"norm1_w": d_n1, "ssd_conv_b": d_scb, "ssd_dt_bias": dpar[:, 0, :8].reshape(1, 16), "ssd_a_log": dpar[:, 1, :8].reshape(1, 16),
             "ssd_d": dpar[:, 2, :8].reshape(1, 16), "ssd_norm_w": d_snw, "lru_conv_b": d_lcb, "lru_ba": d_ba, "lru_bx": d_bx,
             "lru_lambda": d_lam, "lru_norm_w": d_lnw, "norm2_w": d_n2, "final_norm_w": d_fnw,
             "lru_wa": _unpair_blocks(dwa2), "lru_wx": _unpair_blocks(dwx2), "meta_tokens": d_meta,
             "ssd_conv_w": d_scw, "lru_conv_w": d_lcw}
    return loss, grad_x, small


def _pack_small(small, loss):
    rows = [_pad_cols(small[name], -(-n // 1024) * 1024).reshape(-1, 1024) for name, n in SIMPLE]
    rows += [small["lru_wa"].reshape(64, 1024), small["lru_wx"].reshape(64, 1024), small["meta_tokens"],
             _pad_cols(small["ssd_conv_w"], 2048).reshape(8, 1024), small["lru_conv_w"], _pad_cols(loss[:, 0:1], 1024)]
    sm = jnp.concatenate(rows, axis=0)
    return jnp.pad(sm, ((0, SM_ROWS - sm.shape[0]), (0, 0)))


def _slabs(g):
    return g.reshape(8, g.shape[0] // 8, g.shape[1])


def _unslab(g):
    return g.reshape(8 * g.shape[1], g.shape[2])


def kernel(x, meta_tokens, norm1_w, w_in, ssd_conv_w, ssd_conv_b, ssd_dt_bias, ssd_a_log, ssd_d, ssd_norm_w, lru_conv_w, lru_conv_b, lru_wa, lru_ba, lru_wx, lru_bx, lru_lambda, lru_norm_w, w_out, norm2_w, w_gate, w_up, w_down, final_norm_w, loss_target, m_meta_tokens, m_norm1_w, m_w_in, m_ssd_conv_w, m_ssd_conv_b, m_ssd_dt_bias, m_ssd_a_log, m_ssd_d, m_ssd_norm_w, m_lru_conv_w, m_lru_conv_b, m_lru_wa, m_lru_ba, m_lru_wx, m_lru_bx, m_lru_lambda, m_lru_norm_w, m_w_out, m_norm2_w, m_w_gate, m_w_up, m_w_down, m_final_norm_w, v_meta_tokens, v_norm1_w, v_w_in, v_ssd_conv_w, v_ssd_conv_b, v_ssd_dt_bias, v_ssd_a_log, v_ssd_d, v_ssd_norm_w, v_lru_conv_w, v_lru_conv_b, v_lru_wa, v_lru_ba, v_lru_wx, v_lru_bx, v_lru_lambda, v_lru_norm_w, v_w_out, v_norm2_w, v_w_gate, v_w_up, v_w_down, v_final_norm_w):
    w = dict(meta_tokens=meta_tokens, norm1_w=norm1_w, w_in=w_in[0], ssd_conv_w=ssd_conv_w[0], ssd_conv_b=ssd_conv_b, ssd_dt_bias=ssd_dt_bias,
             ssd_a_log=ssd_a_log, ssd_d=ssd_d, ssd_norm_w=ssd_norm_w, lru_conv_w=lru_conv_w[0], lru_conv_b=lru_conv_b, lru_wa=lru_wa[0],
             lru_ba=lru_ba, lru_wx=lru_wx[0], lru_bx=lru_bx, lru_lambda=lru_lambda, lru_norm_w=lru_norm_w, w_out=w_out[0], norm2_w=norm2_w,
             w_gate=w_gate[0], w_up=w_up[0], w_down=w_down[0], final_norm_w=final_norm_w.reshape(1, D))
    m = dict(meta_tokens=m_meta_tokens, norm1_w=m_norm1_w, w_in=m_w_in[0], ssd_conv_w=m_ssd_conv_w[0], ssd_conv_b=m_ssd_conv_b,
             ssd_dt_bias=m_ssd_dt_bias, ssd_a_log=m_ssd_a_log, ssd_d=m_ssd_d, ssd_norm_w=m_ssd_norm_w, lru_conv_w=m_lru_conv_w[0],
             lru_conv_b=m_lru_conv_b, lru_wa=m_lru_wa[0], lru_ba=m_lru_ba, lru_wx=m_lru_wx[0], lru_bx=m_lru_bx, lru_lambda=m_lru_lambda,
             lru_norm_w=m_lru_norm_w, w_out=m_w_out[0], norm2_w=m_norm2_w, w_gate=m_w_gate[0], w_up=m_w_up[0], w_down=m_w_down[0],
             final_norm_w=m_final_norm_w.reshape(1, D))
    v = dict(meta_tokens=v_meta_tokens, norm1_w=v_norm1_w, w_in=v_w_in[0], ssd_conv_w=v_ssd_conv_w[0], ssd_conv_b=v_ssd_conv_b,
             ssd_dt_bias=v_ssd_dt_bias, ssd_a_log=v_ssd_a_log, ssd_d=v_ssd_d, ssd_norm_w=v_ssd_norm_w, lru_conv_w=v_lru_conv_w[0],
             lru_conv_b=v_lru_conv_b, lru_wa=v_lru_wa[0], lru_ba=v_lru_ba, lru_wx=v_lru_wx[0], lru_bx=v_lru_bx, lru_lambda=v_lru_lambda,
             lru_norm_w=v_lru_norm_w, w_out=v_w_out[0], norm2_w=v_norm2_w, w_gate=v_w_gate[0], w_up=v_w_up[0], w_down=v_w_down[0],
             final_norm_w=v_final_norm_w.reshape(1, D))
    shapes = dict(meta_tokens=meta_tokens.shape, norm1_w=norm1_w.shape, w_in=w_in.shape, ssd_conv_w=ssd_conv_w.shape,
                  ssd_conv_b=ssd_conv_b.shape, ssd_dt_bias=ssd_dt_bias.shape, ssd_a_log=ssd_a_log.shape, ssd_d=ssd_d.shape,
                  ssd_norm_w=ssd_norm_w.shape, lru_conv_w=lru_conv_w.shape, lru_conv_b=lru_conv_b.shape, lru_wa=lru_wa.shape,
                  lru_ba=lru_ba.shape, lru_wx=lru_wx.shape, lru_bx=lru_bx.shape, lru_lambda=lru_lambda.shape, lru_norm_w=lru_norm_w.shape,
                  w_out=w_out.shape, norm2_w=norm2_w.shape, w_gate=w_gate.shape, w_up=w_up.shape, w_down=w_down.shape,
                  final_norm_w=final_norm_w.shape)
    me = _index(*_place())
    for n in COLUMN_SHARDED:
        w[n], m[n], v[n] = w[n].T, m[n].T, v[n].T

    small_shard = jnp.concatenate([w["meta_tokens"], _pad_cols(w["ssd_conv_w"], 256).reshape(8, 128), w["lru_conv_w"],
                                   jnp.zeros((4, 128), F32)], axis=0)
    g_in, gs = all_gather("gather_w_in", [w["w_in"].astype(BF), small_shard])
    later = ["w_out", "w_gate", "w_up", "w_down"]
    started, (g_in, gs) = gather_start("gather_rest_start", [w[n].astype(BF) for n in later], [g_in, gs])
    started = dict(zip(later, started))
    meta_full = gs[:, 0:16].transpose(1, 0, 2).reshape(N_META, D)
    ssd_cw = gs[:, 16:24].reshape(8, 4, 256)[:, :, :192].transpose(1, 0, 2).reshape(4, XBC)
    lru_cw = gs[:, 24:28].transpose(1, 0, 2).reshape(4, LRU_W)

    def fetch(names, after):
        pass_on = {"w_out": ["w_down"], "w_gate": [], "w_down": []}[names[0]] if names else ["w_out", "w_gate", "w_up"]
        got, zero = gather_stage("gather_" + (names[0] + "_wait" if names else "pass_on"), [started[n] for n in pass_on],
                                 [started[n] for n in names], after)
        return [_unslab(g) for g in got], zero

    in_flight = {}

    def send(grads):
        names = list(grads)
        st, zero = copies_start("grads_" + names[0] + "_start", [grads[n] if n == "small" else _slabs(grads[n]) for n in names])
        in_flight.update(zip(names, st))
        return zero

    loss, grad_x, small = local_step(x[0], loss_target[0], meta_full, ssd_cw, lru_cw, g_in, fetch, send, w)
    send({"small": _pack_small(small, loss).reshape(8, SM_ROWS // 8, 1024)})

    out = {}
    early = ["w_down", "w_gate", "w_up", "w_out"]
    recv = dict(zip(early, copies_wait("grads_early_wait", [in_flight[n] for n in early], [in_flight["small"][2]])))
    for pair in (early[:2], early[2:]):
        done = adamw_shards("adamw_" + pair[0], [recv[n] for n in pair], [w[n] for n in pair], [m[n] for n in pair], [v[n] for n in pair])
        out.update(zip(pair, done))
    recv_in, recv_small = copies_wait("grads_late_wait", [in_flight["w_in"], in_flight["small"]], [out[n][0] for n in early])
    def lines(a):
        return jnp.transpose(a.reshape(D // 128, 128, IN_COLS // 8), (2, 0, 1))

    out["w_in"] = [jnp.transpose(o, (1, 2, 0)).reshape(D, IN_COLS // 8) for o in adamw_w_in(recv_in, lines(w_in), lines(m_w_in), lines(v_w_in))]
    for n in ("w_gate", "w_up"):
        out[n] = [o.T for o in out[n]]
    sm = all_gather("gather_small_grads", [sum_slabs(recv_small)])[0].reshape(SM_ROWS, 1024)
    special_g =[sm[SM_WA:SM_WA + 64].reshape(16, 64, 64), sm[SM_WX:SM_WX + 64].reshape(16, 64, 64),
                 lax.dynamic_slice(sm[SM_META:SM_META + 16], (0, 128 * me), (16, 128)),
                 lax.dynamic_slice(sm[SM_SCW:SM_SCW + 8].reshape(4, 2048), (0, 192 * me), (4, 192)),
                 lax.dynamic_slice(sm[SM_LCW:SM_LCW + 4], (0, 128 * me), (4, 128))]
    names = [n for n, _ in SIMPLE] + SPECIAL
    res = adamw_small(sm, special_g, [w[n] for n in names], [m[n] for n in names], [v[n] for n in names])
    for k, (n, _) in enumerate(SIMPLE):
        out[n] = res[4 * k:4 * k + 4]
    for k, n in enumerate(SPECIAL):
        o = 4 * len(SIMPLE) + 3 * k
        out[n] = [special_g[k]] + list(res[o:o + 3])
    loss_total = sm[SM_LOSS, 0]
    flat = [loss_total, grad_x[None]]
    for k in range(4):
        flat += [out[n][k].reshape(shapes[n]) for n in WEIGHTS]
    return tuple(flat)
```

```python
import math

import jax
import jax.numpy as jnp
from jax import lax
from jax.experimental import pallas as pl
from jax.experimental.pallas import tpu as pltpu

F32 = jnp.float32
BF = jnp.bfloat16

D = 1024
SEQ = 2048
N_META = 16
Q = 128
NPAD = 112
T = NPAD + N_META + SEQ
NCH = T // Q
RC = 544
D_FF = 2816
SSD_W = 1024
LRU_W = 1024
XBC = 1536
IN_COLS = 4624
PZ, PG, PXL, PXBC = 0, 1024, 2048, 3072
NP_IN = 4608
EPS = 1e-6
LRU_C = 8.0
VMEM_LIMIT = 56 * 1024 * 1024

ADAM_LR, ADAM_B1, ADAM_B2, ADAM_EPS, ADAM_WD, ADAM_STEP = 0.001, 0.9, 0.999, 1e-08, 0.01, 10

NT_DIMS = (((1,), (1,)), ((), ()))
TN_DIMS = (((0,), (0,)), ((), ()))
MESH = pl.DeviceIdType.MESH


def _params(n_grid=1, limit=VMEM_LIMIT):
    return pltpu.CompilerParams(dimension_semantics=("arbitrary",) * n_grid, vmem_limit_bytes=limit)


def _spec(shape, imap, single=False):
    if single:
        return pl.BlockSpec(shape, imap, pipeline_mode=pl.Buffered(1))
    return pl.BlockSpec(shape, imap)


def _sigmoid(x):
    return 0.5 * jnp.tanh(0.5 * x) + 0.5


def _sigmoid_gate(x):
    return 1.0 / (1.0 + jnp.exp(-x))


def _softplus(x):
    return jnp.maximum(x, 0.0) + jnp.log(1.0 + jnp.exp(-jnp.abs(x)))


def _rms_stats(h):
    return lax.rsqrt(jnp.mean(h * h, axis=-1, keepdims=True) + EPS)


def _rms(h, w):
    return (h * _rms_stats(h)) * w


def _rms_bwd(du, h, w):
    r = _rms_stats(h)
    n = h * r
    dn = du * w
    dh = r * (dn - n * jnp.mean(dn * n, axis=-1, keepdims=True))
    return dh, du * n


_G0 = math.sqrt(2.0 / math.pi)


def _gelu(x):
    return 0.5 * x * (1.0 + jnp.tanh(_G0 * (x + 0.044715 * (x * x * x))))


def _gelu_grad(x):
    t = jnp.tanh(_G0 * (x + 0.044715 * (x * x * x)))
    return 0.5 * (1.0 + t) + 0.5 * x * (1.0 - t * t) * (_G0 * (1.0 + 3.0 * 0.044715 * (x * x)))


def _rows(shape, r0=0):
    return lax.broadcasted_iota(jnp.int32, shape, 0) + r0


def _lanes(shape):
    return lax.broadcasted_iota(jnp.int32, shape, 1)


HALO = 8


def _fill_padded(pad_ref, x_ref):
    pad_ref[0:HALO, :] = jnp.zeros((HALO, pad_ref.shape[1]), F32)
    pad_ref[T + HALO:T + 2 * HALO, :] = jnp.zeros((HALO, pad_ref.shape[1]), F32)

    def step(c, carry):
        r0 = pl.multiple_of(c * Q, Q)
        pad_ref[pl.ds(r0 + HALO, Q), :] = x_ref[pl.ds(r0, Q), :].astype(F32)
        return carry

    lax.fori_loop(0, NCH, step, 0)


def _back(pad_ref, r0):
    win = pad_ref[pl.ds(r0, Q + HALO), :]
    return lambda s: win[HALO:, :] if s == 0 else pltpu.roll(win, s, axis=0)[HALO:, :]


def _ahead(pad_ref, r0):
    win = pad_ref[pl.ds(r0 + HALO, Q + HALO), :]
    return lambda s: win[:Q, :] if s == 0 else pltpu.roll(win, Q + HALO - s, axis=0)[:Q, :]


def _conv(back, w, b):
    y = b + w[3:4, :] * back(0)
    for k in range(3):
        y = y + w[k:k + 1, :] * back(3 - k)
    return y


def _conv_bwd_x(ahead, w):
    dx = w[3:4, :] * ahead(0)
    for k in range(3):
        dx = dx + w[k:k + 1, :] * ahead(3 - k)
    return dx


def _conv_bwd_w(dy, back):
    dws = [jnp.sum(dy * back(3 - k), axis=0, keepdims=True) for k in range(4)]
    return jnp.concatenate(dws, axis=0), jnp.sum(dy, axis=0, keepdims=True)


def _chunks(fn, unrolled=False):
    if unrolled:
        for c in range(NCH):
            fn(c * Q)
        return

    def step(c, carry):
        fn(pl.multiple_of(c * Q, Q))
        return carry

    lax.fori_loop(0, NCH, step, 0)


HALF = RC // 2


def _col_tiles(n, tn, fn):
    def step(j, carry):
        fn(pl.multiple_of(j * tn, tn))
        return carry

    lax.fori_loop(0, n // tn, step, 0)


def _rows_spec(cols, block_col=0):
    return _spec((RC, cols), lambda i: (i, block_col))


def _whole(shape):
    return _spec(shape, lambda i: tuple(0 for _ in shape), single=True)


def _vec(cols):
    return _spec((1, cols), lambda i: (0, 0))


def _zero_at_first(*refs):
    @pl.when(pl.program_id(0) == 0)
    def _():
        for r in refs:
            r[...] = jnp.zeros_like(r)


ANY_SPEC = pl.BlockSpec(memory_space=pl.ANY)


def _arriving(src, dst, sems, starts, rows):
    n, ahead = len(starts), 2
    first = pl.program_id(0) == 0

    def piece(k):
        r0 = starts[0]
        for j in range(1, n):
            r0 = jnp.where(k == j, starts[j], r0)
        at = pl.ds(pl.multiple_of(r0, 16), rows)
        return pltpu.make_async_copy(src.at[at], dst.at[at], sems.at[k])

    @pl.when(first)
    def _():
        for k in range(min(ahead, n)):
            piece(k).start()

    def ready(k):
        k = jnp.asarray(k, jnp.int32)

        @pl.when(first)
        def _():
            piece(k).wait()

            @pl.when(k + ahead < n)
            def _():
                piece(k + ahead).start()

    return ready


IN_RUNS = ((PZ, 0, 1024), (PXBC, 1024, XBC), (PG, 2576, 2048))
IN_TILE = 512
IN_TILE_ROWS = [wrow + IN_TILE * j for _, wrow, width in IN_RUNS for j in range(width // IN_TILE)]


def _in_tiles(fn):
    done = 0
    for pcol, wrow, width in IN_RUNS:
        def step(j, carry, pcol=pcol, wrow=wrow, done=done):
            fn(pl.multiple_of(pcol + j * IN_TILE, IN_TILE), pl.multiple_of(wrow + j * IN_TILE, 16), done + j)
            return carry

        lax.fori_loop(0, width // IN_TILE, step, 0)
        done += width // IN_TILE


def in_proj(x, meta, wn, w_shards):
    first = NPAD + N_META
    steps = T // RC
    shard = IN_COLS // 8

    def body(x_hbm, meta_ref, wn_ref, g_hbm, o_ref, dt_ref, u_ref, h_ref, wt_hbm, wdt_ref, raw, w_ref, h_scr, g_sems, h_sems, out_sem):
        i = pl.program_id(0)
        slot = i % 2
        shards = [pltpu.make_async_copy(g_hbm.at[j], raw.at[j], g_sems.at[j]) for j in range(8)]
        head = pltpu.make_async_copy(x_hbm.at[pl.ds(0, RC - first)], h_scr.at[0, pl.ds(first, RC - first)], h_sems.at[0])
        put_back = pltpu.make_async_copy(w_ref, wt_hbm, out_sem)

        def rows_of(step):
            return pltpu.make_async_copy(x_hbm.at[pl.ds(pl.multiple_of(step * RC - first, 32), RC)], h_scr.at[step % 2], h_sems.at[step % 2])

        @pl.when(i == 0)
        def _():
            for cp in shards:
                cp.start()
            head.start()
            h_scr[0, 0:NPAD, :] = jnp.zeros((NPAD, D), F32)
            h_scr[0, NPAD:first, :] = meta_ref[...]

        @pl.when(i + 1 < steps)
        def _():
            rows_of(i + 1).start()

        @pl.when(i == 0)
        def _():
            head.wait()

        @pl.when(i > 0)
        def _():
            rows_of(i).wait()

        h_ref[...] = h_scr[slot]
        for r in (0, HALF):
            u_ref[r:r + HALF, :] = _rms(h_scr[slot, r:r + HALF, :], wn_ref[...]).astype(BF)

        @pl.when(i == 0)
        def _():
            for j, cp in enumerate(shards):
                cp.wait()
                w_ref[shard * j:shard * (j + 1), :] = raw[j]
            put_back.start()
            wdt_ref[...] = jnp.zeros_like(wdt_ref)
            for g in range(2):
                wdt_ref[128 * g:128 * g + 8, :] = w_ref[2560 + 8 * g:2568 + 8 * g, :]

        def tile(pcol, wrow, k):
            o_ref[:, pl.ds(pcol, IN_TILE)] = lax.dot_general(u_ref[...], w_ref[pl.ds(wrow, IN_TILE), :], NT_DIMS,
                                                             preferred_element_type=F32).astype(BF)

        _in_tiles(tile)
        dt_ref[...] = lax.dot_general(u_ref[...], wdt_ref[...], NT_DIMS, preferred_element_type=F32)

        @pl.when(i == steps - 1)
        def _():
            put_back.wait()

    return pl.pallas_call(
        body, grid=(steps,), in_specs=[ANY_SPEC, _spec((N_META, D), lambda i: (0, 0)), _vec(D), ANY_SPEC],
        out_specs=[_rows_spec(NP_IN), _rows_spec(256), _rows_spec(D), _rows_spec(D), ANY_SPEC, _spec((256, D), lambda i: (0, 0))],
        out_shape=[jax.ShapeDtypeStruct((T, NP_IN), BF), jax.ShapeDtypeStruct((T, 256), F32), jax.ShapeDtypeStruct((T, D), BF),
                   jax.ShapeDtypeStruct((T, D), F32), jax.ShapeDtypeStruct((IN_COLS, D), BF), jax.ShapeDtypeStruct((256, D), BF)],
        scratch_shapes=[pltpu.VMEM((8, shard, D), BF), pltpu.VMEM((IN_COLS, D), BF), pltpu.VMEM((2, RC, D), F32),
                        pltpu.SemaphoreType.DMA((8,)), pltpu.SemaphoreType.DMA((2,)), pltpu.SemaphoreType.DMA],
        compiler_params=_params(), name="in_proj")(x, meta, wn, w_shards)


def out_proj(yn_ssd, proj, hseq, lru_nw, w_out, h0):
    def body(y_ref, g_ref, h_ref, wn_ref, w_ref, r_ref, o_ref, cat_ref):
        cat_ref[:, 0:SSD_W] = y_ref[...]
        for r in (0, HALF):
            y = _gelu(g_ref[r:r + HALF, :].astype(F32)) * h_ref[r:r + HALF, :]
            cat_ref[r:r + HALF, SSD_W:] = _rms(y, wn_ref[...]).astype(BF)

        def tile(c0):
            o_ref[:, pl.ds(c0, 512)] = r_ref[:, pl.ds(c0, 512)] + jnp.dot(cat_ref[...], w_ref[:, pl.ds(c0, 512)], preferred_element_type=F32)

        _col_tiles(D, 512, tile)

    return pl.pallas_call(
        body, grid=(T // RC,),
        in_specs=[_rows_spec(SSD_W), _rows_spec(LRU_W, PG // LRU_W), _rows_spec(LRU_W), _vec(LRU_W), _whole((SSD_W + LRU_W, D)), _rows_spec(D)],
        out_specs=[_rows_spec(D), _rows_spec(SSD_W + LRU_W)],
        out_shape=[jax.ShapeDtypeStruct((T, D), F32), jax.ShapeDtypeStruct((T, SSD_W + LRU_W), BF)],
        compiler_params=_params(), name="out_proj")(yn_ssd, proj, hseq, lru_nw, w_out, h0)


def out_proj_bwd(dh1_b, w_out, proj, hseq, lru_nw, after):
    def body(d_ref, w_ref, g_ref, h_ref, wn_ref, _after, dy_ref, dh_ref, dg_ref, dw_ref, dl_scr):
        _zero_at_first(dw_ref)

        def tile(c0):
            dy_ref[:, pl.ds(c0, 512)] = lax.dot_general(d_ref[...], w_ref[pl.ds(c0, 512), :], NT_DIMS, preferred_element_type=F32)
            dl_scr[:, pl.ds(c0, 512)] = lax.dot_general(d_ref[...], w_ref[pl.ds(SSD_W + c0, 512), :], NT_DIMS, preferred_element_type=F32)

        _col_tiles(SSD_W, 512, tile)

        for r in (0, HALF):
            g = g_ref[r:r + HALF, :].astype(F32)
            h = h_ref[r:r + HALF, :]
            ge = _gelu(g)
            dy, dw = _rms_bwd(dl_scr[r:r + HALF, :], ge * h, wn_ref[...])
            dw_ref[...] += jnp.sum(dw, axis=0, keepdims=True)
            dh_ref[r:r + HALF, :] = dy * ge
            dg_ref[r:r + HALF, :] = (dy * h * _gelu_grad(g)).astype(BF)

    return pl.pallas_call(
        body, grid=(T // RC,),
        in_specs=[_rows_spec(D), _whole((SSD_W + LRU_W, D)), _rows_spec(LRU_W, PG // LRU_W), _rows_spec(LRU_W), _vec(LRU_W), ANY_SPEC],
        out_specs=[_rows_spec(SSD_W), _rows_spec(LRU_W), _rows_spec(LRU_W), _vec(LRU_W)],
        out_shape=[jax.ShapeDtypeStruct((T, SSD_W), F32), jax.ShapeDtypeStruct((T, LRU_W), F32), jax.ShapeDtypeStruct((T, LRU_W), BF),
                   jax.ShapeDtypeStruct((1, LRU_W), F32)],
        scratch_shapes=[pltpu.VMEM((RC, LRU_W), F32)],
        compiler_params=_params(), name="out_proj_bwd")(dh1_b, w_out, proj, hseq, lru_nw, after)


def in_proj_bwd(dz, dg, dxl, dxbc, ddt, w_t, w_dt, h0, wn, dh1, after):
    first = NPAD + N_META

    def body(dz_ref, dg_ref, dxl_ref, dxbc_ref, ddt_ref, w_hbm, wdt_ref, h_ref, wn_ref, r_ref, _after, gx_hbm, meta_ref, dw_ref, du_scr, o_ref, sem,
             w_ref, w_sems):
        i = pl.program_id(0)
        ready = _arriving(w_hbm, w_ref, w_sems, IN_TILE_ROWS, IN_TILE)
        _zero_at_first(dw_ref)
        du_scr[...] = jnp.dot(ddt_ref[...], wdt_ref[...], preferred_element_type=F32)
        done = 0
        for d_ref, wrow, width in ((dz_ref, 0, 1024), (dxbc_ref, 1024, XBC), (dg_ref, 2576, 1024), (dxl_ref, 3600, 1024)):
            def step(j, carry, d_ref=d_ref, wrow=wrow, done=done):
                c0 = pl.multiple_of(j * IN_TILE, IN_TILE)
                ready(done + j)
                du_scr[...] += jnp.dot(d_ref[:, pl.ds(c0, IN_TILE)], w_ref[pl.ds(pl.multiple_of(wrow + c0, 16), IN_TILE), :],
                                       preferred_element_type=F32)
                return carry

            lax.fori_loop(0, width // IN_TILE, step, 0)
            done += width // IN_TILE
        for r in (0, HALF):
            dh, dw = _rms_bwd(du_scr[r:r + HALF, :], h_ref[r:r + HALF, :], wn_ref[...])
            dw_ref[...] += jnp.sum(dw, axis=0, keepdims=True)
            o_ref[r:r + HALF, :] = dh + r_ref[r:r + HALF, :]

        @pl.when(i == 0)
        def _():
            meta_ref[...] = o_ref[NPAD:first, :]
            head = pltpu.make_async_copy(o_ref.at[pl.ds(first, RC - first)], gx_hbm.at[pl.ds(0, RC - first)], sem)
            head.start()
            head.wait()

        @pl.when(i > 0)
        def _():
            rest = pltpu.make_async_copy(o_ref, gx_hbm.at[pl.ds(pl.multiple_of(i * RC - first, 32), RC)], sem)
            rest.start()
            rest.wait()

    return pl.pallas_call(
        body, grid=(T // RC,),
        in_specs=[_rows_spec(SSD_W), _rows_spec(LRU_W), _rows_spec(LRU_W), _rows_spec(XBC), _rows_spec(256), ANY_SPEC,
                  _whole((256, D)), _rows_spec(D), _vec(D), _rows_spec(D), ANY_SPEC],
        out_specs=[ANY_SPEC, _spec((N_META, D), lambda i: (0, 0)), _vec(D)],
        out_shape=[jax.ShapeDtypeStruct((SEQ, D), F32), jax.ShapeDtypeStruct((N_META, D), F32), jax.ShapeDtypeStruct((1, D), F32)],
        scratch_shapes=[pltpu.VMEM((RC, D), F32), pltpu.VMEM((RC, D), F32), pltpu.SemaphoreType.DMA,
                        pltpu.VMEM((IN_COLS, D), BF), pltpu.SemaphoreType.DMA((len(IN_TILE_ROWS),))],
        compiler_params=_params(), name="in_proj_bwd")(dz, dg, dxl, dxbc, ddt, w_t, w_dt, h0, wn, dh1, after)


GRAD_TILE = 256


def weight_grad(name, a, u1):
    tm = GRAD_TILE

    def body(a_ref, u_ref, o_ref):
        o_ref[...] = lax.dot_general(a_ref[...], u_ref[...], TN_DIMS, preferred_element_type=F32).astype(BF)

    return pl.pallas_call(
        body, grid=(a.shape[1] // tm,),
        in_specs=[_spec((T, tm), lambda j: (0, j)), _spec((T, D), lambda j: (0, 0), single=True)],
        out_specs=_spec((tm, D), lambda j: (j, 0)),
        out_shape=jax.ShapeDtypeStruct((a.shape[1], D), BF),
        compiler_params=_params(), name=name)(a, u1)


def in_weight_grad(parts, first_rows, ddt, u1):
    tm = GRAD_TILE
    per_row = D // 128
    dt_row, dt_lines = 2560, 8 * per_row
    parts = list(parts) + [ddt]
    first_rows = list(first_rows) + [dt_row]
    tiles = [p.shape[1] // tm for p in parts]
    starts = [sum(tiles[:k]) for k in range(len(parts))]
    last = sum(tiles) - 1

    def body(*refs):
        a_refs, u_ref = refs[:len(parts)], refs[len(parts)]
        o_hbm, mix_scr, stage, sems = refs[len(parts) + 1:]
        step = pl.program_id(0)
        slot = step % 2
        line0 = 0
        for a_ref, start, n, first in zip(a_refs, starts, tiles, first_rows):
            here = (step >= start) & (step < start + n)
            line0 = jnp.where(here, per_row * (first + tm * (step - start)), line0)

            @pl.when(here)
            def _(a_ref=a_ref):
                res = lax.dot_general(a_ref[...], u_ref[...], TN_DIMS, preferred_element_type=F32)
                for q in range(per_row):
                    mix_scr[pl.ds(q, tm, stride=per_row), :] = res[:, 128 * q:128 * q + 128]

        def tile_copy(of_slot, to):
            return pltpu.make_async_copy(stage.at[of_slot], o_hbm.at[pl.ds(to, per_row * tm)], sems.at[of_slot])

        @pl.when(step >= 2)
        def _():
            tile_copy(slot, 0).wait()

        stage[slot] = mix_scr[...].astype(BF)

        @pl.when(step < last)
        def _():
            tile_copy(slot, pl.multiple_of(line0, 128)).start()

        @pl.when(step == last)
        def _():
            halves = [pltpu.make_async_copy(stage.at[slot, pl.ds(128 * per_row * k, dt_lines)],
                                            o_hbm.at[pl.ds(per_row * (dt_row + 8 * k), dt_lines)], sems.at[2 + k]) for k in range(2)]
            for cp in halves:
                cp.start()
            tile_copy(1 - slot, 0).wait()
            for cp in halves:
                cp.wait()

    def tile_of(start, n):
        return lambda j: (0, jnp.clip(j - start, 0, n - 1))

    return pl.pallas_call(
        body, grid=(last + 1,),
        in_specs=[_spec((T, tm), tile_of(s, n)) for s, n in zip(starts, tiles)] + [_spec((T, D), lambda j: (0, 0), single=True)],
        out_specs=ANY_SPEC,
        out_shape=jax.ShapeDtypeStruct((per_row * IN_COLS, 128), BF),
        scratch_shapes=[pltpu.VMEM((per_row * tm, 128), F32), pltpu.VMEM((2, per_row * tm, 128), BF), pltpu.SemaphoreType.DMA((4,))],
        compiler_params=_params(), name="dw_in")(*parts, u1)


def _ssd_chunk_common(row0, dt_ref, b_ref, c_ref, bias, a_neg):
    shape = (Q, Q)
    lane = _lanes(shape)
    sub = _rows(shape)
    live = (_rows(shape, row0) >= NPAD) & (lane < 8)
    dtr = dt_ref[:, :]
    dt = jnp.where(live, _softplus(dtr + bias), 0.0)
    d_a = dt * a_neg
    tri = (sub >= lane).astype(F32)
    cs = jnp.dot(tri, d_a, precision=lax.Precision.HIGHEST, preferred_element_type=F32)
    cs_t = cs.T
    b_f = b_ref[:, :]
    bc = b_f.astype(BF)
    cc = c_ref[:, :].astype(BF)
    cb = lax.dot_general(cc, bc, NT_DIMS, preferred_element_type=F32)
    cs_last = cs[Q - 1:Q, :]
    return dict(lane=lane, sub=sub, live=live, dtr=dtr, dt=dt, cs=cs, cs_t=cs_t, bc=bc, cc=cc, cb=cb, bc_t=b_f.T.astype(BF),
                ecs=jnp.exp(cs), dsm=jnp.exp(cs_last - cs), gam=jnp.exp(cs_last))


def _pair(lane_even, mat, j):
    return jnp.where(lane_even, mat[:, j:j + 1], mat[:, j + 1:j + 2])


def _pair_row(lane_even, mat, j):
    return jnp.where(lane_even[0:1, :], mat[:, j:j + 1], mat[:, j + 1:j + 2])


def _head_decay(cm, j):
    seg = cm["cs"][:, j:j + 1] - cm["cs_t"][j:j + 1, :]
    return jnp.exp(jnp.where(cm["sub"] >= cm["lane"], seg, -jnp.inf))


def _head_decay_t(cm, j):
    seg = cm["cs_t"][j:j + 1, :] - cm["cs"][:, j:j + 1]
    return jnp.exp(jnp.where(cm["lane"] >= cm["sub"], seg, -jnp.inf))


def _conv_window(raw_ref, halo_ref, pad_scr):
    pad_scr[0:HALO, :] = halo_ref[...].astype(F32)[halo_ref.shape[0] - HALO:, :]
    pad_scr[HALO:HALO + Q, :] = raw_ref[...].astype(F32)
    win = pad_scr[...]
    return lambda s: win[HALO:, :] if s == 0 else pltpu.roll(win, s, axis=0)[HALO:, :]


def _xbc_cols(g):
    return slice(512 * g, 512 * g + 512), slice(SSD_W + 128 * g, SSD_W + 128 * g + 128), slice(SSD_W + 256 + 128 * g, SSD_W + 384 + 128 * g)


def ssd_fwd(proj, dt_raw, conv_w, conv_b, dt_bias2, a_log2, d2, norm_w):
    def body(raw_ref, halo_ref, dt_all, z_all, cw_ref, cb_ref, bias_all, alog_all, d_all, nw_all, yn_all, y_all, hp_all,
             h_all, pad_scr, act_scr):
        @pl.when(pl.program_id(0) == 0)
        def _():
            h_all[...] = jnp.zeros_like(h_all)

        pre = _conv(_conv_window(raw_ref, halo_ref, pad_scr), cw_ref[...], cb_ref[...])
        act_scr[...] = pre * _sigmoid(pre)
        for g in range(2):
            wide, thin = slice(512 * g, 512 * g + 512), slice(128 * g, 128 * g + 128)
            xs, bs, cs = _xbc_cols(g)
            group(act_scr.at[:, xs], act_scr.at[:, bs], act_scr.at[:, cs], dt_all.at[:, thin], z_all.at[:, wide], bias_all.at[g],
                  alog_all.at[g], d_all.at[g], nw_all.at[:, wide], yn_all.at[:, wide], y_all.at[:, wide], hp_all.at[g, 0], h_all.at[g])

    def group(x_ref, b_ref, c_ref, dt_ref, z_ref, bias_ref, alog_ref, d_ref, nw_ref, yn_ref, y_ref, hp_ref, h_scr):
        bias = bias_ref[...]
        a_neg = -jnp.exp(alog_ref[...])
        dsk = d_ref[...]
        cm = _ssd_chunk_common(pl.program_id(0) * Q, dt_ref, b_ref, c_ref, bias, a_neg)
        lane_even = cm["lane"] < 64
        for p in range(4):
            je, jo = 2 * p, 2 * p + 1
            xp = x_ref[:, 128 * p:128 * p + 128]
            xdt = xp * _pair(lane_even, cm["dt"], je)
            xdt_b = xdt.astype(BF)
            m_e = (cm["cb"] * _head_decay(cm, je)).astype(BF)
            m_o = (cm["cb"] * _head_decay(cm, jo)).astype(BF)
            zero = jnp.zeros_like(xdt_b)
            yd = (jnp.dot(m_e, jnp.where(lane_even, xdt_b, zero), preferred_element_type=F32)
                  + jnp.dot(m_o, jnp.where(lane_even, zero, xdt_b), preferred_element_type=F32))
            hp = h_scr[p]
            hp_ref[p] = hp
            yo = jnp.dot(cm["cc"], hp.astype(BF), preferred_element_type=F32) * _pair(lane_even, cm["ecs"], je)
            y_ref[:, 128 * p:128 * p + 128] = yd + yo + xp * _pair_row(lane_even, dsk, je)
            st = jnp.dot(cm["bc_t"], (xdt * _pair(lane_even, cm["dsm"], je)).astype(BF), preferred_element_type=F32)
            h_scr[p] = hp * _pair_row(lane_even, cm["gam"], je) + st
        zc = z_ref[:, :].astype(F32)
        gated = y_ref[:, :] * (zc * _sigmoid(zc))
        yn_ref[:, :] = _rms(gated, nw_ref[...]).astype(BF)

    par = _spec((2, 1, 128), lambda c: (0, 0, 0))
    wide = _spec((Q, SSD_W), lambda c: (c, 0))
    xbc = PXBC // XBC
    halo = 2 * HALO
    return pl.pallas_call(
        body, grid=(NCH,),
        in_specs=[_spec((Q, XBC), lambda c: (c, xbc)), _spec((halo, XBC), lambda c: (jnp.maximum(c * (Q // halo) - 1, 0), xbc)),
                  _spec((Q, 256), lambda c: (c, 0)), wide, _spec((4, XBC), lambda c: (0, 0)), _spec((1, XBC), lambda c: (0, 0)),
                  par, par, par, _spec((1, SSD_W), lambda c: (0, 0))],
        out_specs=[wide, wide, _spec((2, 1, 4, 128, 128), lambda c: (0, c, 0, 0, 0))],
        out_shape=[jax.ShapeDtypeStruct((T, SSD_W), BF), jax.ShapeDtypeStruct((T, SSD_W), F32),
                   jax.ShapeDtypeStruct((2, NCH, 4, 128, 128), F32)],
        scratch_shapes=[pltpu.VMEM((2, 4, 128, 128), F32), pltpu.VMEM((Q + HALO, XBC), F32), pltpu.VMEM((Q, XBC), F32)],
        compiler_params=_params(), name="ssd_fwd")(proj, proj, dt_raw, proj, conv_w, conv_b, dt_bias2, a_log2, d2, norm_w)


def ssd_bwd(dyn, proj, dt_raw, conv_w, conv_b, y_pre, h_prev, dt_bias2, a_log2, d2, norm_w, after):
    def body(dyn_all, raw_ref, halo_ref, dt_all, z_all, y_all, hp_all, cw_ref, cb_ref, bias_all, alog_all, d_all, nw_all, _after,
             dz_all, dxbc_ref, ddt_all, dpar_all, dnw_all, dcw_ref, dcb_ref, dh_all, acc_all, pad_scr, act_scr, dsilu_scr, dact_scr, dpad_scr):
        @pl.when(pl.program_id(0) == 0)
        def _():
            dh_all[...] = jnp.zeros_like(dh_all)
            acc_all[...] = jnp.zeros_like(acc_all)
            dnw_all[...] = jnp.zeros_like(dnw_all)
            dcw_ref[...] = jnp.zeros_like(dcw_ref)
            dcb_ref[...] = jnp.zeros_like(dcb_ref)
            dpad_scr[Q:Q + HALO, :] = jnp.zeros((HALO, XBC), F32)

        back = _conv_window(raw_ref, halo_ref, pad_scr)
        pre = _conv(back, cw_ref[...], cb_ref[...])
        sg = _sigmoid(pre)
        act_scr[...] = pre * sg
        dsilu_scr[...] = sg * (1.0 + pre * (1.0 - sg))
        for g in range(2):
            wide, thin = slice(512 * g, 512 * g + 512), slice(128 * g, 128 * g + 128)
            xs, bs, cs = _xbc_cols(g)
            group(dyn_all.at[:, wide], act_scr.at[:, xs], act_scr.at[:, bs], act_scr.at[:, cs], dt_all.at[:, thin], z_all.at[:, wide],
                  y_all.at[:, wide], hp_all.at[g, 0], bias_all.at[g], alog_all.at[g], d_all.at[g], nw_all.at[:, wide],
                  dz_all.at[:, wide], dact_scr.at[:, xs], dact_scr.at[:, bs], dact_scr.at[:, cs], ddt_all.at[:, thin], dpar_all.at[g],
                  dnw_all.at[:, wide], dh_all.at[g], acc_all.at[g])
        dpre = dact_scr[...] * dsilu_scr[...]
        dcw, dcb = _conv_bwd_w(dpre, back)
        dcw_ref[...] += dcw
        dcb_ref[...] += dcb
        dpad_scr[0:Q, :] = dpre
        win = dpad_scr[...]
        dxbc_ref[...] = _conv_bwd_x(lambda s: win[:Q, :] if s == 0 else pltpu.roll(win, Q + HALO - s, axis=0)[:Q, :], cw_ref[...]).astype(BF)
        dpad_scr[Q:Q + HALO, :] = dpre[0:HALO, :]

    def group(dyn_ref, x_ref, b_ref, c_ref, dt_ref, z_ref, y_ref, hp_ref, bias_ref, alog_ref, d_ref, nw_ref,
              dz_ref, dx_ref, db_ref, dc_ref, ddt_ref, dpar_ref, dnw_ref, dh_scr, acc_scr):
        ci = pl.program_id(0)
        bias = bias_ref[...]
        a_neg = -jnp.exp(alog_ref[...])
        dsk = d_ref[...]
        cm = _ssd_chunk_common((NCH - 1 - ci) * Q, dt_ref, b_ref, c_ref, bias, a_neg)
        lane, sub = cm["lane"], cm["sub"]
        lane_even = lane < 64
        cc_t = c_ref[:, :].T.astype(BF)
        cb_t = lax.dot_general(cm["bc"], cm["cc"], NT_DIMS, preferred_element_type=F32)
        zc = z_ref[:, :].astype(F32)
        yc = y_ref[:, :]
        sg = _sigmoid(zc)
        sz = zc * sg
        dgated, dnw = _rms_bwd(dyn_ref[:, :], yc * sz, nw_ref[...])
        dnw_ref[...] += jnp.sum(dnw, axis=0, keepdims=True)
        dz_ref[:, :] = (dgated * yc * (sg * (1.0 + zc * (1.0 - sg)))).astype(BF)
        dy_all = dgated * sz
        dcb = jnp.zeros((Q, Q), F32)
        dcb_t = jnp.zeros((Q, Q), F32)
        db_acc = jnp.zeros((Q, Q), F32)
        dc_acc = jnp.zeros((Q, Q), F32)
        dcs = jnp.zeros((Q, Q), F32)
        ddt = jnp.zeros((Q, Q), F32)
        for p in range(4):
            je, jo = 2 * p, 2 * p + 1
            xp = x_ref[:, 128 * p:128 * p + 128]
            dy = dy_all[:, 128 * p:128 * p + 128]
            dt_p = _pair(lane_even, cm["dt"], je)
            xdt = xp * dt_p
            xdt_b = xdt.astype(BF)
            dy_b = dy.astype(BF)
            zero = jnp.zeros_like(dy_b)
            hp = hp_ref[p]
            hp_b = hp.astype(BF)
            dh = dh_scr[p]
            dh_b = dh.astype(BF)
            acc_scr[p:p + 1, :] += jnp.sum(dy * xp, axis=0, keepdims=True)
            dxp = dy * _pair_row(lane_even, dsk, je)
            e_p = _pair(lane_even, cm["ecs"], je)
            g_p = jnp.dot(cm["cc"], hp_b, preferred_element_type=F32)
            dg_b = (dy * e_p).astype(BF)
            de = dy * g_p * e_p
            dc_acc = dc_acc + lax.dot_general(dg_b, hp_b, NT_DIMS, preferred_element_type=F32)
            dh_in = jnp.dot(cc_t, dg_b, preferred_element_type=F32)
            ds_p = _pair(lane_even, cm["dsm"], je)
            r_p = jnp.dot(cm["bc"], dh_b, preferred_element_type=F32)
            dxdt = r_p * ds_p
            tt = r_p * xdt * ds_p
            db_acc = db_acc + lax.dot_general((xdt * ds_p).astype(BF), dh_b, NT_DIMS, preferred_element_type=F32)
            dgam_m = jnp.sum(dh * hp, axis=0, keepdims=True)
            for j, even in ((je, True), (jo, False)):
                sel = lane_even if even else jnp.logical_not(lane_even)
                dy_j = jnp.where(sel, dy_b, zero)
                l_j = _head_decay(cm, j)
                l_jt = _head_decay_t(cm, j)
                m_j = cm["cb"] * l_j
                m_jt = cb_t * l_jt
                dm = lax.dot_general(dy_j, xdt_b, NT_DIMS, preferred_element_type=F32)
                dm_t = lax.dot_general(xdt_b, dy_j, NT_DIMS, preferred_element_type=F32)
                dxdt = dxdt + jnp.dot(m_jt.astype(BF), dy_j, preferred_element_type=F32)
                dcb = dcb + dm * l_j
                dcb_t = dcb_t + dm_t * l_jt
                t_j = jnp.where(sel, tt, 0.0)
                col = jnp.sum(dm * m_j - dm_t * m_jt + (jnp.where(sel, de, 0.0) - t_j), axis=1, keepdims=True)
                gam_j = cm["gam"][:, j:j + 1]
                last = (jnp.sum(jnp.sum(t_j, axis=0, keepdims=True), axis=1, keepdims=True)
                        + jnp.sum(jnp.where(sel[0:1, :], dgam_m, 0.0), axis=1, keepdims=True) * gam_j)
                col = col + jnp.where(sub[:, 0:1] == Q - 1, last, 0.0)
                dcs = dcs + jnp.where(lane == j, col, 0.0)
            dh_scr[p] = dh_in + dh * _pair_row(lane_even, cm["gam"], je)
            dx_ref[:, 128 * p:128 * p + 128] = dxp + dxdt * dt_p
            dd = dxdt * xp
            ddt = ddt + jnp.where(lane == je, jnp.sum(jnp.where(lane_even, dd, 0.0), axis=1, keepdims=True), 0.0)
            ddt = ddt + jnp.where(lane == jo, jnp.sum(jnp.where(lane_even, 0.0, dd), axis=1, keepdims=True), 0.0)
        dc_ref[:, :] = dc_acc + jnp.dot(dcb.astype(BF), cm["bc"], preferred_element_type=F32)
        db_ref[:, :] = db_acc + jnp.dot(dcb_t.astype(BF), cm["cc"], preferred_element_type=F32)
        tri_t = (sub <= lane).astype(F32)
        dd_a = jnp.dot(tri_t, dcs, precision=lax.Precision.HIGHEST, preferred_element_type=F32)
        ddt = ddt + dd_a * a_neg
        acc_scr[5:6, :] += jnp.sum(dd_a * cm["dt"], axis=0, keepdims=True)
        draw = jnp.where(cm["live"], ddt * _sigmoid_gate(cm["dtr"] + bias), 0.0)
        acc_scr[4:5, :] += jnp.sum(draw, axis=0, keepdims=True)
        ddt_ref[:, :] = draw.astype(BF)

        @pl.when(ci == NCH - 1)
        def _():
            lane1 = _lanes((1, 128))
            dd = jnp.zeros((1, 128), F32)
            for p in range(4):
                row = acc_scr[p:p + 1, :]
                dd = dd + jnp.where(lane1 == 2 * p, jnp.sum(jnp.where(lane1 < 64, row, 0.0), axis=1, keepdims=True), 0.0)
                dd = dd + jnp.where(lane1 == 2 * p + 1, jnp.sum(jnp.where(lane1 < 64, 0.0, row), axis=1, keepdims=True), 0.0)
            dpar_ref[...] = jnp.concatenate([acc_scr[4:5, :], acc_scr[5:6, :] * a_neg, dd, jnp.zeros((5, 128), F32)], axis=0)

    par = _spec((2, 1, 128), lambda c: (0, 0, 0))
    wide = _spec((Q, SSD_W), lambda c: (NCH - 1 - c, 0))
    thin = _spec((Q, 256), lambda c: (NCH - 1 - c, 0))
    vec = _spec((1, SSD_W), lambda c: (0, 0))
    xbc = PXBC // XBC
    halo = 2 * HALO
    chunk = pltpu.VMEM((Q, XBC), F32)
    padded = pltpu.VMEM((Q + HALO, XBC), F32)
    return pl.pallas_call(
        body, grid=(NCH,),
        in_specs=[wide, _spec((Q, XBC), lambda c: (NCH - 1 - c, xbc)),
                  _spec((halo, XBC), lambda c: (jnp.maximum((NCH - 1 - c) * (Q // halo) - 1, 0), xbc)), thin, wide, wide,
                  _spec((2, 1, 4, 128, 128), lambda c: (0, NCH - 1 - c, 0, 0, 0)), _spec((4, XBC), lambda c: (0, 0)),
                  _spec((1, XBC), lambda c: (0, 0)), par, par, par, vec, ANY_SPEC],
        out_specs=[wide, _spec((Q, XBC), lambda c: (NCH - 1 - c, 0)), thin, _spec((2, 8, 128), lambda c: (0, 0, 0)), vec,
                   _spec((4, XBC), lambda c: (0, 0)), _spec((1, XBC), lambda c: (0, 0))],
        out_shape=[jax.ShapeDtypeStruct((T, SSD_W), BF), jax.ShapeDtypeStruct((T, XBC), BF), jax.ShapeDtypeStruct((T, 256), BF),
                   jax.ShapeDtypeStruct((2, 8, 128), F32), jax.ShapeDtypeStruct((1, SSD_W), F32), jax.ShapeDtypeStruct((4, XBC), F32),
                   jax.ShapeDtypeStruct((1, XBC), F32)],
        scratch_shapes=[pltpu.VMEM((2, 4, 128, 128), F32), pltpu.VMEM((2, 8, 128), F32), padded, chunk, chunk, chunk, padded],
        compiler_params=_params(), name="ssd_bwd")(dyn, proj, proj, dt_raw, proj, y_pre, h_prev, conv_w, conv_b, dt_bias2, a_log2, d2, norm_w, after)


def _lru_gates(back, cw, cb, wa, ba, wx, bx, lam):
    xr = _conv(back, cw, cb)
    xr_b = xr.astype(BF)
    r = _sigmoid_gate(jnp.dot(xr_b, wa, preferred_element_type=F32) + ba)
    i = _sigmoid_gate(jnp.dot(xr_b, wx, preferred_element_type=F32) + bx)
    sp = _softplus(-lam)
    la = (-LRU_C) * r * sp
    a = jnp.exp(la)
    mult2 = -jnp.tanh(la) * (a * a + 1.0)
    return xr, xr_b, r, i, sp, a, jnp.sqrt(mult2), mult2


SEG_LEN = 68
SEGS = T // SEG_LEN


def _seg_rows(j, k, off=0):
    return pl.ds(off + j * 8 * SEG_LEN + k, 8, stride=SEG_LEN)


def _segmented_scan(mul_ref, mul_row0, add_ref, out_ref, loc_scr, prod_scr, carry_scr, reverse):
    groups = SEGS // 8
    off = mul_row0 + (1 if reverse else 0)

    def local(i, carry):
        k = SEG_LEN - 1 - i if reverse else i
        new = []
        for j in range(groups):
            h, p = carry[2 * j], carry[2 * j + 1]
            m = mul_ref[_seg_rows(j, k, off), :]
            h = m * h + add_ref[_seg_rows(j, k), :]
            p = m * p
            loc_scr[_seg_rows(j, k), :] = h
            prod_scr[_seg_rows(j, k), :] = p
            new += [h, p]
        return tuple(new)

    lax.fori_loop(0, SEG_LEN, local, (jnp.zeros((8, 128), F32), jnp.ones((8, 128), F32)) * groups)

    def chain(i, c):
        s = SEGS - 1 - i if reverse else i
        carry_scr[pl.ds(s, 1), :] = c
        edge = s * SEG_LEN + (0 if reverse else SEG_LEN - 1)
        return loc_scr[pl.ds(edge, 1), :] + prod_scr[pl.ds(edge, 1), :] * c

    lax.fori_loop(0, SEGS, chain, jnp.zeros((1, 128), F32))

    def fold(k, carry):
        for j in range(groups):
            rows = _seg_rows(j, k)
            out_ref[rows, :] = loc_scr[rows, :] + prod_scr[rows, :] * carry_scr[8 * j:8 * j + 8, :]
        return carry

    lax.fori_loop(0, SEG_LEN, fold, 0)


def lru_fwd(proj, cw, cb, wa2, ba, wx2, bx, lam, after):
    def body(x_ref, cw_ref, cb_ref, wa_ref, ba_ref, wx_ref, bx_ref, lam_ref, _after, h_ref, a_ref, xpad, u_scr, loc_scr, prod_scr, carry_scr):
        _fill_padded(xpad, x_ref)

        def chunk(r0):
            xr, _, _, i, _, a, mult, _ = _lru_gates(_back(xpad, r0), cw_ref[...], cb_ref[...], wa_ref[0], ba_ref[...], wx_ref[0], bx_ref[...],
                                                 lam_ref[...])
            a_ref[pl.ds(r0, Q), :] = a
            u_scr[pl.ds(r0, Q), :] = jnp.where(_rows(a.shape, r0) >= NPAD, mult * (i * xr), 0.0)

        _chunks(chunk, unrolled=True)
        _segmented_scan(a_ref, 0, u_scr, h_ref, loc_scr, prod_scr, carry_scr, reverse=False)

    c0 = PXL // 128
    vec = _spec((1, 128), lambda c: (0, c))
    mat = _spec((1, 128, 128), lambda c: (c, 0, 0))
    seq = pltpu.VMEM((T, 128), F32)
    return pl.pallas_call(
        body, grid=(8,),
        in_specs=[_spec((T, 128), lambda c: (0, c0 + c)), _spec((4, 128), lambda c: (0, c)), vec, mat, vec, mat, vec, vec, ANY_SPEC],
        out_specs=[_spec((T, 128), lambda c: (0, c)), _spec((T, 128), lambda c: (0, c))],
        out_shape=[jax.ShapeDtypeStruct((T, LRU_W), F32), jax.ShapeDtypeStruct((T, LRU_W), F32)],
        scratch_shapes=[pltpu.VMEM((T + 2 * HALO, 128), F32), seq, seq, seq, pltpu.VMEM((SEGS, 128), F32)],
        compiler_params=_params(), name="lru_fwd")(proj, cw, cb, wa2, ba, wx2, bx, lam, after)


def lru_bwd(dh_out, a, hseq, proj, cw, cb, wa2, ba, wx2, bx, lam):
    def body(d_ref, a_ref, h_ref, x_ref, cw_ref, cb_ref, wa_ref, ba_ref, wx_ref, bx_ref, lam_ref,
             dx_ref, dcw_ref, dcb_ref, dwa_ref, dba_ref, dwx_ref, dbx_ref, dlam_ref, xpad, hpad, dpad, dh_ref, loc_scr, prod_scr, carry_scr):
        _fill_padded(dpad, a_ref)
        _segmented_scan(dpad, HALO, d_ref, dh_ref, loc_scr, prod_scr, carry_scr, reverse=True)
        _fill_padded(xpad, x_ref)
        _fill_padded(hpad, h_ref)
        dpad[0:HALO, :] = jnp.zeros((HALO, 128), F32)
        dpad[T + HALO:T + 2 * HALO, :] = jnp.zeros((HALO, 128), F32)
        for ref in (dcw_ref, dcb_ref, dwa_ref, dba_ref, dwx_ref, dbx_ref, dlam_ref):
            ref[...] = jnp.zeros_like(ref)
        lam = lam_ref[...]

        def first(r0):
            back = _back(xpad, r0)
            xr, xr_b, r, i, sp, a, mult, mult2 = _lru_gates(back, cw_ref[...], cb_ref[...], wa_ref[0], ba_ref[...], wx_ref[0], bx_ref[...], lam)
            dh = dh_ref[pl.ds(r0, Q), :]
            da = dh * _back(hpad, r0)(1)
            du = jnp.where(_rows(dh.shape, r0) >= NPAD, dh, 0.0)
            dmult = du * (i * xr)
            di = du * (mult * xr)
            dxr = du * (mult * i)
            dla = da * a - dmult * (a * a) * lax.rsqrt(mult2)
            dr = dla * ((-LRU_C) * sp)
            dlam_ref[...] += jnp.sum(dla * ((-LRU_C) * r), axis=0, keepdims=True)
            dpr = dr * r * (1.0 - r)
            dpi = di * i * (1.0 - i)
            dba_ref[...] += jnp.sum(dpr, axis=0, keepdims=True)
            dbx_ref[...] += jnp.sum(dpi, axis=0, keepdims=True)
            dpr_b = dpr.astype(BF)
            dpi_b = dpi.astype(BF)
            dxr = (dxr + lax.dot_general(dpr_b, wa_ref[0], NT_DIMS, preferred_element_type=F32)
                   + lax.dot_general(dpi_b, wx_ref[0], NT_DIMS, preferred_element_type=F32))
            dwa_ref[0] += lax.dot_general(xr_b, dpr_b, TN_DIMS, preferred_element_type=F32)
            dwx_ref[0] += lax.dot_general(xr_b, dpi_b, TN_DIMS, preferred_element_type=F32)
            dpad[pl.ds(r0 + HALO, Q), :] = dxr
            dcw, dcb = _conv_bwd_w(dxr, back)
            dcw_ref[...] += dcw
            dcb_ref[...] += dcb

        _chunks(first, unrolled=True)
        dlam_ref[...] = -dlam_ref[...] * _sigmoid_gate(-lam)

        def second(r0):
            dx_ref[pl.ds(r0, Q), :] = _conv_bwd_x(_ahead(dpad, r0), cw_ref[...]).astype(BF)

        _chunks(second)

    c0 = PXL // 128
    vec = _spec((1, 128), lambda c: (0, c))
    mat = _spec((1, 128, 128), lambda c: (c, 0, 0))
    col = _spec((T, 128), lambda c: (0, c))
    vshape = jax.ShapeDtypeStruct((1, LRU_W), F32)
    mshape = jax.ShapeDtypeStruct((8, 128, 128), F32)
    pad = pltpu.VMEM((T + 2 * HALO, 128), F32)
    seq = pltpu.VMEM((T, 128), F32)
    return pl.pallas_call(
        body, grid=(8,),
        in_specs=[col, col, col, _spec((T, 128), lambda c: (0, c0 + c)), _spec((4, 128), lambda c: (0, c)), vec, mat, vec, mat, vec, vec],
        out_specs=[col, _spec((4, 128), lambda c: (0, c)), vec, mat, vec, mat, vec, vec],
        out_shape=[jax.ShapeDtypeStruct((T, LRU_W), BF), jax.ShapeDtypeStruct((4, LRU_W), F32), vshape, mshape, vshape, mshape, vshape, vshape],
        scratch_shapes=[pad, pad, pad, seq, seq, seq, pltpu.VMEM((SEGS, 128), F32)],
        compiler_params=_params(), name="lru_bwd")(dh_out, a, hseq, proj, cw, cb, wa2, ba, wx2, bx, lam)


FF_TILE = 256
FF_TILE_ROWS = list(range(0, D_FF, FF_TILE))


def gate_up(h1, wn, w_gate, w_up):
    def body(h_ref, wn_ref, wg_hbm, wu_hbm, gt_ref, up_ref, act_ref, u_ref, wg_ref, wu_ref, wg_sems, wu_sems):
        gate_ready = _arriving(wg_hbm, wg_ref, wg_sems, FF_TILE_ROWS, FF_TILE)
        up_ready = _arriving(wu_hbm, wu_ref, wu_sems, FF_TILE_ROWS, FF_TILE)
        for r in (0, HALF):
            u_ref[r:r + HALF, :] = _rms(h_ref[r:r + HALF, :], wn_ref[...]).astype(BF)

        def tile(c0):
            cols = pl.ds(c0, FF_TILE)
            gate_ready(c0 // FF_TILE)
            up_ready(c0 // FF_TILE)
            gt = lax.dot_general(u_ref[...], wg_ref[cols, :], NT_DIMS, preferred_element_type=F32)
            up = lax.dot_general(u_ref[...], wu_ref[cols, :], NT_DIMS, preferred_element_type=F32)
            gt_ref[:, cols] = gt.astype(BF)
            up_ref[:, cols] = up.astype(BF)
            act_ref[:, cols] = (gt * _sigmoid(gt) * up).astype(BF)

        _col_tiles(D_FF, FF_TILE, tile)

    big = jax.ShapeDtypeStruct((T, D_FF), BF)
    return pl.pallas_call(
        body, grid=(T // RC,), in_specs=[_rows_spec(D), _vec(D), ANY_SPEC, ANY_SPEC],
        out_specs=[_rows_spec(D_FF), _rows_spec(D_FF), _rows_spec(D_FF), _rows_spec(D)],
        out_shape=[big, big, big, jax.ShapeDtypeStruct((T, D), BF)],
        scratch_shapes=[pltpu.VMEM((D_FF, D), BF)] * 2 + [pltpu.SemaphoreType.DMA((len(FF_TILE_ROWS),))] * 2,
        compiler_params=_params(), name="gate_up")(h1, wn, w_gate, w_up)


def down_loss(act, w_down, h1, target, wf):
    first = NPAD + N_META

    def body(a_ref, w_ref, r_ref, t_hbm, wf_ref, d_ref, db_ref, l_ref, dw_ref, h_scr, t_ref, t_sem):
        i = pl.program_id(0)
        _zero_at_first(l_ref, dw_ref)
        head = pltpu.make_async_copy(t_hbm.at[pl.ds(0, RC - first)], t_ref.at[pl.ds(first, RC - first)], t_sem)
        rest = pltpu.make_async_copy(t_hbm.at[pl.ds(pl.multiple_of(jnp.maximum(i * RC - first, 0), 32), RC)], t_ref, t_sem)

        @pl.when(i == 0)
        def _():
            t_ref[0:first, :] = jnp.zeros((first, D), F32)
            head.start()

        @pl.when(i > 0)
        def _():
            rest.start()

        def tile(c0):
            cols = pl.ds(c0, 512)
            h_scr[:, cols] = r_ref[:, cols] + jnp.dot(a_ref[...], w_ref[:, cols], preferred_element_type=F32)

        _col_tiles(D, 512, tile)

        @pl.when(i == 0)
        def _():
            head.wait()

        @pl.when(i > 0)
        def _():
            rest.wait()

        for r in (0, HALF):
            h = h_scr[r:r + HALF, :]
            live = _rows((HALF, D), i * RC + r) >= first
            err = jnp.where(live, _rms(h, wf_ref[...]) - t_ref[r:r + HALF, :], 0.0)
            l_ref[...] += 0.5 * jnp.sum(jnp.sum(err * err, axis=1, keepdims=True) * (1.0 / D), axis=0, keepdims=True)
            dh, dw = _rms_bwd(err * (1.0 / D), h, wf_ref[...])
            dw_ref[...] += jnp.sum(dw, axis=0, keepdims=True)
            d_ref[r:r + HALF, :] = dh
            db_ref[r:r + HALF, :] = dh.astype(BF)

    return pl.pallas_call(
        body, grid=(T // RC,),
        in_specs=[_rows_spec(D_FF), _whole((D_FF, D)), _rows_spec(D), pl.BlockSpec(memory_space=pl.ANY), _vec(D)],
        out_specs=[_rows_spec(D), _rows_spec(D), _spec((1, 128), lambda i: (0, 0)), _vec(D)],
        out_shape=[jax.ShapeDtypeStruct((T, D), F32), jax.ShapeDtypeStruct((T, D), BF), jax.ShapeDtypeStruct((1, 128), F32),
                   jax.ShapeDtypeStruct((1, D), F32)],
        scratch_shapes=[pltpu.VMEM((RC, D), F32), pltpu.VMEM((RC, D), F32), pltpu.SemaphoreType.DMA],
        compiler_params=_params(), name="down_loss")(act, w_down, h1, target, wf)


def swiglu_bwd(dh2_b, w_down, gt, up, act, u2):
    tn = 256

    def body(d_hbm, u_hbm, w_ref, gt_ref, up_ref, act_ref, dg_ref, du_ref, gd_ref, gg_ref, gu_ref, d_ref, u_ref, d_sems, u_sems):
        chunks = list(range(0, T, RC))
        d_ready = _arriving(d_hbm, d_ref, d_sems, chunks, RC)
        u_ready = _arriving(u_hbm, u_ref, u_sems, chunks, RC)

        def rows(r0):
            part = pl.ds(r0, RC)
            d_ready(r0 // RC)
            dact = lax.dot_general(d_ref[part, :], w_ref[...], NT_DIMS, preferred_element_type=F32)
            gt_ = gt_ref[part, :].astype(F32)
            up_ = up_ref[part, :].astype(F32)
            sg = _sigmoid(gt_)
            dg_ref[part, :] = (dact * up_ * (sg * (1.0 + gt_ * (1.0 - sg)))).astype(BF)
            du_ref[part, :] = (dact * (gt_ * sg)).astype(BF)

        _col_tiles(T, RC, rows)
        for k in range(len(chunks)):
            u_ready(k)
        gd_ref[...] = lax.dot_general(act_ref[...], d_ref[...], TN_DIMS, preferred_element_type=F32).astype(BF)
        gg_ref[...] = lax.dot_general(dg_ref[...], u_ref[...], TN_DIMS, preferred_element_type=F32).astype(BF)
        gu_ref[...] = lax.dot_general(du_ref[...], u_ref[...], TN_DIMS, preferred_element_type=F32).astype(BF)

    cols = _spec((T, tn), lambda j: (0, j))
    wrow = _spec((tn, D), lambda j: (j, 0))
    big = jax.ShapeDtypeStruct((T, D_FF), BF)
    grad = jax.ShapeDtypeStruct((D_FF, D), BF)
    return pl.pallas_call(
        body, grid=(D_FF // tn,), in_specs=[ANY_SPEC, ANY_SPEC, wrow, cols, cols, cols],
        out_specs=[cols, cols, wrow, wrow, wrow], out_shape=[big, big, grad, grad, grad],
        scratch_shapes=[pltpu.VMEM((T, D), BF)] * 2 + [pltpu.SemaphoreType.DMA((T // RC,))] * 2,
        compiler_params=_params(), name="swiglu_bwd")(dh2_b, u2, w_down, gt, up, act)


def gate_up_bwd(dgt, dup, w_gate, w_up, h1, wn, dh2, after):
    def body(dg_ref, du_ref, wg_hbm, wu_hbm, h_ref, wn_ref, r_ref, _after, d_ref, db_ref, dw_ref, du_scr, wg_ref, wu_ref, wg_sems, wu_sems):
        gate_ready = _arriving(wg_hbm, wg_ref, wg_sems, FF_TILE_ROWS, FF_TILE)
        up_ready = _arriving(wu_hbm, wu_ref, wu_sems, FF_TILE_ROWS, FF_TILE)
        _zero_at_first(dw_ref)

        du_scr[...] = jnp.zeros_like(du_scr)

        def tile(c0):
            k = pl.ds(c0, FF_TILE)
            gate_ready(c0 // FF_TILE)
            up_ready(c0 // FF_TILE)
            du_scr[...] += (jnp.dot(dg_ref[:, k], wg_ref[k, :], preferred_element_type=F32)
                            + jnp.dot(du_ref[:, k], wu_ref[k, :], preferred_element_type=F32))

        _col_tiles(D_FF, FF_TILE, tile)
        for r in (0, HALF):
            dh, dw = _rms_bwd(du_scr[r:r + HALF, :], h_ref[r:r + HALF, :], wn_ref[...])
            dw_ref[...] += jnp.sum(dw, axis=0, keepdims=True)
            dh = dh + r_ref[r:r + HALF, :]
            d_ref[r:r + HALF, :] = dh
            db_ref[r:r + HALF, :] = dh.astype(BF)

    return pl.pallas_call(
        body, grid=(T // RC,),
        in_specs=[_rows_spec(D_FF), _rows_spec(D_FF), ANY_SPEC, ANY_SPEC, _rows_spec(D), _vec(D), _rows_spec(D), ANY_SPEC],
        out_specs=[_rows_spec(D), _rows_spec(D), _vec(D)],
        out_shape=[jax.ShapeDtypeStruct((T, D), F32), jax.ShapeDtypeStruct((T, D), BF), jax.ShapeDtypeStruct((1, D), F32)],
        scratch_shapes=[pltpu.VMEM((RC, D), F32)] + [pltpu.VMEM((D_FF, D), BF)] * 2 + [pltpu.SemaphoreType.DMA((len(FF_TILE_ROWS),))] * 2,
        compiler_params=_params(), name="gate_up_bwd")(dgt, dup, w_gate, w_up, h1, wn, dh2, after)


def _adamw(w, g, m, v):
    m = ADAM_B1 * m + (1.0 - ADAM_B1) * g
    v = ADAM_B2 * v + (1.0 - ADAM_B2) * (g * g)
    m_hat = m / (1.0 - ADAM_B1 ** ADAM_STEP)
    v_hat = v / (1.0 - ADAM_B2 ** ADAM_STEP)
    delta = -ADAM_LR * (m_hat / (jnp.sqrt(v_hat) + ADAM_EPS) + ADAM_WD * w)
    return delta, m, v


def adamw_shards(name, recvs, ws, ms, vs):
    n = len(ws)

    def body(*refs):
        ins, outs = refs[:4 * n], refs[4 * n:]
        for k in range(n):
            p_ref, w_ref, m_ref, v_ref = ins[k], ins[n + k], ins[2 * n + k], ins[3 * n + k]
            g = p_ref[0].astype(F32)
            for s in range(1, 8):
                g = g + p_ref[s].astype(F32)
            outs[4 * k][...] = g
            outs[4 * k + 1][...], outs[4 * k + 2][...], outs[4 * k + 3][...] = _adamw(w_ref[...], g, m_ref[...], v_ref[...])

    tiles = [_spec((w.shape[0] // 2, w.shape[1]), lambda i: (i, 0)) for w in ws]
    recv_tiles = [_spec((8, w.shape[0] // 2, w.shape[1]), lambda i: (0, i, 0)) for w in ws]
    res = pl.pallas_call(
        body, grid=(2,), in_specs=recv_tiles + tiles * 3,
        out_specs=[t for t in tiles for _ in range(4)],
        out_shape=[jax.ShapeDtypeStruct(w.shape, F32) for w in ws for _ in range(4)],
        compiler_params=_params(), name=name)(*recvs, *ws, *ms, *vs)
    return [list(res[4 * k:4 * k + 4]) for k in range(n)]


def adamw_w_in(recv, w, m, v, after):
    rows = 34
    per_row = D // 128

    def body(p_ref, w_ref, m_ref, v_ref, _after, g_ref, d_ref, mo_ref, vo_ref):
        def chunk(c, carry):
            lines = pl.ds(pl.multiple_of(c * per_row * rows, 16), per_row * rows)
            g = p_ref[0, lines, :].astype(F32)
            for s in range(1, 8):
                g = g + p_ref[s, lines, :].astype(F32)
            g = g.reshape(rows, per_row, 128)
            part = pl.ds(c * rows, rows)
            g_ref[part] = g
            d_ref[part], mo_ref[part], vo_ref[part] = _adamw(w_ref[part], g, m_ref[part], v_ref[part])
            return carry

        lax.fori_loop(0, w.shape[0] // rows, chunk, 0)

    shape = jax.ShapeDtypeStruct(w.shape, F32)
    whole = pl.BlockSpec(memory_space=pltpu.VMEM)
    return pl.pallas_call(body, out_shape=[shape] * 4, in_specs=[whole] * 4 + [ANY_SPEC], compiler_params=_params(0),
                          name="adamw_w_in")(recv, w, m, v, after)


def sum_slabs(recv):
    def body(p_ref, o_ref):
        g = p_ref[0]
        for s in range(1, 8):
            g = g + p_ref[s]
        for s in range(8):
            o_ref[s] = g

    return pl.pallas_call(body, out_shape=jax.ShapeDtypeStruct(recv.shape, F32), compiler_params=_params(0), name="sum_slabs")(recv)


SIMPLE = [("norm1_w", 1024), ("ssd_conv_b", 1536), ("ssd_dt_bias", 16), ("ssd_a_log", 16), ("ssd_d", 16), ("ssd_norm_w", 1024),
          ("lru_conv_b", 1024), ("lru_ba", 1024), ("lru_bx", 1024), ("lru_lambda", 1024), ("lru_norm_w", 1024), ("norm2_w", 1024),
          ("final_norm_w", 1024)]
SPECIAL = ["lru_wa", "lru_wx", "meta_tokens", "ssd_conv_w", "lru_conv_w"]
SM_ROWS = 176
SM_WA, SM_WX, SM_META, SM_SCW, SM_LCW, SM_LOSS = 14, 78, 142, 158, 166, 170


def _simple_rows():
    rows, r = {}, 0
    for name, n in SIMPLE:
        rows[name] = r
        r += -(-n // 1024)
    return rows


def adamw_small(sm, special_g, ws, ms, vs):
    rows = _simple_rows()
    ns, nx = len(SIMPLE), len(SPECIAL)

    def body(*refs):
        sm_ref = refs[0]
        gx = refs[1:1 + nx]
        wr = refs[1 + nx:1 + nx + ns + nx]
        mr = refs[1 + nx + ns + nx:1 + nx + 2 * (ns + nx)]
        vr = refs[1 + nx + 2 * (ns + nx):1 + nx + 3 * (ns + nx)]
        outs = refs[1 + nx + 3 * (ns + nx):]
        o = 0
        for k, (name, n) in enumerate(SIMPLE):
            r0 = rows[name]
            for c0 in range(0, n, 1024):
                wd = min(1024, n - c0)
                g = sm_ref[r0 + c0 // 1024:r0 + c0 // 1024 + 1, 0:wd]
                sl = (slice(None), slice(c0, c0 + wd))
                d, m2, v2 = _adamw(wr[k][sl], g, mr[k][sl], vr[k][sl])
                outs[o][sl] = g
                outs[o + 1][sl] = d
                outs[o + 2][sl] = m2
                outs[o + 3][sl] = v2
            o += 4
        for k in range(nx):
            d, m2, v2 = _adamw(wr[ns + k][...], gx[k][...], mr[ns + k][...], vr[ns + k][...])
            outs[o][...] = d
            outs[o + 1][...] = m2
            outs[o + 2][...] = v2
            o += 3

    out_shape = []
    for k in range(ns):
        out_shape += [jax.ShapeDtypeStruct(ws[k].shape, F32)] * 4
    for k in range(nx):
        out_shape += [jax.ShapeDtypeStruct(ws[ns + k].shape, F32)] * 3
    return pl.pallas_call(body, out_shape=out_shape, compiler_params=_params(0), name="adamw_small")(sm, *special_g, *ws, *ms, *vs)


def _place():
    return lax.axis_index("x"), lax.axis_index("y"), lax.axis_index("c")


def _index(px, py, pc):
    return 4 * px + 2 * py + pc


def all_gather(name, shards):
    n = len(shards)
    hbm = pl.BlockSpec(memory_space=pl.ANY)

    def body(*refs):
        ins, outs = refs[:n], refs[n:2 * n]
        send_sems, recv_sems, local_sems = refs[2 * n:]
        x, y, c = _place()
        me, sibling = (x, y, c), (x, y, 1 - c)
        chips = [(1 - x, y), (x, 1 - y), (1 - x, 1 - y)]

        def copy(i, k, block, to, src=None):
            dst = outs[i].at[_index(*block)]
            return pltpu.make_async_remote_copy(src_ref=dst if src is None else src, dst_ref=dst, send_sem=send_sems.at[7 * i + k],
                                                recv_sem=recv_sems.at[7 * i + k], device_id=to, device_id_type=MESH)

        mine = [pltpu.make_async_copy(ins[i], outs[i].at[_index(*me)], local_sems.at[i]) for i in range(n)]
        for cp in mine:
            cp.start()
        first = []
        for i in range(n):
            first += [copy(i, 1 + j, me, (*chip, c), src=ins[i]) for j, chip in enumerate(chips)]
            first.append(copy(i, 0, me, sibling, src=ins[i]))
        for cp in first:
            cp.start()
        passed = []
        for i in range(n):
            for j, chip in enumerate(chips):
                copy(i, 1 + j, (*chip, c), me).wait_recv()
                cp = copy(i, 4 + j, (*chip, c), sibling)
                cp.start()
                passed.append(cp)
        for i in range(n):
            copy(i, 0, sibling, me).wait_recv()
            for j, chip in enumerate(chips):
                copy(i, 4 + j, (*chip, 1 - c), me).wait_recv()
        for cp in first + passed:
            cp.wait_send()
        for cp in mine:
            cp.wait()

    return pl.pallas_call(
        body, in_specs=[hbm] * n, out_specs=[hbm] * n,
        out_shape=[jax.ShapeDtypeStruct((8,) + s.shape, s.dtype) for s in shards],
        scratch_shapes=[pltpu.SemaphoreType.DMA((7 * n,)), pltpu.SemaphoreType.DMA((7 * n,)), pltpu.SemaphoreType.DMA((n,))],
        name=name)(*shards)


HBM_SPEC = pl.BlockSpec(memory_space=pltpu.HBM)
SEM_SPEC = pl.BlockSpec(memory_space=pltpu.SEMAPHORE)
EFFECT = pltpu.SideEffectType.DATAFLOW_SIDE_EFFECTING


def _peers(x, y, c):
    return [((1 - x) if k & 4 else x, (1 - y) if k & 2 else y, (1 - c) if k & 1 else c) for k in range(1, 8)]


def _pieces(rows):
    for n in (4, 2):
        if rows % (16 * n) == 0:
            return [(r * (rows // n), rows // n) for r in range(n)]
    return [(0, rows)]


def _peer_copies(src, land, send_sems, recv_sems, k, peer, mine):
    block = src.at[_index(*peer)]
    return [pltpu.make_async_remote_copy(src_ref=block.at[pl.ds(r0, nr)], dst_ref=land.at[mine, pl.ds(r0, nr)], send_sem=send_sems.at[k],
                                         recv_sem=recv_sems.at[k], device_id=peer, device_id_type=MESH)
            for r0, nr in _pieces(block.shape[0])]


OWN = 7


def _own_copy(src, land, send_sems, mine):
    return pltpu.make_async_copy(src.at[mine], land.at[mine], send_sems.at[OWN])


def copies_start(name, srcs):
    n = len(srcs)

    def body(*refs):
        ins, lands = refs[:n], refs[n:2 * n]
        sends, recvs = refs[2 * n:3 * n], refs[3 * n:4 * n]
        token = refs[-1]
        x, y, c = _place()
        mine = _index(x, y, c)
        for i in range(n):
            per_peer = [_peer_copies(ins[i], lands[i], sends[i], recvs[i], k, peer, mine) for k, peer in enumerate(_peers(x, y, c))]
            for piece in zip(*per_peer):
                for cp in piece:
                    cp.start()
            _own_copy(ins[i], lands[i], sends[i], mine).start()
        token[...] = jnp.zeros_like(token)

    res = pl.pallas_call(
        body, name=name,
        out_shape=([pltpu.SemaphoreType.DMA((8,))] * n + [pltpu.SemaphoreType.DMA((7,))] * n + [pltpu.HBM(s.shape, s.dtype) for s in srcs] * 2
                   + [jax.ShapeDtypeStruct((8, 128), F32)]),
        in_specs=[HBM_SPEC] * (2 * n),
        out_specs=[SEM_SPEC] * (2 * n) + [HBM_SPEC] * (2 * n) + [pl.BlockSpec(memory_space=pltpu.VMEM)],
        input_output_aliases={i: 2 * n + i for i in range(2 * n)},
        compiler_params=pltpu.CompilerParams(has_side_effects=EFFECT),
    )(*[pltpu.with_memory_space_constraint(s, pltpu.HBM) for s in srcs],
      *[pltpu.with_memory_space_constraint(lax.empty(s.shape, s.dtype), pltpu.HBM) for s in srcs])
    return [(res[i], res[n + i], res[2 * n + i], res[3 * n + i]) for i in range(n)], res[-1]


def copies_wait(name, started, after):
    n = len(started)

    def body(*refs):
        ins, lands = refs[:n], refs[n:2 * n]
        sends, recvs = refs[2 * n:3 * n], refs[3 * n:4 * n]
        x, y, c = _place()
        mine = _index(x, y, c)
        for i in range(n):
            for k, peer in enumerate(_peers(x, y, c)):
                arrival = pltpu.make_async_remote_copy(src_ref=ins[i].at[mine], dst_ref=lands[i].at[_index(*peer)],
                                                       send_sem=sends[i].at[k], recv_sem=recvs[i].at[k], device_id=peer, device_id_type=MESH)
                arrival.wait_send()
                arrival.wait_recv()
            _own_copy(ins[i], lands[i], sends[i], mine).wait()

    srcs = [s[2] for s in started]
    lands = [s[3] for s in started]
    res = pl.pallas_call(
        body, name=name,
        out_shape=[pltpu.HBM(s.shape, s.dtype) for s in srcs] + [pltpu.HBM(z.shape, z.dtype) for z in lands],
        in_specs=[HBM_SPEC] * (2 * n) + [SEM_SPEC] * (2 * n) + [ANY_SPEC] * len(after),
        out_specs=[HBM_SPEC] * (2 * n),
        input_output_aliases={i: i for i in range(2 * n)},
        compiler_params=pltpu.CompilerParams(has_side_effects=EFFECT),
    )(*srcs, *lands, *[s[0] for s in started], *[s[1] for s in started], *after)
    return list(res[n:])


def _hop(src, land, send_sems, recv_sems, k, block, to):
    dst = land.at[_index(*block)]
    return pltpu.make_async_remote_copy(src_ref=dst if src is None else src, dst_ref=dst, send_sem=send_sems.at[k], recv_sem=recv_sems.at[k],
                                        device_id=to, device_id_type=MESH)


def _own_block(src, land, send_sems, mine):
    return pltpu.make_async_copy(src, land.at[mine], send_sems.at[4])


def _other_chips(x, y):
    return [(1 - x, y), (x, 1 - y), (1 - x, 1 - y)]


def gather_start(name, shards, through):
    n, m = len(shards), len(through)

    def body(*refs):
        ins, lands = refs[:n], refs[n:2 * n]
        sends, recvs = refs[2 * n + m:3 * n + m], refs[3 * n + m:4 * n + m]
        x, y, c = _place()
        for i in range(n):
            for j, chip in enumerate(_other_chips(x, y)):
                _hop(ins[i], lands[i], sends[i], recvs[i], 1 + j, (x, y, c), (*chip, c)).start()
            _hop(ins[i], lands[i], sends[i], recvs[i], 0, (x, y, c), (x, y, 1 - c)).start()
            _own_block(ins[i], lands[i], sends[i], _index(x, y, c)).start()

    own, passing = pltpu.SemaphoreType.DMA((5,)), pltpu.SemaphoreType.DMA((3,))
    zones = [jax.ShapeDtypeStruct((8,) + s.shape, s.dtype) for s in shards]
    res = pl.pallas_call(
        body, name=name,
        out_shape=([own] * (2 * n) + [passing] * (2 * n) + [pltpu.HBM(s.shape, s.dtype) for s in shards]
                   + [pltpu.HBM(z.shape, z.dtype) for z in zones] + [pltpu.HBM(t.shape, t.dtype) for t in through]),
        in_specs=[HBM_SPEC] * (2 * n + m),
        out_specs=[SEM_SPEC] * (4 * n) + [HBM_SPEC] * (2 * n + m),
        input_output_aliases={i: 4 * n + i for i in range(2 * n + m)},
        compiler_params=pltpu.CompilerParams(has_side_effects=EFFECT),
    )(*[pltpu.with_memory_space_constraint(s, pltpu.HBM) for s in shards],
      *[pltpu.with_memory_space_constraint(lax.empty(z.shape, z.dtype), pltpu.HBM) for z in zones],
      *[pltpu.with_memory_space_constraint(t, pltpu.HBM) for t in through])
    return [[res[4 * n + i], res[5 * n + i], res[i], res[n + i], res[2 * n + i], res[3 * n + i]] for i in range(n)], list(res[6 * n:])


def gather_stage(name, pass_on, finish, after):
    arrays = pass_on + finish
    n = len(arrays)

    def body(*refs):
        ins, lands = refs[:n], refs[n:2 * n]
        sems = [refs[(2 + q) * n:(3 + q) * n] for q in range(4)]
        x, y, c = _place()
        me, sibling = (x, y, c), (x, y, 1 - c)
        for i in range(len(pass_on)):
            send, recv, send_on, recv_on = (q[i] for q in sems)
            for j, chip in enumerate(_other_chips(x, y)):
                _hop(None, lands[i], send, recv, 1 + j, (*chip, c), me).wait_recv()
                _hop(None, lands[i], send_on, recv_on, j, (*chip, c), sibling).start()
        for i in range(len(pass_on), n):
            send, recv, send_on, recv_on = (q[i] for q in sems)
            _hop(ins[i], lands[i], send, recv, 0, sibling, me).wait_recv()
            for j, chip in enumerate(_other_chips(x, y)):
                _hop(None, lands[i], send_on, recv_on, j, (*chip, 1 - c), me).wait_recv()
            _hop(ins[i], lands[i], send, recv, 0, me, sibling).wait_send()
            _own_block(ins[i], lands[i], send, _index(*me)).wait()
            for j, chip in enumerate(_other_chips(x, y)):
                _hop(ins[i], lands[i], send, recv, 1 + j, me, (*chip, c)).wait_send()
                _hop(None, lands[i], send_on, recv_on, j, (*chip, c), sibling).wait_send()
        refs[-1][...] = jnp.zeros_like(refs[-1])

    res = pl.pallas_call(
        body, name=name,
        out_shape=([pltpu.HBM(a[0].shape, a[0].dtype) for a in arrays] + [pltpu.HBM(a[1].shape, a[1].dtype) for a in arrays]
                   + [jax.ShapeDtypeStruct((8, 128), F32)]),
        in_specs=[HBM_SPEC] * (2 * n) + [SEM_SPEC] * (4 * n) + [ANY_SPEC],
        out_specs=[HBM_SPEC] * (2 * n) + [pl.BlockSpec(memory_space=pltpu.VMEM)],
        input_output_aliases={i: i for i in range(2 * n)},
        compiler_params=pltpu.CompilerParams(has_side_effects=EFFECT),
    )(*[a[0] for a in arrays], *[a[1] for a in arrays], *[a[2 + q] for q in range(4) for a in arrays], after)
    for i, a in enumerate(arrays):
        a[0], a[1] = res[i], res[n + i]
    return [a[1] for a in finish], res[-1]


WEIGHTS = ["meta_tokens", "norm1_w", "w_in", "ssd_conv_w", "ssd_conv_b", "ssd_dt_bias", "ssd_a_log", "ssd_d", "ssd_norm_w", "lru_conv_w",
           "lru_conv_b", "lru_wa", "lru_ba", "lru_wx", "lru_bx", "lru_lambda", "lru_norm_w", "w_out", "norm2_w", "w_gate", "w_up", "w_down",
           "final_norm_w"]
BIG = ["w_in", "w_out", "w_gate", "w_up", "w_down"]
COLUMN_SHARDED = ["w_in", "w_gate", "w_up"]


def _pair_blocks(w):
    w = w.reshape(8, 2, 64, 64)
    z = jnp.zeros((8, 64, 64), w.dtype)
    return jnp.concatenate([jnp.concatenate([w[:, 0], z], axis=2), jnp.concatenate([z, w[:, 1]], axis=2)], axis=1)


def _unpair_blocks(w2):
    return jnp.stack([w2[:, :64, :64], w2[:, 64:, 64:]], axis=1).reshape(16, 64, 64)


def _per_group(v):
    return jnp.pad(v.reshape(2, 1, 8), ((0, 0), (0, 0), (0, 120)))


def _pad_cols(v, n):
    return jnp.pad(v, ((0, 0), (0, n - v.shape[1])))


def local_step(x, target, meta, ssd_cw, lru_cw, w_in_shards, fetch, send, p):
    bias2, alog2, d2 = _per_group(p["ssd_dt_bias"]), _per_group(p["ssd_a_log"]), _per_group(p["ssd_d"])
    wa2 = _pair_blocks(p["lru_wa"]).astype(BF)
    wx2 = _pair_blocks(p["lru_wx"]).astype(BF)
    lru = (lru_cw, p["lru_conv_b"], wa2, p["lru_ba"], wx2, p["lru_bx"], p["lru_lambda"])

    proj, dt_raw, u1, h0, w_in, w_dt = in_proj(x, meta, p["norm1_w"], w_in_shards)
    yn_ssd, y_pre, h_prev = ssd_fwd(proj, dt_raw, ssd_cw, p["ssd_conv_b"], bias2, alog2, d2, p["ssd_norm_w"])
    _, moved = fetch([], yn_ssd)
    hseq, a = lru_fwd(proj, *lru, moved)
    (w_out,), _ = fetch(["w_out"], hseq)
    h1, cat = out_proj(yn_ssd, proj, hseq, p["lru_norm_w"], w_out, h0)
    (w_gate, w_up), _ = fetch(["w_gate", "w_up"], h1)
    gt, up, act, u2 = gate_up(h1, p["norm2_w"], w_gate, w_up)
    (w_down,), _ = fetch(["w_down"], act)
    dh2, dh2_b, loss, d_fnw = down_loss(act, w_down, h1, target, p["final_norm_w"])

    dgt, dup, g_down, g_gate, g_up = swiglu_bwd(dh2_b, w_down, gt, up, act, u2)
    sent = send({"w_down": g_down, "w_gate": g_gate, "w_up": g_up})
    dh1, dh1_b, d_n2 = gate_up_bwd(dgt, dup, w_gate, w_up, h1, p["norm2_w"], dh2, sent)
    sent = send({"w_out": weight_grad("dw_out", cat, dh1_b)})
    dyn, dh_out, dg_b, d_lnw = out_proj_bwd(dh1_b, w_out, proj, hseq, p["lru_norm_w"], sent)

    dxl_b, d_lcw, d_lcb, dwa2, d_ba, dwx2, d_bx, d_lam = lru_bwd(dh_out, a, hseq, proj, *lru)
    dz_b, dxbc_b, ddt_b, dpar, d_snw, d_scw, d_scb = ssd_bwd(dyn, proj, dt_raw, ssd_cw, p["ssd_conv_b"], y_pre, h_prev, bias2, alog2, d2,
                                                             p["ssd_norm_w"], sent)
    sent = send({"w_in": in_weight_grad([dz_b, dxbc_b, dg_b, dxl_b], [0, SSD_W, 2576, 2576 + LRU_W], ddt_b, u1)})
    grad_x, d_meta, d_n1 = in_proj_bwd(dz_b, dg_b, dxl_b, dxbc_b, ddt_b, w_in, w_dt, h0, p["norm1_w"], dh1, sent)
    small = {"norm1_w": d_n1, "ssd_conv_b": d_scb, "ssd_dt_bias": dpar[:, 0, :8].reshape(1, 16), "ssd_a_log": dpar[:, 1, :8].reshape(1, 16),
             "ssd_d": dpar[:, 2, :8].reshape(1, 16), "ssd_norm_w": d_snw, "lru_conv_b": d_lcb, "lru_ba": d_ba, "lru_bx": d_bx,
             "lru_lambda": d_lam, "lru_norm_w": d_lnw, "norm2_w": d_n2, "final_norm_w": d_fnw,
             "lru_wa": _unpair_blocks(dwa2), "lru_wx": _unpair_blocks(dwx2), "meta_tokens": d_meta,
             "ssd_conv_w": d_scw, "lru_conv_w": d_lcw}
    return loss, grad_x, small


def _pack_small(small, loss):
    rows = [_pad_cols(small[name], -(-n // 1024) * 1024).reshape(-1, 1024) for name, n in SIMPLE]
    rows += [small["lru_wa"].reshape(64, 1024), small["lru_wx"].reshape(64, 1024), small["meta_tokens"],
             _pad_cols(small["ssd_conv_w"], 2048).reshape(8, 1024), small["lru_conv_w"], _pad_cols(loss[:, 0:1], 1024)]
    sm = jnp.concatenate(rows, axis=0)
    return jnp.pad(sm, ((0, SM_ROWS - sm.shape[0]), (0, 0)))


def _slabs(g):
    return g.reshape(8, g.shape[0] // 8, g.shape[1])


def _unslab(g):
    return g.reshape(8 * g.shape[1], g.shape[2])


def kernel(x, meta_tokens, norm1_w, w_in, ssd_conv_w, ssd_conv_b, ssd_dt_bias, ssd_a_log, ssd_d, ssd_norm_w, lru_conv_w, lru_conv_b, lru_wa, lru_ba, lru_wx, lru_bx, lru_lambda, lru_norm_w, w_out, norm2_w, w_gate, w_up, w_down, final_norm_w, loss_target, m_meta_tokens, m_norm1_w, m_w_in, m_ssd_conv_w, m_ssd_conv_b, m_ssd_dt_bias, m_ssd_a_log, m_ssd_d, m_ssd_norm_w, m_lru_conv_w, m_lru_conv_b, m_lru_wa, m_lru_ba, m_lru_wx, m_lru_bx, m_lru_lambda, m_lru_norm_w, m_w_out, m_norm2_w, m_w_gate, m_w_up, m_w_down, m_final_norm_w, v_meta_tokens, v_norm1_w, v_w_in, v_ssd_conv_w, v_ssd_conv_b, v_ssd_dt_bias, v_ssd_a_log, v_ssd_d, v_ssd_norm_w, v_lru_conv_w, v_lru_conv_b, v_lru_wa, v_lru_ba, v_lru_wx, v_lru_bx, v_lru_lambda, v_lru_norm_w, v_w_out, v_norm2_w, v_w_gate, v_w_up, v_w_down, v_final_norm_w):
    w = dict(meta_tokens=meta_tokens, norm1_w=norm1_w, w_in=w_in[0], ssd_conv_w=ssd_conv_w[0], ssd_conv_b=ssd_conv_b, ssd_dt_bias=ssd_dt_bias,
             ssd_a_log=ssd_a_log, ssd_d=ssd_d, ssd_norm_w=ssd_norm_w, lru_conv_w=lru_conv_w[0], lru_conv_b=lru_conv_b, lru_wa=lru_wa[0],
             lru_ba=lru_ba, lru_wx=lru_wx[0], lru_bx=lru_bx, lru_lambda=lru_lambda, lru_norm_w=lru_norm_w, w_out=w_out[0], norm2_w=norm2_w,
             w_gate=w_gate[0], w_up=w_up[0], w_down=w_down[0], final_norm_w=final_norm_w.reshape(1, D))
    m = dict(meta_tokens=m_meta_tokens, norm1_w=m_norm1_w, w_in=m_w_in[0], ssd_conv_w=m_ssd_conv_w[0], ssd_conv_b=m_ssd_conv_b,
             ssd_dt_bias=m_ssd_dt_bias, ssd_a_log=m_ssd_a_log, ssd_d=m_ssd_d, ssd_norm_w=m_ssd_norm_w, lru_conv_w=m_lru_conv_w[0],
             lru_conv_b=m_lru_conv_b, lru_wa=m_lru_wa[0], lru_ba=m_lru_ba, lru_wx=m_lru_wx[0], lru_bx=m_lru_bx, lru_lambda=m_lru_lambda,
             lru_norm_w=m_lru_norm_w, w_out=m_w_out[0], norm2_w=m_norm2_w, w_gate=m_w_gate[0], w_up=m_w_up[0], w_down=m_w_down[0],
             final_norm_w=m_final_norm_w.reshape(1, D))
    v = dict(meta_tokens=v_meta_tokens, norm1_w=v_norm1_w, w_in=v_w_in[0], ssd_conv_w=v_ssd_conv_w[0], ssd_conv_b=v_ssd_conv_b,
             ssd_dt_bias=v_ssd_dt_bias, ssd_a_log=v_ssd_a_log, ssd_d=v_ssd_d, ssd_norm_w=v_ssd_norm_w, lru_conv_w=v_lru_conv_w[0],
             lru_conv_b=v_lru_conv_b, lru_wa=v_lru_wa[0], lru_ba=v_lru_ba, lru_wx=v_lru_wx[0], lru_bx=v_lru_bx, lru_lambda=v_lru_lambda,
             lru_norm_w=v_lru_norm_w, w_out=v_w_out[0], norm2_w=v_norm2_w, w_gate=v_w_gate[0], w_up=v_w_up[0], w_down=v_w_down[0],
             final_norm_w=v_final_norm_w.reshape(1, D))
    shapes = dict(meta_tokens=meta_tokens.shape, norm1_w=norm1_w.shape, w_in=w_in.shape, ssd_conv_w=ssd_conv_w.shape,
                  ssd_conv_b=ssd_conv_b.shape, ssd_dt_bias=ssd_dt_bias.shape, ssd_a_log=ssd_a_log.shape, ssd_d=ssd_d.shape,
                  ssd_norm_w=ssd_norm_w.shape, lru_conv_w=lru_conv_w.shape, lru_conv_b=lru_conv_b.shape, lru_wa=lru_wa.shape,
                  lru_ba=lru_ba.shape, lru_wx=lru_wx.shape, lru_bx=lru_bx.shape, lru_lambda=lru_lambda.shape, lru_norm_w=lru_norm_w.shape,
                  w_out=w_out.shape, norm2_w=norm2_w.shape, w_gate=w_gate.shape, w_up=w_up.shape, w_down=w_down.shape,
                  final_norm_w=final_norm_w.shape)
    me = _index(*_place())
    for n in COLUMN_SHARDED:
        w[n], m[n], v[n] = w[n].T, m[n].T, v[n].T

    small_shard = jnp.concatenate([w["meta_tokens"], _pad_cols(w["ssd_conv_w"], 256).reshape(8, 128), w["lru_conv_w"],
                                   jnp.zeros((4, 128), F32)], axis=0)
    g_in, gs = all_gather("gather_w_in", [w["w_in"].astype(BF), small_shard])
    later = ["w_out", "w_gate", "w_up", "w_down"]
    started, (g_in, gs) = gather_start("gather_rest_start", [w[n].astype(BF) for n in later], [g_in, gs])
    started = dict(zip(later, started))
    meta_full = gs[:, 0:16].transpose(1, 0, 2).reshape(N_META, D)
    ssd_cw = gs[:, 16:24].reshape(8, 4, 256)[:, :, :192].transpose(1, 0, 2).reshape(4, XBC)
    lru_cw = gs[:, 24:28].transpose(1, 0, 2).reshape(4, LRU_W)

    def fetch(names, after):
        pass_on = {"w_out": ["w_down"], "w_gate": [], "w_down": []}[names[0]] if names else ["w_out", "w_gate", "w_up"]
        got, zero = gather_stage("gather_" + (names[0] + "_wait" if names else "pass_on"), [started[n] for n in pass_on],
                                 [started[n] for n in names], after)
        return [_unslab(g) for g in got], zero

    in_flight = {}

    def send(grads):
        names = list(grads)
        st, zero = copies_start("grads_" + names[0] + "_start", [grads[n] if n == "small" else _slabs(grads[n]) for n in names])
        in_flight.update(zip(names, st))
        return zero

    loss, grad_x, small = local_step(x[0], loss_target[0], meta_full, ssd_cw, lru_cw, g_in, fetch, send, w)
    send({"small": _pack_small(small, loss).reshape(8, SM_ROWS // 8, 1024)})

    out = {}
    early = ["w_down", "w_gate", "w_up", "w_out"]
    recv = dict(zip(early, copies_wait("grads_early_wait", [in_flight[n] for n in early], [in_flight["small"][2]])))
    for pair in (early[:2], early[2:]):
        done = adamw_shards("adamw_" + pair[0], [recv[n] for n in pair], [w[n] for n in pair], [m[n] for n in pair], [v[n] for n in pair])
        out.update(zip(pair, done))
    recv_in, recv_small = copies_wait("grads_late_wait", [in_flight["w_in"], in_flight["small"]], [out[n][0] for n in early])
    def lines(a):
        return jnp.transpose(a.reshape(D // 128, 128, IN_COLS // 8), (2, 0, 1))

    gathering, zero = copies_start("gather_small_start", [sum_slabs(recv_small)])
    out["w_in"] = [jnp.transpose(o, (1, 2, 0)).reshape(D, IN_COLS // 8)
                   for o in adamw_w_in(recv_in, lines(w_in), lines(m_w_in), lines(v_w_in), zero)]
    for n in ("w_gate", "w_up"):
        out[n] = [o.T for o in out[n]]
    sm = copies_wait("gather_small_wait", gathering, [out["w_in"][0]])[0].reshape(SM_ROWS, 1024)
    special_g =[sm[SM_WA:SM_WA + 64].reshape(16, 64, 64), sm[SM_WX:SM_WX + 64].reshape(16, 64, 64),
                 lax.dynamic_slice(sm[SM_META:SM_META + 16], (0, 128 * me), (16, 128)),
                 lax.dynamic_slice(sm[SM_SCW:SM_SCW + 8].reshape(4, 2048), (0, 192 * me), (4, 192)),
                 lax.dynamic_slice(sm[SM_LCW:SM_LCW + 4], (0, 128 * me), (4, 128))]
    names = [n for n, _ in SIMPLE] + SPECIAL
    res = adamw_small(sm, special_g, [w[n] for n in names], [m[n] for n in names], [v[n] for n in names])
    for k, (n, _) in enumerate(SIMPLE):
        out[n] = res[4 * k:4 * k + 4]
    for k, n in enumerate(SPECIAL):
        o = 4 * len(SIMPLE) + 3 * k
        out[n] = [special_g[k]] + list(res[o:o + 3])
    loss_total = sm[SM_LOSS, 0]
    flat = [loss_total, grad_x[None]]
    for k in range(4):
        flat += [out[n][k].reshape(shapes[n]) for n in WEIGHTS]
    return tuple(flat)
```

```python
import math

import jax
import jax.numpy as jnp
from jax import lax
from jax.experimental import pallas as pl
from jax.experimental.pallas import tpu as pltpu

F32 = jnp.float32
BF = jnp.bfloat16

D = 1024
SEQ = 2048
N_META = 16
Q = 128
NPAD = 112
T = NPAD + N_META + SEQ
NCH = T // Q
RC = 544
D_FF = 2816
SSD_W = 1024
LRU_W = 1024
XBC = 1536
IN_COLS = 4624
PZ, PG, PXL, PXBC = 0, 1024, 2048, 3072
NP_IN = 4608
EPS = 1e-6
LRU_C = 8.0
VMEM_LIMIT = 56 * 1024 * 1024

ADAM_LR, ADAM_B1, ADAM_B2, ADAM_EPS, ADAM_WD, ADAM_STEP = 0.001, 0.9, 0.999, 1e-08, 0.01, 10

NT_DIMS = (((1,), (1,)), ((), ()))
TN_DIMS = (((0,), (0,)), ((), ()))
MESH = pl.DeviceIdType.MESH


def _params(n_grid=1, limit=VMEM_LIMIT):
    return pltpu.CompilerParams(dimension_semantics=("arbitrary",) * n_grid, vmem_limit_bytes=limit)


def _spec(shape, imap, single=False):
    if single:
        return pl.BlockSpec(shape, imap, pipeline_mode=pl.Buffered(1))
    return pl.BlockSpec(shape, imap)


def _sigmoid(x):
    return 0.5 * jnp.tanh(0.5 * x) + 0.5


def _sigmoid_gate(x):
    return 1.0 / (1.0 + jnp.exp(-x))


def _softplus(x):
    return jnp.maximum(x, 0.0) + jnp.log(1.0 + jnp.exp(-jnp.abs(x)))


def _rms_stats(h):
    return lax.rsqrt(jnp.mean(h * h, axis=-1, keepdims=True) + EPS)


def _rms(h, w):
    return (h * _rms_stats(h)) * w


def _rms_bwd(du, h, w):
    r = _rms_stats(h)
    n = h * r
    dn = du * w
    dh = r * (dn - n * jnp.mean(dn * n, axis=-1, keepdims=True))
    return dh, du * n


_G0 = math.sqrt(2.0 / math.pi)


def _gelu(x):
    return 0.5 * x * (1.0 + jnp.tanh(_G0 * (x + 0.044715 * (x * x * x))))


def _gelu_grad(x):
    t = jnp.tanh(_G0 * (x + 0.044715 * (x * x * x)))
    return 0.5 * (1.0 + t) + 0.5 * x * (1.0 - t * t) * (_G0 * (1.0 + 3.0 * 0.044715 * (x * x)))


def _rows(shape, r0=0):
    return lax.broadcasted_iota(jnp.int32, shape, 0) + r0


def _lanes(shape):
    return lax.broadcasted_iota(jnp.int32, shape, 1)


HALO = 8


def _fill_padded(pad_ref, x_ref):
    pad_ref[0:HALO, :] = jnp.zeros((HALO, pad_ref.shape[1]), F32)
    pad_ref[T + HALO:T + 2 * HALO, :] = jnp.zeros((HALO, pad_ref.shape[1]), F32)

    def step(c, carry):
        r0 = pl.multiple_of(c * Q, Q)
        pad_ref[pl.ds(r0 + HALO, Q), :] = x_ref[pl.ds(r0, Q), :].astype(F32)
        return carry

    lax.fori_loop(0, NCH, step, 0)


def _back(pad_ref, r0):
    win = pad_ref[pl.ds(r0, Q + HALO), :]
    return lambda s: win[HALO:, :] if s == 0 else pltpu.roll(win, s, axis=0)[HALO:, :]


def _ahead(pad_ref, r0):
    win = pad_ref[pl.ds(r0 + HALO, Q + HALO), :]
    return lambda s: win[:Q, :] if s == 0 else pltpu.roll(win, Q + HALO - s, axis=0)[:Q, :]


def _conv(back, w, b):
    y = b + w[3:4, :] * back(0)
    for k in range(3):
        y = y + w[k:k + 1, :] * back(3 - k)
    return y


def _conv_bwd_x(ahead, w):
    dx = w[3:4, :] * ahead(0)
    for k in range(3):
        dx = dx + w[k:k + 1, :] * ahead(3 - k)
    return dx


def _conv_bwd_w(dy, back):
    dws = [jnp.sum(dy * back(3 - k), axis=0, keepdims=True) for k in range(4)]
    return jnp.concatenate(dws, axis=0), jnp.sum(dy, axis=0, keepdims=True)


def _chunks(fn, unrolled=False):
    if unrolled:
        for c in range(NCH):
            fn(c * Q)
        return

    def step(c, carry):
        fn(pl.multiple_of(c * Q, Q))
        return carry

    lax.fori_loop(0, NCH, step, 0)


HALF = RC // 2


def _col_tiles(n, tn, fn):
    def step(j, carry):
        fn(pl.multiple_of(j * tn, tn))
        return carry

    lax.fori_loop(0, n // tn, step, 0)


def _rows_spec(cols, block_col=0):
    return _spec((RC, cols), lambda i: (i, block_col))


def _whole(shape):
    return _spec(shape, lambda i: tuple(0 for _ in shape), single=True)


def _vec(cols):
    return _spec((1, cols), lambda i: (0, 0))


def _zero_at_first(*refs):
    @pl.when(pl.program_id(0) == 0)
    def _():
        for r in refs:
            r[...] = jnp.zeros_like(r)


ANY_SPEC = pl.BlockSpec(memory_space=pl.ANY)


def _arriving(src, dst, sems, starts, rows):
    n, ahead = len(starts), 2
    first = pl.program_id(0) == 0

    def piece(k):
        r0 = starts[0]
        for j in range(1, n):
            r0 = jnp.where(k == j, starts[j], r0)
        at = pl.ds(pl.multiple_of(r0, 16), rows)
        return pltpu.make_async_copy(src.at[at], dst.at[at], sems.at[k])

    @pl.when(first)
    def _():
        for k in range(min(ahead, n)):
            piece(k).start()

    def ready(k):
        k = jnp.asarray(k, jnp.int32)

        @pl.when(first)
        def _():
            piece(k).wait()

            @pl.when(k + ahead < n)
            def _():
                piece(k + ahead).start()

    return ready


IN_RUNS = ((PZ, 0, 1024), (PXBC, 1024, XBC), (PG, 2576, 2048))
IN_TILE = 512
IN_TILE_ROWS = [wrow + IN_TILE * j for _, wrow, width in IN_RUNS for j in range(width // IN_TILE)]


def _in_tiles(fn):
    done = 0
    for pcol, wrow, width in IN_RUNS:
        def step(j, carry, pcol=pcol, wrow=wrow, done=done):
            fn(pl.multiple_of(pcol + j * IN_TILE, IN_TILE), pl.multiple_of(wrow + j * IN_TILE, 16), done + j)
            return carry

        lax.fori_loop(0, width // IN_TILE, step, 0)
        done += width // IN_TILE


def in_proj(x, meta, wn, w_shards):
    first = NPAD + N_META
    steps = T // RC
    shard = IN_COLS // 8

    def body(x_hbm, meta_ref, wn_ref, g_hbm, o_ref, dt_ref, u_ref, h_ref, wt_hbm, wdt_ref, raw, w_ref, h_scr, g_sems, h_sems, out_sem):
        i = pl.program_id(0)
        slot = i % 2
        shards = [pltpu.make_async_copy(g_hbm.at[j], raw.at[j], g_sems.at[j]) for j in range(8)]
        head = pltpu.make_async_copy(x_hbm.at[pl.ds(0, RC - first)], h_scr.at[0, pl.ds(first, RC - first)], h_sems.at[0])
        put_back = pltpu.make_async_copy(w_ref, wt_hbm, out_sem)

        def rows_of(step):
            return pltpu.make_async_copy(x_hbm.at[pl.ds(pl.multiple_of(step * RC - first, 32), RC)], h_scr.at[step % 2], h_sems.at[step % 2])

        @pl.when(i == 0)
        def _():
            for cp in shards:
                cp.start()
            head.start()
            h_scr[0, 0:NPAD, :] = jnp.zeros((NPAD, D), F32)
            for j in range(8):
                h_scr[0, NPAD:first, 128 * j:128 * j + 128] = meta_ref[j, 0:N_META, :]

        @pl.when(i + 1 < steps)
        def _():
            rows_of(i + 1).start()

        @pl.when(i == 0)
        def _():
            head.wait()

        @pl.when(i > 0)
        def _():
            rows_of(i).wait()

        h_ref[...] = h_scr[slot]
        for r in (0, HALF):
            u_ref[r:r + HALF, :] = _rms(h_scr[slot, r:r + HALF, :], wn_ref[...]).astype(BF)

        @pl.when(i == 0)
        def _():
            for j, cp in enumerate(shards):
                cp.wait()
                w_ref[shard * j:shard * (j + 1), :] = raw[j]
            put_back.start()
            wdt_ref[...] = jnp.zeros_like(wdt_ref)
            for g in range(2):
                wdt_ref[128 * g:128 * g + 8, :] = w_ref[2560 + 8 * g:2568 + 8 * g, :]

        def tile(pcol, wrow, k):
            o_ref[:, pl.ds(pcol, IN_TILE)] = lax.dot_general(u_ref[...], w_ref[pl.ds(wrow, IN_TILE), :], NT_DIMS,
                                                             preferred_element_type=F32).astype(BF)

        _in_tiles(tile)
        dt_ref[...] = lax.dot_general(u_ref[...], wdt_ref[...], NT_DIMS, preferred_element_type=F32)

        @pl.when(i == steps - 1)
        def _():
            put_back.wait()

    return pl.pallas_call(
        body, grid=(steps,), in_specs=[ANY_SPEC, _spec(meta.shape, lambda i: (0, 0, 0)), _vec(D), ANY_SPEC],
        out_specs=[_rows_spec(NP_IN), _rows_spec(256), _rows_spec(D), _rows_spec(D), ANY_SPEC, _spec((256, D), lambda i: (0, 0))],
        out_shape=[jax.ShapeDtypeStruct((T, NP_IN), BF), jax.ShapeDtypeStruct((T, 256), F32), jax.ShapeDtypeStruct((T, D), BF),
                   jax.ShapeDtypeStruct((T, D), F32), jax.ShapeDtypeStruct((IN_COLS, D), BF), jax.ShapeDtypeStruct((256, D), BF)],
        scratch_shapes=[pltpu.VMEM((8, shard, D), BF), pltpu.VMEM((IN_COLS, D), BF), pltpu.VMEM((2, RC, D), F32),
                        pltpu.SemaphoreType.DMA((8,)), pltpu.SemaphoreType.DMA((2,)), pltpu.SemaphoreType.DMA],
        compiler_params=_params(), name="in_proj")(x, meta, wn, w_shards)


def out_proj(yn_ssd, proj, hseq, lru_nw, w_out, h0):
    def body(y_ref, g_ref, h_ref, wn_ref, w_ref, r_ref, o_ref, cat_ref):
        cat_ref[:, 0:SSD_W] = y_ref[...]
        for r in (0, HALF):
            y = _gelu(g_ref[r:r + HALF, :].astype(F32)) * h_ref[r:r + HALF, :]
            cat_ref[r:r + HALF, SSD_W:] = _rms(y, wn_ref[...]).astype(BF)

        def tile(c0):
            o_ref[:, pl.ds(c0, 512)] = r_ref[:, pl.ds(c0, 512)] + jnp.dot(cat_ref[...], w_ref[:, pl.ds(c0, 512)], preferred_element_type=F32)

        _col_tiles(D, 512, tile)

    return pl.pallas_call(
        body, grid=(T // RC,),
        in_specs=[_rows_spec(SSD_W), _rows_spec(LRU_W, PG // LRU_W), _rows_spec(LRU_W), _vec(LRU_W), _whole((SSD_W + LRU_W, D)), _rows_spec(D)],
        out_specs=[_rows_spec(D), _rows_spec(SSD_W + LRU_W)],
        out_shape=[jax.ShapeDtypeStruct((T, D), F32), jax.ShapeDtypeStruct((T, SSD_W + LRU_W), BF)],
        compiler_params=_params(), name="out_proj")(yn_ssd, proj, hseq, lru_nw, w_out, h0)


def out_proj_bwd(dh1_b, w_out, proj, hseq, lru_nw, after):
    def body(d_ref, w_ref, g_ref, h_ref, wn_ref, _after, dy_ref, dh_ref, dg_ref, dw_ref, dl_scr):
        _zero_at_first(dw_ref)

        def tile(c0):
            dy_ref[:, pl.ds(c0, 512)] = lax.dot_general(d_ref[...], w_ref[pl.ds(c0, 512), :], NT_DIMS, preferred_element_type=F32)
            dl_scr[:, pl.ds(c0, 512)] = lax.dot_general(d_ref[...], w_ref[pl.ds(SSD_W + c0, 512), :], NT_DIMS, preferred_element_type=F32)

        _col_tiles(SSD_W, 512, tile)

        for r in (0, HALF):
            g = g_ref[r:r + HALF, :].astype(F32)
            h = h_ref[r:r + HALF, :]
            ge = _gelu(g)
            dy, dw = _rms_bwd(dl_scr[r:r + HALF, :], ge * h, wn_ref[...])
            dw_ref[...] += jnp.sum(dw, axis=0, keepdims=True)
            dh_ref[r:r + HALF, :] = dy * ge
            dg_ref[r:r + HALF, :] = (dy * h * _gelu_grad(g)).astype(BF)

    return pl.pallas_call(
        body, grid=(T // RC,),
        in_specs=[_rows_spec(D), _whole((SSD_W + LRU_W, D)), _rows_spec(LRU_W, PG // LRU_W), _rows_spec(LRU_W), _vec(LRU_W), ANY_SPEC],
        out_specs=[_rows_spec(SSD_W), _rows_spec(LRU_W), _rows_spec(LRU_W), _vec(LRU_W)],
        out_shape=[jax.ShapeDtypeStruct((T, SSD_W), F32), jax.ShapeDtypeStruct((T, LRU_W), F32), jax.ShapeDtypeStruct((T, LRU_W), BF),
                   jax.ShapeDtypeStruct((1, LRU_W), F32)],
        scratch_shapes=[pltpu.VMEM((RC, LRU_W), F32)],
        compiler_params=_params(), name="out_proj_bwd")(dh1_b, w_out, proj, hseq, lru_nw, after)


def in_proj_bwd(dz, dg, dxl, dxbc, ddt, w_t, w_dt, h0, wn, dh1, after):
    first = NPAD + N_META

    def body(dz_ref, dg_ref, dxl_ref, dxbc_ref, ddt_ref, w_hbm, wdt_ref, h_ref, wn_ref, r_ref, _after, gx_hbm, meta_ref, dw_ref, du_scr, o_ref, sem,
             w_ref, w_sems):
        i = pl.program_id(0)
        ready = _arriving(w_hbm, w_ref, w_sems, IN_TILE_ROWS, IN_TILE)
        _zero_at_first(dw_ref)
        du_scr[...] = jnp.dot(ddt_ref[...], wdt_ref[...], preferred_element_type=F32)
        done = 0
        for d_ref, wrow, width in ((dz_ref, 0, 1024), (dxbc_ref, 1024, XBC), (dg_ref, 2576, 1024), (dxl_ref, 3600, 1024)):
            def step(j, carry, d_ref=d_ref, wrow=wrow, done=done):
                c0 = pl.multiple_of(j * IN_TILE, IN_TILE)
                ready(done + j)
                du_scr[...] += jnp.dot(d_ref[:, pl.ds(c0, IN_TILE)], w_ref[pl.ds(pl.multiple_of(wrow + c0, 16), IN_TILE), :],
                                       preferred_element_type=F32)
                return carry

            lax.fori_loop(0, width // IN_TILE, step, 0)
            done += width // IN_TILE
        for r in (0, HALF):
            dh, dw = _rms_bwd(du_scr[r:r + HALF, :], h_ref[r:r + HALF, :], wn_ref[...])
            dw_ref[...] += jnp.sum(dw, axis=0, keepdims=True)
            o_ref[r:r + HALF, :] = dh + r_ref[r:r + HALF, :]

        @pl.when(i == 0)
        def _():
            meta_ref[...] = o_ref[NPAD:first, :]
            head = pltpu.make_async_copy(o_ref.at[pl.ds(first, RC - first)], gx_hbm.at[pl.ds(0, RC - first)], sem)
            head.start()
            head.wait()

        @pl.when(i > 0)
        def _():
            rest = pltpu.make_async_copy(o_ref, gx_hbm.at[pl.ds(pl.multiple_of(i * RC - first, 32), RC)], sem)
            rest.start()
            rest.wait()

    return pl.pallas_call(
        body, grid=(T // RC,),
        in_specs=[_rows_spec(SSD_W), _rows_spec(LRU_W), _rows_spec(LRU_W), _rows_spec(XBC), _rows_spec(256), ANY_SPEC,
                  _whole((256, D)), _rows_spec(D), _vec(D), _rows_spec(D), ANY_SPEC],
        out_specs=[ANY_SPEC, _spec((N_META, D), lambda i: (0, 0)), _vec(D)],
        out_shape=[jax.ShapeDtypeStruct((SEQ, D), F32), jax.ShapeDtypeStruct((N_META, D), F32), jax.ShapeDtypeStruct((1, D), F32)],
        scratch_shapes=[pltpu.VMEM((RC, D), F32), pltpu.VMEM((RC, D), F32), pltpu.SemaphoreType.DMA,
                        pltpu.VMEM((IN_COLS, D), BF), pltpu.SemaphoreType.DMA((len(IN_TILE_ROWS),))],
        compiler_params=_params(), name="in_proj_bwd")(dz, dg, dxl, dxbc, ddt, w_t, w_dt, h0, wn, dh1, after)


GRAD_TILE = 256


def weight_grad(name, a, u1):
    tm = GRAD_TILE

    def body(a_ref, u_ref, o_ref):
        o_ref[...] = lax.dot_general(a_ref[...], u_ref[...], TN_DIMS, preferred_element_type=F32).astype(BF)

    return pl.pallas_call(
        body, grid=(a.shape[1] // tm,),
        in_specs=[_spec((T, tm), lambda j: (0, j)), _spec((T, D), lambda j: (0, 0), single=True)],
        out_specs=_spec((tm, D), lambda j: (j, 0)),
        out_shape=jax.ShapeDtypeStruct((a.shape[1], D), BF),
        compiler_params=_params(), name=name)(a, u1)


def in_weight_grad(parts, first_rows, ddt, u1):
    tm = GRAD_TILE
    per_row = D // 128
    dt_row, dt_lines = 2560, 8 * per_row
    parts = list(parts) + [ddt]
    first_rows = list(first_rows) + [dt_row]
    tiles = [p.shape[1] // tm for p in parts]
    starts = [sum(tiles[:k]) for k in range(len(parts))]
    last = sum(tiles) - 1

    def body(*refs):
        a_refs, u_ref = refs[:len(parts)], refs[len(parts)]
        o_hbm, mix_scr, stage, sems = refs[len(parts) + 1:]
        step = pl.program_id(0)
        slot = step % 2
        line0 = 0
        for a_ref, start, n, first in zip(a_refs, starts, tiles, first_rows):
            here = (step >= start) & (step < start + n)
            line0 = jnp.where(here, per_row * (first + tm * (step - start)), line0)

            @pl.when(here)
            def _(a_ref=a_ref):
                res = lax.dot_general(a_ref[...], u_ref[...], TN_DIMS, preferred_element_type=F32)
                for q in range(per_row):
                    mix_scr[pl.ds(q, tm, stride=per_row), :] = res[:, 128 * q:128 * q + 128]

        def tile_copy(of_slot, to):
            return pltpu.make_async_copy(stage.at[of_slot], o_hbm.at[pl.ds(to, per_row * tm)], sems.at[of_slot])

        @pl.when(step >= 2)
        def _():
            tile_copy(slot, 0).wait()

        stage[slot] = mix_scr[...].astype(BF)

        @pl.when(step < last)
        def _():
            tile_copy(slot, pl.multiple_of(line0, 128)).start()

        @pl.when(step == last)
        def _():
            halves = [pltpu.make_async_copy(stage.at[slot, pl.ds(128 * per_row * k, dt_lines)],
                                            o_hbm.at[pl.ds(per_row * (dt_row + 8 * k), dt_lines)], sems.at[2 + k]) for k in range(2)]
            for cp in halves:
                cp.start()
            tile_copy(1 - slot, 0).wait()
            for cp in halves:
                cp.wait()

    def tile_of(start, n):
        return lambda j: (0, jnp.clip(j - start, 0, n - 1))

    return pl.pallas_call(
        body, grid=(last + 1,),
        in_specs=[_spec((T, tm), tile_of(s, n)) for s, n in zip(starts, tiles)] + [_spec((T, D), lambda j: (0, 0), single=True)],
        out_specs=ANY_SPEC,
        out_shape=jax.ShapeDtypeStruct((per_row * IN_COLS, 128), BF),
        scratch_shapes=[pltpu.VMEM((per_row * tm, 128), F32), pltpu.VMEM((2, per_row * tm, 128), BF), pltpu.SemaphoreType.DMA((4,))],
        compiler_params=_params(), name="dw_in")(*parts, u1)


def _ssd_chunk_common(row0, dt_ref, b_ref, c_ref, bias, a_neg):
    shape = (Q, Q)
    lane = _lanes(shape)
    sub = _rows(shape)
    live = (_rows(shape, row0) >= NPAD) & (lane < 8)
    dtr = dt_ref[:, :]
    dt = jnp.where(live, _softplus(dtr + bias), 0.0)
    d_a = dt * a_neg
    tri = (sub >= lane).astype(F32)
    cs = jnp.dot(tri, d_a, precision=lax.Precision.HIGHEST, preferred_element_type=F32)
    cs_t = cs.T
    b_f = b_ref[:, :]
    bc = b_f.astype(BF)
    cc = c_ref[:, :].astype(BF)
    cb = lax.dot_general(cc, bc, NT_DIMS, preferred_element_type=F32)
    cs_last = cs[Q - 1:Q, :]
    return dict(lane=lane, sub=sub, live=live, dtr=dtr, dt=dt, cs=cs, cs_t=cs_t, bc=bc, cc=cc, cb=cb, bc_t=b_f.T.astype(BF),
                ecs=jnp.exp(cs), dsm=jnp.exp(cs_last - cs), gam=jnp.exp(cs_last))


def _pair(lane_even, mat, j):
    return jnp.where(lane_even, mat[:, j:j + 1], mat[:, j + 1:j + 2])


def _pair_row(lane_even, mat, j):
    return jnp.where(lane_even[0:1, :], mat[:, j:j + 1], mat[:, j + 1:j + 2])


def _head_decay(cm, j):
    seg = cm["cs"][:, j:j + 1] - cm["cs_t"][j:j + 1, :]
    return jnp.exp(jnp.where(cm["sub"] >= cm["lane"], seg, -jnp.inf))


def _head_decay_t(cm, j):
    seg = cm["cs_t"][j:j + 1, :] - cm["cs"][:, j:j + 1]
    return jnp.exp(jnp.where(cm["lane"] >= cm["sub"], seg, -jnp.inf))


def _conv_window(raw_ref, halo_ref, pad_scr):
    pad_scr[0:HALO, :] = halo_ref[...].astype(F32)[halo_ref.shape[0] - HALO:, :]
    pad_scr[HALO:HALO + Q, :] = raw_ref[...].astype(F32)
    win = pad_scr[...]
    return lambda s: win[HALO:, :] if s == 0 else pltpu.roll(win, s, axis=0)[HALO:, :]


def _xbc_cols(g):
    return slice(512 * g, 512 * g + 512), slice(SSD_W + 128 * g, SSD_W + 128 * g + 128), slice(SSD_W + 256 + 128 * g, SSD_W + 384 + 128 * g)


def ssd_fwd(proj, dt_raw, conv_w, conv_b, dt_bias2, a_log2, d2, norm_w):
    def body(raw_ref, halo_ref, dt_all, z_all, cw_ref, cb_ref, bias_all, alog_all, d_all, nw_all, yn_all, y_all, hp_all,
             h_all, pad_scr, act_scr):
        @pl.when(pl.program_id(0) == 0)
        def _():
            h_all[...] = jnp.zeros_like(h_all)

        pre = _conv(_conv_window(raw_ref, halo_ref, pad_scr), cw_ref[...], cb_ref[...])
        act_scr[...] = pre * _sigmoid(pre)
        for g in range(2):
            wide, thin = slice(512 * g, 512 * g + 512), slice(128 * g, 128 * g + 128)
            xs, bs, cs = _xbc_cols(g)
            group(act_scr.at[:, xs], act_scr.at[:, bs], act_scr.at[:, cs], dt_all.at[:, thin], z_all.at[:, wide], bias_all.at[g],
                  alog_all.at[g], d_all.at[g], nw_all.at[:, wide], yn_all.at[:, wide], y_all.at[:, wide], hp_all.at[g, 0], h_all.at[g])

    def group(x_ref, b_ref, c_ref, dt_ref, z_ref, bias_ref, alog_ref, d_ref, nw_ref, yn_ref, y_ref, hp_ref, h_scr):
        bias = bias_ref[...]
        a_neg = -jnp.exp(alog_ref[...])
        dsk = d_ref[...]
        cm = _ssd_chunk_common(pl.program_id(0) * Q, dt_ref, b_ref, c_ref, bias, a_neg)
        lane_even = cm["lane"] < 64
        for p in range(4):
            je, jo = 2 * p, 2 * p + 1
            xp = x_ref[:, 128 * p:128 * p + 128]
            xdt = xp * _pair(lane_even, cm["dt"], je)
            xdt_b = xdt.astype(BF)
            m_e = (cm["cb"] * _head_decay(cm, je)).astype(BF)
            m_o = (cm["cb"] * _head_decay(cm, jo)).astype(BF)
            zero = jnp.zeros_like(xdt_b)
            yd = (jnp.dot(m_e, jnp.where(lane_even, xdt_b, zero), preferred_element_type=F32)
                  + jnp.dot(m_o, jnp.where(lane_even, zero, xdt_b), preferred_element_type=F32))
            hp = h_scr[p]
            hp_ref[p] = hp
            yo = jnp.dot(cm["cc"], hp.astype(BF), preferred_element_type=F32) * _pair(lane_even, cm["ecs"], je)
            y_ref[:, 128 * p:128 * p + 128] = yd + yo + xp * _pair_row(lane_even, dsk, je)
            st = jnp.dot(cm["bc_t"], (xdt * _pair(lane_even, cm["dsm"], je)).astype(BF), preferred_element_type=F32)
            h_scr[p] = hp * _pair_row(lane_even, cm["gam"], je) + st
        zc = z_ref[:, :].astype(F32)
        gated = y_ref[:, :] * (zc * _sigmoid(zc))
        yn_ref[:, :] = _rms(gated, nw_ref[...]).astype(BF)

    par = _spec((2, 1, 128), lambda c: (0, 0, 0))
    wide = _spec((Q, SSD_W), lambda c: (c, 0))
    xbc = PXBC // XBC
    halo = 2 * HALO
    return pl.pallas_call(
        body, grid=(NCH,),
        in_specs=[_spec((Q, XBC), lambda c: (c, xbc)), _spec((halo, XBC), lambda c: (jnp.maximum(c * (Q // halo) - 1, 0), xbc)),
                  _spec((Q, 256), lambda c: (c, 0)), wide, _spec((4, XBC), lambda c: (0, 0)), _spec((1, XBC), lambda c: (0, 0)),
                  par, par, par, _spec((1, SSD_W), lambda c: (0, 0))],
        out_specs=[wide, wide, _spec((2, 1, 4, 128, 128), lambda c: (0, c, 0, 0, 0))],
        out_shape=[jax.ShapeDtypeStruct((T, SSD_W), BF), jax.ShapeDtypeStruct((T, SSD_W), F32),
                   jax.ShapeDtypeStruct((2, NCH, 4, 128, 128), F32)],
        scratch_shapes=[pltpu.VMEM((2, 4, 128, 128), F32), pltpu.VMEM((Q + HALO, XBC), F32), pltpu.VMEM((Q, XBC), F32)],
        compiler_params=_params(), name="ssd_fwd")(proj, proj, dt_raw, proj, conv_w, conv_b, dt_bias2, a_log2, d2, norm_w)


def ssd_bwd(dyn, proj, dt_raw, conv_w, conv_b, y_pre, h_prev, dt_bias2, a_log2, d2, norm_w, after):
    def body(dyn_all, raw_ref, halo_ref, dt_all, z_all, y_all, hp_all, cw_ref, cb_ref, bias_all, alog_all, d_all, nw_all, _after,
             dz_all, dxbc_ref, ddt_all, dpar_all, dnw_all, dcw_ref, dcb_ref, dh_all, acc_all, pad_scr, act_scr, dsilu_scr, dact_scr, dpad_scr):
        @pl.when(pl.program_id(0) == 0)
        def _():
            dh_all[...] = jnp.zeros_like(dh_all)
            acc_all[...] = jnp.zeros_like(acc_all)
            dnw_all[...] = jnp.zeros_like(dnw_all)
            dcw_ref[...] = jnp.zeros_like(dcw_ref)
            dcb_ref[...] = jnp.zeros_like(dcb_ref)
            dpad_scr[Q:Q + HALO, :] = jnp.zeros((HALO, XBC), F32)

        back = _conv_window(raw_ref, halo_ref, pad_scr)
        pre = _conv(back, cw_ref[...], cb_ref[...])
        sg = _sigmoid(pre)
        act_scr[...] = pre * sg
        dsilu_scr[...] = sg * (1.0 + pre * (1.0 - sg))
        for g in range(2):
            wide, thin = slice(512 * g, 512 * g + 512), slice(128 * g, 128 * g + 128)
            xs, bs, cs = _xbc_cols(g)
            group(dyn_all.at[:, wide], act_scr.at[:, xs], act_scr.at[:, bs], act_scr.at[:, cs], dt_all.at[:, thin], z_all.at[:, wide],
                  y_all.at[:, wide], hp_all.at[g, 0], bias_all.at[g], alog_all.at[g], d_all.at[g], nw_all.at[:, wide],
                  dz_all.at[:, wide], dact_scr.at[:, xs], dact_scr.at[:, bs], dact_scr.at[:, cs], ddt_all.at[:, thin], dpar_all.at[g],
                  dnw_all.at[:, wide], dh_all.at[g], acc_all.at[g])
        dpre = dact_scr[...] * dsilu_scr[...]
        dcw, dcb = _conv_bwd_w(dpre, back)
        dcw_ref[...] += dcw
        dcb_ref[...] += dcb
        dpad_scr[0:Q, :] = dpre
        win = dpad_scr[...]
        dxbc_ref[...] = _conv_bwd_x(lambda s: win[:Q, :] if s == 0 else pltpu.roll(win, Q + HALO - s, axis=0)[:Q, :], cw_ref[...]).astype(BF)
        dpad_scr[Q:Q + HALO, :] = dpre[0:HALO, :]

    def group(dyn_ref, x_ref, b_ref, c_ref, dt_ref, z_ref, y_ref, hp_ref, bias_ref, alog_ref, d_ref, nw_ref,
              dz_ref, dx_ref, db_ref, dc_ref, ddt_ref, dpar_ref, dnw_ref, dh_scr, acc_scr):
        ci = pl.program_id(0)
        bias = bias_ref[...]
        a_neg = -jnp.exp(alog_ref[...])
        dsk = d_ref[...]
        cm = _ssd_chunk_common((NCH - 1 - ci) * Q, dt_ref, b_ref, c_ref, bias, a_neg)
        lane, sub = cm["lane"], cm["sub"]
        lane_even = lane < 64
        cc_t = c_ref[:, :].T.astype(BF)
        cb_t = lax.dot_general(cm["bc"], cm["cc"], NT_DIMS, preferred_element_type=F32)
        zc = z_ref[:, :].astype(F32)
        yc = y_ref[:, :]
        sg = _sigmoid(zc)
        sz = zc * sg
        dgated, dnw = _rms_bwd(dyn_ref[:, :], yc * sz, nw_ref[...])
        dnw_ref[...] += jnp.sum(dnw, axis=0, keepdims=True)
        dz_ref[:, :] = (dgated * yc * (sg * (1.0 + zc * (1.0 - sg)))).astype(BF)
        dy_all = dgated * sz
        dcb = jnp.zeros((Q, Q), F32)
        dcb_t = jnp.zeros((Q, Q), F32)
        db_acc = jnp.zeros((Q, Q), F32)
        dc_acc = jnp.zeros((Q, Q), F32)
        dcs = jnp.zeros((Q, Q), F32)
        ddt = jnp.zeros((Q, Q), F32)
        for p in range(4):
            je, jo = 2 * p, 2 * p + 1
            xp = x_ref[:, 128 * p:128 * p + 128]
            dy = dy_all[:, 128 * p:128 * p + 128]
            dt_p = _pair(lane_even, cm["dt"], je)
            xdt = xp * dt_p
            xdt_b = xdt.astype(BF)
            dy_b = dy.astype(BF)
            zero = jnp.zeros_like(dy_b)
            hp = hp_ref[p]
            hp_b = hp.astype(BF)
            dh = dh_scr[p]
            dh_b = dh.astype(BF)
            acc_scr[p:p + 1, :] += jnp.sum(dy * xp, axis=0, keepdims=True)
            dxp = dy * _pair_row(lane_even, dsk, je)
            e_p = _pair(lane_even, cm["ecs"], je)
            g_p = jnp.dot(cm["cc"], hp_b, preferred_element_type=F32)
            dg_b = (dy * e_p).astype(BF)
            de = dy * g_p * e_p
            dc_acc = dc_acc + lax.dot_general(dg_b, hp_b, NT_DIMS, preferred_element_type=F32)
            dh_in = jnp.dot(cc_t, dg_b, preferred_element_type=F32)
            ds_p = _pair(lane_even, cm["dsm"], je)
            r_p = jnp.dot(cm["bc"], dh_b, preferred_element_type=F32)
            dxdt = r_p * ds_p
            tt = r_p * xdt * ds_p
            db_acc = db_acc + lax.dot_general((xdt * ds_p).astype(BF), dh_b, NT_DIMS, preferred_element_type=F32)
            dgam_m = jnp.sum(dh * hp, axis=0, keepdims=True)
            for j, even in ((je, True), (jo, False)):
                sel = lane_even if even else jnp.logical_not(lane_even)
                dy_j = jnp.where(sel, dy_b, zero)
                l_j = _head_decay(cm, j)
                l_jt = _head_decay_t(cm, j)
                m_j = cm["cb"] * l_j
                m_jt = cb_t * l_jt
                dm = lax.dot_general(dy_j, xdt_b, NT_DIMS, preferred_element_type=F32)
                dm_t = lax.dot_general(xdt_b, dy_j, NT_DIMS, preferred_element_type=F32)
                dxdt = dxdt + jnp.dot(m_jt.astype(BF), dy_j, preferred_element_type=F32)
                dcb = dcb + dm * l_j
                dcb_t = dcb_t + dm_t * l_jt
                t_j = jnp.where(sel, tt, 0.0)
                col = jnp.sum(dm * m_j - dm_t * m_jt + (jnp.where(sel, de, 0.0) - t_j), axis=1, keepdims=True)
                gam_j = cm["gam"][:, j:j + 1]
                last = (jnp.sum(jnp.sum(t_j, axis=0, keepdims=True), axis=1, keepdims=True)
                        + jnp.sum(jnp.where(sel[0:1, :], dgam_m, 0.0), axis=1, keepdims=True) * gam_j)
                col = col + jnp.where(sub[:, 0:1] == Q - 1, last, 0.0)
                dcs = dcs + jnp.where(lane == j, col, 0.0)
            dh_scr[p] = dh_in + dh * _pair_row(lane_even, cm["gam"], je)
            dx_ref[:, 128 * p:128 * p + 128] = dxp + dxdt * dt_p
            dd = dxdt * xp
            ddt = ddt + jnp.where(lane == je, jnp.sum(jnp.where(lane_even, dd, 0.0), axis=1, keepdims=True), 0.0)
            ddt = ddt + jnp.where(lane == jo, jnp.sum(jnp.where(lane_even, 0.0, dd), axis=1, keepdims=True), 0.0)
        dc_ref[:, :] = dc_acc + jnp.dot(dcb.astype(BF), cm["bc"], preferred_element_type=F32)
        db_ref[:, :] = db_acc + jnp.dot(dcb_t.astype(BF), cm["cc"], preferred_element_type=F32)
        tri_t = (sub <= lane).astype(F32)
        dd_a = jnp.dot(tri_t, dcs, precision=lax.Precision.HIGHEST, preferred_element_type=F32)
        ddt = ddt + dd_a * a_neg
        acc_scr[5:6, :] += jnp.sum(dd_a * cm["dt"], axis=0, keepdims=True)
        draw = jnp.where(cm["live"], ddt * _sigmoid_gate(cm["dtr"] + bias), 0.0)
        acc_scr[4:5, :] += jnp.sum(draw, axis=0, keepdims=True)
        ddt_ref[:, :] = draw.astype(BF)

        @pl.when(ci == NCH - 1)
        def _():
            lane1 = _lanes((1, 128))
            dd = jnp.zeros((1, 128), F32)
            for p in range(4):
                row = acc_scr[p:p + 1, :]
                dd = dd + jnp.where(lane1 == 2 * p, jnp.sum(jnp.where(lane1 < 64, row, 0.0), axis=1, keepdims=True), 0.0)
                dd = dd + jnp.where(lane1 == 2 * p + 1, jnp.sum(jnp.where(lane1 < 64, 0.0, row), axis=1, keepdims=True), 0.0)
            dpar_ref[...] = jnp.concatenate([acc_scr[4:5, :], acc_scr[5:6, :] * a_neg, dd, jnp.zeros((5, 128), F32)], axis=0)

    par = _spec((2, 1, 128), lambda c: (0, 0, 0))
    wide = _spec((Q, SSD_W), lambda c: (NCH - 1 - c, 0))
    thin = _spec((Q, 256), lambda c: (NCH - 1 - c, 0))
    vec = _spec((1, SSD_W), lambda c: (0, 0))
    xbc = PXBC // XBC
    halo = 2 * HALO
    chunk = pltpu.VMEM((Q, XBC), F32)
    padded = pltpu.VMEM((Q + HALO, XBC), F32)
    return pl.pallas_call(
        body, grid=(NCH,),
        in_specs=[wide, _spec((Q, XBC), lambda c: (NCH - 1 - c, xbc)),
                  _spec((halo, XBC), lambda c: (jnp.maximum((NCH - 1 - c) * (Q // halo) - 1, 0), xbc)), thin, wide, wide,
                  _spec((2, 1, 4, 128, 128), lambda c: (0, NCH - 1 - c, 0, 0, 0)), _spec((4, XBC), lambda c: (0, 0)),
                  _spec((1, XBC), lambda c: (0, 0)), par, par, par, vec, ANY_SPEC],
        out_specs=[wide, _spec((Q, XBC), lambda c: (NCH - 1 - c, 0)), thin, _spec((2, 8, 128), lambda c: (0, 0, 0)), vec,
                   _spec((4, XBC), lambda c: (0, 0)), _spec((1, XBC), lambda c: (0, 0))],
        out_shape=[jax.ShapeDtypeStruct((T, SSD_W), BF), jax.ShapeDtypeStruct((T, XBC), BF), jax.ShapeDtypeStruct((T, 256), BF),
                   jax.ShapeDtypeStruct((2, 8, 128), F32), jax.ShapeDtypeStruct((1, SSD_W), F32), jax.ShapeDtypeStruct((4, XBC), F32),
                   jax.ShapeDtypeStruct((1, XBC), F32)],
        scratch_shapes=[pltpu.VMEM((2, 4, 128, 128), F32), pltpu.VMEM((2, 8, 128), F32), padded, chunk, chunk, chunk, padded],
        compiler_params=_params(), name="ssd_bwd")(dyn, proj, proj, dt_raw, proj, y_pre, h_prev, conv_w, conv_b, dt_bias2, a_log2, d2, norm_w, after)


def _lru_gates(back, cw, cb, wa, ba, wx, bx, lam):
    xr = _conv(back, cw, cb)
    xr_b = xr.astype(BF)
    r = _sigmoid_gate(jnp.dot(xr_b, wa, preferred_element_type=F32) + ba)
    i = _sigmoid_gate(jnp.dot(xr_b, wx, preferred_element_type=F32) + bx)
    sp = _softplus(-lam)
    la = (-LRU_C) * r * sp
    a = jnp.exp(la)
    mult2 = -jnp.tanh(la) * (a * a + 1.0)
    return xr, xr_b, r, i, sp, a, jnp.sqrt(mult2), mult2


SEG_LEN = 68
SEGS = T // SEG_LEN


def _seg_rows(j, k, off=0):
    return pl.ds(off + j * 8 * SEG_LEN + k, 8, stride=SEG_LEN)


def _segmented_scan(mul_ref, mul_row0, add_ref, out_ref, loc_scr, prod_scr, carry_scr, reverse):
    groups = SEGS // 8
    off = mul_row0 + (1 if reverse else 0)

    def local(i, carry):
        k = SEG_LEN - 1 - i if reverse else i
        new = []
        for j in range(groups):
            h, p = carry[2 * j], carry[2 * j + 1]
            m = mul_ref[_seg_rows(j, k, off), :]
            h = m * h + add_ref[_seg_rows(j, k), :]
            p = m * p
            loc_scr[_seg_rows(j, k), :] = h
            prod_scr[_seg_rows(j, k), :] = p
            new += [h, p]
        return tuple(new)

    lax.fori_loop(0, SEG_LEN, local, (jnp.zeros((8, 128), F32), jnp.ones((8, 128), F32)) * groups)

    def chain(i, c):
        s = SEGS - 1 - i if reverse else i
        carry_scr[pl.ds(s, 1), :] = c
        edge = s * SEG_LEN + (0 if reverse else SEG_LEN - 1)
        return loc_scr[pl.ds(edge, 1), :] + prod_scr[pl.ds(edge, 1), :] * c

    lax.fori_loop(0, SEGS, chain, jnp.zeros((1, 128), F32))

    def fold(k, carry):
        for j in range(groups):
            rows = _seg_rows(j, k)
            out_ref[rows, :] = loc_scr[rows, :] + prod_scr[rows, :] * carry_scr[8 * j:8 * j + 8, :]
        return carry

    lax.fori_loop(0, SEG_LEN, fold, 0)


def lru_fwd(proj, cw, cb, wa2, ba, wx2, bx, lam, after):
    def body(x_ref, cw_ref, cb_ref, wa_ref, ba_ref, wx_ref, bx_ref, lam_ref, _after, h_ref, a_ref, xpad, u_scr, loc_scr, prod_scr, carry_scr):
        _fill_padded(xpad, x_ref)

        def chunk(r0):
            xr, _, _, i, _, a, mult, _ = _lru_gates(_back(xpad, r0), cw_ref[...], cb_ref[...], wa_ref[0], ba_ref[...], wx_ref[0], bx_ref[...],
                                                 lam_ref[...])
            a_ref[pl.ds(r0, Q), :] = a
            u_scr[pl.ds(r0, Q), :] = jnp.where(_rows(a.shape, r0) >= NPAD, mult * (i * xr), 0.0)

        _chunks(chunk, unrolled=True)
        _segmented_scan(a_ref, 0, u_scr, h_ref, loc_scr, prod_scr, carry_scr, reverse=False)

    c0 = PXL // 128
    vec = _spec((1, 128), lambda c: (0, c))
    mat = _spec((1, 128, 128), lambda c: (c, 0, 0))
    seq = pltpu.VMEM((T, 128), F32)
    return pl.pallas_call(
        body, grid=(8,),
        in_specs=[_spec((T, 128), lambda c: (0, c0 + c)), _spec((4, 128), lambda c: (0, c)), vec, mat, vec, mat, vec, vec, ANY_SPEC],
        out_specs=[_spec((T, 128), lambda c: (0, c)), _spec((T, 128), lambda c: (0, c))],
        out_shape=[jax.ShapeDtypeStruct((T, LRU_W), F32), jax.ShapeDtypeStruct((T, LRU_W), F32)],
        scratch_shapes=[pltpu.VMEM((T + 2 * HALO, 128), F32), seq, seq, seq, pltpu.VMEM((SEGS, 128), F32)],
        compiler_params=_params(), name="lru_fwd")(proj, cw, cb, wa2, ba, wx2, bx, lam, after)


def lru_bwd(dh_out, a, hseq, proj, cw, cb, wa2, ba, wx2, bx, lam):
    def body(d_ref, a_ref, h_ref, x_ref, cw_ref, cb_ref, wa_ref, ba_ref, wx_ref, bx_ref, lam_ref,
             dx_ref, dcw_ref, dcb_ref, dwa_ref, dba_ref, dwx_ref, dbx_ref, dlam_ref, xpad, hpad, dpad, dh_ref, loc_scr, prod_scr, carry_scr):
        _fill_padded(dpad, a_ref)
        _segmented_scan(dpad, HALO, d_ref, dh_ref, loc_scr, prod_scr, carry_scr, reverse=True)
        _fill_padded(xpad, x_ref)
        _fill_padded(hpad, h_ref)
        dpad[0:HALO, :] = jnp.zeros((HALO, 128), F32)
        dpad[T + HALO:T + 2 * HALO, :] = jnp.zeros((HALO, 128), F32)
        for ref in (dcw_ref, dcb_ref, dwa_ref, dba_ref, dwx_ref, dbx_ref, dlam_ref):
            ref[...] = jnp.zeros_like(ref)
        lam = lam_ref[...]

        def first(r0):
            back = _back(xpad, r0)
            xr, xr_b, r, i, sp, a, mult, mult2 = _lru_gates(back, cw_ref[...], cb_ref[...], wa_ref[0], ba_ref[...], wx_ref[0], bx_ref[...], lam)
            dh = dh_ref[pl.ds(r0, Q), :]
            da = dh * _back(hpad, r0)(1)
            du = jnp.where(_rows(dh.shape, r0) >= NPAD, dh, 0.0)
            dmult = du * (i * xr)
            di = du * (mult * xr)
            dxr = du * (mult * i)
            dla = da * a - dmult * (a * a) * lax.rsqrt(mult2)
            dr = dla * ((-LRU_C) * sp)
            dlam_ref[...] += jnp.sum(dla * ((-LRU_C) * r), axis=0, keepdims=True)
            dpr = dr * r * (1.0 - r)
            dpi = di * i * (1.0 - i)
            dba_ref[...] += jnp.sum(dpr, axis=0, keepdims=True)
            dbx_ref[...] += jnp.sum(dpi, axis=0, keepdims=True)
            dpr_b = dpr.astype(BF)
            dpi_b = dpi.astype(BF)
            dxr = (dxr + lax.dot_general(dpr_b, wa_ref[0], NT_DIMS, preferred_element_type=F32)
                   + lax.dot_general(dpi_b, wx_ref[0], NT_DIMS, preferred_element_type=F32))
            dwa_ref[0] += lax.dot_general(xr_b, dpr_b, TN_DIMS, preferred_element_type=F32)
            dwx_ref[0] += lax.dot_general(xr_b, dpi_b, TN_DIMS, preferred_element_type=F32)
            dpad[pl.ds(r0 + HALO, Q), :] = dxr
            dcw, dcb = _conv_bwd_w(dxr, back)
            dcw_ref[...] += dcw
            dcb_ref[...] += dcb

        _chunks(first, unrolled=True)
        dlam_ref[...] = -dlam_ref[...] * _sigmoid_gate(-lam)

        def second(r0):
            dx_ref[pl.ds(r0, Q), :] = _conv_bwd_x(_ahead(dpad, r0), cw_ref[...]).astype(BF)

        _chunks(second)

    c0 = PXL // 128
    vec = _spec((1, 128), lambda c: (0, c))
    mat = _spec((1, 128, 128), lambda c: (c, 0, 0))
    col = _spec((T, 128), lambda c: (0, c))
    vshape = jax.ShapeDtypeStruct((1, LRU_W), F32)
    mshape = jax.ShapeDtypeStruct((8, 128, 128), F32)
    pad = pltpu.VMEM((T + 2 * HALO, 128), F32)
    seq = pltpu.VMEM((T, 128), F32)
    return pl.pallas_call(
        body, grid=(8,),
        in_specs=[col, col, col, _spec((T, 128), lambda c: (0, c0 + c)), _spec((4, 128), lambda c: (0, c)), vec, mat, vec, mat, vec, vec],
        out_specs=[col, _spec((4, 128), lambda c: (0, c)), vec, mat, vec, mat, vec, vec],
        out_shape=[jax.ShapeDtypeStruct((T, LRU_W), BF), jax.ShapeDtypeStruct((4, LRU_W), F32), vshape, mshape, vshape, mshape, vshape, vshape],
        scratch_shapes=[pad, pad, pad, seq, seq, seq, pltpu.VMEM((SEGS, 128), F32)],
        compiler_params=_params(), name="lru_bwd")(dh_out, a, hseq, proj, cw, cb, wa2, ba, wx2, bx, lam)


FF_TILE = 256
FF_TILE_ROWS = list(range(0, D_FF, FF_TILE))


def gate_up(h1, wn, w_gate, w_up):
    def body(h_ref, wn_ref, wg_hbm, wu_hbm, gt_ref, up_ref, act_ref, u_ref, wg_ref, wu_ref, wg_sems, wu_sems):
        gate_ready = _arriving(wg_hbm, wg_ref, wg_sems, FF_TILE_ROWS, FF_TILE)
        up_ready = _arriving(wu_hbm, wu_ref, wu_sems, FF_TILE_ROWS, FF_TILE)
        for r in (0, HALF):
            u_ref[r:r + HALF, :] = _rms(h_ref[r:r + HALF, :], wn_ref[...]).astype(BF)

        def tile(c0):
            cols = pl.ds(c0, FF_TILE)
            gate_ready(c0 // FF_TILE)
            up_ready(c0 // FF_TILE)
            gt = lax.dot_general(u_ref[...], wg_ref[cols, :], NT_DIMS, preferred_element_type=F32)
            up = lax.dot_general(u_ref[...], wu_ref[cols, :], NT_DIMS, preferred_element_type=F32)
            gt_ref[:, cols] = gt.astype(BF)
            up_ref[:, cols] = up.astype(BF)
            act_ref[:, cols] = (gt * _sigmoid(gt) * up).astype(BF)

        _col_tiles(D_FF, FF_TILE, tile)

    big = jax.ShapeDtypeStruct((T, D_FF), BF)
    return pl.pallas_call(
        body, grid=(T // RC,), in_specs=[_rows_spec(D), _vec(D), ANY_SPEC, ANY_SPEC],
        out_specs=[_rows_spec(D_FF), _rows_spec(D_FF), _rows_spec(D_FF), _rows_spec(D)],
        out_shape=[big, big, big, jax.ShapeDtypeStruct((T, D), BF)],
        scratch_shapes=[pltpu.VMEM((D_FF, D), BF)] * 2 + [pltpu.SemaphoreType.DMA((len(FF_TILE_ROWS),))] * 2,
        compiler_params=_params(), name="gate_up")(h1, wn, w_gate, w_up)


def down_loss(act, w_down, h1, target, wf):
    first = NPAD + N_META

    def body(a_ref, w_ref, r_ref, t_hbm, wf_ref, d_ref, db_ref, l_ref, dw_ref, h_scr, t_ref, t_sem):
        i = pl.program_id(0)
        _zero_at_first(l_ref, dw_ref)
        head = pltpu.make_async_copy(t_hbm.at[pl.ds(0, RC - first)], t_ref.at[pl.ds(first, RC - first)], t_sem)
        rest = pltpu.make_async_copy(t_hbm.at[pl.ds(pl.multiple_of(jnp.maximum(i * RC - first, 0), 32), RC)], t_ref, t_sem)

        @pl.when(i == 0)
        def _():
            t_ref[0:first, :] = jnp.zeros((first, D), F32)
            head.start()

        @pl.when(i > 0)
        def _():
            rest.start()

        def tile(c0):
            cols = pl.ds(c0, 512)
            h_scr[:, cols] = r_ref[:, cols] + jnp.dot(a_ref[...], w_ref[:, cols], preferred_element_type=F32)

        _col_tiles(D, 512, tile)

        @pl.when(i == 0)
        def _():
            head.wait()

        @pl.when(i > 0)
        def _():
            rest.wait()

        for r in (0, HALF):
            h = h_scr[r:r + HALF, :]
            live = _rows((HALF, D), i * RC + r) >= first
            err = jnp.where(live, _rms(h, wf_ref[...]) - t_ref[r:r + HALF, :], 0.0)
            l_ref[...] += 0.5 * jnp.sum(jnp.sum(err * err, axis=1, keepdims=True) * (1.0 / D), axis=0, keepdims=True)
            dh, dw = _rms_bwd(err * (1.0 / D), h, wf_ref[...])
            dw_ref[...] += jnp.sum(dw, axis=0, keepdims=True)
            d_ref[r:r + HALF, :] = dh
            db_ref[r:r + HALF, :] = dh.astype(BF)

    return pl.pallas_call(
        body, grid=(T // RC,),
        in_specs=[_rows_spec(D_FF), _whole((D_FF, D)), _rows_spec(D), pl.BlockSpec(memory_space=pl.ANY), _vec(D)],
        out_specs=[_rows_spec(D), _rows_spec(D), _spec((1, 128), lambda i: (0, 0)), _vec(D)],
        out_shape=[jax.ShapeDtypeStruct((T, D), F32), jax.ShapeDtypeStruct((T, D), BF), jax.ShapeDtypeStruct((1, 128), F32),
                   jax.ShapeDtypeStruct((1, D), F32)],
        scratch_shapes=[pltpu.VMEM((RC, D), F32), pltpu.VMEM((RC, D), F32), pltpu.SemaphoreType.DMA],
        compiler_params=_params(), name="down_loss")(act, w_down, h1, target, wf)


def swiglu_bwd(dh2_b, w_down, gt, up, act, u2):
    tn = 256

    def body(d_hbm, u_hbm, w_ref, gt_ref, up_ref, act_ref, dg_ref, du_ref, gd_ref, gg_ref, gu_ref, d_ref, u_ref, d_sems, u_sems):
        chunks = list(range(0, T, RC))
        d_ready = _arriving(d_hbm, d_ref, d_sems, chunks, RC)
        u_ready = _arriving(u_hbm, u_ref, u_sems, chunks, RC)

        def rows(r0):
            part = pl.ds(r0, RC)
            d_ready(r0 // RC)
            dact = lax.dot_general(d_ref[part, :], w_ref[...], NT_DIMS, preferred_element_type=F32)
            gt_ = gt_ref[part, :].astype(F32)
            up_ = up_ref[part, :].astype(F32)
            sg = _sigmoid(gt_)
            dg_ref[part, :] = (dact * up_ * (sg * (1.0 + gt_ * (1.0 - sg)))).astype(BF)
            du_ref[part, :] = (dact * (gt_ * sg)).astype(BF)

        _col_tiles(T, RC, rows)
        for k in range(len(chunks)):
            u_ready(k)
        gd_ref[...] = lax.dot_general(act_ref[...], d_ref[...], TN_DIMS, preferred_element_type=F32).astype(BF)
        gg_ref[...] = lax.dot_general(dg_ref[...], u_ref[...], TN_DIMS, preferred_element_type=F32).astype(BF)
        gu_ref[...] = lax.dot_general(du_ref[...], u_ref[...], TN_DIMS, preferred_element_type=F32).astype(BF)

    cols = _spec((T, tn), lambda j: (0, j))
    wrow = _spec((tn, D), lambda j: (j, 0))
    big = jax.ShapeDtypeStruct((T, D_FF), BF)
    grad = jax.ShapeDtypeStruct((D_FF, D), BF)
    return pl.pallas_call(
        body, grid=(D_FF // tn,), in_specs=[ANY_SPEC, ANY_SPEC, wrow, cols, cols, cols],
        out_specs=[cols, cols, wrow, wrow, wrow], out_shape=[big, big, grad, grad, grad],
        scratch_shapes=[pltpu.VMEM((T, D), BF)] * 2 + [pltpu.SemaphoreType.DMA((T // RC,))] * 2,
        compiler_params=_params(), name="swiglu_bwd")(dh2_b, u2, w_down, gt, up, act)


def gate_up_bwd(dgt, dup, w_gate, w_up, h1, wn, dh2, after):
    def body(dg_ref, du_ref, wg_hbm, wu_hbm, h_ref, wn_ref, r_ref, _after, d_ref, db_ref, dw_ref, du_scr, wg_ref, wu_ref, wg_sems, wu_sems):
        gate_ready = _arriving(wg_hbm, wg_ref, wg_sems, FF_TILE_ROWS, FF_TILE)
        up_ready = _arriving(wu_hbm, wu_ref, wu_sems, FF_TILE_ROWS, FF_TILE)
        _zero_at_first(dw_ref)

        du_scr[...] = jnp.zeros_like(du_scr)

        def tile(c0):
            k = pl.ds(c0, FF_TILE)
            gate_ready(c0 // FF_TILE)
            up_ready(c0 // FF_TILE)
            du_scr[...] += (jnp.dot(dg_ref[:, k], wg_ref[k, :], preferred_element_type=F32)
                            + jnp.dot(du_ref[:, k], wu_ref[k, :], preferred_element_type=F32))

        _col_tiles(D_FF, FF_TILE, tile)
        for r in (0, HALF):
            dh, dw = _rms_bwd(du_scr[r:r + HALF, :], h_ref[r:r + HALF, :], wn_ref[...])
            dw_ref[...] += jnp.sum(dw, axis=0, keepdims=True)
            dh = dh + r_ref[r:r + HALF, :]
            d_ref[r:r + HALF, :] = dh
            db_ref[r:r + HALF, :] = dh.astype(BF)

    return pl.pallas_call(
        body, grid=(T // RC,),
        in_specs=[_rows_spec(D_FF), _rows_spec(D_FF), ANY_SPEC, ANY_SPEC, _rows_spec(D), _vec(D), _rows_spec(D), ANY_SPEC],
        out_specs=[_rows_spec(D), _rows_spec(D), _vec(D)],
        out_shape=[jax.ShapeDtypeStruct((T, D), F32), jax.ShapeDtypeStruct((T, D), BF), jax.ShapeDtypeStruct((1, D), F32)],
        scratch_shapes=[pltpu.VMEM((RC, D), F32)] + [pltpu.VMEM((D_FF, D), BF)] * 2 + [pltpu.SemaphoreType.DMA((len(FF_TILE_ROWS),))] * 2,
        compiler_params=_params(), name="gate_up_bwd")(dgt, dup, w_gate, w_up, h1, wn, dh2, after)


def _adamw(w, g, m, v):
    m = ADAM_B1 * m + (1.0 - ADAM_B1) * g
    v = ADAM_B2 * v + (1.0 - ADAM_B2) * (g * g)
    m_hat = m / (1.0 - ADAM_B1 ** ADAM_STEP)
    v_hat = v / (1.0 - ADAM_B2 ** ADAM_STEP)
    delta = -ADAM_LR * (m_hat / (jnp.sqrt(v_hat) + ADAM_EPS) + ADAM_WD * w)
    return delta, m, v


def adamw_shards(name, recvs, ws, ms, vs):
    n = len(ws)

    def body(*refs):
        ins, outs = refs[:4 * n], refs[4 * n:]
        for k in range(n):
            p_ref, w_ref, m_ref, v_ref = ins[k], ins[n + k], ins[2 * n + k], ins[3 * n + k]
            g = p_ref[0].astype(F32)
            for s in range(1, 8):
                g = g + p_ref[s].astype(F32)
            outs[4 * k][...] = g
            outs[4 * k + 1][...], outs[4 * k + 2][...], outs[4 * k + 3][...] = _adamw(w_ref[...], g, m_ref[...], v_ref[...])

    tiles = [_spec((w.shape[0] // 2, w.shape[1]), lambda i: (i, 0)) for w in ws]
    recv_tiles = [_spec((8, w.shape[0] // 2, w.shape[1]), lambda i: (0, i, 0)) for w in ws]
    res = pl.pallas_call(
        body, grid=(2,), in_specs=recv_tiles + tiles * 3,
        out_specs=[t for t in tiles for _ in range(4)],
        out_shape=[jax.ShapeDtypeStruct(w.shape, F32) for w in ws for _ in range(4)],
        compiler_params=_params(), name=name)(*recvs, *ws, *ms, *vs)
    return [list(res[4 * k:4 * k + 4]) for k in range(n)]


def adamw_w_in(recv, w, m, v, after):
    rows = 34
    per_row = D // 128

    def body(p_ref, w_ref, m_ref, v_ref, _after, g_ref, d_ref, mo_ref, vo_ref):
        def chunk(c, carry):
            lines = pl.ds(pl.multiple_of(c * per_row * rows, 16), per_row * rows)
            g = p_ref[0, lines, :].astype(F32)
            for s in range(1, 8):
                g = g + p_ref[s, lines, :].astype(F32)
            g = g.reshape(rows, per_row, 128)
            part = pl.ds(c * rows, rows)
            g_ref[part] = g
            d_ref[part], mo_ref[part], vo_ref[part] = _adamw(w_ref[part], g, m_ref[part], v_ref[part])
            return carry

        lax.fori_loop(0, w.shape[0] // rows, chunk, 0)

    shape = jax.ShapeDtypeStruct(w.shape, F32)
    whole = pl.BlockSpec(memory_space=pltpu.VMEM)
    return pl.pallas_call(body, out_shape=[shape] * 4, in_specs=[whole] * 4 + [ANY_SPEC], compiler_params=_params(0),
                          name="adamw_w_in")(recv, w, m, v, after)


def sum_slabs(recv):
    def body(p_ref, o_ref):
        g = p_ref[0]
        for s in range(1, 8):
            g = g + p_ref[s]
        for s in range(8):
            o_ref[s] = g

    return pl.pallas_call(body, out_shape=jax.ShapeDtypeStruct(recv.shape, F32), compiler_params=_params(0), name="sum_slabs")(recv)


SIMPLE = [("norm1_w", 1024), ("ssd_conv_b", 1536), ("ssd_dt_bias", 16), ("ssd_a_log", 16), ("ssd_d", 16), ("ssd_norm_w", 1024),
          ("lru_conv_b", 1024), ("lru_ba", 1024), ("lru_bx", 1024), ("lru_lambda", 1024), ("lru_norm_w", 1024), ("norm2_w", 1024),
          ("final_norm_w", 1024)]
SPECIAL = ["lru_wa", "lru_wx", "meta_tokens", "ssd_conv_w", "lru_conv_w"]
SM_ROWS = 176
SM_WA, SM_WX, SM_META, SM_SCW, SM_LCW, SM_LOSS = 14, 78, 142, 158, 166, 170


def _simple_rows():
    rows, r = {}, 0
    for name, n in SIMPLE:
        rows[name] = r
        r += -(-n // 1024)
    return rows


def adamw_small(sm, special_g, ws, ms, vs):
    rows = _simple_rows()
    ns, nx = len(SIMPLE), len(SPECIAL)

    def body(*refs):
        sm_ref = refs[0]
        gx = refs[1:1 + nx]
        wr = refs[1 + nx:1 + nx + ns + nx]
        mr = refs[1 + nx + ns + nx:1 + nx + 2 * (ns + nx)]
        vr = refs[1 + nx + 2 * (ns + nx):1 + nx + 3 * (ns + nx)]
        outs = refs[1 + nx + 3 * (ns + nx):]
        o = 0
        for k, (name, n) in enumerate(SIMPLE):
            r0 = rows[name]
            for c0 in range(0, n, 1024):
                wd = min(1024, n - c0)
                g = sm_ref[r0 + c0 // 1024:r0 + c0 // 1024 + 1, 0:wd]
                sl = (slice(None), slice(c0, c0 + wd))
                d, m2, v2 = _adamw(wr[k][sl], g, mr[k][sl], vr[k][sl])
                outs[o][sl] = g
                outs[o + 1][sl] = d
                outs[o + 2][sl] = m2
                outs[o + 3][sl] = v2
            o += 4
        for k in range(nx):
            d, m2, v2 = _adamw(wr[ns + k][...], gx[k][...], mr[ns + k][...], vr[ns + k][...])
            outs[o][...] = d
            outs[o + 1][...] = m2
            outs[o + 2][...] = v2
            o += 3

    out_shape = []
    for k in range(ns):
        out_shape += [jax.ShapeDtypeStruct(ws[k].shape, F32)] * 4
    for k in range(nx):
        out_shape += [jax.ShapeDtypeStruct(ws[ns + k].shape, F32)] * 3
    return pl.pallas_call(body, out_shape=out_shape, compiler_params=_params(0), name="adamw_small")(sm, *special_g, *ws, *ms, *vs)


def _place():
    return lax.axis_index("x"), lax.axis_index("y"), lax.axis_index("c")


def _index(px, py, pc):
    return 4 * px + 2 * py + pc


def all_gather(name, shards):
    n = len(shards)
    hbm = pl.BlockSpec(memory_space=pl.ANY)

    def body(*refs):
        ins, outs = refs[:n], refs[n:2 * n]
        send_sems, recv_sems, local_sems = refs[2 * n:]
        x, y, c = _place()
        me, sibling = (x, y, c), (x, y, 1 - c)
        chips = [(1 - x, y), (x, 1 - y), (1 - x, 1 - y)]

        def copy(i, k, block, to, src=None):
            dst = outs[i].at[_index(*block)]
            return pltpu.make_async_remote_copy(src_ref=dst if src is None else src, dst_ref=dst, send_sem=send_sems.at[7 * i + k],
                                                recv_sem=recv_sems.at[7 * i + k], device_id=to, device_id_type=MESH)

        mine = [pltpu.make_async_copy(ins[i], outs[i].at[_index(*me)], local_sems.at[i]) for i in range(n)]
        for cp in mine:
            cp.start()
        first = []
        for i in range(n):
            first += [copy(i, 1 + j, me, (*chip, c), src=ins[i]) for j, chip in enumerate(chips)]
            first.append(copy(i, 0, me, sibling, src=ins[i]))
        for cp in first:
            cp.start()
        passed = []
        for i in range(n):
            for j, chip in enumerate(chips):
                copy(i, 1 + j, (*chip, c), me).wait_recv()
                cp = copy(i, 4 + j, (*chip, c), sibling)
                cp.start()
                passed.append(cp)
        for i in range(n):
            copy(i, 0, sibling, me).wait_recv()
            for j, chip in enumerate(chips):
                copy(i, 4 + j, (*chip, 1 - c), me).wait_recv()
        for cp in first + passed:
            cp.wait_send()
        for cp in mine:
            cp.wait()

    return pl.pallas_call(
        body, in_specs=[hbm] * n, out_specs=[hbm] * n,
        out_shape=[jax.ShapeDtypeStruct((8,) + s.shape, s.dtype) for s in shards],
        scratch_shapes=[pltpu.SemaphoreType.DMA((7 * n,)), pltpu.SemaphoreType.DMA((7 * n,)), pltpu.SemaphoreType.DMA((n,))],
        name=name)(*shards)


HBM_SPEC = pl.BlockSpec(memory_space=pltpu.HBM)
SEM_SPEC = pl.BlockSpec(memory_space=pltpu.SEMAPHORE)
EFFECT = pltpu.SideEffectType.DATAFLOW_SIDE_EFFECTING


def _peers(x, y, c):
    return [((1 - x) if k & 4 else x, (1 - y) if k & 2 else y, (1 - c) if k & 1 else c) for k in range(1, 8)]


def _pieces(rows):
    for n in (4, 2):
        if rows % (16 * n) == 0:
            return [(r * (rows // n), rows // n) for r in range(n)]
    return [(0, rows)]


def _peer_copies(src, land, send_sems, recv_sems, k, peer, mine):
    block = src.at[_index(*peer)]
    return [pltpu.make_async_remote_copy(src_ref=block.at[pl.ds(r0, nr)], dst_ref=land.at[mine, pl.ds(r0, nr)], send_sem=send_sems.at[k],
                                         recv_sem=recv_sems.at[k], device_id=peer, device_id_type=MESH)
            for r0, nr in _pieces(block.shape[0])]


OWN = 7


def _own_copy(src, land, send_sems, mine):
    return pltpu.make_async_copy(src.at[mine], land.at[mine], send_sems.at[OWN])


def copies_start(name, srcs):
    n = len(srcs)

    def body(*refs):
        ins, lands = refs[:n], refs[n:2 * n]
        sends, recvs = refs[2 * n:3 * n], refs[3 * n:4 * n]
        token = refs[-1]
        x, y, c = _place()
        mine = _index(x, y, c)
        for i in range(n):
            per_peer = [_peer_copies(ins[i], lands[i], sends[i], recvs[i], k, peer, mine) for k, peer in enumerate(_peers(x, y, c))]
            for piece in zip(*per_peer):
                for cp in piece:
                    cp.start()
            _own_copy(ins[i], lands[i], sends[i], mine).start()
        token[...] = jnp.zeros_like(token)

    res = pl.pallas_call(
        body, name=name,
        out_shape=([pltpu.SemaphoreType.DMA((8,))] * n + [pltpu.SemaphoreType.DMA((7,))] * n + [pltpu.HBM(s.shape, s.dtype) for s in srcs] * 2
                   + [jax.ShapeDtypeStruct((8, 128), F32)]),
        in_specs=[HBM_SPEC] * (2 * n),
        out_specs=[SEM_SPEC] * (2 * n) + [HBM_SPEC] * (2 * n) + [pl.BlockSpec(memory_space=pltpu.VMEM)],
        input_output_aliases={i: 2 * n + i for i in range(2 * n)},
        compiler_params=pltpu.CompilerParams(has_side_effects=EFFECT),
    )(*[pltpu.with_memory_space_constraint(s, pltpu.HBM) for s in srcs],
      *[pltpu.with_memory_space_constraint(lax.empty(s.shape, s.dtype), pltpu.HBM) for s in srcs])
    return [(res[i], res[n + i], res[2 * n + i], res[3 * n + i]) for i in range(n)], res[-1]


def copies_wait(name, started, after):
    n = len(started)

    def body(*refs):
        ins, lands = refs[:n], refs[n:2 * n]
        sends, recvs = refs[2 * n:3 * n], refs[3 * n:4 * n]
        x, y, c = _place()
        mine = _index(x, y, c)
        for i in range(n):
            for k, peer in enumerate(_peers(x, y, c)):
                arrival = pltpu.make_async_remote_copy(src_ref=ins[i].at[mine], dst_ref=lands[i].at[_index(*peer)],
                                                       send_sem=sends[i].at[k], recv_sem=recvs[i].at[k], device_id=peer, device_id_type=MESH)
                arrival.wait_send()
                arrival.wait_recv()
            _own_copy(ins[i], lands[i], sends[i], mine).wait()

    srcs = [s[2] for s in started]
    lands = [s[3] for s in started]
    res = pl.pallas_call(
        body, name=name,
        out_shape=[pltpu.HBM(s.shape, s.dtype) for s in srcs] + [pltpu.HBM(z.shape, z.dtype) for z in lands],
        in_specs=[HBM_SPEC] * (2 * n) + [SEM_SPEC] * (2 * n) + [ANY_SPEC] * len(after),
        out_specs=[HBM_SPEC] * (2 * n),
        input_output_aliases={i: i for i in range(2 * n)},
        compiler_params=pltpu.CompilerParams(has_side_effects=EFFECT),
    )(*srcs, *lands, *[s[0] for s in started], *[s[1] for s in started], *after)
    return list(res[n:])


def _hop(src, land, send_sems, recv_sems, k, block, to):
    dst = land.at[_index(*block)]
    return pltpu.make_async_remote_copy(src_ref=dst if src is None else src, dst_ref=dst, send_sem=send_sems.at[k], recv_sem=recv_sems.at[k],
                                        device_id=to, device_id_type=MESH)


def _own_block(src, land, send_sems, mine):
    return pltpu.make_async_copy(src, land.at[mine], send_sems.at[4])


def _other_chips(x, y):
    return [(1 - x, y), (x, 1 - y), (1 - x, 1 - y)]


def gather_start(name, shards, through):
    n, m = len(shards), len(through)

    def body(*refs):
        ins, lands = refs[:n], refs[n:2 * n]
        sends, recvs = refs[2 * n + m:3 * n + m], refs[3 * n + m:4 * n + m]
        x, y, c = _place()
        for i in range(n):
            for j, chip in enumerate(_other_chips(x, y)):
                _hop(ins[i], lands[i], sends[i], recvs[i], 1 + j, (x, y, c), (*chip, c)).start()
            _hop(ins[i], lands[i], sends[i], recvs[i], 0, (x, y, c), (x, y, 1 - c)).start()
            _own_block(ins[i], lands[i], sends[i], _index(x, y, c)).start()

    own, passing = pltpu.SemaphoreType.DMA((5,)), pltpu.SemaphoreType.DMA((3,))
    zones = [jax.ShapeDtypeStruct((8,) + s.shape, s.dtype) for s in shards]
    res = pl.pallas_call(
        body, name=name,
        out_shape=([own] * (2 * n) + [passing] * (2 * n) + [pltpu.HBM(s.shape, s.dtype) for s in shards]
                   + [pltpu.HBM(z.shape, z.dtype) for z in zones] + [pltpu.HBM(t.shape, t.dtype) for t in through]),
        in_specs=[HBM_SPEC] * (2 * n + m),
        out_specs=[SEM_SPEC] * (4 * n) + [HBM_SPEC] * (2 * n + m),
        input_output_aliases={i: 4 * n + i for i in range(2 * n + m)},
        compiler_params=pltpu.CompilerParams(has_side_effects=EFFECT),
    )(*[pltpu.with_memory_space_constraint(s, pltpu.HBM) for s in shards],
      *[pltpu.with_memory_space_constraint(lax.empty(z.shape, z.dtype), pltpu.HBM) for z in zones],
      *[pltpu.with_memory_space_constraint(t, pltpu.HBM) for t in through])
    return [[res[4 * n + i], res[5 * n + i], res[i], res[n + i], res[2 * n + i], res[3 * n + i]] for i in range(n)], list(res[6 * n:])


def gather_stage(name, pass_on, finish, after):
    arrays = pass_on + finish
    n = len(arrays)

    def body(*refs):
        ins, lands = refs[:n], refs[n:2 * n]
        sems = [refs[(2 + q) * n:(3 + q) * n] for q in range(4)]
        x, y, c = _place()
        me, sibling = (x, y, c), (x, y, 1 - c)
        for i in range(len(pass_on)):
            send, recv, send_on, recv_on = (q[i] for q in sems)
            for j, chip in enumerate(_other_chips(x, y)):
                _hop(None, lands[i], send, recv, 1 + j, (*chip, c), me).wait_recv()
                _hop(None, lands[i], send_on, recv_on, j, (*chip, c), sibling).start()
        for i in range(len(pass_on), n):
            send, recv, send_on, recv_on = (q[i] for q in sems)
            _hop(ins[i], lands[i], send, recv, 0, sibling, me).wait_recv()
            for j, chip in enumerate(_other_chips(x, y)):
                _hop(None, lands[i], send_on, recv_on, j, (*chip, 1 - c), me).wait_recv()
            _hop(ins[i], lands[i], send, recv, 0, me, sibling).wait_send()
            _own_block(ins[i], lands[i], send, _index(*me)).wait()
            for j, chip in enumerate(_other_chips(x, y)):
                _hop(ins[i], lands[i], send, recv, 1 + j, me, (*chip, c)).wait_send()
                _hop(None, lands[i], send_on, recv_on, j, (*chip, c), sibling).wait_send()
        refs[-1][...] = jnp.zeros_like(refs[-1])

    res = pl.pallas_call(
        body, name=name,
        out_shape=([pltpu.HBM(a[0].shape, a[0].dtype) for a in arrays] + [pltpu.HBM(a[1].shape, a[1].dtype) for a in arrays]
                   + [jax.ShapeDtypeStruct((8, 128), F32)]),
        in_specs=[HBM_SPEC] * (2 * n) + [SEM_SPEC] * (4 * n) + [ANY_SPEC],
        out_specs=[HBM_SPEC] * (2 * n) + [pl.BlockSpec(memory_space=pltpu.VMEM)],
        input_output_aliases={i: i for i in range(2 * n)},
        compiler_params=pltpu.CompilerParams(has_side_effects=EFFECT),
    )(*[a[0] for a in arrays], *[a[1] for a in arrays], *[a[2 + q] for q in range(4) for a in arrays], after)
    for i, a in enumerate(arrays):
        a[0], a[1] = res[i], res[n + i]
    return [a[1] for a in finish], res[-1]


WEIGHTS = ["meta_tokens", "norm1_w", "w_in", "ssd_conv_w", "ssd_conv_b", "ssd_dt_bias", "ssd_a_log", "ssd_d", "ssd_norm_w", "lru_conv_w",
           "lru_conv_b", "lru_wa", "lru_ba", "lru_wx", "lru_bx", "lru_lambda", "lru_norm_w", "w_out", "norm2_w", "w_gate", "w_up", "w_down",
           "final_norm_w"]
BIG = ["w_in", "w_out", "w_gate", "w_up", "w_down"]
COLUMN_SHARDED = ["w_in", "w_gate", "w_up"]


def _pair_blocks(w):
    w = w.reshape(8, 2, 64, 64)
    z = jnp.zeros((8, 64, 64), w.dtype)
    return jnp.concatenate([jnp.concatenate([w[:, 0], z], axis=2), jnp.concatenate([z, w[:, 1]], axis=2)], axis=1)


def _unpair_blocks(w2):
    return jnp.stack([w2[:, :64, :64], w2[:, 64:, 64:]], axis=1).reshape(16, 64, 64)


def _per_group(v):
    return jnp.pad(v.reshape(2, 1, 8), ((0, 0), (0, 0), (0, 120)))


def _pad_cols(v, n):
    return jnp.pad(v, ((0, 0), (0, n - v.shape[1])))


def local_step(x, target, meta, ssd_cw, lru_cw, w_in_shards, fetch, send, p):
    bias2, alog2, d2 = _per_group(p["ssd_dt_bias"]), _per_group(p["ssd_a_log"]), _per_group(p["ssd_d"])
    wa2 = _pair_blocks(p["lru_wa"]).astype(BF)
    wx2 = _pair_blocks(p["lru_wx"]).astype(BF)
    lru = (lru_cw, p["lru_conv_b"], wa2, p["lru_ba"], wx2, p["lru_bx"], p["lru_lambda"])

    proj, dt_raw, u1, h0, w_in, w_dt = in_proj(x, meta, p["norm1_w"], w_in_shards)
    yn_ssd, y_pre, h_prev = ssd_fwd(proj, dt_raw, ssd_cw, p["ssd_conv_b"], bias2, alog2, d2, p["ssd_norm_w"])
    _, moved = fetch([], yn_ssd)
    hseq, a = lru_fwd(proj, *lru, moved)
    (w_out,), _ = fetch(["w_out"], hseq)
    h1, cat = out_proj(yn_ssd, proj, hseq, p["lru_norm_w"], w_out, h0)
    (w_gate, w_up), _ = fetch(["w_gate", "w_up"], h1)
    gt, up, act, u2 = gate_up(h1, p["norm2_w"], w_gate, w_up)
    (w_down,), _ = fetch(["w_down"], act)
    dh2, dh2_b, loss, d_fnw = down_loss(act, w_down, h1, target, p["final_norm_w"])

    dgt, dup, g_down, g_gate, g_up = swiglu_bwd(dh2_b, w_down, gt, up, act, u2)
    sent = send({"w_down": g_down, "w_gate": g_gate, "w_up": g_up})
    dh1, dh1_b, d_n2 = gate_up_bwd(dgt, dup, w_gate, w_up, h1, p["norm2_w"], dh2, sent)
    sent = send({"w_out": weight_grad("dw_out", cat, dh1_b)})
    dyn, dh_out, dg_b, d_lnw = out_proj_bwd(dh1_b, w_out, proj, hseq, p["lru_norm_w"], sent)

    dxl_b, d_lcw, d_lcb, dwa2, d_ba, dwx2, d_bx, d_lam = lru_bwd(dh_out, a, hseq, proj, *lru)
    dz_b, dxbc_b, ddt_b, dpar, d_snw, d_scw, d_scb = ssd_bwd(dyn, proj, dt_raw, ssd_cw, p["ssd_conv_b"], y_pre, h_prev, bias2, alog2, d2,
                                                             p["ssd_norm_w"], sent)
    sent = send({"w_in": in_weight_grad([dz_b, dxbc_b, dg_b, dxl_b], [0, SSD_W, 2576, 2576 + LRU_W], ddt_b, u1)})
    grad_x, d_meta, d_n1 = in_proj_bwd(dz_b, dg_b, dxl_b, dxbc_b, ddt_b, w_in, w_dt, h0, p["norm1_w"], dh1, sent)
    small = {"norm1_w": d_n1, "ssd_conv_b": d_scb, "ssd_dt_bias": dpar[:, 0, :8].reshape(1, 16), "ssd_a_log": dpar[:, 1, :8].reshape(1, 16),
             "ssd_d": dpar[:, 2, :8].reshape(1, 16), "ssd_norm_w": d_snw, "lru_conv_b": d_lcb, "lru_ba": d_ba, "lru_bx": d_bx,
             "lru_lambda": d_lam, "lru_norm_w": d_lnw, "norm2_w": d_n2, "final_norm_w": d_fnw,
             "lru_wa": _unpair_blocks(dwa2), "lru_wx": _unpair_blocks(dwx2), "meta_tokens": d_meta,
             "ssd_conv_w": d_scw, "lru_conv_w": d_lcw}
    return loss, grad_x, small


def _pack_small(small, loss):
    rows = [_pad_cols(small[name], -(-n // 1024) * 1024).reshape(-1, 1024) for name, n in SIMPLE]
    rows += [small["lru_wa"].reshape(64, 1024), small["lru_wx"].reshape(64, 1024), small["meta_tokens"],
             _pad_cols(small["ssd_conv_w"], 2048).reshape(8, 1024), small["lru_conv_w"], _pad_cols(loss[:, 0:1], 1024)]
    sm = jnp.concatenate(rows, axis=0)
    return jnp.pad(sm, ((0, SM_ROWS - sm.shape[0]), (0, 0)))


def _slabs(g):
    return g.reshape(8, g.shape[0] // 8, g.shape[1])


def _unslab(g):
    return g.reshape(8 * g.shape[1], g.shape[2])


def kernel(x, meta_tokens, norm1_w, w_in, ssd_conv_w, ssd_conv_b, ssd_dt_bias, ssd_a_log, ssd_d, ssd_norm_w, lru_conv_w, lru_conv_b, lru_wa, lru_ba, lru_wx, lru_bx, lru_lambda, lru_norm_w, w_out, norm2_w, w_gate, w_up, w_down, final_norm_w, loss_target, m_meta_tokens, m_norm1_w, m_w_in, m_ssd_conv_w, m_ssd_conv_b, m_ssd_dt_bias, m_ssd_a_log, m_ssd_d, m_ssd_norm_w, m_lru_conv_w, m_lru_conv_b, m_lru_wa, m_lru_ba, m_lru_wx, m_lru_bx, m_lru_lambda, m_lru_norm_w, m_w_out, m_norm2_w, m_w_gate, m_w_up, m_w_down, m_final_norm_w, v_meta_tokens, v_norm1_w, v_w_in, v_ssd_conv_w, v_ssd_conv_b, v_ssd_dt_bias, v_ssd_a_log, v_ssd_d, v_ssd_norm_w, v_lru_conv_w, v_lru_conv_b, v_lru_wa, v_lru_ba, v_lru_wx, v_lru_bx, v_lru_lambda, v_lru_norm_w, v_w_out, v_norm2_w, v_w_gate, v_w_up, v_w_down, v_final_norm_w):
    w = dict(meta_tokens=meta_tokens, norm1_w=norm1_w, w_in=w_in[0], ssd_conv_w=ssd_conv_w[0], ssd_conv_b=ssd_conv_b, ssd_dt_bias=ssd_dt_bias,
             ssd_a_log=ssd_a_log, ssd_d=ssd_d, ssd_norm_w=ssd_norm_w, lru_conv_w=lru_conv_w[0], lru_conv_b=lru_conv_b, lru_wa=lru_wa[0],
             lru_ba=lru_ba, lru_wx=lru_wx[0], lru_bx=lru_bx, lru_lambda=lru_lambda, lru_norm_w=lru_norm_w, w_out=w_out[0], norm2_w=norm2_w,
             w_gate=w_gate[0], w_up=w_up[0], w_down=w_down[0], final_norm_w=final_norm_w.reshape(1, D))
    m = dict(meta_tokens=m_meta_tokens, norm1_w=m_norm1_w, w_in=m_w_in[0], ssd_conv_w=m_ssd_conv_w[0], ssd_conv_b=m_ssd_conv_b,
             ssd_dt_bias=m_ssd_dt_bias, ssd_a_log=m_ssd_a_log, ssd_d=m_ssd_d, ssd_norm_w=m_ssd_norm_w, lru_conv_w=m_lru_conv_w[0],
             lru_conv_b=m_lru_conv_b, lru_wa=m_lru_wa[0], lru_ba=m_lru_ba, lru_wx=m_lru_wx[0], lru_bx=m_lru_bx, lru_lambda=m_lru_lambda,
             lru_norm_w=m_lru_norm_w, w_out=m_w_out[0], norm2_w=m_norm2_w, w_gate=m_w_gate[0], w_up=m_w_up[0], w_down=m_w_down[0],
             final_norm_w=m_final_norm_w.reshape(1, D))
    v = dict(meta_tokens=v_meta_tokens, norm1_w=v_norm1_w, w_in=v_w_in[0], ssd_conv_w=v_ssd_conv_w[0], ssd_conv_b=v_ssd_conv_b,
             ssd_dt_bias=v_ssd_dt_bias, ssd_a_log=v_ssd_a_log, ssd_d=v_ssd_d, ssd_norm_w=v_ssd_norm_w, lru_conv_w=v_lru_conv_w[0],
             lru_conv_b=v_lru_conv_b, lru_wa=v_lru_wa[0], lru_ba=v_lru_ba, lru_wx=v_lru_wx[0], lru_bx=v_lru_bx, lru_lambda=v_lru_lambda,
             lru_norm_w=v_lru_norm_w, w_out=v_w_out[0], norm2_w=v_norm2_w, w_gate=v_w_gate[0], w_up=v_w_up[0], w_down=v_w_down[0],
             final_norm_w=v_final_norm_w.reshape(1, D))
    shapes = dict(meta_tokens=meta_tokens.shape, norm1_w=norm1_w.shape, w_in=w_in.shape, ssd_conv_w=ssd_conv_w.shape,
                  ssd_conv_b=ssd_conv_b.shape, ssd_dt_bias=ssd_dt_bias.shape, ssd_a_log=ssd_a_log.shape, ssd_d=ssd_d.shape,
                  ssd_norm_w=ssd_norm_w.shape, lru_conv_w=lru_conv_w.shape, lru_conv_b=lru_conv_b.shape, lru_wa=lru_wa.shape,
                  lru_ba=lru_ba.shape, lru_wx=lru_wx.shape, lru_bx=lru_bx.shape, lru_lambda=lru_lambda.shape, lru_norm_w=lru_norm_w.shape,
                  w_out=w_out.shape, norm2_w=norm2_w.shape, w_gate=w_gate.shape, w_up=w_up.shape, w_down=w_down.shape,
                  final_norm_w=final_norm_w.shape)
    me = _index(*_place())
    for n in COLUMN_SHARDED:
        w[n], m[n], v[n] = w[n].T, m[n].T, v[n].T

    small_shard = jnp.concatenate([w["meta_tokens"], _pad_cols(w["ssd_conv_w"], 256).reshape(8, 128), w["lru_conv_w"],
                                   jnp.zeros((4, 128), F32)], axis=0)
    g_in, gs = all_gather("gather_w_in", [w["w_in"].astype(BF), small_shard])
    later = ["w_out", "w_gate", "w_up", "w_down"]
    started, (g_in, gs) = gather_start("gather_rest_start", [w[n].astype(BF) for n in later], [g_in, gs])
    started = dict(zip(later, started))
    ssd_cw = gs[:, 16:24].reshape(8, 4, 256)[:, :, :192].transpose(1, 0, 2).reshape(4, XBC)
    lru_cw = gs[:, 24:28].transpose(1, 0, 2).reshape(4, LRU_W)

    def fetch(names, after):
        pass_on = {"w_out": ["w_down"], "w_gate": [], "w_down": []}[names[0]] if names else ["w_out", "w_gate", "w_up"]
        got, zero = gather_stage("gather_" + (names[0] + "_wait" if names else "pass_on"), [started[n] for n in pass_on],
                                 [started[n] for n in names], after)
        return [_unslab(g) for g in got], zero

    in_flight = {}

    def send(grads):
        names = list(grads)
        st, zero = copies_start("grads_" + names[0] + "_start", [grads[n] if n == "small" else _slabs(grads[n]) for n in names])
        in_flight.update(zip(names, st))
        return zero

    loss, grad_x, small = local_step(x[0], loss_target[0], gs, ssd_cw, lru_cw, g_in, fetch, send, w)
    send({"small": _pack_small(small, loss).reshape(8, SM_ROWS // 8, 1024)})

    out = {}
    early = ["w_down", "w_gate", "w_up", "w_out"]
    recv = dict(zip(early, copies_wait("grads_early_wait", [in_flight[n] for n in early], [in_flight["small"][2]])))
    for pair in (early[:2], early[2:]):
        done = adamw_shards("adamw_" + pair[0], [recv[n] for n in pair], [w[n] for n in pair], [m[n] for n in pair], [v[n] for n in pair])
        out.update(zip(pair, done))
    recv_in, recv_small = copies_wait("grads_late_wait", [in_flight["w_in"], in_flight["small"]], [out[n][0] for n in early])
    def lines(a):
        return jnp.transpose(a.reshape(D // 128, 128, IN_COLS // 8), (2, 0, 1))

    gathering, zero = copies_start("gather_small_start", [sum_slabs(recv_small)])
    updated = adamw_w_in(recv_in, lines(w_in), lines(m_w_in), lines(v_w_in), zero)
    out["w_in"] = [jnp.transpose(o, (1, 2, 0)).reshape(D, IN_COLS // 8) for o in updated]
    for n in ("w_gate", "w_up"):
        out[n] = [o.T for o in out[n]]
    sm = copies_wait("gather_small_wait", gathering, [updated[0]])[0].reshape(SM_ROWS, 1024)
    special_g =[sm[SM_WA:SM_WA + 64].reshape(16, 64, 64), sm[SM_WX:SM_WX + 64].reshape(16, 64, 64),
                 lax.dynamic_slice(sm[SM_META:SM_META + 16], (0, 128 * me), (16, 128)),
                 lax.dynamic_slice(sm[SM_SCW:SM_SCW + 8].reshape(4, 2048), (0, 192 * me), (4, 192)),
                 lax.dynamic_slice(sm[SM_LCW:SM_LCW + 4], (0, 128 * me), (4, 128))]
    names = [n for n, _ in SIMPLE] + SPECIAL
    res = adamw_small(sm, special_g, [w[n] for n in names], [m[n] for n in names], [v[n] for n in names])
    for k, (n, _) in enumerate(SIMPLE):
        out[n] = res[4 * k:4 * k + 4]
    for k, n in enumerate(SPECIAL):
        o = 4 * len(SIMPLE) + 3 * k
        out[n] = [special_g[k]] + list(res[o:o + 3])
    loss_total = sm[SM_LOSS, 0]
    flat = [loss_total, grad_x[None]]
    for k in range(4):
        flat += [out[n][k].reshape(shapes[n]) for n in WEIGHTS]
    return tuple(flat)
```

```python
import math

import jax
import jax.numpy as jnp
from jax import lax
from jax.experimental import pallas as pl
from jax.experimental.pallas import tpu as pltpu

F32 = jnp.float32
BF = jnp.bfloat16

D = 1024
SEQ = 2048
N_META = 16
Q = 128
NPAD = 112
T = NPAD + N_META + SEQ
NCH = T // Q
RC = 544
D_FF = 2816
SSD_W = 1024
LRU_W = 1024
XBC = 1536
IN_COLS = 4624
PZ, PG, PXL, PXBC = 0, 1024, 2048, 3072
NP_IN = 4608
EPS = 1e-6
LRU_C = 8.0
VMEM_LIMIT = 56 * 1024 * 1024

ADAM_LR, ADAM_B1, ADAM_B2, ADAM_EPS, ADAM_WD, ADAM_STEP = 0.001, 0.9, 0.999, 1e-08, 0.01, 10

NT_DIMS = (((1,), (1,)), ((), ()))
TN_DIMS = (((0,), (0,)), ((), ()))
MESH = pl.DeviceIdType.MESH


def _params(n_grid=1, limit=VMEM_LIMIT):
    return pltpu.CompilerParams(dimension_semantics=("arbitrary",) * n_grid, vmem_limit_bytes=limit)


def _spec(shape, imap, single=False):
    if single:
        return pl.BlockSpec(shape, imap, pipeline_mode=pl.Buffered(1))
    return pl.BlockSpec(shape, imap)


def _sigmoid(x):
    return 0.5 * jnp.tanh(0.5 * x) + 0.5


def _sigmoid_gate(x):
    return 1.0 / (1.0 + jnp.exp(-x))


def _softplus(x):
    return jnp.maximum(x, 0.0) + jnp.log(1.0 + jnp.exp(-jnp.abs(x)))


def _rms_stats(h):
    return lax.rsqrt(jnp.mean(h * h, axis=-1, keepdims=True) + EPS)


def _rms(h, w):
    return (h * _rms_stats(h)) * w


def _rms_bwd(du, h, w):
    r = _rms_stats(h)
    n = h * r
    dn = du * w
    dh = r * (dn - n * jnp.mean(dn * n, axis=-1, keepdims=True))
    return dh, du * n


_G0 = math.sqrt(2.0 / math.pi)


def _gelu(x):
    return 0.5 * x * (1.0 + jnp.tanh(_G0 * (x + 0.044715 * (x * x * x))))


def _gelu_grad(x):
    t = jnp.tanh(_G0 * (x + 0.044715 * (x * x * x)))
    return 0.5 * (1.0 + t) + 0.5 * x * (1.0 - t * t) * (_G0 * (1.0 + 3.0 * 0.044715 * (x * x)))


def _rows(shape, r0=0):
    return lax.broadcasted_iota(jnp.int32, shape, 0) + r0


def _lanes(shape):
    return lax.broadcasted_iota(jnp.int32, shape, 1)


HALO = 8


def _fill_padded(pad_ref, x_ref):
    pad_ref[0:HALO, :] = jnp.zeros((HALO, pad_ref.shape[1]), F32)
    pad_ref[T + HALO:T + 2 * HALO, :] = jnp.zeros((HALO, pad_ref.shape[1]), F32)

    def step(c, carry):
        r0 = pl.multiple_of(c * Q, Q)
        pad_ref[pl.ds(r0 + HALO, Q), :] = x_ref[pl.ds(r0, Q), :].astype(F32)
        return carry

    lax.fori_loop(0, NCH, step, 0)


def _back(pad_ref, r0):
    win = pad_ref[pl.ds(r0, Q + HALO), :]
    return lambda s: win[HALO:, :] if s == 0 else pltpu.roll(win, s, axis=0)[HALO:, :]


def _ahead(pad_ref, r0):
    win = pad_ref[pl.ds(r0 + HALO, Q + HALO), :]
    return lambda s: win[:Q, :] if s == 0 else pltpu.roll(win, Q + HALO - s, axis=0)[:Q, :]


def _conv(back, w, b):
    y = b + w[3:4, :] * back(0)
    for k in range(3):
        y = y + w[k:k + 1, :] * back(3 - k)
    return y


def _conv_bwd_x(ahead, w):
    dx = w[3:4, :] * ahead(0)
    for k in range(3):
        dx = dx + w[k:k + 1, :] * ahead(3 - k)
    return dx


def _conv_bwd_w(dy, back):
    dws = [jnp.sum(dy * back(3 - k), axis=0, keepdims=True) for k in range(4)]
    return jnp.concatenate(dws, axis=0), jnp.sum(dy, axis=0, keepdims=True)


def _chunks(fn, unrolled=False):
    if unrolled:
        for c in range(NCH):
            fn(c * Q)
        return

    def step(c, carry):
        fn(pl.multiple_of(c * Q, Q))
        return carry

    lax.fori_loop(0, NCH, step, 0)


HALF = RC // 2


def _col_tiles(n, tn, fn):
    def step(j, carry):
        fn(pl.multiple_of(j * tn, tn))
        return carry

    lax.fori_loop(0, n // tn, step, 0)


def _rows_spec(cols, block_col=0):
    return _spec((RC, cols), lambda i: (i, block_col))


def _whole(shape):
    return _spec(shape, lambda i: tuple(0 for _ in shape), single=True)


def _vec(cols):
    return _spec((1, cols), lambda i: (0, 0))


def _zero_at_first(*refs):
    @pl.when(pl.program_id(0) == 0)
    def _():
        for r in refs:
            r[...] = jnp.zeros_like(r)


ANY_SPEC = pl.BlockSpec(memory_space=pl.ANY)


def _arriving(src, dst, sems, starts, rows):
    n, ahead = len(starts), 2
    first = pl.program_id(0) == 0

    def piece(k):
        r0 = starts[0]
        for j in range(1, n):
            r0 = jnp.where(k == j, starts[j], r0)
        at = pl.ds(pl.multiple_of(r0, 16), rows)
        return pltpu.make_async_copy(src.at[at], dst.at[at], sems.at[k])

    @pl.when(first)
    def _():
        for k in range(min(ahead, n)):
            piece(k).start()

    def ready(k):
        k = jnp.asarray(k, jnp.int32)

        @pl.when(first)
        def _():
            piece(k).wait()

            @pl.when(k + ahead < n)
            def _():
                piece(k + ahead).start()

    return ready


IN_RUNS = ((PZ, 0, 1024), (PXBC, 1024, XBC), (PG, 2576, 2048))
IN_TILE = 512
IN_TILE_ROWS = [wrow + IN_TILE * j for _, wrow, width in IN_RUNS for j in range(width // IN_TILE)]


def _in_tiles(fn):
    done = 0
    for pcol, wrow, width in IN_RUNS:
        def step(j, carry, pcol=pcol, wrow=wrow, done=done):
            fn(pl.multiple_of(pcol + j * IN_TILE, IN_TILE), pl.multiple_of(wrow + j * IN_TILE, 16), done + j)
            return carry

        lax.fori_loop(0, width // IN_TILE, step, 0)
        done += width // IN_TILE


def in_proj(x, meta, wn, w_shards):
    first = NPAD + N_META
    steps = T // RC
    shard = IN_COLS // 8

    def body(x_hbm, meta_ref, wn_ref, g_hbm, o_ref, dt_ref, u_ref, h_ref, wt_hbm, wdt_ref, raw, w_ref, h_scr, g_sems, h_sems, out_sem):
        i = pl.program_id(0)
        slot = i % 2
        shards = [pltpu.make_async_copy(g_hbm.at[j], raw.at[j], g_sems.at[j]) for j in range(8)]
        head = pltpu.make_async_copy(x_hbm.at[pl.ds(0, RC - first)], h_scr.at[0, pl.ds(first, RC - first)], h_sems.at[0])
        put_back = pltpu.make_async_copy(w_ref, wt_hbm, out_sem)

        def rows_of(step):
            return pltpu.make_async_copy(x_hbm.at[pl.ds(pl.multiple_of(step * RC - first, 32), RC)], h_scr.at[step % 2], h_sems.at[step % 2])

        @pl.when(i == 0)
        def _():
            for cp in shards:
                cp.start()
            head.start()
            h_scr[0, 0:NPAD, :] = jnp.zeros((NPAD, D), F32)
            for j in range(8):
                h_scr[0, NPAD:first, 128 * j:128 * j + 128] = meta_ref[j, 0:N_META, :]

        @pl.when(i + 1 < steps)
        def _():
            rows_of(i + 1).start()

        @pl.when(i == 0)
        def _():
            head.wait()

        @pl.when(i > 0)
        def _():
            rows_of(i).wait()

        h_ref[...] = h_scr[slot]
        for r in (0, HALF):
            u_ref[r:r + HALF, :] = _rms(h_scr[slot, r:r + HALF, :], wn_ref[...]).astype(BF)

        @pl.when(i == 0)
        def _():
            for j, cp in enumerate(shards):
                cp.wait()
                w_ref[shard * j:shard * (j + 1), :] = raw[j]
            put_back.start()
            wdt_ref[...] = jnp.zeros_like(wdt_ref)
            for g in range(2):
                wdt_ref[128 * g:128 * g + 8, :] = w_ref[2560 + 8 * g:2568 + 8 * g, :]

        def tile(pcol, wrow, k):
            o_ref[:, pl.ds(pcol, IN_TILE)] = lax.dot_general(u_ref[...], w_ref[pl.ds(wrow, IN_TILE), :], NT_DIMS,
                                                             preferred_element_type=F32).astype(BF)

        _in_tiles(tile)
        dt_ref[...] = lax.dot_general(u_ref[...], wdt_ref[...], NT_DIMS, preferred_element_type=F32)

        @pl.when(i == steps - 1)
        def _():
            put_back.wait()

    return pl.pallas_call(
        body, grid=(steps,), in_specs=[ANY_SPEC, _spec(meta.shape, lambda i: (0, 0, 0)), _vec(D), ANY_SPEC],
        out_specs=[_rows_spec(NP_IN), _rows_spec(256), _rows_spec(D), _rows_spec(D), ANY_SPEC, _spec((256, D), lambda i: (0, 0))],
        out_shape=[jax.ShapeDtypeStruct((T, NP_IN), BF), jax.ShapeDtypeStruct((T, 256), F32), jax.ShapeDtypeStruct((T, D), BF),
                   jax.ShapeDtypeStruct((T, D), F32), jax.ShapeDtypeStruct((IN_COLS, D), BF), jax.ShapeDtypeStruct((256, D), BF)],
        scratch_shapes=[pltpu.VMEM((8, shard, D), BF), pltpu.VMEM((IN_COLS, D), BF), pltpu.VMEM((2, RC, D), F32),
                        pltpu.SemaphoreType.DMA((8,)), pltpu.SemaphoreType.DMA((2,)), pltpu.SemaphoreType.DMA],
        compiler_params=_params(), name="in_proj")(x, meta, wn, w_shards)


def out_proj(yn_ssd, proj, hseq, lru_nw, w_out, h0):
    def body(y_ref, g_ref, h_ref, wn_ref, w_ref, r_ref, o_ref, cat_ref):
        cat_ref[:, 0:SSD_W] = y_ref[...]
        for r in (0, HALF):
            y = _gelu(g_ref[r:r + HALF, :].astype(F32)) * h_ref[r:r + HALF, :]
            cat_ref[r:r + HALF, SSD_W:] = _rms(y, wn_ref[...]).astype(BF)

        def tile(c0):
            o_ref[:, pl.ds(c0, 512)] = r_ref[:, pl.ds(c0, 512)] + jnp.dot(cat_ref[...], w_ref[:, pl.ds(c0, 512)], preferred_element_type=F32)

        _col_tiles(D, 512, tile)

    return pl.pallas_call(
        body, grid=(T // RC,),
        in_specs=[_rows_spec(SSD_W), _rows_spec(LRU_W, PG // LRU_W), _rows_spec(LRU_W), _vec(LRU_W), _whole((SSD_W + LRU_W, D)), _rows_spec(D)],
        out_specs=[_rows_spec(D), _rows_spec(SSD_W + LRU_W)],
        out_shape=[jax.ShapeDtypeStruct((T, D), F32), jax.ShapeDtypeStruct((T, SSD_W + LRU_W), BF)],
        compiler_params=_params(), name="out_proj")(yn_ssd, proj, hseq, lru_nw, w_out, h0)


def out_proj_bwd(dh1_b, w_out, proj, hseq, lru_nw, after):
    def body(d_ref, w_ref, g_ref, h_ref, wn_ref, _after, dy_ref, dh_ref, dg_ref, dw_ref, dl_scr):
        _zero_at_first(dw_ref)

        def tile(c0):
            dy_ref[:, pl.ds(c0, 512)] = lax.dot_general(d_ref[...], w_ref[pl.ds(c0, 512), :], NT_DIMS, preferred_element_type=F32)
            dl_scr[:, pl.ds(c0, 512)] = lax.dot_general(d_ref[...], w_ref[pl.ds(SSD_W + c0, 512), :], NT_DIMS, preferred_element_type=F32)

        _col_tiles(SSD_W, 512, tile)

        for r in (0, HALF):
            g = g_ref[r:r + HALF, :].astype(F32)
            h = h_ref[r:r + HALF, :]
            ge = _gelu(g)
            dy, dw = _rms_bwd(dl_scr[r:r + HALF, :], ge * h, wn_ref[...])
            dw_ref[...] += jnp.sum(dw, axis=0, keepdims=True)
            dh_ref[r:r + HALF, :] = dy * ge
            dg_ref[r:r + HALF, :] = (dy * h * _gelu_grad(g)).astype(BF)

    return pl.pallas_call(
        body, grid=(T // RC,),
        in_specs=[_rows_spec(D), _whole((SSD_W + LRU_W, D)), _rows_spec(LRU_W, PG // LRU_W), _rows_spec(LRU_W), _vec(LRU_W), ANY_SPEC],
        out_specs=[_rows_spec(SSD_W), _rows_spec(LRU_W), _rows_spec(LRU_W), _vec(LRU_W)],
        out_shape=[jax.ShapeDtypeStruct((T, SSD_W), F32), jax.ShapeDtypeStruct((T, LRU_W), F32), jax.ShapeDtypeStruct((T, LRU_W), BF),
                   jax.ShapeDtypeStruct((1, LRU_W), F32)],
        scratch_shapes=[pltpu.VMEM((RC, LRU_W), F32)],
        compiler_params=_params(), name="out_proj_bwd")(dh1_b, w_out, proj, hseq, lru_nw, after)


def in_proj_bwd(dz, dg, dxl, dxbc, ddt, w_t, w_dt, h0, wn, dh1, after):
    first = NPAD + N_META

    def body(dz_ref, dg_ref, dxl_ref, dxbc_ref, ddt_ref, w_hbm, wdt_ref, h_ref, wn_ref, r_ref, _after, gx_hbm, meta_ref, dw_ref, du_scr, o_ref, sem,
             w_ref, w_sems):
        i = pl.program_id(0)
        ready = _arriving(w_hbm, w_ref, w_sems, IN_TILE_ROWS, IN_TILE)
        _zero_at_first(dw_ref)
        du_scr[...] = jnp.dot(ddt_ref[...], wdt_ref[...], preferred_element_type=F32)
        done = 0
        for d_ref, wrow, width in ((dz_ref, 0, 1024), (dxbc_ref, 1024, XBC), (dg_ref, 2576, 1024), (dxl_ref, 3600, 1024)):
            def step(j, carry, d_ref=d_ref, wrow=wrow, done=done):
                c0 = pl.multiple_of(j * IN_TILE, IN_TILE)
                ready(done + j)
                du_scr[...] += jnp.dot(d_ref[:, pl.ds(c0, IN_TILE)], w_ref[pl.ds(pl.multiple_of(wrow + c0, 16), IN_TILE), :],
                                       preferred_element_type=F32)
                return carry

            lax.fori_loop(0, width // IN_TILE, step, 0)
            done += width // IN_TILE
        for r in (0, HALF):
            dh, dw = _rms_bwd(du_scr[r:r + HALF, :], h_ref[r:r + HALF, :], wn_ref[...])
            dw_ref[...] += jnp.sum(dw, axis=0, keepdims=True)
            o_ref[r:r + HALF, :] = dh + r_ref[r:r + HALF, :]

        @pl.when(i == 0)
        def _():
            meta_ref[...] = o_ref[NPAD:first, :]
            head = pltpu.make_async_copy(o_ref.at[pl.ds(first, RC - first)], gx_hbm.at[pl.ds(0, RC - first)], sem)
            head.start()
            head.wait()

        @pl.when(i > 0)
        def _():
            rest = pltpu.make_async_copy(o_ref, gx_hbm.at[pl.ds(pl.multiple_of(i * RC - first, 32), RC)], sem)
            rest.start()
            rest.wait()

    return pl.pallas_call(
        body, grid=(T // RC,),
        in_specs=[_rows_spec(SSD_W), _rows_spec(LRU_W), _rows_spec(LRU_W), _rows_spec(XBC), _rows_spec(256), ANY_SPEC,
                  _whole((256, D)), _rows_spec(D), _vec(D), _rows_spec(D), ANY_SPEC],
        out_specs=[ANY_SPEC, _spec((N_META, D), lambda i: (0, 0)), _vec(D)],
        out_shape=[jax.ShapeDtypeStruct((SEQ, D), F32), jax.ShapeDtypeStruct((N_META, D), F32), jax.ShapeDtypeStruct((1, D), F32)],
        scratch_shapes=[pltpu.VMEM((RC, D), F32), pltpu.VMEM((RC, D), F32), pltpu.SemaphoreType.DMA,
                        pltpu.VMEM((IN_COLS, D), BF), pltpu.SemaphoreType.DMA((len(IN_TILE_ROWS),))],
        compiler_params=_params(), name="in_proj_bwd")(dz, dg, dxl, dxbc, ddt, w_t, w_dt, h0, wn, dh1, after)


GRAD_TILE = 256


def weight_grad(name, a, u1):
    tm = GRAD_TILE

    def body(a_ref, u_ref, o_ref):
        o_ref[...] = lax.dot_general(a_ref[...], u_ref[...], TN_DIMS, preferred_element_type=F32).astype(BF)

    return pl.pallas_call(
        body, grid=(a.shape[1] // tm,),
        in_specs=[_spec((T, tm), lambda j: (0, j)), _spec((T, D), lambda j: (0, 0), single=True)],
        out_specs=_spec((tm, D), lambda j: (j, 0)),
        out_shape=jax.ShapeDtypeStruct((a.shape[1], D), BF),
        compiler_params=_params(), name=name)(a, u1)


def in_weight_grad(parts, first_rows, ddt, u1):
    tm = GRAD_TILE
    per_row = D // 128
    dt_row, dt_lines = 2560, 8 * per_row
    parts = list(parts) + [ddt]
    first_rows = list(first_rows) + [dt_row]
    tiles = [p.shape[1] // tm for p in parts]
    starts = [sum(tiles[:k]) for k in range(len(parts))]
    last = sum(tiles) - 1

    def body(*refs):
        a_refs, u_ref = refs[:len(parts)], refs[len(parts)]
        o_hbm, mix_scr, stage, sems = refs[len(parts) + 1:]
        step = pl.program_id(0)
        slot = step % 2
        line0 = 0
        for a_ref, start, n, first in zip(a_refs, starts, tiles, first_rows):
            here = (step >= start) & (step < start + n)
            line0 = jnp.where(here, per_row * (first + tm * (step - start)), line0)

            @pl.when(here)
            def _(a_ref=a_ref):
                res = lax.dot_general(a_ref[...], u_ref[...], TN_DIMS, preferred_element_type=F32)
                for q in range(per_row):
                    mix_scr[pl.ds(q, tm, stride=per_row), :] = res[:, 128 * q:128 * q + 128]

        def tile_copy(of_slot, to):
            return pltpu.make_async_copy(stage.at[of_slot], o_hbm.at[pl.ds(to, per_row * tm)], sems.at[of_slot])

        @pl.when(step >= 2)
        def _():
            tile_copy(slot, 0).wait()

        stage[slot] = mix_scr[...].astype(BF)

        @pl.when(step < last)
        def _():
            tile_copy(slot, pl.multiple_of(line0, 128)).start()

        @pl.when(step == last)
        def _():
            halves = [pltpu.make_async_copy(stage.at[slot, pl.ds(128 * per_row * k, dt_lines)],
                                            o_hbm.at[pl.ds(per_row * (dt_row + 8 * k), dt_lines)], sems.at[2 + k]) for k in range(2)]
            for cp in halves:
                cp.start()
            tile_copy(1 - slot, 0).wait()
            for cp in halves:
                cp.wait()

    def tile_of(start, n):
        return lambda j: (0, jnp.clip(j - start, 0, n - 1))

    return pl.pallas_call(
        body, grid=(last + 1,),
        in_specs=[_spec((T, tm), tile_of(s, n)) for s, n in zip(starts, tiles)] + [_spec((T, D), lambda j: (0, 0), single=True)],
        out_specs=ANY_SPEC,
        out_shape=jax.ShapeDtypeStruct((per_row * IN_COLS, 128), BF),
        scratch_shapes=[pltpu.VMEM((per_row * tm, 128), F32), pltpu.VMEM((2, per_row * tm, 128), BF), pltpu.SemaphoreType.DMA((4,))],
        compiler_params=_params(), name="dw_in")(*parts, u1)


def _ssd_chunk_common(row0, dt_ref, b_ref, c_ref, bias, a_neg):
    shape = (Q, Q)
    lane = _lanes(shape)
    sub = _rows(shape)
    live = (_rows(shape, row0) >= NPAD) & (lane < 8)
    dtr = dt_ref[:, :]
    dt = jnp.where(live, _softplus(dtr + bias), 0.0)
    d_a = dt * a_neg
    tri = (sub >= lane).astype(F32)
    cs = jnp.dot(tri, d_a, precision=lax.Precision.HIGHEST, preferred_element_type=F32)
    cs_t = cs.T
    b_f = b_ref[:, :]
    bc = b_f.astype(BF)
    cc = c_ref[:, :].astype(BF)
    cb = lax.dot_general(cc, bc, NT_DIMS, preferred_element_type=F32)
    cs_last = cs[Q - 1:Q, :]
    return dict(lane=lane, sub=sub, live=live, dtr=dtr, dt=dt, cs=cs, cs_t=cs_t, bc=bc, cc=cc, cb=cb, bc_t=b_f.T.astype(BF),
                ecs=jnp.exp(cs), dsm=jnp.exp(cs_last - cs), gam=jnp.exp(cs_last))


def _pair(lane_even, mat, j):
    return jnp.where(lane_even, mat[:, j:j + 1], mat[:, j + 1:j + 2])


def _pair_row(lane_even, mat, j):
    return jnp.where(lane_even[0:1, :], mat[:, j:j + 1], mat[:, j + 1:j + 2])


def _head_decay(cm, j):
    seg = cm["cs"][:, j:j + 1] - cm["cs_t"][j:j + 1, :]
    return jnp.exp(jnp.where(cm["sub"] >= cm["lane"], seg, -jnp.inf))


def _head_decay_t(cm, j):
    seg = cm["cs_t"][j:j + 1, :] - cm["cs"][:, j:j + 1]
    return jnp.exp(jnp.where(cm["lane"] >= cm["sub"], seg, -jnp.inf))


def _conv_window(raw_ref, halo_ref, pad_scr):
    pad_scr[0:HALO, :] = halo_ref[...].astype(F32)[halo_ref.shape[0] - HALO:, :]
    pad_scr[HALO:HALO + Q, :] = raw_ref[...].astype(F32)
    win = pad_scr[...]
    return lambda s: win[HALO:, :] if s == 0 else pltpu.roll(win, s, axis=0)[HALO:, :]


def _xbc_cols(g):
    return slice(512 * g, 512 * g + 512), slice(SSD_W + 128 * g, SSD_W + 128 * g + 128), slice(SSD_W + 256 + 128 * g, SSD_W + 384 + 128 * g)


def ssd_fwd(proj, dt_raw, conv_w, conv_b, dt_bias2, a_log2, d2, norm_w):
    def body(raw_ref, halo_ref, dt_all, z_all, cw_ref, cb_ref, bias_all, alog_all, d_all, nw_all, yn_all, y_all, hp_all,
             h_all, pad_scr, act_scr):
        @pl.when(pl.program_id(0) == 0)
        def _():
            h_all[...] = jnp.zeros_like(h_all)

        pre = _conv(_conv_window(raw_ref, halo_ref, pad_scr), cw_ref[...], cb_ref[...])
        act_scr[...] = pre * _sigmoid(pre)
        for g in range(2):
            wide, thin = slice(512 * g, 512 * g + 512), slice(128 * g, 128 * g + 128)
            xs, bs, cs = _xbc_cols(g)
            group(act_scr.at[:, xs], act_scr.at[:, bs], act_scr.at[:, cs], dt_all.at[:, thin], z_all.at[:, wide], bias_all.at[g],
                  alog_all.at[g], d_all.at[g], nw_all.at[:, wide], yn_all.at[:, wide], y_all.at[:, wide], hp_all.at[g, 0], h_all.at[g])

    def group(x_ref, b_ref, c_ref, dt_ref, z_ref, bias_ref, alog_ref, d_ref, nw_ref, yn_ref, y_ref, hp_ref, h_scr):
        bias = bias_ref[...]
        a_neg = -jnp.exp(alog_ref[...])
        dsk = d_ref[...]
        cm = _ssd_chunk_common(pl.program_id(0) * Q, dt_ref, b_ref, c_ref, bias, a_neg)
        lane_even = cm["lane"] < 64
        for p in range(4):
            je, jo = 2 * p, 2 * p + 1
            xp = x_ref[:, 128 * p:128 * p + 128]
            xdt = xp * _pair(lane_even, cm["dt"], je)
            xdt_b = xdt.astype(BF)
            m_e = (cm["cb"] * _head_decay(cm, je)).astype(BF)
            m_o = (cm["cb"] * _head_decay(cm, jo)).astype(BF)
            zero = jnp.zeros_like(xdt_b)
            yd = (jnp.dot(m_e, jnp.where(lane_even, xdt_b, zero), preferred_element_type=F32)
                  + jnp.dot(m_o, jnp.where(lane_even, zero, xdt_b), preferred_element_type=F32))
            hp = h_scr[p]
            hp_ref[p] = hp
            yo = jnp.dot(cm["cc"], hp.astype(BF), preferred_element_type=F32) * _pair(lane_even, cm["ecs"], je)
            y_ref[:, 128 * p:128 * p + 128] = yd + yo + xp * _pair_row(lane_even, dsk, je)
            st = jnp.dot(cm["bc_t"], (xdt * _pair(lane_even, cm["dsm"], je)).astype(BF), preferred_element_type=F32)
            h_scr[p] = hp * _pair_row(lane_even, cm["gam"], je) + st
        zc = z_ref[:, :].astype(F32)
        gated = y_ref[:, :] * (zc * _sigmoid(zc))
        yn_ref[:, :] = _rms(gated, nw_ref[...]).astype(BF)

    par = _spec((2, 1, 128), lambda c: (0, 0, 0))
    wide = _spec((Q, SSD_W), lambda c: (c, 0))
    xbc = PXBC // XBC
    halo = 2 * HALO
    return pl.pallas_call(
        body, grid=(NCH,),
        in_specs=[_spec((Q, XBC), lambda c: (c, xbc)), _spec((halo, XBC), lambda c: (jnp.maximum(c * (Q // halo) - 1, 0), xbc)),
                  _spec((Q, 256), lambda c: (c, 0)), wide, _spec((4, XBC), lambda c: (0, 0)), _spec((1, XBC), lambda c: (0, 0)),
                  par, par, par, _spec((1, SSD_W), lambda c: (0, 0))],
        out_specs=[wide, wide, _spec((2, 1, 4, 128, 128), lambda c: (0, c, 0, 0, 0))],
        out_shape=[jax.ShapeDtypeStruct((T, SSD_W), BF), jax.ShapeDtypeStruct((T, SSD_W), F32),
                   jax.ShapeDtypeStruct((2, NCH, 4, 128, 128), F32)],
        scratch_shapes=[pltpu.VMEM((2, 4, 128, 128), F32), pltpu.VMEM((Q + HALO, XBC), F32), pltpu.VMEM((Q, XBC), F32)],
        compiler_params=_params(), name="ssd_fwd")(proj, proj, dt_raw, proj, conv_w, conv_b, dt_bias2, a_log2, d2, norm_w)


def ssd_bwd(dyn, proj, dt_raw, conv_w, conv_b, y_pre, h_prev, dt_bias2, a_log2, d2, norm_w, after):
    def body(dyn_all, raw_ref, halo_ref, dt_all, z_all, y_all, hp_all, cw_ref, cb_ref, bias_all, alog_all, d_all, nw_all, _after,
             dz_all, dxbc_ref, ddt_all, dpar_all, dnw_all, dcw_ref, dcb_ref, dh_all, acc_all, pad_scr, act_scr, dsilu_scr, dact_scr, dpad_scr):
        @pl.when(pl.program_id(0) == 0)
        def _():
            dh_all[...] = jnp.zeros_like(dh_all)
            acc_all[...] = jnp.zeros_like(acc_all)
            dnw_all[...] = jnp.zeros_like(dnw_all)
            dcw_ref[...] = jnp.zeros_like(dcw_ref)
            dcb_ref[...] = jnp.zeros_like(dcb_ref)
            dpad_scr[Q:Q + HALO, :] = jnp.zeros((HALO, XBC), F32)

        back = _conv_window(raw_ref, halo_ref, pad_scr)
        pre = _conv(back, cw_ref[...], cb_ref[...])
        sg = _sigmoid(pre)
        act_scr[...] = pre * sg
        dsilu_scr[...] = sg * (1.0 + pre * (1.0 - sg))
        for g in range(2):
            wide, thin = slice(512 * g, 512 * g + 512), slice(128 * g, 128 * g + 128)
            xs, bs, cs = _xbc_cols(g)
            group(dyn_all.at[:, wide], act_scr.at[:, xs], act_scr.at[:, bs], act_scr.at[:, cs], dt_all.at[:, thin], z_all.at[:, wide],
                  y_all.at[:, wide], hp_all.at[g, 0], bias_all.at[g], alog_all.at[g], d_all.at[g], nw_all.at[:, wide],
                  dz_all.at[:, wide], dact_scr.at[:, xs], dact_scr.at[:, bs], dact_scr.at[:, cs], ddt_all.at[:, thin], dpar_all.at[g],
                  dnw_all.at[:, wide], dh_all.at[g], acc_all.at[g])
        dpre = dact_scr[...] * dsilu_scr[...]
        dcw, dcb = _conv_bwd_w(dpre, back)
        dcw_ref[...] += dcw
        dcb_ref[...] += dcb
        dpad_scr[0:Q, :] = dpre
        win = dpad_scr[...]
        dxbc_ref[...] = _conv_bwd_x(lambda s: win[:Q, :] if s == 0 else pltpu.roll(win, Q + HALO - s, axis=0)[:Q, :], cw_ref[...]).astype(BF)
        dpad_scr[Q:Q + HALO, :] = dpre[0:HALO, :]

    def group(dyn_ref, x_ref, b_ref, c_ref, dt_ref, z_ref, y_ref, hp_ref, bias_ref, alog_ref, d_ref, nw_ref,
              dz_ref, dx_ref, db_ref, dc_ref, ddt_ref, dpar_ref, dnw_ref, dh_scr, acc_scr):
        ci = pl.program_id(0)
        bias = bias_ref[...]
        a_neg = -jnp.exp(alog_ref[...])
        dsk = d_ref[...]
        cm = _ssd_chunk_common((NCH - 1 - ci) * Q, dt_ref, b_ref, c_ref, bias, a_neg)
        lane, sub = cm["lane"], cm["sub"]
        lane_even = lane < 64
        cc_t = c_ref[:, :].T.astype(BF)
        cb_t = lax.dot_general(cm["bc"], cm["cc"], NT_DIMS, preferred_element_type=F32)
        zc = z_ref[:, :].astype(F32)
        yc = y_ref[:, :]
        sg = _sigmoid(zc)
        sz = zc * sg
        dgated, dnw = _rms_bwd(dyn_ref[:, :], yc * sz, nw_ref[...])
        dnw_ref[...] += jnp.sum(dnw, axis=0, keepdims=True)
        dz_ref[:, :] = (dgated * yc * (sg * (1.0 + zc * (1.0 - sg)))).astype(BF)
        dy_all = dgated * sz
        dcb = jnp.zeros((Q, Q), F32)
        dcb_t = jnp.zeros((Q, Q), F32)
        db_acc = jnp.zeros((Q, Q), F32)
        dc_acc = jnp.zeros((Q, Q), F32)
        dcs = jnp.zeros((Q, Q), F32)
        ddt = jnp.zeros((Q, Q), F32)
        for p in range(4):
            je, jo = 2 * p, 2 * p + 1
            xp = x_ref[:, 128 * p:128 * p + 128]
            dy = dy_all[:, 128 * p:128 * p + 128]
            dt_p = _pair(lane_even, cm["dt"], je)
            xdt = xp * dt_p
            xdt_b = xdt.astype(BF)
            dy_b = dy.astype(BF)
            zero = jnp.zeros_like(dy_b)
            hp = hp_ref[p]
            hp_b = hp.astype(BF)
            dh = dh_scr[p]
            dh_b = dh.astype(BF)
            acc_scr[p:p + 1, :] += jnp.sum(dy * xp, axis=0, keepdims=True)
            dxp = dy * _pair_row(lane_even, dsk, je)
            e_p = _pair(lane_even, cm["ecs"], je)
            g_p = jnp.dot(cm["cc"], hp_b, preferred_element_type=F32)
            dg_b = (dy * e_p).astype(BF)
            de = dy * g_p * e_p
            dc_acc = dc_acc + lax.dot_general(dg_b, hp_b, NT_DIMS, preferred_element_type=F32)
            dh_in = jnp.dot(cc_t, dg_b, preferred_element_type=F32)
            ds_p = _pair(lane_even, cm["dsm"], je)
            r_p = jnp.dot(cm["bc"], dh_b, preferred_element_type=F32)
            dxdt = r_p * ds_p
            tt = r_p * xdt * ds_p
            db_acc = db_acc + lax.dot_general((xdt * ds_p).astype(BF), dh_b, NT_DIMS, preferred_element_type=F32)
            dgam_m = jnp.sum(dh * hp, axis=0, keepdims=True)
            for j, even in ((je, True), (jo, False)):
                sel = lane_even if even else jnp.logical_not(lane_even)
                dy_j = jnp.where(sel, dy_b, zero)
                l_j = _head_decay(cm, j)
                l_jt = _head_decay_t(cm, j)
                m_j = cm["cb"] * l_j
                m_jt = cb_t * l_jt
                dm = lax.dot_general(dy_j, xdt_b, NT_DIMS, preferred_element_type=F32)
                dm_t = lax.dot_general(xdt_b, dy_j, NT_DIMS, preferred_element_type=F32)
                dxdt = dxdt + jnp.dot(m_jt.astype(BF), dy_j, preferred_element_type=F32)
                dcb = dcb + dm * l_j
                dcb_t = dcb_t + dm_t * l_jt
                t_j = jnp.where(sel, tt, 0.0)
                col = jnp.sum(dm * m_j - dm_t * m_jt + (jnp.where(sel, de, 0.0) - t_j), axis=1, keepdims=True)
                gam_j = cm["gam"][:, j:j + 1]
                last = (jnp.sum(jnp.sum(t_j, axis=0, keepdims=True), axis=1, keepdims=True)
                        + jnp.sum(jnp.where(sel[0:1, :], dgam_m, 0.0), axis=1, keepdims=True) * gam_j)
                col = col + jnp.where(sub[:, 0:1] == Q - 1, last, 0.0)
                dcs = dcs + jnp.where(lane == j, col, 0.0)
            dh_scr[p] = dh_in + dh * _pair_row(lane_even, cm["gam"], je)
            dx_ref[:, 128 * p:128 * p + 128] = dxp + dxdt * dt_p
            dd = dxdt * xp
            ddt = ddt + jnp.where(lane == je, jnp.sum(jnp.where(lane_even, dd, 0.0), axis=1, keepdims=True), 0.0)
            ddt = ddt + jnp.where(lane == jo, jnp.sum(jnp.where(lane_even, 0.0, dd), axis=1, keepdims=True), 0.0)
        dc_ref[:, :] = dc_acc + jnp.dot(dcb.astype(BF), cm["bc"], preferred_element_type=F32)
        db_ref[:, :] = db_acc + jnp.dot(dcb_t.astype(BF), cm["cc"], preferred_element_type=F32)
        tri_t = (sub <= lane).astype(F32)
        dd_a = jnp.dot(tri_t, dcs, precision=lax.Precision.HIGHEST, preferred_element_type=F32)
        ddt = ddt + dd_a * a_neg
        acc_scr[5:6, :] += jnp.sum(dd_a * cm["dt"], axis=0, keepdims=True)
        draw = jnp.where(cm["live"], ddt * _sigmoid_gate(cm["dtr"] + bias), 0.0)
        acc_scr[4:5, :] += jnp.sum(draw, axis=0, keepdims=True)
        ddt_ref[:, :] = draw.astype(BF)

        @pl.when(ci == NCH - 1)
        def _():
            lane1 = _lanes((1, 128))
            dd = jnp.zeros((1, 128), F32)
            for p in range(4):
                row = acc_scr[p:p + 1, :]
                dd = dd + jnp.where(lane1 == 2 * p, jnp.sum(jnp.where(lane1 < 64, row, 0.0), axis=1, keepdims=True), 0.0)
                dd = dd + jnp.where(lane1 == 2 * p + 1, jnp.sum(jnp.where(lane1 < 64, 0.0, row), axis=1, keepdims=True), 0.0)
            dpar_ref[...] = jnp.concatenate([acc_scr[4:5, :], acc_scr[5:6, :] * a_neg, dd, jnp.zeros((5, 128), F32)], axis=0)

    par = _spec((2, 1, 128), lambda c: (0, 0, 0))
    wide = _spec((Q, SSD_W), lambda c: (NCH - 1 - c, 0))
    thin = _spec((Q, 256), lambda c: (NCH - 1 - c, 0))
    vec = _spec((1, SSD_W), lambda c: (0, 0))
    xbc = PXBC // XBC
    halo = 2 * HALO
    chunk = pltpu.VMEM((Q, XBC), F32)
    padded = pltpu.VMEM((Q + HALO, XBC), F32)
    return pl.pallas_call(
        body, grid=(NCH,),
        in_specs=[wide, _spec((Q, XBC), lambda c: (NCH - 1 - c, xbc)),
                  _spec((halo, XBC), lambda c: (jnp.maximum((NCH - 1 - c) * (Q // halo) - 1, 0), xbc)), thin, wide, wide,
                  _spec((2, 1, 4, 128, 128), lambda c: (0, NCH - 1 - c, 0, 0, 0)), _spec((4, XBC), lambda c: (0, 0)),
                  _spec((1, XBC), lambda c: (0, 0)), par, par, par, vec, ANY_SPEC],
        out_specs=[wide, _spec((Q, XBC), lambda c: (NCH - 1 - c, 0)), thin, _spec((2, 8, 128), lambda c: (0, 0, 0)), vec,
                   _spec((4, XBC), lambda c: (0, 0)), _spec((1, XBC), lambda c: (0, 0))],
        out_shape=[jax.ShapeDtypeStruct((T, SSD_W), BF), jax.ShapeDtypeStruct((T, XBC), BF), jax.ShapeDtypeStruct((T, 256), BF),
                   jax.ShapeDtypeStruct((2, 8, 128), F32), jax.ShapeDtypeStruct((1, SSD_W), F32), jax.ShapeDtypeStruct((4, XBC), F32),
                   jax.ShapeDtypeStruct((1, XBC), F32)],
        scratch_shapes=[pltpu.VMEM((2, 4, 128, 128), F32), pltpu.VMEM((2, 8, 128), F32), padded, chunk, chunk, chunk, padded],
        compiler_params=_params(), name="ssd_bwd")(dyn, proj, proj, dt_raw, proj, y_pre, h_prev, conv_w, conv_b, dt_bias2, a_log2, d2, norm_w, after)


def _lru_gates(back, cw, cb, wa, ba, wx, bx, lam):
    xr = _conv(back, cw, cb)
    xr_b = xr.astype(BF)
    r = _sigmoid_gate(jnp.dot(xr_b, wa, preferred_element_type=F32) + ba)
    i = _sigmoid_gate(jnp.dot(xr_b, wx, preferred_element_type=F32) + bx)
    sp = _softplus(-lam)
    la = (-LRU_C) * r * sp
    a = jnp.exp(la)
    mult2 = -jnp.tanh(la) * (a * a + 1.0)
    return xr, xr_b, r, i, sp, a, jnp.sqrt(mult2), mult2


SEG_LEN = 68
SEGS = T // SEG_LEN


def _seg_rows(j, k, off=0):
    return pl.ds(off + j * 8 * SEG_LEN + k, 8, stride=SEG_LEN)


def _segmented_scan(mul_ref, mul_row0, add_ref, out_ref, loc_scr, prod_scr, carry_scr, reverse):
    groups = SEGS // 8
    off = mul_row0 + (1 if reverse else 0)

    def local(i, carry):
        k = SEG_LEN - 1 - i if reverse else i
        new = []
        for j in range(groups):
            h, p = carry[2 * j], carry[2 * j + 1]
            m = mul_ref[_seg_rows(j, k, off), :]
            h = m * h + add_ref[_seg_rows(j, k), :]
            p = m * p
            loc_scr[_seg_rows(j, k), :] = h
            prod_scr[_seg_rows(j, k), :] = p
            new += [h, p]
        return tuple(new)

    lax.fori_loop(0, SEG_LEN, local, (jnp.zeros((8, 128), F32), jnp.ones((8, 128), F32)) * groups)

    def chain(i, c):
        s = SEGS - 1 - i if reverse else i
        carry_scr[pl.ds(s, 1), :] = c
        edge = s * SEG_LEN + (0 if reverse else SEG_LEN - 1)
        return loc_scr[pl.ds(edge, 1), :] + prod_scr[pl.ds(edge, 1), :] * c

    lax.fori_loop(0, SEGS, chain, jnp.zeros((1, 128), F32))

    def fold(k, carry):
        for j in range(groups):
            rows = _seg_rows(j, k)
            out_ref[rows, :] = loc_scr[rows, :] + prod_scr[rows, :] * carry_scr[8 * j:8 * j + 8, :]
        return carry

    lax.fori_loop(0, SEG_LEN, fold, 0)


def lru_fwd(proj, cw, cb, wa2, ba, wx2, bx, lam, after):
    def body(x_ref, cw_ref, cb_ref, wa_ref, ba_ref, wx_ref, bx_ref, lam_ref, _after, h_ref, a_ref, xpad, u_scr, loc_scr, prod_scr, carry_scr):
        _fill_padded(xpad, x_ref)

        def chunk(r0):
            xr, _, _, i, _, a, mult, _ = _lru_gates(_back(xpad, r0), cw_ref[...], cb_ref[...], wa_ref[0], ba_ref[...], wx_ref[0], bx_ref[...],
                                                 lam_ref[...])
            a_ref[pl.ds(r0, Q), :] = a
            u_scr[pl.ds(r0, Q), :] = jnp.where(_rows(a.shape, r0) >= NPAD, mult * (i * xr), 0.0)

        _chunks(chunk, unrolled=True)
        _segmented_scan(a_ref, 0, u_scr, h_ref, loc_scr, prod_scr, carry_scr, reverse=False)

    c0 = PXL // 128
    vec = _spec((1, 128), lambda c: (0, c))
    mat = _spec((1, 128, 128), lambda c: (c, 0, 0))
    seq = pltpu.VMEM((T, 128), F32)
    return pl.pallas_call(
        body, grid=(8,),
        in_specs=[_spec((T, 128), lambda c: (0, c0 + c)), _spec((4, 128), lambda c: (0, c)), vec, mat, vec, mat, vec, vec, ANY_SPEC],
        out_specs=[_spec((T, 128), lambda c: (0, c)), _spec((T, 128), lambda c: (0, c))],
        out_shape=[jax.ShapeDtypeStruct((T, LRU_W), F32), jax.ShapeDtypeStruct((T, LRU_W), F32)],
        scratch_shapes=[pltpu.VMEM((T + 2 * HALO, 128), F32), seq, seq, seq, pltpu.VMEM((SEGS, 128), F32)],
        compiler_params=_params(), name="lru_fwd")(proj, cw, cb, wa2, ba, wx2, bx, lam, after)


def lru_bwd(dh_out, a, hseq, proj, cw, cb, wa2, ba, wx2, bx, lam):
    def body(d_ref, a_ref, h_ref, x_ref, cw_ref, cb_ref, wa_ref, ba_ref, wx_ref, bx_ref, lam_ref,
             dx_ref, dcw_ref, dcb_ref, dwa_ref, dba_ref, dwx_ref, dbx_ref, dlam_ref, xpad, hpad, dpad, dh_ref, loc_scr, prod_scr, carry_scr):
        _fill_padded(dpad, a_ref)
        _segmented_scan(dpad, HALO, d_ref, dh_ref, loc_scr, prod_scr, carry_scr, reverse=True)
        _fill_padded(xpad, x_ref)
        _fill_padded(hpad, h_ref)
        dpad[0:HALO, :] = jnp.zeros((HALO, 128), F32)
        dpad[T + HALO:T + 2 * HALO, :] = jnp.zeros((HALO, 128), F32)
        for ref in (dcw_ref, dcb_ref, dwa_ref, dba_ref, dwx_ref, dbx_ref, dlam_ref):
            ref[...] = jnp.zeros_like(ref)
        lam = lam_ref[...]

        def first(r0):
            back = _back(xpad, r0)
            xr, xr_b, r, i, sp, a, mult, mult2 = _lru_gates(back, cw_ref[...], cb_ref[...], wa_ref[0], ba_ref[...], wx_ref[0], bx_ref[...], lam)
            dh = dh_ref[pl.ds(r0, Q), :]
            da = dh * _back(hpad, r0)(1)
            du = jnp.where(_rows(dh.shape, r0) >= NPAD, dh, 0.0)
            dmult = du * (i * xr)
            di = du * (mult * xr)
            dxr = du * (mult * i)
            dla = da * a - dmult * (a * a) * lax.rsqrt(mult2)
            dr = dla * ((-LRU_C) * sp)
            dlam_ref[...] += jnp.sum(dla * ((-LRU_C) * r), axis=0, keepdims=True)
            dpr = dr * r * (1.0 - r)
            dpi = di * i * (1.0 - i)
            dba_ref[...] += jnp.sum(dpr, axis=0, keepdims=True)
            dbx_ref[...] += jnp.sum(dpi, axis=0, keepdims=True)
            dpr_b = dpr.astype(BF)
            dpi_b = dpi.astype(BF)
            dxr = (dxr + lax.dot_general(dpr_b, wa_ref[0], NT_DIMS, preferred_element_type=F32)
                   + lax.dot_general(dpi_b, wx_ref[0], NT_DIMS, preferred_element_type=F32))
            dwa_ref[0] += lax.dot_general(xr_b, dpr_b, TN_DIMS, preferred_element_type=F32)
            dwx_ref[0] += lax.dot_general(xr_b, dpi_b, TN_DIMS, preferred_element_type=F32)
            dpad[pl.ds(r0 + HALO, Q), :] = dxr
            dcw, dcb = _conv_bwd_w(dxr, back)
            dcw_ref[...] += dcw
            dcb_ref[...] += dcb

        _chunks(first, unrolled=True)
        dlam_ref[...] = -dlam_ref[...] * _sigmoid_gate(-lam)

        def second(r0):
            dx_ref[pl.ds(r0, Q), :] = _conv_bwd_x(_ahead(dpad, r0), cw_ref[...]).astype(BF)

        _chunks(second)

    c0 = PXL // 128
    vec = _spec((1, 128), lambda c: (0, c))
    mat = _spec((1, 128, 128), lambda c: (c, 0, 0))
    col = _spec((T, 128), lambda c: (0, c))
    vshape = jax.ShapeDtypeStruct((1, LRU_W), F32)
    mshape = jax.ShapeDtypeStruct((8, 128, 128), F32)
    pad = pltpu.VMEM((T + 2 * HALO, 128), F32)
    seq = pltpu.VMEM((T, 128), F32)
    return pl.pallas_call(
        body, grid=(8,),
        in_specs=[col, col, col, _spec((T, 128), lambda c: (0, c0 + c)), _spec((4, 128), lambda c: (0, c)), vec, mat, vec, mat, vec, vec],
        out_specs=[col, _spec((4, 128), lambda c: (0, c)), vec, mat, vec, mat, vec, vec],
        out_shape=[jax.ShapeDtypeStruct((T, LRU_W), BF), jax.ShapeDtypeStruct((4, LRU_W), F32), vshape, mshape, vshape, mshape, vshape, vshape],
        scratch_shapes=[pad, pad, pad, seq, seq, seq, pltpu.VMEM((SEGS, 128), F32)],
        compiler_params=_params(), name="lru_bwd")(dh_out, a, hseq, proj, cw, cb, wa2, ba, wx2, bx, lam)


FF_TILE = 256
FF_TILE_ROWS = list(range(0, D_FF, FF_TILE))


def gate_up(h1, wn, w_gate, w_up):
    def body(h_ref, wn_ref, wg_hbm, wu_hbm, gt_ref, up_ref, act_ref, u_ref, wg_ref, wu_ref, wg_sems, wu_sems):
        gate_ready = _arriving(wg_hbm, wg_ref, wg_sems, FF_TILE_ROWS, FF_TILE)
        up_ready = _arriving(wu_hbm, wu_ref, wu_sems, FF_TILE_ROWS, FF_TILE)
        for r in (0, HALF):
            u_ref[r:r + HALF, :] = _rms(h_ref[r:r + HALF, :], wn_ref[...]).astype(BF)

        def tile(c0):
            cols = pl.ds(c0, FF_TILE)
            gate_ready(c0 // FF_TILE)
            up_ready(c0 // FF_TILE)
            gt = lax.dot_general(u_ref[...], wg_ref[cols, :], NT_DIMS, preferred_element_type=F32)
            up = lax.dot_general(u_ref[...], wu_ref[cols, :], NT_DIMS, preferred_element_type=F32)
            gt_ref[:, cols] = gt.astype(BF)
            up_ref[:, cols] = up.astype(BF)
            act_ref[:, cols] = (gt * _sigmoid(gt) * up).astype(BF)

        _col_tiles(D_FF, FF_TILE, tile)

    big = jax.ShapeDtypeStruct((T, D_FF), BF)
    return pl.pallas_call(
        body, grid=(T // RC,), in_specs=[_rows_spec(D), _vec(D), ANY_SPEC, ANY_SPEC],
        out_specs=[_rows_spec(D_FF), _rows_spec(D_FF), _rows_spec(D_FF), _rows_spec(D)],
        out_shape=[big, big, big, jax.ShapeDtypeStruct((T, D), BF)],
        scratch_shapes=[pltpu.VMEM((D_FF, D), BF)] * 2 + [pltpu.SemaphoreType.DMA((len(FF_TILE_ROWS),))] * 2,
        compiler_params=_params(), name="gate_up")(h1, wn, w_gate, w_up)


def down_loss(act, w_down, h1, target, wf):
    first = NPAD + N_META

    def body(a_ref, w_ref, r_ref, t_hbm, wf_ref, d_ref, db_ref, l_ref, dw_ref, h_scr, t_ref, t_sem):
        i = pl.program_id(0)
        _zero_at_first(l_ref, dw_ref)
        head = pltpu.make_async_copy(t_hbm.at[pl.ds(0, RC - first)], t_ref.at[pl.ds(first, RC - first)], t_sem)
        rest = pltpu.make_async_copy(t_hbm.at[pl.ds(pl.multiple_of(jnp.maximum(i * RC - first, 0), 32), RC)], t_ref, t_sem)

        @pl.when(i == 0)
        def _():
            t_ref[0:first, :] = jnp.zeros((first, D), F32)
            head.start()

        @pl.when(i > 0)
        def _():
            rest.start()

        def tile(c0):
            cols = pl.ds(c0, 512)
            h_scr[:, cols] = r_ref[:, cols] + jnp.dot(a_ref[...], w_ref[:, cols], preferred_element_type=F32)

        _col_tiles(D, 512, tile)

        @pl.when(i == 0)
        def _():
            head.wait()

        @pl.when(i > 0)
        def _():
            rest.wait()

        for r in (0, HALF):
            h = h_scr[r:r + HALF, :]
            live = _rows((HALF, D), i * RC + r) >= first
            err = jnp.where(live, _rms(h, wf_ref[...]) - t_ref[r:r + HALF, :], 0.0)
            l_ref[...] += 0.5 * jnp.sum(jnp.sum(err * err, axis=1, keepdims=True) * (1.0 / D), axis=0, keepdims=True)
            dh, dw = _rms_bwd(err * (1.0 / D), h, wf_ref[...])
            dw_ref[...] += jnp.sum(dw, axis=0, keepdims=True)
            d_ref[r:r + HALF, :] = dh
            db_ref[r:r + HALF, :] = dh.astype(BF)

    return pl.pallas_call(
        body, grid=(T // RC,),
        in_specs=[_rows_spec(D_FF), _whole((D_FF, D)), _rows_spec(D), pl.BlockSpec(memory_space=pl.ANY), _vec(D)],
        out_specs=[_rows_spec(D), _rows_spec(D), _spec((1, 128), lambda i: (0, 0)), _vec(D)],
        out_shape=[jax.ShapeDtypeStruct((T, D), F32), jax.ShapeDtypeStruct((T, D), BF), jax.ShapeDtypeStruct((1, 128), F32),
                   jax.ShapeDtypeStruct((1, D), F32)],
        scratch_shapes=[pltpu.VMEM((RC, D), F32), pltpu.VMEM((RC, D), F32), pltpu.SemaphoreType.DMA],
        compiler_params=_params(), name="down_loss")(act, w_down, h1, target, wf)


def swiglu_bwd(dh2_b, w_down, gt, up, act, u2):
    tn = 256

    def body(d_hbm, u_hbm, w_ref, gt_ref, up_ref, act_ref, dg_ref, du_ref, gd_ref, gg_ref, gu_ref, d_ref, u_ref, d_sems, u_sems):
        chunks = list(range(0, T, RC))
        d_ready = _arriving(d_hbm, d_ref, d_sems, chunks, RC)
        u_ready = _arriving(u_hbm, u_ref, u_sems, chunks, RC)

        def rows(r0):
            part = pl.ds(r0, RC)
            d_ready(r0 // RC)
            dact = lax.dot_general(d_ref[part, :], w_ref[...], NT_DIMS, preferred_element_type=F32)
            gt_ = gt_ref[part, :].astype(F32)
            up_ = up_ref[part, :].astype(F32)
            sg = _sigmoid(gt_)
            dg_ref[part, :] = (dact * up_ * (sg * (1.0 + gt_ * (1.0 - sg)))).astype(BF)
            du_ref[part, :] = (dact * (gt_ * sg)).astype(BF)

        _col_tiles(T, RC, rows)
        for k in range(len(chunks)):
            u_ready(k)
        gd_ref[...] = lax.dot_general(act_ref[...], d_ref[...], TN_DIMS, preferred_element_type=F32).astype(BF)
        gg_ref[...] = lax.dot_general(dg_ref[...], u_ref[...], TN_DIMS, preferred_element_type=F32).astype(BF)
        gu_ref[...] = lax.dot_general(du_ref[...], u_ref[...], TN_DIMS, preferred_element_type=F32).astype(BF)

    cols = _spec((T, tn), lambda j: (0, j))
    wrow = _spec((tn, D), lambda j: (j, 0))
    big = jax.ShapeDtypeStruct((T, D_FF), BF)
    grad = jax.ShapeDtypeStruct((D_FF, D), BF)
    return pl.pallas_call(
        body, grid=(D_FF // tn,), in_specs=[ANY_SPEC, ANY_SPEC, wrow, cols, cols, cols],
        out_specs=[cols, cols, wrow, wrow, wrow], out_shape=[big, big, grad, grad, grad],
        scratch_shapes=[pltpu.VMEM((T, D), BF)] * 2 + [pltpu.SemaphoreType.DMA((T // RC,))] * 2,
        compiler_params=_params(), name="swiglu_bwd")(dh2_b, u2, w_down, gt, up, act)


def gate_up_bwd(dgt, dup, w_gate, w_up, h1, wn, dh2):
    def body(dg_ref, du_ref, wg_hbm, wu_hbm, h_ref, wn_ref, r_ref, d_ref, db_ref, dw_ref, du_scr, wg_ref, wu_ref, wg_sems, wu_sems):
        gate_ready = _arriving(wg_hbm, wg_ref, wg_sems, FF_TILE_ROWS, FF_TILE)
        up_ready = _arriving(wu_hbm, wu_ref, wu_sems, FF_TILE_ROWS, FF_TILE)
        _zero_at_first(dw_ref)

        du_scr[...] = jnp.zeros_like(du_scr)

        def tile(c0):
            k = pl.ds(c0, FF_TILE)
            gate_ready(c0 // FF_TILE)
            up_ready(c0 // FF_TILE)
            du_scr[...] += (jnp.dot(dg_ref[:, k], wg_ref[k, :], preferred_element_type=F32)
                            + jnp.dot(du_ref[:, k], wu_ref[k, :], preferred_element_type=F32))

        _col_tiles(D_FF, FF_TILE, tile)
        for r in (0, HALF):
            dh, dw = _rms_bwd(du_scr[r:r + HALF, :], h_ref[r:r + HALF, :], wn_ref[...])
            dw_ref[...] += jnp.sum(dw, axis=0, keepdims=True)
            dh = dh + r_ref[r:r + HALF, :]
            d_ref[r:r + HALF, :] = dh
            db_ref[r:r + HALF, :] = dh.astype(BF)

    return pl.pallas_call(
        body, grid=(T // RC,),
        in_specs=[_rows_spec(D_FF), _rows_spec(D_FF), ANY_SPEC, ANY_SPEC, _rows_spec(D), _vec(D), _rows_spec(D)],
        out_specs=[_rows_spec(D), _rows_spec(D), _vec(D)],
        out_shape=[jax.ShapeDtypeStruct((T, D), F32), jax.ShapeDtypeStruct((T, D), BF), jax.ShapeDtypeStruct((1, D), F32)],
        scratch_shapes=[pltpu.VMEM((RC, D), F32)] + [pltpu.VMEM((D_FF, D), BF)] * 2 + [pltpu.SemaphoreType.DMA((len(FF_TILE_ROWS),))] * 2,
        compiler_params=_params(), name="gate_up_bwd")(dgt, dup, w_gate, w_up, h1, wn, dh2)


def _adamw(w, g, m, v):
    m = ADAM_B1 * m + (1.0 - ADAM_B1) * g
    v = ADAM_B2 * v + (1.0 - ADAM_B2) * (g * g)
    m_hat = m / (1.0 - ADAM_B1 ** ADAM_STEP)
    v_hat = v / (1.0 - ADAM_B2 ** ADAM_STEP)
    delta = -ADAM_LR * (m_hat / (jnp.sqrt(v_hat) + ADAM_EPS) + ADAM_WD * w)
    return delta, m, v


def adamw_shards(name, recvs, ws, ms, vs):
    n = len(ws)

    def body(*refs):
        ins, outs = refs[:4 * n], refs[4 * n:]
        for k in range(n):
            p_ref, w_ref, m_ref, v_ref = ins[k], ins[n + k], ins[2 * n + k], ins[3 * n + k]
            g = p_ref[0].astype(F32)
            for s in range(1, 8):
                g = g + p_ref[s].astype(F32)
            outs[4 * k][...] = g
            outs[4 * k + 1][...], outs[4 * k + 2][...], outs[4 * k + 3][...] = _adamw(w_ref[...], g, m_ref[...], v_ref[...])

    tiles = [_spec((w.shape[0] // 2, w.shape[1]), lambda i: (i, 0)) for w in ws]
    recv_tiles = [_spec((8, w.shape[0] // 2, w.shape[1]), lambda i: (0, i, 0)) for w in ws]
    res = pl.pallas_call(
        body, grid=(2,), in_specs=recv_tiles + tiles * 3,
        out_specs=[t for t in tiles for _ in range(4)],
        out_shape=[jax.ShapeDtypeStruct(w.shape, F32) for w in ws for _ in range(4)],
        compiler_params=_params(), name=name)(*recvs, *ws, *ms, *vs)
    return [list(res[4 * k:4 * k + 4]) for k in range(n)]


def adamw_w_in(recv, w, m, v, after):
    rows = 34
    per_row = D // 128

    def body(p_ref, w_ref, m_ref, v_ref, _after, g_ref, d_ref, mo_ref, vo_ref):
        def chunk(c, carry):
            lines = pl.ds(pl.multiple_of(c * per_row * rows, 16), per_row * rows)
            g = p_ref[0, lines, :].astype(F32)
            for s in range(1, 8):
                g = g + p_ref[s, lines, :].astype(F32)
            g = g.reshape(rows, per_row, 128)
            part = pl.ds(c * rows, rows)
            g_ref[part] = g
            d_ref[part], mo_ref[part], vo_ref[part] = _adamw(w_ref[part], g, m_ref[part], v_ref[part])
            return carry

        lax.fori_loop(0, w.shape[0] // rows, chunk, 0)

    shape = jax.ShapeDtypeStruct(w.shape, F32)
    whole = pl.BlockSpec(memory_space=pltpu.VMEM)
    return pl.pallas_call(body, out_shape=[shape] * 4, in_specs=[whole] * 4 + [ANY_SPEC], compiler_params=_params(0),
                          name="adamw_w_in")(recv, w, m, v, after)


def sum_slabs(recv):
    def body(p_ref, o_ref):
        g = p_ref[0]
        for s in range(1, 8):
            g = g + p_ref[s]
        for s in range(8):
            o_ref[s] = g

    return pl.pallas_call(body, out_shape=jax.ShapeDtypeStruct(recv.shape, F32), compiler_params=_params(0), name="sum_slabs")(recv)


SIMPLE = [("norm1_w", 1024), ("ssd_conv_b", 1536), ("ssd_dt_bias", 16), ("ssd_a_log", 16), ("ssd_d", 16), ("ssd_norm_w", 1024),
          ("lru_conv_b", 1024), ("lru_ba", 1024), ("lru_bx", 1024), ("lru_lambda", 1024), ("lru_norm_w", 1024), ("norm2_w", 1024),
          ("final_norm_w", 1024)]
SPECIAL = ["lru_wa", "lru_wx", "meta_tokens", "ssd_conv_w", "lru_conv_w"]
SM_ROWS = 176
SM_WA, SM_WX, SM_META, SM_SCW, SM_LCW, SM_LOSS = 14, 78, 142, 158, 166, 170


def _simple_rows():
    rows, r = {}, 0
    for name, n in SIMPLE:
        rows[name] = r
        r += -(-n // 1024)
    return rows


def adamw_small(sm, special_g, ws, ms, vs):
    rows = _simple_rows()
    ns, nx = len(SIMPLE), len(SPECIAL)

    def body(*refs):
        sm_ref = refs[0]
        gx = refs[1:1 + nx]
        wr = refs[1 + nx:1 + nx + ns + nx]
        mr = refs[1 + nx + ns + nx:1 + nx + 2 * (ns + nx)]
        vr = refs[1 + nx + 2 * (ns + nx):1 + nx + 3 * (ns + nx)]
        outs = refs[1 + nx + 3 * (ns + nx):]
        o = 0
        for k, (name, n) in enumerate(SIMPLE):
            r0 = rows[name]
            for c0 in range(0, n, 1024):
                wd = min(1024, n - c0)
                g = sm_ref[r0 + c0 // 1024:r0 + c0 // 1024 + 1, 0:wd]
                sl = (slice(None), slice(c0, c0 + wd))
                d, m2, v2 = _adamw(wr[k][sl], g, mr[k][sl], vr[k][sl])
                outs[o][sl] = g
                outs[o + 1][sl] = d
                outs[o + 2][sl] = m2
                outs[o + 3][sl] = v2
            o += 4
        for k in range(nx):
            d, m2, v2 = _adamw(wr[ns + k][...], gx[k][...], mr[ns + k][...], vr[ns + k][...])
            outs[o][...] = d
            outs[o + 1][...] = m2
            outs[o + 2][...] = v2
            o += 3

    out_shape = []
    for k in range(ns):
        out_shape += [jax.ShapeDtypeStruct(ws[k].shape, F32)] * 4
    for k in range(nx):
        out_shape += [jax.ShapeDtypeStruct(ws[ns + k].shape, F32)] * 3
    return pl.pallas_call(body, out_shape=out_shape, compiler_params=_params(0), name="adamw_small")(sm, *special_g, *ws, *ms, *vs)


def _place():
    return lax.axis_index("x"), lax.axis_index("y"), lax.axis_index("c")


def _index(px, py, pc):
    return 4 * px + 2 * py + pc


def all_gather(name, shards):
    n = len(shards)
    hbm = pl.BlockSpec(memory_space=pl.ANY)

    def pieces(s):
        tile = 32 // s.dtype.itemsize
        per = s.shape[0] // tile // 4 * tile
        return [(0, s.shape[0])] if s.shape[0] < 256 else [(r * per, per if r < 3 else s.shape[0] - 3 * per) for r in range(4)]

    parts = [pieces(s) for s in shards]
    first_sem = [7 * sum(len(p) for p in parts[:i]) for i in range(n + 1)]

    def body(*refs):
        ins, outs = refs[:n], refs[n:2 * n]
        send_sems, recv_sems, local_sems = refs[2 * n:]
        x, y, c = _place()
        me, sibling = (x, y, c), (x, y, 1 - c)
        chips = [(1 - x, y), (x, 1 - y), (1 - x, 1 - y)]

        def copy(i, r, k, block, to, src=None):
            rows = pl.ds(*parts[i][r])
            dst = outs[i].at[_index(*block), rows]
            sem = first_sem[i] + 7 * r + k
            return pltpu.make_async_remote_copy(src_ref=dst if src is None else src.at[rows], dst_ref=dst, send_sem=send_sems.at[sem],
                                                recv_sem=recv_sems.at[sem], device_id=to, device_id_type=MESH)

        every = [(i, r) for i in range(n) for r in range(len(parts[i]))]
        mine = [pltpu.make_async_copy(ins[i], outs[i].at[_index(*me)], local_sems.at[i]) for i in range(n)]
        for cp in mine:
            cp.start()
        first = []
        for i, r in every:
            first += [copy(i, r, 1 + j, me, (*chip, c), src=ins[i]) for j, chip in enumerate(chips)]
            first.append(copy(i, r, 0, me, sibling, src=ins[i]))
        for cp in first:
            cp.start()
        passed = []
        for i, r in every:
            for j, chip in enumerate(chips):
                copy(i, r, 1 + j, (*chip, c), me).wait_recv()
                cp = copy(i, r, 4 + j, (*chip, c), sibling)
                cp.start()
                passed.append(cp)
        for i, r in every:
            copy(i, r, 0, sibling, me).wait_recv()
            for j, chip in enumerate(chips):
                copy(i, r, 4 + j, (*chip, 1 - c), me).wait_recv()
        for cp in first + passed:
            cp.wait_send()
        for cp in mine:
            cp.wait()

    return pl.pallas_call(
        body, in_specs=[hbm] * n, out_specs=[hbm] * n,
        out_shape=[jax.ShapeDtypeStruct((8,) + s.shape, s.dtype) for s in shards],
        scratch_shapes=[pltpu.SemaphoreType.DMA((first_sem[n],)), pltpu.SemaphoreType.DMA((first_sem[n],)), pltpu.SemaphoreType.DMA((n,))],
        name=name)(*shards)


HBM_SPEC = pl.BlockSpec(memory_space=pltpu.HBM)
SEM_SPEC = pl.BlockSpec(memory_space=pltpu.SEMAPHORE)
EFFECT = pltpu.SideEffectType.DATAFLOW_SIDE_EFFECTING


def _peers(x, y, c):
    return [((1 - x) if k & 4 else x, (1 - y) if k & 2 else y, (1 - c) if k & 1 else c) for k in range(1, 8)]


def _pieces(rows):
    for n in (4, 2):
        if rows % (16 * n) == 0:
            return [(r * (rows // n), rows // n) for r in range(n)]
    return [(0, rows)]


def _peer_copies(src, land, send_sems, recv_sems, k, peer, mine):
    block = src.at[_index(*peer)]
    return [pltpu.make_async_remote_copy(src_ref=block.at[pl.ds(r0, nr)], dst_ref=land.at[mine, pl.ds(r0, nr)], send_sem=send_sems.at[k],
                                         recv_sem=recv_sems.at[k], device_id=peer, device_id_type=MESH)
            for r0, nr in _pieces(block.shape[0])]


OWN = 7


def _own_copy(src, land, send_sems, mine):
    return pltpu.make_async_copy(src.at[mine], land.at[mine], send_sems.at[OWN])


def copies_start(name, srcs):
    n = len(srcs)

    def body(*refs):
        ins, lands = refs[:n], refs[n:2 * n]
        sends, recvs = refs[2 * n:3 * n], refs[3 * n:4 * n]
        token = refs[-1]
        x, y, c = _place()
        mine = _index(x, y, c)
        for i in range(n):
            per_peer = [_peer_copies(ins[i], lands[i], sends[i], recvs[i], k, peer, mine) for k, peer in enumerate(_peers(x, y, c))]
            for piece in zip(*per_peer):
                for cp in piece:
                    cp.start()
            _own_copy(ins[i], lands[i], sends[i], mine).start()
        token[...] = jnp.zeros_like(token)

    res = pl.pallas_call(
        body, name=name,
        out_shape=([pltpu.SemaphoreType.DMA((8,))] * n + [pltpu.SemaphoreType.DMA((7,))] * n + [pltpu.HBM(s.shape, s.dtype) for s in srcs] * 2
                   + [jax.ShapeDtypeStruct((8, 128), F32)]),
        in_specs=[HBM_SPEC] * (2 * n),
        out_specs=[SEM_SPEC] * (2 * n) + [HBM_SPEC] * (2 * n) + [pl.BlockSpec(memory_space=pltpu.VMEM)],
        input_output_aliases={i: 2 * n + i for i in range(2 * n)},
        compiler_params=pltpu.CompilerParams(has_side_effects=EFFECT),
    )(*[pltpu.with_memory_space_constraint(s, pltpu.HBM) for s in srcs],
      *[pltpu.with_memory_space_constraint(lax.empty(s.shape, s.dtype), pltpu.HBM) for s in srcs])
    return [(res[i], res[n + i], res[2 * n + i], res[3 * n + i]) for i in range(n)], res[-1]


def copies_wait(name, started, after):
    n = len(started)

    def body(*refs):
        ins, lands = refs[:n], refs[n:2 * n]
        sends, recvs = refs[2 * n:3 * n], refs[3 * n:4 * n]
        x, y, c = _place()
        mine = _index(x, y, c)
        for i in range(n):
            for k, peer in enumerate(_peers(x, y, c)):
                arrival = pltpu.make_async_remote_copy(src_ref=ins[i].at[mine], dst_ref=lands[i].at[_index(*peer)],
                                                       send_sem=sends[i].at[k], recv_sem=recvs[i].at[k], device_id=peer, device_id_type=MESH)
                arrival.wait_send()
                arrival.wait_recv()
            _own_copy(ins[i], lands[i], sends[i], mine).wait()

    srcs = [s[2] for s in started]
    lands = [s[3] for s in started]
    res = pl.pallas_call(
        body, name=name,
        out_shape=[pltpu.HBM(s.shape, s.dtype) for s in srcs] + [pltpu.HBM(z.shape, z.dtype) for z in lands],
        in_specs=[HBM_SPEC] * (2 * n) + [SEM_SPEC] * (2 * n) + [ANY_SPEC] * len(after),
        out_specs=[HBM_SPEC] * (2 * n),
        input_output_aliases={i: i for i in range(2 * n)},
        compiler_params=pltpu.CompilerParams(has_side_effects=EFFECT),
    )(*srcs, *lands, *[s[0] for s in started], *[s[1] for s in started], *after)
    return list(res[n:])


def _hop(src, land, send_sems, recv_sems, k, block, to):
    dst = land.at[_index(*block)]
    return pltpu.make_async_remote_copy(src_ref=dst if src is None else src, dst_ref=dst, send_sem=send_sems.at[k], recv_sem=recv_sems.at[k],
                                        device_id=to, device_id_type=MESH)


def _own_block(src, land, send_sems, mine):
    return pltpu.make_async_copy(src, land.at[mine], send_sems.at[4])


def _other_chips(x, y):
    return [(1 - x, y), (x, 1 - y), (1 - x, 1 - y)]


def gather_start(name, shards, through):
    n, m = len(shards), len(through)

    def body(*refs):
        ins, lands = refs[:n], refs[n:2 * n]
        sends, recvs = refs[2 * n + m:3 * n + m], refs[3 * n + m:4 * n + m]
        x, y, c = _place()
        for i in range(n):
            for j, chip in enumerate(_other_chips(x, y)):
                _hop(ins[i], lands[i], sends[i], recvs[i], 1 + j, (x, y, c), (*chip, c)).start()
            _hop(ins[i], lands[i], sends[i], recvs[i], 0, (x, y, c), (x, y, 1 - c)).start()
            _own_block(ins[i], lands[i], sends[i], _index(x, y, c)).start()

    own, passing = pltpu.SemaphoreType.DMA((5,)), pltpu.SemaphoreType.DMA((3,))
    zones = [jax.ShapeDtypeStruct((8,) + s.shape, s.dtype) for s in shards]
    res = pl.pallas_call(
        body, name=name,
        out_shape=([own] * (2 * n) + [passing] * (2 * n) + [pltpu.HBM(s.shape, s.dtype) for s in shards]
                   + [pltpu.HBM(z.shape, z.dtype) for z in zones] + [pltpu.HBM(t.shape, t.dtype) for t in through]),
        in_specs=[HBM_SPEC] * (2 * n + m),
        out_specs=[SEM_SPEC] * (4 * n) + [HBM_SPEC] * (2 * n + m),
        input_output_aliases={i: 4 * n + i for i in range(2 * n + m)},
        compiler_params=pltpu.CompilerParams(has_side_effects=EFFECT),
    )(*[pltpu.with_memory_space_constraint(s, pltpu.HBM) for s in shards],
      *[pltpu.with_memory_space_constraint(lax.empty(z.shape, z.dtype), pltpu.HBM) for z in zones],
      *[pltpu.with_memory_space_constraint(t, pltpu.HBM) for t in through])
    return [[res[4 * n + i], res[5 * n + i], res[i], res[n + i], res[2 * n + i], res[3 * n + i]] for i in range(n)], list(res[6 * n:])


def gather_stage(name, pass_on, finish, after):
    arrays = pass_on + finish
    n = len(arrays)

    def body(*refs):
        ins, lands = refs[:n], refs[n:2 * n]
        sems = [refs[(2 + q) * n:(3 + q) * n] for q in range(4)]
        x, y, c = _place()
        me, sibling = (x, y, c), (x, y, 1 - c)
        for i in range(len(pass_on)):
            send, recv, send_on, recv_on = (q[i] for q in sems)
            for j, chip in enumerate(_other_chips(x, y)):
                _hop(None, lands[i], send, recv, 1 + j, (*chip, c), me).wait_recv()
                _hop(None, lands[i], send_on, recv_on, j, (*chip, c), sibling).start()
        for i in range(len(pass_on), n):
            send, recv, send_on, recv_on = (q[i] for q in sems)
            _hop(ins[i], lands[i], send, recv, 0, sibling, me).wait_recv()
            for j, chip in enumerate(_other_chips(x, y)):
                _hop(None, lands[i], send_on, recv_on, j, (*chip, 1 - c), me).wait_recv()
            _hop(ins[i], lands[i], send, recv, 0, me, sibling).wait_send()
            _own_block(ins[i], lands[i], send, _index(*me)).wait()
            for j, chip in enumerate(_other_chips(x, y)):
                _hop(ins[i], lands[i], send, recv, 1 + j, me, (*chip, c)).wait_send()
                _hop(None, lands[i], send_on, recv_on, j, (*chip, c), sibling).wait_send()
        refs[-1][...] = jnp.zeros_like(refs[-1])

    res = pl.pallas_call(
        body, name=name,
        out_shape=([pltpu.HBM(a[0].shape, a[0].dtype) for a in arrays] + [pltpu.HBM(a[1].shape, a[1].dtype) for a in arrays]
                   + [jax.ShapeDtypeStruct((8, 128), F32)]),
        in_specs=[HBM_SPEC] * (2 * n) + [SEM_SPEC] * (4 * n) + [ANY_SPEC],
        out_specs=[HBM_SPEC] * (2 * n) + [pl.BlockSpec(memory_space=pltpu.VMEM)],
        input_output_aliases={i: i for i in range(2 * n)},
        compiler_params=pltpu.CompilerParams(has_side_effects=EFFECT),
    )(*[a[0] for a in arrays], *[a[1] for a in arrays], *[a[2 + q] for q in range(4) for a in arrays], after)
    for i, a in enumerate(arrays):
        a[0], a[1] = res[i], res[n + i]
    return [a[1] for a in finish], res[-1]


WEIGHTS = ["meta_tokens", "norm1_w", "w_in", "ssd_conv_w", "ssd_conv_b", "ssd_dt_bias", "ssd_a_log", "ssd_d", "ssd_norm_w", "lru_conv_w",
           "lru_conv_b", "lru_wa", "lru_ba", "lru_wx", "lru_bx", "lru_lambda", "lru_norm_w", "w_out", "norm2_w", "w_gate", "w_up", "w_down",
           "final_norm_w"]
BIG = ["w_in", "w_out", "w_gate", "w_up", "w_down"]
COLUMN_SHARDED = ["w_in", "w_gate", "w_up"]


def _pair_blocks(w):
    w = w.reshape(8, 2, 64, 64)
    z = jnp.zeros((8, 64, 64), w.dtype)
    return jnp.concatenate([jnp.concatenate([w[:, 0], z], axis=2), jnp.concatenate([z, w[:, 1]], axis=2)], axis=1)


def _unpair_blocks(w2):
    return jnp.stack([w2[:, :64, :64], w2[:, 64:, 64:]], axis=1).reshape(16, 64, 64)


def _per_group(v):
    return jnp.pad(v.reshape(2, 1, 8), ((0, 0), (0, 0), (0, 120)))


def _pad_cols(v, n):
    return jnp.pad(v, ((0, 0), (0, n - v.shape[1])))


def local_step(x, target, meta, ssd_cw, lru_cw, w_in_shards, fetch, send, p):
    bias2, alog2, d2 = _per_group(p["ssd_dt_bias"]), _per_group(p["ssd_a_log"]), _per_group(p["ssd_d"])
    wa2 = _pair_blocks(p["lru_wa"]).astype(BF)
    wx2 = _pair_blocks(p["lru_wx"]).astype(BF)
    lru = (lru_cw, p["lru_conv_b"], wa2, p["lru_ba"], wx2, p["lru_bx"], p["lru_lambda"])

    proj, dt_raw, u1, h0, w_in, w_dt = in_proj(x, meta, p["norm1_w"], w_in_shards)
    yn_ssd, y_pre, h_prev = ssd_fwd(proj, dt_raw, ssd_cw, p["ssd_conv_b"], bias2, alog2, d2, p["ssd_norm_w"])
    _, moved = fetch([], yn_ssd)
    hseq, a = lru_fwd(proj, *lru, moved)
    (w_out,), _ = fetch(["w_out"], hseq)
    h1, cat = out_proj(yn_ssd, proj, hseq, p["lru_norm_w"], w_out, h0)
    (w_gate, w_up), _ = fetch(["w_gate", "w_up"], h1)
    gt, up, act, u2 = gate_up(h1, p["norm2_w"], w_gate, w_up)
    (w_down,), _ = fetch(["w_down"], act)
    dh2, dh2_b, loss, d_fnw = down_loss(act, w_down, h1, target, p["final_norm_w"])

    dgt, dup, g_down, g_gate, g_up = swiglu_bwd(dh2_b, w_down, gt, up, act, u2)
    dh1, dh1_b, d_n2 = gate_up_bwd(dgt, dup, w_gate, w_up, h1, p["norm2_w"], dh2)
    sent = send({"w_down": g_down, "w_gate": g_gate, "w_up": g_up, "w_out": weight_grad("dw_out", cat, dh1_b)})
    dyn, dh_out, dg_b, d_lnw = out_proj_bwd(dh1_b, w_out, proj, hseq, p["lru_norm_w"], sent)

    dxl_b, d_lcw, d_lcb, dwa2, d_ba, dwx2, d_bx, d_lam = lru_bwd(dh_out, a, hseq, proj, *lru)
    dz_b, dxbc_b, ddt_b, dpar, d_snw, d_scw, d_scb = ssd_bwd(dyn, proj, dt_raw, ssd_cw, p["ssd_conv_b"], y_pre, h_prev, bias2, alog2, d2,
                                                             p["ssd_norm_w"], sent)
    sent = send({"w_in": in_weight_grad([dz_b, dxbc_b, dg_b, dxl_b], [0, SSD_W, 2576, 2576 + LRU_W], ddt_b, u1)})
    grad_x, d_meta, d_n1 = in_proj_bwd(dz_b, dg_b, dxl_b, dxbc_b, ddt_b, w_in, w_dt, h0, p["norm1_w"], dh1, sent)
    small = {"norm1_w": d_n1, "ssd_conv_b": d_scb, "ssd_dt_bias": dpar[:, 0, :8].reshape(1, 16), "ssd_a_log": dpar[:, 1, :8].reshape(1, 16),
             "ssd_d": dpar[:, 2, :8].reshape(1, 16), "ssd_norm_w": d_snw, "lru_conv_b": d_lcb, "lru_ba": d_ba, "lru_bx": d_bx,
             "lru_lambda": d_lam, "lru_norm_w": d_lnw, "norm2_w": d_n2, "final_norm_w": d_fnw,
             "lru_wa": _unpair_blocks(dwa2), "lru_wx": _unpair_blocks(dwx2), "meta_tokens": d_meta,
             "ssd_conv_w": d_scw, "lru_conv_w": d_lcw}
    return loss, grad_x, small


def _pack_small(small, loss):
    rows = [_pad_cols(small[name], -(-n // 1024) * 1024).reshape(-1, 1024) for name, n in SIMPLE]
    rows += [small["lru_wa"].reshape(64, 1024), small["lru_wx"].reshape(64, 1024), small["meta_tokens"],
             _pad_cols(small["ssd_conv_w"], 2048).reshape(8, 1024), small["lru_conv_w"], _pad_cols(loss[:, 0:1], 1024)]
    sm = jnp.concatenate(rows, axis=0)
    return jnp.pad(sm, ((0, SM_ROWS - sm.shape[0]), (0, 0)))


def _slabs(g):
    return g.reshape(8, g.shape[0] // 8, g.shape[1])


def _unslab(g):
    return g.reshape(8 * g.shape[1], g.shape[2])


def kernel(x, meta_tokens, norm1_w, w_in, ssd_conv_w, ssd_conv_b, ssd_dt_bias, ssd_a_log, ssd_d, ssd_norm_w, lru_conv_w, lru_conv_b, lru_wa, lru_ba, lru_wx, lru_bx, lru_lambda, lru_norm_w, w_out, norm2_w, w_gate, w_up, w_down, final_norm_w, loss_target, m_meta_tokens, m_norm1_w, m_w_in, m_ssd_conv_w, m_ssd_conv_b, m_ssd_dt_bias, m_ssd_a_log, m_ssd_d, m_ssd_norm_w, m_lru_conv_w, m_lru_conv_b, m_lru_wa, m_lru_ba, m_lru_wx, m_lru_bx, m_lru_lambda, m_lru_norm_w, m_w_out, m_norm2_w, m_w_gate, m_w_up, m_w_down, m_final_norm_w, v_meta_tokens, v_norm1_w, v_w_in, v_ssd_conv_w, v_ssd_conv_b, v_ssd_dt_bias, v_ssd_a_log, v_ssd_d, v_ssd_norm_w, v_lru_conv_w, v_lru_conv_b, v_lru_wa, v_lru_ba, v_lru_wx, v_lru_bx, v_lru_lambda, v_lru_norm_w, v_w_out, v_norm2_w, v_w_gate, v_w_up, v_w_down, v_final_norm_w):
    w = dict(meta_tokens=meta_tokens, norm1_w=norm1_w, w_in=w_in[0], ssd_conv_w=ssd_conv_w[0], ssd_conv_b=ssd_conv_b, ssd_dt_bias=ssd_dt_bias,
             ssd_a_log=ssd_a_log, ssd_d=ssd_d, ssd_norm_w=ssd_norm_w, lru_conv_w=lru_conv_w[0], lru_conv_b=lru_conv_b, lru_wa=lru_wa[0],
             lru_ba=lru_ba, lru_wx=lru_wx[0], lru_bx=lru_bx, lru_lambda=lru_lambda, lru_norm_w=lru_norm_w, w_out=w_out[0], norm2_w=norm2_w,
             w_gate=w_gate[0], w_up=w_up[0], w_down=w_down[0], final_norm_w=final_norm_w.reshape(1, D))
    m = dict(meta_tokens=m_meta_tokens, norm1_w=m_norm1_w, w_in=m_w_in[0], ssd_conv_w=m_ssd_conv_w[0], ssd_conv_b=m_ssd_conv_b,
             ssd_dt_bias=m_ssd_dt_bias, ssd_a_log=m_ssd_a_log, ssd_d=m_ssd_d, ssd_norm_w=m_ssd_norm_w, lru_conv_w=m_lru_conv_w[0],
             lru_conv_b=m_lru_conv_b, lru_wa=m_lru_wa[0], lru_ba=m_lru_ba, lru_wx=m_lru_wx[0], lru_bx=m_lru_bx, lru_lambda=m_lru_lambda,
             lru_norm_w=m_lru_norm_w, w_out=m_w_out[0], norm2_w=m_norm2_w, w_gate=m_w_gate[0], w_up=m_w_up[0], w_down=m_w_down[0],
             final_norm_w=m_final_norm_w.reshape(1, D))
    v = dict(meta_tokens=v_meta_tokens, norm1_w=v_norm1_w, w_in=v_w_in[0], ssd_conv_w=v_ssd_conv_w[0], ssd_conv_b=v_ssd_conv_b,
             ssd_dt_bias=v_ssd_dt_bias, ssd_a_log=v_ssd_a_log, ssd_d=v_ssd_d, ssd_norm_w=v_ssd_norm_w, lru_conv_w=v_lru_conv_w[0],
             lru_conv_b=v_lru_conv_b, lru_wa=v_lru_wa[0], lru_ba=v_lru_ba, lru_wx=v_lru_wx[0], lru_bx=v_lru_bx, lru_lambda=v_lru_lambda,
             lru_norm_w=v_lru_norm_w, w_out=v_w_out[0], norm2_w=v_norm2_w, w_gate=v_w_gate[0], w_up=v_w_up[0], w_down=v_w_down[0],
             final_norm_w=v_final_norm_w.reshape(1, D))
    shapes = dict(meta_tokens=meta_tokens.shape, norm1_w=norm1_w.shape, w_in=w_in.shape, ssd_conv_w=ssd_conv_w.shape,
                  ssd_conv_b=ssd_conv_b.shape, ssd_dt_bias=ssd_dt_bias.shape, ssd_a_log=ssd_a_log.shape, ssd_d=ssd_d.shape,
                  ssd_norm_w=ssd_norm_w.shape, lru_conv_w=lru_conv_w.shape, lru_conv_b=lru_conv_b.shape, lru_wa=lru_wa.shape,
                  lru_ba=lru_ba.shape, lru_wx=lru_wx.shape, lru_bx=lru_bx.shape, lru_lambda=lru_lambda.shape, lru_norm_w=lru_norm_w.shape,
                  w_out=w_out.shape, norm2_w=norm2_w.shape, w_gate=w_gate.shape, w_up=w_up.shape, w_down=w_down.shape,
                  final_norm_w=final_norm_w.shape)
    me = _index(*_place())
    for n in COLUMN_SHARDED:
        w[n], m[n], v[n] = w[n].T, m[n].T, v[n].T

    small_shard = jnp.concatenate([w["meta_tokens"], _pad_cols(w["ssd_conv_w"], 256).reshape(8, 128), w["lru_conv_w"],
                                   jnp.zeros((4, 128), F32)], axis=0)
    g_in, gs = all_gather("gather_w_in", [w["w_in"].astype(BF), small_shard])
    later = ["w_out", "w_gate", "w_up", "w_down"]
    started, (g_in, gs) = gather_start("gather_rest_start", [w[n].astype(BF) for n in later], [g_in, gs])
    started = dict(zip(later, started))
    ssd_cw = gs[:, 16:24].reshape(8, 4, 256)[:, :, :192].transpose(1, 0, 2).reshape(4, XBC)
    lru_cw = gs[:, 24:28].transpose(1, 0, 2).reshape(4, LRU_W)

    def fetch(names, after):
        pass_on = {"w_out": ["w_down"], "w_gate": [], "w_down": []}[names[0]] if names else ["w_out", "w_gate", "w_up"]
        got, zero = gather_stage("gather_" + (names[0] + "_wait" if names else "pass_on"), [started[n] for n in pass_on],
                                 [started[n] for n in names], after)
        return [_unslab(g) for g in got], zero

    in_flight = {}

    def send(grads):
        names = list(grads)
        st, zero = copies_start("grads_" + names[0] + "_start", [grads[n] if n == "small" else _slabs(grads[n]) for n in names])
        in_flight.update(zip(names, st))
        return zero

    loss, grad_x, small = local_step(x[0], loss_target[0], gs, ssd_cw, lru_cw, g_in, fetch, send, w)
    send({"small": _pack_small(small, loss).reshape(8, SM_ROWS // 8, 1024)})

    out = {}
    early = ["w_down", "w_gate", "w_up", "w_out"]
    recv = dict(zip(early, copies_wait("grads_early_wait", [in_flight[n] for n in early], [in_flight["small"][2]])))
    for pair in (early[:2], early[2:]):
        done = adamw_shards("adamw_" + pair[0], [recv[n] for n in pair], [w[n] for n in pair], [m[n] for n in pair], [v[n] for n in pair])
        out.update(zip(pair, done))
    recv_in, recv_small = copies_wait("grads_late_wait", [in_flight["w_in"], in_flight["small"]], [out[n][0] for n in early])
    def lines(a):
        return jnp.transpose(a.reshape(D // 128, 128, IN_COLS // 8), (2, 0, 1))

    gathering, zero = copies_start("gather_small_start", [sum_slabs(recv_small)])
    updated = adamw_w_in(recv_in, lines(w_in), lines(m_w_in), lines(v_w_in), zero)
    out["w_in"] = [jnp.transpose(o, (1, 2, 0)).reshape(D, IN_COLS // 8) for o in updated]
    for n in ("w_gate", "w_up"):
        out[n] = [o.T for o in out[n]]
    sm = copies_wait("gather_small_wait", gathering, [updated[0]])[0].reshape(SM_ROWS, 1024)
    special_g =[sm[SM_WA:SM_WA + 64].reshape(16, 64, 64), sm[SM_WX:SM_WX + 64].reshape(16, 64, 64),
                 lax.dynamic_slice(sm[SM_META:SM_META + 16], (0, 128 * me), (16, 128)),
                 lax.dynamic_slice(sm[SM_SCW:SM_SCW + 8].reshape(4, 2048), (0, 192 * me), (4, 192)),
                 lax.dynamic_slice(sm[SM_LCW:SM_LCW + 4], (0, 128 * me), (4, 128))]
    names = [n for n, _ in SIMPLE] + SPECIAL
    res = adamw_small(sm, special_g, [w[n] for n in names], [m[n] for n in names], [v[n] for n in names])
    for k, (n, _) in enumerate(SIMPLE):
        out[n] = res[4 * k:4 * k + 4]
    for k, n in enumerate(SPECIAL):
        o = 4 * len(SIMPLE) + 3 * k
        out[n] = [special_g[k]] + list(res[o:o + 3])
    loss_total = sm[SM_LOSS, 0]
    flat = [loss_total, grad_x[None]]
    for k in range(4):
        flat += [out[n][k].reshape(shapes[n]) for n in WEIGHTS]
    return tuple(flat)
```

```python
import math

import jax
import jax.numpy as jnp
from jax import lax
from jax.experimental import pallas as pl
from jax.experimental.pallas import tpu as pltpu

F32 = jnp.float32
BF = jnp.bfloat16

D = 1024
SEQ = 2048
N_META = 16
Q = 128
NPAD = 112
T = NPAD + N_META + SEQ
NCH = T // Q
RC = 544
D_FF = 2816
SSD_W = 1024
LRU_W = 1024
XBC = 1536
IN_COLS = 4624
PZ, PG, PXL, PXBC = 0, 1024, 2048, 3072
NP_IN = 4608
EPS = 1e-6
LRU_C = 8.0
VMEM_LIMIT = 56 * 1024 * 1024

ADAM_LR, ADAM_B1, ADAM_B2, ADAM_EPS, ADAM_WD, ADAM_STEP = 0.001, 0.9, 0.999, 1e-08, 0.01, 10

NT_DIMS = (((1,), (1,)), ((), ()))
TN_DIMS = (((0,), (0,)), ((), ()))
MESH = pl.DeviceIdType.MESH


def _params(n_grid=1, limit=VMEM_LIMIT):
    return pltpu.CompilerParams(dimension_semantics=("arbitrary",) * n_grid, vmem_limit_bytes=limit)


def _spec(shape, imap, single=False):
    if single:
        return pl.BlockSpec(shape, imap, pipeline_mode=pl.Buffered(1))
    return pl.BlockSpec(shape, imap)


def _sigmoid(x):
    return 0.5 * jnp.tanh(0.5 * x) + 0.5


def _sigmoid_gate(x):
    return 1.0 / (1.0 + jnp.exp(-x))


def _softplus(x):
    return jnp.maximum(x, 0.0) + jnp.log(1.0 + jnp.exp(-jnp.abs(x)))


def _rms_stats(h):
    return lax.rsqrt(jnp.mean(h * h, axis=-1, keepdims=True) + EPS)


def _rms(h, w):
    return (h * _rms_stats(h)) * w


def _rms_bwd(du, h, w):
    r = _rms_stats(h)
    n = h * r
    dn = du * w
    dh = r * (dn - n * jnp.mean(dn * n, axis=-1, keepdims=True))
    return dh, du * n


_G0 = math.sqrt(2.0 / math.pi)


def _gelu(x):
    return 0.5 * x * (1.0 + jnp.tanh(_G0 * (x + 0.044715 * (x * x * x))))


def _gelu_grad(x):
    t = jnp.tanh(_G0 * (x + 0.044715 * (x * x * x)))
    return 0.5 * (1.0 + t) + 0.5 * x * (1.0 - t * t) * (_G0 * (1.0 + 3.0 * 0.044715 * (x * x)))


def _rows(shape, r0=0):
    return lax.broadcasted_iota(jnp.int32, shape, 0) + r0


def _lanes(shape):
    return lax.broadcasted_iota(jnp.int32, shape, 1)


HALO = 8


def _fill_padded(pad_ref, x_ref):
    pad_ref[0:HALO, :] = jnp.zeros((HALO, pad_ref.shape[1]), F32)
    pad_ref[T + HALO:T + 2 * HALO, :] = jnp.zeros((HALO, pad_ref.shape[1]), F32)

    def step(c, carry):
        r0 = pl.multiple_of(c * Q, Q)
        pad_ref[pl.ds(r0 + HALO, Q), :] = x_ref[pl.ds(r0, Q), :].astype(F32)
        return carry

    lax.fori_loop(0, NCH, step, 0)


def _back(pad_ref, r0):
    win = pad_ref[pl.ds(r0, Q + HALO), :]
    return lambda s: win[HALO:, :] if s == 0 else pltpu.roll(win, s, axis=0)[HALO:, :]


def _ahead(pad_ref, r0):
    win = pad_ref[pl.ds(r0 + HALO, Q + HALO), :]
    return lambda s: win[:Q, :] if s == 0 else pltpu.roll(win, Q + HALO - s, axis=0)[:Q, :]


def _conv(back, w, b):
    y = b + w[3:4, :] * back(0)
    for k in range(3):
        y = y + w[k:k + 1, :] * back(3 - k)
    return y


def _conv_bwd_x(ahead, w):
    dx = w[3:4, :] * ahead(0)
    for k in range(3):
        dx = dx + w[k:k + 1, :] * ahead(3 - k)
    return dx


def _conv_bwd_w(dy, back):
    dws = [jnp.sum(dy * back(3 - k), axis=0, keepdims=True) for k in range(4)]
    return jnp.concatenate(dws, axis=0), jnp.sum(dy, axis=0, keepdims=True)


def _chunks(fn, unrolled=False):
    if unrolled:
        for c in range(NCH):
            fn(c * Q)
        return

    def step(c, carry):
        fn(pl.multiple_of(c * Q, Q))
        return carry

    lax.fori_loop(0, NCH, step, 0)


HALF = RC // 2


def _col_tiles(n, tn, fn):
    def step(j, carry):
        fn(pl.multiple_of(j * tn, tn))
        return carry

    lax.fori_loop(0, n // tn, step, 0)


def _rows_spec(cols, block_col=0):
    return _spec((RC, cols), lambda i: (i, block_col))


def _whole(shape):
    return _spec(shape, lambda i: tuple(0 for _ in shape), single=True)


def _vec(cols):
    return _spec((1, cols), lambda i: (0, 0))


def _zero_at_first(*refs):
    @pl.when(pl.program_id(0) == 0)
    def _():
        for r in refs:
            r[...] = jnp.zeros_like(r)


ANY_SPEC = pl.BlockSpec(memory_space=pl.ANY)


def _arriving(src, dst, sems, starts, rows):
    n, ahead = len(starts), 2
    first = pl.program_id(0) == 0

    def piece(k):
        r0 = starts[0]
        for j in range(1, n):
            r0 = jnp.where(k == j, starts[j], r0)
        at = pl.ds(pl.multiple_of(r0, 16), rows)
        return pltpu.make_async_copy(src.at[at], dst.at[at], sems.at[k])

    @pl.when(first)
    def _():
        for k in range(min(ahead, n)):
            piece(k).start()

    def ready(k):
        k = jnp.asarray(k, jnp.int32)

        @pl.when(first)
        def _():
            piece(k).wait()

            @pl.when(k + ahead < n)
            def _():
                piece(k + ahead).start()

    return ready


IN_RUNS = ((PZ, 0, 1024), (PXBC, 1024, XBC), (PG, 2576, 2048))
IN_TILE = 512
IN_TILE_ROWS = [wrow + IN_TILE * j for _, wrow, width in IN_RUNS for j in range(width // IN_TILE)]


def _in_tiles(fn, before_run=lambda run: None):
    done = 0
    for run, (pcol, wrow, width) in enumerate(IN_RUNS):
        before_run(run)
        def step(j, carry, pcol=pcol, wrow=wrow, done=done):
            fn(pl.multiple_of(pcol + j * IN_TILE, IN_TILE), pl.multiple_of(wrow + j * IN_TILE, 16), done + j)
            return carry

        lax.fori_loop(0, width // IN_TILE, step, 0)
        done += width // IN_TILE


def in_proj(x, meta, wn, w_shards):
    first = NPAD + N_META
    steps = T // RC
    shard = IN_COLS // 8

    def body(x_hbm, meta_ref, wn_ref, g_hbm, o_ref, dt_ref, u_ref, h_ref, wt_hbm, wdt_ref, raw, w_ref, h_scr, g_sems, h_sems, out_sem):
        i = pl.program_id(0)
        slot = i % 2
        shards = [pltpu.make_async_copy(g_hbm.at[j], raw.at[j], g_sems.at[j]) for j in range(8)]
        head = pltpu.make_async_copy(x_hbm.at[pl.ds(0, RC - first)], h_scr.at[0, pl.ds(first, RC - first)], h_sems.at[0])
        put_back = pltpu.make_async_copy(w_ref, wt_hbm, out_sem)

        def rows_of(step):
            return pltpu.make_async_copy(x_hbm.at[pl.ds(pl.multiple_of(step * RC - first, 32), RC)], h_scr.at[step % 2], h_sems.at[step % 2])

        @pl.when(i == 0)
        def _():
            for cp in shards:
                cp.start()
            head.start()
            h_scr[0, 0:NPAD, :] = jnp.zeros((NPAD, D), F32)
            for j in range(8):
                h_scr[0, NPAD:first, 128 * j:128 * j + 128] = meta_ref[j, 0:N_META, :]

        @pl.when(i + 1 < steps)
        def _():
            rows_of(i + 1).start()

        @pl.when(i == 0)
        def _():
            head.wait()

        @pl.when(i > 0)
        def _():
            rows_of(i).wait()

        h_ref[...] = h_scr[slot]
        for r in (0, HALF):
            u_ref[r:r + HALF, :] = _rms(h_scr[slot, r:r + HALF, :], wn_ref[...]).astype(BF)

        def place_shards(run):
            @pl.when(i == 0)
            def _():
                for j in ((0, 1), (2, 3, 4), (5, 6, 7))[run]:
                    shards[j].wait()
                    w_ref[shard * j:shard * (j + 1), :] = raw[j]
                if run == 1:
                    wdt_ref[...] = jnp.zeros_like(wdt_ref)
                    for g in range(2):
                        wdt_ref[128 * g:128 * g + 8, :] = w_ref[2560 + 8 * g:2568 + 8 * g, :]
                if run == 2:
                    put_back.start()

        def tile(pcol, wrow, k):
            o_ref[:, pl.ds(pcol, IN_TILE)] = lax.dot_general(u_ref[...], w_ref[pl.ds(wrow, IN_TILE), :], NT_DIMS,
                                                             preferred_element_type=F32).astype(BF)

        _in_tiles(tile, place_shards)
        dt_ref[...] = lax.dot_general(u_ref[...], wdt_ref[...], NT_DIMS, preferred_element_type=F32)

        @pl.when(i == steps - 1)
        def _():
            put_back.wait()

    return pl.pallas_call(
        body, grid=(steps,), in_specs=[ANY_SPEC, _spec(meta.shape, lambda i: (0, 0, 0)), _vec(D), ANY_SPEC],
        out_specs=[_rows_spec(NP_IN), _rows_spec(256), _rows_spec(D), _rows_spec(D), ANY_SPEC, _spec((256, D), lambda i: (0, 0))],
        out_shape=[jax.ShapeDtypeStruct((T, NP_IN), BF), jax.ShapeDtypeStruct((T, 256), F32), jax.ShapeDtypeStruct((T, D), BF),
                   jax.ShapeDtypeStruct((T, D), F32), jax.ShapeDtypeStruct((IN_COLS, D), BF), jax.ShapeDtypeStruct((256, D), BF)],
        scratch_shapes=[pltpu.VMEM((8, shard, D), BF), pltpu.VMEM((IN_COLS, D), BF), pltpu.VMEM((2, RC, D), F32),
                        pltpu.SemaphoreType.DMA((8,)), pltpu.SemaphoreType.DMA((2,)), pltpu.SemaphoreType.DMA],
        compiler_params=_params(), name="in_proj")(x, meta, wn, w_shards)


def out_proj(yn_ssd, proj, hseq, lru_nw, w_out, h0):
    def body(y_ref, g_ref, h_ref, wn_ref, w_ref, r_ref, o_ref, cat_ref):
        cat_ref[:, 0:SSD_W] = y_ref[...]
        for r in (0, HALF):
            y = _gelu(g_ref[r:r + HALF, :].astype(F32)) * h_ref[r:r + HALF, :]
            cat_ref[r:r + HALF, SSD_W:] = _rms(y, wn_ref[...]).astype(BF)

        def tile(c0):
            o_ref[:, pl.ds(c0, 512)] = r_ref[:, pl.ds(c0, 512)] + jnp.dot(cat_ref[...], w_ref[:, pl.ds(c0, 512)], preferred_element_type=F32)

        _col_tiles(D, 512, tile)

    return pl.pallas_call(
        body, grid=(T // RC,),
        in_specs=[_rows_spec(SSD_W), _rows_spec(LRU_W, PG // LRU_W), _rows_spec(LRU_W), _vec(LRU_W), _whole((SSD_W + LRU_W, D)), _rows_spec(D)],
        out_specs=[_rows_spec(D), _rows_spec(SSD_W + LRU_W)],
        out_shape=[jax.ShapeDtypeStruct((T, D), F32), jax.ShapeDtypeStruct((T, SSD_W + LRU_W), BF)],
        compiler_params=_params(), name="out_proj")(yn_ssd, proj, hseq, lru_nw, w_out, h0)


def out_proj_bwd(dh1_b, w_out, proj, hseq, lru_nw, after):
    def body(d_ref, w_ref, g_ref, h_ref, wn_ref, _after, dy_ref, dh_ref, dg_ref, dw_ref, dl_scr):
        _zero_at_first(dw_ref)

        def tile(c0):
            dy_ref[:, pl.ds(c0, 512)] = lax.dot_general(d_ref[...], w_ref[pl.ds(c0, 512), :], NT_DIMS, preferred_element_type=F32)
            dl_scr[:, pl.ds(c0, 512)] = lax.dot_general(d_ref[...], w_ref[pl.ds(SSD_W + c0, 512), :], NT_DIMS, preferred_element_type=F32)

        _col_tiles(SSD_W, 512, tile)

        for r in (0, HALF):
            g = g_ref[r:r + HALF, :].astype(F32)
            h = h_ref[r:r + HALF, :]
            ge = _gelu(g)
            dy, dw = _rms_bwd(dl_scr[r:r + HALF, :], ge * h, wn_ref[...])
            dw_ref[...] += jnp.sum(dw, axis=0, keepdims=True)
            dh_ref[r:r + HALF, :] = dy * ge
            dg_ref[r:r + HALF, :] = (dy * h * _gelu_grad(g)).astype(BF)

    return pl.pallas_call(
        body, grid=(T // RC,),
        in_specs=[_rows_spec(D), _whole((SSD_W + LRU_W, D)), _rows_spec(LRU_W, PG // LRU_W), _rows_spec(LRU_W), _vec(LRU_W), ANY_SPEC],
        out_specs=[_rows_spec(SSD_W), _rows_spec(LRU_W), _rows_spec(LRU_W), _vec(LRU_W)],
        out_shape=[jax.ShapeDtypeStruct((T, SSD_W), F32), jax.ShapeDtypeStruct((T, LRU_W), F32), jax.ShapeDtypeStruct((T, LRU_W), BF),
                   jax.ShapeDtypeStruct((1, LRU_W), F32)],
        scratch_shapes=[pltpu.VMEM((RC, LRU_W), F32)],
        compiler_params=_params(), name="out_proj_bwd")(dh1_b, w_out, proj, hseq, lru_nw, after)


def in_proj_bwd(dz, dg, dxl, dxbc, ddt, w_t, w_dt, h0, wn, dh1, after):
    first = NPAD + N_META

    def body(dz_ref, dg_ref, dxl_ref, dxbc_ref, ddt_ref, w_hbm, wdt_ref, h_ref, wn_ref, r_ref, _after, gx_hbm, meta_ref, dw_ref, du_scr, o_ref, sem,
             w_ref, w_sems):
        i = pl.program_id(0)
        ready = _arriving(w_hbm, w_ref, w_sems, IN_TILE_ROWS, IN_TILE)
        _zero_at_first(dw_ref)
        du_scr[...] = jnp.dot(ddt_ref[...], wdt_ref[...], preferred_element_type=F32)
        done = 0
        for d_ref, wrow, width in ((dz_ref, 0, 1024), (dxbc_ref, 1024, XBC), (dg_ref, 2576, 1024), (dxl_ref, 3600, 1024)):
            def step(j, carry, d_ref=d_ref, wrow=wrow, done=done):
                c0 = pl.multiple_of(j * IN_TILE, IN_TILE)
                ready(done + j)
                du_scr[...] += jnp.dot(d_ref[:, pl.ds(c0, IN_TILE)], w_ref[pl.ds(pl.multiple_of(wrow + c0, 16), IN_TILE), :],
                                       preferred_element_type=F32)
                return carry

            lax.fori_loop(0, width // IN_TILE, step, 0)
            done += width // IN_TILE
        for r in (0, HALF):
            dh, dw = _rms_bwd(du_scr[r:r + HALF, :], h_ref[r:r + HALF, :], wn_ref[...])
            dw_ref[...] += jnp.sum(dw, axis=0, keepdims=True)
            o_ref[r:r + HALF, :] = dh + r_ref[r:r + HALF, :]

        @pl.when(i == 0)
        def _():
            meta_ref[...] = o_ref[NPAD:first, :]
            head = pltpu.make_async_copy(o_ref.at[pl.ds(first, RC - first)], gx_hbm.at[pl.ds(0, RC - first)], sem)
            head.start()
            head.wait()

        @pl.when(i > 0)
        def _():
            rest = pltpu.make_async_copy(o_ref, gx_hbm.at[pl.ds(pl.multiple_of(i * RC - first, 32), RC)], sem)
            rest.start()
            rest.wait()

    return pl.pallas_call(
        body, grid=(T // RC,),
        in_specs=[_rows_spec(SSD_W), _rows_spec(LRU_W), _rows_spec(LRU_W), _rows_spec(XBC), _rows_spec(256), ANY_SPEC,
                  _whole((256, D)), _rows_spec(D), _vec(D), _rows_spec(D), ANY_SPEC],
        out_specs=[ANY_SPEC, _spec((N_META, D), lambda i: (0, 0)), _vec(D)],
        out_shape=[jax.ShapeDtypeStruct((SEQ, D), F32), jax.ShapeDtypeStruct((N_META, D), F32), jax.ShapeDtypeStruct((1, D), F32)],
        scratch_shapes=[pltpu.VMEM((RC, D), F32), pltpu.VMEM((RC, D), F32), pltpu.SemaphoreType.DMA,
                        pltpu.VMEM((IN_COLS, D), BF), pltpu.SemaphoreType.DMA((len(IN_TILE_ROWS),))],
        compiler_params=_params(), name="in_proj_bwd")(dz, dg, dxl, dxbc, ddt, w_t, w_dt, h0, wn, dh1, after)


GRAD_TILE = 256


def weight_grad(name, a, u1):
    tm = GRAD_TILE

    def body(a_ref, u_ref, o_ref):
        o_ref[...] = lax.dot_general(a_ref[...], u_ref[...], TN_DIMS, preferred_element_type=F32).astype(BF)

    return pl.pallas_call(
        body, grid=(a.shape[1] // tm,),
        in_specs=[_spec((T, tm), lambda j: (0, j)), _spec((T, D), lambda j: (0, 0), single=True)],
        out_specs=_spec((tm, D), lambda j: (j, 0)),
        out_shape=jax.ShapeDtypeStruct((a.shape[1], D), BF),
        compiler_params=_params(), name=name)(a, u1)


def in_weight_grad(parts, first_rows, ddt, u1):
    tm = GRAD_TILE
    per_row = D // 128
    dt_row, dt_lines = 2560, 8 * per_row
    parts = list(parts) + [ddt]
    first_rows = list(first_rows) + [dt_row]
    tiles = [p.shape[1] // tm for p in parts]
    starts = [sum(tiles[:k]) for k in range(len(parts))]
    last = sum(tiles) - 1

    def body(*refs):
        a_refs, u_ref = refs[:len(parts)], refs[len(parts)]
        o_hbm, mix_scr, stage, sems = refs[len(parts) + 1:]
        step = pl.program_id(0)
        slot = step % 2
        line0 = 0
        for a_ref, start, n, first in zip(a_refs, starts, tiles, first_rows):
            here = (step >= start) & (step < start + n)
            line0 = jnp.where(here, per_row * (first + tm * (step - start)), line0)

            @pl.when(here)
            def _(a_ref=a_ref):
                res = lax.dot_general(a_ref[...], u_ref[...], TN_DIMS, preferred_element_type=F32)
                for q in range(per_row):
                    mix_scr[pl.ds(q, tm, stride=per_row), :] = res[:, 128 * q:128 * q + 128]

        def tile_copy(of_slot, to):
            return pltpu.make_async_copy(stage.at[of_slot], o_hbm.at[pl.ds(to, per_row * tm)], sems.at[of_slot])

        @pl.when(step >= 2)
        def _():
            tile_copy(slot, 0).wait()

        stage[slot] = mix_scr[...].astype(BF)

        @pl.when(step < last)
        def _():
            tile_copy(slot, pl.multiple_of(line0, 128)).start()

        @pl.when(step == last)
        def _():
            halves = [pltpu.make_async_copy(stage.at[slot, pl.ds(128 * per_row * k, dt_lines)],
                                            o_hbm.at[pl.ds(per_row * (dt_row + 8 * k), dt_lines)], sems.at[2 + k]) for k in range(2)]
            for cp in halves:
                cp.start()
            tile_copy(1 - slot, 0).wait()
            for cp in halves:
                cp.wait()

    def tile_of(start, n):
        return lambda j: (0, jnp.clip(j - start, 0, n - 1))

    return pl.pallas_call(
        body, grid=(last + 1,),
        in_specs=[_spec((T, tm), tile_of(s, n)) for s, n in zip(starts, tiles)] + [_spec((T, D), lambda j: (0, 0), single=True)],
        out_specs=ANY_SPEC,
        out_shape=jax.ShapeDtypeStruct((per_row * IN_COLS, 128), BF),
        scratch_shapes=[pltpu.VMEM((per_row * tm, 128), F32), pltpu.VMEM((2, per_row * tm, 128), BF), pltpu.SemaphoreType.DMA((4,))],
        compiler_params=_params(), name="dw_in")(*parts, u1)


def _ssd_chunk_common(row0, dt_ref, b_ref, c_ref, bias, a_neg):
    shape = (Q, Q)
    lane = _lanes(shape)
    sub = _rows(shape)
    live = (_rows(shape, row0) >= NPAD) & (lane < 8)
    dtr = dt_ref[:, :]
    dt = jnp.where(live, _softplus(dtr + bias), 0.0)
    d_a = dt * a_neg
    tri = (sub >= lane).astype(F32)
    cs = jnp.dot(tri, d_a, precision=lax.Precision.HIGHEST, preferred_element_type=F32)
    cs_t = cs.T
    b_f = b_ref[:, :]
    bc = b_f.astype(BF)
    cc = c_ref[:, :].astype(BF)
    cb = lax.dot_general(cc, bc, NT_DIMS, preferred_element_type=F32)
    cs_last = cs[Q - 1:Q, :]
    return dict(lane=lane, sub=sub, live=live, dtr=dtr, dt=dt, cs=cs, cs_t=cs_t, bc=bc, cc=cc, cb=cb, bc_t=b_f.T.astype(BF),
                ecs=jnp.exp(cs), dsm=jnp.exp(cs_last - cs), gam=jnp.exp(cs_last))


def _pair(lane_even, mat, j):
    return jnp.where(lane_even, mat[:, j:j + 1], mat[:, j + 1:j + 2])


def _pair_row(lane_even, mat, j):
    return jnp.where(lane_even[0:1, :], mat[:, j:j + 1], mat[:, j + 1:j + 2])


def _head_decay(cm, j):
    seg = cm["cs"][:, j:j + 1] - cm["cs_t"][j:j + 1, :]
    return jnp.exp(jnp.where(cm["sub"] >= cm["lane"], seg, -jnp.inf))


def _head_decay_t(cm, j):
    seg = cm["cs_t"][j:j + 1, :] - cm["cs"][:, j:j + 1]
    return jnp.exp(jnp.where(cm["lane"] >= cm["sub"], seg, -jnp.inf))


def _conv_window(raw_ref, halo_ref, pad_scr):
    pad_scr[0:HALO, :] = halo_ref[...].astype(F32)[halo_ref.shape[0] - HALO:, :]
    pad_scr[HALO:HALO + Q, :] = raw_ref[...].astype(F32)
    win = pad_scr[...]
    return lambda s: win[HALO:, :] if s == 0 else pltpu.roll(win, s, axis=0)[HALO:, :]


def _xbc_cols(g):
    return slice(512 * g, 512 * g + 512), slice(SSD_W + 128 * g, SSD_W + 128 * g + 128), slice(SSD_W + 256 + 128 * g, SSD_W + 384 + 128 * g)


def ssd_fwd(proj, dt_raw, conv_w, conv_b, dt_bias2, a_log2, d2, norm_w):
    def body(raw_ref, halo_ref, dt_all, z_all, cw_ref, cb_ref, bias_all, alog_all, d_all, nw_all, yn_all, y_all, hp_all,
             h_all, pad_scr, act_scr):
        @pl.when(pl.program_id(0) == 0)
        def _():
            h_all[...] = jnp.zeros_like(h_all)

        pre = _conv(_conv_window(raw_ref, halo_ref, pad_scr), cw_ref[...], cb_ref[...])
        act_scr[...] = pre * _sigmoid(pre)
        for g in range(2):
            wide, thin = slice(512 * g, 512 * g + 512), slice(128 * g, 128 * g + 128)
            xs, bs, cs = _xbc_cols(g)
            group(act_scr.at[:, xs], act_scr.at[:, bs], act_scr.at[:, cs], dt_all.at[:, thin], z_all.at[:, wide], bias_all.at[g],
                  alog_all.at[g], d_all.at[g], nw_all.at[:, wide], yn_all.at[:, wide], y_all.at[:, wide], hp_all.at[g, 0], h_all.at[g])

    def group(x_ref, b_ref, c_ref, dt_ref, z_ref, bias_ref, alog_ref, d_ref, nw_ref, yn_ref, y_ref, hp_ref, h_scr):
        bias = bias_ref[...]
        a_neg = -jnp.exp(alog_ref[...])
        dsk = d_ref[...]
        cm = _ssd_chunk_common(pl.program_id(0) * Q, dt_ref, b_ref, c_ref, bias, a_neg)
        lane_even = cm["lane"] < 64
        for p in range(4):
            je, jo = 2 * p, 2 * p + 1
            xp = x_ref[:, 128 * p:128 * p + 128]
            xdt = xp * _pair(lane_even, cm["dt"], je)
            xdt_b = xdt.astype(BF)
            m_e = (cm["cb"] * _head_decay(cm, je)).astype(BF)
            m_o = (cm["cb"] * _head_decay(cm, jo)).astype(BF)
            zero = jnp.zeros_like(xdt_b)
            yd = (jnp.dot(m_e, jnp.where(lane_even, xdt_b, zero), preferred_element_type=F32)
                  + jnp.dot(m_o, jnp.where(lane_even, zero, xdt_b), preferred_element_type=F32))
            hp = h_scr[p]
            hp_ref[p] = hp
            yo = jnp.dot(cm["cc"], hp.astype(BF), preferred_element_type=F32) * _pair(lane_even, cm["ecs"], je)
            y_ref[:, 128 * p:128 * p + 128] = yd + yo + xp * _pair_row(lane_even, dsk, je)
            st = jnp.dot(cm["bc_t"], (xdt * _pair(lane_even, cm["dsm"], je)).astype(BF), preferred_element_type=F32)
            h_scr[p] = hp * _pair_row(lane_even, cm["gam"], je) + st
        zc = z_ref[:, :].astype(F32)
        gated = y_ref[:, :] * (zc * _sigmoid(zc))
        yn_ref[:, :] = _rms(gated, nw_ref[...]).astype(BF)

    par = _spec((2, 1, 128), lambda c: (0, 0, 0))
    wide = _spec((Q, SSD_W), lambda c: (c, 0))
    xbc = PXBC // XBC
    halo = 2 * HALO
    return pl.pallas_call(
        body, grid=(NCH,),
        in_specs=[_spec((Q, XBC), lambda c: (c, xbc)), _spec((halo, XBC), lambda c: (jnp.maximum(c * (Q // halo) - 1, 0), xbc)),
                  _spec((Q, 256), lambda c: (c, 0)), wide, _spec((4, XBC), lambda c: (0, 0)), _spec((1, XBC), lambda c: (0, 0)),
                  par, par, par, _spec((1, SSD_W), lambda c: (0, 0))],
        out_specs=[wide, wide, _spec((2, 1, 4, 128, 128), lambda c: (0, c, 0, 0, 0))],
        out_shape=[jax.ShapeDtypeStruct((T, SSD_W), BF), jax.ShapeDtypeStruct((T, SSD_W), F32),
                   jax.ShapeDtypeStruct((2, NCH, 4, 128, 128), F32)],
        scratch_shapes=[pltpu.VMEM((2, 4, 128, 128), F32), pltpu.VMEM((Q + HALO, XBC), F32), pltpu.VMEM((Q, XBC), F32)],
        compiler_params=_params(), name="ssd_fwd")(proj, proj, dt_raw, proj, conv_w, conv_b, dt_bias2, a_log2, d2, norm_w)


def ssd_bwd(dyn, proj, dt_raw, conv_w, conv_b, y_pre, h_prev, dt_bias2, a_log2, d2, norm_w, after):
    def body(dyn_all, raw_ref, halo_ref, dt_all, z_all, y_all, hp_all, cw_ref, cb_ref, bias_all, alog_all, d_all, nw_all, _after,
             dz_all, dxbc_ref, ddt_all, dpar_all, dnw_all, dcw_ref, dcb_ref, dh_all, acc_all, pad_scr, act_scr, dsilu_scr, dact_scr, dpad_scr):
        @pl.when(pl.program_id(0) == 0)
        def _():
            dh_all[...] = jnp.zeros_like(dh_all)
            acc_all[...] = jnp.zeros_like(acc_all)
            dnw_all[...] = jnp.zeros_like(dnw_all)
            dcw_ref[...] = jnp.zeros_like(dcw_ref)
            dcb_ref[...] = jnp.zeros_like(dcb_ref)
            dpad_scr[Q:Q + HALO, :] = jnp.zeros((HALO, XBC), F32)

        back = _conv_window(raw_ref, halo_ref, pad_scr)
        pre = _conv(back, cw_ref[...], cb_ref[...])
        sg = _sigmoid(pre)
        act_scr[...] = pre * sg
        dsilu_scr[...] = sg * (1.0 + pre * (1.0 - sg))
        for g in range(2):
            wide, thin = slice(512 * g, 512 * g + 512), slice(128 * g, 128 * g + 128)
            xs, bs, cs = _xbc_cols(g)
            group(dyn_all.at[:, wide], act_scr.at[:, xs], act_scr.at[:, bs], act_scr.at[:, cs], dt_all.at[:, thin], z_all.at[:, wide],
                  y_all.at[:, wide], hp_all.at[g, 0], bias_all.at[g], alog_all.at[g], d_all.at[g], nw_all.at[:, wide],
                  dz_all.at[:, wide], dact_scr.at[:, xs], dact_scr.at[:, bs], dact_scr.at[:, cs], ddt_all.at[:, thin], dpar_all.at[g],
                  dnw_all.at[:, wide], dh_all.at[g], acc_all.at[g])
        dpre = dact_scr[...] * dsilu_scr[...]
        dcw, dcb = _conv_bwd_w(dpre, back)
        dcw_ref[...] += dcw
        dcb_ref[...] += dcb
        dpad_scr[0:Q, :] = dpre
        win = dpad_scr[...]
        dxbc_ref[...] = _conv_bwd_x(lambda s: win[:Q, :] if s == 0 else pltpu.roll(win, Q + HALO - s, axis=0)[:Q, :], cw_ref[...]).astype(BF)
        dpad_scr[Q:Q + HALO, :] = dpre[0:HALO, :]

    def group(dyn_ref, x_ref, b_ref, c_ref, dt_ref, z_ref, y_ref, hp_ref, bias_ref, alog_ref, d_ref, nw_ref,
              dz_ref, dx_ref, db_ref, dc_ref, ddt_ref, dpar_ref, dnw_ref, dh_scr, acc_scr):
        ci = pl.program_id(0)
        bias = bias_ref[...]
        a_neg = -jnp.exp(alog_ref[...])
        dsk = d_ref[...]
        cm = _ssd_chunk_common((NCH - 1 - ci) * Q, dt_ref, b_ref, c_ref, bias, a_neg)
        lane, sub = cm["lane"], cm["sub"]
        lane_even = lane < 64
        cc_t = c_ref[:, :].T.astype(BF)
        cb_t = lax.dot_general(cm["bc"], cm["cc"], NT_DIMS, preferred_element_type=F32)
        zc = z_ref[:, :].astype(F32)
        yc = y_ref[:, :]
        sg = _sigmoid(zc)
        sz = zc * sg
        dgated, dnw = _rms_bwd(dyn_ref[:, :], yc * sz, nw_ref[...])
        dnw_ref[...] += jnp.sum(dnw, axis=0, keepdims=True)
        dz_ref[:, :] = (dgated * yc * (sg * (1.0 + zc * (1.0 - sg)))).astype(BF)
        dy_all = dgated * sz
        dcb = jnp.zeros((Q, Q), F32)
        dcb_t = jnp.zeros((Q, Q), F32)
        db_acc = jnp.zeros((Q, Q), F32)
        dc_acc = jnp.zeros((Q, Q), F32)
        dcs = jnp.zeros((Q, Q), F32)
        ddt = jnp.zeros((Q, Q), F32)
        for p in range(4):
            je, jo = 2 * p, 2 * p + 1
            xp = x_ref[:, 128 * p:128 * p + 128]
            dy = dy_all[:, 128 * p:128 * p + 128]
            dt_p = _pair(lane_even, cm["dt"], je)
            xdt = xp * dt_p
            xdt_b = xdt.astype(BF)
            dy_b = dy.astype(BF)
            zero = jnp.zeros_like(dy_b)
            hp = hp_ref[p]
            hp_b = hp.astype(BF)
            dh = dh_scr[p]
            dh_b = dh.astype(BF)
            acc_scr[p:p + 1, :] += jnp.sum(dy * xp, axis=0, keepdims=True)
            dxp = dy * _pair_row(lane_even, dsk, je)
            e_p = _pair(lane_even, cm["ecs"], je)
            g_p = jnp.dot(cm["cc"], hp_b, preferred_element_type=F32)
            dg_b = (dy * e_p).astype(BF)
            de = dy * g_p * e_p
            dc_acc = dc_acc + lax.dot_general(dg_b, hp_b, NT_DIMS, preferred_element_type=F32)
            dh_in = jnp.dot(cc_t, dg_b, preferred_element_type=F32)
            ds_p = _pair(lane_even, cm["dsm"], je)
            r_p = jnp.dot(cm["bc"], dh_b, preferred_element_type=F32)
            dxdt = r_p * ds_p
            tt = r_p * xdt * ds_p
            db_acc = db_acc + lax.dot_general((xdt * ds_p).astype(BF), dh_b, NT_DIMS, preferred_element_type=F32)
            dgam_m = jnp.sum(dh * hp, axis=0, keepdims=True)
            for j, even in ((je, True), (jo, False)):
                sel = lane_even if even else jnp.logical_not(lane_even)
                dy_j = jnp.where(sel, dy_b, zero)
                l_j = _head_decay(cm, j)
                l_jt = _head_decay_t(cm, j)
                m_j = cm["cb"] * l_j
                m_jt = cb_t * l_jt
                dm = lax.dot_general(dy_j, xdt_b, NT_DIMS, preferred_element_type=F32)
                dm_t = lax.dot_general(xdt_b, dy_j, NT_DIMS, preferred_element_type=F32)
                dxdt = dxdt + jnp.dot(m_jt.astype(BF), dy_j, preferred_element_type=F32)
                dcb = dcb + dm * l_j
                dcb_t = dcb_t + dm_t * l_jt
                t_j = jnp.where(sel, tt, 0.0)
                col = jnp.sum(dm * m_j - dm_t * m_jt + (jnp.where(sel, de, 0.0) - t_j), axis=1, keepdims=True)
                gam_j = cm["gam"][:, j:j + 1]
                last = (jnp.sum(jnp.sum(t_j, axis=0, keepdims=True), axis=1, keepdims=True)
                        + jnp.sum(jnp.where(sel[0:1, :], dgam_m, 0.0), axis=1, keepdims=True) * gam_j)
                col = col + jnp.where(sub[:, 0:1] == Q - 1, last, 0.0)
                dcs = dcs + jnp.where(lane == j, col, 0.0)
            dh_scr[p] = dh_in + dh * _pair_row(lane_even, cm["gam"], je)
            dx_ref[:, 128 * p:128 * p + 128] = dxp + dxdt * dt_p
            dd = dxdt * xp
            ddt = ddt + jnp.where(lane == je, jnp.sum(jnp.where(lane_even, dd, 0.0), axis=1, keepdims=True), 0.0)
            ddt = ddt + jnp.where(lane == jo, jnp.sum(jnp.where(lane_even, 0.0, dd), axis=1, keepdims=True), 0.0)
        dc_ref[:, :] = dc_acc + jnp.dot(dcb.astype(BF), cm["bc"], preferred_element_type=F32)
        db_ref[:, :] = db_acc + jnp.dot(dcb_t.astype(BF), cm["cc"], preferred_element_type=F32)
        tri_t = (sub <= lane).astype(F32)
        dd_a = jnp.dot(tri_t, dcs, precision=lax.Precision.HIGHEST, preferred_element_type=F32)
        ddt = ddt + dd_a * a_neg
        acc_scr[5:6, :] += jnp.sum(dd_a * cm["dt"], axis=0, keepdims=True)
        draw = jnp.where(cm["live"], ddt * _sigmoid_gate(cm["dtr"] + bias), 0.0)
        acc_scr[4:5, :] += jnp.sum(draw, axis=0, keepdims=True)
        ddt_ref[:, :] = draw.astype(BF)

        @pl.when(ci == NCH - 1)
        def _():
            lane1 = _lanes((1, 128))
            dd = jnp.zeros((1, 128), F32)
            for p in range(4):
                row = acc_scr[p:p + 1, :]
                dd = dd + jnp.where(lane1 == 2 * p, jnp.sum(jnp.where(lane1 < 64, row, 0.0), axis=1, keepdims=True), 0.0)
                dd = dd + jnp.where(lane1 == 2 * p + 1, jnp.sum(jnp.where(lane1 < 64, 0.0, row), axis=1, keepdims=True), 0.0)
            dpar_ref[...] = jnp.concatenate([acc_scr[4:5, :], acc_scr[5:6, :] * a_neg, dd, jnp.zeros((5, 128), F32)], axis=0)

    par = _spec((2, 1, 128), lambda c: (0, 0, 0))
    wide = _spec((Q, SSD_W), lambda c: (NCH - 1 - c, 0))
    thin = _spec((Q, 256), lambda c: (NCH - 1 - c, 0))
    vec = _spec((1, SSD_W), lambda c: (0, 0))
    xbc = PXBC // XBC
    halo = 2 * HALO
    chunk = pltpu.VMEM((Q, XBC), F32)
    padded = pltpu.VMEM((Q + HALO, XBC), F32)
    return pl.pallas_call(
        body, grid=(NCH,),
        in_specs=[wide, _spec((Q, XBC), lambda c: (NCH - 1 - c, xbc)),
                  _spec((halo, XBC), lambda c: (jnp.maximum((NCH - 1 - c) * (Q // halo) - 1, 0), xbc)), thin, wide, wide,
                  _spec((2, 1, 4, 128, 128), lambda c: (0, NCH - 1 - c, 0, 0, 0)), _spec((4, XBC), lambda c: (0, 0)),
                  _spec((1, XBC), lambda c: (0, 0)), par, par, par, vec, ANY_SPEC],
        out_specs=[wide, _spec((Q, XBC), lambda c: (NCH - 1 - c, 0)), thin, _spec((2, 8, 128), lambda c: (0, 0, 0)), vec,
                   _spec((4, XBC), lambda c: (0, 0)), _spec((1, XBC), lambda c: (0, 0))],
        out_shape=[jax.ShapeDtypeStruct((T, SSD_W), BF), jax.ShapeDtypeStruct((T, XBC), BF), jax.ShapeDtypeStruct((T, 256), BF),
                   jax.ShapeDtypeStruct((2, 8, 128), F32), jax.ShapeDtypeStruct((1, SSD_W), F32), jax.ShapeDtypeStruct((4, XBC), F32),
                   jax.ShapeDtypeStruct((1, XBC), F32)],
        scratch_shapes=[pltpu.VMEM((2, 4, 128, 128), F32), pltpu.VMEM((2, 8, 128), F32), padded, chunk, chunk, chunk, padded],
        compiler_params=_params(), name="ssd_bwd")(dyn, proj, proj, dt_raw, proj, y_pre, h_prev, conv_w, conv_b, dt_bias2, a_log2, d2, norm_w, after)


def _lru_gates(back, cw, cb, wa, ba, wx, bx, lam):
    xr = _conv(back, cw, cb)
    xr_b = xr.astype(BF)
    r = _sigmoid_gate(jnp.dot(xr_b, wa, preferred_element_type=F32) + ba)
    i = _sigmoid_gate(jnp.dot(xr_b, wx, preferred_element_type=F32) + bx)
    sp = _softplus(-lam)
    la = (-LRU_C) * r * sp
    a = jnp.exp(la)
    mult2 = -jnp.tanh(la) * (a * a + 1.0)
    return xr, xr_b, r, i, sp, a, jnp.sqrt(mult2), mult2


SEG_LEN = 68
SEGS = T // SEG_LEN


def _seg_rows(j, k, off=0):
    return pl.ds(off + j * 8 * SEG_LEN + k, 8, stride=SEG_LEN)


def _segmented_scan(mul_ref, mul_row0, add_ref, out_ref, loc_scr, prod_scr, carry_scr, reverse):
    groups = SEGS // 8
    off = mul_row0 + (1 if reverse else 0)

    def local(i, carry):
        k = SEG_LEN - 1 - i if reverse else i
        new = []
        for j in range(groups):
            h, p = carry[2 * j], carry[2 * j + 1]
            m = mul_ref[_seg_rows(j, k, off), :]
            h = m * h + add_ref[_seg_rows(j, k), :]
            p = m * p
            loc_scr[_seg_rows(j, k), :] = h
            prod_scr[_seg_rows(j, k), :] = p
            new += [h, p]
        return tuple(new)

    lax.fori_loop(0, SEG_LEN, local, (jnp.zeros((8, 128), F32), jnp.ones((8, 128), F32)) * groups)

    def chain(i, c):
        s = SEGS - 1 - i if reverse else i
        carry_scr[pl.ds(s, 1), :] = c
        edge = s * SEG_LEN + (0 if reverse else SEG_LEN - 1)
        return loc_scr[pl.ds(edge, 1), :] + prod_scr[pl.ds(edge, 1), :] * c

    lax.fori_loop(0, SEGS, chain, jnp.zeros((1, 128), F32))

    def fold(k, carry):
        for j in range(groups):
            rows = _seg_rows(j, k)
            out_ref[rows, :] = loc_scr[rows, :] + prod_scr[rows, :] * carry_scr[8 * j:8 * j + 8, :]
        return carry

    lax.fori_loop(0, SEG_LEN, fold, 0)


def lru_fwd(proj, cw, cb, wa2, ba, wx2, bx, lam, after):
    def body(x_ref, cw_ref, cb_ref, wa_ref, ba_ref, wx_ref, bx_ref, lam_ref, _after, h_ref, a_ref, xpad, u_scr, loc_scr, prod_scr, carry_scr):
        _fill_padded(xpad, x_ref)

        def chunk(r0):
            xr, _, _, i, _, a, mult, _ = _lru_gates(_back(xpad, r0), cw_ref[...], cb_ref[...], wa_ref[0], ba_ref[...], wx_ref[0], bx_ref[...],
                                                 lam_ref[...])
            a_ref[pl.ds(r0, Q), :] = a
            u_scr[pl.ds(r0, Q), :] = jnp.where(_rows(a.shape, r0) >= NPAD, mult * (i * xr), 0.0)

        _chunks(chunk, unrolled=True)
        _segmented_scan(a_ref, 0, u_scr, h_ref, loc_scr, prod_scr, carry_scr, reverse=False)

    c0 = PXL // 128
    vec = _spec((1, 128), lambda c: (0, c))
    mat = _spec((1, 128, 128), lambda c: (c, 0, 0))
    seq = pltpu.VMEM((T, 128), F32)
    return pl.pallas_call(
        body, grid=(8,),
        in_specs=[_spec((T, 128), lambda c: (0, c0 + c)), _spec((4, 128), lambda c: (0, c)), vec, mat, vec, mat, vec, vec, ANY_SPEC],
        out_specs=[_spec((T, 128), lambda c: (0, c)), _spec((T, 128), lambda c: (0, c))],
        out_shape=[jax.ShapeDtypeStruct((T, LRU_W), F32), jax.ShapeDtypeStruct((T, LRU_W), F32)],
        scratch_shapes=[pltpu.VMEM((T + 2 * HALO, 128), F32), seq, seq, seq, pltpu.VMEM((SEGS, 128), F32)],
        compiler_params=_params(), name="lru_fwd")(proj, cw, cb, wa2, ba, wx2, bx, lam, after)


def lru_bwd(dh_out, a, hseq, proj, cw, cb, wa2, ba, wx2, bx, lam):
    def body(d_ref, a_ref, h_ref, x_ref, cw_ref, cb_ref, wa_ref, ba_ref, wx_ref, bx_ref, lam_ref,
             dx_ref, dcw_ref, dcb_ref, dwa_ref, dba_ref, dwx_ref, dbx_ref, dlam_ref, xpad, hpad, dpad, dh_ref, loc_scr, prod_scr, carry_scr):
        _fill_padded(dpad, a_ref)
        _segmented_scan(dpad, HALO, d_ref, dh_ref, loc_scr, prod_scr, carry_scr, reverse=True)
        _fill_padded(xpad, x_ref)
        _fill_padded(hpad, h_ref)
        dpad[0:HALO, :] = jnp.zeros((HALO, 128), F32)
        dpad[T + HALO:T + 2 * HALO, :] = jnp.zeros((HALO, 128), F32)
        for ref in (dcw_ref, dcb_ref, dwa_ref, dba_ref, dwx_ref, dbx_ref, dlam_ref):
            ref[...] = jnp.zeros_like(ref)
        lam = lam_ref[...]

        def first(r0):
            back = _back(xpad, r0)
            xr, xr_b, r, i, sp, a, mult, mult2 = _lru_gates(back, cw_ref[...], cb_ref[...], wa_ref[0], ba_ref[...], wx_ref[0], bx_ref[...], lam)
            dh = dh_ref[pl.ds(r0, Q), :]
            da = dh * _back(hpad, r0)(1)
            du = jnp.where(_rows(dh.shape, r0) >= NPAD, dh, 0.0)
            dmult = du * (i * xr)
            di = du * (mult * xr)
            dxr = du * (mult * i)
            dla = da * a - dmult * (a * a) * lax.rsqrt(mult2)
            dr = dla * ((-LRU_C) * sp)
            dlam_ref[...] += jnp.sum(dla * ((-LRU_C) * r), axis=0, keepdims=True)
            dpr = dr * r * (1.0 - r)
            dpi = di * i * (1.0 - i)
            dba_ref[...] += jnp.sum(dpr, axis=0, keepdims=True)
            dbx_ref[...] += jnp.sum(dpi, axis=0, keepdims=True)
            dpr_b = dpr.astype(BF)
            dpi_b = dpi.astype(BF)
            dxr = (dxr + lax.dot_general(dpr_b, wa_ref[0], NT_DIMS, preferred_element_type=F32)
                   + lax.dot_general(dpi_b, wx_ref[0], NT_DIMS, preferred_element_type=F32))
            dwa_ref[0] += lax.dot_general(xr_b, dpr_b, TN_DIMS, preferred_element_type=F32)
            dwx_ref[0] += lax.dot_general(xr_b, dpi_b, TN_DIMS, preferred_element_type=F32)
            dpad[pl.ds(r0 + HALO, Q), :] = dxr
            dcw, dcb = _conv_bwd_w(dxr, back)
            dcw_ref[...] += dcw
            dcb_ref[...] += dcb

        _chunks(first, unrolled=True)
        dlam_ref[...] = -dlam_ref[...] * _sigmoid_gate(-lam)

        def second(r0):
            dx_ref[pl.ds(r0, Q), :] = _conv_bwd_x(_ahead(dpad, r0), cw_ref[...]).astype(BF)

        _chunks(second)

    c0 = PXL // 128
    vec = _spec((1, 128), lambda c: (0, c))
    mat = _spec((1, 128, 128), lambda c: (c, 0, 0))
    col = _spec((T, 128), lambda c: (0, c))
    vshape = jax.ShapeDtypeStruct((1, LRU_W), F32)
    mshape = jax.ShapeDtypeStruct((8, 128, 128), F32)
    pad = pltpu.VMEM((T + 2 * HALO, 128), F32)
    seq = pltpu.VMEM((T, 128), F32)
    return pl.pallas_call(
        body, grid=(8,),
        in_specs=[col, col, col, _spec((T, 128), lambda c: (0, c0 + c)), _spec((4, 128), lambda c: (0, c)), vec, mat, vec, mat, vec, vec],
        out_specs=[col, _spec((4, 128), lambda c: (0, c)), vec, mat, vec, mat, vec, vec],
        out_shape=[jax.ShapeDtypeStruct((T, LRU_W), BF), jax.ShapeDtypeStruct((4, LRU_W), F32), vshape, mshape, vshape, mshape, vshape, vshape],
        scratch_shapes=[pad, pad, pad, seq, seq, seq, pltpu.VMEM((SEGS, 128), F32)],
        compiler_params=_params(), name="lru_bwd")(dh_out, a, hseq, proj, cw, cb, wa2, ba, wx2, bx, lam)


FF_TILE = 256
FF_TILE_ROWS = list(range(0, D_FF, FF_TILE))


def gate_up(h1, wn, w_gate, w_up):
    def body(h_ref, wn_ref, wg_hbm, wu_hbm, gt_ref, up_ref, act_ref, u_ref, wg_ref, wu_ref, wg_sems, wu_sems):
        gate_ready = _arriving(wg_hbm, wg_ref, wg_sems, FF_TILE_ROWS, FF_TILE)
        up_ready = _arriving(wu_hbm, wu_ref, wu_sems, FF_TILE_ROWS, FF_TILE)
        for r in (0, HALF):
            u_ref[r:r + HALF, :] = _rms(h_ref[r:r + HALF, :], wn_ref[...]).astype(BF)

        def tile(c0):
            cols = pl.ds(c0, FF_TILE)
            gate_ready(c0 // FF_TILE)
            up_ready(c0 // FF_TILE)
            gt = lax.dot_general(u_ref[...], wg_ref[cols, :], NT_DIMS, preferred_element_type=F32)
            up = lax.dot_general(u_ref[...], wu_ref[cols, :], NT_DIMS, preferred_element_type=F32)
            gt_ref[:, cols] = gt.astype(BF)
            up_ref[:, cols] = up.astype(BF)
            act_ref[:, cols] = (gt * _sigmoid(gt) * up).astype(BF)

        _col_tiles(D_FF, FF_TILE, tile)

    big = jax.ShapeDtypeStruct((T, D_FF), BF)
    return pl.pallas_call(
        body, grid=(T // RC,), in_specs=[_rows_spec(D), _vec(D), ANY_SPEC, ANY_SPEC],
        out_specs=[_rows_spec(D_FF), _rows_spec(D_FF), _rows_spec(D_FF), _rows_spec(D)],
        out_shape=[big, big, big, jax.ShapeDtypeStruct((T, D), BF)],
        scratch_shapes=[pltpu.VMEM((D_FF, D), BF)] * 2 + [pltpu.SemaphoreType.DMA((len(FF_TILE_ROWS),))] * 2,
        compiler_params=_params(), name="gate_up")(h1, wn, w_gate, w_up)


def down_loss(act, w_down, h1, target, wf):
    first = NPAD + N_META

    def body(a_ref, w_ref, r_ref, t_hbm, wf_ref, d_ref, db_ref, l_ref, dw_ref, h_scr, t_ref, t_sem):
        i = pl.program_id(0)
        _zero_at_first(l_ref, dw_ref)
        head = pltpu.make_async_copy(t_hbm.at[pl.ds(0, RC - first)], t_ref.at[pl.ds(first, RC - first)], t_sem)
        rest = pltpu.make_async_copy(t_hbm.at[pl.ds(pl.multiple_of(jnp.maximum(i * RC - first, 0), 32), RC)], t_ref, t_sem)

        @pl.when(i == 0)
        def _():
            t_ref[0:first, :] = jnp.zeros((first, D), F32)
            head.start()

        @pl.when(i > 0)
        def _():
            rest.start()

        def tile(c0):
            cols = pl.ds(c0, 512)
            h_scr[:, cols] = r_ref[:, cols] + jnp.dot(a_ref[...], w_ref[:, cols], preferred_element_type=F32)

        _col_tiles(D, 512, tile)

        @pl.when(i == 0)
        def _():
            head.wait()

        @pl.when(i > 0)
        def _():
            rest.wait()

        for r in (0, HALF):
            h = h_scr[r:r + HALF, :]
            live = _rows((HALF, D), i * RC + r) >= first
            err = jnp.where(live, _rms(h, wf_ref[...]) - t_ref[r:r + HALF, :], 0.0)
            l_ref[...] += 0.5 * jnp.sum(jnp.sum(err * err, axis=1, keepdims=True) * (1.0 / D), axis=0, keepdims=True)
            dh, dw = _rms_bwd(err * (1.0 / D), h, wf_ref[...])
            dw_ref[...] += jnp.sum(dw, axis=0, keepdims=True)
            d_ref[r:r + HALF, :] = dh
            db_ref[r:r + HALF, :] = dh.astype(BF)

    return pl.pallas_call(
        body, grid=(T // RC,),
        in_specs=[_rows_spec(D_FF), _whole((D_FF, D)), _rows_spec(D), pl.BlockSpec(memory_space=pl.ANY), _vec(D)],
        out_specs=[_rows_spec(D), _rows_spec(D), _spec((1, 128), lambda i: (0, 0)), _vec(D)],
        out_shape=[jax.ShapeDtypeStruct((T, D), F32), jax.ShapeDtypeStruct((T, D), BF), jax.ShapeDtypeStruct((1, 128), F32),
                   jax.ShapeDtypeStruct((1, D), F32)],
        scratch_shapes=[pltpu.VMEM((RC, D), F32), pltpu.VMEM((RC, D), F32), pltpu.SemaphoreType.DMA],
        compiler_params=_params(), name="down_loss")(act, w_down, h1, target, wf)


def swiglu_bwd(dh2_b, w_down, gt, up, act, u2):
    tn = 256

    def body(d_hbm, u_hbm, w_ref, gt_ref, up_ref, act_ref, dg_ref, du_ref, gd_ref, gg_ref, gu_ref, d_ref, u_ref, d_sems, u_sems):
        chunks = list(range(0, T, RC))
        d_ready = _arriving(d_hbm, d_ref, d_sems, chunks, RC)
        u_ready = _arriving(u_hbm, u_ref, u_sems, chunks, RC)

        def rows(r0):
            part = pl.ds(r0, RC)
            d_ready(r0 // RC)
            dact = lax.dot_general(d_ref[part, :], w_ref[...], NT_DIMS, preferred_element_type=F32)
            gt_ = gt_ref[part, :].astype(F32)
            up_ = up_ref[part, :].astype(F32)
            sg = _sigmoid(gt_)
            dg_ref[part, :] = (dact * up_ * (sg * (1.0 + gt_ * (1.0 - sg)))).astype(BF)
            du_ref[part, :] = (dact * (gt_ * sg)).astype(BF)

        _col_tiles(T, RC, rows)
        for k in range(len(chunks)):
            u_ready(k)
        gd_ref[...] = lax.dot_general(act_ref[...], d_ref[...], TN_DIMS, preferred_element_type=F32).astype(BF)
        gg_ref[...] = lax.dot_general(dg_ref[...], u_ref[...], TN_DIMS, preferred_element_type=F32).astype(BF)
        gu_ref[...] = lax.dot_general(du_ref[...], u_ref[...], TN_DIMS, preferred_element_type=F32).astype(BF)

    cols = _spec((T, tn), lambda j: (0, j))
    wrow = _spec((tn, D), lambda j: (j, 0))
    big = jax.ShapeDtypeStruct((T, D_FF), BF)
    grad = jax.ShapeDtypeStruct((D_FF, D), BF)
    return pl.pallas_call(
        body, grid=(D_FF // tn,), in_specs=[ANY_SPEC, ANY_SPEC, wrow, cols, cols, cols],
        out_specs=[cols, cols, wrow, wrow, wrow], out_shape=[big, big, grad, grad, grad],
        scratch_shapes=[pltpu.VMEM((T, D), BF)] * 2 + [pltpu.SemaphoreType.DMA((T // RC,))] * 2,
        compiler_params=_params(), name="swiglu_bwd")(dh2_b, u2, w_down, gt, up, act)


def gate_up_bwd(dgt, dup, w_gate, w_up, h1, wn, dh2):
    def body(dg_ref, du_ref, wg_hbm, wu_hbm, h_ref, wn_ref, r_ref, d_ref, db_ref, dw_ref, du_scr, wg_ref, wu_ref, wg_sems, wu_sems):
        gate_ready = _arriving(wg_hbm, wg_ref, wg_sems, FF_TILE_ROWS, FF_TILE)
        up_ready = _arriving(wu_hbm, wu_ref, wu_sems, FF_TILE_ROWS, FF_TILE)
        _zero_at_first(dw_ref)

        du_scr[...] = jnp.zeros_like(du_scr)

        def tile(c0):
            k = pl.ds(c0, FF_TILE)
            gate_ready(c0 // FF_TILE)
            up_ready(c0 // FF_TILE)
            du_scr[...] += (jnp.dot(dg_ref[:, k], wg_ref[k, :], preferred_element_type=F32)
                            + jnp.dot(du_ref[:, k], wu_ref[k, :], preferred_element_type=F32))

        _col_tiles(D_FF, FF_TILE, tile)
        for r in (0, HALF):
            dh, dw = _rms_bwd(du_scr[r:r + HALF, :], h_ref[r:r + HALF, :], wn_ref[...])
            dw_ref[...] += jnp.sum(dw, axis=0, keepdims=True)
            dh = dh + r_ref[r:r + HALF, :]
            d_ref[r:r + HALF, :] = dh
            db_ref[r:r + HALF, :] = dh.astype(BF)

    return pl.pallas_call(
        body, grid=(T // RC,),
        in_specs=[_rows_spec(D_FF), _rows_spec(D_FF), ANY_SPEC, ANY_SPEC, _rows_spec(D), _vec(D), _rows_spec(D)],
        out_specs=[_rows_spec(D), _rows_spec(D), _vec(D)],
        out_shape=[jax.ShapeDtypeStruct((T, D), F32), jax.ShapeDtypeStruct((T, D), BF), jax.ShapeDtypeStruct((1, D), F32)],
        scratch_shapes=[pltpu.VMEM((RC, D), F32)] + [pltpu.VMEM((D_FF, D), BF)] * 2 + [pltpu.SemaphoreType.DMA((len(FF_TILE_ROWS),))] * 2,
        compiler_params=_params(), name="gate_up_bwd")(dgt, dup, w_gate, w_up, h1, wn, dh2)


def _adamw(w, g, m, v):
    m = ADAM_B1 * m + (1.0 - ADAM_B1) * g
    v = ADAM_B2 * v + (1.0 - ADAM_B2) * (g * g)
    m_hat = m / (1.0 - ADAM_B1 ** ADAM_STEP)
    v_hat = v / (1.0 - ADAM_B2 ** ADAM_STEP)
    delta = -ADAM_LR * (m_hat / (jnp.sqrt(v_hat) + ADAM_EPS) + ADAM_WD * w)
    return delta, m, v


def adamw_shards(name, recvs, ws, ms, vs):
    n = len(ws)

    def body(*refs):
        ins, outs = refs[:4 * n], refs[4 * n:]
        for k in range(n):
            p_ref, w_ref, m_ref, v_ref = ins[k], ins[n + k], ins[2 * n + k], ins[3 * n + k]
            g = p_ref[0].astype(F32)
            for s in range(1, 8):
                g = g + p_ref[s].astype(F32)
            outs[4 * k][...] = g
            outs[4 * k + 1][...], outs[4 * k + 2][...], outs[4 * k + 3][...] = _adamw(w_ref[...], g, m_ref[...], v_ref[...])

    tiles = [_spec((w.shape[0] // 2, w.shape[1]), lambda i: (i, 0)) for w in ws]
    recv_tiles = [_spec((8, w.shape[0] // 2, w.shape[1]), lambda i: (0, i, 0)) for w in ws]
    res = pl.pallas_call(
        body, grid=(2,), in_specs=recv_tiles + tiles * 3,
        out_specs=[t for t in tiles for _ in range(4)],
        out_shape=[jax.ShapeDtypeStruct(w.shape, F32) for w in ws for _ in range(4)],
        compiler_params=_params(), name=name)(*recvs, *ws, *ms, *vs)
    return [list(res[4 * k:4 * k + 4]) for k in range(n)]


def adamw_w_in(recv, w, m, v, after):
    rows = 34
    per_row = D // 128

    def body(p_ref, w_ref, m_ref, v_ref, _after, g_ref, d_ref, mo_ref, vo_ref):
        def chunk(c, carry):
            lines = pl.ds(pl.multiple_of(c * per_row * rows, 16), per_row * rows)
            g = p_ref[0, lines, :].astype(F32)
            for s in range(1, 8):
                g = g + p_ref[s, lines, :].astype(F32)
            g = g.reshape(rows, per_row, 128)
            part = pl.ds(c * rows, rows)
            g_ref[part] = g
            d_ref[part], mo_ref[part], vo_ref[part] = _adamw(w_ref[part], g, m_ref[part], v_ref[part])
            return carry

        lax.fori_loop(0, w.shape[0] // rows, chunk, 0)

    shape = jax.ShapeDtypeStruct(w.shape, F32)
    whole = pl.BlockSpec(memory_space=pltpu.VMEM)
    return pl.pallas_call(body, out_shape=[shape] * 4, in_specs=[whole] * 4 + [ANY_SPEC], compiler_params=_params(0),
                          name="adamw_w_in")(recv, w, m, v, after)


def sum_slabs(recv):
    def body(p_ref, o_ref):
        g = p_ref[0]
        for s in range(1, 8):
            g = g + p_ref[s]
        for s in range(8):
            o_ref[s] = g

    return pl.pallas_call(body, out_shape=jax.ShapeDtypeStruct(recv.shape, F32), compiler_params=_params(0), name="sum_slabs")(recv)


SIMPLE = [("norm1_w", 1024), ("ssd_conv_b", 1536), ("ssd_dt_bias", 16), ("ssd_a_log", 16), ("ssd_d", 16), ("ssd_norm_w", 1024),
          ("lru_conv_b", 1024), ("lru_ba", 1024), ("lru_bx", 1024), ("lru_lambda", 1024), ("lru_norm_w", 1024), ("norm2_w", 1024),
          ("final_norm_w", 1024)]
SPECIAL = ["lru_wa", "lru_wx", "meta_tokens", "ssd_conv_w", "lru_conv_w"]
SM_ROWS = 176
SM_WA, SM_WX, SM_META, SM_SCW, SM_LCW, SM_LOSS = 14, 78, 142, 158, 166, 170


def _simple_rows():
    rows, r = {}, 0
    for name, n in SIMPLE:
        rows[name] = r
        r += -(-n // 1024)
    return rows


def adamw_small(sm, special_g, ws, ms, vs):
    rows = _simple_rows()
    ns, nx = len(SIMPLE), len(SPECIAL)

    def body(*refs):
        sm_ref = refs[0]
        gx = refs[1:1 + nx]
        wr = refs[1 + nx:1 + nx + ns + nx]
        mr = refs[1 + nx + ns + nx:1 + nx + 2 * (ns + nx)]
        vr = refs[1 + nx + 2 * (ns + nx):1 + nx + 3 * (ns + nx)]
        outs = refs[1 + nx + 3 * (ns + nx):]
        o = 0
        for k, (name, n) in enumerate(SIMPLE):
            r0 = rows[name]
            for c0 in range(0, n, 1024):
                wd = min(1024, n - c0)
                g = sm_ref[r0 + c0 // 1024:r0 + c0 // 1024 + 1, 0:wd]
                sl = (slice(None), slice(c0, c0 + wd))
                d, m2, v2 = _adamw(wr[k][sl], g, mr[k][sl], vr[k][sl])
                outs[o][sl] = g
                outs[o + 1][sl] = d
                outs[o + 2][sl] = m2
                outs[o + 3][sl] = v2
            o += 4
        for k in range(nx):
            d, m2, v2 = _adamw(wr[ns + k][...], gx[k][...], mr[ns + k][...], vr[ns + k][...])
            outs[o][...] = d
            outs[o + 1][...] = m2
            outs[o + 2][...] = v2
            o += 3

    out_shape = []
    for k in range(ns):
        out_shape += [jax.ShapeDtypeStruct(ws[k].shape, F32)] * 4
    for k in range(nx):
        out_shape += [jax.ShapeDtypeStruct(ws[ns + k].shape, F32)] * 3
    return pl.pallas_call(body, out_shape=out_shape, compiler_params=_params(0), name="adamw_small")(sm, *special_g, *ws, *ms, *vs)


def _place():
    return lax.axis_index("x"), lax.axis_index("y"), lax.axis_index("c")


def _index(px, py, pc):
    return 4 * px + 2 * py + pc


def all_gather(name, shards):
    n = len(shards)
    hbm = pl.BlockSpec(memory_space=pl.ANY)

    def pieces(s):
        tile = 32 // s.dtype.itemsize
        per = s.shape[0] // tile // 4 * tile
        return [(0, s.shape[0])] if s.shape[0] < 256 else [(r * per, per if r < 3 else s.shape[0] - 3 * per) for r in range(4)]

    parts = [pieces(s) for s in shards]
    first_sem = [7 * sum(len(p) for p in parts[:i]) for i in range(n + 1)]

    def body(*refs):
        ins, outs = refs[:n], refs[n:2 * n]
        send_sems, recv_sems, local_sems = refs[2 * n:]
        x, y, c = _place()
        me, sibling = (x, y, c), (x, y, 1 - c)
        chips = [(1 - x, y), (x, 1 - y), (1 - x, 1 - y)]

        def copy(i, r, k, block, to, src=None):
            rows = pl.ds(*parts[i][r])
            dst = outs[i].at[_index(*block), rows]
            sem = first_sem[i] + 7 * r + k
            return pltpu.make_async_remote_copy(src_ref=dst if src is None else src.at[rows], dst_ref=dst, send_sem=send_sems.at[sem],
                                                recv_sem=recv_sems.at[sem], device_id=to, device_id_type=MESH)

        every = [(i, r) for i in range(n) for r in range(len(parts[i]))]
        mine = [pltpu.make_async_copy(ins[i], outs[i].at[_index(*me)], local_sems.at[i]) for i in range(n)]
        for cp in mine:
            cp.start()
        first = []
        for i, r in every:
            first += [copy(i, r, 1 + j, me, (*chip, c), src=ins[i]) for j, chip in enumerate(chips)]
            first.append(copy(i, r, 0, me, sibling, src=ins[i]))
        for cp in first:
            cp.start()
        passed = []
        for i, r in every:
            for j, chip in enumerate(chips):
                copy(i, r, 1 + j, (*chip, c), me).wait_recv()
                cp = copy(i, r, 4 + j, (*chip, c), sibling)
                cp.start()
                passed.append(cp)
        for i, r in every:
            copy(i, r, 0, sibling, me).wait_recv()
            for j, chip in enumerate(chips):
                copy(i, r, 4 + j, (*chip, 1 - c), me).wait_recv()
        for cp in first + passed:
            cp.wait_send()
        for cp in mine:
            cp.wait()

    return pl.pallas_call(
        body, in_specs=[hbm] * n, out_specs=[hbm] * n,
        out_shape=[jax.ShapeDtypeStruct((8,) + s.shape, s.dtype) for s in shards],
        scratch_shapes=[pltpu.SemaphoreType.DMA((first_sem[n],)), pltpu.SemaphoreType.DMA((first_sem[n],)), pltpu.SemaphoreType.DMA((n,))],
        name=name)(*shards)


HBM_SPEC = pl.BlockSpec(memory_space=pltpu.HBM)
SEM_SPEC = pl.BlockSpec(memory_space=pltpu.SEMAPHORE)
EFFECT = pltpu.SideEffectType.DATAFLOW_SIDE_EFFECTING


def _peers(x, y, c):
    return [((1 - x) if k & 4 else x, (1 - y) if k & 2 else y, (1 - c) if k & 1 else c) for k in range(1, 8)]


def _pieces(rows):
    for n in (4, 2):
        if rows % (16 * n) == 0:
            return [(r * (rows // n), rows // n) for r in range(n)]
    return [(0, rows)]


def _peer_copies(src, land, send_sems, recv_sems, k, peer, mine):
    block = src.at[_index(*peer)]
    return [pltpu.make_async_remote_copy(src_ref=block.at[pl.ds(r0, nr)], dst_ref=land.at[mine, pl.ds(r0, nr)], send_sem=send_sems.at[k],
                                         recv_sem=recv_sems.at[k], device_id=peer, device_id_type=MESH)
            for r0, nr in _pieces(block.shape[0])]


OWN = 7


def _own_copy(src, land, send_sems, mine):
    return pltpu.make_async_copy(src.at[mine], land.at[mine], send_sems.at[OWN])


def copies_start(name, srcs):
    n = len(srcs)

    def body(*refs):
        ins, lands = refs[:n], refs[n:2 * n]
        sends, recvs = refs[2 * n:3 * n], refs[3 * n:4 * n]
        token = refs[-1]
        x, y, c = _place()
        mine = _index(x, y, c)
        for i in range(n):
            per_peer = [_peer_copies(ins[i], lands[i], sends[i], recvs[i], k, peer, mine) for k, peer in enumerate(_peers(x, y, c))]
            for piece in zip(*per_peer):
                for cp in piece:
                    cp.start()
            _own_copy(ins[i], lands[i], sends[i], mine).start()
        token[...] = jnp.zeros_like(token)

    res = pl.pallas_call(
        body, name=name,
        out_shape=([pltpu.SemaphoreType.DMA((8,))] * n + [pltpu.SemaphoreType.DMA((7,))] * n + [pltpu.HBM(s.shape, s.dtype) for s in srcs] * 2
                   + [jax.ShapeDtypeStruct((8, 128), F32)]),
        in_specs=[HBM_SPEC] * (2 * n),
        out_specs=[SEM_SPEC] * (2 * n) + [HBM_SPEC] * (2 * n) + [pl.BlockSpec(memory_space=pltpu.VMEM)],
        input_output_aliases={i: 2 * n + i for i in range(2 * n)},
        compiler_params=pltpu.CompilerParams(has_side_effects=EFFECT),
    )(*[pltpu.with_memory_space_constraint(s, pltpu.HBM) for s in srcs],
      *[pltpu.with_memory_space_constraint(lax.empty(s.shape, s.dtype), pltpu.HBM) for s in srcs])
    return [(res[i], res[n + i], res[2 * n + i], res[3 * n + i]) for i in range(n)], res[-1]


def copies_wait(name, started, after):
    n = len(started)

    def body(*refs):
        ins, lands = refs[:n], refs[n:2 * n]
        sends, recvs = refs[2 * n:3 * n], refs[3 * n:4 * n]
        x, y, c = _place()
        mine = _index(x, y, c)
        for i in range(n):
            for k, peer in enumerate(_peers(x, y, c)):
                arrival = pltpu.make_async_remote_copy(src_ref=ins[i].at[mine], dst_ref=lands[i].at[_index(*peer)],
                                                       send_sem=sends[i].at[k], recv_sem=recvs[i].at[k], device_id=peer, device_id_type=MESH)
                arrival.wait_send()
                arrival.wait_recv()
            _own_copy(ins[i], lands[i], sends[i], mine).wait()

    srcs = [s[2] for s in started]
    lands = [s[3] for s in started]
    res = pl.pallas_call(
        body, name=name,
        out_shape=[pltpu.HBM(s.shape, s.dtype) for s in srcs] + [pltpu.HBM(z.shape, z.dtype) for z in lands],
        in_specs=[HBM_SPEC] * (2 * n) + [SEM_SPEC] * (2 * n) + [ANY_SPEC] * len(after),
        out_specs=[HBM_SPEC] * (2 * n),
        input_output_aliases={i: i for i in range(2 * n)},
        compiler_params=pltpu.CompilerParams(has_side_effects=EFFECT),
    )(*srcs, *lands, *[s[0] for s in started], *[s[1] for s in started], *after)
    return list(res[n:])


def _hop(src, land, send_sems, recv_sems, k, block, to):
    dst = land.at[_index(*block)]
    return pltpu.make_async_remote_copy(src_ref=dst if src is None else src, dst_ref=dst, send_sem=send_sems.at[k], recv_sem=recv_sems.at[k],
                                        device_id=to, device_id_type=MESH)


def _own_block(src, land, send_sems, mine):
    return pltpu.make_async_copy(src, land.at[mine], send_sems.at[4])


def _other_chips(x, y):
    return [(1 - x, y), (x, 1 - y), (1 - x, 1 - y)]


def gather_start(name, shards, through):
    n, m = len(shards), len(through)

    def body(*refs):
        ins, lands = refs[:n], refs[n:2 * n]
        sends, recvs = refs[2 * n + m:3 * n + m], refs[3 * n + m:4 * n + m]
        x, y, c = _place()
        for i in range(n):
            for j, chip in enumerate(_other_chips(x, y)):
                _hop(ins[i], lands[i], sends[i], recvs[i], 1 + j, (x, y, c), (*chip, c)).start()
            _hop(ins[i], lands[i], sends[i], recvs[i], 0, (x, y, c), (x, y, 1 - c)).start()
            _own_block(ins[i], lands[i], sends[i], _index(x, y, c)).start()

    own, passing = pltpu.SemaphoreType.DMA((5,)), pltpu.SemaphoreType.DMA((3,))
    zones = [jax.ShapeDtypeStruct((8,) + s.shape, s.dtype) for s in shards]
    res = pl.pallas_call(
        body, name=name,
        out_shape=([own] * (2 * n) + [passing] * (2 * n) + [pltpu.HBM(s.shape, s.dtype) for s in shards]
                   + [pltpu.HBM(z.shape, z.dtype) for z in zones] + [pltpu.HBM(t.shape, t.dtype) for t in through]),
        in_specs=[HBM_SPEC] * (2 * n + m),
        out_specs=[SEM_SPEC] * (4 * n) + [HBM_SPEC] * (2 * n + m),
        input_output_aliases={i: 4 * n + i for i in range(2 * n + m)},
        compiler_params=pltpu.CompilerParams(has_side_effects=EFFECT),
    )(*[pltpu.with_memory_space_constraint(s, pltpu.HBM) for s in shards],
      *[pltpu.with_memory_space_constraint(lax.empty(z.shape, z.dtype), pltpu.HBM) for z in zones],
      *[pltpu.with_memory_space_constraint(t, pltpu.HBM) for t in through])
    return [[res[4 * n + i], res[5 * n + i], res[i], res[n + i], res[2 * n + i], res[3 * n + i]] for i in range(n)], list(res[6 * n:])


def gather_stage(name, pass_on, finish, after):
    arrays = pass_on + finish
    n = len(arrays)

    def body(*refs):
        ins, lands = refs[:n], refs[n:2 * n]
        sems = [refs[(2 + q) * n:(3 + q) * n] for q in range(4)]
        x, y, c = _place()
        me, sibling = (x, y, c), (x, y, 1 - c)
        for i in range(len(pass_on)):
            send, recv, send_on, recv_on = (q[i] for q in sems)
            for j, chip in enumerate(_other_chips(x, y)):
                _hop(None, lands[i], send, recv, 1 + j, (*chip, c), me).wait_recv()
                _hop(None, lands[i], send_on, recv_on, j, (*chip, c), sibling).start()
        for i in range(len(pass_on), n):
            send, recv, send_on, recv_on = (q[i] for q in sems)
            _hop(ins[i], lands[i], send, recv, 0, sibling, me).wait_recv()
            for j, chip in enumerate(_other_chips(x, y)):
                _hop(None, lands[i], send_on, recv_on, j, (*chip, 1 - c), me).wait_recv()
            _hop(ins[i], lands[i], send, recv, 0, me, sibling).wait_send()
            _own_block(ins[i], lands[i], send, _index(*me)).wait()
            for j, chip in enumerate(_other_chips(x, y)):
                _hop(ins[i], lands[i], send, recv, 1 + j, me, (*chip, c)).wait_send()
                _hop(None, lands[i], send_on, recv_on, j, (*chip, c), sibling).wait_send()
        refs[-1][...] = jnp.zeros_like(refs[-1])

    res = pl.pallas_call(
        body, name=name,
        out_shape=([pltpu.HBM(a[0].shape, a[0].dtype) for a in arrays] + [pltpu.HBM(a[1].shape, a[1].dtype) for a in arrays]
                   + [jax.ShapeDtypeStruct((8, 128), F32)]),
        in_specs=[HBM_SPEC] * (2 * n) + [SEM_SPEC] * (4 * n) + [ANY_SPEC],
        out_specs=[HBM_SPEC] * (2 * n) + [pl.BlockSpec(memory_space=pltpu.VMEM)],
        input_output_aliases={i: i for i in range(2 * n)},
        compiler_params=pltpu.CompilerParams(has_side_effects=EFFECT),
    )(*[a[0] for a in arrays], *[a[1] for a in arrays], *[a[2 + q] for q in range(4) for a in arrays], after)
    for i, a in enumerate(arrays):
        a[0], a[1] = res[i], res[n + i]
    return [a[1] for a in finish], res[-1]


WEIGHTS = ["meta_tokens", "norm1_w", "w_in", "ssd_conv_w", "ssd_conv_b", "ssd_dt_bias", "ssd_a_log", "ssd_d", "ssd_norm_w", "lru_conv_w",
           "lru_conv_b", "lru_wa", "lru_ba", "lru_wx", "lru_bx", "lru_lambda", "lru_norm_w", "w_out", "norm2_w", "w_gate", "w_up", "w_down",
           "final_norm_w"]
BIG = ["w_in", "w_out", "w_gate", "w_up", "w_down"]
COLUMN_SHARDED = ["w_in", "w_gate", "w_up"]


def _pair_blocks(w):
    w = w.reshape(8, 2, 64, 64)
    z = jnp.zeros((8, 64, 64), w.dtype)
    return jnp.concatenate([jnp.concatenate([w[:, 0], z], axis=2), jnp.concatenate([z, w[:, 1]], axis=2)], axis=1)


def _unpair_blocks(w2):
    return jnp.stack([w2[:, :64, :64], w2[:, 64:, 64:]], axis=1).reshape(16, 64, 64)


def _per_group(v):
    return jnp.pad(v.reshape(2, 1, 8), ((0, 0), (0, 0), (0, 120)))


def _pad_cols(v, n):
    return jnp.pad(v, ((0, 0), (0, n - v.shape[1])))


def local_step(x, target, meta, ssd_cw, lru_cw, w_in_shards, fetch, send, p):
    bias2, alog2, d2 = _per_group(p["ssd_dt_bias"]), _per_group(p["ssd_a_log"]), _per_group(p["ssd_d"])
    wa2 = _pair_blocks(p["lru_wa"]).astype(BF)
    wx2 = _pair_blocks(p["lru_wx"]).astype(BF)
    lru = (lru_cw, p["lru_conv_b"], wa2, p["lru_ba"], wx2, p["lru_bx"], p["lru_lambda"])

    proj, dt_raw, u1, h0, w_in, w_dt = in_proj(x, meta, p["norm1_w"], w_in_shards)
    yn_ssd, y_pre, h_prev = ssd_fwd(proj, dt_raw, ssd_cw, p["ssd_conv_b"], bias2, alog2, d2, p["ssd_norm_w"])
    _, moved = fetch([], yn_ssd)
    hseq, a = lru_fwd(proj, *lru, moved)
    (w_out,), _ = fetch(["w_out"], hseq)
    h1, cat = out_proj(yn_ssd, proj, hseq, p["lru_norm_w"], w_out, h0)
    (w_gate, w_up), _ = fetch(["w_gate", "w_up"], h1)
    gt, up, act, u2 = gate_up(h1, p["norm2_w"], w_gate, w_up)
    (w_down,), _ = fetch(["w_down"], act)
    dh2, dh2_b, loss, d_fnw = down_loss(act, w_down, h1, target, p["final_norm_w"])

    dgt, dup, g_down, g_gate, g_up = swiglu_bwd(dh2_b, w_down, gt, up, act, u2)
    dh1, dh1_b, d_n2 = gate_up_bwd(dgt, dup, w_gate, w_up, h1, p["norm2_w"], dh2)
    sent = send({"w_down": g_down, "w_gate": g_gate, "w_up": g_up, "w_out": weight_grad("dw_out", cat, dh1_b)})
    dyn, dh_out, dg_b, d_lnw = out_proj_bwd(dh1_b, w_out, proj, hseq, p["lru_norm_w"], sent)

    dxl_b, d_lcw, d_lcb, dwa2, d_ba, dwx2, d_bx, d_lam = lru_bwd(dh_out, a, hseq, proj, *lru)
    dz_b, dxbc_b, ddt_b, dpar, d_snw, d_scw, d_scb = ssd_bwd(dyn, proj, dt_raw, ssd_cw, p["ssd_conv_b"], y_pre, h_prev, bias2, alog2, d2,
                                                             p["ssd_norm_w"], sent)
    sent = send({"w_in": in_weight_grad([dz_b, dxbc_b, dg_b, dxl_b], [0, SSD_W, 2576, 2576 + LRU_W], ddt_b, u1)})
    grad_x, d_meta, d_n1 = in_proj_bwd(dz_b, dg_b, dxl_b, dxbc_b, ddt_b, w_in, w_dt, h0, p["norm1_w"], dh1, sent)
    small = {"norm1_w": d_n1, "ssd_conv_b": d_scb, "ssd_dt_bias": dpar[:, 0, :8].reshape(1, 16), "ssd_a_log": dpar[:, 1, :8].reshape(1, 16),
             "ssd_d": dpar[:, 2, :8].reshape(1, 16), "ssd_norm_w": d_snw, "lru_conv_b": d_lcb, "lru_ba": d_ba, "lru_bx": d_bx,
             "lru_lambda": d_lam, "lru_norm_w": d_lnw, "norm2_w": d_n2, "final_norm_w": d_fnw,
             "lru_wa": _unpair_blocks(dwa2), "lru_wx": _unpair_blocks(dwx2), "meta_tokens": d_meta,
             "ssd_conv_w": d_scw, "lru_conv_w": d_lcw}
    return loss, grad_x, small


def _pack_small(small, loss):
    rows = [_pad_cols(small[name], -(-n // 1024) * 1024).reshape(-1, 1024) for name, n in SIMPLE]
    rows += [small["lru_wa"].reshape(64, 1024), small["lru_wx"].reshape(64, 1024), small["meta_tokens"],
             _pad_cols(small["ssd_conv_w"], 2048).reshape(8, 1024), small["lru_conv_w"], _pad_cols(loss[:, 0:1], 1024)]
    sm = jnp.concatenate(rows, axis=0)
    return jnp.pad(sm, ((0, SM_ROWS - sm.shape[0]), (0, 0)))


def _slabs(g):
    return g.reshape(8, g.shape[0] // 8, g.shape[1])


def _unslab(g):
    return g.reshape(8 * g.shape[1], g.shape[2])


def kernel(x, meta_tokens, norm1_w, w_in, ssd_conv_w, ssd_conv_b, ssd_dt_bias, ssd_a_log, ssd_d, ssd_norm_w, lru_conv_w, lru_conv_b, lru_wa, lru_ba, lru_wx, lru_bx, lru_lambda, lru_norm_w, w_out, norm2_w, w_gate, w_up, w_down, final_norm_w, loss_target, m_meta_tokens, m_norm1_w, m_w_in, m_ssd_conv_w, m_ssd_conv_b, m_ssd_dt_bias, m_ssd_a_log, m_ssd_d, m_ssd_norm_w, m_lru_conv_w, m_lru_conv_b, m_lru_wa, m_lru_ba, m_lru_wx, m_lru_bx, m_lru_lambda, m_lru_norm_w, m_w_out, m_norm2_w, m_w_gate, m_w_up, m_w_down, m_final_norm_w, v_meta_tokens, v_norm1_w, v_w_in, v_ssd_conv_w, v_ssd_conv_b, v_ssd_dt_bias, v_ssd_a_log, v_ssd_d, v_ssd_norm_w, v_lru_conv_w, v_lru_conv_b, v_lru_wa, v_lru_ba, v_lru_wx, v_lru_bx, v_lru_lambda, v_lru_norm_w, v_w_out, v_norm2_w, v_w_gate, v_w_up, v_w_down, v_final_norm_w):
    w = dict(meta_tokens=meta_tokens, norm1_w=norm1_w, w_in=w_in[0], ssd_conv_w=ssd_conv_w[0], ssd_conv_b=ssd_conv_b, ssd_dt_bias=ssd_dt_bias,
             ssd_a_log=ssd_a_log, ssd_d=ssd_d, ssd_norm_w=ssd_norm_w, lru_conv_w=lru_conv_w[0], lru_conv_b=lru_conv_b, lru_wa=lru_wa[0],
             lru_ba=lru_ba, lru_wx=lru_wx[0], lru_bx=lru_bx, lru_lambda=lru_lambda, lru_norm_w=lru_norm_w, w_out=w_out[0], norm2_w=norm2_w,
             w_gate=w_gate[0], w_up=w_up[0], w_down=w_down[0], final_norm_w=final_norm_w.reshape(1, D))
    m = dict(meta_tokens=m_meta_tokens, norm1_w=m_norm1_w, w_in=m_w_in[0], ssd_conv_w=m_ssd_conv_w[0], ssd_conv_b=m_ssd_conv_b,
             ssd_dt_bias=m_ssd_dt_bias, ssd_a_log=m_ssd_a_log, ssd_d=m_ssd_d, ssd_norm_w=m_ssd_norm_w, lru_conv_w=m_lru_conv_w[0],
             lru_conv_b=m_lru_conv_b, lru_wa=m_lru_wa[0], lru_ba=m_lru_ba, lru_wx=m_lru_wx[0], lru_bx=m_lru_bx, lru_lambda=m_lru_lambda,
             lru_norm_w=m_lru_norm_w, w_out=m_w_out[0], norm2_w=m_norm2_w, w_gate=m_w_gate[0], w_up=m_w_up[0], w_down=m_w_down[0],
             final_norm_w=m_final_norm_w.reshape(1, D))
    v = dict(meta_tokens=v_meta_tokens, norm1_w=v_norm1_w, w_in=v_w_in[0], ssd_conv_w=v_ssd_conv_w[0], ssd_conv_b=v_ssd_conv_b,
             ssd_dt_bias=v_ssd_dt_bias, ssd_a_log=v_ssd_a_log, ssd_d=v_ssd_d, ssd_norm_w=v_ssd_norm_w, lru_conv_w=v_lru_conv_w[0],
             lru_conv_b=v_lru_conv_b, lru_wa=v_lru_wa[0], lru_ba=v_lru_ba, lru_wx=v_lru_wx[0], lru_bx=v_lru_bx, lru_lambda=v_lru_lambda,
             lru_norm_w=v_lru_norm_w, w_out=v_w_out[0], norm2_w=v_norm2_w, w_gate=v_w_gate[0], w_up=v_w_up[0], w_down=v_w_down[0],
             final_norm_w=v_final_norm_w.reshape(1, D))
    shapes = dict(meta_tokens=meta_tokens.shape, norm1_w=norm1_w.shape, w_in=w_in.shape, ssd_conv_w=ssd_conv_w.shape,
                  ssd_conv_b=ssd_conv_b.shape, ssd_dt_bias=ssd_dt_bias.shape, ssd_a_log=ssd_a_log.shape, ssd_d=ssd_d.shape,
                  ssd_norm_w=ssd_norm_w.shape, lru_conv_w=lru_conv_w.shape, lru_conv_b=lru_conv_b.shape, lru_wa=lru_wa.shape,
                  lru_ba=lru_ba.shape, lru_wx=lru_wx.shape, lru_bx=lru_bx.shape, lru_lambda=lru_lambda.shape, lru_norm_w=lru_norm_w.shape,
                  w_out=w_out.shape, norm2_w=norm2_w.shape, w_gate=w_gate.shape, w_up=w_up.shape, w_down=w_down.shape,
                  final_norm_w=final_norm_w.shape)
    me = _index(*_place())
    for n in COLUMN_SHARDED:
        w[n], m[n], v[n] = w[n].T, m[n].T, v[n].T

    small_shard = jnp.concatenate([w["meta_tokens"], _pad_cols(w["ssd_conv_w"], 256).reshape(8, 128), w["lru_conv_w"],
                                   jnp.zeros((4, 128), F32)], axis=0)
    g_in, gs = all_gather("gather_w_in", [w["w_in"].astype(BF), small_shard])
    later = ["w_out", "w_gate", "w_up", "w_down"]
    started, (g_in, gs) = gather_start("gather_rest_start", [w[n].astype(BF) for n in later], [g_in, gs])
    started = dict(zip(later, started))
    ssd_cw = gs[:, 16:24].reshape(8, 4, 256)[:, :, :192].transpose(1, 0, 2).reshape(4, XBC)
    lru_cw = gs[:, 24:28].transpose(1, 0, 2).reshape(4, LRU_W)

    def fetch(names, after):
        pass_on = {"w_out": ["w_down"], "w_gate": [], "w_down": []}[names[0]] if names else ["w_out", "w_gate", "w_up"]
        got, zero = gather_stage("gather_" + (names[0] + "_wait" if names else "pass_on"), [started[n] for n in pass_on],
                                 [started[n] for n in names], after)
        return [_unslab(g) for g in got], zero

    in_flight = {}

    def send(grads):
        names = list(grads)
        st, zero = copies_start("grads_" + names[0] + "_start", [grads[n] if n == "small" else _slabs(grads[n]) for n in names])
        in_flight.update(zip(names, st))
        return zero

    loss, grad_x, small = local_step(x[0], loss_target[0], gs, ssd_cw, lru_cw, g_in, fetch, send, w)
    send({"small": _pack_small(small, loss).reshape(8, SM_ROWS // 8, 1024)})

    out = {}
    early = ["w_down", "w_gate", "w_up", "w_out"]
    recv = dict(zip(early, copies_wait("grads_early_wait", [in_flight[n] for n in early], [in_flight["small"][2]])))
    for pair in (early[:2], early[2:]):
        done = adamw_shards("adamw_" + pair[0], [recv[n] for n in pair], [w[n] for n in pair], [m[n] for n in pair], [v[n] for n in pair])
        out.update(zip(pair, done))
    recv_in, recv_small = copies_wait("grads_late_wait", [in_flight["w_in"], in_flight["small"]], [out[n][0] for n in early])
    def lines(a):
        return jnp.transpose(a.reshape(D // 128, 128, IN_COLS // 8), (2, 0, 1))

    gathering, zero = copies_start("gather_small_start", [sum_slabs(recv_small)])
    updated = adamw_w_in(recv_in, lines(w_in), lines(m_w_in), lines(v_w_in), zero)
    out["w_in"] = [jnp.transpose(o, (1, 2, 0)).reshape(D, IN_COLS // 8) for o in updated]
    for n in ("w_gate", "w_up"):
        out[n] = [o.T for o in out[n]]
    sm = copies_wait("gather_small_wait", gathering, [updated[0]])[0].reshape(SM_ROWS, 1024)
    special_g =[sm[SM_WA:SM_WA + 64].reshape(16, 64, 64), sm[SM_WX:SM_WX + 64].reshape(16, 64, 64),
                 lax.dynamic_slice(sm[SM_META:SM_META + 16], (0, 128 * me), (16, 128)),
                 lax.dynamic_slice(sm[SM_SCW:SM_SCW + 8].reshape(4, 2048), (0, 192 * me), (4, 192)),
                 lax.dynamic_slice(sm[SM_LCW:SM_LCW + 4], (0, 128 * me), (4, 128))]
    names = [n for n, _ in SIMPLE] + SPECIAL
    res = adamw_small(sm, special_g, [w[n] for n in names], [m[n] for n in names], [v[n] for n in names])
    for k, (n, _) in enumerate(SIMPLE):
        out[n] = res[4 * k:4 * k + 4]
    for k, n in enumerate(SPECIAL):
        o = 4 * len(SIMPLE) + 3 * k
        out[n] = [special_g[k]] + list(res[o:o + 3])
    loss_total = sm[SM_LOSS, 0]
    flat = [loss_total, grad_x[None]]
    for k in range(4):
        flat += [out[n][k].reshape(shapes[n]) for n in WEIGHTS]
    return tuple(flat)
```

```python
import math

import jax
import jax.numpy as jnp
from jax import lax
from jax.experimental import pallas as pl
from jax.experimental.pallas import tpu as pltpu

F32 = jnp.float32
BF = jnp.bfloat16

D = 1024
SEQ = 2048
N_META = 16
Q = 128
NPAD = 112
T = NPAD + N_META + SEQ
NCH = T // Q
RC = 544
D_FF = 2816
SSD_W = 1024
LRU_W = 1024
XBC = 1536
IN_COLS = 4624
PZ, PG, PXL, PXBC = 0, 1024, 2048, 3072
NP_IN = 4608
EPS = 1e-6
LRU_C = 8.0
VMEM_LIMIT = 56 * 1024 * 1024

ADAM_LR, ADAM_B1, ADAM_B2, ADAM_EPS, ADAM_WD, ADAM_STEP = 0.001, 0.9, 0.999, 1e-08, 0.01, 10

NT_DIMS = (((1,), (1,)), ((), ()))
TN_DIMS = (((0,), (0,)), ((), ()))
MESH = pl.DeviceIdType.MESH


def _params(n_grid=1, limit=VMEM_LIMIT):
    return pltpu.CompilerParams(dimension_semantics=("arbitrary",) * n_grid, vmem_limit_bytes=limit)


def _spec(shape, imap, single=False):
    if single:
        return pl.BlockSpec(shape, imap, pipeline_mode=pl.Buffered(1))
    return pl.BlockSpec(shape, imap)


def _sigmoid(x):
    return 0.5 * jnp.tanh(0.5 * x) + 0.5


def _sigmoid_gate(x):
    return 1.0 / (1.0 + jnp.exp(-x))


def _softplus(x):
    return jnp.maximum(x, 0.0) + jnp.log(1.0 + jnp.exp(-jnp.abs(x)))


def _rms_stats(h):
    return lax.rsqrt(jnp.mean(h * h, axis=-1, keepdims=True) + EPS)


def _rms(h, w):
    return (h * _rms_stats(h)) * w


def _rms_bwd(du, h, w):
    r = _rms_stats(h)
    n = h * r
    dn = du * w
    dh = r * (dn - n * jnp.mean(dn * n, axis=-1, keepdims=True))
    return dh, du * n


_G0 = math.sqrt(2.0 / math.pi)


def _gelu(x):
    return 0.5 * x * (1.0 + jnp.tanh(_G0 * (x + 0.044715 * (x * x * x))))


def _gelu_grad(x):
    t = jnp.tanh(_G0 * (x + 0.044715 * (x * x * x)))
    return 0.5 * (1.0 + t) + 0.5 * x * (1.0 - t * t) * (_G0 * (1.0 + 3.0 * 0.044715 * (x * x)))


def _rows(shape, r0=0):
    return lax.broadcasted_iota(jnp.int32, shape, 0) + r0


def _lanes(shape):
    return lax.broadcasted_iota(jnp.int32, shape, 1)


HALO = 8


def _fill_padded(pad_ref, x_ref):
    pad_ref[0:HALO, :] = jnp.zeros((HALO, pad_ref.shape[1]), F32)
    pad_ref[T + HALO:T + 2 * HALO, :] = jnp.zeros((HALO, pad_ref.shape[1]), F32)

    def step(c, carry):
        r0 = pl.multiple_of(c * Q, Q)
        pad_ref[pl.ds(r0 + HALO, Q), :] = x_ref[pl.ds(r0, Q), :].astype(F32)
        return carry

    lax.fori_loop(0, NCH, step, 0)


def _back(pad_ref, r0):
    win = pad_ref[pl.ds(r0, Q + HALO), :]
    return lambda s: win[HALO:, :] if s == 0 else pltpu.roll(win, s, axis=0)[HALO:, :]


def _ahead(pad_ref, r0):
    win = pad_ref[pl.ds(r0 + HALO, Q + HALO), :]
    return lambda s: win[:Q, :] if s == 0 else pltpu.roll(win, Q + HALO - s, axis=0)[:Q, :]


def _conv(back, w, b):
    y = b + w[3:4, :] * back(0)
    for k in range(3):
        y = y + w[k:k + 1, :] * back(3 - k)
    return y


def _conv_bwd_x(ahead, w):
    dx = w[3:4, :] * ahead(0)
    for k in range(3):
        dx = dx + w[k:k + 1, :] * ahead(3 - k)
    return dx


def _conv_bwd_w(dy, back):
    dws = [jnp.sum(dy * back(3 - k), axis=0, keepdims=True) for k in range(4)]
    return jnp.concatenate(dws, axis=0), jnp.sum(dy, axis=0, keepdims=True)


def _chunks(fn, unrolled=False):
    if unrolled:
        for c in range(NCH):
            fn(c * Q)
        return

    def step(c, carry):
        fn(pl.multiple_of(c * Q, Q))
        return carry

    lax.fori_loop(0, NCH, step, 0)


HALF = RC // 2


def _col_tiles(n, tn, fn):
    def step(j, carry):
        fn(pl.multiple_of(j * tn, tn))
        return carry

    lax.fori_loop(0, n // tn, step, 0)


def _rows_spec(cols, block_col=0):
    return _spec((RC, cols), lambda i: (i, block_col))


def _whole(shape):
    return _spec(shape, lambda i: tuple(0 for _ in shape), single=True)


def _vec(cols):
    return _spec((1, cols), lambda i: (0, 0))


def _zero_at_first(*refs):
    @pl.when(pl.program_id(0) == 0)
    def _():
        for r in refs:
            r[...] = jnp.zeros_like(r)


ANY_SPEC = pl.BlockSpec(memory_space=pl.ANY)


def _arriving(src, dst, sems, starts, rows):
    n, ahead = len(starts), 2
    first = pl.program_id(0) == 0

    def piece(k):
        r0 = starts[0]
        for j in range(1, n):
            r0 = jnp.where(k == j, starts[j], r0)
        at = pl.ds(pl.multiple_of(r0, 16), rows)
        return pltpu.make_async_copy(src.at[at], dst.at[at], sems.at[k])

    @pl.when(first)
    def _():
        for k in range(min(ahead, n)):
            piece(k).start()

    def ready(k):
        k = jnp.asarray(k, jnp.int32)

        @pl.when(first)
        def _():
            piece(k).wait()

            @pl.when(k + ahead < n)
            def _():
                piece(k + ahead).start()

    return ready


IN_RUNS = ((PZ, 0, 1024), (PXBC, 1024, XBC), (PG, 2576, 2048))
IN_TILE = 512
IN_TILE_ROWS = [wrow + IN_TILE * j for _, wrow, width in IN_RUNS for j in range(width // IN_TILE)]


def _in_tiles(fn, before_run=lambda run: None):
    done = 0
    for run, (pcol, wrow, width) in enumerate(IN_RUNS):
        before_run(run)
        def step(j, carry, pcol=pcol, wrow=wrow, done=done):
            fn(pl.multiple_of(pcol + j * IN_TILE, IN_TILE), pl.multiple_of(wrow + j * IN_TILE, 16), done + j)
            return carry

        lax.fori_loop(0, width // IN_TILE, step, 0)
        done += width // IN_TILE


def in_proj(x, meta, wn, w_shards):
    first = NPAD + N_META
    steps = T // RC
    shard = IN_COLS // 8

    def body(x_hbm, meta_ref, wn_ref, g_hbm, o_ref, dt_ref, u_ref, h_ref, wt_hbm, wdt_ref, raw, w_ref, h_scr, g_sems, h_sems, out_sem):
        i = pl.program_id(0)
        slot = i % 2
        shards = [pltpu.make_async_copy(g_hbm.at[j], raw.at[j], g_sems.at[j]) for j in range(8)]
        head = pltpu.make_async_copy(x_hbm.at[pl.ds(0, RC - first)], h_scr.at[0, pl.ds(first, RC - first)], h_sems.at[0])
        put_back = pltpu.make_async_copy(w_ref, wt_hbm, out_sem)

        def rows_of(step):
            return pltpu.make_async_copy(x_hbm.at[pl.ds(pl.multiple_of(step * RC - first, 32), RC)], h_scr.at[step % 2], h_sems.at[step % 2])

        @pl.when(i == 0)
        def _():
            for cp in shards:
                cp.start()
            head.start()
            h_scr[0, 0:NPAD, :] = jnp.zeros((NPAD, D), F32)
            for j in range(8):
                h_scr[0, NPAD:first, 128 * j:128 * j + 128] = meta_ref[j, 0:N_META, :]

        @pl.when(i + 1 < steps)
        def _():
            rows_of(i + 1).start()

        @pl.when(i == 0)
        def _():
            head.wait()

        @pl.when(i > 0)
        def _():
            rows_of(i).wait()

        h_ref[...] = h_scr[slot]
        for r in (0, HALF):
            u_ref[r:r + HALF, :] = _rms(h_scr[slot, r:r + HALF, :], wn_ref[...]).astype(BF)

        def place_shards(run):
            @pl.when(i == 0)
            def _():
                for j in ((0, 1), (2, 3, 4), (5, 6, 7))[run]:
                    shards[j].wait()
                    w_ref[shard * j:shard * (j + 1), :] = raw[j]
                if run == 1:
                    wdt_ref[...] = jnp.zeros_like(wdt_ref)
                    for g in range(2):
                        wdt_ref[128 * g:128 * g + 8, :] = w_ref[2560 + 8 * g:2568 + 8 * g, :]
                if run == 2:
                    put_back.start()

        def tile(pcol, wrow, k):
            o_ref[:, pl.ds(pcol, IN_TILE)] = lax.dot_general(u_ref[...], w_ref[pl.ds(wrow, IN_TILE), :], NT_DIMS,
                                                             preferred_element_type=F32).astype(BF)

        _in_tiles(tile, place_shards)
        dt_ref[...] = lax.dot_general(u_ref[...], wdt_ref[...], NT_DIMS, preferred_element_type=F32)

        @pl.when(i == steps - 1)
        def _():
            put_back.wait()

    return pl.pallas_call(
        body, grid=(steps,), in_specs=[ANY_SPEC, _spec(meta.shape, lambda i: (0, 0, 0)), _vec(D), ANY_SPEC],
        out_specs=[_rows_spec(NP_IN), _rows_spec(256), _rows_spec(D), _rows_spec(D), ANY_SPEC, _spec((256, D), lambda i: (0, 0))],
        out_shape=[jax.ShapeDtypeStruct((T, NP_IN), BF), jax.ShapeDtypeStruct((T, 256), F32), jax.ShapeDtypeStruct((T, D), BF),
                   jax.ShapeDtypeStruct((T, D), F32), jax.ShapeDtypeStruct((IN_COLS, D), BF), jax.ShapeDtypeStruct((256, D), BF)],
        scratch_shapes=[pltpu.VMEM((8, shard, D), BF), pltpu.VMEM((IN_COLS, D), BF), pltpu.VMEM((2, RC, D), F32),
                        pltpu.SemaphoreType.DMA((8,)), pltpu.SemaphoreType.DMA((2,)), pltpu.SemaphoreType.DMA],
        compiler_params=_params(), name="in_proj")(x, meta, wn, w_shards)


def out_proj(yn_ssd, proj, hseq, lru_nw, w_out, h0):
    def body(y_ref, g_ref, h_ref, wn_ref, w_ref, r_ref, o_ref, cat_ref):
        cat_ref[:, 0:SSD_W] = y_ref[...]
        for r in (0, HALF):
            y = _gelu(g_ref[r:r + HALF, :].astype(F32)) * h_ref[r:r + HALF, :]
            cat_ref[r:r + HALF, SSD_W:] = _rms(y, wn_ref[...]).astype(BF)

        def tile(c0):
            o_ref[:, pl.ds(c0, 512)] = r_ref[:, pl.ds(c0, 512)] + jnp.dot(cat_ref[...], w_ref[:, pl.ds(c0, 512)], preferred_element_type=F32)

        _col_tiles(D, 512, tile)

    return pl.pallas_call(
        body, grid=(T // RC,),
        in_specs=[_rows_spec(SSD_W), _rows_spec(LRU_W, PG // LRU_W), _rows_spec(LRU_W), _vec(LRU_W), _whole((SSD_W + LRU_W, D)), _rows_spec(D)],
        out_specs=[_rows_spec(D), _rows_spec(SSD_W + LRU_W)],
        out_shape=[jax.ShapeDtypeStruct((T, D), F32), jax.ShapeDtypeStruct((T, SSD_W + LRU_W), BF)],
        compiler_params=_params(), name="out_proj")(yn_ssd, proj, hseq, lru_nw, w_out, h0)


def out_proj_bwd(dh1_b, w_out, proj, hseq, lru_nw, after):
    def body(d_ref, w_ref, g_ref, h_ref, wn_ref, _after, dy_ref, dh_ref, dg_ref, dw_ref, dl_scr):
        _zero_at_first(dw_ref)

        def tile(c0):
            dy_ref[:, pl.ds(c0, 512)] = lax.dot_general(d_ref[...], w_ref[pl.ds(c0, 512), :], NT_DIMS, preferred_element_type=F32)
            dl_scr[:, pl.ds(c0, 512)] = lax.dot_general(d_ref[...], w_ref[pl.ds(SSD_W + c0, 512), :], NT_DIMS, preferred_element_type=F32)

        _col_tiles(SSD_W, 512, tile)

        for r in (0, HALF):
            g = g_ref[r:r + HALF, :].astype(F32)
            h = h_ref[r:r + HALF, :]
            ge = _gelu(g)
            dy, dw = _rms_bwd(dl_scr[r:r + HALF, :], ge * h, wn_ref[...])
            dw_ref[...] += jnp.sum(dw, axis=0, keepdims=True)
            dh_ref[r:r + HALF, :] = dy * ge
            dg_ref[r:r + HALF, :] = (dy * h * _gelu_grad(g)).astype(BF)

    return pl.pallas_call(
        body, grid=(T // RC,),
        in_specs=[_rows_spec(D), _whole((SSD_W + LRU_W, D)), _rows_spec(LRU_W, PG // LRU_W), _rows_spec(LRU_W), _vec(LRU_W), ANY_SPEC],
        out_specs=[_rows_spec(SSD_W), _rows_spec(LRU_W), _rows_spec(LRU_W), _vec(LRU_W)],
        out_shape=[jax.ShapeDtypeStruct((T, SSD_W), F32), jax.ShapeDtypeStruct((T, LRU_W), F32), jax.ShapeDtypeStruct((T, LRU_W), BF),
                   jax.ShapeDtypeStruct((1, LRU_W), F32)],
        scratch_shapes=[pltpu.VMEM((RC, LRU_W), F32)],
        compiler_params=_params(), name="out_proj_bwd")(dh1_b, w_out, proj, hseq, lru_nw, after)


def in_proj_bwd(dz, dg, dxl, dxbc, ddt, w_t, w_dt, h0, wn, dh1, after):
    first = NPAD + N_META

    def body(dz_ref, dg_ref, dxl_ref, dxbc_ref, ddt_ref, w_hbm, wdt_ref, h_ref, wn_ref, r_ref, _after, gx_hbm, meta_ref, dw_ref, du_scr, o_ref, sem,
             w_ref, w_sems):
        i = pl.program_id(0)
        ready = _arriving(w_hbm, w_ref, w_sems, IN_TILE_ROWS, IN_TILE)
        _zero_at_first(dw_ref)
        du_scr[...] = jnp.dot(ddt_ref[...], wdt_ref[...], preferred_element_type=F32)
        done = 0
        for d_ref, wrow, width in ((dz_ref, 0, 1024), (dxbc_ref, 1024, XBC), (dg_ref, 2576, 1024), (dxl_ref, 3600, 1024)):
            def step(j, carry, d_ref=d_ref, wrow=wrow, done=done):
                c0 = pl.multiple_of(j * IN_TILE, IN_TILE)
                ready(done + j)
                du_scr[...] += jnp.dot(d_ref[:, pl.ds(c0, IN_TILE)], w_ref[pl.ds(pl.multiple_of(wrow + c0, 16), IN_TILE), :],
                                       preferred_element_type=F32)
                return carry

            lax.fori_loop(0, width // IN_TILE, step, 0)
            done += width // IN_TILE
        for r in (0, HALF):
            dh, dw = _rms_bwd(du_scr[r:r + HALF, :], h_ref[r:r + HALF, :], wn_ref[...])
            dw_ref[...] += jnp.sum(dw, axis=0, keepdims=True)
            o_ref[r:r + HALF, :] = dh + r_ref[r:r + HALF, :]

        @pl.when(i == 0)
        def _():
            meta_ref[...] = o_ref[NPAD:first, :]
            head = pltpu.make_async_copy(o_ref.at[pl.ds(first, RC - first)], gx_hbm.at[pl.ds(0, RC - first)], sem)
            head.start()
            head.wait()

        @pl.when(i > 0)
        def _():
            rest = pltpu.make_async_copy(o_ref, gx_hbm.at[pl.ds(pl.multiple_of(i * RC - first, 32), RC)], sem)
            rest.start()
            rest.wait()

    return pl.pallas_call(
        body, grid=(T // RC,),
        in_specs=[_rows_spec(SSD_W), _rows_spec(LRU_W), _rows_spec(LRU_W), _rows_spec(XBC), _rows_spec(256), ANY_SPEC,
                  _whole((256, D)), _rows_spec(D), _vec(D), _rows_spec(D), ANY_SPEC],
        out_specs=[ANY_SPEC, _spec((N_META, D), lambda i: (0, 0)), _vec(D)],
        out_shape=[jax.ShapeDtypeStruct((SEQ, D), F32), jax.ShapeDtypeStruct((N_META, D), F32), jax.ShapeDtypeStruct((1, D), F32)],
        scratch_shapes=[pltpu.VMEM((RC, D), F32), pltpu.VMEM((RC, D), F32), pltpu.SemaphoreType.DMA,
                        pltpu.VMEM((IN_COLS, D), BF), pltpu.SemaphoreType.DMA((len(IN_TILE_ROWS),))],
        compiler_params=_params(), name="in_proj_bwd")(dz, dg, dxl, dxbc, ddt, w_t, w_dt, h0, wn, dh1, after)


GRAD_TILE = 256


def weight_grad(name, a, u1):
    tm = GRAD_TILE

    def body(a_ref, u_ref, o_ref):
        o_ref[...] = lax.dot_general(a_ref[...], u_ref[...], TN_DIMS, preferred_element_type=F32).astype(BF)

    return pl.pallas_call(
        body, grid=(a.shape[1] // tm,),
        in_specs=[_spec((T, tm), lambda j: (0, j)), _spec((T, D), lambda j: (0, 0), single=True)],
        out_specs=_spec((tm, D), lambda j: (j, 0)),
        out_shape=jax.ShapeDtypeStruct((a.shape[1], D), BF),
        compiler_params=_params(), name=name)(a, u1)


def in_weight_grad(parts, first_rows, ddt, u1):
    tm = GRAD_TILE
    per_row = D // 128
    dt_row, dt_lines = 2560, 8 * per_row
    parts = list(parts) + [ddt]
    first_rows = list(first_rows) + [dt_row]
    tiles = [p.shape[1] // tm for p in parts]
    starts = [sum(tiles[:k]) for k in range(len(parts))]
    last = sum(tiles) - 1

    def body(*refs):
        a_refs, u_ref = refs[:len(parts)], refs[len(parts)]
        o_hbm, mix_scr, stage, sems = refs[len(parts) + 1:]
        step = pl.program_id(0)
        slot = step % 2
        line0 = 0
        for a_ref, start, n, first in zip(a_refs, starts, tiles, first_rows):
            here = (step >= start) & (step < start + n)
            line0 = jnp.where(here, per_row * (first + tm * (step - start)), line0)

            @pl.when(here)
            def _(a_ref=a_ref):
                res = lax.dot_general(a_ref[...], u_ref[...], TN_DIMS, preferred_element_type=F32)
                for q in range(per_row):
                    mix_scr[pl.ds(q, tm, stride=per_row), :] = res[:, 128 * q:128 * q + 128]

        def tile_copy(of_slot, to):
            return pltpu.make_async_copy(stage.at[of_slot], o_hbm.at[pl.ds(to, per_row * tm)], sems.at[of_slot])

        @pl.when(step >= 2)
        def _():
            tile_copy(slot, 0).wait()

        stage[slot] = mix_scr[...].astype(BF)

        @pl.when(step < last)
        def _():
            tile_copy(slot, pl.multiple_of(line0, 128)).start()

        @pl.when(step == last)
        def _():
            halves = [pltpu.make_async_copy(stage.at[slot, pl.ds(128 * per_row * k, dt_lines)],
                                            o_hbm.at[pl.ds(per_row * (dt_row + 8 * k), dt_lines)], sems.at[2 + k]) for k in range(2)]
            for cp in halves:
                cp.start()
            tile_copy(1 - slot, 0).wait()
            for cp in halves:
                cp.wait()

    def tile_of(start, n):
        return lambda j: (0, jnp.clip(j - start, 0, n - 1))

    return pl.pallas_call(
        body, grid=(last + 1,),
        in_specs=[_spec((T, tm), tile_of(s, n)) for s, n in zip(starts, tiles)] + [_spec((T, D), lambda j: (0, 0), single=True)],
        out_specs=ANY_SPEC,
        out_shape=jax.ShapeDtypeStruct((per_row * IN_COLS, 128), BF),
        scratch_shapes=[pltpu.VMEM((per_row * tm, 128), F32), pltpu.VMEM((2, per_row * tm, 128), BF), pltpu.SemaphoreType.DMA((4,))],
        compiler_params=_params(), name="dw_in")(*parts, u1)


def _ssd_chunk_common(row0, dt_ref, b_ref, c_ref, bias, a_neg):
    shape = (Q, Q)
    lane = _lanes(shape)
    sub = _rows(shape)
    live = (_rows(shape, row0) >= NPAD) & (lane < 8)
    dtr = dt_ref[:, :]
    dt = jnp.where(live, _softplus(dtr + bias), 0.0)
    d_a = dt * a_neg
    tri = (sub >= lane).astype(F32)
    cs = jnp.dot(tri, d_a, precision=lax.Precision.HIGHEST, preferred_element_type=F32)
    cs_t = cs.T
    b_f = b_ref[:, :]
    bc = b_f.astype(BF)
    cc = c_ref[:, :].astype(BF)
    cb = lax.dot_general(cc, bc, NT_DIMS, preferred_element_type=F32)
    cs_last = cs[Q - 1:Q, :]
    return dict(lane=lane, sub=sub, live=live, dtr=dtr, dt=dt, cs=cs, cs_t=cs_t, bc=bc, cc=cc, cb=cb, bc_t=b_f.T.astype(BF),
                ecs=jnp.exp(cs), dsm=jnp.exp(cs_last - cs), gam=jnp.exp(cs_last))


def _pair(lane_even, mat, j):
    return jnp.where(lane_even, mat[:, j:j + 1], mat[:, j + 1:j + 2])


def _pair_row(lane_even, mat, j):
    return jnp.where(lane_even[0:1, :], mat[:, j:j + 1], mat[:, j + 1:j + 2])


def _head_decay(cm, j):
    seg = cm["cs"][:, j:j + 1] - cm["cs_t"][j:j + 1, :]
    return jnp.exp(jnp.where(cm["sub"] >= cm["lane"], seg, -jnp.inf))


def _head_decay_t(cm, j):
    seg = cm["cs_t"][j:j + 1, :] - cm["cs"][:, j:j + 1]
    return jnp.exp(jnp.where(cm["lane"] >= cm["sub"], seg, -jnp.inf))


def _conv_window(raw_ref, halo_ref, pad_scr):
    pad_scr[0:HALO, :] = halo_ref[...].astype(F32)[halo_ref.shape[0] - HALO:, :]
    pad_scr[HALO:HALO + Q, :] = raw_ref[...].astype(F32)
    win = pad_scr[...]
    return lambda s: win[HALO:, :] if s == 0 else pltpu.roll(win, s, axis=0)[HALO:, :]


def _xbc_cols(g):
    return slice(512 * g, 512 * g + 512), slice(SSD_W + 128 * g, SSD_W + 128 * g + 128), slice(SSD_W + 256 + 128 * g, SSD_W + 384 + 128 * g)


def ssd_fwd(proj, dt_raw, conv_w, conv_b, dt_bias2, a_log2, d2, norm_w):
    def body(raw_ref, halo_ref, dt_all, z_all, cw_ref, cb_ref, bias_all, alog_all, d_all, nw_all, yn_all, y_all, hp_all,
             h_all, pad_scr, act_scr):
        @pl.when(pl.program_id(0) == 0)
        def _():
            h_all[...] = jnp.zeros_like(h_all)

        pre = _conv(_conv_window(raw_ref, halo_ref, pad_scr), cw_ref[...], cb_ref[...])
        act_scr[...] = pre * _sigmoid(pre)
        for g in range(2):
            wide, thin = slice(512 * g, 512 * g + 512), slice(128 * g, 128 * g + 128)
            xs, bs, cs = _xbc_cols(g)
            group(act_scr.at[:, xs], act_scr.at[:, bs], act_scr.at[:, cs], dt_all.at[:, thin], z_all.at[:, wide], bias_all.at[g],
                  alog_all.at[g], d_all.at[g], nw_all.at[:, wide], yn_all.at[:, wide], y_all.at[:, wide], hp_all.at[g, 0], h_all.at[g])

    def group(x_ref, b_ref, c_ref, dt_ref, z_ref, bias_ref, alog_ref, d_ref, nw_ref, yn_ref, y_ref, hp_ref, h_scr):
        bias = bias_ref[...]
        a_neg = -jnp.exp(alog_ref[...])
        dsk = d_ref[...]
        cm = _ssd_chunk_common(pl.program_id(0) * Q, dt_ref, b_ref, c_ref, bias, a_neg)
        lane_even = cm["lane"] < 64
        for p in range(4):
            je, jo = 2 * p, 2 * p + 1
            xp = x_ref[:, 128 * p:128 * p + 128]
            xdt = xp * _pair(lane_even, cm["dt"], je)
            xdt_b = xdt.astype(BF)
            m_e = (cm["cb"] * _head_decay(cm, je)).astype(BF)
            m_o = (cm["cb"] * _head_decay(cm, jo)).astype(BF)
            zero = jnp.zeros_like(xdt_b)
            yd = (jnp.dot(m_e, jnp.where(lane_even, xdt_b, zero), preferred_element_type=F32)
                  + jnp.dot(m_o, jnp.where(lane_even, zero, xdt_b), preferred_element_type=F32))
            hp = h_scr[p]
            hp_ref[p] = hp
            yo = jnp.dot(cm["cc"], hp.astype(BF), preferred_element_type=F32) * _pair(lane_even, cm["ecs"], je)
            y_ref[:, 128 * p:128 * p + 128] = yd + yo + xp * _pair_row(lane_even, dsk, je)
            st = jnp.dot(cm["bc_t"], (xdt * _pair(lane_even, cm["dsm"], je)).astype(BF), preferred_element_type=F32)
            h_scr[p] = hp * _pair_row(lane_even, cm["gam"], je) + st
        zc = z_ref[:, :].astype(F32)
        gated = y_ref[:, :] * (zc * _sigmoid(zc))
        yn_ref[:, :] = _rms(gated, nw_ref[...]).astype(BF)

    par = _spec((2, 1, 128), lambda c: (0, 0, 0))
    wide = _spec((Q, SSD_W), lambda c: (c, 0))
    xbc = PXBC // XBC
    halo = 2 * HALO
    return pl.pallas_call(
        body, grid=(NCH,),
        in_specs=[_spec((Q, XBC), lambda c: (c, xbc)), _spec((halo, XBC), lambda c: (jnp.maximum(c * (Q // halo) - 1, 0), xbc)),
                  _spec((Q, 256), lambda c: (c, 0)), wide, _spec((4, XBC), lambda c: (0, 0)), _spec((1, XBC), lambda c: (0, 0)),
                  par, par, par, _spec((1, SSD_W), lambda c: (0, 0))],
        out_specs=[wide, wide, _spec((2, 1, 4, 128, 128), lambda c: (0, c, 0, 0, 0))],
        out_shape=[jax.ShapeDtypeStruct((T, SSD_W), BF), jax.ShapeDtypeStruct((T, SSD_W), F32),
                   jax.ShapeDtypeStruct((2, NCH, 4, 128, 128), F32)],
        scratch_shapes=[pltpu.VMEM((2, 4, 128, 128), F32), pltpu.VMEM((Q + HALO, XBC), F32), pltpu.VMEM((Q, XBC), F32)],
        compiler_params=_params(), name="ssd_fwd")(proj, proj, dt_raw, proj, conv_w, conv_b, dt_bias2, a_log2, d2, norm_w)


def ssd_bwd(dyn, proj, dt_raw, conv_w, conv_b, y_pre, h_prev, dt_bias2, a_log2, d2, norm_w, after):
    def body(dyn_all, raw_ref, halo_ref, dt_all, z_all, y_all, hp_all, cw_ref, cb_ref, bias_all, alog_all, d_all, nw_all, _after,
             dz_all, dxbc_ref, ddt_all, dpar_all, dnw_all, dcw_ref, dcb_ref, dh_all, acc_all, pad_scr, act_scr, dsilu_scr, dact_scr, dpad_scr):
        @pl.when(pl.program_id(0) == 0)
        def _():
            dh_all[...] = jnp.zeros_like(dh_all)
            acc_all[...] = jnp.zeros_like(acc_all)
            dnw_all[...] = jnp.zeros_like(dnw_all)
            dcw_ref[...] = jnp.zeros_like(dcw_ref)
            dcb_ref[...] = jnp.zeros_like(dcb_ref)
            dpad_scr[Q:Q + HALO, :] = jnp.zeros((HALO, XBC), F32)

        back = _conv_window(raw_ref, halo_ref, pad_scr)
        pre = _conv(back, cw_ref[...], cb_ref[...])
        sg = _sigmoid(pre)
        act_scr[...] = pre * sg
        dsilu_scr[...] = sg * (1.0 + pre * (1.0 - sg))
        for g in range(2):
            wide, thin = slice(512 * g, 512 * g + 512), slice(128 * g, 128 * g + 128)
            xs, bs, cs = _xbc_cols(g)
            group(dyn_all.at[:, wide], act_scr.at[:, xs], act_scr.at[:, bs], act_scr.at[:, cs], dt_all.at[:, thin], z_all.at[:, wide],
                  y_all.at[:, wide], hp_all.at[g, 0], bias_all.at[g], alog_all.at[g], d_all.at[g], nw_all.at[:, wide],
                  dz_all.at[:, wide], dact_scr.at[:, xs], dact_scr.at[:, bs], dact_scr.at[:, cs], ddt_all.at[:, thin], dpar_all.at[g],
                  dnw_all.at[:, wide], dh_all.at[g], acc_all.at[g])
        dpre = dact_scr[...] * dsilu_scr[...]
        dcw, dcb = _conv_bwd_w(dpre, back)
        dcw_ref[...] += dcw
        dcb_ref[...] += dcb
        dpad_scr[0:Q, :] = dpre
        win = dpad_scr[...]
        dxbc_ref[...] = _conv_bwd_x(lambda s: win[:Q, :] if s == 0 else pltpu.roll(win, Q + HALO - s, axis=0)[:Q, :], cw_ref[...]).astype(BF)
        dpad_scr[Q:Q + HALO, :] = dpre[0:HALO, :]

    def group(dyn_ref, x_ref, b_ref, c_ref, dt_ref, z_ref, y_ref, hp_ref, bias_ref, alog_ref, d_ref, nw_ref,
              dz_ref, dx_ref, db_ref, dc_ref, ddt_ref, dpar_ref, dnw_ref, dh_scr, acc_scr):
        ci = pl.program_id(0)
        bias = bias_ref[...]
        a_neg = -jnp.exp(alog_ref[...])
        dsk = d_ref[...]
        cm = _ssd_chunk_common((NCH - 1 - ci) * Q, dt_ref, b_ref, c_ref, bias, a_neg)
        lane, sub = cm["lane"], cm["sub"]
        lane_even = lane < 64
        cc_t = c_ref[:, :].T.astype(BF)
        cb_t = lax.dot_general(cm["bc"], cm["cc"], NT_DIMS, preferred_element_type=F32)
        zc = z_ref[:, :].astype(F32)
        yc = y_ref[:, :]
        sg = _sigmoid(zc)
        sz = zc * sg
        dgated, dnw = _rms_bwd(dyn_ref[:, :], yc * sz, nw_ref[...])
        dnw_ref[...] += jnp.sum(dnw, axis=0, keepdims=True)
        dz_ref[:, :] = (dgated * yc * (sg * (1.0 + zc * (1.0 - sg)))).astype(BF)
        dy_all = dgated * sz
        dcb = jnp.zeros((Q, Q), F32)
        dcb_t = jnp.zeros((Q, Q), F32)
        db_acc = jnp.zeros((Q, Q), F32)
        dc_acc = jnp.zeros((Q, Q), F32)
        dcs = jnp.zeros((Q, Q), F32)
        ddt = jnp.zeros((Q, Q), F32)
        for p in range(4):
            je, jo = 2 * p, 2 * p + 1
            xp = x_ref[:, 128 * p:128 * p + 128]
            dy = dy_all[:, 128 * p:128 * p + 128]
            dt_p = _pair(lane_even, cm["dt"], je)
            xdt = xp * dt_p
            xdt_b = xdt.astype(BF)
            dy_b = dy.astype(BF)
            zero = jnp.zeros_like(dy_b)
            hp = hp_ref[p]
            hp_b = hp.astype(BF)
            dh = dh_scr[p]
            dh_b = dh.astype(BF)
            acc_scr[p:p + 1, :] += jnp.sum(dy * xp, axis=0, keepdims=True)
            dxp = dy * _pair_row(lane_even, dsk, je)
            e_p = _pair(lane_even, cm["ecs"], je)
            g_p = jnp.dot(cm["cc"], hp_b, preferred_element_type=F32)
            dg_b = (dy * e_p).astype(BF)
            de = dy * g_p * e_p
            dc_acc = dc_acc + lax.dot_general(dg_b, hp_b, NT_DIMS, preferred_element_type=F32)
            dh_in = jnp.dot(cc_t, dg_b, preferred_element_type=F32)
            ds_p = _pair(lane_even, cm["dsm"], je)
            r_p = jnp.dot(cm["bc"], dh_b, preferred_element_type=F32)
            dxdt = r_p * ds_p
            tt = r_p * xdt * ds_p
            db_acc = db_acc + lax.dot_general((xdt * ds_p).astype(BF), dh_b, NT_DIMS, preferred_element_type=F32)
            dgam_m = jnp.sum(dh * hp, axis=0, keepdims=True)
            for j, even in ((je, True), (jo, False)):
                sel = lane_even if even else jnp.logical_not(lane_even)
                dy_j = jnp.where(sel, dy_b, zero)
                l_j = _head_decay(cm, j)
                l_jt = _head_decay_t(cm, j)
                m_j = cm["cb"] * l_j
                m_jt = cb_t * l_jt
                dm = lax.dot_general(dy_j, xdt_b, NT_DIMS, preferred_element_type=F32)
                dm_t = lax.dot_general(xdt_b, dy_j, NT_DIMS, preferred_element_type=F32)
                dxdt = dxdt + jnp.dot(m_jt.astype(BF), dy_j, preferred_element_type=F32)
                dcb = dcb + dm * l_j
                dcb_t = dcb_t + dm_t * l_jt
                t_j = jnp.where(sel, tt, 0.0)
                col = jnp.sum(dm * m_j - dm_t * m_jt + (jnp.where(sel, de, 0.0) - t_j), axis=1, keepdims=True)
                gam_j = cm["gam"][:, j:j + 1]
                last = (jnp.sum(jnp.sum(t_j, axis=0, keepdims=True), axis=1, keepdims=True)
                        + jnp.sum(jnp.where(sel[0:1, :], dgam_m, 0.0), axis=1, keepdims=True) * gam_j)
                col = col + jnp.where(sub[:, 0:1] == Q - 1, last, 0.0)
                dcs = dcs + jnp.where(lane == j, col, 0.0)
            dh_scr[p] = dh_in + dh * _pair_row(lane_even, cm["gam"], je)
            dx_ref[:, 128 * p:128 * p + 128] = dxp + dxdt * dt_p
            dd = dxdt * xp
            ddt = ddt + jnp.where(lane == je, jnp.sum(jnp.where(lane_even, dd, 0.0), axis=1, keepdims=True), 0.0)
            ddt = ddt + jnp.where(lane == jo, jnp.sum(jnp.where(lane_even, 0.0, dd), axis=1, keepdims=True), 0.0)
        dc_ref[:, :] = dc_acc + jnp.dot(dcb.astype(BF), cm["bc"], preferred_element_type=F32)
        db_ref[:, :] = db_acc + jnp.dot(dcb_t.astype(BF), cm["cc"], preferred_element_type=F32)
        tri_t = (sub <= lane).astype(F32)
        dd_a = jnp.dot(tri_t, dcs, precision=lax.Precision.HIGHEST, preferred_element_type=F32)
        ddt = ddt + dd_a * a_neg
        acc_scr[5:6, :] += jnp.sum(dd_a * cm["dt"], axis=0, keepdims=True)
        draw = jnp.where(cm["live"], ddt * _sigmoid_gate(cm["dtr"] + bias), 0.0)
        acc_scr[4:5, :] += jnp.sum(draw, axis=0, keepdims=True)
        ddt_ref[:, :] = draw.astype(BF)

        @pl.when(ci == NCH - 1)
        def _():
            lane1 = _lanes((1, 128))
            dd = jnp.zeros((1, 128), F32)
            for p in range(4):
                row = acc_scr[p:p + 1, :]
                dd = dd + jnp.where(lane1 == 2 * p, jnp.sum(jnp.where(lane1 < 64, row, 0.0), axis=1, keepdims=True), 0.0)
                dd = dd + jnp.where(lane1 == 2 * p + 1, jnp.sum(jnp.where(lane1 < 64, 0.0, row), axis=1, keepdims=True), 0.0)
            dpar_ref[...] = jnp.concatenate([acc_scr[4:5, :], acc_scr[5:6, :] * a_neg, dd, jnp.zeros((5, 128), F32)], axis=0)

    par = _spec((2, 1, 128), lambda c: (0, 0, 0))
    wide = _spec((Q, SSD_W), lambda c: (NCH - 1 - c, 0))
    thin = _spec((Q, 256), lambda c: (NCH - 1 - c, 0))
    vec = _spec((1, SSD_W), lambda c: (0, 0))
    xbc = PXBC // XBC
    halo = 2 * HALO
    chunk = pltpu.VMEM((Q, XBC), F32)
    padded = pltpu.VMEM((Q + HALO, XBC), F32)
    return pl.pallas_call(
        body, grid=(NCH,),
        in_specs=[wide, _spec((Q, XBC), lambda c: (NCH - 1 - c, xbc)),
                  _spec((halo, XBC), lambda c: (jnp.maximum((NCH - 1 - c) * (Q // halo) - 1, 0), xbc)), thin, wide, wide,
                  _spec((2, 1, 4, 128, 128), lambda c: (0, NCH - 1 - c, 0, 0, 0)), _spec((4, XBC), lambda c: (0, 0)),
                  _spec((1, XBC), lambda c: (0, 0)), par, par, par, vec, ANY_SPEC],
        out_specs=[wide, _spec((Q, XBC), lambda c: (NCH - 1 - c, 0)), thin, _spec((2, 8, 128), lambda c: (0, 0, 0)), vec,
                   _spec((4, XBC), lambda c: (0, 0)), _spec((1, XBC), lambda c: (0, 0))],
        out_shape=[jax.ShapeDtypeStruct((T, SSD_W), BF), jax.ShapeDtypeStruct((T, XBC), BF), jax.ShapeDtypeStruct((T, 256), BF),
                   jax.ShapeDtypeStruct((2, 8, 128), F32), jax.ShapeDtypeStruct((1, SSD_W), F32), jax.ShapeDtypeStruct((4, XBC), F32),
                   jax.ShapeDtypeStruct((1, XBC), F32)],
        scratch_shapes=[pltpu.VMEM((2, 4, 128, 128), F32), pltpu.VMEM((2, 8, 128), F32), padded, chunk, chunk, chunk, padded],
        compiler_params=_params(), name="ssd_bwd")(dyn, proj, proj, dt_raw, proj, y_pre, h_prev, conv_w, conv_b, dt_bias2, a_log2, d2, norm_w, after)


def _lru_gates(back, cw, cb, wa, ba, wx, bx, lam):
    xr = _conv(back, cw, cb)
    xr_b = xr.astype(BF)
    r = _sigmoid_gate(jnp.dot(xr_b, wa, preferred_element_type=F32) + ba)
    i = _sigmoid_gate(jnp.dot(xr_b, wx, preferred_element_type=F32) + bx)
    sp = _softplus(-lam)
    la = (-LRU_C) * r * sp
    a = jnp.exp(la)
    mult2 = -jnp.tanh(la) * (a * a + 1.0)
    return xr, xr_b, r, i, sp, a, jnp.sqrt(mult2), mult2


SEG_LEN = 68
SEGS = T // SEG_LEN


def _seg_rows(j, k, off=0):
    return pl.ds(off + j * 8 * SEG_LEN + k, 8, stride=SEG_LEN)


def _segmented_scan(mul_ref, mul_row0, add_ref, out_ref, loc_scr, prod_scr, carry_scr, reverse):
    groups = SEGS // 8
    off = mul_row0 + (1 if reverse else 0)

    def local(i, carry):
        k = SEG_LEN - 1 - i if reverse else i
        new = []
        for j in range(groups):
            h, p = carry[2 * j], carry[2 * j + 1]
            m = mul_ref[_seg_rows(j, k, off), :]
            h = m * h + add_ref[_seg_rows(j, k), :]
            p = m * p
            loc_scr[_seg_rows(j, k), :] = h
            prod_scr[_seg_rows(j, k), :] = p
            new += [h, p]
        return tuple(new)

    lax.fori_loop(0, SEG_LEN, local, (jnp.zeros((8, 128), F32), jnp.ones((8, 128), F32)) * groups)

    def chain(i, c):
        s = SEGS - 1 - i if reverse else i
        carry_scr[pl.ds(s, 1), :] = c
        edge = s * SEG_LEN + (0 if reverse else SEG_LEN - 1)
        return loc_scr[pl.ds(edge, 1), :] + prod_scr[pl.ds(edge, 1), :] * c

    lax.fori_loop(0, SEGS, chain, jnp.zeros((1, 128), F32))

    def fold(k, carry):
        for j in range(groups):
            rows = _seg_rows(j, k)
            out_ref[rows, :] = loc_scr[rows, :] + prod_scr[rows, :] * carry_scr[8 * j:8 * j + 8, :]
        return carry

    lax.fori_loop(0, SEG_LEN, fold, 0)


def lru_fwd(proj, cw, cb, wa2, ba, wx2, bx, lam, after):
    def body(x_ref, cw_ref, cb_ref, wa_ref, ba_ref, wx_ref, bx_ref, lam_ref, _after, h_ref, a_ref, xpad, u_scr, loc_scr, prod_scr, carry_scr):
        _fill_padded(xpad, x_ref)

        def chunk(r0):
            xr, _, _, i, _, a, mult, _ = _lru_gates(_back(xpad, r0), cw_ref[...], cb_ref[...], wa_ref[0], ba_ref[...], wx_ref[0], bx_ref[...],
                                                 lam_ref[...])
            a_ref[pl.ds(r0, Q), :] = a
            u_scr[pl.ds(r0, Q), :] = jnp.where(_rows(a.shape, r0) >= NPAD, mult * (i * xr), 0.0)

        _chunks(chunk, unrolled=True)
        _segmented_scan(a_ref, 0, u_scr, h_ref, loc_scr, prod_scr, carry_scr, reverse=False)

    c0 = PXL // 128
    vec = _spec((1, 128), lambda c: (0, c))
    mat = _spec((1, 128, 128), lambda c: (c, 0, 0))
    seq = pltpu.VMEM((T, 128), F32)
    return pl.pallas_call(
        body, grid=(8,),
        in_specs=[_spec((T, 128), lambda c: (0, c0 + c)), _spec((4, 128), lambda c: (0, c)), vec, mat, vec, mat, vec, vec, ANY_SPEC],
        out_specs=[_spec((T, 128), lambda c: (0, c)), _spec((T, 128), lambda c: (0, c))],
        out_shape=[jax.ShapeDtypeStruct((T, LRU_W), F32), jax.ShapeDtypeStruct((T, LRU_W), F32)],
        scratch_shapes=[pltpu.VMEM((T + 2 * HALO, 128), F32), seq, seq, seq, pltpu.VMEM((SEGS, 128), F32)],
        compiler_params=_params(), name="lru_fwd")(proj, cw, cb, wa2, ba, wx2, bx, lam, after)


def lru_bwd(dh_out, a, hseq, proj, cw, cb, wa2, ba, wx2, bx, lam):
    def body(d_ref, a_ref, h_ref, x_ref, cw_ref, cb_ref, wa_ref, ba_ref, wx_ref, bx_ref, lam_ref,
             dx_ref, dcw_ref, dcb_ref, dwa_ref, dba_ref, dwx_ref, dbx_ref, dlam_ref, xpad, hpad, dpad, dh_ref, loc_scr, prod_scr, carry_scr):
        _fill_padded(dpad, a_ref)
        _segmented_scan(dpad, HALO, d_ref, dh_ref, loc_scr, prod_scr, carry_scr, reverse=True)
        _fill_padded(xpad, x_ref)
        _fill_padded(hpad, h_ref)
        dpad[0:HALO, :] = jnp.zeros((HALO, 128), F32)
        dpad[T + HALO:T + 2 * HALO, :] = jnp.zeros((HALO, 128), F32)
        for ref in (dcw_ref, dcb_ref, dwa_ref, dba_ref, dwx_ref, dbx_ref, dlam_ref):
            ref[...] = jnp.zeros_like(ref)
        lam = lam_ref[...]

        def first(r0):
            back = _back(xpad, r0)
            xr, xr_b, r, i, sp, a, mult, mult2 = _lru_gates(back, cw_ref[...], cb_ref[...], wa_ref[0], ba_ref[...], wx_ref[0], bx_ref[...], lam)
            dh = dh_ref[pl.ds(r0, Q), :]
            da = dh * _back(hpad, r0)(1)
            du = jnp.where(_rows(dh.shape, r0) >= NPAD, dh, 0.0)
            dmult = du * (i * xr)
            di = du * (mult * xr)
            dxr = du * (mult * i)
            dla = da * a - dmult * (a * a) * lax.rsqrt(mult2)
            dr = dla * ((-LRU_C) * sp)
            dlam_ref[...] += jnp.sum(dla * ((-LRU_C) * r), axis=0, keepdims=True)
            dpr = dr * r * (1.0 - r)
            dpi = di * i * (1.0 - i)
            dba_ref[...] += jnp.sum(dpr, axis=0, keepdims=True)
            dbx_ref[...] += jnp.sum(dpi, axis=0, keepdims=True)
            dpr_b = dpr.astype(BF)
            dpi_b = dpi.astype(BF)
            dxr = (dxr + lax.dot_general(dpr_b, wa_ref[0], NT_DIMS, preferred_element_type=F32)
                   + lax.dot_general(dpi_b, wx_ref[0], NT_DIMS, preferred_element_type=F32))
            dwa_ref[0] += lax.dot_general(xr_b, dpr_b, TN_DIMS, preferred_element_type=F32)
            dwx_ref[0] += lax.dot_general(xr_b, dpi_b, TN_DIMS, preferred_element_type=F32)
            dpad[pl.ds(r0 + HALO, Q), :] = dxr
            dcw, dcb = _conv_bwd_w(dxr, back)
            dcw_ref[...] += dcw
            dcb_ref[...] += dcb

        _chunks(first, unrolled=True)
        dlam_ref[...] = -dlam_ref[...] * _sigmoid_gate(-lam)

        def second(r0):
            dx_ref[pl.ds(r0, Q), :] = _conv_bwd_x(_ahead(dpad, r0), cw_ref[...]).astype(BF)

        _chunks(second)

    c0 = PXL // 128
    vec = _spec((1, 128), lambda c: (0, c))
    mat = _spec((1, 128, 128), lambda c: (c, 0, 0))
    col = _spec((T, 128), lambda c: (0, c))
    vshape = jax.ShapeDtypeStruct((1, LRU_W), F32)
    mshape = jax.ShapeDtypeStruct((8, 128, 128), F32)
    pad = pltpu.VMEM((T + 2 * HALO, 128), F32)
    seq = pltpu.VMEM((T, 128), F32)
    return pl.pallas_call(
        body, grid=(8,),
        in_specs=[col, col, col, _spec((T, 128), lambda c: (0, c0 + c)), _spec((4, 128), lambda c: (0, c)), vec, mat, vec, mat, vec, vec],
        out_specs=[col, _spec((4, 128), lambda c: (0, c)), vec, mat, vec, mat, vec, vec],
        out_shape=[jax.ShapeDtypeStruct((T, LRU_W), BF), jax.ShapeDtypeStruct((4, LRU_W), F32), vshape, mshape, vshape, mshape, vshape, vshape],
        scratch_shapes=[pad, pad, pad, seq, seq, seq, pltpu.VMEM((SEGS, 128), F32)],
        compiler_params=_params(), name="lru_bwd")(dh_out, a, hseq, proj, cw, cb, wa2, ba, wx2, bx, lam)


FF_TILE = 256
FF_TILE_ROWS = list(range(0, D_FF, FF_TILE))


def gate_up(h1, wn, w_gate, w_up):
    def body(h_ref, wn_ref, wg_hbm, wu_hbm, gt_ref, up_ref, act_ref, u_ref, wg_ref, wu_ref, wg_sems, wu_sems):
        gate_ready = _arriving(wg_hbm, wg_ref, wg_sems, FF_TILE_ROWS, FF_TILE)
        up_ready = _arriving(wu_hbm, wu_ref, wu_sems, FF_TILE_ROWS, FF_TILE)
        for r in (0, HALF):
            u_ref[r:r + HALF, :] = _rms(h_ref[r:r + HALF, :], wn_ref[...]).astype(BF)

        def tile(c0):
            cols = pl.ds(c0, FF_TILE)
            gate_ready(c0 // FF_TILE)
            up_ready(c0 // FF_TILE)
            gt = lax.dot_general(u_ref[...], wg_ref[cols, :], NT_DIMS, preferred_element_type=F32)
            up = lax.dot_general(u_ref[...], wu_ref[cols, :], NT_DIMS, preferred_element_type=F32)
            gt_ref[:, cols] = gt.astype(BF)
            up_ref[:, cols] = up.astype(BF)
            act_ref[:, cols] = (gt * _sigmoid(gt) * up).astype(BF)

        _col_tiles(D_FF, FF_TILE, tile)

    big = jax.ShapeDtypeStruct((T, D_FF), BF)
    return pl.pallas_call(
        body, grid=(T // RC,), in_specs=[_rows_spec(D), _vec(D), ANY_SPEC, ANY_SPEC],
        out_specs=[_rows_spec(D_FF), _rows_spec(D_FF), _rows_spec(D_FF), _rows_spec(D)],
        out_shape=[big, big, big, jax.ShapeDtypeStruct((T, D), BF)],
        scratch_shapes=[pltpu.VMEM((D_FF, D), BF)] * 2 + [pltpu.SemaphoreType.DMA((len(FF_TILE_ROWS),))] * 2,
        compiler_params=_params(), name="gate_up")(h1, wn, w_gate, w_up)


def down_loss(act, w_down, h1, target, wf):
    first = NPAD + N_META

    def body(a_ref, w_ref, r_ref, t_hbm, wf_ref, d_ref, db_ref, l_ref, dw_ref, h_scr, t_ref, t_sem):
        i = pl.program_id(0)
        _zero_at_first(l_ref, dw_ref)
        head = pltpu.make_async_copy(t_hbm.at[pl.ds(0, RC - first)], t_ref.at[pl.ds(first, RC - first)], t_sem)
        rest = pltpu.make_async_copy(t_hbm.at[pl.ds(pl.multiple_of(jnp.maximum(i * RC - first, 0), 32), RC)], t_ref, t_sem)

        @pl.when(i == 0)
        def _():
            t_ref[0:first, :] = jnp.zeros((first, D), F32)
            head.start()

        @pl.when(i > 0)
        def _():
            rest.start()

        def tile(c0):
            cols = pl.ds(c0, 512)
            h_scr[:, cols] = r_ref[:, cols] + jnp.dot(a_ref[...], w_ref[:, cols], preferred_element_type=F32)

        _col_tiles(D, 512, tile)

        @pl.when(i == 0)
        def _():
            head.wait()

        @pl.when(i > 0)
        def _():
            rest.wait()

        for r in (0, HALF):
            h = h_scr[r:r + HALF, :]
            live = _rows((HALF, D), i * RC + r) >= first
            err = jnp.where(live, _rms(h, wf_ref[...]) - t_ref[r:r + HALF, :], 0.0)
            l_ref[...] += 0.5 * jnp.sum(jnp.sum(err * err, axis=1, keepdims=True) * (1.0 / D), axis=0, keepdims=True)
            dh, dw = _rms_bwd(err * (1.0 / D), h, wf_ref[...])
            dw_ref[...] += jnp.sum(dw, axis=0, keepdims=True)
            d_ref[r:r + HALF, :] = dh
            db_ref[r:r + HALF, :] = dh.astype(BF)

    return pl.pallas_call(
        body, grid=(T // RC,),
        in_specs=[_rows_spec(D_FF), _whole((D_FF, D)), _rows_spec(D), pl.BlockSpec(memory_space=pl.ANY), _vec(D)],
        out_specs=[_rows_spec(D), _rows_spec(D), _spec((1, 128), lambda i: (0, 0)), _vec(D)],
        out_shape=[jax.ShapeDtypeStruct((T, D), F32), jax.ShapeDtypeStruct((T, D), BF), jax.ShapeDtypeStruct((1, 128), F32),
                   jax.ShapeDtypeStruct((1, D), F32)],
        scratch_shapes=[pltpu.VMEM((RC, D), F32), pltpu.VMEM((RC, D), F32), pltpu.SemaphoreType.DMA],
        compiler_params=_params(), name="down_loss")(act, w_down, h1, target, wf)


def swiglu_bwd(dh2_b, w_down, gt, up, act, u2):
    tn = 256

    def body(d_hbm, u_hbm, w_ref, gt_ref, up_ref, act_ref, dg_ref, du_ref, gd_ref, gg_ref, gu_ref, d_ref, u_ref, d_sems, u_sems):
        chunks = list(range(0, T, RC))
        d_ready = _arriving(d_hbm, d_ref, d_sems, chunks, RC)
        u_ready = _arriving(u_hbm, u_ref, u_sems, chunks, RC)

        def rows(r0):
            part = pl.ds(r0, RC)
            d_ready(r0 // RC)
            dact = lax.dot_general(d_ref[part, :], w_ref[...], NT_DIMS, preferred_element_type=F32)
            gt_ = gt_ref[part, :].astype(F32)
            up_ = up_ref[part, :].astype(F32)
            sg = _sigmoid(gt_)
            dg_ref[part, :] = (dact * up_ * (sg * (1.0 + gt_ * (1.0 - sg)))).astype(BF)
            du_ref[part, :] = (dact * (gt_ * sg)).astype(BF)

        _col_tiles(T, RC, rows)
        for k in range(len(chunks)):
            u_ready(k)
        gd_ref[...] = lax.dot_general(act_ref[...], d_ref[...], TN_DIMS, preferred_element_type=F32).astype(BF)
        gg_ref[...] = lax.dot_general(dg_ref[...], u_ref[...], TN_DIMS, preferred_element_type=F32).astype(BF)
        gu_ref[...] = lax.dot_general(du_ref[...], u_ref[...], TN_DIMS, preferred_element_type=F32).astype(BF)

    cols = _spec((T, tn), lambda j: (0, j))
    wrow = _spec((tn, D), lambda j: (j, 0))
    big = jax.ShapeDtypeStruct((T, D_FF), BF)
    grad = jax.ShapeDtypeStruct((D_FF, D), BF)
    return pl.pallas_call(
        body, grid=(D_FF // tn,), in_specs=[ANY_SPEC, ANY_SPEC, wrow, cols, cols, cols],
        out_specs=[cols, cols, wrow, wrow, wrow], out_shape=[big, big, grad, grad, grad],
        scratch_shapes=[pltpu.VMEM((T, D), BF)] * 2 + [pltpu.SemaphoreType.DMA((T // RC,))] * 2,
        compiler_params=_params(), name="swiglu_bwd")(dh2_b, u2, w_down, gt, up, act)


def gate_up_bwd(dgt, dup, w_gate, w_up, h1, wn, dh2):
    def body(dg_ref, du_ref, wg_hbm, wu_hbm, h_ref, wn_ref, r_ref, d_ref, db_ref, dw_ref, du_scr, wg_ref, wu_ref, wg_sems, wu_sems):
        gate_ready = _arriving(wg_hbm, wg_ref, wg_sems, FF_TILE_ROWS, FF_TILE)
        up_ready = _arriving(wu_hbm, wu_ref, wu_sems, FF_TILE_ROWS, FF_TILE)
        _zero_at_first(dw_ref)

        du_scr[...] = jnp.zeros_like(du_scr)

        def tile(c0):
            k = pl.ds(c0, FF_TILE)
            gate_ready(c0 // FF_TILE)
            up_ready(c0 // FF_TILE)
            du_scr[...] += (jnp.dot(dg_ref[:, k], wg_ref[k, :], preferred_element_type=F32)
                            + jnp.dot(du_ref[:, k], wu_ref[k, :], preferred_element_type=F32))

        _col_tiles(D_FF, FF_TILE, tile)
        for r in (0, HALF):
            dh, dw = _rms_bwd(du_scr[r:r + HALF, :], h_ref[r:r + HALF, :], wn_ref[...])
            dw_ref[...] += jnp.sum(dw, axis=0, keepdims=True)
            dh = dh + r_ref[r:r + HALF, :]
            d_ref[r:r + HALF, :] = dh
            db_ref[r:r + HALF, :] = dh.astype(BF)

    return pl.pallas_call(
        body, grid=(T // RC,),
        in_specs=[_rows_spec(D_FF), _rows_spec(D_FF), ANY_SPEC, ANY_SPEC, _rows_spec(D), _vec(D), _rows_spec(D)],
        out_specs=[_rows_spec(D), _rows_spec(D), _vec(D)],
        out_shape=[jax.ShapeDtypeStruct((T, D), F32), jax.ShapeDtypeStruct((T, D), BF), jax.ShapeDtypeStruct((1, D), F32)],
        scratch_shapes=[pltpu.VMEM((RC, D), F32)] + [pltpu.VMEM((D_FF, D), BF)] * 2 + [pltpu.SemaphoreType.DMA((len(FF_TILE_ROWS),))] * 2,
        compiler_params=_params(), name="gate_up_bwd")(dgt, dup, w_gate, w_up, h1, wn, dh2)


def _adamw(w, g, m, v):
    m = ADAM_B1 * m + (1.0 - ADAM_B1) * g
    v = ADAM_B2 * v + (1.0 - ADAM_B2) * (g * g)
    m_hat = m / (1.0 - ADAM_B1 ** ADAM_STEP)
    v_hat = v / (1.0 - ADAM_B2 ** ADAM_STEP)
    delta = -ADAM_LR * (m_hat / (jnp.sqrt(v_hat) + ADAM_EPS) + ADAM_WD * w)
    return delta, m, v


def adamw_shards(name, recvs, ws, ms, vs):
    n = len(ws)

    def body(*refs):
        ins, outs = refs[:4 * n], refs[4 * n:]
        for k in range(n):
            p_ref, w_ref, m_ref, v_ref = ins[k], ins[n + k], ins[2 * n + k], ins[3 * n + k]
            g = p_ref[0].astype(F32)
            for s in range(1, 8):
                g = g + p_ref[s].astype(F32)
            outs[4 * k][...] = g
            outs[4 * k + 1][...], outs[4 * k + 2][...], outs[4 * k + 3][...] = _adamw(w_ref[...], g, m_ref[...], v_ref[...])

    tiles = [_spec((w.shape[0] // 2, w.shape[1]), lambda i: (i, 0)) for w in ws]
    recv_tiles = [_spec((8, w.shape[0] // 2, w.shape[1]), lambda i: (0, i, 0)) for w in ws]
    res = pl.pallas_call(
        body, grid=(2,), in_specs=recv_tiles + tiles * 3,
        out_specs=[t for t in tiles for _ in range(4)],
        out_shape=[jax.ShapeDtypeStruct(w.shape, F32) for w in ws for _ in range(4)],
        compiler_params=_params(), name=name)(*recvs, *ws, *ms, *vs)
    return [list(res[4 * k:4 * k + 4]) for k in range(n)]


def adamw_w_in(recv, w, m, v, after):
    rows = 34
    per_row = D // 128

    def body(p_ref, w_ref, m_ref, v_ref, _after, g_ref, d_ref, mo_ref, vo_ref):
        def chunk(c, carry):
            lines = pl.ds(pl.multiple_of(c * per_row * rows, 16), per_row * rows)
            g = p_ref[0, lines, :].astype(F32)
            for s in range(1, 8):
                g = g + p_ref[s, lines, :].astype(F32)
            g = g.reshape(rows, per_row, 128)
            part = pl.ds(c * rows, rows)
            g_ref[part] = g
            d_ref[part], mo_ref[part], vo_ref[part] = _adamw(w_ref[part], g, m_ref[part], v_ref[part])
            return carry

        lax.fori_loop(0, w.shape[0] // rows, chunk, 0)

    shape = jax.ShapeDtypeStruct(w.shape, F32)
    whole = pl.BlockSpec(memory_space=pltpu.VMEM)
    return pl.pallas_call(body, out_shape=[shape] * 4, in_specs=[whole] * 4 + [ANY_SPEC], compiler_params=_params(0),
                          name="adamw_w_in")(recv, w, m, v, after)


def sum_slabs(recv):
    def body(p_ref, o_ref):
        g = p_ref[0]
        for s in range(1, 8):
            g = g + p_ref[s]
        for s in range(8):
            o_ref[s] = g

    return pl.pallas_call(body, out_shape=jax.ShapeDtypeStruct(recv.shape, F32), compiler_params=_params(0), name="sum_slabs")(recv)


SIMPLE = [("norm1_w", 1024), ("ssd_conv_b", 1536), ("ssd_dt_bias", 16), ("ssd_a_log", 16), ("ssd_d", 16), ("ssd_norm_w", 1024),
          ("lru_conv_b", 1024), ("lru_ba", 1024), ("lru_bx", 1024), ("lru_lambda", 1024), ("lru_norm_w", 1024), ("norm2_w", 1024),
          ("final_norm_w", 1024)]
SPECIAL = ["lru_wa", "lru_wx", "meta_tokens", "ssd_conv_w", "lru_conv_w"]
SM_ROWS = 192
SM_WA, SM_WX, SM_META, SM_SCW, SM_LCW, SM_LOSS = 14, 78, 142, 158, 166, 170


def _simple_rows():
    rows, r = {}, 0
    for name, n in SIMPLE:
        rows[name] = r
        r += -(-n // 1024)
    return rows


def adamw_small(sm, special_g, ws, ms, vs):
    rows = _simple_rows()
    ns, nx = len(SIMPLE), len(SPECIAL)

    def body(*refs):
        sm_ref = refs[0]
        gx = refs[1:1 + nx]
        wr = refs[1 + nx:1 + nx + ns + nx]
        mr = refs[1 + nx + ns + nx:1 + nx + 2 * (ns + nx)]
        vr = refs[1 + nx + 2 * (ns + nx):1 + nx + 3 * (ns + nx)]
        outs = refs[1 + nx + 3 * (ns + nx):]
        o = 0
        for k, (name, n) in enumerate(SIMPLE):
            r0 = rows[name]
            for c0 in range(0, n, 1024):
                wd = min(1024, n - c0)
                g = sm_ref[r0 + c0 // 1024:r0 + c0 // 1024 + 1, 0:wd]
                sl = (slice(None), slice(c0, c0 + wd))
                d, m2, v2 = _adamw(wr[k][sl], g, mr[k][sl], vr[k][sl])
                outs[o][sl] = g
                outs[o + 1][sl] = d
                outs[o + 2][sl] = m2
                outs[o + 3][sl] = v2
            o += 4
        for k in range(nx):
            d, m2, v2 = _adamw(wr[ns + k][...], gx[k][...], mr[ns + k][...], vr[ns + k][...])
            outs[o][...] = d
            outs[o + 1][...] = m2
            outs[o + 2][...] = v2
            o += 3

    out_shape = []
    for k in range(ns):
        out_shape += [jax.ShapeDtypeStruct(ws[k].shape, F32)] * 4
    for k in range(nx):
        out_shape += [jax.ShapeDtypeStruct(ws[ns + k].shape, F32)] * 3
    return pl.pallas_call(body, out_shape=out_shape, compiler_params=_params(0), name="adamw_small")(sm, *special_g, *ws, *ms, *vs)


def _place():
    return lax.axis_index("x"), lax.axis_index("y"), lax.axis_index("c")


def _index(px, py, pc):
    return 4 * px + 2 * py + pc


def all_gather(name, shards):
    n = len(shards)
    hbm = pl.BlockSpec(memory_space=pl.ANY)

    def pieces(s):
        tile = 32 // s.dtype.itemsize
        per = s.shape[0] // tile // 4 * tile
        return [(0, s.shape[0])] if s.shape[0] < 256 else [(r * per, per if r < 3 else s.shape[0] - 3 * per) for r in range(4)]

    parts = [pieces(s) for s in shards]
    first_sem = [7 * sum(len(p) for p in parts[:i]) for i in range(n + 1)]

    def body(*refs):
        ins, outs = refs[:n], refs[n:2 * n]
        send_sems, recv_sems, local_sems = refs[2 * n:]
        x, y, c = _place()
        me, sibling = (x, y, c), (x, y, 1 - c)
        chips = [(1 - x, y), (x, 1 - y), (1 - x, 1 - y)]

        def copy(i, r, k, block, to, src=None):
            rows = pl.ds(*parts[i][r])
            dst = outs[i].at[_index(*block), rows]
            sem = first_sem[i] + 7 * r + k
            return pltpu.make_async_remote_copy(src_ref=dst if src is None else src.at[rows], dst_ref=dst, send_sem=send_sems.at[sem],
                                                recv_sem=recv_sems.at[sem], device_id=to, device_id_type=MESH)

        every = [(i, r) for i in range(n) for r in range(len(parts[i]))]
        mine = [pltpu.make_async_copy(ins[i], outs[i].at[_index(*me)], local_sems.at[i]) for i in range(n)]
        for cp in mine:
            cp.start()
        first = []
        for i, r in every:
            first += [copy(i, r, 1 + j, me, (*chip, c), src=ins[i]) for j, chip in enumerate(chips)]
            first.append(copy(i, r, 0, me, sibling, src=ins[i]))
        for cp in first:
            cp.start()
        passed = []
        for i, r in every:
            for j, chip in enumerate(chips):
                copy(i, r, 1 + j, (*chip, c), me).wait_recv()
                cp = copy(i, r, 4 + j, (*chip, c), sibling)
                cp.start()
                passed.append(cp)
        for i, r in every:
            copy(i, r, 0, sibling, me).wait_recv()
            for j, chip in enumerate(chips):
                copy(i, r, 4 + j, (*chip, 1 - c), me).wait_recv()
        for cp in first + passed:
            cp.wait_send()
        for cp in mine:
            cp.wait()

    return pl.pallas_call(
        body, in_specs=[hbm] * n, out_specs=[hbm] * n,
        out_shape=[jax.ShapeDtypeStruct((8,) + s.shape, s.dtype) for s in shards],
        scratch_shapes=[pltpu.SemaphoreType.DMA((first_sem[n],)), pltpu.SemaphoreType.DMA((first_sem[n],)), pltpu.SemaphoreType.DMA((n,))],
        name=name)(*shards)


HBM_SPEC = pl.BlockSpec(memory_space=pltpu.HBM)
SEM_SPEC = pl.BlockSpec(memory_space=pltpu.SEMAPHORE)
EFFECT = pltpu.SideEffectType.DATAFLOW_SIDE_EFFECTING


def _peers(x, y, c):
    return [((1 - x) if k & 4 else x, (1 - y) if k & 2 else y, (1 - c) if k & 1 else c) for k in range(1, 8)]


def _pieces(rows):
    for n in (4, 2):
        if rows % (16 * n) == 0:
            return [(r * (rows // n), rows // n) for r in range(n)]
    return [(0, rows)]


def _peer_copies(src, land, send_sems, recv_sems, k, peer, mine):
    block = src.at[_index(*peer)]
    return [pltpu.make_async_remote_copy(src_ref=block.at[pl.ds(r0, nr)], dst_ref=land.at[mine, pl.ds(r0, nr)], send_sem=send_sems.at[k],
                                         recv_sem=recv_sems.at[k], device_id=peer, device_id_type=MESH)
            for r0, nr in _pieces(block.shape[0])]


OWN = 7


def _own_copy(src, land, send_sems, mine):
    return pltpu.make_async_copy(src.at[mine], land.at[mine], send_sems.at[OWN])


def copies_start(name, srcs):
    n = len(srcs)

    def body(*refs):
        ins, lands = refs[:n], refs[n:2 * n]
        sends, recvs = refs[2 * n:3 * n], refs[3 * n:4 * n]
        token = refs[-1]
        x, y, c = _place()
        mine = _index(x, y, c)
        for i in range(n):
            per_peer = [_peer_copies(ins[i], lands[i], sends[i], recvs[i], k, peer, mine) for k, peer in enumerate(_peers(x, y, c))]
            for piece in zip(*per_peer):
                for cp in piece:
                    cp.start()
            _own_copy(ins[i], lands[i], sends[i], mine).start()
        token[...] = jnp.zeros_like(token)

    res = pl.pallas_call(
        body, name=name,
        out_shape=([pltpu.SemaphoreType.DMA((8,))] * n + [pltpu.SemaphoreType.DMA((7,))] * n + [pltpu.HBM(s.shape, s.dtype) for s in srcs] * 2
                   + [jax.ShapeDtypeStruct((8, 128), F32)]),
        in_specs=[HBM_SPEC] * (2 * n),
        out_specs=[SEM_SPEC] * (2 * n) + [HBM_SPEC] * (2 * n) + [pl.BlockSpec(memory_space=pltpu.VMEM)],
        input_output_aliases={i: 2 * n + i for i in range(2 * n)},
        compiler_params=pltpu.CompilerParams(has_side_effects=EFFECT),
    )(*[pltpu.with_memory_space_constraint(s, pltpu.HBM) for s in srcs],
      *[pltpu.with_memory_space_constraint(lax.empty(s.shape, s.dtype), pltpu.HBM) for s in srcs])
    return [(res[i], res[n + i], res[2 * n + i], res[3 * n + i]) for i in range(n)], res[-1]


def copies_wait(name, started, after):
    n = len(started)

    def body(*refs):
        ins, lands = refs[:n], refs[n:2 * n]
        sends, recvs = refs[2 * n:3 * n], refs[3 * n:4 * n]
        x, y, c = _place()
        mine = _index(x, y, c)
        for i in range(n):
            for k, peer in enumerate(_peers(x, y, c)):
                arrival = pltpu.make_async_remote_copy(src_ref=ins[i].at[mine], dst_ref=lands[i].at[_index(*peer)],
                                                       send_sem=sends[i].at[k], recv_sem=recvs[i].at[k], device_id=peer, device_id_type=MESH)
                arrival.wait_send()
                arrival.wait_recv()
            _own_copy(ins[i], lands[i], sends[i], mine).wait()

    srcs = [s[2] for s in started]
    lands = [s[3] for s in started]
    res = pl.pallas_call(
        body, name=name,
        out_shape=[pltpu.HBM(s.shape, s.dtype) for s in srcs] + [pltpu.HBM(z.shape, z.dtype) for z in lands],
        in_specs=[HBM_SPEC] * (2 * n) + [SEM_SPEC] * (2 * n) + [ANY_SPEC] * len(after),
        out_specs=[HBM_SPEC] * (2 * n),
        input_output_aliases={i: i for i in range(2 * n)},
        compiler_params=pltpu.CompilerParams(has_side_effects=EFFECT),
    )(*srcs, *lands, *[s[0] for s in started], *[s[1] for s in started], *after)
    return list(res[n:])


def _hop(src, land, send_sems, recv_sems, k, block, to):
    dst = land.at[_index(*block)]
    return pltpu.make_async_remote_copy(src_ref=dst if src is None else src, dst_ref=dst, send_sem=send_sems.at[k], recv_sem=recv_sems.at[k],
                                        device_id=to, device_id_type=MESH)


def _own_block(src, land, send_sems, mine):
    return pltpu.make_async_copy(src, land.at[mine], send_sems.at[4])


def _other_chips(x, y):
    return [(1 - x, y), (x, 1 - y), (1 - x, 1 - y)]


def gather_start(name, shards, through):
    n, m = len(shards), len(through)

    def body(*refs):
        ins, lands = refs[:n], refs[n:2 * n]
        sends, recvs = refs[2 * n + m:3 * n + m], refs[3 * n + m:4 * n + m]
        x, y, c = _place()
        for i in range(n):
            for j, chip in enumerate(_other_chips(x, y)):
                _hop(ins[i], lands[i], sends[i], recvs[i], 1 + j, (x, y, c), (*chip, c)).start()
            _hop(ins[i], lands[i], sends[i], recvs[i], 0, (x, y, c), (x, y, 1 - c)).start()
            _own_block(ins[i], lands[i], sends[i], _index(x, y, c)).start()

    own, passing = pltpu.SemaphoreType.DMA((5,)), pltpu.SemaphoreType.DMA((3,))
    zones = [jax.ShapeDtypeStruct((8,) + s.shape, s.dtype) for s in shards]
    res = pl.pallas_call(
        body, name=name,
        out_shape=([own] * (2 * n) + [passing] * (2 * n) + [pltpu.HBM(s.shape, s.dtype) for s in shards]
                   + [pltpu.HBM(z.shape, z.dtype) for z in zones] + [pltpu.HBM(t.shape, t.dtype) for t in through]),
        in_specs=[HBM_SPEC] * (2 * n + m),
        out_specs=[SEM_SPEC] * (4 * n) + [HBM_SPEC] * (2 * n + m),
        input_output_aliases={i: 4 * n + i for i in range(2 * n + m)},
        compiler_params=pltpu.CompilerParams(has_side_effects=EFFECT),
    )(*[pltpu.with_memory_space_constraint(s, pltpu.HBM) for s in shards],
      *[pltpu.with_memory_space_constraint(lax.empty(z.shape, z.dtype), pltpu.HBM) for z in zones],
      *[pltpu.with_memory_space_constraint(t, pltpu.HBM) for t in through])
    return [[res[4 * n + i], res[5 * n + i], res[i], res[n + i], res[2 * n + i], res[3 * n + i]] for i in range(n)], list(res[6 * n:])


def gather_stage(name, pass_on, finish, after):
    arrays = pass_on + finish
    n = len(arrays)

    def body(*refs):
        ins, lands = refs[:n], refs[n:2 * n]
        sems = [refs[(2 + q) * n:(3 + q) * n] for q in range(4)]
        x, y, c = _place()
        me, sibling = (x, y, c), (x, y, 1 - c)
        for i in range(len(pass_on)):
            send, recv, send_on, recv_on = (q[i] for q in sems)
            for j, chip in enumerate(_other_chips(x, y)):
                _hop(None, lands[i], send, recv, 1 + j, (*chip, c), me).wait_recv()
                _hop(None, lands[i], send_on, recv_on, j, (*chip, c), sibling).start()
        for i in range(len(pass_on), n):
            send, recv, send_on, recv_on = (q[i] for q in sems)
            _hop(ins[i], lands[i], send, recv, 0, sibling, me).wait_recv()
            for j, chip in enumerate(_other_chips(x, y)):
                _hop(None, lands[i], send_on, recv_on, j, (*chip, 1 - c), me).wait_recv()
            _hop(ins[i], lands[i], send, recv, 0, me, sibling).wait_send()
            _own_block(ins[i], lands[i], send, _index(*me)).wait()
            for j, chip in enumerate(_other_chips(x, y)):
                _hop(ins[i], lands[i], send, recv, 1 + j, me, (*chip, c)).wait_send()
                _hop(None, lands[i], send_on, recv_on, j, (*chip, c), sibling).wait_send()
        refs[-1][...] = jnp.zeros_like(refs[-1])

    res = pl.pallas_call(
        body, name=name,
        out_shape=([pltpu.HBM(a[0].shape, a[0].dtype) for a in arrays] + [pltpu.HBM(a[1].shape, a[1].dtype) for a in arrays]
                   + [jax.ShapeDtypeStruct((8, 128), F32)]),
        in_specs=[HBM_SPEC] * (2 * n) + [SEM_SPEC] * (4 * n) + [ANY_SPEC],
        out_specs=[HBM_SPEC] * (2 * n) + [pl.BlockSpec(memory_space=pltpu.VMEM)],
        input_output_aliases={i: i for i in range(2 * n)},
        compiler_params=pltpu.CompilerParams(has_side_effects=EFFECT),
    )(*[a[0] for a in arrays], *[a[1] for a in arrays], *[a[2 + q] for q in range(4) for a in arrays], after)
    for i, a in enumerate(arrays):
        a[0], a[1] = res[i], res[n + i]
    return [a[1] for a in finish], res[-1]


WEIGHTS = ["meta_tokens", "norm1_w", "w_in", "ssd_conv_w", "ssd_conv_b", "ssd_dt_bias", "ssd_a_log", "ssd_d", "ssd_norm_w", "lru_conv_w",
           "lru_conv_b", "lru_wa", "lru_ba", "lru_wx", "lru_bx", "lru_lambda", "lru_norm_w", "w_out", "norm2_w", "w_gate", "w_up", "w_down",
           "final_norm_w"]
COLUMN_SHARDED = ["w_in", "w_gate", "w_up"]


def _pair_blocks(w):
    w = w.reshape(8, 2, 64, 64)
    z = jnp.zeros((8, 64, 64), w.dtype)
    return jnp.concatenate([jnp.concatenate([w[:, 0], z], axis=2), jnp.concatenate([z, w[:, 1]], axis=2)], axis=1)


def _unpair_blocks(w2):
    return jnp.stack([w2[:, :64, :64], w2[:, 64:, 64:]], axis=1).reshape(16, 64, 64)


def _per_group(v):
    return jnp.pad(v.reshape(2, 1, 8), ((0, 0), (0, 0), (0, 120)))


def _pad_cols(v, n):
    return jnp.pad(v, ((0, 0), (0, n - v.shape[1])))


def local_step(x, target, meta, ssd_cw, lru_cw, w_in_shards, fetch, send, p):
    bias2, alog2, d2 = _per_group(p["ssd_dt_bias"]), _per_group(p["ssd_a_log"]), _per_group(p["ssd_d"])
    wa2 = _pair_blocks(p["lru_wa"]).astype(BF)
    wx2 = _pair_blocks(p["lru_wx"]).astype(BF)
    lru = (lru_cw, p["lru_conv_b"], wa2, p["lru_ba"], wx2, p["lru_bx"], p["lru_lambda"])

    proj, dt_raw, u1, h0, w_in, w_dt = in_proj(x, meta, p["norm1_w"], w_in_shards)
    yn_ssd, y_pre, h_prev = ssd_fwd(proj, dt_raw, ssd_cw, p["ssd_conv_b"], bias2, alog2, d2, p["ssd_norm_w"])
    _, moved = fetch([], yn_ssd)
    hseq, a = lru_fwd(proj, *lru, moved)
    (w_out,), _ = fetch(["w_out"], hseq)
    h1, cat = out_proj(yn_ssd, proj, hseq, p["lru_norm_w"], w_out, h0)
    (w_gate, w_up), _ = fetch(["w_gate", "w_up"], h1)
    gt, up, act, u2 = gate_up(h1, p["norm2_w"], w_gate, w_up)
    (w_down,), _ = fetch(["w_down"], act)
    dh2, dh2_b, loss, d_fnw = down_loss(act, w_down, h1, target, p["final_norm_w"])

    dgt, dup, g_down, g_gate, g_up = swiglu_bwd(dh2_b, w_down, gt, up, act, u2)
    dh1, dh1_b, d_n2 = gate_up_bwd(dgt, dup, w_gate, w_up, h1, p["norm2_w"], dh2)
    sent = send({"w_down": g_down, "w_gate": g_gate, "w_up": g_up, "w_out": weight_grad("dw_out", cat, dh1_b)})
    dyn, dh_out, dg_b, d_lnw = out_proj_bwd(dh1_b, w_out, proj, hseq, p["lru_norm_w"], sent)

    dxl_b, d_lcw, d_lcb, dwa2, d_ba, dwx2, d_bx, d_lam = lru_bwd(dh_out, a, hseq, proj, *lru)
    dz_b, dxbc_b, ddt_b, dpar, d_snw, d_scw, d_scb = ssd_bwd(dyn, proj, dt_raw, ssd_cw, p["ssd_conv_b"], y_pre, h_prev, bias2, alog2, d2,
                                                             p["ssd_norm_w"], sent)
    sent = send({"w_in": in_weight_grad([dz_b, dxbc_b, dg_b, dxl_b], [0, SSD_W, 2576, 2576 + LRU_W], ddt_b, u1)})
    grad_x, d_meta, d_n1 = in_proj_bwd(dz_b, dg_b, dxl_b, dxbc_b, ddt_b, w_in, w_dt, h0, p["norm1_w"], dh1, sent)
    small = {"norm1_w": d_n1, "ssd_conv_b": d_scb, "ssd_dt_bias": dpar[:, 0, :8].reshape(1, 16), "ssd_a_log": dpar[:, 1, :8].reshape(1, 16),
             "ssd_d": dpar[:, 2, :8].reshape(1, 16), "ssd_norm_w": d_snw, "lru_conv_b": d_lcb, "lru_ba": d_ba, "lru_bx": d_bx,
             "lru_lambda": d_lam, "lru_norm_w": d_lnw, "norm2_w": d_n2, "final_norm_w": d_fnw,
             "lru_wa": _unpair_blocks(dwa2), "lru_wx": _unpair_blocks(dwx2), "meta_tokens": d_meta,
             "ssd_conv_w": d_scw, "lru_conv_w": d_lcw}
    return loss, grad_x, small


def _pack_small(small, loss):
    rows = [_pad_cols(small[name], -(-n // 1024) * 1024).reshape(-1, 1024) for name, n in SIMPLE]
    rows += [small["lru_wa"].reshape(64, 1024), small["lru_wx"].reshape(64, 1024), small["meta_tokens"],
             _pad_cols(small["ssd_conv_w"], 2048).reshape(8, 1024), small["lru_conv_w"], _pad_cols(loss[:, 0:1], 1024)]
    sm = jnp.concatenate(rows, axis=0)
    return jnp.pad(sm, ((0, SM_ROWS - sm.shape[0]), (0, 0)))


def _slabs(g):
    return g.reshape(8, g.shape[0] // 8, g.shape[1])


def _unslab(g):
    return g.reshape(8 * g.shape[1], g.shape[2])


def kernel(x, meta_tokens, norm1_w, w_in, ssd_conv_w, ssd_conv_b, ssd_dt_bias, ssd_a_log, ssd_d, ssd_norm_w, lru_conv_w, lru_conv_b, lru_wa, lru_ba, lru_wx, lru_bx, lru_lambda, lru_norm_w, w_out, norm2_w, w_gate, w_up, w_down, final_norm_w, loss_target, m_meta_tokens, m_norm1_w, m_w_in, m_ssd_conv_w, m_ssd_conv_b, m_ssd_dt_bias, m_ssd_a_log, m_ssd_d, m_ssd_norm_w, m_lru_conv_w, m_lru_conv_b, m_lru_wa, m_lru_ba, m_lru_wx, m_lru_bx, m_lru_lambda, m_lru_norm_w, m_w_out, m_norm2_w, m_w_gate, m_w_up, m_w_down, m_final_norm_w, v_meta_tokens, v_norm1_w, v_w_in, v_ssd_conv_w, v_ssd_conv_b, v_ssd_dt_bias, v_ssd_a_log, v_ssd_d, v_ssd_norm_w, v_lru_conv_w, v_lru_conv_b, v_lru_wa, v_lru_ba, v_lru_wx, v_lru_bx, v_lru_lambda, v_lru_norm_w, v_w_out, v_norm2_w, v_w_gate, v_w_up, v_w_down, v_final_norm_w):
    w = dict(meta_tokens=meta_tokens, norm1_w=norm1_w, w_in=w_in[0], ssd_conv_w=ssd_conv_w[0], ssd_conv_b=ssd_conv_b, ssd_dt_bias=ssd_dt_bias,
             ssd_a_log=ssd_a_log, ssd_d=ssd_d, ssd_norm_w=ssd_norm_w, lru_conv_w=lru_conv_w[0], lru_conv_b=lru_conv_b, lru_wa=lru_wa[0],
             lru_ba=lru_ba, lru_wx=lru_wx[0], lru_bx=lru_bx, lru_lambda=lru_lambda, lru_norm_w=lru_norm_w, w_out=w_out[0], norm2_w=norm2_w,
             w_gate=w_gate[0], w_up=w_up[0], w_down=w_down[0], final_norm_w=final_norm_w.reshape(1, D))
    m = dict(meta_tokens=m_meta_tokens, norm1_w=m_norm1_w, w_in=m_w_in[0], ssd_conv_w=m_ssd_conv_w[0], ssd_conv_b=m_ssd_conv_b,
             ssd_dt_bias=m_ssd_dt_bias, ssd_a_log=m_ssd_a_log, ssd_d=m_ssd_d, ssd_norm_w=m_ssd_norm_w, lru_conv_w=m_lru_conv_w[0],
             lru_conv_b=m_lru_conv_b, lru_wa=m_lru_wa[0], lru_ba=m_lru_ba, lru_wx=m_lru_wx[0], lru_bx=m_lru_bx, lru_lambda=m_lru_lambda,
             lru_norm_w=m_lru_norm_w, w_out=m_w_out[0], norm2_w=m_norm2_w, w_gate=m_w_gate[0], w_up=m_w_up[0], w_down=m_w_down[0],
             final_norm_w=m_final_norm_w.reshape(1, D))
    v = dict(meta_tokens=v_meta_tokens, norm1_w=v_norm1_w, w_in=v_w_in[0], ssd_conv_w=v_ssd_conv_w[0], ssd_conv_b=v_ssd_conv_b,
             ssd_dt_bias=v_ssd_dt_bias, ssd_a_log=v_ssd_a_log, ssd_d=v_ssd_d, ssd_norm_w=v_ssd_norm_w, lru_conv_w=v_lru_conv_w[0],
             lru_conv_b=v_lru_conv_b, lru_wa=v_lru_wa[0], lru_ba=v_lru_ba, lru_wx=v_lru_wx[0], lru_bx=v_lru_bx, lru_lambda=v_lru_lambda,
             lru_norm_w=v_lru_norm_w, w_out=v_w_out[0], norm2_w=v_norm2_w, w_gate=v_w_gate[0], w_up=v_w_up[0], w_down=v_w_down[0],
             final_norm_w=v_final_norm_w.reshape(1, D))
    shapes = dict(meta_tokens=meta_tokens.shape, norm1_w=norm1_w.shape, w_in=w_in.shape, ssd_conv_w=ssd_conv_w.shape,
                  ssd_conv_b=ssd_conv_b.shape, ssd_dt_bias=ssd_dt_bias.shape, ssd_a_log=ssd_a_log.shape, ssd_d=ssd_d.shape,
                  ssd_norm_w=ssd_norm_w.shape, lru_conv_w=lru_conv_w.shape, lru_conv_b=lru_conv_b.shape, lru_wa=lru_wa.shape,
                  lru_ba=lru_ba.shape, lru_wx=lru_wx.shape, lru_bx=lru_bx.shape, lru_lambda=lru_lambda.shape, lru_norm_w=lru_norm_w.shape,
                  w_out=w_out.shape, norm2_w=norm2_w.shape, w_gate=w_gate.shape, w_up=w_up.shape, w_down=w_down.shape,
                  final_norm_w=final_norm_w.shape)
    me = _index(*_place())
    for n in COLUMN_SHARDED:
        w[n], m[n], v[n] = w[n].T, m[n].T, v[n].T

    small_shard = jnp.concatenate([w["meta_tokens"], _pad_cols(w["ssd_conv_w"], 256).reshape(8, 128), w["lru_conv_w"],
                                   jnp.zeros((4, 128), F32)], axis=0)
    g_in, gs = all_gather("gather_w_in", [w["w_in"].astype(BF), small_shard])
    later = ["w_out", "w_gate", "w_up", "w_down"]
    started, (g_in, gs) = gather_start("gather_rest_start", [w[n].astype(BF) for n in later], [g_in, gs])
    started = dict(zip(later, started))
    ssd_cw = gs[:, 16:24].reshape(8, 4, 256)[:, :, :192].transpose(1, 0, 2).reshape(4, XBC)
    lru_cw = gs[:, 24:28].transpose(1, 0, 2).reshape(4, LRU_W)

    def fetch(names, after):
        pass_on = {"w_out": ["w_down"], "w_gate": [], "w_down": []}[names[0]] if names else ["w_out", "w_gate", "w_up"]
        got, zero = gather_stage("gather_" + (names[0] + "_wait" if names else "pass_on"), [started[n] for n in pass_on],
                                 [started[n] for n in names], after)
        return [_unslab(g) for g in got], zero

    in_flight = {}

    def send(grads):
        names = list(grads)
        st, zero = copies_start("grads_" + names[0] + "_start", [grads[n] if n == "small" else _slabs(grads[n]) for n in names])
        in_flight.update(zip(names, st))
        return zero

    loss, grad_x, small = local_step(x[0], loss_target[0], gs, ssd_cw, lru_cw, g_in, fetch, send, w)
    send({"small": _pack_small(small, loss).reshape(8, SM_ROWS // 8, 1024)})

    out = {}
    early = ["w_down", "w_gate", "w_up", "w_out"]
    recv = dict(zip(early, copies_wait("grads_early_wait", [in_flight[n] for n in early], [in_flight["small"][2]])))
    for pair in (early[:2], early[2:]):
        done = adamw_shards("adamw_" + pair[0], [recv[n] for n in pair], [w[n] for n in pair], [m[n] for n in pair], [v[n] for n in pair])
        out.update(zip(pair, done))
    recv_in, recv_small = copies_wait("grads_late_wait", [in_flight["w_in"], in_flight["small"]], [out[n][0] for n in early])
    def lines(a):
        return jnp.transpose(a.reshape(D // 128, 128, IN_COLS // 8), (2, 0, 1))

    gathering, zero = copies_start("gather_small_start", [sum_slabs(recv_small)])
    updated = adamw_w_in(recv_in, lines(w_in), lines(m_w_in), lines(v_w_in), zero)
    out["w_in"] = [jnp.transpose(o, (1, 2, 0)).reshape(D, IN_COLS // 8) for o in updated]
    for n in ("w_gate", "w_up"):
        out[n] = [o.T for o in out[n]]
    sm = copies_wait("gather_small_wait", gathering, [updated[0]])[0].reshape(SM_ROWS, 1024)
    special_g =[sm[SM_WA:SM_WA + 64].reshape(16, 64, 64), sm[SM_WX:SM_WX + 64].reshape(16, 64, 64),
                 lax.dynamic_slice(sm[SM_META:SM_META + 16], (0, 128 * me), (16, 128)),
                 lax.dynamic_slice(sm[SM_SCW:SM_SCW + 8].reshape(4, 2048), (0, 192 * me), (4, 192)),
                 lax.dynamic_slice(sm[SM_LCW:SM_LCW + 4], (0, 128 * me), (4, 128))]
    names = [n for n, _ in SIMPLE] + SPECIAL
    res = adamw_small(sm, special_g, [w[n] for n in names], [m[n] for n in names], [v[n] for n in names])
    for k, (n, _) in enumerate(SIMPLE):
        out[n] = res[4 * k:4 * k + 4]
    for k, n in enumerate(SPECIAL):
        o = 4 * len(SIMPLE) + 3 * k
        out[n] = [special_g[k]] + list(res[o:o + 3])
    loss_total = sm[SM_LOSS, 0]
    flat = [loss_total, grad_x[None]]
    for k in range(4):
        flat += [out[n][k].reshape(shapes[n]) for n in WEIGHTS]
    return tuple(flat)
```

```python
import math

import jax
import jax.numpy as jnp
from jax import lax
from jax.experimental import pallas as pl
from jax.experimental.pallas import tpu as pltpu

F32 = jnp.float32
BF = jnp.bfloat16

D = 1024
SEQ = 2048
N_META = 16
Q = 128
NPAD = 112
T = NPAD + N_META + SEQ
NCH = T // Q
RC = 544
D_FF = 2816
SSD_W = 1024
LRU_W = 1024
XBC = 1536
IN_COLS = 4624
PZ, PG, PXL, PXBC = 0, 1024, 2048, 3072
NP_IN = 4608
EPS = 1e-6
LRU_C = 8.0
VMEM_LIMIT = 56 * 1024 * 1024

ADAM_LR, ADAM_B1, ADAM_B2, ADAM_EPS, ADAM_WD, ADAM_STEP = 0.001, 0.9, 0.999, 1e-08, 0.01, 10

NT_DIMS = (((1,), (1,)), ((), ()))
TN_DIMS = (((0,), (0,)), ((), ()))
MESH = pl.DeviceIdType.MESH


def _params(n_grid=1, limit=VMEM_LIMIT):
    return pltpu.CompilerParams(dimension_semantics=("arbitrary",) * n_grid, vmem_limit_bytes=limit)


def _spec(shape, imap, single=False):
    if single:
        return pl.BlockSpec(shape, imap, pipeline_mode=pl.Buffered(1))
    return pl.BlockSpec(shape, imap)


def _sigmoid(x):
    return 0.5 * jnp.tanh(0.5 * x) + 0.5


def _sigmoid_gate(x):
    return 1.0 / (1.0 + jnp.exp(-x))


def _softplus(x):
    return jnp.maximum(x, 0.0) + jnp.log(1.0 + jnp.exp(-jnp.abs(x)))


def _rms_stats(h):
    return lax.rsqrt(jnp.mean(h * h, axis=-1, keepdims=True) + EPS)


def _rms(h, w):
    return (h * _rms_stats(h)) * w


def _rms_bwd(du, h, w):
    r = _rms_stats(h)
    n = h * r
    dn = du * w
    dh = r * (dn - n * jnp.mean(dn * n, axis=-1, keepdims=True))
    return dh, du * n


_G0 = math.sqrt(2.0 / math.pi)


def _gelu(x):
    return 0.5 * x * (1.0 + jnp.tanh(_G0 * (x + 0.044715 * (x * x * x))))


def _gelu_grad(x):
    t = jnp.tanh(_G0 * (x + 0.044715 * (x * x * x)))
    return 0.5 * (1.0 + t) + 0.5 * x * (1.0 - t * t) * (_G0 * (1.0 + 3.0 * 0.044715 * (x * x)))


def _rows(shape, r0=0):
    return lax.broadcasted_iota(jnp.int32, shape, 0) + r0


def _lanes(shape):
    return lax.broadcasted_iota(jnp.int32, shape, 1)


HALO = 8


def _fill_padded(pad_ref, x_ref):
    pad_ref[0:HALO, :] = jnp.zeros((HALO, pad_ref.shape[1]), F32)
    pad_ref[T + HALO:T + 2 * HALO, :] = jnp.zeros((HALO, pad_ref.shape[1]), F32)

    def step(c, carry):
        r0 = pl.multiple_of(c * Q, Q)
        pad_ref[pl.ds(r0 + HALO, Q), :] = x_ref[pl.ds(r0, Q), :].astype(F32)
        return carry

    lax.fori_loop(0, NCH, step, 0)


def _back(pad_ref, r0):
    win = pad_ref[pl.ds(r0, Q + HALO), :]
    return lambda s: win[HALO:, :] if s == 0 else pltpu.roll(win, s, axis=0)[HALO:, :]


def _ahead(pad_ref, r0):
    win = pad_ref[pl.ds(r0 + HALO, Q + HALO), :]
    return lambda s: win[:Q, :] if s == 0 else pltpu.roll(win, Q + HALO - s, axis=0)[:Q, :]


def _conv(back, w, b):
    y = b + w[3:4, :] * back(0)
    for k in range(3):
        y = y + w[k:k + 1, :] * back(3 - k)
    return y


def _conv_bwd_x(ahead, w):
    dx = w[3:4, :] * ahead(0)
    for k in range(3):
        dx = dx + w[k:k + 1, :] * ahead(3 - k)
    return dx


def _conv_bwd_w(dy, back):
    dws = [jnp.sum(dy * back(3 - k), axis=0, keepdims=True) for k in range(4)]
    return jnp.concatenate(dws, axis=0), jnp.sum(dy, axis=0, keepdims=True)


def _chunks(fn, unrolled=False):
    if unrolled:
        for c in range(NCH):
            fn(c * Q)
        return

    def step(c, carry):
        fn(pl.multiple_of(c * Q, Q))
        return carry

    lax.fori_loop(0, NCH, step, 0)


HALF = RC // 2


def _col_tiles(n, tn, fn):
    def step(j, carry):
        fn(pl.multiple_of(j * tn, tn))
        return carry

    lax.fori_loop(0, n // tn, step, 0)


def _rows_spec(cols, block_col=0):
    return _spec((RC, cols), lambda i: (i, block_col))


def _whole(shape):
    return _spec(shape, lambda i: tuple(0 for _ in shape), single=True)


def _vec(cols):
    return _spec((1, cols), lambda i: (0, 0))


def _zero_at_first(*refs):
    @pl.when(pl.program_id(0) == 0)
    def _():
        for r in refs:
            r[...] = jnp.zeros_like(r)


ANY_SPEC = pl.BlockSpec(memory_space=pl.ANY)


def _arriving(src, dst, sems, starts, rows):
    n, ahead = len(starts), 2
    first = pl.program_id(0) == 0

    def piece(k):
        r0 = starts[0]
        for j in range(1, n):
            r0 = jnp.where(k == j, starts[j], r0)
        at = pl.ds(pl.multiple_of(r0, 16), rows)
        return pltpu.make_async_copy(src.at[at], dst.at[at], sems.at[k])

    @pl.when(first)
    def _():
        for k in range(min(ahead, n)):
            piece(k).start()

    def ready(k):
        k = jnp.asarray(k, jnp.int32)

        @pl.when(first)
        def _():
            piece(k).wait()

            @pl.when(k + ahead < n)
            def _():
                piece(k + ahead).start()

    return ready


IN_RUNS = ((PZ, 0, 1024), (PXBC, 1024, XBC), (PG, 2576, 2048))
IN_TILE = 512
IN_TILE_ROWS = [wrow + IN_TILE * j for _, wrow, width in IN_RUNS for j in range(width // IN_TILE)]


def _in_tiles(fn, before_run=lambda run: None):
    done = 0
    for run, (pcol, wrow, width) in enumerate(IN_RUNS):
        before_run(run)
        def step(j, carry, pcol=pcol, wrow=wrow, done=done):
            fn(pl.multiple_of(pcol + j * IN_TILE, IN_TILE), pl.multiple_of(wrow + j * IN_TILE, 16), done + j)
            return carry

        lax.fori_loop(0, width // IN_TILE, step, 0)
        done += width // IN_TILE


def in_proj(x, meta, wn, w_shards):
    first = NPAD + N_META
    steps = T // RC
    shard = IN_COLS // 8

    def body(x_hbm, meta_ref, wn_ref, g_hbm, o_ref, dt_ref, u_ref, h_ref, wt_hbm, wdt_ref, raw, w_ref, h_scr, g_sems, h_sems, out_sem):
        i = pl.program_id(0)
        slot = i % 2
        shards = [pltpu.make_async_copy(g_hbm.at[j], raw.at[j], g_sems.at[j]) for j in range(8)]
        head = pltpu.make_async_copy(x_hbm.at[pl.ds(0, RC - first)], h_scr.at[0, pl.ds(first, RC - first)], h_sems.at[0])
        put_back = pltpu.make_async_copy(w_ref, wt_hbm, out_sem)

        def rows_of(step):
            return pltpu.make_async_copy(x_hbm.at[pl.ds(pl.multiple_of(step * RC - first, 32), RC)], h_scr.at[step % 2], h_sems.at[step % 2])

        @pl.when(i == 0)
        def _():
            for cp in shards:
                cp.start()
            head.start()
            h_scr[0, 0:NPAD, :] = jnp.zeros((NPAD, D), F32)
            for j in range(8):
                h_scr[0, NPAD:first, 128 * j:128 * j + 128] = meta_ref[j, 0:N_META, :]

        @pl.when(i + 1 < steps)
        def _():
            rows_of(i + 1).start()

        @pl.when(i == 0)
        def _():
            head.wait()

        @pl.when(i > 0)
        def _():
            rows_of(i).wait()

        h_ref[...] = h_scr[slot]
        for r in (0, HALF):
            u_ref[r:r + HALF, :] = _rms(h_scr[slot, r:r + HALF, :], wn_ref[...]).astype(BF)

        def place_shards(run):
            @pl.when(i == 0)
            def _():
                for j in ((0, 1), (2, 3, 4), (5, 6, 7))[run]:
                    shards[j].wait()
                    w_ref[shard * j:shard * (j + 1), :] = raw[j]
                if run == 1:
                    wdt_ref[...] = jnp.zeros_like(wdt_ref)
                    for g in range(2):
                        wdt_ref[128 * g:128 * g + 8, :] = w_ref[2560 + 8 * g:2568 + 8 * g, :]
                if run == 2:
                    put_back.start()

        def tile(pcol, wrow, k):
            o_ref[:, pl.ds(pcol, IN_TILE)] = lax.dot_general(u_ref[...], w_ref[pl.ds(wrow, IN_TILE), :], NT_DIMS,
                                                             preferred_element_type=F32).astype(BF)

        _in_tiles(tile, place_shards)
        dt_ref[...] = lax.dot_general(u_ref[...], wdt_ref[...], NT_DIMS, preferred_element_type=F32)

        @pl.when(i == steps - 1)
        def _():
            put_back.wait()

    return pl.pallas_call(
        body, grid=(steps,), in_specs=[ANY_SPEC, _spec(meta.shape, lambda i: (0, 0, 0)), _vec(D), ANY_SPEC],
        out_specs=[_rows_spec(NP_IN), _rows_spec(256), _rows_spec(D), _rows_spec(D), ANY_SPEC, _spec((256, D), lambda i: (0, 0))],
        out_shape=[jax.ShapeDtypeStruct((T, NP_IN), BF), jax.ShapeDtypeStruct((T, 256), F32), jax.ShapeDtypeStruct((T, D), BF),
                   jax.ShapeDtypeStruct((T, D), F32), jax.ShapeDtypeStruct((IN_COLS, D), BF), jax.ShapeDtypeStruct((256, D), BF)],
        scratch_shapes=[pltpu.VMEM((8, shard, D), BF), pltpu.VMEM((IN_COLS, D), BF), pltpu.VMEM((2, RC, D), F32),
                        pltpu.SemaphoreType.DMA((8,)), pltpu.SemaphoreType.DMA((2,)), pltpu.SemaphoreType.DMA],
        compiler_params=_params(), name="in_proj")(x, meta, wn, w_shards)


def out_proj(yn_ssd, proj, hseq, lru_nw, w_out, h0):
    def body(y_ref, g_ref, h_ref, wn_ref, w_ref, r_ref, o_ref, cat_ref):
        cat_ref[:, 0:SSD_W] = y_ref[...]
        for r in (0, HALF):
            y = _gelu(g_ref[r:r + HALF, :].astype(F32)) * h_ref[r:r + HALF, :]
            cat_ref[r:r + HALF, SSD_W:] = _rms(y, wn_ref[...]).astype(BF)

        def tile(c0):
            o_ref[:, pl.ds(c0, 512)] = r_ref[:, pl.ds(c0, 512)] + jnp.dot(cat_ref[...], w_ref[:, pl.ds(c0, 512)], preferred_element_type=F32)

        _col_tiles(D, 512, tile)

    return pl.pallas_call(
        body, grid=(T // RC,),
        in_specs=[_rows_spec(SSD_W), _rows_spec(LRU_W, PG // LRU_W), _rows_spec(LRU_W), _vec(LRU_W), _whole((SSD_W + LRU_W, D)), _rows_spec(D)],
        out_specs=[_rows_spec(D), _rows_spec(SSD_W + LRU_W)],
        out_shape=[jax.ShapeDtypeStruct((T, D), F32), jax.ShapeDtypeStruct((T, SSD_W + LRU_W), BF)],
        compiler_params=_params(), name="out_proj")(yn_ssd, proj, hseq, lru_nw, w_out, h0)


def out_proj_bwd(dh1_b, w_out, proj, hseq, lru_nw, after):
    def body(d_ref, w_ref, g_ref, h_ref, wn_ref, _after, dy_ref, dh_ref, dg_ref, dw_ref, dl_scr):
        _zero_at_first(dw_ref)

        def tile(c0):
            dy_ref[:, pl.ds(c0, 512)] = lax.dot_general(d_ref[...], w_ref[pl.ds(c0, 512), :], NT_DIMS, preferred_element_type=F32)
            dl_scr[:, pl.ds(c0, 512)] = lax.dot_general(d_ref[...], w_ref[pl.ds(SSD_W + c0, 512), :], NT_DIMS, preferred_element_type=F32)

        _col_tiles(SSD_W, 512, tile)

        for r in (0, HALF):
            g = g_ref[r:r + HALF, :].astype(F32)
            h = h_ref[r:r + HALF, :]
            ge = _gelu(g)
            dy, dw = _rms_bwd(dl_scr[r:r + HALF, :], ge * h, wn_ref[...])
            dw_ref[...] += jnp.sum(dw, axis=0, keepdims=True)
            dh_ref[r:r + HALF, :] = dy * ge
            dg_ref[r:r + HALF, :] = (dy * h * _gelu_grad(g)).astype(BF)

    return pl.pallas_call(
        body, grid=(T // RC,),
        in_specs=[_rows_spec(D), _whole((SSD_W + LRU_W, D)), _rows_spec(LRU_W, PG // LRU_W), _rows_spec(LRU_W), _vec(LRU_W), ANY_SPEC],
        out_specs=[_rows_spec(SSD_W), _rows_spec(LRU_W), _rows_spec(LRU_W), _vec(LRU_W)],
        out_shape=[jax.ShapeDtypeStruct((T, SSD_W), F32), jax.ShapeDtypeStruct((T, LRU_W), F32), jax.ShapeDtypeStruct((T, LRU_W), BF),
                   jax.ShapeDtypeStruct((1, LRU_W), F32)],
        scratch_shapes=[pltpu.VMEM((RC, LRU_W), F32)],
        compiler_params=_params(), name="out_proj_bwd")(dh1_b, w_out, proj, hseq, lru_nw, after)


def in_proj_bwd(dz, dg, dxl, dxbc, ddt, w_t, w_dt, h0, wn, dh1, after, through):
    first = NPAD + N_META

    def body(dz_ref, dg_ref, dxl_ref, dxbc_ref, ddt_ref, w_hbm, wdt_ref, h_ref, wn_ref, r_ref, _after, _in0, _in1, gx_hbm, meta_ref, dw_ref,
             _out0, _out1, du_scr, o_ref, sem, w_ref, w_sems):
        i = pl.program_id(0)
        ready = _arriving(w_hbm, w_ref, w_sems, IN_TILE_ROWS, IN_TILE)
        _zero_at_first(dw_ref)
        du_scr[...] = jnp.dot(ddt_ref[...], wdt_ref[...], preferred_element_type=F32)
        done = 0
        for d_ref, wrow, width in ((dz_ref, 0, 1024), (dxbc_ref, 1024, XBC), (dg_ref, 2576, 1024), (dxl_ref, 3600, 1024)):
            def step(j, carry, d_ref=d_ref, wrow=wrow, done=done):
                c0 = pl.multiple_of(j * IN_TILE, IN_TILE)
                ready(done + j)
                du_scr[...] += jnp.dot(d_ref[:, pl.ds(c0, IN_TILE)], w_ref[pl.ds(pl.multiple_of(wrow + c0, 16), IN_TILE), :],
                                       preferred_element_type=F32)
                return carry

            lax.fori_loop(0, width // IN_TILE, step, 0)
            done += width // IN_TILE
        for r in (0, HALF):
            dh, dw = _rms_bwd(du_scr[r:r + HALF, :], h_ref[r:r + HALF, :], wn_ref[...])
            dw_ref[...] += jnp.sum(dw, axis=0, keepdims=True)
            o_ref[r:r + HALF, :] = dh + r_ref[r:r + HALF, :]

        @pl.when(i == 0)
        def _():
            meta_ref[...] = o_ref[NPAD:first, :]
            head = pltpu.make_async_copy(o_ref.at[pl.ds(first, RC - first)], gx_hbm.at[pl.ds(0, RC - first)], sem)
            head.start()
            head.wait()

        @pl.when(i > 0)
        def _():
            rest = pltpu.make_async_copy(o_ref, gx_hbm.at[pl.ds(pl.multiple_of(i * RC - first, 32), RC)], sem)
            rest.start()
            rest.wait()

    return pl.pallas_call(
        body, grid=(T // RC,),
        in_specs=[_rows_spec(SSD_W), _rows_spec(LRU_W), _rows_spec(LRU_W), _rows_spec(XBC), _rows_spec(256), ANY_SPEC,
                  _whole((256, D)), _rows_spec(D), _vec(D), _rows_spec(D), ANY_SPEC, ANY_SPEC, ANY_SPEC],
        out_specs=[ANY_SPEC, _spec((N_META, D), lambda i: (0, 0)), _vec(D), ANY_SPEC, ANY_SPEC],
        out_shape=[jax.ShapeDtypeStruct((SEQ, D), F32), jax.ShapeDtypeStruct((N_META, D), F32), jax.ShapeDtypeStruct((1, D), F32)]
        + [jax.ShapeDtypeStruct(t.shape, t.dtype) for t in through],
        scratch_shapes=[pltpu.VMEM((RC, D), F32), pltpu.VMEM((RC, D), F32), pltpu.SemaphoreType.DMA,
                        pltpu.VMEM((IN_COLS, D), BF), pltpu.SemaphoreType.DMA((len(IN_TILE_ROWS),))],
        input_output_aliases={11: 3, 12: 4},
        compiler_params=_params(), name="in_proj_bwd")(dz, dg, dxl, dxbc, ddt, w_t, w_dt, h0, wn, dh1, after, *through)


GRAD_TILE = 256


def weight_grad(name, a, u1):
    tm = GRAD_TILE

    def body(a_ref, u_ref, o_ref):
        o_ref[...] = lax.dot_general(a_ref[...], u_ref[...], TN_DIMS, preferred_element_type=F32).astype(BF)

    return pl.pallas_call(
        body, grid=(a.shape[1] // tm,),
        in_specs=[_spec((T, tm), lambda j: (0, j)), _spec((T, D), lambda j: (0, 0), single=True)],
        out_specs=_spec((tm, D), lambda j: (j, 0)),
        out_shape=jax.ShapeDtypeStruct((a.shape[1], D), BF),
        compiler_params=_params(), name=name)(a, u1)


def in_weight_grad(parts, first_rows, ddt, u1):
    tm = GRAD_TILE
    per_row = D // 128
    dt_row, dt_lines = 2560, 8 * per_row
    parts = list(parts) + [ddt]
    first_rows = list(first_rows) + [dt_row]
    tiles = [p.shape[1] // tm for p in parts]
    starts = [sum(tiles[:k]) for k in range(len(parts))]
    last = sum(tiles) - 1

    def body(*refs):
        a_refs, u_ref = refs[:len(parts)], refs[len(parts)]
        o_hbm, mix_scr, stage, sems = refs[len(parts) + 1:]
        step = pl.program_id(0)
        slot = step % 2
        line0 = 0
        for a_ref, start, n, first in zip(a_refs, starts, tiles, first_rows):
            here = (step >= start) & (step < start + n)
            line0 = jnp.where(here, per_row * (first + tm * (step - start)), line0)

            @pl.when(here)
            def _(a_ref=a_ref):
                res = lax.dot_general(a_ref[...], u_ref[...], TN_DIMS, preferred_element_type=F32)
                for q in range(per_row):
                    mix_scr[pl.ds(q, tm, stride=per_row), :] = res[:, 128 * q:128 * q + 128]

        def tile_copy(of_slot, to):
            return pltpu.make_async_copy(stage.at[of_slot], o_hbm.at[pl.ds(to, per_row * tm)], sems.at[of_slot])

        @pl.when(step >= 2)
        def _():
            tile_copy(slot, 0).wait()

        stage[slot] = mix_scr[...].astype(BF)

        @pl.when(step < last)
        def _():
            tile_copy(slot, pl.multiple_of(line0, 128)).start()

        @pl.when(step == last)
        def _():
            halves = [pltpu.make_async_copy(stage.at[slot, pl.ds(128 * per_row * k, dt_lines)],
                                            o_hbm.at[pl.ds(per_row * (dt_row + 8 * k), dt_lines)], sems.at[2 + k]) for k in range(2)]
            for cp in halves:
                cp.start()
            tile_copy(1 - slot, 0).wait()
            for cp in halves:
                cp.wait()

    def tile_of(start, n):
        return lambda j: (0, jnp.clip(j - start, 0, n - 1))

    return pl.pallas_call(
        body, grid=(last + 1,),
        in_specs=[_spec((T, tm), tile_of(s, n)) for s, n in zip(starts, tiles)] + [_spec((T, D), lambda j: (0, 0), single=True)],
        out_specs=ANY_SPEC,
        out_shape=jax.ShapeDtypeStruct((per_row * IN_COLS, 128), BF),
        scratch_shapes=[pltpu.VMEM((per_row * tm, 128), F32), pltpu.VMEM((2, per_row * tm, 128), BF), pltpu.SemaphoreType.DMA((4,))],
        compiler_params=_params(), name="dw_in")(*parts, u1)


def _ssd_chunk_common(row0, dt_ref, b_ref, c_ref, bias, a_neg):
    shape = (Q, Q)
    lane = _lanes(shape)
    sub = _rows(shape)
    live = (_rows(shape, row0) >= NPAD) & (lane < 8)
    dtr = dt_ref[:, :]
    dt = jnp.where(live, _softplus(dtr + bias), 0.0)
    d_a = dt * a_neg
    tri = (sub >= lane).astype(F32)
    cs = jnp.dot(tri, d_a, precision=lax.Precision.HIGHEST, preferred_element_type=F32)
    cs_t = cs.T
    b_f = b_ref[:, :]
    bc = b_f.astype(BF)
    cc = c_ref[:, :].astype(BF)
    cb = lax.dot_general(cc, bc, NT_DIMS, preferred_element_type=F32)
    cs_last = cs[Q - 1:Q, :]
    return dict(lane=lane, sub=sub, live=live, dtr=dtr, dt=dt, cs=cs, cs_t=cs_t, bc=bc, cc=cc, cb=cb, bc_t=b_f.T.astype(BF),
                ecs=jnp.exp(cs), dsm=jnp.exp(cs_last - cs), gam=jnp.exp(cs_last))


def _pair(lane_even, mat, j):
    return jnp.where(lane_even, mat[:, j:j + 1], mat[:, j + 1:j + 2])


def _pair_row(lane_even, mat, j):
    return jnp.where(lane_even[0:1, :], mat[:, j:j + 1], mat[:, j + 1:j + 2])


def _head_decay(cm, j):
    seg = cm["cs"][:, j:j + 1] - cm["cs_t"][j:j + 1, :]
    return jnp.exp(jnp.where(cm["sub"] >= cm["lane"], seg, -jnp.inf))


def _head_decay_t(cm, j):
    seg = cm["cs_t"][j:j + 1, :] - cm["cs"][:, j:j + 1]
    return jnp.exp(jnp.where(cm["lane"] >= cm["sub"], seg, -jnp.inf))


def _conv_window(raw_ref, halo_ref, pad_scr):
    pad_scr[0:HALO, :] = halo_ref[...].astype(F32)[halo_ref.shape[0] - HALO:, :]
    pad_scr[HALO:HALO + Q, :] = raw_ref[...].astype(F32)
    win = pad_scr[...]
    return lambda s: win[HALO:, :] if s == 0 else pltpu.roll(win, s, axis=0)[HALO:, :]


def _xbc_cols(g):
    return slice(512 * g, 512 * g + 512), slice(SSD_W + 128 * g, SSD_W + 128 * g + 128), slice(SSD_W + 256 + 128 * g, SSD_W + 384 + 128 * g)


def ssd_fwd(proj, dt_raw, conv_w, conv_b, dt_bias2, a_log2, d2, norm_w):
    def body(raw_ref, halo_ref, dt_all, z_all, cw_ref, cb_ref, bias_all, alog_all, d_all, nw_all, yn_all, y_all, hp_all,
             h_all, pad_scr, act_scr):
        @pl.when(pl.program_id(0) == 0)
        def _():
            h_all[...] = jnp.zeros_like(h_all)

        pre = _conv(_conv_window(raw_ref, halo_ref, pad_scr), cw_ref[...], cb_ref[...])
        act_scr[...] = pre * _sigmoid(pre)
        for g in range(2):
            wide, thin = slice(512 * g, 512 * g + 512), slice(128 * g, 128 * g + 128)
            xs, bs, cs = _xbc_cols(g)
            group(act_scr.at[:, xs], act_scr.at[:, bs], act_scr.at[:, cs], dt_all.at[:, thin], z_all.at[:, wide], bias_all.at[g],
                  alog_all.at[g], d_all.at[g], nw_all.at[:, wide], yn_all.at[:, wide], y_all.at[:, wide], hp_all.at[g, 0], h_all.at[g])

    def group(x_ref, b_ref, c_ref, dt_ref, z_ref, bias_ref, alog_ref, d_ref, nw_ref, yn_ref, y_ref, hp_ref, h_scr):
        bias = bias_ref[...]
        a_neg = -jnp.exp(alog_ref[...])
        dsk = d_ref[...]
        cm = _ssd_chunk_common(pl.program_id(0) * Q, dt_ref, b_ref, c_ref, bias, a_neg)
        lane_even = cm["lane"] < 64
        for p in range(4):
            je, jo = 2 * p, 2 * p + 1
            xp = x_ref[:, 128 * p:128 * p + 128]
            xdt = xp * _pair(lane_even, cm["dt"], je)
            xdt_b = xdt.astype(BF)
            m_e = (cm["cb"] * _head_decay(cm, je)).astype(BF)
            m_o = (cm["cb"] * _head_decay(cm, jo)).astype(BF)
            zero = jnp.zeros_like(xdt_b)
            yd = (jnp.dot(m_e, jnp.where(lane_even, xdt_b, zero), preferred_element_type=F32)
                  + jnp.dot(m_o, jnp.where(lane_even, zero, xdt_b), preferred_element_type=F32))
            hp = h_scr[p]
            hp_ref[p] = hp
            yo = jnp.dot(cm["cc"], hp.astype(BF), preferred_element_type=F32) * _pair(lane_even, cm["ecs"], je)
            y_ref[:, 128 * p:128 * p + 128] = yd + yo + xp * _pair_row(lane_even, dsk, je)
            st = jnp.dot(cm["bc_t"], (xdt * _pair(lane_even, cm["dsm"], je)).astype(BF), preferred_element_type=F32)
            h_scr[p] = hp * _pair_row(lane_even, cm["gam"], je) + st
        zc = z_ref[:, :].astype(F32)
        gated = y_ref[:, :] * (zc * _sigmoid(zc))
        yn_ref[:, :] = _rms(gated, nw_ref[...]).astype(BF)

    par = _spec((2, 1, 128), lambda c: (0, 0, 0))
    wide = _spec((Q, SSD_W), lambda c: (c, 0))
    xbc = PXBC // XBC
    halo = 2 * HALO
    return pl.pallas_call(
        body, grid=(NCH,),
        in_specs=[_spec((Q, XBC), lambda c: (c, xbc)), _spec((halo, XBC), lambda c: (jnp.maximum(c * (Q // halo) - 1, 0), xbc)),
                  _spec((Q, 256), lambda c: (c, 0)), wide, _spec((4, XBC), lambda c: (0, 0)), _spec((1, XBC), lambda c: (0, 0)),
                  par, par, par, _spec((1, SSD_W), lambda c: (0, 0))],
        out_specs=[wide, wide, _spec((2, 1, 4, 128, 128), lambda c: (0, c, 0, 0, 0))],
        out_shape=[jax.ShapeDtypeStruct((T, SSD_W), BF), jax.ShapeDtypeStruct((T, SSD_W), F32),
                   jax.ShapeDtypeStruct((2, NCH, 4, 128, 128), F32)],
        scratch_shapes=[pltpu.VMEM((2, 4, 128, 128), F32), pltpu.VMEM((Q + HALO, XBC), F32), pltpu.VMEM((Q, XBC), F32)],
        compiler_params=_params(), name="ssd_fwd")(proj, proj, dt_raw, proj, conv_w, conv_b, dt_bias2, a_log2, d2, norm_w)


def ssd_bwd(dyn, proj, dt_raw, conv_w, conv_b, y_pre, h_prev, dt_bias2, a_log2, d2, norm_w, after):
    def body(dyn_all, raw_ref, halo_ref, dt_all, z_all, y_all, hp_all, cw_ref, cb_ref, bias_all, alog_all, d_all, nw_all, _after,
             dz_all, dxbc_ref, ddt_all, dpar_all, dnw_all, dcw_ref, dcb_ref, dh_all, acc_all, pad_scr, act_scr, dsilu_scr, dact_scr, dpad_scr):
        @pl.when(pl.program_id(0) == 0)
        def _():
            dh_all[...] = jnp.zeros_like(dh_all)
            acc_all[...] = jnp.zeros_like(acc_all)
            dnw_all[...] = jnp.zeros_like(dnw_all)
            dcw_ref[...] = jnp.zeros_like(dcw_ref)
            dcb_ref[...] = jnp.zeros_like(dcb_ref)
            dpad_scr[Q:Q + HALO, :] = jnp.zeros((HALO, XBC), F32)

        back = _conv_window(raw_ref, halo_ref, pad_scr)
        pre = _conv(back, cw_ref[...], cb_ref[...])
        sg = _sigmoid(pre)
        act_scr[...] = pre * sg
        dsilu_scr[...] = sg * (1.0 + pre * (1.0 - sg))
        for g in range(2):
            wide, thin = slice(512 * g, 512 * g + 512), slice(128 * g, 128 * g + 128)
            xs, bs, cs = _xbc_cols(g)
            group(dyn_all.at[:, wide], act_scr.at[:, xs], act_scr.at[:, bs], act_scr.at[:, cs], dt_all.at[:, thin], z_all.at[:, wide],
                  y_all.at[:, wide], hp_all.at[g, 0], bias_all.at[g], alog_all.at[g], d_all.at[g], nw_all.at[:, wide],
                  dz_all.at[:, wide], dact_scr.at[:, xs], dact_scr.at[:, bs], dact_scr.at[:, cs], ddt_all.at[:, thin], dpar_all.at[g],
                  dnw_all.at[:, wide], dh_all.at[g], acc_all.at[g])
        dpre = dact_scr[...] * dsilu_scr[...]
        dcw, dcb = _conv_bwd_w(dpre, back)
        dcw_ref[...] += dcw
        dcb_ref[...] += dcb
        dpad_scr[0:Q, :] = dpre
        win = dpad_scr[...]
        dxbc_ref[...] = _conv_bwd_x(lambda s: win[:Q, :] if s == 0 else pltpu.roll(win, Q + HALO - s, axis=0)[:Q, :], cw_ref[...]).astype(BF)
        dpad_scr[Q:Q + HALO, :] = dpre[0:HALO, :]

    def group(dyn_ref, x_ref, b_ref, c_ref, dt_ref, z_ref, y_ref, hp_ref, bias_ref, alog_ref, d_ref, nw_ref,
              dz_ref, dx_ref, db_ref, dc_ref, ddt_ref, dpar_ref, dnw_ref, dh_scr, acc_scr):
        ci = pl.program_id(0)
        bias = bias_ref[...]
        a_neg = -jnp.exp(alog_ref[...])
        dsk = d_ref[...]
        cm = _ssd_chunk_common((NCH - 1 - ci) * Q, dt_ref, b_ref, c_ref, bias, a_neg)
        lane, sub = cm["lane"], cm["sub"]
        lane_even = lane < 64
        cc_t = c_ref[:, :].T.astype(BF)
        cb_t = lax.dot_general(cm["bc"], cm["cc"], NT_DIMS, preferred_element_type=F32)
        zc = z_ref[:, :].astype(F32)
        yc = y_ref[:, :]
        sg = _sigmoid(zc)
        sz = zc * sg
        dgated, dnw = _rms_bwd(dyn_ref[:, :], yc * sz, nw_ref[...])
        dnw_ref[...] += jnp.sum(dnw, axis=0, keepdims=True)
        dz_ref[:, :] = (dgated * yc * (sg * (1.0 + zc * (1.0 - sg)))).astype(BF)
        dy_all = dgated * sz
        dcb = jnp.zeros((Q, Q), F32)
        dcb_t = jnp.zeros((Q, Q), F32)
        db_acc = jnp.zeros((Q, Q), F32)
        dc_acc = jnp.zeros((Q, Q), F32)
        dcs = jnp.zeros((Q, Q), F32)
        ddt = jnp.zeros((Q, Q), F32)
        for p in range(4):
            je, jo = 2 * p, 2 * p + 1
            xp = x_ref[:, 128 * p:128 * p + 128]
            dy = dy_all[:, 128 * p:128 * p + 128]
            dt_p = _pair(lane_even, cm["dt"], je)
            xdt = xp * dt_p
            xdt_b = xdt.astype(BF)
            dy_b = dy.astype(BF)
            zero = jnp.zeros_like(dy_b)
            hp = hp_ref[p]
            hp_b = hp.astype(BF)
            dh = dh_scr[p]
            dh_b = dh.astype(BF)
            acc_scr[p:p + 1, :] += jnp.sum(dy * xp, axis=0, keepdims=True)
            dxp = dy * _pair_row(lane_even, dsk, je)
            e_p = _pair(lane_even, cm["ecs"], je)
            g_p = jnp.dot(cm["cc"], hp_b, preferred_element_type=F32)
            dg_b = (dy * e_p).astype(BF)
            de = dy * g_p * e_p
            dc_acc = dc_acc + lax.dot_general(dg_b, hp_b, NT_DIMS, preferred_element_type=F32)
            dh_in = jnp.dot(cc_t, dg_b, preferred_element_type=F32)
            ds_p = _pair(lane_even, cm["dsm"], je)
            r_p = jnp.dot(cm["bc"], dh_b, preferred_element_type=F32)
            dxdt = r_p * ds_p
            tt = r_p * xdt * ds_p
            db_acc = db_acc + lax.dot_general((xdt * ds_p).astype(BF), dh_b, NT_DIMS, preferred_element_type=F32)
            dgam_m = jnp.sum(dh * hp, axis=0, keepdims=True)
            for j, even in ((je, True), (jo, False)):
                sel = lane_even if even else jnp.logical_not(lane_even)
                dy_j = jnp.where(sel, dy_b, zero)
                l_j = _head_decay(cm, j)
                l_jt = _head_decay_t(cm, j)
                m_j = cm["cb"] * l_j
                m_jt = cb_t * l_jt
                dm = lax.dot_general(dy_j, xdt_b, NT_DIMS, preferred_element_type=F32)
                dm_t = lax.dot_general(xdt_b, dy_j, NT_DIMS, preferred_element_type=F32)
                dxdt = dxdt + jnp.dot(m_jt.astype(BF), dy_j, preferred_element_type=F32)
                dcb = dcb + dm * l_j
                dcb_t = dcb_t + dm_t * l_jt
                t_j = jnp.where(sel, tt, 0.0)
                col = jnp.sum(dm * m_j - dm_t * m_jt + (jnp.where(sel, de, 0.0) - t_j), axis=1, keepdims=True)
                gam_j = cm["gam"][:, j:j + 1]
                last = (jnp.sum(jnp.sum(t_j, axis=0, keepdims=True), axis=1, keepdims=True)
                        + jnp.sum(jnp.where(sel[0:1, :], dgam_m, 0.0), axis=1, keepdims=True) * gam_j)
                col = col + jnp.where(sub[:, 0:1] == Q - 1, last, 0.0)
                dcs = dcs + jnp.where(lane == j, col, 0.0)
            dh_scr[p] = dh_in + dh * _pair_row(lane_even, cm["gam"], je)
            dx_ref[:, 128 * p:128 * p + 128] = dxp + dxdt * dt_p
            dd = dxdt * xp
            ddt = ddt + jnp.where(lane == je, jnp.sum(jnp.where(lane_even, dd, 0.0), axis=1, keepdims=True), 0.0)
            ddt = ddt + jnp.where(lane == jo, jnp.sum(jnp.where(lane_even, 0.0, dd), axis=1, keepdims=True), 0.0)
        dc_ref[:, :] = dc_acc + jnp.dot(dcb.astype(BF), cm["bc"], preferred_element_type=F32)
        db_ref[:, :] = db_acc + jnp.dot(dcb_t.astype(BF), cm["cc"], preferred_element_type=F32)
        tri_t = (sub <= lane).astype(F32)
        dd_a = jnp.dot(tri_t, dcs, precision=lax.Precision.HIGHEST, preferred_element_type=F32)
        ddt = ddt + dd_a * a_neg
        acc_scr[5:6, :] += jnp.sum(dd_a * cm["dt"], axis=0, keepdims=True)
        draw = jnp.where(cm["live"], ddt * _sigmoid_gate(cm["dtr"] + bias), 0.0)
        acc_scr[4:5, :] += jnp.sum(draw, axis=0, keepdims=True)
        ddt_ref[:, :] = draw.astype(BF)

        @pl.when(ci == NCH - 1)
        def _():
            lane1 = _lanes((1, 128))
            dd = jnp.zeros((1, 128), F32)
            for p in range(4):
                row = acc_scr[p:p + 1, :]
                dd = dd + jnp.where(lane1 == 2 * p, jnp.sum(jnp.where(lane1 < 64, row, 0.0), axis=1, keepdims=True), 0.0)
                dd = dd + jnp.where(lane1 == 2 * p + 1, jnp.sum(jnp.where(lane1 < 64, 0.0, row), axis=1, keepdims=True), 0.0)
            dpar_ref[...] = jnp.concatenate([acc_scr[4:5, :], acc_scr[5:6, :] * a_neg, dd, jnp.zeros((5, 128), F32)], axis=0)

    par = _spec((2, 1, 128), lambda c: (0, 0, 0))
    wide = _spec((Q, SSD_W), lambda c: (NCH - 1 - c, 0))
    thin = _spec((Q, 256), lambda c: (NCH - 1 - c, 0))
    vec = _spec((1, SSD_W), lambda c: (0, 0))
    xbc = PXBC // XBC
    halo = 2 * HALO
    chunk = pltpu.VMEM((Q, XBC), F32)
    padded = pltpu.VMEM((Q + HALO, XBC), F32)
    return pl.pallas_call(
        body, grid=(NCH,),
        in_specs=[wide, _spec((Q, XBC), lambda c: (NCH - 1 - c, xbc)),
                  _spec((halo, XBC), lambda c: (jnp.maximum((NCH - 1 - c) * (Q // halo) - 1, 0), xbc)), thin, wide, wide,
                  _spec((2, 1, 4, 128, 128), lambda c: (0, NCH - 1 - c, 0, 0, 0)), _spec((4, XBC), lambda c: (0, 0)),
                  _spec((1, XBC), lambda c: (0, 0)), par, par, par, vec, ANY_SPEC],
        out_specs=[wide, _spec((Q, XBC), lambda c: (NCH - 1 - c, 0)), thin, _spec((2, 8, 128), lambda c: (0, 0, 0)), vec,
                   _spec((4, XBC), lambda c: (0, 0)), _spec((1, XBC), lambda c: (0, 0))],
        out_shape=[jax.ShapeDtypeStruct((T, SSD_W), BF), jax.ShapeDtypeStruct((T, XBC), BF), jax.ShapeDtypeStruct((T, 256), BF),
                   jax.ShapeDtypeStruct((2, 8, 128), F32), jax.ShapeDtypeStruct((1, SSD_W), F32), jax.ShapeDtypeStruct((4, XBC), F32),
                   jax.ShapeDtypeStruct((1, XBC), F32)],
        scratch_shapes=[pltpu.VMEM((2, 4, 128, 128), F32), pltpu.VMEM((2, 8, 128), F32), padded, chunk, chunk, chunk, padded],
        compiler_params=_params(), name="ssd_bwd")(dyn, proj, proj, dt_raw, proj, y_pre, h_prev, conv_w, conv_b, dt_bias2, a_log2, d2, norm_w, after)


def _lru_gates(back, cw, cb, wa, ba, wx, bx, lam):
    xr = _conv(back, cw, cb)
    xr_b = xr.astype(BF)
    r = _sigmoid_gate(jnp.dot(xr_b, wa, preferred_element_type=F32) + ba)
    i = _sigmoid_gate(jnp.dot(xr_b, wx, preferred_element_type=F32) + bx)
    sp = _softplus(-lam)
    la = (-LRU_C) * r * sp
    a = jnp.exp(la)
    mult2 = -jnp.tanh(la) * (a * a + 1.0)
    return xr, xr_b, r, i, sp, a, jnp.sqrt(mult2), mult2


SEG_LEN = 68
SEGS = T // SEG_LEN


def _seg_rows(j, k, off=0):
    return pl.ds(off + j * 8 * SEG_LEN + k, 8, stride=SEG_LEN)


def _segmented_scan(mul_ref, mul_row0, add_ref, out_ref, loc_scr, prod_scr, carry_scr, reverse):
    groups = SEGS // 8
    off = mul_row0 + (1 if reverse else 0)

    def local(i, carry):
        k = SEG_LEN - 1 - i if reverse else i
        new = []
        for j in range(groups):
            h, p = carry[2 * j], carry[2 * j + 1]
            m = mul_ref[_seg_rows(j, k, off), :]
            h = m * h + add_ref[_seg_rows(j, k), :]
            p = m * p
            loc_scr[_seg_rows(j, k), :] = h
            prod_scr[_seg_rows(j, k), :] = p
            new += [h, p]
        return tuple(new)

    lax.fori_loop(0, SEG_LEN, local, (jnp.zeros((8, 128), F32), jnp.ones((8, 128), F32)) * groups)

    def chain(i, c):
        s = SEGS - 1 - i if reverse else i
        carry_scr[pl.ds(s, 1), :] = c
        edge = s * SEG_LEN + (0 if reverse else SEG_LEN - 1)
        return loc_scr[pl.ds(edge, 1), :] + prod_scr[pl.ds(edge, 1), :] * c

    lax.fori_loop(0, SEGS, chain, jnp.zeros((1, 128), F32))

    def fold(k, carry):
        for j in range(groups):
            rows = _seg_rows(j, k)
            out_ref[rows, :] = loc_scr[rows, :] + prod_scr[rows, :] * carry_scr[8 * j:8 * j + 8, :]
        return carry

    lax.fori_loop(0, SEG_LEN, fold, 0)


def lru_fwd(proj, cw, cb, wa2, ba, wx2, bx, lam, after):
    def body(x_ref, cw_ref, cb_ref, wa_ref, ba_ref, wx_ref, bx_ref, lam_ref, _after, h_ref, a_ref, xpad, u_scr, loc_scr, prod_scr, carry_scr):
        _fill_padded(xpad, x_ref)

        def chunk(r0):
            xr, _, _, i, _, a, mult, _ = _lru_gates(_back(xpad, r0), cw_ref[...], cb_ref[...], wa_ref[0], ba_ref[...], wx_ref[0], bx_ref[...],
                                                 lam_ref[...])
            a_ref[pl.ds(r0, Q), :] = a
            u_scr[pl.ds(r0, Q), :] = jnp.where(_rows(a.shape, r0) >= NPAD, mult * (i * xr), 0.0)

        _chunks(chunk, unrolled=True)
        _segmented_scan(a_ref, 0, u_scr, h_ref, loc_scr, prod_scr, carry_scr, reverse=False)

    c0 = PXL // 128
    vec = _spec((1, 128), lambda c: (0, c))
    mat = _spec((1, 128, 128), lambda c: (c, 0, 0))
    seq = pltpu.VMEM((T, 128), F32)
    return pl.pallas_call(
        body, grid=(8,),
        in_specs=[_spec((T, 128), lambda c: (0, c0 + c)), _spec((4, 128), lambda c: (0, c)), vec, mat, vec, mat, vec, vec, ANY_SPEC],
        out_specs=[_spec((T, 128), lambda c: (0, c)), _spec((T, 128), lambda c: (0, c))],
        out_shape=[jax.ShapeDtypeStruct((T, LRU_W), F32), jax.ShapeDtypeStruct((T, LRU_W), F32)],
        scratch_shapes=[pltpu.VMEM((T + 2 * HALO, 128), F32), seq, seq, seq, pltpu.VMEM((SEGS, 128), F32)],
        compiler_params=_params(), name="lru_fwd")(proj, cw, cb, wa2, ba, wx2, bx, lam, after)


def lru_bwd(dh_out, a, hseq, proj, cw, cb, wa2, ba, wx2, bx, lam):
    def body(d_ref, a_ref, h_ref, x_ref, cw_ref, cb_ref, wa_ref, ba_ref, wx_ref, bx_ref, lam_ref,
             dx_ref, dcw_ref, dcb_ref, dwa_ref, dba_ref, dwx_ref, dbx_ref, dlam_ref, xpad, hpad, dpad, dh_ref, loc_scr, prod_scr, carry_scr):
        _fill_padded(dpad, a_ref)
        _segmented_scan(dpad, HALO, d_ref, dh_ref, loc_scr, prod_scr, carry_scr, reverse=True)
        _fill_padded(xpad, x_ref)
        _fill_padded(hpad, h_ref)
        dpad[0:HALO, :] = jnp.zeros((HALO, 128), F32)
        dpad[T + HALO:T + 2 * HALO, :] = jnp.zeros((HALO, 128), F32)
        for ref in (dcw_ref, dcb_ref, dwa_ref, dba_ref, dwx_ref, dbx_ref, dlam_ref):
            ref[...] = jnp.zeros_like(ref)
        lam = lam_ref[...]

        def first(r0):
            back = _back(xpad, r0)
            xr, xr_b, r, i, sp, a, mult, mult2 = _lru_gates(back, cw_ref[...], cb_ref[...], wa_ref[0], ba_ref[...], wx_ref[0], bx_ref[...], lam)
            dh = dh_ref[pl.ds(r0, Q), :]
            da = dh * _back(hpad, r0)(1)
            du = jnp.where(_rows(dh.shape, r0) >= NPAD, dh, 0.0)
            dmult = du * (i * xr)
            di = du * (mult * xr)
            dxr = du * (mult * i)
            dla = da * a - dmult * (a * a) * lax.rsqrt(mult2)
            dr = dla * ((-LRU_C) * sp)
            dlam_ref[...] += jnp.sum(dla * ((-LRU_C) * r), axis=0, keepdims=True)
            dpr = dr * r * (1.0 - r)
            dpi = di * i * (1.0 - i)
            dba_ref[...] += jnp.sum(dpr, axis=0, keepdims=True)
            dbx_ref[...] += jnp.sum(dpi, axis=0, keepdims=True)
            dpr_b = dpr.astype(BF)
            dpi_b = dpi.astype(BF)
            dxr = (dxr + lax.dot_general(dpr_b, wa_ref[0], NT_DIMS, preferred_element_type=F32)
                   + lax.dot_general(dpi_b, wx_ref[0], NT_DIMS, preferred_element_type=F32))
            dwa_ref[0] += lax.dot_general(xr_b, dpr_b, TN_DIMS, preferred_element_type=F32)
            dwx_ref[0] += lax.dot_general(xr_b, dpi_b, TN_DIMS, preferred_element_type=F32)
            dpad[pl.ds(r0 + HALO, Q), :] = dxr
            dcw, dcb = _conv_bwd_w(dxr, back)
            dcw_ref[...] += dcw
            dcb_ref[...] += dcb

        _chunks(first, unrolled=True)
        dlam_ref[...] = -dlam_ref[...] * _sigmoid_gate(-lam)

        def second(r0):
            dx_ref[pl.ds(r0, Q), :] = _conv_bwd_x(_ahead(dpad, r0), cw_ref[...]).astype(BF)

        _chunks(second)

    c0 = PXL // 128
    vec = _spec((1, 128), lambda c: (0, c))
    mat = _spec((1, 128, 128), lambda c: (c, 0, 0))
    col = _spec((T, 128), lambda c: (0, c))
    vshape = jax.ShapeDtypeStruct((1, LRU_W), F32)
    mshape = jax.ShapeDtypeStruct((8, 128, 128), F32)
    pad = pltpu.VMEM((T + 2 * HALO, 128), F32)
    seq = pltpu.VMEM((T, 128), F32)
    return pl.pallas_call(
        body, grid=(8,),
        in_specs=[col, col, col, _spec((T, 128), lambda c: (0, c0 + c)), _spec((4, 128), lambda c: (0, c)), vec, mat, vec, mat, vec, vec],
        out_specs=[col, _spec((4, 128), lambda c: (0, c)), vec, mat, vec, mat, vec, vec],
        out_shape=[jax.ShapeDtypeStruct((T, LRU_W), BF), jax.ShapeDtypeStruct((4, LRU_W), F32), vshape, mshape, vshape, mshape, vshape, vshape],
        scratch_shapes=[pad, pad, pad, seq, seq, seq, pltpu.VMEM((SEGS, 128), F32)],
        compiler_params=_params(), name="lru_bwd")(dh_out, a, hseq, proj, cw, cb, wa2, ba, wx2, bx, lam)


FF_TILE = 256
FF_TILE_ROWS = list(range(0, D_FF, FF_TILE))


def gate_up(h1, wn, w_gate, w_up):
    def body(h_ref, wn_ref, wg_hbm, wu_hbm, gt_ref, up_ref, act_ref, u_ref, wg_ref, wu_ref, wg_sems, wu_sems):
        gate_ready = _arriving(wg_hbm, wg_ref, wg_sems, FF_TILE_ROWS, FF_TILE)
        up_ready = _arriving(wu_hbm, wu_ref, wu_sems, FF_TILE_ROWS, FF_TILE)
        for r in (0, HALF):
            u_ref[r:r + HALF, :] = _rms(h_ref[r:r + HALF, :], wn_ref[...]).astype(BF)

        def tile(c0):
            cols = pl.ds(c0, FF_TILE)
            gate_ready(c0 // FF_TILE)
            up_ready(c0 // FF_TILE)
            gt = lax.dot_general(u_ref[...], wg_ref[cols, :], NT_DIMS, preferred_element_type=F32)
            up = lax.dot_general(u_ref[...], wu_ref[cols, :], NT_DIMS, preferred_element_type=F32)
            gt_ref[:, cols] = gt.astype(BF)
            up_ref[:, cols] = up.astype(BF)
            act_ref[:, cols] = (gt * _sigmoid(gt) * up).astype(BF)

        _col_tiles(D_FF, FF_TILE, tile)

    big = jax.ShapeDtypeStruct((T, D_FF), BF)
    return pl.pallas_call(
        body, grid=(T // RC,), in_specs=[_rows_spec(D), _vec(D), ANY_SPEC, ANY_SPEC],
        out_specs=[_rows_spec(D_FF), _rows_spec(D_FF), _rows_spec(D_FF), _rows_spec(D)],
        out_shape=[big, big, big, jax.ShapeDtypeStruct((T, D), BF)],
        scratch_shapes=[pltpu.VMEM((D_FF, D), BF)] * 2 + [pltpu.SemaphoreType.DMA((len(FF_TILE_ROWS),))] * 2,
        compiler_params=_params(), name="gate_up")(h1, wn, w_gate, w_up)


def down_loss(act, w_down, h1, target, wf):
    first = NPAD + N_META

    def body(a_ref, w_ref, r_ref, t_hbm, wf_ref, d_ref, db_ref, l_ref, dw_ref, h_scr, t_ref, t_sem):
        i = pl.program_id(0)
        _zero_at_first(l_ref, dw_ref)
        head = pltpu.make_async_copy(t_hbm.at[pl.ds(0, RC - first)], t_ref.at[pl.ds(first, RC - first)], t_sem)
        rest = pltpu.make_async_copy(t_hbm.at[pl.ds(pl.multiple_of(jnp.maximum(i * RC - first, 0), 32), RC)], t_ref, t_sem)

        @pl.when(i == 0)
        def _():
            t_ref[0:first, :] = jnp.zeros((first, D), F32)
            head.start()

        @pl.when(i > 0)
        def _():
            rest.start()

        def tile(c0):
            cols = pl.ds(c0, 512)
            h_scr[:, cols] = r_ref[:, cols] + jnp.dot(a_ref[...], w_ref[:, cols], preferred_element_type=F32)

        _col_tiles(D, 512, tile)

        @pl.when(i == 0)
        def _():
            head.wait()

        @pl.when(i > 0)
        def _():
            rest.wait()

        for r in (0, HALF):
            h = h_scr[r:r + HALF, :]
            live = _rows((HALF, D), i * RC + r) >= first
            err = jnp.where(live, _rms(h, wf_ref[...]) - t_ref[r:r + HALF, :], 0.0)
            l_ref[...] += 0.5 * jnp.sum(jnp.sum(err * err, axis=1, keepdims=True) * (1.0 / D), axis=0, keepdims=True)
            dh, dw = _rms_bwd(err * (1.0 / D), h, wf_ref[...])
            dw_ref[...] += jnp.sum(dw, axis=0, keepdims=True)
            d_ref[r:r + HALF, :] = dh
            db_ref[r:r + HALF, :] = dh.astype(BF)

    return pl.pallas_call(
        body, grid=(T // RC,),
        in_specs=[_rows_spec(D_FF), _whole((D_FF, D)), _rows_spec(D), pl.BlockSpec(memory_space=pl.ANY), _vec(D)],
        out_specs=[_rows_spec(D), _rows_spec(D), _spec((1, 128), lambda i: (0, 0)), _vec(D)],
        out_shape=[jax.ShapeDtypeStruct((T, D), F32), jax.ShapeDtypeStruct((T, D), BF), jax.ShapeDtypeStruct((1, 128), F32),
                   jax.ShapeDtypeStruct((1, D), F32)],
        scratch_shapes=[pltpu.VMEM((RC, D), F32), pltpu.VMEM((RC, D), F32), pltpu.SemaphoreType.DMA],
        compiler_params=_params(), name="down_loss")(act, w_down, h1, target, wf)


def swiglu_bwd(dh2_b, w_down, gt, up, act, u2):
    tn = 256

    def body(d_hbm, u_hbm, w_ref, gt_ref, up_ref, act_ref, dg_ref, du_ref, gd_ref, gg_ref, gu_ref, d_ref, u_ref, d_sems, u_sems):
        chunks = list(range(0, T, RC))
        d_ready = _arriving(d_hbm, d_ref, d_sems, chunks, RC)
        u_ready = _arriving(u_hbm, u_ref, u_sems, chunks, RC)

        def rows(r0):
            part = pl.ds(r0, RC)
            d_ready(r0 // RC)
            dact = lax.dot_general(d_ref[part, :], w_ref[...], NT_DIMS, preferred_element_type=F32)
            gt_ = gt_ref[part, :].astype(F32)
            up_ = up_ref[part, :].astype(F32)
            sg = _sigmoid(gt_)
            dg_ref[part, :] = (dact * up_ * (sg * (1.0 + gt_ * (1.0 - sg)))).astype(BF)
            du_ref[part, :] = (dact * (gt_ * sg)).astype(BF)

        _col_tiles(T, RC, rows)
        for k in range(len(chunks)):
            u_ready(k)
        gd_ref[...] = lax.dot_general(act_ref[...], d_ref[...], TN_DIMS, preferred_element_type=F32).astype(BF)
        gg_ref[...] = lax.dot_general(dg_ref[...], u_ref[...], TN_DIMS, preferred_element_type=F32).astype(BF)
        gu_ref[...] = lax.dot_general(du_ref[...], u_ref[...], TN_DIMS, preferred_element_type=F32).astype(BF)

    cols = _spec((T, tn), lambda j: (0, j))
    wrow = _spec((tn, D), lambda j: (j, 0))
    big = jax.ShapeDtypeStruct((T, D_FF), BF)
    grad = jax.ShapeDtypeStruct((D_FF, D), BF)
    return pl.pallas_call(
        body, grid=(D_FF // tn,), in_specs=[ANY_SPEC, ANY_SPEC, wrow, cols, cols, cols],
        out_specs=[cols, cols, wrow, wrow, wrow], out_shape=[big, big, grad, grad, grad],
        scratch_shapes=[pltpu.VMEM((T, D), BF)] * 2 + [pltpu.SemaphoreType.DMA((T // RC,))] * 2,
        compiler_params=_params(), name="swiglu_bwd")(dh2_b, u2, w_down, gt, up, act)


def gate_up_bwd(dgt, dup, w_gate, w_up, h1, wn, dh2):
    def body(dg_ref, du_ref, wg_hbm, wu_hbm, h_ref, wn_ref, r_ref, d_ref, db_ref, dw_ref, du_scr, wg_ref, wu_ref, wg_sems, wu_sems):
        gate_ready = _arriving(wg_hbm, wg_ref, wg_sems, FF_TILE_ROWS, FF_TILE)
        up_ready = _arriving(wu_hbm, wu_ref, wu_sems, FF_TILE_ROWS, FF_TILE)
        _zero_at_first(dw_ref)

        du_scr[...] = jnp.zeros_like(du_scr)

        def tile(c0):
            k = pl.ds(c0, FF_TILE)
            gate_ready(c0 // FF_TILE)
            up_ready(c0 // FF_TILE)
            du_scr[...] += (jnp.dot(dg_ref[:, k], wg_ref[k, :], preferred_element_type=F32)
                            + jnp.dot(du_ref[:, k], wu_ref[k, :], preferred_element_type=F32))

        _col_tiles(D_FF, FF_TILE, tile)
        for r in (0, HALF):
            dh, dw = _rms_bwd(du_scr[r:r + HALF, :], h_ref[r:r + HALF, :], wn_ref[...])
            dw_ref[...] += jnp.sum(dw, axis=0, keepdims=True)
            dh = dh + r_ref[r:r + HALF, :]
            d_ref[r:r + HALF, :] = dh
            db_ref[r:r + HALF, :] = dh.astype(BF)

    return pl.pallas_call(
        body, grid=(T // RC,),
        in_specs=[_rows_spec(D_FF), _rows_spec(D_FF), ANY_SPEC, ANY_SPEC, _rows_spec(D), _vec(D), _rows_spec(D)],
        out_specs=[_rows_spec(D), _rows_spec(D), _vec(D)],
        out_shape=[jax.ShapeDtypeStruct((T, D), F32), jax.ShapeDtypeStruct((T, D), BF), jax.ShapeDtypeStruct((1, D), F32)],
        scratch_shapes=[pltpu.VMEM((RC, D), F32)] + [pltpu.VMEM((D_FF, D), BF)] * 2 + [pltpu.SemaphoreType.DMA((len(FF_TILE_ROWS),))] * 2,
        compiler_params=_params(), name="gate_up_bwd")(dgt, dup, w_gate, w_up, h1, wn, dh2)


def _adamw(w, g, m, v):
    m = ADAM_B1 * m + (1.0 - ADAM_B1) * g
    v = ADAM_B2 * v + (1.0 - ADAM_B2) * (g * g)
    m_hat = m / (1.0 - ADAM_B1 ** ADAM_STEP)
    v_hat = v / (1.0 - ADAM_B2 ** ADAM_STEP)
    delta = -ADAM_LR * (m_hat / (jnp.sqrt(v_hat) + ADAM_EPS) + ADAM_WD * w)
    return delta, m, v


def adamw_shards(name, recvs, ws, ms, vs):
    n = len(ws)

    def body(*refs):
        ins, outs = refs[:4 * n], refs[4 * n:]
        for k in range(n):
            p_ref, w_ref, m_ref, v_ref = ins[k], ins[n + k], ins[2 * n + k], ins[3 * n + k]
            g = p_ref[0].astype(F32)
            for s in range(1, 8):
                g = g + p_ref[s].astype(F32)
            outs[4 * k][...] = g
            outs[4 * k + 1][...], outs[4 * k + 2][...], outs[4 * k + 3][...] = _adamw(w_ref[...], g, m_ref[...], v_ref[...])

    tiles = [_spec((w.shape[0] // 2, w.shape[1]), lambda i: (i, 0)) for w in ws]
    recv_tiles = [_spec((8, w.shape[0] // 2, w.shape[1]), lambda i: (0, i, 0)) for w in ws]
    res = pl.pallas_call(
        body, grid=(2,), in_specs=recv_tiles + tiles * 3,
        out_specs=[t for t in tiles for _ in range(4)],
        out_shape=[jax.ShapeDtypeStruct(w.shape, F32) for w in ws for _ in range(4)],
        compiler_params=_params(), name=name)(*recvs, *ws, *ms, *vs)
    return [list(res[4 * k:4 * k + 4]) for k in range(n)]


def adamw_w_in(recv, w, m, v, after):
    rows = 34
    per_row = D // 128

    def body(p_ref, w_ref, m_ref, v_ref, _after, g_ref, d_ref, mo_ref, vo_ref):
        def chunk(c, carry):
            lines = pl.ds(pl.multiple_of(c * per_row * rows, 16), per_row * rows)
            g = p_ref[0, lines, :].astype(F32)
            for s in range(1, 8):
                g = g + p_ref[s, lines, :].astype(F32)
            g = g.reshape(rows, per_row, 128)
            part = pl.ds(c * rows, rows)
            g_ref[part] = g
            d_ref[part], mo_ref[part], vo_ref[part] = _adamw(w_ref[part], g, m_ref[part], v_ref[part])
            return carry

        lax.fori_loop(0, w.shape[0] // rows, chunk, 0)

    shape = jax.ShapeDtypeStruct(w.shape, F32)
    whole = pl.BlockSpec(memory_space=pltpu.VMEM)
    return pl.pallas_call(body, out_shape=[shape] * 4, in_specs=[whole] * 4 + [ANY_SPEC], compiler_params=_params(0),
                          name="adamw_w_in")(recv, w, m, v, after)


def sum_slabs(recv):
    def body(p_ref, o_ref):
        g = p_ref[0]
        for s in range(1, 8):
            g = g + p_ref[s]
        for s in range(8):
            o_ref[s] = g

    return pl.pallas_call(body, out_shape=jax.ShapeDtypeStruct(recv.shape, F32), compiler_params=_params(0), name="sum_slabs")(recv)


SIMPLE = [("norm1_w", 1024), ("ssd_conv_b", 1536), ("ssd_dt_bias", 16), ("ssd_a_log", 16), ("ssd_d", 16), ("ssd_norm_w", 1024),
          ("lru_conv_b", 1024), ("lru_ba", 1024), ("lru_bx", 1024), ("lru_lambda", 1024), ("lru_norm_w", 1024), ("norm2_w", 1024),
          ("final_norm_w", 1024)]
SPECIAL = ["lru_wa", "lru_wx", "meta_tokens", "ssd_conv_w", "lru_conv_w"]
SM_ROWS = 176
SM_WA, SM_WX, SM_META, SM_SCW, SM_LCW, SM_LOSS = 14, 78, 142, 158, 166, 170


def _simple_rows():
    rows, r = {}, 0
    for name, n in SIMPLE:
        rows[name] = r
        r += -(-n // 1024)
    return rows


def adamw_small(sm, special_g, ws, ms, vs):
    rows = _simple_rows()
    ns, nx = len(SIMPLE), len(SPECIAL)

    def body(*refs):
        sm_ref = refs[0]
        gx = refs[1:1 + nx]
        wr = refs[1 + nx:1 + nx + ns + nx]
        mr = refs[1 + nx + ns + nx:1 + nx + 2 * (ns + nx)]
        vr = refs[1 + nx + 2 * (ns + nx):1 + nx + 3 * (ns + nx)]
        outs = refs[1 + nx + 3 * (ns + nx):]
        o = 0
        for k, (name, n) in enumerate(SIMPLE):
            r0 = rows[name]
            for c0 in range(0, n, 1024):
                wd = min(1024, n - c0)
                g = sm_ref[r0 + c0 // 1024:r0 + c0 // 1024 + 1, 0:wd]
                sl = (slice(None), slice(c0, c0 + wd))
                d, m2, v2 = _adamw(wr[k][sl], g, mr[k][sl], vr[k][sl])
                outs[o][sl] = g
                outs[o + 1][sl] = d
                outs[o + 2][sl] = m2
                outs[o + 3][sl] = v2
            o += 4
        for k in range(nx):
            d, m2, v2 = _adamw(wr[ns + k][...], gx[k][...], mr[ns + k][...], vr[ns + k][...])
            outs[o][...] = d
            outs[o + 1][...] = m2
            outs[o + 2][...] = v2
            o += 3

    out_shape = []
    for k in range(ns):
        out_shape += [jax.ShapeDtypeStruct(ws[k].shape, F32)] * 4
    for k in range(nx):
        out_shape += [jax.ShapeDtypeStruct(ws[ns + k].shape, F32)] * 3
    return pl.pallas_call(body, out_shape=out_shape, compiler_params=_params(0), name="adamw_small")(sm, *special_g, *ws, *ms, *vs)


def _place():
    return lax.axis_index("x"), lax.axis_index("y"), lax.axis_index("c")


def _index(px, py, pc):
    return 4 * px + 2 * py + pc


def all_gather(name, shards):
    n = len(shards)
    hbm = pl.BlockSpec(memory_space=pl.ANY)

    def pieces(s):
        tile = 32 // s.dtype.itemsize
        per = s.shape[0] // tile // 4 * tile
        return [(0, s.shape[0])] if s.shape[0] < 256 else [(r * per, per if r < 3 else s.shape[0] - 3 * per) for r in range(4)]

    parts = [pieces(s) for s in shards]
    first_sem = [7 * sum(len(p) for p in parts[:i]) for i in range(n + 1)]

    def body(*refs):
        ins, outs = refs[:n], refs[n:2 * n]
        send_sems, recv_sems, local_sems = refs[2 * n:]
        x, y, c = _place()
        me, sibling = (x, y, c), (x, y, 1 - c)
        chips = [(1 - x, y), (x, 1 - y), (1 - x, 1 - y)]

        def copy(i, r, k, block, to, src=None):
            rows = pl.ds(*parts[i][r])
            dst = outs[i].at[_index(*block), rows]
            sem = first_sem[i] + 7 * r + k
            return pltpu.make_async_remote_copy(src_ref=dst if src is None else src.at[rows], dst_ref=dst, send_sem=send_sems.at[sem],
                                                recv_sem=recv_sems.at[sem], device_id=to, device_id_type=MESH)

        every = [(i, r) for i in range(n) for r in range(len(parts[i]))]
        mine = [pltpu.make_async_copy(ins[i], outs[i].at[_index(*me)], local_sems.at[i]) for i in range(n)]
        for cp in mine:
            cp.start()
        first = []
        for i, r in every:
            first += [copy(i, r, 1 + j, me, (*chip, c), src=ins[i]) for j, chip in enumerate(chips)]
            first.append(copy(i, r, 0, me, sibling, src=ins[i]))
        for cp in first:
            cp.start()
        passed = []
        for i, r in every:
            for j, chip in enumerate(chips):
                copy(i, r, 1 + j, (*chip, c), me).wait_recv()
                cp = copy(i, r, 4 + j, (*chip, c), sibling)
                cp.start()
                passed.append(cp)
        for i, r in every:
            copy(i, r, 0, sibling, me).wait_recv()
            for j, chip in enumerate(chips):
                copy(i, r, 4 + j, (*chip, 1 - c), me).wait_recv()
        for cp in first + passed:
            cp.wait_send()
        for cp in mine:
            cp.wait()

    return pl.pallas_call(
        body, in_specs=[hbm] * n, out_specs=[hbm] * n,
        out_shape=[jax.ShapeDtypeStruct((8,) + s.shape, s.dtype) for s in shards],
        scratch_shapes=[pltpu.SemaphoreType.DMA((first_sem[n],)), pltpu.SemaphoreType.DMA((first_sem[n],)), pltpu.SemaphoreType.DMA((n,))],
        name=name)(*shards)


HBM_SPEC = pl.BlockSpec(memory_space=pltpu.HBM)
SEM_SPEC = pl.BlockSpec(memory_space=pltpu.SEMAPHORE)
EFFECT = pltpu.SideEffectType.DATAFLOW_SIDE_EFFECTING


def _peers(x, y, c):
    return [((1 - x) if k & 4 else x, (1 - y) if k & 2 else y, (1 - c) if k & 1 else c) for k in range(1, 8)]


def _pieces(rows):
    for n in (4, 2):
        if rows % (16 * n) == 0:
            return [(r * (rows // n), rows // n) for r in range(n)]
    return [(0, rows)]


def _peer_copies(src, land, send_sems, recv_sems, k, peer, mine):
    block = src.at[_index(*peer)]
    return [pltpu.make_async_remote_copy(src_ref=block.at[pl.ds(r0, nr)], dst_ref=land.at[mine, pl.ds(r0, nr)], send_sem=send_sems.at[k],
                                         recv_sem=recv_sems.at[k], device_id=peer, device_id_type=MESH)
            for r0, nr in _pieces(block.shape[0])]


OWN = 7


def _own_copy(src, land, send_sems, mine):
    return pltpu.make_async_copy(src.at[mine], land.at[mine], send_sems.at[OWN])


def copies_start(name, srcs):
    n = len(srcs)

    def body(*refs):
        ins, lands = refs[:n], refs[n:2 * n]
        sends, recvs = refs[2 * n:3 * n], refs[3 * n:4 * n]
        token = refs[-1]
        x, y, c = _place()
        mine = _index(x, y, c)
        for i in range(n):
            per_peer = [_peer_copies(ins[i], lands[i], sends[i], recvs[i], k, peer, mine) for k, peer in enumerate(_peers(x, y, c))]
            for piece in zip(*per_peer):
                for cp in piece:
                    cp.start()
            _own_copy(ins[i], lands[i], sends[i], mine).start()
        token[...] = jnp.zeros_like(token)

    res = pl.pallas_call(
        body, name=name,
        out_shape=([pltpu.SemaphoreType.DMA((8,))] * n + [pltpu.SemaphoreType.DMA((7,))] * n + [pltpu.HBM(s.shape, s.dtype) for s in srcs] * 2
                   + [jax.ShapeDtypeStruct((8, 128), F32)]),
        in_specs=[HBM_SPEC] * (2 * n),
        out_specs=[SEM_SPEC] * (2 * n) + [HBM_SPEC] * (2 * n) + [pl.BlockSpec(memory_space=pltpu.VMEM)],
        input_output_aliases={i: 2 * n + i for i in range(2 * n)},
        compiler_params=pltpu.CompilerParams(has_side_effects=EFFECT),
    )(*[pltpu.with_memory_space_constraint(s, pltpu.HBM) for s in srcs],
      *[pltpu.with_memory_space_constraint(lax.empty(s.shape, s.dtype), pltpu.HBM) for s in srcs])
    return [(res[i], res[n + i], res[2 * n + i], res[3 * n + i]) for i in range(n)], res[-1]


def copies_wait(name, started, after):
    n = len(started)

    def body(*refs):
        ins, lands = refs[:n], refs[n:2 * n]
        sends, recvs = refs[2 * n:3 * n], refs[3 * n:4 * n]
        x, y, c = _place()
        mine = _index(x, y, c)
        for i in range(n):
            for k, peer in enumerate(_peers(x, y, c)):
                arrival = pltpu.make_async_remote_copy(src_ref=ins[i].at[mine], dst_ref=lands[i].at[_index(*peer)],
                                                       send_sem=sends[i].at[k], recv_sem=recvs[i].at[k], device_id=peer, device_id_type=MESH)
                arrival.wait_send()
                arrival.wait_recv()
            _own_copy(ins[i], lands[i], sends[i], mine).wait()

    srcs = [s[2] for s in started]
    lands = [s[3] for s in started]
    res = pl.pallas_call(
        body, name=name,
        out_shape=[pltpu.HBM(s.shape, s.dtype) for s in srcs] + [pltpu.HBM(z.shape, z.dtype) for z in lands],
        in_specs=[HBM_SPEC] * (2 * n) + [SEM_SPEC] * (2 * n) + [ANY_SPEC] * len(after),
        out_specs=[HBM_SPEC] * (2 * n),
        input_output_aliases={i: i for i in range(2 * n)},
        compiler_params=pltpu.CompilerParams(has_side_effects=EFFECT),
    )(*srcs, *lands, *[s[0] for s in started], *[s[1] for s in started], *after)
    return list(res[n:])


def _hop(src, land, send_sems, recv_sems, k, block, to):
    dst = land.at[_index(*block)]
    return pltpu.make_async_remote_copy(src_ref=dst if src is None else src, dst_ref=dst, send_sem=send_sems.at[k], recv_sem=recv_sems.at[k],
                                        device_id=to, device_id_type=MESH)


def _own_block(src, land, send_sems, mine):
    return pltpu.make_async_copy(src, land.at[mine], send_sems.at[4])


def _other_chips(x, y):
    return [(1 - x, y), (x, 1 - y), (1 - x, 1 - y)]


def gather_start(name, shards, through):
    n, m = len(shards), len(through)

    def body(*refs):
        ins, lands = refs[:n], refs[n:2 * n]
        sends, recvs = refs[2 * n + m:3 * n + m], refs[3 * n + m:4 * n + m]
        x, y, c = _place()
        for i in range(n):
            for j, chip in enumerate(_other_chips(x, y)):
                _hop(ins[i], lands[i], sends[i], recvs[i], 1 + j, (x, y, c), (*chip, c)).start()
            _hop(ins[i], lands[i], sends[i], recvs[i], 0, (x, y, c), (x, y, 1 - c)).start()
            _own_block(ins[i], lands[i], sends[i], _index(x, y, c)).start()

    own, passing = pltpu.SemaphoreType.DMA((5,)), pltpu.SemaphoreType.DMA((3,))
    zones = [jax.ShapeDtypeStruct((8,) + s.shape, s.dtype) for s in shards]
    res = pl.pallas_call(
        body, name=name,
        out_shape=([own] * (2 * n) + [passing] * (2 * n) + [pltpu.HBM(s.shape, s.dtype) for s in shards]
                   + [pltpu.HBM(z.shape, z.dtype) for z in zones] + [pltpu.HBM(t.shape, t.dtype) for t in through]),
        in_specs=[HBM_SPEC] * (2 * n + m),
        out_specs=[SEM_SPEC] * (4 * n) + [HBM_SPEC] * (2 * n + m),
        input_output_aliases={i: 4 * n + i for i in range(2 * n + m)},
        compiler_params=pltpu.CompilerParams(has_side_effects=EFFECT),
    )(*[pltpu.with_memory_space_constraint(s, pltpu.HBM) for s in shards],
      *[pltpu.with_memory_space_constraint(lax.empty(z.shape, z.dtype), pltpu.HBM) for z in zones],
      *[pltpu.with_memory_space_constraint(t, pltpu.HBM) for t in through])
    return [[res[4 * n + i], res[5 * n + i], res[i], res[n + i], res[2 * n + i], res[3 * n + i]] for i in range(n)], list(res[6 * n:])


def gather_stage(name, pass_on, finish, after):
    arrays = pass_on + finish
    n = len(arrays)

    def body(*refs):
        ins, lands = refs[:n], refs[n:2 * n]
        sems = [refs[(2 + q) * n:(3 + q) * n] for q in range(4)]
        x, y, c = _place()
        me, sibling = (x, y, c), (x, y, 1 - c)
        for i in range(len(pass_on)):
            send, recv, send_on, recv_on = (q[i] for q in sems)
            for j, chip in enumerate(_other_chips(x, y)):
                _hop(None, lands[i], send, recv, 1 + j, (*chip, c), me).wait_recv()
                _hop(None, lands[i], send_on, recv_on, j, (*chip, c), sibling).start()
        for i in range(len(pass_on), n):
            send, recv, send_on, recv_on = (q[i] for q in sems)
            _hop(ins[i], lands[i], send, recv, 0, sibling, me).wait_recv()
            for j, chip in enumerate(_other_chips(x, y)):
                _hop(None, lands[i], send_on, recv_on, j, (*chip, 1 - c), me).wait_recv()
            _hop(ins[i], lands[i], send, recv, 0, me, sibling).wait_send()
            _own_block(ins[i], lands[i], send, _index(*me)).wait()
            for j, chip in enumerate(_other_chips(x, y)):
                _hop(ins[i], lands[i], send, recv, 1 + j, me, (*chip, c)).wait_send()
                _hop(None, lands[i], send_on, recv_on, j, (*chip, c), sibling).wait_send()
        refs[-1][...] = jnp.zeros_like(refs[-1])

    res = pl.pallas_call(
        body, name=name,
        out_shape=([pltpu.HBM(a[0].shape, a[0].dtype) for a in arrays] + [pltpu.HBM(a[1].shape, a[1].dtype) for a in arrays]
                   + [jax.ShapeDtypeStruct((8, 128), F32)]),
        in_specs=[HBM_SPEC] * (2 * n) + [SEM_SPEC] * (4 * n) + [ANY_SPEC],
        out_specs=[HBM_SPEC] * (2 * n) + [pl.BlockSpec(memory_space=pltpu.VMEM)],
        input_output_aliases={i: i for i in range(2 * n)},
        compiler_params=pltpu.CompilerParams(has_side_effects=EFFECT),
    )(*[a[0] for a in arrays], *[a[1] for a in arrays], *[a[2 + q] for q in range(4) for a in arrays], after)
    for i, a in enumerate(arrays):
        a[0], a[1] = res[i], res[n + i]
    return [a[1] for a in finish], res[-1]


WEIGHTS = ["meta_tokens", "norm1_w", "w_in", "ssd_conv_w", "ssd_conv_b", "ssd_dt_bias", "ssd_a_log", "ssd_d", "ssd_norm_w", "lru_conv_w",
           "lru_conv_b", "lru_wa", "lru_ba", "lru_wx", "lru_bx", "lru_lambda", "lru_norm_w", "w_out", "norm2_w", "w_gate", "w_up", "w_down",
           "final_norm_w"]
BIG = ["w_in", "w_out", "w_gate", "w_up", "w_down"]
COLUMN_SHARDED = ["w_in", "w_gate", "w_up"]


def _pair_blocks(w):
    w = w.reshape(8, 2, 64, 64)
    z = jnp.zeros((8, 64, 64), w.dtype)
    return jnp.concatenate([jnp.concatenate([w[:, 0], z], axis=2), jnp.concatenate([z, w[:, 1]], axis=2)], axis=1)


def _unpair_blocks(w2):
    return jnp.stack([w2[:, :64, :64], w2[:, 64:, 64:]], axis=1).reshape(16, 64, 64)


def _per_group(v):
    return jnp.pad(v.reshape(2, 1, 8), ((0, 0), (0, 0), (0, 120)))


def _pad_cols(v, n):
    return jnp.pad(v, ((0, 0), (0, n - v.shape[1])))


def local_step(x, target, meta, ssd_cw, lru_cw, w_in_shards, fetch, send, p):
    bias2, alog2, d2 = _per_group(p["ssd_dt_bias"]), _per_group(p["ssd_a_log"]), _per_group(p["ssd_d"])
    wa2 = _pair_blocks(p["lru_wa"]).astype(BF)
    wx2 = _pair_blocks(p["lru_wx"]).astype(BF)
    lru = (lru_cw, p["lru_conv_b"], wa2, p["lru_ba"], wx2, p["lru_bx"], p["lru_lambda"])

    proj, dt_raw, u1, h0, w_in, w_dt = in_proj(x, meta, p["norm1_w"], w_in_shards)
    yn_ssd, y_pre, h_prev = ssd_fwd(proj, dt_raw, ssd_cw, p["ssd_conv_b"], bias2, alog2, d2, p["ssd_norm_w"])
    _, moved = fetch([], yn_ssd)
    hseq, a = lru_fwd(proj, *lru, moved)
    (w_out,), _ = fetch(["w_out"], hseq)
    h1, cat = out_proj(yn_ssd, proj, hseq, p["lru_norm_w"], w_out, h0)
    (w_gate, w_up), _ = fetch(["w_gate", "w_up"], h1)
    gt, up, act, u2 = gate_up(h1, p["norm2_w"], w_gate, w_up)
    (w_down,), _ = fetch(["w_down"], act)
    dh2, dh2_b, loss, d_fnw = down_loss(act, w_down, h1, target, p["final_norm_w"])

    dgt, dup, g_down, g_gate, g_up = swiglu_bwd(dh2_b, w_down, gt, up, act, u2)
    dh1, dh1_b, d_n2 = gate_up_bwd(dgt, dup, w_gate, w_up, h1, p["norm2_w"], dh2)
    sent = send({"w_down": g_down, "w_gate": g_gate, "w_up": g_up, "w_out": weight_grad("dw_out", cat, dh1_b)})
    dyn, dh_out, dg_b, d_lnw = out_proj_bwd(dh1_b, w_out, proj, hseq, p["lru_norm_w"], sent)

    dxl_b, d_lcw, d_lcb, dwa2, d_ba, dwx2, d_bx, d_lam = lru_bwd(dh_out, a, hseq, proj, *lru)
    dz_b, dxbc_b, ddt_b, dpar, d_snw, d_scw, d_scb = ssd_bwd(dyn, proj, dt_raw, ssd_cw, p["ssd_conv_b"], y_pre, h_prev, bias2, alog2, d2,
                                                             p["ssd_norm_w"], sent)
    sent = send({"w_in": in_weight_grad([dz_b, dxbc_b, dg_b, dxl_b], [0, SSD_W, 2576, 2576 + LRU_W], ddt_b, u1)})
    grad_x, d_meta, d_n1, dwa2, dwx2 = in_proj_bwd(dz_b, dg_b, dxl_b, dxbc_b, ddt_b, w_in, w_dt, h0, p["norm1_w"], dh1, sent, [dwa2, dwx2])
    small = {"norm1_w": d_n1, "ssd_conv_b": d_scb, "ssd_dt_bias": dpar[:, 0, :8].reshape(1, 16), "ssd_a_log": dpar[:, 1, :8].reshape(1, 16),
             "ssd_d": dpar[:, 2, :8].reshape(1, 16), "ssd_norm_w": d_snw, "lru_conv_b": d_lcb, "lru_ba": d_ba, "lru_bx": d_bx,
             "lru_lambda": d_lam, "lru_norm_w": d_lnw, "norm2_w": d_n2, "final_norm_w": d_fnw,
             "lru_wa": _unpair_blocks(dwa2), "lru_wx": _unpair_blocks(dwx2), "meta_tokens": d_meta,
             "ssd_conv_w": d_scw, "lru_conv_w": d_lcw}
    return loss, grad_x, small


def _pack_small(small, loss):
    rows = [_pad_cols(small[name], -(-n // 1024) * 1024).reshape(-1, 1024) for name, n in SIMPLE]
    rows += [small["lru_wa"].reshape(64, 1024), small["lru_wx"].reshape(64, 1024), small["meta_tokens"],
             _pad_cols(small["ssd_conv_w"], 2048).reshape(8, 1024), small["lru_conv_w"], _pad_cols(loss[:, 0:1], 1024)]
    sm = jnp.concatenate(rows, axis=0)
    return jnp.pad(sm, ((0, SM_ROWS - sm.shape[0]), (0, 0)))


def _slabs(g):
    return g.reshape(8, g.shape[0] // 8, g.shape[1])


def _unslab(g):
    return g.reshape(8 * g.shape[1], g.shape[2])


def kernel(x, meta_tokens, norm1_w, w_in, ssd_conv_w, ssd_conv_b, ssd_dt_bias, ssd_a_log, ssd_d, ssd_norm_w, lru_conv_w, lru_conv_b, lru_wa, lru_ba, lru_wx, lru_bx, lru_lambda, lru_norm_w, w_out, norm2_w, w_gate, w_up, w_down, final_norm_w, loss_target, m_meta_tokens, m_norm1_w, m_w_in, m_ssd_conv_w, m_ssd_conv_b, m_ssd_dt_bias, m_ssd_a_log, m_ssd_d, m_ssd_norm_w, m_lru_conv_w, m_lru_conv_b, m_lru_wa, m_lru_ba, m_lru_wx, m_lru_bx, m_lru_lambda, m_lru_norm_w, m_w_out, m_norm2_w, m_w_gate, m_w_up, m_w_down, m_final_norm_w, v_meta_tokens, v_norm1_w, v_w_in, v_ssd_conv_w, v_ssd_conv_b, v_ssd_dt_bias, v_ssd_a_log, v_ssd_d, v_ssd_norm_w, v_lru_conv_w, v_lru_conv_b, v_lru_wa, v_lru_ba, v_lru_wx, v_lru_bx, v_lru_lambda, v_lru_norm_w, v_w_out, v_norm2_w, v_w_gate, v_w_up, v_w_down, v_final_norm_w):
    w = dict(meta_tokens=meta_tokens, norm1_w=norm1_w, w_in=w_in[0], ssd_conv_w=ssd_conv_w[0], ssd_conv_b=ssd_conv_b, ssd_dt_bias=ssd_dt_bias,
             ssd_a_log=ssd_a_log, ssd_d=ssd_d, ssd_norm_w=ssd_norm_w, lru_conv_w=lru_conv_w[0], lru_conv_b=lru_conv_b, lru_wa=lru_wa[0],
             lru_ba=lru_ba, lru_wx=lru_wx[0], lru_bx=lru_bx, lru_lambda=lru_lambda, lru_norm_w=lru_norm_w, w_out=w_out[0], norm2_w=norm2_w,
             w_gate=w_gate[0], w_up=w_up[0], w_down=w_down[0], final_norm_w=final_norm_w.reshape(1, D))
    m = dict(meta_tokens=m_meta_tokens, norm1_w=m_norm1_w, w_in=m_w_in[0], ssd_conv_w=m_ssd_conv_w[0], ssd_conv_b=m_ssd_conv_b,
             ssd_dt_bias=m_ssd_dt_bias, ssd_a_log=m_ssd_a_log, ssd_d=m_ssd_d, ssd_norm_w=m_ssd_norm_w, lru_conv_w=m_lru_conv_w[0],
             lru_conv_b=m_lru_conv_b, lru_wa=m_lru_wa[0], lru_ba=m_lru_ba, lru_wx=m_lru_wx[0], lru_bx=m_lru_bx, lru_lambda=m_lru_lambda,
             lru_norm_w=m_lru_norm_w, w_out=m_w_out[0], norm2_w=m_norm2_w, w_gate=m_w_gate[0], w_up=m_w_up[0], w_down=m_w_down[0],
             final_norm_w=m_final_norm_w.reshape(1, D))
    v = dict(meta_tokens=v_meta_tokens, norm1_w=v_norm1_w, w_in=v_w_in[0], ssd_conv_w=v_ssd_conv_w[0], ssd_conv_b=v_ssd_conv_b,
             ssd_dt_bias=v_ssd_dt_bias, ssd_a_log=v_ssd_a_log, ssd_d=v_ssd_d, ssd_norm_w=v_ssd_norm_w, lru_conv_w=v_lru_conv_w[0],
             lru_conv_b=v_lru_conv_b, lru_wa=v_lru_wa[0], lru_ba=v_lru_ba, lru_wx=v_lru_wx[0], lru_bx=v_lru_bx, lru_lambda=v_lru_lambda,
             lru_norm_w=v_lru_norm_w, w_out=v_w_out[0], norm2_w=v_norm2_w, w_gate=v_w_gate[0], w_up=v_w_up[0], w_down=v_w_down[0],
             final_norm_w=v_final_norm_w.reshape(1, D))
    shapes = dict(meta_tokens=meta_tokens.shape, norm1_w=norm1_w.shape, w_in=w_in.shape, ssd_conv_w=ssd_conv_w.shape,
                  ssd_conv_b=ssd_conv_b.shape, ssd_dt_bias=ssd_dt_bias.shape, ssd_a_log=ssd_a_log.shape, ssd_d=ssd_d.shape,
                  ssd_norm_w=ssd_norm_w.shape, lru_conv_w=lru_conv_w.shape, lru_conv_b=lru_conv_b.shape, lru_wa=lru_wa.shape,
                  lru_ba=lru_ba.shape, lru_wx=lru_wx.shape, lru_bx=lru_bx.shape, lru_lambda=lru_lambda.shape, lru_norm_w=lru_norm_w.shape,
                  w_out=w_out.shape, norm2_w=norm2_w.shape, w_gate=w_gate.shape, w_up=w_up.shape, w_down=w_down.shape,
                  final_norm_w=final_norm_w.shape)
    me = _index(*_place())
    for n in COLUMN_SHARDED:
        w[n], m[n], v[n] = w[n].T, m[n].T, v[n].T

    small_shard = jnp.concatenate([w["meta_tokens"], _pad_cols(w["ssd_conv_w"], 256).reshape(8, 128), w["lru_conv_w"],
                                   jnp.zeros((4, 128), F32)], axis=0)
    g_in, gs = all_gather("gather_w_in", [w["w_in"].astype(BF), small_shard])
    later = ["w_out", "w_gate", "w_up", "w_down"]
    started, (g_in, gs) = gather_start("gather_rest_start", [w[n].astype(BF) for n in later], [g_in, gs])
    started = dict(zip(later, started))
    ssd_cw = gs[:, 16:24].reshape(8, 4, 256)[:, :, :192].transpose(1, 0, 2).reshape(4, XBC)
    lru_cw = gs[:, 24:28].transpose(1, 0, 2).reshape(4, LRU_W)

    def fetch(names, after):
        pass_on = {"w_out": ["w_down"], "w_gate": [], "w_down": []}[names[0]] if names else ["w_out", "w_gate", "w_up"]
        got, zero = gather_stage("gather_" + (names[0] + "_wait" if names else "pass_on"), [started[n] for n in pass_on],
                                 [started[n] for n in names], after)
        return [_unslab(g) for g in got], zero

    in_flight = {}

    def send(grads):
        names = list(grads)
        st, zero = copies_start("grads_" + names[0] + "_start", [grads[n] if n == "small" else _slabs(grads[n]) for n in names])
        in_flight.update(zip(names, st))
        return zero

    loss, grad_x, small = local_step(x[0], loss_target[0], gs, ssd_cw, lru_cw, g_in, fetch, send, w)
    send({"small": _pack_small(small, loss).reshape(8, SM_ROWS // 8, 1024)})

    out = {}
    early = ["w_down", "w_gate", "w_up", "w_out"]
    recv = dict(zip(early, copies_wait("grads_early_wait", [in_flight[n] for n in early], [in_flight["small"][2]])))
    for pair in (early[:2], early[2:]):
        done = adamw_shards("adamw_" + pair[0], [recv[n] for n in pair], [w[n] for n in pair], [m[n] for n in pair], [v[n] for n in pair])
        out.update(zip(pair, done))
    recv_in, recv_small = copies_wait("grads_late_wait", [in_flight["w_in"], in_flight["small"]], [out[n][0] for n in early])
    def lines(a):
        return jnp.transpose(a.reshape(D // 128, 128, IN_COLS // 8), (2, 0, 1))

    gathering, zero = copies_start("gather_small_start", [sum_slabs(recv_small)])
    updated = adamw_w_in(recv_in, lines(w_in), lines(m_w_in), lines(v_w_in), zero)
    out["w_in"] = [jnp.transpose(o, (1, 2, 0)).reshape(D, IN_COLS // 8) for o in updated]
    for n in ("w_gate", "w_up"):
        out[n] = [o.T for o in out[n]]
    sm = copies_wait("gather_small_wait", gathering, [updated[0]])[0].reshape(SM_ROWS, 1024)
    special_g =[sm[SM_WA:SM_WA + 64].reshape(16, 64, 64), sm[SM_WX:SM_WX + 64].reshape(16, 64, 64),
                 lax.dynamic_slice(sm[SM_META:SM_META + 16], (0, 128 * me), (16, 128)),
                 lax.dynamic_slice(sm[SM_SCW:SM_SCW + 8].reshape(4, 2048), (0, 192 * me), (4, 192)),
                 lax.dynamic_slice(sm[SM_LCW:SM_LCW + 4], (0, 128 * me), (4, 128))]
    names = [n for n, _ in SIMPLE] + SPECIAL
    res = adamw_small(sm, special_g, [w[n] for n in names], [m[n] for n in names], [v[n] for n in names])
    for k, (n, _) in enumerate(SIMPLE):
        out[n] = res[4 * k:4 * k + 4]
    for k, n in enumerate(SPECIAL):
        o = 4 * len(SIMPLE) + 3 * k
        out[n] = [special_g[k]] + list(res[o:o + 3])
    loss_total = sm[SM_LOSS, 0]
    flat = [loss_total, grad_x[None]]
    for k in range(4):
        flat += [out[n][k].reshape(shapes[n]) for n in WEIGHTS]
    return tuple(flat)
```

```python
import math

import jax
import jax.numpy as jnp
from jax import lax
from jax.experimental import pallas as pl
from jax.experimental.pallas import tpu as pltpu

F32 = jnp.float32
BF = jnp.bfloat16

D = 1024
SEQ = 2048
N_META = 16
Q = 128
NPAD = 112
T = NPAD + N_META + SEQ
NCH = T // Q
RC = 544
D_FF = 2816
SSD_W = 1024
LRU_W = 1024
XBC = 1536
IN_COLS = 4624
PZ, PG, PXL, PXBC = 0, 1024, 2048, 3072
NP_IN = 4608
EPS = 1e-6
LRU_C = 8.0
VMEM_LIMIT = 56 * 1024 * 1024

ADAM_LR, ADAM_B1, ADAM_B2, ADAM_EPS, ADAM_WD, ADAM_STEP = 0.001, 0.9, 0.999, 1e-08, 0.01, 10

NT_DIMS = (((1,), (1,)), ((), ()))
TN_DIMS = (((0,), (0,)), ((), ()))
MESH = pl.DeviceIdType.MESH


def _params(n_grid=1, limit=VMEM_LIMIT):
    return pltpu.CompilerParams(dimension_semantics=("arbitrary",) * n_grid, vmem_limit_bytes=limit)


def _spec(shape, imap, single=False):
    if single:
        return pl.BlockSpec(shape, imap, pipeline_mode=pl.Buffered(1))
    return pl.BlockSpec(shape, imap)


def _sigmoid(x):
    return 0.5 * jnp.tanh(0.5 * x) + 0.5


def _sigmoid_gate(x):
    return 1.0 / (1.0 + jnp.exp(-x))


def _softplus(x):
    return jnp.maximum(x, 0.0) + jnp.log(1.0 + jnp.exp(-jnp.abs(x)))


def _rms_stats(h):
    return lax.rsqrt(jnp.mean(h * h, axis=-1, keepdims=True) + EPS)


def _rms(h, w):
    return (h * _rms_stats(h)) * w


def _rms_bwd(du, h, w):
    r = _rms_stats(h)
    n = h * r
    dn = du * w
    dh = r * (dn - n * jnp.mean(dn * n, axis=-1, keepdims=True))
    return dh, du * n


_G0 = math.sqrt(2.0 / math.pi)


def _gelu(x):
    return 0.5 * x * (1.0 + jnp.tanh(_G0 * (x + 0.044715 * (x * x * x))))


def _gelu_grad(x):
    t = jnp.tanh(_G0 * (x + 0.044715 * (x * x * x)))
    return 0.5 * (1.0 + t) + 0.5 * x * (1.0 - t * t) * (_G0 * (1.0 + 3.0 * 0.044715 * (x * x)))


def _rows(shape, r0=0):
    return lax.broadcasted_iota(jnp.int32, shape, 0) + r0


def _lanes(shape):
    return lax.broadcasted_iota(jnp.int32, shape, 1)


HALO = 8


def _fill_padded(pad_ref, x_ref):
    pad_ref[0:HALO, :] = jnp.zeros((HALO, pad_ref.shape[1]), F32)
    pad_ref[T + HALO:T + 2 * HALO, :] = jnp.zeros((HALO, pad_ref.shape[1]), F32)

    def step(c, carry):
        r0 = pl.multiple_of(c * Q, Q)
        pad_ref[pl.ds(r0 + HALO, Q), :] = x_ref[pl.ds(r0, Q), :].astype(F32)
        return carry

    lax.fori_loop(0, NCH, step, 0)


def _back(pad_ref, r0):
    win = pad_ref[pl.ds(r0, Q + HALO), :]
    return lambda s: win[HALO:, :] if s == 0 else pltpu.roll(win, s, axis=0)[HALO:, :]


def _ahead(pad_ref, r0):
    win = pad_ref[pl.ds(r0 + HALO, Q + HALO), :]
    return lambda s: win[:Q, :] if s == 0 else pltpu.roll(win, Q + HALO - s, axis=0)[:Q, :]


def _conv(back, w, b):
    y = b + w[3:4, :] * back(0)
    for k in range(3):
        y = y + w[k:k + 1, :] * back(3 - k)
    return y


def _conv_bwd_x(ahead, w):
    dx = w[3:4, :] * ahead(0)
    for k in range(3):
        dx = dx + w[k:k + 1, :] * ahead(3 - k)
    return dx


def _conv_bwd_w(dy, back):
    dws = [jnp.sum(dy * back(3 - k), axis=0, keepdims=True) for k in range(4)]
    return jnp.concatenate(dws, axis=0), jnp.sum(dy, axis=0, keepdims=True)


def _chunks(fn, unrolled=False):
    if unrolled:
        for c in range(NCH):
            fn(c * Q)
        return

    def step(c, carry):
        fn(pl.multiple_of(c * Q, Q))
        return carry

    lax.fori_loop(0, NCH, step, 0)


HALF = RC // 2


def _col_tiles(n, tn, fn):
    def step(j, carry):
        fn(pl.multiple_of(j * tn, tn))
        return carry

    lax.fori_loop(0, n // tn, step, 0)


def _rows_spec(cols, block_col=0):
    return _spec((RC, cols), lambda i: (i, block_col))


def _whole(shape):
    return _spec(shape, lambda i: tuple(0 for _ in shape), single=True)


def _vec(cols):
    return _spec((1, cols), lambda i: (0, 0))


def _zero_at_first(*refs):
    @pl.when(pl.program_id(0) == 0)
    def _():
        for r in refs:
            r[...] = jnp.zeros_like(r)


ANY_SPEC = pl.BlockSpec(memory_space=pl.ANY)


def _arriving(src, dst, sems, starts, rows):
    n, ahead = len(starts), 2
    first = pl.program_id(0) == 0

    def piece(k):
        r0 = starts[0]
        for j in range(1, n):
            r0 = jnp.where(k == j, starts[j], r0)
        at = pl.ds(pl.multiple_of(r0, 16), rows)
        return pltpu.make_async_copy(src.at[at], dst.at[at], sems.at[k])

    @pl.when(first)
    def _():
        for k in range(min(ahead, n)):
            piece(k).start()

    def ready(k):
        k = jnp.asarray(k, jnp.int32)

        @pl.when(first)
        def _():
            piece(k).wait()

            @pl.when(k + ahead < n)
            def _():
                piece(k + ahead).start()

    return ready


IN_RUNS = ((PZ, 0, 1024), (PXBC, 1024, XBC), (PG, 2576, 2048))
IN_TILE = 512
IN_TILE_ROWS = [wrow + IN_TILE * j for _, wrow, width in IN_RUNS for j in range(width // IN_TILE)]


def _in_tiles(fn, before_run=lambda run: None):
    done = 0
    for run, (pcol, wrow, width) in enumerate(IN_RUNS):
        before_run(run)
        def step(j, carry, pcol=pcol, wrow=wrow, done=done):
            fn(pl.multiple_of(pcol + j * IN_TILE, IN_TILE), pl.multiple_of(wrow + j * IN_TILE, 16), done + j)
            return carry

        lax.fori_loop(0, width // IN_TILE, step, 0)
        done += width // IN_TILE


def in_proj(x, meta, wn, w_shards):
    first = NPAD + N_META
    steps = T // RC
    shard = IN_COLS // 8

    def body(x_hbm, meta_ref, wn_ref, g_hbm, o_ref, dt_ref, u_ref, h_ref, wt_hbm, wdt_ref, raw, w_ref, h_scr, g_sems, h_sems, out_sem):
        i = pl.program_id(0)
        slot = i % 2
        shards = [pltpu.make_async_copy(g_hbm.at[j], raw.at[j], g_sems.at[j]) for j in range(8)]
        head = pltpu.make_async_copy(x_hbm.at[pl.ds(0, RC - first)], h_scr.at[0, pl.ds(first, RC - first)], h_sems.at[0])
        put_back = pltpu.make_async_copy(w_ref, wt_hbm, out_sem)

        def rows_of(step):
            return pltpu.make_async_copy(x_hbm.at[pl.ds(pl.multiple_of(step * RC - first, 32), RC)], h_scr.at[step % 2], h_sems.at[step % 2])

        @pl.when(i == 0)
        def _():
            for cp in shards:
                cp.start()
            head.start()
            h_scr[0, 0:NPAD, :] = jnp.zeros((NPAD, D), F32)
            for j in range(8):
                h_scr[0, NPAD:first, 128 * j:128 * j + 128] = meta_ref[j, 0:N_META, :]

        @pl.when(i + 1 < steps)
        def _():
            rows_of(i + 1).start()

        @pl.when(i == 0)
        def _():
            head.wait()

        @pl.when(i > 0)
        def _():
            rows_of(i).wait()

        h_ref[...] = h_scr[slot]
        for r in (0, HALF):
            u_ref[r:r + HALF, :] = _rms(h_scr[slot, r:r + HALF, :], wn_ref[...]).astype(BF)

        def place_shards(run):
            @pl.when(i == 0)
            def _():
                for j in ((0, 1), (2, 3, 4), (5, 6, 7))[run]:
                    shards[j].wait()
                    w_ref[shard * j:shard * (j + 1), :] = raw[j]
                if run == 1:
                    wdt_ref[...] = jnp.zeros_like(wdt_ref)
                    for g in range(2):
                        wdt_ref[128 * g:128 * g + 8, :] = w_ref[2560 + 8 * g:2568 + 8 * g, :]
                if run == 2:
                    put_back.start()

        def tile(pcol, wrow, k):
            o_ref[:, pl.ds(pcol, IN_TILE)] = lax.dot_general(u_ref[...], w_ref[pl.ds(wrow, IN_TILE), :], NT_DIMS,
                                                             preferred_element_type=F32).astype(BF)

        _in_tiles(tile, place_shards)
        dt_ref[...] = lax.dot_general(u_ref[...], wdt_ref[...], NT_DIMS, preferred_element_type=F32)

        @pl.when(i == steps - 1)
        def _():
            put_back.wait()

    return pl.pallas_call(
        body, grid=(steps,), in_specs=[ANY_SPEC, _spec(meta.shape, lambda i: (0, 0, 0)), _vec(D), ANY_SPEC],
        out_specs=[_rows_spec(NP_IN), _rows_spec(256), _rows_spec(D), _rows_spec(D), ANY_SPEC, _spec((256, D), lambda i: (0, 0))],
        out_shape=[jax.ShapeDtypeStruct((T, NP_IN), BF), jax.ShapeDtypeStruct((T, 256), F32), jax.ShapeDtypeStruct((T, D), BF),
                   jax.ShapeDtypeStruct((T, D), F32), jax.ShapeDtypeStruct((IN_COLS, D), BF), jax.ShapeDtypeStruct((256, D), BF)],
        scratch_shapes=[pltpu.VMEM((8, shard, D), BF), pltpu.VMEM((IN_COLS, D), BF), pltpu.VMEM((2, RC, D), F32),
                        pltpu.SemaphoreType.DMA((8,)), pltpu.SemaphoreType.DMA((2,)), pltpu.SemaphoreType.DMA],
        compiler_params=_params(), name="in_proj")(x, meta, wn, w_shards)


def out_proj(yn_ssd, proj, hseq, lru_nw, w_out, h0):
    def body(y_ref, g_ref, h_ref, wn_ref, w_ref, r_ref, o_ref, cat_ref):
        cat_ref[:, 0:SSD_W] = y_ref[...]
        for r in (0, HALF):
            y = _gelu(g_ref[r:r + HALF, :].astype(F32)) * h_ref[r:r + HALF, :]
            cat_ref[r:r + HALF, SSD_W:] = _rms(y, wn_ref[...]).astype(BF)

        def tile(c0):
            o_ref[:, pl.ds(c0, 512)] = r_ref[:, pl.ds(c0, 512)] + jnp.dot(cat_ref[...], w_ref[:, pl.ds(c0, 512)], preferred_element_type=F32)

        _col_tiles(D, 512, tile)

    return pl.pallas_call(
        body, grid=(T // RC,),
        in_specs=[_rows_spec(SSD_W), _rows_spec(LRU_W, PG // LRU_W), _rows_spec(LRU_W), _vec(LRU_W), _whole((SSD_W + LRU_W, D)), _rows_spec(D)],
        out_specs=[_rows_spec(D), _rows_spec(SSD_W + LRU_W)],
        out_shape=[jax.ShapeDtypeStruct((T, D), F32), jax.ShapeDtypeStruct((T, SSD_W + LRU_W), BF)],
        compiler_params=_params(), name="out_proj")(yn_ssd, proj, hseq, lru_nw, w_out, h0)


def out_proj_bwd(dh1_b, w_out, proj, hseq, lru_nw, after):
    def body(d_ref, w_ref, g_ref, h_ref, wn_ref, _after, dy_ref, dh_ref, dg_ref, dw_ref, dl_scr):
        _zero_at_first(dw_ref)

        def tile(c0):
            dy_ref[:, pl.ds(c0, 512)] = lax.dot_general(d_ref[...], w_ref[pl.ds(c0, 512), :], NT_DIMS, preferred_element_type=F32)
            dl_scr[:, pl.ds(c0, 512)] = lax.dot_general(d_ref[...], w_ref[pl.ds(SSD_W + c0, 512), :], NT_DIMS, preferred_element_type=F32)

        _col_tiles(SSD_W, 512, tile)

        for r in (0, HALF):
            g = g_ref[r:r + HALF, :].astype(F32)
            h = h_ref[r:r + HALF, :]
            ge = _gelu(g)
            dy, dw = _rms_bwd(dl_scr[r:r + HALF, :], ge * h, wn_ref[...])
            dw_ref[...] += jnp.sum(dw, axis=0, keepdims=True)
            dh_ref[r:r + HALF, :] = dy * ge
            dg_ref[r:r + HALF, :] = (dy * h * _gelu_grad(g)).astype(BF)

    return pl.pallas_call(
        body, grid=(T // RC,),
        in_specs=[_rows_spec(D), _whole((SSD_W + LRU_W, D)), _rows_spec(LRU_W, PG // LRU_W), _rows_spec(LRU_W), _vec(LRU_W), ANY_SPEC],
        out_specs=[_rows_spec(SSD_W), _rows_spec(LRU_W), _rows_spec(LRU_W), _vec(LRU_W)],
        out_shape=[jax.ShapeDtypeStruct((T, SSD_W), F32), jax.ShapeDtypeStruct((T, LRU_W), F32), jax.ShapeDtypeStruct((T, LRU_W), BF),
                   jax.ShapeDtypeStruct((1, LRU_W), F32)],
        scratch_shapes=[pltpu.VMEM((RC, LRU_W), F32)],
        compiler_params=_params(), name="out_proj_bwd")(dh1_b, w_out, proj, hseq, lru_nw, after)


def in_proj_bwd(dz, dg, dxl, dxbc, ddt, w_t, w_dt, h0, wn, dh1, after, through):
    first = NPAD + N_META

    def body(dz_ref, dg_ref, dxl_ref, dxbc_ref, ddt_ref, w_hbm, wdt_ref, h_ref, wn_ref, r_ref, _after, _in0, _in1, gx_hbm, meta_ref, dw_ref,
             _out0, _out1, du_scr, o_ref, sem, w_ref, w_sems):
        i = pl.program_id(0)
        ready = _arriving(w_hbm, w_ref, w_sems, IN_TILE_ROWS, IN_TILE)
        _zero_at_first(dw_ref)
        du_scr[...] = jnp.dot(ddt_ref[...], wdt_ref[...], preferred_element_type=F32)
        done = 0
        for d_ref, wrow, width in ((dz_ref, 0, 1024), (dxbc_ref, 1024, XBC), (dg_ref, 2576, 1024), (dxl_ref, 3600, 1024)):
            def step(j, carry, d_ref=d_ref, wrow=wrow, done=done):
                c0 = pl.multiple_of(j * IN_TILE, IN_TILE)
                ready(done + j)
                du_scr[...] += jnp.dot(d_ref[:, pl.ds(c0, IN_TILE)], w_ref[pl.ds(pl.multiple_of(wrow + c0, 16), IN_TILE), :],
                                       preferred_element_type=F32)
                return carry

            lax.fori_loop(0, width // IN_TILE, step, 0)
            done += width // IN_TILE
        for r in (0, HALF):
            dh, dw = _rms_bwd(du_scr[r:r + HALF, :], h_ref[r:r + HALF, :], wn_ref[...])
            dw_ref[...] += jnp.sum(dw, axis=0, keepdims=True)
            o_ref[r:r + HALF, :] = dh + r_ref[r:r + HALF, :]

        @pl.when(i == 0)
        def _():
            meta_ref[...] = o_ref[NPAD:first, :]
            head = pltpu.make_async_copy(o_ref.at[pl.ds(first, RC - first)], gx_hbm.at[pl.ds(0, RC - first)], sem)
            head.start()
            head.wait()

        @pl.when(i > 0)
        def _():
            rest = pltpu.make_async_copy(o_ref, gx_hbm.at[pl.ds(pl.multiple_of(i * RC - first, 32), RC)], sem)
            rest.start()
            rest.wait()

    return pl.pallas_call(
        body, grid=(T // RC,),
        in_specs=[_rows_spec(SSD_W), _rows_spec(LRU_W), _rows_spec(LRU_W), _rows_spec(XBC), _rows_spec(256), ANY_SPEC,
                  _whole((256, D)), _rows_spec(D), _vec(D), _rows_spec(D), ANY_SPEC, ANY_SPEC, ANY_SPEC],
        out_specs=[ANY_SPEC, _spec((N_META, D), lambda i: (0, 0)), _vec(D), ANY_SPEC, ANY_SPEC],
        out_shape=[jax.ShapeDtypeStruct((SEQ, D), F32), jax.ShapeDtypeStruct((N_META, D), F32), jax.ShapeDtypeStruct((1, D), F32)]
        + [jax.ShapeDtypeStruct(t.shape, t.dtype) for t in through],
        scratch_shapes=[pltpu.VMEM((RC, D), F32), pltpu.VMEM((RC, D), F32), pltpu.SemaphoreType.DMA,
                        pltpu.VMEM((IN_COLS, D), BF), pltpu.SemaphoreType.DMA((len(IN_TILE_ROWS),))],
        input_output_aliases={11: 3, 12: 4},
        compiler_params=_params(), name="in_proj_bwd")(dz, dg, dxl, dxbc, ddt, w_t, w_dt, h0, wn, dh1, after, *through)


GRAD_TILE = 256


def weight_grad(name, a, u1):
    tm = GRAD_TILE

    def body(a_ref, u_ref, o_ref):
        o_ref[...] = lax.dot_general(a_ref[...], u_ref[...], TN_DIMS, preferred_element_type=F32).astype(BF)

    return pl.pallas_call(
        body, grid=(a.shape[1] // tm,),
        in_specs=[_spec((T, tm), lambda j: (0, j)), _spec((T, D), lambda j: (0, 0), single=True)],
        out_specs=_spec((tm, D), lambda j: (j, 0)),
        out_shape=jax.ShapeDtypeStruct((a.shape[1], D), BF),
        compiler_params=_params(), name=name)(a, u1)


def in_weight_grad(parts, first_rows, ddt, u1):
    tm = GRAD_TILE
    per_row = D // 128
    dt_row, dt_lines = 2560, 8 * per_row
    parts = list(parts) + [ddt]
    first_rows = list(first_rows) + [dt_row]
    tiles = [p.shape[1] // tm for p in parts]
    starts = [sum(tiles[:k]) for k in range(len(parts))]
    last = sum(tiles) - 1

    def body(*refs):
        a_refs, u_ref = refs[:len(parts)], refs[len(parts)]
        o_hbm, mix_scr, stage, sems = refs[len(parts) + 1:]
        step = pl.program_id(0)
        slot = step % 2
        line0 = 0
        for a_ref, start, n, first in zip(a_refs, starts, tiles, first_rows):
            here = (step >= start) & (step < start + n)
            line0 = jnp.where(here, per_row * (first + tm * (step - start)), line0)

            @pl.when(here)
            def _(a_ref=a_ref):
                res = lax.dot_general(a_ref[...], u_ref[...], TN_DIMS, preferred_element_type=F32)
                for q in range(per_row):
                    mix_scr[pl.ds(q, tm, stride=per_row), :] = res[:, 128 * q:128 * q + 128]

        def tile_copy(of_slot, to):
            return pltpu.make_async_copy(stage.at[of_slot], o_hbm.at[pl.ds(to, per_row * tm)], sems.at[of_slot])

        @pl.when(step >= 2)
        def _():
            tile_copy(slot, 0).wait()

        stage[slot] = mix_scr[...].astype(BF)

        @pl.when(step < last)
        def _():
            tile_copy(slot, pl.multiple_of(line0, 128)).start()

        @pl.when(step == last)
        def _():
            halves = [pltpu.make_async_copy(stage.at[slot, pl.ds(128 * per_row * k, dt_lines)],
                                            o_hbm.at[pl.ds(per_row * (dt_row + 8 * k), dt_lines)], sems.at[2 + k]) for k in range(2)]
            for cp in halves:
                cp.start()
            tile_copy(1 - slot, 0).wait()
            for cp in halves:
                cp.wait()

    def tile_of(start, n):
        return lambda j: (0, jnp.clip(j - start, 0, n - 1))

    return pl.pallas_call(
        body, grid=(last + 1,),
        in_specs=[_spec((T, tm), tile_of(s, n)) for s, n in zip(starts, tiles)] + [_spec((T, D), lambda j: (0, 0), single=True)],
        out_specs=ANY_SPEC,
        out_shape=jax.ShapeDtypeStruct((per_row * IN_COLS, 128), BF),
        scratch_shapes=[pltpu.VMEM((per_row * tm, 128), F32), pltpu.VMEM((2, per_row * tm, 128), BF), pltpu.SemaphoreType.DMA((4,))],
        compiler_params=_params(), name="dw_in")(*parts, u1)


def _ssd_chunk_common(row0, dt_ref, b_ref, c_ref, bias, a_neg):
    shape = (Q, Q)
    lane = _lanes(shape)
    sub = _rows(shape)
    live = (_rows(shape, row0) >= NPAD) & (lane < 8)
    dtr = dt_ref[:, :]
    dt = jnp.where(live, _softplus(dtr + bias), 0.0)
    d_a = dt * a_neg
    tri = (sub >= lane).astype(F32)
    cs = jnp.dot(tri, d_a, precision=lax.Precision.HIGHEST, preferred_element_type=F32)
    cs_t = cs.T
    b_f = b_ref[:, :]
    bc = b_f.astype(BF)
    cc = c_ref[:, :].astype(BF)
    cb = lax.dot_general(cc, bc, NT_DIMS, preferred_element_type=F32)
    cs_last = cs[Q - 1:Q, :]
    return dict(lane=lane, sub=sub, live=live, dtr=dtr, dt=dt, cs=cs, cs_t=cs_t, bc=bc, cc=cc, cb=cb, bc_t=b_f.T.astype(BF),
                ecs=jnp.exp(cs), dsm=jnp.exp(cs_last - cs), gam=jnp.exp(cs_last))


def _pair(lane_even, mat, j):
    return jnp.where(lane_even, mat[:, j:j + 1], mat[:, j + 1:j + 2])


def _pair_row(lane_even, mat, j):
    return jnp.where(lane_even[0:1, :], mat[:, j:j + 1], mat[:, j + 1:j + 2])


def _head_decay(cm, j):
    seg = cm["cs"][:, j:j + 1] - cm["cs_t"][j:j + 1, :]
    return jnp.exp(jnp.where(cm["sub"] >= cm["lane"], seg, -jnp.inf))


def _head_decay_t(cm, j):
    seg = cm["cs_t"][j:j + 1, :] - cm["cs"][:, j:j + 1]
    return jnp.exp(jnp.where(cm["lane"] >= cm["sub"], seg, -jnp.inf))


def _conv_window(raw_ref, halo_ref, pad_scr):
    pad_scr[0:HALO, :] = halo_ref[...].astype(F32)[halo_ref.shape[0] - HALO:, :]
    pad_scr[HALO:HALO + Q, :] = raw_ref[...].astype(F32)
    win = pad_scr[...]
    return lambda s: win[HALO:, :] if s == 0 else pltpu.roll(win, s, axis=0)[HALO:, :]


def _xbc_cols(g):
    return slice(512 * g, 512 * g + 512), slice(SSD_W + 128 * g, SSD_W + 128 * g + 128), slice(SSD_W + 256 + 128 * g, SSD_W + 384 + 128 * g)


def ssd_fwd(proj, dt_raw, conv_w, conv_b, dt_bias2, a_log2, d2, norm_w):
    def body(raw_ref, halo_ref, dt_all, z_all, cw_ref, cb_ref, bias_all, alog_all, d_all, nw_all, yn_all, y_all, hp_all,
             h_all, pad_scr, act_scr):
        @pl.when(pl.program_id(0) == 0)
        def _():
            h_all[...] = jnp.zeros_like(h_all)

        pre = _conv(_conv_window(raw_ref, halo_ref, pad_scr), cw_ref[...], cb_ref[...])
        act_scr[...] = pre * _sigmoid(pre)
        for g in range(2):
            wide, thin = slice(512 * g, 512 * g + 512), slice(128 * g, 128 * g + 128)
            xs, bs, cs = _xbc_cols(g)
            group(act_scr.at[:, xs], act_scr.at[:, bs], act_scr.at[:, cs], dt_all.at[:, thin], z_all.at[:, wide], bias_all.at[g],
                  alog_all.at[g], d_all.at[g], nw_all.at[:, wide], yn_all.at[:, wide], y_all.at[:, wide], hp_all.at[g, 0], h_all.at[g])

    def group(x_ref, b_ref, c_ref, dt_ref, z_ref, bias_ref, alog_ref, d_ref, nw_ref, yn_ref, y_ref, hp_ref, h_scr):
        bias = bias_ref[...]
        a_neg = -jnp.exp(alog_ref[...])
        dsk = d_ref[...]
        cm = _ssd_chunk_common(pl.program_id(0) * Q, dt_ref, b_ref, c_ref, bias, a_neg)
        lane_even = cm["lane"] < 64
        for p in range(4):
            je, jo = 2 * p, 2 * p + 1
            xp = x_ref[:, 128 * p:128 * p + 128]
            xdt = xp * _pair(lane_even, cm["dt"], je)
            xdt_b = xdt.astype(BF)
            m_e = (cm["cb"] * _head_decay(cm, je)).astype(BF)
            m_o = (cm["cb"] * _head_decay(cm, jo)).astype(BF)
            zero = jnp.zeros_like(xdt_b)
            yd = (jnp.dot(m_e, jnp.where(lane_even, xdt_b, zero), preferred_element_type=F32)
                  + jnp.dot(m_o, jnp.where(lane_even, zero, xdt_b), preferred_element_type=F32))
            hp = h_scr[p]
            hp_ref[p] = hp
            yo = jnp.dot(cm["cc"], hp.astype(BF), preferred_element_type=F32) * _pair(lane_even, cm["ecs"], je)
            y_ref[:, 128 * p:128 * p + 128] = yd + yo + xp * _pair_row(lane_even, dsk, je)
            st = jnp.dot(cm["bc_t"], (xdt * _pair(lane_even, cm["dsm"], je)).astype(BF), preferred_element_type=F32)
            h_scr[p] = hp * _pair_row(lane_even, cm["gam"], je) + st
        zc = z_ref[:, :].astype(F32)
        gated = y_ref[:, :] * (zc * _sigmoid(zc))
        yn_ref[:, :] = _rms(gated, nw_ref[...]).astype(BF)

    par = _spec((2, 1, 128), lambda c: (0, 0, 0))
    wide = _spec((Q, SSD_W), lambda c: (c, 0))
    xbc = PXBC // XBC
    halo = 2 * HALO
    return pl.pallas_call(
        body, grid=(NCH,),
        in_specs=[_spec((Q, XBC), lambda c: (c, xbc)), _spec((halo, XBC), lambda c: (jnp.maximum(c * (Q // halo) - 1, 0), xbc)),
                  _spec((Q, 256), lambda c: (c, 0)), wide, _spec((4, XBC), lambda c: (0, 0)), _spec((1, XBC), lambda c: (0, 0)),
                  par, par, par, _spec((1, SSD_W), lambda c: (0, 0))],
        out_specs=[wide, wide, _spec((2, 1, 4, 128, 128), lambda c: (0, c, 0, 0, 0))],
        out_shape=[jax.ShapeDtypeStruct((T, SSD_W), BF), jax.ShapeDtypeStruct((T, SSD_W), F32),
                   jax.ShapeDtypeStruct((2, NCH, 4, 128, 128), F32)],
        scratch_shapes=[pltpu.VMEM((2, 4, 128, 128), F32), pltpu.VMEM((Q + HALO, XBC), F32), pltpu.VMEM((Q, XBC), F32)],
        compiler_params=_params(), name="ssd_fwd")(proj, proj, dt_raw, proj, conv_w, conv_b, dt_bias2, a_log2, d2, norm_w)


def ssd_bwd(dyn, proj, dt_raw, conv_w, conv_b, y_pre, h_prev, dt_bias2, a_log2, d2, norm_w, after):
    def body(dyn_all, raw_ref, halo_ref, dt_all, z_all, y_all, hp_all, cw_ref, cb_ref, bias_all, alog_all, d_all, nw_all, _after,
             dz_all, dxbc_ref, ddt_all, dpar_all, dnw_all, dcw_ref, dcb_ref, dh_all, acc_all, pad_scr, act_scr, dsilu_scr, dact_scr, dpad_scr):
        @pl.when(pl.program_id(0) == 0)
        def _():
            dh_all[...] = jnp.zeros_like(dh_all)
            acc_all[...] = jnp.zeros_like(acc_all)
            dnw_all[...] = jnp.zeros_like(dnw_all)
            dcw_ref[...] = jnp.zeros_like(dcw_ref)
            dcb_ref[...] = jnp.zeros_like(dcb_ref)
            dpad_scr[Q:Q + HALO, :] = jnp.zeros((HALO, XBC), F32)

        back = _conv_window(raw_ref, halo_ref, pad_scr)
        pre = _conv(back, cw_ref[...], cb_ref[...])
        sg = _sigmoid(pre)
        act_scr[...] = pre * sg
        dsilu_scr[...] = sg * (1.0 + pre * (1.0 - sg))
        for g in range(2):
            wide, thin = slice(512 * g, 512 * g + 512), slice(128 * g, 128 * g + 128)
            xs, bs, cs = _xbc_cols(g)
            group(dyn_all.at[:, wide], act_scr.at[:, xs], act_scr.at[:, bs], act_scr.at[:, cs], dt_all.at[:, thin], z_all.at[:, wide],
                  y_all.at[:, wide], hp_all.at[g, 0], bias_all.at[g], alog_all.at[g], d_all.at[g], nw_all.at[:, wide],
                  dz_all.at[:, wide], dact_scr.at[:, xs], dact_scr.at[:, bs], dact_scr.at[:, cs], ddt_all.at[:, thin], dpar_all.at[g],
                  dnw_all.at[:, wide], dh_all.at[g], acc_all.at[g])
        dpre = dact_scr[...] * dsilu_scr[...]
        dcw, dcb = _conv_bwd_w(dpre, back)
        dcw_ref[...] += dcw
        dcb_ref[...] += dcb
        dpad_scr[0:Q, :] = dpre
        win = dpad_scr[...]
        dxbc_ref[...] = _conv_bwd_x(lambda s: win[:Q, :] if s == 0 else pltpu.roll(win, Q + HALO - s, axis=0)[:Q, :], cw_ref[...]).astype(BF)
        dpad_scr[Q:Q + HALO, :] = dpre[0:HALO, :]

    def group(dyn_ref, x_ref, b_ref, c_ref, dt_ref, z_ref, y_ref, hp_ref, bias_ref, alog_ref, d_ref, nw_ref,
              dz_ref, dx_ref, db_ref, dc_ref, ddt_ref, dpar_ref, dnw_ref, dh_scr, acc_scr):
        ci = pl.program_id(0)
        bias = bias_ref[...]
        a_neg = -jnp.exp(alog_ref[...])
        dsk = d_ref[...]
        cm = _ssd_chunk_common((NCH - 1 - ci) * Q, dt_ref, b_ref, c_ref, bias, a_neg)
        lane, sub = cm["lane"], cm["sub"]
        lane_even = lane < 64
        cc_t = c_ref[:, :].T.astype(BF)
        cb_t = lax.dot_general(cm["bc"], cm["cc"], NT_DIMS, preferred_element_type=F32)
        zc = z_ref[:, :].astype(F32)
        yc = y_ref[:, :]
        sg = _sigmoid(zc)
        sz = zc * sg
        dgated, dnw = _rms_bwd(dyn_ref[:, :], yc * sz, nw_ref[...])
        dnw_ref[...] += jnp.sum(dnw, axis=0, keepdims=True)
        dz_ref[:, :] = (dgated * yc * (sg * (1.0 + zc * (1.0 - sg)))).astype(BF)
        dy_all = dgated * sz
        dcb = jnp.zeros((Q, Q), F32)
        dcb_t = jnp.zeros((Q, Q), F32)
        db_acc = jnp.zeros((Q, Q), F32)
        dc_acc = jnp.zeros((Q, Q), F32)
        dcs = jnp.zeros((Q, Q), F32)
        ddt = jnp.zeros((Q, Q), F32)
        for p in range(4):
            je, jo = 2 * p, 2 * p + 1
            xp = x_ref[:, 128 * p:128 * p + 128]
            dy = dy_all[:, 128 * p:128 * p + 128]
            dt_p = _pair(lane_even, cm["dt"], je)
            xdt = xp * dt_p
            xdt_b = xdt.astype(BF)
            dy_b = dy.astype(BF)
            zero = jnp.zeros_like(dy_b)
            hp = hp_ref[p]
            hp_b = hp.astype(BF)
            dh = dh_scr[p]
            dh_b = dh.astype(BF)
            acc_scr[p:p + 1, :] += jnp.sum(dy * xp, axis=0, keepdims=True)
            dxp = dy * _pair_row(lane_even, dsk, je)
            e_p = _pair(lane_even, cm["ecs"], je)
            g_p = jnp.dot(cm["cc"], hp_b, preferred_element_type=F32)
            dg_b = (dy * e_p).astype(BF)
            de = dy * g_p * e_p
            dc_acc = dc_acc + lax.dot_general(dg_b, hp_b, NT_DIMS, preferred_element_type=F32)
            dh_in = jnp.dot(cc_t, dg_b, preferred_element_type=F32)
            ds_p = _pair(lane_even, cm["dsm"], je)
            r_p = jnp.dot(cm["bc"], dh_b, preferred_element_type=F32)
            dxdt = r_p * ds_p
            tt = r_p * xdt * ds_p
            db_acc = db_acc + lax.dot_general((xdt * ds_p).astype(BF), dh_b, NT_DIMS, preferred_element_type=F32)
            dgam_m = jnp.sum(dh * hp, axis=0, keepdims=True)
            for j, even in ((je, True), (jo, False)):
                sel = lane_even if even else jnp.logical_not(lane_even)
                dy_j = jnp.where(sel, dy_b, zero)
                l_j = _head_decay(cm, j)
                l_jt = _head_decay_t(cm, j)
                m_j = cm["cb"] * l_j
                m_jt = cb_t * l_jt
                dm = lax.dot_general(dy_j, xdt_b, NT_DIMS, preferred_element_type=F32)
                dm_t = lax.dot_general(xdt_b, dy_j, NT_DIMS, preferred_element_type=F32)
                dxdt = dxdt + jnp.dot(m_jt.astype(BF), dy_j, preferred_element_type=F32)
                dcb = dcb + dm * l_j
                dcb_t = dcb_t + dm_t * l_jt
                t_j = jnp.where(sel, tt, 0.0)
                col = jnp.sum(dm * m_j - dm_t * m_jt + (jnp.where(sel, de, 0.0) - t_j), axis=1, keepdims=True)
                gam_j = cm["gam"][:, j:j + 1]
                last = (jnp.sum(jnp.sum(t_j, axis=0, keepdims=True), axis=1, keepdims=True)
                        + jnp.sum(jnp.where(sel[0:1, :], dgam_m, 0.0), axis=1, keepdims=True) * gam_j)
                col = col + jnp.where(sub[:, 0:1] == Q - 1, last, 0.0)
                dcs = dcs + jnp.where(lane == j, col, 0.0)
            dh_scr[p] = dh_in + dh * _pair_row(lane_even, cm["gam"], je)
            dx_ref[:, 128 * p:128 * p + 128] = dxp + dxdt * dt_p
            dd = dxdt * xp
            ddt = ddt + jnp.where(lane == je, jnp.sum(jnp.where(lane_even, dd, 0.0), axis=1, keepdims=True), 0.0)
            ddt = ddt + jnp.where(lane == jo, jnp.sum(jnp.where(lane_even, 0.0, dd), axis=1, keepdims=True), 0.0)
        dc_ref[:, :] = dc_acc + jnp.dot(dcb.astype(BF), cm["bc"], preferred_element_type=F32)
        db_ref[:, :] = db_acc + jnp.dot(dcb_t.astype(BF), cm["cc"], preferred_element_type=F32)
        tri_t = (sub <= lane).astype(F32)
        dd_a = jnp.dot(tri_t, dcs, precision=lax.Precision.HIGHEST, preferred_element_type=F32)
        ddt = ddt + dd_a * a_neg
        acc_scr[5:6, :] += jnp.sum(dd_a * cm["dt"], axis=0, keepdims=True)
        draw = jnp.where(cm["live"], ddt * _sigmoid_gate(cm["dtr"] + bias), 0.0)
        acc_scr[4:5, :] += jnp.sum(draw, axis=0, keepdims=True)
        ddt_ref[:, :] = draw.astype(BF)

        @pl.when(ci == NCH - 1)
        def _():
            lane1 = _lanes((1, 128))
            dd = jnp.zeros((1, 128), F32)
            for p in range(4):
                row = acc_scr[p:p + 1, :]
                dd = dd + jnp.where(lane1 == 2 * p, jnp.sum(jnp.where(lane1 < 64, row, 0.0), axis=1, keepdims=True), 0.0)
                dd = dd + jnp.where(lane1 == 2 * p + 1, jnp.sum(jnp.where(lane1 < 64, 0.0, row), axis=1, keepdims=True), 0.0)
            dpar_ref[...] = jnp.concatenate([acc_scr[4:5, :], acc_scr[5:6, :] * a_neg, dd, jnp.zeros((5, 128), F32)], axis=0)

    par = _spec((2, 1, 128), lambda c: (0, 0, 0))
    wide = _spec((Q, SSD_W), lambda c: (NCH - 1 - c, 0))
    thin = _spec((Q, 256), lambda c: (NCH - 1 - c, 0))
    vec = _spec((1, SSD_W), lambda c: (0, 0))
    xbc = PXBC // XBC
    halo = 2 * HALO
    chunk = pltpu.VMEM((Q, XBC), F32)
    padded = pltpu.VMEM((Q + HALO, XBC), F32)
    return pl.pallas_call(
        body, grid=(NCH,),
        in_specs=[wide, _spec((Q, XBC), lambda c: (NCH - 1 - c, xbc)),
                  _spec((halo, XBC), lambda c: (jnp.maximum((NCH - 1 - c) * (Q // halo) - 1, 0), xbc)), thin, wide, wide,
                  _spec((2, 1, 4, 128, 128), lambda c: (0, NCH - 1 - c, 0, 0, 0)), _spec((4, XBC), lambda c: (0, 0)),
                  _spec((1, XBC), lambda c: (0, 0)), par, par, par, vec, ANY_SPEC],
        out_specs=[wide, _spec((Q, XBC), lambda c: (NCH - 1 - c, 0)), thin, _spec((2, 8, 128), lambda c: (0, 0, 0)), vec,
                   _spec((4, XBC), lambda c: (0, 0)), _spec((1, XBC), lambda c: (0, 0))],
        out_shape=[jax.ShapeDtypeStruct((T, SSD_W), BF), jax.ShapeDtypeStruct((T, XBC), BF), jax.ShapeDtypeStruct((T, 256), BF),
                   jax.ShapeDtypeStruct((2, 8, 128), F32), jax.ShapeDtypeStruct((1, SSD_W), F32), jax.ShapeDtypeStruct((4, XBC), F32),
                   jax.ShapeDtypeStruct((1, XBC), F32)],
        scratch_shapes=[pltpu.VMEM((2, 4, 128, 128), F32), pltpu.VMEM((2, 8, 128), F32), padded, chunk, chunk, chunk, padded],
        compiler_params=_params(), name="ssd_bwd")(dyn, proj, proj, dt_raw, proj, y_pre, h_prev, conv_w, conv_b, dt_bias2, a_log2, d2, norm_w, after)


def _lru_gates(back, cw, cb, wa, ba, wx, bx, lam):
    xr = _conv(back, cw, cb)
    xr_b = xr.astype(BF)
    r = _sigmoid_gate(jnp.dot(xr_b, wa, preferred_element_type=F32) + ba)
    i = _sigmoid_gate(jnp.dot(xr_b, wx, preferred_element_type=F32) + bx)
    sp = _softplus(-lam)
    la = (-LRU_C) * r * sp
    a = jnp.exp(la)
    mult2 = -jnp.tanh(la) * (a * a + 1.0)
    return xr, xr_b, r, i, sp, a, jnp.sqrt(mult2), mult2


SEG_LEN = 68
SEGS = T // SEG_LEN


def _seg_rows(j, k, off=0):
    return pl.ds(off + j * 8 * SEG_LEN + k, 8, stride=SEG_LEN)


def _segmented_scan(mul_ref, mul_row0, add_ref, out_ref, loc_scr, prod_scr, carry_scr, reverse):
    groups = SEGS // 8
    off = mul_row0 + (1 if reverse else 0)

    def local(i, carry):
        k = SEG_LEN - 1 - i if reverse else i
        new = []
        for j in range(groups):
            h, p = carry[2 * j], carry[2 * j + 1]
            m = mul_ref[_seg_rows(j, k, off), :]
            h = m * h + add_ref[_seg_rows(j, k), :]
            p = m * p
            loc_scr[_seg_rows(j, k), :] = h
            prod_scr[_seg_rows(j, k), :] = p
            new += [h, p]
        return tuple(new)

    lax.fori_loop(0, SEG_LEN, local, (jnp.zeros((8, 128), F32), jnp.ones((8, 128), F32)) * groups)

    def chain(i, c):
        s = SEGS - 1 - i if reverse else i
        carry_scr[pl.ds(s, 1), :] = c
        edge = s * SEG_LEN + (0 if reverse else SEG_LEN - 1)
        return loc_scr[pl.ds(edge, 1), :] + prod_scr[pl.ds(edge, 1), :] * c

    lax.fori_loop(0, SEGS, chain, jnp.zeros((1, 128), F32))

    def fold(k, carry):
        for j in range(groups):
            rows = _seg_rows(j, k)
            out_ref[rows, :] = loc_scr[rows, :] + prod_scr[rows, :] * carry_scr[8 * j:8 * j + 8, :]
        return carry

    lax.fori_loop(0, SEG_LEN, fold, 0)


def lru_fwd(proj, cw, cb, wa2, ba, wx2, bx, lam, after):
    def body(x_ref, cw_ref, cb_ref, wa_ref, ba_ref, wx_ref, bx_ref, lam_ref, _after, h_ref, a_ref, xpad, u_scr, loc_scr, prod_scr, carry_scr):
        _fill_padded(xpad, x_ref)

        def chunk(r0):
            xr, _, _, i, _, a, mult, _ = _lru_gates(_back(xpad, r0), cw_ref[...], cb_ref[...], wa_ref[0], ba_ref[...], wx_ref[0], bx_ref[...],
                                                 lam_ref[...])
            a_ref[pl.ds(r0, Q), :] = a
            u_scr[pl.ds(r0, Q), :] = jnp.where(_rows(a.shape, r0) >= NPAD, mult * (i * xr), 0.0)

        _chunks(chunk, unrolled=True)
        _segmented_scan(a_ref, 0, u_scr, h_ref, loc_scr, prod_scr, carry_scr, reverse=False)

    c0 = PXL // 128
    vec = _spec((1, 128), lambda c: (0, c))
    mat = _spec((1, 128, 128), lambda c: (c, 0, 0))
    seq = pltpu.VMEM((T, 128), F32)
    return pl.pallas_call(
        body, grid=(8,),
        in_specs=[_spec((T, 128), lambda c: (0, c0 + c)), _spec((4, 128), lambda c: (0, c)), vec, mat, vec, mat, vec, vec, ANY_SPEC],
        out_specs=[_spec((T, 128), lambda c: (0, c)), _spec((T, 128), lambda c: (0, c))],
        out_shape=[jax.ShapeDtypeStruct((T, LRU_W), F32), jax.ShapeDtypeStruct((T, LRU_W), F32)],
        scratch_shapes=[pltpu.VMEM((T + 2 * HALO, 128), F32), seq, seq, seq, pltpu.VMEM((SEGS, 128), F32)],
        compiler_params=_params(), name="lru_fwd")(proj, cw, cb, wa2, ba, wx2, bx, lam, after)


def lru_bwd(dh_out, a, hseq, proj, cw, cb, wa2, ba, wx2, bx, lam):
    def body(d_ref, a_ref, h_ref, x_ref, cw_ref, cb_ref, wa_ref, ba_ref, wx_ref, bx_ref, lam_ref,
             dx_ref, dcw_ref, dcb_ref, dwa_ref, dba_ref, dwx_ref, dbx_ref, dlam_ref, xpad, hpad, dpad, dh_ref, loc_scr, prod_scr, carry_scr):
        _fill_padded(dpad, a_ref)
        _segmented_scan(dpad, HALO, d_ref, dh_ref, loc_scr, prod_scr, carry_scr, reverse=True)
        _fill_padded(xpad, x_ref)
        _fill_padded(hpad, h_ref)
        dpad[0:HALO, :] = jnp.zeros((HALO, 128), F32)
        dpad[T + HALO:T + 2 * HALO, :] = jnp.zeros((HALO, 128), F32)
        for ref in (dcw_ref, dcb_ref, dwa_ref, dba_ref, dwx_ref, dbx_ref, dlam_ref):
            ref[...] = jnp.zeros_like(ref)
        lam = lam_ref[...]

        def first(r0):
            back = _back(xpad, r0)
            xr, xr_b, r, i, sp, a, mult, mult2 = _lru_gates(back, cw_ref[...], cb_ref[...], wa_ref[0], ba_ref[...], wx_ref[0], bx_ref[...], lam)
            dh = dh_ref[pl.ds(r0, Q), :]
            da = dh * _back(hpad, r0)(1)
            du = jnp.where(_rows(dh.shape, r0) >= NPAD, dh, 0.0)
            dmult = du * (i * xr)
            di = du * (mult * xr)
            dxr = du * (mult * i)
            dla = da * a - dmult * (a * a) * lax.rsqrt(mult2)
            dr = dla * ((-LRU_C) * sp)
            dlam_ref[...] += jnp.sum(dla * ((-LRU_C) * r), axis=0, keepdims=True)
            dpr = dr * r * (1.0 - r)
            dpi = di * i * (1.0 - i)
            dba_ref[...] += jnp.sum(dpr, axis=0, keepdims=True)
            dbx_ref[...] += jnp.sum(dpi, axis=0, keepdims=True)
            dpr_b = dpr.astype(BF)
            dpi_b = dpi.astype(BF)
            dxr = (dxr + lax.dot_general(dpr_b, wa_ref[0], NT_DIMS, preferred_element_type=F32)
                   + lax.dot_general(dpi_b, wx_ref[0], NT_DIMS, preferred_element_type=F32))
            dwa_ref[0] += lax.dot_general(xr_b, dpr_b, TN_DIMS, preferred_element_type=F32)
            dwx_ref[0] += lax.dot_general(xr_b, dpi_b, TN_DIMS, preferred_element_type=F32)
            dpad[pl.ds(r0 + HALO, Q), :] = dxr
            dcw, dcb = _conv_bwd_w(dxr, back)
            dcw_ref[...] += dcw
            dcb_ref[...] += dcb

        _chunks(first, unrolled=True)
        dlam_ref[...] = -dlam_ref[...] * _sigmoid_gate(-lam)

        def second(r0):
            dx_ref[pl.ds(r0, Q), :] = _conv_bwd_x(_ahead(dpad, r0), cw_ref[...]).astype(BF)

        _chunks(second)

    c0 = PXL // 128
    vec = _spec((1, 128), lambda c: (0, c))
    mat = _spec((1, 128, 128), lambda c: (c, 0, 0))
    col = _spec((T, 128), lambda c: (0, c))
    vshape = jax.ShapeDtypeStruct((1, LRU_W), F32)
    mshape = jax.ShapeDtypeStruct((8, 128, 128), F32)
    pad = pltpu.VMEM((T + 2 * HALO, 128), F32)
    seq = pltpu.VMEM((T, 128), F32)
    return pl.pallas_call(
        body, grid=(8,),
        in_specs=[col, col, col, _spec((T, 128), lambda c: (0, c0 + c)), _spec((4, 128), lambda c: (0, c)), vec, mat, vec, mat, vec, vec],
        out_specs=[col, _spec((4, 128), lambda c: (0, c)), vec, mat, vec, mat, vec, vec],
        out_shape=[jax.ShapeDtypeStruct((T, LRU_W), BF), jax.ShapeDtypeStruct((4, LRU_W), F32), vshape, mshape, vshape, mshape, vshape, vshape],
        scratch_shapes=[pad, pad, pad, seq, seq, seq, pltpu.VMEM((SEGS, 128), F32)],
        compiler_params=_params(), name="lru_bwd")(dh_out, a, hseq, proj, cw, cb, wa2, ba, wx2, bx, lam)


FF_TILE = 256
FF_TILE_ROWS = list(range(0, D_FF, FF_TILE))


def gate_up(h1, wn, w_gate, w_up):
    def body(h_ref, wn_ref, wg_hbm, wu_hbm, gt_ref, up_ref, act_ref, u_ref, wg_ref, wu_ref, wg_sems, wu_sems):
        gate_ready = _arriving(wg_hbm, wg_ref, wg_sems, FF_TILE_ROWS, FF_TILE)
        up_ready = _arriving(wu_hbm, wu_ref, wu_sems, FF_TILE_ROWS, FF_TILE)
        for r in (0, HALF):
            u_ref[r:r + HALF, :] = _rms(h_ref[r:r + HALF, :], wn_ref[...]).astype(BF)

        def tile(c0):
            cols = pl.ds(c0, FF_TILE)
            gate_ready(c0 // FF_TILE)
            up_ready(c0 // FF_TILE)
            gt = lax.dot_general(u_ref[...], wg_ref[cols, :], NT_DIMS, preferred_element_type=F32)
            up = lax.dot_general(u_ref[...], wu_ref[cols, :], NT_DIMS, preferred_element_type=F32)
            gt_ref[:, cols] = gt.astype(BF)
            up_ref[:, cols] = up.astype(BF)
            act_ref[:, cols] = (gt * _sigmoid(gt) * up).astype(BF)

        _col_tiles(D_FF, FF_TILE, tile)

    big = jax.ShapeDtypeStruct((T, D_FF), BF)
    return pl.pallas_call(
        body, grid=(T // RC,), in_specs=[_rows_spec(D), _vec(D), ANY_SPEC, ANY_SPEC],
        out_specs=[_rows_spec(D_FF), _rows_spec(D_FF), _rows_spec(D_FF), _rows_spec(D)],
        out_shape=[big, big, big, jax.ShapeDtypeStruct((T, D), BF)],
        scratch_shapes=[pltpu.VMEM((D_FF, D), BF)] * 2 + [pltpu.SemaphoreType.DMA((len(FF_TILE_ROWS),))] * 2,
        compiler_params=_params(), name="gate_up")(h1, wn, w_gate, w_up)


def down_loss(act, w_down, h1, target, wf):
    first = NPAD + N_META

    def body(a_ref, w_ref, r_ref, t_hbm, wf_ref, d_ref, db_ref, l_ref, dw_ref, h_scr, t_ref, t_sem):
        i = pl.program_id(0)
        _zero_at_first(l_ref, dw_ref)
        head = pltpu.make_async_copy(t_hbm.at[pl.ds(0, RC - first)], t_ref.at[pl.ds(first, RC - first)], t_sem)
        rest = pltpu.make_async_copy(t_hbm.at[pl.ds(pl.multiple_of(jnp.maximum(i * RC - first, 0), 32), RC)], t_ref, t_sem)

        @pl.when(i == 0)
        def _():
            t_ref[0:first, :] = jnp.zeros((first, D), F32)
            head.start()

        @pl.when(i > 0)
        def _():
            rest.start()

        def tile(c0):
            cols = pl.ds(c0, 512)
            h_scr[:, cols] = r_ref[:, cols] + jnp.dot(a_ref[...], w_ref[:, cols], preferred_element_type=F32)

        _col_tiles(D, 512, tile)

        @pl.when(i == 0)
        def _():
            head.wait()

        @pl.when(i > 0)
        def _():
            rest.wait()

        for r in (0, HALF):
            h = h_scr[r:r + HALF, :]
            live = _rows((HALF, D), i * RC + r) >= first
            err = jnp.where(live, _rms(h, wf_ref[...]) - t_ref[r:r + HALF, :], 0.0)
            l_ref[...] += 0.5 * jnp.sum(jnp.sum(err * err, axis=1, keepdims=True) * (1.0 / D), axis=0, keepdims=True)
            dh, dw = _rms_bwd(err * (1.0 / D), h, wf_ref[...])
            dw_ref[...] += jnp.sum(dw, axis=0, keepdims=True)
            d_ref[r:r + HALF, :] = dh
            db_ref[r:r + HALF, :] = dh.astype(BF)

    return pl.pallas_call(
        body, grid=(T // RC,),
        in_specs=[_rows_spec(D_FF), _whole((D_FF, D)), _rows_spec(D), pl.BlockSpec(memory_space=pl.ANY), _vec(D)],
        out_specs=[_rows_spec(D), _rows_spec(D), _spec((1, 128), lambda i: (0, 0)), _vec(D)],
        out_shape=[jax.ShapeDtypeStruct((T, D), F32), jax.ShapeDtypeStruct((T, D), BF), jax.ShapeDtypeStruct((1, 128), F32),
                   jax.ShapeDtypeStruct((1, D), F32)],
        scratch_shapes=[pltpu.VMEM((RC, D), F32), pltpu.VMEM((RC, D), F32), pltpu.SemaphoreType.DMA],
        compiler_params=_params(), name="down_loss")(act, w_down, h1, target, wf)


def swiglu_bwd(dh2_b, w_down, gt, up, act, u2):
    tn = 256

    def body(d_hbm, u_hbm, w_ref, gt_ref, up_ref, act_ref, dg_ref, du_ref, gd_ref, gg_ref, gu_ref, d_ref, u_ref, d_sems, u_sems):
        chunks = list(range(0, T, RC))
        d_ready = _arriving(d_hbm, d_ref, d_sems, chunks, RC)
        u_ready = _arriving(u_hbm, u_ref, u_sems, chunks, RC)

        def rows(r0):
            part = pl.ds(r0, RC)
            d_ready(r0 // RC)
            dact = lax.dot_general(d_ref[part, :], w_ref[...], NT_DIMS, preferred_element_type=F32)
            gt_ = gt_ref[part, :].astype(F32)
            up_ = up_ref[part, :].astype(F32)
            sg = _sigmoid(gt_)
            dg_ref[part, :] = (dact * up_ * (sg * (1.0 + gt_ * (1.0 - sg)))).astype(BF)
            du_ref[part, :] = (dact * (gt_ * sg)).astype(BF)

        _col_tiles(T, RC, rows)
        for k in range(len(chunks)):
            u_ready(k)
        gd_ref[...] = lax.dot_general(act_ref[...], d_ref[...], TN_DIMS, preferred_element_type=F32).astype(BF)
        gg_ref[...] = lax.dot_general(dg_ref[...], u_ref[...], TN_DIMS, preferred_element_type=F32).astype(BF)
        gu_ref[...] = lax.dot_general(du_ref[...], u_ref[...], TN_DIMS, preferred_element_type=F32).astype(BF)

    cols = _spec((T, tn), lambda j: (0, j))
    wrow = _spec((tn, D), lambda j: (j, 0))
    big = jax.ShapeDtypeStruct((T, D_FF), BF)
    grad = jax.ShapeDtypeStruct((D_FF, D), BF)
    return pl.pallas_call(
        body, grid=(D_FF // tn,), in_specs=[ANY_SPEC, ANY_SPEC, wrow, cols, cols, cols],
        out_specs=[cols, cols, wrow, wrow, wrow], out_shape=[big, big, grad, grad, grad],
        scratch_shapes=[pltpu.VMEM((T, D), BF)] * 2 + [pltpu.SemaphoreType.DMA((T // RC,))] * 2,
        compiler_params=_params(), name="swiglu_bwd")(dh2_b, u2, w_down, gt, up, act)


def gate_up_bwd(dgt, dup, w_gate, w_up, h1, wn, dh2):
    def body(dg_ref, du_ref, wg_hbm, wu_hbm, h_ref, wn_ref, r_ref, d_ref, db_ref, dw_ref, du_scr, wg_ref, wu_ref, wg_sems, wu_sems):
        gate_ready = _arriving(wg_hbm, wg_ref, wg_sems, FF_TILE_ROWS, FF_TILE)
        up_ready = _arriving(wu_hbm, wu_ref, wu_sems, FF_TILE_ROWS, FF_TILE)
        _zero_at_first(dw_ref)

        du_scr[...] = jnp.zeros_like(du_scr)

        def tile(c0):
            k = pl.ds(c0, FF_TILE)
            gate_ready(c0 // FF_TILE)
            up_ready(c0 // FF_TILE)
            du_scr[...] += (jnp.dot(dg_ref[:, k], wg_ref[k, :], preferred_element_type=F32)
                            + jnp.dot(du_ref[:, k], wu_ref[k, :], preferred_element_type=F32))

        _col_tiles(D_FF, FF_TILE, tile)
        for r in (0, HALF):
            dh, dw = _rms_bwd(du_scr[r:r + HALF, :], h_ref[r:r + HALF, :], wn_ref[...])
            dw_ref[...] += jnp.sum(dw, axis=0, keepdims=True)
            dh = dh + r_ref[r:r + HALF, :]
            d_ref[r:r + HALF, :] = dh
            db_ref[r:r + HALF, :] = dh.astype(BF)

    return pl.pallas_call(
        body, grid=(T // RC,),
        in_specs=[_rows_spec(D_FF), _rows_spec(D_FF), ANY_SPEC, ANY_SPEC, _rows_spec(D), _vec(D), _rows_spec(D)],
        out_specs=[_rows_spec(D), _rows_spec(D), _vec(D)],
        out_shape=[jax.ShapeDtypeStruct((T, D), F32), jax.ShapeDtypeStruct((T, D), BF), jax.ShapeDtypeStruct((1, D), F32)],
        scratch_shapes=[pltpu.VMEM((RC, D), F32)] + [pltpu.VMEM((D_FF, D), BF)] * 2 + [pltpu.SemaphoreType.DMA((len(FF_TILE_ROWS),))] * 2,
        compiler_params=_params(), name="gate_up_bwd")(dgt, dup, w_gate, w_up, h1, wn, dh2)


def _adamw(w, g, m, v):
    m = ADAM_B1 * m + (1.0 - ADAM_B1) * g
    v = ADAM_B2 * v + (1.0 - ADAM_B2) * (g * g)
    m_hat = m / (1.0 - ADAM_B1 ** ADAM_STEP)
    v_hat = v / (1.0 - ADAM_B2 ** ADAM_STEP)
    delta = -ADAM_LR * (m_hat / (jnp.sqrt(v_hat) + ADAM_EPS) + ADAM_WD * w)
    return delta, m, v


def adamw_shards(name, recvs, ws, ms, vs):
    n = len(ws)

    def body(*refs):
        ins, outs = refs[:4 * n], refs[4 * n:]
        for k in range(n):
            p_ref, w_ref, m_ref, v_ref = ins[k], ins[n + k], ins[2 * n + k], ins[3 * n + k]
            g = p_ref[0].astype(F32)
            for s in range(1, 8):
                g = g + p_ref[s].astype(F32)
            outs[4 * k][...] = g
            outs[4 * k + 1][...], outs[4 * k + 2][...], outs[4 * k + 3][...] = _adamw(w_ref[...], g, m_ref[...], v_ref[...])

    tiles = [_spec((w.shape[0] // 2, w.shape[1]), lambda i: (i, 0)) for w in ws]
    recv_tiles = [_spec((8, w.shape[0] // 2, w.shape[1]), lambda i: (0, i, 0)) for w in ws]
    res = pl.pallas_call(
        body, grid=(2,), in_specs=recv_tiles + tiles * 3,
        out_specs=[t for t in tiles for _ in range(4)],
        out_shape=[jax.ShapeDtypeStruct(w.shape, F32) for w in ws for _ in range(4)],
        compiler_params=_params(), name=name)(*recvs, *ws, *ms, *vs)
    return [list(res[4 * k:4 * k + 4]) for k in range(n)]


def adamw_w_in(recv, w, m, v, after):
    rows = 34
    per_row = D // 128

    def body(p_ref, w_ref, m_ref, v_ref, _after, g_ref, d_ref, mo_ref, vo_ref):
        def chunk(c, carry):
            lines = pl.ds(pl.multiple_of(c * per_row * rows, 16), per_row * rows)
            g = p_ref[0, lines, :].astype(F32)
            for s in range(1, 8):
                g = g + p_ref[s, lines, :].astype(F32)
            g = g.reshape(rows, per_row, 128)
            part = pl.ds(c * rows, rows)
            g_ref[part] = g
            d_ref[part], mo_ref[part], vo_ref[part] = _adamw(w_ref[part], g, m_ref[part], v_ref[part])
            return carry

        lax.fori_loop(0, w.shape[0] // rows, chunk, 0)

    shape = jax.ShapeDtypeStruct(w.shape, F32)
    whole = pl.BlockSpec(memory_space=pltpu.VMEM)
    return pl.pallas_call(body, out_shape=[shape] * 4, in_specs=[whole] * 4 + [ANY_SPEC], compiler_params=_params(0),
                          name="adamw_w_in")(recv, w, m, v, after)


def sum_slabs(recv):
    def body(p_ref, o_ref):
        g = p_ref[0]
        for s in range(1, 8):
            g = g + p_ref[s]
        for s in range(8):
            o_ref[s] = g

    return pl.pallas_call(body, out_shape=jax.ShapeDtypeStruct(recv.shape, F32), compiler_params=_params(0), name="sum_slabs")(recv)


SIMPLE = [("norm1_w", 1024), ("ssd_conv_b", 1536), ("ssd_dt_bias", 16), ("ssd_a_log", 16), ("ssd_d", 16), ("ssd_norm_w", 1024),
          ("lru_conv_b", 1024), ("lru_ba", 1024), ("lru_bx", 1024), ("lru_lambda", 1024), ("lru_norm_w", 1024), ("norm2_w", 1024),
          ("final_norm_w", 1024)]
SPECIAL = ["lru_wa", "lru_wx", "meta_tokens", "ssd_conv_w", "lru_conv_w"]
SM_ROWS = 176
SM_WA, SM_WX, SM_META, SM_SCW, SM_LCW, SM_LOSS = 14, 78, 142, 158, 166, 170


def _simple_rows():
    rows, r = {}, 0
    for name, n in SIMPLE:
        rows[name] = r
        r += -(-n // 1024)
    return rows


def adamw_small(sm, special_g, ws, ms, vs):
    rows = _simple_rows()
    ns, nx = len(SIMPLE), len(SPECIAL)

    def body(*refs):
        sm_ref = refs[0]
        gx = refs[1:1 + nx]
        wr = refs[1 + nx:1 + nx + ns + nx]
        mr = refs[1 + nx + ns + nx:1 + nx + 2 * (ns + nx)]
        vr = refs[1 + nx + 2 * (ns + nx):1 + nx + 3 * (ns + nx)]
        outs = refs[1 + nx + 3 * (ns + nx):]
        o = 0
        for k, (name, n) in enumerate(SIMPLE):
            r0 = rows[name]
            for c0 in range(0, n, 1024):
                wd = min(1024, n - c0)
                g = sm_ref[r0 + c0 // 1024:r0 + c0 // 1024 + 1, 0:wd]
                sl = (slice(None), slice(c0, c0 + wd))
                d, m2, v2 = _adamw(wr[k][sl], g, mr[k][sl], vr[k][sl])
                outs[o][sl] = g
                outs[o + 1][sl] = d
                outs[o + 2][sl] = m2
                outs[o + 3][sl] = v2
            o += 4
        for k in range(nx):
            g = gx[k][...]
            d, m2, v2 = _adamw(wr[ns + k][...], g, mr[ns + k][...], vr[ns + k][...])
            outs[o][...] = g
            outs[o + 1][...] = d
            outs[o + 2][...] = m2
            outs[o + 3][...] = v2
            o += 4
        outs[o][...] = sm_ref[SM_LOSS:SM_LOSS + 1, 0:1]

    out_shape = []
    for k in range(ns + nx):
        out_shape += [jax.ShapeDtypeStruct(ws[k].shape, F32)] * 4
    out_shape.append(jax.ShapeDtypeStruct((1, 1), F32))
    return pl.pallas_call(body, out_shape=out_shape, compiler_params=_params(0), name="adamw_small")(sm, *special_g, *ws, *ms, *vs)


def _place():
    return lax.axis_index("x"), lax.axis_index("y"), lax.axis_index("c")


def _index(px, py, pc):
    return 4 * px + 2 * py + pc


def all_gather(name, shards):
    n = len(shards)
    hbm = pl.BlockSpec(memory_space=pl.ANY)

    def pieces(s):
        tile = 32 // s.dtype.itemsize
        per = s.shape[0] // tile // 4 * tile
        return [(0, s.shape[0])] if s.shape[0] < 256 else [(r * per, per if r < 3 else s.shape[0] - 3 * per) for r in range(4)]

    parts = [pieces(s) for s in shards]
    first_sem = [7 * sum(len(p) for p in parts[:i]) for i in range(n + 1)]

    def body(*refs):
        ins, outs = refs[:n], refs[n:2 * n]
        send_sems, recv_sems, local_sems = refs[2 * n:]
        x, y, c = _place()
        me, sibling = (x, y, c), (x, y, 1 - c)
        chips = [(1 - x, y), (x, 1 - y), (1 - x, 1 - y)]

        def copy(i, r, k, block, to, src=None):
            rows = pl.ds(*parts[i][r])
            dst = outs[i].at[_index(*block), rows]
            sem = first_sem[i] + 7 * r + k
            return pltpu.make_async_remote_copy(src_ref=dst if src is None else src.at[rows], dst_ref=dst, send_sem=send_sems.at[sem],
                                                recv_sem=recv_sems.at[sem], device_id=to, device_id_type=MESH)

        every = [(i, r) for i in range(n) for r in range(len(parts[i]))]
        mine = [pltpu.make_async_copy(ins[i], outs[i].at[_index(*me)], local_sems.at[i]) for i in range(n)]
        for cp in mine:
            cp.start()
        first = []
        for i, r in every:
            first += [copy(i, r, 1 + j, me, (*chip, c), src=ins[i]) for j, chip in enumerate(chips)]
            first.append(copy(i, r, 0, me, sibling, src=ins[i]))
        for cp in first:
            cp.start()
        passed = []
        for i, r in every:
            for j, chip in enumerate(chips):
                copy(i, r, 1 + j, (*chip, c), me).wait_recv()
                cp = copy(i, r, 4 + j, (*chip, c), sibling)
                cp.start()
                passed.append(cp)
        for i, r in every:
            copy(i, r, 0, sibling, me).wait_recv()
            for j, chip in enumerate(chips):
                copy(i, r, 4 + j, (*chip, 1 - c), me).wait_recv()
        for cp in first + passed:
            cp.wait_send()
        for cp in mine:
            cp.wait()

    return pl.pallas_call(
        body, in_specs=[hbm] * n, out_specs=[hbm] * n,
        out_shape=[jax.ShapeDtypeStruct((8,) + s.shape, s.dtype) for s in shards],
        scratch_shapes=[pltpu.SemaphoreType.DMA((first_sem[n],)), pltpu.SemaphoreType.DMA((first_sem[n],)), pltpu.SemaphoreType.DMA((n,))],
        name=name)(*shards)


HBM_SPEC = pl.BlockSpec(memory_space=pltpu.HBM)
SEM_SPEC = pl.BlockSpec(memory_space=pltpu.SEMAPHORE)
EFFECT = pltpu.SideEffectType.DATAFLOW_SIDE_EFFECTING


def _peers(x, y, c):
    return [((1 - x) if k & 4 else x, (1 - y) if k & 2 else y, (1 - c) if k & 1 else c) for k in range(1, 8)]


def _pieces(rows):
    for n in (4, 2):
        if rows % (16 * n) == 0:
            return [(r * (rows // n), rows // n) for r in range(n)]
    return [(0, rows)]


def _peer_copies(src, land, send_sems, recv_sems, k, peer, mine):
    block = src.at[_index(*peer)]
    return [pltpu.make_async_remote_copy(src_ref=block.at[pl.ds(r0, nr)], dst_ref=land.at[mine, pl.ds(r0, nr)], send_sem=send_sems.at[k],
                                         recv_sem=recv_sems.at[k], device_id=peer, device_id_type=MESH)
            for r0, nr in _pieces(block.shape[0])]


OWN = 7


def _own_copy(src, land, send_sems, mine):
    return pltpu.make_async_copy(src.at[mine], land.at[mine], send_sems.at[OWN])


def copies_start(name, srcs):
    n = len(srcs)

    def body(*refs):
        ins, lands = refs[:n], refs[n:2 * n]
        sends, recvs = refs[2 * n:3 * n], refs[3 * n:4 * n]
        token = refs[-1]
        x, y, c = _place()
        mine = _index(x, y, c)
        for i in range(n):
            per_peer = [_peer_copies(ins[i], lands[i], sends[i], recvs[i], k, peer, mine) for k, peer in enumerate(_peers(x, y, c))]
            for piece in zip(*per_peer):
                for cp in piece:
                    cp.start()
            _own_copy(ins[i], lands[i], sends[i], mine).start()
        token[...] = jnp.zeros_like(token)

    res = pl.pallas_call(
        body, name=name,
        out_shape=([pltpu.SemaphoreType.DMA((8,))] * n + [pltpu.SemaphoreType.DMA((7,))] * n + [pltpu.HBM(s.shape, s.dtype) for s in srcs] * 2
                   + [jax.ShapeDtypeStruct((8, 128), F32)]),
        in_specs=[HBM_SPEC] * (2 * n),
        out_specs=[SEM_SPEC] * (2 * n) + [HBM_SPEC] * (2 * n) + [pl.BlockSpec(memory_space=pltpu.VMEM)],
        input_output_aliases={i: 2 * n + i for i in range(2 * n)},
        compiler_params=pltpu.CompilerParams(has_side_effects=EFFECT),
    )(*[pltpu.with_memory_space_constraint(s, pltpu.HBM) for s in srcs],
      *[pltpu.with_memory_space_constraint(lax.empty(s.shape, s.dtype), pltpu.HBM) for s in srcs])
    return [(res[i], res[n + i], res[2 * n + i], res[3 * n + i]) for i in range(n)], res[-1]


def copies_wait(name, started, after):
    n = len(started)

    def body(*refs):
        ins, lands = refs[:n], refs[n:2 * n]
        sends, recvs = refs[2 * n:3 * n], refs[3 * n:4 * n]
        x, y, c = _place()
        mine = _index(x, y, c)
        for i in range(n):
            for k, peer in enumerate(_peers(x, y, c)):
                arrival = pltpu.make_async_remote_copy(src_ref=ins[i].at[mine], dst_ref=lands[i].at[_index(*peer)],
                                                       send_sem=sends[i].at[k], recv_sem=recvs[i].at[k], device_id=peer, device_id_type=MESH)
                arrival.wait_send()
                arrival.wait_recv()
            _own_copy(ins[i], lands[i], sends[i], mine).wait()

    srcs = [s[2] for s in started]
    lands = [s[3] for s in started]
    res = pl.pallas_call(
        body, name=name,
        out_shape=[pltpu.HBM(s.shape, s.dtype) for s in srcs] + [pltpu.HBM(z.shape, z.dtype) for z in lands],
        in_specs=[HBM_SPEC] * (2 * n) + [SEM_SPEC] * (2 * n) + [ANY_SPEC] * len(after),
        out_specs=[HBM_SPEC] * (2 * n),
        input_output_aliases={i: i for i in range(2 * n)},
        compiler_params=pltpu.CompilerParams(has_side_effects=EFFECT),
    )(*srcs, *lands, *[s[0] for s in started], *[s[1] for s in started], *after)
    return list(res[n:])


def _hop(src, land, send_sems, recv_sems, k, block, to):
    dst = land.at[_index(*block)]
    return pltpu.make_async_remote_copy(src_ref=dst if src is None else src, dst_ref=dst, send_sem=send_sems.at[k], recv_sem=recv_sems.at[k],
                                        device_id=to, device_id_type=MESH)


def _own_block(src, land, send_sems, mine):
    return pltpu.make_async_copy(src, land.at[mine], send_sems.at[4])


def _other_chips(x, y):
    return [(1 - x, y), (x, 1 - y), (1 - x, 1 - y)]


def gather_start(name, shards, through):
    n, m = len(shards), len(through)

    def body(*refs):
        ins, lands = refs[:n], refs[n:2 * n]
        sends, recvs = refs[2 * n + m:3 * n + m], refs[3 * n + m:4 * n + m]
        x, y, c = _place()
        for i in range(n):
            for j, chip in enumerate(_other_chips(x, y)):
                _hop(ins[i], lands[i], sends[i], recvs[i], 1 + j, (x, y, c), (*chip, c)).start()
            _hop(ins[i], lands[i], sends[i], recvs[i], 0, (x, y, c), (x, y, 1 - c)).start()
            _own_block(ins[i], lands[i], sends[i], _index(x, y, c)).start()

    own, passing = pltpu.SemaphoreType.DMA((5,)), pltpu.SemaphoreType.DMA((3,))
    zones = [jax.ShapeDtypeStruct((8,) + s.shape, s.dtype) for s in shards]
    res = pl.pallas_call(
        body, name=name,
        out_shape=([own] * (2 * n) + [passing] * (2 * n) + [pltpu.HBM(s.shape, s.dtype) for s in shards]
                   + [pltpu.HBM(z.shape, z.dtype) for z in zones] + [pltpu.HBM(t.shape, t.dtype) for t in through]),
        in_specs=[HBM_SPEC] * (2 * n + m),
        out_specs=[SEM_SPEC] * (4 * n) + [HBM_SPEC] * (2 * n + m),
        input_output_aliases={i: 4 * n + i for i in range(2 * n + m)},
        compiler_params=pltpu.CompilerParams(has_side_effects=EFFECT),
    )(*[pltpu.with_memory_space_constraint(s, pltpu.HBM) for s in shards],
      *[pltpu.with_memory_space_constraint(lax.empty(z.shape, z.dtype), pltpu.HBM) for z in zones],
      *[pltpu.with_memory_space_constraint(t, pltpu.HBM) for t in through])
    return [[res[4 * n + i], res[5 * n + i], res[i], res[n + i], res[2 * n + i], res[3 * n + i]] for i in range(n)], list(res[6 * n:])


def gather_stage(name, pass_on, finish, after):
    arrays = pass_on + finish
    n = len(arrays)

    def body(*refs):
        ins, lands = refs[:n], refs[n:2 * n]
        sems = [refs[(2 + q) * n:(3 + q) * n] for q in range(4)]
        x, y, c = _place()
        me, sibling = (x, y, c), (x, y, 1 - c)
        for i in range(len(pass_on)):
            send, recv, send_on, recv_on = (q[i] for q in sems)
            for j, chip in enumerate(_other_chips(x, y)):
                _hop(None, lands[i], send, recv, 1 + j, (*chip, c), me).wait_recv()
                _hop(None, lands[i], send_on, recv_on, j, (*chip, c), sibling).start()
        for i in range(len(pass_on), n):
            send, recv, send_on, recv_on = (q[i] for q in sems)
            _hop(ins[i], lands[i], send, recv, 0, sibling, me).wait_recv()
            for j, chip in enumerate(_other_chips(x, y)):
                _hop(None, lands[i], send_on, recv_on, j, (*chip, 1 - c), me).wait_recv()
            _hop(ins[i], lands[i], send, recv, 0, me, sibling).wait_send()
            _own_block(ins[i], lands[i], send, _index(*me)).wait()
            for j, chip in enumerate(_other_chips(x, y)):
                _hop(ins[i], lands[i], send, recv, 1 + j, me, (*chip, c)).wait_send()
                _hop(None, lands[i], send_on, recv_on, j, (*chip, c), sibling).wait_send()
        refs[-1][...] = jnp.zeros_like(refs[-1])

    res = pl.pallas_call(
        body, name=name,
        out_shape=([pltpu.HBM(a[0].shape, a[0].dtype) for a in arrays] + [pltpu.HBM(a[1].shape, a[1].dtype) for a in arrays]
                   + [jax.ShapeDtypeStruct((8, 128), F32)]),
        in_specs=[HBM_SPEC] * (2 * n) + [SEM_SPEC] * (4 * n) + [ANY_SPEC],
        out_specs=[HBM_SPEC] * (2 * n) + [pl.BlockSpec(memory_space=pltpu.VMEM)],
        input_output_aliases={i: i for i in range(2 * n)},
        compiler_params=pltpu.CompilerParams(has_side_effects=EFFECT),
    )(*[a[0] for a in arrays], *[a[1] for a in arrays], *[a[2 + q] for q in range(4) for a in arrays], after)
    for i, a in enumerate(arrays):
        a[0], a[1] = res[i], res[n + i]
    return [a[1] for a in finish], res[-1]


WEIGHTS = ["meta_tokens", "norm1_w", "w_in", "ssd_conv_w", "ssd_conv_b", "ssd_dt_bias", "ssd_a_log", "ssd_d", "ssd_norm_w", "lru_conv_w",
           "lru_conv_b", "lru_wa", "lru_ba", "lru_wx", "lru_bx", "lru_lambda", "lru_norm_w", "w_out", "norm2_w", "w_gate", "w_up", "w_down",
           "final_norm_w"]
BIG = ["w_in", "w_out", "w_gate", "w_up", "w_down"]
COLUMN_SHARDED = ["w_in", "w_gate", "w_up"]


def _pair_blocks(w):
    w = w.reshape(8, 2, 64, 64)
    z = jnp.zeros((8, 64, 64), w.dtype)
    return jnp.concatenate([jnp.concatenate([w[:, 0], z], axis=2), jnp.concatenate([z, w[:, 1]], axis=2)], axis=1)


def _unpair_blocks(w2):
    return jnp.stack([w2[:, :64, :64], w2[:, 64:, 64:]], axis=1).reshape(16, 64, 64)


def _per_group(v):
    return jnp.pad(v.reshape(2, 1, 8), ((0, 0), (0, 0), (0, 120)))


def _pad_cols(v, n):
    return jnp.pad(v, ((0, 0), (0, n - v.shape[1])))


def local_step(x, target, meta, ssd_cw, lru_cw, w_in_shards, fetch, send, p):
    bias2, alog2, d2 = _per_group(p["ssd_dt_bias"]), _per_group(p["ssd_a_log"]), _per_group(p["ssd_d"])
    wa2 = _pair_blocks(p["lru_wa"]).astype(BF)
    wx2 = _pair_blocks(p["lru_wx"]).astype(BF)
    lru = (lru_cw, p["lru_conv_b"], wa2, p["lru_ba"], wx2, p["lru_bx"], p["lru_lambda"])

    proj, dt_raw, u1, h0, w_in, w_dt = in_proj(x, meta, p["norm1_w"], w_in_shards)
    yn_ssd, y_pre, h_prev = ssd_fwd(proj, dt_raw, ssd_cw, p["ssd_conv_b"], bias2, alog2, d2, p["ssd_norm_w"])
    _, moved = fetch([], yn_ssd)
    hseq, a = lru_fwd(proj, *lru, moved)
    (w_out,), _ = fetch(["w_out"], hseq)
    h1, cat = out_proj(yn_ssd, proj, hseq, p["lru_norm_w"], w_out, h0)
    (w_gate, w_up), _ = fetch(["w_gate", "w_up"], h1)
    gt, up, act, u2 = gate_up(h1, p["norm2_w"], w_gate, w_up)
    (w_down,), _ = fetch(["w_down"], act)
    dh2, dh2_b, loss, d_fnw = down_loss(act, w_down, h1, target, p["final_norm_w"])

    dgt, dup, g_down, g_gate, g_up = swiglu_bwd(dh2_b, w_down, gt, up, act, u2)
    dh1, dh1_b, d_n2 = gate_up_bwd(dgt, dup, w_gate, w_up, h1, p["norm2_w"], dh2)
    sent = send({"w_down": g_down, "w_gate": g_gate, "w_up": g_up, "w_out": weight_grad("dw_out", cat, dh1_b)})
    dyn, dh_out, dg_b, d_lnw = out_proj_bwd(dh1_b, w_out, proj, hseq, p["lru_norm_w"], sent)

    dxl_b, d_lcw, d_lcb, dwa2, d_ba, dwx2, d_bx, d_lam = lru_bwd(dh_out, a, hseq, proj, *lru)
    dz_b, dxbc_b, ddt_b, dpar, d_snw, d_scw, d_scb = ssd_bwd(dyn, proj, dt_raw, ssd_cw, p["ssd_conv_b"], y_pre, h_prev, bias2, alog2, d2,
                                                             p["ssd_norm_w"], sent)
    sent = send({"w_in": in_weight_grad([dz_b, dxbc_b, dg_b, dxl_b], [0, SSD_W, 2576, 2576 + LRU_W], ddt_b, u1)})
    grad_x, d_meta, d_n1, dwa2, dwx2 = in_proj_bwd(dz_b, dg_b, dxl_b, dxbc_b, ddt_b, w_in, w_dt, h0, p["norm1_w"], dh1, sent, [dwa2, dwx2])
    small = {"norm1_w": d_n1, "ssd_conv_b": d_scb, "ssd_dt_bias": dpar[:, 0, :8].reshape(1, 16), "ssd_a_log": dpar[:, 1, :8].reshape(1, 16),
             "ssd_d": dpar[:, 2, :8].reshape(1, 16), "ssd_norm_w": d_snw, "lru_conv_b": d_lcb, "lru_ba": d_ba, "lru_bx": d_bx,
             "lru_lambda": d_lam, "lru_norm_w": d_lnw, "norm2_w": d_n2, "final_norm_w": d_fnw,
             "lru_wa": _unpair_blocks(dwa2), "lru_wx": _unpair_blocks(dwx2), "meta_tokens": d_meta,
             "ssd_conv_w": d_scw, "lru_conv_w": d_lcw}
    return loss, grad_x, small


def _pack_small(small, loss):
    rows = [_pad_cols(small[name], -(-n // 1024) * 1024).reshape(-1, 1024) for name, n in SIMPLE]
    rows += [small["lru_wa"].reshape(64, 1024), small["lru_wx"].reshape(64, 1024), small["meta_tokens"],
             _pad_cols(small["ssd_conv_w"], 2048).reshape(8, 1024), small["lru_conv_w"], _pad_cols(loss[:, 0:1], 1024)]
    sm = jnp.concatenate(rows, axis=0)
    return jnp.pad(sm, ((0, SM_ROWS - sm.shape[0]), (0, 0)))


def _slabs(g):
    return g.reshape(8, g.shape[0] // 8, g.shape[1])


def _unslab(g):
    return g.reshape(8 * g.shape[1], g.shape[2])


def kernel(x, meta_tokens, norm1_w, w_in, ssd_conv_w, ssd_conv_b, ssd_dt_bias, ssd_a_log, ssd_d, ssd_norm_w, lru_conv_w, lru_conv_b, lru_wa, lru_ba, lru_wx, lru_bx, lru_lambda, lru_norm_w, w_out, norm2_w, w_gate, w_up, w_down, final_norm_w, loss_target, m_meta_tokens, m_norm1_w, m_w_in, m_ssd_conv_w, m_ssd_conv_b, m_ssd_dt_bias, m_ssd_a_log, m_ssd_d, m_ssd_norm_w, m_lru_conv_w, m_lru_conv_b, m_lru_wa, m_lru_ba, m_lru_wx, m_lru_bx, m_lru_lambda, m_lru_norm_w, m_w_out, m_norm2_w, m_w_gate, m_w_up, m_w_down, m_final_norm_w, v_meta_tokens, v_norm1_w, v_w_in, v_ssd_conv_w, v_ssd_conv_b, v_ssd_dt_bias, v_ssd_a_log, v_ssd_d, v_ssd_norm_w, v_lru_conv_w, v_lru_conv_b, v_lru_wa, v_lru_ba, v_lru_wx, v_lru_bx, v_lru_lambda, v_lru_norm_w, v_w_out, v_norm2_w, v_w_gate, v_w_up, v_w_down, v_final_norm_w):
    w = dict(meta_tokens=meta_tokens, norm1_w=norm1_w, w_in=w_in[0], ssd_conv_w=ssd_conv_w[0], ssd_conv_b=ssd_conv_b, ssd_dt_bias=ssd_dt_bias,
             ssd_a_log=ssd_a_log, ssd_d=ssd_d, ssd_norm_w=ssd_norm_w, lru_conv_w=lru_conv_w[0], lru_conv_b=lru_conv_b, lru_wa=lru_wa[0],
             lru_ba=lru_ba, lru_wx=lru_wx[0], lru_bx=lru_bx, lru_lambda=lru_lambda, lru_norm_w=lru_norm_w, w_out=w_out[0], norm2_w=norm2_w,
             w_gate=w_gate[0], w_up=w_up[0], w_down=w_down[0], final_norm_w=final_norm_w.reshape(1, D))
    m = dict(meta_tokens=m_meta_tokens, norm1_w=m_norm1_w, w_in=m_w_in[0], ssd_conv_w=m_ssd_conv_w[0], ssd_conv_b=m_ssd_conv_b,
             ssd_dt_bias=m_ssd_dt_bias, ssd_a_log=m_ssd_a_log, ssd_d=m_ssd_d, ssd_norm_w=m_ssd_norm_w, lru_conv_w=m_lru_conv_w[0],
             lru_conv_b=m_lru_conv_b, lru_wa=m_lru_wa[0], lru_ba=m_lru_ba, lru_wx=m_lru_wx[0], lru_bx=m_lru_bx, lru_lambda=m_lru_lambda,
             lru_norm_w=m_lru_norm_w, w_out=m_w_out[0], norm2_w=m_norm2_w, w_gate=m_w_gate[0], w_up=m_w_up[0], w_down=m_w_down[0],
             final_norm_w=m_final_norm_w.reshape(1, D))
    v = dict(meta_tokens=v_meta_tokens, norm1_w=v_norm1_w, w_in=v_w_in[0], ssd_conv_w=v_ssd_conv_w[0], ssd_conv_b=v_ssd_conv_b,
             ssd_dt_bias=v_ssd_dt_bias, ssd_a_log=v_ssd_a_log, ssd_d=v_ssd_d, ssd_norm_w=v_ssd_norm_w, lru_conv_w=v_lru_conv_w[0],
             lru_conv_b=v_lru_conv_b, lru_wa=v_lru_wa[0], lru_ba=v_lru_ba, lru_wx=v_lru_wx[0], lru_bx=v_lru_bx, lru_lambda=v_lru_lambda,
             lru_norm_w=v_lru_norm_w, w_out=v_w_out[0], norm2_w=v_norm2_w, w_gate=v_w_gate[0], w_up=v_w_up[0], w_down=v_w_down[0],
             final_norm_w=v_final_norm_w.reshape(1, D))
    shapes = dict(meta_tokens=meta_tokens.shape, norm1_w=norm1_w.shape, w_in=w_in.shape, ssd_conv_w=ssd_conv_w.shape,
                  ssd_conv_b=ssd_conv_b.shape, ssd_dt_bias=ssd_dt_bias.shape, ssd_a_log=ssd_a_log.shape, ssd_d=ssd_d.shape,
                  ssd_norm_w=ssd_norm_w.shape, lru_conv_w=lru_conv_w.shape, lru_conv_b=lru_conv_b.shape, lru_wa=lru_wa.shape,
                  lru_ba=lru_ba.shape, lru_wx=lru_wx.shape, lru_bx=lru_bx.shape, lru_lambda=lru_lambda.shape, lru_norm_w=lru_norm_w.shape,
                  w_out=w_out.shape, norm2_w=norm2_w.shape, w_gate=w_gate.shape, w_up=w_up.shape, w_down=w_down.shape,
                  final_norm_w=final_norm_w.shape)
    me = _index(*_place())
    for n in COLUMN_SHARDED:
        w[n], m[n], v[n] = w[n].T, m[n].T, v[n].T

    small_shard = jnp.concatenate([w["meta_tokens"], _pad_cols(w["ssd_conv_w"], 256).reshape(8, 128), w["lru_conv_w"],
                                   jnp.zeros((4, 128), F32)], axis=0)
    g_in, gs = all_gather("gather_w_in", [w["w_in"].astype(BF), small_shard])
    later = ["w_out", "w_gate", "w_up", "w_down"]
    started, (g_in, gs) = gather_start("gather_rest_start", [w[n].astype(BF) for n in later], [g_in, gs])
    started = dict(zip(later, started))
    ssd_cw = gs[:, 16:24].reshape(8, 4, 256)[:, :, :192].transpose(1, 0, 2).reshape(4, XBC)
    lru_cw = gs[:, 24:28].transpose(1, 0, 2).reshape(4, LRU_W)

    def fetch(names, after):
        pass_on = {"w_out": ["w_down"], "w_gate": [], "w_down": []}[names[0]] if names else ["w_out", "w_gate", "w_up"]
        got, zero = gather_stage("gather_" + (names[0] + "_wait" if names else "pass_on"), [started[n] for n in pass_on],
                                 [started[n] for n in names], after)
        return [_unslab(g) for g in got], zero

    in_flight = {}

    def send(grads):
        names = list(grads)
        st, zero = copies_start("grads_" + names[0] + "_start", [grads[n] if n == "small" else _slabs(grads[n]) for n in names])
        in_flight.update(zip(names, st))
        return zero

    loss, grad_x, small = local_step(x[0], loss_target[0], gs, ssd_cw, lru_cw, g_in, fetch, send, w)
    send({"small": _pack_small(small, loss).reshape(8, SM_ROWS // 8, 1024)})

    out = {}
    early = ["w_down", "w_gate", "w_up", "w_out"]
    recv = dict(zip(early, copies_wait("grads_early_wait", [in_flight[n] for n in early], [in_flight["small"][2]])))
    for pair in (early[:2], early[2:]):
        done = adamw_shards("adamw_" + pair[0], [recv[n] for n in pair], [w[n] for n in pair], [m[n] for n in pair], [v[n] for n in pair])
        out.update(zip(pair, done))
    recv_in, recv_small = copies_wait("grads_late_wait", [in_flight["w_in"], in_flight["small"]], [out[n][0] for n in early])
    def lines(a):
        return jnp.transpose(a.reshape(D // 128, 128, IN_COLS // 8), (2, 0, 1))

    gathering, zero = copies_start("gather_small_start", [sum_slabs(recv_small)])
    updated = adamw_w_in(recv_in, lines(w_in), lines(m_w_in), lines(v_w_in), zero)
    out["w_in"] = [jnp.transpose(o, (1, 2, 0)).reshape(D, IN_COLS // 8) for o in updated]
    for n in ("w_gate", "w_up"):
        out[n] = [o.T for o in out[n]]
    sm = copies_wait("gather_small_wait", gathering, [updated[0]])[0].reshape(SM_ROWS, 1024)
    special_g =[sm[SM_WA:SM_WA + 64].reshape(16, 64, 64), sm[SM_WX:SM_WX + 64].reshape(16, 64, 64),
                 lax.dynamic_slice(sm[SM_META:SM_META + 16], (0, 128 * me), (16, 128)),
                 lax.dynamic_slice(sm[SM_SCW:SM_SCW + 8].reshape(4, 2048), (0, 192 * me), (4, 192)),
                 lax.dynamic_slice(sm[SM_LCW:SM_LCW + 4], (0, 128 * me), (4, 128))]
    names = [n for n, _ in SIMPLE] + SPECIAL
    res = adamw_small(sm, special_g, [w[n] for n in names], [m[n] for n in names], [v[n] for n in names])
    for k, n in enumerate(names):
        out[n] = res[4 * k:4 * k + 4]
    flat = [res[-1].reshape(()), grad_x[None]]
    for k in range(4):
        flat += [out[n][k].reshape(shapes[n]) for n in WEIGHTS]
    return tuple(flat)
```

```python
import math

import jax
import jax.numpy as jnp
from jax import lax
from jax.experimental import pallas as pl
from jax.experimental.pallas import tpu as pltpu

F32 = jnp.float32
BF = jnp.bfloat16

D = 1024
SEQ = 2048
N_META = 16
Q = 128
NPAD = 112
T = NPAD + N_META + SEQ
NCH = T // Q
RC = 544
D_FF = 2816
SSD_W = 1024
LRU_W = 1024
XBC = 1536
IN_COLS = 4624
PZ, PG, PXL, PXBC = 0, 1024, 2048, 3072
NP_IN = 4608
EPS = 1e-6
LRU_C = 8.0
VMEM_LIMIT = 56 * 1024 * 1024

ADAM_LR, ADAM_B1, ADAM_B2, ADAM_EPS, ADAM_WD, ADAM_STEP = 0.001, 0.9, 0.999, 1e-08, 0.01, 10

NT_DIMS = (((1,), (1,)), ((), ()))
TN_DIMS = (((0,), (0,)), ((), ()))
MESH = pl.DeviceIdType.MESH


def _params(n_grid=1, limit=VMEM_LIMIT):
    return pltpu.CompilerParams(dimension_semantics=("arbitrary",) * n_grid, vmem_limit_bytes=limit)


def _spec(shape, imap, single=False):
    if single:
        return pl.BlockSpec(shape, imap, pipeline_mode=pl.Buffered(1))
    return pl.BlockSpec(shape, imap)


def _sigmoid(x):
    return 0.5 * jnp.tanh(0.5 * x) + 0.5


def _sigmoid_gate(x):
    return 1.0 / (1.0 + jnp.exp(-x))


def _softplus(x):
    return jnp.maximum(x, 0.0) + jnp.log(1.0 + jnp.exp(-jnp.abs(x)))


def _rms_stats(h):
    return lax.rsqrt(jnp.mean(h * h, axis=-1, keepdims=True) + EPS)


def _rms(h, w):
    return (h * _rms_stats(h)) * w


def _rms_bwd(du, h, w):
    r = _rms_stats(h)
    n = h * r
    dn = du * w
    dh = r * (dn - n * jnp.mean(dn * n, axis=-1, keepdims=True))
    return dh, du * n


_G0 = math.sqrt(2.0 / math.pi)


def _gelu(x):
    return 0.5 * x * (1.0 + jnp.tanh(_G0 * (x + 0.044715 * (x * x * x))))


def _gelu_grad(x):
    t = jnp.tanh(_G0 * (x + 0.044715 * (x * x * x)))
    return 0.5 * (1.0 + t) + 0.5 * x * (1.0 - t * t) * (_G0 * (1.0 + 3.0 * 0.044715 * (x * x)))


def _rows(shape, r0=0):
    return lax.broadcasted_iota(jnp.int32, shape, 0) + r0


def _lanes(shape):
    return lax.broadcasted_iota(jnp.int32, shape, 1)


HALO = 8


def _fill_padded(pad_ref, x_ref):
    pad_ref[0:HALO, :] = jnp.zeros((HALO, pad_ref.shape[1]), F32)
    pad_ref[T + HALO:T + 2 * HALO, :] = jnp.zeros((HALO, pad_ref.shape[1]), F32)

    def step(c, carry):
        r0 = pl.multiple_of(c * Q, Q)
        pad_ref[pl.ds(r0 + HALO, Q), :] = x_ref[pl.ds(r0, Q), :].astype(F32)
        return carry

    lax.fori_loop(0, NCH, step, 0)


def _back(pad_ref, r0):
    win = pad_ref[pl.ds(r0, Q + HALO), :]
    return lambda s: win[HALO:, :] if s == 0 else pltpu.roll(win, s, axis=0)[HALO:, :]


def _ahead(pad_ref, r0):
    win = pad_ref[pl.ds(r0 + HALO, Q + HALO), :]
    return lambda s: win[:Q, :] if s == 0 else pltpu.roll(win, Q + HALO - s, axis=0)[:Q, :]


def _conv(back, w, b):
    y = b + w[3:4, :] * back(0)
    for k in range(3):
        y = y + w[k:k + 1, :] * back(3 - k)
    return y


def _conv_bwd_x(ahead, w):
    dx = w[3:4, :] * ahead(0)
    for k in range(3):
        dx = dx + w[k:k + 1, :] * ahead(3 - k)
    return dx


def _conv_bwd_w(dy, back):
    dws = [jnp.sum(dy * back(3 - k), axis=0, keepdims=True) for k in range(4)]
    return jnp.concatenate(dws, axis=0), jnp.sum(dy, axis=0, keepdims=True)


def _chunks(fn, unrolled=False):
    if unrolled:
        for c in range(NCH):
            fn(c * Q)
        return

    def step(c, carry):
        fn(pl.multiple_of(c * Q, Q))
        return carry

    lax.fori_loop(0, NCH, step, 0)


HALF = RC // 2


def _col_tiles(n, tn, fn):
    def step(j, carry):
        fn(pl.multiple_of(j * tn, tn))
        return carry

    lax.fori_loop(0, n // tn, step, 0)


def _rows_spec(cols, block_col=0):
    return _spec((RC, cols), lambda i: (i, block_col))


def _whole(shape):
    return _spec(shape, lambda i: tuple(0 for _ in shape), single=True)


def _vec(cols):
    return _spec((1, cols), lambda i: (0, 0))


def _zero_at_first(*refs):
    @pl.when(pl.program_id(0) == 0)
    def _():
        for r in refs:
            r[...] = jnp.zeros_like(r)


ANY_SPEC = pl.BlockSpec(memory_space=pl.ANY)


def _arriving(src, dst, sems, starts, rows):
    n, ahead = len(starts), 2
    first = pl.program_id(0) == 0

    def piece(k):
        r0 = starts[0]
        for j in range(1, n):
            r0 = jnp.where(k == j, starts[j], r0)
        at = pl.ds(pl.multiple_of(r0, 16), rows)
        return pltpu.make_async_copy(src.at[at], dst.at[at], sems.at[k])

    @pl.when(first)
    def _():
        for k in range(min(ahead, n)):
            piece(k).start()

    def ready(k):
        k = jnp.asarray(k, jnp.int32)

        @pl.when(first)
        def _():
            piece(k).wait()

            @pl.when(k + ahead < n)
            def _():
                piece(k + ahead).start()

    return ready


IN_RUNS = ((PZ, 0, 1024), (PXBC, 1024, XBC), (PG, 2576, 2048))
IN_TILE = 512
IN_TILE_ROWS = [wrow + IN_TILE * j for _, wrow, width in IN_RUNS for j in range(width // IN_TILE)]


def _in_tiles(fn, before_run=lambda run: None):
    done = 0
    for run, (pcol, wrow, width) in enumerate(IN_RUNS):
        before_run(run)
        def step(j, carry, pcol=pcol, wrow=wrow, done=done):
            fn(pl.multiple_of(pcol + j * IN_TILE, IN_TILE), pl.multiple_of(wrow + j * IN_TILE, 16), done + j)
            return carry

        lax.fori_loop(0, width // IN_TILE, step, 0)
        done += width // IN_TILE


def in_proj(x, meta, wn, w_shards):
    first = NPAD + N_META
    steps = T // RC
    shard = IN_COLS // 8

    def body(x_hbm, meta_ref, wn_ref, g_hbm, o_ref, dt_ref, u_ref, h_ref, wt_hbm, wdt_ref, raw, w_ref, h_scr, g_sems, h_sems, out_sem):
        i = pl.program_id(0)
        slot = i % 2
        shards = [pltpu.make_async_copy(g_hbm.at[j], raw.at[j], g_sems.at[j]) for j in range(8)]
        head = pltpu.make_async_copy(x_hbm.at[pl.ds(0, RC - first)], h_scr.at[0, pl.ds(first, RC - first)], h_sems.at[0])
        put_back = pltpu.make_async_copy(w_ref, wt_hbm, out_sem)

        def rows_of(step):
            return pltpu.make_async_copy(x_hbm.at[pl.ds(pl.multiple_of(step * RC - first, 32), RC)], h_scr.at[step % 2], h_sems.at[step % 2])

        @pl.when(i == 0)
        def _():
            for cp in shards:
                cp.start()
            head.start()
            h_scr[0, 0:NPAD, :] = jnp.zeros((NPAD, D), F32)
            for j in range(8):
                h_scr[0, NPAD:first, 128 * j:128 * j + 128] = meta_ref[j, 0:N_META, :]

        @pl.when(i + 1 < steps)
        def _():
            rows_of(i + 1).start()

        @pl.when(i == 0)
        def _():
            head.wait()

        @pl.when(i > 0)
        def _():
            rows_of(i).wait()

        h_ref[...] = h_scr[slot]
        for r in (0, HALF):
            u_ref[r:r + HALF, :] = _rms(h_scr[slot, r:r + HALF, :], wn_ref[...]).astype(BF)

        def place_shards(run):
            @pl.when(i == 0)
            def _():
                for j in ((0, 1), (2, 3, 4), (5, 6, 7))[run]:
                    shards[j].wait()
                    w_ref[shard * j:shard * (j + 1), :] = raw[j]
                if run == 1:
                    wdt_ref[...] = jnp.zeros_like(wdt_ref)
                    for g in range(2):
                        wdt_ref[128 * g:128 * g + 8, :] = w_ref[2560 + 8 * g:2568 + 8 * g, :]
                if run == 2:
                    put_back.start()

        def tile(pcol, wrow, k):
            o_ref[:, pl.ds(pcol, IN_TILE)] = lax.dot_general(u_ref[...], w_ref[pl.ds(wrow, IN_TILE), :], NT_DIMS,
                                                             preferred_element_type=F32).astype(BF)

        _in_tiles(tile, place_shards)
        dt_ref[...] = lax.dot_general(u_ref[...], wdt_ref[...], NT_DIMS, preferred_element_type=F32)

        @pl.when(i == steps - 1)
        def _():
            put_back.wait()

    return pl.pallas_call(
        body, grid=(steps,), in_specs=[ANY_SPEC, _spec(meta.shape, lambda i: (0, 0, 0)), _vec(D), ANY_SPEC],
        out_specs=[_rows_spec(NP_IN), _rows_spec(256), _rows_spec(D), _rows_spec(D), ANY_SPEC, _spec((256, D), lambda i: (0, 0))],
        out_shape=[jax.ShapeDtypeStruct((T, NP_IN), BF), jax.ShapeDtypeStruct((T, 256), F32), jax.ShapeDtypeStruct((T, D), BF),
                   jax.ShapeDtypeStruct((T, D), F32), jax.ShapeDtypeStruct((IN_COLS, D), BF), jax.ShapeDtypeStruct((256, D), BF)],
        scratch_shapes=[pltpu.VMEM((8, shard, D), BF), pltpu.VMEM((IN_COLS, D), BF), pltpu.VMEM((2, RC, D), F32),
                        pltpu.SemaphoreType.DMA((8,)), pltpu.SemaphoreType.DMA((2,)), pltpu.SemaphoreType.DMA],
        compiler_params=_params(), name="in_proj")(x, meta, wn, w_shards)


def out_proj(yn_ssd, proj, hseq, lru_nw, w_out, h0):
    def body(y_ref, g_ref, h_ref, wn_ref, w_ref, r_ref, o_ref, cat_ref):
        cat_ref[:, 0:SSD_W] = y_ref[...]
        for r in (0, HALF):
            y = _gelu(g_ref[r:r + HALF, :].astype(F32)) * h_ref[r:r + HALF, :]
            cat_ref[r:r + HALF, SSD_W:] = _rms(y, wn_ref[...]).astype(BF)

        def tile(c0):
            o_ref[:, pl.ds(c0, 512)] = r_ref[:, pl.ds(c0, 512)] + jnp.dot(cat_ref[...], w_ref[:, pl.ds(c0, 512)], preferred_element_type=F32)

        _col_tiles(D, 512, tile)

    return pl.pallas_call(
        body, grid=(T // RC,),
        in_specs=[_rows_spec(SSD_W), _rows_spec(LRU_W, PG // LRU_W), _rows_spec(LRU_W), _vec(LRU_W), _whole((SSD_W + LRU_W, D)), _rows_spec(D)],
        out_specs=[_rows_spec(D), _rows_spec(SSD_W + LRU_W)],
        out_shape=[jax.ShapeDtypeStruct((T, D), F32), jax.ShapeDtypeStruct((T, SSD_W + LRU_W), BF)],
        compiler_params=_params(), name="out_proj")(yn_ssd, proj, hseq, lru_nw, w_out, h0)


def out_proj_bwd(dh1_b, w_out, proj, hseq, lru_nw, after):
    def body(d_ref, w_ref, g_ref, h_ref, wn_ref, _after, dy_ref, dh_ref, dg_ref, dw_ref, dl_scr):
        _zero_at_first(dw_ref)

        def tile(c0):
            dy_ref[:, pl.ds(c0, 512)] = lax.dot_general(d_ref[...], w_ref[pl.ds(c0, 512), :], NT_DIMS, preferred_element_type=F32)
            dl_scr[:, pl.ds(c0, 512)] = lax.dot_general(d_ref[...], w_ref[pl.ds(SSD_W + c0, 512), :], NT_DIMS, preferred_element_type=F32)

        _col_tiles(SSD_W, 512, tile)

        for r in (0, HALF):
            g = g_ref[r:r + HALF, :].astype(F32)
            h = h_ref[r:r + HALF, :]
            ge = _gelu(g)
            dy, dw = _rms_bwd(dl_scr[r:r + HALF, :], ge * h, wn_ref[...])
            dw_ref[...] += jnp.sum(dw, axis=0, keepdims=True)
            dh_ref[r:r + HALF, :] = dy * ge
            dg_ref[r:r + HALF, :] = (dy * h * _gelu_grad(g)).astype(BF)

    return pl.pallas_call(
        body, grid=(T // RC,),
        in_specs=[_rows_spec(D), _whole((SSD_W + LRU_W, D)), _rows_spec(LRU_W, PG // LRU_W), _rows_spec(LRU_W), _vec(LRU_W), ANY_SPEC],
        out_specs=[_rows_spec(SSD_W), _rows_spec(LRU_W), _rows_spec(LRU_W), _vec(LRU_W)],
        out_shape=[jax.ShapeDtypeStruct((T, SSD_W), F32), jax.ShapeDtypeStruct((T, LRU_W), F32), jax.ShapeDtypeStruct((T, LRU_W), BF),
                   jax.ShapeDtypeStruct((1, LRU_W), F32)],
        scratch_shapes=[pltpu.VMEM((RC, LRU_W), F32)],
        compiler_params=_params(), name="out_proj_bwd")(dh1_b, w_out, proj, hseq, lru_nw, after)


def in_proj_bwd(dz, dg, dxl, dxbc, ddt, w_t, w_dt, h0, wn, dh1, after, through):
    first = NPAD + N_META

    def body(dz_ref, dg_ref, dxl_ref, dxbc_ref, ddt_ref, w_hbm, wdt_ref, h_ref, wn_ref, r_ref, _after, _in0, _in1, gx_hbm, meta_ref, dw_ref,
             _out0, _out1, du_scr, o_ref, sem, w_ref, w_sems):
        i = pl.program_id(0)
        ready = _arriving(w_hbm, w_ref, w_sems, IN_TILE_ROWS, IN_TILE)
        _zero_at_first(dw_ref)
        du_scr[...] = jnp.dot(ddt_ref[...], wdt_ref[...], preferred_element_type=F32)
        done = 0
        for d_ref, wrow, width in ((dz_ref, 0, 1024), (dxbc_ref, 1024, XBC), (dg_ref, 2576, 1024), (dxl_ref, 3600, 1024)):
            def step(j, carry, d_ref=d_ref, wrow=wrow, done=done):
                c0 = pl.multiple_of(j * IN_TILE, IN_TILE)
                ready(done + j)
                du_scr[...] += jnp.dot(d_ref[:, pl.ds(c0, IN_TILE)], w_ref[pl.ds(pl.multiple_of(wrow + c0, 16), IN_TILE), :],
                                       preferred_element_type=F32)
                return carry

            lax.fori_loop(0, width // IN_TILE, step, 0)
            done += width // IN_TILE
        for r in (0, HALF):
            dh, dw = _rms_bwd(du_scr[r:r + HALF, :], h_ref[r:r + HALF, :], wn_ref[...])
            dw_ref[...] += jnp.sum(dw, axis=0, keepdims=True)
            o_ref[r:r + HALF, :] = dh + r_ref[r:r + HALF, :]

        @pl.when(i == 0)
        def _():
            meta_ref[...] = o_ref[NPAD:first, :]
            head = pltpu.make_async_copy(o_ref.at[pl.ds(first, RC - first)], gx_hbm.at[pl.ds(0, RC - first)], sem)
            head.start()
            head.wait()

        @pl.when(i > 0)
        def _():
            rest = pltpu.make_async_copy(o_ref, gx_hbm.at[pl.ds(pl.multiple_of(i * RC - first, 32), RC)], sem)
            rest.start()
            rest.wait()

    return pl.pallas_call(
        body, grid=(T // RC,),
        in_specs=[_rows_spec(SSD_W), _rows_spec(LRU_W), _rows_spec(LRU_W), _rows_spec(XBC), _rows_spec(256), ANY_SPEC,
                  _whole((256, D)), _rows_spec(D), _vec(D), _rows_spec(D), ANY_SPEC, ANY_SPEC, ANY_SPEC],
        out_specs=[ANY_SPEC, _spec((N_META, D), lambda i: (0, 0)), _vec(D), ANY_SPEC, ANY_SPEC],
        out_shape=[jax.ShapeDtypeStruct((SEQ, D), F32), jax.ShapeDtypeStruct((N_META, D), F32), jax.ShapeDtypeStruct((1, D), F32)]
        + [jax.ShapeDtypeStruct(t.shape, t.dtype) for t in through],
        scratch_shapes=[pltpu.VMEM((RC, D), F32), pltpu.VMEM((RC, D), F32), pltpu.SemaphoreType.DMA,
                        pltpu.VMEM((IN_COLS, D), BF), pltpu.SemaphoreType.DMA((len(IN_TILE_ROWS),))],
        input_output_aliases={11: 3, 12: 4},
        compiler_params=_params(), name="in_proj_bwd")(dz, dg, dxl, dxbc, ddt, w_t, w_dt, h0, wn, dh1, after, *through)


GRAD_TILE = 256


def weight_grad(name, a, u1):
    tm = GRAD_TILE

    def body(a_ref, u_ref, o_ref):
        o_ref[...] = lax.dot_general(a_ref[...], u_ref[...], TN_DIMS, preferred_element_type=F32).astype(BF)

    return pl.pallas_call(
        body, grid=(a.shape[1] // tm,),
        in_specs=[_spec((T, tm), lambda j: (0, j)), _spec((T, D), lambda j: (0, 0), single=True)],
        out_specs=_spec((tm, D), lambda j: (j, 0)),
        out_shape=jax.ShapeDtypeStruct((a.shape[1], D), BF),
        compiler_params=_params(), name=name)(a, u1)


def in_weight_grad(parts, first_rows, ddt, u1):
    tm = GRAD_TILE
    per_row = D // 128
    dt_row, dt_lines = 2560, 8 * per_row
    parts = list(parts) + [ddt]
    first_rows = list(first_rows) + [dt_row]
    tiles = [p.shape[1] // tm for p in parts]
    starts = [sum(tiles[:k]) for k in range(len(parts))]
    last = sum(tiles) - 1

    def body(*refs):
        a_refs, u_ref = refs[:len(parts)], refs[len(parts)]
        o_hbm, mix_scr, stage, sems = refs[len(parts) + 1:]
        step = pl.program_id(0)
        slot = step % 2
        line0 = 0
        for a_ref, start, n, first in zip(a_refs, starts, tiles, first_rows):
            here = (step >= start) & (step < start + n)
            line0 = jnp.where(here, per_row * (first + tm * (step - start)), line0)

            @pl.when(here)
            def _(a_ref=a_ref):
                res = lax.dot_general(a_ref[...], u_ref[...], TN_DIMS, preferred_element_type=F32)
                for q in range(per_row):
                    mix_scr[pl.ds(q, tm, stride=per_row), :] = res[:, 128 * q:128 * q + 128]

        def tile_copy(of_slot, to):
            return pltpu.make_async_copy(stage.at[of_slot], o_hbm.at[pl.ds(to, per_row * tm)], sems.at[of_slot])

        @pl.when(step >= 2)
        def _():
            tile_copy(slot, 0).wait()

        stage[slot] = mix_scr[...].astype(BF)

        @pl.when(step < last)
        def _():
            tile_copy(slot, pl.multiple_of(line0, 128)).start()

        @pl.when(step == last)
        def _():
            halves = [pltpu.make_async_copy(stage.at[slot, pl.ds(128 * per_row * k, dt_lines)],
                                            o_hbm.at[pl.ds(per_row * (dt_row + 8 * k), dt_lines)], sems.at[2 + k]) for k in range(2)]
            for cp in halves:
                cp.start()
            tile_copy(1 - slot, 0).wait()
            for cp in halves:
                cp.wait()

    def tile_of(start, n):
        return lambda j: (0, jnp.clip(j - start, 0, n - 1))

    return pl.pallas_call(
        body, grid=(last + 1,),
        in_specs=[_spec((T, tm), tile_of(s, n)) for s, n in zip(starts, tiles)] + [_spec((T, D), lambda j: (0, 0), single=True)],
        out_specs=ANY_SPEC,
        out_shape=jax.ShapeDtypeStruct((per_row * IN_COLS, 128), BF),
        scratch_shapes=[pltpu.VMEM((per_row * tm, 128), F32), pltpu.VMEM((2, per_row * tm, 128), BF), pltpu.SemaphoreType.DMA((4,))],
        compiler_params=_params(), name="dw_in")(*parts, u1)


def _ssd_chunk_common(row0, dt_ref, b_ref, c_ref, bias, a_neg):
    shape = (Q, Q)
    lane = _lanes(shape)
    sub = _rows(shape)
    live = (_rows(shape, row0) >= NPAD) & (lane < 8)
    dtr = dt_ref[:, :]
    dt = jnp.where(live, _softplus(dtr + bias), 0.0)
    d_a = dt * a_neg
    tri = (sub >= lane).astype(F32)
    cs = jnp.dot(tri, d_a, precision=lax.Precision.HIGHEST, preferred_element_type=F32)
    cs_t = cs.T
    b_f = b_ref[:, :]
    bc = b_f.astype(BF)
    cc = c_ref[:, :].astype(BF)
    cb = lax.dot_general(cc, bc, NT_DIMS, preferred_element_type=F32)
    cs_last = cs[Q - 1:Q, :]
    return dict(lane=lane, sub=sub, live=live, dtr=dtr, dt=dt, cs=cs, cs_t=cs_t, bc=bc, cc=cc, cb=cb, bc_t=b_f.T.astype(BF),
                ecs=jnp.exp(cs), dsm=jnp.exp(cs_last - cs), gam=jnp.exp(cs_last))


def _pair(lane_even, mat, j):
    return jnp.where(lane_even, mat[:, j:j + 1], mat[:, j + 1:j + 2])


def _pair_row(lane_even, mat, j):
    return jnp.where(lane_even[0:1, :], mat[:, j:j + 1], mat[:, j + 1:j + 2])


def _head_decay(cm, j):
    seg = cm["cs"][:, j:j + 1] - cm["cs_t"][j:j + 1, :]
    return jnp.exp(jnp.where(cm["sub"] >= cm["lane"], seg, -jnp.inf))


def _head_decay_t(cm, j):
    seg = cm["cs_t"][j:j + 1, :] - cm["cs"][:, j:j + 1]
    return jnp.exp(jnp.where(cm["lane"] >= cm["sub"], seg, -jnp.inf))


def _conv_window(raw_ref, halo_ref, pad_scr):
    pad_scr[0:HALO, :] = halo_ref[...].astype(F32)[halo_ref.shape[0] - HALO:, :]
    pad_scr[HALO:HALO + Q, :] = raw_ref[...].astype(F32)
    win = pad_scr[...]
    return lambda s: win[HALO:, :] if s == 0 else pltpu.roll(win, s, axis=0)[HALO:, :]


def _xbc_cols(g):
    return slice(512 * g, 512 * g + 512), slice(SSD_W + 128 * g, SSD_W + 128 * g + 128), slice(SSD_W + 256 + 128 * g, SSD_W + 384 + 128 * g)


def ssd_fwd(proj, dt_raw, conv_w, conv_b, heads, norm_w):
    def body(raw_ref, halo_ref, dt_all, z_all, cw_ref, cb_ref, bias_all, alog_all, d_all, nw_all, yn_all, y_all, hp_all,
             h_all, pad_scr, act_scr):
        @pl.when(pl.program_id(0) == 0)
        def _():
            h_all[...] = jnp.zeros_like(h_all)

        pre = _conv(_conv_window(raw_ref, halo_ref, pad_scr), cw_ref[...], cb_ref[...])
        act_scr[...] = pre * _sigmoid(pre)
        for g in range(2):
            wide, thin = slice(512 * g, 512 * g + 512), slice(128 * g, 128 * g + 128)
            xs, bs, cs = _xbc_cols(g)
            group(act_scr.at[:, xs], act_scr.at[:, bs], act_scr.at[:, cs], dt_all.at[:, thin], z_all.at[:, wide], bias_all.at[g],
                  alog_all.at[g], d_all.at[g], nw_all.at[:, wide], yn_all.at[:, wide], y_all.at[:, wide], hp_all.at[g, 0], h_all.at[g])

    def group(x_ref, b_ref, c_ref, dt_ref, z_ref, bias_ref, alog_ref, d_ref, nw_ref, yn_ref, y_ref, hp_ref, h_scr):
        bias = bias_ref[...]
        a_neg = -jnp.exp(alog_ref[...])
        dsk = d_ref[...]
        cm = _ssd_chunk_common(pl.program_id(0) * Q, dt_ref, b_ref, c_ref, bias, a_neg)
        lane_even = cm["lane"] < 64
        for p in range(4):
            je, jo = 2 * p, 2 * p + 1
            xp = x_ref[:, 128 * p:128 * p + 128]
            xdt = xp * _pair(lane_even, cm["dt"], je)
            xdt_b = xdt.astype(BF)
            m_e = (cm["cb"] * _head_decay(cm, je)).astype(BF)
            m_o = (cm["cb"] * _head_decay(cm, jo)).astype(BF)
            zero = jnp.zeros_like(xdt_b)
            yd = (jnp.dot(m_e, jnp.where(lane_even, xdt_b, zero), preferred_element_type=F32)
                  + jnp.dot(m_o, jnp.where(lane_even, zero, xdt_b), preferred_element_type=F32))
            hp = h_scr[p]
            hp_ref[p] = hp
            yo = jnp.dot(cm["cc"], hp.astype(BF), preferred_element_type=F32) * _pair(lane_even, cm["ecs"], je)
            y_ref[:, 128 * p:128 * p + 128] = yd + yo + xp * _pair_row(lane_even, dsk, je)
            st = jnp.dot(cm["bc_t"], (xdt * _pair(lane_even, cm["dsm"], je)).astype(BF), preferred_element_type=F32)
            h_scr[p] = hp * _pair_row(lane_even, cm["gam"], je) + st
        zc = z_ref[:, :].astype(F32)
        gated = y_ref[:, :] * (zc * _sigmoid(zc))
        yn_ref[:, :] = _rms(gated, nw_ref[...]).astype(BF)

    par = [_spec((None, 2, 1, 128), lambda c, k=k: (k, 0, 0, 0)) for k in range(3)]
    wide = _spec((Q, SSD_W), lambda c: (c, 0))
    xbc = PXBC // XBC
    halo = 2 * HALO
    return pl.pallas_call(
        body, grid=(NCH,),
        in_specs=[_spec((Q, XBC), lambda c: (c, xbc)), _spec((halo, XBC), lambda c: (jnp.maximum(c * (Q // halo) - 1, 0), xbc)),
                  _spec((Q, 256), lambda c: (c, 0)), wide, _spec((4, XBC), lambda c: (0, 0)), _spec((1, XBC), lambda c: (0, 0)),
                  *par, _spec((1, SSD_W), lambda c: (0, 0))],
        out_specs=[wide, wide, _spec((2, 1, 4, 128, 128), lambda c: (0, c, 0, 0, 0))],
        out_shape=[jax.ShapeDtypeStruct((T, SSD_W), BF), jax.ShapeDtypeStruct((T, SSD_W), F32),
                   jax.ShapeDtypeStruct((2, NCH, 4, 128, 128), F32)],
        scratch_shapes=[pltpu.VMEM((2, 4, 128, 128), F32), pltpu.VMEM((Q + HALO, XBC), F32), pltpu.VMEM((Q, XBC), F32)],
        compiler_params=_params(), name="ssd_fwd")(proj, proj, dt_raw, proj, conv_w, conv_b, heads, heads, heads, norm_w)


def ssd_bwd(dyn, proj, dt_raw, conv_w, conv_b, y_pre, h_prev, heads, norm_w, after):
    def body(dyn_all, raw_ref, halo_ref, dt_all, z_all, y_all, hp_all, cw_ref, cb_ref, bias_all, alog_all, d_all, nw_all, _after,
             dz_all, dxbc_ref, ddt_all, dpar_all, dnw_all, dcw_ref, dcb_ref, dh_all, acc_all, pad_scr, act_scr, dsilu_scr, dact_scr, dpad_scr):
        @pl.when(pl.program_id(0) == 0)
        def _():
            dh_all[...] = jnp.zeros_like(dh_all)
            acc_all[...] = jnp.zeros_like(acc_all)
            dnw_all[...] = jnp.zeros_like(dnw_all)
            dcw_ref[...] = jnp.zeros_like(dcw_ref)
            dcb_ref[...] = jnp.zeros_like(dcb_ref)
            dpad_scr[Q:Q + HALO, :] = jnp.zeros((HALO, XBC), F32)

        back = _conv_window(raw_ref, halo_ref, pad_scr)
        pre = _conv(back, cw_ref[...], cb_ref[...])
        sg = _sigmoid(pre)
        act_scr[...] = pre * sg
        dsilu_scr[...] = sg * (1.0 + pre * (1.0 - sg))
        for g in range(2):
            wide, thin = slice(512 * g, 512 * g + 512), slice(128 * g, 128 * g + 128)
            xs, bs, cs = _xbc_cols(g)
            group(dyn_all.at[:, wide], act_scr.at[:, xs], act_scr.at[:, bs], act_scr.at[:, cs], dt_all.at[:, thin], z_all.at[:, wide],
                  y_all.at[:, wide], hp_all.at[g, 0], bias_all.at[g], alog_all.at[g], d_all.at[g], nw_all.at[:, wide],
                  dz_all.at[:, wide], dact_scr.at[:, xs], dact_scr.at[:, bs], dact_scr.at[:, cs], ddt_all.at[:, thin], dpar_all.at[g],
                  dnw_all.at[:, wide], dh_all.at[g], acc_all.at[g])
        dpre = dact_scr[...] * dsilu_scr[...]
        dcw, dcb = _conv_bwd_w(dpre, back)
        dcw_ref[...] += dcw
        dcb_ref[...] += dcb
        dpad_scr[0:Q, :] = dpre
        win = dpad_scr[...]
        dxbc_ref[...] = _conv_bwd_x(lambda s: win[:Q, :] if s == 0 else pltpu.roll(win, Q + HALO - s, axis=0)[:Q, :], cw_ref[...]).astype(BF)
        dpad_scr[Q:Q + HALO, :] = dpre[0:HALO, :]

    def group(dyn_ref, x_ref, b_ref, c_ref, dt_ref, z_ref, y_ref, hp_ref, bias_ref, alog_ref, d_ref, nw_ref,
              dz_ref, dx_ref, db_ref, dc_ref, ddt_ref, dpar_ref, dnw_ref, dh_scr, acc_scr):
        ci = pl.program_id(0)
        bias = bias_ref[...]
        a_neg = -jnp.exp(alog_ref[...])
        dsk = d_ref[...]
        cm = _ssd_chunk_common((NCH - 1 - ci) * Q, dt_ref, b_ref, c_ref, bias, a_neg)
        lane, sub = cm["lane"], cm["sub"]
        lane_even = lane < 64
        cc_t = c_ref[:, :].T.astype(BF)
        cb_t = lax.dot_general(cm["bc"], cm["cc"], NT_DIMS, preferred_element_type=F32)
        zc = z_ref[:, :].astype(F32)
        yc = y_ref[:, :]
        sg = _sigmoid(zc)
        sz = zc * sg
        dgated, dnw = _rms_bwd(dyn_ref[:, :], yc * sz, nw_ref[...])
        dnw_ref[...] += jnp.sum(dnw, axis=0, keepdims=True)
        dz_ref[:, :] = (dgated * yc * (sg * (1.0 + zc * (1.0 - sg)))).astype(BF)
        dy_all = dgated * sz
        dcb = jnp.zeros((Q, Q), F32)
        dcb_t = jnp.zeros((Q, Q), F32)
        db_acc = jnp.zeros((Q, Q), F32)
        dc_acc = jnp.zeros((Q, Q), F32)
        dcs = jnp.zeros((Q, Q), F32)
        ddt = jnp.zeros((Q, Q), F32)
        for p in range(4):
            je, jo = 2 * p, 2 * p + 1
            xp = x_ref[:, 128 * p:128 * p + 128]
            dy = dy_all[:, 128 * p:128 * p + 128]
            dt_p = _pair(lane_even, cm["dt"], je)
            xdt = xp * dt_p
            xdt_b = xdt.astype(BF)
            dy_b = dy.astype(BF)
            zero = jnp.zeros_like(dy_b)
            hp = hp_ref[p]
            hp_b = hp.astype(BF)
            dh = dh_scr[p]
            dh_b = dh.astype(BF)
            acc_scr[p:p + 1, :] += jnp.sum(dy * xp, axis=0, keepdims=True)
            dxp = dy * _pair_row(lane_even, dsk, je)
            e_p = _pair(lane_even, cm["ecs"], je)
            g_p = jnp.dot(cm["cc"], hp_b, preferred_element_type=F32)
            dg_b = (dy * e_p).astype(BF)
            de = dy * g_p * e_p
            dc_acc = dc_acc + lax.dot_general(dg_b, hp_b, NT_DIMS, preferred_element_type=F32)
            dh_in = jnp.dot(cc_t, dg_b, preferred_element_type=F32)
            ds_p = _pair(lane_even, cm["dsm"], je)
            r_p = jnp.dot(cm["bc"], dh_b, preferred_element_type=F32)
            dxdt = r_p * ds_p
            tt = r_p * xdt * ds_p
            db_acc = db_acc + lax.dot_general((xdt * ds_p).astype(BF), dh_b, NT_DIMS, preferred_element_type=F32)
            dgam_m = jnp.sum(dh * hp, axis=0, keepdims=True)
            for j, even in ((je, True), (jo, False)):
                sel = lane_even if even else jnp.logical_not(lane_even)
                dy_j = jnp.where(sel, dy_b, zero)
                l_j = _head_decay(cm, j)
                l_jt = _head_decay_t(cm, j)
                m_j = cm["cb"] * l_j
                m_jt = cb_t * l_jt
                dm = lax.dot_general(dy_j, xdt_b, NT_DIMS, preferred_element_type=F32)
                dm_t = lax.dot_general(xdt_b, dy_j, NT_DIMS, preferred_element_type=F32)
                dxdt = dxdt + jnp.dot(m_jt.astype(BF), dy_j, preferred_element_type=F32)
                dcb = dcb + dm * l_j
                dcb_t = dcb_t + dm_t * l_jt
                t_j = jnp.where(sel, tt, 0.0)
                col = jnp.sum(dm * m_j - dm_t * m_jt + (jnp.where(sel, de, 0.0) - t_j), axis=1, keepdims=True)
                gam_j = cm["gam"][:, j:j + 1]
                last = (jnp.sum(jnp.sum(t_j, axis=0, keepdims=True), axis=1, keepdims=True)
                        + jnp.sum(jnp.where(sel[0:1, :], dgam_m, 0.0), axis=1, keepdims=True) * gam_j)
                col = col + jnp.where(sub[:, 0:1] == Q - 1, last, 0.0)
                dcs = dcs + jnp.where(lane == j, col, 0.0)
            dh_scr[p] = dh_in + dh * _pair_row(lane_even, cm["gam"], je)
            dx_ref[:, 128 * p:128 * p + 128] = dxp + dxdt * dt_p
            dd = dxdt * xp
            ddt = ddt + jnp.where(lane == je, jnp.sum(jnp.where(lane_even, dd, 0.0), axis=1, keepdims=True), 0.0)
            ddt = ddt + jnp.where(lane == jo, jnp.sum(jnp.where(lane_even, 0.0, dd), axis=1, keepdims=True), 0.0)
        dc_ref[:, :] = dc_acc + jnp.dot(dcb.astype(BF), cm["bc"], preferred_element_type=F32)
        db_ref[:, :] = db_acc + jnp.dot(dcb_t.astype(BF), cm["cc"], preferred_element_type=F32)
        tri_t = (sub <= lane).astype(F32)
        dd_a = jnp.dot(tri_t, dcs, precision=lax.Precision.HIGHEST, preferred_element_type=F32)
        ddt = ddt + dd_a * a_neg
        acc_scr[5:6, :] += jnp.sum(dd_a * cm["dt"], axis=0, keepdims=True)
        draw = jnp.where(cm["live"], ddt * _sigmoid_gate(cm["dtr"] + bias), 0.0)
        acc_scr[4:5, :] += jnp.sum(draw, axis=0, keepdims=True)
        ddt_ref[:, :] = draw.astype(BF)

        @pl.when(ci == NCH - 1)
        def _():
            lane1 = _lanes((1, 128))
            dd = jnp.zeros((1, 128), F32)
            for p in range(4):
                row = acc_scr[p:p + 1, :]
                dd = dd + jnp.where(lane1 == 2 * p, jnp.sum(jnp.where(lane1 < 64, row, 0.0), axis=1, keepdims=True), 0.0)
                dd = dd + jnp.where(lane1 == 2 * p + 1, jnp.sum(jnp.where(lane1 < 64, 0.0, row), axis=1, keepdims=True), 0.0)
            dpar_ref[...] = jnp.concatenate([acc_scr[4:5, :], acc_scr[5:6, :] * a_neg, dd, jnp.zeros((5, 128), F32)], axis=0)

    par = [_spec((None, 2, 1, 128), lambda c, k=k: (k, 0, 0, 0)) for k in range(3)]
    wide = _spec((Q, SSD_W), lambda c: (NCH - 1 - c, 0))
    thin = _spec((Q, 256), lambda c: (NCH - 1 - c, 0))
    vec = _spec((1, SSD_W), lambda c: (0, 0))
    xbc = PXBC // XBC
    halo = 2 * HALO
    chunk = pltpu.VMEM((Q, XBC), F32)
    padded = pltpu.VMEM((Q + HALO, XBC), F32)
    return pl.pallas_call(
        body, grid=(NCH,),
        in_specs=[wide, _spec((Q, XBC), lambda c: (NCH - 1 - c, xbc)),
                  _spec((halo, XBC), lambda c: (jnp.maximum((NCH - 1 - c) * (Q // halo) - 1, 0), xbc)), thin, wide, wide,
                  _spec((2, 1, 4, 128, 128), lambda c: (0, NCH - 1 - c, 0, 0, 0)), _spec((4, XBC), lambda c: (0, 0)),
                  _spec((1, XBC), lambda c: (0, 0)), *par, vec, ANY_SPEC],
        out_specs=[wide, _spec((Q, XBC), lambda c: (NCH - 1 - c, 0)), thin, _spec((2, 8, 128), lambda c: (0, 0, 0)), vec,
                   _spec((4, XBC), lambda c: (0, 0)), _spec((1, XBC), lambda c: (0, 0))],
        out_shape=[jax.ShapeDtypeStruct((T, SSD_W), BF), jax.ShapeDtypeStruct((T, XBC), BF), jax.ShapeDtypeStruct((T, 256), BF),
                   jax.ShapeDtypeStruct((2, 8, 128), F32), jax.ShapeDtypeStruct((1, SSD_W), F32), jax.ShapeDtypeStruct((4, XBC), F32),
                   jax.ShapeDtypeStruct((1, XBC), F32)],
        scratch_shapes=[pltpu.VMEM((2, 4, 128, 128), F32), pltpu.VMEM((2, 8, 128), F32), padded, chunk, chunk, chunk, padded],
        compiler_params=_params(), name="ssd_bwd")(dyn, proj, proj, dt_raw, proj, y_pre, h_prev, conv_w, conv_b, heads, heads, heads, norm_w, after)


def _lru_gates(back, cw, cb, wa, ba, wx, bx, lam):
    xr = _conv(back, cw, cb)
    xr_b = xr.astype(BF)
    r = _sigmoid_gate(jnp.dot(xr_b, wa, preferred_element_type=F32) + ba)
    i = _sigmoid_gate(jnp.dot(xr_b, wx, preferred_element_type=F32) + bx)
    sp = _softplus(-lam)
    la = (-LRU_C) * r * sp
    a = jnp.exp(la)
    mult2 = -jnp.tanh(la) * (a * a + 1.0)
    return xr, xr_b, r, i, sp, a, jnp.sqrt(mult2), mult2


SEG_LEN = 68
SEGS = T // SEG_LEN


def _seg_rows(j, k, off=0):
    return pl.ds(off + j * 8 * SEG_LEN + k, 8, stride=SEG_LEN)


def _segmented_scan(mul_ref, mul_row0, add_ref, out_ref, loc_scr, prod_scr, carry_scr, reverse):
    groups = SEGS // 8
    off = mul_row0 + (1 if reverse else 0)

    def local(i, carry):
        k = SEG_LEN - 1 - i if reverse else i
        new = []
        for j in range(groups):
            h, p = carry[2 * j], carry[2 * j + 1]
            m = mul_ref[_seg_rows(j, k, off), :]
            h = m * h + add_ref[_seg_rows(j, k), :]
            p = m * p
            loc_scr[_seg_rows(j, k), :] = h
            prod_scr[_seg_rows(j, k), :] = p
            new += [h, p]
        return tuple(new)

    lax.fori_loop(0, SEG_LEN, local, (jnp.zeros((8, 128), F32), jnp.ones((8, 128), F32)) * groups)

    def chain(i, c):
        s = SEGS - 1 - i if reverse else i
        carry_scr[pl.ds(s, 1), :] = c
        edge = s * SEG_LEN + (0 if reverse else SEG_LEN - 1)
        return loc_scr[pl.ds(edge, 1), :] + prod_scr[pl.ds(edge, 1), :] * c

    lax.fori_loop(0, SEGS, chain, jnp.zeros((1, 128), F32))

    def fold(k, carry):
        for j in range(groups):
            rows = _seg_rows(j, k)
            out_ref[rows, :] = loc_scr[rows, :] + prod_scr[rows, :] * carry_scr[8 * j:8 * j + 8, :]
        return carry

    lax.fori_loop(0, SEG_LEN, fold, 0)


def lru_fwd(proj, cw, cb, wa2, ba, wx2, bx, lam, after):
    def body(x_ref, cw_ref, cb_ref, wa_ref, ba_ref, wx_ref, bx_ref, lam_ref, _after, h_ref, a_ref, xpad, u_scr, loc_scr, prod_scr, carry_scr):
        _fill_padded(xpad, x_ref)

        def chunk(r0):
            xr, _, _, i, _, a, mult, _ = _lru_gates(_back(xpad, r0), cw_ref[...], cb_ref[...], wa_ref[0], ba_ref[...], wx_ref[0], bx_ref[...],
                                                 lam_ref[...])
            a_ref[pl.ds(r0, Q), :] = a
            u_scr[pl.ds(r0, Q), :] = jnp.where(_rows(a.shape, r0) >= NPAD, mult * (i * xr), 0.0)

        _chunks(chunk, unrolled=True)
        _segmented_scan(a_ref, 0, u_scr, h_ref, loc_scr, prod_scr, carry_scr, reverse=False)

    c0 = PXL // 128
    vec = _spec((1, 128), lambda c: (0, c))
    mat = _spec((1, 128, 128), lambda c: (c, 0, 0))
    seq = pltpu.VMEM((T, 128), F32)
    return pl.pallas_call(
        body, grid=(8,),
        in_specs=[_spec((T, 128), lambda c: (0, c0 + c)), _spec((4, 128), lambda c: (0, c)), vec, mat, vec, mat, vec, vec, ANY_SPEC],
        out_specs=[_spec((T, 128), lambda c: (0, c)), _spec((T, 128), lambda c: (0, c))],
        out_shape=[jax.ShapeDtypeStruct((T, LRU_W), F32), jax.ShapeDtypeStruct((T, LRU_W), F32)],
        scratch_shapes=[pltpu.VMEM((T + 2 * HALO, 128), F32), seq, seq, seq, pltpu.VMEM((SEGS, 128), F32)],
        compiler_params=_params(), name="lru_fwd")(proj, cw, cb, wa2, ba, wx2, bx, lam, after)


def lru_bwd(dh_out, a, hseq, proj, cw, cb, wa2, ba, wx2, bx, lam):
    def body(d_ref, a_ref, h_ref, x_ref, cw_ref, cb_ref, wa_ref, ba_ref, wx_ref, bx_ref, lam_ref,
             dx_ref, dcw_ref, dcb_ref, dwa_ref, dba_ref, dwx_ref, dbx_ref, dlam_ref, xpad, hpad, dpad, dh_ref, loc_scr, prod_scr, carry_scr):
        _fill_padded(dpad, a_ref)
        _segmented_scan(dpad, HALO, d_ref, dh_ref, loc_scr, prod_scr, carry_scr, reverse=True)
        _fill_padded(xpad, x_ref)
        _fill_padded(hpad, h_ref)
        dpad[0:HALO, :] = jnp.zeros((HALO, 128), F32)
        dpad[T + HALO:T + 2 * HALO, :] = jnp.zeros((HALO, 128), F32)
        for ref in (dcw_ref, dcb_ref, dwa_ref, dba_ref, dwx_ref, dbx_ref, dlam_ref):
            ref[...] = jnp.zeros_like(ref)
        lam = lam_ref[...]

        def first(r0):
            back = _back(xpad, r0)
            xr, xr_b, r, i, sp, a, mult, mult2 = _lru_gates(back, cw_ref[...], cb_ref[...], wa_ref[0], ba_ref[...], wx_ref[0], bx_ref[...], lam)
            dh = dh_ref[pl.ds(r0, Q), :]
            da = dh * _back(hpad, r0)(1)
            du = jnp.where(_rows(dh.shape, r0) >= NPAD, dh, 0.0)
            dmult = du * (i * xr)
            di = du * (mult * xr)
            dxr = du * (mult * i)
            dla = da * a - dmult * (a * a) * lax.rsqrt(mult2)
            dr = dla * ((-LRU_C) * sp)
            dlam_ref[...] += jnp.sum(dla * ((-LRU_C) * r), axis=0, keepdims=True)
            dpr = dr * r * (1.0 - r)
            dpi = di * i * (1.0 - i)
            dba_ref[...] += jnp.sum(dpr, axis=0, keepdims=True)
            dbx_ref[...] += jnp.sum(dpi, axis=0, keepdims=True)
            dpr_b = dpr.astype(BF)
            dpi_b = dpi.astype(BF)
            dxr = (dxr + lax.dot_general(dpr_b, wa_ref[0], NT_DIMS, preferred_element_type=F32)
                   + lax.dot_general(dpi_b, wx_ref[0], NT_DIMS, preferred_element_type=F32))
            dwa_ref[0] += lax.dot_general(xr_b, dpr_b, TN_DIMS, preferred_element_type=F32)
            dwx_ref[0] += lax.dot_general(xr_b, dpi_b, TN_DIMS, preferred_element_type=F32)
            dpad[pl.ds(r0 + HALO, Q), :] = dxr
            dcw, dcb = _conv_bwd_w(dxr, back)
            dcw_ref[...] += dcw
            dcb_ref[...] += dcb

        _chunks(first, unrolled=True)
        dlam_ref[...] = -dlam_ref[...] * _sigmoid_gate(-lam)

        def second(r0):
            dx_ref[pl.ds(r0, Q), :] = _conv_bwd_x(_ahead(dpad, r0), cw_ref[...]).astype(BF)

        _chunks(second)

    c0 = PXL // 128
    vec = _spec((1, 128), lambda c: (0, c))
    mat = _spec((1, 128, 128), lambda c: (c, 0, 0))
    col = _spec((T, 128), lambda c: (0, c))
    vshape = jax.ShapeDtypeStruct((1, LRU_W), F32)
    mshape = jax.ShapeDtypeStruct((8, 128, 128), F32)
    pad = pltpu.VMEM((T + 2 * HALO, 128), F32)
    seq = pltpu.VMEM((T, 128), F32)
    return pl.pallas_call(
        body, grid=(8,),
        in_specs=[col, col, col, _spec((T, 128), lambda c: (0, c0 + c)), _spec((4, 128), lambda c: (0, c)), vec, mat, vec, mat, vec, vec],
        out_specs=[col, _spec((4, 128), lambda c: (0, c)), vec, mat, vec, mat, vec, vec],
        out_shape=[jax.ShapeDtypeStruct((T, LRU_W), BF), jax.ShapeDtypeStruct((4, LRU_W), F32), vshape, mshape, vshape, mshape, vshape, vshape],
        scratch_shapes=[pad, pad, pad, seq, seq, seq, pltpu.VMEM((SEGS, 128), F32)],
        compiler_params=_params(), name="lru_bwd")(dh_out, a, hseq, proj, cw, cb, wa2, ba, wx2, bx, lam)


FF_TILE = 256
FF_TILE_ROWS = list(range(0, D_FF, FF_TILE))


def gate_up(h1, wn, w_gate, w_up):
    def body(h_ref, wn_ref, wg_hbm, wu_hbm, gt_ref, up_ref, act_ref, u_ref, wg_ref, wu_ref, wg_sems, wu_sems):
        gate_ready = _arriving(wg_hbm, wg_ref, wg_sems, FF_TILE_ROWS, FF_TILE)
        up_ready = _arriving(wu_hbm, wu_ref, wu_sems, FF_TILE_ROWS, FF_TILE)
        for r in (0, HALF):
            u_ref[r:r + HALF, :] = _rms(h_ref[r:r + HALF, :], wn_ref[...]).astype(BF)

        def tile(c0):
            cols = pl.ds(c0, FF_TILE)
            gate_ready(c0 // FF_TILE)
            up_ready(c0 // FF_TILE)
            gt = lax.dot_general(u_ref[...], wg_ref[cols, :], NT_DIMS, preferred_element_type=F32)
            up = lax.dot_general(u_ref[...], wu_ref[cols, :], NT_DIMS, preferred_element_type=F32)
            gt_ref[:, cols] = gt.astype(BF)
            up_ref[:, cols] = up.astype(BF)
            act_ref[:, cols] = (gt * _sigmoid(gt) * up).astype(BF)

        _col_tiles(D_FF, FF_TILE, tile)

    big = jax.ShapeDtypeStruct((T, D_FF), BF)
    return pl.pallas_call(
        body, grid=(T // RC,), in_specs=[_rows_spec(D), _vec(D), ANY_SPEC, ANY_SPEC],
        out_specs=[_rows_spec(D_FF), _rows_spec(D_FF), _rows_spec(D_FF), _rows_spec(D)],
        out_shape=[big, big, big, jax.ShapeDtypeStruct((T, D), BF)],
        scratch_shapes=[pltpu.VMEM((D_FF, D), BF)] * 2 + [pltpu.SemaphoreType.DMA((len(FF_TILE_ROWS),))] * 2,
        compiler_params=_params(), name="gate_up")(h1, wn, w_gate, w_up)


def down_loss(act, w_down, h1, target, wf):
    first = NPAD + N_META

    def body(a_ref, w_ref, r_ref, t_hbm, wf_ref, d_ref, db_ref, l_ref, dw_ref, h_scr, t_ref, t_sem):
        i = pl.program_id(0)
        _zero_at_first(l_ref, dw_ref)
        head = pltpu.make_async_copy(t_hbm.at[pl.ds(0, RC - first)], t_ref.at[pl.ds(first, RC - first)], t_sem)
        rest = pltpu.make_async_copy(t_hbm.at[pl.ds(pl.multiple_of(jnp.maximum(i * RC - first, 0), 32), RC)], t_ref, t_sem)

        @pl.when(i == 0)
        def _():
            t_ref[0:first, :] = jnp.zeros((first, D), F32)
            head.start()

        @pl.when(i > 0)
        def _():
            rest.start()

        def tile(c0):
            cols = pl.ds(c0, 512)
            h_scr[:, cols] = r_ref[:, cols] + jnp.dot(a_ref[...], w_ref[:, cols], preferred_element_type=F32)

        _col_tiles(D, 512, tile)

        @pl.when(i == 0)
        def _():
            head.wait()

        @pl.when(i > 0)
        def _():
            rest.wait()

        for r in (0, HALF):
            h = h_scr[r:r + HALF, :]
            live = _rows((HALF, D), i * RC + r) >= first
            err = jnp.where(live, _rms(h, wf_ref[...]) - t_ref[r:r + HALF, :], 0.0)
            l_ref[...] += 0.5 * jnp.sum(jnp.sum(err * err, axis=1, keepdims=True) * (1.0 / D), axis=0, keepdims=True)
            dh, dw = _rms_bwd(err * (1.0 / D), h, wf_ref[...])
            dw_ref[...] += jnp.sum(dw, axis=0, keepdims=True)
            d_ref[r:r + HALF, :] = dh
            db_ref[r:r + HALF, :] = dh.astype(BF)

    return pl.pallas_call(
        body, grid=(T // RC,),
        in_specs=[_rows_spec(D_FF), _whole((D_FF, D)), _rows_spec(D), pl.BlockSpec(memory_space=pl.ANY), _vec(D)],
        out_specs=[_rows_spec(D), _rows_spec(D), _spec((1, 128), lambda i: (0, 0)), _vec(D)],
        out_shape=[jax.ShapeDtypeStruct((T, D), F32), jax.ShapeDtypeStruct((T, D), BF), jax.ShapeDtypeStruct((1, 128), F32),
                   jax.ShapeDtypeStruct((1, D), F32)],
        scratch_shapes=[pltpu.VMEM((RC, D), F32), pltpu.VMEM((RC, D), F32), pltpu.SemaphoreType.DMA],
        compiler_params=_params(), name="down_loss")(act, w_down, h1, target, wf)


def swiglu_bwd(dh2_b, w_down, gt, up, act, u2):
    tn = 256

    def body(d_hbm, u_hbm, w_ref, gt_ref, up_ref, act_ref, dg_ref, du_ref, gd_ref, gg_ref, gu_ref, d_ref, u_ref, d_sems, u_sems):
        chunks = list(range(0, T, RC))
        d_ready = _arriving(d_hbm, d_ref, d_sems, chunks, RC)
        u_ready = _arriving(u_hbm, u_ref, u_sems, chunks, RC)

        def rows(r0):
            part = pl.ds(r0, RC)
            d_ready(r0 // RC)
            dact = lax.dot_general(d_ref[part, :], w_ref[...], NT_DIMS, preferred_element_type=F32)
            gt_ = gt_ref[part, :].astype(F32)
            up_ = up_ref[part, :].astype(F32)
            sg = _sigmoid(gt_)
            dg_ref[part, :] = (dact * up_ * (sg * (1.0 + gt_ * (1.0 - sg)))).astype(BF)
            du_ref[part, :] = (dact * (gt_ * sg)).astype(BF)

        _col_tiles(T, RC, rows)
        for k in range(len(chunks)):
            u_ready(k)
        gd_ref[...] = lax.dot_general(act_ref[...], d_ref[...], TN_DIMS, preferred_element_type=F32).astype(BF)
        gg_ref[...] = lax.dot_general(dg_ref[...], u_ref[...], TN_DIMS, preferred_element_type=F32).astype(BF)
        gu_ref[...] = lax.dot_general(du_ref[...], u_ref[...], TN_DIMS, preferred_element_type=F32).astype(BF)

    cols = _spec((T, tn), lambda j: (0, j))
    wrow = _spec((tn, D), lambda j: (j, 0))
    big = jax.ShapeDtypeStruct((T, D_FF), BF)
    grad = jax.ShapeDtypeStruct((D_FF, D), BF)
    return pl.pallas_call(
        body, grid=(D_FF // tn,), in_specs=[ANY_SPEC, ANY_SPEC, wrow, cols, cols, cols],
        out_specs=[cols, cols, wrow, wrow, wrow], out_shape=[big, big, grad, grad, grad],
        scratch_shapes=[pltpu.VMEM((T, D), BF)] * 2 + [pltpu.SemaphoreType.DMA((T // RC,))] * 2,
        compiler_params=_params(), name="swiglu_bwd")(dh2_b, u2, w_down, gt, up, act)


def gate_up_bwd(dgt, dup, w_gate, w_up, h1, wn, dh2):
    def body(dg_ref, du_ref, wg_hbm, wu_hbm, h_ref, wn_ref, r_ref, d_ref, db_ref, dw_ref, du_scr, wg_ref, wu_ref, wg_sems, wu_sems):
        gate_ready = _arriving(wg_hbm, wg_ref, wg_sems, FF_TILE_ROWS, FF_TILE)
        up_ready = _arriving(wu_hbm, wu_ref, wu_sems, FF_TILE_ROWS, FF_TILE)
        _zero_at_first(dw_ref)

        du_scr[...] = jnp.zeros_like(du_scr)

        def tile(c0):
            k = pl.ds(c0, FF_TILE)
            gate_ready(c0 // FF_TILE)
            up_ready(c0 // FF_TILE)
            du_scr[...] += (jnp.dot(dg_ref[:, k], wg_ref[k, :], preferred_element_type=F32)
                            + jnp.dot(du_ref[:, k], wu_ref[k, :], preferred_element_type=F32))

        _col_tiles(D_FF, FF_TILE, tile)
        for r in (0, HALF):
            dh, dw = _rms_bwd(du_scr[r:r + HALF, :], h_ref[r:r + HALF, :], wn_ref[...])
            dw_ref[...] += jnp.sum(dw, axis=0, keepdims=True)
            dh = dh + r_ref[r:r + HALF, :]
            d_ref[r:r + HALF, :] = dh
            db_ref[r:r + HALF, :] = dh.astype(BF)

    return pl.pallas_call(
        body, grid=(T // RC,),
        in_specs=[_rows_spec(D_FF), _rows_spec(D_FF), ANY_SPEC, ANY_SPEC, _rows_spec(D), _vec(D), _rows_spec(D)],
        out_specs=[_rows_spec(D), _rows_spec(D), _vec(D)],
        out_shape=[jax.ShapeDtypeStruct((T, D), F32), jax.ShapeDtypeStruct((T, D), BF), jax.ShapeDtypeStruct((1, D), F32)],
        scratch_shapes=[pltpu.VMEM((RC, D), F32)] + [pltpu.VMEM((D_FF, D), BF)] * 2 + [pltpu.SemaphoreType.DMA((len(FF_TILE_ROWS),))] * 2,
        compiler_params=_params(), name="gate_up_bwd")(dgt, dup, w_gate, w_up, h1, wn, dh2)


def _adamw(w, g, m, v):
    m = ADAM_B1 * m + (1.0 - ADAM_B1) * g
    v = ADAM_B2 * v + (1.0 - ADAM_B2) * (g * g)
    m_hat = m / (1.0 - ADAM_B1 ** ADAM_STEP)
    v_hat = v / (1.0 - ADAM_B2 ** ADAM_STEP)
    delta = -ADAM_LR * (m_hat / (jnp.sqrt(v_hat) + ADAM_EPS) + ADAM_WD * w)
    return delta, m, v


def adamw_shards(name, recvs, ws, ms, vs):
    n = len(ws)

    def body(*refs):
        ins, outs = refs[:4 * n], refs[4 * n:]
        for k in range(n):
            p_ref, w_ref, m_ref, v_ref = ins[k], ins[n + k], ins[2 * n + k], ins[3 * n + k]
            g = p_ref[0].astype(F32)
            for s in range(1, 8):
                g = g + p_ref[s].astype(F32)
            outs[4 * k][...] = g
            outs[4 * k + 1][...], outs[4 * k + 2][...], outs[4 * k + 3][...] = _adamw(w_ref[...], g, m_ref[...], v_ref[...])

    tiles = [_spec((w.shape[0] // 2, w.shape[1]), lambda i: (i, 0)) for w in ws]
    recv_tiles = [_spec((8, w.shape[0] // 2, w.shape[1]), lambda i: (0, i, 0)) for w in ws]
    res = pl.pallas_call(
        body, grid=(2,), in_specs=recv_tiles + tiles * 3,
        out_specs=[t for t in tiles for _ in range(4)],
        out_shape=[jax.ShapeDtypeStruct(w.shape, F32) for w in ws for _ in range(4)],
        compiler_params=_params(), name=name)(*recvs, *ws, *ms, *vs)
    return [list(res[4 * k:4 * k + 4]) for k in range(n)]


def adamw_w_in(recv, w, m, v, after):
    rows = 34
    per_row = D // 128

    def body(p_ref, w_ref, m_ref, v_ref, _after, g_ref, d_ref, mo_ref, vo_ref):
        def chunk(c, carry):
            lines = pl.ds(pl.multiple_of(c * per_row * rows, 16), per_row * rows)
            g = p_ref[0, lines, :].astype(F32)
            for s in range(1, 8):
                g = g + p_ref[s, lines, :].astype(F32)
            g = g.reshape(rows, per_row, 128)
            part = pl.ds(c * rows, rows)
            g_ref[part] = g
            d_ref[part], mo_ref[part], vo_ref[part] = _adamw(w_ref[part], g, m_ref[part], v_ref[part])
            return carry

        lax.fori_loop(0, w.shape[0] // rows, chunk, 0)

    shape = jax.ShapeDtypeStruct(w.shape, F32)
    whole = pl.BlockSpec(memory_space=pltpu.VMEM)
    return pl.pallas_call(body, out_shape=[shape] * 4, in_specs=[whole] * 4 + [ANY_SPEC], compiler_params=_params(0),
                          name="adamw_w_in")(recv, w, m, v, after)


def sum_slabs(recv):
    def body(p_ref, o_ref):
        g = p_ref[0]
        for s in range(1, 8):
            g = g + p_ref[s]
        for s in range(8):
            o_ref[s] = g

    return pl.pallas_call(body, out_shape=jax.ShapeDtypeStruct(recv.shape, F32), compiler_params=_params(0), name="sum_slabs")(recv)


SIMPLE = [("norm1_w", 1024), ("ssd_conv_b", 1536), ("ssd_dt_bias", 16), ("ssd_a_log", 16), ("ssd_d", 16), ("ssd_norm_w", 1024),
          ("lru_conv_b", 1024), ("lru_ba", 1024), ("lru_bx", 1024), ("lru_lambda", 1024), ("lru_norm_w", 1024), ("norm2_w", 1024),
          ("final_norm_w", 1024)]
SPECIAL = ["lru_wa", "lru_wx", "meta_tokens", "ssd_conv_w", "lru_conv_w"]
SM_ROWS = 176
SM_WA, SM_WX, SM_META, SM_SCW, SM_LCW, SM_LOSS = 14, 78, 142, 158, 166, 170


def _simple_rows():
    rows, r = {}, 0
    for name, n in SIMPLE:
        rows[name] = r
        r += -(-n // 1024)
    return rows


def adamw_small(sm, special_g, ws, ms, vs):
    rows = _simple_rows()
    ns, nx = len(SIMPLE), len(SPECIAL)

    def body(*refs):
        sm_ref = refs[0]
        gx = refs[1:1 + nx]
        wr = refs[1 + nx:1 + nx + ns + nx]
        mr = refs[1 + nx + ns + nx:1 + nx + 2 * (ns + nx)]
        vr = refs[1 + nx + 2 * (ns + nx):1 + nx + 3 * (ns + nx)]
        outs = refs[1 + nx + 3 * (ns + nx):]
        o = 0
        for k, (name, n) in enumerate(SIMPLE):
            r0 = rows[name]
            for c0 in range(0, n, 1024):
                wd = min(1024, n - c0)
                g = sm_ref[r0 + c0 // 1024:r0 + c0 // 1024 + 1, 0:wd]
                sl = (slice(None), slice(c0, c0 + wd))
                d, m2, v2 = _adamw(wr[k][sl], g, mr[k][sl], vr[k][sl])
                outs[o][sl] = g
                outs[o + 1][sl] = d
                outs[o + 2][sl] = m2
                outs[o + 3][sl] = v2
            o += 4
        for k in range(nx):
            g = gx[k][...]
            d, m2, v2 = _adamw(wr[ns + k][...], g, mr[ns + k][...], vr[ns + k][...])
            outs[o][...] = g
            outs[o + 1][...] = d
            outs[o + 2][...] = m2
            outs[o + 3][...] = v2
            o += 4
        outs[o][...] = sm_ref[SM_LOSS:SM_LOSS + 1, 0:1]

    out_shape = []
    for k in range(ns + nx):
        out_shape += [jax.ShapeDtypeStruct(ws[k].shape, F32)] * 4
    out_shape.append(jax.ShapeDtypeStruct((1, 1), F32))
    return pl.pallas_call(body, out_shape=out_shape, compiler_params=_params(0), name="adamw_small")(sm, *special_g, *ws, *ms, *vs)


def _place():
    return lax.axis_index("x"), lax.axis_index("y"), lax.axis_index("c")


def _index(px, py, pc):
    return 4 * px + 2 * py + pc


def all_gather(name, shards):
    n = len(shards)
    hbm = pl.BlockSpec(memory_space=pl.ANY)

    def pieces(s):
        tile = 32 // s.dtype.itemsize
        per = s.shape[0] // tile // 4 * tile
        return [(0, s.shape[0])] if s.shape[0] < 256 else [(r * per, per if r < 3 else s.shape[0] - 3 * per) for r in range(4)]

    parts = [pieces(s) for s in shards]
    first_sem = [7 * sum(len(p) for p in parts[:i]) for i in range(n + 1)]

    def body(*refs):
        ins, outs = refs[:n], refs[n:2 * n]
        send_sems, recv_sems, local_sems = refs[2 * n:]
        x, y, c = _place()
        me, sibling = (x, y, c), (x, y, 1 - c)
        chips = [(1 - x, y), (x, 1 - y), (1 - x, 1 - y)]

        def copy(i, r, k, block, to, src=None):
            rows = pl.ds(*parts[i][r])
            dst = outs[i].at[_index(*block), rows]
            sem = first_sem[i] + 7 * r + k
            return pltpu.make_async_remote_copy(src_ref=dst if src is None else src.at[rows], dst_ref=dst, send_sem=send_sems.at[sem],
                                                recv_sem=recv_sems.at[sem], device_id=to, device_id_type=MESH)

        every = [(i, r) for i in range(n) for r in range(len(parts[i]))]
        mine = [pltpu.make_async_copy(ins[i], outs[i].at[_index(*me)], local_sems.at[i]) for i in range(n)]
        for cp in mine:
            cp.start()
        first = []
        for i, r in every:
            first += [copy(i, r, 1 + j, me, (*chip, c), src=ins[i]) for j, chip in enumerate(chips)]
            first.append(copy(i, r, 0, me, sibling, src=ins[i]))
        for cp in first:
            cp.start()
        passed = []
        for i, r in every:
            for j, chip in enumerate(chips):
                copy(i, r, 1 + j, (*chip, c), me).wait_recv()
                cp = copy(i, r, 4 + j, (*chip, c), sibling)
                cp.start()
                passed.append(cp)
        for i, r in every:
            copy(i, r, 0, sibling, me).wait_recv()
            for j, chip in enumerate(chips):
                copy(i, r, 4 + j, (*chip, 1 - c), me).wait_recv()
        for cp in first + passed:
            cp.wait_send()
        for cp in mine:
            cp.wait()

    return pl.pallas_call(
        body, in_specs=[hbm] * n, out_specs=[hbm] * n,
        out_shape=[jax.ShapeDtypeStruct((8,) + s.shape, s.dtype) for s in shards],
        scratch_shapes=[pltpu.SemaphoreType.DMA((first_sem[n],)), pltpu.SemaphoreType.DMA((first_sem[n],)), pltpu.SemaphoreType.DMA((n,))],
        name=name)(*shards)


HBM_SPEC = pl.BlockSpec(memory_space=pltpu.HBM)
SEM_SPEC = pl.BlockSpec(memory_space=pltpu.SEMAPHORE)
EFFECT = pltpu.SideEffectType.DATAFLOW_SIDE_EFFECTING


def _peers(x, y, c):
    return [((1 - x) if k & 4 else x, (1 - y) if k & 2 else y, (1 - c) if k & 1 else c) for k in range(1, 8)]


def _pieces(rows):
    for n in (4, 2):
        if rows % (16 * n) == 0:
            return [(r * (rows // n), rows // n) for r in range(n)]
    return [(0, rows)]


def _peer_copies(src, land, send_sems, recv_sems, k, peer, mine):
    block = src.at[_index(*peer)]
    return [pltpu.make_async_remote_copy(src_ref=block.at[pl.ds(r0, nr)], dst_ref=land.at[mine, pl.ds(r0, nr)], send_sem=send_sems.at[k],
                                         recv_sem=recv_sems.at[k], device_id=peer, device_id_type=MESH)
            for r0, nr in _pieces(block.shape[0])]


OWN = 7


def _own_copy(src, land, send_sems, mine):
    return pltpu.make_async_copy(src.at[mine], land.at[mine], send_sems.at[OWN])


def copies_start(name, srcs):
    n = len(srcs)

    def body(*refs):
        ins, lands = refs[:n], refs[n:2 * n]
        sends, recvs = refs[2 * n:3 * n], refs[3 * n:4 * n]
        token = refs[-1]
        x, y, c = _place()
        mine = _index(x, y, c)
        for i in range(n):
            per_peer = [_peer_copies(ins[i], lands[i], sends[i], recvs[i], k, peer, mine) for k, peer in enumerate(_peers(x, y, c))]
            for piece in zip(*per_peer):
                for cp in piece:
                    cp.start()
            _own_copy(ins[i], lands[i], sends[i], mine).start()
        token[...] = jnp.zeros_like(token)

    res = pl.pallas_call(
        body, name=name,
        out_shape=([pltpu.SemaphoreType.DMA((8,))] * n + [pltpu.SemaphoreType.DMA((7,))] * n + [pltpu.HBM(s.shape, s.dtype) for s in srcs] * 2
                   + [jax.ShapeDtypeStruct((8, 128), F32)]),
        in_specs=[HBM_SPEC] * (2 * n),
        out_specs=[SEM_SPEC] * (2 * n) + [HBM_SPEC] * (2 * n) + [pl.BlockSpec(memory_space=pltpu.VMEM)],
        input_output_aliases={i: 2 * n + i for i in range(2 * n)},
        compiler_params=pltpu.CompilerParams(has_side_effects=EFFECT),
    )(*[pltpu.with_memory_space_constraint(s, pltpu.HBM) for s in srcs],
      *[pltpu.with_memory_space_constraint(lax.empty(s.shape, s.dtype), pltpu.HBM) for s in srcs])
    return [(res[i], res[n + i], res[2 * n + i], res[3 * n + i]) for i in range(n)], res[-1]


def copies_wait(name, started, after):
    n = len(started)

    def body(*refs):
        ins, lands = refs[:n], refs[n:2 * n]
        sends, recvs = refs[2 * n:3 * n], refs[3 * n:4 * n]
        x, y, c = _place()
        mine = _index(x, y, c)
        for i in range(n):
            for k, peer in enumerate(_peers(x, y, c)):
                arrival = pltpu.make_async_remote_copy(src_ref=ins[i].at[mine], dst_ref=lands[i].at[_index(*peer)],
                                                       send_sem=sends[i].at[k], recv_sem=recvs[i].at[k], device_id=peer, device_id_type=MESH)
                arrival.wait_send()
                arrival.wait_recv()
            _own_copy(ins[i], lands[i], sends[i], mine).wait()

    srcs = [s[2] for s in started]
    lands = [s[3] for s in started]
    res = pl.pallas_call(
        body, name=name,
        out_shape=[pltpu.HBM(s.shape, s.dtype) for s in srcs] + [pltpu.HBM(z.shape, z.dtype) for z in lands],
        in_specs=[HBM_SPEC] * (2 * n) + [SEM_SPEC] * (2 * n) + [ANY_SPEC] * len(after),
        out_specs=[HBM_SPEC] * (2 * n),
        input_output_aliases={i: i for i in range(2 * n)},
        compiler_params=pltpu.CompilerParams(has_side_effects=EFFECT),
    )(*srcs, *lands, *[s[0] for s in started], *[s[1] for s in started], *after)
    return list(res[n:])


def _hop(src, land, send_sems, recv_sems, k, block, to):
    dst = land.at[_index(*block)]
    return pltpu.make_async_remote_copy(src_ref=dst if src is None else src, dst_ref=dst, send_sem=send_sems.at[k], recv_sem=recv_sems.at[k],
                                        device_id=to, device_id_type=MESH)


def _own_block(src, land, send_sems, mine):
    return pltpu.make_async_copy(src, land.at[mine], send_sems.at[4])


def _other_chips(x, y):
    return [(1 - x, y), (x, 1 - y), (1 - x, 1 - y)]


def gather_start(name, shards, through):
    n, m = len(shards), len(through)

    def body(*refs):
        ins, lands = refs[:n], refs[n:2 * n]
        sends, recvs = refs[2 * n + m:3 * n + m], refs[3 * n + m:4 * n + m]
        x, y, c = _place()
        for i in range(n):
            for j, chip in enumerate(_other_chips(x, y)):
                _hop(ins[i], lands[i], sends[i], recvs[i], 1 + j, (x, y, c), (*chip, c)).start()
            _hop(ins[i], lands[i], sends[i], recvs[i], 0, (x, y, c), (x, y, 1 - c)).start()
            _own_block(ins[i], lands[i], sends[i], _index(x, y, c)).start()

    own, passing = pltpu.SemaphoreType.DMA((5,)), pltpu.SemaphoreType.DMA((3,))
    zones = [jax.ShapeDtypeStruct((8,) + s.shape, s.dtype) for s in shards]
    res = pl.pallas_call(
        body, name=name,
        out_shape=([own] * (2 * n) + [passing] * (2 * n) + [pltpu.HBM(s.shape, s.dtype) for s in shards]
                   + [pltpu.HBM(z.shape, z.dtype) for z in zones] + [pltpu.HBM(t.shape, t.dtype) for t in through]),
        in_specs=[HBM_SPEC] * (2 * n + m),
        out_specs=[SEM_SPEC] * (4 * n) + [HBM_SPEC] * (2 * n + m),
        input_output_aliases={i: 4 * n + i for i in range(2 * n + m)},
        compiler_params=pltpu.CompilerParams(has_side_effects=EFFECT),
    )(*[pltpu.with_memory_space_constraint(s, pltpu.HBM) for s in shards],
      *[pltpu.with_memory_space_constraint(lax.empty(z.shape, z.dtype), pltpu.HBM) for z in zones],
      *[pltpu.with_memory_space_constraint(t, pltpu.HBM) for t in through])
    return [[res[4 * n + i], res[5 * n + i], res[i], res[n + i], res[2 * n + i], res[3 * n + i]] for i in range(n)], list(res[6 * n:])


def gather_stage(name, pass_on, finish, after):
    arrays = pass_on + finish
    n = len(arrays)

    def body(*refs):
        ins, lands = refs[:n], refs[n:2 * n]
        sems = [refs[(2 + q) * n:(3 + q) * n] for q in range(4)]
        x, y, c = _place()
        me, sibling = (x, y, c), (x, y, 1 - c)
        for i in range(len(pass_on)):
            send, recv, send_on, recv_on = (q[i] for q in sems)
            for j, chip in enumerate(_other_chips(x, y)):
                _hop(None, lands[i], send, recv, 1 + j, (*chip, c), me).wait_recv()
                _hop(None, lands[i], send_on, recv_on, j, (*chip, c), sibling).start()
        for i in range(len(pass_on), n):
            send, recv, send_on, recv_on = (q[i] for q in sems)
            _hop(ins[i], lands[i], send, recv, 0, sibling, me).wait_recv()
            for j, chip in enumerate(_other_chips(x, y)):
                _hop(None, lands[i], send_on, recv_on, j, (*chip, 1 - c), me).wait_recv()
            _hop(ins[i], lands[i], send, recv, 0, me, sibling).wait_send()
            _own_block(ins[i], lands[i], send, _index(*me)).wait()
            for j, chip in enumerate(_other_chips(x, y)):
                _hop(ins[i], lands[i], send, recv, 1 + j, me, (*chip, c)).wait_send()
                _hop(None, lands[i], send_on, recv_on, j, (*chip, c), sibling).wait_send()
        refs[-1][...] = jnp.zeros_like(refs[-1])

    res = pl.pallas_call(
        body, name=name,
        out_shape=([pltpu.HBM(a[0].shape, a[0].dtype) for a in arrays] + [pltpu.HBM(a[1].shape, a[1].dtype) for a in arrays]
                   + [jax.ShapeDtypeStruct((8, 128), F32)]),
        in_specs=[HBM_SPEC] * (2 * n) + [SEM_SPEC] * (4 * n) + [ANY_SPEC],
        out_specs=[HBM_SPEC] * (2 * n) + [pl.BlockSpec(memory_space=pltpu.VMEM)],
        input_output_aliases={i: i for i in range(2 * n)},
        compiler_params=pltpu.CompilerParams(has_side_effects=EFFECT),
    )(*[a[0] for a in arrays], *[a[1] for a in arrays], *[a[2 + q] for q in range(4) for a in arrays], after)
    for i, a in enumerate(arrays):
        a[0], a[1] = res[i], res[n + i]
    return [a[1] for a in finish], res[-1]


WEIGHTS = ["meta_tokens", "norm1_w", "w_in", "ssd_conv_w", "ssd_conv_b", "ssd_dt_bias", "ssd_a_log", "ssd_d", "ssd_norm_w", "lru_conv_w",
           "lru_conv_b", "lru_wa", "lru_ba", "lru_wx", "lru_bx", "lru_lambda", "lru_norm_w", "w_out", "norm2_w", "w_gate", "w_up", "w_down",
           "final_norm_w"]
BIG = ["w_in", "w_out", "w_gate", "w_up", "w_down"]
COLUMN_SHARDED = ["w_in", "w_gate", "w_up"]


def _pair_blocks(w):
    w = w.reshape(8, 2, 64, 64)
    z = jnp.zeros((8, 64, 64), w.dtype)
    return jnp.concatenate([jnp.concatenate([w[:, 0], z], axis=2), jnp.concatenate([z, w[:, 1]], axis=2)], axis=1)


def _unpair_blocks(w2):
    return jnp.stack([w2[:, :64, :64], w2[:, 64:, 64:]], axis=1).reshape(16, 64, 64)


def _per_group(vs):
    return jnp.pad(jnp.concatenate(vs, axis=0).reshape(len(vs), 2, 1, 8), ((0, 0), (0, 0), (0, 0), (0, 120)))


def _pad_cols(v, n):
    return jnp.pad(v, ((0, 0), (0, n - v.shape[1])))


def local_step(x, target, meta, ssd_cw, lru_cw, w_in_shards, fetch, send, p):
    heads = _per_group([p["ssd_dt_bias"], p["ssd_a_log"], p["ssd_d"]])
    wa2 = _pair_blocks(p["lru_wa"]).astype(BF)
    wx2 = _pair_blocks(p["lru_wx"]).astype(BF)
    lru = (lru_cw, p["lru_conv_b"], wa2, p["lru_ba"], wx2, p["lru_bx"], p["lru_lambda"])

    proj, dt_raw, u1, h0, w_in, w_dt = in_proj(x, meta, p["norm1_w"], w_in_shards)
    yn_ssd, y_pre, h_prev = ssd_fwd(proj, dt_raw, ssd_cw, p["ssd_conv_b"], heads, p["ssd_norm_w"])
    _, moved = fetch([], yn_ssd)
    hseq, a = lru_fwd(proj, *lru, moved)
    (w_out,), _ = fetch(["w_out"], hseq)
    h1, cat = out_proj(yn_ssd, proj, hseq, p["lru_norm_w"], w_out, h0)
    (w_gate, w_up), _ = fetch(["w_gate", "w_up"], h1)
    gt, up, act, u2 = gate_up(h1, p["norm2_w"], w_gate, w_up)
    (w_down,), _ = fetch(["w_down"], act)
    dh2, dh2_b, loss, d_fnw = down_loss(act, w_down, h1, target, p["final_norm_w"])

    dgt, dup, g_down, g_gate, g_up = swiglu_bwd(dh2_b, w_down, gt, up, act, u2)
    dh1, dh1_b, d_n2 = gate_up_bwd(dgt, dup, w_gate, w_up, h1, p["norm2_w"], dh2)
    sent = send({"w_down": g_down, "w_gate": g_gate, "w_up": g_up, "w_out": weight_grad("dw_out", cat, dh1_b)})
    dyn, dh_out, dg_b, d_lnw = out_proj_bwd(dh1_b, w_out, proj, hseq, p["lru_norm_w"], sent)

    dxl_b, d_lcw, d_lcb, dwa2, d_ba, dwx2, d_bx, d_lam = lru_bwd(dh_out, a, hseq, proj, *lru)
    dz_b, dxbc_b, ddt_b, dpar, d_snw, d_scw, d_scb = ssd_bwd(dyn, proj, dt_raw, ssd_cw, p["ssd_conv_b"], y_pre, h_prev, heads,
                                                             p["ssd_norm_w"], sent)
    sent = send({"w_in": in_weight_grad([dz_b, dxbc_b, dg_b, dxl_b], [0, SSD_W, 2576, 2576 + LRU_W], ddt_b, u1)})
    grad_x, d_meta, d_n1, dwa2, dwx2 = in_proj_bwd(dz_b, dg_b, dxl_b, dxbc_b, ddt_b, w_in, w_dt, h0, p["norm1_w"], dh1, sent, [dwa2, dwx2])
    small = {"norm1_w": d_n1, "ssd_conv_b": d_scb, "ssd_dt_bias": dpar[:, 0, :8].reshape(1, 16), "ssd_a_log": dpar[:, 1, :8].reshape(1, 16),
             "ssd_d": dpar[:, 2, :8].reshape(1, 16), "ssd_norm_w": d_snw, "lru_conv_b": d_lcb, "lru_ba": d_ba, "lru_bx": d_bx,
             "lru_lambda": d_lam, "lru_norm_w": d_lnw, "norm2_w": d_n2, "final_norm_w": d_fnw,
             "lru_wa": _unpair_blocks(dwa2), "lru_wx": _unpair_blocks(dwx2), "meta_tokens": d_meta,
             "ssd_conv_w": d_scw, "lru_conv_w": d_lcw}
    return loss, grad_x, small


def _pack_small(small, loss):
    rows = [_pad_cols(small[name], -(-n // 1024) * 1024).reshape(-1, 1024) for name, n in SIMPLE]
    rows += [small["lru_wa"].reshape(64, 1024), small["lru_wx"].reshape(64, 1024), small["meta_tokens"],
             _pad_cols(small["ssd_conv_w"], 2048).reshape(8, 1024), small["lru_conv_w"], _pad_cols(loss[:, 0:1], 1024)]
    sm = jnp.concatenate(rows, axis=0)
    return jnp.pad(sm, ((0, SM_ROWS - sm.shape[0]), (0, 0)))


def _slabs(g):
    return g.reshape(8, g.shape[0] // 8, g.shape[1])


def _unslab(g):
    return g.reshape(8 * g.shape[1], g.shape[2])


def kernel(x, meta_tokens, norm1_w, w_in, ssd_conv_w, ssd_conv_b, ssd_dt_bias, ssd_a_log, ssd_d, ssd_norm_w, lru_conv_w, lru_conv_b, lru_wa, lru_ba, lru_wx, lru_bx, lru_lambda, lru_norm_w, w_out, norm2_w, w_gate, w_up, w_down, final_norm_w, loss_target, m_meta_tokens, m_norm1_w, m_w_in, m_ssd_conv_w, m_ssd_conv_b, m_ssd_dt_bias, m_ssd_a_log, m_ssd_d, m_ssd_norm_w, m_lru_conv_w, m_lru_conv_b, m_lru_wa, m_lru_ba, m_lru_wx, m_lru_bx, m_lru_lambda, m_lru_norm_w, m_w_out, m_norm2_w, m_w_gate, m_w_up, m_w_down, m_final_norm_w, v_meta_tokens, v_norm1_w, v_w_in, v_ssd_conv_w, v_ssd_conv_b, v_ssd_dt_bias, v_ssd_a_log, v_ssd_d, v_ssd_norm_w, v_lru_conv_w, v_lru_conv_b, v_lru_wa, v_lru_ba, v_lru_wx, v_lru_bx, v_lru_lambda, v_lru_norm_w, v_w_out, v_norm2_w, v_w_gate, v_w_up, v_w_down, v_final_norm_w):
    w = dict(meta_tokens=meta_tokens, norm1_w=norm1_w, w_in=w_in[0], ssd_conv_w=ssd_conv_w[0], ssd_conv_b=ssd_conv_b, ssd_dt_bias=ssd_dt_bias,
             ssd_a_log=ssd_a_log, ssd_d=ssd_d, ssd_norm_w=ssd_norm_w, lru_conv_w=lru_conv_w[0], lru_conv_b=lru_conv_b, lru_wa=lru_wa[0],
             lru_ba=lru_ba, lru_wx=lru_wx[0], lru_bx=lru_bx, lru_lambda=lru_lambda, lru_norm_w=lru_norm_w, w_out=w_out[0], norm2_w=norm2_w,
             w_gate=w_gate[0], w_up=w_up[0], w_down=w_down[0], final_norm_w=final_norm_w.reshape(1, D))
    m = dict(meta_tokens=m_meta_tokens, norm1_w=m_norm1_w, w_in=m_w_in[0], ssd_conv_w=m_ssd_conv_w[0], ssd_conv_b=m_ssd_conv_b,
             ssd_dt_bias=m_ssd_dt_bias, ssd_a_log=m_ssd_a_log, ssd_d=m_ssd_d, ssd_norm_w=m_ssd_norm_w, lru_conv_w=m_lru_conv_w[0],
             lru_conv_b=m_lru_conv_b, lru_wa=m_lru_wa[0], lru_ba=m_lru_ba, lru_wx=m_lru_wx[0], lru_bx=m_lru_bx, lru_lambda=m_lru_lambda,
             lru_norm_w=m_lru_norm_w, w_out=m_w_out[0], norm2_w=m_norm2_w, w_gate=m_w_gate[0], w_up=m_w_up[0], w_down=m_w_down[0],
             final_norm_w=m_final_norm_w.reshape(1, D))
    v = dict(meta_tokens=v_meta_tokens, norm1_w=v_norm1_w, w_in=v_w_in[0], ssd_conv_w=v_ssd_conv_w[0], ssd_conv_b=v_ssd_conv_b,
             ssd_dt_bias=v_ssd_dt_bias, ssd_a_log=v_ssd_a_log, ssd_d=v_ssd_d, ssd_norm_w=v_ssd_norm_w, lru_conv_w=v_lru_conv_w[0],
             lru_conv_b=v_lru_conv_b, lru_wa=v_lru_wa[0], lru_ba=v_lru_ba, lru_wx=v_lru_wx[0], lru_bx=v_lru_bx, lru_lambda=v_lru_lambda,
             lru_norm_w=v_lru_norm_w, w_out=v_w_out[0], norm2_w=v_norm2_w, w_gate=v_w_gate[0], w_up=v_w_up[0], w_down=v_w_down[0],
             final_norm_w=v_final_norm_w.reshape(1, D))
    shapes = dict(meta_tokens=meta_tokens.shape, norm1_w=norm1_w.shape, w_in=w_in.shape, ssd_conv_w=ssd_conv_w.shape,
                  ssd_conv_b=ssd_conv_b.shape, ssd_dt_bias=ssd_dt_bias.shape, ssd_a_log=ssd_a_log.shape, ssd_d=ssd_d.shape,
                  ssd_norm_w=ssd_norm_w.shape, lru_conv_w=lru_conv_w.shape, lru_conv_b=lru_conv_b.shape, lru_wa=lru_wa.shape,
                  lru_ba=lru_ba.shape, lru_wx=lru_wx.shape, lru_bx=lru_bx.shape, lru_lambda=lru_lambda.shape, lru_norm_w=lru_norm_w.shape,
                  w_out=w_out.shape, norm2_w=norm2_w.shape, w_gate=w_gate.shape, w_up=w_up.shape, w_down=w_down.shape,
                  final_norm_w=final_norm_w.shape)
    me = _index(*_place())
    for n in COLUMN_SHARDED:
        w[n], m[n], v[n] = w[n].T, m[n].T, v[n].T

    small_shard = jnp.concatenate([w["meta_tokens"], _pad_cols(w["ssd_conv_w"], 256).reshape(8, 128), w["lru_conv_w"],
                                   jnp.zeros((4, 128), F32)], axis=0)
    g_in, gs = all_gather("gather_w_in", [w["w_in"].astype(BF), small_shard])
    later = ["w_out", "w_gate", "w_up", "w_down"]
    started, (g_in, gs) = gather_start("gather_rest_start", [w[n].astype(BF) for n in later], [g_in, gs])
    started = dict(zip(later, started))
    ssd_cw = gs[:, 16:24].reshape(8, 4, 256)[:, :, :192].transpose(1, 0, 2).reshape(4, XBC)
    lru_cw = gs[:, 24:28].transpose(1, 0, 2).reshape(4, LRU_W)

    def fetch(names, after):
        pass_on = {"w_out": ["w_down"], "w_gate": [], "w_down": []}[names[0]] if names else ["w_out", "w_gate", "w_up"]
        got, zero = gather_stage("gather_" + (names[0] + "_wait" if names else "pass_on"), [started[n] for n in pass_on],
                                 [started[n] for n in names], after)
        return [_unslab(g) for g in got], zero

    in_flight = {}

    def send(grads):
        names = list(grads)
        st, zero = copies_start("grads_" + names[0] + "_start", [grads[n] if n == "small" else _slabs(grads[n]) for n in names])
        in_flight.update(zip(names, st))
        return zero

    loss, grad_x, small = local_step(x[0], loss_target[0], gs, ssd_cw, lru_cw, g_in, fetch, send, w)
    send({"small": _pack_small(small, loss).reshape(8, SM_ROWS // 8, 1024)})

    out = {}
    early = ["w_down", "w_gate", "w_up", "w_out"]
    recv = dict(zip(early, copies_wait("grads_early_wait", [in_flight[n] for n in early], [in_flight["small"][2]])))
    for pair in (early[:2], early[2:]):
        done = adamw_shards("adamw_" + pair[0], [recv[n] for n in pair], [w[n] for n in pair], [m[n] for n in pair], [v[n] for n in pair])
        out.update(zip(pair, done))
    recv_in, recv_small = copies_wait("grads_late_wait", [in_flight["w_in"], in_flight["small"]], [out[n][0] for n in early])
    def lines(a):
        return jnp.transpose(a.reshape(D // 128, 128, IN_COLS // 8), (2, 0, 1))

    gathering, zero = copies_start("gather_small_start", [sum_slabs(recv_small)])
    updated = adamw_w_in(recv_in, lines(w_in), lines(m_w_in), lines(v_w_in), zero)
    out["w_in"] = [jnp.transpose(o, (1, 2, 0)).reshape(D, IN_COLS // 8) for o in updated]
    for n in ("w_gate", "w_up"):
        out[n] = [o.T for o in out[n]]
    sm = copies_wait("gather_small_wait", gathering, [updated[0]])[0].reshape(SM_ROWS, 1024)
    special_g =[sm[SM_WA:SM_WA + 64].reshape(16, 64, 64), sm[SM_WX:SM_WX + 64].reshape(16, 64, 64),
                 lax.dynamic_slice(sm[SM_META:SM_META + 16], (0, 128 * me), (16, 128)),
                 lax.dynamic_slice(sm[SM_SCW:SM_SCW + 8].reshape(4, 2048), (0, 192 * me), (4, 192)),
                 lax.dynamic_slice(sm[SM_LCW:SM_LCW + 4], (0, 128 * me), (4, 128))]
    names = [n for n, _ in SIMPLE] + SPECIAL
    res = adamw_small(sm, special_g, [w[n] for n in names], [m[n] for n in names], [v[n] for n in names])
    for k, n in enumerate(names):
        out[n] = res[4 * k:4 * k + 4]
    flat = [res[-1].reshape(()), grad_x[None]]
    for k in range(4):
        flat += [out[n][k].reshape(shapes[n]) for n in WEIGHTS]
    return tuple(flat)
```

```python
import math

import jax
import jax.numpy as jnp
from jax import lax
from jax.experimental import pallas as pl
from jax.experimental.pallas import tpu as pltpu

F32 = jnp.float32
BF = jnp.bfloat16

D = 1024
SEQ = 2048
N_META = 16
Q = 128
NPAD = 112
T = NPAD + N_META + SEQ
NCH = T // Q
RC = 544
D_FF = 2816
SSD_W = 1024
LRU_W = 1024
XBC = 1536
IN_COLS = 4624
PZ, PG, PXL, PXBC = 0, 1024, 2048, 3072
NP_IN = 4608
EPS = 1e-6
LRU_C = 8.0
VMEM_LIMIT = 56 * 1024 * 1024

ADAM_LR, ADAM_B1, ADAM_B2, ADAM_EPS, ADAM_WD, ADAM_STEP = 0.001, 0.9, 0.999, 1e-08, 0.01, 10

NT_DIMS = (((1,), (1,)), ((), ()))
TN_DIMS = (((0,), (0,)), ((), ()))
MESH = pl.DeviceIdType.MESH


def _params(n_grid=1, limit=VMEM_LIMIT):
    return pltpu.CompilerParams(dimension_semantics=("arbitrary",) * n_grid, vmem_limit_bytes=limit)


def _spec(shape, imap, single=False):
    if single:
        return pl.BlockSpec(shape, imap, pipeline_mode=pl.Buffered(1))
    return pl.BlockSpec(shape, imap)


def _sigmoid(x):
    return 0.5 * jnp.tanh(0.5 * x) + 0.5


def _sigmoid_gate(x):
    return 1.0 / (1.0 + jnp.exp(-x))


def _softplus(x):
    return jnp.maximum(x, 0.0) + jnp.log(1.0 + jnp.exp(-jnp.abs(x)))


def _rms_stats(h):
    return lax.rsqrt(jnp.mean(h * h, axis=-1, keepdims=True) + EPS)


def _rms(h, w):
    return (h * _rms_stats(h)) * w


def _rms_bwd(du, h, w):
    r = _rms_stats(h)
    n = h * r
    dn = du * w
    dh = r * (dn - n * jnp.mean(dn * n, axis=-1, keepdims=True))
    return dh, du * n


_G0 = math.sqrt(2.0 / math.pi)


def _gelu(x):
    return 0.5 * x * (1.0 + jnp.tanh(_G0 * (x + 0.044715 * (x * x * x))))


def _gelu_grad(x):
    t = jnp.tanh(_G0 * (x + 0.044715 * (x * x * x)))
    return 0.5 * (1.0 + t) + 0.5 * x * (1.0 - t * t) * (_G0 * (1.0 + 3.0 * 0.044715 * (x * x)))


def _rows(shape, r0=0):
    return lax.broadcasted_iota(jnp.int32, shape, 0) + r0


def _lanes(shape):
    return lax.broadcasted_iota(jnp.int32, shape, 1)


HALO = 8


def _fill_padded(pad_ref, x_ref):
    pad_ref[0:HALO, :] = jnp.zeros((HALO, pad_ref.shape[1]), F32)
    pad_ref[T + HALO:T + 2 * HALO, :] = jnp.zeros((HALO, pad_ref.shape[1]), F32)

    def step(c, carry):
        r0 = pl.multiple_of(c * Q, Q)
        pad_ref[pl.ds(r0 + HALO, Q), :] = x_ref[pl.ds(r0, Q), :].astype(F32)
        return carry

    lax.fori_loop(0, NCH, step, 0)


def _back(pad_ref, r0):
    win = pad_ref[pl.ds(r0, Q + HALO), :]
    return lambda s: win[HALO:, :] if s == 0 else pltpu.roll(win, s, axis=0)[HALO:, :]


def _ahead(pad_ref, r0):
    win = pad_ref[pl.ds(r0 + HALO, Q + HALO), :]
    return lambda s: win[:Q, :] if s == 0 else pltpu.roll(win, Q + HALO - s, axis=0)[:Q, :]


def _conv(back, w, b):
    y = b + w[3:4, :] * back(0)
    for k in range(3):
        y = y + w[k:k + 1, :] * back(3 - k)
    return y


def _conv_bwd_x(ahead, w):
    dx = w[3:4, :] * ahead(0)
    for k in range(3):
        dx = dx + w[k:k + 1, :] * ahead(3 - k)
    return dx


def _conv_bwd_w(dy, back):
    dws = [jnp.sum(dy * back(3 - k), axis=0, keepdims=True) for k in range(4)]
    return jnp.concatenate(dws, axis=0), jnp.sum(dy, axis=0, keepdims=True)


def _chunks(fn, unrolled=False):
    if unrolled:
        for c in range(NCH):
            fn(c * Q)
        return

    def step(c, carry):
        fn(pl.multiple_of(c * Q, Q))
        return carry

    lax.fori_loop(0, NCH, step, 0)


HALF = RC // 2


def _col_tiles(n, tn, fn):
    def step(j, carry):
        fn(pl.multiple_of(j * tn, tn))
        return carry

    lax.fori_loop(0, n // tn, step, 0)


def _rows_spec(cols, block_col=0):
    return _spec((RC, cols), lambda i: (i, block_col))


def _whole(shape):
    return _spec(shape, lambda i: tuple(0 for _ in shape), single=True)


def _vec(cols):
    return _spec((1, cols), lambda i: (0, 0))


def _zero_at_first(*refs):
    @pl.when(pl.program_id(0) == 0)
    def _():
        for r in refs:
            r[...] = jnp.zeros_like(r)


ANY_SPEC = pl.BlockSpec(memory_space=pl.ANY)


def _arriving(src, dst, sems, starts, rows):
    n, ahead = len(starts), 2
    first = pl.program_id(0) == 0

    def piece(k):
        r0 = starts[0]
        for j in range(1, n):
            r0 = jnp.where(k == j, starts[j], r0)
        at = pl.ds(pl.multiple_of(r0, 16), rows)
        return pltpu.make_async_copy(src.at[at], dst.at[at], sems.at[k])

    @pl.when(first)
    def _():
        for k in range(min(ahead, n)):
            piece(k).start()

    def ready(k):
        k = jnp.asarray(k, jnp.int32)

        @pl.when(first)
        def _():
            piece(k).wait()

            @pl.when(k + ahead < n)
            def _():
                piece(k + ahead).start()

    return ready


IN_RUNS = ((PZ, 0, 1024), (PXBC, 1024, XBC), (PG, 2576, 2048))
IN_TILE = 512
IN_TILE_ROWS = [wrow + IN_TILE * j for _, wrow, width in IN_RUNS for j in range(width // IN_TILE)]


def _in_tiles(fn, before_run=lambda run: None):
    done = 0
    for run, (pcol, wrow, width) in enumerate(IN_RUNS):
        before_run(run)
        def step(j, carry, pcol=pcol, wrow=wrow, done=done):
            fn(pl.multiple_of(pcol + j * IN_TILE, IN_TILE), pl.multiple_of(wrow + j * IN_TILE, 16), done + j)
            return carry

        lax.fori_loop(0, width // IN_TILE, step, 0)
        done += width // IN_TILE


def in_proj(x, meta, wn, w_shards):
    first = NPAD + N_META
    steps = T // RC
    shard = IN_COLS // 8

    def body(x_hbm, meta_ref, wn_ref, g_hbm, o_ref, dt_ref, u_ref, h_ref, wt_hbm, wdt_ref, raw, w_ref, h_scr, g_sems, h_sems, out_sem):
        i = pl.program_id(0)
        slot = i % 2
        shards = [pltpu.make_async_copy(g_hbm.at[j], raw.at[j], g_sems.at[j]) for j in range(8)]
        head = pltpu.make_async_copy(x_hbm.at[pl.ds(0, RC - first)], h_scr.at[0, pl.ds(first, RC - first)], h_sems.at[0])
        put_back = pltpu.make_async_copy(w_ref, wt_hbm, out_sem)

        def rows_of(step):
            return pltpu.make_async_copy(x_hbm.at[pl.ds(pl.multiple_of(step * RC - first, 32), RC)], h_scr.at[step % 2], h_sems.at[step % 2])

        @pl.when(i == 0)
        def _():
            for cp in shards:
                cp.start()
            head.start()
            h_scr[0, 0:NPAD, :] = jnp.zeros((NPAD, D), F32)
            for j in range(8):
                h_scr[0, NPAD:first, 128 * j:128 * j + 128] = meta_ref[j, 0:N_META, :]

        @pl.when(i + 1 < steps)
        def _():
            rows_of(i + 1).start()

        @pl.when(i == 0)
        def _():
            head.wait()

        @pl.when(i > 0)
        def _():
            rows_of(i).wait()

        h_ref[...] = h_scr[slot]
        for r in (0, HALF):
            u_ref[r:r + HALF, :] = _rms(h_scr[slot, r:r + HALF, :], wn_ref[...]).astype(BF)

        def place_shards(run):
            @pl.when(i == 0)
            def _():
                for j in ((0, 1), (2, 3, 4), (5, 6, 7))[run]:
                    shards[j].wait()
                    w_ref[shard * j:shard * (j + 1), :] = raw[j]
                if run == 1:
                    wdt_ref[...] = jnp.zeros_like(wdt_ref)
                    for g in range(2):
                        wdt_ref[128 * g:128 * g + 8, :] = w_ref[2560 + 8 * g:2568 + 8 * g, :]
                if run == 2:
                    put_back.start()

        def tile(pcol, wrow, k):
            o_ref[:, pl.ds(pcol, IN_TILE)] = lax.dot_general(u_ref[...], w_ref[pl.ds(wrow, IN_TILE), :], NT_DIMS,
                                                             preferred_element_type=F32).astype(BF)

        _in_tiles(tile, place_shards)
        dt_ref[...] = lax.dot_general(u_ref[...], wdt_ref[...], NT_DIMS, preferred_element_type=F32)

        @pl.when(i == steps - 1)
        def _():
            put_back.wait()

    return pl.pallas_call(
        body, grid=(steps,), in_specs=[ANY_SPEC, _spec(meta.shape, lambda i: (0, 0, 0)), _vec(D), ANY_SPEC],
        out_specs=[_rows_spec(NP_IN), _rows_spec(256), _rows_spec(D), _rows_spec(D), ANY_SPEC, _spec((256, D), lambda i: (0, 0))],
        out_shape=[jax.ShapeDtypeStruct((T, NP_IN), BF), jax.ShapeDtypeStruct((T, 256), F32), jax.ShapeDtypeStruct((T, D), BF),
                   jax.ShapeDtypeStruct((T, D), F32), jax.ShapeDtypeStruct((IN_COLS, D), BF), jax.ShapeDtypeStruct((256, D), BF)],
        scratch_shapes=[pltpu.VMEM((8, shard, D), BF), pltpu.VMEM((IN_COLS, D), BF), pltpu.VMEM((2, RC, D), F32),
                        pltpu.SemaphoreType.DMA((8,)), pltpu.SemaphoreType.DMA((2,)), pltpu.SemaphoreType.DMA],
        compiler_params=_params(), name="in_proj")(x, meta, wn, w_shards)


def out_proj(yn_ssd, proj, hseq, lru_nw, w_out, h0):
    def body(y_ref, g_ref, h_ref, wn_ref, w_ref, r_ref, o_ref, cat_ref):
        cat_ref[:, 0:SSD_W] = y_ref[...]
        for r in (0, HALF):
            y = _gelu(g_ref[r:r + HALF, :].astype(F32)) * h_ref[r:r + HALF, :]
            cat_ref[r:r + HALF, SSD_W:] = _rms(y, wn_ref[...]).astype(BF)

        def tile(c0):
            o_ref[:, pl.ds(c0, 512)] = r_ref[:, pl.ds(c0, 512)] + jnp.dot(cat_ref[...], w_ref[:, pl.ds(c0, 512)], preferred_element_type=F32)

        _col_tiles(D, 512, tile)

    return pl.pallas_call(
        body, grid=(T // RC,),
        in_specs=[_rows_spec(SSD_W), _rows_spec(LRU_W, PG // LRU_W), _rows_spec(LRU_W), _vec(LRU_W), _whole((SSD_W + LRU_W, D)), _rows_spec(D)],
        out_specs=[_rows_spec(D), _rows_spec(SSD_W + LRU_W)],
        out_shape=[jax.ShapeDtypeStruct((T, D), F32), jax.ShapeDtypeStruct((T, SSD_W + LRU_W), BF)],
        compiler_params=_params(), name="out_proj")(yn_ssd, proj, hseq, lru_nw, w_out, h0)


def out_proj_bwd(dh1_b, w_out, proj, hseq, lru_nw, after):
    def body(d_ref, w_ref, g_ref, h_ref, wn_ref, _after, dy_ref, dh_ref, dg_ref, dw_ref, dl_scr):
        _zero_at_first(dw_ref)

        def tile(c0):
            dy_ref[:, pl.ds(c0, 512)] = lax.dot_general(d_ref[...], w_ref[pl.ds(c0, 512), :], NT_DIMS, preferred_element_type=F32)
            dl_scr[:, pl.ds(c0, 512)] = lax.dot_general(d_ref[...], w_ref[pl.ds(SSD_W + c0, 512), :], NT_DIMS, preferred_element_type=F32)

        _col_tiles(SSD_W, 512, tile)

        for r in (0, HALF):
            g = g_ref[r:r + HALF, :].astype(F32)
            h = h_ref[r:r + HALF, :]
            ge = _gelu(g)
            dy, dw = _rms_bwd(dl_scr[r:r + HALF, :], ge * h, wn_ref[...])
            dw_ref[...] += jnp.sum(dw, axis=0, keepdims=True)
            dh_ref[r:r + HALF, :] = dy * ge
            dg_ref[r:r + HALF, :] = (dy * h * _gelu_grad(g)).astype(BF)

    return pl.pallas_call(
        body, grid=(T // RC,),
        in_specs=[_rows_spec(D), _whole((SSD_W + LRU_W, D)), _rows_spec(LRU_W, PG // LRU_W), _rows_spec(LRU_W), _vec(LRU_W), ANY_SPEC],
        out_specs=[_rows_spec(SSD_W), _rows_spec(LRU_W), _rows_spec(LRU_W), _vec(LRU_W)],
        out_shape=[jax.ShapeDtypeStruct((T, SSD_W), F32), jax.ShapeDtypeStruct((T, LRU_W), F32), jax.ShapeDtypeStruct((T, LRU_W), BF),
                   jax.ShapeDtypeStruct((1, LRU_W), F32)],
        scratch_shapes=[pltpu.VMEM((RC, LRU_W), F32)],
        compiler_params=_params(), name="out_proj_bwd")(dh1_b, w_out, proj, hseq, lru_nw, after)


def in_proj_bwd(dz, dg, dxl, dxbc, ddt, w_t, w_dt, h0, wn, dh1, after, through):
    first = NPAD + N_META

    def body(dz_ref, dg_ref, dxl_ref, dxbc_ref, ddt_ref, w_hbm, wdt_ref, h_ref, wn_ref, r_ref, _after, _in0, _in1, gx_hbm, meta_ref, dw_ref,
             _out0, _out1, du_scr, o_ref, sem, w_ref, w_sems):
        i = pl.program_id(0)
        ready = _arriving(w_hbm, w_ref, w_sems, IN_TILE_ROWS, IN_TILE)
        _zero_at_first(dw_ref)
        du_scr[...] = jnp.dot(ddt_ref[...], wdt_ref[...], preferred_element_type=F32)
        done = 0
        for d_ref, wrow, width in ((dz_ref, 0, 1024), (dxbc_ref, 1024, XBC), (dg_ref, 2576, 1024), (dxl_ref, 3600, 1024)):
            def step(j, carry, d_ref=d_ref, wrow=wrow, done=done):
                c0 = pl.multiple_of(j * IN_TILE, IN_TILE)
                ready(done + j)
                du_scr[...] += jnp.dot(d_ref[:, pl.ds(c0, IN_TILE)], w_ref[pl.ds(pl.multiple_of(wrow + c0, 16), IN_TILE), :],
                                       preferred_element_type=F32)
                return carry

            lax.fori_loop(0, width // IN_TILE, step, 0)
            done += width // IN_TILE
        for r in (0, HALF):
            dh, dw = _rms_bwd(du_scr[r:r + HALF, :], h_ref[r:r + HALF, :], wn_ref[...])
            dw_ref[...] += jnp.sum(dw, axis=0, keepdims=True)
            o_ref[r:r + HALF, :] = dh + r_ref[r:r + HALF, :]

        @pl.when(i == 0)
        def _():
            meta_ref[...] = o_ref[NPAD:first, :]
            head = pltpu.make_async_copy(o_ref.at[pl.ds(first, RC - first)], gx_hbm.at[pl.ds(0, RC - first)], sem)
            head.start()
            head.wait()

        @pl.when(i > 0)
        def _():
            rest = pltpu.make_async_copy(o_ref, gx_hbm.at[pl.ds(pl.multiple_of(i * RC - first, 32), RC)], sem)
            rest.start()
            rest.wait()

    return pl.pallas_call(
        body, grid=(T // RC,),
        in_specs=[_rows_spec(SSD_W), _rows_spec(LRU_W), _rows_spec(LRU_W), _rows_spec(XBC), _rows_spec(256), ANY_SPEC,
                  _whole((256, D)), _rows_spec(D), _vec(D), _rows_spec(D), ANY_SPEC, ANY_SPEC, ANY_SPEC],
        out_specs=[ANY_SPEC, _spec((N_META, D), lambda i: (0, 0)), _vec(D), ANY_SPEC, ANY_SPEC],
        out_shape=[jax.ShapeDtypeStruct((SEQ, D), F32), jax.ShapeDtypeStruct((N_META, D), F32), jax.ShapeDtypeStruct((1, D), F32)]
        + [jax.ShapeDtypeStruct(t.shape, t.dtype) for t in through],
        scratch_shapes=[pltpu.VMEM((RC, D), F32), pltpu.VMEM((RC, D), F32), pltpu.SemaphoreType.DMA,
                        pltpu.VMEM((IN_COLS, D), BF), pltpu.SemaphoreType.DMA((len(IN_TILE_ROWS),))],
        input_output_aliases={11: 3, 12: 4},
        compiler_params=_params(), name="in_proj_bwd")(dz, dg, dxl, dxbc, ddt, w_t, w_dt, h0, wn, dh1, after, *through)


GRAD_TILE = 256


def weight_grad(name, a, u1):
    tm = GRAD_TILE

    def body(a_ref, u_ref, o_ref):
        o_ref[...] = lax.dot_general(a_ref[...], u_ref[...], TN_DIMS, preferred_element_type=F32).astype(BF)

    return pl.pallas_call(
        body, grid=(a.shape[1] // tm,),
        in_specs=[_spec((T, tm), lambda j: (0, j)), _spec((T, D), lambda j: (0, 0), single=True)],
        out_specs=_spec((tm, D), lambda j: (j, 0)),
        out_shape=jax.ShapeDtypeStruct((a.shape[1], D), BF),
        compiler_params=_params(), name=name)(a, u1)


def in_weight_grad(parts, first_rows, ddt, u1):
    tm = GRAD_TILE
    per_row = D // 128
    dt_row, dt_lines = 2560, 8 * per_row
    parts = list(parts) + [ddt]
    first_rows = list(first_rows) + [dt_row]
    tiles = [p.shape[1] // tm for p in parts]
    starts = [sum(tiles[:k]) for k in range(len(parts))]
    last = sum(tiles) - 1

    def body(*refs):
        a_refs, u_ref = refs[:len(parts)], refs[len(parts)]
        o_hbm, mix_scr, stage, sems = refs[len(parts) + 1:]
        step = pl.program_id(0)
        slot = step % 2
        line0 = 0
        for a_ref, start, n, first in zip(a_refs, starts, tiles, first_rows):
            here = (step >= start) & (step < start + n)
            line0 = jnp.where(here, per_row * (first + tm * (step - start)), line0)

            @pl.when(here)
            def _(a_ref=a_ref):
                res = lax.dot_general(a_ref[...], u_ref[...], TN_DIMS, preferred_element_type=F32)
                for q in range(per_row):
                    mix_scr[pl.ds(q, tm, stride=per_row), :] = res[:, 128 * q:128 * q + 128]

        def tile_copy(of_slot, to):
            return pltpu.make_async_copy(stage.at[of_slot], o_hbm.at[pl.ds(to, per_row * tm)], sems.at[of_slot])

        @pl.when(step >= 2)
        def _():
            tile_copy(slot, 0).wait()

        stage[slot] = mix_scr[...].astype(BF)

        @pl.when(step < last)
        def _():
            tile_copy(slot, pl.multiple_of(line0, 128)).start()

        @pl.when(step == last)
        def _():
            halves = [pltpu.make_async_copy(stage.at[slot, pl.ds(128 * per_row * k, dt_lines)],
                                            o_hbm.at[pl.ds(per_row * (dt_row + 8 * k), dt_lines)], sems.at[2 + k]) for k in range(2)]
            for cp in halves:
                cp.start()
            tile_copy(1 - slot, 0).wait()
            for cp in halves:
                cp.wait()

    def tile_of(start, n):
        return lambda j: (0, jnp.clip(j - start, 0, n - 1))

    return pl.pallas_call(
        body, grid=(last + 1,),
        in_specs=[_spec((T, tm), tile_of(s, n)) for s, n in zip(starts, tiles)] + [_spec((T, D), lambda j: (0, 0), single=True)],
        out_specs=ANY_SPEC,
        out_shape=jax.ShapeDtypeStruct((per_row * IN_COLS, 128), BF),
        scratch_shapes=[pltpu.VMEM((per_row * tm, 128), F32), pltpu.VMEM((2, per_row * tm, 128), BF), pltpu.SemaphoreType.DMA((4,))],
        compiler_params=_params(), name="dw_in")(*parts, u1)


def _ssd_chunk_common(row0, dt_ref, b_ref, c_ref, bias, a_neg):
    shape = (Q, Q)
    lane = _lanes(shape)
    sub = _rows(shape)
    live = (_rows(shape, row0) >= NPAD) & (lane < 8)
    dtr = dt_ref[:, :]
    dt = jnp.where(live, _softplus(dtr + bias), 0.0)
    d_a = dt * a_neg
    tri = (sub >= lane).astype(F32)
    cs = jnp.dot(tri, d_a, precision=lax.Precision.HIGHEST, preferred_element_type=F32)
    cs_t = cs.T
    b_f = b_ref[:, :]
    bc = b_f.astype(BF)
    cc = c_ref[:, :].astype(BF)
    cb = lax.dot_general(cc, bc, NT_DIMS, preferred_element_type=F32)
    cs_last = cs[Q - 1:Q, :]
    return dict(lane=lane, sub=sub, live=live, dtr=dtr, dt=dt, cs=cs, cs_t=cs_t, bc=bc, cc=cc, cb=cb, bc_t=b_f.T.astype(BF),
                ecs=jnp.exp(cs), dsm=jnp.exp(cs_last - cs), gam=jnp.exp(cs_last))


def _pair(lane_even, mat, j):
    return jnp.where(lane_even, mat[:, j:j + 1], mat[:, j + 1:j + 2])


def _pair_row(lane_even, mat, j):
    return jnp.where(lane_even[0:1, :], mat[:, j:j + 1], mat[:, j + 1:j + 2])


def _head_decay(cm, j):
    seg = cm["cs"][:, j:j + 1] - cm["cs_t"][j:j + 1, :]
    return jnp.exp(jnp.where(cm["sub"] >= cm["lane"], seg, -jnp.inf))


def _head_decay_t(cm, j):
    seg = cm["cs_t"][j:j + 1, :] - cm["cs"][:, j:j + 1]
    return jnp.exp(jnp.where(cm["lane"] >= cm["sub"], seg, -jnp.inf))


def _conv_window(raw_ref, halo_ref, pad_scr):
    pad_scr[0:HALO, :] = halo_ref[...].astype(F32)[halo_ref.shape[0] - HALO:, :]
    pad_scr[HALO:HALO + Q, :] = raw_ref[...].astype(F32)
    win = pad_scr[...]
    return lambda s: win[HALO:, :] if s == 0 else pltpu.roll(win, s, axis=0)[HALO:, :]


def _xbc_cols(g):
    return slice(512 * g, 512 * g + 512), slice(SSD_W + 128 * g, SSD_W + 128 * g + 128), slice(SSD_W + 256 + 128 * g, SSD_W + 384 + 128 * g)


def ssd_fwd(proj, dt_raw, conv_w, conv_b, heads, norm_w):
    def body(raw_ref, halo_ref, dt_all, z_all, cw_ref, cb_ref, bias_all, alog_all, d_all, nw_all, yn_all, y_all, hp_all,
             h_all, pad_scr, act_scr):
        @pl.when(pl.program_id(0) == 0)
        def _():
            h_all[...] = jnp.zeros_like(h_all)

        pre = _conv(_conv_window(raw_ref, halo_ref, pad_scr), cw_ref[...], cb_ref[...])
        act_scr[...] = pre * _sigmoid(pre)
        for g in range(2):
            wide, thin = slice(512 * g, 512 * g + 512), slice(128 * g, 128 * g + 128)
            xs, bs, cs = _xbc_cols(g)
            group(act_scr.at[:, xs], act_scr.at[:, bs], act_scr.at[:, cs], dt_all.at[:, thin], z_all.at[:, wide], bias_all.at[g],
                  alog_all.at[g], d_all.at[g], nw_all.at[:, wide], yn_all.at[:, wide], y_all.at[:, wide], hp_all.at[g, 0], h_all.at[g])

    def group(x_ref, b_ref, c_ref, dt_ref, z_ref, bias_ref, alog_ref, d_ref, nw_ref, yn_ref, y_ref, hp_ref, h_scr):
        bias = bias_ref[...]
        a_neg = -jnp.exp(alog_ref[...])
        dsk = d_ref[...]
        cm = _ssd_chunk_common(pl.program_id(0) * Q, dt_ref, b_ref, c_ref, bias, a_neg)
        lane_even = cm["lane"] < 64
        for p in range(4):
            je, jo = 2 * p, 2 * p + 1
            xp = x_ref[:, 128 * p:128 * p + 128]
            xdt = xp * _pair(lane_even, cm["dt"], je)
            xdt_b = xdt.astype(BF)
            m_e = (cm["cb"] * _head_decay(cm, je)).astype(BF)
            m_o = (cm["cb"] * _head_decay(cm, jo)).astype(BF)
            zero = jnp.zeros_like(xdt_b)
            yd = (jnp.dot(m_e, jnp.where(lane_even, xdt_b, zero), preferred_element_type=F32)
                  + jnp.dot(m_o, jnp.where(lane_even, zero, xdt_b), preferred_element_type=F32))
            hp = h_scr[p]
            hp_ref[p] = hp
            yo = jnp.dot(cm["cc"], hp.astype(BF), preferred_element_type=F32) * _pair(lane_even, cm["ecs"], je)
            y_ref[:, 128 * p:128 * p + 128] = yd + yo + xp * _pair_row(lane_even, dsk, je)
            st = jnp.dot(cm["bc_t"], (xdt * _pair(lane_even, cm["dsm"], je)).astype(BF), preferred_element_type=F32)
            h_scr[p] = hp * _pair_row(lane_even, cm["gam"], je) + st
        zc = z_ref[:, :].astype(F32)
        gated = y_ref[:, :] * (zc * _sigmoid(zc))
        yn_ref[:, :] = _rms(gated, nw_ref[...]).astype(BF)

    par = [_spec((None, 2, 1, 128), lambda c, k=k: (k, 0, 0, 0)) for k in range(3)]
    wide = _spec((Q, SSD_W), lambda c: (c, 0))
    xbc = PXBC // XBC
    halo = 2 * HALO
    return pl.pallas_call(
        body, grid=(NCH,),
        in_specs=[_spec((Q, XBC), lambda c: (c, xbc)), _spec((halo, XBC), lambda c: (jnp.maximum(c * (Q // halo) - 1, 0), xbc)),
                  _spec((Q, 256), lambda c: (c, 0)), wide, _spec((4, XBC), lambda c: (0, 0)), _spec((1, XBC), lambda c: (0, 0)),
                  *par, _spec((1, SSD_W), lambda c: (0, 0))],
        out_specs=[wide, wide, _spec((2, 1, 4, 128, 128), lambda c: (0, c, 0, 0, 0))],
        out_shape=[jax.ShapeDtypeStruct((T, SSD_W), BF), jax.ShapeDtypeStruct((T, SSD_W), F32),
                   jax.ShapeDtypeStruct((2, NCH, 4, 128, 128), F32)],
        scratch_shapes=[pltpu.VMEM((2, 4, 128, 128), F32), pltpu.VMEM((Q + HALO, XBC), F32), pltpu.VMEM((Q, XBC), F32)],
        compiler_params=_params(), name="ssd_fwd")(proj, proj, dt_raw, proj, conv_w, conv_b, heads, heads, heads, norm_w)


def ssd_bwd(dyn, proj, dt_raw, conv_w, conv_b, y_pre, h_prev, heads, norm_w, after):
    def body(dyn_all, raw_ref, halo_ref, dt_all, z_all, y_all, hp_all, cw_ref, cb_ref, bias_all, alog_all, d_all, nw_all, _after,
             dz_all, dxbc_ref, ddt_all, dpar_all, dnw_all, dcw_ref, dcb_ref, dh_all, acc_all, pad_scr, act_scr, dsilu_scr, dact_scr, dpad_scr):
        @pl.when(pl.program_id(0) == 0)
        def _():
            dh_all[...] = jnp.zeros_like(dh_all)
            acc_all[...] = jnp.zeros_like(acc_all)
            dnw_all[...] = jnp.zeros_like(dnw_all)
            dcw_ref[...] = jnp.zeros_like(dcw_ref)
            dcb_ref[...] = jnp.zeros_like(dcb_ref)
            dpad_scr[Q:Q + HALO, :] = jnp.zeros((HALO, XBC), F32)

        back = _conv_window(raw_ref, halo_ref, pad_scr)
        pre = _conv(back, cw_ref[...], cb_ref[...])
        sg = _sigmoid(pre)
        act_scr[...] = pre * sg
        dsilu_scr[...] = sg * (1.0 + pre * (1.0 - sg))
        for g in range(2):
            wide, thin = slice(512 * g, 512 * g + 512), slice(128 * g, 128 * g + 128)
            xs, bs, cs = _xbc_cols(g)
            group(dyn_all.at[:, wide], act_scr.at[:, xs], act_scr.at[:, bs], act_scr.at[:, cs], dt_all.at[:, thin], z_all.at[:, wide],
                  y_all.at[:, wide], hp_all.at[g, 0], bias_all.at[g], alog_all.at[g], d_all.at[g], nw_all.at[:, wide],
                  dz_all.at[:, wide], dact_scr.at[:, xs], dact_scr.at[:, bs], dact_scr.at[:, cs], ddt_all.at[:, thin], dpar_all.at[g],
                  dnw_all.at[:, wide], dh_all.at[g], acc_all.at[g])
        dpre = dact_scr[...] * dsilu_scr[...]
        dcw, dcb = _conv_bwd_w(dpre, back)
        dcw_ref[...] += dcw
        dcb_ref[...] += dcb
        dpad_scr[0:Q, :] = dpre
        win = dpad_scr[...]
        dxbc_ref[...] = _conv_bwd_x(lambda s: win[:Q, :] if s == 0 else pltpu.roll(win, Q + HALO - s, axis=0)[:Q, :], cw_ref[...]).astype(BF)
        dpad_scr[Q:Q + HALO, :] = dpre[0:HALO, :]

    def group(dyn_ref, x_ref, b_ref, c_ref, dt_ref, z_ref, y_ref, hp_ref, bias_ref, alog_ref, d_ref, nw_ref,
              dz_ref, dx_ref, db_ref, dc_ref, ddt_ref, dpar_ref, dnw_ref, dh_scr, acc_scr):
        ci = pl.program_id(0)
        bias = bias_ref[...]
        a_neg = -jnp.exp(alog_ref[...])
        dsk = d_ref[...]
        cm = _ssd_chunk_common((NCH - 1 - ci) * Q, dt_ref, b_ref, c_ref, bias, a_neg)
        lane, sub = cm["lane"], cm["sub"]
        lane_even = lane < 64
        cc_t = c_ref[:, :].T.astype(BF)
        cb_t = lax.dot_general(cm["bc"], cm["cc"], NT_DIMS, preferred_element_type=F32)
        zc = z_ref[:, :].astype(F32)
        yc = y_ref[:, :]
        sg = _sigmoid(zc)
        sz = zc * sg
        dgated, dnw = _rms_bwd(dyn_ref[:, :], yc * sz, nw_ref[...])
        dnw_ref[...] += jnp.sum(dnw, axis=0, keepdims=True)
        dz_ref[:, :] = (dgated * yc * (sg * (1.0 + zc * (1.0 - sg)))).astype(BF)
        dy_all = dgated * sz
        dcb = jnp.zeros((Q, Q), F32)
        dcb_t = jnp.zeros((Q, Q), F32)
        db_acc = jnp.zeros((Q, Q), F32)
        dc_acc = jnp.zeros((Q, Q), F32)
        dcs = jnp.zeros((Q, Q), F32)
        ddt = jnp.zeros((Q, Q), F32)
        for p in range(4):
            je, jo = 2 * p, 2 * p + 1
            xp = x_ref[:, 128 * p:128 * p + 128]
            dy = dy_all[:, 128 * p:128 * p + 128]
            dt_p = _pair(lane_even, cm["dt"], je)
            xdt = xp * dt_p
            xdt_b = xdt.astype(BF)
            dy_b = dy.astype(BF)
            zero = jnp.zeros_like(dy_b)
            hp = hp_ref[p]
            hp_b = hp.astype(BF)
            dh = dh_scr[p]
            dh_b = dh.astype(BF)
            acc_scr[p:p + 1, :] += jnp.sum(dy * xp, axis=0, keepdims=True)
            dxp = dy * _pair_row(lane_even, dsk, je)
            e_p = _pair(lane_even, cm["ecs"], je)
            g_p = jnp.dot(cm["cc"], hp_b, preferred_element_type=F32)
            dg_b = (dy * e_p).astype(BF)
            de = dy * g_p * e_p
            dc_acc = dc_acc + lax.dot_general(dg_b, hp_b, NT_DIMS, preferred_element_type=F32)
            dh_in = jnp.dot(cc_t, dg_b, preferred_element_type=F32)
            ds_p = _pair(lane_even, cm["dsm"], je)
            r_p = jnp.dot(cm["bc"], dh_b, preferred_element_type=F32)
            dxdt = r_p * ds_p
            tt = r_p * xdt * ds_p
            db_acc = db_acc + lax.dot_general((xdt * ds_p).astype(BF), dh_b, NT_DIMS, preferred_element_type=F32)
            dgam_m = jnp.sum(dh * hp, axis=0, keepdims=True)
            for j, even in ((je, True), (jo, False)):
                sel = lane_even if even else jnp.logical_not(lane_even)
                dy_j = jnp.where(sel, dy_b, zero)
                l_j = _head_decay(cm, j)
                l_jt = _head_decay_t(cm, j)
                m_j = cm["cb"] * l_j
                m_jt = cb_t * l_jt
                dm = lax.dot_general(dy_j, xdt_b, NT_DIMS, preferred_element_type=F32)
                dm_t = lax.dot_general(xdt_b, dy_j, NT_DIMS, preferred_element_type=F32)
                dxdt = dxdt + jnp.dot(m_jt.astype(BF), dy_j, preferred_element_type=F32)
                dcb = dcb + dm * l_j
                dcb_t = dcb_t + dm_t * l_jt
                t_j = jnp.where(sel, tt, 0.0)
                col = jnp.sum(dm * m_j - dm_t * m_jt + (jnp.where(sel, de, 0.0) - t_j), axis=1, keepdims=True)
                gam_j = cm["gam"][:, j:j + 1]
                last = (jnp.sum(jnp.sum(t_j, axis=0, keepdims=True), axis=1, keepdims=True)
                        + jnp.sum(jnp.where(sel[0:1, :], dgam_m, 0.0), axis=1, keepdims=True) * gam_j)
                col = col + jnp.where(sub[:, 0:1] == Q - 1, last, 0.0)
                dcs = dcs + jnp.where(lane == j, col, 0.0)
            dh_scr[p] = dh_in + dh * _pair_row(lane_even, cm["gam"], je)
            dx_ref[:, 128 * p:128 * p + 128] = dxp + dxdt * dt_p
            dd = dxdt * xp
            ddt = ddt + jnp.where(lane == je, jnp.sum(jnp.where(lane_even, dd, 0.0), axis=1, keepdims=True), 0.0)
            ddt = ddt + jnp.where(lane == jo, jnp.sum(jnp.where(lane_even, 0.0, dd), axis=1, keepdims=True), 0.0)
        dc_ref[:, :] = dc_acc + jnp.dot(dcb.astype(BF), cm["bc"], preferred_element_type=F32)
        db_ref[:, :] = db_acc + jnp.dot(dcb_t.astype(BF), cm["cc"], preferred_element_type=F32)
        tri_t = (sub <= lane).astype(F32)
        dd_a = jnp.dot(tri_t, dcs, precision=lax.Precision.HIGHEST, preferred_element_type=F32)
        ddt = ddt + dd_a * a_neg
        acc_scr[5:6, :] += jnp.sum(dd_a * cm["dt"], axis=0, keepdims=True)
        draw = jnp.where(cm["live"], ddt * _sigmoid_gate(cm["dtr"] + bias), 0.0)
        acc_scr[4:5, :] += jnp.sum(draw, axis=0, keepdims=True)
        ddt_ref[:, :] = draw.astype(BF)

        @pl.when(ci == NCH - 1)
        def _():
            lane1 = _lanes((1, 128))
            dd = jnp.zeros((1, 128), F32)
            for p in range(4):
                row = acc_scr[p:p + 1, :]
                dd = dd + jnp.where(lane1 == 2 * p, jnp.sum(jnp.where(lane1 < 64, row, 0.0), axis=1, keepdims=True), 0.0)
                dd = dd + jnp.where(lane1 == 2 * p + 1, jnp.sum(jnp.where(lane1 < 64, 0.0, row), axis=1, keepdims=True), 0.0)
            dpar_ref[...] = jnp.concatenate([acc_scr[4:5, :], acc_scr[5:6, :] * a_neg, dd, jnp.zeros((5, 128), F32)], axis=0)

    par = [_spec((None, 2, 1, 128), lambda c, k=k: (k, 0, 0, 0)) for k in range(3)]
    wide = _spec((Q, SSD_W), lambda c: (NCH - 1 - c, 0))
    thin = _spec((Q, 256), lambda c: (NCH - 1 - c, 0))
    vec = _spec((1, SSD_W), lambda c: (0, 0))
    xbc = PXBC // XBC
    halo = 2 * HALO
    chunk = pltpu.VMEM((Q, XBC), F32)
    padded = pltpu.VMEM((Q + HALO, XBC), F32)
    return pl.pallas_call(
        body, grid=(NCH,),
        in_specs=[wide, _spec((Q, XBC), lambda c: (NCH - 1 - c, xbc)),
                  _spec((halo, XBC), lambda c: (jnp.maximum((NCH - 1 - c) * (Q // halo) - 1, 0), xbc)), thin, wide, wide,
                  _spec((2, 1, 4, 128, 128), lambda c: (0, NCH - 1 - c, 0, 0, 0)), _spec((4, XBC), lambda c: (0, 0)),
                  _spec((1, XBC), lambda c: (0, 0)), *par, vec, ANY_SPEC],
        out_specs=[wide, _spec((Q, XBC), lambda c: (NCH - 1 - c, 0)), thin, _spec((2, 8, 128), lambda c: (0, 0, 0)), vec,
                   _spec((4, XBC), lambda c: (0, 0)), _spec((1, XBC), lambda c: (0, 0))],
        out_shape=[jax.ShapeDtypeStruct((T, SSD_W), BF), jax.ShapeDtypeStruct((T, XBC), BF), jax.ShapeDtypeStruct((T, 256), BF),
                   jax.ShapeDtypeStruct((2, 8, 128), F32), jax.ShapeDtypeStruct((1, SSD_W), F32), jax.ShapeDtypeStruct((4, XBC), F32),
                   jax.ShapeDtypeStruct((1, XBC), F32)],
        scratch_shapes=[pltpu.VMEM((2, 4, 128, 128), F32), pltpu.VMEM((2, 8, 128), F32), padded, chunk, chunk, chunk, padded],
        compiler_params=_params(), name="ssd_bwd")(dyn, proj, proj, dt_raw, proj, y_pre, h_prev, conv_w, conv_b, heads, heads, heads, norm_w, after)


def _lru_gates(back, cw, cb, wa, ba, wx, bx, lam):
    xr = _conv(back, cw, cb)
    xr_b = xr.astype(BF)
    r = _sigmoid_gate(jnp.dot(xr_b, wa, preferred_element_type=F32) + ba)
    i = _sigmoid_gate(jnp.dot(xr_b, wx, preferred_element_type=F32) + bx)
    sp = _softplus(-lam)
    la = (-LRU_C) * r * sp
    a = jnp.exp(la)
    mult2 = -jnp.tanh(la) * (a * a + 1.0)
    return xr, xr_b, r, i, sp, a, jnp.sqrt(mult2), mult2


SEG_LEN = 68
SEGS = T // SEG_LEN


def _seg_rows(j, k, off=0):
    return pl.ds(off + j * 8 * SEG_LEN + k, 8, stride=SEG_LEN)


def _segmented_scan(mul_ref, mul_row0, add_ref, out_ref, loc_scr, prod_scr, carry_scr, reverse):
    groups = SEGS // 8
    off = mul_row0 + (1 if reverse else 0)

    def local(i, carry):
        k = SEG_LEN - 1 - i if reverse else i
        new = []
        for j in range(groups):
            h, p = carry[2 * j], carry[2 * j + 1]
            m = mul_ref[_seg_rows(j, k, off), :]
            h = m * h + add_ref[_seg_rows(j, k), :]
            p = m * p
            loc_scr[_seg_rows(j, k), :] = h
            prod_scr[_seg_rows(j, k), :] = p
            new += [h, p]
        return tuple(new)

    lax.fori_loop(0, SEG_LEN, local, (jnp.zeros((8, 128), F32), jnp.ones((8, 128), F32)) * groups)

    def chain(i, c):
        s = SEGS - 1 - i if reverse else i
        carry_scr[pl.ds(s, 1), :] = c
        edge = s * SEG_LEN + (0 if reverse else SEG_LEN - 1)
        return loc_scr[pl.ds(edge, 1), :] + prod_scr[pl.ds(edge, 1), :] * c

    lax.fori_loop(0, SEGS, chain, jnp.zeros((1, 128), F32))

    def fold(k, carry):
        for j in range(groups):
            rows = _seg_rows(j, k)
            out_ref[rows, :] = loc_scr[rows, :] + prod_scr[rows, :] * carry_scr[8 * j:8 * j + 8, :]
        return carry

    lax.fori_loop(0, SEG_LEN, fold, 0)


LRU_CW_SPEC = _spec((None, 8, 128), lambda c: (c, 3, 0))


def lru_fwd(proj, cw, cb, wa2, ba, wx2, bx, lam, after):
    def body(x_ref, cw_ref, cb_ref, wa_ref, ba_ref, wx_ref, bx_ref, lam_ref, _after, h_ref, a_ref, xpad, u_scr, loc_scr, prod_scr, carry_scr):
        _fill_padded(xpad, x_ref)

        def chunk(r0):
            xr, _, _, i, _, a, mult, _ = _lru_gates(_back(xpad, r0), cw_ref[0:4, :], cb_ref[...], wa_ref[0], ba_ref[...], wx_ref[0], bx_ref[...],
                                                 lam_ref[...])
            a_ref[pl.ds(r0, Q), :] = a
            u_scr[pl.ds(r0, Q), :] = jnp.where(_rows(a.shape, r0) >= NPAD, mult * (i * xr), 0.0)

        _chunks(chunk, unrolled=True)
        _segmented_scan(a_ref, 0, u_scr, h_ref, loc_scr, prod_scr, carry_scr, reverse=False)

    c0 = PXL // 128
    vec = _spec((1, 128), lambda c: (0, c))
    mat = _spec((1, 128, 128), lambda c: (c, 0, 0))
    seq = pltpu.VMEM((T, 128), F32)
    return pl.pallas_call(
        body, grid=(8,),
        in_specs=[_spec((T, 128), lambda c: (0, c0 + c)), LRU_CW_SPEC, vec, mat, vec, mat, vec, vec, ANY_SPEC],
        out_specs=[_spec((T, 128), lambda c: (0, c)), _spec((T, 128), lambda c: (0, c))],
        out_shape=[jax.ShapeDtypeStruct((T, LRU_W), F32), jax.ShapeDtypeStruct((T, LRU_W), F32)],
        scratch_shapes=[pltpu.VMEM((T + 2 * HALO, 128), F32), seq, seq, seq, pltpu.VMEM((SEGS, 128), F32)],
        compiler_params=_params(), name="lru_fwd")(proj, cw, cb, wa2, ba, wx2, bx, lam, after)


def lru_bwd(dh_out, a, hseq, proj, cw, cb, wa2, ba, wx2, bx, lam):
    def body(d_ref, a_ref, h_ref, x_ref, cw_ref, cb_ref, wa_ref, ba_ref, wx_ref, bx_ref, lam_ref,
             dx_ref, dcw_ref, dcb_ref, dwa_ref, dba_ref, dwx_ref, dbx_ref, dlam_ref, xpad, hpad, dpad, dh_ref, loc_scr, prod_scr, carry_scr):
        _fill_padded(dpad, a_ref)
        _segmented_scan(dpad, HALO, d_ref, dh_ref, loc_scr, prod_scr, carry_scr, reverse=True)
        _fill_padded(xpad, x_ref)
        _fill_padded(hpad, h_ref)
        dpad[0:HALO, :] = jnp.zeros((HALO, 128), F32)
        dpad[T + HALO:T + 2 * HALO, :] = jnp.zeros((HALO, 128), F32)
        for ref in (dcw_ref, dcb_ref, dwa_ref, dba_ref, dwx_ref, dbx_ref, dlam_ref):
            ref[...] = jnp.zeros_like(ref)
        lam = lam_ref[...]

        def first(r0):
            back = _back(xpad, r0)
            xr, xr_b, r, i, sp, a, mult, mult2 = _lru_gates(back, cw_ref[0:4, :], cb_ref[...], wa_ref[0], ba_ref[...], wx_ref[0], bx_ref[...], lam)
            dh = dh_ref[pl.ds(r0, Q), :]
            da = dh * _back(hpad, r0)(1)
            du = jnp.where(_rows(dh.shape, r0) >= NPAD, dh, 0.0)
            dmult = du * (i * xr)
            di = du * (mult * xr)
            dxr = du * (mult * i)
            dla = da * a - dmult * (a * a) * lax.rsqrt(mult2)
            dr = dla * ((-LRU_C) * sp)
            dlam_ref[...] += jnp.sum(dla * ((-LRU_C) * r), axis=0, keepdims=True)
            dpr = dr * r * (1.0 - r)
            dpi = di * i * (1.0 - i)
            dba_ref[...] += jnp.sum(dpr, axis=0, keepdims=True)
            dbx_ref[...] += jnp.sum(dpi, axis=0, keepdims=True)
            dpr_b = dpr.astype(BF)
            dpi_b = dpi.astype(BF)
            dxr = (dxr + lax.dot_general(dpr_b, wa_ref[0], NT_DIMS, preferred_element_type=F32)
                   + lax.dot_general(dpi_b, wx_ref[0], NT_DIMS, preferred_element_type=F32))
            dwa_ref[0] += lax.dot_general(xr_b, dpr_b, TN_DIMS, preferred_element_type=F32)
            dwx_ref[0] += lax.dot_general(xr_b, dpi_b, TN_DIMS, preferred_element_type=F32)
            dpad[pl.ds(r0 + HALO, Q), :] = dxr
            dcw, dcb = _conv_bwd_w(dxr, back)
            dcw_ref[...] += dcw
            dcb_ref[...] += dcb

        _chunks(first, unrolled=True)
        dlam_ref[...] = -dlam_ref[...] * _sigmoid_gate(-lam)

        def second(r0):
            dx_ref[pl.ds(r0, Q), :] = _conv_bwd_x(_ahead(dpad, r0), cw_ref[0:4, :]).astype(BF)

        _chunks(second)

    c0 = PXL // 128
    vec = _spec((1, 128), lambda c: (0, c))
    mat = _spec((1, 128, 128), lambda c: (c, 0, 0))
    col = _spec((T, 128), lambda c: (0, c))
    vshape = jax.ShapeDtypeStruct((1, LRU_W), F32)
    mshape = jax.ShapeDtypeStruct((8, 128, 128), F32)
    pad = pltpu.VMEM((T + 2 * HALO, 128), F32)
    seq = pltpu.VMEM((T, 128), F32)
    return pl.pallas_call(
        body, grid=(8,),
        in_specs=[col, col, col, _spec((T, 128), lambda c: (0, c0 + c)), LRU_CW_SPEC, vec, mat, vec, mat, vec, vec],
        out_specs=[col, _spec((4, 128), lambda c: (0, c)), vec, mat, vec, mat, vec, vec],
        out_shape=[jax.ShapeDtypeStruct((T, LRU_W), BF), jax.ShapeDtypeStruct((4, LRU_W), F32), vshape, mshape, vshape, mshape, vshape, vshape],
        scratch_shapes=[pad, pad, pad, seq, seq, seq, pltpu.VMEM((SEGS, 128), F32)],
        compiler_params=_params(), name="lru_bwd")(dh_out, a, hseq, proj, cw, cb, wa2, ba, wx2, bx, lam)


FF_TILE = 256
FF_TILE_ROWS = list(range(0, D_FF, FF_TILE))


def gate_up(h1, wn, w_gate, w_up):
    def body(h_ref, wn_ref, wg_hbm, wu_hbm, gt_ref, up_ref, act_ref, u_ref, wg_ref, wu_ref, wg_sems, wu_sems):
        gate_ready = _arriving(wg_hbm, wg_ref, wg_sems, FF_TILE_ROWS, FF_TILE)
        up_ready = _arriving(wu_hbm, wu_ref, wu_sems, FF_TILE_ROWS, FF_TILE)
        for r in (0, HALF):
            u_ref[r:r + HALF, :] = _rms(h_ref[r:r + HALF, :], wn_ref[...]).astype(BF)

        def tile(c0):
            cols = pl.ds(c0, FF_TILE)
            gate_ready(c0 // FF_TILE)
            up_ready(c0 // FF_TILE)
            gt = lax.dot_general(u_ref[...], wg_ref[cols, :], NT_DIMS, preferred_element_type=F32)
            up = lax.dot_general(u_ref[...], wu_ref[cols, :], NT_DIMS, preferred_element_type=F32)
            gt_ref[:, cols] = gt.astype(BF)
            up_ref[:, cols] = up.astype(BF)
            act_ref[:, cols] = (gt * _sigmoid(gt) * up).astype(BF)

        _col_tiles(D_FF, FF_TILE, tile)

    big = jax.ShapeDtypeStruct((T, D_FF), BF)
    return pl.pallas_call(
        body, grid=(T // RC,), in_specs=[_rows_spec(D), _vec(D), ANY_SPEC, ANY_SPEC],
        out_specs=[_rows_spec(D_FF), _rows_spec(D_FF), _rows_spec(D_FF), _rows_spec(D)],
        out_shape=[big, big, big, jax.ShapeDtypeStruct((T, D), BF)],
        scratch_shapes=[pltpu.VMEM((D_FF, D), BF)] * 2 + [pltpu.SemaphoreType.DMA((len(FF_TILE_ROWS),))] * 2,
        compiler_params=_params(), name="gate_up")(h1, wn, w_gate, w_up)


def down_loss(act, w_down, h1, target, wf):
    first = NPAD + N_META

    def body(a_ref, w_ref, r_ref, t_hbm, wf_ref, d_ref, db_ref, l_ref, dw_ref, h_scr, t_ref, t_sem):
        i = pl.program_id(0)
        _zero_at_first(l_ref, dw_ref)
        head = pltpu.make_async_copy(t_hbm.at[pl.ds(0, RC - first)], t_ref.at[pl.ds(first, RC - first)], t_sem)
        rest = pltpu.make_async_copy(t_hbm.at[pl.ds(pl.multiple_of(jnp.maximum(i * RC - first, 0), 32), RC)], t_ref, t_sem)

        @pl.when(i == 0)
        def _():
            t_ref[0:first, :] = jnp.zeros((first, D), F32)
            head.start()

        @pl.when(i > 0)
        def _():
            rest.start()

        def tile(c0):
            cols = pl.ds(c0, 512)
            h_scr[:, cols] = r_ref[:, cols] + jnp.dot(a_ref[...], w_ref[:, cols], preferred_element_type=F32)

        _col_tiles(D, 512, tile)

        @pl.when(i == 0)
        def _():
            head.wait()

        @pl.when(i > 0)
        def _():
            rest.wait()

        for r in (0, HALF):
            h = h_scr[r:r + HALF, :]
            live = _rows((HALF, D), i * RC + r) >= first
            err = jnp.where(live, _rms(h, wf_ref[...]) - t_ref[r:r + HALF, :], 0.0)
            l_ref[...] += 0.5 * jnp.sum(jnp.sum(err * err, axis=1, keepdims=True) * (1.0 / D), axis=0, keepdims=True)
            dh, dw = _rms_bwd(err * (1.0 / D), h, wf_ref[...])
            dw_ref[...] += jnp.sum(dw, axis=0, keepdims=True)
            d_ref[r:r + HALF, :] = dh
            db_ref[r:r + HALF, :] = dh.astype(BF)

    return pl.pallas_call(
        body, grid=(T // RC,),
        in_specs=[_rows_spec(D_FF), _whole((D_FF, D)), _rows_spec(D), pl.BlockSpec(memory_space=pl.ANY), _vec(D)],
        out_specs=[_rows_spec(D), _rows_spec(D), _spec((1, 128), lambda i: (0, 0)), _vec(D)],
        out_shape=[jax.ShapeDtypeStruct((T, D), F32), jax.ShapeDtypeStruct((T, D), BF), jax.ShapeDtypeStruct((1, 128), F32),
                   jax.ShapeDtypeStruct((1, D), F32)],
        scratch_shapes=[pltpu.VMEM((RC, D), F32), pltpu.VMEM((RC, D), F32), pltpu.SemaphoreType.DMA],
        compiler_params=_params(), name="down_loss")(act, w_down, h1, target, wf)


def swiglu_bwd(dh2_b, w_down, gt, up, act, u2):
    tn = 256

    def body(d_hbm, u_hbm, w_ref, gt_ref, up_ref, act_ref, dg_ref, du_ref, gd_ref, gg_ref, gu_ref, d_ref, u_ref, d_sems, u_sems):
        chunks = list(range(0, T, RC))
        d_ready = _arriving(d_hbm, d_ref, d_sems, chunks, RC)
        u_ready = _arriving(u_hbm, u_ref, u_sems, chunks, RC)

        def rows(r0):
            part = pl.ds(r0, RC)
            d_ready(r0 // RC)
            dact = lax.dot_general(d_ref[part, :], w_ref[...], NT_DIMS, preferred_element_type=F32)
            gt_ = gt_ref[part, :].astype(F32)
            up_ = up_ref[part, :].astype(F32)
            sg = _sigmoid(gt_)
            dg_ref[part, :] = (dact * up_ * (sg * (1.0 + gt_ * (1.0 - sg)))).astype(BF)
            du_ref[part, :] = (dact * (gt_ * sg)).astype(BF)

        _col_tiles(T, RC, rows)
        for k in range(len(chunks)):
            u_ready(k)
        gd_ref[...] = lax.dot_general(act_ref[...], d_ref[...], TN_DIMS, preferred_element_type=F32).astype(BF)
        gg_ref[...] = lax.dot_general(dg_ref[...], u_ref[...], TN_DIMS, preferred_element_type=F32).astype(BF)
        gu_ref[...] = lax.dot_general(du_ref[...], u_ref[...], TN_DIMS, preferred_element_type=F32).astype(BF)

    cols = _spec((T, tn), lambda j: (0, j))
    wrow = _spec((tn, D), lambda j: (j, 0))
    big = jax.ShapeDtypeStruct((T, D_FF), BF)
    grad = jax.ShapeDtypeStruct((D_FF, D), BF)
    return pl.pallas_call(
        body, grid=(D_FF // tn,), in_specs=[ANY_SPEC, ANY_SPEC, wrow, cols, cols, cols],
        out_specs=[cols, cols, wrow, wrow, wrow], out_shape=[big, big, grad, grad, grad],
        scratch_shapes=[pltpu.VMEM((T, D), BF)] * 2 + [pltpu.SemaphoreType.DMA((T // RC,))] * 2,
        compiler_params=_params(), name="swiglu_bwd")(dh2_b, u2, w_down, gt, up, act)


def gate_up_bwd(dgt, dup, w_gate, w_up, h1, wn, dh2):
    def body(dg_ref, du_ref, wg_hbm, wu_hbm, h_ref, wn_ref, r_ref, d_ref, db_ref, dw_ref, du_scr, wg_ref, wu_ref, wg_sems, wu_sems):
        gate_ready = _arriving(wg_hbm, wg_ref, wg_sems, FF_TILE_ROWS, FF_TILE)
        up_ready = _arriving(wu_hbm, wu_ref, wu_sems, FF_TILE_ROWS, FF_TILE)
        _zero_at_first(dw_ref)

        du_scr[...] = jnp.zeros_like(du_scr)

        def tile(c0):
            k = pl.ds(c0, FF_TILE)
            gate_ready(c0 // FF_TILE)
            up_ready(c0 // FF_TILE)
            du_scr[...] += (jnp.dot(dg_ref[:, k], wg_ref[k, :], preferred_element_type=F32)
                            + jnp.dot(du_ref[:, k], wu_ref[k, :], preferred_element_type=F32))

        _col_tiles(D_FF, FF_TILE, tile)
        for r in (0, HALF):
            dh, dw = _rms_bwd(du_scr[r:r + HALF, :], h_ref[r:r + HALF, :], wn_ref[...])
            dw_ref[...] += jnp.sum(dw, axis=0, keepdims=True)
            dh = dh + r_ref[r:r + HALF, :]
            d_ref[r:r + HALF, :] = dh
            db_ref[r:r + HALF, :] = dh.astype(BF)

    return pl.pallas_call(
        body, grid=(T // RC,),
        in_specs=[_rows_spec(D_FF), _rows_spec(D_FF), ANY_SPEC, ANY_SPEC, _rows_spec(D), _vec(D), _rows_spec(D)],
        out_specs=[_rows_spec(D), _rows_spec(D), _vec(D)],
        out_shape=[jax.ShapeDtypeStruct((T, D), F32), jax.ShapeDtypeStruct((T, D), BF), jax.ShapeDtypeStruct((1, D), F32)],
        scratch_shapes=[pltpu.VMEM((RC, D), F32)] + [pltpu.VMEM((D_FF, D), BF)] * 2 + [pltpu.SemaphoreType.DMA((len(FF_TILE_ROWS),))] * 2,
        compiler_params=_params(), name="gate_up_bwd")(dgt, dup, w_gate, w_up, h1, wn, dh2)


def _adamw(w, g, m, v):
    m = ADAM_B1 * m + (1.0 - ADAM_B1) * g
    v = ADAM_B2 * v + (1.0 - ADAM_B2) * (g * g)
    m_hat = m / (1.0 - ADAM_B1 ** ADAM_STEP)
    v_hat = v / (1.0 - ADAM_B2 ** ADAM_STEP)
    delta = -ADAM_LR * (m_hat / (jnp.sqrt(v_hat) + ADAM_EPS) + ADAM_WD * w)
    return delta, m, v


def adamw_shards(name, recvs, ws, ms, vs):
    n = len(ws)

    def body(*refs):
        ins, outs = refs[:4 * n], refs[4 * n:]
        for k in range(n):
            p_ref, w_ref, m_ref, v_ref = ins[k], ins[n + k], ins[2 * n + k], ins[3 * n + k]
            g = p_ref[0].astype(F32)
            for s in range(1, 8):
                g = g + p_ref[s].astype(F32)
            outs[4 * k][...] = g
            outs[4 * k + 1][...], outs[4 * k + 2][...], outs[4 * k + 3][...] = _adamw(w_ref[...], g, m_ref[...], v_ref[...])

    tiles = [_spec((w.shape[0] // 2, w.shape[1]), lambda i: (i, 0)) for w in ws]
    recv_tiles = [_spec((8, w.shape[0] // 2, w.shape[1]), lambda i: (0, i, 0)) for w in ws]
    res = pl.pallas_call(
        body, grid=(2,), in_specs=recv_tiles + tiles * 3,
        out_specs=[t for t in tiles for _ in range(4)],
        out_shape=[jax.ShapeDtypeStruct(w.shape, F32) for w in ws for _ in range(4)],
        compiler_params=_params(), name=name)(*recvs, *ws, *ms, *vs)
    return [list(res[4 * k:4 * k + 4]) for k in range(n)]


def adamw_w_in(recv, w, m, v, after):
    rows = 34
    per_row = D // 128

    def body(p_ref, w_ref, m_ref, v_ref, _after, g_ref, d_ref, mo_ref, vo_ref):
        def chunk(c, carry):
            lines = pl.ds(pl.multiple_of(c * per_row * rows, 16), per_row * rows)
            g = p_ref[0, lines, :].astype(F32)
            for s in range(1, 8):
                g = g + p_ref[s, lines, :].astype(F32)
            g = g.reshape(rows, per_row, 128)
            part = pl.ds(c * rows, rows)
            g_ref[part] = g
            d_ref[part], mo_ref[part], vo_ref[part] = _adamw(w_ref[part], g, m_ref[part], v_ref[part])
            return carry

        lax.fori_loop(0, w.shape[0] // rows, chunk, 0)

    shape = jax.ShapeDtypeStruct(w.shape, F32)
    whole = pl.BlockSpec(memory_space=pltpu.VMEM)
    return pl.pallas_call(body, out_shape=[shape] * 4, in_specs=[whole] * 4 + [ANY_SPEC], compiler_params=_params(0),
                          name="adamw_w_in")(recv, w, m, v, after)


def sum_slabs(recv):
    def body(p_ref, o_ref):
        g = p_ref[0]
        for s in range(1, 8):
            g = g + p_ref[s]
        for s in range(8):
            o_ref[s] = g

    return pl.pallas_call(body, out_shape=jax.ShapeDtypeStruct(recv.shape, F32), compiler_params=_params(0), name="sum_slabs")(recv)


SIMPLE = [("norm1_w", 1024), ("ssd_conv_b", 1536), ("ssd_dt_bias", 16), ("ssd_a_log", 16), ("ssd_d", 16), ("ssd_norm_w", 1024),
          ("lru_conv_b", 1024), ("lru_ba", 1024), ("lru_bx", 1024), ("lru_lambda", 1024), ("lru_norm_w", 1024), ("norm2_w", 1024),
          ("final_norm_w", 1024)]
SPECIAL = ["lru_wa", "lru_wx", "meta_tokens", "ssd_conv_w", "lru_conv_w"]
SM_ROWS = 176
SM_WA, SM_WX, SM_META, SM_SCW, SM_LCW, SM_LOSS = 14, 78, 142, 158, 166, 170


def _simple_rows():
    rows, r = {}, 0
    for name, n in SIMPLE:
        rows[name] = r
        r += -(-n // 1024)
    return rows


def adamw_small(sm, special_g, ws, ms, vs):
    rows = _simple_rows()
    ns, nx = len(SIMPLE), len(SPECIAL)

    def body(*refs):
        sm_ref = refs[0]
        gx = refs[1:1 + nx]
        wr = refs[1 + nx:1 + nx + ns + nx]
        mr = refs[1 + nx + ns + nx:1 + nx + 2 * (ns + nx)]
        vr = refs[1 + nx + 2 * (ns + nx):1 + nx + 3 * (ns + nx)]
        outs = refs[1 + nx + 3 * (ns + nx):]
        o = 0
        for k, (name, n) in enumerate(SIMPLE):
            r0 = rows[name]
            for c0 in range(0, n, 1024):
                wd = min(1024, n - c0)
                g = sm_ref[r0 + c0 // 1024:r0 + c0 // 1024 + 1, 0:wd]
                sl = (slice(None), slice(c0, c0 + wd))
                d, m2, v2 = _adamw(wr[k][sl], g, mr[k][sl], vr[k][sl])
                outs[o][sl] = g
                outs[o + 1][sl] = d
                outs[o + 2][sl] = m2
                outs[o + 3][sl] = v2
            o += 4
        for k in range(nx):
            g = gx[k][...]
            d, m2, v2 = _adamw(wr[ns + k][...], g, mr[ns + k][...], vr[ns + k][...])
            outs[o][...] = g
            outs[o + 1][...] = d
            outs[o + 2][...] = m2
            outs[o + 3][...] = v2
            o += 4
        outs[o][...] = sm_ref[SM_LOSS:SM_LOSS + 1, 0:1]

    out_shape = []
    for k in range(ns + nx):
        out_shape += [jax.ShapeDtypeStruct(ws[k].shape, F32)] * 4
    out_shape.append(jax.ShapeDtypeStruct((1, 1), F32))
    return pl.pallas_call(body, out_shape=out_shape, compiler_params=_params(0), name="adamw_small")(sm, *special_g, *ws, *ms, *vs)


def _place():
    return lax.axis_index("x"), lax.axis_index("y"), lax.axis_index("c")


def _index(px, py, pc):
    return 4 * px + 2 * py + pc


def all_gather(name, shards):
    n = len(shards)
    hbm = pl.BlockSpec(memory_space=pl.ANY)

    def pieces(s):
        tile = 32 // s.dtype.itemsize
        per = s.shape[0] // tile // 4 * tile
        return [(0, s.shape[0])] if s.shape[0] < 256 else [(r * per, per if r < 3 else s.shape[0] - 3 * per) for r in range(4)]

    parts = [pieces(s) for s in shards]
    first_sem = [7 * sum(len(p) for p in parts[:i]) for i in range(n + 1)]

    def body(*refs):
        ins, outs = refs[:n], refs[n:2 * n]
        send_sems, recv_sems, local_sems = refs[2 * n:]
        x, y, c = _place()
        me, sibling = (x, y, c), (x, y, 1 - c)
        chips = [(1 - x, y), (x, 1 - y), (1 - x, 1 - y)]

        def copy(i, r, k, block, to, src=None):
            rows = pl.ds(*parts[i][r])
            dst = outs[i].at[_index(*block), rows]
            sem = first_sem[i] + 7 * r + k
            return pltpu.make_async_remote_copy(src_ref=dst if src is None else src.at[rows], dst_ref=dst, send_sem=send_sems.at[sem],
                                                recv_sem=recv_sems.at[sem], device_id=to, device_id_type=MESH)

        every = [(i, r) for i in range(n) for r in range(len(parts[i]))]
        mine = [pltpu.make_async_copy(ins[i], outs[i].at[_index(*me)], local_sems.at[i]) for i in range(n)]
        for cp in mine:
            cp.start()
        first = []
        for i, r in every:
            first += [copy(i, r, 1 + j, me, (*chip, c), src=ins[i]) for j, chip in enumerate(chips)]
            first.append(copy(i, r, 0, me, sibling, src=ins[i]))
        for cp in first:
            cp.start()
        passed = []
        for i, r in every:
            for j, chip in enumerate(chips):
                copy(i, r, 1 + j, (*chip, c), me).wait_recv()
                cp = copy(i, r, 4 + j, (*chip, c), sibling)
                cp.start()
                passed.append(cp)
        for i, r in every:
            copy(i, r, 0, sibling, me).wait_recv()
            for j, chip in enumerate(chips):
                copy(i, r, 4 + j, (*chip, 1 - c), me).wait_recv()
        for cp in first + passed:
            cp.wait_send()
        for cp in mine:
            cp.wait()

    return pl.pallas_call(
        body, in_specs=[hbm] * n, out_specs=[hbm] * n,
        out_shape=[jax.ShapeDtypeStruct((8,) + s.shape, s.dtype) for s in shards],
        scratch_shapes=[pltpu.SemaphoreType.DMA((first_sem[n],)), pltpu.SemaphoreType.DMA((first_sem[n],)), pltpu.SemaphoreType.DMA((n,))],
        name=name)(*shards)


HBM_SPEC = pl.BlockSpec(memory_space=pltpu.HBM)
SEM_SPEC = pl.BlockSpec(memory_space=pltpu.SEMAPHORE)
EFFECT = pltpu.SideEffectType.DATAFLOW_SIDE_EFFECTING


def _peers(x, y, c):
    return [((1 - x) if k & 4 else x, (1 - y) if k & 2 else y, (1 - c) if k & 1 else c) for k in range(1, 8)]


def _pieces(rows):
    for n in (4, 2):
        if rows % (16 * n) == 0:
            return [(r * (rows // n), rows // n) for r in range(n)]
    return [(0, rows)]


def _peer_copies(src, land, send_sems, recv_sems, k, peer, mine):
    block = src.at[_index(*peer)]
    return [pltpu.make_async_remote_copy(src_ref=block.at[pl.ds(r0, nr)], dst_ref=land.at[mine, pl.ds(r0, nr)], send_sem=send_sems.at[k],
                                         recv_sem=recv_sems.at[k], device_id=peer, device_id_type=MESH)
            for r0, nr in _pieces(block.shape[0])]


OWN = 7


def _own_copy(src, land, send_sems, mine):
    return pltpu.make_async_copy(src.at[mine], land.at[mine], send_sems.at[OWN])


def copies_start(name, srcs):
    n = len(srcs)

    def body(*refs):
        ins, lands = refs[:n], refs[n:2 * n]
        sends, recvs = refs[2 * n:3 * n], refs[3 * n:4 * n]
        token = refs[-1]
        x, y, c = _place()
        mine = _index(x, y, c)
        for i in range(n):
            per_peer = [_peer_copies(ins[i], lands[i], sends[i], recvs[i], k, peer, mine) for k, peer in enumerate(_peers(x, y, c))]
            for piece in zip(*per_peer):
                for cp in piece:
                    cp.start()
            _own_copy(ins[i], lands[i], sends[i], mine).start()
        token[...] = jnp.zeros_like(token)

    res = pl.pallas_call(
        body, name=name,
        out_shape=([pltpu.SemaphoreType.DMA((8,))] * n + [pltpu.SemaphoreType.DMA((7,))] * n + [pltpu.HBM(s.shape, s.dtype) for s in srcs] * 2
                   + [jax.ShapeDtypeStruct((8, 128), F32)]),
        in_specs=[HBM_SPEC] * (2 * n),
        out_specs=[SEM_SPEC] * (2 * n) + [HBM_SPEC] * (2 * n) + [pl.BlockSpec(memory_space=pltpu.VMEM)],
        input_output_aliases={i: 2 * n + i for i in range(2 * n)},
        compiler_params=pltpu.CompilerParams(has_side_effects=EFFECT),
    )(*[pltpu.with_memory_space_constraint(s, pltpu.HBM) for s in srcs],
      *[pltpu.with_memory_space_constraint(lax.empty(s.shape, s.dtype), pltpu.HBM) for s in srcs])
    return [(res[i], res[n + i], res[2 * n + i], res[3 * n + i]) for i in range(n)], res[-1]


def copies_wait(name, started, after):
    n = len(started)

    def body(*refs):
        ins, lands = refs[:n], refs[n:2 * n]
        sends, recvs = refs[2 * n:3 * n], refs[3 * n:4 * n]
        x, y, c = _place()
        mine = _index(x, y, c)
        for i in range(n):
            for k, peer in enumerate(_peers(x, y, c)):
                arrival = pltpu.make_async_remote_copy(src_ref=ins[i].at[mine], dst_ref=lands[i].at[_index(*peer)],
                                                       send_sem=sends[i].at[k], recv_sem=recvs[i].at[k], device_id=peer, device_id_type=MESH)
                arrival.wait_send()
                arrival.wait_recv()
            _own_copy(ins[i], lands[i], sends[i], mine).wait()

    srcs = [s[2] for s in started]
    lands = [s[3] for s in started]
    res = pl.pallas_call(
        body, name=name,
        out_shape=[pltpu.HBM(s.shape, s.dtype) for s in srcs] + [pltpu.HBM(z.shape, z.dtype) for z in lands],
        in_specs=[HBM_SPEC] * (2 * n) + [SEM_SPEC] * (2 * n) + [ANY_SPEC] * len(after),
        out_specs=[HBM_SPEC] * (2 * n),
        input_output_aliases={i: i for i in range(2 * n)},
        compiler_params=pltpu.CompilerParams(has_side_effects=EFFECT),
    )(*srcs, *lands, *[s[0] for s in started], *[s[1] for s in started], *after)
    return list(res[n:])


def _hop(src, land, send_sems, recv_sems, k, block, to):
    dst = land.at[_index(*block)]
    return pltpu.make_async_remote_copy(src_ref=dst if src is None else src, dst_ref=dst, send_sem=send_sems.at[k], recv_sem=recv_sems.at[k],
                                        device_id=to, device_id_type=MESH)


def _own_block(src, land, send_sems, mine):
    return pltpu.make_async_copy(src, land.at[mine], send_sems.at[4])


def _other_chips(x, y):
    return [(1 - x, y), (x, 1 - y), (1 - x, 1 - y)]


def gather_start(name, shards, through):
    n, m = len(shards), len(through)

    def body(*refs):
        ins, lands = refs[:n], refs[n:2 * n]
        sends, recvs = refs[2 * n + m:3 * n + m], refs[3 * n + m:4 * n + m]
        x, y, c = _place()
        for i in range(n):
            for j, chip in enumerate(_other_chips(x, y)):
                _hop(ins[i], lands[i], sends[i], recvs[i], 1 + j, (x, y, c), (*chip, c)).start()
            _hop(ins[i], lands[i], sends[i], recvs[i], 0, (x, y, c), (x, y, 1 - c)).start()
            _own_block(ins[i], lands[i], sends[i], _index(x, y, c)).start()

    own, passing = pltpu.SemaphoreType.DMA((5,)), pltpu.SemaphoreType.DMA((3,))
    zones = [jax.ShapeDtypeStruct((8,) + s.shape, s.dtype) for s in shards]
    res = pl.pallas_call(
        body, name=name,
        out_shape=([own] * (2 * n) + [passing] * (2 * n) + [pltpu.HBM(s.shape, s.dtype) for s in shards]
                   + [pltpu.HBM(z.shape, z.dtype) for z in zones] + [pltpu.HBM(t.shape, t.dtype) for t in through]),
        in_specs=[HBM_SPEC] * (2 * n + m),
        out_specs=[SEM_SPEC] * (4 * n) + [HBM_SPEC] * (2 * n + m),
        input_output_aliases={i: 4 * n + i for i in range(2 * n + m)},
        compiler_params=pltpu.CompilerParams(has_side_effects=EFFECT),
    )(*[pltpu.with_memory_space_constraint(s, pltpu.HBM) for s in shards],
      *[pltpu.with_memory_space_constraint(lax.empty(z.shape, z.dtype), pltpu.HBM) for z in zones],
      *[pltpu.with_memory_space_constraint(t, pltpu.HBM) for t in through])
    return [[res[4 * n + i], res[5 * n + i], res[i], res[n + i], res[2 * n + i], res[3 * n + i]] for i in range(n)], list(res[6 * n:])


def gather_stage(name, pass_on, finish, after):
    arrays = pass_on + finish
    n = len(arrays)

    def body(*refs):
        ins, lands = refs[:n], refs[n:2 * n]
        sems = [refs[(2 + q) * n:(3 + q) * n] for q in range(4)]
        x, y, c = _place()
        me, sibling = (x, y, c), (x, y, 1 - c)
        for i in range(len(pass_on)):
            send, recv, send_on, recv_on = (q[i] for q in sems)
            for j, chip in enumerate(_other_chips(x, y)):
                _hop(None, lands[i], send, recv, 1 + j, (*chip, c), me).wait_recv()
                _hop(None, lands[i], send_on, recv_on, j, (*chip, c), sibling).start()
        for i in range(len(pass_on), n):
            send, recv, send_on, recv_on = (q[i] for q in sems)
            _hop(ins[i], lands[i], send, recv, 0, sibling, me).wait_recv()
            for j, chip in enumerate(_other_chips(x, y)):
                _hop(None, lands[i], send_on, recv_on, j, (*chip, 1 - c), me).wait_recv()
            _hop(ins[i], lands[i], send, recv, 0, me, sibling).wait_send()
            _own_block(ins[i], lands[i], send, _index(*me)).wait()
            for j, chip in enumerate(_other_chips(x, y)):
                _hop(ins[i], lands[i], send, recv, 1 + j, me, (*chip, c)).wait_send()
                _hop(None, lands[i], send_on, recv_on, j, (*chip, c), sibling).wait_send()
        refs[-1][...] = jnp.zeros_like(refs[-1])

    res = pl.pallas_call(
        body, name=name,
        out_shape=([pltpu.HBM(a[0].shape, a[0].dtype) for a in arrays] + [pltpu.HBM(a[1].shape, a[1].dtype) for a in arrays]
                   + [jax.ShapeDtypeStruct((8, 128), F32)]),
        in_specs=[HBM_SPEC] * (2 * n) + [SEM_SPEC] * (4 * n) + [ANY_SPEC],
        out_specs=[HBM_SPEC] * (2 * n) + [pl.BlockSpec(memory_space=pltpu.VMEM)],
        input_output_aliases={i: i for i in range(2 * n)},
        compiler_params=pltpu.CompilerParams(has_side_effects=EFFECT),
    )(*[a[0] for a in arrays], *[a[1] for a in arrays], *[a[2 + q] for q in range(4) for a in arrays], after)
    for i, a in enumerate(arrays):
        a[0], a[1] = res[i], res[n + i]
    return [a[1] for a in finish], res[-1]


WEIGHTS = ["meta_tokens", "norm1_w", "w_in", "ssd_conv_w", "ssd_conv_b", "ssd_dt_bias", "ssd_a_log", "ssd_d", "ssd_norm_w", "lru_conv_w",
           "lru_conv_b", "lru_wa", "lru_ba", "lru_wx", "lru_bx", "lru_lambda", "lru_norm_w", "w_out", "norm2_w", "w_gate", "w_up", "w_down",
           "final_norm_w"]
BIG = ["w_in", "w_out", "w_gate", "w_up", "w_down"]
COLUMN_SHARDED = ["w_in", "w_gate", "w_up"]


def _pair_blocks(w):
    w = w.reshape(8, 2, 64, 64)
    z = jnp.zeros((8, 64, 64), w.dtype)
    return jnp.concatenate([jnp.concatenate([w[:, 0], z], axis=2), jnp.concatenate([z, w[:, 1]], axis=2)], axis=1)


def _unpair_blocks(w2):
    return jnp.stack([w2[:, :64, :64], w2[:, 64:, 64:]], axis=1).reshape(16, 64, 64)


def _per_group(vs):
    return jnp.pad(jnp.concatenate(vs, axis=0).reshape(len(vs), 2, 1, 8), ((0, 0), (0, 0), (0, 0), (0, 120)))


def _pad_cols(v, n):
    return jnp.pad(v, ((0, 0), (0, n - v.shape[1])))


def local_step(x, target, small_w, ssd_cw, w_in_shards, fetch, send, p):
    heads = _per_group([p["ssd_dt_bias"], p["ssd_a_log"], p["ssd_d"]])
    wa2 = _pair_blocks(p["lru_wa"]).astype(BF)
    wx2 = _pair_blocks(p["lru_wx"]).astype(BF)
    lru = (small_w, p["lru_conv_b"], wa2, p["lru_ba"], wx2, p["lru_bx"], p["lru_lambda"])

    proj, dt_raw, u1, h0, w_in, w_dt = in_proj(x, small_w, p["norm1_w"], w_in_shards)
    yn_ssd, y_pre, h_prev = ssd_fwd(proj, dt_raw, ssd_cw, p["ssd_conv_b"], heads, p["ssd_norm_w"])
    _, moved = fetch([], yn_ssd)
    hseq, a = lru_fwd(proj, *lru, moved)
    (w_out,), _ = fetch(["w_out"], hseq)
    h1, cat = out_proj(yn_ssd, proj, hseq, p["lru_norm_w"], w_out, h0)
    (w_gate, w_up), _ = fetch(["w_gate", "w_up"], h1)
    gt, up, act, u2 = gate_up(h1, p["norm2_w"], w_gate, w_up)
    (w_down,), _ = fetch(["w_down"], act)
    dh2, dh2_b, loss, d_fnw = down_loss(act, w_down, h1, target, p["final_norm_w"])

    dgt, dup, g_down, g_gate, g_up = swiglu_bwd(dh2_b, w_down, gt, up, act, u2)
    dh1, dh1_b, d_n2 = gate_up_bwd(dgt, dup, w_gate, w_up, h1, p["norm2_w"], dh2)
    sent = send({"w_down": g_down, "w_gate": g_gate, "w_up": g_up, "w_out": weight_grad("dw_out", cat, dh1_b)})
    dyn, dh_out, dg_b, d_lnw = out_proj_bwd(dh1_b, w_out, proj, hseq, p["lru_norm_w"], sent)

    dxl_b, d_lcw, d_lcb, dwa2, d_ba, dwx2, d_bx, d_lam = lru_bwd(dh_out, a, hseq, proj, *lru)
    dz_b, dxbc_b, ddt_b, dpar, d_snw, d_scw, d_scb = ssd_bwd(dyn, proj, dt_raw, ssd_cw, p["ssd_conv_b"], y_pre, h_prev, heads,
                                                             p["ssd_norm_w"], sent)
    sent = send({"w_in": in_weight_grad([dz_b, dxbc_b, dg_b, dxl_b], [0, SSD_W, 2576, 2576 + LRU_W], ddt_b, u1)})
    grad_x, d_meta, d_n1, dwa2, dwx2 = in_proj_bwd(dz_b, dg_b, dxl_b, dxbc_b, ddt_b, w_in, w_dt, h0, p["norm1_w"], dh1, sent, [dwa2, dwx2])
    small = {"norm1_w": d_n1, "ssd_conv_b": d_scb, "ssd_dt_bias": dpar[:, 0, :8].reshape(1, 16), "ssd_a_log": dpar[:, 1, :8].reshape(1, 16),
             "ssd_d": dpar[:, 2, :8].reshape(1, 16), "ssd_norm_w": d_snw, "lru_conv_b": d_lcb, "lru_ba": d_ba, "lru_bx": d_bx,
             "lru_lambda": d_lam, "lru_norm_w": d_lnw, "norm2_w": d_n2, "final_norm_w": d_fnw,
             "lru_wa": _unpair_blocks(dwa2), "lru_wx": _unpair_blocks(dwx2), "meta_tokens": d_meta,
             "ssd_conv_w": d_scw, "lru_conv_w": d_lcw}
    return loss, grad_x, small


def _pack_small(small, loss):
    rows = [_pad_cols(small[name], -(-n // 1024) * 1024).reshape(-1, 1024) for name, n in SIMPLE]
    rows += [small["lru_wa"].reshape(64, 1024), small["lru_wx"].reshape(64, 1024), small["meta_tokens"],
             _pad_cols(small["ssd_conv_w"], 2048).reshape(8, 1024), small["lru_conv_w"], _pad_cols(loss[:, 0:1], 1024)]
    sm = jnp.concatenate(rows, axis=0)
    return jnp.pad(sm, ((0, SM_ROWS - sm.shape[0]), (0, 0)))


def _slabs(g):
    return g.reshape(8, g.shape[0] // 8, g.shape[1])


def _unslab(g):
    return g.reshape(8 * g.shape[1], g.shape[2])


def kernel(x, meta_tokens, norm1_w, w_in, ssd_conv_w, ssd_conv_b, ssd_dt_bias, ssd_a_log, ssd_d, ssd_norm_w, lru_conv_w, lru_conv_b, lru_wa, lru_ba, lru_wx, lru_bx, lru_lambda, lru_norm_w, w_out, norm2_w, w_gate, w_up, w_down, final_norm_w, loss_target, m_meta_tokens, m_norm1_w, m_w_in, m_ssd_conv_w, m_ssd_conv_b, m_ssd_dt_bias, m_ssd_a_log, m_ssd_d, m_ssd_norm_w, m_lru_conv_w, m_lru_conv_b, m_lru_wa, m_lru_ba, m_lru_wx, m_lru_bx, m_lru_lambda, m_lru_norm_w, m_w_out, m_norm2_w, m_w_gate, m_w_up, m_w_down, m_final_norm_w, v_meta_tokens, v_norm1_w, v_w_in, v_ssd_conv_w, v_ssd_conv_b, v_ssd_dt_bias, v_ssd_a_log, v_ssd_d, v_ssd_norm_w, v_lru_conv_w, v_lru_conv_b, v_lru_wa, v_lru_ba, v_lru_wx, v_lru_bx, v_lru_lambda, v_lru_norm_w, v_w_out, v_norm2_w, v_w_gate, v_w_up, v_w_down, v_final_norm_w):
    w = dict(meta_tokens=meta_tokens, norm1_w=norm1_w, w_in=w_in[0], ssd_conv_w=ssd_conv_w[0], ssd_conv_b=ssd_conv_b, ssd_dt_bias=ssd_dt_bias,
             ssd_a_log=ssd_a_log, ssd_d=ssd_d, ssd_norm_w=ssd_norm_w, lru_conv_w=lru_conv_w[0], lru_conv_b=lru_conv_b, lru_wa=lru_wa[0],
             lru_ba=lru_ba, lru_wx=lru_wx[0], lru_bx=lru_bx, lru_lambda=lru_lambda, lru_norm_w=lru_norm_w, w_out=w_out[0], norm2_w=norm2_w,
             w_gate=w_gate[0], w_up=w_up[0], w_down=w_down[0], final_norm_w=final_norm_w.reshape(1, D))
    m = dict(meta_tokens=m_meta_tokens, norm1_w=m_norm1_w, w_in=m_w_in[0], ssd_conv_w=m_ssd_conv_w[0], ssd_conv_b=m_ssd_conv_b,
             ssd_dt_bias=m_ssd_dt_bias, ssd_a_log=m_ssd_a_log, ssd_d=m_ssd_d, ssd_norm_w=m_ssd_norm_w, lru_conv_w=m_lru_conv_w[0],
             lru_conv_b=m_lru_conv_b, lru_wa=m_lru_wa[0], lru_ba=m_lru_ba, lru_wx=m_lru_wx[0], lru_bx=m_lru_bx, lru_lambda=m_lru_lambda,
             lru_norm_w=m_lru_norm_w, w_out=m_w_out[0], norm2_w=m_norm2_w, w_gate=m_w_gate[0], w_up=m_w_up[0], w_down=m_w_down[0],
             final_norm_w=m_final_norm_w.reshape(1, D))
    v = dict(meta_tokens=v_meta_tokens, norm1_w=v_norm1_w, w_in=v_w_in[0], ssd_conv_w=v_ssd_conv_w[0], ssd_conv_b=v_ssd_conv_b,
             ssd_dt_bias=v_ssd_dt_bias, ssd_a_log=v_ssd_a_log, ssd_d=v_ssd_d, ssd_norm_w=v_ssd_norm_w, lru_conv_w=v_lru_conv_w[0],
             lru_conv_b=v_lru_conv_b, lru_wa=v_lru_wa[0], lru_ba=v_lru_ba, lru_wx=v_lru_wx[0], lru_bx=v_lru_bx, lru_lambda=v_lru_lambda,
             lru_norm_w=v_lru_norm_w, w_out=v_w_out[0], norm2_w=v_norm2_w, w_gate=v_w_gate[0], w_up=v_w_up[0], w_down=v_w_down[0],
             final_norm_w=v_final_norm_w.reshape(1, D))
    shapes = dict(meta_tokens=meta_tokens.shape, norm1_w=norm1_w.shape, w_in=w_in.shape, ssd_conv_w=ssd_conv_w.shape,
                  ssd_conv_b=ssd_conv_b.shape, ssd_dt_bias=ssd_dt_bias.shape, ssd_a_log=ssd_a_log.shape, ssd_d=ssd_d.shape,
                  ssd_norm_w=ssd_norm_w.shape, lru_conv_w=lru_conv_w.shape, lru_conv_b=lru_conv_b.shape, lru_wa=lru_wa.shape,
                  lru_ba=lru_ba.shape, lru_wx=lru_wx.shape, lru_bx=lru_bx.shape, lru_lambda=lru_lambda.shape, lru_norm_w=lru_norm_w.shape,
                  w_out=w_out.shape, norm2_w=norm2_w.shape, w_gate=w_gate.shape, w_up=w_up.shape, w_down=w_down.shape,
                  final_norm_w=final_norm_w.shape)
    me = _index(*_place())
    for n in COLUMN_SHARDED:
        w[n], m[n], v[n] = w[n].T, m[n].T, v[n].T

    small_shard = jnp.concatenate([w["meta_tokens"], _pad_cols(w["ssd_conv_w"], 256).reshape(8, 128), w["lru_conv_w"],
                                   jnp.zeros((4, 128), F32)], axis=0)
    g_in, gs = all_gather("gather_w_in", [w["w_in"].astype(BF), small_shard])
    later = ["w_out", "w_gate", "w_up", "w_down"]
    started, (g_in, gs) = gather_start("gather_rest_start", [w[n].astype(BF) for n in later], [g_in, gs])
    started = dict(zip(later, started))
    ssd_cw = gs[:, 16:24].reshape(8, 4, 256)[:, :, :192].transpose(1, 0, 2).reshape(4, XBC)

    def fetch(names, after):
        pass_on = {"w_out": ["w_down"], "w_gate": [], "w_down": []}[names[0]] if names else ["w_out", "w_gate", "w_up"]
        got, zero = gather_stage("gather_" + (names[0] + "_wait" if names else "pass_on"), [started[n] for n in pass_on],
                                 [started[n] for n in names], after)
        return [_unslab(g) for g in got], zero

    in_flight = {}

    def send(grads):
        names = list(grads)
        st, zero = copies_start("grads_" + names[0] + "_start", [grads[n] if n == "small" else _slabs(grads[n]) for n in names])
        in_flight.update(zip(names, st))
        return zero

    loss, grad_x, small = local_step(x[0], loss_target[0], gs, ssd_cw, g_in, fetch, send, w)
    send({"small": _pack_small(small, loss).reshape(8, SM_ROWS // 8, 1024)})

    out = {}
    early = ["w_down", "w_gate", "w_up", "w_out"]
    recv = dict(zip(early, copies_wait("grads_early_wait", [in_flight[n] for n in early], [in_flight["small"][2]])))
    for pair in (early[:2], early[2:]):
        done = adamw_shards("adamw_" + pair[0], [recv[n] for n in pair], [w[n] for n in pair], [m[n] for n in pair], [v[n] for n in pair])
        out.update(zip(pair, done))
    recv_in, recv_small = copies_wait("grads_late_wait", [in_flight["w_in"], in_flight["small"]], [out[n][0] for n in early])
    def lines(a):
        return jnp.transpose(a.reshape(D // 128, 128, IN_COLS // 8), (2, 0, 1))

    gathering, zero = copies_start("gather_small_start", [sum_slabs(recv_small)])
    updated = adamw_w_in(recv_in, lines(w_in), lines(m_w_in), lines(v_w_in), zero)
    out["w_in"] = [jnp.transpose(o, (1, 2, 0)).reshape(D, IN_COLS // 8) for o in updated]
    for n in ("w_gate", "w_up"):
        out[n] = [o.T for o in out[n]]
    sm = copies_wait("gather_small_wait", gathering, [updated[0]])[0].reshape(SM_ROWS, 1024)
    special_g =[sm[SM_WA:SM_WA + 64].reshape(16, 64, 64), sm[SM_WX:SM_WX + 64].reshape(16, 64, 64),
                 lax.dynamic_slice(sm[SM_META:SM_META + 16], (0, 128 * me), (16, 128)),
                 lax.dynamic_slice(sm[SM_SCW:SM_SCW + 8].reshape(4, 2048), (0, 192 * me), (4, 192)),
                 lax.dynamic_slice(sm[SM_LCW:SM_LCW + 4], (0, 128 * me), (4, 128))]
    names = [n for n, _ in SIMPLE] + SPECIAL
    res = adamw_small(sm, special_g, [w[n] for n in names], [m[n] for n in names], [v[n] for n in names])
    for k, n in enumerate(names):
        out[n] = res[4 * k:4 * k + 4]
    flat = [res[-1].reshape(()), grad_x[None]]
    for k in range(4):
        flat += [out[n][k].reshape(shapes[n]) for n in WEIGHTS]
    return tuple(flat)
```

```python
import math

import jax
import jax.numpy as jnp
from jax import lax
from jax.experimental import pallas as pl
from jax.experimental.pallas import tpu as pltpu

F32 = jnp.float32
BF = jnp.bfloat16

D = 1024
SEQ = 2048
N_META = 16
Q = 128
NPAD = 112
T = NPAD + N_META + SEQ
NCH = T // Q
RC = 544
D_FF = 2816
SSD_W = 1024
LRU_W = 1024
XBC = 1536
IN_COLS = 4624
PZ, PG, PXL, PXBC = 0, 1024, 2048, 3072
NP_IN = 4608
EPS = 1e-6
LRU_C = 8.0
VMEM_LIMIT = 56 * 1024 * 1024

ADAM_LR, ADAM_B1, ADAM_B2, ADAM_EPS, ADAM_WD, ADAM_STEP = 0.001, 0.9, 0.999, 1e-08, 0.01, 10

NT_DIMS = (((1,), (1,)), ((), ()))
TN_DIMS = (((0,), (0,)), ((), ()))
MESH = pl.DeviceIdType.MESH


def _params(n_grid=1, limit=VMEM_LIMIT):
    return pltpu.CompilerParams(dimension_semantics=("arbitrary",) * n_grid, vmem_limit_bytes=limit)


def _spec(shape, imap, single=False):
    if single:
        return pl.BlockSpec(shape, imap, pipeline_mode=pl.Buffered(1))
    return pl.BlockSpec(shape, imap)


def _sigmoid(x):
    return 0.5 * jnp.tanh(0.5 * x) + 0.5


def _sigmoid_gate(x):
    return 1.0 / (1.0 + jnp.exp(-x))


def _softplus(x):
    return jnp.maximum(x, 0.0) + jnp.log(1.0 + jnp.exp(-jnp.abs(x)))


def _rms_stats(h):
    return lax.rsqrt(jnp.mean(h * h, axis=-1, keepdims=True) + EPS)


def _rms(h, w):
    return (h * _rms_stats(h)) * w


def _rms_bwd(du, h, w):
    r = _rms_stats(h)
    n = h * r
    dn = du * w
    dh = r * (dn - n * jnp.mean(dn * n, axis=-1, keepdims=True))
    return dh, du * n


_G0 = math.sqrt(2.0 / math.pi)


def _gelu(x):
    return 0.5 * x * (1.0 + jnp.tanh(_G0 * (x + 0.044715 * (x * x * x))))


def _gelu_grad(x):
    t = jnp.tanh(_G0 * (x + 0.044715 * (x * x * x)))
    return 0.5 * (1.0 + t) + 0.5 * x * (1.0 - t * t) * (_G0 * (1.0 + 3.0 * 0.044715 * (x * x)))


def _rows(shape, r0=0):
    return lax.broadcasted_iota(jnp.int32, shape, 0) + r0


def _lanes(shape):
    return lax.broadcasted_iota(jnp.int32, shape, 1)


HALO = 8


def _fill_padded(pad_ref, x_ref):
    pad_ref[0:HALO, :] = jnp.zeros((HALO, pad_ref.shape[1]), F32)
    pad_ref[T + HALO:T + 2 * HALO, :] = jnp.zeros((HALO, pad_ref.shape[1]), F32)

    def step(c, carry):
        r0 = pl.multiple_of(c * Q, Q)
        pad_ref[pl.ds(r0 + HALO, Q), :] = x_ref[pl.ds(r0, Q), :].astype(F32)
        return carry

    lax.fori_loop(0, NCH, step, 0)


def _back(pad_ref, r0):
    win = pad_ref[pl.ds(r0, Q + HALO), :]
    return lambda s: win[HALO:, :] if s == 0 else pltpu.roll(win, s, axis=0)[HALO:, :]


def _ahead(pad_ref, r0):
    win = pad_ref[pl.ds(r0 + HALO, Q + HALO), :]
    return lambda s: win[:Q, :] if s == 0 else pltpu.roll(win, Q + HALO - s, axis=0)[:Q, :]


def _conv(back, w, b):
    y = b + w[3:4, :] * back(0)
    for k in range(3):
        y = y + w[k:k + 1, :] * back(3 - k)
    return y


def _conv_bwd_x(ahead, w):
    dx = w[3:4, :] * ahead(0)
    for k in range(3):
        dx = dx + w[k:k + 1, :] * ahead(3 - k)
    return dx


def _conv_bwd_w(dy, back):
    dws = [jnp.sum(dy * back(3 - k), axis=0, keepdims=True) for k in range(4)]
    return jnp.concatenate(dws, axis=0), jnp.sum(dy, axis=0, keepdims=True)


def _chunks(fn, unrolled=False):
    if unrolled:
        for c in range(NCH):
            fn(c * Q)
        return

    def step(c, carry):
        fn(pl.multiple_of(c * Q, Q))
        return carry

    lax.fori_loop(0, NCH, step, 0)


HALF = RC // 2


def _col_tiles(n, tn, fn):
    def step(j, carry):
        fn(pl.multiple_of(j * tn, tn))
        return carry

    lax.fori_loop(0, n // tn, step, 0)


def _rows_spec(cols, block_col=0):
    return _spec((RC, cols), lambda i: (i, block_col))


def _whole(shape):
    return _spec(shape, lambda i: tuple(0 for _ in shape), single=True)


def _vec(cols):
    return _spec((1, cols), lambda i: (0, 0))


def _zero_at_first(*refs):
    @pl.when(pl.program_id(0) == 0)
    def _():
        for r in refs:
            r[...] = jnp.zeros_like(r)


ANY_SPEC = pl.BlockSpec(memory_space=pl.ANY)


def _arriving(src, dst, sems, starts, rows):
    n, ahead = len(starts), 2
    first = pl.program_id(0) == 0

    def piece(k):
        r0 = starts[0]
        for j in range(1, n):
            r0 = jnp.where(k == j, starts[j], r0)
        at = pl.ds(pl.multiple_of(r0, 16), rows)
        return pltpu.make_async_copy(src.at[at], dst.at[at], sems.at[k])

    @pl.when(first)
    def _():
        for k in range(min(ahead, n)):
            piece(k).start()

    def ready(k):
        k = jnp.asarray(k, jnp.int32)

        @pl.when(first)
        def _():
            piece(k).wait()

            @pl.when(k + ahead < n)
            def _():
                piece(k + ahead).start()

    return ready


IN_RUNS = ((PZ, 0, 1024), (PXBC, 1024, XBC), (PG, 2576, 2048))
IN_TILE = 512
IN_TILE_ROWS = [wrow + IN_TILE * j for _, wrow, width in IN_RUNS for j in range(width // IN_TILE)]


def _in_tiles(fn, before_run=lambda run: None):
    done = 0
    for run, (pcol, wrow, width) in enumerate(IN_RUNS):
        before_run(run)
        def step(j, carry, pcol=pcol, wrow=wrow, done=done):
            fn(pl.multiple_of(pcol + j * IN_TILE, IN_TILE), pl.multiple_of(wrow + j * IN_TILE, 16), done + j)
            return carry

        lax.fori_loop(0, width // IN_TILE, step, 0)
        done += width // IN_TILE


def in_proj(x, meta, wn, w_shards):
    first = NPAD + N_META
    steps = T // RC
    shard = IN_COLS // 8

    def body(x_hbm, meta_ref, wn_ref, g_hbm, o_ref, dt_ref, u_ref, h_ref, wt_hbm, wdt_ref, raw, w_ref, h_scr, g_sems, h_sems, out_sem):
        i = pl.program_id(0)
        slot = i % 2
        shards = [pltpu.make_async_copy(g_hbm.at[j], raw.at[j], g_sems.at[j]) for j in range(8)]
        head = pltpu.make_async_copy(x_hbm.at[pl.ds(0, RC - first)], h_scr.at[0, pl.ds(first, RC - first)], h_sems.at[0])
        put_back = pltpu.make_async_copy(w_ref, wt_hbm, out_sem)

        def rows_of(step):
            return pltpu.make_async_copy(x_hbm.at[pl.ds(pl.multiple_of(step * RC - first, 32), RC)], h_scr.at[step % 2], h_sems.at[step % 2])

        @pl.when(i == 0)
        def _():
            for cp in shards:
                cp.start()
            head.start()
            h_scr[0, 0:NPAD, :] = jnp.zeros((NPAD, D), F32)
            for j in range(8):
                h_scr[0, NPAD:first, 128 * j:128 * j + 128] = meta_ref[j, 0:N_META, :]

        @pl.when(i + 1 < steps)
        def _():
            rows_of(i + 1).start()

        @pl.when(i == 0)
        def _():
            head.wait()

        @pl.when(i > 0)
        def _():
            rows_of(i).wait()

        h_ref[...] = h_scr[slot]
        for r in (0, HALF):
            u_ref[r:r + HALF, :] = _rms(h_scr[slot, r:r + HALF, :], wn_ref[...]).astype(BF)

        def place_shards(run):
            @pl.when(i == 0)
            def _():
                for j in ((0, 1), (2, 3, 4), (5, 6, 7))[run]:
                    shards[j].wait()
                    w_ref[shard * j:shard * (j + 1), :] = raw[j]
                if run == 1:
                    wdt_ref[...] = jnp.zeros_like(wdt_ref)
                    for g in range(2):
                        wdt_ref[128 * g:128 * g + 8, :] = w_ref[2560 + 8 * g:2568 + 8 * g, :]
                if run == 2:
                    put_back.start()

        def tile(pcol, wrow, k):
            o_ref[:, pl.ds(pcol, IN_TILE)] = lax.dot_general(u_ref[...], w_ref[pl.ds(wrow, IN_TILE), :], NT_DIMS,
                                                             preferred_element_type=F32).astype(BF)

        _in_tiles(tile, place_shards)
        dt_ref[...] = lax.dot_general(u_ref[...], wdt_ref[...], NT_DIMS, preferred_element_type=F32)

        @pl.when(i == steps - 1)
        def _():
            put_back.wait()

    return pl.pallas_call(
        body, grid=(steps,), in_specs=[ANY_SPEC, _spec(meta.shape, lambda i: (0, 0, 0)), _vec(D), ANY_SPEC],
        out_specs=[_rows_spec(NP_IN), _rows_spec(256), _rows_spec(D), _rows_spec(D), ANY_SPEC, _spec((256, D), lambda i: (0, 0))],
        out_shape=[jax.ShapeDtypeStruct((T, NP_IN), BF), jax.ShapeDtypeStruct((T, 256), F32), jax.ShapeDtypeStruct((T, D), BF),
                   jax.ShapeDtypeStruct((T, D), F32), jax.ShapeDtypeStruct((IN_COLS, D), BF), jax.ShapeDtypeStruct((256, D), BF)],
        scratch_shapes=[pltpu.VMEM((8, shard, D), BF), pltpu.VMEM((IN_COLS, D), BF), pltpu.VMEM((2, RC, D), F32),
                        pltpu.SemaphoreType.DMA((8,)), pltpu.SemaphoreType.DMA((2,)), pltpu.SemaphoreType.DMA],
        compiler_params=_params(), name="in_proj")(x, meta, wn, w_shards)


def out_proj(yn_ssd, proj, hseq, lru_nw, w_out, h0):
    def body(y_ref, g_ref, h_ref, wn_ref, w_ref, r_ref, o_ref, cat_ref):
        cat_ref[:, 0:SSD_W] = y_ref[...]
        for r in (0, HALF):
            y = _gelu(g_ref[r:r + HALF, :].astype(F32)) * h_ref[r:r + HALF, :]
            cat_ref[r:r + HALF, SSD_W:] = _rms(y, wn_ref[...]).astype(BF)

        def tile(c0):
            o_ref[:, pl.ds(c0, 512)] = r_ref[:, pl.ds(c0, 512)] + jnp.dot(cat_ref[...], w_ref[:, pl.ds(c0, 512)], preferred_element_type=F32)

        _col_tiles(D, 512, tile)

    return pl.pallas_call(
        body, grid=(T // RC,),
        in_specs=[_rows_spec(SSD_W), _rows_spec(LRU_W, PG // LRU_W), _rows_spec(LRU_W), _vec(LRU_W), _whole((SSD_W + LRU_W, D)), _rows_spec(D)],
        out_specs=[_rows_spec(D), _rows_spec(SSD_W + LRU_W)],
        out_shape=[jax.ShapeDtypeStruct((T, D), F32), jax.ShapeDtypeStruct((T, SSD_W + LRU_W), BF)],
        compiler_params=_params(), name="out_proj")(yn_ssd, proj, hseq, lru_nw, w_out, h0)


def out_proj_bwd(dh1_b, w_out, proj, hseq, lru_nw, after):
    def body(d_ref, w_ref, g_ref, h_ref, wn_ref, _after, dy_ref, dh_ref, dg_ref, dw_ref, dl_scr):
        _zero_at_first(dw_ref)

        def tile(c0):
            dy_ref[:, pl.ds(c0, 512)] = lax.dot_general(d_ref[...], w_ref[pl.ds(c0, 512), :], NT_DIMS, preferred_element_type=F32)
            dl_scr[:, pl.ds(c0, 512)] = lax.dot_general(d_ref[...], w_ref[pl.ds(SSD_W + c0, 512), :], NT_DIMS, preferred_element_type=F32)

        _col_tiles(SSD_W, 512, tile)

        for r in (0, HALF):
            g = g_ref[r:r + HALF, :].astype(F32)
            h = h_ref[r:r + HALF, :]
            ge = _gelu(g)
            dy, dw = _rms_bwd(dl_scr[r:r + HALF, :], ge * h, wn_ref[...])
            dw_ref[...] += jnp.sum(dw, axis=0, keepdims=True)
            dh_ref[r:r + HALF, :] = dy * ge
            dg_ref[r:r + HALF, :] = (dy * h * _gelu_grad(g)).astype(BF)

    return pl.pallas_call(
        body, grid=(T // RC,),
        in_specs=[_rows_spec(D), _whole((SSD_W + LRU_W, D)), _rows_spec(LRU_W, PG // LRU_W), _rows_spec(LRU_W), _vec(LRU_W), ANY_SPEC],
        out_specs=[_rows_spec(SSD_W), _rows_spec(LRU_W), _rows_spec(LRU_W), _vec(LRU_W)],
        out_shape=[jax.ShapeDtypeStruct((T, SSD_W), F32), jax.ShapeDtypeStruct((T, LRU_W), F32), jax.ShapeDtypeStruct((T, LRU_W), BF),
                   jax.ShapeDtypeStruct((1, LRU_W), F32)],
        scratch_shapes=[pltpu.VMEM((RC, LRU_W), F32)],
        compiler_params=_params(), name="out_proj_bwd")(dh1_b, w_out, proj, hseq, lru_nw, after)


def in_proj_bwd(dz, dg, dxl, dxbc, ddt, w_t, w_dt, h0, wn, dh1, after, through):
    first = NPAD + N_META

    def body(dz_ref, dg_ref, dxl_ref, dxbc_ref, ddt_ref, w_hbm, wdt_ref, h_ref, wn_ref, r_ref, _after, _in0, _in1, gx_hbm, meta_ref, dw_ref,
             _out0, _out1, du_scr, o_ref, sem, w_ref, w_sems):
        i = pl.program_id(0)
        ready = _arriving(w_hbm, w_ref, w_sems, IN_TILE_ROWS, IN_TILE)
        _zero_at_first(dw_ref)
        du_scr[...] = jnp.dot(ddt_ref[...], wdt_ref[...], preferred_element_type=F32)
        done = 0
        for d_ref, wrow, width in ((dz_ref, 0, 1024), (dxbc_ref, 1024, XBC), (dg_ref, 2576, 1024), (dxl_ref, 3600, 1024)):
            def step(j, carry, d_ref=d_ref, wrow=wrow, done=done):
                c0 = pl.multiple_of(j * IN_TILE, IN_TILE)
                ready(done + j)
                du_scr[...] += jnp.dot(d_ref[:, pl.ds(c0, IN_TILE)], w_ref[pl.ds(pl.multiple_of(wrow + c0, 16), IN_TILE), :],
                                       preferred_element_type=F32)
                return carry

            lax.fori_loop(0, width // IN_TILE, step, 0)
            done += width // IN_TILE
        for r in (0, HALF):
            dh, dw = _rms_bwd(du_scr[r:r + HALF, :], h_ref[r:r + HALF, :], wn_ref[...])
            dw_ref[...] += jnp.sum(dw, axis=0, keepdims=True)
            o_ref[r:r + HALF, :] = dh + r_ref[r:r + HALF, :]

        @pl.when(i == 0)
        def _():
            meta_ref[...] = o_ref[NPAD:first, :]
            head = pltpu.make_async_copy(o_ref.at[pl.ds(first, RC - first)], gx_hbm.at[pl.ds(0, RC - first)], sem)
            head.start()
            head.wait()

        @pl.when(i > 0)
        def _():
            rest = pltpu.make_async_copy(o_ref, gx_hbm.at[pl.ds(pl.multiple_of(i * RC - first, 32), RC)], sem)
            rest.start()
            rest.wait()

    return pl.pallas_call(
        body, grid=(T // RC,),
        in_specs=[_rows_spec(SSD_W), _rows_spec(LRU_W), _rows_spec(LRU_W), _rows_spec(XBC), _rows_spec(256), ANY_SPEC,
                  _whole((256, D)), _rows_spec(D), _vec(D), _rows_spec(D), ANY_SPEC, ANY_SPEC, ANY_SPEC],
        out_specs=[ANY_SPEC, _spec((N_META, D), lambda i: (0, 0)), _vec(D), ANY_SPEC, ANY_SPEC],
        out_shape=[jax.ShapeDtypeStruct((SEQ, D), F32), jax.ShapeDtypeStruct((N_META, D), F32), jax.ShapeDtypeStruct((1, D), F32)]
        + [jax.ShapeDtypeStruct(t.shape, t.dtype) for t in through],
        scratch_shapes=[pltpu.VMEM((RC, D), F32), pltpu.VMEM((RC, D), F32), pltpu.SemaphoreType.DMA,
                        pltpu.VMEM((IN_COLS, D), BF), pltpu.SemaphoreType.DMA((len(IN_TILE_ROWS),))],
        input_output_aliases={11: 3, 12: 4},
        compiler_params=_params(), name="in_proj_bwd")(dz, dg, dxl, dxbc, ddt, w_t, w_dt, h0, wn, dh1, after, *through)


GRAD_TILE = 256


def weight_grad(name, a, u1):
    tm = GRAD_TILE

    def body(a_ref, u_ref, o_ref):
        o_ref[...] = lax.dot_general(a_ref[...], u_ref[...], TN_DIMS, preferred_element_type=F32).astype(BF)

    return pl.pallas_call(
        body, grid=(a.shape[1] // tm,),
        in_specs=[_spec((T, tm), lambda j: (0, j)), _spec((T, D), lambda j: (0, 0), single=True)],
        out_specs=_spec((tm, D), lambda j: (j, 0)),
        out_shape=jax.ShapeDtypeStruct((a.shape[1], D), BF),
        compiler_params=_params(), name=name)(a, u1)


def in_weight_grad(parts, first_rows, ddt, u1):
    tm = GRAD_TILE
    per_row = D // 128
    dt_row, dt_lines = 2560, 8 * per_row
    parts = list(parts) + [ddt]
    first_rows = list(first_rows) + [dt_row]
    tiles = [p.shape[1] // tm for p in parts]
    starts = [sum(tiles[:k]) for k in range(len(parts))]
    last = sum(tiles) - 1

    def body(*refs):
        a_refs, u_ref = refs[:len(parts)], refs[len(parts)]
        o_hbm, mix_scr, stage, sems = refs[len(parts) + 1:]
        step = pl.program_id(0)
        slot = step % 2
        line0 = 0
        for a_ref, start, n, first in zip(a_refs, starts, tiles, first_rows):
            here = (step >= start) & (step < start + n)
            line0 = jnp.where(here, per_row * (first + tm * (step - start)), line0)

            @pl.when(here)
            def _(a_ref=a_ref):
                res = lax.dot_general(a_ref[...], u_ref[...], TN_DIMS, preferred_element_type=F32)
                for q in range(per_row):
                    mix_scr[pl.ds(q, tm, stride=per_row), :] = res[:, 128 * q:128 * q + 128]

        def tile_copy(of_slot, to):
            return pltpu.make_async_copy(stage.at[of_slot], o_hbm.at[pl.ds(to, per_row * tm)], sems.at[of_slot])

        @pl.when(step >= 2)
        def _():
            tile_copy(slot, 0).wait()

        stage[slot] = mix_scr[...].astype(BF)

        @pl.when(step < last)
        def _():
            tile_copy(slot, pl.multiple_of(line0, 128)).start()

        @pl.when(step == last)
        def _():
            halves = [pltpu.make_async_copy(stage.at[slot, pl.ds(128 * per_row * k, dt_lines)],
                                            o_hbm.at[pl.ds(per_row * (dt_row + 8 * k), dt_lines)], sems.at[2 + k]) for k in range(2)]
            for cp in halves:
                cp.start()
            tile_copy(1 - slot, 0).wait()
            for cp in halves:
                cp.wait()

    def tile_of(start, n):
        return lambda j: (0, jnp.clip(j - start, 0, n - 1))

    return pl.pallas_call(
        body, grid=(last + 1,),
        in_specs=[_spec((T, tm), tile_of(s, n)) for s, n in zip(starts, tiles)] + [_spec((T, D), lambda j: (0, 0), single=True)],
        out_specs=ANY_SPEC,
        out_shape=jax.ShapeDtypeStruct((per_row * IN_COLS, 128), BF),
        scratch_shapes=[pltpu.VMEM((per_row * tm, 128), F32), pltpu.VMEM((2, per_row * tm, 128), BF), pltpu.SemaphoreType.DMA((4,))],
        compiler_params=_params(), name="dw_in")(*parts, u1)


def _ssd_chunk_common(row0, dt_ref, b_ref, c_ref, bias, a_neg):
    shape = (Q, Q)
    lane = _lanes(shape)
    sub = _rows(shape)
    live = (_rows(shape, row0) >= NPAD) & (lane < 8)
    dtr = dt_ref[:, :]
    dt = jnp.where(live, _softplus(dtr + bias), 0.0)
    d_a = dt * a_neg
    tri = (sub >= lane).astype(F32)
    cs = jnp.dot(tri, d_a, precision=lax.Precision.HIGHEST, preferred_element_type=F32)
    cs_t = cs.T
    b_f = b_ref[:, :]
    bc = b_f.astype(BF)
    cc = c_ref[:, :].astype(BF)
    cb = lax.dot_general(cc, bc, NT_DIMS, preferred_element_type=F32)
    cs_last = cs[Q - 1:Q, :]
    return dict(lane=lane, sub=sub, live=live, dtr=dtr, dt=dt, cs=cs, cs_t=cs_t, bc=bc, cc=cc, cb=cb, bc_t=b_f.T.astype(BF),
                ecs=jnp.exp(cs), dsm=jnp.exp(cs_last - cs), gam=jnp.exp(cs_last))


def _pair(lane_even, mat, j):
    return jnp.where(lane_even, mat[:, j:j + 1], mat[:, j + 1:j + 2])


def _pair_row(lane_even, mat, j):
    return jnp.where(lane_even[0:1, :], mat[:, j:j + 1], mat[:, j + 1:j + 2])


def _head_decay(cm, j):
    seg = cm["cs"][:, j:j + 1] - cm["cs_t"][j:j + 1, :]
    return jnp.exp(jnp.where(cm["sub"] >= cm["lane"], seg, -jnp.inf))


def _head_decay_t(cm, j):
    seg = cm["cs_t"][j:j + 1, :] - cm["cs"][:, j:j + 1]
    return jnp.exp(jnp.where(cm["lane"] >= cm["sub"], seg, -jnp.inf))


def _conv_window(raw_ref, halo_ref, pad_scr):
    pad_scr[0:HALO, :] = halo_ref[...].astype(F32)[halo_ref.shape[0] - HALO:, :]
    pad_scr[HALO:HALO + Q, :] = raw_ref[...].astype(F32)
    win = pad_scr[...]
    return lambda s: win[HALO:, :] if s == 0 else pltpu.roll(win, s, axis=0)[HALO:, :]


def _xbc_cols(g):
    return slice(512 * g, 512 * g + 512), slice(SSD_W + 128 * g, SSD_W + 128 * g + 128), slice(SSD_W + 256 + 128 * g, SSD_W + 384 + 128 * g)


def ssd_fwd(proj, dt_raw, conv_w, conv_b, heads, norm_w):
    def body(raw_ref, halo_ref, dt_all, z_all, cw_ref, cb_ref, bias_all, alog_all, d_all, nw_all, yn_all, y_all, hp_all,
             h_all, pad_scr, act_scr):
        @pl.when(pl.program_id(0) == 0)
        def _():
            h_all[...] = jnp.zeros_like(h_all)

        pre = _conv(_conv_window(raw_ref, halo_ref, pad_scr), cw_ref[...], cb_ref[...])
        act_scr[...] = pre * _sigmoid(pre)
        for g in range(2):
            wide, thin = slice(512 * g, 512 * g + 512), slice(128 * g, 128 * g + 128)
            xs, bs, cs = _xbc_cols(g)
            group(act_scr.at[:, xs], act_scr.at[:, bs], act_scr.at[:, cs], dt_all.at[:, thin], z_all.at[:, wide], bias_all.at[g],
                  alog_all.at[g], d_all.at[g], nw_all.at[:, wide], yn_all.at[:, wide], y_all.at[:, wide], hp_all.at[g, 0], h_all.at[g])

    def group(x_ref, b_ref, c_ref, dt_ref, z_ref, bias_ref, alog_ref, d_ref, nw_ref, yn_ref, y_ref, hp_ref, h_scr):
        bias = bias_ref[...]
        a_neg = -jnp.exp(alog_ref[...])
        dsk = d_ref[...]
        cm = _ssd_chunk_common(pl.program_id(0) * Q, dt_ref, b_ref, c_ref, bias, a_neg)
        lane_even = cm["lane"] < 64
        for p in range(4):
            je, jo = 2 * p, 2 * p + 1
            xp = x_ref[:, 128 * p:128 * p + 128]
            xdt = xp * _pair(lane_even, cm["dt"], je)
            xdt_b = xdt.astype(BF)
            m_e = (cm["cb"] * _head_decay(cm, je)).astype(BF)
            m_o = (cm["cb"] * _head_decay(cm, jo)).astype(BF)
            zero = jnp.zeros_like(xdt_b)
            yd = (jnp.dot(m_e, jnp.where(lane_even, xdt_b, zero), preferred_element_type=F32)
                  + jnp.dot(m_o, jnp.where(lane_even, zero, xdt_b), preferred_element_type=F32))
            hp = h_scr[p]
            hp_ref[p] = hp
            yo = jnp.dot(cm["cc"], hp.astype(BF), preferred_element_type=F32) * _pair(lane_even, cm["ecs"], je)
            y_ref[:, 128 * p:128 * p + 128] = yd + yo + xp * _pair_row(lane_even, dsk, je)
            st = jnp.dot(cm["bc_t"], (xdt * _pair(lane_even, cm["dsm"], je)).astype(BF), preferred_element_type=F32)
            h_scr[p] = hp * _pair_row(lane_even, cm["gam"], je) + st
        zc = z_ref[:, :].astype(F32)
        gated = y_ref[:, :] * (zc * _sigmoid(zc))
        yn_ref[:, :] = _rms(gated, nw_ref[...]).astype(BF)

    par = [_spec((None, 2, 1, 128), lambda c, k=k: (k, 0, 0, 0)) for k in range(3)]
    wide = _spec((Q, SSD_W), lambda c: (c, 0))
    xbc = PXBC // XBC
    halo = 2 * HALO
    return pl.pallas_call(
        body, grid=(NCH,),
        in_specs=[_spec((Q, XBC), lambda c: (c, xbc)), _spec((halo, XBC), lambda c: (jnp.maximum(c * (Q // halo) - 1, 0), xbc)),
                  _spec((Q, 256), lambda c: (c, 0)), wide, _spec((4, XBC), lambda c: (0, 0)), _spec((1, XBC), lambda c: (0, 0)),
                  *par, _spec((1, SSD_W), lambda c: (0, 0))],
        out_specs=[wide, wide, _spec((2, 1, 4, 128, 128), lambda c: (0, c, 0, 0, 0))],
        out_shape=[jax.ShapeDtypeStruct((T, SSD_W), BF), jax.ShapeDtypeStruct((T, SSD_W), F32),
                   jax.ShapeDtypeStruct((2, NCH, 4, 128, 128), F32)],
        scratch_shapes=[pltpu.VMEM((2, 4, 128, 128), F32), pltpu.VMEM((Q + HALO, XBC), F32), pltpu.VMEM((Q, XBC), F32)],
        compiler_params=_params(), name="ssd_fwd")(proj, proj, dt_raw, proj, conv_w, conv_b, heads, heads, heads, norm_w)


def ssd_bwd(dyn, proj, dt_raw, conv_w, conv_b, y_pre, h_prev, heads, norm_w, after):
    def body(dyn_all, raw_ref, halo_ref, dt_all, z_all, y_all, hp_all, cw_ref, cb_ref, bias_all, alog_all, d_all, nw_all, _after,
             dz_all, dxbc_ref, ddt_all, dpar_all, dnw_all, dcw_ref, dcb_ref, dh_all, acc_all, pad_scr, act_scr, dsilu_scr, dact_scr, dpad_scr, sums_scr):
        @pl.when(pl.program_id(0) == 0)
        def _():
            dh_all[...] = jnp.zeros_like(dh_all)
            acc_all[...] = jnp.zeros_like(acc_all)
            dnw_all[...] = jnp.zeros_like(dnw_all)
            dcw_ref[...] = jnp.zeros_like(dcw_ref)
            dcb_ref[...] = jnp.zeros_like(dcb_ref)
            dpad_scr[Q:Q + HALO, :] = jnp.zeros((HALO, XBC), F32)

        back = _conv_window(raw_ref, halo_ref, pad_scr)
        pre = _conv(back, cw_ref[...], cb_ref[...])
        sg = _sigmoid(pre)
        act_scr[...] = pre * sg
        dsilu_scr[...] = sg * (1.0 + pre * (1.0 - sg))
        for g in range(2):
            wide, thin = slice(512 * g, 512 * g + 512), slice(128 * g, 128 * g + 128)
            xs, bs, cs = _xbc_cols(g)
            group(dyn_all.at[:, wide], act_scr.at[:, xs], act_scr.at[:, bs], act_scr.at[:, cs], dt_all.at[:, thin], z_all.at[:, wide],
                  y_all.at[:, wide], hp_all.at[g, 0], bias_all.at[g], alog_all.at[g], d_all.at[g], nw_all.at[:, wide],
                  dz_all.at[:, wide], dact_scr.at[:, xs], dact_scr.at[:, bs], dact_scr.at[:, cs], ddt_all.at[:, thin], dpar_all.at[g],
                  dnw_all.at[:, wide], dh_all.at[g], acc_all.at[g], sums_scr)
        dpre = dact_scr[...] * dsilu_scr[...]
        dcw, dcb = _conv_bwd_w(dpre, back)
        dcw_ref[...] += dcw
        dcb_ref[...] += dcb
        dpad_scr[0:Q, :] = dpre
        win = dpad_scr[...]
        dxbc_ref[...] = _conv_bwd_x(lambda s: win[:Q, :] if s == 0 else pltpu.roll(win, Q + HALO - s, axis=0)[:Q, :], cw_ref[...]).astype(BF)
        dpad_scr[Q:Q + HALO, :] = dpre[0:HALO, :]

    def group(dyn_ref, x_ref, b_ref, c_ref, dt_ref, z_ref, y_ref, hp_ref, bias_ref, alog_ref, d_ref, nw_ref,
              dz_ref, dx_ref, db_ref, dc_ref, ddt_ref, dpar_ref, dnw_ref, dh_scr, acc_scr, sums):
        ci = pl.program_id(0)
        bias = bias_ref[...]
        a_neg = -jnp.exp(alog_ref[...])
        dsk = d_ref[...]
        cm = _ssd_chunk_common((NCH - 1 - ci) * Q, dt_ref, b_ref, c_ref, bias, a_neg)
        lane, sub = cm["lane"], cm["sub"]
        lane_even = lane < 64
        cc_t = c_ref[:, :].T.astype(BF)
        cb_t = lax.dot_general(cm["bc"], cm["cc"], NT_DIMS, preferred_element_type=F32)
        zc = z_ref[:, :].astype(F32)
        yc = y_ref[:, :]
        sg = _sigmoid(zc)
        sz = zc * sg
        dgated, dnw = _rms_bwd(dyn_ref[:, :], yc * sz, nw_ref[...])
        dnw_ref[...] += jnp.sum(dnw, axis=0, keepdims=True)
        dz_ref[:, :] = (dgated * yc * (sg * (1.0 + zc * (1.0 - sg)))).astype(BF)
        dx_ref[:, :] = dgated * sz
        db_ref[:, :] = jnp.zeros((Q, Q), F32)
        dc_ref[:, :] = jnp.zeros((Q, Q), F32)
        sums[...] = jnp.zeros_like(sums)
        for p in range(4):
            je, jo = 2 * p, 2 * p + 1
            xp = x_ref[:, 128 * p:128 * p + 128]
            dy = dx_ref[:, 128 * p:128 * p + 128]
            dt_p = _pair(lane_even, cm["dt"], je)
            xdt = xp * dt_p
            xdt_b = xdt.astype(BF)
            dy_b = dy.astype(BF)
            zero = jnp.zeros_like(dy_b)
            hp = hp_ref[p]
            hp_b = hp.astype(BF)
            dh = dh_scr[p]
            dh_b = dh.astype(BF)
            acc_scr[p:p + 1, :] += jnp.sum(dy * xp, axis=0, keepdims=True)
            dxp = dy * _pair_row(lane_even, dsk, je)
            e_p = _pair(lane_even, cm["ecs"], je)
            g_p = jnp.dot(cm["cc"], hp_b, preferred_element_type=F32)
            dg_b = (dy * e_p).astype(BF)
            de = dy * g_p * e_p
            dc_ref[:, :] += lax.dot_general(dg_b, hp_b, NT_DIMS, preferred_element_type=F32)
            dh_in = jnp.dot(cc_t, dg_b, preferred_element_type=F32)
            ds_p = _pair(lane_even, cm["dsm"], je)
            r_p = jnp.dot(cm["bc"], dh_b, preferred_element_type=F32)
            dxdt = r_p * ds_p
            tt = r_p * xdt * ds_p
            db_ref[:, :] += lax.dot_general((xdt * ds_p).astype(BF), dh_b, NT_DIMS, preferred_element_type=F32)
            dgam_m = jnp.sum(dh * hp, axis=0, keepdims=True)
            for j, even in ((je, True), (jo, False)):
                sel = lane_even if even else jnp.logical_not(lane_even)
                dy_j = jnp.where(sel, dy_b, zero)
                l_j = _head_decay(cm, j)
                l_jt = _head_decay_t(cm, j)
                m_j = cm["cb"] * l_j
                m_jt = cb_t * l_jt
                dm = lax.dot_general(dy_j, xdt_b, NT_DIMS, preferred_element_type=F32)
                dm_t = lax.dot_general(xdt_b, dy_j, NT_DIMS, preferred_element_type=F32)
                dxdt = dxdt + jnp.dot(m_jt.astype(BF), dy_j, preferred_element_type=F32)
                sums[0] += dm * l_j
                sums[1] += dm_t * l_jt
                t_j = jnp.where(sel, tt, 0.0)
                col = jnp.sum(dm * m_j - dm_t * m_jt + (jnp.where(sel, de, 0.0) - t_j), axis=1, keepdims=True)
                gam_j = cm["gam"][:, j:j + 1]
                last = (jnp.sum(jnp.sum(t_j, axis=0, keepdims=True), axis=1, keepdims=True)
                        + jnp.sum(jnp.where(sel[0:1, :], dgam_m, 0.0), axis=1, keepdims=True) * gam_j)
                col = col + jnp.where(sub[:, 0:1] == Q - 1, last, 0.0)
                sums[2] += jnp.where(lane == j, col, 0.0)
            dh_scr[p] = dh_in + dh * _pair_row(lane_even, cm["gam"], je)
            dx_ref[:, 128 * p:128 * p + 128] = dxp + dxdt * dt_p
            dd = dxdt * xp
            sums[3] += (jnp.where(lane == je, jnp.sum(jnp.where(lane_even, dd, 0.0), axis=1, keepdims=True), 0.0)
                        + jnp.where(lane == jo, jnp.sum(jnp.where(lane_even, 0.0, dd), axis=1, keepdims=True), 0.0))
        dc_ref[:, :] += jnp.dot(sums[0].astype(BF), cm["bc"], preferred_element_type=F32)
        db_ref[:, :] += jnp.dot(sums[1].astype(BF), cm["cc"], preferred_element_type=F32)
        tri_t = (sub <= lane).astype(F32)
        dd_a = jnp.dot(tri_t, sums[2], precision=lax.Precision.HIGHEST, preferred_element_type=F32)
        ddt = sums[3] + dd_a * a_neg
        acc_scr[5:6, :] += jnp.sum(dd_a * cm["dt"], axis=0, keepdims=True)
        draw = jnp.where(cm["live"], ddt * _sigmoid_gate(cm["dtr"] + bias), 0.0)
        acc_scr[4:5, :] += jnp.sum(draw, axis=0, keepdims=True)
        ddt_ref[:, :] = draw.astype(BF)

        @pl.when(ci == NCH - 1)
        def _():
            lane1 = _lanes((1, 128))
            dd = jnp.zeros((1, 128), F32)
            for p in range(4):
                row = acc_scr[p:p + 1, :]
                dd = dd + jnp.where(lane1 == 2 * p, jnp.sum(jnp.where(lane1 < 64, row, 0.0), axis=1, keepdims=True), 0.0)
                dd = dd + jnp.where(lane1 == 2 * p + 1, jnp.sum(jnp.where(lane1 < 64, 0.0, row), axis=1, keepdims=True), 0.0)
            dpar_ref[...] = jnp.concatenate([acc_scr[4:5, :], acc_scr[5:6, :] * a_neg, dd, jnp.zeros((5, 128), F32)], axis=0)

    par = [_spec((None, 2, 1, 128), lambda c, k=k: (k, 0, 0, 0)) for k in range(3)]
    wide = _spec((Q, SSD_W), lambda c: (NCH - 1 - c, 0))
    thin = _spec((Q, 256), lambda c: (NCH - 1 - c, 0))
    vec = _spec((1, SSD_W), lambda c: (0, 0))
    xbc = PXBC // XBC
    halo = 2 * HALO
    chunk = pltpu.VMEM((Q, XBC), F32)
    padded = pltpu.VMEM((Q + HALO, XBC), F32)
    return pl.pallas_call(
        body, grid=(NCH,),
        in_specs=[wide, _spec((Q, XBC), lambda c: (NCH - 1 - c, xbc)),
                  _spec((halo, XBC), lambda c: (jnp.maximum((NCH - 1 - c) * (Q // halo) - 1, 0), xbc)), thin, wide, wide,
                  _spec((2, 1, 4, 128, 128), lambda c: (0, NCH - 1 - c, 0, 0, 0)), _spec((4, XBC), lambda c: (0, 0)),
                  _spec((1, XBC), lambda c: (0, 0)), *par, vec, ANY_SPEC],
        out_specs=[wide, _spec((Q, XBC), lambda c: (NCH - 1 - c, 0)), thin, _spec((2, 8, 128), lambda c: (0, 0, 0)), vec,
                   _spec((4, XBC), lambda c: (0, 0)), _spec((1, XBC), lambda c: (0, 0))],
        out_shape=[jax.ShapeDtypeStruct((T, SSD_W), BF), jax.ShapeDtypeStruct((T, XBC), BF), jax.ShapeDtypeStruct((T, 256), BF),
                   jax.ShapeDtypeStruct((2, 8, 128), F32), jax.ShapeDtypeStruct((1, SSD_W), F32), jax.ShapeDtypeStruct((4, XBC), F32),
                   jax.ShapeDtypeStruct((1, XBC), F32)],
        scratch_shapes=[pltpu.VMEM((2, 4, 128, 128), F32), pltpu.VMEM((2, 8, 128), F32), padded, chunk, chunk, chunk, padded,
                        pltpu.VMEM((4, Q, Q), F32)],
        compiler_params=_params(), name="ssd_bwd")(dyn, proj, proj, dt_raw, proj, y_pre, h_prev, conv_w, conv_b, heads, heads, heads, norm_w, after)


def _lru_gates(back, cw, cb, wa, ba, wx, bx, lam):
    xr = _conv(back, cw, cb)
    xr_b = xr.astype(BF)
    r = _sigmoid_gate(jnp.dot(xr_b, wa, preferred_element_type=F32) + ba)
    i = _sigmoid_gate(jnp.dot(xr_b, wx, preferred_element_type=F32) + bx)
    sp = _softplus(-lam)
    la = (-LRU_C) * r * sp
    a = jnp.exp(la)
    mult2 = -jnp.tanh(la) * (a * a + 1.0)
    return xr, xr_b, r, i, sp, a, jnp.sqrt(mult2), mult2


SEG_LEN = 68
SEGS = T // SEG_LEN


def _seg_rows(j, k, off=0):
    return pl.ds(off + j * 8 * SEG_LEN + k, 8, stride=SEG_LEN)


def _segmented_scan(mul_ref, mul_row0, add_ref, out_ref, loc_scr, prod_scr, carry_scr, reverse):
    groups = SEGS // 8
    off = mul_row0 + (1 if reverse else 0)

    def local(i, carry):
        k = SEG_LEN - 1 - i if reverse else i
        new = []
        for j in range(groups):
            h, p = carry[2 * j], carry[2 * j + 1]
            m = mul_ref[_seg_rows(j, k, off), :]
            h = m * h + add_ref[_seg_rows(j, k), :]
            p = m * p
            loc_scr[_seg_rows(j, k), :] = h
            prod_scr[_seg_rows(j, k), :] = p
            new += [h, p]
        return tuple(new)

    lax.fori_loop(0, SEG_LEN, local, (jnp.zeros((8, 128), F32), jnp.ones((8, 128), F32)) * groups)

    def chain(i, c):
        s = SEGS - 1 - i if reverse else i
        carry_scr[pl.ds(s, 1), :] = c
        edge = s * SEG_LEN + (0 if reverse else SEG_LEN - 1)
        return loc_scr[pl.ds(edge, 1), :] + prod_scr[pl.ds(edge, 1), :] * c

    lax.fori_loop(0, SEGS, chain, jnp.zeros((1, 128), F32))

    def fold(k, carry):
        for j in range(groups):
            rows = _seg_rows(j, k)
            out_ref[rows, :] = loc_scr[rows, :] + prod_scr[rows, :] * carry_scr[8 * j:8 * j + 8, :]
        return carry

    lax.fori_loop(0, SEG_LEN, fold, 0)


LRU_CW_SPEC = _spec((None, 8, 128), lambda c: (c, 3, 0))


def lru_fwd(proj, cw, cb, wa2, ba, wx2, bx, lam, after):
    def body(x_ref, cw_ref, cb_ref, wa_ref, ba_ref, wx_ref, bx_ref, lam_ref, _after, h_ref, a_ref, xpad, u_scr, loc_scr, prod_scr, carry_scr):
        _fill_padded(xpad, x_ref)

        def chunk(r0):
            xr, _, _, i, _, a, mult, _ = _lru_gates(_back(xpad, r0), cw_ref[0:4, :], cb_ref[...], wa_ref[0], ba_ref[...], wx_ref[0], bx_ref[...],
                                                 lam_ref[...])
            a_ref[pl.ds(r0, Q), :] = a
            u_scr[pl.ds(r0, Q), :] = jnp.where(_rows(a.shape, r0) >= NPAD, mult * (i * xr), 0.0)

        _chunks(chunk, unrolled=True)
        _segmented_scan(a_ref, 0, u_scr, h_ref, loc_scr, prod_scr, carry_scr, reverse=False)

    c0 = PXL // 128
    vec = _spec((1, 128), lambda c: (0, c))
    mat = _spec((1, 128, 128), lambda c: (c, 0, 0))
    seq = pltpu.VMEM((T, 128), F32)
    return pl.pallas_call(
        body, grid=(8,),
        in_specs=[_spec((T, 128), lambda c: (0, c0 + c)), LRU_CW_SPEC, vec, mat, vec, mat, vec, vec, ANY_SPEC],
        out_specs=[_spec((T, 128), lambda c: (0, c)), _spec((T, 128), lambda c: (0, c))],
        out_shape=[jax.ShapeDtypeStruct((T, LRU_W), F32), jax.ShapeDtypeStruct((T, LRU_W), F32)],
        scratch_shapes=[pltpu.VMEM((T + 2 * HALO, 128), F32), seq, seq, seq, pltpu.VMEM((SEGS, 128), F32)],
        compiler_params=_params(), name="lru_fwd")(proj, cw, cb, wa2, ba, wx2, bx, lam, after)


def lru_bwd(dh_out, a, hseq, proj, cw, cb, wa2, ba, wx2, bx, lam):
    def body(d_ref, a_ref, h_ref, x_ref, cw_ref, cb_ref, wa_ref, ba_ref, wx_ref, bx_ref, lam_ref,
             dx_ref, dcw_ref, dcb_ref, dwa_ref, dba_ref, dwx_ref, dbx_ref, dlam_ref, xpad, hpad, dpad, dh_ref, loc_scr, prod_scr, carry_scr):
        _fill_padded(dpad, a_ref)
        _segmented_scan(dpad, HALO, d_ref, dh_ref, loc_scr, prod_scr, carry_scr, reverse=True)
        _fill_padded(xpad, x_ref)
        _fill_padded(hpad, h_ref)
        dpad[0:HALO, :] = jnp.zeros((HALO, 128), F32)
        dpad[T + HALO:T + 2 * HALO, :] = jnp.zeros((HALO, 128), F32)
        for ref in (dcw_ref, dcb_ref, dwa_ref, dba_ref, dwx_ref, dbx_ref, dlam_ref):
            ref[...] = jnp.zeros_like(ref)
        lam = lam_ref[...]

        def first(r0):
            back = _back(xpad, r0)
            xr, xr_b, r, i, sp, a, mult, mult2 = _lru_gates(back, cw_ref[0:4, :], cb_ref[...], wa_ref[0], ba_ref[...], wx_ref[0], bx_ref[...], lam)
            dh = dh_ref[pl.ds(r0, Q), :]
            da = dh * _back(hpad, r0)(1)
            du = jnp.where(_rows(dh.shape, r0) >= NPAD, dh, 0.0)
            dmult = du * (i * xr)
            di = du * (mult * xr)
            dxr = du * (mult * i)
            dla = da * a - dmult * (a * a) * lax.rsqrt(mult2)
            dr = dla * ((-LRU_C) * sp)
            dlam_ref[...] += jnp.sum(dla * ((-LRU_C) * r), axis=0, keepdims=True)
            dpr = dr * r * (1.0 - r)
            dpi = di * i * (1.0 - i)
            dba_ref[...] += jnp.sum(dpr, axis=0, keepdims=True)
            dbx_ref[...] += jnp.sum(dpi, axis=0, keepdims=True)
            dpr_b = dpr.astype(BF)
            dpi_b = dpi.astype(BF)
            dxr = (dxr + lax.dot_general(dpr_b, wa_ref[0], NT_DIMS, preferred_element_type=F32)
                   + lax.dot_general(dpi_b, wx_ref[0], NT_DIMS, preferred_element_type=F32))
            dwa_ref[0] += lax.dot_general(xr_b, dpr_b, TN_DIMS, preferred_element_type=F32)
            dwx_ref[0] += lax.dot_general(xr_b, dpi_b, TN_DIMS, preferred_element_type=F32)
            dpad[pl.ds(r0 + HALO, Q), :] = dxr
            dcw, dcb = _conv_bwd_w(dxr, back)
            dcw_ref[...] += dcw
            dcb_ref[...] += dcb

        _chunks(first, unrolled=True)
        dlam_ref[...] = -dlam_ref[...] * _sigmoid_gate(-lam)

        def second(r0):
            dx_ref[pl.ds(r0, Q), :] = _conv_bwd_x(_ahead(dpad, r0), cw_ref[0:4, :]).astype(BF)

        _chunks(second)

    c0 = PXL // 128
    vec = _spec((1, 128), lambda c: (0, c))
    mat = _spec((1, 128, 128), lambda c: (c, 0, 0))
    col = _spec((T, 128), lambda c: (0, c))
    vshape = jax.ShapeDtypeStruct((1, LRU_W), F32)
    mshape = jax.ShapeDtypeStruct((8, 128, 128), F32)
    pad = pltpu.VMEM((T + 2 * HALO, 128), F32)
    seq = pltpu.VMEM((T, 128), F32)
    return pl.pallas_call(
        body, grid=(8,),
        in_specs=[col, col, col, _spec((T, 128), lambda c: (0, c0 + c)), LRU_CW_SPEC, vec, mat, vec, mat, vec, vec],
        out_specs=[col, _spec((4, 128), lambda c: (0, c)), vec, mat, vec, mat, vec, vec],
        out_shape=[jax.ShapeDtypeStruct((T, LRU_W), BF), jax.ShapeDtypeStruct((4, LRU_W), F32), vshape, mshape, vshape, mshape, vshape, vshape],
        scratch_shapes=[pad, pad, pad, seq, seq, seq, pltpu.VMEM((SEGS, 128), F32)],
        compiler_params=_params(), name="lru_bwd")(dh_out, a, hseq, proj, cw, cb, wa2, ba, wx2, bx, lam)


FF_TILE = 256
FF_TILE_ROWS = list(range(0, D_FF, FF_TILE))


def gate_up(h1, wn, w_gate, w_up):
    def body(h_ref, wn_ref, wg_hbm, wu_hbm, gt_ref, up_ref, act_ref, u_ref, wg_ref, wu_ref, wg_sems, wu_sems):
        gate_ready = _arriving(wg_hbm, wg_ref, wg_sems, FF_TILE_ROWS, FF_TILE)
        up_ready = _arriving(wu_hbm, wu_ref, wu_sems, FF_TILE_ROWS, FF_TILE)
        for r in (0, HALF):
            u_ref[r:r + HALF, :] = _rms(h_ref[r:r + HALF, :], wn_ref[...]).astype(BF)

        def tile(c0):
            cols = pl.ds(c0, FF_TILE)
            gate_ready(c0 // FF_TILE)
            up_ready(c0 // FF_TILE)
            gt = lax.dot_general(u_ref[...], wg_ref[cols, :], NT_DIMS, preferred_element_type=F32)
            up = lax.dot_general(u_ref[...], wu_ref[cols, :], NT_DIMS, preferred_element_type=F32)
            gt_ref[:, cols] = gt.astype(BF)
            up_ref[:, cols] = up.astype(BF)
            act_ref[:, cols] = (gt * _sigmoid(gt) * up).astype(BF)

        _col_tiles(D_FF, FF_TILE, tile)

    big = jax.ShapeDtypeStruct((T, D_FF), BF)
    return pl.pallas_call(
        body, grid=(T // RC,), in_specs=[_rows_spec(D), _vec(D), ANY_SPEC, ANY_SPEC],
        out_specs=[_rows_spec(D_FF), _rows_spec(D_FF), _rows_spec(D_FF), _rows_spec(D)],
        out_shape=[big, big, big, jax.ShapeDtypeStruct((T, D), BF)],
        scratch_shapes=[pltpu.VMEM((D_FF, D), BF)] * 2 + [pltpu.SemaphoreType.DMA((len(FF_TILE_ROWS),))] * 2,
        compiler_params=_params(), name="gate_up")(h1, wn, w_gate, w_up)


def down_loss(act, w_down, h1, target, wf):
    first = NPAD + N_META

    def body(a_ref, w_ref, r_ref, t_hbm, wf_ref, d_ref, db_ref, l_ref, dw_ref, h_scr, t_ref, t_sem):
        i = pl.program_id(0)
        _zero_at_first(l_ref, dw_ref)
        head = pltpu.make_async_copy(t_hbm.at[pl.ds(0, RC - first)], t_ref.at[pl.ds(first, RC - first)], t_sem)
        rest = pltpu.make_async_copy(t_hbm.at[pl.ds(pl.multiple_of(jnp.maximum(i * RC - first, 0), 32), RC)], t_ref, t_sem)

        @pl.when(i == 0)
        def _():
            t_ref[0:first, :] = jnp.zeros((first, D), F32)
            head.start()

        @pl.when(i > 0)
        def _():
            rest.start()

        def tile(c0):
            cols = pl.ds(c0, 512)
            h_scr[:, cols] = r_ref[:, cols] + jnp.dot(a_ref[...], w_ref[:, cols], preferred_element_type=F32)

        _col_tiles(D, 512, tile)

        @pl.when(i == 0)
        def _():
            head.wait()

        @pl.when(i > 0)
        def _():
            rest.wait()

        for r in (0, HALF):
            h = h_scr[r:r + HALF, :]
            live = _rows((HALF, D), i * RC + r) >= first
            err = jnp.where(live, _rms(h, wf_ref[...]) - t_ref[r:r + HALF, :], 0.0)
            l_ref[...] += 0.5 * jnp.sum(jnp.sum(err * err, axis=1, keepdims=True) * (1.0 / D), axis=0, keepdims=True)
            dh, dw = _rms_bwd(err * (1.0 / D), h, wf_ref[...])
            dw_ref[...] += jnp.sum(dw, axis=0, keepdims=True)
            d_ref[r:r + HALF, :] = dh
            db_ref[r:r + HALF, :] = dh.astype(BF)

    return pl.pallas_call(
        body, grid=(T // RC,),
        in_specs=[_rows_spec(D_FF), _whole((D_FF, D)), _rows_spec(D), pl.BlockSpec(memory_space=pl.ANY), _vec(D)],
        out_specs=[_rows_spec(D), _rows_spec(D), _spec((1, 128), lambda i: (0, 0)), _vec(D)],
        out_shape=[jax.ShapeDtypeStruct((T, D), F32), jax.ShapeDtypeStruct((T, D), BF), jax.ShapeDtypeStruct((1, 128), F32),
                   jax.ShapeDtypeStruct((1, D), F32)],
        scratch_shapes=[pltpu.VMEM((RC, D), F32), pltpu.VMEM((RC, D), F32), pltpu.SemaphoreType.DMA],
        compiler_params=_params(), name="down_loss")(act, w_down, h1, target, wf)


def swiglu_bwd(dh2_b, w_down, gt, up, act, u2):
    tn = 256

    def body(d_hbm, u_hbm, w_ref, gt_ref, up_ref, act_ref, dg_ref, du_ref, gd_ref, gg_ref, gu_ref, d_ref, u_ref, d_sems, u_sems):
        chunks = list(range(0, T, RC))
        d_ready = _arriving(d_hbm, d_ref, d_sems, chunks, RC)
        u_ready = _arriving(u_hbm, u_ref, u_sems, chunks, RC)

        def rows(r0):
            part = pl.ds(r0, RC)
            d_ready(r0 // RC)
            dact = lax.dot_general(d_ref[part, :], w_ref[...], NT_DIMS, preferred_element_type=F32)
            gt_ = gt_ref[part, :].astype(F32)
            up_ = up_ref[part, :].astype(F32)
            sg = _sigmoid(gt_)
            dg_ref[part, :] = (dact * up_ * (sg * (1.0 + gt_ * (1.0 - sg)))).astype(BF)
            du_ref[part, :] = (dact * (gt_ * sg)).astype(BF)

        _col_tiles(T, RC, rows)
        for k in range(len(chunks)):
            u_ready(k)
        gd_ref[...] = lax.dot_general(act_ref[...], d_ref[...], TN_DIMS, preferred_element_type=F32).astype(BF)
        gg_ref[...] = lax.dot_general(dg_ref[...], u_ref[...], TN_DIMS, preferred_element_type=F32).astype(BF)
        gu_ref[...] = lax.dot_general(du_ref[...], u_ref[...], TN_DIMS, preferred_element_type=F32).astype(BF)

    cols = _spec((T, tn), lambda j: (0, j))
    wrow = _spec((tn, D), lambda j: (j, 0))
    big = jax.ShapeDtypeStruct((T, D_FF), BF)
    grad = jax.ShapeDtypeStruct((D_FF, D), BF)
    return pl.pallas_call(
        body, grid=(D_FF // tn,), in_specs=[ANY_SPEC, ANY_SPEC, wrow, cols, cols, cols],
        out_specs=[cols, cols, wrow, wrow, wrow], out_shape=[big, big, grad, grad, grad],
        scratch_shapes=[pltpu.VMEM((T, D), BF)] * 2 + [pltpu.SemaphoreType.DMA((T // RC,))] * 2,
        compiler_params=_params(), name="swiglu_bwd")(dh2_b, u2, w_down, gt, up, act)


def gate_up_bwd(dgt, dup, w_gate, w_up, h1, wn, dh2):
    def body(dg_ref, du_ref, wg_hbm, wu_hbm, h_ref, wn_ref, r_ref, d_ref, db_ref, dw_ref, du_scr, wg_ref, wu_ref, wg_sems, wu_sems):
        gate_ready = _arriving(wg_hbm, wg_ref, wg_sems, FF_TILE_ROWS, FF_TILE)
        up_ready = _arriving(wu_hbm, wu_ref, wu_sems, FF_TILE_ROWS, FF_TILE)
        _zero_at_first(dw_ref)

        du_scr[...] = jnp.zeros_like(du_scr)

        def tile(c0):
            k = pl.ds(c0, FF_TILE)
            gate_ready(c0 // FF_TILE)
            up_ready(c0 // FF_TILE)
            du_scr[...] += (jnp.dot(dg_ref[:, k], wg_ref[k, :], preferred_element_type=F32)
                            + jnp.dot(du_ref[:, k], wu_ref[k, :], preferred_element_type=F32))

        _col_tiles(D_FF, FF_TILE, tile)
        for r in (0, HALF):
            dh, dw = _rms_bwd(du_scr[r:r + HALF, :], h_ref[r:r + HALF, :], wn_ref[...])
            dw_ref[...] += jnp.sum(dw, axis=0, keepdims=True)
            dh = dh + r_ref[r:r + HALF, :]
            d_ref[r:r + HALF, :] = dh
            db_ref[r:r + HALF, :] = dh.astype(BF)

    return pl.pallas_call(
        body, grid=(T // RC,),
        in_specs=[_rows_spec(D_FF), _rows_spec(D_FF), ANY_SPEC, ANY_SPEC, _rows_spec(D), _vec(D), _rows_spec(D)],
        out_specs=[_rows_spec(D), _rows_spec(D), _vec(D)],
        out_shape=[jax.ShapeDtypeStruct((T, D), F32), jax.ShapeDtypeStruct((T, D), BF), jax.ShapeDtypeStruct((1, D), F32)],
        scratch_shapes=[pltpu.VMEM((RC, D), F32)] + [pltpu.VMEM((D_FF, D), BF)] * 2 + [pltpu.SemaphoreType.DMA((len(FF_TILE_ROWS),))] * 2,
        compiler_params=_params(), name="gate_up_bwd")(dgt, dup, w_gate, w_up, h1, wn, dh2)


def _adamw(w, g, m, v):
    m = ADAM_B1 * m + (1.0 - ADAM_B1) * g
    v = ADAM_B2 * v + (1.0 - ADAM_B2) * (g * g)
    m_hat = m / (1.0 - ADAM_B1 ** ADAM_STEP)
    v_hat = v / (1.0 - ADAM_B2 ** ADAM_STEP)
    delta = -ADAM_LR * (m_hat / (jnp.sqrt(v_hat) + ADAM_EPS) + ADAM_WD * w)
    return delta, m, v


def adamw_shards(name, recvs, ws, ms, vs):
    n = len(ws)

    def body(*refs):
        ins, outs = refs[:4 * n], refs[4 * n:]
        for k in range(n):
            p_ref, w_ref, m_ref, v_ref = ins[k], ins[n + k], ins[2 * n + k], ins[3 * n + k]
            g = p_ref[0].astype(F32)
            for s in range(1, 8):
                g = g + p_ref[s].astype(F32)
            outs[4 * k][...] = g
            outs[4 * k + 1][...], outs[4 * k + 2][...], outs[4 * k + 3][...] = _adamw(w_ref[...], g, m_ref[...], v_ref[...])

    tiles = [_spec((w.shape[0] // 2, w.shape[1]), lambda i: (i, 0)) for w in ws]
    recv_tiles = [_spec((8, w.shape[0] // 2, w.shape[1]), lambda i: (0, i, 0)) for w in ws]
    res = pl.pallas_call(
        body, grid=(2,), in_specs=recv_tiles + tiles * 3,
        out_specs=[t for t in tiles for _ in range(4)],
        out_shape=[jax.ShapeDtypeStruct(w.shape, F32) for w in ws for _ in range(4)],
        compiler_params=_params(), name=name)(*recvs, *ws, *ms, *vs)
    return [list(res[4 * k:4 * k + 4]) for k in range(n)]


def adamw_w_in(recv, w, m, v, after):
    rows = 34
    per_row = D // 128

    def body(p_ref, w_ref, m_ref, v_ref, _after, g_ref, d_ref, mo_ref, vo_ref):
        def chunk(c, carry):
            lines = pl.ds(pl.multiple_of(c * per_row * rows, 16), per_row * rows)
            g = p_ref[0, lines, :].astype(F32)
            for s in range(1, 8):
                g = g + p_ref[s, lines, :].astype(F32)
            g = g.reshape(rows, per_row, 128)
            part = pl.ds(c * rows, rows)
            g_ref[part] = g
            d_ref[part], mo_ref[part], vo_ref[part] = _adamw(w_ref[part], g, m_ref[part], v_ref[part])
            return carry

        lax.fori_loop(0, w.shape[0] // rows, chunk, 0)

    shape = jax.ShapeDtypeStruct(w.shape, F32)
    whole = pl.BlockSpec(memory_space=pltpu.VMEM)
    return pl.pallas_call(body, out_shape=[shape] * 4, in_specs=[whole] * 4 + [ANY_SPEC], compiler_params=_params(0),
                          name="adamw_w_in")(recv, w, m, v, after)


def sum_slabs(recv):
    def body(p_ref, o_ref):
        g = p_ref[0]
        for s in range(1, 8):
            g = g + p_ref[s]
        for s in range(8):
            o_ref[s] = g

    return pl.pallas_call(body, out_shape=jax.ShapeDtypeStruct(recv.shape, F32), compiler_params=_params(0), name="sum_slabs")(recv)


SIMPLE = [("norm1_w", 1024), ("ssd_conv_b", 1536), ("ssd_dt_bias", 16), ("ssd_a_log", 16), ("ssd_d", 16), ("ssd_norm_w", 1024),
          ("lru_conv_b", 1024), ("lru_ba", 1024), ("lru_bx", 1024), ("lru_lambda", 1024), ("lru_norm_w", 1024), ("norm2_w", 1024),
          ("final_norm_w", 1024)]
SPECIAL = ["lru_wa", "lru_wx", "meta_tokens", "ssd_conv_w", "lru_conv_w"]
SM_ROWS = 176
SM_WA, SM_WX, SM_META, SM_SCW, SM_LCW, SM_LOSS = 14, 78, 142, 158, 166, 170


def _simple_rows():
    rows, r = {}, 0
    for name, n in SIMPLE:
        rows[name] = r
        r += -(-n // 1024)
    return rows


def adamw_small(sm, special_g, ws, ms, vs):
    rows = _simple_rows()
    ns, nx = len(SIMPLE), len(SPECIAL)

    def body(*refs):
        sm_ref = refs[0]
        gx = refs[1:1 + nx]
        wr = refs[1 + nx:1 + nx + ns + nx]
        mr = refs[1 + nx + ns + nx:1 + nx + 2 * (ns + nx)]
        vr = refs[1 + nx + 2 * (ns + nx):1 + nx + 3 * (ns + nx)]
        outs = refs[1 + nx + 3 * (ns + nx):]
        o = 0
        for k, (name, n) in enumerate(SIMPLE):
            r0 = rows[name]
            for c0 in range(0, n, 1024):
                wd = min(1024, n - c0)
                g = sm_ref[r0 + c0 // 1024:r0 + c0 // 1024 + 1, 0:wd]
                sl = (slice(None), slice(c0, c0 + wd))
                d, m2, v2 = _adamw(wr[k][sl], g, mr[k][sl], vr[k][sl])
                outs[o][sl] = g
                outs[o + 1][sl] = d
                outs[o + 2][sl] = m2
                outs[o + 3][sl] = v2
            o += 4
        for k in range(nx):
            g = gx[k][...]
            d, m2, v2 = _adamw(wr[ns + k][...], g, mr[ns + k][...], vr[ns + k][...])
            outs[o][...] = g
            outs[o + 1][...] = d
            outs[o + 2][...] = m2
            outs[o + 3][...] = v2
            o += 4
        outs[o][...] = sm_ref[SM_LOSS:SM_LOSS + 1, 0:1]

    out_shape = []
    for k in range(ns + nx):
        out_shape += [jax.ShapeDtypeStruct(ws[k].shape, F32)] * 4
    out_shape.append(jax.ShapeDtypeStruct((1, 1), F32))
    return pl.pallas_call(body, out_shape=out_shape, compiler_params=_params(0), name="adamw_small")(sm, *special_g, *ws, *ms, *vs)


def _place():
    return lax.axis_index("x"), lax.axis_index("y"), lax.axis_index("c")


def _index(px, py, pc):
    return 4 * px + 2 * py + pc


def all_gather(name, shards):
    n = len(shards)
    hbm = pl.BlockSpec(memory_space=pl.ANY)

    def pieces(s):
        tile = 32 // s.dtype.itemsize
        per = s.shape[0] // tile // 4 * tile
        return [(0, s.shape[0])] if s.shape[0] < 256 else [(r * per, per if r < 3 else s.shape[0] - 3 * per) for r in range(4)]

    parts = [pieces(s) for s in shards]
    first_sem = [7 * sum(len(p) for p in parts[:i]) for i in range(n + 1)]

    def body(*refs):
        ins, outs = refs[:n], refs[n:2 * n]
        send_sems, recv_sems, local_sems = refs[2 * n:]
        x, y, c = _place()
        me, sibling = (x, y, c), (x, y, 1 - c)
        chips = [(1 - x, y), (x, 1 - y), (1 - x, 1 - y)]

        def copy(i, r, k, block, to, src=None):
            rows = pl.ds(*parts[i][r])
            dst = outs[i].at[_index(*block), rows]
            sem = first_sem[i] + 7 * r + k
            return pltpu.make_async_remote_copy(src_ref=dst if src is None else src.at[rows], dst_ref=dst, send_sem=send_sems.at[sem],
                                                recv_sem=recv_sems.at[sem], device_id=to, device_id_type=MESH)

        every = [(i, r) for i in range(n) for r in range(len(parts[i]))]
        mine = [pltpu.make_async_copy(ins[i], outs[i].at[_index(*me)], local_sems.at[i]) for i in range(n)]
        for cp in mine:
            cp.start()
        first = []
        for i, r in every:
            first += [copy(i, r, 1 + j, me, (*chip, c), src=ins[i]) for j, chip in enumerate(chips)]
            first.append(copy(i, r, 0, me, sibling, src=ins[i]))
        for cp in first:
            cp.start()
        passed = []
        for i, r in every:
            for j, chip in enumerate(chips):
                copy(i, r, 1 + j, (*chip, c), me).wait_recv()
                cp = copy(i, r, 4 + j, (*chip, c), sibling)
                cp.start()
                passed.append(cp)
        for i, r in every:
            copy(i, r, 0, sibling, me).wait_recv()
            for j, chip in enumerate(chips):
                copy(i, r, 4 + j, (*chip, 1 - c), me).wait_recv()
        for cp in first + passed:
            cp.wait_send()
        for cp in mine:
            cp.wait()

    return pl.pallas_call(
        body, in_specs=[hbm] * n, out_specs=[hbm] * n,
        out_shape=[jax.ShapeDtypeStruct((8,) + s.shape, s.dtype) for s in shards],
        scratch_shapes=[pltpu.SemaphoreType.DMA((first_sem[n],)), pltpu.SemaphoreType.DMA((first_sem[n],)), pltpu.SemaphoreType.DMA((n,))],
        name=name)(*shards)


HBM_SPEC = pl.BlockSpec(memory_space=pltpu.HBM)
SEM_SPEC = pl.BlockSpec(memory_space=pltpu.SEMAPHORE)
EFFECT = pltpu.SideEffectType.DATAFLOW_SIDE_EFFECTING


def _peers(x, y, c):
    return [((1 - x) if k & 4 else x, (1 - y) if k & 2 else y, (1 - c) if k & 1 else c) for k in range(1, 8)]


def _pieces(rows):
    for n in (4, 2):
        if rows % (16 * n) == 0:
            return [(r * (rows // n), rows // n) for r in range(n)]
    return [(0, rows)]


def _peer_copies(src, land, send_sems, recv_sems, k, peer, mine):
    block = src.at[_index(*peer)]
    return [pltpu.make_async_remote_copy(src_ref=block.at[pl.ds(r0, nr)], dst_ref=land.at[mine, pl.ds(r0, nr)], send_sem=send_sems.at[k],
                                         recv_sem=recv_sems.at[k], device_id=peer, device_id_type=MESH)
            for r0, nr in _pieces(block.shape[0])]


OWN = 7


def _own_copy(src, land, send_sems, mine):
    return pltpu.make_async_copy(src.at[mine], land.at[mine], send_sems.at[OWN])


def copies_start(name, srcs):
    n = len(srcs)

    def body(*refs):
        ins, lands = refs[:n], refs[n:2 * n]
        sends, recvs = refs[2 * n:3 * n], refs[3 * n:4 * n]
        token = refs[-1]
        x, y, c = _place()
        mine = _index(x, y, c)
        for i in range(n):
            per_peer = [_peer_copies(ins[i], lands[i], sends[i], recvs[i], k, peer, mine) for k, peer in enumerate(_peers(x, y, c))]
            for piece in zip(*per_peer):
                for cp in piece:
                    cp.start()
            _own_copy(ins[i], lands[i], sends[i], mine).start()
        token[...] = jnp.zeros_like(token)

    res = pl.pallas_call(
        body, name=name,
        out_shape=([pltpu.SemaphoreType.DMA((8,))] * n + [pltpu.SemaphoreType.DMA((7,))] * n + [pltpu.HBM(s.shape, s.dtype) for s in srcs] * 2
                   + [jax.ShapeDtypeStruct((8, 128), F32)]),
        in_specs=[HBM_SPEC] * (2 * n),
        out_specs=[SEM_SPEC] * (2 * n) + [HBM_SPEC] * (2 * n) + [pl.BlockSpec(memory_space=pltpu.VMEM)],
        input_output_aliases={i: 2 * n + i for i in range(2 * n)},
        compiler_params=pltpu.CompilerParams(has_side_effects=EFFECT),
    )(*[pltpu.with_memory_space_constraint(s, pltpu.HBM) for s in srcs],
      *[pltpu.with_memory_space_constraint(lax.empty(s.shape, s.dtype), pltpu.HBM) for s in srcs])
    return [(res[i], res[n + i], res[2 * n + i], res[3 * n + i]) for i in range(n)], res[-1]


def copies_wait(name, started, after):
    n = len(started)

    def body(*refs):
        ins, lands = refs[:n], refs[n:2 * n]
        sends, recvs = refs[2 * n:3 * n], refs[3 * n:4 * n]
        x, y, c = _place()
        mine = _index(x, y, c)
        for i in range(n):
            for k, peer in enumerate(_peers(x, y, c)):
                arrival = pltpu.make_async_remote_copy(src_ref=ins[i].at[mine], dst_ref=lands[i].at[_index(*peer)],
                                                       send_sem=sends[i].at[k], recv_sem=recvs[i].at[k], device_id=peer, device_id_type=MESH)
                arrival.wait_send()
                arrival.wait_recv()
            _own_copy(ins[i], lands[i], sends[i], mine).wait()

    srcs = [s[2] for s in started]
    lands = [s[3] for s in started]
    res = pl.pallas_call(
        body, name=name,
        out_shape=[pltpu.HBM(s.shape, s.dtype) for s in srcs] + [pltpu.HBM(z.shape, z.dtype) for z in lands],
        in_specs=[HBM_SPEC] * (2 * n) + [SEM_SPEC] * (2 * n) + [ANY_SPEC] * len(after),
        out_specs=[HBM_SPEC] * (2 * n),
        input_output_aliases={i: i for i in range(2 * n)},
        compiler_params=pltpu.CompilerParams(has_side_effects=EFFECT),
    )(*srcs, *lands, *[s[0] for s in started], *[s[1] for s in started], *after)
    return list(res[n:])


def _hop(src, land, send_sems, recv_sems, k, block, to):
    dst = land.at[_index(*block)]
    return pltpu.make_async_remote_copy(src_ref=dst if src is None else src, dst_ref=dst, send_sem=send_sems.at[k], recv_sem=recv_sems.at[k],
                                        device_id=to, device_id_type=MESH)


def _own_block(src, land, send_sems, mine):
    return pltpu.make_async_copy(src, land.at[mine], send_sems.at[4])


def _other_chips(x, y):
    return [(1 - x, y), (x, 1 - y), (1 - x, 1 - y)]


def gather_start(name, shards, through):
    n, m = len(shards), len(through)

    def body(*refs):
        ins, lands = refs[:n], refs[n:2 * n]
        sends, recvs = refs[2 * n + m:3 * n + m], refs[3 * n + m:4 * n + m]
        x, y, c = _place()
        for i in range(n):
            for j, chip in enumerate(_other_chips(x, y)):
                _hop(ins[i], lands[i], sends[i], recvs[i], 1 + j, (x, y, c), (*chip, c)).start()
            _hop(ins[i], lands[i], sends[i], recvs[i], 0, (x, y, c), (x, y, 1 - c)).start()
            _own_block(ins[i], lands[i], sends[i], _index(x, y, c)).start()

    own, passing = pltpu.SemaphoreType.DMA((5,)), pltpu.SemaphoreType.DMA((3,))
    zones = [jax.ShapeDtypeStruct((8,) + s.shape, s.dtype) for s in shards]
    res = pl.pallas_call(
        body, name=name,
        out_shape=([own] * (2 * n) + [passing] * (2 * n) + [pltpu.HBM(s.shape, s.dtype) for s in shards]
                   + [pltpu.HBM(z.shape, z.dtype) for z in zones] + [pltpu.HBM(t.shape, t.dtype) for t in through]),
        in_specs=[HBM_SPEC] * (2 * n + m),
        out_specs=[SEM_SPEC] * (4 * n) + [HBM_SPEC] * (2 * n + m),
        input_output_aliases={i: 4 * n + i for i in range(2 * n + m)},
        compiler_params=pltpu.CompilerParams(has_side_effects=EFFECT),
    )(*[pltpu.with_memory_space_constraint(s, pltpu.HBM) for s in shards],
      *[pltpu.with_memory_space_constraint(lax.empty(z.shape, z.dtype), pltpu.HBM) for z in zones],
      *[pltpu.with_memory_space_constraint(t, pltpu.HBM) for t in through])
    return [[res[4 * n + i], res[5 * n + i], res[i], res[n + i], res[2 * n + i], res[3 * n + i]] for i in range(n)], list(res[6 * n:])


def gather_stage(name, pass_on, finish, after):
    arrays = pass_on + finish
    n = len(arrays)

    def body(*refs):
        ins, lands = refs[:n], refs[n:2 * n]
        sems = [refs[(2 + q) * n:(3 + q) * n] for q in range(4)]
        x, y, c = _place()
        me, sibling = (x, y, c), (x, y, 1 - c)
        for i in range(len(pass_on)):
            send, recv, send_on, recv_on = (q[i] for q in sems)
            for j, chip in enumerate(_other_chips(x, y)):
                _hop(None, lands[i], send, recv, 1 + j, (*chip, c), me).wait_recv()
                _hop(None, lands[i], send_on, recv_on, j, (*chip, c), sibling).start()
        for i in range(len(pass_on), n):
            send, recv, send_on, recv_on = (q[i] for q in sems)
            _hop(ins[i], lands[i], send, recv, 0, sibling, me).wait_recv()
            for j, chip in enumerate(_other_chips(x, y)):
                _hop(None, lands[i], send_on, recv_on, j, (*chip, 1 - c), me).wait_recv()
            _hop(ins[i], lands[i], send, recv, 0, me, sibling).wait_send()
            _own_block(ins[i], lands[i], send, _index(*me)).wait()
            for j, chip in enumerate(_other_chips(x, y)):
                _hop(ins[i], lands[i], send, recv, 1 + j, me, (*chip, c)).wait_send()
                _hop(None, lands[i], send_on, recv_on, j, (*chip, c), sibling).wait_send()
        refs[-1][...] = jnp.zeros_like(refs[-1])

    res = pl.pallas_call(
        body, name=name,
        out_shape=([pltpu.HBM(a[0].shape, a[0].dtype) for a in arrays] + [pltpu.HBM(a[1].shape, a[1].dtype) for a in arrays]
                   + [jax.ShapeDtypeStruct((8, 128), F32)]),
        in_specs=[HBM_SPEC] * (2 * n) + [SEM_SPEC] * (4 * n) + [ANY_SPEC],
        out_specs=[HBM_SPEC] * (2 * n) + [pl.BlockSpec(memory_space=pltpu.VMEM)],
        input_output_aliases={i: i for i in range(2 * n)},
        compiler_params=pltpu.CompilerParams(has_side_effects=EFFECT),
    )(*[a[0] for a in arrays], *[a[1] for a in arrays], *[a[2 + q] for q in range(4) for a in arrays], after)
    for i, a in enumerate(arrays):
        a[0], a[1] = res[i], res[n + i]
    return [a[1] for a in finish], res[-1]


WEIGHTS = ["meta_tokens", "norm1_w", "w_in", "ssd_conv_w", "ssd_conv_b", "ssd_dt_bias", "ssd_a_log", "ssd_d", "ssd_norm_w", "lru_conv_w",
           "lru_conv_b", "lru_wa", "lru_ba", "lru_wx", "lru_bx", "lru_lambda", "lru_norm_w", "w_out", "norm2_w", "w_gate", "w_up", "w_down",
           "final_norm_w"]
BIG = ["w_in", "w_out", "w_gate", "w_up", "w_down"]
COLUMN_SHARDED = ["w_in", "w_gate", "w_up"]


def _pair_blocks(w):
    w = w.reshape(8, 2, 64, 64)
    z = jnp.zeros((8, 64, 64), w.dtype)
    return jnp.concatenate([jnp.concatenate([w[:, 0], z], axis=2), jnp.concatenate([z, w[:, 1]], axis=2)], axis=1)


def _unpair_blocks(w2):
    return jnp.stack([w2[:, :64, :64], w2[:, 64:, 64:]], axis=1).reshape(16, 64, 64)


def _per_group(vs):
    return jnp.pad(jnp.concatenate(vs, axis=0).reshape(len(vs), 2, 1, 8), ((0, 0), (0, 0), (0, 0), (0, 120)))


def _pad_cols(v, n):
    return jnp.pad(v, ((0, 0), (0, n - v.shape[1])))


def local_step(x, target, small_w, ssd_cw, w_in_shards, fetch, send, p):
    heads = _per_group([p["ssd_dt_bias"], p["ssd_a_log"], p["ssd_d"]])
    wa2 = _pair_blocks(p["lru_wa"]).astype(BF)
    wx2 = _pair_blocks(p["lru_wx"]).astype(BF)
    lru = (small_w, p["lru_conv_b"], wa2, p["lru_ba"], wx2, p["lru_bx"], p["lru_lambda"])

    proj, dt_raw, u1, h0, w_in, w_dt = in_proj(x, small_w, p["norm1_w"], w_in_shards)
    yn_ssd, y_pre, h_prev = ssd_fwd(proj, dt_raw, ssd_cw, p["ssd_conv_b"], heads, p["ssd_norm_w"])
    _, moved = fetch([], yn_ssd)
    hseq, a = lru_fwd(proj, *lru, moved)
    (w_out,), _ = fetch(["w_out"], hseq)
    h1, cat = out_proj(yn_ssd, proj, hseq, p["lru_norm_w"], w_out, h0)
    (w_gate, w_up), _ = fetch(["w_gate", "w_up"], h1)
    gt, up, act, u2 = gate_up(h1, p["norm2_w"], w_gate, w_up)
    (w_down,), _ = fetch(["w_down"], act)
    dh2, dh2_b, loss, d_fnw = down_loss(act, w_down, h1, target, p["final_norm_w"])

    dgt, dup, g_down, g_gate, g_up = swiglu_bwd(dh2_b, w_down, gt, up, act, u2)
    dh1, dh1_b, d_n2 = gate_up_bwd(dgt, dup, w_gate, w_up, h1, p["norm2_w"], dh2)
    sent = send({"w_down": g_down, "w_gate": g_gate, "w_up": g_up, "w_out": weight_grad("dw_out", cat, dh1_b)})
    dyn, dh_out, dg_b, d_lnw = out_proj_bwd(dh1_b, w_out, proj, hseq, p["lru_norm_w"], sent)

    dxl_b, d_lcw, d_lcb, dwa2, d_ba, dwx2, d_bx, d_lam = lru_bwd(dh_out, a, hseq, proj, *lru)
    dz_b, dxbc_b, ddt_b, dpar, d_snw, d_scw, d_scb = ssd_bwd(dyn, proj, dt_raw, ssd_cw, p["ssd_conv_b"], y_pre, h_prev, heads,
                                                             p["ssd_norm_w"], sent)
    sent = send({"w_in": in_weight_grad([dz_b, dxbc_b, dg_b, dxl_b], [0, SSD_W, 2576, 2576 + LRU_W], ddt_b, u1)})
    grad_x, d_meta, d_n1, dwa2, dwx2 = in_proj_bwd(dz_b, dg_b, dxl_b, dxbc_b, ddt_b, w_in, w_dt, h0, p["norm1_w"], dh1, sent, [dwa2, dwx2])
    small = {"norm1_w": d_n1, "ssd_conv_b": d_scb, "ssd_dt_bias": dpar[:, 0, :8].reshape(1, 16), "ssd_a_log": dpar[:, 1, :8].reshape(1, 16),
             "ssd_d": dpar[:, 2, :8].reshape(1, 16), "ssd_norm_w": d_snw, "lru_conv_b": d_lcb, "lru_ba": d_ba, "lru_bx": d_bx,
             "lru_lambda": d_lam, "lru_norm_w": d_lnw, "norm2_w": d_n2, "final_norm_w": d_fnw,
             "lru_wa": _unpair_blocks(dwa2), "lru_wx": _unpair_blocks(dwx2), "meta_tokens": d_meta,
             "ssd_conv_w": d_scw, "lru_conv_w": d_lcw}
    return loss, grad_x, small


def _pack_small(small, loss):
    rows = [_pad_cols(small[name], -(-n // 1024) * 1024).reshape(-1, 1024) for name, n in SIMPLE]
    rows += [small["lru_wa"].reshape(64, 1024), small["lru_wx"].reshape(64, 1024), small["meta_tokens"],
             _pad_cols(small["ssd_conv_w"], 2048).reshape(8, 1024), small["lru_conv_w"], _pad_cols(loss[:, 0:1], 1024)]
    sm = jnp.concatenate(rows, axis=0)
    return jnp.pad(sm, ((0, SM_ROWS - sm.shape[0]), (0, 0)))


def _slabs(g):
    return g.reshape(8, g.shape[0] // 8, g.shape[1])


def _unslab(g):
    return g.reshape(8 * g.shape[1], g.shape[2])


def kernel(x, meta_tokens, norm1_w, w_in, ssd_conv_w, ssd_conv_b, ssd_dt_bias, ssd_a_log, ssd_d, ssd_norm_w, lru_conv_w, lru_conv_b, lru_wa, lru_ba, lru_wx, lru_bx, lru_lambda, lru_norm_w, w_out, norm2_w, w_gate, w_up, w_down, final_norm_w, loss_target, m_meta_tokens, m_norm1_w, m_w_in, m_ssd_conv_w, m_ssd_conv_b, m_ssd_dt_bias, m_ssd_a_log, m_ssd_d, m_ssd_norm_w, m_lru_conv_w, m_lru_conv_b, m_lru_wa, m_lru_ba, m_lru_wx, m_lru_bx, m_lru_lambda, m_lru_norm_w, m_w_out, m_norm2_w, m_w_gate, m_w_up, m_w_down, m_final_norm_w, v_meta_tokens, v_norm1_w, v_w_in, v_ssd_conv_w, v_ssd_conv_b, v_ssd_dt_bias, v_ssd_a_log, v_ssd_d, v_ssd_norm_w, v_lru_conv_w, v_lru_conv_b, v_lru_wa, v_lru_ba, v_lru_wx, v_lru_bx, v_lru_lambda, v_lru_norm_w, v_w_out, v_norm2_w, v_w_gate, v_w_up, v_w_down, v_final_norm_w):
    w = dict(meta_tokens=meta_tokens, norm1_w=norm1_w, w_in=w_in[0], ssd_conv_w=ssd_conv_w[0], ssd_conv_b=ssd_conv_b, ssd_dt_bias=ssd_dt_bias,
             ssd_a_log=ssd_a_log, ssd_d=ssd_d, ssd_norm_w=ssd_norm_w, lru_conv_w=lru_conv_w[0], lru_conv_b=lru_conv_b, lru_wa=lru_wa[0],
             lru_ba=lru_ba, lru_wx=lru_wx[0], lru_bx=lru_bx, lru_lambda=lru_lambda, lru_norm_w=lru_norm_w, w_out=w_out[0], norm2_w=norm2_w,
             w_gate=w_gate[0], w_up=w_up[0], w_down=w_down[0], final_norm_w=final_norm_w.reshape(1, D))
    m = dict(meta_tokens=m_meta_tokens, norm1_w=m_norm1_w, w_in=m_w_in[0], ssd_conv_w=m_ssd_conv_w[0], ssd_conv_b=m_ssd_conv_b,
             ssd_dt_bias=m_ssd_dt_bias, ssd_a_log=m_ssd_a_log, ssd_d=m_ssd_d, ssd_norm_w=m_ssd_norm_w, lru_conv_w=m_lru_conv_w[0],
             lru_conv_b=m_lru_conv_b, lru_wa=m_lru_wa[0], lru_ba=m_lru_ba, lru_wx=m_lru_wx[0], lru_bx=m_lru_bx, lru_lambda=m_lru_lambda,
             lru_norm_w=m_lru_norm_w, w_out=m_w_out[0], norm2_w=m_norm2_w, w_gate=m_w_gate[0], w_up=m_w_up[0], w_down=m_w_down[0],
             final_norm_w=m_final_norm_w.reshape(1, D))
    v = dict(meta_tokens=v_meta_tokens, norm1_w=v_norm1_w, w_in=v_w_in[0], ssd_conv_w=v_ssd_conv_w[0], ssd_conv_b=v_ssd_conv_b,
             ssd_dt_bias=v_ssd_dt_bias, ssd_a_log=v_ssd_a_log, ssd_d=v_ssd_d, ssd_norm_w=v_ssd_norm_w, lru_conv_w=v_lru_conv_w[0],
             lru_conv_b=v_lru_conv_b, lru_wa=v_lru_wa[0], lru_ba=v_lru_ba, lru_wx=v_lru_wx[0], lru_bx=v_lru_bx, lru_lambda=v_lru_lambda,
             lru_norm_w=v_lru_norm_w, w_out=v_w_out[0], norm2_w=v_norm2_w, w_gate=v_w_gate[0], w_up=v_w_up[0], w_down=v_w_down[0],
             final_norm_w=v_final_norm_w.reshape(1, D))
    shapes = dict(meta_tokens=meta_tokens.shape, norm1_w=norm1_w.shape, w_in=w_in.shape, ssd_conv_w=ssd_conv_w.shape,
                  ssd_conv_b=ssd_conv_b.shape, ssd_dt_bias=ssd_dt_bias.shape, ssd_a_log=ssd_a_log.shape, ssd_d=ssd_d.shape,
                  ssd_norm_w=ssd_norm_w.shape, lru_conv_w=lru_conv_w.shape, lru_conv_b=lru_conv_b.shape, lru_wa=lru_wa.shape,
                  lru_ba=lru_ba.shape, lru_wx=lru_wx.shape, lru_bx=lru_bx.shape, lru_lambda=lru_lambda.shape, lru_norm_w=lru_norm_w.shape,
                  w_out=w_out.shape, norm2_w=norm2_w.shape, w_gate=w_gate.shape, w_up=w_up.shape, w_down=w_down.shape,
                  final_norm_w=final_norm_w.shape)
    me = _index(*_place())
    for n in COLUMN_SHARDED:
        w[n], m[n], v[n] = w[n].T, m[n].T, v[n].T

    small_shard = jnp.concatenate([w["meta_tokens"], _pad_cols(w["ssd_conv_w"], 256).reshape(8, 128), w["lru_conv_w"],
                                   jnp.zeros((4, 128), F32)], axis=0)
    g_in, gs = all_gather("gather_w_in", [w["w_in"].astype(BF), small_shard])
    later = ["w_out", "w_gate", "w_up", "w_down"]
    started, (g_in, gs) = gather_start("gather_rest_start", [w[n].astype(BF) for n in later], [g_in, gs])
    started = dict(zip(later, started))
    ssd_cw = gs[:, 16:24].reshape(8, 4, 256)[:, :, :192].transpose(1, 0, 2).reshape(4, XBC)

    def fetch(names, after):
        pass_on = {"w_out": ["w_down"], "w_gate": [], "w_down": []}[names[0]] if names else ["w_out", "w_gate", "w_up"]
        got, zero = gather_stage("gather_" + (names[0] + "_wait" if names else "pass_on"), [started[n] for n in pass_on],
                                 [started[n] for n in names], after)
        return [_unslab(g) for g in got], zero

    in_flight = {}

    def send(grads):
        names = list(grads)
        st, zero = copies_start("grads_" + names[0] + "_start", [grads[n] if n == "small" else _slabs(grads[n]) for n in names])
        in_flight.update(zip(names, st))
        return zero

    loss, grad_x, small = local_step(x[0], loss_target[0], gs, ssd_cw, g_in, fetch, send, w)
    send({"small": _pack_small(small, loss).reshape(8, SM_ROWS // 8, 1024)})

    out = {}
    early = ["w_down", "w_gate", "w_up", "w_out"]
    recv = dict(zip(early, copies_wait("grads_early_wait", [in_flight[n] for n in early], [in_flight["small"][2]])))
    for pair in (early[:2], early[2:]):
        done = adamw_shards("adamw_" + pair[0], [recv[n] for n in pair], [w[n] for n in pair], [m[n] for n in pair], [v[n] for n in pair])
        out.update(zip(pair, done))
    recv_in, recv_small = copies_wait("grads_late_wait", [in_flight["w_in"], in_flight["small"]], [out[n][0] for n in early])
    def lines(a):
        return jnp.transpose(a.reshape(D // 128, 128, IN_COLS // 8), (2, 0, 1))

    gathering, zero = copies_start("gather_small_start", [sum_slabs(recv_small)])
    updated = adamw_w_in(recv_in, lines(w_in), lines(m_w_in), lines(v_w_in), zero)
    out["w_in"] = [jnp.transpose(o, (1, 2, 0)).reshape(D, IN_COLS // 8) for o in updated]
    for n in ("w_gate", "w_up"):
        out[n] = [o.T for o in out[n]]
    sm = copies_wait("gather_small_wait", gathering, [updated[0]])[0].reshape(SM_ROWS, 1024)
    special_g =[sm[SM_WA:SM_WA + 64].reshape(16, 64, 64), sm[SM_WX:SM_WX + 64].reshape(16, 64, 64),
                 lax.dynamic_slice(sm[SM_META:SM_META + 16], (0, 128 * me), (16, 128)),
                 lax.dynamic_slice(sm[SM_SCW:SM_SCW + 8].reshape(4, 2048), (0, 192 * me), (4, 192)),
                 lax.dynamic_slice(sm[SM_LCW:SM_LCW + 4], (0, 128 * me), (4, 128))]
    names = [n for n, _ in SIMPLE] + SPECIAL
    res = adamw_small(sm, special_g, [w[n] for n in names], [m[n] for n in names], [v[n] for n in names])
    for k, n in enumerate(names):
        out[n] = res[4 * k:4 * k + 4]
    flat = [res[-1].reshape(()), grad_x[None]]
    for k in range(4):
        flat += [out[n][k].reshape(shapes[n]) for n in WEIGHTS]
    return tuple(flat)
```

```python
import math

import jax
import jax.numpy as jnp
from jax import lax
from jax.experimental import pallas as pl
from jax.experimental.pallas import tpu as pltpu

F32 = jnp.float32
BF = jnp.bfloat16

D = 1024
SEQ = 2048
N_META = 16
Q = 128
NPAD = 112
T = NPAD + N_META + SEQ
NCH = T // Q
RC = 544
D_FF = 2816
SSD_W = 1024
LRU_W = 1024
XBC = 1536
IN_COLS = 4624
PZ, PG, PXL, PXBC = 0, 1024, 2048, 3072
NP_IN = 4608
EPS = 1e-6
LRU_C = 8.0
VMEM_LIMIT = 56 * 1024 * 1024

ADAM_LR, ADAM_B1, ADAM_B2, ADAM_EPS, ADAM_WD, ADAM_STEP = 0.001, 0.9, 0.999, 1e-08, 0.01, 10

NT_DIMS = (((1,), (1,)), ((), ()))
TN_DIMS = (((0,), (0,)), ((), ()))
MESH = pl.DeviceIdType.MESH


def _params(n_grid=1, limit=VMEM_LIMIT):
    return pltpu.CompilerParams(dimension_semantics=("arbitrary",) * n_grid, vmem_limit_bytes=limit)


def _spec(shape, imap, single=False):
    if single:
        return pl.BlockSpec(shape, imap, pipeline_mode=pl.Buffered(1))
    return pl.BlockSpec(shape, imap)


def _sigmoid(x):
    return 0.5 * jnp.tanh(0.5 * x) + 0.5


def _sigmoid_gate(x):
    return 1.0 / (1.0 + jnp.exp(-x))


def _softplus(x):
    return jnp.maximum(x, 0.0) + jnp.log(1.0 + jnp.exp(-jnp.abs(x)))


def _rms_stats(h):
    return lax.rsqrt(jnp.mean(h * h, axis=-1, keepdims=True) + EPS)


def _rms(h, w):
    return (h * _rms_stats(h)) * w


def _rms_bwd(du, h, w):
    r = _rms_stats(h)
    n = h * r
    dn = du * w
    dh = r * (dn - n * jnp.mean(dn * n, axis=-1, keepdims=True))
    return dh, du * n


_G0 = math.sqrt(2.0 / math.pi)


def _gelu(x):
    return 0.5 * x * (1.0 + jnp.tanh(_G0 * (x + 0.044715 * (x * x * x))))


def _gelu_grad(x):
    t = jnp.tanh(_G0 * (x + 0.044715 * (x * x * x)))
    return 0.5 * (1.0 + t) + 0.5 * x * (1.0 - t * t) * (_G0 * (1.0 + 3.0 * 0.044715 * (x * x)))


def _rows(shape, r0=0):
    return lax.broadcasted_iota(jnp.int32, shape, 0) + r0


def _lanes(shape):
    return lax.broadcasted_iota(jnp.int32, shape, 1)


HALO = 8


def _fill_padded(pad_ref, x_ref):
    pad_ref[0:HALO, :] = jnp.zeros((HALO, pad_ref.shape[1]), F32)
    pad_ref[T + HALO:T + 2 * HALO, :] = jnp.zeros((HALO, pad_ref.shape[1]), F32)

    def step(c, carry):
        r0 = pl.multiple_of(c * Q, Q)
        pad_ref[pl.ds(r0 + HALO, Q), :] = x_ref[pl.ds(r0, Q), :].astype(F32)
        return carry

    lax.fori_loop(0, NCH, step, 0)


def _back(pad_ref, r0):
    win = pad_ref[pl.ds(r0, Q + HALO), :]
    return lambda s: win[HALO:, :] if s == 0 else pltpu.roll(win, s, axis=0)[HALO:, :]


def _ahead(pad_ref, r0):
    win = pad_ref[pl.ds(r0 + HALO, Q + HALO), :]
    return lambda s: win[:Q, :] if s == 0 else pltpu.roll(win, Q + HALO - s, axis=0)[:Q, :]


def _conv(back, w, b):
    y = b + w[3:4, :] * back(0)
    for k in range(3):
        y = y + w[k:k + 1, :] * back(3 - k)
    return y


def _conv_bwd_x(ahead, w):
    dx = w[3:4, :] * ahead(0)
    for k in range(3):
        dx = dx + w[k:k + 1, :] * ahead(3 - k)
    return dx


def _conv_bwd_w(dy, back):
    dws = [jnp.sum(dy * back(3 - k), axis=0, keepdims=True) for k in range(4)]
    return jnp.concatenate(dws, axis=0), jnp.sum(dy, axis=0, keepdims=True)


def _chunks(fn, unrolled=False):
    if unrolled:
        for c in range(NCH):
            fn(c * Q)
        return

    def step(c, carry):
        fn(pl.multiple_of(c * Q, Q))
        return carry

    lax.fori_loop(0, NCH, step, 0)


HALF = RC // 2


def _col_tiles(n, tn, fn):
    def step(j, carry):
        fn(pl.multiple_of(j * tn, tn))
        return carry

    lax.fori_loop(0, n // tn, step, 0)


def _rows_spec(cols, block_col=0):
    return _spec((RC, cols), lambda i: (i, block_col))


def _whole(shape):
    return _spec(shape, lambda i: tuple(0 for _ in shape), single=True)


def _vec(cols):
    return _spec((1, cols), lambda i: (0, 0))


def _zero_at_first(*refs):
    @pl.when(pl.program_id(0) == 0)
    def _():
        for r in refs:
            r[...] = jnp.zeros_like(r)


ANY_SPEC = pl.BlockSpec(memory_space=pl.ANY)


def _arriving(src, dst, sems, starts, rows):
    n, ahead = len(starts), 2
    first = pl.program_id(0) == 0

    def piece(k):
        r0 = starts[0]
        for j in range(1, n):
            r0 = jnp.where(k == j, starts[j], r0)
        at = pl.ds(pl.multiple_of(r0, 16), rows)
        return pltpu.make_async_copy(src.at[at], dst.at[at], sems.at[k])

    @pl.when(first)
    def _():
        for k in range(min(ahead, n)):
            piece(k).start()

    def ready(k):
        k = jnp.asarray(k, jnp.int32)

        @pl.when(first)
        def _():
            piece(k).wait()

            @pl.when(k + ahead < n)
            def _():
                piece(k + ahead).start()

    return ready


IN_RUNS = ((PZ, 0, 1024), (PXBC, 1024, XBC), (PG, 2576, 2048))
IN_TILE = 512
IN_TILE_ROWS = [wrow + IN_TILE * j for _, wrow, width in IN_RUNS for j in range(width // IN_TILE)]


def _in_tiles(fn, before_run=lambda run: None):
    done = 0
    for run, (pcol, wrow, width) in enumerate(IN_RUNS):
        before_run(run)
        def step(j, carry, pcol=pcol, wrow=wrow, done=done):
            fn(pl.multiple_of(pcol + j * IN_TILE, IN_TILE), pl.multiple_of(wrow + j * IN_TILE, 16), done + j)
            return carry

        lax.fori_loop(0, width // IN_TILE, step, 0)
        done += width // IN_TILE


def in_proj(x, meta, wn, w_shards):
    first = NPAD + N_META
    steps = T // RC
    shard = IN_COLS // 8

    def body(x_hbm, meta_ref, wn_ref, g_hbm, o_ref, dt_ref, u_ref, h_ref, wt_hbm, wdt_ref, raw, w_ref, h_scr, g_sems, h_sems, out_sem):
        i = pl.program_id(0)
        slot = i % 2
        shards = [pltpu.make_async_copy(g_hbm.at[j], raw.at[j], g_sems.at[j]) for j in range(8)]
        head = pltpu.make_async_copy(x_hbm.at[pl.ds(0, RC - first)], h_scr.at[0, pl.ds(first, RC - first)], h_sems.at[0])
        put_back = pltpu.make_async_copy(w_ref, wt_hbm, out_sem)

        def rows_of(step):
            return pltpu.make_async_copy(x_hbm.at[pl.ds(pl.multiple_of(step * RC - first, 32), RC)], h_scr.at[step % 2], h_sems.at[step % 2])

        @pl.when(i == 0)
        def _():
            for cp in shards:
                cp.start()
            head.start()
            h_scr[0, 0:NPAD, :] = jnp.zeros((NPAD, D), F32)
            for j in range(8):
                h_scr[0, NPAD:first, 128 * j:128 * j + 128] = meta_ref[j, 0:N_META, :]

        @pl.when(i + 1 < steps)
        def _():
            rows_of(i + 1).start()

        @pl.when(i == 0)
        def _():
            head.wait()

        @pl.when(i > 0)
        def _():
            rows_of(i).wait()

        h_ref[...] = h_scr[slot]
        for r in (0, HALF):
            u_ref[r:r + HALF, :] = _rms(h_scr[slot, r:r + HALF, :], wn_ref[...]).astype(BF)

        def place_shards(run):
            @pl.when(i == 0)
            def _():
                for j in ((0, 1), (2, 3, 4), (5, 6, 7))[run]:
                    shards[j].wait()
                    w_ref[shard * j:shard * (j + 1), :] = raw[j]
                if run == 1:
                    wdt_ref[...] = jnp.zeros_like(wdt_ref)
                    for g in range(2):
                        wdt_ref[128 * g:128 * g + 8, :] = w_ref[2560 + 8 * g:2568 + 8 * g, :]
                if run == 2:
                    put_back.start()

        def tile(pcol, wrow, k):
            o_ref[:, pl.ds(pcol, IN_TILE)] = lax.dot_general(u_ref[...], w_ref[pl.ds(wrow, IN_TILE), :], NT_DIMS,
                                                             preferred_element_type=F32).astype(BF)

        _in_tiles(tile, place_shards)
        dt_ref[...] = lax.dot_general(u_ref[...], wdt_ref[...], NT_DIMS, preferred_element_type=F32)

        @pl.when(i == steps - 1)
        def _():
            put_back.wait()

    return pl.pallas_call(
        body, grid=(steps,), in_specs=[ANY_SPEC, _spec(meta.shape, lambda i: (0, 0, 0)), _vec(D), ANY_SPEC],
        out_specs=[_rows_spec(NP_IN), _rows_spec(256), _rows_spec(D), _rows_spec(D), ANY_SPEC, _spec((256, D), lambda i: (0, 0))],
        out_shape=[jax.ShapeDtypeStruct((T, NP_IN), BF), jax.ShapeDtypeStruct((T, 256), F32), jax.ShapeDtypeStruct((T, D), BF),
                   jax.ShapeDtypeStruct((T, D), F32), jax.ShapeDtypeStruct((IN_COLS, D), BF), jax.ShapeDtypeStruct((256, D), BF)],
        scratch_shapes=[pltpu.VMEM((8, shard, D), BF), pltpu.VMEM((IN_COLS, D), BF), pltpu.VMEM((2, RC, D), F32),
                        pltpu.SemaphoreType.DMA((8,)), pltpu.SemaphoreType.DMA((2,)), pltpu.SemaphoreType.DMA],
        compiler_params=_params(), name="in_proj")(x, meta, wn, w_shards)


def out_proj(yn_ssd, proj, hseq, lru_nw, w_out, h0):
    def body(y_ref, g_ref, h_ref, wn_ref, w_ref, r_ref, o_ref, cat_ref):
        cat_ref[:, 0:SSD_W] = y_ref[...]
        for r in (0, HALF):
            y = _gelu(g_ref[r:r + HALF, :].astype(F32)) * h_ref[r:r + HALF, :]
            cat_ref[r:r + HALF, SSD_W:] = _rms(y, wn_ref[...]).astype(BF)

        def tile(c0):
            o_ref[:, pl.ds(c0, 512)] = r_ref[:, pl.ds(c0, 512)] + jnp.dot(cat_ref[...], w_ref[:, pl.ds(c0, 512)], preferred_element_type=F32)

        _col_tiles(D, 512, tile)

    return pl.pallas_call(
        body, grid=(T // RC,),
        in_specs=[_rows_spec(SSD_W), _rows_spec(LRU_W, PG // LRU_W), _rows_spec(LRU_W), _vec(LRU_W), _whole((SSD_W + LRU_W, D)), _rows_spec(D)],
        out_specs=[_rows_spec(D), _rows_spec(SSD_W + LRU_W)],
        out_shape=[jax.ShapeDtypeStruct((T, D), F32), jax.ShapeDtypeStruct((T, SSD_W + LRU_W), BF)],
        compiler_params=_params(), name="out_proj")(yn_ssd, proj, hseq, lru_nw, w_out, h0)


def out_proj_bwd(dh1_b, w_out, proj, hseq, lru_nw, after):
    def body(d_ref, w_ref, g_ref, h_ref, wn_ref, _after, dy_ref, dh_ref, dg_ref, dw_ref, dl_scr):
        _zero_at_first(dw_ref)

        def tile(c0):
            dy_ref[:, pl.ds(c0, 512)] = lax.dot_general(d_ref[...], w_ref[pl.ds(c0, 512), :], NT_DIMS, preferred_element_type=F32)
            dl_scr[:, pl.ds(c0, 512)] = lax.dot_general(d_ref[...], w_ref[pl.ds(SSD_W + c0, 512), :], NT_DIMS, preferred_element_type=F32)

        _col_tiles(SSD_W, 512, tile)

        for r in (0, HALF):
            g = g_ref[r:r + HALF, :].astype(F32)
            h = h_ref[r:r + HALF, :]
            ge = _gelu(g)
            dy, dw = _rms_bwd(dl_scr[r:r + HALF, :], ge * h, wn_ref[...])
            dw_ref[...] += jnp.sum(dw, axis=0, keepdims=True)
            dh_ref[r:r + HALF, :] = dy * ge
            dg_ref[r:r + HALF, :] = (dy * h * _gelu_grad(g)).astype(BF)

    return pl.pallas_call(
        body, grid=(T // RC,),
        in_specs=[_rows_spec(D), _whole((SSD_W + LRU_W, D)), _rows_spec(LRU_W, PG // LRU_W), _rows_spec(LRU_W), _vec(LRU_W), ANY_SPEC],
        out_specs=[_rows_spec(SSD_W), _rows_spec(LRU_W), _rows_spec(LRU_W), _vec(LRU_W)],
        out_shape=[jax.ShapeDtypeStruct((T, SSD_W), F32), jax.ShapeDtypeStruct((T, LRU_W), F32), jax.ShapeDtypeStruct((T, LRU_W), BF),
                   jax.ShapeDtypeStruct((1, LRU_W), F32)],
        scratch_shapes=[pltpu.VMEM((RC, LRU_W), F32)],
        compiler_params=_params(), name="out_proj_bwd")(dh1_b, w_out, proj, hseq, lru_nw, after)


def in_proj_bwd(dz, dg, dxl, dxbc, ddt, w_t, w_dt, h0, wn, dh1, after, through):
    first = NPAD + N_META

    def body(dz_ref, dg_ref, dxl_ref, dxbc_ref, ddt_ref, w_hbm, wdt_ref, h_ref, wn_ref, r_ref, _after, _in0, _in1, gx_hbm, meta_ref, dw_ref,
             _out0, _out1, du_scr, o_ref, sem, w_ref, w_sems):
        i = pl.program_id(0)
        ready = _arriving(w_hbm, w_ref, w_sems, IN_TILE_ROWS, IN_TILE)
        _zero_at_first(dw_ref)
        du_scr[...] = jnp.dot(ddt_ref[...], wdt_ref[...], preferred_element_type=F32)
        done = 0
        for d_ref, wrow, width in ((dz_ref, 0, 1024), (dxbc_ref, 1024, XBC), (dg_ref, 2576, 1024), (dxl_ref, 3600, 1024)):
            def step(j, carry, d_ref=d_ref, wrow=wrow, done=done):
                c0 = pl.multiple_of(j * IN_TILE, IN_TILE)
                ready(done + j)
                du_scr[...] += jnp.dot(d_ref[:, pl.ds(c0, IN_TILE)], w_ref[pl.ds(pl.multiple_of(wrow + c0, 16), IN_TILE), :],
                                       preferred_element_type=F32)
                return carry

            lax.fori_loop(0, width // IN_TILE, step, 0)
            done += width // IN_TILE
        for r in (0, HALF):
            dh, dw = _rms_bwd(du_scr[r:r + HALF, :], h_ref[r:r + HALF, :], wn_ref[...])
            dw_ref[...] += jnp.sum(dw, axis=0, keepdims=True)
            o_ref[r:r + HALF, :] = dh + r_ref[r:r + HALF, :]

        @pl.when(i == 0)
        def _():
            meta_ref[...] = o_ref[NPAD:first, :]
            head = pltpu.make_async_copy(o_ref.at[pl.ds(first, RC - first)], gx_hbm.at[pl.ds(0, RC - first)], sem)
            head.start()
            head.wait()

        @pl.when(i > 0)
        def _():
            rest = pltpu.make_async_copy(o_ref, gx_hbm.at[pl.ds(pl.multiple_of(i * RC - first, 32), RC)], sem)
            rest.start()
            rest.wait()

    return pl.pallas_call(
        body, grid=(T // RC,),
        in_specs=[_rows_spec(SSD_W), _rows_spec(LRU_W), _rows_spec(LRU_W), _rows_spec(XBC), _rows_spec(256), ANY_SPEC,
                  _whole((256, D)), _rows_spec(D), _vec(D), _rows_spec(D), ANY_SPEC, ANY_SPEC, ANY_SPEC],
        out_specs=[ANY_SPEC, _spec((N_META, D), lambda i: (0, 0)), _vec(D), ANY_SPEC, ANY_SPEC],
        out_shape=[jax.ShapeDtypeStruct((SEQ, D), F32), jax.ShapeDtypeStruct((N_META, D), F32), jax.ShapeDtypeStruct((1, D), F32)]
        + [jax.ShapeDtypeStruct(t.shape, t.dtype) for t in through],
        scratch_shapes=[pltpu.VMEM((RC, D), F32), pltpu.VMEM((RC, D), F32), pltpu.SemaphoreType.DMA,
                        pltpu.VMEM((IN_COLS, D), BF), pltpu.SemaphoreType.DMA((len(IN_TILE_ROWS),))],
        input_output_aliases={11: 3, 12: 4},
        compiler_params=_params(), name="in_proj_bwd")(dz, dg, dxl, dxbc, ddt, w_t, w_dt, h0, wn, dh1, after, *through)


GRAD_TILE = 256


def weight_grad(name, a, u1):
    tm = GRAD_TILE

    def body(a_ref, u_ref, o_ref):
        o_ref[...] = lax.dot_general(a_ref[...], u_ref[...], TN_DIMS, preferred_element_type=F32).astype(BF)

    return pl.pallas_call(
        body, grid=(a.shape[1] // tm,),
        in_specs=[_spec((T, tm), lambda j: (0, j)), _spec((T, D), lambda j: (0, 0), single=True)],
        out_specs=_spec((tm, D), lambda j: (j, 0)),
        out_shape=jax.ShapeDtypeStruct((a.shape[1], D), BF),
        compiler_params=_params(), name=name)(a, u1)


def in_weight_grad(parts, first_rows, ddt, u1):
    tm = GRAD_TILE
    per_row = D // 128
    dt_row, dt_lines = 2560, 8 * per_row
    parts = list(parts) + [ddt]
    first_rows = list(first_rows) + [dt_row]
    tiles = [p.shape[1] // tm for p in parts]
    starts = [sum(tiles[:k]) for k in range(len(parts))]
    last = sum(tiles) - 1

    def body(*refs):
        a_refs, u_ref = refs[:len(parts)], refs[len(parts)]
        o_hbm, mix_scr, stage, sems = refs[len(parts) + 1:]
        step = pl.program_id(0)
        slot = step % 2
        line0 = 0
        for a_ref, start, n, first in zip(a_refs, starts, tiles, first_rows):
            here = (step >= start) & (step < start + n)
            line0 = jnp.where(here, per_row * (first + tm * (step - start)), line0)

            @pl.when(here)
            def _(a_ref=a_ref):
                res = lax.dot_general(a_ref[...], u_ref[...], TN_DIMS, preferred_element_type=F32)
                for q in range(per_row):
                    mix_scr[pl.ds(q, tm, stride=per_row), :] = res[:, 128 * q:128 * q + 128]

        def tile_copy(of_slot, to):
            return pltpu.make_async_copy(stage.at[of_slot], o_hbm.at[pl.ds(to, per_row * tm)], sems.at[of_slot])

        @pl.when(step >= 2)
        def _():
            tile_copy(slot, 0).wait()

        stage[slot] = mix_scr[...].astype(BF)

        @pl.when(step < last)
        def _():
            tile_copy(slot, pl.multiple_of(line0, 128)).start()

        @pl.when(step == last)
        def _():
            halves = [pltpu.make_async_copy(stage.at[slot, pl.ds(128 * per_row * k, dt_lines)],
                                            o_hbm.at[pl.ds(per_row * (dt_row + 8 * k), dt_lines)], sems.at[2 + k]) for k in range(2)]
            for cp in halves:
                cp.start()
            tile_copy(1 - slot, 0).wait()
            for cp in halves:
                cp.wait()

    def tile_of(start, n):
        return lambda j: (0, jnp.clip(j - start, 0, n - 1))

    return pl.pallas_call(
        body, grid=(last + 1,),
        in_specs=[_spec((T, tm), tile_of(s, n)) for s, n in zip(starts, tiles)] + [_spec((T, D), lambda j: (0, 0), single=True)],
        out_specs=ANY_SPEC,
        out_shape=jax.ShapeDtypeStruct((per_row * IN_COLS, 128), BF),
        scratch_shapes=[pltpu.VMEM((per_row * tm, 128), F32), pltpu.VMEM((2, per_row * tm, 128), BF), pltpu.SemaphoreType.DMA((4,))],
        compiler_params=_params(), name="dw_in")(*parts, u1)


def _ssd_chunk_common(row0, dt_ref, b_ref, c_ref, bias, a_neg):
    shape = (Q, Q)
    lane = _lanes(shape)
    sub = _rows(shape)
    live = (_rows(shape, row0) >= NPAD) & (lane < 8)
    dtr = dt_ref[:, :]
    dt = jnp.where(live, _softplus(dtr + bias), 0.0)
    d_a = dt * a_neg
    tri = (sub >= lane).astype(F32)
    cs = jnp.dot(tri, d_a, precision=lax.Precision.HIGHEST, preferred_element_type=F32)
    cs_t = cs.T
    b_f = b_ref[:, :]
    bc = b_f.astype(BF)
    cc = c_ref[:, :].astype(BF)
    cb = lax.dot_general(cc, bc, NT_DIMS, preferred_element_type=F32)
    cs_last = cs[Q - 1:Q, :]
    return dict(lane=lane, sub=sub, live=live, dtr=dtr, dt=dt, cs=cs, cs_t=cs_t, bc=bc, cc=cc, cb=cb, bc_t=b_f.T.astype(BF),
                ecs=jnp.exp(cs), dsm=jnp.exp(cs_last - cs), gam=jnp.exp(cs_last))


def _pair(lane_even, mat, j):
    return jnp.where(lane_even, mat[:, j:j + 1], mat[:, j + 1:j + 2])


def _pair_row(lane_even, mat, j):
    return jnp.where(lane_even[0:1, :], mat[:, j:j + 1], mat[:, j + 1:j + 2])


def _head_decay(cm, j):
    seg = cm["cs"][:, j:j + 1] - cm["cs_t"][j:j + 1, :]
    return jnp.exp(jnp.where(cm["sub"] >= cm["lane"], seg, -jnp.inf))


def _head_decay_t(cm, j):
    seg = cm["cs_t"][j:j + 1, :] - cm["cs"][:, j:j + 1]
    return jnp.exp(jnp.where(cm["lane"] >= cm["sub"], seg, -jnp.inf))


def _conv_window(raw_ref, halo_ref, pad_scr):
    pad_scr[0:HALO, :] = halo_ref[...].astype(F32)[halo_ref.shape[0] - HALO:, :]
    pad_scr[HALO:HALO + Q, :] = raw_ref[...].astype(F32)
    win = pad_scr[...]
    return lambda s: win[HALO:, :] if s == 0 else pltpu.roll(win, s, axis=0)[HALO:, :]


def _xbc_cols(g):
    return slice(512 * g, 512 * g + 512), slice(SSD_W + 128 * g, SSD_W + 128 * g + 128), slice(SSD_W + 256 + 128 * g, SSD_W + 384 + 128 * g)


def ssd_fwd(proj, dt_raw, conv_w, conv_b, heads, norm_w):
    def body(raw_ref, halo_ref, dt_all, z_all, cw_ref, cb_ref, bias_all, alog_all, d_all, nw_all, yn_all, y_all, hp_all,
             h_all, pad_scr, act_scr):
        @pl.when(pl.program_id(0) == 0)
        def _():
            h_all[...] = jnp.zeros_like(h_all)

        pre = _conv(_conv_window(raw_ref, halo_ref, pad_scr), cw_ref[...], cb_ref[...])
        act_scr[...] = pre * _sigmoid(pre)
        for g in range(2):
            wide, thin = slice(512 * g, 512 * g + 512), slice(128 * g, 128 * g + 128)
            xs, bs, cs = _xbc_cols(g)
            group(act_scr.at[:, xs], act_scr.at[:, bs], act_scr.at[:, cs], dt_all.at[:, thin], z_all.at[:, wide], bias_all.at[g],
                  alog_all.at[g], d_all.at[g], nw_all.at[:, wide], yn_all.at[:, wide], y_all.at[:, wide], hp_all.at[g, 0], h_all.at[g])

    def group(x_ref, b_ref, c_ref, dt_ref, z_ref, bias_ref, alog_ref, d_ref, nw_ref, yn_ref, y_ref, hp_ref, h_scr):
        bias = bias_ref[...]
        a_neg = -jnp.exp(alog_ref[...])
        dsk = d_ref[...]
        cm = _ssd_chunk_common(pl.program_id(0) * Q, dt_ref, b_ref, c_ref, bias, a_neg)
        lane_even = cm["lane"] < 64
        for p in range(4):
            je, jo = 2 * p, 2 * p + 1
            xp = x_ref[:, 128 * p:128 * p + 128]
            xdt = xp * _pair(lane_even, cm["dt"], je)
            xdt_b = xdt.astype(BF)
            m_e = (cm["cb"] * _head_decay(cm, je)).astype(BF)
            m_o = (cm["cb"] * _head_decay(cm, jo)).astype(BF)
            zero = jnp.zeros_like(xdt_b)
            yd = (jnp.dot(m_e, jnp.where(lane_even, xdt_b, zero), preferred_element_type=F32)
                  + jnp.dot(m_o, jnp.where(lane_even, zero, xdt_b), preferred_element_type=F32))
            hp = h_scr[p]
            hp_ref[p] = hp
            yo = jnp.dot(cm["cc"], hp.astype(BF), preferred_element_type=F32) * _pair(lane_even, cm["ecs"], je)
            y_ref[:, 128 * p:128 * p + 128] = yd + yo + xp * _pair_row(lane_even, dsk, je)
            st = jnp.dot(cm["bc_t"], (xdt * _pair(lane_even, cm["dsm"], je)).astype(BF), preferred_element_type=F32)
            h_scr[p] = hp * _pair_row(lane_even, cm["gam"], je) + st
        zc = z_ref[:, :].astype(F32)
        gated = y_ref[:, :] * (zc * _sigmoid(zc))
        yn_ref[:, :] = _rms(gated, nw_ref[...]).astype(BF)

    par = [_spec((None, 2, 1, 128), lambda c, k=k: (k, 0, 0, 0)) for k in range(3)]
    wide = _spec((Q, SSD_W), lambda c: (c, 0))
    xbc = PXBC // XBC
    halo = 2 * HALO
    return pl.pallas_call(
        body, grid=(NCH,),
        in_specs=[_spec((Q, XBC), lambda c: (c, xbc)), _spec((halo, XBC), lambda c: (jnp.maximum(c * (Q // halo) - 1, 0), xbc)),
                  _spec((Q, 256), lambda c: (c, 0)), wide, _spec((4, XBC), lambda c: (0, 0)), _spec((1, XBC), lambda c: (0, 0)),
                  *par, _spec((1, SSD_W), lambda c: (0, 0))],
        out_specs=[wide, wide, _spec((2, 1, 4, 128, 128), lambda c: (0, c, 0, 0, 0))],
        out_shape=[jax.ShapeDtypeStruct((T, SSD_W), BF), jax.ShapeDtypeStruct((T, SSD_W), F32),
                   jax.ShapeDtypeStruct((2, NCH, 4, 128, 128), F32)],
        scratch_shapes=[pltpu.VMEM((2, 4, 128, 128), F32), pltpu.VMEM((Q + HALO, XBC), F32), pltpu.VMEM((Q, XBC), F32)],
        compiler_params=_params(), name="ssd_fwd")(proj, proj, dt_raw, proj, conv_w, conv_b, heads, heads, heads, norm_w)


def ssd_bwd(dyn, proj, dt_raw, conv_w, conv_b, y_pre, h_prev, heads, norm_w, after):
    def body(dyn_all, raw_ref, halo_ref, dt_all, z_all, y_all, hp_all, cw_ref, cb_ref, bias_all, alog_all, d_all, nw_all, _after,
             dz_all, dxbc_ref, ddt_all, dpar_all, dnw_all, dcw_ref, dcb_ref, dh_all, acc_all, pad_scr, act_scr, dsilu_scr, dact_scr, dpad_scr, sums_scr):
        @pl.when(pl.program_id(0) == 0)
        def _():
            dh_all[...] = jnp.zeros_like(dh_all)
            acc_all[...] = jnp.zeros_like(acc_all)
            dnw_all[...] = jnp.zeros_like(dnw_all)
            dcw_ref[...] = jnp.zeros_like(dcw_ref)
            dcb_ref[...] = jnp.zeros_like(dcb_ref)
            dpad_scr[Q:Q + HALO, :] = jnp.zeros((HALO, XBC), F32)

        back = _conv_window(raw_ref, halo_ref, pad_scr)
        pre = _conv(back, cw_ref[...], cb_ref[...])
        sg = _sigmoid(pre)
        act_scr[...] = pre * sg
        dsilu_scr[...] = sg * (1.0 + pre * (1.0 - sg))
        for g in range(2):
            wide, thin = slice(512 * g, 512 * g + 512), slice(128 * g, 128 * g + 128)
            xs, bs, cs = _xbc_cols(g)
            group(dyn_all.at[:, wide], act_scr.at[:, xs], act_scr.at[:, bs], act_scr.at[:, cs], dt_all.at[:, thin], z_all.at[:, wide],
                  y_all.at[:, wide], hp_all.at[g, 0], bias_all.at[g], alog_all.at[g], d_all.at[g], nw_all.at[:, wide],
                  dz_all.at[:, wide], dact_scr.at[:, xs], dact_scr.at[:, bs], dact_scr.at[:, cs], ddt_all.at[:, thin], dpar_all.at[g],
                  dnw_all.at[:, wide], dh_all.at[g], acc_all.at[g], sums_scr)
        dpre = dact_scr[...] * dsilu_scr[...]
        dcw, dcb = _conv_bwd_w(dpre, back)
        dcw_ref[...] += dcw
        dcb_ref[...] += dcb
        dpad_scr[0:Q, :] = dpre
        win = dpad_scr[...]
        dxbc_ref[...] = _conv_bwd_x(lambda s: win[:Q, :] if s == 0 else pltpu.roll(win, Q + HALO - s, axis=0)[:Q, :], cw_ref[...]).astype(BF)
        dpad_scr[Q:Q + HALO, :] = dpre[0:HALO, :]

    def group(dyn_ref, x_ref, b_ref, c_ref, dt_ref, z_ref, y_ref, hp_ref, bias_ref, alog_ref, d_ref, nw_ref,
              dz_ref, dx_ref, db_ref, dc_ref, ddt_ref, dpar_ref, dnw_ref, dh_scr, acc_scr, sums):
        ci = pl.program_id(0)
        bias = bias_ref[...]
        a_neg = -jnp.exp(alog_ref[...])
        dsk = d_ref[...]
        cm = _ssd_chunk_common((NCH - 1 - ci) * Q, dt_ref, b_ref, c_ref, bias, a_neg)
        lane, sub = cm["lane"], cm["sub"]
        lane_even = lane < 64
        cc_t = c_ref[:, :].T.astype(BF)
        cb_t = lax.dot_general(cm["bc"], cm["cc"], NT_DIMS, preferred_element_type=F32)
        zc = z_ref[:, :].astype(F32)
        yc = y_ref[:, :]
        sg = _sigmoid(zc)
        sz = zc * sg
        dgated, dnw = _rms_bwd(dyn_ref[:, :], yc * sz, nw_ref[...])
        dnw_ref[...] += jnp.sum(dnw, axis=0, keepdims=True)
        dz_ref[:, :] = (dgated * yc * (sg * (1.0 + zc * (1.0 - sg)))).astype(BF)
        dx_ref[:, :] = dgated * sz
        db_ref[:, :] = jnp.zeros((Q, Q), F32)
        dc_ref[:, :] = jnp.zeros((Q, Q), F32)
        sums[...] = jnp.zeros_like(sums)
        for p in range(4):
            je, jo = 2 * p, 2 * p + 1
            xp = x_ref[:, 128 * p:128 * p + 128]
            dy = dx_ref[:, 128 * p:128 * p + 128]
            dt_p = _pair(lane_even, cm["dt"], je)
            xdt = xp * dt_p
            xdt_b = xdt.astype(BF)
            dy_b = dy.astype(BF)
            zero = jnp.zeros_like(dy_b)
            hp = hp_ref[p]
            hp_b = hp.astype(BF)
            dh = dh_scr[p]
            dh_b = dh.astype(BF)
            acc_scr[p:p + 1, :] += jnp.sum(dy * xp, axis=0, keepdims=True)
            dxp = dy * _pair_row(lane_even, dsk, je)
            e_p = _pair(lane_even, cm["ecs"], je)
            g_p = jnp.dot(cm["cc"], hp_b, preferred_element_type=F32)
            dg_b = (dy * e_p).astype(BF)
            de = dy * g_p * e_p
            dc_ref[:, :] += lax.dot_general(dg_b, hp_b, NT_DIMS, preferred_element_type=F32)
            dh_in = jnp.dot(cc_t, dg_b, preferred_element_type=F32)
            ds_p = _pair(lane_even, cm["dsm"], je)
            r_p = jnp.dot(cm["bc"], dh_b, preferred_element_type=F32)
            dxdt = r_p * ds_p
            tt = r_p * xdt * ds_p
            db_ref[:, :] += lax.dot_general((xdt * ds_p).astype(BF), dh_b, NT_DIMS, preferred_element_type=F32)
            dgam_m = jnp.sum(dh * hp, axis=0, keepdims=True)
            for j, even in ((je, True), (jo, False)):
                sel = lane_even if even else jnp.logical_not(lane_even)
                dy_j = jnp.where(sel, dy_b, zero)
                l_j = _head_decay(cm, j)
                l_jt = _head_decay_t(cm, j)
                m_j = cm["cb"] * l_j
                m_jt = cb_t * l_jt
                dm = lax.dot_general(dy_j, xdt_b, NT_DIMS, preferred_element_type=F32)
                dm_t = lax.dot_general(xdt_b, dy_j, NT_DIMS, preferred_element_type=F32)
                dxdt = dxdt + jnp.dot(m_jt.astype(BF), dy_j, preferred_element_type=F32)
                sums[0] += dm * l_j
                sums[1] += dm_t * l_jt
                t_j = jnp.where(sel, tt, 0.0)
                col = jnp.sum(dm * m_j - dm_t * m_jt + (jnp.where(sel, de, 0.0) - t_j), axis=1, keepdims=True)
                gam_j = cm["gam"][:, j:j + 1]
                last = (jnp.sum(jnp.sum(t_j, axis=0, keepdims=True), axis=1, keepdims=True)
                        + jnp.sum(jnp.where(sel[0:1, :], dgam_m, 0.0), axis=1, keepdims=True) * gam_j)
                col = col + jnp.where(sub[:, 0:1] == Q - 1, last, 0.0)
                sums[2] += jnp.where(lane == j, col, 0.0)
            dh_scr[p] = dh_in + dh * _pair_row(lane_even, cm["gam"], je)
            dx_ref[:, 128 * p:128 * p + 128] = dxp + dxdt * dt_p
            dd = dxdt * xp
            sums[3] += (jnp.where(lane == je, jnp.sum(jnp.where(lane_even, dd, 0.0), axis=1, keepdims=True), 0.0)
                        + jnp.where(lane == jo, jnp.sum(jnp.where(lane_even, 0.0, dd), axis=1, keepdims=True), 0.0))
        dc_ref[:, :] += jnp.dot(sums[0].astype(BF), cm["bc"], preferred_element_type=F32)
        db_ref[:, :] += jnp.dot(sums[1].astype(BF), cm["cc"], preferred_element_type=F32)
        tri_t = (sub <= lane).astype(F32)
        dd_a = jnp.dot(tri_t, sums[2], precision=lax.Precision.HIGHEST, preferred_element_type=F32)
        ddt = sums[3] + dd_a * a_neg
        acc_scr[5:6, :] += jnp.sum(dd_a * cm["dt"], axis=0, keepdims=True)
        draw = jnp.where(cm["live"], ddt * _sigmoid_gate(cm["dtr"] + bias), 0.0)
        acc_scr[4:5, :] += jnp.sum(draw, axis=0, keepdims=True)
        ddt_ref[:, :] = draw.astype(BF)

        @pl.when(ci == NCH - 1)
        def _():
            lane1 = _lanes((1, 128))
            dd = jnp.zeros((1, 128), F32)
            for p in range(4):
                row = acc_scr[p:p + 1, :]
                dd = dd + jnp.where(lane1 == 2 * p, jnp.sum(jnp.where(lane1 < 64, row, 0.0), axis=1, keepdims=True), 0.0)
                dd = dd + jnp.where(lane1 == 2 * p + 1, jnp.sum(jnp.where(lane1 < 64, 0.0, row), axis=1, keepdims=True), 0.0)
            dpar_ref[...] = jnp.concatenate([acc_scr[4:5, :], acc_scr[5:6, :] * a_neg, dd, jnp.zeros((5, 128), F32)], axis=0)

    par = [_spec((None, 2, 1, 128), lambda c, k=k: (k, 0, 0, 0)) for k in range(3)]
    wide = _spec((Q, SSD_W), lambda c: (NCH - 1 - c, 0))
    thin = _spec((Q, 256), lambda c: (NCH - 1 - c, 0))
    vec = _spec((1, SSD_W), lambda c: (0, 0))
    xbc = PXBC // XBC
    halo = 2 * HALO
    chunk = pltpu.VMEM((Q, XBC), F32)
    padded = pltpu.VMEM((Q + HALO, XBC), F32)
    return pl.pallas_call(
        body, grid=(NCH,),
        in_specs=[wide, _spec((Q, XBC), lambda c: (NCH - 1 - c, xbc)),
                  _spec((halo, XBC), lambda c: (jnp.maximum((NCH - 1 - c) * (Q // halo) - 1, 0), xbc)), thin, wide, wide,
                  _spec((2, 1, 4, 128, 128), lambda c: (0, NCH - 1 - c, 0, 0, 0)), _spec((4, XBC), lambda c: (0, 0)),
                  _spec((1, XBC), lambda c: (0, 0)), *par, vec, ANY_SPEC],
        out_specs=[wide, _spec((Q, XBC), lambda c: (NCH - 1 - c, 0)), thin, _spec((2, 8, 128), lambda c: (0, 0, 0)), vec,
                   _spec((4, XBC), lambda c: (0, 0)), _spec((1, XBC), lambda c: (0, 0))],
        out_shape=[jax.ShapeDtypeStruct((T, SSD_W), BF), jax.ShapeDtypeStruct((T, XBC), BF), jax.ShapeDtypeStruct((T, 256), BF),
                   jax.ShapeDtypeStruct((2, 8, 128), F32), jax.ShapeDtypeStruct((1, SSD_W), F32), jax.ShapeDtypeStruct((4, XBC), F32),
                   jax.ShapeDtypeStruct((1, XBC), F32)],
        scratch_shapes=[pltpu.VMEM((2, 4, 128, 128), F32), pltpu.VMEM((2, 8, 128), F32), padded, chunk, chunk, chunk, padded,
                        pltpu.VMEM((4, Q, Q), F32)],
        compiler_params=_params(), name="ssd_bwd")(dyn, proj, proj, dt_raw, proj, y_pre, h_prev, conv_w, conv_b, heads, heads, heads, norm_w, after)


def _lru_gates(back, cw, cb, wa, ba, wx, bx, lam):
    xr = _conv(back, cw, cb)
    xr_b = xr.astype(BF)
    r = _sigmoid_gate(jnp.dot(xr_b, wa, preferred_element_type=F32) + ba)
    i = _sigmoid_gate(jnp.dot(xr_b, wx, preferred_element_type=F32) + bx)
    sp = _softplus(-lam)
    la = (-LRU_C) * r * sp
    a = jnp.exp(la)
    mult2 = -jnp.tanh(la) * (a * a + 1.0)
    return xr, xr_b, r, i, sp, a, jnp.sqrt(mult2), mult2


SEG_LEN = 68
SEGS = T // SEG_LEN


def _seg_rows(j, k, off=0):
    return pl.ds(off + j * 8 * SEG_LEN + k, 8, stride=SEG_LEN)


def _segmented_scan(mul_ref, mul_row0, add_ref, out_ref, loc_scr, prod_scr, carry_scr, reverse):
    groups = SEGS // 8
    off = mul_row0 + (1 if reverse else 0)

    def local(i, carry):
        k = SEG_LEN - 1 - i if reverse else i
        new = []
        for j in range(groups):
            h, p = carry[2 * j], carry[2 * j + 1]
            m = mul_ref[_seg_rows(j, k, off), :]
            h = m * h + add_ref[_seg_rows(j, k), :]
            p = m * p
            loc_scr[_seg_rows(j, k), :] = h
            prod_scr[_seg_rows(j, k), :] = p
            new += [h, p]
        return tuple(new)

    lax.fori_loop(0, SEG_LEN, local, (jnp.zeros((8, 128), F32), jnp.ones((8, 128), F32)) * groups)

    def chain(i, c):
        s = SEGS - 1 - i if reverse else i
        carry_scr[pl.ds(s, 1), :] = c
        edge = s * SEG_LEN + (0 if reverse else SEG_LEN - 1)
        return loc_scr[pl.ds(edge, 1), :] + prod_scr[pl.ds(edge, 1), :] * c

    lax.fori_loop(0, SEGS, chain, jnp.zeros((1, 128), F32))

    def fold(k, carry):
        for j in range(groups):
            rows = _seg_rows(j, k)
            out_ref[rows, :] = loc_scr[rows, :] + prod_scr[rows, :] * carry_scr[8 * j:8 * j + 8, :]
        return carry

    lax.fori_loop(0, SEG_LEN, fold, 0)


LRU_CW_SPEC = _spec((None, 8, 128), lambda c: (c, 3, 0))


def lru_fwd(proj, cw, cb, wa2, ba, wx2, bx, lam, after):
    def body(x_ref, cw_ref, cb_ref, wa_ref, ba_ref, wx_ref, bx_ref, lam_ref, _after, h_ref, a_ref, xpad, u_scr, loc_scr, prod_scr, carry_scr):
        _fill_padded(xpad, x_ref)

        def chunk(r0):
            xr, _, _, i, _, a, mult, _ = _lru_gates(_back(xpad, r0), cw_ref[0:4, :], cb_ref[...], wa_ref[0], ba_ref[...], wx_ref[0], bx_ref[...],
                                                 lam_ref[...])
            a_ref[pl.ds(r0, Q), :] = a
            u_scr[pl.ds(r0, Q), :] = jnp.where(_rows(a.shape, r0) >= NPAD, mult * (i * xr), 0.0)

        _chunks(chunk, unrolled=True)
        _segmented_scan(a_ref, 0, u_scr, h_ref, loc_scr, prod_scr, carry_scr, reverse=False)

    c0 = PXL // 128
    vec = _spec((1, 128), lambda c: (0, c))
    mat = _spec((1, 128, 128), lambda c: (c, 0, 0))
    seq = pltpu.VMEM((T, 128), F32)
    return pl.pallas_call(
        body, grid=(8,),
        in_specs=[_spec((T, 128), lambda c: (0, c0 + c)), LRU_CW_SPEC, vec, mat, vec, mat, vec, vec, ANY_SPEC],
        out_specs=[_spec((T, 128), lambda c: (0, c)), _spec((T, 128), lambda c: (0, c))],
        out_shape=[jax.ShapeDtypeStruct((T, LRU_W), F32), jax.ShapeDtypeStruct((T, LRU_W), F32)],
        scratch_shapes=[pltpu.VMEM((T + 2 * HALO, 128), F32), seq, seq, seq, pltpu.VMEM((SEGS, 128), F32)],
        compiler_params=_params(), name="lru_fwd")(proj, cw, cb, wa2, ba, wx2, bx, lam, after)


def lru_bwd(dh_out, a, hseq, proj, cw, cb, wa2, ba, wx2, bx, lam):
    def body(d_ref, a_ref, h_ref, x_ref, cw_ref, cb_ref, wa_ref, ba_ref, wx_ref, bx_ref, lam_ref,
             dx_ref, dcw_ref, dcb_ref, dwa_ref, dba_ref, dwx_ref, dbx_ref, dlam_ref, xpad, hpad, dpad, dh_ref, loc_scr, prod_scr, carry_scr):
        _fill_padded(dpad, a_ref)
        _segmented_scan(dpad, HALO, d_ref, dh_ref, loc_scr, prod_scr, carry_scr, reverse=True)
        _fill_padded(xpad, x_ref)
        _fill_padded(hpad, h_ref)
        dpad[0:HALO, :] = jnp.zeros((HALO, 128), F32)
        dpad[T + HALO:T + 2 * HALO, :] = jnp.zeros((HALO, 128), F32)
        for ref in (dcw_ref, dcb_ref, dwa_ref, dba_ref, dwx_ref, dbx_ref, dlam_ref):
            ref[...] = jnp.zeros_like(ref)
        lam = lam_ref[...]

        def first(r0):
            back = _back(xpad, r0)
            xr, xr_b, r, i, sp, a, mult, mult2 = _lru_gates(back, cw_ref[0:4, :], cb_ref[...], wa_ref[0], ba_ref[...], wx_ref[0], bx_ref[...], lam)
            dh = dh_ref[pl.ds(r0, Q), :]
            da = dh * _back(hpad, r0)(1)
            du = jnp.where(_rows(dh.shape, r0) >= NPAD, dh, 0.0)
            dmult = du * (i * xr)
            di = du * (mult * xr)
            dxr = du * (mult * i)
            dla = da * a - dmult * (a * a) * lax.rsqrt(mult2)
            dr = dla * ((-LRU_C) * sp)
            dlam_ref[...] += jnp.sum(dla * ((-LRU_C) * r), axis=0, keepdims=True)
            dpr = dr * r * (1.0 - r)
            dpi = di * i * (1.0 - i)
            dba_ref[...] += jnp.sum(dpr, axis=0, keepdims=True)
            dbx_ref[...] += jnp.sum(dpi, axis=0, keepdims=True)
            dpr_b = dpr.astype(BF)
            dpi_b = dpi.astype(BF)
            dxr = (dxr + lax.dot_general(dpr_b, wa_ref[0], NT_DIMS, preferred_element_type=F32)
                   + lax.dot_general(dpi_b, wx_ref[0], NT_DIMS, preferred_element_type=F32))
            dwa_ref[0] += lax.dot_general(xr_b, dpr_b, TN_DIMS, preferred_element_type=F32)
            dwx_ref[0] += lax.dot_general(xr_b, dpi_b, TN_DIMS, preferred_element_type=F32)
            dpad[pl.ds(r0 + HALO, Q), :] = dxr
            dcw, dcb = _conv_bwd_w(dxr, back)
            dcw_ref[...] += dcw
            dcb_ref[...] += dcb

        _chunks(first, unrolled=True)
        dlam_ref[...] = -dlam_ref[...] * _sigmoid_gate(-lam)

        def second(r0):
            dx_ref[pl.ds(r0, Q), :] = _conv_bwd_x(_ahead(dpad, r0), cw_ref[0:4, :]).astype(BF)

        _chunks(second)

    c0 = PXL // 128
    vec = _spec((1, 128), lambda c: (0, c))
    mat = _spec((1, 128, 128), lambda c: (c, 0, 0))
    col = _spec((T, 128), lambda c: (0, c))
    vshape = jax.ShapeDtypeStruct((1, LRU_W), F32)
    mshape = jax.ShapeDtypeStruct((8, 128, 128), F32)
    pad = pltpu.VMEM((T + 2 * HALO, 128), F32)
    seq = pltpu.VMEM((T, 128), F32)
    return pl.pallas_call(
        body, grid=(8,),
        in_specs=[col, col, col, _spec((T, 128), lambda c: (0, c0 + c)), LRU_CW_SPEC, vec, mat, vec, mat, vec, vec],
        out_specs=[col, _spec((4, 128), lambda c: (0, c)), vec, mat, vec, mat, vec, vec],
        out_shape=[jax.ShapeDtypeStruct((T, LRU_W), BF), jax.ShapeDtypeStruct((4, LRU_W), F32), vshape, mshape, vshape, mshape, vshape, vshape],
        scratch_shapes=[pad, pad, pad, seq, seq, seq, pltpu.VMEM((SEGS, 128), F32)],
        compiler_params=_params(), name="lru_bwd")(dh_out, a, hseq, proj, cw, cb, wa2, ba, wx2, bx, lam)


FF_TILE = 256
FF_TILE_ROWS = list(range(0, D_FF, FF_TILE))


def gate_up(h1, wn, w_gate, w_up):
    def body(h_ref, wn_ref, wg_hbm, wu_hbm, gt_ref, up_ref, act_ref, u_ref, wg_ref, wu_ref, wg_sems, wu_sems):
        gate_ready = _arriving(wg_hbm, wg_ref, wg_sems, FF_TILE_ROWS, FF_TILE)
        up_ready = _arriving(wu_hbm, wu_ref, wu_sems, FF_TILE_ROWS, FF_TILE)
        for r in (0, HALF):
            u_ref[r:r + HALF, :] = _rms(h_ref[r:r + HALF, :], wn_ref[...]).astype(BF)

        def tile(c0):
            cols = pl.ds(c0, FF_TILE)
            gate_ready(c0 // FF_TILE)
            up_ready(c0 // FF_TILE)
            gt = lax.dot_general(u_ref[...], wg_ref[cols, :], NT_DIMS, preferred_element_type=F32)
            up = lax.dot_general(u_ref[...], wu_ref[cols, :], NT_DIMS, preferred_element_type=F32)
            gt_ref[:, cols] = gt.astype(BF)
            up_ref[:, cols] = up.astype(BF)
            act_ref[:, cols] = (gt * _sigmoid(gt) * up).astype(BF)

        _col_tiles(D_FF, FF_TILE, tile)

    big = jax.ShapeDtypeStruct((T, D_FF), BF)
    return pl.pallas_call(
        body, grid=(T // RC,), in_specs=[_rows_spec(D), _vec(D), ANY_SPEC, ANY_SPEC],
        out_specs=[_rows_spec(D_FF), _rows_spec(D_FF), _rows_spec(D_FF), _rows_spec(D)],
        out_shape=[big, big, big, jax.ShapeDtypeStruct((T, D), BF)],
        scratch_shapes=[pltpu.VMEM((D_FF, D), BF)] * 2 + [pltpu.SemaphoreType.DMA((len(FF_TILE_ROWS),))] * 2,
        compiler_params=_params(), name="gate_up")(h1, wn, w_gate, w_up)


def down_loss(act, w_down, h1, target, wf):
    first = NPAD + N_META

    def body(a_ref, w_ref, r_ref, t_hbm, wf_ref, d_ref, db_ref, l_ref, dw_ref, h_scr, t_ref, t_sem):
        i = pl.program_id(0)
        _zero_at_first(l_ref, dw_ref)
        head = pltpu.make_async_copy(t_hbm.at[pl.ds(0, RC - first)], t_ref.at[pl.ds(first, RC - first)], t_sem)
        rest = pltpu.make_async_copy(t_hbm.at[pl.ds(pl.multiple_of(jnp.maximum(i * RC - first, 0), 32), RC)], t_ref, t_sem)

        @pl.when(i == 0)
        def _():
            t_ref[0:first, :] = jnp.zeros((first, D), F32)
            head.start()

        @pl.when(i > 0)
        def _():
            rest.start()

        def tile(c0):
            cols = pl.ds(c0, 512)
            h_scr[:, cols] = r_ref[:, cols] + jnp.dot(a_ref[...], w_ref[:, cols], preferred_element_type=F32)

        _col_tiles(D, 512, tile)

        @pl.when(i == 0)
        def _():
            head.wait()

        @pl.when(i > 0)
        def _():
            rest.wait()

        for r in (0, HALF):
            h = h_scr[r:r + HALF, :]
            live = _rows((HALF, D), i * RC + r) >= first
            err = jnp.where(live, _rms(h, wf_ref[...]) - t_ref[r:r + HALF, :], 0.0)
            l_ref[...] += 0.5 * jnp.sum(jnp.sum(err * err, axis=1, keepdims=True) * (1.0 / D), axis=0, keepdims=True)
            dh, dw = _rms_bwd(err * (1.0 / D), h, wf_ref[...])
            dw_ref[...] += jnp.sum(dw, axis=0, keepdims=True)
            d_ref[r:r + HALF, :] = dh
            db_ref[r:r + HALF, :] = dh.astype(BF)

    return pl.pallas_call(
        body, grid=(T // RC,),
        in_specs=[_rows_spec(D_FF), _whole((D_FF, D)), _rows_spec(D), pl.BlockSpec(memory_space=pl.ANY), _vec(D)],
        out_specs=[_rows_spec(D), _rows_spec(D), _spec((1, 128), lambda i: (0, 0)), _vec(D)],
        out_shape=[jax.ShapeDtypeStruct((T, D), F32), jax.ShapeDtypeStruct((T, D), BF), jax.ShapeDtypeStruct((1, 128), F32),
                   jax.ShapeDtypeStruct((1, D), F32)],
        scratch_shapes=[pltpu.VMEM((RC, D), F32), pltpu.VMEM((RC, D), F32), pltpu.SemaphoreType.DMA],
        compiler_params=_params(), name="down_loss")(act, w_down, h1, target, wf)


def swiglu_bwd(dh2_b, w_down, gt, up, act, u2):
    tn = 256

    def body(d_hbm, u_hbm, w_ref, gt_ref, up_ref, act_ref, dg_ref, du_ref, gd_ref, gg_ref, gu_ref, d_ref, u_ref, d_sems, u_sems):
        chunks = list(range(0, T, RC))
        d_ready = _arriving(d_hbm, d_ref, d_sems, chunks, RC)
        u_ready = _arriving(u_hbm, u_ref, u_sems, chunks, RC)

        def rows(r0):
            part = pl.ds(r0, RC)
            d_ready(r0 // RC)
            dact = lax.dot_general(d_ref[part, :], w_ref[...], NT_DIMS, preferred_element_type=F32)
            gt_ = gt_ref[part, :].astype(F32)
            up_ = up_ref[part, :].astype(F32)
            sg = _sigmoid(gt_)
            dg_ref[part, :] = (dact * up_ * (sg * (1.0 + gt_ * (1.0 - sg)))).astype(BF)
            du_ref[part, :] = (dact * (gt_ * sg)).astype(BF)

        _col_tiles(T, RC, rows)
        for k in range(len(chunks)):
            u_ready(k)
        gd_ref[...] = lax.dot_general(act_ref[...], d_ref[...], TN_DIMS, preferred_element_type=F32).astype(BF)
        gg_ref[...] = lax.dot_general(dg_ref[...], u_ref[...], TN_DIMS, preferred_element_type=F32).astype(BF)
        gu_ref[...] = lax.dot_general(du_ref[...], u_ref[...], TN_DIMS, preferred_element_type=F32).astype(BF)

    cols = _spec((T, tn), lambda j: (0, j))
    wrow = _spec((tn, D), lambda j: (j, 0))
    big = jax.ShapeDtypeStruct((T, D_FF), BF)
    grad = jax.ShapeDtypeStruct((D_FF, D), BF)
    return pl.pallas_call(
        body, grid=(D_FF // tn,), in_specs=[ANY_SPEC, ANY_SPEC, wrow, cols, cols, cols],
        out_specs=[cols, cols, wrow, wrow, wrow], out_shape=[big, big, grad, grad, grad],
        scratch_shapes=[pltpu.VMEM((T, D), BF)] * 2 + [pltpu.SemaphoreType.DMA((T // RC,))] * 2,
        compiler_params=_params(), name="swiglu_bwd")(dh2_b, u2, w_down, gt, up, act)


def gate_up_bwd(dgt, dup, w_gate, w_up, h1, wn, dh2):
    def body(dg_ref, du_ref, wg_hbm, wu_hbm, h_ref, wn_ref, r_ref, d_ref, db_ref, dw_ref, du_scr, wg_ref, wu_ref, wg_sems, wu_sems):
        gate_ready = _arriving(wg_hbm, wg_ref, wg_sems, FF_TILE_ROWS, FF_TILE)
        up_ready = _arriving(wu_hbm, wu_ref, wu_sems, FF_TILE_ROWS, FF_TILE)
        _zero_at_first(dw_ref)

        du_scr[...] = jnp.zeros_like(du_scr)

        def tile(c0):
            k = pl.ds(c0, FF_TILE)
            gate_ready(c0 // FF_TILE)
            up_ready(c0 // FF_TILE)
            du_scr[...] += (jnp.dot(dg_ref[:, k], wg_ref[k, :], preferred_element_type=F32)
                            + jnp.dot(du_ref[:, k], wu_ref[k, :], preferred_element_type=F32))

        _col_tiles(D_FF, FF_TILE, tile)
        for r in (0, HALF):
            dh, dw = _rms_bwd(du_scr[r:r + HALF, :], h_ref[r:r + HALF, :], wn_ref[...])
            dw_ref[...] += jnp.sum(dw, axis=0, keepdims=True)
            dh = dh + r_ref[r:r + HALF, :]
            d_ref[r:r + HALF, :] = dh
            db_ref[r:r + HALF, :] = dh.astype(BF)

    return pl.pallas_call(
        body, grid=(T // RC,),
        in_specs=[_rows_spec(D_FF), _rows_spec(D_FF), ANY_SPEC, ANY_SPEC, _rows_spec(D), _vec(D), _rows_spec(D)],
        out_specs=[_rows_spec(D), _rows_spec(D), _vec(D)],
        out_shape=[jax.ShapeDtypeStruct((T, D), F32), jax.ShapeDtypeStruct((T, D), BF), jax.ShapeDtypeStruct((1, D), F32)],
        scratch_shapes=[pltpu.VMEM((RC, D), F32)] + [pltpu.VMEM((D_FF, D), BF)] * 2 + [pltpu.SemaphoreType.DMA((len(FF_TILE_ROWS),))] * 2,
        compiler_params=_params(), name="gate_up_bwd")(dgt, dup, w_gate, w_up, h1, wn, dh2)


def _adamw(w, g, m, v):
    m = ADAM_B1 * m + (1.0 - ADAM_B1) * g
    v = ADAM_B2 * v + (1.0 - ADAM_B2) * (g * g)
    m_hat = m / (1.0 - ADAM_B1 ** ADAM_STEP)
    v_hat = v / (1.0 - ADAM_B2 ** ADAM_STEP)
    delta = -ADAM_LR * (m_hat / (jnp.sqrt(v_hat) + ADAM_EPS) + ADAM_WD * w)
    return delta, m, v


def adamw_shards(name, recvs, ws, ms, vs):
    n = len(ws)

    def body(*refs):
        ins, outs = refs[:4 * n], refs[4 * n:]
        for k in range(n):
            p_ref, w_ref, m_ref, v_ref = ins[k], ins[n + k], ins[2 * n + k], ins[3 * n + k]
            g = p_ref[0].astype(F32)
            for s in range(1, 8):
                g = g + p_ref[s].astype(F32)
            outs[4 * k][...] = g
            outs[4 * k + 1][...], outs[4 * k + 2][...], outs[4 * k + 3][...] = _adamw(w_ref[...], g, m_ref[...], v_ref[...])

    tiles = [_spec((w.shape[0] // 2, w.shape[1]), lambda i: (i, 0)) for w in ws]
    recv_tiles = [_spec((8, w.shape[0] // 2, w.shape[1]), lambda i: (0, i, 0)) for w in ws]
    res = pl.pallas_call(
        body, grid=(2,), in_specs=recv_tiles + tiles * 3,
        out_specs=[t for t in tiles for _ in range(4)],
        out_shape=[jax.ShapeDtypeStruct(w.shape, F32) for w in ws for _ in range(4)],
        compiler_params=_params(), name=name)(*recvs, *ws, *ms, *vs)
    return [list(res[4 * k:4 * k + 4]) for k in range(n)]


def adamw_w_in(recv, w, m, v, after):
    rows = 34
    per_row = D // 128

    def body(p_ref, w_ref, m_ref, v_ref, _after, g_ref, d_ref, mo_ref, vo_ref):
        def chunk(c, carry):
            lines = pl.ds(pl.multiple_of(c * per_row * rows, 16), per_row * rows)
            g = p_ref[0, lines, :].astype(F32)
            for s in range(1, 8):
                g = g + p_ref[s, lines, :].astype(F32)
            g = g.reshape(rows, per_row, 128)
            part = pl.ds(c * rows, rows)
            g_ref[part] = g
            d_ref[part], mo_ref[part], vo_ref[part] = _adamw(w_ref[part], g, m_ref[part], v_ref[part])
            return carry

        lax.fori_loop(0, w.shape[0] // rows, chunk, 0)

    shape = jax.ShapeDtypeStruct(w.shape, F32)
    whole = pl.BlockSpec(memory_space=pltpu.VMEM)
    return pl.pallas_call(body, out_shape=[shape] * 4, in_specs=[whole] * 4 + [ANY_SPEC], compiler_params=_params(0),
                          name="adamw_w_in")(recv, w, m, v, after)


def sum_slabs(recv):
    def body(p_ref, o_ref):
        g = p_ref[0]
        for s in range(1, 8):
            g = g + p_ref[s]
        for s in range(8):
            o_ref[s] = g

    return pl.pallas_call(body, out_shape=jax.ShapeDtypeStruct(recv.shape, F32), compiler_params=_params(0), name="sum_slabs")(recv)


SIMPLE = [("norm1_w", 1024), ("ssd_conv_b", 1536), ("ssd_dt_bias", 16), ("ssd_a_log", 16), ("ssd_d", 16), ("ssd_norm_w", 1024),
          ("lru_conv_b", 1024), ("lru_ba", 1024), ("lru_bx", 1024), ("lru_lambda", 1024), ("lru_norm_w", 1024), ("norm2_w", 1024),
          ("final_norm_w", 1024)]
SPECIAL = ["lru_wa", "lru_wx", "meta_tokens", "ssd_conv_w", "lru_conv_w"]
SM_ROWS = 176
SM_WA, SM_WX, SM_META, SM_SCW, SM_LCW, SM_LOSS = 14, 78, 142, 158, 166, 170


def _simple_rows():
    rows, r = {}, 0
    for name, n in SIMPLE:
        rows[name] = r
        r += -(-n // 1024)
    return rows


def adamw_small(sm, special_g, ws, ms, vs):
    rows = _simple_rows()
    ns, nx = len(SIMPLE), len(SPECIAL)

    def body(*refs):
        sm_ref = refs[0]
        gx = refs[1:1 + nx]
        wr = refs[1 + nx:1 + nx + ns + nx]
        mr = refs[1 + nx + ns + nx:1 + nx + 2 * (ns + nx)]
        vr = refs[1 + nx + 2 * (ns + nx):1 + nx + 3 * (ns + nx)]
        outs = refs[1 + nx + 3 * (ns + nx):]
        o = 0
        for k, (name, n) in enumerate(SIMPLE):
            r0 = rows[name]
            for c0 in range(0, n, 1024):
                wd = min(1024, n - c0)
                g = sm_ref[r0 + c0 // 1024:r0 + c0 // 1024 + 1, 0:wd]
                sl = (slice(None), slice(c0, c0 + wd))
                d, m2, v2 = _adamw(wr[k][sl], g, mr[k][sl], vr[k][sl])
                outs[o][sl] = g
                outs[o + 1][sl] = d
                outs[o + 2][sl] = m2
                outs[o + 3][sl] = v2
            o += 4
        for k in range(nx):
            g = gx[k][...]
            d, m2, v2 = _adamw(wr[ns + k][...], g, mr[ns + k][...], vr[ns + k][...])
            outs[o][...] = g
            outs[o + 1][...] = d
            outs[o + 2][...] = m2
            outs[o + 3][...] = v2
            o += 4
        outs[o][...] = sm_ref[SM_LOSS:SM_LOSS + 1, 0:1]

    out_shape = []
    for k in range(ns + nx):
        out_shape += [jax.ShapeDtypeStruct(ws[k].shape, F32)] * 4
    out_shape.append(jax.ShapeDtypeStruct((1, 1), F32))
    return pl.pallas_call(body, out_shape=out_shape, compiler_params=_params(0), name="adamw_small")(sm, *special_g, *ws, *ms, *vs)


def _place():
    return lax.axis_index("x"), lax.axis_index("y"), lax.axis_index("c")


def _index(px, py, pc):
    return 4 * px + 2 * py + pc


def all_gather(name, shards):
    n = len(shards)
    hbm = pl.BlockSpec(memory_space=pl.ANY)

    def pieces(s):
        tile = 32 // s.dtype.itemsize
        per = s.shape[0] // tile // 4 * tile
        return [(0, s.shape[0])] if s.shape[0] < 256 else [(r * per, per if r < 3 else s.shape[0] - 3 * per) for r in range(4)]

    parts = [pieces(s) for s in shards]
    first_sem = [7 * sum(len(p) for p in parts[:i]) for i in range(n + 1)]

    def body(*refs):
        ins, outs = refs[:n], refs[n:2 * n]
        send_sems, recv_sems, local_sems = refs[2 * n:]
        x, y, c = _place()
        me, sibling = (x, y, c), (x, y, 1 - c)
        chips = [(1 - x, y), (x, 1 - y), (1 - x, 1 - y)]

        def copy(i, r, k, block, to, src=None):
            rows = pl.ds(*parts[i][r])
            dst = outs[i].at[_index(*block), rows]
            sem = first_sem[i] + 7 * r + k
            return pltpu.make_async_remote_copy(src_ref=dst if src is None else src.at[rows], dst_ref=dst, send_sem=send_sems.at[sem],
                                                recv_sem=recv_sems.at[sem], device_id=to, device_id_type=MESH)

        every = [(i, r) for i in range(n) for r in range(len(parts[i]))]
        mine = [pltpu.make_async_copy(ins[i], outs[i].at[_index(*me)], local_sems.at[i]) for i in range(n)]
        for cp in mine:
            cp.start()
        first = []
        for i, r in every:
            first += [copy(i, r, 1 + j, me, (*chip, c), src=ins[i]) for j, chip in enumerate(chips)]
            first.append(copy(i, r, 0, me, sibling, src=ins[i]))
        for cp in first:
            cp.start()
        passed = []
        for i, r in every:
            for j, chip in enumerate(chips):
                copy(i, r, 1 + j, (*chip, c), me).wait_recv()
                cp = copy(i, r, 4 + j, (*chip, c), sibling)
                cp.start()
                passed.append(cp)
        for i, r in every:
            copy(i, r, 0, sibling, me).wait_recv()
            for j, chip in enumerate(chips):
                copy(i, r, 4 + j, (*chip, 1 - c), me).wait_recv()
        for cp in first + passed:
            cp.wait_send()
        for cp in mine:
            cp.wait()

    return pl.pallas_call(
        body, in_specs=[hbm] * n, out_specs=[hbm] * n,
        out_shape=[jax.ShapeDtypeStruct((8,) + s.shape, s.dtype) for s in shards],
        scratch_shapes=[pltpu.SemaphoreType.DMA((first_sem[n],)), pltpu.SemaphoreType.DMA((first_sem[n],)), pltpu.SemaphoreType.DMA((n,))],
        name=name)(*shards)


HBM_SPEC = pl.BlockSpec(memory_space=pltpu.HBM)
SEM_SPEC = pl.BlockSpec(memory_space=pltpu.SEMAPHORE)
EFFECT = pltpu.SideEffectType.DATAFLOW_SIDE_EFFECTING


def _peers(x, y, c):
    return [((1 - x) if k & 4 else x, (1 - y) if k & 2 else y, (1 - c) if k & 1 else c) for k in range(1, 8)]


def _pieces(rows):
    per = rows // 16 // 4 * 16
    if rows % 16 or per == 0:
        return [(0, rows)]
    return [(r * per, per if r < 3 else rows - 3 * per) for r in range(4)]


def _peer_copies(src, land, send_sems, recv_sems, k, peer, mine):
    block = src.at[_index(*peer)]
    return [pltpu.make_async_remote_copy(src_ref=block.at[pl.ds(r0, nr)], dst_ref=land.at[mine, pl.ds(r0, nr)], send_sem=send_sems.at[k],
                                         recv_sem=recv_sems.at[k], device_id=peer, device_id_type=MESH)
            for r0, nr in _pieces(block.shape[0])]


OWN = 7


def _own_copy(src, land, send_sems, mine):
    return pltpu.make_async_copy(src.at[mine], land.at[mine], send_sems.at[OWN])


def copies_start(name, srcs):
    n = len(srcs)

    def body(*refs):
        ins, lands = refs[:n], refs[n:2 * n]
        sends, recvs = refs[2 * n:3 * n], refs[3 * n:4 * n]
        token = refs[-1]
        x, y, c = _place()
        mine = _index(x, y, c)
        for i in range(n):
            per_peer = [_peer_copies(ins[i], lands[i], sends[i], recvs[i], k, peer, mine) for k, peer in enumerate(_peers(x, y, c))]
            for piece in zip(*per_peer):
                for cp in piece:
                    cp.start()
            _own_copy(ins[i], lands[i], sends[i], mine).start()
        token[...] = jnp.zeros_like(token)

    res = pl.pallas_call(
        body, name=name,
        out_shape=([pltpu.SemaphoreType.DMA((8,))] * n + [pltpu.SemaphoreType.DMA((7,))] * n + [pltpu.HBM(s.shape, s.dtype) for s in srcs] * 2
                   + [jax.ShapeDtypeStruct((8, 128), F32)]),
        in_specs=[HBM_SPEC] * (2 * n),
        out_specs=[SEM_SPEC] * (2 * n) + [HBM_SPEC] * (2 * n) + [pl.BlockSpec(memory_space=pltpu.VMEM)],
        input_output_aliases={i: 2 * n + i for i in range(2 * n)},
        compiler_params=pltpu.CompilerParams(has_side_effects=EFFECT),
    )(*[pltpu.with_memory_space_constraint(s, pltpu.HBM) for s in srcs],
      *[pltpu.with_memory_space_constraint(lax.empty(s.shape, s.dtype), pltpu.HBM) for s in srcs])
    return [(res[i], res[n + i], res[2 * n + i], res[3 * n + i]) for i in range(n)], res[-1]


def copies_wait(name, started, after):
    n = len(started)

    def body(*refs):
        ins, lands = refs[:n], refs[n:2 * n]
        sends, recvs = refs[2 * n:3 * n], refs[3 * n:4 * n]
        x, y, c = _place()
        mine = _index(x, y, c)
        for i in range(n):
            for k, peer in enumerate(_peers(x, y, c)):
                arrival = pltpu.make_async_remote_copy(src_ref=ins[i].at[mine], dst_ref=lands[i].at[_index(*peer)],
                                                       send_sem=sends[i].at[k], recv_sem=recvs[i].at[k], device_id=peer, device_id_type=MESH)
                arrival.wait_send()
                arrival.wait_recv()
            _own_copy(ins[i], lands[i], sends[i], mine).wait()

    srcs = [s[2] for s in started]
    lands = [s[3] for s in started]
    res = pl.pallas_call(
        body, name=name,
        out_shape=[pltpu.HBM(s.shape, s.dtype) for s in srcs] + [pltpu.HBM(z.shape, z.dtype) for z in lands],
        in_specs=[HBM_SPEC] * (2 * n) + [SEM_SPEC] * (2 * n) + [ANY_SPEC] * len(after),
        out_specs=[HBM_SPEC] * (2 * n),
        input_output_aliases={i: i for i in range(2 * n)},
        compiler_params=pltpu.CompilerParams(has_side_effects=EFFECT),
    )(*srcs, *lands, *[s[0] for s in started], *[s[1] for s in started], *after)
    return list(res[n:])


def _hop(src, land, send_sems, recv_sems, k, block, to):
    dst = land.at[_index(*block)]
    return pltpu.make_async_remote_copy(src_ref=dst if src is None else src, dst_ref=dst, send_sem=send_sems.at[k], recv_sem=recv_sems.at[k],
                                        device_id=to, device_id_type=MESH)


def _own_block(src, land, send_sems, mine):
    return pltpu.make_async_copy(src, land.at[mine], send_sems.at[4])


def _other_chips(x, y):
    return [(1 - x, y), (x, 1 - y), (1 - x, 1 - y)]


def gather_start(name, shards, through):
    n, m = len(shards), len(through)

    def body(*refs):
        ins, lands = refs[:n], refs[n:2 * n]
        sends, recvs = refs[2 * n + m:3 * n + m], refs[3 * n + m:4 * n + m]
        x, y, c = _place()
        for i in range(n):
            for j, chip in enumerate(_other_chips(x, y)):
                _hop(ins[i], lands[i], sends[i], recvs[i], 1 + j, (x, y, c), (*chip, c)).start()
            _hop(ins[i], lands[i], sends[i], recvs[i], 0, (x, y, c), (x, y, 1 - c)).start()
            _own_block(ins[i], lands[i], sends[i], _index(x, y, c)).start()

    own, passing = pltpu.SemaphoreType.DMA((5,)), pltpu.SemaphoreType.DMA((3,))
    zones = [jax.ShapeDtypeStruct((8,) + s.shape, s.dtype) for s in shards]
    res = pl.pallas_call(
        body, name=name,
        out_shape=([own] * (2 * n) + [passing] * (2 * n) + [pltpu.HBM(s.shape, s.dtype) for s in shards]
                   + [pltpu.HBM(z.shape, z.dtype) for z in zones] + [pltpu.HBM(t.shape, t.dtype) for t in through]),
        in_specs=[HBM_SPEC] * (2 * n + m),
        out_specs=[SEM_SPEC] * (4 * n) + [HBM_SPEC] * (2 * n + m),
        input_output_aliases={i: 4 * n + i for i in range(2 * n + m)},
        compiler_params=pltpu.CompilerParams(has_side_effects=EFFECT),
    )(*[pltpu.with_memory_space_constraint(s, pltpu.HBM) for s in shards],
      *[pltpu.with_memory_space_constraint(lax.empty(z.shape, z.dtype), pltpu.HBM) for z in zones],
      *[pltpu.with_memory_space_constraint(t, pltpu.HBM) for t in through])
    return [[res[4 * n + i], res[5 * n + i], res[i], res[n + i], res[2 * n + i], res[3 * n + i]] for i in range(n)], list(res[6 * n:])


def gather_stage(name, pass_on, finish, after):
    arrays = pass_on + finish
    n = len(arrays)

    def body(*refs):
        ins, lands = refs[:n], refs[n:2 * n]
        sems = [refs[(2 + q) * n:(3 + q) * n] for q in range(4)]
        x, y, c = _place()
        me, sibling = (x, y, c), (x, y, 1 - c)
        for i in range(len(pass_on)):
            send, recv, send_on, recv_on = (q[i] for q in sems)
            for j, chip in enumerate(_other_chips(x, y)):
                _hop(None, lands[i], send, recv, 1 + j, (*chip, c), me).wait_recv()
                _hop(None, lands[i], send_on, recv_on, j, (*chip, c), sibling).start()
        for i in range(len(pass_on), n):
            send, recv, send_on, recv_on = (q[i] for q in sems)
            _hop(ins[i], lands[i], send, recv, 0, sibling, me).wait_recv()
            for j, chip in enumerate(_other_chips(x, y)):
                _hop(None, lands[i], send_on, recv_on, j, (*chip, 1 - c), me).wait_recv()
            _hop(ins[i], lands[i], send, recv, 0, me, sibling).wait_send()
            _own_block(ins[i], lands[i], send, _index(*me)).wait()
            for j, chip in enumerate(_other_chips(x, y)):
                _hop(ins[i], lands[i], send, recv, 1 + j, me, (*chip, c)).wait_send()
                _hop(None, lands[i], send_on, recv_on, j, (*chip, c), sibling).wait_send()
        refs[-1][...] = jnp.zeros_like(refs[-1])

    res = pl.pallas_call(
        body, name=name,
        out_shape=([pltpu.HBM(a[0].shape, a[0].dtype) for a in arrays] + [pltpu.HBM(a[1].shape, a[1].dtype) for a in arrays]
                   + [jax.ShapeDtypeStruct((8, 128), F32)]),
        in_specs=[HBM_SPEC] * (2 * n) + [SEM_SPEC] * (4 * n) + [ANY_SPEC],
        out_specs=[HBM_SPEC] * (2 * n) + [pl.BlockSpec(memory_space=pltpu.VMEM)],
        input_output_aliases={i: i for i in range(2 * n)},
        compiler_params=pltpu.CompilerParams(has_side_effects=EFFECT),
    )(*[a[0] for a in arrays], *[a[1] for a in arrays], *[a[2 + q] for q in range(4) for a in arrays], after)
    for i, a in enumerate(arrays):
        a[0], a[1] = res[i], res[n + i]
    return [a[1] for a in finish], res[-1]


WEIGHTS = ["meta_tokens", "norm1_w", "w_in", "ssd_conv_w", "ssd_conv_b", "ssd_dt_bias", "ssd_a_log", "ssd_d", "ssd_norm_w", "lru_conv_w",
           "lru_conv_b", "lru_wa", "lru_ba", "lru_wx", "lru_bx", "lru_lambda", "lru_norm_w", "w_out", "norm2_w", "w_gate", "w_up", "w_down",
           "final_norm_w"]
BIG = ["w_in", "w_out", "w_gate", "w_up", "w_down"]
COLUMN_SHARDED = ["w_in", "w_gate", "w_up"]


def _pair_blocks(w):
    w = w.reshape(8, 2, 64, 64)
    z = jnp.zeros((8, 64, 64), w.dtype)
    return jnp.concatenate([jnp.concatenate([w[:, 0], z], axis=2), jnp.concatenate([z, w[:, 1]], axis=2)], axis=1)


def _unpair_blocks(w2):
    return jnp.stack([w2[:, :64, :64], w2[:, 64:, 64:]], axis=1).reshape(16, 64, 64)


def _per_group(vs):
    return jnp.pad(jnp.concatenate(vs, axis=0).reshape(len(vs), 2, 1, 8), ((0, 0), (0, 0), (0, 0), (0, 120)))


def _pad_cols(v, n):
    return jnp.pad(v, ((0, 0), (0, n - v.shape[1])))


def local_step(x, target, small_w, ssd_cw, w_in_shards, fetch, send, p):
    heads = _per_group([p["ssd_dt_bias"], p["ssd_a_log"], p["ssd_d"]])
    wa2 = _pair_blocks(p["lru_wa"]).astype(BF)
    wx2 = _pair_blocks(p["lru_wx"]).astype(BF)
    lru = (small_w, p["lru_conv_b"], wa2, p["lru_ba"], wx2, p["lru_bx"], p["lru_lambda"])

    proj, dt_raw, u1, h0, w_in, w_dt = in_proj(x, small_w, p["norm1_w"], w_in_shards)
    yn_ssd, y_pre, h_prev = ssd_fwd(proj, dt_raw, ssd_cw, p["ssd_conv_b"], heads, p["ssd_norm_w"])
    _, moved = fetch([], yn_ssd)
    hseq, a = lru_fwd(proj, *lru, moved)
    (w_out,), _ = fetch(["w_out"], hseq)
    h1, cat = out_proj(yn_ssd, proj, hseq, p["lru_norm_w"], w_out, h0)
    (w_gate, w_up), _ = fetch(["w_gate", "w_up"], h1)
    gt, up, act, u2 = gate_up(h1, p["norm2_w"], w_gate, w_up)
    (w_down,), _ = fetch(["w_down"], act)
    dh2, dh2_b, loss, d_fnw = down_loss(act, w_down, h1, target, p["final_norm_w"])

    dgt, dup, g_down, g_gate, g_up = swiglu_bwd(dh2_b, w_down, gt, up, act, u2)
    dh1, dh1_b, d_n2 = gate_up_bwd(dgt, dup, w_gate, w_up, h1, p["norm2_w"], dh2)
    sent = send({"w_down": g_down, "w_gate": g_gate, "w_up": g_up, "w_out": weight_grad("dw_out", cat, dh1_b)})
    dyn, dh_out, dg_b, d_lnw = out_proj_bwd(dh1_b, w_out, proj, hseq, p["lru_norm_w"], sent)

    dxl_b, d_lcw, d_lcb, dwa2, d_ba, dwx2, d_bx, d_lam = lru_bwd(dh_out, a, hseq, proj, *lru)
    dz_b, dxbc_b, ddt_b, dpar, d_snw, d_scw, d_scb = ssd_bwd(dyn, proj, dt_raw, ssd_cw, p["ssd_conv_b"], y_pre, h_prev, heads,
                                                             p["ssd_norm_w"], sent)
    sent = send({"w_in": in_weight_grad([dz_b, dxbc_b, dg_b, dxl_b], [0, SSD_W, 2576, 2576 + LRU_W], ddt_b, u1)})
    grad_x, d_meta, d_n1, dwa2, dwx2 = in_proj_bwd(dz_b, dg_b, dxl_b, dxbc_b, ddt_b, w_in, w_dt, h0, p["norm1_w"], dh1, sent, [dwa2, dwx2])
    small = {"norm1_w": d_n1, "ssd_conv_b": d_scb, "ssd_dt_bias": dpar[:, 0, :8].reshape(1, 16), "ssd_a_log": dpar[:, 1, :8].reshape(1, 16),
             "ssd_d": dpar[:, 2, :8].reshape(1, 16), "ssd_norm_w": d_snw, "lru_conv_b": d_lcb, "lru_ba": d_ba, "lru_bx": d_bx,
             "lru_lambda": d_lam, "lru_norm_w": d_lnw, "norm2_w": d_n2, "final_norm_w": d_fnw,
             "lru_wa": _unpair_blocks(dwa2), "lru_wx": _unpair_blocks(dwx2), "meta_tokens": d_meta,
             "ssd_conv_w": d_scw, "lru_conv_w": d_lcw}
    return loss, grad_x, small


def _pack_small(small, loss):
    rows = [_pad_cols(small[name], -(-n // 1024) * 1024).reshape(-1, 1024) for name, n in SIMPLE]
    rows += [small["lru_wa"].reshape(64, 1024), small["lru_wx"].reshape(64, 1024), small["meta_tokens"],
             _pad_cols(small["ssd_conv_w"], 2048).reshape(8, 1024), small["lru_conv_w"], _pad_cols(loss[:, 0:1], 1024)]
    sm = jnp.concatenate(rows, axis=0)
    return jnp.pad(sm, ((0, SM_ROWS - sm.shape[0]), (0, 0)))


def _slabs(g):
    return g.reshape(8, g.shape[0] // 8, g.shape[1])


def _unslab(g):
    return g.reshape(8 * g.shape[1], g.shape[2])


def kernel(x, meta_tokens, norm1_w, w_in, ssd_conv_w, ssd_conv_b, ssd_dt_bias, ssd_a_log, ssd_d, ssd_norm_w, lru_conv_w, lru_conv_b, lru_wa, lru_ba, lru_wx, lru_bx, lru_lambda, lru_norm_w, w_out, norm2_w, w_gate, w_up, w_down, final_norm_w, loss_target, m_meta_tokens, m_norm1_w, m_w_in, m_ssd_conv_w, m_ssd_conv_b, m_ssd_dt_bias, m_ssd_a_log, m_ssd_d, m_ssd_norm_w, m_lru_conv_w, m_lru_conv_b, m_lru_wa, m_lru_ba, m_lru_wx, m_lru_bx, m_lru_lambda, m_lru_norm_w, m_w_out, m_norm2_w, m_w_gate, m_w_up, m_w_down, m_final_norm_w, v_meta_tokens, v_norm1_w, v_w_in, v_ssd_conv_w, v_ssd_conv_b, v_ssd_dt_bias, v_ssd_a_log, v_ssd_d, v_ssd_norm_w, v_lru_conv_w, v_lru_conv_b, v_lru_wa, v_lru_ba, v_lru_wx, v_lru_bx, v_lru_lambda, v_lru_norm_w, v_w_out, v_norm2_w, v_w_gate, v_w_up, v_w_down, v_final_norm_w):
    w = dict(meta_tokens=meta_tokens, norm1_w=norm1_w, w_in=w_in[0], ssd_conv_w=ssd_conv_w[0], ssd_conv_b=ssd_conv_b, ssd_dt_bias=ssd_dt_bias,
             ssd_a_log=ssd_a_log, ssd_d=ssd_d, ssd_norm_w=ssd_norm_w, lru_conv_w=lru_conv_w[0], lru_conv_b=lru_conv_b, lru_wa=lru_wa[0],
             lru_ba=lru_ba, lru_wx=lru_wx[0], lru_bx=lru_bx, lru_lambda=lru_lambda, lru_norm_w=lru_norm_w, w_out=w_out[0], norm2_w=norm2_w,
             w_gate=w_gate[0], w_up=w_up[0], w_down=w_down[0], final_norm_w=final_norm_w.reshape(1, D))
    m = dict(meta_tokens=m_meta_tokens, norm1_w=m_norm1_w, w_in=m_w_in[0], ssd_conv_w=m_ssd_conv_w[0], ssd_conv_b=m_ssd_conv_b,
             ssd_dt_bias=m_ssd_dt_bias, ssd_a_log=m_ssd_a_log, ssd_d=m_ssd_d, ssd_norm_w=m_ssd_norm_w, lru_conv_w=m_lru_conv_w[0],
             lru_conv_b=m_lru_conv_b, lru_wa=m_lru_wa[0], lru_ba=m_lru_ba, lru_wx=m_lru_wx[0], lru_bx=m_lru_bx, lru_lambda=m_lru_lambda,
             lru_norm_w=m_lru_norm_w, w_out=m_w_out[0], norm2_w=m_norm2_w, w_gate=m_w_gate[0], w_up=m_w_up[0], w_down=m_w_down[0],
             final_norm_w=m_final_norm_w.reshape(1, D))
    v = dict(meta_tokens=v_meta_tokens, norm1_w=v_norm1_w, w_in=v_w_in[0], ssd_conv_w=v_ssd_conv_w[0], ssd_conv_b=v_ssd_conv_b,
             ssd_dt_bias=v_ssd_dt_bias, ssd_a_log=v_ssd_a_log, ssd_d=v_ssd_d, ssd_norm_w=v_ssd_norm_w, lru_conv_w=v_lru_conv_w[0],
             lru_conv_b=v_lru_conv_b, lru_wa=v_lru_wa[0], lru_ba=v_lru_ba, lru_wx=v_lru_wx[0], lru_bx=v_lru_bx, lru_lambda=v_lru_lambda,
             lru_norm_w=v_lru_norm_w, w_out=v_w_out[0], norm2_w=v_norm2_w, w_gate=v_w_gate[0], w_up=v_w_up[0], w_down=v_w_down[0],
             final_norm_w=v_final_norm_w.reshape(1, D))
    shapes = dict(meta_tokens=meta_tokens.shape, norm1_w=norm1_w.shape, w_in=w_in.shape, ssd_conv_w=ssd_conv_w.shape,
                  ssd_conv_b=ssd_conv_b.shape, ssd_dt_bias=ssd_dt_bias.shape, ssd_a_log=ssd_a_log.shape, ssd_d=ssd_d.shape,
                  ssd_norm_w=ssd_norm_w.shape, lru_conv_w=lru_conv_w.shape, lru_conv_b=lru_conv_b.shape, lru_wa=lru_wa.shape,
                  lru_ba=lru_ba.shape, lru_wx=lru_wx.shape, lru_bx=lru_bx.shape, lru_lambda=lru_lambda.shape, lru_norm_w=lru_norm_w.shape,
                  w_out=w_out.shape, norm2_w=norm2_w.shape, w_gate=w_gate.shape, w_up=w_up.shape, w_down=w_down.shape,
                  final_norm_w=final_norm_w.shape)
    me = _index(*_place())
    for n in COLUMN_SHARDED:
        w[n], m[n], v[n] = w[n].T, m[n].T, v[n].T

    small_shard = jnp.concatenate([w["meta_tokens"], _pad_cols(w["ssd_conv_w"], 256).reshape(8, 128), w["lru_conv_w"],
                                   jnp.zeros((4, 128), F32)], axis=0)
    g_in, gs = all_gather("gather_w_in", [w["w_in"].astype(BF), small_shard])
    later = ["w_out", "w_gate", "w_up", "w_down"]
    started, (g_in, gs) = gather_start("gather_rest_start", [w[n].astype(BF) for n in later], [g_in, gs])
    started = dict(zip(later, started))
    ssd_cw = gs[:, 16:24].reshape(8, 4, 256)[:, :, :192].transpose(1, 0, 2).reshape(4, XBC)

    def fetch(names, after):
        pass_on = {"w_out": ["w_down"], "w_gate": [], "w_down": []}[names[0]] if names else ["w_out", "w_gate", "w_up"]
        got, zero = gather_stage("gather_" + (names[0] + "_wait" if names else "pass_on"), [started[n] for n in pass_on],
                                 [started[n] for n in names], after)
        return [_unslab(g) for g in got], zero

    in_flight = {}

    def send(grads):
        names = list(grads)
        st, zero = copies_start("grads_" + names[0] + "_start", [grads[n] if n == "small" else _slabs(grads[n]) for n in names])
        in_flight.update(zip(names, st))
        return zero

    loss, grad_x, small = local_step(x[0], loss_target[0], gs, ssd_cw, g_in, fetch, send, w)
    send({"small": _pack_small(small, loss).reshape(8, SM_ROWS // 8, 1024)})

    out = {}
    early = ["w_down", "w_gate", "w_up", "w_out"]
    recv = dict(zip(early, copies_wait("grads_early_wait", [in_flight[n] for n in early], [in_flight["small"][2]])))
    for pair in (early[:2], early[2:]):
        done = adamw_shards("adamw_" + pair[0], [recv[n] for n in pair], [w[n] for n in pair], [m[n] for n in pair], [v[n] for n in pair])
        out.update(zip(pair, done))
    recv_in, recv_small = copies_wait("grads_late_wait", [in_flight["w_in"], in_flight["small"]], [out[n][0] for n in early])
    def lines(a):
        return jnp.transpose(a.reshape(D // 128, 128, IN_COLS // 8), (2, 0, 1))

    gathering, zero = copies_start("gather_small_start", [sum_slabs(recv_small)])
    updated = adamw_w_in(recv_in, lines(w_in), lines(m_w_in), lines(v_w_in), zero)
    out["w_in"] = [jnp.transpose(o, (1, 2, 0)).reshape(D, IN_COLS // 8) for o in updated]
    for n in ("w_gate", "w_up"):
        out[n] = [o.T for o in out[n]]
    sm = copies_wait("gather_small_wait", gathering, [updated[0]])[0].reshape(SM_ROWS, 1024)
    special_g =[sm[SM_WA:SM_WA + 64].reshape(16, 64, 64), sm[SM_WX:SM_WX + 64].reshape(16, 64, 64),
                 lax.dynamic_slice(sm[SM_META:SM_META + 16], (0, 128 * me), (16, 128)),
                 lax.dynamic_slice(sm[SM_SCW:SM_SCW + 8].reshape(4, 2048), (0, 192 * me), (4, 192)),
                 lax.dynamic_slice(sm[SM_LCW:SM_LCW + 4], (0, 128 * me), (4, 128))]
    names = [n for n, _ in SIMPLE] + SPECIAL
    res = adamw_small(sm, special_g, [w[n] for n in names], [m[n] for n in names], [v[n] for n in names])
    for k, n in enumerate(names):
        out[n] = res[4 * k:4 * k + 4]
    flat = [res[-1].reshape(()), grad_x[None]]
    for k in range(4):
        flat += [out[n][k].reshape(shapes[n]) for n in WEIGHTS]
    return tuple(flat)
```
